```python
import jax
import jax.numpy as jnp
from jax import lax
import numpy as np

D_MODEL = 1024
BATCH = 8
SEQ = 2048
DEPTH = 1

MIX_WIDTH = D_MODEL
ATT_WIDTH = MIX_WIDTH // 2
RWKV_WIDTH = MIX_WIDTH - ATT_WIDTH
HEAD_DIM = 64
ATT_HEADS = ATT_WIDTH // HEAD_DIM
RWKV_HEAD_SIZE = 64
RWKV_HEADS = RWKV_WIDTH // RWKV_HEAD_SIZE
DILATED_PATTERNS = ((128, 1), (512, 4), (2048, 16))
ATT_BLOCK = 128
D_FF = ((8 * D_MODEL // 3 + 127) // 128) * 128
DECAY_LORA = max(32, int(round(1.8 * RWKV_WIDTH ** 0.5 / 32)) * 32)
ICL_LORA = max(32, int(round(1.8 * RWKV_WIDTH ** 0.5 / 32)) * 32)
GATE_LORA = max(32, int(round(0.6 * RWKV_WIDTH ** 0.8 / 32)) * 32)
IN_SPLITS = (ATT_WIDTH, ATT_WIDTH, ATT_WIDTH, RWKV_WIDTH, RWKV_WIDTH, RWKV_WIDTH, RWKV_WIDTH)
IN_COLS = sum(IN_SPLITS)
FFN_RESIDUAL = 0.5
RMS_EPS = 1e-6
GN_EPS = 64e-5
NEG_INF = -1e30

kernel_name = 'hymba_dilated_rwkv7_macaron'


def rms_norm(x, gain):
    xf = x.astype(jnp.float32)
    y = xf * lax.rsqrt(jnp.mean(xf * xf, axis=-1, keepdims=True) + RMS_EPS)
    return (y * gain.astype(jnp.float32)).astype(x.dtype)


def swiglu(h, w_gate, w_up, w_down):
    return (jax.nn.silu(h @ w_gate) * (h @ w_up)) @ w_down


def token_shift_lerp(x, mu):
    prev = jnp.pad(x, ((0, 0), (1, 0), (0, 0)))[:, :-1]
    return x + (prev - x) * mu


def banded_causal_attention(q, k, v, window):
    b, g, length, dh = q.shape
    nb = -(-length // ATT_BLOCK)
    lp = nb * ATT_BLOCK
    qb = jnp.pad(q, ((0, 0), (0, 0), (0, lp - length), (0, 0))).reshape(b, g, nb, ATT_BLOCK, dh)
    kv_pad = ((0, 0), (0, 0), (ATT_BLOCK, lp - length), (0, 0))
    kb = jnp.pad(k, kv_pad).reshape(b, g, nb + 1, ATT_BLOCK, dh)
    vb = jnp.pad(v, kv_pad).reshape(b, g, nb + 1, ATT_BLOCK, dh)
    kw = jnp.concatenate([kb[:, :, :-1], kb[:, :, 1:]], axis=3)
    vw = jnp.concatenate([vb[:, :, :-1], vb[:, :, 1:]], axis=3)
    s = jnp.einsum('bgnqd,bgnkd->bgnqk', qb, kw).astype(jnp.float32) * (dh ** -0.5)
    qi = jnp.arange(ATT_BLOCK)[:, None]
    kj = jnp.arange(2 * ATT_BLOCK)[None, :]
    dist = ATT_BLOCK + qi - kj
    kpos = (jnp.arange(nb)[:, None, None] - 1) * ATT_BLOCK + kj[None]
    mask = (dist >= 0) & (dist <= window) & (kpos >= 0)
    s = jnp.where(mask, s, NEG_INF)
    lse = jax.nn.logsumexp(s, axis=-1)
    p = jnp.exp(s - lse[..., None])
    o = jnp.einsum('bgnqk,bgnkd->bgnqd', p.astype(v.dtype), vw)
    return (o.reshape(b, g, lp, dh)[:, :, :length], lse.reshape(b, g, lp)[:, :, :length])


def dilated_attention(q, k, v):
    b, h, s, dh = q.shape
    outs, lses = [], []
    for window, dil in DILATED_PATTERNS:
        length = s // dil

        def to_sub(t):
            return t.reshape(b, h, length, dil, dh).transpose(0, 1, 3, 2, 4).reshape(b, h * dil, length, dh)

        o, lse = banded_causal_attention(to_sub(q), to_sub(k), to_sub(v), window // dil)
        outs.append(o.reshape(b, h, dil, length, dh).transpose(0, 1, 3, 2, 4).reshape(b, h, s, dh))
        lses.append(lse.reshape(b, h, dil, length).transpose(0, 1, 3, 2).reshape(b, h, s))
    wts = jax.nn.softmax(jnp.stack(lses), axis=0)
    o = jnp.sum(wts[..., None] * jnp.stack(outs).astype(jnp.float32), axis=0)
    return o.astype(q.dtype)


def rwkv7_scan(r, decay, k, v, a, b):
    bsz, _, nh, n = r.shape

    def step(state, inp):
        r_t, w_t, k_t, v_t, a_t, b_t = inp
        sa = jnp.einsum('bhij,bhj->bhi', state, a_t)
        state = state * w_t[:, :, None, :] + sa[..., None] * b_t[:, :, None, :] + v_t[..., None] * k_t[:, :, None, :]
        return state, jnp.einsum('bhij,bhj->bhi', state, r_t)

    xs = tuple(jnp.moveaxis(t, 1, 0) for t in (r, decay, k, v, a, b))
    s0 = jnp.zeros((bsz, nh, n, n), jnp.float32)
    _, ys = lax.scan(step, s0, xs)
    return jnp.moveaxis(ys, 0, 1)


def rwkv7_time_mix(r_in, k_in, v_in, c_in, mu_r, mu_k, mu_v, mu_w, mu_a, mu_g,
                   w0, w1, w2, a0, a1, a2, g1, g2, k_k, k_a, r_k, ln_x_w, ln_x_b):
    bsz, s, c = r_in.shape
    f32 = jnp.float32

    def heads(t):
        return t.astype(f32).reshape(bsz, s, RWKV_HEADS, RWKV_HEAD_SIZE)

    r = token_shift_lerp(r_in, mu_r)
    k = token_shift_lerp(k_in, mu_k)
    v = token_shift_lerp(v_in, mu_v)
    cw = token_shift_lerp(c_in, mu_w)
    ca = token_shift_lerp(c_in, mu_a)
    cg = token_shift_lerp(c_in, mu_g)
    w_log = -jax.nn.softplus(-(w0 + jnp.tanh(cw @ w1) @ w2).astype(f32)) - 0.5
    decay = jnp.exp(-jnp.exp(w_log))
    a = jax.nn.sigmoid((a0 + (ca @ a1) @ a2).astype(f32))
    g = jax.nn.sigmoid(cg @ g1) @ g2
    kk = heads(k * k_k)
    kk = kk / jnp.maximum(jnp.linalg.norm(kk, axis=-1, keepdims=True), 1e-12)
    k = k.astype(f32) * (1.0 + (a - 1.0) * k_a.astype(f32))
    rh, kh, vh = heads(r), heads(k), heads(v)
    y = rwkv7_scan(rh, heads(decay), kh, vh, -kk, kk * heads(a))
    mean = jnp.mean(y, axis=-1, keepdims=True)
    var = jnp.mean(jnp.square(y - mean), axis=-1, keepdims=True)
    y = ((y - mean) * lax.rsqrt(var + GN_EPS)).reshape(bsz, s, c) * ln_x_w.astype(f32) + ln_x_b.astype(f32)
    bonus = jnp.sum(rh * kh * r_k.astype(f32), axis=-1, keepdims=True) * vh
    out = (y + bonus.reshape(bsz, s, c)) * g.astype(f32)
    return out.astype(r_in.dtype)


def _fwd_setup_inputs(seed: int = 0) -> dict:
    key = jax.random.key(seed)
    ks = iter(jax.random.split(key, 40))
    f32 = jnp.float32

    def normal(shape, scale):
        return jax.random.normal(next(ks), shape, f32) * scale

    def gain(shape):
        return 1.0 + 0.02 * jax.random.normal(next(ks), shape, f32)

    def unif(shape, lo, hi):
        return jax.random.uniform(next(ks), shape, f32, lo, hi)

    L, D, RW = DEPTH, D_MODEL, RWKV_WIDTH
    return {
        'x': jax.random.normal(next(ks), (BATCH, SEQ, D), f32),
        'ffn1_norm': gain((L, D)),
        'ffn1_w_gate': normal((L, D, D_FF), D ** -0.5),
        'ffn1_w_up': normal((L, D, D_FF), D ** -0.5),
        'ffn1_w_down': normal((L, D_FF, D), D_FF ** -0.5),
        'mix_norm': gain((L, D)),
        'w_in': normal((L, D, IN_COLS), D ** -0.5),
        'q_norm': gain((L, HEAD_DIM)),
        'k_norm': gain((L, HEAD_DIM)),
        'mu_r': unif((L, RW), 0.0, 1.0),
        'mu_k': unif((L, RW), 0.0, 1.0),
        'mu_v': unif((L, RW), 0.0, 1.0),
        'mu_w': unif((L, RW), 0.0, 1.0),
        'mu_a': unif((L, RW), 0.0, 1.0),
        'mu_g': unif((L, RW), 0.0, 1.0),
        'w0': unif((L, RW), -6.5, -1.0),
        'w1': normal((L, RW, DECAY_LORA), RW ** -0.5),
        'w2': normal((L, DECAY_LORA, RW), 0.1 * DECAY_LORA ** -0.5),
        'a0': normal((L, RW), 0.1),
        'a1': normal((L, RW, ICL_LORA), RW ** -0.5),
        'a2': normal((L, ICL_LORA, RW), 0.1 * ICL_LORA ** -0.5),
        'g1': normal((L, RW, GATE_LORA), RW ** -0.5),
        'g2': normal((L, GATE_LORA, RW), GATE_LORA ** -0.5),
        'k_k': 0.85 + normal((L, RW), 0.05),
        'k_a': 1.0 + normal((L, RW), 0.05),
        'r_k': -0.04 + normal((L, RWKV_HEADS, RWKV_HEAD_SIZE), 0.02),
        'ln_x_w': gain((L, RW)),
        'ln_x_b': normal((L, RW), 0.02),
        'w_out': normal((L, MIX_WIDTH, D), MIX_WIDTH ** -0.5),
        'ffn2_norm': gain((L, D)),
        'ffn2_w_gate': normal((L, D, D_FF), D ** -0.5),
        'ffn2_w_up': normal((L, D, D_FF), D ** -0.5),
        'ffn2_w_down': normal((L, D_FF, D), D_FF ** -0.5),
    }


def _fwd_reference(x, ffn1_norm, ffn1_w_gate, ffn1_w_up, ffn1_w_down, mix_norm, w_in, q_norm, k_norm,
              mu_r, mu_k, mu_v, mu_w, mu_a, mu_g, w0, w1, w2, a0, a1, a2, g1, g2,
              k_k, k_a, r_k, ln_x_w, ln_x_b, w_out, ffn2_norm, ffn2_w_gate, ffn2_w_up, ffn2_w_down):
    b, s, _ = x.shape
    split_idx = [int(i) for i in np.cumsum(IN_SPLITS)[:-1]]
    for l in range(DEPTH):
        h = rms_norm(x, ffn1_norm[l])
        x = x + FFN_RESIDUAL * swiglu(h, ffn1_w_gate[l], ffn1_w_up[l], ffn1_w_down[l])
        h = rms_norm(x, mix_norm[l])
        proj = h @ w_in[l]
        q, k, v, rr, rk, rv, rc = jnp.split(proj, split_idx, axis=-1)
        qh = rms_norm(q.reshape(b, s, ATT_HEADS, HEAD_DIM), q_norm[l]).transpose(0, 2, 1, 3)
        kh = rms_norm(k.reshape(b, s, ATT_HEADS, HEAD_DIM), k_norm[l]).transpose(0, 2, 1, 3)
        vh = v.reshape(b, s, ATT_HEADS, HEAD_DIM).transpose(0, 2, 1, 3)
        att = dilated_attention(qh, kh, vh).transpose(0, 2, 1, 3).reshape(b, s, ATT_WIDTH)
        rw = rwkv7_time_mix(rr, rk, rv, rc, mu_r[l], mu_k[l], mu_v[l], mu_w[l], mu_a[l], mu_g[l],
                            w0[l], w1[l], w2[l], a0[l], a1[l], a2[l], g1[l], g2[l],
                            k_k[l], k_a[l], r_k[l], ln_x_w[l], ln_x_b[l])
        x = x + jnp.concatenate([att, rw], axis=-1) @ w_out[l]
        h = rms_norm(x, ffn2_norm[l])
        x = x + FFN_RESIDUAL * swiglu(h, ffn2_w_gate[l], ffn2_w_up[l], ffn2_w_down[l])
    return x


import jax as _jax
import jax.numpy as _jnp

TWIN_FORMAT = 'train_step'
FWD_PARAMS = ['x', 'ffn1_norm', 'ffn1_w_gate', 'ffn1_w_up', 'ffn1_w_down', 'mix_norm', 'w_in', 'q_norm', 'k_norm', 'mu_r', 'mu_k', 'mu_v', 'mu_w', 'mu_a', 'mu_g', 'w0', 'w1', 'w2', 'a0', 'a1', 'a2', 'g1', 'g2', 'k_k', 'k_a', 'r_k', 'ln_x_w', 'ln_x_b', 'w_out', 'ffn2_norm', 'ffn2_w_gate', 'ffn2_w_up', 'ffn2_w_down']
TWIN_WEIGHTS = ['ffn1_norm', 'ffn1_w_gate', 'ffn1_w_up', 'ffn1_w_down', 'mix_norm', 'w_in', 'q_norm', 'k_norm', 'mu_r', 'mu_k', 'mu_v', 'mu_w', 'mu_a', 'mu_g', 'w0', 'w1', 'w2', 'a0', 'a1', 'a2', 'g1', 'g2', 'k_k', 'k_a', 'r_k', 'ln_x_w', 'ln_x_b', 'w_out', 'ffn2_norm', 'ffn2_w_gate', 'ffn2_w_up', 'ffn2_w_down']
TWIN_DIFF_INPUT = 'x'
TWIN_INPUTS = ['x', 'ffn1_norm', 'ffn1_w_gate', 'ffn1_w_up', 'ffn1_w_down', 'mix_norm', 'w_in', 'q_norm', 'k_norm', 'mu_r', 'mu_k', 'mu_v', 'mu_w', 'mu_a', 'mu_g', 'w0', 'w1', 'w2', 'a0', 'a1', 'a2', 'g1', 'g2', 'k_k', 'k_a', 'r_k', 'ln_x_w', 'ln_x_b', 'w_out', 'ffn2_norm', 'ffn2_w_gate', 'ffn2_w_up', 'ffn2_w_down', 'loss_target', 'm_ffn1_norm', 'm_ffn1_w_gate', 'm_ffn1_w_up', 'm_ffn1_w_down', 'm_mix_norm', 'm_w_in', 'm_q_norm', 'm_k_norm', 'm_mu_r', 'm_mu_k', 'm_mu_v', 'm_mu_w', 'm_mu_a', 'm_mu_g', 'm_w0', 'm_w1', 'm_w2', 'm_a0', 'm_a1', 'm_a2', 'm_g1', 'm_g2', 'm_k_k', 'm_k_a', 'm_r_k', 'm_ln_x_w', 'm_ln_x_b', 'm_w_out', 'm_ffn2_norm', 'm_ffn2_w_gate', 'm_ffn2_w_up', 'm_ffn2_w_down', 'v_ffn1_norm', 'v_ffn1_w_gate', 'v_ffn1_w_up', 'v_ffn1_w_down', 'v_mix_norm', 'v_w_in', 'v_q_norm', 'v_k_norm', 'v_mu_r', 'v_mu_k', 'v_mu_v', 'v_mu_w', 'v_mu_a', 'v_mu_g', 'v_w0', 'v_w1', 'v_w2', 'v_a0', 'v_a1', 'v_a2', 'v_g1', 'v_g2', 'v_k_k', 'v_k_a', 'v_r_k', 'v_ln_x_w', 'v_ln_x_b', 'v_w_out', 'v_ffn2_norm', 'v_ffn2_w_gate', 'v_ffn2_w_up', 'v_ffn2_w_down']
TWIN_OUTPUTS = ['loss', 'grad_x', 'grad_ffn1_norm', 'grad_ffn1_w_gate', 'grad_ffn1_w_up', 'grad_ffn1_w_down', 'grad_mix_norm', 'grad_w_in', 'grad_q_norm', 'grad_k_norm', 'grad_mu_r', 'grad_mu_k', 'grad_mu_v', 'grad_mu_w', 'grad_mu_a', 'grad_mu_g', 'grad_w0', 'grad_w1', 'grad_w2', 'grad_a0', 'grad_a1', 'grad_a2', 'grad_g1', 'grad_g2', 'grad_k_k', 'grad_k_a', 'grad_r_k', 'grad_ln_x_w', 'grad_ln_x_b', 'grad_w_out', 'grad_ffn2_norm', 'grad_ffn2_w_gate', 'grad_ffn2_w_up', 'grad_ffn2_w_down', 'delta_ffn1_norm', 'delta_ffn1_w_gate', 'delta_ffn1_w_up', 'delta_ffn1_w_down', 'delta_mix_norm', 'delta_w_in', 'delta_q_norm', 'delta_k_norm', 'delta_mu_r', 'delta_mu_k', 'delta_mu_v', 'delta_mu_w', 'delta_mu_a', 'delta_mu_g', 'delta_w0', 'delta_w1', 'delta_w2', 'delta_a0', 'delta_a1', 'delta_a2', 'delta_g1', 'delta_g2', 'delta_k_k', 'delta_k_a', 'delta_r_k', 'delta_ln_x_w', 'delta_ln_x_b', 'delta_w_out', 'delta_ffn2_norm', 'delta_ffn2_w_gate', 'delta_ffn2_w_up', 'delta_ffn2_w_down', 'new_m_ffn1_norm', 'new_m_ffn1_w_gate', 'new_m_ffn1_w_up', 'new_m_ffn1_w_down', 'new_m_mix_norm', 'new_m_w_in', 'new_m_q_norm', 'new_m_k_norm', 'new_m_mu_r', 'new_m_mu_k', 'new_m_mu_v', 'new_m_mu_w', 'new_m_mu_a', 'new_m_mu_g', 'new_m_w0', 'new_m_w1', 'new_m_w2', 'new_m_a0', 'new_m_a1', 'new_m_a2', 'new_m_g1', 'new_m_g2', 'new_m_k_k', 'new_m_k_a', 'new_m_r_k', 'new_m_ln_x_w', 'new_m_ln_x_b', 'new_m_w_out', 'new_m_ffn2_norm', 'new_m_ffn2_w_gate', 'new_m_ffn2_w_up', 'new_m_ffn2_w_down', 'new_v_ffn1_norm', 'new_v_ffn1_w_gate', 'new_v_ffn1_w_up', 'new_v_ffn1_w_down', 'new_v_mix_norm', 'new_v_w_in', 'new_v_q_norm', 'new_v_k_norm', 'new_v_mu_r', 'new_v_mu_k', 'new_v_mu_v', 'new_v_mu_w', 'new_v_mu_a', 'new_v_mu_g', 'new_v_w0', 'new_v_w1', 'new_v_w2', 'new_v_a0', 'new_v_a1', 'new_v_a2', 'new_v_g1', 'new_v_g2', 'new_v_k_k', 'new_v_k_a', 'new_v_r_k', 'new_v_ln_x_w', 'new_v_ln_x_b', 'new_v_w_out', 'new_v_ffn2_norm', 'new_v_ffn2_w_gate', 'new_v_ffn2_w_up', 'new_v_ffn2_w_down']
TWIN_LEAF_KINDS = {'loss': 'loss', 'grad_x': 'grad_x', 'grad_ffn1_norm': 'grad_w', 'grad_ffn1_w_gate': 'grad_w', 'grad_ffn1_w_up': 'grad_w', 'grad_ffn1_w_down': 'grad_w', 'grad_mix_norm': 'grad_w', 'grad_w_in': 'grad_w', 'grad_q_norm': 'grad_w', 'grad_k_norm': 'grad_w', 'grad_mu_r': 'grad_w', 'grad_mu_k': 'grad_w', 'grad_mu_v': 'grad_w', 'grad_mu_w': 'grad_w', 'grad_mu_a': 'grad_w', 'grad_mu_g': 'grad_w', 'grad_w0': 'grad_w', 'grad_w1': 'grad_w', 'grad_w2': 'grad_w', 'grad_a0': 'grad_w', 'grad_a1': 'grad_w', 'grad_a2': 'grad_w', 'grad_g1': 'grad_w', 'grad_g2': 'grad_w', 'grad_k_k': 'grad_w', 'grad_k_a': 'grad_w', 'grad_r_k': 'grad_w', 'grad_ln_x_w': 'grad_w', 'grad_ln_x_b': 'grad_w', 'grad_w_out': 'grad_w', 'grad_ffn2_norm': 'grad_w', 'grad_ffn2_w_gate': 'grad_w', 'grad_ffn2_w_up': 'grad_w', 'grad_ffn2_w_down': 'grad_w', 'delta_ffn1_norm': 'delta_w', 'delta_ffn1_w_gate': 'delta_w', 'delta_ffn1_w_up': 'delta_w', 'delta_ffn1_w_down': 'delta_w', 'delta_mix_norm': 'delta_w', 'delta_w_in': 'delta_w', 'delta_q_norm': 'delta_w', 'delta_k_norm': 'delta_w', 'delta_mu_r': 'delta_w', 'delta_mu_k': 'delta_w', 'delta_mu_v': 'delta_w', 'delta_mu_w': 'delta_w', 'delta_mu_a': 'delta_w', 'delta_mu_g': 'delta_w', 'delta_w0': 'delta_w', 'delta_w1': 'delta_w', 'delta_w2': 'delta_w', 'delta_a0': 'delta_w', 'delta_a1': 'delta_w', 'delta_a2': 'delta_w', 'delta_g1': 'delta_w', 'delta_g2': 'delta_w', 'delta_k_k': 'delta_w', 'delta_k_a': 'delta_w', 'delta_r_k': 'delta_w', 'delta_ln_x_w': 'delta_w', 'delta_ln_x_b': 'delta_w', 'delta_w_out': 'delta_w', 'delta_ffn2_norm': 'delta_w', 'delta_ffn2_w_gate': 'delta_w', 'delta_ffn2_w_up': 'delta_w', 'delta_ffn2_w_down': 'delta_w', 'new_m_ffn1_norm': 'new_m', 'new_m_ffn1_w_gate': 'new_m', 'new_m_ffn1_w_up': 'new_m', 'new_m_ffn1_w_down': 'new_m', 'new_m_mix_norm': 'new_m', 'new_m_w_in': 'new_m', 'new_m_q_norm': 'new_m', 'new_m_k_norm': 'new_m', 'new_m_mu_r': 'new_m', 'new_m_mu_k': 'new_m', 'new_m_mu_v': 'new_m', 'new_m_mu_w': 'new_m', 'new_m_mu_a': 'new_m', 'new_m_mu_g': 'new_m', 'new_m_w0': 'new_m', 'new_m_w1': 'new_m', 'new_m_w2': 'new_m', 'new_m_a0': 'new_m', 'new_m_a1': 'new_m', 'new_m_a2': 'new_m', 'new_m_g1': 'new_m', 'new_m_g2': 'new_m', 'new_m_k_k': 'new_m', 'new_m_k_a': 'new_m', 'new_m_r_k': 'new_m', 'new_m_ln_x_w': 'new_m', 'new_m_ln_x_b': 'new_m', 'new_m_w_out': 'new_m', 'new_m_ffn2_norm': 'new_m', 'new_m_ffn2_w_gate': 'new_m', 'new_m_ffn2_w_up': 'new_m', 'new_m_ffn2_w_down': 'new_m', 'new_v_ffn1_norm': 'new_v', 'new_v_ffn1_w_gate': 'new_v', 'new_v_ffn1_w_up': 'new_v', 'new_v_ffn1_w_down': 'new_v', 'new_v_mix_norm': 'new_v', 'new_v_w_in': 'new_v', 'new_v_q_norm': 'new_v', 'new_v_k_norm': 'new_v', 'new_v_mu_r': 'new_v', 'new_v_mu_k': 'new_v', 'new_v_mu_v': 'new_v', 'new_v_mu_w': 'new_v', 'new_v_mu_a': 'new_v', 'new_v_mu_g': 'new_v', 'new_v_w0': 'new_v', 'new_v_w1': 'new_v', 'new_v_w2': 'new_v', 'new_v_a0': 'new_v', 'new_v_a1': 'new_v', 'new_v_a2': 'new_v', 'new_v_g1': 'new_v', 'new_v_g2': 'new_v', 'new_v_k_k': 'new_v', 'new_v_k_a': 'new_v', 'new_v_r_k': 'new_v', 'new_v_ln_x_w': 'new_v', 'new_v_ln_x_b': 'new_v', 'new_v_w_out': 'new_v', 'new_v_ffn2_norm': 'new_v', 'new_v_ffn2_w_gate': 'new_v', 'new_v_ffn2_w_up': 'new_v', 'new_v_ffn2_w_down': 'new_v'}


def _forward(args):
    return _fwd_reference(*[args[k] for k in FWD_PARAMS])


def _output_shape():
    out = _jax.eval_shape(lambda: _forward(_fwd_setup_inputs(0)))
    return out.shape, out.dtype

N_MICROBATCH = 1
ADAM_LR = 0.001
ADAM_B1 = 0.9
ADAM_B2 = 0.999
ADAM_EPS = 1e-08
ADAM_WD = 0.01
ADAM_STEP = 10
PER_EXAMPLE_BATCH_AXIS = {'x': 0, 'loss_target': 0}
SHARED_INPUTS = []
_WEIGHT_DTYPES = {'ffn1_norm': _jnp.float32, 'ffn1_w_gate': _jnp.float32, 'ffn1_w_up': _jnp.float32, 'ffn1_w_down': _jnp.float32, 'mix_norm': _jnp.float32, 'w_in': _jnp.float32, 'q_norm': _jnp.float32, 'k_norm': _jnp.float32, 'mu_r': _jnp.float32, 'mu_k': _jnp.float32, 'mu_v': _jnp.float32, 'mu_w': _jnp.float32, 'mu_a': _jnp.float32, 'mu_g': _jnp.float32, 'w0': _jnp.float32, 'w1': _jnp.float32, 'w2': _jnp.float32, 'a0': _jnp.float32, 'a1': _jnp.float32, 'a2': _jnp.float32, 'g1': _jnp.float32, 'g2': _jnp.float32, 'k_k': _jnp.float32, 'k_a': _jnp.float32, 'r_k': _jnp.float32, 'ln_x_w': _jnp.float32, 'ln_x_b': _jnp.float32, 'w_out': _jnp.float32, 'ffn2_norm': _jnp.float32, 'ffn2_w_gate': _jnp.float32, 'ffn2_w_up': _jnp.float32, 'ffn2_w_down': _jnp.float32}
MOMENT_SCALE = {'ffn1_norm': 3.033170e+00, 'ffn1_w_gate': 5.303598e-02, 'ffn1_w_up': 5.558155e-02, 'ffn1_w_down': 9.066497e-02, 'mix_norm': 3.022145e-01, 'w_in': 1.109834e-01, 'q_norm': 1.141442e+00, 'k_norm': 1.146801e+00, 'mu_r': 2.623330e-01, 'mu_k': 3.329878e-01, 'mu_v': 2.922773e+00, 'mu_w': 4.004091e-04, 'mu_a': 9.130243e-03, 'mu_g': 3.223315e-01, 'w0': 6.990690e-02, 'w1': 2.198795e-03, 'w2': 5.360408e-03, 'a0': 6.444577e-02, 'a1': 2.396428e-02, 'a2': 6.002552e-02, 'g1': 1.675139e-01, 'g2': 4.054833e+00, 'k_k': 4.983600e-01, 'k_a': 2.372858e-01, 'r_k': 9.718561e-01, 'ln_x_w': 6.930446e+00, 'ln_x_b': 3.072174e-01, 'w_out': 1.400927e-01, 'ffn2_norm': 3.063005e+00, 'ffn2_w_gate': 3.973206e-02, 'ffn2_w_up': 4.548633e-02, 'ffn2_w_down': 7.267620e-02}


def _to_microbatches(a, axis):
    t = _jnp.moveaxis(a, axis, 0)
    t = t.reshape((N_MICROBATCH, t.shape[0] // N_MICROBATCH) + t.shape[1:])
    return _jnp.moveaxis(t, 1, axis + 1)


def setup_inputs(seed: int = 0) -> dict:
    inp = _fwd_setup_inputs(seed)
    key = _jax.random.fold_in(_jax.random.key(seed), 7919)
    shape, _ = _output_shape()
    out = dict(inp)
    out["loss_target"] = _jax.random.normal(_jax.random.fold_in(key, 0), shape, _jnp.float32)
    for i, name in enumerate(TWIN_WEIGHTS):
        w = inp[name].astype(_jnp.float32)
        if MOMENT_SCALE is None:
            s = _jnp.sqrt(_jnp.mean(_jnp.square(w)) + 1e-30)
        else:
            s = MOMENT_SCALE[name]
        km, kv = _jax.random.split(_jax.random.fold_in(key, i + 1))
        out[name] = w
        out["m_" + name] = s * _jax.random.normal(km, w.shape, _jnp.float32)
        out["v_" + name] = (s * s) * _jax.random.uniform(kv, w.shape, _jnp.float32, 0.5, 1.5)
    if N_MICROBATCH > 1:
        for name, axis in PER_EXAMPLE_BATCH_AXIS.items():
            out[name] = _to_microbatches(out[name], axis)
    return {'x': out['x'], 'ffn1_norm': out['ffn1_norm'], 'ffn1_w_gate': out['ffn1_w_gate'], 'ffn1_w_up': out['ffn1_w_up'], 'ffn1_w_down': out['ffn1_w_down'], 'mix_norm': out['mix_norm'], 'w_in': out['w_in'], 'q_norm': out['q_norm'], 'k_norm': out['k_norm'], 'mu_r': out['mu_r'], 'mu_k': out['mu_k'], 'mu_v': out['mu_v'], 'mu_w': out['mu_w'], 'mu_a': out['mu_a'], 'mu_g': out['mu_g'], 'w0': out['w0'], 'w1': out['w1'], 'w2': out['w2'], 'a0': out['a0'], 'a1': out['a1'], 'a2': out['a2'], 'g1': out['g1'], 'g2': out['g2'], 'k_k': out['k_k'], 'k_a': out['k_a'], 'r_k': out['r_k'], 'ln_x_w': out['ln_x_w'], 'ln_x_b': out['ln_x_b'], 'w_out': out['w_out'], 'ffn2_norm': out['ffn2_norm'], 'ffn2_w_gate': out['ffn2_w_gate'], 'ffn2_w_up': out['ffn2_w_up'], 'ffn2_w_down': out['ffn2_w_down'], 'loss_target': out['loss_target'], 'm_ffn1_norm': out['m_ffn1_norm'], 'm_ffn1_w_gate': out['m_ffn1_w_gate'], 'm_ffn1_w_up': out['m_ffn1_w_up'], 'm_ffn1_w_down': out['m_ffn1_w_down'], 'm_mix_norm': out['m_mix_norm'], 'm_w_in': out['m_w_in'], 'm_q_norm': out['m_q_norm'], 'm_k_norm': out['m_k_norm'], 'm_mu_r': out['m_mu_r'], 'm_mu_k': out['m_mu_k'], 'm_mu_v': out['m_mu_v'], 'm_mu_w': out['m_mu_w'], 'm_mu_a': out['m_mu_a'], 'm_mu_g': out['m_mu_g'], 'm_w0': out['m_w0'], 'm_w1': out['m_w1'], 'm_w2': out['m_w2'], 'm_a0': out['m_a0'], 'm_a1': out['m_a1'], 'm_a2': out['m_a2'], 'm_g1': out['m_g1'], 'm_g2': out['m_g2'], 'm_k_k': out['m_k_k'], 'm_k_a': out['m_k_a'], 'm_r_k': out['m_r_k'], 'm_ln_x_w': out['m_ln_x_w'], 'm_ln_x_b': out['m_ln_x_b'], 'm_w_out': out['m_w_out'], 'm_ffn2_norm': out['m_ffn2_norm'], 'm_ffn2_w_gate': out['m_ffn2_w_gate'], 'm_ffn2_w_up': out['m_ffn2_w_up'], 'm_ffn2_w_down': out['m_ffn2_w_down'], 'v_ffn1_norm': out['v_ffn1_norm'], 'v_ffn1_w_gate': out['v_ffn1_w_gate'], 'v_ffn1_w_up': out['v_ffn1_w_up'], 'v_ffn1_w_down': out['v_ffn1_w_down'], 'v_mix_norm': out['v_mix_norm'], 'v_w_in': out['v_w_in'], 'v_q_norm': out['v_q_norm'], 'v_k_norm': out['v_k_norm'], 'v_mu_r': out['v_mu_r'], 'v_mu_k': out['v_mu_k'], 'v_mu_v': out['v_mu_v'], 'v_mu_w': out['v_mu_w'], 'v_mu_a': out['v_mu_a'], 'v_mu_g': out['v_mu_g'], 'v_w0': out['v_w0'], 'v_w1': out['v_w1'], 'v_w2': out['v_w2'], 'v_a0': out['v_a0'], 'v_a1': out['v_a1'], 'v_a2': out['v_a2'], 'v_g1': out['v_g1'], 'v_g2': out['v_g2'], 'v_k_k': out['v_k_k'], 'v_k_a': out['v_k_a'], 'v_r_k': out['v_r_k'], 'v_ln_x_w': out['v_ln_x_w'], 'v_ln_x_b': out['v_ln_x_b'], 'v_w_out': out['v_w_out'], 'v_ffn2_norm': out['v_ffn2_norm'], 'v_ffn2_w_gate': out['v_ffn2_w_gate'], 'v_ffn2_w_up': out['v_ffn2_w_up'], 'v_ffn2_w_down': out['v_ffn2_w_down']}


def _loss(weights, diff, rest, loss_target):
    with _jax.named_scope("forward"):
        args = {**rest, TWIN_DIFF_INPUT: diff, **{k: w.astype(_WEIGHT_DTYPES[k]) for k, w in weights.items()}}
        y = _forward(args)
    with _jax.named_scope("loss_head"):
        err = _jnp.square(y.astype(_jnp.float32) - loss_target)
        return 0.5 * _jnp.sum(_jnp.mean(err, axis=-1)) if err.ndim else 0.5 * err


def _adamw(w, g, m, v):
    m = ADAM_B1 * m + (1.0 - ADAM_B1) * g
    v = ADAM_B2 * v + (1.0 - ADAM_B2) * _jnp.square(g)
    m_hat = m / (1.0 - ADAM_B1 ** ADAM_STEP)
    v_hat = v / (1.0 - ADAM_B2 ** ADAM_STEP)
    delta = -ADAM_LR * (m_hat / (_jnp.sqrt(v_hat) + ADAM_EPS) + ADAM_WD * w)
    return delta, m, v


def reference(x, ffn1_norm, ffn1_w_gate, ffn1_w_up, ffn1_w_down, mix_norm, w_in, q_norm, k_norm, mu_r, mu_k, mu_v, mu_w, mu_a, mu_g, w0, w1, w2, a0, a1, a2, g1, g2, k_k, k_a, r_k, ln_x_w, ln_x_b, w_out, ffn2_norm, ffn2_w_gate, ffn2_w_up, ffn2_w_down, loss_target, m_ffn1_norm, m_ffn1_w_gate, m_ffn1_w_up, m_ffn1_w_down, m_mix_norm, m_w_in, m_q_norm, m_k_norm, m_mu_r, m_mu_k, m_mu_v, m_mu_w, m_mu_a, m_mu_g, m_w0, m_w1, m_w2, m_a0, m_a1, m_a2, m_g1, m_g2, m_k_k, m_k_a, m_r_k, m_ln_x_w, m_ln_x_b, m_w_out, m_ffn2_norm, m_ffn2_w_gate, m_ffn2_w_up, m_ffn2_w_down, v_ffn1_norm, v_ffn1_w_gate, v_ffn1_w_up, v_ffn1_w_down, v_mix_norm, v_w_in, v_q_norm, v_k_norm, v_mu_r, v_mu_k, v_mu_v, v_mu_w, v_mu_a, v_mu_g, v_w0, v_w1, v_w2, v_a0, v_a1, v_a2, v_g1, v_g2, v_k_k, v_k_a, v_r_k, v_ln_x_w, v_ln_x_b, v_w_out, v_ffn2_norm, v_ffn2_w_gate, v_ffn2_w_up, v_ffn2_w_down):
    given = dict(x=x, ffn1_norm=ffn1_norm, ffn1_w_gate=ffn1_w_gate, ffn1_w_up=ffn1_w_up, ffn1_w_down=ffn1_w_down, mix_norm=mix_norm, w_in=w_in, q_norm=q_norm, k_norm=k_norm, mu_r=mu_r, mu_k=mu_k, mu_v=mu_v, mu_w=mu_w, mu_a=mu_a, mu_g=mu_g, w0=w0, w1=w1, w2=w2, a0=a0, a1=a1, a2=a2, g1=g1, g2=g2, k_k=k_k, k_a=k_a, r_k=r_k, ln_x_w=ln_x_w, ln_x_b=ln_x_b, w_out=w_out, ffn2_norm=ffn2_norm, ffn2_w_gate=ffn2_w_gate, ffn2_w_up=ffn2_w_up, ffn2_w_down=ffn2_w_down, loss_target=loss_target, m_ffn1_norm=m_ffn1_norm, m_ffn1_w_gate=m_ffn1_w_gate, m_ffn1_w_up=m_ffn1_w_up, m_ffn1_w_down=m_ffn1_w_down, m_mix_norm=m_mix_norm, m_w_in=m_w_in, m_q_norm=m_q_norm, m_k_norm=m_k_norm, m_mu_r=m_mu_r, m_mu_k=m_mu_k, m_mu_v=m_mu_v, m_mu_w=m_mu_w, m_mu_a=m_mu_a, m_mu_g=m_mu_g, m_w0=m_w0, m_w1=m_w1, m_w2=m_w2, m_a0=m_a0, m_a1=m_a1, m_a2=m_a2, m_g1=m_g1, m_g2=m_g2, m_k_k=m_k_k, m_k_a=m_k_a, m_r_k=m_r_k, m_ln_x_w=m_ln_x_w, m_ln_x_b=m_ln_x_b, m_w_out=m_w_out, m_ffn2_norm=m_ffn2_norm, m_ffn2_w_gate=m_ffn2_w_gate, m_ffn2_w_up=m_ffn2_w_up, m_ffn2_w_down=m_ffn2_w_down, v_ffn1_norm=v_ffn1_norm, v_ffn1_w_gate=v_ffn1_w_gate, v_ffn1_w_up=v_ffn1_w_up, v_ffn1_w_down=v_ffn1_w_down, v_mix_norm=v_mix_norm, v_w_in=v_w_in, v_q_norm=v_q_norm, v_k_norm=v_k_norm, v_mu_r=v_mu_r, v_mu_k=v_mu_k, v_mu_v=v_mu_v, v_mu_w=v_mu_w, v_mu_a=v_mu_a, v_mu_g=v_mu_g, v_w0=v_w0, v_w1=v_w1, v_w2=v_w2, v_a0=v_a0, v_a1=v_a1, v_a2=v_a2, v_g1=v_g1, v_g2=v_g2, v_k_k=v_k_k, v_k_a=v_k_a, v_r_k=v_r_k, v_ln_x_w=v_ln_x_w, v_ln_x_b=v_ln_x_b, v_w_out=v_w_out, v_ffn2_norm=v_ffn2_norm, v_ffn2_w_gate=v_ffn2_w_gate, v_ffn2_w_up=v_ffn2_w_up, v_ffn2_w_down=v_ffn2_w_down)
    weights = {n: given[n] for n in TWIN_WEIGHTS}
    shared = {n: given[n] for n in SHARED_INPUTS}
    per_example = {n: given[n] for n in ['x']}
    grad_fn = _jax.value_and_grad(_loss, argnums=(0, 1))

    def one_microbatch(ex, loss_target):
        ex = dict(ex)
        diff = ex.pop(TWIN_DIFF_INPUT)
        return grad_fn(weights, diff, {**shared, **ex}, loss_target)

    if N_MICROBATCH == 1:
        loss, (grad_w, grad_x) = one_microbatch(per_example, given["loss_target"])
    else:
        def body(carry, xs):
            loss_sum, grad_sum = carry
            l_k, (gw_k, gx_k) = one_microbatch(xs[0], xs[1])
            with _jax.named_scope("update"):
                return (loss_sum + l_k, _jax.tree.map(_jnp.add, grad_sum, gw_k)), gx_k

        init = (_jnp.zeros((), _jnp.float32), _jax.tree.map(_jnp.zeros_like, weights))
        (loss, grad_w), grad_x = _jax.lax.scan(body, init, (per_example, given["loss_target"]))
    with _jax.named_scope("update"):
        delta_w, new_m, new_v = {}, {}, {}
        for n in TWIN_WEIGHTS:
            delta_w[n], new_m[n], new_v[n] = _adamw(weights[n], grad_w[n], given["m_" + n], given["v_" + n])
    return (loss, grad_x, *[grad_w[n] for n in TWIN_WEIGHTS], *[delta_w[n] for n in TWIN_WEIGHTS],
            *[new_m[n] for n in TWIN_WEIGHTS], *[new_v[n] for n in TWIN_WEIGHTS])
```

```python
import functools

import jax
import jax.numpy as jnp
from jax import lax
from jax.experimental import pallas as pl
from jax.experimental.pallas import tpu as pltpu

F32 = jnp.float32
BF16 = jnp.bfloat16
HIGHEST = lax.Precision.HIGHEST
MESH = pl.DeviceIdType.MESH

RMS_EPS = 1e-6
GN_EPS = 64e-5
NEG_INF = -1e30
FFN_RESIDUAL = 0.5
HEAD_DIM = 64
ATT_BLOCK = 128
DILATIONS = (1, 4, 16)
SCAN_CHUNK = 64
TOKEN_TILE = 256

ADAM_LR = 0.001
ADAM_B1 = 0.9
ADAM_B2 = 0.999
ADAM_EPS = 1e-08
ADAM_WD = 0.01
ADAM_STEP = 10

VMEM_FULL = pl.BlockSpec(memory_space=pltpu.VMEM)
ANY = pl.BlockSpec(memory_space=pl.ANY)


VMEM_LIMIT = 56 * 1024 * 1024


def _params(*sem):
    return pltpu.CompilerParams(dimension_semantics=sem, vmem_limit_bytes=VMEM_LIMIT)


def _dot(a, b, dims):
    return lax.dot_general(a.astype(BF16), b.astype(BF16), (dims, ((), ())), preferred_element_type=F32)


def _dot_nn(a, b):
    return _dot(a, b, ((1,), (0,)))


def _dot_nt(a, b):
    return _dot(a, b, ((1,), (1,)))


def _dot_tn(a, b):
    return _dot(a, b, ((0,), (0,)))


@jax.custom_vjp
def _mm(a, b):
    return _dot_nn(a, b)


def _mm_fwd(a, b):
    return _dot_nn(a, b), (a, b)


def _mm_bwd(res, g):
    a, b = res
    return _dot_nt(g, b).astype(a.dtype), _dot_tn(a, g).astype(b.dtype)


_mm.defvjp(_mm_fwd, _mm_bwd)


def _bdot(a, b, ca, cb):
    return lax.dot_general(a.astype(BF16), b.astype(BF16), (((ca,), (cb,)), ((0,), (0,))), preferred_element_type=F32)


@jax.custom_vjp
def _bmm_nt(a, b):
    return _bdot(a, b, 2, 2)


def _bmm_nt_fwd(a, b):
    return _bdot(a, b, 2, 2), (a, b)


def _bmm_nt_bwd(res, g):
    a, b = res
    return _bdot(g, b, 2, 1), _bdot(g, a, 1, 1)


_bmm_nt.defvjp(_bmm_nt_fwd, _bmm_nt_bwd)


@jax.custom_vjp
def _bmm_nn(a, b):
    return _bdot(a, b, 2, 1)


def _bmm_nn_fwd(a, b):
    return _bdot(a, b, 2, 1), (a, b)


def _bmm_nn_bwd(res, g):
    a, b = res
    return _bdot(g, b, 2, 2), _bdot(a, g, 1, 1)


_bmm_nn.defvjp(_bmm_nn_fwd, _bmm_nn_bwd)


def _hdot(a, b, ca, cb):
    return lax.dot_general(a, b, (((ca,), (cb,)), ((0,), (0,))), precision=HIGHEST, preferred_element_type=F32)


def _sigmoid(x):
    return 1.0 / (1.0 + jnp.exp(-x))


def _rms(x):
    return lax.rsqrt(jnp.mean(x * x, axis=-1, keepdims=True) + RMS_EPS)


def _ffn_fwd(x, norm, wg, wu, wd, name):
    t, d = x.shape
    nc = wg.shape[0]
    tm = TOKEN_TILE

    def body(x_ref, n_ref, wg_ref, wu_ref, wd_ref, o_ref):
        xv = x_ref[...]
        h = (xv * _rms(xv) * n_ref[...]).astype(BF16)
        acc = jnp.zeros((tm, d), F32)
        for c in range(nc):
            g = jnp.dot(h, wg_ref[c], preferred_element_type=F32)
            u = jnp.dot(h, wu_ref[c], preferred_element_type=F32)
            a = (g * _sigmoid(g) * u).astype(BF16)
            acc = acc + jnp.dot(a, wd_ref[c], preferred_element_type=F32)
        o_ref[...] = xv + FFN_RESIDUAL * acc

    tile = pl.BlockSpec((tm, d), lambda i: (i, 0))
    return pl.pallas_call(
        body, name=name, grid=(t // tm,), out_shape=jax.ShapeDtypeStruct((t, d), F32),
        in_specs=[tile, pl.BlockSpec((1, d), lambda i: (0, 0)), VMEM_FULL, VMEM_FULL, VMEM_FULL],
        out_specs=tile, compiler_params=_params("arbitrary"),
    )(x, norm, wg, wu, wd)


def _rmsnorm_bwd(xv, gain, dh):
    rs = _rms(xv)
    xn = xv * rs
    dxn = dh * gain
    dx = rs * (dxn - xn * jnp.mean(dxn * xn, axis=-1, keepdims=True))
    return dx, jnp.sum(dh * xn, axis=0, keepdims=True)


def _ffn_bwd(x, norm, wg, wu, wd, dy, name):
    t, d = x.shape
    nc, _, fc = wg.shape
    tm = TOKEN_TILE
    nt = t // tm

    def body(x_ref, n_ref, wg_ref, wu_ref, wd_ref, dy_ref, dx_ref, dn_ref, dwg_ref, dwu_ref, dwd_ref, dh_ref):
        c, i = pl.program_id(0), pl.program_id(1)
        rows = pl.ds(pl.multiple_of(i * tm, tm), tm)
        xv = x_ref[...]
        gain = n_ref[...]
        h = (xv * _rms(xv) * gain).astype(BF16)
        dy = dy_ref[...]
        dyb = (FFN_RESIDUAL * dy).astype(BF16)
        g = jnp.dot(h, wg_ref[0], preferred_element_type=F32)
        u = jnp.dot(h, wu_ref[0], preferred_element_type=F32)
        sg = _sigmoid(g)
        s = g * sg
        a = (s * u).astype(BF16)
        da = _dot_nt(dyb, wd_ref[0])
        dub = (da * s).astype(BF16)
        dgb = (da * u * (sg * (1.0 + g * (1.0 - sg)))).astype(BF16)
        dwd_c = _dot_tn(a, dyb)
        dwg_c = _dot_tn(h, dgb)
        dwu_c = _dot_tn(h, dub)
        dh_c = _dot_nt(dgb, wg_ref[0]) + _dot_nt(dub, wu_ref[0])

        @pl.when(i == 0)
        def _():
            dwd_ref[0] = dwd_c
            dwg_ref[0] = dwg_c
            dwu_ref[0] = dwu_c

        @pl.when(i > 0)
        def _():
            dwd_ref[0] += dwd_c
            dwg_ref[0] += dwg_c
            dwu_ref[0] += dwu_c

        @pl.when(c == 0)
        def _():
            dh_ref[rows, :] = dh_c

        @pl.when(c > 0)
        def _():
            dh_ref[rows, :] += dh_c

        @pl.when(c == nc - 1)
        def _():
            dx, dn = _rmsnorm_bwd(xv, gain, dh_ref[rows, :])
            dx_ref[...] = dx + dy

            @pl.when(i == 0)
            def _():
                dn_ref[...] = dn

            @pl.when(i > 0)
            def _():
                dn_ref[...] += dn

    tile = pl.BlockSpec((tm, d), lambda c, i: (i, 0))
    row = pl.BlockSpec((1, d), lambda c, i: (0, 0))
    wcol = pl.BlockSpec((1, d, fc), lambda c, i: (c, 0, 0))
    wrow = pl.BlockSpec((1, fc, d), lambda c, i: (c, 0, 0))
    last = pl.BlockSpec((tm, d), lambda c, i: (jnp.where(c == nc - 1, i, 0), 0))
    return pl.pallas_call(
        body, name=name, grid=(nc, nt),
        out_shape=(jax.ShapeDtypeStruct((t, d), F32), jax.ShapeDtypeStruct((1, d), F32),
                   jax.ShapeDtypeStruct(wg.shape, F32), jax.ShapeDtypeStruct(wu.shape, F32),
                   jax.ShapeDtypeStruct(wd.shape, F32)),
        in_specs=[tile, row, wcol, wcol, wrow, tile],
        out_specs=(last, row, wcol, wcol, wrow),
        scratch_shapes=[pltpu.VMEM((t, d), F32)],
        compiler_params=_params("arbitrary", "arbitrary"),
    )(x, norm, wg, wu, wd, dy)


def _proj_fwd(x, norm, w):
    t, d = x.shape
    nc, _, ncol = w.shape
    tm = TOKEN_TILE

    def body(x_ref, n_ref, w_ref, o_ref):
        xv = x_ref[...]
        h = (xv * _rms(xv) * n_ref[...]).astype(BF16)
        for c in range(nc):
            o_ref[:, c * ncol:(c + 1) * ncol] = jnp.dot(h, w_ref[c], preferred_element_type=F32)

    return pl.pallas_call(
        body, name="proj_fwd", grid=(t // tm,), out_shape=jax.ShapeDtypeStruct((t, nc * ncol), F32),
        in_specs=[pl.BlockSpec((tm, d), lambda i: (i, 0)), pl.BlockSpec((1, d), lambda i: (0, 0)), VMEM_FULL],
        out_specs=pl.BlockSpec((tm, nc * ncol), lambda i: (i, 0)), compiler_params=_params("arbitrary"),
    )(x, norm, w)


def _proj_bwd(x, norm, w, dpa, dpb, dpc, dres):
    t, d = x.shape
    nc, _, ncol = w.shape
    tm = TOKEN_TILE
    nt = t // tm

    def body(x_ref, n_ref, w_ref, dpa_ref, dpb_ref, dpc_ref, dres_ref, dx_ref, dn_ref, dw_ref, dh_ref):
        c, i = pl.program_id(0), pl.program_id(1)
        rows = pl.ds(pl.multiple_of(i * tm, tm), tm)
        xv = x_ref[...]
        gain = n_ref[...]
        h = (xv * _rms(xv) * gain).astype(BF16)
        dpv = (dpa_ref[...] + dpb_ref[...] + dpc_ref[...]).astype(BF16)
        dw_c = _dot_tn(h, dpv)
        dh_c = _dot_nt(dpv, w_ref[0])

        @pl.when(i == 0)
        def _():
            dw_ref[0] = dw_c

        @pl.when(i > 0)
        def _():
            dw_ref[0] += dw_c

        @pl.when(c == 0)
        def _():
            dh_ref[rows, :] = dh_c

        @pl.when(c > 0)
        def _():
            dh_ref[rows, :] += dh_c

        @pl.when(c == nc - 1)
        def _():
            dx, dn = _rmsnorm_bwd(xv, gain, dh_ref[rows, :])
            dx_ref[...] = dx + dres_ref[...]

            @pl.when(i == 0)
            def _():
                dn_ref[...] = dn

            @pl.when(i > 0)
            def _():
                dn_ref[...] += dn

    tile = pl.BlockSpec((tm, d), lambda c, i: (i, 0))
    row = pl.BlockSpec((1, d), lambda c, i: (0, 0))
    wcol = pl.BlockSpec((1, d, ncol), lambda c, i: (c, 0, 0))
    ptile = pl.BlockSpec((tm, ncol), lambda c, i: (i, c))
    last = pl.BlockSpec((tm, d), lambda c, i: (jnp.where(c == nc - 1, i, 0), 0))
    return pl.pallas_call(
        body, name="proj_bwd", grid=(nc, nt),
        out_shape=(jax.ShapeDtypeStruct((t, d), F32), jax.ShapeDtypeStruct((1, d), F32),
                   jax.ShapeDtypeStruct(w.shape, F32)),
        in_specs=[tile, row, wcol, ptile, ptile, ptile, tile],
        out_specs=(last, row, wcol),
        scratch_shapes=[pltpu.VMEM((t, d), F32)],
        compiler_params=_params("arbitrary", "arbitrary"),
    )(x, norm, w, dpa, dpb, dpc, dres)


def _mixout_fwd(x, att, opg, gate, w):
    t, d = x.shape
    half = att.shape[1]
    tm = TOKEN_TILE

    def body(x_ref, att_ref, opg_ref, g_ref, w_ref, o_ref):
        mix = jnp.concatenate([att_ref[...], opg_ref[...] * g_ref[...]], axis=-1).astype(BF16)
        o_ref[...] = x_ref[...] + jnp.dot(mix, w_ref[...], preferred_element_type=F32)

    tile = pl.BlockSpec((tm, d), lambda i: (i, 0))
    htile = pl.BlockSpec((tm, half), lambda i: (i, 0))
    return pl.pallas_call(
        body, name="mixout_fwd", grid=(t // tm,), out_shape=jax.ShapeDtypeStruct((t, d), F32),
        in_specs=[tile, htile, htile, htile, VMEM_FULL], out_specs=tile, compiler_params=_params("arbitrary"),
    )(x, att, opg, gate, w)


def _mixout_bwd(att, opg, gate, w, dy):
    t, half = att.shape
    d = dy.shape[1]
    tm = TOKEN_TILE

    def body(att_ref, opg_ref, g_ref, w_ref, dy_ref, datt_ref, dopg_ref, dg_ref, dw_ref):
        i = pl.program_id(0)
        opg_v, g_v = opg_ref[...], g_ref[...]
        mix = jnp.concatenate([att_ref[...], opg_v * g_v], axis=-1).astype(BF16)
        dyb = dy_ref[...].astype(BF16)
        dmix = _dot_nt(dyb, w_ref[...])
        dw = _dot_tn(mix, dyb)
        datt_ref[...] = dmix[:, :half]
        drw = dmix[:, half:]
        dopg_ref[...] = drw * g_v
        dg_ref[...] = drw * opg_v

        @pl.when(i == 0)
        def _():
            dw_ref[...] = dw

        @pl.when(i > 0)
        def _():
            dw_ref[...] += dw

    tile = pl.BlockSpec((tm, d), lambda i: (i, 0))
    htile = pl.BlockSpec((tm, half), lambda i: (i, 0))
    hshape = jax.ShapeDtypeStruct((t, half), F32)
    return pl.pallas_call(
        body, name="mixout_bwd", grid=(t // tm,),
        out_shape=(hshape, hshape, hshape, jax.ShapeDtypeStruct(w.shape, F32)),
        in_specs=[htile, htile, htile, VMEM_FULL, tile],
        out_specs=(htile, htile, htile, pl.BlockSpec(w.shape, lambda i: (0, 0))),
        compiler_params=_params("arbitrary"),
    )(att, opg, gate, w, dy)


def _loss_head(y, target):
    t, d = y.shape
    tm = TOKEN_TILE

    def body(y_ref, t_ref, dy_ref, loss_ref):
        i = pl.program_id(0)
        err = y_ref[...] - t_ref[...]
        dy_ref[...] = err * (1.0 / d)
        part = 0.5 * jnp.sum(jnp.mean(err * err, axis=-1, keepdims=True), axis=0, keepdims=True)

        @pl.when(i == 0)
        def _():
            loss_ref[...] = jnp.zeros_like(loss_ref)

        loss_ref[...] += jnp.broadcast_to(part, loss_ref.shape)

    tile = pl.BlockSpec((tm, d), lambda i: (i, 0))
    return pl.pallas_call(
        body, name="loss_head", grid=(t // tm,),
        out_shape=(jax.ShapeDtypeStruct((t, d), F32), jax.ShapeDtypeStruct((1, 128), F32)),
        in_specs=[tile, tile], out_specs=(tile, pl.BlockSpec((1, 128), lambda i: (0, 0))),
        compiler_params=_params("arbitrary"),
    )(y, target)


def _att_block(q, kp, kc, vp, vc, qn, kn, has_prev):
    blk = q.shape[1]

    def hn(v, gain):
        return v * _rms(v) * gain

    qh, khp, khc = hn(q, qn), hn(kp, kn), hn(kc, kn)
    scale = HEAD_DIM ** -0.5
    sp = _bmm_nt(qh, khp) * scale
    sc = _bmm_nt(qh, khc) * scale
    qi = lax.broadcasted_iota(jnp.int32, (blk, blk), 0)
    kj = lax.broadcasted_iota(jnp.int32, (blk, blk), 1)
    sp = jnp.where((kj >= qi) & has_prev, sp, NEG_INF)
    sc = jnp.where(kj <= qi, sc, NEG_INF)
    m = lax.stop_gradient(jnp.maximum(jnp.max(sp, axis=-1, keepdims=True), jnp.max(sc, axis=-1, keepdims=True)))
    pp, pc = jnp.exp(sp - m), jnp.exp(sc - m)
    den = jnp.sum(pp, axis=-1, keepdims=True) + jnp.sum(pc, axis=-1, keepdims=True)
    o = (_bmm_nn(pp, vp) + _bmm_nn(pc, vc)) / den
    return o, m + jnp.log(den)


def _att_specs(g, length, gt):
    blk = ATT_BLOCK
    cur = pl.BlockSpec((gt, blk, HEAD_DIM), lambda gi, n: (gi, n, 0))
    prev = pl.BlockSpec((gt, blk, HEAD_DIM), lambda gi, n: (gi, jnp.maximum(n - 1, 0), 0))
    gain = pl.BlockSpec((1, 1, HEAD_DIM), lambda gi, n: (0, 0, 0))
    col = pl.BlockSpec((gt, blk, 1), lambda gi, n: (gi, n, 0))
    return cur, prev, gain, col


def _att_fwd(q, k, v, qn, kn, name):
    g, length, dh = q.shape
    gt = g // 8
    cur, prev, gain, col = _att_specs(g, length, gt)

    def body(q_ref, kp_ref, kc_ref, vp_ref, vc_ref, qn_ref, kn_ref, o_ref, lse_ref):
        o, lse = _att_block(q_ref[...], kp_ref[...], kc_ref[...], vp_ref[...], vc_ref[...],
                            qn_ref[...], kn_ref[...], pl.program_id(1) > 0)
        o_ref[...] = o
        lse_ref[...] = lse

    return pl.pallas_call(
        body, name=name, grid=(g // gt, length // ATT_BLOCK),
        out_shape=(jax.ShapeDtypeStruct(q.shape, F32), jax.ShapeDtypeStruct((g, length, 1), F32)),
        in_specs=[cur, prev, cur, prev, cur, gain, gain], out_specs=(cur, col),
        compiler_params=_params("arbitrary", "arbitrary"),
    )(q, k, k, v, v, qn, kn)


def _att_bwd(q, k, v, qn, kn, do, dlse, dqn0, dkn0, name):
    g, length, dh = q.shape
    gt = g // 8
    blk = ATT_BLOCK
    cur, prev, gain, col = _att_specs(g, length, gt)
    whole = pl.BlockSpec((gt, length, dh), lambda gi, n: (gi, 0, 0))

    def body(q_ref, kp_ref, kc_ref, vp_ref, vc_ref, qn_ref, kn_ref, do_ref, dl_ref, dqn0_ref, dkn0_ref,
             dq_ref, dk_ref, dv_ref, dqn_ref, dkn_ref):
        gi, n = pl.program_id(0), pl.program_id(1)
        has_prev = n > 0
        fn = functools.partial(_att_block, has_prev=has_prev)
        _, vjp = jax.vjp(fn, q_ref[...], kp_ref[...], kc_ref[...], vp_ref[...], vc_ref[...], qn_ref[...], kn_ref[...])
        dq, dkp, dkc, dvp, dvc, dqn, dkn = vjp((do_ref[...], dl_ref[...]))
        dq_ref[...] = dq
        here = pl.ds(pl.multiple_of(n * blk, blk), blk)
        dk_ref[:, here, :] = dkc
        dv_ref[:, here, :] = dvc

        @pl.when(has_prev)
        def _():
            before = pl.ds(pl.multiple_of((n - 1) * blk, blk), blk)
            dk_ref[:, before, :] += dkp
            dv_ref[:, before, :] += dvp

        @pl.when((gi == 0) & (n == 0))
        def _():
            dqn_ref[...] = dqn0_ref[...]
            dkn_ref[...] = dkn0_ref[...]

        dqn_ref[...] += dqn
        dkn_ref[...] += dkn

    gshape = jax.ShapeDtypeStruct((1, 1, dh), F32)
    return pl.pallas_call(
        body, name=name, grid=(g // gt, length // blk),
        out_shape=(jax.ShapeDtypeStruct(q.shape, F32),) * 3 + (gshape, gshape),
        in_specs=[cur, prev, cur, prev, cur, gain, gain, cur, col, gain, gain],
        out_specs=(cur, whole, whole, gain, gain),
        compiler_params=_params("arbitrary", "arbitrary"),
    )(q, k, k, v, v, qn, kn, do, dlse, dqn0, dkn0)


def _merge_fn(o1, o2, o3, l1, l2, l3):
    m = lax.stop_gradient(jnp.maximum(jnp.maximum(l1, l2), l3))
    e1, e2, e3 = jnp.exp(l1 - m), jnp.exp(l2 - m), jnp.exp(l3 - m)
    return (e1 * o1 + e2 * o2 + e3 * o3) / (e1 + e2 + e3)


def _merge_specs(h, tm, dh):
    return pl.BlockSpec((h, tm, dh), lambda i: (0, i, 0)), pl.BlockSpec((h, tm, 1), lambda i: (0, i, 0))


MERGE_TILE = 128


def _merge_fwd(os, ls):
    h, t, dh = os[0].shape
    tm = MERGE_TILE
    wide, col = _merge_specs(h, tm, dh)

    def body(o1, o2, o3, l1, l2, l3, out):
        out[...] = _merge_fn(o1[...], o2[...], o3[...], l1[...], l2[...], l3[...])

    return pl.pallas_call(
        body, name="merge_fwd", grid=(t // tm,), out_shape=jax.ShapeDtypeStruct(os[0].shape, F32),
        in_specs=[wide] * 3 + [col] * 3, out_specs=wide, compiler_params=_params("arbitrary"),
    )(*os, *ls)


def _merge_bwd(os, ls, do):
    h, t, dh = os[0].shape
    tm = MERGE_TILE
    wide, col = _merge_specs(h, tm, dh)

    def body(o1, o2, o3, l1, l2, l3, do_ref, d1, d2, d3, e1, e2, e3):
        _, vjp = jax.vjp(_merge_fn, o1[...], o2[...], o3[...], l1[...], l2[...], l3[...])
        outs = vjp(do_ref[...])
        for ref, val in zip((d1, d2, d3, e1, e2, e3), outs):
            ref[...] = val

    oshape = jax.ShapeDtypeStruct(os[0].shape, F32)
    lshape = jax.ShapeDtypeStruct(ls[0].shape, F32)
    return pl.pallas_call(
        body, name="merge_bwd", grid=(t // tm,), out_shape=(oshape,) * 3 + (lshape,) * 3,
        in_specs=[wide] * 3 + [col] * 3 + [wide], out_specs=(wide,) * 3 + (col,) * 3,
        compiler_params=_params("arbitrary"),
    )(*os, *ls, do)


RWKV_VEC = ("mu_r", "mu_k", "mu_v", "mu_w", "mu_a", "mu_g", "w0", "a0", "k_k", "k_a")
RWKV_MAT = ("w1", "w2", "a1", "a2", "g1", "g2")


def _rwkv_pre_fn(cur, prev, vec, w1, w2, a1, a2, g1, g2):
    c = cur.shape[1] // 4
    mu_r, mu_k, mu_v, mu_w, mu_a, mu_g, w0, a0, k_k, k_a = (vec[j:j + 1] for j in range(10))

    def lerp(j, mu):
        xc, xp = cur[:, j * c:(j + 1) * c], prev[:, j * c:(j + 1) * c]
        return xc + (xp - xc) * mu

    r, k, v = lerp(0, mu_r), lerp(1, mu_k), lerp(2, mu_v)
    cw, ca, cg = lerp(3, mu_w), lerp(3, mu_a), lerp(3, mu_g)
    z = w0 + _mm(jnp.tanh(_mm(cw, w1)), w2)
    w_log = jnp.minimum(z, 0.0) - jnp.log(1.0 + jnp.exp(-jnp.abs(z))) - 0.5
    lw = -jnp.exp(w_log)
    a = _sigmoid(a0 + _mm(_mm(ca, a1), a2))
    gate = _mm(_sigmoid(_mm(cg, g1)), g2)
    kkraw = k * k_k
    kmod = k * (1.0 + (a - 1.0) * k_a)
    return r, lw, kmod, v, kkraw, a, gate


def _rwkv_pre_specs(c, mats):
    tm = TOKEN_TILE
    wide = pl.BlockSpec((tm, 4 * c), lambda i: (i, 0))
    one = pl.BlockSpec((tm, c), lambda i: (i, 0))
    vec = pl.BlockSpec((10, c), lambda i: (0, 0))
    mspecs = [pl.BlockSpec(m.shape, lambda i: (0, 0)) for m in mats]
    return wide, one, vec, mspecs


def _rwkv_pre_fwd(cur, prev, vec, mats):
    t, c4 = cur.shape
    c = c4 // 4
    wide, one, vspec, mspecs = _rwkv_pre_specs(c, mats)

    def body(cur_ref, prev_ref, vec_ref, *rest):
        mrefs, outs = rest[:6], rest[6:]
        vals = _rwkv_pre_fn(cur_ref[...], prev_ref[...], vec_ref[...], *(m[...] for m in mrefs))
        for ref, val in zip(outs, vals):
            ref[...] = val

    return pl.pallas_call(
        body, name="rwkv_pre_fwd", grid=(t // TOKEN_TILE,), out_shape=(jax.ShapeDtypeStruct((t, c), F32),) * 7,
        in_specs=[wide, wide, vspec] + mspecs, out_specs=(one,) * 7, compiler_params=_params("arbitrary"),
    )(cur, prev, vec, *mats)


def _rwkv_pre_bwd(cur, prev, vec, mats, cts):
    t, c4 = cur.shape
    c = c4 // 4
    wide, one, vspec, mspecs = _rwkv_pre_specs(c, mats)

    def body(cur_ref, prev_ref, vec_ref, *rest):
        mrefs, ctrefs, outs = rest[:6], rest[6:13], rest[13:]
        _, vjp = jax.vjp(_rwkv_pre_fn, cur_ref[...], prev_ref[...], vec_ref[...], *(m[...] for m in mrefs))
        grads = vjp(tuple(r[...] for r in ctrefs))
        outs[0][...] = grads[0]
        outs[1][...] = grads[1]
        first = pl.program_id(0) == 0

        @pl.when(first)
        def _():
            for ref, val in zip(outs[2:], grads[2:]):
                ref[...] = val

        @pl.when(jnp.logical_not(first))
        def _():
            for ref, val in zip(outs[2:], grads[2:]):
                ref[...] += val

    wshape = jax.ShapeDtypeStruct(cur.shape, F32)
    return pl.pallas_call(
        body, name="rwkv_pre_bwd", grid=(t // TOKEN_TILE,),
        out_shape=(wshape, wshape, jax.ShapeDtypeStruct(vec.shape, F32)) + tuple(jax.ShapeDtypeStruct(m.shape, F32) for m in mats),
        in_specs=[wide, wide, vspec] + mspecs + [one] * 7, out_specs=(wide, wide, vspec) + tuple(mspecs),
        compiler_params=_params("arbitrary"),
    )(cur, prev, vec, *mats, *cts)


def _scan_chunk_fn(h0, r, lw, k, v, kkraw, a, rk, lnw, lnb):
    n = r.shape[1]
    nrm = jnp.sqrt(jnp.sum(kkraw * kkraw, axis=-1, keepdims=True))
    kk = kkraw / jnp.maximum(nrm, 1e-12)
    av, bv = -kk, kk * a
    ti = lax.broadcasted_iota(jnp.int32, (n, n), 0)
    si = lax.broadcasted_iota(jnp.int32, (n, n), 1)
    incl, strict = ti >= si, ti > si
    ones = jnp.broadcast_to(incl.astype(F32)[None], (r.shape[0], n, n))
    cum = _hdot(ones, lw, 2, 1)
    at, rt = av * jnp.exp(cum - lw), r * jnp.exp(cum)
    inv = jnp.exp(-cum)
    bt, kt = bv * inv, k * inv
    lab = jnp.where(strict, _hdot(at, bt, 2, 2), 0.0)
    lak = jnp.where(strict, _hdot(at, kt, 2, 2), 0.0)
    rb = jnp.where(incl, _hdot(rt, bt, 2, 2), 0.0)
    rkm = jnp.where(incl, _hdot(rt, kt, 2, 2), 0.0)
    u = _hdot(at, h0, 2, 1) + _hdot(lak, v, 2, 1)
    p = lab
    m = 1
    while m < n:
        u = u + _hdot(p, u, 2, 1)
        m *= 2
        if m < n:
            p = _hdot(p, p, 2, 1)
    y = _hdot(rt, h0, 2, 1) + _hdot(rb, u, 2, 1) + _hdot(rkm, v, 2, 1)
    last = jnp.exp(jnp.sum(lw, axis=1, keepdims=True))
    h1 = jnp.swapaxes(last, 1, 2) * (h0 + _hdot(bt, u, 1, 1) + _hdot(kt, v, 1, 1))
    mean = jnp.mean(y, axis=-1, keepdims=True)
    yc = y - mean
    var = jnp.mean(yc * yc, axis=-1, keepdims=True)
    yn = yc * lax.rsqrt(var + GN_EPS) * lnw + lnb
    bonus = jnp.sum(r * k * rk, axis=-1, keepdims=True) * v
    return yn + bonus, h1


def _scan_specs(h, t, dh, rev):
    n = SCAN_CHUNK
    nc = t // n
    pos = (lambda c: (0, nc - 1 - c, 0)) if rev else (lambda c: (0, c, 0))
    st = (lambda c: (nc - 1 - c, 0, 0, 0)) if rev else (lambda c: (c, 0, 0, 0))
    seq = pl.BlockSpec((h, n, dh), pos)
    par = pl.BlockSpec((h, 1, dh), lambda c: (0, 0, 0))
    state = pl.BlockSpec((1, h, dh, dh), st)
    return seq, par, state


def _scan_fwd(seqs, pars):
    h, t, dh = seqs[0].shape
    nc = t // SCAN_CHUNK
    seq, par, state = _scan_specs(h, t, dh, False)

    def body(r, lw, k, v, kkraw, a, rk, lnw, lnb, o_ref, st_ref, h_ref):
        @pl.when(pl.program_id(0) == 0)
        def _():
            h_ref[...] = jnp.zeros_like(h_ref)

        h0 = h_ref[...]
        st_ref[0] = h0
        o, h1 = _scan_chunk_fn(h0, r[...], lw[...], k[...], v[...], kkraw[...], a[...], rk[...], lnw[...], lnb[...])
        o_ref[...] = o
        h_ref[...] = h1

    return pl.pallas_call(
        body, name="rwkv_scan_fwd", grid=(nc,),
        out_shape=(jax.ShapeDtypeStruct((h, t, dh), F32), jax.ShapeDtypeStruct((nc, h, dh, dh), F32)),
        in_specs=[seq] * 6 + [par] * 3, out_specs=(seq, state),
        scratch_shapes=[pltpu.VMEM((h, dh, dh), F32)], compiler_params=_params("arbitrary"),
    )(*seqs, *pars)


def _scan_bwd(seqs, pars, states, do):
    h, t, dh = seqs[0].shape
    nc = t // SCAN_CHUNK
    seq, par, state = _scan_specs(h, t, dh, True)

    def body(r, lw, k, v, kkraw, a, rk, lnw, lnb, st_ref, do_ref, *rest):
        douts, dpars, dh_ref = rest[:6], rest[6:9], rest[9]
        first = pl.program_id(0) == 0

        @pl.when(first)
        def _():
            dh_ref[...] = jnp.zeros_like(dh_ref)

        _, vjp = jax.vjp(_scan_chunk_fn, st_ref[0], r[...], lw[...], k[...], v[...], kkraw[...], a[...],
                         rk[...], lnw[...], lnb[...])
        grads = vjp((do_ref[...], dh_ref[...]))
        dh_ref[...] = grads[0]
        for ref, val in zip(douts, grads[1:7]):
            ref[...] = val

        @pl.when(first)
        def _():
            for ref, val in zip(dpars, grads[7:]):
                ref[...] = val

        @pl.when(jnp.logical_not(first))
        def _():
            for ref, val in zip(dpars, grads[7:]):
                ref[...] += val

    sshape = jax.ShapeDtypeStruct((h, t, dh), F32)
    pshape = jax.ShapeDtypeStruct((h, 1, dh), F32)
    return pl.pallas_call(
        body, name="rwkv_scan_bwd", grid=(nc,), out_shape=(sshape,) * 6 + (pshape,) * 3,
        in_specs=[seq] * 6 + [par] * 3 + [state, seq], out_specs=(seq,) * 6 + (par,) * 3,
        scratch_shapes=[pltpu.VMEM((h, dh, dh), F32)], compiler_params=_params("arbitrary"),
    )(*seqs, *pars, states, do)


def _heads(x):
    return x.reshape(x.shape[0], -1, HEAD_DIM).transpose(1, 0, 2)


def _unheads(x):
    return x.transpose(1, 0, 2).reshape(x.shape[1], -1)


def _to_sub(x, dil):
    h, t, d = x.shape
    return x.reshape(h, t // dil, dil, d).transpose(0, 2, 1, 3).reshape(h * dil, t // dil, d)


def _from_sub(x, dil):
    g, length, d = x.shape
    return x.reshape(g // dil, dil, length, d).transpose(0, 2, 1, 3).reshape(g // dil, length * dil, d)


def _local_step(x, target, w):
    c = w["mu_r"].shape[-1]
    att_w = w["w_in"].shape[0] * w["w_in"].shape[2] - 4 * c
    qn, kn = w["q_norm"].reshape(1, 1, HEAD_DIM), w["k_norm"].reshape(1, 1, HEAD_DIM)
    vec = jnp.concatenate([w[n].reshape(1, c) for n in RWKV_VEC], axis=0)
    mats = [w[n] for n in RWKV_MAT]
    pars = [w[n].reshape(-1, 1, HEAD_DIM) for n in ("r_k", "ln_x_w", "ln_x_b")]

    x1 = _ffn_fwd(x, w["ffn1_norm"], w["ffn1_w_gate"], w["ffn1_w_up"], w["ffn1_w_down"], "ffn1_fwd")
    proj = _proj_fwd(x1, w["mix_norm"], w["w_in"])
    hw = att_w // 3
    qkv = [_heads(proj[:, j * hw:(j + 1) * hw]) for j in range(3)]
    subs = [[_to_sub(a, dil) for a in qkv] for dil in DILATIONS]
    outs = [_att_fwd(*s, qn, kn, f"att_fwd_d{dil}") for s, dil in zip(subs, DILATIONS)]
    os_ = [_from_sub(o, dil) for (o, _), dil in zip(outs, DILATIONS)]
    ls_ = [_from_sub(l, dil) for (_, l), dil in zip(outs, DILATIONS)]
    att = _unheads(_merge_fwd(os_, ls_))
    cur = proj[:, att_w:]
    prev = jnp.concatenate([jnp.zeros_like(cur[:1]), cur[:-1]], axis=0)
    pre = _rwkv_pre_fwd(cur, prev, vec, mats)
    seqs = [_heads(a) for a in pre[:6]]
    gate = pre[6]
    opg_h, states = _scan_fwd(seqs, pars)
    opg = _unheads(opg_h)
    x2 = _mixout_fwd(x1, att, opg, gate, w["w_out"])
    x3 = _ffn_fwd(x2, w["ffn2_norm"], w["ffn2_w_gate"], w["ffn2_w_up"], w["ffn2_w_down"], "ffn2_fwd")
    dy, loss = _loss_head(x3, target)

    g = {}
    dx2, g["ffn2_norm"], g["ffn2_w_gate"], g["ffn2_w_up"], g["ffn2_w_down"] = _ffn_bwd(
        x2, w["ffn2_norm"], w["ffn2_w_gate"], w["ffn2_w_up"], w["ffn2_w_down"], dy, "ffn2_bwd")
    datt, dopg, dgate, g["w_out"] = _mixout_bwd(att, opg, gate, w["w_out"], dx2)
    dscan = _scan_bwd(seqs, pars, states, _heads(dopg))
    for n, d in zip(("r_k", "ln_x_w", "ln_x_b"), dscan[6:]):
        g[n] = d
    dpre = _rwkv_pre_bwd(cur, prev, vec, mats, [_unheads(d) for d in dscan[:6]] + [dgate])
    dcur, dprev, dvec = dpre[:3]
    for n, d in zip(RWKV_MAT, dpre[3:]):
        g[n] = d
    for j, n in enumerate(RWKV_VEC):
        g[n] = dvec[j:j + 1]
    dmerge = _merge_bwd(os_, ls_, _heads(datt))
    dqn = dkn = jnp.zeros((1, 1, HEAD_DIM), F32)
    dqkv = []
    for j, dil in enumerate(DILATIONS):
        dq, dk, dv, dqn, dkn = _att_bwd(*subs[j], qn, kn, _to_sub(dmerge[j], dil), _to_sub(dmerge[3 + j], dil),
                                        dqn, dkn, f"att_bwd_d{dil}")
        dqkv.append([_unheads(_from_sub(a, dil)) for a in (dq, dk, dv)])
    g["q_norm"], g["k_norm"] = dqn, dkn
    dshift = jnp.concatenate([dprev[1:], jnp.zeros_like(dprev[:1])], axis=0)
    dps = [jnp.concatenate(dqkv[j] + [tail], axis=1) for j, tail in enumerate((dcur, dshift, jnp.zeros_like(dcur)))]
    dx1, g["mix_norm"], g["w_in"] = _proj_bwd(x1, w["mix_norm"], w["w_in"], *dps, dx2)
    dx, g["ffn1_norm"], g["ffn1_w_gate"], g["ffn1_w_up"], g["ffn1_w_down"] = _ffn_bwd(
        x, w["ffn1_norm"], w["ffn1_w_gate"], w["ffn1_w_up"], w["ffn1_w_down"], dx1, "ffn1_bwd")
    return loss, dx, g


N_SHARDS = 4


def _place():
    return lax.axis_index("x"), lax.axis_index("y"), lax.axis_index("c")


def _chip_peers(x, y):
    return [(1 - x, y), (x, 1 - y), (1 - x, 1 - y)]


def _gather_xy(shards):
    n = len(shards)

    def body(*refs):
        ins, outs = refs[:n], refs[n:2 * n]
        send_sems, recv_sems, local_sems = refs[2 * n:]
        x, y, c = _place()
        me = 2 * x + y
        copies = []
        for i in range(n):
            own = pltpu.make_async_copy(ins[i], outs[i].at[me], local_sems.at[i])
            own.start()
            copies.append(own)
            for k, (px, py) in enumerate(_chip_peers(x, y)):
                cp = pltpu.make_async_remote_copy(
                    src_ref=ins[i], dst_ref=outs[i].at[me], send_sem=send_sems.at[i, k], recv_sem=recv_sems.at[i, k],
                    device_id=(px, py, c), device_id_type=MESH)
                cp.start()
                copies.append(cp)
        for cp in copies:
            cp.wait()

    return pl.pallas_call(
        body, name="gather_weights",
        out_shape=tuple(jax.ShapeDtypeStruct((N_SHARDS,) + s.shape, s.dtype) for s in shards),
        in_specs=[ANY] * n, out_specs=(ANY,) * n,
        scratch_shapes=[pltpu.SemaphoreType.DMA((n, 3)), pltpu.SemaphoreType.DMA((n, 3)), pltpu.SemaphoreType.DMA((n,))],
    )(*shards)


def _scatter_partials(parts):
    n = len(parts)

    def body(*refs):
        ins, outs = refs[:n], refs[n:2 * n]
        send_sems, recv_sems = refs[2 * n:]
        x, y, c = _place()
        copies = []
        for i in range(n):
            for k, (px, py) in enumerate(_chip_peers(x, y)):
                cp = pltpu.make_async_remote_copy(
                    src_ref=ins[i].at[2 * px + py], dst_ref=outs[i].at[k], send_sem=send_sems.at[i, k],
                    recv_sem=recv_sems.at[i, k], device_id=(px, py, c), device_id_type=MESH)
                cp.start()
                copies.append(cp)
        for cp in copies:
            cp.wait()

    return pl.pallas_call(
        body, name="scatter_partials",
        out_shape=tuple(jax.ShapeDtypeStruct((3,) + p.shape[1:], p.dtype) for p in parts),
        in_specs=[ANY] * n, out_specs=(ANY,) * n,
        scratch_shapes=[pltpu.SemaphoreType.DMA((n, 3)), pltpu.SemaphoreType.DMA((n, 3))],
    )(*parts)


def _sibling_swap(arrays):
    n = len(arrays)

    def body(*refs):
        ins, outs = refs[:n], refs[n:2 * n]
        send_sems, recv_sems = refs[2 * n:]
        x, y, c = _place()
        copies = []
        for i in range(n):
            cp = pltpu.make_async_remote_copy(
                src_ref=ins[i], dst_ref=outs[i], send_sem=send_sems.at[i], recv_sem=recv_sems.at[i],
                device_id=(x, y, 1 - c), device_id_type=MESH)
            cp.start()
            copies.append(cp)
        for cp in copies:
            cp.wait()

    return pl.pallas_call(
        body, name="sibling_swap",
        out_shape=tuple(jax.ShapeDtypeStruct(a.shape, a.dtype) for a in arrays),
        in_specs=[ANY] * n, out_specs=(ANY,) * n,
        scratch_shapes=[pltpu.SemaphoreType.DMA((n,)), pltpu.SemaphoreType.DMA((n,))],
    )(*arrays)


N_DEV = 8


def _allreduce_small(pack):
    def body(in_ref, out_ref, buf, send_sems, recv_sems):
        x, y, c = _place()
        me = 4 * x + 2 * y + c
        buf[me] = in_ref[...]

        def copy(j, slot):
            px, py, pc = x ^ (j >> 2), y ^ ((j >> 1) & 1), c ^ (j & 1)
            return pltpu.make_async_remote_copy(
                src_ref=in_ref, dst_ref=buf.at[slot(px, py, pc)], send_sem=send_sems.at[j], recv_sem=recv_sems.at[j],
                device_id=(px, py, pc), device_id_type=MESH)

        for j in range(1, N_DEV):
            copy(j, lambda px, py, pc: me).start()
        for j in range(1, N_DEV):
            landing = copy(j, lambda px, py, pc: 4 * px + 2 * py + pc)
            landing.wait_send()
            landing.wait_recv()
        acc = buf[0]
        for s in range(1, N_DEV):
            acc = acc + buf[s]
        out_ref[...] = acc

    return pl.pallas_call(
        body, name="allreduce_small", out_shape=jax.ShapeDtypeStruct(pack.shape, F32),
        in_specs=[VMEM_FULL], out_specs=VMEM_FULL,
        scratch_shapes=[pltpu.VMEM((N_DEV,) + pack.shape, F32), pltpu.SemaphoreType.DMA((N_DEV,)),
                        pltpu.SemaphoreType.DMA((N_DEV,))],
    )(pack)


ROW_TILE_MAX = 256
BF16_SUBLANES = 16


def _row_tile(rows):
    for tr in range(min(rows, ROW_TILE_MAX), 0, -1):
        if rows % tr == 0 and tr % BF16_SUBLANES == 0:
            return tr
    return rows


def _reduce_own(me, part, recv, name):
    _, r, cols = part.shape
    tr = _row_tile(r)

    def body(me_ref, p_ref, rv_ref, o_ref):
        acc = p_ref[0]
        for k in range(3):
            acc = acc + rv_ref[k].astype(F32)
        o_ref[...] = acc

    return pl.pallas_call(
        body, name=name, out_shape=jax.ShapeDtypeStruct((r, cols), F32),
        grid_spec=pltpu.PrefetchScalarGridSpec(
            num_scalar_prefetch=1, grid=(r // tr,),
            in_specs=[pl.BlockSpec((1, tr, cols), lambda i, me_ref: (me_ref[0], i, 0)),
                      pl.BlockSpec((3, tr, cols), lambda i, me_ref: (0, i, 0))],
            out_specs=pl.BlockSpec((tr, cols), lambda i, me_ref: (i, 0))),
        compiler_params=_params("arbitrary"),
    )(me, part, recv)


def _adamw(w, ga, gb, m, v, name):
    r, cols = w.shape
    tr = _row_tile(r)
    c1 = 1.0 - ADAM_B1 ** ADAM_STEP
    c2 = 1.0 - ADAM_B2 ** ADAM_STEP

    def body(w_ref, ga_ref, gb_ref, m_ref, v_ref, g_out, d_out, m_out, v_out):
        g = ga_ref[...] + gb_ref[...]
        mn = ADAM_B1 * m_ref[...] + (1.0 - ADAM_B1) * g
        vn = ADAM_B2 * v_ref[...] + (1.0 - ADAM_B2) * (g * g)
        g_out[...] = g
        m_out[...] = mn
        v_out[...] = vn
        d_out[...] = -ADAM_LR * ((mn / c1) / (jnp.sqrt(vn / c2) + ADAM_EPS) + ADAM_WD * w_ref[...])

    tile = pl.BlockSpec((tr, cols), lambda i: (i, 0))
    shape = jax.ShapeDtypeStruct((r, cols), F32)
    return pl.pallas_call(
        body, name=name, grid=(r // tr,), out_shape=(shape,) * 4, in_specs=[tile] * 5, out_specs=(tile,) * 4,
        compiler_params=_params("arbitrary"),
    )(w, ga, gb, m, v)


PACK_COLS = 512


def _to_rows(a):
    flat = a.reshape(-1)
    pad = (-flat.shape[0]) % PACK_COLS
    return jnp.pad(flat, (0, pad)).reshape(-1, PACK_COLS)


def _pack(arrays, extra_rows=0):
    rows = [_to_rows(a) for a in arrays]
    n = sum(r.shape[0] for r in rows) + extra_rows
    pad = (-n) % 8
    return jnp.concatenate(rows + [jnp.zeros((extra_rows + pad, PACK_COLS), F32)], axis=0)


def _unpack(pack, like):
    out, at = [], 0
    for a in like:
        n = -(-a.size // PACK_COLS)
        out.append(pack[at:at + n].reshape(-1)[:a.size].reshape(a.shape))
        at += n
    return out


COL_SHARDED = ("ffn1_w_gate", "ffn1_w_up", "w_in", "ffn2_w_gate", "ffn2_w_up", "w2", "a2", "g2")
ROW_SHARDED = ("ffn1_w_down", "ffn2_w_down", "w_out", "w1", "a1", "g1")
CHUNKED = ("ffn1_w_gate", "ffn1_w_up", "ffn1_w_down", "w_in", "ffn2_w_gate", "ffn2_w_up", "ffn2_w_down")
WEIGHTS = ("ffn1_norm", "ffn1_w_gate", "ffn1_w_up", "ffn1_w_down", "mix_norm", "w_in", "q_norm", "k_norm",
           "mu_r", "mu_k", "mu_v", "mu_w", "mu_a", "mu_g", "w0", "w1", "w2", "a0", "a1", "a2", "g1", "g2",
           "k_k", "k_a", "r_k", "ln_x_w", "ln_x_b", "w_out", "ffn2_norm", "ffn2_w_gate", "ffn2_w_up", "ffn2_w_down")


def _full_from_blocks(name, blocks):
    if name in CHUNKED:
        return blocks
    if name in ROW_SHARDED:
        return blocks.reshape(-1, blocks.shape[-1])
    return blocks.transpose(1, 0, 2).reshape(blocks.shape[1], -1)


def _blocks_from_full(name, full):
    if name in CHUNKED:
        return full
    if name in ROW_SHARDED:
        return full.reshape(N_SHARDS, -1, full.shape[-1])
    return full.reshape(full.shape[0], N_SHARDS, -1).transpose(1, 0, 2)


def kernel(
        x, ffn1_norm, ffn1_w_gate, ffn1_w_up, ffn1_w_down, mix_norm, w_in, q_norm, k_norm, mu_r, mu_k, mu_v, mu_w,
        mu_a, mu_g, w0, w1, w2, a0, a1, a2, g1, g2, k_k, k_a, r_k, ln_x_w, ln_x_b, w_out, ffn2_norm, ffn2_w_gate,
        ffn2_w_up, ffn2_w_down, loss_target, m_ffn1_norm, m_ffn1_w_gate, m_ffn1_w_up, m_ffn1_w_down, m_mix_norm,
        m_w_in, m_q_norm, m_k_norm, m_mu_r, m_mu_k, m_mu_v, m_mu_w, m_mu_a, m_mu_g, m_w0, m_w1, m_w2, m_a0, m_a1,
        m_a2, m_g1, m_g2, m_k_k, m_k_a, m_r_k, m_ln_x_w, m_ln_x_b, m_w_out, m_ffn2_norm, m_ffn2_w_gate, m_ffn2_w_up,
        m_ffn2_w_down, v_ffn1_norm, v_ffn1_w_gate, v_ffn1_w_up, v_ffn1_w_down, v_mix_norm, v_w_in, v_q_norm, v_k_norm,
        v_mu_r, v_mu_k, v_mu_v, v_mu_w, v_mu_a, v_mu_g, v_w0, v_w1, v_w2, v_a0, v_a1, v_a2, v_g1, v_g2, v_k_k, v_k_a,
        v_r_k, v_ln_x_w, v_ln_x_b, v_w_out, v_ffn2_norm, v_ffn2_w_gate, v_ffn2_w_up, v_ffn2_w_down):
    given = dict(locals())
    sharded = COL_SHARDED + ROW_SHARDED
    sharded = tuple(n for n in WEIGHTS if n in sharded)
    small = tuple(n for n in WEIGHTS if n not in sharded)

    blocks = _gather_xy([given[n][0].astype(BF16) for n in sharded])
    w = {n: given[n] for n in small}
    for n, b in zip(sharded, blocks):
        full = _full_from_blocks(n, b)
        w[n] = full.astype(F32) if n in RWKV_MAT else full

    loss, dx, g = _local_step(x[0], loss_target[0], w)

    me = (2 * lax.axis_index("x") + lax.axis_index("y")).astype(jnp.int32).reshape(1)
    parts = [_blocks_from_full(n, g[n]) for n in sharded]
    recv = _scatter_partials([p.astype(BF16) for p in parts])
    mine = []
    for n, p, rv in zip(sharded, parts, recv):
        p2 = p.reshape(N_SHARDS, -1, p.shape[-1])
        mine.append(_reduce_own(me, p2, rv.reshape(3, -1, rv.shape[-1]), f"reduce_{n}"))
    theirs = _sibling_swap(mine)
    out = {}
    for n, a, b in zip(sharded, mine, theirs):
        shape = given[n].shape
        two_d = (-1, shape[-1])
        res = _adamw(given[n].reshape(two_d), a, b, given["m_" + n].reshape(two_d), given["v_" + n].reshape(two_d), f"adamw_{n}")
        out[n] = [r.reshape(shape) for r in res]

    gpack = _pack([g[n] for n in small], extra_rows=1)
    n_rows = sum(-(-given[n].size // PACK_COLS) for n in small)
    gpack = gpack.at[n_rows, :loss.shape[1]].set(loss[0])
    gsum = _allreduce_small(gpack)
    res = _adamw(_pack([given[n] for n in small], 1), gsum, jnp.zeros_like(gsum), _pack([given["m_" + n] for n in small], 1),
                 _pack([given["v_" + n] for n in small], 1), "adamw_small")
    like = [given[n] for n in small]
    for j, r in enumerate(res):
        for n, a in zip(small, _unpack(r, like)):
            out.setdefault(n, [None] * 4)[j] = a
    total_loss = gsum[n_rows, 0]
    return (total_loss, dx[None], *[out[n][0] for n in WEIGHTS], *[out[n][1] for n in WEIGHTS],
            *[out[n][2] for n in WEIGHTS], *[out[n][3] for n in WEIGHTS])
```

```python
import functools

import jax
import jax.numpy as jnp
from jax import lax
from jax.experimental import pallas as pl
from jax.experimental.pallas import tpu as pltpu

F32 = jnp.float32
BF16 = jnp.bfloat16
HIGHEST = lax.Precision.HIGHEST
MESH = pl.DeviceIdType.MESH

RMS_EPS = 1e-6
GN_EPS = 64e-5
NEG_INF = -1e30
FFN_RESIDUAL = 0.5
HEAD_DIM = 64
ATT_BLOCK = 128
DILATIONS = (1, 4, 16)
SCAN_CHUNK = 64
TOKEN_TILE = 256

ADAM_LR = 0.001
ADAM_B1 = 0.9
ADAM_B2 = 0.999
ADAM_EPS = 1e-08
ADAM_WD = 0.01
ADAM_STEP = 10

VMEM_FULL = pl.BlockSpec(memory_space=pltpu.VMEM)
ANY = pl.BlockSpec(memory_space=pl.ANY)


VMEM_LIMIT = 56 * 1024 * 1024


def _params(*sem):
    return pltpu.CompilerParams(dimension_semantics=sem, vmem_limit_bytes=VMEM_LIMIT)


def _dot(a, b, dims):
    return lax.dot_general(a.astype(BF16), b.astype(BF16), (dims, ((), ())), preferred_element_type=F32)


def _dot_nn(a, b):
    return _dot(a, b, ((1,), (0,)))


def _dot_nt(a, b):
    return _dot(a, b, ((1,), (1,)))


def _dot_tn(a, b):
    return _dot(a, b, ((0,), (0,)))


@jax.custom_vjp
def _mm(a, b):
    return _dot_nn(a, b)


def _mm_fwd(a, b):
    return _dot_nn(a, b), (a, b)


def _mm_bwd(res, g):
    a, b = res
    return _dot_nt(g, b).astype(a.dtype), _dot_tn(a, g).astype(b.dtype)


_mm.defvjp(_mm_fwd, _mm_bwd)


def _bdot(a, b, ca, cb):
    return lax.dot_general(a.astype(BF16), b.astype(BF16), (((ca,), (cb,)), ((0,), (0,))), preferred_element_type=F32)


@jax.custom_vjp
def _bmm_nt(a, b):
    return _bdot(a, b, 2, 2)


def _bmm_nt_fwd(a, b):
    return _bdot(a, b, 2, 2), (a, b)


def _bmm_nt_bwd(res, g):
    a, b = res
    return _bdot(g, b, 2, 1), _bdot(g, a, 1, 1)


_bmm_nt.defvjp(_bmm_nt_fwd, _bmm_nt_bwd)


@jax.custom_vjp
def _bmm_nn(a, b):
    return _bdot(a, b, 2, 1)


def _bmm_nn_fwd(a, b):
    return _bdot(a, b, 2, 1), (a, b)


def _bmm_nn_bwd(res, g):
    a, b = res
    return _bdot(g, b, 2, 2), _bdot(a, g, 1, 1)


_bmm_nn.defvjp(_bmm_nn_fwd, _bmm_nn_bwd)


def _hdot(a, b, ca, cb):
    return lax.dot_general(a, b, (((ca,), (cb,)), ((0,), (0,))), precision=lax.Precision.HIGH, preferred_element_type=F32)


def _sigmoid(x):
    return 1.0 / (1.0 + jnp.exp(-x))


def _rms(x):
    return lax.rsqrt(jnp.mean(x * x, axis=-1, keepdims=True) + RMS_EPS)


def _ffn_fwd(x, norm, wg, wu, wd, name):
    t, d = x.shape
    nc = wg.shape[0]
    tm = TOKEN_TILE

    def body(x_ref, n_ref, wg_ref, wu_ref, wd_ref, o_ref):
        xv = x_ref[...]
        h = (xv * _rms(xv) * n_ref[...]).astype(BF16)
        acc = jnp.zeros((tm, d), F32)
        for c in range(nc):
            g = jnp.dot(h, wg_ref[c], preferred_element_type=F32)
            u = jnp.dot(h, wu_ref[c], preferred_element_type=F32)
            a = (g * _sigmoid(g) * u).astype(BF16)
            acc = acc + jnp.dot(a, wd_ref[c], preferred_element_type=F32)
        o_ref[...] = xv + FFN_RESIDUAL * acc

    tile = pl.BlockSpec((tm, d), lambda i: (i, 0))
    return pl.pallas_call(
        body, name=name, grid=(t // tm,), out_shape=jax.ShapeDtypeStruct((t, d), F32),
        in_specs=[tile, pl.BlockSpec((1, d), lambda i: (0, 0)), VMEM_FULL, VMEM_FULL, VMEM_FULL],
        out_specs=tile, compiler_params=_params("arbitrary"),
    )(x, norm, wg, wu, wd)


def _rmsnorm_bwd(xv, gain, dh):
    rs = _rms(xv)
    xn = xv * rs
    dxn = dh * gain
    dx = rs * (dxn - xn * jnp.mean(dxn * xn, axis=-1, keepdims=True))
    return dx, jnp.sum(dh * xn, axis=0, keepdims=True)


def _ffn_bwd(x, norm, wg, wu, wd, dy, name):
    t, d = x.shape
    nc, _, fc = wg.shape
    tm = TOKEN_TILE
    nt = t // tm

    def body(x_ref, n_ref, wg_ref, wu_ref, wd_ref, dy_ref, dx_ref, dn_ref, dwg_ref, dwu_ref, dwd_ref, dh_ref):
        c, i = pl.program_id(0), pl.program_id(1)
        rows = pl.ds(pl.multiple_of(i * tm, tm), tm)
        xv = x_ref[...]
        gain = n_ref[...]
        h = (xv * _rms(xv) * gain).astype(BF16)
        dy = dy_ref[...]
        dyb = (FFN_RESIDUAL * dy).astype(BF16)
        g = jnp.dot(h, wg_ref[0], preferred_element_type=F32)
        u = jnp.dot(h, wu_ref[0], preferred_element_type=F32)
        sg = _sigmoid(g)
        s = g * sg
        a = (s * u).astype(BF16)
        da = _dot_nt(dyb, wd_ref[0])
        dub = (da * s).astype(BF16)
        dgb = (da * u * (sg * (1.0 + g * (1.0 - sg)))).astype(BF16)
        dwd_c = _dot_tn(a, dyb)
        dwg_c = _dot_tn(h, dgb)
        dwu_c = _dot_tn(h, dub)
        dh_c = _dot_nt(dgb, wg_ref[0]) + _dot_nt(dub, wu_ref[0])

        @pl.when(i == 0)
        def _():
            dwd_ref[0] = dwd_c
            dwg_ref[0] = dwg_c
            dwu_ref[0] = dwu_c

        @pl.when(i > 0)
        def _():
            dwd_ref[0] += dwd_c
            dwg_ref[0] += dwg_c
            dwu_ref[0] += dwu_c

        @pl.when(c == 0)
        def _():
            dh_ref[rows, :] = dh_c

        @pl.when(c > 0)
        def _():
            dh_ref[rows, :] += dh_c

        @pl.when(c == nc - 1)
        def _():
            dx, dn = _rmsnorm_bwd(xv, gain, dh_ref[rows, :])
            dx_ref[...] = dx + dy

            @pl.when(i == 0)
            def _():
                dn_ref[...] = dn

            @pl.when(i > 0)
            def _():
                dn_ref[...] += dn

    tile = pl.BlockSpec((tm, d), lambda c, i: (i, 0))
    row = pl.BlockSpec((1, d), lambda c, i: (0, 0))
    wcol = pl.BlockSpec((1, d, fc), lambda c, i: (c, 0, 0))
    wrow = pl.BlockSpec((1, fc, d), lambda c, i: (c, 0, 0))
    last = pl.BlockSpec((tm, d), lambda c, i: (jnp.where(c == nc - 1, i, 0), 0))
    return pl.pallas_call(
        body, name=name, grid=(nc, nt),
        out_shape=(jax.ShapeDtypeStruct((t, d), F32), jax.ShapeDtypeStruct((1, d), F32),
                   jax.ShapeDtypeStruct(wg.shape, F32), jax.ShapeDtypeStruct(wu.shape, F32),
                   jax.ShapeDtypeStruct(wd.shape, F32)),
        in_specs=[tile, row, wcol, wcol, wrow, tile],
        out_specs=(last, row, wcol, wcol, wrow),
        scratch_shapes=[pltpu.VMEM((t, d), F32)],
        compiler_params=_params("arbitrary", "arbitrary"),
    )(x, norm, wg, wu, wd, dy)


def _proj_fwd(x, norm, w):
    t, d = x.shape
    nc, _, ncol = w.shape
    tm = TOKEN_TILE

    def body(x_ref, n_ref, w_ref, o_ref):
        xv = x_ref[...]
        h = (xv * _rms(xv) * n_ref[...]).astype(BF16)
        for c in range(nc):
            o_ref[:, c * ncol:(c + 1) * ncol] = jnp.dot(h, w_ref[c], preferred_element_type=F32)

    return pl.pallas_call(
        body, name="proj_fwd", grid=(t // tm,), out_shape=jax.ShapeDtypeStruct((t, nc * ncol), F32),
        in_specs=[pl.BlockSpec((tm, d), lambda i: (i, 0)), pl.BlockSpec((1, d), lambda i: (0, 0)), VMEM_FULL],
        out_specs=pl.BlockSpec((tm, nc * ncol), lambda i: (i, 0)), compiler_params=_params("arbitrary"),
    )(x, norm, w)


def _proj_bwd(x, norm, w, dpa, dpb, dpc, dres):
    t, d = x.shape
    nc, _, ncol = w.shape
    tm = TOKEN_TILE
    nt = t // tm

    def body(x_ref, n_ref, w_ref, dpa_ref, dpb_ref, dpc_ref, dres_ref, dx_ref, dn_ref, dw_ref, dh_ref):
        c, i = pl.program_id(0), pl.program_id(1)
        rows = pl.ds(pl.multiple_of(i * tm, tm), tm)
        xv = x_ref[...]
        gain = n_ref[...]
        h = (xv * _rms(xv) * gain).astype(BF16)
        dpv = (dpa_ref[...] + dpb_ref[...] + dpc_ref[...]).astype(BF16)
        dw_c = _dot_tn(h, dpv)
        dh_c = _dot_nt(dpv, w_ref[0])

        @pl.when(i == 0)
        def _():
            dw_ref[0] = dw_c

        @pl.when(i > 0)
        def _():
            dw_ref[0] += dw_c

        @pl.when(c == 0)
        def _():
            dh_ref[rows, :] = dh_c

        @pl.when(c > 0)
        def _():
            dh_ref[rows, :] += dh_c

        @pl.when(c == nc - 1)
        def _():
            dx, dn = _rmsnorm_bwd(xv, gain, dh_ref[rows, :])
            dx_ref[...] = dx + dres_ref[...]

            @pl.when(i == 0)
            def _():
                dn_ref[...] = dn

            @pl.when(i > 0)
            def _():
                dn_ref[...] += dn

    tile = pl.BlockSpec((tm, d), lambda c, i: (i, 0))
    row = pl.BlockSpec((1, d), lambda c, i: (0, 0))
    wcol = pl.BlockSpec((1, d, ncol), lambda c, i: (c, 0, 0))
    ptile = pl.BlockSpec((tm, ncol), lambda c, i: (i, c))
    last = pl.BlockSpec((tm, d), lambda c, i: (jnp.where(c == nc - 1, i, 0), 0))
    return pl.pallas_call(
        body, name="proj_bwd", grid=(nc, nt),
        out_shape=(jax.ShapeDtypeStruct((t, d), F32), jax.ShapeDtypeStruct((1, d), F32),
                   jax.ShapeDtypeStruct(w.shape, F32)),
        in_specs=[tile, row, wcol, ptile, ptile, ptile, tile],
        out_specs=(last, row, wcol),
        scratch_shapes=[pltpu.VMEM((t, d), F32)],
        compiler_params=_params("arbitrary", "arbitrary"),
    )(x, norm, w, dpa, dpb, dpc, dres)


def _mixout_fwd(x, att, opg, gate, w):
    t, d = x.shape
    half = att.shape[1]
    tm = TOKEN_TILE

    def body(x_ref, att_ref, opg_ref, g_ref, w_ref, o_ref):
        mix = jnp.concatenate([att_ref[...], opg_ref[...] * g_ref[...]], axis=-1).astype(BF16)
        o_ref[...] = x_ref[...] + jnp.dot(mix, w_ref[...], preferred_element_type=F32)

    tile = pl.BlockSpec((tm, d), lambda i: (i, 0))
    htile = pl.BlockSpec((tm, half), lambda i: (i, 0))
    return pl.pallas_call(
        body, name="mixout_fwd", grid=(t // tm,), out_shape=jax.ShapeDtypeStruct((t, d), F32),
        in_specs=[tile, htile, htile, htile, VMEM_FULL], out_specs=tile, compiler_params=_params("arbitrary"),
    )(x, att, opg, gate, w)


def _mixout_bwd(att, opg, gate, w, dy):
    t, half = att.shape
    d = dy.shape[1]
    tm = TOKEN_TILE

    def body(att_ref, opg_ref, g_ref, w_ref, dy_ref, datt_ref, dopg_ref, dg_ref, dw_ref):
        i = pl.program_id(0)
        opg_v, g_v = opg_ref[...], g_ref[...]
        mix = jnp.concatenate([att_ref[...], opg_v * g_v], axis=-1).astype(BF16)
        dyb = dy_ref[...].astype(BF16)
        dmix = _dot_nt(dyb, w_ref[...])
        dw = _dot_tn(mix, dyb)
        datt_ref[...] = dmix[:, :half]
        drw = dmix[:, half:]
        dopg_ref[...] = drw * g_v
        dg_ref[...] = drw * opg_v

        @pl.when(i == 0)
        def _():
            dw_ref[...] = dw

        @pl.when(i > 0)
        def _():
            dw_ref[...] += dw

    tile = pl.BlockSpec((tm, d), lambda i: (i, 0))
    htile = pl.BlockSpec((tm, half), lambda i: (i, 0))
    hshape = jax.ShapeDtypeStruct((t, half), F32)
    return pl.pallas_call(
        body, name="mixout_bwd", grid=(t // tm,),
        out_shape=(hshape, hshape, hshape, jax.ShapeDtypeStruct(w.shape, F32)),
        in_specs=[htile, htile, htile, VMEM_FULL, tile],
        out_specs=(htile, htile, htile, pl.BlockSpec(w.shape, lambda i: (0, 0))),
        compiler_params=_params("arbitrary"),
    )(att, opg, gate, w, dy)


def _loss_head(y, target):
    t, d = y.shape
    tm = TOKEN_TILE

    def body(y_ref, t_ref, dy_ref, loss_ref):
        i = pl.program_id(0)
        err = y_ref[...] - t_ref[...]
        dy_ref[...] = err * (1.0 / d)
        part = 0.5 * jnp.sum(jnp.mean(err * err, axis=-1, keepdims=True), axis=0, keepdims=True)

        @pl.when(i == 0)
        def _():
            loss_ref[...] = jnp.zeros_like(loss_ref)

        loss_ref[...] += jnp.broadcast_to(part, loss_ref.shape)

    tile = pl.BlockSpec((tm, d), lambda i: (i, 0))
    return pl.pallas_call(
        body, name="loss_head", grid=(t // tm,),
        out_shape=(jax.ShapeDtypeStruct((t, d), F32), jax.ShapeDtypeStruct((1, 128), F32)),
        in_specs=[tile, tile], out_specs=(tile, pl.BlockSpec((1, 128), lambda i: (0, 0))),
        compiler_params=_params("arbitrary"),
    )(y, target)


def _att_block(q, kp, kc, vp, vc, qn, kn, has_prev):
    blk = q.shape[1]

    def hn(v, gain):
        return v * _rms(v) * gain

    qh, khp, khc = hn(q, qn), hn(kp, kn), hn(kc, kn)
    scale = HEAD_DIM ** -0.5
    sp = _bmm_nt(qh, khp) * scale
    sc = _bmm_nt(qh, khc) * scale
    qi = lax.broadcasted_iota(jnp.int32, (blk, blk), 0)
    kj = lax.broadcasted_iota(jnp.int32, (blk, blk), 1)
    sp = jnp.where((kj >= qi) & has_prev, sp, NEG_INF)
    sc = jnp.where(kj <= qi, sc, NEG_INF)
    m = lax.stop_gradient(jnp.maximum(jnp.max(sp, axis=-1, keepdims=True), jnp.max(sc, axis=-1, keepdims=True)))
    pp, pc = jnp.exp(sp - m), jnp.exp(sc - m)
    den = jnp.sum(pp, axis=-1, keepdims=True) + jnp.sum(pc, axis=-1, keepdims=True)
    o = (_bmm_nn(pp, vp) + _bmm_nn(pc, vc)) / den
    return o, m + jnp.log(den)


def _att_specs(g, length, gt):
    blk = ATT_BLOCK
    cur = pl.BlockSpec((gt, blk, HEAD_DIM), lambda gi, n: (gi, n, 0))
    prev = pl.BlockSpec((gt, blk, HEAD_DIM), lambda gi, n: (gi, jnp.maximum(n - 1, 0), 0))
    gain = pl.BlockSpec((1, 1, HEAD_DIM), lambda gi, n: (0, 0, 0))
    col = pl.BlockSpec((gt, blk, 1), lambda gi, n: (gi, n, 0))
    return cur, prev, gain, col


def _att_fwd(q, k, v, qn, kn, name):
    g, length, dh = q.shape
    gt = g // 8
    cur, prev, gain, col = _att_specs(g, length, gt)

    def body(q_ref, kp_ref, kc_ref, vp_ref, vc_ref, qn_ref, kn_ref, o_ref, lse_ref):
        o, lse = _att_block(q_ref[...], kp_ref[...], kc_ref[...], vp_ref[...], vc_ref[...],
                            qn_ref[...], kn_ref[...], pl.program_id(1) > 0)
        o_ref[...] = o
        lse_ref[...] = lse

    return pl.pallas_call(
        body, name=name, grid=(g // gt, length // ATT_BLOCK),
        out_shape=(jax.ShapeDtypeStruct(q.shape, F32), jax.ShapeDtypeStruct((g, length, 1), F32)),
        in_specs=[cur, prev, cur, prev, cur, gain, gain], out_specs=(cur, col),
        compiler_params=_params("arbitrary", "arbitrary"),
    )(q, k, k, v, v, qn, kn)


def _att_bwd(q, k, v, qn, kn, do, dlse, dqn0, dkn0, name):
    g, length, dh = q.shape
    gt = g // 8
    blk = ATT_BLOCK
    cur, prev, gain, col = _att_specs(g, length, gt)
    whole = pl.BlockSpec((gt, length, dh), lambda gi, n: (gi, 0, 0))

    def body(q_ref, kp_ref, kc_ref, vp_ref, vc_ref, qn_ref, kn_ref, do_ref, dl_ref, dqn0_ref, dkn0_ref,
             dq_ref, dk_ref, dv_ref, dqn_ref, dkn_ref):
        gi, n = pl.program_id(0), pl.program_id(1)
        has_prev = n > 0
        fn = functools.partial(_att_block, has_prev=has_prev)
        _, vjp = jax.vjp(fn, q_ref[...], kp_ref[...], kc_ref[...], vp_ref[...], vc_ref[...], qn_ref[...], kn_ref[...])
        dq, dkp, dkc, dvp, dvc, dqn, dkn = vjp((do_ref[...], dl_ref[...]))
        dq_ref[...] = dq
        here = pl.ds(pl.multiple_of(n * blk, blk), blk)
        dk_ref[:, here, :] = dkc
        dv_ref[:, here, :] = dvc

        @pl.when(has_prev)
        def _():
            before = pl.ds(pl.multiple_of((n - 1) * blk, blk), blk)
            dk_ref[:, before, :] += dkp
            dv_ref[:, before, :] += dvp

        @pl.when((gi == 0) & (n == 0))
        def _():
            dqn_ref[...] = dqn0_ref[...]
            dkn_ref[...] = dkn0_ref[...]

        dqn_ref[...] += dqn
        dkn_ref[...] += dkn

    gshape = jax.ShapeDtypeStruct((1, 1, dh), F32)
    return pl.pallas_call(
        body, name=name, grid=(g // gt, length // blk),
        out_shape=(jax.ShapeDtypeStruct(q.shape, F32),) * 3 + (gshape, gshape),
        in_specs=[cur, prev, cur, prev, cur, gain, gain, cur, col, gain, gain],
        out_specs=(cur, whole, whole, gain, gain),
        compiler_params=_params("arbitrary", "arbitrary"),
    )(q, k, k, v, v, qn, kn, do, dlse, dqn0, dkn0)


def _merge_fn(o1, o2, o3, l1, l2, l3):
    m = lax.stop_gradient(jnp.maximum(jnp.maximum(l1, l2), l3))
    e1, e2, e3 = jnp.exp(l1 - m), jnp.exp(l2 - m), jnp.exp(l3 - m)
    return (e1 * o1 + e2 * o2 + e3 * o3) / (e1 + e2 + e3)


def _merge_specs(h, tm, dh):
    return pl.BlockSpec((h, tm, dh), lambda i: (0, i, 0)), pl.BlockSpec((h, tm, 1), lambda i: (0, i, 0))


MERGE_TILE = 128


def _merge_fwd(os, ls):
    h, t, dh = os[0].shape
    tm = MERGE_TILE
    wide, col = _merge_specs(h, tm, dh)

    def body(o1, o2, o3, l1, l2, l3, out):
        out[...] = _merge_fn(o1[...], o2[...], o3[...], l1[...], l2[...], l3[...])

    return pl.pallas_call(
        body, name="merge_fwd", grid=(t // tm,), out_shape=jax.ShapeDtypeStruct(os[0].shape, F32),
        in_specs=[wide] * 3 + [col] * 3, out_specs=wide, compiler_params=_params("arbitrary"),
    )(*os, *ls)


def _merge_bwd(os, ls, do):
    h, t, dh = os[0].shape
    tm = MERGE_TILE
    wide, col = _merge_specs(h, tm, dh)

    def body(o1, o2, o3, l1, l2, l3, do_ref, d1, d2, d3, e1, e2, e3):
        _, vjp = jax.vjp(_merge_fn, o1[...], o2[...], o3[...], l1[...], l2[...], l3[...])
        outs = vjp(do_ref[...])
        for ref, val in zip((d1, d2, d3, e1, e2, e3), outs):
            ref[...] = val

    oshape = jax.ShapeDtypeStruct(os[0].shape, F32)
    lshape = jax.ShapeDtypeStruct(ls[0].shape, F32)
    return pl.pallas_call(
        body, name="merge_bwd", grid=(t // tm,), out_shape=(oshape,) * 3 + (lshape,) * 3,
        in_specs=[wide] * 3 + [col] * 3 + [wide], out_specs=(wide,) * 3 + (col,) * 3,
        compiler_params=_params("arbitrary"),
    )(*os, *ls, do)


RWKV_VEC = ("mu_r", "mu_k", "mu_v", "mu_w", "mu_a", "mu_g", "w0", "a0", "k_k", "k_a")
RWKV_MAT = ("w1", "w2", "a1", "a2", "g1", "g2")


def _rwkv_pre_fn(cur, prev, vec, w1, w2, a1, a2, g1, g2):
    c = cur.shape[1] // 4
    mu_r, mu_k, mu_v, mu_w, mu_a, mu_g, w0, a0, k_k, k_a = (vec[j:j + 1] for j in range(10))

    def lerp(j, mu):
        xc, xp = cur[:, j * c:(j + 1) * c], prev[:, j * c:(j + 1) * c]
        return xc + (xp - xc) * mu

    r, k, v = lerp(0, mu_r), lerp(1, mu_k), lerp(2, mu_v)
    cw, ca, cg = lerp(3, mu_w), lerp(3, mu_a), lerp(3, mu_g)
    z = w0 + _mm(jnp.tanh(_mm(cw, w1)), w2)
    w_log = jnp.minimum(z, 0.0) - jnp.log(1.0 + jnp.exp(-jnp.abs(z))) - 0.5
    lw = -jnp.exp(w_log)
    a = _sigmoid(a0 + _mm(_mm(ca, a1), a2))
    gate = _mm(_sigmoid(_mm(cg, g1)), g2)
    kkraw = k * k_k
    kmod = k * (1.0 + (a - 1.0) * k_a)
    return r, lw, kmod, v, kkraw, a, gate


def _rwkv_pre_specs(c, mats):
    tm = TOKEN_TILE
    wide = pl.BlockSpec((tm, 4 * c), lambda i: (i, 0))
    one = pl.BlockSpec((tm, c), lambda i: (i, 0))
    vec = pl.BlockSpec((10, c), lambda i: (0, 0))
    mspecs = [pl.BlockSpec(m.shape, lambda i: (0, 0)) for m in mats]
    return wide, one, vec, mspecs


def _rwkv_pre_fwd(cur, prev, vec, mats):
    t, c4 = cur.shape
    c = c4 // 4
    wide, one, vspec, mspecs = _rwkv_pre_specs(c, mats)

    def body(cur_ref, prev_ref, vec_ref, *rest):
        mrefs, outs = rest[:6], rest[6:]
        vals = _rwkv_pre_fn(cur_ref[...], prev_ref[...], vec_ref[...], *(m[...] for m in mrefs))
        for ref, val in zip(outs, vals):
            ref[...] = val

    return pl.pallas_call(
        body, name="rwkv_pre_fwd", grid=(t // TOKEN_TILE,), out_shape=(jax.ShapeDtypeStruct((t, c), F32),) * 7,
        in_specs=[wide, wide, vspec] + mspecs, out_specs=(one,) * 7, compiler_params=_params("arbitrary"),
    )(cur, prev, vec, *mats)


def _rwkv_pre_bwd(cur, prev, vec, mats, cts):
    t, c4 = cur.shape
    c = c4 // 4
    wide, one, vspec, mspecs = _rwkv_pre_specs(c, mats)

    def body(cur_ref, prev_ref, vec_ref, *rest):
        mrefs, ctrefs, outs = rest[:6], rest[6:13], rest[13:]
        _, vjp = jax.vjp(_rwkv_pre_fn, cur_ref[...], prev_ref[...], vec_ref[...], *(m[...] for m in mrefs))
        grads = vjp(tuple(r[...] for r in ctrefs))
        outs[0][...] = grads[0]
        outs[1][...] = grads[1]
        first = pl.program_id(0) == 0

        @pl.when(first)
        def _():
            for ref, val in zip(outs[2:], grads[2:]):
                ref[...] = val

        @pl.when(jnp.logical_not(first))
        def _():
            for ref, val in zip(outs[2:], grads[2:]):
                ref[...] += val

    wshape = jax.ShapeDtypeStruct(cur.shape, F32)
    return pl.pallas_call(
        body, name="rwkv_pre_bwd", grid=(t // TOKEN_TILE,),
        out_shape=(wshape, wshape, jax.ShapeDtypeStruct(vec.shape, F32)) + tuple(jax.ShapeDtypeStruct(m.shape, F32) for m in mats),
        in_specs=[wide, wide, vspec] + mspecs + [one] * 7, out_specs=(wide, wide, vspec) + tuple(mspecs),
        compiler_params=_params("arbitrary"),
    )(cur, prev, vec, *mats, *cts)


def _scan_chunk_fn(h0, r, lw, k, v, kkraw, a, rk, lnw, lnb):
    n = r.shape[1]
    nrm = jnp.sqrt(jnp.sum(kkraw * kkraw, axis=-1, keepdims=True))
    kk = kkraw / jnp.maximum(nrm, 1e-12)
    av, bv = -kk, kk * a
    ti = lax.broadcasted_iota(jnp.int32, (n, n), 0)
    si = lax.broadcasted_iota(jnp.int32, (n, n), 1)
    incl, strict = ti >= si, ti > si
    ones = jnp.broadcast_to(incl.astype(F32)[None], (r.shape[0], n, n))
    cum = _hdot(ones, lw, 2, 1)
    at, rt = av * jnp.exp(cum - lw), r * jnp.exp(cum)
    inv = jnp.exp(-cum)
    bt, kt = bv * inv, k * inv
    lab = jnp.where(strict, _hdot(at, bt, 2, 2), 0.0)
    lak = jnp.where(strict, _hdot(at, kt, 2, 2), 0.0)
    rb = jnp.where(incl, _hdot(rt, bt, 2, 2), 0.0)
    rkm = jnp.where(incl, _hdot(rt, kt, 2, 2), 0.0)
    u = _hdot(at, h0, 2, 1) + _hdot(lak, v, 2, 1)
    p = lab
    m = 1
    while m < n:
        u = u + _hdot(p, u, 2, 1)
        m *= 2
        if m < n:
            p = _hdot(p, p, 2, 1)
    y = _hdot(rt, h0, 2, 1) + _hdot(rb, u, 2, 1) + _hdot(rkm, v, 2, 1)
    last = jnp.exp(jnp.sum(lw, axis=1, keepdims=True))
    h1 = jnp.swapaxes(last, 1, 2) * (h0 + _hdot(bt, u, 1, 1) + _hdot(kt, v, 1, 1))
    mean = jnp.mean(y, axis=-1, keepdims=True)
    yc = y - mean
    var = jnp.mean(yc * yc, axis=-1, keepdims=True)
    yn = yc * lax.rsqrt(var + GN_EPS) * lnw + lnb
    bonus = jnp.sum(r * k * rk, axis=-1, keepdims=True) * v
    return yn + bonus, h1


def _scan_specs(h, t, dh, rev):
    n = SCAN_CHUNK
    nc = t // n
    pos = (lambda c: (0, nc - 1 - c, 0)) if rev else (lambda c: (0, c, 0))
    st = (lambda c: (nc - 1 - c, 0, 0, 0)) if rev else (lambda c: (c, 0, 0, 0))
    seq = pl.BlockSpec((h, n, dh), pos)
    par = pl.BlockSpec((h, 1, dh), lambda c: (0, 0, 0))
    state = pl.BlockSpec((1, h, dh, dh), st)
    return seq, par, state


def _scan_fwd(seqs, pars):
    h, t, dh = seqs[0].shape
    nc = t // SCAN_CHUNK
    seq, par, state = _scan_specs(h, t, dh, False)

    def body(r, lw, k, v, kkraw, a, rk, lnw, lnb, o_ref, st_ref, h_ref):
        @pl.when(pl.program_id(0) == 0)
        def _():
            h_ref[...] = jnp.zeros_like(h_ref)

        h0 = h_ref[...]
        st_ref[0] = h0
        o, h1 = _scan_chunk_fn(h0, r[...], lw[...], k[...], v[...], kkraw[...], a[...], rk[...], lnw[...], lnb[...])
        o_ref[...] = o
        h_ref[...] = h1

    return pl.pallas_call(
        body, name="rwkv_scan_fwd", grid=(nc,),
        out_shape=(jax.ShapeDtypeStruct((h, t, dh), F32), jax.ShapeDtypeStruct((nc, h, dh, dh), F32)),
        in_specs=[seq] * 6 + [par] * 3, out_specs=(seq, state),
        scratch_shapes=[pltpu.VMEM((h, dh, dh), F32)], compiler_params=_params("arbitrary"),
    )(*seqs, *pars)


def _scan_bwd(seqs, pars, states, do):
    h, t, dh = seqs[0].shape
    nc = t // SCAN_CHUNK
    seq, par, state = _scan_specs(h, t, dh, True)

    def body(r, lw, k, v, kkraw, a, rk, lnw, lnb, st_ref, do_ref, *rest):
        douts, dpars, dh_ref = rest[:6], rest[6:9], rest[9]
        first = pl.program_id(0) == 0

        @pl.when(first)
        def _():
            dh_ref[...] = jnp.zeros_like(dh_ref)

        _, vjp = jax.vjp(_scan_chunk_fn, st_ref[0], r[...], lw[...], k[...], v[...], kkraw[...], a[...],
                         rk[...], lnw[...], lnb[...])
        grads = vjp((do_ref[...], dh_ref[...]))
        dh_ref[...] = grads[0]
        for ref, val in zip(douts, grads[1:7]):
            ref[...] = val

        @pl.when(first)
        def _():
            for ref, val in zip(dpars, grads[7:]):
                ref[...] = val

        @pl.when(jnp.logical_not(first))
        def _():
            for ref, val in zip(dpars, grads[7:]):
                ref[...] += val

    sshape = jax.ShapeDtypeStruct((h, t, dh), F32)
    pshape = jax.ShapeDtypeStruct((h, 1, dh), F32)
    return pl.pallas_call(
        body, name="rwkv_scan_bwd", grid=(nc,), out_shape=(sshape,) * 6 + (pshape,) * 3,
        in_specs=[seq] * 6 + [par] * 3 + [state, seq], out_specs=(seq,) * 6 + (par,) * 3,
        scratch_shapes=[pltpu.VMEM((h, dh, dh), F32)], compiler_params=_params("arbitrary"),
    )(*seqs, *pars, states, do)


def _heads(x):
    return x.reshape(x.shape[0], -1, HEAD_DIM).transpose(1, 0, 2)


def _unheads(x):
    return x.transpose(1, 0, 2).reshape(x.shape[1], -1)


def _to_sub(x, dil):
    h, t, d = x.shape
    return x.reshape(h, t // dil, dil, d).transpose(0, 2, 1, 3).reshape(h * dil, t // dil, d)


def _from_sub(x, dil):
    g, length, d = x.shape
    return x.reshape(g // dil, dil, length, d).transpose(0, 2, 1, 3).reshape(g // dil, length * dil, d)


def _local_step(x, target, w):
    c = w["mu_r"].shape[-1]
    att_w = w["w_in"].shape[0] * w["w_in"].shape[2] - 4 * c
    qn, kn = w["q_norm"].reshape(1, 1, HEAD_DIM), w["k_norm"].reshape(1, 1, HEAD_DIM)
    vec = jnp.concatenate([w[n].reshape(1, c) for n in RWKV_VEC], axis=0)
    mats = [w[n] for n in RWKV_MAT]
    pars = [w[n].reshape(-1, 1, HEAD_DIM) for n in ("r_k", "ln_x_w", "ln_x_b")]

    x1 = _ffn_fwd(x, w["ffn1_norm"], w["ffn1_w_gate"], w["ffn1_w_up"], w["ffn1_w_down"], "ffn1_fwd")
    proj = _proj_fwd(x1, w["mix_norm"], w["w_in"])
    hw = att_w // 3
    qkv = [_heads(proj[:, j * hw:(j + 1) * hw]) for j in range(3)]
    subs = [[_to_sub(a, dil) for a in qkv] for dil in DILATIONS]
    outs = [_att_fwd(*s, qn, kn, f"att_fwd_d{dil}") for s, dil in zip(subs, DILATIONS)]
    os_ = [_from_sub(o, dil) for (o, _), dil in zip(outs, DILATIONS)]
    ls_ = [_from_sub(l, dil) for (_, l), dil in zip(outs, DILATIONS)]
    att = _unheads(_merge_fwd(os_, ls_))
    cur = proj[:, att_w:]
    prev = jnp.concatenate([jnp.zeros_like(cur[:1]), cur[:-1]], axis=0)
    pre = _rwkv_pre_fwd(cur, prev, vec, mats)
    seqs = [_heads(a) for a in pre[:6]]
    gate = pre[6]
    opg_h, states = _scan_fwd(seqs, pars)
    opg = _unheads(opg_h)
    x2 = _mixout_fwd(x1, att, opg, gate, w["w_out"])
    x3 = _ffn_fwd(x2, w["ffn2_norm"], w["ffn2_w_gate"], w["ffn2_w_up"], w["ffn2_w_down"], "ffn2_fwd")
    dy, loss = _loss_head(x3, target)

    g = {}
    dx2, g["ffn2_norm"], g["ffn2_w_gate"], g["ffn2_w_up"], g["ffn2_w_down"] = _ffn_bwd(
        x2, w["ffn2_norm"], w["ffn2_w_gate"], w["ffn2_w_up"], w["ffn2_w_down"], dy, "ffn2_bwd")
    datt, dopg, dgate, g["w_out"] = _mixout_bwd(att, opg, gate, w["w_out"], dx2)
    dscan = _scan_bwd(seqs, pars, states, _heads(dopg))
    for n, d in zip(("r_k", "ln_x_w", "ln_x_b"), dscan[6:]):
        g[n] = d
    dpre = _rwkv_pre_bwd(cur, prev, vec, mats, [_unheads(d) for d in dscan[:6]] + [dgate])
    dcur, dprev, dvec = dpre[:3]
    for n, d in zip(RWKV_MAT, dpre[3:]):
        g[n] = d
    for j, n in enumerate(RWKV_VEC):
        g[n] = dvec[j:j + 1]
    dmerge = _merge_bwd(os_, ls_, _heads(datt))
    dqn = dkn = jnp.zeros((1, 1, HEAD_DIM), F32)
    dqkv = []
    for j, dil in enumerate(DILATIONS):
        dq, dk, dv, dqn, dkn = _att_bwd(*subs[j], qn, kn, _to_sub(dmerge[j], dil), _to_sub(dmerge[3 + j], dil),
                                        dqn, dkn, f"att_bwd_d{dil}")
        dqkv.append([_unheads(_from_sub(a, dil)) for a in (dq, dk, dv)])
    g["q_norm"], g["k_norm"] = dqn, dkn
    dshift = jnp.concatenate([dprev[1:], jnp.zeros_like(dprev[:1])], axis=0)
    dps = [jnp.concatenate(dqkv[j] + [tail], axis=1) for j, tail in enumerate((dcur, dshift, jnp.zeros_like(dcur)))]
    dx1, g["mix_norm"], g["w_in"] = _proj_bwd(x1, w["mix_norm"], w["w_in"], *dps, dx2)
    dx, g["ffn1_norm"], g["ffn1_w_gate"], g["ffn1_w_up"], g["ffn1_w_down"] = _ffn_bwd(
        x, w["ffn1_norm"], w["ffn1_w_gate"], w["ffn1_w_up"], w["ffn1_w_down"], dx1, "ffn1_bwd")
    return loss, dx, g


N_SHARDS = 4


def _place():
    return lax.axis_index("x"), lax.axis_index("y"), lax.axis_index("c")


def _chip_peers(x, y):
    return [(1 - x, y), (x, 1 - y), (1 - x, 1 - y)]


def _gather_xy(shards):
    n = len(shards)

    def body(*refs):
        ins, outs = refs[:n], refs[n:2 * n]
        send_sems, recv_sems, local_sems = refs[2 * n:]
        x, y, c = _place()
        me = 2 * x + y
        copies = []
        for i in range(n):
            own = pltpu.make_async_copy(ins[i], outs[i].at[me], local_sems.at[i])
            own.start()
            copies.append(own)
            for k, (px, py) in enumerate(_chip_peers(x, y)):
                cp = pltpu.make_async_remote_copy(
                    src_ref=ins[i], dst_ref=outs[i].at[me], send_sem=send_sems.at[i, k], recv_sem=recv_sems.at[i, k],
                    device_id=(px, py, c), device_id_type=MESH)
                cp.start()
                copies.append(cp)
        for cp in copies:
            cp.wait()

    return pl.pallas_call(
        body, name="gather_weights",
        out_shape=tuple(jax.ShapeDtypeStruct((N_SHARDS,) + s.shape, s.dtype) for s in shards),
        in_specs=[ANY] * n, out_specs=(ANY,) * n,
        scratch_shapes=[pltpu.SemaphoreType.DMA((n, 3)), pltpu.SemaphoreType.DMA((n, 3)), pltpu.SemaphoreType.DMA((n,))],
    )(*shards)


def _scatter_partials(parts):
    n = len(parts)

    def body(*refs):
        ins, outs = refs[:n], refs[n:2 * n]
        send_sems, recv_sems = refs[2 * n:]
        x, y, c = _place()
        copies = []
        for i in range(n):
            for k, (px, py) in enumerate(_chip_peers(x, y)):
                cp = pltpu.make_async_remote_copy(
                    src_ref=ins[i].at[2 * px + py], dst_ref=outs[i].at[k], send_sem=send_sems.at[i, k],
                    recv_sem=recv_sems.at[i, k], device_id=(px, py, c), device_id_type=MESH)
                cp.start()
                copies.append(cp)
        for cp in copies:
            cp.wait()

    return pl.pallas_call(
        body, name="scatter_partials",
        out_shape=tuple(jax.ShapeDtypeStruct((3,) + p.shape[1:], p.dtype) for p in parts),
        in_specs=[ANY] * n, out_specs=(ANY,) * n,
        scratch_shapes=[pltpu.SemaphoreType.DMA((n, 3)), pltpu.SemaphoreType.DMA((n, 3))],
    )(*parts)


def _sibling_swap(arrays):
    n = len(arrays)

    def body(*refs):
        ins, outs = refs[:n], refs[n:2 * n]
        send_sems, recv_sems = refs[2 * n:]
        x, y, c = _place()
        copies = []
        for i in range(n):
            cp = pltpu.make_async_remote_copy(
                src_ref=ins[i], dst_ref=outs[i], send_sem=send_sems.at[i], recv_sem=recv_sems.at[i],
                device_id=(x, y, 1 - c), device_id_type=MESH)
            cp.start()
            copies.append(cp)
        for cp in copies:
            cp.wait()

    return pl.pallas_call(
        body, name="sibling_swap",
        out_shape=tuple(jax.ShapeDtypeStruct(a.shape, a.dtype) for a in arrays),
        in_specs=[ANY] * n, out_specs=(ANY,) * n,
        scratch_shapes=[pltpu.SemaphoreType.DMA((n,)), pltpu.SemaphoreType.DMA((n,))],
    )(*arrays)


N_DEV = 8


def _allreduce_small(pack):
    def body(in_ref, out_ref, buf, send_sems, recv_sems):
        x, y, c = _place()
        me = 4 * x + 2 * y + c
        buf[me] = in_ref[...]

        def copy(j, slot):
            px, py, pc = x ^ (j >> 2), y ^ ((j >> 1) & 1), c ^ (j & 1)
            return pltpu.make_async_remote_copy(
                src_ref=in_ref, dst_ref=buf.at[slot(px, py, pc)], send_sem=send_sems.at[j], recv_sem=recv_sems.at[j],
                device_id=(px, py, pc), device_id_type=MESH)

        for j in range(1, N_DEV):
            copy(j, lambda px, py, pc: me).start()
        for j in range(1, N_DEV):
            landing = copy(j, lambda px, py, pc: 4 * px + 2 * py + pc)
            landing.wait_send()
            landing.wait_recv()
        acc = buf[0]
        for s in range(1, N_DEV):
            acc = acc + buf[s]
        out_ref[...] = acc

    return pl.pallas_call(
        body, name="allreduce_small", out_shape=jax.ShapeDtypeStruct(pack.shape, F32),
        in_specs=[VMEM_FULL], out_specs=VMEM_FULL,
        scratch_shapes=[pltpu.VMEM((N_DEV,) + pack.shape, F32), pltpu.SemaphoreType.DMA((N_DEV,)),
                        pltpu.SemaphoreType.DMA((N_DEV,))],
    )(pack)


ROW_TILE_MAX = 256
BF16_SUBLANES = 16


def _row_tile(rows):
    for tr in range(min(rows, ROW_TILE_MAX), 0, -1):
        if rows % tr == 0 and tr % BF16_SUBLANES == 0:
            return tr
    return rows


def _reduce_own(me, part, recv, name):
    _, r, cols = part.shape
    tr = _row_tile(r)

    def body(me_ref, p_ref, rv_ref, o_ref):
        acc = p_ref[0]
        for k in range(3):
            acc = acc + rv_ref[k].astype(F32)
        o_ref[...] = acc

    return pl.pallas_call(
        body, name=name, out_shape=jax.ShapeDtypeStruct((r, cols), F32),
        grid_spec=pltpu.PrefetchScalarGridSpec(
            num_scalar_prefetch=1, grid=(r // tr,),
            in_specs=[pl.BlockSpec((1, tr, cols), lambda i, me_ref: (me_ref[0], i, 0)),
                      pl.BlockSpec((3, tr, cols), lambda i, me_ref: (0, i, 0))],
            out_specs=pl.BlockSpec((tr, cols), lambda i, me_ref: (i, 0))),
        compiler_params=_params("arbitrary"),
    )(me, part, recv)


def _adamw(w, ga, gb, m, v, name):
    r, cols = w.shape
    tr = _row_tile(r)
    c1 = 1.0 - ADAM_B1 ** ADAM_STEP
    c2 = 1.0 - ADAM_B2 ** ADAM_STEP

    def body(w_ref, ga_ref, gb_ref, m_ref, v_ref, g_out, d_out, m_out, v_out):
        g = ga_ref[...] + gb_ref[...]
        mn = ADAM_B1 * m_ref[...] + (1.0 - ADAM_B1) * g
        vn = ADAM_B2 * v_ref[...] + (1.0 - ADAM_B2) * (g * g)
        g_out[...] = g
        m_out[...] = mn
        v_out[...] = vn
        d_out[...] = -ADAM_LR * ((mn / c1) / (jnp.sqrt(vn / c2) + ADAM_EPS) + ADAM_WD * w_ref[...])

    tile = pl.BlockSpec((tr, cols), lambda i: (i, 0))
    shape = jax.ShapeDtypeStruct((r, cols), F32)
    return pl.pallas_call(
        body, name=name, grid=(r // tr,), out_shape=(shape,) * 4, in_specs=[tile] * 5, out_specs=(tile,) * 4,
        compiler_params=_params("arbitrary"),
    )(w, ga, gb, m, v)


PACK_COLS = 512


def _to_rows(a):
    flat = a.reshape(-1)
    pad = (-flat.shape[0]) % PACK_COLS
    return jnp.pad(flat, (0, pad)).reshape(-1, PACK_COLS)


def _pack(arrays, extra_rows=0):
    rows = [_to_rows(a) for a in arrays]
    n = sum(r.shape[0] for r in rows) + extra_rows
    pad = (-n) % 8
    return jnp.concatenate(rows + [jnp.zeros((extra_rows + pad, PACK_COLS), F32)], axis=0)


def _unpack(pack, like):
    out, at = [], 0
    for a in like:
        n = -(-a.size // PACK_COLS)
        out.append(pack[at:at + n].reshape(-1)[:a.size].reshape(a.shape))
        at += n
    return out


COL_SHARDED = ("ffn1_w_gate", "ffn1_w_up", "w_in", "ffn2_w_gate", "ffn2_w_up", "w2", "a2", "g2")
ROW_SHARDED = ("ffn1_w_down", "ffn2_w_down", "w_out", "w1", "a1", "g1")
CHUNKED = ("ffn1_w_gate", "ffn1_w_up", "ffn1_w_down", "w_in", "ffn2_w_gate", "ffn2_w_up", "ffn2_w_down")
WEIGHTS = ("ffn1_norm", "ffn1_w_gate", "ffn1_w_up", "ffn1_w_down", "mix_norm", "w_in", "q_norm", "k_norm",
           "mu_r", "mu_k", "mu_v", "mu_w", "mu_a", "mu_g", "w0", "w1", "w2", "a0", "a1", "a2", "g1", "g2",
           "k_k", "k_a", "r_k", "ln_x_w", "ln_x_b", "w_out", "ffn2_norm", "ffn2_w_gate", "ffn2_w_up", "ffn2_w_down")


def _full_from_blocks(name, blocks):
    if name in CHUNKED:
        return blocks
    if name in ROW_SHARDED:
        return blocks.reshape(-1, blocks.shape[-1])
    return blocks.transpose(1, 0, 2).reshape(blocks.shape[1], -1)


def _blocks_from_full(name, full):
    if name in CHUNKED:
        return full
    if name in ROW_SHARDED:
        return full.reshape(N_SHARDS, -1, full.shape[-1])
    return full.reshape(full.shape[0], N_SHARDS, -1).transpose(1, 0, 2)


def kernel(
        x, ffn1_norm, ffn1_w_gate, ffn1_w_up, ffn1_w_down, mix_norm, w_in, q_norm, k_norm, mu_r, mu_k, mu_v, mu_w,
        mu_a, mu_g, w0, w1, w2, a0, a1, a2, g1, g2, k_k, k_a, r_k, ln_x_w, ln_x_b, w_out, ffn2_norm, ffn2_w_gate,
        ffn2_w_up, ffn2_w_down, loss_target, m_ffn1_norm, m_ffn1_w_gate, m_ffn1_w_up, m_ffn1_w_down, m_mix_norm,
        m_w_in, m_q_norm, m_k_norm, m_mu_r, m_mu_k, m_mu_v, m_mu_w, m_mu_a, m_mu_g, m_w0, m_w1, m_w2, m_a0, m_a1,
        m_a2, m_g1, m_g2, m_k_k, m_k_a, m_r_k, m_ln_x_w, m_ln_x_b, m_w_out, m_ffn2_norm, m_ffn2_w_gate, m_ffn2_w_up,
        m_ffn2_w_down, v_ffn1_norm, v_ffn1_w_gate, v_ffn1_w_up, v_ffn1_w_down, v_mix_norm, v_w_in, v_q_norm, v_k_norm,
        v_mu_r, v_mu_k, v_mu_v, v_mu_w, v_mu_a, v_mu_g, v_w0, v_w1, v_w2, v_a0, v_a1, v_a2, v_g1, v_g2, v_k_k, v_k_a,
        v_r_k, v_ln_x_w, v_ln_x_b, v_w_out, v_ffn2_norm, v_ffn2_w_gate, v_ffn2_w_up, v_ffn2_w_down):
    given = dict(locals())
    sharded = COL_SHARDED + ROW_SHARDED
    sharded = tuple(n for n in WEIGHTS if n in sharded)
    small = tuple(n for n in WEIGHTS if n not in sharded)

    blocks = _gather_xy([given[n][0].astype(BF16) for n in sharded])
    w = {n: given[n] for n in small}
    for n, b in zip(sharded, blocks):
        full = _full_from_blocks(n, b)
        w[n] = full.astype(F32) if n in RWKV_MAT else full

    loss, dx, g = _local_step(x[0], loss_target[0], w)

    me = (2 * lax.axis_index("x") + lax.axis_index("y")).astype(jnp.int32).reshape(1)
    parts = [_blocks_from_full(n, g[n]) for n in sharded]
    recv = _scatter_partials([p.astype(BF16) for p in parts])
    mine = []
    for n, p, rv in zip(sharded, parts, recv):
        p2 = p.reshape(N_SHARDS, -1, p.shape[-1])
        mine.append(_reduce_own(me, p2, rv.reshape(3, -1, rv.shape[-1]), f"reduce_{n}"))
    theirs = _sibling_swap(mine)
    out = {}
    for n, a, b in zip(sharded, mine, theirs):
        shape = given[n].shape
        two_d = (-1, shape[-1])
        res = _adamw(given[n].reshape(two_d), a, b, given["m_" + n].reshape(two_d), given["v_" + n].reshape(two_d), f"adamw_{n}")
        out[n] = [r.reshape(shape) for r in res]

    gpack = _pack([g[n] for n in small], extra_rows=1)
    n_rows = sum(-(-given[n].size // PACK_COLS) for n in small)
    gpack = gpack.at[n_rows, :loss.shape[1]].set(loss[0])
    gsum = _allreduce_small(gpack)
    res = _adamw(_pack([given[n] for n in small], 1), gsum, jnp.zeros_like(gsum), _pack([given["m_" + n] for n in small], 1),
                 _pack([given["v_" + n] for n in small], 1), "adamw_small")
    like = [given[n] for n in small]
    for j, r in enumerate(res):
        for n, a in zip(small, _unpack(r, like)):
            out.setdefault(n, [None] * 4)[j] = a
    total_loss = gsum[n_rows, 0]
    return (total_loss, dx[None], *[out[n][0] for n in WEIGHTS], *[out[n][1] for n in WEIGHTS],
            *[out[n][2] for n in WEIGHTS], *[out[n][3] for n in WEIGHTS])
```

```python
import functools

import jax
import jax.numpy as jnp
from jax import lax
from jax.experimental import pallas as pl
from jax.experimental.pallas import tpu as pltpu

F32 = jnp.float32
BF16 = jnp.bfloat16
HIGHEST = lax.Precision.HIGHEST
MESH = pl.DeviceIdType.MESH

RMS_EPS = 1e-6
GN_EPS = 64e-5
NEG_INF = -1e30
FFN_RESIDUAL = 0.5
HEAD_DIM = 64
ATT_BLOCK = 128
DILATIONS = (1, 4, 16)
SCAN_CHUNK = 64
TOKEN_TILE = 256

ADAM_LR = 0.001
ADAM_B1 = 0.9
ADAM_B2 = 0.999
ADAM_EPS = 1e-08
ADAM_WD = 0.01
ADAM_STEP = 10

VMEM_FULL = pl.BlockSpec(memory_space=pltpu.VMEM)
ANY = pl.BlockSpec(memory_space=pl.ANY)


VMEM_LIMIT = 56 * 1024 * 1024


def _params(*sem):
    return pltpu.CompilerParams(dimension_semantics=sem, vmem_limit_bytes=VMEM_LIMIT)


def _dot(a, b, dims):
    return lax.dot_general(a.astype(BF16), b.astype(BF16), (dims, ((), ())), preferred_element_type=F32)


def _dot_nn(a, b):
    return _dot(a, b, ((1,), (0,)))


def _dot_nt(a, b):
    return _dot(a, b, ((1,), (1,)))


def _dot_tn(a, b):
    return _dot(a, b, ((0,), (0,)))


@jax.custom_vjp
def _mm(a, b):
    return _dot_nn(a, b)


def _mm_fwd(a, b):
    return _dot_nn(a, b), (a, b)


def _mm_bwd(res, g):
    a, b = res
    return _dot_nt(g, b).astype(a.dtype), _dot_tn(a, g).astype(b.dtype)


_mm.defvjp(_mm_fwd, _mm_bwd)


def _bdot(a, b, ca, cb):
    return lax.dot_general(a.astype(BF16), b.astype(BF16), (((ca,), (cb,)), ((0,), (0,))), preferred_element_type=F32)


@jax.custom_vjp
def _bmm_nt(a, b):
    return _bdot(a, b, 2, 2)


def _bmm_nt_fwd(a, b):
    return _bdot(a, b, 2, 2), (a, b)


def _bmm_nt_bwd(res, g):
    a, b = res
    return _bdot(g, b, 2, 1), _bdot(g, a, 1, 1)


_bmm_nt.defvjp(_bmm_nt_fwd, _bmm_nt_bwd)


@jax.custom_vjp
def _bmm_nn(a, b):
    return _bdot(a, b, 2, 1)


def _bmm_nn_fwd(a, b):
    return _bdot(a, b, 2, 1), (a, b)


def _bmm_nn_bwd(res, g):
    a, b = res
    return _bdot(g, b, 2, 2), _bdot(a, g, 1, 1)


_bmm_nn.defvjp(_bmm_nn_fwd, _bmm_nn_bwd)


def _hdot(a, b, ca, cb):
    return lax.dot_general(a, b, (((ca,), (cb,)), ((0,), (0,))), precision=lax.Precision.HIGH, preferred_element_type=F32)


def _sigmoid(x):
    return 1.0 / (1.0 + jnp.exp(-x))


def _rms(x):
    return lax.rsqrt(jnp.mean(x * x, axis=-1, keepdims=True) + RMS_EPS)


def _ffn_fwd(x, norm, wg, wu, wd, dep, name):
    t, d = x.shape
    nc = wg.shape[0]
    tm = TOKEN_TILE

    def body(x_ref, n_ref, wg_ref, wu_ref, wd_ref, dep_ref, o_ref):
        xv = x_ref[...]
        h = (xv * _rms(xv) * n_ref[...]).astype(BF16)
        acc = jnp.zeros((tm, d), F32)
        for c in range(nc):
            g = jnp.dot(h, wg_ref[c], preferred_element_type=F32)
            u = jnp.dot(h, wu_ref[c], preferred_element_type=F32)
            a = (g * _sigmoid(g) * u).astype(BF16)
            acc = acc + jnp.dot(a, wd_ref[c], preferred_element_type=F32)
        o_ref[...] = xv + FFN_RESIDUAL * acc

    tile = pl.BlockSpec((tm, d), lambda i: (i, 0))
    return pl.pallas_call(
        body, name=name, grid=(t // tm,), out_shape=jax.ShapeDtypeStruct((t, d), F32),
        in_specs=[tile, pl.BlockSpec((1, d), lambda i: (0, 0)), VMEM_FULL, VMEM_FULL, VMEM_FULL, ANY],
        out_specs=tile, compiler_params=_params("arbitrary"),
    )(x, norm, wg, wu, wd, dep)


def _rmsnorm_bwd(xv, gain, dh):
    rs = _rms(xv)
    xn = xv * rs
    dxn = dh * gain
    dx = rs * (dxn - xn * jnp.mean(dxn * xn, axis=-1, keepdims=True))
    return dx, jnp.sum(dh * xn, axis=0, keepdims=True)


def _ffn_bwd(x, norm, wg, wu, wd, dy, dep, name):
    t, d = x.shape
    nc, _, fc = wg.shape
    tm = TOKEN_TILE
    nt = t // tm

    def body(x_ref, n_ref, wg_ref, wu_ref, wd_ref, dy_ref, dep_ref, dx_ref, dn_ref, dwg_ref, dwu_ref, dwd_ref, dh_ref):
        c, i = pl.program_id(0), pl.program_id(1)
        rows = pl.ds(pl.multiple_of(i * tm, tm), tm)
        xv = x_ref[...]
        gain = n_ref[...]
        h = (xv * _rms(xv) * gain).astype(BF16)
        dy = dy_ref[...]
        dyb = (FFN_RESIDUAL * dy).astype(BF16)
        g = jnp.dot(h, wg_ref[0], preferred_element_type=F32)
        u = jnp.dot(h, wu_ref[0], preferred_element_type=F32)
        sg = _sigmoid(g)
        s = g * sg
        a = (s * u).astype(BF16)
        da = _dot_nt(dyb, wd_ref[0])
        dub = (da * s).astype(BF16)
        dgb = (da * u * (sg * (1.0 + g * (1.0 - sg)))).astype(BF16)
        dwd_c = _dot_tn(a, dyb)
        dwg_c = _dot_tn(h, dgb)
        dwu_c = _dot_tn(h, dub)
        dh_c = _dot_nt(dgb, wg_ref[0]) + _dot_nt(dub, wu_ref[0])

        @pl.when(i == 0)
        def _():
            dwd_ref[0] = dwd_c
            dwg_ref[0] = dwg_c
            dwu_ref[0] = dwu_c

        @pl.when(i > 0)
        def _():
            dwd_ref[0] += dwd_c
            dwg_ref[0] += dwg_c
            dwu_ref[0] += dwu_c

        @pl.when(c == 0)
        def _():
            dh_ref[rows, :] = dh_c

        @pl.when(c > 0)
        def _():
            dh_ref[rows, :] += dh_c

        @pl.when(c == nc - 1)
        def _():
            dx, dn = _rmsnorm_bwd(xv, gain, dh_ref[rows, :])
            dx_ref[...] = dx + dy

            @pl.when(i == 0)
            def _():
                dn_ref[...] = dn

            @pl.when(i > 0)
            def _():
                dn_ref[...] += dn

    tile = pl.BlockSpec((tm, d), lambda c, i: (i, 0))
    row = pl.BlockSpec((1, d), lambda c, i: (0, 0))
    wcol = pl.BlockSpec((1, d, fc), lambda c, i: (c, 0, 0))
    wrow = pl.BlockSpec((1, fc, d), lambda c, i: (c, 0, 0))
    last = pl.BlockSpec((tm, d), lambda c, i: (jnp.where(c == nc - 1, i, 0), 0))
    return pl.pallas_call(
        body, name=name, grid=(nc, nt),
        out_shape=(jax.ShapeDtypeStruct((t, d), F32), jax.ShapeDtypeStruct((1, d), F32),
                   jax.ShapeDtypeStruct(wg.shape, F32), jax.ShapeDtypeStruct(wu.shape, F32),
                   jax.ShapeDtypeStruct(wd.shape, F32)),
        in_specs=[tile, row, wcol, wcol, wrow, tile, ANY],
        out_specs=(last, row, wcol, wcol, wrow),
        scratch_shapes=[pltpu.VMEM((t, d), F32)],
        compiler_params=_params("arbitrary", "arbitrary"),
    )(x, norm, wg, wu, wd, dy, dep)


def _proj_fwd(x, norm, w):
    t, d = x.shape
    nc, _, ncol = w.shape
    tm = TOKEN_TILE

    def body(x_ref, n_ref, w_ref, o_ref):
        xv = x_ref[...]
        h = (xv * _rms(xv) * n_ref[...]).astype(BF16)
        for c in range(nc):
            o_ref[:, c * ncol:(c + 1) * ncol] = jnp.dot(h, w_ref[c], preferred_element_type=F32)

    return pl.pallas_call(
        body, name="proj_fwd", grid=(t // tm,), out_shape=jax.ShapeDtypeStruct((t, nc * ncol), F32),
        in_specs=[pl.BlockSpec((tm, d), lambda i: (i, 0)), pl.BlockSpec((1, d), lambda i: (0, 0)), VMEM_FULL],
        out_specs=pl.BlockSpec((tm, nc * ncol), lambda i: (i, 0)), compiler_params=_params("arbitrary"),
    )(x, norm, w)


def _proj_bwd(x, norm, w, dpa, dpb, dpc, dres):
    t, d = x.shape
    nc, _, ncol = w.shape
    tm = TOKEN_TILE
    nt = t // tm

    def body(x_ref, n_ref, w_ref, dpa_ref, dpb_ref, dpc_ref, dres_ref, dx_ref, dn_ref, dw_ref, dh_ref):
        c, i = pl.program_id(0), pl.program_id(1)
        rows = pl.ds(pl.multiple_of(i * tm, tm), tm)
        xv = x_ref[...]
        gain = n_ref[...]
        h = (xv * _rms(xv) * gain).astype(BF16)
        dpv = (dpa_ref[...] + dpb_ref[...] + dpc_ref[...]).astype(BF16)
        dw_c = _dot_tn(h, dpv)
        dh_c = _dot_nt(dpv, w_ref[0])

        @pl.when(i == 0)
        def _():
            dw_ref[0] = dw_c

        @pl.when(i > 0)
        def _():
            dw_ref[0] += dw_c

        @pl.when(c == 0)
        def _():
            dh_ref[rows, :] = dh_c

        @pl.when(c > 0)
        def _():
            dh_ref[rows, :] += dh_c

        @pl.when(c == nc - 1)
        def _():
            dx, dn = _rmsnorm_bwd(xv, gain, dh_ref[rows, :])
            dx_ref[...] = dx + dres_ref[...]

            @pl.when(i == 0)
            def _():
                dn_ref[...] = dn

            @pl.when(i > 0)
            def _():
                dn_ref[...] += dn

    tile = pl.BlockSpec((tm, d), lambda c, i: (i, 0))
    row = pl.BlockSpec((1, d), lambda c, i: (0, 0))
    wcol = pl.BlockSpec((1, d, ncol), lambda c, i: (c, 0, 0))
    ptile = pl.BlockSpec((tm, ncol), lambda c, i: (i, c))
    last = pl.BlockSpec((tm, d), lambda c, i: (jnp.where(c == nc - 1, i, 0), 0))
    return pl.pallas_call(
        body, name="proj_bwd", grid=(nc, nt),
        out_shape=(jax.ShapeDtypeStruct((t, d), F32), jax.ShapeDtypeStruct((1, d), F32),
                   jax.ShapeDtypeStruct(w.shape, F32)),
        in_specs=[tile, row, wcol, ptile, ptile, ptile, tile],
        out_specs=(last, row, wcol),
        scratch_shapes=[pltpu.VMEM((t, d), F32)],
        compiler_params=_params("arbitrary", "arbitrary"),
    )(x, norm, w, dpa, dpb, dpc, dres)


def _mixout_fwd(x, att, opg, gate, w):
    t, d = x.shape
    half = att.shape[1]
    tm = TOKEN_TILE

    def body(x_ref, att_ref, opg_ref, g_ref, w_ref, o_ref):
        mix = jnp.concatenate([att_ref[...], opg_ref[...] * g_ref[...]], axis=-1).astype(BF16)
        o_ref[...] = x_ref[...] + jnp.dot(mix, w_ref[...], preferred_element_type=F32)

    tile = pl.BlockSpec((tm, d), lambda i: (i, 0))
    htile = pl.BlockSpec((tm, half), lambda i: (i, 0))
    return pl.pallas_call(
        body, name="mixout_fwd", grid=(t // tm,), out_shape=jax.ShapeDtypeStruct((t, d), F32),
        in_specs=[tile, htile, htile, htile, VMEM_FULL], out_specs=tile, compiler_params=_params("arbitrary"),
    )(x, att, opg, gate, w)


def _mixout_bwd(att, opg, gate, w, dy, dep):
    t, half = att.shape
    d = dy.shape[1]
    tm = TOKEN_TILE

    def body(att_ref, opg_ref, g_ref, w_ref, dy_ref, dep_ref, datt_ref, dopg_ref, dg_ref, dw_ref):
        i = pl.program_id(0)
        opg_v, g_v = opg_ref[...], g_ref[...]
        mix = jnp.concatenate([att_ref[...], opg_v * g_v], axis=-1).astype(BF16)
        dyb = dy_ref[...].astype(BF16)
        dmix = _dot_nt(dyb, w_ref[...])
        dw = _dot_tn(mix, dyb)
        datt_ref[...] = dmix[:, :half]
        drw = dmix[:, half:]
        dopg_ref[...] = drw * g_v
        dg_ref[...] = drw * opg_v

        @pl.when(i == 0)
        def _():
            dw_ref[...] = dw

        @pl.when(i > 0)
        def _():
            dw_ref[...] += dw

    tile = pl.BlockSpec((tm, d), lambda i: (i, 0))
    htile = pl.BlockSpec((tm, half), lambda i: (i, 0))
    hshape = jax.ShapeDtypeStruct((t, half), F32)
    return pl.pallas_call(
        body, name="mixout_bwd", grid=(t // tm,),
        out_shape=(hshape, hshape, hshape, jax.ShapeDtypeStruct(w.shape, F32)),
        in_specs=[htile, htile, htile, VMEM_FULL, tile, ANY],
        out_specs=(htile, htile, htile, pl.BlockSpec(w.shape, lambda i: (0, 0))),
        compiler_params=_params("arbitrary"),
    )(att, opg, gate, w, dy, dep)


def _loss_head(y, target):
    t, d = y.shape
    tm = TOKEN_TILE

    def body(y_ref, t_ref, dy_ref, loss_ref):
        i = pl.program_id(0)
        err = y_ref[...] - t_ref[...]
        dy_ref[...] = err * (1.0 / d)
        part = 0.5 * jnp.sum(jnp.mean(err * err, axis=-1, keepdims=True), axis=0, keepdims=True)

        @pl.when(i == 0)
        def _():
            loss_ref[...] = jnp.zeros_like(loss_ref)

        loss_ref[...] += jnp.broadcast_to(part, loss_ref.shape)

    tile = pl.BlockSpec((tm, d), lambda i: (i, 0))
    return pl.pallas_call(
        body, name="loss_head", grid=(t // tm,),
        out_shape=(jax.ShapeDtypeStruct((t, d), F32), jax.ShapeDtypeStruct((1, 128), F32)),
        in_specs=[tile, tile], out_specs=(tile, pl.BlockSpec((1, 128), lambda i: (0, 0))),
        compiler_params=_params("arbitrary"),
    )(y, target)


def _att_block(q, kp, kc, vp, vc, qn, kn, has_prev):
    blk = q.shape[1]

    def hn(v, gain):
        return v * _rms(v) * gain

    qh, khp, khc = hn(q, qn), hn(kp, kn), hn(kc, kn)
    scale = HEAD_DIM ** -0.5
    sp = _bmm_nt(qh, khp) * scale
    sc = _bmm_nt(qh, khc) * scale
    qi = lax.broadcasted_iota(jnp.int32, (blk, blk), 0)
    kj = lax.broadcasted_iota(jnp.int32, (blk, blk), 1)
    sp = jnp.where((kj >= qi) & has_prev, sp, NEG_INF)
    sc = jnp.where(kj <= qi, sc, NEG_INF)
    m = lax.stop_gradient(jnp.maximum(jnp.max(sp, axis=-1, keepdims=True), jnp.max(sc, axis=-1, keepdims=True)))
    pp, pc = jnp.exp(sp - m), jnp.exp(sc - m)
    den = jnp.sum(pp, axis=-1, keepdims=True) + jnp.sum(pc, axis=-1, keepdims=True)
    o = (_bmm_nn(pp, vp) + _bmm_nn(pc, vc)) / den
    return o, m + jnp.log(den)


def _att_specs(g, length, gt):
    blk = ATT_BLOCK
    cur = pl.BlockSpec((gt, blk, HEAD_DIM), lambda gi, n: (gi, n, 0))
    prev = pl.BlockSpec((gt, blk, HEAD_DIM), lambda gi, n: (gi, jnp.maximum(n - 1, 0), 0))
    gain = pl.BlockSpec((1, 1, HEAD_DIM), lambda gi, n: (0, 0, 0))
    col = pl.BlockSpec((gt, blk, 1), lambda gi, n: (gi, n, 0))
    return cur, prev, gain, col


def _att_fwd(q, k, v, qn, kn, name):
    g, length, dh = q.shape
    gt = g // 8
    cur, prev, gain, col = _att_specs(g, length, gt)

    def body(q_ref, kp_ref, kc_ref, vp_ref, vc_ref, qn_ref, kn_ref, o_ref, lse_ref):
        o, lse = _att_block(q_ref[...], kp_ref[...], kc_ref[...], vp_ref[...], vc_ref[...],
                            qn_ref[...], kn_ref[...], pl.program_id(1) > 0)
        o_ref[...] = o
        lse_ref[...] = lse

    return pl.pallas_call(
        body, name=name, grid=(g // gt, length // ATT_BLOCK),
        out_shape=(jax.ShapeDtypeStruct(q.shape, F32), jax.ShapeDtypeStruct((g, length, 1), F32)),
        in_specs=[cur, prev, cur, prev, cur, gain, gain], out_specs=(cur, col),
        compiler_params=_params("arbitrary", "arbitrary"),
    )(q, k, k, v, v, qn, kn)


def _att_bwd(q, k, v, qn, kn, do, dlse, dqn0, dkn0, name):
    g, length, dh = q.shape
    gt = g // 8
    blk = ATT_BLOCK
    cur, prev, gain, col = _att_specs(g, length, gt)
    whole = pl.BlockSpec((gt, length, dh), lambda gi, n: (gi, 0, 0))

    def body(q_ref, kp_ref, kc_ref, vp_ref, vc_ref, qn_ref, kn_ref, do_ref, dl_ref, dqn0_ref, dkn0_ref,
             dq_ref, dk_ref, dv_ref, dqn_ref, dkn_ref):
        gi, n = pl.program_id(0), pl.program_id(1)
        has_prev = n > 0
        fn = functools.partial(_att_block, has_prev=has_prev)
        _, vjp = jax.vjp(fn, q_ref[...], kp_ref[...], kc_ref[...], vp_ref[...], vc_ref[...], qn_ref[...], kn_ref[...])
        dq, dkp, dkc, dvp, dvc, dqn, dkn = vjp((do_ref[...], dl_ref[...]))
        dq_ref[...] = dq
        here = pl.ds(pl.multiple_of(n * blk, blk), blk)
        dk_ref[:, here, :] = dkc
        dv_ref[:, here, :] = dvc

        @pl.when(has_prev)
        def _():
            before = pl.ds(pl.multiple_of((n - 1) * blk, blk), blk)
            dk_ref[:, before, :] += dkp
            dv_ref[:, before, :] += dvp

        @pl.when((gi == 0) & (n == 0))
        def _():
            dqn_ref[...] = dqn0_ref[...]
            dkn_ref[...] = dkn0_ref[...]

        dqn_ref[...] += dqn
        dkn_ref[...] += dkn

    gshape = jax.ShapeDtypeStruct((1, 1, dh), F32)
    return pl.pallas_call(
        body, name=name, grid=(g // gt, length // blk),
        out_shape=(jax.ShapeDtypeStruct(q.shape, F32),) * 3 + (gshape, gshape),
        in_specs=[cur, prev, cur, prev, cur, gain, gain, cur, col, gain, gain],
        out_specs=(cur, whole, whole, gain, gain),
        compiler_params=_params("arbitrary", "arbitrary"),
    )(q, k, k, v, v, qn, kn, do, dlse, dqn0, dkn0)


def _merge_fn(o1, o2, o3, l1, l2, l3):
    m = lax.stop_gradient(jnp.maximum(jnp.maximum(l1, l2), l3))
    e1, e2, e3 = jnp.exp(l1 - m), jnp.exp(l2 - m), jnp.exp(l3 - m)
    return (e1 * o1 + e2 * o2 + e3 * o3) / (e1 + e2 + e3)


def _merge_specs(h, tm, dh):
    return pl.BlockSpec((h, tm, dh), lambda i: (0, i, 0)), pl.BlockSpec((h, tm, 1), lambda i: (0, i, 0))


MERGE_TILE = 128


def _merge_fwd(os, ls):
    h, t, dh = os[0].shape
    tm = MERGE_TILE
    wide, col = _merge_specs(h, tm, dh)

    def body(o1, o2, o3, l1, l2, l3, out):
        out[...] = _merge_fn(o1[...], o2[...], o3[...], l1[...], l2[...], l3[...])

    return pl.pallas_call(
        body, name="merge_fwd", grid=(t // tm,), out_shape=jax.ShapeDtypeStruct(os[0].shape, F32),
        in_specs=[wide] * 3 + [col] * 3, out_specs=wide, compiler_params=_params("arbitrary"),
    )(*os, *ls)


def _merge_bwd(os, ls, do):
    h, t, dh = os[0].shape
    tm = MERGE_TILE
    wide, col = _merge_specs(h, tm, dh)

    def body(o1, o2, o3, l1, l2, l3, do_ref, d1, d2, d3, e1, e2, e3):
        _, vjp = jax.vjp(_merge_fn, o1[...], o2[...], o3[...], l1[...], l2[...], l3[...])
        outs = vjp(do_ref[...])
        for ref, val in zip((d1, d2, d3, e1, e2, e3), outs):
            ref[...] = val

    oshape = jax.ShapeDtypeStruct(os[0].shape, F32)
    lshape = jax.ShapeDtypeStruct(ls[0].shape, F32)
    return pl.pallas_call(
        body, name="merge_bwd", grid=(t // tm,), out_shape=(oshape,) * 3 + (lshape,) * 3,
        in_specs=[wide] * 3 + [col] * 3 + [wide], out_specs=(wide,) * 3 + (col,) * 3,
        compiler_params=_params("arbitrary"),
    )(*os, *ls, do)


RWKV_VEC = ("mu_r", "mu_k", "mu_v", "mu_w", "mu_a", "mu_g", "w0", "a0", "k_k", "k_a")
RWKV_MAT = ("w1", "w2", "a1", "a2", "g1", "g2")


def _rwkv_pre_fn(cur, prev, vec, w1, w2, a1, a2, g1, g2):
    c = cur.shape[1] // 4
    mu_r, mu_k, mu_v, mu_w, mu_a, mu_g, w0, a0, k_k, k_a = (vec[j:j + 1] for j in range(10))

    def lerp(j, mu):
        xc, xp = cur[:, j * c:(j + 1) * c], prev[:, j * c:(j + 1) * c]
        return xc + (xp - xc) * mu

    r, k, v = lerp(0, mu_r), lerp(1, mu_k), lerp(2, mu_v)
    cw, ca, cg = lerp(3, mu_w), lerp(3, mu_a), lerp(3, mu_g)
    z = w0 + _mm(jnp.tanh(_mm(cw, w1)), w2)
    w_log = jnp.minimum(z, 0.0) - jnp.log(1.0 + jnp.exp(-jnp.abs(z))) - 0.5
    lw = -jnp.exp(w_log)
    a = _sigmoid(a0 + _mm(_mm(ca, a1), a2))
    gate = _mm(_sigmoid(_mm(cg, g1)), g2)
    kkraw = k * k_k
    kmod = k * (1.0 + (a - 1.0) * k_a)
    return r, lw, kmod, v, kkraw, a, gate


def _rwkv_pre_specs(c, mats):
    tm = TOKEN_TILE
    wide = pl.BlockSpec((tm, 4 * c), lambda i: (i, 0))
    one = pl.BlockSpec((tm, c), lambda i: (i, 0))
    vec = pl.BlockSpec((10, c), lambda i: (0, 0))
    mspecs = [pl.BlockSpec(m.shape, lambda i: (0, 0)) for m in mats]
    return wide, one, vec, mspecs


def _rwkv_pre_fwd(cur, prev, vec, mats):
    t, c4 = cur.shape
    c = c4 // 4
    wide, one, vspec, mspecs = _rwkv_pre_specs(c, mats)

    def body(cur_ref, prev_ref, vec_ref, *rest):
        mrefs, outs = rest[:6], rest[6:]
        vals = _rwkv_pre_fn(cur_ref[...], prev_ref[...], vec_ref[...], *(m[...] for m in mrefs))
        for ref, val in zip(outs, vals):
            ref[...] = val

    return pl.pallas_call(
        body, name="rwkv_pre_fwd", grid=(t // TOKEN_TILE,), out_shape=(jax.ShapeDtypeStruct((t, c), F32),) * 7,
        in_specs=[wide, wide, vspec] + mspecs, out_specs=(one,) * 7, compiler_params=_params("arbitrary"),
    )(cur, prev, vec, *mats)


def _rwkv_pre_bwd(cur, prev, vec, mats, cts):
    t, c4 = cur.shape
    c = c4 // 4
    wide, one, vspec, mspecs = _rwkv_pre_specs(c, mats)

    def body(cur_ref, prev_ref, vec_ref, *rest):
        mrefs, ctrefs, outs = rest[:6], rest[6:13], rest[13:]
        _, vjp = jax.vjp(_rwkv_pre_fn, cur_ref[...], prev_ref[...], vec_ref[...], *(m[...] for m in mrefs))
        grads = vjp(tuple(r[...] for r in ctrefs))
        outs[0][...] = grads[0]
        outs[1][...] = grads[1]
        first = pl.program_id(0) == 0

        @pl.when(first)
        def _():
            for ref, val in zip(outs[2:], grads[2:]):
                ref[...] = val

        @pl.when(jnp.logical_not(first))
        def _():
            for ref, val in zip(outs[2:], grads[2:]):
                ref[...] += val

    wshape = jax.ShapeDtypeStruct(cur.shape, F32)
    return pl.pallas_call(
        body, name="rwkv_pre_bwd", grid=(t // TOKEN_TILE,),
        out_shape=(wshape, wshape, jax.ShapeDtypeStruct(vec.shape, F32)) + tuple(jax.ShapeDtypeStruct(m.shape, F32) for m in mats),
        in_specs=[wide, wide, vspec] + mspecs + [one] * 7, out_specs=(wide, wide, vspec) + tuple(mspecs),
        compiler_params=_params("arbitrary"),
    )(cur, prev, vec, *mats, *cts)


def _scan_chunk_fn(h0, r, lw, k, v, kkraw, a, rk, lnw, lnb):
    n = r.shape[1]
    nrm = jnp.sqrt(jnp.sum(kkraw * kkraw, axis=-1, keepdims=True))
    kk = kkraw / jnp.maximum(nrm, 1e-12)
    av, bv = -kk, kk * a
    ti = lax.broadcasted_iota(jnp.int32, (n, n), 0)
    si = lax.broadcasted_iota(jnp.int32, (n, n), 1)
    incl, strict = ti >= si, ti > si
    ones = jnp.broadcast_to(incl.astype(F32)[None], (r.shape[0], n, n))
    cum = _hdot(ones, lw, 2, 1)
    at, rt = av * jnp.exp(cum - lw), r * jnp.exp(cum)
    inv = jnp.exp(-cum)
    bt, kt = bv * inv, k * inv
    lab = jnp.where(strict, _hdot(at, bt, 2, 2), 0.0)
    lak = jnp.where(strict, _hdot(at, kt, 2, 2), 0.0)
    rb = jnp.where(incl, _hdot(rt, bt, 2, 2), 0.0)
    rkm = jnp.where(incl, _hdot(rt, kt, 2, 2), 0.0)
    u = _hdot(at, h0, 2, 1) + _hdot(lak, v, 2, 1)
    p = lab
    m = 1
    while m < n:
        u = u + _hdot(p, u, 2, 1)
        m *= 2
        if m < n:
            p = _hdot(p, p, 2, 1)
    y = _hdot(rt, h0, 2, 1) + _hdot(rb, u, 2, 1) + _hdot(rkm, v, 2, 1)
    last = jnp.exp(jnp.sum(lw, axis=1, keepdims=True))
    h1 = jnp.swapaxes(last, 1, 2) * (h0 + _hdot(bt, u, 1, 1) + _hdot(kt, v, 1, 1))
    mean = jnp.mean(y, axis=-1, keepdims=True)
    yc = y - mean
    var = jnp.mean(yc * yc, axis=-1, keepdims=True)
    yn = yc * lax.rsqrt(var + GN_EPS) * lnw + lnb
    bonus = jnp.sum(r * k * rk, axis=-1, keepdims=True) * v
    return yn + bonus, h1


def _scan_specs(h, t, dh, rev):
    n = SCAN_CHUNK
    nc = t // n
    pos = (lambda c: (0, nc - 1 - c, 0)) if rev else (lambda c: (0, c, 0))
    st = (lambda c: (nc - 1 - c, 0, 0, 0)) if rev else (lambda c: (c, 0, 0, 0))
    seq = pl.BlockSpec((h, n, dh), pos)
    par = pl.BlockSpec((h, 1, dh), lambda c: (0, 0, 0))
    state = pl.BlockSpec((1, h, dh, dh), st)
    return seq, par, state


def _scan_fwd(seqs, pars):
    h, t, dh = seqs[0].shape
    nc = t // SCAN_CHUNK
    seq, par, state = _scan_specs(h, t, dh, False)

    def body(r, lw, k, v, kkraw, a, rk, lnw, lnb, o_ref, st_ref, h_ref):
        @pl.when(pl.program_id(0) == 0)
        def _():
            h_ref[...] = jnp.zeros_like(h_ref)

        h0 = h_ref[...]
        st_ref[0] = h0
        o, h1 = _scan_chunk_fn(h0, r[...], lw[...], k[...], v[...], kkraw[...], a[...], rk[...], lnw[...], lnb[...])
        o_ref[...] = o
        h_ref[...] = h1

    return pl.pallas_call(
        body, name="rwkv_scan_fwd", grid=(nc,),
        out_shape=(jax.ShapeDtypeStruct((h, t, dh), F32), jax.ShapeDtypeStruct((nc, h, dh, dh), F32)),
        in_specs=[seq] * 6 + [par] * 3, out_specs=(seq, state),
        scratch_shapes=[pltpu.VMEM((h, dh, dh), F32)], compiler_params=_params("arbitrary"),
    )(*seqs, *pars)


def _scan_bwd(seqs, pars, states, do):
    h, t, dh = seqs[0].shape
    nc = t // SCAN_CHUNK
    seq, par, state = _scan_specs(h, t, dh, True)

    def body(r, lw, k, v, kkraw, a, rk, lnw, lnb, st_ref, do_ref, *rest):
        douts, dpars, dh_ref = rest[:6], rest[6:9], rest[9]
        first = pl.program_id(0) == 0

        @pl.when(first)
        def _():
            dh_ref[...] = jnp.zeros_like(dh_ref)

        _, vjp = jax.vjp(_scan_chunk_fn, st_ref[0], r[...], lw[...], k[...], v[...], kkraw[...], a[...],
                         rk[...], lnw[...], lnb[...])
        grads = vjp((do_ref[...], dh_ref[...]))
        dh_ref[...] = grads[0]
        for ref, val in zip(douts, grads[1:7]):
            ref[...] = val

        @pl.when(first)
        def _():
            for ref, val in zip(dpars, grads[7:]):
                ref[...] = val

        @pl.when(jnp.logical_not(first))
        def _():
            for ref, val in zip(dpars, grads[7:]):
                ref[...] += val

    sshape = jax.ShapeDtypeStruct((h, t, dh), F32)
    pshape = jax.ShapeDtypeStruct((h, 1, dh), F32)
    return pl.pallas_call(
        body, name="rwkv_scan_bwd", grid=(nc,), out_shape=(sshape,) * 6 + (pshape,) * 3,
        in_specs=[seq] * 6 + [par] * 3 + [state, seq], out_specs=(seq,) * 6 + (par,) * 3,
        scratch_shapes=[pltpu.VMEM((h, dh, dh), F32)], compiler_params=_params("arbitrary"),
    )(*seqs, *pars, states, do)


def _heads(x):
    return x.reshape(x.shape[0], -1, HEAD_DIM).transpose(1, 0, 2)


def _unheads(x):
    return x.transpose(1, 0, 2).reshape(x.shape[1], -1)


def _to_sub(x, dil):
    h, t, d = x.shape
    return x.reshape(h, t // dil, dil, d).transpose(0, 2, 1, 3).reshape(h * dil, t // dil, d)


def _from_sub(x, dil):
    g, length, d = x.shape
    return x.reshape(g // dil, dil, length, d).transpose(0, 2, 1, 3).reshape(g // dil, length * dil, d)


def _local_step(x, target, w, ex):
    w = dict(w)
    c = w["mu_r"].shape[-1]
    qn, kn = w["q_norm"].reshape(1, 1, HEAD_DIM), w["k_norm"].reshape(1, 1, HEAD_DIM)
    vec = jnp.concatenate([w[n].reshape(1, c) for n in RWKV_VEC], axis=0)
    pars = [w[n].reshape(-1, 1, HEAD_DIM) for n in ("r_k", "ln_x_w", "ln_x_b")]
    no_dep = jnp.zeros(DEP_SHAPE, F32)

    x1 = _ffn_fwd(x, w["ffn1_norm"], w["ffn1_w_gate"], w["ffn1_w_up"], w["ffn1_w_down"], ex.first_dep, "ffn1_fwd")
    w.update(ex.mix_weights(x1))
    att_w = w["w_in"].shape[0] * w["w_in"].shape[2] - 4 * c
    mats = [w[n] for n in RWKV_MAT]
    proj = _proj_fwd(x1, w["mix_norm"], w["w_in"])
    hw = att_w // 3
    qkv = [_heads(proj[:, j * hw:(j + 1) * hw]) for j in range(3)]
    subs = [[_to_sub(a, dil) for a in qkv] for dil in DILATIONS]
    outs = [_att_fwd(*s, qn, kn, f"att_fwd_d{dil}") for s, dil in zip(subs, DILATIONS)]
    os_ = [_from_sub(o, dil) for (o, _), dil in zip(outs, DILATIONS)]
    ls_ = [_from_sub(l, dil) for (_, l), dil in zip(outs, DILATIONS)]
    att = _unheads(_merge_fwd(os_, ls_))
    cur = proj[:, att_w:]
    prev = jnp.concatenate([jnp.zeros_like(cur[:1]), cur[:-1]], axis=0)
    pre = _rwkv_pre_fwd(cur, prev, vec, mats)
    seqs = [_heads(a) for a in pre[:6]]
    gate = pre[6]
    opg_h, states = _scan_fwd(seqs, pars)
    opg = _unheads(opg_h)
    w.update(ex.out_weights(opg))
    x2 = _mixout_fwd(x1, att, opg, gate, w["w_out"])
    x3 = _ffn_fwd(x2, w["ffn2_norm"], w["ffn2_w_gate"], w["ffn2_w_up"], w["ffn2_w_down"], no_dep, "ffn2_fwd")
    dy, loss = _loss_head(x3, target)

    g = {}
    dx2, g["ffn2_norm"], g["ffn2_w_gate"], g["ffn2_w_up"], g["ffn2_w_down"] = _ffn_bwd(
        x2, w["ffn2_norm"], w["ffn2_w_gate"], w["ffn2_w_up"], w["ffn2_w_down"], dy, no_dep, "ffn2_bwd")
    dep = ex.send_ffn2({n: g[n] for n in ("ffn2_w_gate", "ffn2_w_up", "ffn2_w_down")})
    datt, dopg, dgate, g["w_out"] = _mixout_bwd(att, opg, gate, w["w_out"], dx2, dep)
    dscan = _scan_bwd(seqs, pars, states, _heads(dopg))
    for n, d in zip(("r_k", "ln_x_w", "ln_x_b"), dscan[6:]):
        g[n] = d
    dpre = _rwkv_pre_bwd(cur, prev, vec, mats, [_unheads(d) for d in dscan[:6]] + [dgate])
    dcur, dprev, dvec = dpre[:3]
    for n, d in zip(RWKV_MAT, dpre[3:]):
        g[n] = d
    for j, n in enumerate(RWKV_VEC):
        g[n] = dvec[j:j + 1]
    dmerge = _merge_bwd(os_, ls_, _heads(datt))
    dqn = dkn = jnp.zeros((1, 1, HEAD_DIM), F32)
    dqkv = []
    for j, dil in enumerate(DILATIONS):
        dq, dk, dv, dqn, dkn = _att_bwd(*subs[j], qn, kn, _to_sub(dmerge[j], dil), _to_sub(dmerge[3 + j], dil),
                                        dqn, dkn, f"att_bwd_d{dil}")
        dqkv.append([_unheads(_from_sub(a, dil)) for a in (dq, dk, dv)])
    g["q_norm"], g["k_norm"] = dqn, dkn
    dshift = jnp.concatenate([dprev[1:], jnp.zeros_like(dprev[:1])], axis=0)
    dps = [jnp.concatenate(dqkv[j] + [tail], axis=1) for j, tail in enumerate((dcur, dshift, jnp.zeros_like(dcur)))]
    dx1, g["mix_norm"], g["w_in"] = _proj_bwd(x1, w["mix_norm"], w["w_in"], *dps, dx2)
    dep = ex.send_mix({n: g[n] for n in ("w_in", "w_out") + RWKV_MAT}, dx1)
    dx, g["ffn1_norm"], g["ffn1_w_gate"], g["ffn1_w_up"], g["ffn1_w_down"] = _ffn_bwd(
        x, w["ffn1_norm"], w["ffn1_w_gate"], w["ffn1_w_up"], w["ffn1_w_down"], dx1, dep, "ffn1_bwd")
    return loss, dx, g


N_SHARDS = 4


def _place():
    return lax.axis_index("x"), lax.axis_index("y"), lax.axis_index("c")


def _chip_peers(x, y):
    return [(1 - x, y), (x, 1 - y), (1 - x, 1 - y)]


def _gather_xy(shards):
    n = len(shards)

    def body(*refs):
        ins, outs = refs[:n], refs[n:2 * n]
        send_sems, recv_sems, local_sems = refs[2 * n:]
        x, y, c = _place()
        me = 2 * x + y
        copies = []
        for i in range(n):
            own = pltpu.make_async_copy(ins[i], outs[i].at[me], local_sems.at[i])
            own.start()
            copies.append(own)
            for k, (px, py) in enumerate(_chip_peers(x, y)):
                cp = pltpu.make_async_remote_copy(
                    src_ref=ins[i], dst_ref=outs[i].at[me], send_sem=send_sems.at[i, k], recv_sem=recv_sems.at[i, k],
                    device_id=(px, py, c), device_id_type=MESH)
                cp.start()
                copies.append(cp)
        for cp in copies:
            cp.wait()

    return pl.pallas_call(
        body, name="gather_weights",
        out_shape=tuple(jax.ShapeDtypeStruct((N_SHARDS,) + s.shape, s.dtype) for s in shards),
        in_specs=[ANY] * n, out_specs=(ANY,) * n,
        scratch_shapes=[pltpu.SemaphoreType.DMA((n, 3)), pltpu.SemaphoreType.DMA((n, 3)), pltpu.SemaphoreType.DMA((n,))],
    )(*shards)


def _scatter_partials(parts):
    n = len(parts)

    def body(*refs):
        ins, outs = refs[:n], refs[n:2 * n]
        send_sems, recv_sems = refs[2 * n:]
        x, y, c = _place()
        copies = []
        for i in range(n):
            for k, (px, py) in enumerate(_chip_peers(x, y)):
                cp = pltpu.make_async_remote_copy(
                    src_ref=ins[i].at[2 * px + py], dst_ref=outs[i].at[k], send_sem=send_sems.at[i, k],
                    recv_sem=recv_sems.at[i, k], device_id=(px, py, c), device_id_type=MESH)
                cp.start()
                copies.append(cp)
        for cp in copies:
            cp.wait()

    return pl.pallas_call(
        body, name="scatter_partials",
        out_shape=tuple(jax.ShapeDtypeStruct((3,) + p.shape[1:], p.dtype) for p in parts),
        in_specs=[ANY] * n, out_specs=(ANY,) * n,
        scratch_shapes=[pltpu.SemaphoreType.DMA((n, 3)), pltpu.SemaphoreType.DMA((n, 3))],
    )(*parts)


HBM = pl.BlockSpec(memory_space=pltpu.HBM)
SEM = pl.BlockSpec(memory_space=pltpu.SEMAPHORE)
DEP_SHAPE = (8, 128)


def _gather_views(i, srcs, lands, k, px, py, me):
    return (srcs[i], lands[i].at[me]), (srcs[i], lands[i].at[2 * px + py])


def _scatter_views(i, srcs, lands, k, px, py, me):
    return (srcs[i].at[2 * px + py], lands[i].at[k]), (srcs[i].at[me], lands[i].at[k])


def _push_start(srcs, land_shapes, views, own_slot, after, name):
    n = len(srcs)

    def body(*refs):
        src_refs, land_refs = refs[:n], refs[n:2 * n]
        send_sems, recv_sems = refs[2 * n + 1:2 * n + 3]
        token, local_sems = refs[4 * n + 3], refs[4 * n + 4]
        x, y, c = _place()
        me = 2 * x + y
        for i in range(n):
            for k, (px, py) in enumerate(_chip_peers(x, y)):
                (src, dst), _ = views(i, src_refs, land_refs, k, px, py, me)
                pltpu.make_async_remote_copy(
                    src_ref=src, dst_ref=dst, send_sem=send_sems.at[3 * i + k], recv_sem=recv_sems.at[3 * i + k],
                    device_id=(px, py, c), device_id_type=MESH).start()
        if own_slot:
            own = [pltpu.make_async_copy(src_refs[i], land_refs[i].at[me], local_sems.at[i]) for i in range(n)]
            for cp in own:
                cp.start()
            for cp in own:
                cp.wait()
        token[...] = jnp.zeros_like(token)

    sems = pltpu.SemaphoreType.DMA((3 * n,))
    lands = [pltpu.with_memory_space_constraint(lax.empty(s.shape, s.dtype), pltpu.HBM) for s in land_shapes]
    srcs = [pltpu.with_memory_space_constraint(s, pltpu.HBM) for s in srcs]
    outs = pl.pallas_call(
        body, name=name,
        out_shape=(sems, sems, *[pltpu.HBM(s.shape, s.dtype) for s in srcs], *[pltpu.HBM(s.shape, s.dtype) for s in land_shapes],
                   jax.ShapeDtypeStruct(DEP_SHAPE, F32)),
        in_specs=[HBM] * (2 * n) + [ANY], out_specs=(SEM, SEM, *[HBM] * (2 * n), VMEM_FULL),
        input_output_aliases={i: 2 + i for i in range(2 * n)},
        scratch_shapes=[pltpu.SemaphoreType.DMA((n,))],
        compiler_params=pltpu.CompilerParams(has_side_effects=pltpu.SideEffectType.DATAFLOW_SIDE_EFFECTING),
    )(*srcs, *lands, after)
    return outs[0], outs[1], outs[2:2 + n], outs[2 + n:2 + 2 * n], outs[2 + 2 * n]


def _push_wait(started, views, after, name):
    send_sems, recv_sems, srcs, lands, _ = started
    n = len(srcs)

    def body(*refs):
        src_refs, land_refs = refs[:n], refs[n:2 * n]
        send_sems, recv_sems = refs[2 * n:2 * n + 2]
        x, y, c = _place()
        me = 2 * x + y
        for i in range(n):
            for k, (px, py) in enumerate(_chip_peers(x, y)):
                _, (src, dst) = views(i, src_refs, land_refs, k, px, py, me)
                landing = pltpu.make_async_remote_copy(
                    src_ref=src, dst_ref=dst, send_sem=send_sems.at[3 * i + k], recv_sem=recv_sems.at[3 * i + k],
                    device_id=(px, py, c), device_id_type=MESH)
                landing.wait_send()
                landing.wait_recv()

    outs = pl.pallas_call(
        body, name=name,
        out_shape=tuple(pltpu.HBM(s.shape, s.dtype) for s in (*srcs, *lands)),
        in_specs=[HBM] * (2 * n) + [SEM, SEM, ANY], out_specs=(HBM,) * (2 * n),
        input_output_aliases={i: i for i in range(2 * n)},
        compiler_params=pltpu.CompilerParams(has_side_effects=pltpu.SideEffectType.DATAFLOW_SIDE_EFFECTING),
    )(*srcs, *lands, send_sems, recv_sems, after)
    return outs[n:]


def _sibling_swap(arrays):
    n = len(arrays)

    def body(*refs):
        ins, outs = refs[:n], refs[n:2 * n]
        send_sems, recv_sems = refs[2 * n:]
        x, y, c = _place()
        copies = []
        for i in range(n):
            cp = pltpu.make_async_remote_copy(
                src_ref=ins[i], dst_ref=outs[i], send_sem=send_sems.at[i], recv_sem=recv_sems.at[i],
                device_id=(x, y, 1 - c), device_id_type=MESH)
            cp.start()
            copies.append(cp)
        for cp in copies:
            cp.wait()

    return pl.pallas_call(
        body, name="sibling_swap",
        out_shape=tuple(jax.ShapeDtypeStruct(a.shape, a.dtype) for a in arrays),
        in_specs=[ANY] * n, out_specs=(ANY,) * n,
        scratch_shapes=[pltpu.SemaphoreType.DMA((n,)), pltpu.SemaphoreType.DMA((n,))],
    )(*arrays)


N_DEV = 8


def _allreduce_small(pack):
    def body(in_ref, out_ref, buf, send_sems, recv_sems):
        x, y, c = _place()
        me = 4 * x + 2 * y + c
        buf[me] = in_ref[...]

        def copy(j, slot):
            px, py, pc = x ^ (j >> 2), y ^ ((j >> 1) & 1), c ^ (j & 1)
            return pltpu.make_async_remote_copy(
                src_ref=in_ref, dst_ref=buf.at[slot(px, py, pc)], send_sem=send_sems.at[j], recv_sem=recv_sems.at[j],
                device_id=(px, py, pc), device_id_type=MESH)

        for j in range(1, N_DEV):
            copy(j, lambda px, py, pc: me).start()
        for j in range(1, N_DEV):
            landing = copy(j, lambda px, py, pc: 4 * px + 2 * py + pc)
            landing.wait_send()
            landing.wait_recv()
        acc = buf[0]
        for s in range(1, N_DEV):
            acc = acc + buf[s]
        out_ref[...] = acc

    return pl.pallas_call(
        body, name="allreduce_small", out_shape=jax.ShapeDtypeStruct(pack.shape, F32),
        in_specs=[VMEM_FULL], out_specs=VMEM_FULL,
        scratch_shapes=[pltpu.VMEM((N_DEV,) + pack.shape, F32), pltpu.SemaphoreType.DMA((N_DEV,)),
                        pltpu.SemaphoreType.DMA((N_DEV,))],
    )(pack)


ROW_TILE_MAX = 256
BF16_SUBLANES = 16


def _row_tile(rows):
    for tr in range(min(rows, ROW_TILE_MAX), 0, -1):
        if rows % tr == 0 and tr % BF16_SUBLANES == 0:
            return tr
    return rows


def _reduce_own(me, part, recv, name):
    _, r, cols = part.shape
    tr = _row_tile(r)

    def body(me_ref, p_ref, rv_ref, o_ref):
        acc = p_ref[0]
        for k in range(3):
            acc = acc + rv_ref[k].astype(F32)
        o_ref[...] = acc

    return pl.pallas_call(
        body, name=name, out_shape=jax.ShapeDtypeStruct((r, cols), F32),
        grid_spec=pltpu.PrefetchScalarGridSpec(
            num_scalar_prefetch=1, grid=(r // tr,),
            in_specs=[pl.BlockSpec((1, tr, cols), lambda i, me_ref: (me_ref[0], i, 0)),
                      pl.BlockSpec((3, tr, cols), lambda i, me_ref: (0, i, 0))],
            out_specs=pl.BlockSpec((tr, cols), lambda i, me_ref: (i, 0))),
        compiler_params=_params("arbitrary"),
    )(me, part, recv)


def _adamw(w, ga, gb, m, v, name):
    r, cols = w.shape
    tr = _row_tile(r)
    c1 = 1.0 - ADAM_B1 ** ADAM_STEP
    c2 = 1.0 - ADAM_B2 ** ADAM_STEP

    def body(w_ref, ga_ref, gb_ref, m_ref, v_ref, g_out, d_out, m_out, v_out):
        g = ga_ref[...] + gb_ref[...]
        mn = ADAM_B1 * m_ref[...] + (1.0 - ADAM_B1) * g
        vn = ADAM_B2 * v_ref[...] + (1.0 - ADAM_B2) * (g * g)
        g_out[...] = g
        m_out[...] = mn
        v_out[...] = vn
        d_out[...] = -ADAM_LR * ((mn / c1) / (jnp.sqrt(vn / c2) + ADAM_EPS) + ADAM_WD * w_ref[...])

    tile = pl.BlockSpec((tr, cols), lambda i: (i, 0))
    shape = jax.ShapeDtypeStruct((r, cols), F32)
    return pl.pallas_call(
        body, name=name, grid=(r // tr,), out_shape=(shape,) * 4, in_specs=[tile] * 5, out_specs=(tile,) * 4,
        compiler_params=_params("arbitrary"),
    )(w, ga, gb, m, v)


PACK_COLS = 512


def _to_rows(a):
    flat = a.reshape(-1)
    pad = (-flat.shape[0]) % PACK_COLS
    return jnp.pad(flat, (0, pad)).reshape(-1, PACK_COLS)


def _pack(arrays, extra_rows=0):
    rows = [_to_rows(a) for a in arrays]
    n = sum(r.shape[0] for r in rows) + extra_rows
    pad = (-n) % 8
    return jnp.concatenate(rows + [jnp.zeros((extra_rows + pad, PACK_COLS), F32)], axis=0)


def _unpack(pack, like):
    out, at = [], 0
    for a in like:
        n = -(-a.size // PACK_COLS)
        out.append(pack[at:at + n].reshape(-1)[:a.size].reshape(a.shape))
        at += n
    return out


COL_SHARDED = ("ffn1_w_gate", "ffn1_w_up", "w_in", "ffn2_w_gate", "ffn2_w_up", "w2", "a2", "g2")
ROW_SHARDED = ("ffn1_w_down", "ffn2_w_down", "w_out", "w1", "a1", "g1")
CHUNKED = ("ffn1_w_gate", "ffn1_w_up", "ffn1_w_down", "w_in", "ffn2_w_gate", "ffn2_w_up", "ffn2_w_down")
WEIGHTS = ("ffn1_norm", "ffn1_w_gate", "ffn1_w_up", "ffn1_w_down", "mix_norm", "w_in", "q_norm", "k_norm",
           "mu_r", "mu_k", "mu_v", "mu_w", "mu_a", "mu_g", "w0", "w1", "w2", "a0", "a1", "a2", "g1", "g2",
           "k_k", "k_a", "r_k", "ln_x_w", "ln_x_b", "w_out", "ffn2_norm", "ffn2_w_gate", "ffn2_w_up", "ffn2_w_down")


def _full_from_blocks(name, blocks):
    if name in CHUNKED:
        return blocks
    if name in ROW_SHARDED:
        return blocks.reshape(-1, blocks.shape[-1])
    return blocks.transpose(1, 0, 2).reshape(blocks.shape[1], -1)


def _blocks_from_full(name, full):
    if name in CHUNKED:
        return full
    if name in ROW_SHARDED:
        return full.reshape(N_SHARDS, -1, full.shape[-1])
    return full.reshape(full.shape[0], N_SHARDS, -1).transpose(1, 0, 2)


FFN1_GROUP = ("ffn1_w_gate", "ffn1_w_up", "ffn1_w_down")
MIX_GROUP = ("w_in",) + RWKV_MAT
OUT_GROUP = ("w_out", "ffn2_w_gate", "ffn2_w_up", "ffn2_w_down")
FFN2_GROUP = OUT_GROUP[1:]
LATE_GROUP = ("w_in", "w_out") + RWKV_MAT


class _Exchange:
    def __init__(self, given):
        self.given = given
        first = _gather_xy(self._shards(FFN1_GROUP))
        self.first_weights = self._full(FFN1_GROUP, first)
        self.mix = self._gather_start(MIX_GROUP, first[0], "gather_mix_start")
        self.out = self._gather_start(OUT_GROUP, self.mix[4], "gather_out_start")
        self.first_dep = self.out[4]
        self.parts, self.recv = {}, {}

    def _shards(self, names):
        return [self.given[n][0].astype(BF16) for n in names]

    @staticmethod
    def _full(names, blocks):
        out = {}
        for n, b in zip(names, blocks):
            full = _full_from_blocks(n, b)
            out[n] = full.astype(F32) if n in RWKV_MAT else full
        return out

    def _gather_start(self, names, after, name):
        shards = self._shards(names)
        lands = [jax.ShapeDtypeStruct((N_SHARDS,) + s.shape, s.dtype) for s in shards]
        return _push_start(shards, lands, _gather_views, True, after, name)

    def mix_weights(self, after):
        return self._full(MIX_GROUP, _push_wait(self.mix, _gather_views, after, "gather_mix_wait"))

    def out_weights(self, after):
        return self._full(OUT_GROUP, _push_wait(self.out, _gather_views, after, "gather_out_wait"))

    def _scatter_start(self, grads, name):
        names = tuple(grads)
        parts = [_blocks_from_full(n, grads[n]) for n in names]
        self.parts.update(zip(names, parts))
        lands = [jax.ShapeDtypeStruct((3,) + p.shape[1:], BF16) for p in parts]
        return _push_start([p.astype(BF16) for p in parts], lands, _scatter_views, False, parts[0], name)

    def send_ffn2(self, grads):
        self.ffn2 = self._scatter_start(grads, "scatter_ffn2_start")
        return self.ffn2[4]

    def send_mix(self, grads, after):
        self.recv.update(zip(FFN2_GROUP, _push_wait(self.ffn2, _scatter_views, after, "scatter_ffn2_wait")))
        self.late = self._scatter_start(grads, "scatter_late_start")
        return self.late[4]

    def finish(self, grads):
        names = tuple(grads)
        parts = [_blocks_from_full(n, grads[n]) for n in names]
        self.parts.update(zip(names, parts))
        got = _scatter_partials([p.astype(BF16) for p in parts])
        self.recv.update(zip(names, got))
        self.recv.update(zip(LATE_GROUP, _push_wait(self.late, _scatter_views, got[0], "scatter_late_wait")))
        return self.parts, self.recv


def kernel(
        x, ffn1_norm, ffn1_w_gate, ffn1_w_up, ffn1_w_down, mix_norm, w_in, q_norm, k_norm, mu_r, mu_k, mu_v, mu_w,
        mu_a, mu_g, w0, w1, w2, a0, a1, a2, g1, g2, k_k, k_a, r_k, ln_x_w, ln_x_b, w_out, ffn2_norm, ffn2_w_gate,
        ffn2_w_up, ffn2_w_down, loss_target, m_ffn1_norm, m_ffn1_w_gate, m_ffn1_w_up, m_ffn1_w_down, m_mix_norm,
        m_w_in, m_q_norm, m_k_norm, m_mu_r, m_mu_k, m_mu_v, m_mu_w, m_mu_a, m_mu_g, m_w0, m_w1, m_w2, m_a0, m_a1,
        m_a2, m_g1, m_g2, m_k_k, m_k_a, m_r_k, m_ln_x_w, m_ln_x_b, m_w_out, m_ffn2_norm, m_ffn2_w_gate, m_ffn2_w_up,
        m_ffn2_w_down, v_ffn1_norm, v_ffn1_w_gate, v_ffn1_w_up, v_ffn1_w_down, v_mix_norm, v_w_in, v_q_norm, v_k_norm,
        v_mu_r, v_mu_k, v_mu_v, v_mu_w, v_mu_a, v_mu_g, v_w0, v_w1, v_w2, v_a0, v_a1, v_a2, v_g1, v_g2, v_k_k, v_k_a,
        v_r_k, v_ln_x_w, v_ln_x_b, v_w_out, v_ffn2_norm, v_ffn2_w_gate, v_ffn2_w_up, v_ffn2_w_down):
    given = dict(locals())
    sharded = COL_SHARDED + ROW_SHARDED
    sharded = tuple(n for n in WEIGHTS if n in sharded)
    small = tuple(n for n in WEIGHTS if n not in sharded)

    ex = _Exchange(given)
    w = {n: given[n] for n in small}
    w.update(ex.first_weights)
    loss, dx, g = _local_step(x[0], loss_target[0], w, ex)
    parts, recv = ex.finish({n: g[n] for n in FFN1_GROUP})

    me = (2 * lax.axis_index("x") + lax.axis_index("y")).astype(jnp.int32).reshape(1)
    mine = []
    for n in sharded:
        p, rv = parts[n], recv[n]
        p2 = p.reshape(N_SHARDS, -1, p.shape[-1])
        mine.append(_reduce_own(me, p2, rv.reshape(3, -1, rv.shape[-1]), f"reduce_{n}"))
    theirs = _sibling_swap(mine)
    out = {}
    for n, a, b in zip(sharded, mine, theirs):
        shape = given[n].shape
        two_d = (-1, shape[-1])
        res = _adamw(given[n].reshape(two_d), a, b, given["m_" + n].reshape(two_d), given["v_" + n].reshape(two_d), f"adamw_{n}")
        out[n] = [r.reshape(shape) for r in res]

    gpack = _pack([g[n] for n in small], extra_rows=1)
    n_rows = sum(-(-given[n].size // PACK_COLS) for n in small)
    gpack = gpack.at[n_rows, :loss.shape[1]].set(loss[0])
    gsum = _allreduce_small(gpack)
    res = _adamw(_pack([given[n] for n in small], 1), gsum, jnp.zeros_like(gsum), _pack([given["m_" + n] for n in small], 1),
                 _pack([given["v_" + n] for n in small], 1), "adamw_small")
    like = [given[n] for n in small]
    for j, r in enumerate(res):
        for n, a in zip(small, _unpack(r, like)):
            out.setdefault(n, [None] * 4)[j] = a
    total_loss = gsum[n_rows, 0]
    return (total_loss, dx[None], *[out[n][0] for n in WEIGHTS], *[out[n][1] for n in WEIGHTS],
            *[out[n][2] for n in WEIGHTS], *[out[n][3] for n in WEIGHTS])
```

```python
import functools

import jax
import jax.numpy as jnp
from jax import lax
from jax.experimental import pallas as pl
from jax.experimental.pallas import tpu as pltpu

F32 = jnp.float32
BF16 = jnp.bfloat16
HIGHEST = lax.Precision.HIGHEST
MESH = pl.DeviceIdType.MESH

RMS_EPS = 1e-6
GN_EPS = 64e-5
NEG_INF = -1e30
FFN_RESIDUAL = 0.5
HEAD_DIM = 64
ATT_BLOCK = 128
DILATIONS = (1, 4, 16)
SCAN_CHUNK = 64
TOKEN_TILE = 256

ADAM_LR = 0.001
ADAM_B1 = 0.9
ADAM_B2 = 0.999
ADAM_EPS = 1e-08
ADAM_WD = 0.01
ADAM_STEP = 10

VMEM_FULL = pl.BlockSpec(memory_space=pltpu.VMEM)
ANY = pl.BlockSpec(memory_space=pl.ANY)


VMEM_LIMIT = 56 * 1024 * 1024


def _params(*sem):
    return pltpu.CompilerParams(dimension_semantics=sem, vmem_limit_bytes=VMEM_LIMIT)


def _dot(a, b, dims):
    return lax.dot_general(a.astype(BF16), b.astype(BF16), (dims, ((), ())), preferred_element_type=F32)


def _dot_nn(a, b):
    return _dot(a, b, ((1,), (0,)))


def _dot_nt(a, b):
    return _dot(a, b, ((1,), (1,)))


def _dot_tn(a, b):
    return _dot(a, b, ((0,), (0,)))


@jax.custom_vjp
def _mm(a, b):
    return _dot_nn(a, b)


def _mm_fwd(a, b):
    return _dot_nn(a, b), (a, b)


def _mm_bwd(res, g):
    a, b = res
    return _dot_nt(g, b).astype(a.dtype), _dot_tn(a, g).astype(b.dtype)


_mm.defvjp(_mm_fwd, _mm_bwd)


def _bdot(a, b, ca, cb):
    return lax.dot_general(a.astype(BF16), b.astype(BF16), (((ca,), (cb,)), ((0,), (0,))), preferred_element_type=F32)


@jax.custom_vjp
def _bmm_nt(a, b):
    return _bdot(a, b, 2, 2)


def _bmm_nt_fwd(a, b):
    return _bdot(a, b, 2, 2), (a, b)


def _bmm_nt_bwd(res, g):
    a, b = res
    return _bdot(g, b, 2, 1), _bdot(g, a, 1, 1)


_bmm_nt.defvjp(_bmm_nt_fwd, _bmm_nt_bwd)


@jax.custom_vjp
def _bmm_nn(a, b):
    return _bdot(a, b, 2, 1)


def _bmm_nn_fwd(a, b):
    return _bdot(a, b, 2, 1), (a, b)


def _bmm_nn_bwd(res, g):
    a, b = res
    return _bdot(g, b, 2, 2), _bdot(a, g, 1, 1)


_bmm_nn.defvjp(_bmm_nn_fwd, _bmm_nn_bwd)


def _hdot(a, b, ca, cb):
    return lax.dot_general(a, b, (((ca,), (cb,)), ((0,), (0,))), precision=lax.Precision.HIGH, preferred_element_type=F32)


def _sigmoid(x):
    return 1.0 / (1.0 + jnp.exp(-x))


def _rms(x):
    return lax.rsqrt(jnp.mean(x * x, axis=-1, keepdims=True) + RMS_EPS)


def _ffn_fwd(x, norm, wg, wu, wd, dep, name):
    t, d = x.shape
    nc = wg.shape[0]
    tm = TOKEN_TILE

    def body(x_ref, n_ref, wg_ref, wu_ref, wd_ref, dep_ref, o_ref):
        xv = x_ref[...]
        h = (xv * _rms(xv) * n_ref[...]).astype(BF16)
        acc = jnp.zeros((tm, d), F32)
        for c in range(nc):
            g = jnp.dot(h, wg_ref[c], preferred_element_type=F32)
            u = jnp.dot(h, wu_ref[c], preferred_element_type=F32)
            a = (g * _sigmoid(g) * u).astype(BF16)
            acc = acc + jnp.dot(a, wd_ref[c], preferred_element_type=F32)
        o_ref[...] = xv + FFN_RESIDUAL * acc

    tile = pl.BlockSpec((tm, d), lambda i: (i, 0))
    return pl.pallas_call(
        body, name=name, grid=(t // tm,), out_shape=jax.ShapeDtypeStruct((t, d), F32),
        in_specs=[tile, pl.BlockSpec((1, d), lambda i: (0, 0)), VMEM_FULL, VMEM_FULL, VMEM_FULL, ANY],
        out_specs=tile, compiler_params=_params("arbitrary"),
    )(x, norm, wg, wu, wd, dep)


def _rmsnorm_bwd(xv, gain, dh):
    rs = _rms(xv)
    xn = xv * rs
    dxn = dh * gain
    dx = rs * (dxn - xn * jnp.mean(dxn * xn, axis=-1, keepdims=True))
    return dx, jnp.sum(dh * xn, axis=0, keepdims=True)


def _ffn_bwd(x, norm, wg, wu, wd, dy, dep, name):
    t, d = x.shape
    nc, _, fc = wg.shape
    tm = TOKEN_TILE
    nt = t // tm

    def body(x_ref, n_ref, wg_ref, wu_ref, wd_ref, dy_ref, dep_ref, dx_ref, dn_ref, dwg_ref, dwu_ref, dwd_ref, dh_ref):
        c, i = pl.program_id(0), pl.program_id(1)
        rows = pl.ds(pl.multiple_of(i * tm, tm), tm)
        xv = x_ref[...]
        gain = n_ref[...]
        h = (xv * _rms(xv) * gain).astype(BF16)
        dy = dy_ref[...]
        dyb = (FFN_RESIDUAL * dy).astype(BF16)
        g = jnp.dot(h, wg_ref[0], preferred_element_type=F32)
        u = jnp.dot(h, wu_ref[0], preferred_element_type=F32)
        sg = _sigmoid(g)
        s = g * sg
        a = (s * u).astype(BF16)
        da = _dot_nt(dyb, wd_ref[0])
        dub = (da * s).astype(BF16)
        dgb = (da * u * (sg * (1.0 + g * (1.0 - sg)))).astype(BF16)
        dwd_c = _dot_tn(a, dyb)
        dwg_c = _dot_tn(h, dgb)
        dwu_c = _dot_tn(h, dub)
        dh_c = _dot_nt(dgb, wg_ref[0]) + _dot_nt(dub, wu_ref[0])

        @pl.when(i == 0)
        def _():
            dwd_ref[0] = dwd_c
            dwg_ref[0] = dwg_c
            dwu_ref[0] = dwu_c

        @pl.when(i > 0)
        def _():
            dwd_ref[0] += dwd_c
            dwg_ref[0] += dwg_c
            dwu_ref[0] += dwu_c

        @pl.when(c == 0)
        def _():
            dh_ref[rows, :] = dh_c

        @pl.when(c > 0)
        def _():
            dh_ref[rows, :] += dh_c

        @pl.when(c == nc - 1)
        def _():
            dx, dn = _rmsnorm_bwd(xv, gain, dh_ref[rows, :])
            dx_ref[...] = dx + dy

            @pl.when(i == 0)
            def _():
                dn_ref[...] = dn

            @pl.when(i > 0)
            def _():
                dn_ref[...] += dn

    tile = pl.BlockSpec((tm, d), lambda c, i: (i, 0))
    row = pl.BlockSpec((1, d), lambda c, i: (0, 0))
    wcol = pl.BlockSpec((1, d, fc), lambda c, i: (c, 0, 0))
    wrow = pl.BlockSpec((1, fc, d), lambda c, i: (c, 0, 0))
    last = pl.BlockSpec((tm, d), lambda c, i: (jnp.where(c == nc - 1, i, 0), 0))
    return pl.pallas_call(
        body, name=name, grid=(nc, nt),
        out_shape=(jax.ShapeDtypeStruct((t, d), F32), jax.ShapeDtypeStruct((1, d), F32),
                   jax.ShapeDtypeStruct(wg.shape, F32), jax.ShapeDtypeStruct(wu.shape, F32),
                   jax.ShapeDtypeStruct(wd.shape, F32)),
        in_specs=[tile, row, wcol, wcol, wrow, tile, ANY],
        out_specs=(last, row, wcol, wcol, wrow),
        scratch_shapes=[pltpu.VMEM((t, d), F32)],
        compiler_params=_params("arbitrary", "arbitrary"),
    )(x, norm, wg, wu, wd, dy, dep)


def _proj_fwd(x, norm, w):
    t, d = x.shape
    nc, _, ncol = w.shape
    tm = TOKEN_TILE

    def body(x_ref, n_ref, w_ref, o_ref):
        xv = x_ref[...]
        h = (xv * _rms(xv) * n_ref[...]).astype(BF16)
        for c in range(nc):
            o_ref[:, c * ncol:(c + 1) * ncol] = jnp.dot(h, w_ref[c], preferred_element_type=F32)

    return pl.pallas_call(
        body, name="proj_fwd", grid=(t // tm,), out_shape=jax.ShapeDtypeStruct((t, nc * ncol), F32),
        in_specs=[pl.BlockSpec((tm, d), lambda i: (i, 0)), pl.BlockSpec((1, d), lambda i: (0, 0)), VMEM_FULL],
        out_specs=pl.BlockSpec((tm, nc * ncol), lambda i: (i, 0)), compiler_params=_params("arbitrary"),
    )(x, norm, w)


def _proj_bwd(x, norm, w, dpa, dpb, dpc, dres):
    t, d = x.shape
    nc, _, ncol = w.shape
    tm = TOKEN_TILE
    nt = t // tm

    def body(x_ref, n_ref, w_ref, dpa_ref, dpb_ref, dpc_ref, dres_ref, dx_ref, dn_ref, dw_ref, dh_ref):
        c, i = pl.program_id(0), pl.program_id(1)
        rows = pl.ds(pl.multiple_of(i * tm, tm), tm)
        xv = x_ref[...]
        gain = n_ref[...]
        h = (xv * _rms(xv) * gain).astype(BF16)
        dpv = (dpa_ref[...] + dpb_ref[...] + dpc_ref[...]).astype(BF16)
        dw_c = _dot_tn(h, dpv)
        dh_c = _dot_nt(dpv, w_ref[0])

        @pl.when(i == 0)
        def _():
            dw_ref[0] = dw_c

        @pl.when(i > 0)
        def _():
            dw_ref[0] += dw_c

        @pl.when(c == 0)
        def _():
            dh_ref[rows, :] = dh_c

        @pl.when(c > 0)
        def _():
            dh_ref[rows, :] += dh_c

        @pl.when(c == nc - 1)
        def _():
            dx, dn = _rmsnorm_bwd(xv, gain, dh_ref[rows, :])
            dx_ref[...] = dx + dres_ref[...]

            @pl.when(i == 0)
            def _():
                dn_ref[...] = dn

            @pl.when(i > 0)
            def _():
                dn_ref[...] += dn

    tile = pl.BlockSpec((tm, d), lambda c, i: (i, 0))
    row = pl.BlockSpec((1, d), lambda c, i: (0, 0))
    wcol = pl.BlockSpec((1, d, ncol), lambda c, i: (c, 0, 0))
    ptile = pl.BlockSpec((tm, ncol), lambda c, i: (i, c))
    last = pl.BlockSpec((tm, d), lambda c, i: (jnp.where(c == nc - 1, i, 0), 0))
    return pl.pallas_call(
        body, name="proj_bwd", grid=(nc, nt),
        out_shape=(jax.ShapeDtypeStruct((t, d), F32), jax.ShapeDtypeStruct((1, d), F32),
                   jax.ShapeDtypeStruct(w.shape, F32)),
        in_specs=[tile, row, wcol, ptile, ptile, ptile, tile],
        out_specs=(last, row, wcol),
        scratch_shapes=[pltpu.VMEM((t, d), F32)],
        compiler_params=_params("arbitrary", "arbitrary"),
    )(x, norm, w, dpa, dpb, dpc, dres)


def _mixout_fwd(x, att, opg, gate, w):
    t, d = x.shape
    half = att.shape[1]
    tm = TOKEN_TILE

    def body(x_ref, att_ref, opg_ref, g_ref, w_ref, o_ref):
        mix = jnp.concatenate([att_ref[...], opg_ref[...] * g_ref[...]], axis=-1).astype(BF16)
        o_ref[...] = x_ref[...] + jnp.dot(mix, w_ref[...], preferred_element_type=F32)

    tile = pl.BlockSpec((tm, d), lambda i: (i, 0))
    htile = pl.BlockSpec((tm, half), lambda i: (i, 0))
    return pl.pallas_call(
        body, name="mixout_fwd", grid=(t // tm,), out_shape=jax.ShapeDtypeStruct((t, d), F32),
        in_specs=[tile, htile, htile, htile, VMEM_FULL], out_specs=tile, compiler_params=_params("arbitrary"),
    )(x, att, opg, gate, w)


def _mixout_bwd(att, opg, gate, w, dy, dep):
    t, half = att.shape
    d = dy.shape[1]
    tm = TOKEN_TILE

    def body(att_ref, opg_ref, g_ref, w_ref, dy_ref, dep_ref, datt_ref, dopg_ref, dg_ref, dw_ref):
        i = pl.program_id(0)
        opg_v, g_v = opg_ref[...], g_ref[...]
        mix = jnp.concatenate([att_ref[...], opg_v * g_v], axis=-1).astype(BF16)
        dyb = dy_ref[...].astype(BF16)
        dmix = _dot_nt(dyb, w_ref[...])
        dw = _dot_tn(mix, dyb)
        datt_ref[...] = dmix[:, :half]
        drw = dmix[:, half:]
        dopg_ref[...] = drw * g_v
        dg_ref[...] = drw * opg_v

        @pl.when(i == 0)
        def _():
            dw_ref[...] = dw

        @pl.when(i > 0)
        def _():
            dw_ref[...] += dw

    tile = pl.BlockSpec((tm, d), lambda i: (i, 0))
    htile = pl.BlockSpec((tm, half), lambda i: (i, 0))
    hshape = jax.ShapeDtypeStruct((t, half), F32)
    return pl.pallas_call(
        body, name="mixout_bwd", grid=(t // tm,),
        out_shape=(hshape, hshape, hshape, jax.ShapeDtypeStruct(w.shape, F32)),
        in_specs=[htile, htile, htile, VMEM_FULL, tile, ANY],
        out_specs=(htile, htile, htile, pl.BlockSpec(w.shape, lambda i: (0, 0))),
        compiler_params=_params("arbitrary"),
    )(att, opg, gate, w, dy, dep)


def _loss_head(y, target):
    t, d = y.shape
    tm = TOKEN_TILE

    def body(y_ref, t_ref, dy_ref, loss_ref):
        i = pl.program_id(0)
        err = y_ref[...] - t_ref[...]
        dy_ref[...] = err * (1.0 / d)
        part = 0.5 * jnp.sum(jnp.mean(err * err, axis=-1, keepdims=True), axis=0, keepdims=True)

        @pl.when(i == 0)
        def _():
            loss_ref[...] = jnp.zeros_like(loss_ref)

        loss_ref[...] += jnp.broadcast_to(part, loss_ref.shape)

    tile = pl.BlockSpec((tm, d), lambda i: (i, 0))
    return pl.pallas_call(
        body, name="loss_head", grid=(t // tm,),
        out_shape=(jax.ShapeDtypeStruct((t, d), F32), jax.ShapeDtypeStruct((1, 128), F32)),
        in_specs=[tile, tile], out_specs=(tile, pl.BlockSpec((1, 128), lambda i: (0, 0))),
        compiler_params=_params("arbitrary"),
    )(y, target)


def _att_block(q, kp, kc, vp, vc, qn, kn, has_prev):
    blk = q.shape[1]

    def hn(v, gain):
        return v * _rms(v) * gain

    qh, khp, khc = hn(q, qn), hn(kp, kn), hn(kc, kn)
    scale = HEAD_DIM ** -0.5
    sp = _bmm_nt(qh, khp) * scale
    sc = _bmm_nt(qh, khc) * scale
    qi = lax.broadcasted_iota(jnp.int32, (blk, blk), 0)
    kj = lax.broadcasted_iota(jnp.int32, (blk, blk), 1)
    sp = jnp.where((kj >= qi) & has_prev, sp, NEG_INF)
    sc = jnp.where(kj <= qi, sc, NEG_INF)
    m = lax.stop_gradient(jnp.maximum(jnp.max(sp, axis=-1, keepdims=True), jnp.max(sc, axis=-1, keepdims=True)))
    pp, pc = jnp.exp(sp - m), jnp.exp(sc - m)
    den = jnp.sum(pp, axis=-1, keepdims=True) + jnp.sum(pc, axis=-1, keepdims=True)
    o = (_bmm_nn(pp, vp) + _bmm_nn(pc, vc)) / den
    return o, m + jnp.log(den)


def _att_specs(g, length, gt):
    blk = ATT_BLOCK
    cur = pl.BlockSpec((gt, blk, HEAD_DIM), lambda gi, n: (gi, n, 0))
    prev = pl.BlockSpec((gt, blk, HEAD_DIM), lambda gi, n: (gi, jnp.maximum(n - 1, 0), 0))
    gain = pl.BlockSpec((1, 1, HEAD_DIM), lambda gi, n: (0, 0, 0))
    col = pl.BlockSpec((gt, blk, 1), lambda gi, n: (gi, n, 0))
    return cur, prev, gain, col


def _att_fwd(q, k, v, qn, kn, name):
    g, length, dh = q.shape
    gt = g // 8
    cur, prev, gain, col = _att_specs(g, length, gt)

    def body(q_ref, kp_ref, kc_ref, vp_ref, vc_ref, qn_ref, kn_ref, o_ref, lse_ref):
        o, lse = _att_block(q_ref[...], kp_ref[...], kc_ref[...], vp_ref[...], vc_ref[...],
                            qn_ref[...], kn_ref[...], pl.program_id(1) > 0)
        o_ref[...] = o
        lse_ref[...] = lse

    return pl.pallas_call(
        body, name=name, grid=(g // gt, length // ATT_BLOCK),
        out_shape=(jax.ShapeDtypeStruct(q.shape, F32), jax.ShapeDtypeStruct((g, length, 1), F32)),
        in_specs=[cur, prev, cur, prev, cur, gain, gain], out_specs=(cur, col),
        compiler_params=_params("arbitrary", "arbitrary"),
    )(q, k, k, v, v, qn, kn)


def _att_bwd(q, k, v, qn, kn, do, dlse, dqn0, dkn0, name):
    g, length, dh = q.shape
    gt = g // 8
    blk = ATT_BLOCK
    cur, prev, gain, col = _att_specs(g, length, gt)
    whole = pl.BlockSpec((gt, length, dh), lambda gi, n: (gi, 0, 0))

    def body(q_ref, kp_ref, kc_ref, vp_ref, vc_ref, qn_ref, kn_ref, do_ref, dl_ref, dqn0_ref, dkn0_ref,
             dq_ref, dk_ref, dv_ref, dqn_ref, dkn_ref):
        gi, n = pl.program_id(0), pl.program_id(1)
        has_prev = n > 0
        fn = functools.partial(_att_block, has_prev=has_prev)
        _, vjp = jax.vjp(fn, q_ref[...], kp_ref[...], kc_ref[...], vp_ref[...], vc_ref[...], qn_ref[...], kn_ref[...])
        dq, dkp, dkc, dvp, dvc, dqn, dkn = vjp((do_ref[...], dl_ref[...]))
        dq_ref[...] = dq
        here = pl.ds(pl.multiple_of(n * blk, blk), blk)
        dk_ref[:, here, :] = dkc
        dv_ref[:, here, :] = dvc

        @pl.when(has_prev)
        def _():
            before = pl.ds(pl.multiple_of((n - 1) * blk, blk), blk)
            dk_ref[:, before, :] += dkp
            dv_ref[:, before, :] += dvp

        @pl.when((gi == 0) & (n == 0))
        def _():
            dqn_ref[...] = dqn0_ref[...]
            dkn_ref[...] = dkn0_ref[...]

        dqn_ref[...] += dqn
        dkn_ref[...] += dkn

    gshape = jax.ShapeDtypeStruct((1, 1, dh), F32)
    return pl.pallas_call(
        body, name=name, grid=(g // gt, length // blk),
        out_shape=(jax.ShapeDtypeStruct(q.shape, F32),) * 3 + (gshape, gshape),
        in_specs=[cur, prev, cur, prev, cur, gain, gain, cur, col, gain, gain],
        out_specs=(cur, whole, whole, gain, gain),
        compiler_params=_params("arbitrary", "arbitrary"),
    )(q, k, k, v, v, qn, kn, do, dlse, dqn0, dkn0)


def _merge_fn(o1, o2, o3, l1, l2, l3):
    m = lax.stop_gradient(jnp.maximum(jnp.maximum(l1, l2), l3))
    e1, e2, e3 = jnp.exp(l1 - m), jnp.exp(l2 - m), jnp.exp(l3 - m)
    return (e1 * o1 + e2 * o2 + e3 * o3) / (e1 + e2 + e3)


def _merge_specs(h, tm, dh):
    return pl.BlockSpec((h, tm, dh), lambda i: (0, i, 0)), pl.BlockSpec((h, tm, 1), lambda i: (0, i, 0))


MERGE_TILE = 128


def _merge_fwd(os, ls):
    h, t, dh = os[0].shape
    tm = MERGE_TILE
    wide, col = _merge_specs(h, tm, dh)

    def body(o1, o2, o3, l1, l2, l3, out):
        out[...] = _merge_fn(o1[...], o2[...], o3[...], l1[...], l2[...], l3[...])

    return pl.pallas_call(
        body, name="merge_fwd", grid=(t // tm,), out_shape=jax.ShapeDtypeStruct(os[0].shape, F32),
        in_specs=[wide] * 3 + [col] * 3, out_specs=wide, compiler_params=_params("arbitrary"),
    )(*os, *ls)


def _merge_bwd(os, ls, do):
    h, t, dh = os[0].shape
    tm = MERGE_TILE
    wide, col = _merge_specs(h, tm, dh)

    def body(o1, o2, o3, l1, l2, l3, do_ref, d1, d2, d3, e1, e2, e3):
        _, vjp = jax.vjp(_merge_fn, o1[...], o2[...], o3[...], l1[...], l2[...], l3[...])
        outs = vjp(do_ref[...])
        for ref, val in zip((d1, d2, d3, e1, e2, e3), outs):
            ref[...] = val

    oshape = jax.ShapeDtypeStruct(os[0].shape, F32)
    lshape = jax.ShapeDtypeStruct(ls[0].shape, F32)
    return pl.pallas_call(
        body, name="merge_bwd", grid=(t // tm,), out_shape=(oshape,) * 3 + (lshape,) * 3,
        in_specs=[wide] * 3 + [col] * 3 + [wide], out_specs=(wide,) * 3 + (col,) * 3,
        compiler_params=_params("arbitrary"),
    )(*os, *ls, do)


RWKV_VEC = ("mu_r", "mu_k", "mu_v", "mu_w", "mu_a", "mu_g", "w0", "a0", "k_k", "k_a")
RWKV_MAT = ("w1", "w2", "a1", "a2", "g1", "g2")


def _rwkv_pre_fn(cur, prev, vec, w1, w2, a1, a2, g1, g2):
    c = cur.shape[1] // 4
    mu_r, mu_k, mu_v, mu_w, mu_a, mu_g, w0, a0, k_k, k_a = (vec[j:j + 1] for j in range(10))

    def lerp(j, mu):
        xc, xp = cur[:, j * c:(j + 1) * c], prev[:, j * c:(j + 1) * c]
        return xc + (xp - xc) * mu

    r, k, v = lerp(0, mu_r), lerp(1, mu_k), lerp(2, mu_v)
    cw, ca, cg = lerp(3, mu_w), lerp(3, mu_a), lerp(3, mu_g)
    z = w0 + _mm(jnp.tanh(_mm(cw, w1)), w2)
    w_log = jnp.minimum(z, 0.0) - jnp.log(1.0 + jnp.exp(-jnp.abs(z))) - 0.5
    lw = -jnp.exp(w_log)
    a = _sigmoid(a0 + _mm(_mm(ca, a1), a2))
    gate = _mm(_sigmoid(_mm(cg, g1)), g2)
    kkraw = k * k_k
    kmod = k * (1.0 + (a - 1.0) * k_a)
    return r, lw, kmod, v, kkraw, a, gate


def _rwkv_pre_specs(c, mats):
    tm = TOKEN_TILE
    wide = pl.BlockSpec((tm, 4 * c), lambda i: (i, 0))
    one = pl.BlockSpec((tm, c), lambda i: (i, 0))
    vec = pl.BlockSpec((10, c), lambda i: (0, 0))
    mspecs = [pl.BlockSpec(m.shape, lambda i: (0, 0)) for m in mats]
    return wide, one, vec, mspecs


def _rwkv_pre_fwd(cur, prev, vec, mats):
    t, c4 = cur.shape
    c = c4 // 4
    wide, one, vspec, mspecs = _rwkv_pre_specs(c, mats)

    def body(cur_ref, prev_ref, vec_ref, *rest):
        mrefs, outs = rest[:6], rest[6:]
        vals = _rwkv_pre_fn(cur_ref[...], prev_ref[...], vec_ref[...], *(m[...] for m in mrefs))
        for ref, val in zip(outs, vals):
            ref[...] = val

    return pl.pallas_call(
        body, name="rwkv_pre_fwd", grid=(t // TOKEN_TILE,), out_shape=(jax.ShapeDtypeStruct((t, c), F32),) * 7,
        in_specs=[wide, wide, vspec] + mspecs, out_specs=(one,) * 7, compiler_params=_params("arbitrary"),
    )(cur, prev, vec, *mats)


def _rwkv_pre_bwd(cur, prev, vec, mats, cts):
    t, c4 = cur.shape
    c = c4 // 4
    wide, one, vspec, mspecs = _rwkv_pre_specs(c, mats)

    def body(cur_ref, prev_ref, vec_ref, *rest):
        mrefs, ctrefs, outs = rest[:6], rest[6:13], rest[13:]
        _, vjp = jax.vjp(_rwkv_pre_fn, cur_ref[...], prev_ref[...], vec_ref[...], *(m[...] for m in mrefs))
        grads = vjp(tuple(r[...] for r in ctrefs))
        outs[0][...] = grads[0]
        outs[1][...] = grads[1]
        first = pl.program_id(0) == 0

        @pl.when(first)
        def _():
            for ref, val in zip(outs[2:], grads[2:]):
                ref[...] = val

        @pl.when(jnp.logical_not(first))
        def _():
            for ref, val in zip(outs[2:], grads[2:]):
                ref[...] += val

    wshape = jax.ShapeDtypeStruct(cur.shape, F32)
    return pl.pallas_call(
        body, name="rwkv_pre_bwd", grid=(t // TOKEN_TILE,),
        out_shape=(wshape, wshape, jax.ShapeDtypeStruct(vec.shape, F32)) + tuple(jax.ShapeDtypeStruct(m.shape, F32) for m in mats),
        in_specs=[wide, wide, vspec] + mspecs + [one] * 7, out_specs=(wide, wide, vspec) + tuple(mspecs),
        compiler_params=_params("arbitrary"),
    )(cur, prev, vec, *mats, *cts)


def _scan_chunk_fn(h0, r, lw, k, v, kkraw, a, rk, lnw, lnb):
    n = r.shape[1]
    nrm = jnp.sqrt(jnp.sum(kkraw * kkraw, axis=-1, keepdims=True))
    kk = kkraw / jnp.maximum(nrm, 1e-12)
    av, bv = -kk, kk * a
    ti = lax.broadcasted_iota(jnp.int32, (n, n), 0)
    si = lax.broadcasted_iota(jnp.int32, (n, n), 1)
    incl, strict = ti >= si, ti > si
    ones = jnp.broadcast_to(incl.astype(F32)[None], (r.shape[0], n, n))
    cum = _hdot(ones, lw, 2, 1)
    at, rt = av * jnp.exp(cum - lw), r * jnp.exp(cum)
    inv = jnp.exp(-cum)
    bt, kt = bv * inv, k * inv
    lab = jnp.where(strict, _hdot(at, bt, 2, 2), 0.0)
    lak = jnp.where(strict, _hdot(at, kt, 2, 2), 0.0)
    rb = jnp.where(incl, _hdot(rt, bt, 2, 2), 0.0)
    rkm = jnp.where(incl, _hdot(rt, kt, 2, 2), 0.0)
    u = _hdot(at, h0, 2, 1) + _hdot(lak, v, 2, 1)
    p = lab
    m = 1
    while m < n:
        u = u + _hdot(p, u, 2, 1)
        m *= 2
        if m < n:
            p = _hdot(p, p, 2, 1)
    y = _hdot(rt, h0, 2, 1) + _hdot(rb, u, 2, 1) + _hdot(rkm, v, 2, 1)
    last = jnp.exp(jnp.sum(lw, axis=1, keepdims=True))
    h1 = jnp.swapaxes(last, 1, 2) * (h0 + _hdot(bt, u, 1, 1) + _hdot(kt, v, 1, 1))
    mean = jnp.mean(y, axis=-1, keepdims=True)
    yc = y - mean
    var = jnp.mean(yc * yc, axis=-1, keepdims=True)
    yn = yc * lax.rsqrt(var + GN_EPS) * lnw + lnb
    bonus = jnp.sum(r * k * rk, axis=-1, keepdims=True) * v
    return yn + bonus, h1


def _scan_specs(h, t, dh, rev):
    n = SCAN_CHUNK
    nc = t // n
    pos = (lambda c: (0, nc - 1 - c, 0)) if rev else (lambda c: (0, c, 0))
    st = (lambda c: (nc - 1 - c, 0, 0, 0)) if rev else (lambda c: (c, 0, 0, 0))
    seq = pl.BlockSpec((h, n, dh), pos)
    par = pl.BlockSpec((h, 1, dh), lambda c: (0, 0, 0))
    state = pl.BlockSpec((1, h, dh, dh), st)
    return seq, par, state


def _scan_fwd(seqs, pars):
    h, t, dh = seqs[0].shape
    nc = t // SCAN_CHUNK
    seq, par, state = _scan_specs(h, t, dh, False)

    def body(r, lw, k, v, kkraw, a, rk, lnw, lnb, o_ref, st_ref, h_ref):
        @pl.when(pl.program_id(0) == 0)
        def _():
            h_ref[...] = jnp.zeros_like(h_ref)

        h0 = h_ref[...]
        st_ref[0] = h0
        o, h1 = _scan_chunk_fn(h0, r[...], lw[...], k[...], v[...], kkraw[...], a[...], rk[...], lnw[...], lnb[...])
        o_ref[...] = o
        h_ref[...] = h1

    return pl.pallas_call(
        body, name="rwkv_scan_fwd", grid=(nc,),
        out_shape=(jax.ShapeDtypeStruct((h, t, dh), F32), jax.ShapeDtypeStruct((nc, h, dh, dh), F32)),
        in_specs=[seq] * 6 + [par] * 3, out_specs=(seq, state),
        scratch_shapes=[pltpu.VMEM((h, dh, dh), F32)], compiler_params=_params("arbitrary"),
    )(*seqs, *pars)


def _scan_bwd(seqs, pars, states, do):
    h, t, dh = seqs[0].shape
    nc = t // SCAN_CHUNK
    seq, par, state = _scan_specs(h, t, dh, True)

    def body(r, lw, k, v, kkraw, a, rk, lnw, lnb, st_ref, do_ref, *rest):
        douts, dpars, dh_ref = rest[:6], rest[6:9], rest[9]
        first = pl.program_id(0) == 0

        @pl.when(first)
        def _():
            dh_ref[...] = jnp.zeros_like(dh_ref)

        _, vjp = jax.vjp(_scan_chunk_fn, st_ref[0], r[...], lw[...], k[...], v[...], kkraw[...], a[...],
                         rk[...], lnw[...], lnb[...])
        grads = vjp((do_ref[...], dh_ref[...]))
        dh_ref[...] = grads[0]
        for ref, val in zip(douts, grads[1:7]):
            ref[...] = val

        @pl.when(first)
        def _():
            for ref, val in zip(dpars, grads[7:]):
                ref[...] = val

        @pl.when(jnp.logical_not(first))
        def _():
            for ref, val in zip(dpars, grads[7:]):
                ref[...] += val

    sshape = jax.ShapeDtypeStruct((h, t, dh), F32)
    pshape = jax.ShapeDtypeStruct((h, 1, dh), F32)
    return pl.pallas_call(
        body, name="rwkv_scan_bwd", grid=(nc,), out_shape=(sshape,) * 6 + (pshape,) * 3,
        in_specs=[seq] * 6 + [par] * 3 + [state, seq], out_specs=(seq,) * 6 + (par,) * 3,
        scratch_shapes=[pltpu.VMEM((h, dh, dh), F32)], compiler_params=_params("arbitrary"),
    )(*seqs, *pars, states, do)


def _heads(x):
    return x.reshape(x.shape[0], -1, HEAD_DIM).transpose(1, 0, 2)


def _unheads(x):
    return x.transpose(1, 0, 2).reshape(x.shape[1], -1)


def _to_sub(x, dil):
    h, t, d = x.shape
    return x.reshape(h, t // dil, dil, d).transpose(0, 2, 1, 3).reshape(h * dil, t // dil, d)


def _from_sub(x, dil):
    g, length, d = x.shape
    return x.reshape(g // dil, dil, length, d).transpose(0, 2, 1, 3).reshape(g // dil, length * dil, d)


def _local_step(x, target, w, ex):
    w = dict(w)
    c = w["mu_r"].shape[-1]
    qn, kn = w["q_norm"].reshape(1, 1, HEAD_DIM), w["k_norm"].reshape(1, 1, HEAD_DIM)
    vec = jnp.concatenate([w[n].reshape(1, c) for n in RWKV_VEC], axis=0)
    pars = [w[n].reshape(-1, 1, HEAD_DIM) for n in ("r_k", "ln_x_w", "ln_x_b")]
    no_dep = jnp.zeros(DEP_SHAPE, F32)

    x1 = _ffn_fwd(x, w["ffn1_norm"], w["ffn1_w_gate"], w["ffn1_w_up"], w["ffn1_w_down"], ex.first_dep, "ffn1_fwd")
    w.update(ex.mix_weights(x1))
    att_w = w["w_in"].shape[0] * w["w_in"].shape[2] - 4 * c
    mats = [w[n] for n in RWKV_MAT]
    proj = _proj_fwd(x1, w["mix_norm"], w["w_in"])
    hw = att_w // 3
    qkv = [_heads(proj[:, j * hw:(j + 1) * hw]) for j in range(3)]
    subs = [[_to_sub(a, dil) for a in qkv] for dil in DILATIONS]
    outs = [_att_fwd(*s, qn, kn, f"att_fwd_d{dil}") for s, dil in zip(subs, DILATIONS)]
    os_ = [_from_sub(o, dil) for (o, _), dil in zip(outs, DILATIONS)]
    ls_ = [_from_sub(l, dil) for (_, l), dil in zip(outs, DILATIONS)]
    att = _unheads(_merge_fwd(os_, ls_))
    cur = proj[:, att_w:]
    prev = jnp.concatenate([jnp.zeros_like(cur[:1]), cur[:-1]], axis=0)
    pre = _rwkv_pre_fwd(cur, prev, vec, mats)
    seqs = [_heads(a) for a in pre[:6]]
    gate = pre[6]
    opg_h, states = _scan_fwd(seqs, pars)
    opg = _unheads(opg_h)
    w.update(ex.out_weights(opg))
    x2 = _mixout_fwd(x1, att, opg, gate, w["w_out"])
    x3 = _ffn_fwd(x2, w["ffn2_norm"], w["ffn2_w_gate"], w["ffn2_w_up"], w["ffn2_w_down"], no_dep, "ffn2_fwd")
    dy, loss = _loss_head(x3, target)

    g = {}
    dx2, g["ffn2_norm"], g["ffn2_w_gate"], g["ffn2_w_up"], g["ffn2_w_down"] = _ffn_bwd(
        x2, w["ffn2_norm"], w["ffn2_w_gate"], w["ffn2_w_up"], w["ffn2_w_down"], dy, no_dep, "ffn2_bwd")
    dep = ex.send_ffn2({n: g[n] for n in ("ffn2_w_gate", "ffn2_w_up", "ffn2_w_down")})
    datt, dopg, dgate, g["w_out"] = _mixout_bwd(att, opg, gate, w["w_out"], dx2, dep)
    dscan = _scan_bwd(seqs, pars, states, _heads(dopg))
    for n, d in zip(("r_k", "ln_x_w", "ln_x_b"), dscan[6:]):
        g[n] = d
    dpre = _rwkv_pre_bwd(cur, prev, vec, mats, [_unheads(d) for d in dscan[:6]] + [dgate])
    dcur, dprev, dvec = dpre[:3]
    for n, d in zip(RWKV_MAT, dpre[3:]):
        g[n] = d
    for j, n in enumerate(RWKV_VEC):
        g[n] = dvec[j:j + 1]
    dmerge = _merge_bwd(os_, ls_, _heads(datt))
    dqn = dkn = jnp.zeros((1, 1, HEAD_DIM), F32)
    dqkv = []
    for j, dil in enumerate(DILATIONS):
        dq, dk, dv, dqn, dkn = _att_bwd(*subs[j], qn, kn, _to_sub(dmerge[j], dil), _to_sub(dmerge[3 + j], dil),
                                        dqn, dkn, f"att_bwd_d{dil}")
        dqkv.append([_unheads(_from_sub(a, dil)) for a in (dq, dk, dv)])
    g["q_norm"], g["k_norm"] = dqn, dkn
    dshift = jnp.concatenate([dprev[1:], jnp.zeros_like(dprev[:1])], axis=0)
    dps = [jnp.concatenate(dqkv[j] + [tail], axis=1) for j, tail in enumerate((dcur, dshift, jnp.zeros_like(dcur)))]
    dx1, g["mix_norm"], g["w_in"] = _proj_bwd(x1, w["mix_norm"], w["w_in"], *dps, dx2)
    dep = ex.send_mix({n: g[n] for n in ("w_in", "w_out") + RWKV_MAT}, dx1)
    dx, g["ffn1_norm"], g["ffn1_w_gate"], g["ffn1_w_up"], g["ffn1_w_down"] = _ffn_bwd(
        x, w["ffn1_norm"], w["ffn1_w_gate"], w["ffn1_w_up"], w["ffn1_w_down"], dx1, dep, "ffn1_bwd")
    return loss, dx, g


N_SHARDS = 4


def _place():
    return lax.axis_index("x"), lax.axis_index("y"), lax.axis_index("c")


def _chip_peers(x, y):
    return [(1 - x, y), (x, 1 - y), (1 - x, 1 - y)]


def _gather_xy(shards):
    n = len(shards)

    def body(*refs):
        ins, outs = refs[:n], refs[n:2 * n]
        send_sems, recv_sems, local_sems = refs[2 * n:]
        x, y, c = _place()
        me = 2 * x + y
        copies = []
        for i in range(n):
            own = pltpu.make_async_copy(ins[i], outs[i].at[me], local_sems.at[i])
            own.start()
            copies.append(own)
            for k, (px, py) in enumerate(_chip_peers(x, y)):
                cp = pltpu.make_async_remote_copy(
                    src_ref=ins[i], dst_ref=outs[i].at[me], send_sem=send_sems.at[i, k], recv_sem=recv_sems.at[i, k],
                    device_id=(px, py, c), device_id_type=MESH)
                cp.start()
                copies.append(cp)
        for cp in copies:
            cp.wait()

    return pl.pallas_call(
        body, name="gather_weights",
        out_shape=tuple(jax.ShapeDtypeStruct((N_SHARDS,) + s.shape, s.dtype) for s in shards),
        in_specs=[ANY] * n, out_specs=(ANY,) * n,
        scratch_shapes=[pltpu.SemaphoreType.DMA((n, 3)), pltpu.SemaphoreType.DMA((n, 3)), pltpu.SemaphoreType.DMA((n,))],
    )(*shards)


def _scatter_partials(parts):
    n = len(parts)

    def body(*refs):
        ins, outs = refs[:n], refs[n:2 * n]
        send_sems, recv_sems = refs[2 * n:]
        x, y, c = _place()
        copies = []
        for i in range(n):
            for k, (px, py) in enumerate(_chip_peers(x, y)):
                cp = pltpu.make_async_remote_copy(
                    src_ref=ins[i].at[2 * px + py], dst_ref=outs[i].at[k], send_sem=send_sems.at[i, k],
                    recv_sem=recv_sems.at[i, k], device_id=(px, py, c), device_id_type=MESH)
                cp.start()
                copies.append(cp)
        for cp in copies:
            cp.wait()

    return pl.pallas_call(
        body, name="scatter_partials",
        out_shape=tuple(jax.ShapeDtypeStruct((3,) + p.shape[1:], p.dtype) for p in parts),
        in_specs=[ANY] * n, out_specs=(ANY,) * n,
        scratch_shapes=[pltpu.SemaphoreType.DMA((n, 3)), pltpu.SemaphoreType.DMA((n, 3))],
    )(*parts)


HBM = pl.BlockSpec(memory_space=pltpu.HBM)
SEM = pl.BlockSpec(memory_space=pltpu.SEMAPHORE)
DEP_SHAPE = (8, 128)


def _gather_views(i, srcs, lands, k, px, py, me):
    return (srcs[i], lands[i].at[me]), (srcs[i], lands[i].at[2 * px + py])


def _scatter_views(i, srcs, lands, k, px, py, me):
    return (srcs[i].at[2 * px + py], lands[i].at[k]), (srcs[i].at[me], lands[i].at[k])


def _push_start(srcs, land_shapes, views, after, name):
    n = len(srcs)

    def body(*refs):
        src_refs, land_refs = refs[:n], refs[n:2 * n]
        send_sems, recv_sems = refs[2 * n + 1:2 * n + 3]
        token = refs[4 * n + 3]
        x, y, c = _place()
        me = 2 * x + y
        for i in range(n):
            for k, (px, py) in enumerate(_chip_peers(x, y)):
                (src, dst), _ = views(i, src_refs, land_refs, k, px, py, me)
                pltpu.make_async_remote_copy(
                    src_ref=src, dst_ref=dst, send_sem=send_sems.at[3 * i + k], recv_sem=recv_sems.at[3 * i + k],
                    device_id=(px, py, c), device_id_type=MESH).start()
        token[...] = jnp.zeros_like(token)

    sems = pltpu.SemaphoreType.DMA((3 * n,))
    lands = [pltpu.with_memory_space_constraint(lax.empty(s.shape, s.dtype), pltpu.HBM) for s in land_shapes]
    srcs = [pltpu.with_memory_space_constraint(s, pltpu.HBM) for s in srcs]
    outs = pl.pallas_call(
        body, name=name,
        out_shape=(sems, sems, *[pltpu.HBM(s.shape, s.dtype) for s in srcs], *[pltpu.HBM(s.shape, s.dtype) for s in land_shapes],
                   jax.ShapeDtypeStruct(DEP_SHAPE, F32)),
        in_specs=[HBM] * (2 * n) + [ANY], out_specs=(SEM, SEM, *[HBM] * (2 * n), VMEM_FULL),
        input_output_aliases={i: 2 + i for i in range(2 * n)},
        compiler_params=pltpu.CompilerParams(has_side_effects=pltpu.SideEffectType.DATAFLOW_SIDE_EFFECTING),
    )(*srcs, *lands, after)
    return outs[0], outs[1], outs[2:2 + n], outs[2 + n:2 + 2 * n], outs[2 + 2 * n]


def _push_wait(started, views, own_slot, after, name):
    send_sems, recv_sems, srcs, lands, _ = started
    n = len(srcs)

    def body(*refs):
        src_refs, land_refs = refs[:n], refs[n:2 * n]
        send_sems, recv_sems = refs[2 * n:2 * n + 2]
        local_sems = refs[-1]
        x, y, c = _place()
        me = 2 * x + y
        own = [pltpu.make_async_copy(src_refs[i], land_refs[i].at[me], local_sems.at[i]) for i in range(n)] if own_slot else []
        for cp in own:
            cp.start()
        for cp in own:
            cp.wait()
        for i in range(n):
            for k, (px, py) in enumerate(_chip_peers(x, y)):
                _, (src, dst) = views(i, src_refs, land_refs, k, px, py, me)
                landing = pltpu.make_async_remote_copy(
                    src_ref=src, dst_ref=dst, send_sem=send_sems.at[3 * i + k], recv_sem=recv_sems.at[3 * i + k],
                    device_id=(px, py, c), device_id_type=MESH)
                landing.wait_send()
                landing.wait_recv()

    outs = pl.pallas_call(
        body, name=name,
        out_shape=tuple(pltpu.HBM(s.shape, s.dtype) for s in (*srcs, *lands)),
        in_specs=[HBM] * (2 * n) + [SEM, SEM, ANY], out_specs=(HBM,) * (2 * n),
        input_output_aliases={i: i for i in range(2 * n)},
        scratch_shapes=[pltpu.SemaphoreType.DMA((n,))],
        compiler_params=pltpu.CompilerParams(has_side_effects=pltpu.SideEffectType.DATAFLOW_SIDE_EFFECTING),
    )(*srcs, *lands, send_sems, recv_sems, after)
    return outs[n:]


def _sibling_swap(arrays):
    n = len(arrays)

    def body(*refs):
        ins, outs = refs[:n], refs[n:2 * n]
        send_sems, recv_sems = refs[2 * n:]
        x, y, c = _place()
        copies = []
        for i in range(n):
            cp = pltpu.make_async_remote_copy(
                src_ref=ins[i], dst_ref=outs[i], send_sem=send_sems.at[i], recv_sem=recv_sems.at[i],
                device_id=(x, y, 1 - c), device_id_type=MESH)
            cp.start()
            copies.append(cp)
        for cp in copies:
            cp.wait()

    return pl.pallas_call(
        body, name="sibling_swap",
        out_shape=tuple(jax.ShapeDtypeStruct(a.shape, a.dtype) for a in arrays),
        in_specs=[ANY] * n, out_specs=(ANY,) * n,
        scratch_shapes=[pltpu.SemaphoreType.DMA((n,)), pltpu.SemaphoreType.DMA((n,))],
    )(*arrays)


N_DEV = 8


def _allreduce_small(pack):
    def body(in_ref, out_ref, buf, send_sems, recv_sems):
        x, y, c = _place()
        me = 4 * x + 2 * y + c
        buf[me] = in_ref[...]

        def copy(j, slot):
            px, py, pc = x ^ (j >> 2), y ^ ((j >> 1) & 1), c ^ (j & 1)
            return pltpu.make_async_remote_copy(
                src_ref=in_ref, dst_ref=buf.at[slot(px, py, pc)], send_sem=send_sems.at[j], recv_sem=recv_sems.at[j],
                device_id=(px, py, pc), device_id_type=MESH)

        for j in range(1, N_DEV):
            copy(j, lambda px, py, pc: me).start()
        for j in range(1, N_DEV):
            landing = copy(j, lambda px, py, pc: 4 * px + 2 * py + pc)
            landing.wait_send()
            landing.wait_recv()
        acc = buf[0]
        for s in range(1, N_DEV):
            acc = acc + buf[s]
        out_ref[...] = acc

    return pl.pallas_call(
        body, name="allreduce_small", out_shape=jax.ShapeDtypeStruct(pack.shape, F32),
        in_specs=[VMEM_FULL], out_specs=VMEM_FULL,
        scratch_shapes=[pltpu.VMEM((N_DEV,) + pack.shape, F32), pltpu.SemaphoreType.DMA((N_DEV,)),
                        pltpu.SemaphoreType.DMA((N_DEV,))],
    )(pack)


ROW_TILE_MAX = 256
BF16_SUBLANES = 16


def _row_tile(rows):
    for tr in range(min(rows, ROW_TILE_MAX), 0, -1):
        if rows % tr == 0 and tr % BF16_SUBLANES == 0:
            return tr
    return rows


def _reduce_own(me, part, recv, name):
    _, r, cols = part.shape
    tr = _row_tile(r)

    def body(me_ref, p_ref, rv_ref, o_ref):
        acc = p_ref[0]
        for k in range(3):
            acc = acc + rv_ref[k].astype(F32)
        o_ref[...] = acc

    return pl.pallas_call(
        body, name=name, out_shape=jax.ShapeDtypeStruct((r, cols), F32),
        grid_spec=pltpu.PrefetchScalarGridSpec(
            num_scalar_prefetch=1, grid=(r // tr,),
            in_specs=[pl.BlockSpec((1, tr, cols), lambda i, me_ref: (me_ref[0], i, 0)),
                      pl.BlockSpec((3, tr, cols), lambda i, me_ref: (0, i, 0))],
            out_specs=pl.BlockSpec((tr, cols), lambda i, me_ref: (i, 0))),
        compiler_params=_params("arbitrary"),
    )(me, part, recv)


def _adamw(w, ga, gb, m, v, name):
    r, cols = w.shape
    tr = _row_tile(r)
    c1 = 1.0 - ADAM_B1 ** ADAM_STEP
    c2 = 1.0 - ADAM_B2 ** ADAM_STEP

    def body(w_ref, ga_ref, gb_ref, m_ref, v_ref, g_out, d_out, m_out, v_out):
        g = ga_ref[...] + gb_ref[...]
        mn = ADAM_B1 * m_ref[...] + (1.0 - ADAM_B1) * g
        vn = ADAM_B2 * v_ref[...] + (1.0 - ADAM_B2) * (g * g)
        g_out[...] = g
        m_out[...] = mn
        v_out[...] = vn
        d_out[...] = -ADAM_LR * ((mn / c1) / (jnp.sqrt(vn / c2) + ADAM_EPS) + ADAM_WD * w_ref[...])

    tile = pl.BlockSpec((tr, cols), lambda i: (i, 0))
    shape = jax.ShapeDtypeStruct((r, cols), F32)
    return pl.pallas_call(
        body, name=name, grid=(r // tr,), out_shape=(shape,) * 4, in_specs=[tile] * 5, out_specs=(tile,) * 4,
        compiler_params=_params("arbitrary"),
    )(w, ga, gb, m, v)


PACK_COLS = 512


def _to_rows(a):
    flat = a.reshape(-1)
    pad = (-flat.shape[0]) % PACK_COLS
    return jnp.pad(flat, (0, pad)).reshape(-1, PACK_COLS)


def _pack(arrays, extra_rows=0):
    rows = [_to_rows(a) for a in arrays]
    n = sum(r.shape[0] for r in rows) + extra_rows
    pad = (-n) % 8
    return jnp.concatenate(rows + [jnp.zeros((extra_rows + pad, PACK_COLS), F32)], axis=0)


def _unpack(pack, like):
    out, at = [], 0
    for a in like:
        n = -(-a.size // PACK_COLS)
        out.append(pack[at:at + n].reshape(-1)[:a.size].reshape(a.shape))
        at += n
    return out


COL_SHARDED = ("ffn1_w_gate", "ffn1_w_up", "w_in", "ffn2_w_gate", "ffn2_w_up", "w2", "a2", "g2")
ROW_SHARDED = ("ffn1_w_down", "ffn2_w_down", "w_out", "w1", "a1", "g1")
CHUNKED = ("ffn1_w_gate", "ffn1_w_up", "ffn1_w_down", "w_in", "ffn2_w_gate", "ffn2_w_up", "ffn2_w_down")
WEIGHTS = ("ffn1_norm", "ffn1_w_gate", "ffn1_w_up", "ffn1_w_down", "mix_norm", "w_in", "q_norm", "k_norm",
           "mu_r", "mu_k", "mu_v", "mu_w", "mu_a", "mu_g", "w0", "w1", "w2", "a0", "a1", "a2", "g1", "g2",
           "k_k", "k_a", "r_k", "ln_x_w", "ln_x_b", "w_out", "ffn2_norm", "ffn2_w_gate", "ffn2_w_up", "ffn2_w_down")


def _full_from_blocks(name, blocks):
    if name in CHUNKED:
        return blocks
    if name in ROW_SHARDED:
        return blocks.reshape(-1, blocks.shape[-1])
    return blocks.transpose(1, 0, 2).reshape(blocks.shape[1], -1)


def _blocks_from_full(name, full):
    if name in CHUNKED:
        return full
    if name in ROW_SHARDED:
        return full.reshape(N_SHARDS, -1, full.shape[-1])
    return full.reshape(full.shape[0], N_SHARDS, -1).transpose(1, 0, 2)


FFN1_GROUP = ("ffn1_w_gate", "ffn1_w_up", "ffn1_w_down")
MIX_GROUP = ("w_in",) + RWKV_MAT
OUT_GROUP = ("w_out", "ffn2_w_gate", "ffn2_w_up", "ffn2_w_down")
FFN2_GROUP = OUT_GROUP[1:]
LATE_GROUP = ("w_in", "w_out") + RWKV_MAT


class _Exchange:
    def __init__(self, given):
        self.given = given
        first = _gather_xy(self._shards(FFN1_GROUP))
        self.first_weights = self._full(FFN1_GROUP, first)
        self.mix = self._gather_start(MIX_GROUP, first[0], "gather_mix_start")
        self.out = self._gather_start(OUT_GROUP, self.mix[4], "gather_out_start")
        self.first_dep = self.out[4]
        self.parts, self.recv = {}, {}

    def _shards(self, names):
        return [self.given[n][0].astype(BF16) for n in names]

    @staticmethod
    def _full(names, blocks):
        out = {}
        for n, b in zip(names, blocks):
            full = _full_from_blocks(n, b)
            out[n] = full.astype(F32) if n in RWKV_MAT else full
        return out

    def _gather_start(self, names, after, name):
        shards = self._shards(names)
        lands = [jax.ShapeDtypeStruct((N_SHARDS,) + s.shape, s.dtype) for s in shards]
        return _push_start(shards, lands, _gather_views, after, name)

    def mix_weights(self, after):
        return self._full(MIX_GROUP, _push_wait(self.mix, _gather_views, True, after, "gather_mix_wait"))

    def out_weights(self, after):
        return self._full(OUT_GROUP, _push_wait(self.out, _gather_views, True, after, "gather_out_wait"))

    def _scatter_start(self, grads, name):
        names = tuple(grads)
        parts = [_blocks_from_full(n, grads[n]) for n in names]
        self.parts.update(zip(names, parts))
        lands = [jax.ShapeDtypeStruct((3,) + p.shape[1:], BF16) for p in parts]
        return _push_start([p.astype(BF16) for p in parts], lands, _scatter_views, parts[0], name)

    def send_ffn2(self, grads):
        self.ffn2 = self._scatter_start(grads, "scatter_ffn2_start")
        return self.ffn2[4]

    def send_mix(self, grads, after):
        self.recv.update(zip(FFN2_GROUP, _push_wait(self.ffn2, _scatter_views, False, after, "scatter_ffn2_wait")))
        self.late = self._scatter_start(grads, "scatter_late_start")
        return self.late[4]

    def finish(self, grads):
        names = tuple(grads)
        parts = [_blocks_from_full(n, grads[n]) for n in names]
        self.parts.update(zip(names, parts))
        got = _scatter_partials([p.astype(BF16) for p in parts])
        self.recv.update(zip(names, got))
        self.recv.update(zip(LATE_GROUP, _push_wait(self.late, _scatter_views, False, got[0], "scatter_late_wait")))
        return self.parts, self.recv


def kernel(
        x, ffn1_norm, ffn1_w_gate, ffn1_w_up, ffn1_w_down, mix_norm, w_in, q_norm, k_norm, mu_r, mu_k, mu_v, mu_w,
        mu_a, mu_g, w0, w1, w2, a0, a1, a2, g1, g2, k_k, k_a, r_k, ln_x_w, ln_x_b, w_out, ffn2_norm, ffn2_w_gate,
        ffn2_w_up, ffn2_w_down, loss_target, m_ffn1_norm, m_ffn1_w_gate, m_ffn1_w_up, m_ffn1_w_down, m_mix_norm,
        m_w_in, m_q_norm, m_k_norm, m_mu_r, m_mu_k, m_mu_v, m_mu_w, m_mu_a, m_mu_g, m_w0, m_w1, m_w2, m_a0, m_a1,
        m_a2, m_g1, m_g2, m_k_k, m_k_a, m_r_k, m_ln_x_w, m_ln_x_b, m_w_out, m_ffn2_norm, m_ffn2_w_gate, m_ffn2_w_up,
        m_ffn2_w_down, v_ffn1_norm, v_ffn1_w_gate, v_ffn1_w_up, v_ffn1_w_down, v_mix_norm, v_w_in, v_q_norm, v_k_norm,
        v_mu_r, v_mu_k, v_mu_v, v_mu_w, v_mu_a, v_mu_g, v_w0, v_w1, v_w2, v_a0, v_a1, v_a2, v_g1, v_g2, v_k_k, v_k_a,
        v_r_k, v_ln_x_w, v_ln_x_b, v_w_out, v_ffn2_norm, v_ffn2_w_gate, v_ffn2_w_up, v_ffn2_w_down):
    given = dict(locals())
    sharded = COL_SHARDED + ROW_SHARDED
    sharded = tuple(n for n in WEIGHTS if n in sharded)
    small = tuple(n for n in WEIGHTS if n not in sharded)

    ex = _Exchange(given)
    w = {n: given[n] for n in small}
    w.update(ex.first_weights)
    loss, dx, g = _local_step(x[0], loss_target[0], w, ex)
    parts, recv = ex.finish({n: g[n] for n in FFN1_GROUP})

    me = (2 * lax.axis_index("x") + lax.axis_index("y")).astype(jnp.int32).reshape(1)
    mine = []
    for n in sharded:
        p, rv = parts[n], recv[n]
        p2 = p.reshape(N_SHARDS, -1, p.shape[-1])
        mine.append(_reduce_own(me, p2, rv.reshape(3, -1, rv.shape[-1]), f"reduce_{n}"))
    theirs = _sibling_swap(mine)
    out = {}
    for n, a, b in zip(sharded, mine, theirs):
        shape = given[n].shape
        two_d = (-1, shape[-1])
        res = _adamw(given[n].reshape(two_d), a, b, given["m_" + n].reshape(two_d), given["v_" + n].reshape(two_d), f"adamw_{n}")
        out[n] = [r.reshape(shape) for r in res]

    gpack = _pack([g[n] for n in small], extra_rows=1)
    n_rows = sum(-(-given[n].size // PACK_COLS) for n in small)
    gpack = gpack.at[n_rows, :loss.shape[1]].set(loss[0])
    gsum = _allreduce_small(gpack)
    res = _adamw(_pack([given[n] for n in small], 1), gsum, jnp.zeros_like(gsum), _pack([given["m_" + n] for n in small], 1),
                 _pack([given["v_" + n] for n in small], 1), "adamw_small")
    like = [given[n] for n in small]
    for j, r in enumerate(res):
        for n, a in zip(small, _unpack(r, like)):
            out.setdefault(n, [None] * 4)[j] = a
    total_loss = gsum[n_rows, 0]
    return (total_loss, dx[None], *[out[n][0] for n in WEIGHTS], *[out[n][1] for n in WEIGHTS],
            *[out[n][2] for n in WEIGHTS], *[out[n][3] for n in WEIGHTS])
```

```python
import functools

import jax
import jax.numpy as jnp
from jax import lax
from jax.experimental import pallas as pl
from jax.experimental.pallas import tpu as pltpu

F32 = jnp.float32
BF16 = jnp.bfloat16
HIGHEST = lax.Precision.HIGHEST
MESH = pl.DeviceIdType.MESH

RMS_EPS = 1e-6
GN_EPS = 64e-5
NEG_INF = -1e30
FFN_RESIDUAL = 0.5
HEAD_DIM = 64
ATT_BLOCK = 128
DILATIONS = (1, 4, 16)
SCAN_CHUNK = 64
TOKEN_TILE = 256

ADAM_LR = 0.001
ADAM_B1 = 0.9
ADAM_B2 = 0.999
ADAM_EPS = 1e-08
ADAM_WD = 0.01
ADAM_STEP = 10

VMEM_FULL = pl.BlockSpec(memory_space=pltpu.VMEM)
ANY = pl.BlockSpec(memory_space=pl.ANY)


VMEM_LIMIT = 56 * 1024 * 1024


def _params(*sem):
    return pltpu.CompilerParams(dimension_semantics=sem, vmem_limit_bytes=VMEM_LIMIT)


def _dot(a, b, dims):
    return lax.dot_general(a.astype(BF16), b.astype(BF16), (dims, ((), ())), preferred_element_type=F32)


def _dot_nn(a, b):
    return _dot(a, b, ((1,), (0,)))


def _dot_nt(a, b):
    return _dot(a, b, ((1,), (1,)))


def _dot_tn(a, b):
    return _dot(a, b, ((0,), (0,)))


@jax.custom_vjp
def _mm(a, b):
    return _dot_nn(a, b)


def _mm_fwd(a, b):
    return _dot_nn(a, b), (a, b)


def _mm_bwd(res, g):
    a, b = res
    return _dot_nt(g, b).astype(a.dtype), _dot_tn(a, g).astype(b.dtype)


_mm.defvjp(_mm_fwd, _mm_bwd)


def _bdot(a, b, ca, cb):
    return lax.dot_general(a.astype(BF16), b.astype(BF16), (((ca,), (cb,)), ((0,), (0,))), preferred_element_type=F32)


@jax.custom_vjp
def _bmm_nt(a, b):
    return _bdot(a, b, 2, 2)


def _bmm_nt_fwd(a, b):
    return _bdot(a, b, 2, 2), (a, b)


def _bmm_nt_bwd(res, g):
    a, b = res
    return _bdot(g, b, 2, 1), _bdot(g, a, 1, 1)


_bmm_nt.defvjp(_bmm_nt_fwd, _bmm_nt_bwd)


@jax.custom_vjp
def _bmm_nn(a, b):
    return _bdot(a, b, 2, 1)


def _bmm_nn_fwd(a, b):
    return _bdot(a, b, 2, 1), (a, b)


def _bmm_nn_bwd(res, g):
    a, b = res
    return _bdot(g, b, 2, 2), _bdot(a, g, 1, 1)


_bmm_nn.defvjp(_bmm_nn_fwd, _bmm_nn_bwd)


def _hdot(a, b, ca, cb):
    return lax.dot_general(a, b, (((ca,), (cb,)), ((0,), (0,))), precision=lax.Precision.HIGH, preferred_element_type=F32)


def _sigmoid(x):
    return 1.0 / (1.0 + jnp.exp(-x))


def _rms(x):
    return lax.rsqrt(jnp.mean(x * x, axis=-1, keepdims=True) + RMS_EPS)


def _ffn_fwd(x, norm, wg, wu, wd, dep, name):
    t, d = x.shape
    nc = wg.shape[0]
    tm = TOKEN_TILE

    def body(x_ref, n_ref, wg_ref, wu_ref, wd_ref, dep_ref, o_ref):
        xv = x_ref[...]
        h = (xv * _rms(xv) * n_ref[...]).astype(BF16)
        acc = jnp.zeros((tm, d), F32)
        for c in range(nc):
            g = jnp.dot(h, wg_ref[c], preferred_element_type=F32)
            u = jnp.dot(h, wu_ref[c], preferred_element_type=F32)
            a = (g * _sigmoid(g) * u).astype(BF16)
            acc = acc + jnp.dot(a, wd_ref[c], preferred_element_type=F32)
        o_ref[...] = xv + FFN_RESIDUAL * acc

    tile = pl.BlockSpec((tm, d), lambda i: (i, 0))
    return pl.pallas_call(
        body, name=name, grid=(t // tm,), out_shape=jax.ShapeDtypeStruct((t, d), F32),
        in_specs=[tile, pl.BlockSpec((1, d), lambda i: (0, 0)), VMEM_FULL, VMEM_FULL, VMEM_FULL, ANY],
        out_specs=tile, compiler_params=_params("arbitrary"),
    )(x, norm, wg, wu, wd, dep)


def _rmsnorm_bwd(xv, gain, dh):
    rs = _rms(xv)
    xn = xv * rs
    dxn = dh * gain
    dx = rs * (dxn - xn * jnp.mean(dxn * xn, axis=-1, keepdims=True))
    return dx, jnp.sum(dh * xn, axis=0, keepdims=True)


def _ffn_bwd(x, norm, wg, wu, wd, dy, dep, name):
    t, d = x.shape
    nc, _, fc = wg.shape
    tm = TOKEN_TILE
    nt = t // tm

    def body(x_ref, n_ref, wg_ref, wu_ref, wd_ref, dy_ref, dep_ref, dx_ref, dn_ref, dwg_ref, dwu_ref, dwd_ref, dh_ref):
        c, i = pl.program_id(0), pl.program_id(1)
        rows = pl.ds(pl.multiple_of(i * tm, tm), tm)
        xv = x_ref[...]
        gain = n_ref[...]
        h = (xv * _rms(xv) * gain).astype(BF16)
        dy = dy_ref[...]
        dyb = (FFN_RESIDUAL * dy).astype(BF16)
        g = jnp.dot(h, wg_ref[0], preferred_element_type=F32)
        u = jnp.dot(h, wu_ref[0], preferred_element_type=F32)
        sg = _sigmoid(g)
        s = g * sg
        a = (s * u).astype(BF16)
        da = _dot_nt(dyb, wd_ref[0])
        dub = (da * s).astype(BF16)
        dgb = (da * u * (sg * (1.0 + g * (1.0 - sg)))).astype(BF16)
        dwd_c = _dot_tn(a, dyb)
        dwg_c = _dot_tn(h, dgb)
        dwu_c = _dot_tn(h, dub)
        dh_c = _dot_nt(dgb, wg_ref[0]) + _dot_nt(dub, wu_ref[0])

        @pl.when(i == 0)
        def _():
            dwd_ref[0] = dwd_c
            dwg_ref[0] = dwg_c
            dwu_ref[0] = dwu_c

        @pl.when(i > 0)
        def _():
            dwd_ref[0] += dwd_c
            dwg_ref[0] += dwg_c
            dwu_ref[0] += dwu_c

        @pl.when(c == 0)
        def _():
            dh_ref[rows, :] = dh_c

        @pl.when(c > 0)
        def _():
            dh_ref[rows, :] += dh_c

        @pl.when(c == nc - 1)
        def _():
            dx, dn = _rmsnorm_bwd(xv, gain, dh_ref[rows, :])
            dx_ref[...] = dx + dy

            @pl.when(i == 0)
            def _():
                dn_ref[...] = dn

            @pl.when(i > 0)
            def _():
                dn_ref[...] += dn

    tile = pl.BlockSpec((tm, d), lambda c, i: (i, 0))
    row = pl.BlockSpec((1, d), lambda c, i: (0, 0))
    wcol = pl.BlockSpec((1, d, fc), lambda c, i: (c, 0, 0))
    wrow = pl.BlockSpec((1, fc, d), lambda c, i: (c, 0, 0))
    last = pl.BlockSpec((tm, d), lambda c, i: (jnp.where(c == nc - 1, i, 0), 0))
    return pl.pallas_call(
        body, name=name, grid=(nc, nt),
        out_shape=(jax.ShapeDtypeStruct((t, d), F32), jax.ShapeDtypeStruct((1, d), F32),
                   jax.ShapeDtypeStruct(wg.shape, F32), jax.ShapeDtypeStruct(wu.shape, F32),
                   jax.ShapeDtypeStruct(wd.shape, F32)),
        in_specs=[tile, row, wcol, wcol, wrow, tile, ANY],
        out_specs=(last, row, wcol, wcol, wrow),
        scratch_shapes=[pltpu.VMEM((t, d), F32)],
        compiler_params=_params("arbitrary", "arbitrary"),
    )(x, norm, wg, wu, wd, dy, dep)


def _proj_fwd(x, norm, w):
    t, d = x.shape
    nc, _, ncol = w.shape
    tm = TOKEN_TILE

    def body(x_ref, n_ref, w_ref, o_ref):
        xv = x_ref[...]
        h = (xv * _rms(xv) * n_ref[...]).astype(BF16)
        for c in range(nc):
            o_ref[:, c * ncol:(c + 1) * ncol] = jnp.dot(h, w_ref[c], preferred_element_type=F32)

    return pl.pallas_call(
        body, name="proj_fwd", grid=(t // tm,), out_shape=jax.ShapeDtypeStruct((t, nc * ncol), F32),
        in_specs=[pl.BlockSpec((tm, d), lambda i: (i, 0)), pl.BlockSpec((1, d), lambda i: (0, 0)), VMEM_FULL],
        out_specs=pl.BlockSpec((tm, nc * ncol), lambda i: (i, 0)), compiler_params=_params("arbitrary"),
    )(x, norm, w)


def _proj_bwd(x, norm, w, dpa, dpb, dpc, dres):
    t, d = x.shape
    nc, _, ncol = w.shape
    tm = TOKEN_TILE
    nt = t // tm

    def body(x_ref, n_ref, w_ref, dpa_ref, dpb_ref, dpc_ref, dres_ref, dx_ref, dn_ref, dw_ref, dh_ref):
        c, i = pl.program_id(0), pl.program_id(1)
        rows = pl.ds(pl.multiple_of(i * tm, tm), tm)
        xv = x_ref[...]
        gain = n_ref[...]
        h = (xv * _rms(xv) * gain).astype(BF16)
        dpv = (dpa_ref[...] + dpb_ref[...] + dpc_ref[...]).astype(BF16)
        dw_c = _dot_tn(h, dpv)
        dh_c = _dot_nt(dpv, w_ref[0])

        @pl.when(i == 0)
        def _():
            dw_ref[0] = dw_c

        @pl.when(i > 0)
        def _():
            dw_ref[0] += dw_c

        @pl.when(c == 0)
        def _():
            dh_ref[rows, :] = dh_c

        @pl.when(c > 0)
        def _():
            dh_ref[rows, :] += dh_c

        @pl.when(c == nc - 1)
        def _():
            dx, dn = _rmsnorm_bwd(xv, gain, dh_ref[rows, :])
            dx_ref[...] = dx + dres_ref[...]

            @pl.when(i == 0)
            def _():
                dn_ref[...] = dn

            @pl.when(i > 0)
            def _():
                dn_ref[...] += dn

    tile = pl.BlockSpec((tm, d), lambda c, i: (i, 0))
    row = pl.BlockSpec((1, d), lambda c, i: (0, 0))
    wcol = pl.BlockSpec((1, d, ncol), lambda c, i: (c, 0, 0))
    ptile = pl.BlockSpec((tm, ncol), lambda c, i: (i, c))
    last = pl.BlockSpec((tm, d), lambda c, i: (jnp.where(c == nc - 1, i, 0), 0))
    return pl.pallas_call(
        body, name="proj_bwd", grid=(nc, nt),
        out_shape=(jax.ShapeDtypeStruct((t, d), F32), jax.ShapeDtypeStruct((1, d), F32),
                   jax.ShapeDtypeStruct(w.shape, F32)),
        in_specs=[tile, row, wcol, ptile, ptile, ptile, tile],
        out_specs=(last, row, wcol),
        scratch_shapes=[pltpu.VMEM((t, d), F32)],
        compiler_params=_params("arbitrary", "arbitrary"),
    )(x, norm, w, dpa, dpb, dpc, dres)


def _mixout_fwd(x, att, opg, gate, w):
    t, d = x.shape
    half = att.shape[1]
    tm = TOKEN_TILE

    def body(x_ref, att_ref, opg_ref, g_ref, w_ref, o_ref):
        mix = jnp.concatenate([att_ref[...], opg_ref[...] * g_ref[...]], axis=-1).astype(BF16)
        o_ref[...] = x_ref[...] + jnp.dot(mix, w_ref[...], preferred_element_type=F32)

    tile = pl.BlockSpec((tm, d), lambda i: (i, 0))
    htile = pl.BlockSpec((tm, half), lambda i: (i, 0))
    return pl.pallas_call(
        body, name="mixout_fwd", grid=(t // tm,), out_shape=jax.ShapeDtypeStruct((t, d), F32),
        in_specs=[tile, htile, htile, htile, VMEM_FULL], out_specs=tile, compiler_params=_params("arbitrary"),
    )(x, att, opg, gate, w)


def _mixout_bwd(att, opg, gate, w, dy, dep):
    t, half = att.shape
    d = dy.shape[1]
    tm = TOKEN_TILE

    def body(att_ref, opg_ref, g_ref, w_ref, dy_ref, dep_ref, datt_ref, dopg_ref, dg_ref, dw_ref):
        i = pl.program_id(0)
        opg_v, g_v = opg_ref[...], g_ref[...]
        mix = jnp.concatenate([att_ref[...], opg_v * g_v], axis=-1).astype(BF16)
        dyb = dy_ref[...].astype(BF16)
        dmix = _dot_nt(dyb, w_ref[...])
        dw = _dot_tn(mix, dyb)
        datt_ref[...] = dmix[:, :half]
        drw = dmix[:, half:]
        dopg_ref[...] = drw * g_v
        dg_ref[...] = drw * opg_v

        @pl.when(i == 0)
        def _():
            dw_ref[...] = dw

        @pl.when(i > 0)
        def _():
            dw_ref[...] += dw

    tile = pl.BlockSpec((tm, d), lambda i: (i, 0))
    htile = pl.BlockSpec((tm, half), lambda i: (i, 0))
    hshape = jax.ShapeDtypeStruct((t, half), F32)
    return pl.pallas_call(
        body, name="mixout_bwd", grid=(t // tm,),
        out_shape=(hshape, hshape, hshape, jax.ShapeDtypeStruct(w.shape, F32)),
        in_specs=[htile, htile, htile, VMEM_FULL, tile, ANY],
        out_specs=(htile, htile, htile, pl.BlockSpec(w.shape, lambda i: (0, 0))),
        compiler_params=_params("arbitrary"),
    )(att, opg, gate, w, dy, dep)


def _loss_head(y, target):
    t, d = y.shape
    tm = TOKEN_TILE

    def body(y_ref, t_ref, dy_ref, loss_ref):
        i = pl.program_id(0)
        err = y_ref[...] - t_ref[...]
        dy_ref[...] = err * (1.0 / d)
        part = 0.5 * jnp.sum(jnp.mean(err * err, axis=-1, keepdims=True), axis=0, keepdims=True)

        @pl.when(i == 0)
        def _():
            loss_ref[...] = jnp.zeros_like(loss_ref)

        loss_ref[...] += jnp.broadcast_to(part, loss_ref.shape)

    tile = pl.BlockSpec((tm, d), lambda i: (i, 0))
    return pl.pallas_call(
        body, name="loss_head", grid=(t // tm,),
        out_shape=(jax.ShapeDtypeStruct((t, d), F32), jax.ShapeDtypeStruct((1, 128), F32)),
        in_specs=[tile, tile], out_specs=(tile, pl.BlockSpec((1, 128), lambda i: (0, 0))),
        compiler_params=_params("arbitrary"),
    )(y, target)


def _att_block(q, kp, kc, vp, vc, qn, kn, has_prev):
    blk = q.shape[1]

    def hn(v, gain):
        return v * _rms(v) * gain

    qh, khp, khc = hn(q, qn), hn(kp, kn), hn(kc, kn)
    scale = HEAD_DIM ** -0.5
    sp = _bmm_nt(qh, khp) * scale
    sc = _bmm_nt(qh, khc) * scale
    qi = lax.broadcasted_iota(jnp.int32, (blk, blk), 0)
    kj = lax.broadcasted_iota(jnp.int32, (blk, blk), 1)
    sp = jnp.where((kj >= qi) & has_prev, sp, NEG_INF)
    sc = jnp.where(kj <= qi, sc, NEG_INF)
    m = lax.stop_gradient(jnp.maximum(jnp.max(sp, axis=-1, keepdims=True), jnp.max(sc, axis=-1, keepdims=True)))
    pp, pc = jnp.exp(sp - m), jnp.exp(sc - m)
    den = jnp.sum(pp, axis=-1, keepdims=True) + jnp.sum(pc, axis=-1, keepdims=True)
    o = (_bmm_nn(pp, vp) + _bmm_nn(pc, vc)) / den
    return o, m + jnp.log(den)


def _att_specs(g, length, gt):
    blk = ATT_BLOCK
    cur = pl.BlockSpec((gt, blk, HEAD_DIM), lambda gi, n: (gi, n, 0))
    prev = pl.BlockSpec((gt, blk, HEAD_DIM), lambda gi, n: (gi, jnp.maximum(n - 1, 0), 0))
    gain = pl.BlockSpec((1, 1, HEAD_DIM), lambda gi, n: (0, 0, 0))
    col = pl.BlockSpec((gt, blk, 1), lambda gi, n: (gi, n, 0))
    return cur, prev, gain, col


def _att_fwd(q, k, v, qn, kn, name):
    g, length, dh = q.shape
    gt = g // 8
    cur, prev, gain, col = _att_specs(g, length, gt)

    def body(q_ref, kp_ref, kc_ref, vp_ref, vc_ref, qn_ref, kn_ref, o_ref, lse_ref):
        o, lse = _att_block(q_ref[...], kp_ref[...], kc_ref[...], vp_ref[...], vc_ref[...],
                            qn_ref[...], kn_ref[...], pl.program_id(1) > 0)
        o_ref[...] = o
        lse_ref[...] = lse

    return pl.pallas_call(
        body, name=name, grid=(g // gt, length // ATT_BLOCK),
        out_shape=(jax.ShapeDtypeStruct(q.shape, F32), jax.ShapeDtypeStruct((g, length, 1), F32)),
        in_specs=[cur, prev, cur, prev, cur, gain, gain], out_specs=(cur, col),
        compiler_params=_params("arbitrary", "arbitrary"),
    )(q, k, k, v, v, qn, kn)


def _att_bwd(q, k, v, qn, kn, do, dlse, dqn0, dkn0, name):
    g, length, dh = q.shape
    gt = g // 8
    blk = ATT_BLOCK
    cur, prev, gain, col = _att_specs(g, length, gt)
    whole = pl.BlockSpec((gt, length, dh), lambda gi, n: (gi, 0, 0))

    def body(q_ref, kp_ref, kc_ref, vp_ref, vc_ref, qn_ref, kn_ref, do_ref, dl_ref, dqn0_ref, dkn0_ref,
             dq_ref, dk_ref, dv_ref, dqn_ref, dkn_ref):
        gi, n = pl.program_id(0), pl.program_id(1)
        has_prev = n > 0
        fn = functools.partial(_att_block, has_prev=has_prev)
        _, vjp = jax.vjp(fn, q_ref[...], kp_ref[...], kc_ref[...], vp_ref[...], vc_ref[...], qn_ref[...], kn_ref[...])
        dq, dkp, dkc, dvp, dvc, dqn, dkn = vjp((do_ref[...], dl_ref[...]))
        dq_ref[...] = dq
        here = pl.ds(pl.multiple_of(n * blk, blk), blk)
        dk_ref[:, here, :] = dkc
        dv_ref[:, here, :] = dvc

        @pl.when(has_prev)
        def _():
            before = pl.ds(pl.multiple_of((n - 1) * blk, blk), blk)
            dk_ref[:, before, :] += dkp
            dv_ref[:, before, :] += dvp

        @pl.when((gi == 0) & (n == 0))
        def _():
            dqn_ref[...] = dqn0_ref[...]
            dkn_ref[...] = dkn0_ref[...]

        dqn_ref[...] += dqn
        dkn_ref[...] += dkn

    gshape = jax.ShapeDtypeStruct((1, 1, dh), F32)
    return pl.pallas_call(
        body, name=name, grid=(g // gt, length // blk),
        out_shape=(jax.ShapeDtypeStruct(q.shape, F32),) * 3 + (gshape, gshape),
        in_specs=[cur, prev, cur, prev, cur, gain, gain, cur, col, gain, gain],
        out_specs=(cur, whole, whole, gain, gain),
        compiler_params=_params("arbitrary", "arbitrary"),
    )(q, k, k, v, v, qn, kn, do, dlse, dqn0, dkn0)


def _merge_fn(o1, o2, o3, l1, l2, l3):
    m = lax.stop_gradient(jnp.maximum(jnp.maximum(l1, l2), l3))
    e1, e2, e3 = jnp.exp(l1 - m), jnp.exp(l2 - m), jnp.exp(l3 - m)
    return (e1 * o1 + e2 * o2 + e3 * o3) / (e1 + e2 + e3)


def _merge_specs(h, tm, dh):
    return pl.BlockSpec((h, tm, dh), lambda i: (0, i, 0)), pl.BlockSpec((h, tm, 1), lambda i: (0, i, 0))


MERGE_TILE = 128


def _merge_fwd(os, ls):
    h, t, dh = os[0].shape
    tm = MERGE_TILE
    wide, col = _merge_specs(h, tm, dh)

    def body(o1, o2, o3, l1, l2, l3, out):
        out[...] = _merge_fn(o1[...], o2[...], o3[...], l1[...], l2[...], l3[...])

    return pl.pallas_call(
        body, name="merge_fwd", grid=(t // tm,), out_shape=jax.ShapeDtypeStruct(os[0].shape, F32),
        in_specs=[wide] * 3 + [col] * 3, out_specs=wide, compiler_params=_params("arbitrary"),
    )(*os, *ls)


def _merge_bwd(os, ls, do):
    h, t, dh = os[0].shape
    tm = MERGE_TILE
    wide, col = _merge_specs(h, tm, dh)

    def body(o1, o2, o3, l1, l2, l3, do_ref, d1, d2, d3, e1, e2, e3):
        _, vjp = jax.vjp(_merge_fn, o1[...], o2[...], o3[...], l1[...], l2[...], l3[...])
        outs = vjp(do_ref[...])
        for ref, val in zip((d1, d2, d3, e1, e2, e3), outs):
            ref[...] = val

    oshape = jax.ShapeDtypeStruct(os[0].shape, F32)
    lshape = jax.ShapeDtypeStruct(ls[0].shape, F32)
    return pl.pallas_call(
        body, name="merge_bwd", grid=(t // tm,), out_shape=(oshape,) * 3 + (lshape,) * 3,
        in_specs=[wide] * 3 + [col] * 3 + [wide], out_specs=(wide,) * 3 + (col,) * 3,
        compiler_params=_params("arbitrary"),
    )(*os, *ls, do)


RWKV_VEC = ("mu_r", "mu_k", "mu_v", "mu_w", "mu_a", "mu_g", "w0", "a0", "k_k", "k_a")
RWKV_MAT = ("w1", "w2", "a1", "a2", "g1", "g2")


def _rwkv_pre_fn(cur, prev, vec, w1, w2, a1, a2, g1, g2):
    c = cur.shape[1] // 4
    mu_r, mu_k, mu_v, mu_w, mu_a, mu_g, w0, a0, k_k, k_a = (vec[j:j + 1] for j in range(10))

    def lerp(j, mu):
        xc, xp = cur[:, j * c:(j + 1) * c], prev[:, j * c:(j + 1) * c]
        return xc + (xp - xc) * mu

    r, k, v = lerp(0, mu_r), lerp(1, mu_k), lerp(2, mu_v)
    cw, ca, cg = lerp(3, mu_w), lerp(3, mu_a), lerp(3, mu_g)
    z = w0 + _mm(jnp.tanh(_mm(cw, w1)), w2)
    w_log = jnp.minimum(z, 0.0) - jnp.log(1.0 + jnp.exp(-jnp.abs(z))) - 0.5
    lw = -jnp.exp(w_log)
    a = _sigmoid(a0 + _mm(_mm(ca, a1), a2))
    gate = _mm(_sigmoid(_mm(cg, g1)), g2)
    kkraw = k * k_k
    kmod = k * (1.0 + (a - 1.0) * k_a)
    return r, lw, kmod, v, kkraw, a, gate


def _rwkv_pre_specs(c, mats):
    tm = TOKEN_TILE
    wide = pl.BlockSpec((tm, 4 * c), lambda i: (i, 0))
    one = pl.BlockSpec((tm, c), lambda i: (i, 0))
    vec = pl.BlockSpec((10, c), lambda i: (0, 0))
    mspecs = [pl.BlockSpec(m.shape, lambda i: (0, 0)) for m in mats]
    return wide, one, vec, mspecs


def _rwkv_pre_fwd(cur, prev, vec, mats):
    t, c4 = cur.shape
    c = c4 // 4
    wide, one, vspec, mspecs = _rwkv_pre_specs(c, mats)

    def body(cur_ref, prev_ref, vec_ref, *rest):
        mrefs, outs = rest[:6], rest[6:]
        vals = _rwkv_pre_fn(cur_ref[...], prev_ref[...], vec_ref[...], *(m[...] for m in mrefs))
        for ref, val in zip(outs, vals):
            ref[...] = val

    return pl.pallas_call(
        body, name="rwkv_pre_fwd", grid=(t // TOKEN_TILE,), out_shape=(jax.ShapeDtypeStruct((t, c), F32),) * 7,
        in_specs=[wide, wide, vspec] + mspecs, out_specs=(one,) * 7, compiler_params=_params("arbitrary"),
    )(cur, prev, vec, *mats)


def _rwkv_pre_bwd(cur, prev, vec, mats, cts):
    t, c4 = cur.shape
    c = c4 // 4
    wide, one, vspec, mspecs = _rwkv_pre_specs(c, mats)

    def body(cur_ref, prev_ref, vec_ref, *rest):
        mrefs, ctrefs, outs = rest[:6], rest[6:13], rest[13:]
        _, vjp = jax.vjp(_rwkv_pre_fn, cur_ref[...], prev_ref[...], vec_ref[...], *(m[...] for m in mrefs))
        grads = vjp(tuple(r[...] for r in ctrefs))
        outs[0][...] = grads[0]
        outs[1][...] = grads[1]
        first = pl.program_id(0) == 0

        @pl.when(first)
        def _():
            for ref, val in zip(outs[2:], grads[2:]):
                ref[...] = val

        @pl.when(jnp.logical_not(first))
        def _():
            for ref, val in zip(outs[2:], grads[2:]):
                ref[...] += val

    wshape = jax.ShapeDtypeStruct(cur.shape, F32)
    return pl.pallas_call(
        body, name="rwkv_pre_bwd", grid=(t // TOKEN_TILE,),
        out_shape=(wshape, wshape, jax.ShapeDtypeStruct(vec.shape, F32)) + tuple(jax.ShapeDtypeStruct(m.shape, F32) for m in mats),
        in_specs=[wide, wide, vspec] + mspecs + [one] * 7, out_specs=(wide, wide, vspec) + tuple(mspecs),
        compiler_params=_params("arbitrary"),
    )(cur, prev, vec, *mats, *cts)


def _scan_chunk_fn(h0, r, lw, k, v, kkraw, a, rk, lnw, lnb):
    n = r.shape[1]
    nrm = jnp.sqrt(jnp.sum(kkraw * kkraw, axis=-1, keepdims=True))
    kk = kkraw / jnp.maximum(nrm, 1e-12)
    av, bv = -kk, kk * a
    ti = lax.broadcasted_iota(jnp.int32, (n, n), 0)
    si = lax.broadcasted_iota(jnp.int32, (n, n), 1)
    incl, strict = ti >= si, ti > si
    ones = jnp.broadcast_to(incl.astype(F32)[None], (r.shape[0], n, n))
    cum = _hdot(ones, lw, 2, 1)
    at, rt = av * jnp.exp(cum - lw), r * jnp.exp(cum)
    inv = jnp.exp(-cum)
    bt, kt = bv * inv, k * inv
    lab = jnp.where(strict, _hdot(at, bt, 2, 2), 0.0)
    lak = jnp.where(strict, _hdot(at, kt, 2, 2), 0.0)
    rb = jnp.where(incl, _hdot(rt, bt, 2, 2), 0.0)
    rkm = jnp.where(incl, _hdot(rt, kt, 2, 2), 0.0)
    u = _hdot(at, h0, 2, 1) + _hdot(lak, v, 2, 1)
    p = lab
    m = 1
    while m < n:
        u = u + _hdot(p, u, 2, 1)
        m *= 2
        if m < n:
            p = _hdot(p, p, 2, 1)
    y = _hdot(rt, h0, 2, 1) + _hdot(rb, u, 2, 1) + _hdot(rkm, v, 2, 1)
    last = jnp.exp(jnp.sum(lw, axis=1, keepdims=True))
    h1 = jnp.swapaxes(last, 1, 2) * (h0 + _hdot(bt, u, 1, 1) + _hdot(kt, v, 1, 1))
    mean = jnp.mean(y, axis=-1, keepdims=True)
    yc = y - mean
    var = jnp.mean(yc * yc, axis=-1, keepdims=True)
    yn = yc * lax.rsqrt(var + GN_EPS) * lnw + lnb
    bonus = jnp.sum(r * k * rk, axis=-1, keepdims=True) * v
    return yn + bonus, h1


def _scan_specs(h, t, dh, rev):
    n = SCAN_CHUNK
    nc = t // n
    pos = (lambda c: (0, nc - 1 - c, 0)) if rev else (lambda c: (0, c, 0))
    st = (lambda c: (nc - 1 - c, 0, 0, 0)) if rev else (lambda c: (c, 0, 0, 0))
    seq = pl.BlockSpec((h, n, dh), pos)
    par = pl.BlockSpec((h, 1, dh), lambda c: (0, 0, 0))
    state = pl.BlockSpec((1, h, dh, dh), st)
    return seq, par, state


def _scan_fwd(seqs, pars):
    h, t, dh = seqs[0].shape
    nc = t // SCAN_CHUNK
    seq, par, state = _scan_specs(h, t, dh, False)

    def body(r, lw, k, v, kkraw, a, rk, lnw, lnb, o_ref, st_ref, h_ref):
        @pl.when(pl.program_id(0) == 0)
        def _():
            h_ref[...] = jnp.zeros_like(h_ref)

        h0 = h_ref[...]
        st_ref[0] = h0
        o, h1 = _scan_chunk_fn(h0, r[...], lw[...], k[...], v[...], kkraw[...], a[...], rk[...], lnw[...], lnb[...])
        o_ref[...] = o
        h_ref[...] = h1

    return pl.pallas_call(
        body, name="rwkv_scan_fwd", grid=(nc,),
        out_shape=(jax.ShapeDtypeStruct((h, t, dh), F32), jax.ShapeDtypeStruct((nc, h, dh, dh), F32)),
        in_specs=[seq] * 6 + [par] * 3, out_specs=(seq, state),
        scratch_shapes=[pltpu.VMEM((h, dh, dh), F32)], compiler_params=_params("arbitrary"),
    )(*seqs, *pars)


def _scan_bwd(seqs, pars, states, do):
    h, t, dh = seqs[0].shape
    nc = t // SCAN_CHUNK
    seq, par, state = _scan_specs(h, t, dh, True)

    def body(r, lw, k, v, kkraw, a, rk, lnw, lnb, st_ref, do_ref, *rest):
        douts, dpars, dh_ref = rest[:6], rest[6:9], rest[9]
        first = pl.program_id(0) == 0

        @pl.when(first)
        def _():
            dh_ref[...] = jnp.zeros_like(dh_ref)

        _, vjp = jax.vjp(_scan_chunk_fn, st_ref[0], r[...], lw[...], k[...], v[...], kkraw[...], a[...],
                         rk[...], lnw[...], lnb[...])
        grads = vjp((do_ref[...], dh_ref[...]))
        dh_ref[...] = grads[0]
        for ref, val in zip(douts, grads[1:7]):
            ref[...] = val

        @pl.when(first)
        def _():
            for ref, val in zip(dpars, grads[7:]):
                ref[...] = val

        @pl.when(jnp.logical_not(first))
        def _():
            for ref, val in zip(dpars, grads[7:]):
                ref[...] += val

    sshape = jax.ShapeDtypeStruct((h, t, dh), F32)
    pshape = jax.ShapeDtypeStruct((h, 1, dh), F32)
    return pl.pallas_call(
        body, name="rwkv_scan_bwd", grid=(nc,), out_shape=(sshape,) * 6 + (pshape,) * 3,
        in_specs=[seq] * 6 + [par] * 3 + [state, seq], out_specs=(seq,) * 6 + (par,) * 3,
        scratch_shapes=[pltpu.VMEM((h, dh, dh), F32)], compiler_params=_params("arbitrary"),
    )(*seqs, *pars, states, do)


def _heads(x):
    return x.reshape(x.shape[0], -1, HEAD_DIM).transpose(1, 0, 2)


def _unheads(x):
    return x.transpose(1, 0, 2).reshape(x.shape[1], -1)


def _to_sub(x, dil):
    h, t, d = x.shape
    return x.reshape(h, t // dil, dil, d).transpose(0, 2, 1, 3).reshape(h * dil, t // dil, d)


def _from_sub(x, dil):
    g, length, d = x.shape
    return x.reshape(g // dil, dil, length, d).transpose(0, 2, 1, 3).reshape(g // dil, length * dil, d)


def _local_step(x, target, w, ex):
    w = dict(w)
    c = w["mu_r"].shape[-1]
    qn, kn = w["q_norm"].reshape(1, 1, HEAD_DIM), w["k_norm"].reshape(1, 1, HEAD_DIM)
    vec = jnp.concatenate([w[n].reshape(1, c) for n in RWKV_VEC], axis=0)
    pars = [w[n].reshape(-1, 1, HEAD_DIM) for n in ("r_k", "ln_x_w", "ln_x_b")]
    no_dep = jnp.zeros(DEP_SHAPE, F32)

    x1 = _ffn_fwd(x, w["ffn1_norm"], w["ffn1_w_gate"], w["ffn1_w_up"], w["ffn1_w_down"], ex.first_dep, "ffn1_fwd")
    w.update(ex.mix_weights((x1,)))
    att_w = w["w_in"].shape[0] * w["w_in"].shape[2] - 4 * c
    mats = [w[n] for n in RWKV_MAT]
    proj = _proj_fwd(x1, w["mix_norm"], w["w_in"])
    hw = att_w // 3
    qkv = [_heads(proj[:, j * hw:(j + 1) * hw]) for j in range(3)]
    subs = [[_to_sub(a, dil) for a in qkv] for dil in DILATIONS]
    outs = [_att_fwd(*s, qn, kn, f"att_fwd_d{dil}") for s, dil in zip(subs, DILATIONS)]
    os_ = [_from_sub(o, dil) for (o, _), dil in zip(outs, DILATIONS)]
    ls_ = [_from_sub(l, dil) for (_, l), dil in zip(outs, DILATIONS)]
    att = _unheads(_merge_fwd(os_, ls_))
    cur = proj[:, att_w:]
    prev = jnp.concatenate([jnp.zeros_like(cur[:1]), cur[:-1]], axis=0)
    pre = _rwkv_pre_fwd(cur, prev, vec, mats)
    seqs = [_heads(a) for a in pre[:6]]
    gate = pre[6]
    opg_h, states = _scan_fwd(seqs, pars)
    opg = _unheads(opg_h)
    w.update(ex.out_weights((att, opg)))
    x2 = _mixout_fwd(x1, att, opg, gate, w["w_out"])
    x3 = _ffn_fwd(x2, w["ffn2_norm"], w["ffn2_w_gate"], w["ffn2_w_up"], w["ffn2_w_down"], no_dep, "ffn2_fwd")
    dy, loss = _loss_head(x3, target)

    g = {}
    dx2, g["ffn2_norm"], g["ffn2_w_gate"], g["ffn2_w_up"], g["ffn2_w_down"] = _ffn_bwd(
        x2, w["ffn2_norm"], w["ffn2_w_gate"], w["ffn2_w_up"], w["ffn2_w_down"], dy, no_dep, "ffn2_bwd")
    dep = ex.send_ffn2({n: g[n] for n in ("ffn2_w_gate", "ffn2_w_up", "ffn2_w_down")})
    datt, dopg, dgate, g["w_out"] = _mixout_bwd(att, opg, gate, w["w_out"], dx2, dep)
    dscan = _scan_bwd(seqs, pars, states, _heads(dopg))
    for n, d in zip(("r_k", "ln_x_w", "ln_x_b"), dscan[6:]):
        g[n] = d
    dpre = _rwkv_pre_bwd(cur, prev, vec, mats, [_unheads(d) for d in dscan[:6]] + [dgate])
    dcur, dprev, dvec = dpre[:3]
    for n, d in zip(RWKV_MAT, dpre[3:]):
        g[n] = d
    for j, n in enumerate(RWKV_VEC):
        g[n] = dvec[j:j + 1]
    dmerge = _merge_bwd(os_, ls_, _heads(datt))
    dqn = dkn = jnp.zeros((1, 1, HEAD_DIM), F32)
    dqkv = []
    for j, dil in enumerate(DILATIONS):
        dq, dk, dv, dqn, dkn = _att_bwd(*subs[j], qn, kn, _to_sub(dmerge[j], dil), _to_sub(dmerge[3 + j], dil),
                                        dqn, dkn, f"att_bwd_d{dil}")
        dqkv.append([_unheads(_from_sub(a, dil)) for a in (dq, dk, dv)])
    g["q_norm"], g["k_norm"] = dqn, dkn
    dshift = jnp.concatenate([dprev[1:], jnp.zeros_like(dprev[:1])], axis=0)
    dps = [jnp.concatenate(dqkv[j] + [tail], axis=1) for j, tail in enumerate((dcur, dshift, jnp.zeros_like(dcur)))]
    dx1, g["mix_norm"], g["w_in"] = _proj_bwd(x1, w["mix_norm"], w["w_in"], *dps, dx2)
    dep = ex.send_mix({n: g[n] for n in ("w_in", "w_out") + RWKV_MAT}, (dx1,))
    dx, g["ffn1_norm"], g["ffn1_w_gate"], g["ffn1_w_up"], g["ffn1_w_down"] = _ffn_bwd(
        x, w["ffn1_norm"], w["ffn1_w_gate"], w["ffn1_w_up"], w["ffn1_w_down"], dx1, dep, "ffn1_bwd")
    return loss, dx, g


N_SHARDS = 4


def _place():
    return lax.axis_index("x"), lax.axis_index("y"), lax.axis_index("c")


def _chip_peers(x, y):
    return [(1 - x, y), (x, 1 - y), (1 - x, 1 - y)]


def _gather_xy(shards):
    n = len(shards)

    def body(*refs):
        ins, outs = refs[:n], refs[n:2 * n]
        send_sems, recv_sems, local_sems = refs[2 * n:]
        x, y, c = _place()
        me = 2 * x + y
        copies = []
        for i in range(n):
            own = pltpu.make_async_copy(ins[i], outs[i].at[me], local_sems.at[i])
            own.start()
            copies.append(own)
            for k, (px, py) in enumerate(_chip_peers(x, y)):
                cp = pltpu.make_async_remote_copy(
                    src_ref=ins[i], dst_ref=outs[i].at[me], send_sem=send_sems.at[i, k], recv_sem=recv_sems.at[i, k],
                    device_id=(px, py, c), device_id_type=MESH)
                cp.start()
                copies.append(cp)
        for cp in copies:
            cp.wait()

    return pl.pallas_call(
        body, name="gather_weights",
        out_shape=tuple(jax.ShapeDtypeStruct((N_SHARDS,) + s.shape, s.dtype) for s in shards),
        in_specs=[ANY] * n, out_specs=(ANY,) * n,
        scratch_shapes=[pltpu.SemaphoreType.DMA((n, 3)), pltpu.SemaphoreType.DMA((n, 3)), pltpu.SemaphoreType.DMA((n,))],
    )(*shards)


def _scatter_partials(parts):
    n = len(parts)

    def body(*refs):
        ins, outs = refs[:n], refs[n:2 * n]
        send_sems, recv_sems = refs[2 * n:]
        x, y, c = _place()
        copies = []
        for i in range(n):
            for k, (px, py) in enumerate(_chip_peers(x, y)):
                cp = pltpu.make_async_remote_copy(
                    src_ref=ins[i].at[2 * px + py], dst_ref=outs[i].at[k], send_sem=send_sems.at[i, k],
                    recv_sem=recv_sems.at[i, k], device_id=(px, py, c), device_id_type=MESH)
                cp.start()
                copies.append(cp)
        for cp in copies:
            cp.wait()

    return pl.pallas_call(
        body, name="scatter_partials",
        out_shape=tuple(jax.ShapeDtypeStruct((3,) + p.shape[1:], p.dtype) for p in parts),
        in_specs=[ANY] * n, out_specs=(ANY,) * n,
        scratch_shapes=[pltpu.SemaphoreType.DMA((n, 3)), pltpu.SemaphoreType.DMA((n, 3))],
    )(*parts)


HBM = pl.BlockSpec(memory_space=pltpu.HBM)
SEM = pl.BlockSpec(memory_space=pltpu.SEMAPHORE)
DEP_SHAPE = (8, 128)


def _gather_views(i, srcs, lands, k, px, py, me):
    return (srcs[i], lands[i].at[me]), (srcs[i], lands[i].at[2 * px + py])


def _scatter_views(i, srcs, lands, k, px, py, me):
    return (srcs[i].at[2 * px + py], lands[i].at[k]), (srcs[i].at[me], lands[i].at[k])


def _push_start(srcs, land_shapes, views, after, name):
    n = len(srcs)

    def body(*refs):
        src_refs, land_refs = refs[:n], refs[n:2 * n]
        send_sems, recv_sems = refs[2 * n + 1:2 * n + 3]
        token = refs[4 * n + 3]
        x, y, c = _place()
        me = 2 * x + y
        for i in range(n):
            for k, (px, py) in enumerate(_chip_peers(x, y)):
                (src, dst), _ = views(i, src_refs, land_refs, k, px, py, me)
                pltpu.make_async_remote_copy(
                    src_ref=src, dst_ref=dst, send_sem=send_sems.at[3 * i + k], recv_sem=recv_sems.at[3 * i + k],
                    device_id=(px, py, c), device_id_type=MESH).start()
        token[...] = jnp.zeros_like(token)

    sems = pltpu.SemaphoreType.DMA((3 * n,))
    lands = [pltpu.with_memory_space_constraint(lax.empty(s.shape, s.dtype), pltpu.HBM) for s in land_shapes]
    srcs = [pltpu.with_memory_space_constraint(s, pltpu.HBM) for s in srcs]
    outs = pl.pallas_call(
        body, name=name,
        out_shape=(sems, sems, *[pltpu.HBM(s.shape, s.dtype) for s in srcs], *[pltpu.HBM(s.shape, s.dtype) for s in land_shapes],
                   jax.ShapeDtypeStruct(DEP_SHAPE, F32)),
        in_specs=[HBM] * (2 * n) + [ANY], out_specs=(SEM, SEM, *[HBM] * (2 * n), VMEM_FULL),
        input_output_aliases={i: 2 + i for i in range(2 * n)},
        compiler_params=pltpu.CompilerParams(has_side_effects=pltpu.SideEffectType.DATAFLOW_SIDE_EFFECTING),
    )(*srcs, *lands, after)
    return outs[0], outs[1], outs[2:2 + n], outs[2 + n:2 + 2 * n], outs[2 + 2 * n]


def _push_wait(started, views, own_slot, after, name):
    send_sems, recv_sems, srcs, lands, _ = started
    n = len(srcs)

    def body(*refs):
        src_refs, land_refs = refs[:n], refs[n:2 * n]
        send_sems, recv_sems = refs[2 * n:2 * n + 2]
        local_sems = refs[-1]
        x, y, c = _place()
        me = 2 * x + y
        own = [pltpu.make_async_copy(src_refs[i], land_refs[i].at[me], local_sems.at[i]) for i in range(n)] if own_slot else []
        for cp in own:
            cp.start()
        for cp in own:
            cp.wait()
        for i in range(n):
            for k, (px, py) in enumerate(_chip_peers(x, y)):
                _, (src, dst) = views(i, src_refs, land_refs, k, px, py, me)
                landing = pltpu.make_async_remote_copy(
                    src_ref=src, dst_ref=dst, send_sem=send_sems.at[3 * i + k], recv_sem=recv_sems.at[3 * i + k],
                    device_id=(px, py, c), device_id_type=MESH)
                landing.wait_send()
                landing.wait_recv()

    outs = pl.pallas_call(
        body, name=name,
        out_shape=tuple(pltpu.HBM(s.shape, s.dtype) for s in (*srcs, *lands)),
        in_specs=[HBM] * (2 * n) + [SEM, SEM] + [ANY] * len(after), out_specs=(HBM,) * (2 * n),
        input_output_aliases={i: i for i in range(2 * n)},
        scratch_shapes=[pltpu.SemaphoreType.DMA((n,))],
        compiler_params=pltpu.CompilerParams(has_side_effects=pltpu.SideEffectType.DATAFLOW_SIDE_EFFECTING),
    )(*srcs, *lands, send_sems, recv_sems, *after)
    return outs[n:]


def _sibling_swap(arrays):
    n = len(arrays)

    def body(*refs):
        ins, outs = refs[:n], refs[n:2 * n]
        send_sems, recv_sems = refs[2 * n:]
        x, y, c = _place()
        copies = []
        for i in range(n):
            cp = pltpu.make_async_remote_copy(
                src_ref=ins[i], dst_ref=outs[i], send_sem=send_sems.at[i], recv_sem=recv_sems.at[i],
                device_id=(x, y, 1 - c), device_id_type=MESH)
            cp.start()
            copies.append(cp)
        for cp in copies:
            cp.wait()

    return pl.pallas_call(
        body, name="sibling_swap",
        out_shape=tuple(jax.ShapeDtypeStruct(a.shape, a.dtype) for a in arrays),
        in_specs=[ANY] * n, out_specs=(ANY,) * n,
        scratch_shapes=[pltpu.SemaphoreType.DMA((n,)), pltpu.SemaphoreType.DMA((n,))],
    )(*arrays)


N_DEV = 8


def _allreduce_small(pack):
    def body(in_ref, out_ref, buf, send_sems, recv_sems):
        x, y, c = _place()
        me = 4 * x + 2 * y + c
        buf[me] = in_ref[...]

        def copy(j, slot):
            px, py, pc = x ^ (j >> 2), y ^ ((j >> 1) & 1), c ^ (j & 1)
            return pltpu.make_async_remote_copy(
                src_ref=in_ref, dst_ref=buf.at[slot(px, py, pc)], send_sem=send_sems.at[j], recv_sem=recv_sems.at[j],
                device_id=(px, py, pc), device_id_type=MESH)

        for j in range(1, N_DEV):
            copy(j, lambda px, py, pc: me).start()
        for j in range(1, N_DEV):
            landing = copy(j, lambda px, py, pc: 4 * px + 2 * py + pc)
            landing.wait_send()
            landing.wait_recv()
        acc = buf[0]
        for s in range(1, N_DEV):
            acc = acc + buf[s]
        out_ref[...] = acc

    return pl.pallas_call(
        body, name="allreduce_small", out_shape=jax.ShapeDtypeStruct(pack.shape, F32),
        in_specs=[VMEM_FULL], out_specs=VMEM_FULL,
        scratch_shapes=[pltpu.VMEM((N_DEV,) + pack.shape, F32), pltpu.SemaphoreType.DMA((N_DEV,)),
                        pltpu.SemaphoreType.DMA((N_DEV,))],
    )(pack)


ROW_TILE_MAX = 256
BF16_SUBLANES = 16


def _row_tile(rows):
    for tr in range(min(rows, ROW_TILE_MAX), 0, -1):
        if rows % tr == 0 and tr % BF16_SUBLANES == 0:
            return tr
    return rows


def _reduce_own(me, part, recv, name):
    _, r, cols = part.shape
    tr = _row_tile(r)

    def body(me_ref, p_ref, rv_ref, o_ref):
        acc = p_ref[0]
        for k in range(3):
            acc = acc + rv_ref[k].astype(F32)
        o_ref[...] = acc

    return pl.pallas_call(
        body, name=name, out_shape=jax.ShapeDtypeStruct((r, cols), F32),
        grid_spec=pltpu.PrefetchScalarGridSpec(
            num_scalar_prefetch=1, grid=(r // tr,),
            in_specs=[pl.BlockSpec((1, tr, cols), lambda i, me_ref: (me_ref[0], i, 0)),
                      pl.BlockSpec((3, tr, cols), lambda i, me_ref: (0, i, 0))],
            out_specs=pl.BlockSpec((tr, cols), lambda i, me_ref: (i, 0))),
        compiler_params=_params("arbitrary"),
    )(me, part, recv)


def _adamw(w, ga, gb, m, v, name):
    r, cols = w.shape
    tr = _row_tile(r)
    c1 = 1.0 - ADAM_B1 ** ADAM_STEP
    c2 = 1.0 - ADAM_B2 ** ADAM_STEP

    def body(w_ref, ga_ref, gb_ref, m_ref, v_ref, g_out, d_out, m_out, v_out):
        g = ga_ref[...] + gb_ref[...]
        mn = ADAM_B1 * m_ref[...] + (1.0 - ADAM_B1) * g
        vn = ADAM_B2 * v_ref[...] + (1.0 - ADAM_B2) * (g * g)
        g_out[...] = g
        m_out[...] = mn
        v_out[...] = vn
        d_out[...] = -ADAM_LR * ((mn / c1) / (jnp.sqrt(vn / c2) + ADAM_EPS) + ADAM_WD * w_ref[...])

    tile = pl.BlockSpec((tr, cols), lambda i: (i, 0))
    shape = jax.ShapeDtypeStruct((r, cols), F32)
    return pl.pallas_call(
        body, name=name, grid=(r // tr,), out_shape=(shape,) * 4, in_specs=[tile] * 5, out_specs=(tile,) * 4,
        compiler_params=_params("arbitrary"),
    )(w, ga, gb, m, v)


PACK_COLS = 512


def _to_rows(a):
    flat = a.reshape(-1)
    pad = (-flat.shape[0]) % PACK_COLS
    return jnp.pad(flat, (0, pad)).reshape(-1, PACK_COLS)


def _pack(arrays, extra_rows=0):
    rows = [_to_rows(a) for a in arrays]
    n = sum(r.shape[0] for r in rows) + extra_rows
    pad = (-n) % 8
    return jnp.concatenate(rows + [jnp.zeros((extra_rows + pad, PACK_COLS), F32)], axis=0)


def _unpack(pack, like):
    out, at = [], 0
    for a in like:
        n = -(-a.size // PACK_COLS)
        out.append(pack[at:at + n].reshape(-1)[:a.size].reshape(a.shape))
        at += n
    return out


COL_SHARDED = ("ffn1_w_gate", "ffn1_w_up", "w_in", "ffn2_w_gate", "ffn2_w_up", "w2", "a2", "g2")
ROW_SHARDED = ("ffn1_w_down", "ffn2_w_down", "w_out", "w1", "a1", "g1")
CHUNKED = ("ffn1_w_gate", "ffn1_w_up", "ffn1_w_down", "w_in", "ffn2_w_gate", "ffn2_w_up", "ffn2_w_down")
WEIGHTS = ("ffn1_norm", "ffn1_w_gate", "ffn1_w_up", "ffn1_w_down", "mix_norm", "w_in", "q_norm", "k_norm",
           "mu_r", "mu_k", "mu_v", "mu_w", "mu_a", "mu_g", "w0", "w1", "w2", "a0", "a1", "a2", "g1", "g2",
           "k_k", "k_a", "r_k", "ln_x_w", "ln_x_b", "w_out", "ffn2_norm", "ffn2_w_gate", "ffn2_w_up", "ffn2_w_down")


def _full_from_blocks(name, blocks):
    if name in CHUNKED:
        return blocks
    if name in ROW_SHARDED:
        return blocks.reshape(-1, blocks.shape[-1])
    return blocks.transpose(1, 0, 2).reshape(blocks.shape[1], -1)


def _blocks_from_full(name, full):
    if name in CHUNKED:
        return full
    if name in ROW_SHARDED:
        return full.reshape(N_SHARDS, -1, full.shape[-1])
    return full.reshape(full.shape[0], N_SHARDS, -1).transpose(1, 0, 2)


FFN1_GROUP = ("ffn1_w_gate", "ffn1_w_up", "ffn1_w_down")
MIX_GROUP = ("w_in",) + RWKV_MAT
OUT_GROUP = ("w_out", "ffn2_w_gate", "ffn2_w_up", "ffn2_w_down")
FFN2_GROUP = OUT_GROUP[1:]
LATE_GROUP = ("w_in", "w_out") + RWKV_MAT


class _Exchange:
    def __init__(self, given):
        self.given = given
        first = _gather_xy(self._shards(FFN1_GROUP))
        self.first_weights = self._full(FFN1_GROUP, first)
        self.mix = self._gather_start(MIX_GROUP, first[0], "gather_mix_start")
        self.out = self._gather_start(OUT_GROUP, self.mix[4], "gather_out_start")
        self.first_dep = self.out[4]
        self.parts, self.recv = {}, {}

    def _shards(self, names):
        return [self.given[n][0].astype(BF16) for n in names]

    @staticmethod
    def _full(names, blocks):
        out = {}
        for n, b in zip(names, blocks):
            full = _full_from_blocks(n, b)
            out[n] = full.astype(F32) if n in RWKV_MAT else full
        return out

    def _gather_start(self, names, after, name):
        shards = self._shards(names)
        lands = [jax.ShapeDtypeStruct((N_SHARDS,) + s.shape, s.dtype) for s in shards]
        return _push_start(shards, lands, _gather_views, after, name)

    def mix_weights(self, after):
        return self._full(MIX_GROUP, _push_wait(self.mix, _gather_views, True, after, "gather_mix_wait"))

    def out_weights(self, after):
        return self._full(OUT_GROUP, _push_wait(self.out, _gather_views, True, after, "gather_out_wait"))

    def _scatter_start(self, grads, name):
        names = tuple(grads)
        parts = [_blocks_from_full(n, grads[n]) for n in names]
        self.parts.update(zip(names, parts))
        lands = [jax.ShapeDtypeStruct((3,) + p.shape[1:], BF16) for p in parts]
        return _push_start([p.astype(BF16) for p in parts], lands, _scatter_views, parts[0], name)

    def send_ffn2(self, grads):
        self.ffn2 = self._scatter_start(grads, "scatter_ffn2_start")
        return self.ffn2[4]

    def send_mix(self, grads, after):
        self.recv.update(zip(FFN2_GROUP, _push_wait(self.ffn2, _scatter_views, False, after, "scatter_ffn2_wait")))
        self.late = self._scatter_start(grads, "scatter_late_start")
        return self.late[4]

    def finish(self, grads):
        names = tuple(grads)
        parts = [_blocks_from_full(n, grads[n]) for n in names]
        self.parts.update(zip(names, parts))
        got = _scatter_partials([p.astype(BF16) for p in parts])
        self.recv.update(zip(names, got))
        self.recv.update(zip(LATE_GROUP, _push_wait(self.late, _scatter_views, False, (got[0],), "scatter_late_wait")))
        return self.parts, self.recv


def kernel(
        x, ffn1_norm, ffn1_w_gate, ffn1_w_up, ffn1_w_down, mix_norm, w_in, q_norm, k_norm, mu_r, mu_k, mu_v, mu_w,
        mu_a, mu_g, w0, w1, w2, a0, a1, a2, g1, g2, k_k, k_a, r_k, ln_x_w, ln_x_b, w_out, ffn2_norm, ffn2_w_gate,
        ffn2_w_up, ffn2_w_down, loss_target, m_ffn1_norm, m_ffn1_w_gate, m_ffn1_w_up, m_ffn1_w_down, m_mix_norm,
        m_w_in, m_q_norm, m_k_norm, m_mu_r, m_mu_k, m_mu_v, m_mu_w, m_mu_a, m_mu_g, m_w0, m_w1, m_w2, m_a0, m_a1,
        m_a2, m_g1, m_g2, m_k_k, m_k_a, m_r_k, m_ln_x_w, m_ln_x_b, m_w_out, m_ffn2_norm, m_ffn2_w_gate, m_ffn2_w_up,
        m_ffn2_w_down, v_ffn1_norm, v_ffn1_w_gate, v_ffn1_w_up, v_ffn1_w_down, v_mix_norm, v_w_in, v_q_norm, v_k_norm,
        v_mu_r, v_mu_k, v_mu_v, v_mu_w, v_mu_a, v_mu_g, v_w0, v_w1, v_w2, v_a0, v_a1, v_a2, v_g1, v_g2, v_k_k, v_k_a,
        v_r_k, v_ln_x_w, v_ln_x_b, v_w_out, v_ffn2_norm, v_ffn2_w_gate, v_ffn2_w_up, v_ffn2_w_down):
    given = dict(locals())
    sharded = COL_SHARDED + ROW_SHARDED
    sharded = tuple(n for n in WEIGHTS if n in sharded)
    small = tuple(n for n in WEIGHTS if n not in sharded)

    ex = _Exchange(given)
    w = {n: given[n] for n in small}
    w.update(ex.first_weights)
    loss, dx, g = _local_step(x[0], loss_target[0], w, ex)
    parts, recv = ex.finish({n: g[n] for n in FFN1_GROUP})

    me = (2 * lax.axis_index("x") + lax.axis_index("y")).astype(jnp.int32).reshape(1)
    mine = []
    for n in sharded:
        p, rv = parts[n], recv[n]
        p2 = p.reshape(N_SHARDS, -1, p.shape[-1])
        mine.append(_reduce_own(me, p2, rv.reshape(3, -1, rv.shape[-1]), f"reduce_{n}"))
    theirs = _sibling_swap(mine)
    out = {}
    for n, a, b in zip(sharded, mine, theirs):
        shape = given[n].shape
        two_d = (-1, shape[-1])
        res = _adamw(given[n].reshape(two_d), a, b, given["m_" + n].reshape(two_d), given["v_" + n].reshape(two_d), f"adamw_{n}")
        out[n] = [r.reshape(shape) for r in res]

    gpack = _pack([g[n] for n in small], extra_rows=1)
    n_rows = sum(-(-given[n].size // PACK_COLS) for n in small)
    gpack = gpack.at[n_rows, :loss.shape[1]].set(loss[0])
    gsum = _allreduce_small(gpack)
    res = _adamw(_pack([given[n] for n in small], 1), gsum, jnp.zeros_like(gsum), _pack([given["m_" + n] for n in small], 1),
                 _pack([given["v_" + n] for n in small], 1), "adamw_small")
    like = [given[n] for n in small]
    for j, r in enumerate(res):
        for n, a in zip(small, _unpack(r, like)):
            out.setdefault(n, [None] * 4)[j] = a
    total_loss = gsum[n_rows, 0]
    return (total_loss, dx[None], *[out[n][0] for n in WEIGHTS], *[out[n][1] for n in WEIGHTS],
            *[out[n][2] for n in WEIGHTS], *[out[n][3] for n in WEIGHTS])
```

```python
import functools

import jax
import jax.numpy as jnp
from jax import lax
from jax.experimental import pallas as pl
from jax.experimental.pallas import tpu as pltpu

F32 = jnp.float32
BF16 = jnp.bfloat16
HIGHEST = lax.Precision.HIGHEST
MESH = pl.DeviceIdType.MESH

RMS_EPS = 1e-6
GN_EPS = 64e-5
NEG_INF = -1e30
FFN_RESIDUAL = 0.5
HEAD_DIM = 64
ATT_BLOCK = 128
DILATIONS = (1, 4, 16)
SCAN_CHUNK = 64
TOKEN_TILE = 256

ADAM_LR = 0.001
ADAM_B1 = 0.9
ADAM_B2 = 0.999
ADAM_EPS = 1e-08
ADAM_WD = 0.01
ADAM_STEP = 10

VMEM_FULL = pl.BlockSpec(memory_space=pltpu.VMEM)
ANY = pl.BlockSpec(memory_space=pl.ANY)


VMEM_LIMIT = 56 * 1024 * 1024


def _params(*sem):
    return pltpu.CompilerParams(dimension_semantics=sem, vmem_limit_bytes=VMEM_LIMIT)


def _dot(a, b, dims):
    return lax.dot_general(a.astype(BF16), b.astype(BF16), (dims, ((), ())), preferred_element_type=F32)


def _dot_nn(a, b):
    return _dot(a, b, ((1,), (0,)))


def _dot_nt(a, b):
    return _dot(a, b, ((1,), (1,)))


def _dot_tn(a, b):
    return _dot(a, b, ((0,), (0,)))


@jax.custom_vjp
def _mm(a, b):
    return _dot_nn(a, b)


def _mm_fwd(a, b):
    return _dot_nn(a, b), (a, b)


def _mm_bwd(res, g):
    a, b = res
    return _dot_nt(g, b).astype(a.dtype), _dot_tn(a, g).astype(b.dtype)


_mm.defvjp(_mm_fwd, _mm_bwd)


def _bdot(a, b, ca, cb):
    return lax.dot_general(a.astype(BF16), b.astype(BF16), (((ca,), (cb,)), ((0,), (0,))), preferred_element_type=F32)


@jax.custom_vjp
def _bmm_nt(a, b):
    return _bdot(a, b, 2, 2)


def _bmm_nt_fwd(a, b):
    return _bdot(a, b, 2, 2), (a, b)


def _bmm_nt_bwd(res, g):
    a, b = res
    return _bdot(g, b, 2, 1), _bdot(g, a, 1, 1)


_bmm_nt.defvjp(_bmm_nt_fwd, _bmm_nt_bwd)


@jax.custom_vjp
def _bmm_nn(a, b):
    return _bdot(a, b, 2, 1)


def _bmm_nn_fwd(a, b):
    return _bdot(a, b, 2, 1), (a, b)


def _bmm_nn_bwd(res, g):
    a, b = res
    return _bdot(g, b, 2, 2), _bdot(a, g, 1, 1)


_bmm_nn.defvjp(_bmm_nn_fwd, _bmm_nn_bwd)


def _hdot(a, b, ca, cb):
    return lax.dot_general(a, b, (((ca,), (cb,)), ((0,), (0,))), precision=lax.Precision.HIGH, preferred_element_type=F32)


def _sigmoid(x):
    return 1.0 / (1.0 + jnp.exp(-x))


def _rms(x):
    return lax.rsqrt(jnp.mean(x * x, axis=-1, keepdims=True) + RMS_EPS)


def _ffn_fwd(x, norm, wg, wu, wd, dep, name):
    t, d = x.shape
    nc = wg.shape[0]
    tm = TOKEN_TILE

    def body(x_ref, n_ref, wg_ref, wu_ref, wd_ref, dep_ref, o_ref):
        xv = x_ref[...]
        h = (xv * _rms(xv) * n_ref[...]).astype(BF16)
        acc = jnp.zeros((tm, d), F32)
        for c in range(nc):
            g = jnp.dot(h, wg_ref[c], preferred_element_type=F32)
            u = jnp.dot(h, wu_ref[c], preferred_element_type=F32)
            a = (g * _sigmoid(g) * u).astype(BF16)
            acc = acc + jnp.dot(a, wd_ref[c], preferred_element_type=F32)
        o_ref[...] = xv + FFN_RESIDUAL * acc

    tile = pl.BlockSpec((tm, d), lambda i: (i, 0))
    return pl.pallas_call(
        body, name=name, grid=(t // tm,), out_shape=jax.ShapeDtypeStruct((t, d), F32),
        in_specs=[tile, pl.BlockSpec((1, d), lambda i: (0, 0)), VMEM_FULL, VMEM_FULL, VMEM_FULL, ANY],
        out_specs=tile, compiler_params=_params("arbitrary"),
    )(x, norm, wg, wu, wd, dep)


def _rmsnorm_bwd(xv, gain, dh):
    rs = _rms(xv)
    xn = xv * rs
    dxn = dh * gain
    dx = rs * (dxn - xn * jnp.mean(dxn * xn, axis=-1, keepdims=True))
    return dx, jnp.sum(dh * xn, axis=0, keepdims=True)


def _ffn_bwd(x, norm, wg, wu, wd, dy, dep, name):
    t, d = x.shape
    nc, _, fc = wg.shape
    tm = TOKEN_TILE
    nt = t // tm

    def body(x_ref, n_ref, wg_ref, wu_ref, wd_ref, dy_ref, dep_ref, dx_ref, dn_ref, dwg_ref, dwu_ref, dwd_ref, dh_ref):
        c, i = pl.program_id(0), pl.program_id(1)
        rows = pl.ds(pl.multiple_of(i * tm, tm), tm)
        xv = x_ref[...]
        gain = n_ref[...]
        h = (xv * _rms(xv) * gain).astype(BF16)
        dy = dy_ref[...]
        dyb = (FFN_RESIDUAL * dy).astype(BF16)
        g = jnp.dot(h, wg_ref[0], preferred_element_type=F32)
        u = jnp.dot(h, wu_ref[0], preferred_element_type=F32)
        sg = _sigmoid(g)
        s = g * sg
        a = (s * u).astype(BF16)
        da = _dot_nt(dyb, wd_ref[0])
        dub = (da * s).astype(BF16)
        dgb = (da * u * (sg * (1.0 + g * (1.0 - sg)))).astype(BF16)
        dwd_c = _dot_tn(a, dyb)
        dwg_c = _dot_tn(h, dgb)
        dwu_c = _dot_tn(h, dub)
        dh_c = _dot_nt(dgb, wg_ref[0]) + _dot_nt(dub, wu_ref[0])

        @pl.when(i == 0)
        def _():
            dwd_ref[0] = dwd_c
            dwg_ref[0] = dwg_c
            dwu_ref[0] = dwu_c

        @pl.when(i > 0)
        def _():
            dwd_ref[0] += dwd_c
            dwg_ref[0] += dwg_c
            dwu_ref[0] += dwu_c

        @pl.when(c == 0)
        def _():
            dh_ref[rows, :] = dh_c

        @pl.when(c > 0)
        def _():
            dh_ref[rows, :] += dh_c

        @pl.when(c == nc - 1)
        def _():
            dx, dn = _rmsnorm_bwd(xv, gain, dh_ref[rows, :])
            dx_ref[...] = dx + dy

            @pl.when(i == 0)
            def _():
                dn_ref[...] = dn

            @pl.when(i > 0)
            def _():
                dn_ref[...] += dn

    tile = pl.BlockSpec((tm, d), lambda c, i: (i, 0))
    row = pl.BlockSpec((1, d), lambda c, i: (0, 0))
    wcol = pl.BlockSpec((1, d, fc), lambda c, i: (c, 0, 0))
    wrow = pl.BlockSpec((1, fc, d), lambda c, i: (c, 0, 0))
    last = pl.BlockSpec((tm, d), lambda c, i: (jnp.where(c == nc - 1, i, 0), 0))
    return pl.pallas_call(
        body, name=name, grid=(nc, nt),
        out_shape=(jax.ShapeDtypeStruct((t, d), F32), jax.ShapeDtypeStruct((1, d), F32),
                   jax.ShapeDtypeStruct(wg.shape, F32), jax.ShapeDtypeStruct(wu.shape, F32),
                   jax.ShapeDtypeStruct(wd.shape, F32)),
        in_specs=[tile, row, wcol, wcol, wrow, tile, ANY],
        out_specs=(last, row, wcol, wcol, wrow),
        scratch_shapes=[pltpu.VMEM((t, d), F32)],
        compiler_params=_params("arbitrary", "arbitrary"),
    )(x, norm, wg, wu, wd, dy, dep)


def _proj_fwd(x, norm, w):
    t, d = x.shape
    nc, _, ncol = w.shape
    tm = TOKEN_TILE

    def body(x_ref, n_ref, w_ref, o_ref):
        xv = x_ref[...]
        h = (xv * _rms(xv) * n_ref[...]).astype(BF16)
        for c in range(nc):
            o_ref[:, c * ncol:(c + 1) * ncol] = jnp.dot(h, w_ref[c], preferred_element_type=F32)

    return pl.pallas_call(
        body, name="proj_fwd", grid=(t // tm,), out_shape=jax.ShapeDtypeStruct((t, nc * ncol), F32),
        in_specs=[pl.BlockSpec((tm, d), lambda i: (i, 0)), pl.BlockSpec((1, d), lambda i: (0, 0)), VMEM_FULL],
        out_specs=pl.BlockSpec((tm, nc * ncol), lambda i: (i, 0)), compiler_params=_params("arbitrary"),
    )(x, norm, w)


def _proj_bwd(x, norm, w, dpa, dpb, dpc, dres):
    t, d = x.shape
    nc, _, ncol = w.shape
    tm = TOKEN_TILE
    nt = t // tm

    def body(x_ref, n_ref, w_ref, dpa_ref, dpb_ref, dpc_ref, dres_ref, dx_ref, dn_ref, dw_ref, dh_ref):
        c, i = pl.program_id(0), pl.program_id(1)
        rows = pl.ds(pl.multiple_of(i * tm, tm), tm)
        xv = x_ref[...]
        gain = n_ref[...]
        h = (xv * _rms(xv) * gain).astype(BF16)
        dpv = (dpa_ref[...] + dpb_ref[...] + dpc_ref[...]).astype(BF16)
        dw_c = _dot_tn(h, dpv)
        dh_c = _dot_nt(dpv, w_ref[0])

        @pl.when(i == 0)
        def _():
            dw_ref[0] = dw_c

        @pl.when(i > 0)
        def _():
            dw_ref[0] += dw_c

        @pl.when(c == 0)
        def _():
            dh_ref[rows, :] = dh_c

        @pl.when(c > 0)
        def _():
            dh_ref[rows, :] += dh_c

        @pl.when(c == nc - 1)
        def _():
            dx, dn = _rmsnorm_bwd(xv, gain, dh_ref[rows, :])
            dx_ref[...] = dx + dres_ref[...]

            @pl.when(i == 0)
            def _():
                dn_ref[...] = dn

            @pl.when(i > 0)
            def _():
                dn_ref[...] += dn

    tile = pl.BlockSpec((tm, d), lambda c, i: (i, 0))
    row = pl.BlockSpec((1, d), lambda c, i: (0, 0))
    wcol = pl.BlockSpec((1, d, ncol), lambda c, i: (c, 0, 0))
    ptile = pl.BlockSpec((tm, ncol), lambda c, i: (i, c))
    last = pl.BlockSpec((tm, d), lambda c, i: (jnp.where(c == nc - 1, i, 0), 0))
    return pl.pallas_call(
        body, name="proj_bwd", grid=(nc, nt),
        out_shape=(jax.ShapeDtypeStruct((t, d), F32), jax.ShapeDtypeStruct((1, d), F32),
                   jax.ShapeDtypeStruct(w.shape, F32)),
        in_specs=[tile, row, wcol, ptile, ptile, ptile, tile],
        out_specs=(last, row, wcol),
        scratch_shapes=[pltpu.VMEM((t, d), F32)],
        compiler_params=_params("arbitrary", "arbitrary"),
    )(x, norm, w, dpa, dpb, dpc, dres)


def _mixout_fwd(x, att, opg, gate, w):
    t, d = x.shape
    half = att.shape[1]
    tm = TOKEN_TILE

    def body(x_ref, att_ref, opg_ref, g_ref, w_ref, o_ref):
        mix = jnp.concatenate([att_ref[...], opg_ref[...] * g_ref[...]], axis=-1).astype(BF16)
        o_ref[...] = x_ref[...] + jnp.dot(mix, w_ref[...], preferred_element_type=F32)

    tile = pl.BlockSpec((tm, d), lambda i: (i, 0))
    htile = pl.BlockSpec((tm, half), lambda i: (i, 0))
    return pl.pallas_call(
        body, name="mixout_fwd", grid=(t // tm,), out_shape=jax.ShapeDtypeStruct((t, d), F32),
        in_specs=[tile, htile, htile, htile, VMEM_FULL], out_specs=tile, compiler_params=_params("arbitrary"),
    )(x, att, opg, gate, w)


def _mixout_bwd(att, opg, gate, w, dy, dep):
    t, half = att.shape
    d = dy.shape[1]
    tm = TOKEN_TILE

    def body(att_ref, opg_ref, g_ref, w_ref, dy_ref, dep_ref, datt_ref, dopg_ref, dg_ref, dw_ref):
        i = pl.program_id(0)
        opg_v, g_v = opg_ref[...], g_ref[...]
        mix = jnp.concatenate([att_ref[...], opg_v * g_v], axis=-1).astype(BF16)
        dyb = dy_ref[...].astype(BF16)
        dmix = _dot_nt(dyb, w_ref[...])
        dw = _dot_tn(mix, dyb)
        datt_ref[...] = dmix[:, :half]
        drw = dmix[:, half:]
        dopg_ref[...] = drw * g_v
        dg_ref[...] = drw * opg_v

        @pl.when(i == 0)
        def _():
            dw_ref[...] = dw

        @pl.when(i > 0)
        def _():
            dw_ref[...] += dw

    tile = pl.BlockSpec((tm, d), lambda i: (i, 0))
    htile = pl.BlockSpec((tm, half), lambda i: (i, 0))
    hshape = jax.ShapeDtypeStruct((t, half), F32)
    return pl.pallas_call(
        body, name="mixout_bwd", grid=(t // tm,),
        out_shape=(hshape, hshape, hshape, jax.ShapeDtypeStruct(w.shape, F32)),
        in_specs=[htile, htile, htile, VMEM_FULL, tile, ANY],
        out_specs=(htile, htile, htile, pl.BlockSpec(w.shape, lambda i: (0, 0))),
        compiler_params=_params("arbitrary"),
    )(att, opg, gate, w, dy, dep)


def _loss_head(y, target):
    t, d = y.shape
    tm = TOKEN_TILE

    def body(y_ref, t_ref, dy_ref, loss_ref):
        i = pl.program_id(0)
        err = y_ref[...] - t_ref[...]
        dy_ref[...] = err * (1.0 / d)
        part = 0.5 * jnp.sum(jnp.mean(err * err, axis=-1, keepdims=True), axis=0, keepdims=True)

        @pl.when(i == 0)
        def _():
            loss_ref[...] = jnp.zeros_like(loss_ref)

        loss_ref[...] += jnp.broadcast_to(part, loss_ref.shape)

    tile = pl.BlockSpec((tm, d), lambda i: (i, 0))
    return pl.pallas_call(
        body, name="loss_head", grid=(t // tm,),
        out_shape=(jax.ShapeDtypeStruct((t, d), F32), jax.ShapeDtypeStruct((1, 128), F32)),
        in_specs=[tile, tile], out_specs=(tile, pl.BlockSpec((1, 128), lambda i: (0, 0))),
        compiler_params=_params("arbitrary"),
    )(y, target)


def _att_block(q, kp, kc, vp, vc, qn, kn, has_prev):
    blk = q.shape[1]

    def hn(v, gain):
        return v * _rms(v) * gain

    qh, khp, khc = hn(q, qn), hn(kp, kn), hn(kc, kn)
    scale = HEAD_DIM ** -0.5
    sp = _bmm_nt(qh, khp) * scale
    sc = _bmm_nt(qh, khc) * scale
    qi = lax.broadcasted_iota(jnp.int32, (blk, blk), 0)
    kj = lax.broadcasted_iota(jnp.int32, (blk, blk), 1)
    sp = jnp.where((kj >= qi) & has_prev, sp, NEG_INF)
    sc = jnp.where(kj <= qi, sc, NEG_INF)
    m = lax.stop_gradient(jnp.maximum(jnp.max(sp, axis=-1, keepdims=True), jnp.max(sc, axis=-1, keepdims=True)))
    pp, pc = jnp.exp(sp - m), jnp.exp(sc - m)
    den = jnp.sum(pp, axis=-1, keepdims=True) + jnp.sum(pc, axis=-1, keepdims=True)
    o = (_bmm_nn(pp, vp) + _bmm_nn(pc, vc)) / den
    return o, m + jnp.log(den)


def _att_specs(g, length, gt):
    blk = ATT_BLOCK
    cur = pl.BlockSpec((gt, blk, HEAD_DIM), lambda gi, n: (gi, n, 0))
    prev = pl.BlockSpec((gt, blk, HEAD_DIM), lambda gi, n: (gi, jnp.maximum(n - 1, 0), 0))
    gain = pl.BlockSpec((1, 1, HEAD_DIM), lambda gi, n: (0, 0, 0))
    col = pl.BlockSpec((gt, blk, 1), lambda gi, n: (gi, n, 0))
    return cur, prev, gain, col


def _att_fwd(q, k, v, qn, kn, name):
    g, length, dh = q.shape
    gt = g // 8
    cur, prev, gain, col = _att_specs(g, length, gt)

    def body(q_ref, kp_ref, kc_ref, vp_ref, vc_ref, qn_ref, kn_ref, o_ref, lse_ref):
        o, lse = _att_block(q_ref[...], kp_ref[...], kc_ref[...], vp_ref[...], vc_ref[...],
                            qn_ref[...], kn_ref[...], pl.program_id(1) > 0)
        o_ref[...] = o
        lse_ref[...] = lse

    return pl.pallas_call(
        body, name=name, grid=(g // gt, length // ATT_BLOCK),
        out_shape=(jax.ShapeDtypeStruct(q.shape, F32), jax.ShapeDtypeStruct((g, length, 1), F32)),
        in_specs=[cur, prev, cur, prev, cur, gain, gain], out_specs=(cur, col),
        compiler_params=_params("arbitrary", "arbitrary"),
    )(q, k, k, v, v, qn, kn)


def _att_bwd(q, k, v, qn, kn, do, dlse, dqn0, dkn0, name):
    g, length, dh = q.shape
    gt = g // 8
    blk = ATT_BLOCK
    cur, prev, gain, col = _att_specs(g, length, gt)
    whole = pl.BlockSpec((gt, length, dh), lambda gi, n: (gi, 0, 0))

    def body(q_ref, kp_ref, kc_ref, vp_ref, vc_ref, qn_ref, kn_ref, do_ref, dl_ref, dqn0_ref, dkn0_ref,
             dq_ref, dk_ref, dv_ref, dqn_ref, dkn_ref):
        gi, n = pl.program_id(0), pl.program_id(1)
        has_prev = n > 0
        fn = functools.partial(_att_block, has_prev=has_prev)
        _, vjp = jax.vjp(fn, q_ref[...], kp_ref[...], kc_ref[...], vp_ref[...], vc_ref[...], qn_ref[...], kn_ref[...])
        dq, dkp, dkc, dvp, dvc, dqn, dkn = vjp((do_ref[...], dl_ref[...]))
        dq_ref[...] = dq
        here = pl.ds(pl.multiple_of(n * blk, blk), blk)
        dk_ref[:, here, :] = dkc
        dv_ref[:, here, :] = dvc

        @pl.when(has_prev)
        def _():
            before = pl.ds(pl.multiple_of((n - 1) * blk, blk), blk)
            dk_ref[:, before, :] += dkp
            dv_ref[:, before, :] += dvp

        @pl.when((gi == 0) & (n == 0))
        def _():
            dqn_ref[...] = dqn0_ref[...]
            dkn_ref[...] = dkn0_ref[...]

        dqn_ref[...] += dqn
        dkn_ref[...] += dkn

    gshape = jax.ShapeDtypeStruct((1, 1, dh), F32)
    return pl.pallas_call(
        body, name=name, grid=(g // gt, length // blk),
        out_shape=(jax.ShapeDtypeStruct(q.shape, F32),) * 3 + (gshape, gshape),
        in_specs=[cur, prev, cur, prev, cur, gain, gain, cur, col, gain, gain],
        out_specs=(cur, whole, whole, gain, gain),
        compiler_params=_params("arbitrary", "arbitrary"),
    )(q, k, k, v, v, qn, kn, do, dlse, dqn0, dkn0)


def _merge_fn(o1, o2, o3, l1, l2, l3):
    m = lax.stop_gradient(jnp.maximum(jnp.maximum(l1, l2), l3))
    e1, e2, e3 = jnp.exp(l1 - m), jnp.exp(l2 - m), jnp.exp(l3 - m)
    return (e1 * o1 + e2 * o2 + e3 * o3) / (e1 + e2 + e3)


def _merge_specs(h, tm, dh):
    return pl.BlockSpec((h, tm, dh), lambda i: (0, i, 0)), pl.BlockSpec((h, tm, 1), lambda i: (0, i, 0))


MERGE_TILE = 128


def _merge_fwd(os, ls):
    h, t, dh = os[0].shape
    tm = MERGE_TILE
    wide, col = _merge_specs(h, tm, dh)

    def body(o1, o2, o3, l1, l2, l3, out):
        out[...] = _merge_fn(o1[...], o2[...], o3[...], l1[...], l2[...], l3[...])

    return pl.pallas_call(
        body, name="merge_fwd", grid=(t // tm,), out_shape=jax.ShapeDtypeStruct(os[0].shape, F32),
        in_specs=[wide] * 3 + [col] * 3, out_specs=wide, compiler_params=_params("arbitrary"),
    )(*os, *ls)


def _merge_bwd(os, ls, do):
    h, t, dh = os[0].shape
    tm = MERGE_TILE
    wide, col = _merge_specs(h, tm, dh)

    def body(o1, o2, o3, l1, l2, l3, do_ref, d1, d2, d3, e1, e2, e3):
        _, vjp = jax.vjp(_merge_fn, o1[...], o2[...], o3[...], l1[...], l2[...], l3[...])
        outs = vjp(do_ref[...])
        for ref, val in zip((d1, d2, d3, e1, e2, e3), outs):
            ref[...] = val

    oshape = jax.ShapeDtypeStruct(os[0].shape, F32)
    lshape = jax.ShapeDtypeStruct(ls[0].shape, F32)
    return pl.pallas_call(
        body, name="merge_bwd", grid=(t // tm,), out_shape=(oshape,) * 3 + (lshape,) * 3,
        in_specs=[wide] * 3 + [col] * 3 + [wide], out_specs=(wide,) * 3 + (col,) * 3,
        compiler_params=_params("arbitrary"),
    )(*os, *ls, do)


RWKV_VEC = ("mu_r", "mu_k", "mu_v", "mu_w", "mu_a", "mu_g", "w0", "a0", "k_k", "k_a")
RWKV_MAT = ("w1", "w2", "a1", "a2", "g1", "g2")


def _rwkv_pre_fn(cur, prev, vec, w1, w2, a1, a2, g1, g2):
    c = cur.shape[1] // 4
    mu_r, mu_k, mu_v, mu_w, mu_a, mu_g, w0, a0, k_k, k_a = (vec[j:j + 1] for j in range(10))

    def lerp(j, mu):
        xc, xp = cur[:, j * c:(j + 1) * c], prev[:, j * c:(j + 1) * c]
        return xc + (xp - xc) * mu

    r, k, v = lerp(0, mu_r), lerp(1, mu_k), lerp(2, mu_v)
    cw, ca, cg = lerp(3, mu_w), lerp(3, mu_a), lerp(3, mu_g)
    z = w0 + _mm(jnp.tanh(_mm(cw, w1)), w2)
    w_log = jnp.minimum(z, 0.0) - jnp.log(1.0 + jnp.exp(-jnp.abs(z))) - 0.5
    lw = -jnp.exp(w_log)
    a = _sigmoid(a0 + _mm(_mm(ca, a1), a2))
    gate = _mm(_sigmoid(_mm(cg, g1)), g2)
    kkraw = k * k_k
    kmod = k * (1.0 + (a - 1.0) * k_a)
    return r, lw, kmod, v, kkraw, a, gate


def _rwkv_pre_specs(c, mats):
    tm = TOKEN_TILE
    wide = pl.BlockSpec((tm, 4 * c), lambda i: (i, 0))
    one = pl.BlockSpec((tm, c), lambda i: (i, 0))
    vec = pl.BlockSpec((10, c), lambda i: (0, 0))
    mspecs = [pl.BlockSpec(m.shape, lambda i: (0, 0)) for m in mats]
    return wide, one, vec, mspecs


def _rwkv_pre_fwd(cur, prev, vec, mats):
    t, c4 = cur.shape
    c = c4 // 4
    wide, one, vspec, mspecs = _rwkv_pre_specs(c, mats)

    def body(cur_ref, prev_ref, vec_ref, *rest):
        mrefs, outs = rest[:6], rest[6:]
        vals = _rwkv_pre_fn(cur_ref[...], prev_ref[...], vec_ref[...], *(m[...] for m in mrefs))
        for ref, val in zip(outs, vals):
            ref[...] = val

    return pl.pallas_call(
        body, name="rwkv_pre_fwd", grid=(t // TOKEN_TILE,), out_shape=(jax.ShapeDtypeStruct((t, c), F32),) * 7,
        in_specs=[wide, wide, vspec] + mspecs, out_specs=(one,) * 7, compiler_params=_params("arbitrary"),
    )(cur, prev, vec, *mats)


def _rwkv_pre_bwd(cur, prev, vec, mats, cts):
    t, c4 = cur.shape
    c = c4 // 4
    wide, one, vspec, mspecs = _rwkv_pre_specs(c, mats)

    def body(cur_ref, prev_ref, vec_ref, *rest):
        mrefs, ctrefs, outs = rest[:6], rest[6:13], rest[13:]
        _, vjp = jax.vjp(_rwkv_pre_fn, cur_ref[...], prev_ref[...], vec_ref[...], *(m[...] for m in mrefs))
        grads = vjp(tuple(r[...] for r in ctrefs))
        outs[0][...] = grads[0]
        outs[1][...] = grads[1]
        first = pl.program_id(0) == 0

        @pl.when(first)
        def _():
            for ref, val in zip(outs[2:], grads[2:]):
                ref[...] = val

        @pl.when(jnp.logical_not(first))
        def _():
            for ref, val in zip(outs[2:], grads[2:]):
                ref[...] += val

    wshape = jax.ShapeDtypeStruct(cur.shape, F32)
    return pl.pallas_call(
        body, name="rwkv_pre_bwd", grid=(t // TOKEN_TILE,),
        out_shape=(wshape, wshape, jax.ShapeDtypeStruct(vec.shape, F32)) + tuple(jax.ShapeDtypeStruct(m.shape, F32) for m in mats),
        in_specs=[wide, wide, vspec] + mspecs + [one] * 7, out_specs=(wide, wide, vspec) + tuple(mspecs),
        compiler_params=_params("arbitrary"),
    )(cur, prev, vec, *mats, *cts)


def _scan_chunk_fn(h0, r, lw, k, v, kkraw, a, rk, lnw, lnb):
    n = r.shape[1]
    nrm = jnp.sqrt(jnp.sum(kkraw * kkraw, axis=-1, keepdims=True))
    kk = kkraw / jnp.maximum(nrm, 1e-12)
    av, bv = -kk, kk * a
    ti = lax.broadcasted_iota(jnp.int32, (n, n), 0)
    si = lax.broadcasted_iota(jnp.int32, (n, n), 1)
    incl, strict = ti >= si, ti > si
    ones = jnp.broadcast_to(incl.astype(F32)[None], (r.shape[0], n, n))
    cum = _hdot(ones, lw, 2, 1)
    at, rt = av * jnp.exp(cum - lw), r * jnp.exp(cum)
    inv = jnp.exp(-cum)
    bt, kt = bv * inv, k * inv
    lab = jnp.where(strict, _hdot(at, bt, 2, 2), 0.0)
    lak = jnp.where(strict, _hdot(at, kt, 2, 2), 0.0)
    rb = jnp.where(incl, _hdot(rt, bt, 2, 2), 0.0)
    rkm = jnp.where(incl, _hdot(rt, kt, 2, 2), 0.0)
    u = _hdot(at, h0, 2, 1) + _hdot(lak, v, 2, 1)
    p = lab
    m = 1
    while m < n:
        u = u + _hdot(p, u, 2, 1)
        m *= 2
        if m < n:
            p = _hdot(p, p, 2, 1)
    y = _hdot(rt, h0, 2, 1) + _hdot(rb, u, 2, 1) + _hdot(rkm, v, 2, 1)
    last = jnp.exp(jnp.sum(lw, axis=1, keepdims=True))
    h1 = jnp.swapaxes(last, 1, 2) * (h0 + _hdot(bt, u, 1, 1) + _hdot(kt, v, 1, 1))
    mean = jnp.mean(y, axis=-1, keepdims=True)
    yc = y - mean
    var = jnp.mean(yc * yc, axis=-1, keepdims=True)
    yn = yc * lax.rsqrt(var + GN_EPS) * lnw + lnb
    bonus = jnp.sum(r * k * rk, axis=-1, keepdims=True) * v
    return yn + bonus, h1


def _scan_specs(h, t, dh, rev):
    n = SCAN_CHUNK
    nc = t // n
    pos = (lambda c: (0, nc - 1 - c, 0)) if rev else (lambda c: (0, c, 0))
    st = (lambda c: (nc - 1 - c, 0, 0, 0)) if rev else (lambda c: (c, 0, 0, 0))
    seq = pl.BlockSpec((h, n, dh), pos)
    par = pl.BlockSpec((h, 1, dh), lambda c: (0, 0, 0))
    state = pl.BlockSpec((1, h, dh, dh), st)
    return seq, par, state


def _scan_fwd(seqs, pars):
    h, t, dh = seqs[0].shape
    nc = t // SCAN_CHUNK
    seq, par, state = _scan_specs(h, t, dh, False)

    def body(r, lw, k, v, kkraw, a, rk, lnw, lnb, o_ref, st_ref, h_ref):
        @pl.when(pl.program_id(0) == 0)
        def _():
            h_ref[...] = jnp.zeros_like(h_ref)

        h0 = h_ref[...]
        st_ref[0] = h0
        o, h1 = _scan_chunk_fn(h0, r[...], lw[...], k[...], v[...], kkraw[...], a[...], rk[...], lnw[...], lnb[...])
        o_ref[...] = o
        h_ref[...] = h1

    return pl.pallas_call(
        body, name="rwkv_scan_fwd", grid=(nc,),
        out_shape=(jax.ShapeDtypeStruct((h, t, dh), F32), jax.ShapeDtypeStruct((nc, h, dh, dh), F32)),
        in_specs=[seq] * 6 + [par] * 3, out_specs=(seq, state),
        scratch_shapes=[pltpu.VMEM((h, dh, dh), F32)], compiler_params=_params("arbitrary"),
    )(*seqs, *pars)


def _scan_bwd(seqs, pars, states, do):
    h, t, dh = seqs[0].shape
    nc = t // SCAN_CHUNK
    seq, par, state = _scan_specs(h, t, dh, True)

    def body(r, lw, k, v, kkraw, a, rk, lnw, lnb, st_ref, do_ref, *rest):
        douts, dpars, dh_ref = rest[:6], rest[6:9], rest[9]
        first = pl.program_id(0) == 0

        @pl.when(first)
        def _():
            dh_ref[...] = jnp.zeros_like(dh_ref)

        _, vjp = jax.vjp(_scan_chunk_fn, st_ref[0], r[...], lw[...], k[...], v[...], kkraw[...], a[...],
                         rk[...], lnw[...], lnb[...])
        grads = vjp((do_ref[...], dh_ref[...]))
        dh_ref[...] = grads[0]
        for ref, val in zip(douts, grads[1:7]):
            ref[...] = val

        @pl.when(first)
        def _():
            for ref, val in zip(dpars, grads[7:]):
                ref[...] = val

        @pl.when(jnp.logical_not(first))
        def _():
            for ref, val in zip(dpars, grads[7:]):
                ref[...] += val

    sshape = jax.ShapeDtypeStruct((h, t, dh), F32)
    pshape = jax.ShapeDtypeStruct((h, 1, dh), F32)
    return pl.pallas_call(
        body, name="rwkv_scan_bwd", grid=(nc,), out_shape=(sshape,) * 6 + (pshape,) * 3,
        in_specs=[seq] * 6 + [par] * 3 + [state, seq], out_specs=(seq,) * 6 + (par,) * 3,
        scratch_shapes=[pltpu.VMEM((h, dh, dh), F32)], compiler_params=_params("arbitrary"),
    )(*seqs, *pars, states, do)


def _heads(x):
    return x.reshape(x.shape[0], -1, HEAD_DIM).transpose(1, 0, 2)


def _unheads(x):
    return x.transpose(1, 0, 2).reshape(x.shape[1], -1)


def _to_sub(x, dil):
    h, t, d = x.shape
    return x.reshape(h, t // dil, dil, d).transpose(0, 2, 1, 3).reshape(h * dil, t // dil, d)


def _from_sub(x, dil):
    g, length, d = x.shape
    return x.reshape(g // dil, dil, length, d).transpose(0, 2, 1, 3).reshape(g // dil, length * dil, d)


def _local_step(x, target, w, ex):
    w = dict(w)
    c = w["mu_r"].shape[-1]
    qn, kn = w["q_norm"].reshape(1, 1, HEAD_DIM), w["k_norm"].reshape(1, 1, HEAD_DIM)
    vec = jnp.concatenate([w[n].reshape(1, c) for n in RWKV_VEC], axis=0)
    pars = [w[n].reshape(-1, 1, HEAD_DIM) for n in ("r_k", "ln_x_w", "ln_x_b")]
    no_dep = jnp.zeros(DEP_SHAPE, F32)

    x1 = _ffn_fwd(x, w["ffn1_norm"], w["ffn1_w_gate"], w["ffn1_w_up"], w["ffn1_w_down"], ex.first_dep, "ffn1_fwd")
    w.update(ex.mix_weights((x1,)))
    att_w = w["w_in"].shape[0] * w["w_in"].shape[2] - 4 * c
    mats = [w[n] for n in RWKV_MAT]
    proj = _proj_fwd(x1, w["mix_norm"], w["w_in"])
    hw = att_w // 3
    qkv = [_heads(proj[:, j * hw:(j + 1) * hw]) for j in range(3)]
    subs = [[_to_sub(a, dil) for a in qkv] for dil in DILATIONS]
    outs = [_att_fwd(*s, qn, kn, f"att_fwd_d{dil}") for s, dil in zip(subs, DILATIONS)]
    os_ = [_from_sub(o, dil) for (o, _), dil in zip(outs, DILATIONS)]
    ls_ = [_from_sub(l, dil) for (_, l), dil in zip(outs, DILATIONS)]
    att = _unheads(_merge_fwd(os_, ls_))
    cur = proj[:, att_w:]
    prev = jnp.concatenate([jnp.zeros_like(cur[:1]), cur[:-1]], axis=0)
    pre = _rwkv_pre_fwd(cur, prev, vec, mats)
    seqs = [_heads(a) for a in pre[:6]]
    gate = pre[6]
    opg_h, states = _scan_fwd(seqs, pars)
    opg = _unheads(opg_h)
    w.update(ex.out_weights((att, opg)))
    x2 = _mixout_fwd(x1, att, opg, gate, w["w_out"])
    x3 = _ffn_fwd(x2, w["ffn2_norm"], w["ffn2_w_gate"], w["ffn2_w_up"], w["ffn2_w_down"], no_dep, "ffn2_fwd")
    dy, loss = _loss_head(x3, target)

    g = {}
    dx2, g["ffn2_norm"], g["ffn2_w_gate"], g["ffn2_w_up"], g["ffn2_w_down"] = _ffn_bwd(
        x2, w["ffn2_norm"], w["ffn2_w_gate"], w["ffn2_w_up"], w["ffn2_w_down"], dy, no_dep, "ffn2_bwd")
    dep = ex.send_ffn2({n: g[n] for n in ("ffn2_w_gate", "ffn2_w_up", "ffn2_w_down")})
    datt, dopg, dgate, g["w_out"] = _mixout_bwd(att, opg, gate, w["w_out"], dx2, dep)
    dscan = _scan_bwd(seqs, pars, states, _heads(dopg))
    for n, d in zip(("r_k", "ln_x_w", "ln_x_b"), dscan[6:]):
        g[n] = d
    dpre = _rwkv_pre_bwd(cur, prev, vec, mats, [_unheads(d) for d in dscan[:6]] + [dgate])
    dcur, dprev, dvec = dpre[:3]
    for n, d in zip(RWKV_MAT, dpre[3:]):
        g[n] = d
    for j, n in enumerate(RWKV_VEC):
        g[n] = dvec[j:j + 1]
    dmerge = _merge_bwd(os_, ls_, _heads(datt))
    dqn = dkn = jnp.zeros((1, 1, HEAD_DIM), F32)
    dqkv = []
    for j, dil in enumerate(DILATIONS):
        dq, dk, dv, dqn, dkn = _att_bwd(*subs[j], qn, kn, _to_sub(dmerge[j], dil), _to_sub(dmerge[3 + j], dil),
                                        dqn, dkn, f"att_bwd_d{dil}")
        dqkv.append([_unheads(_from_sub(a, dil)) for a in (dq, dk, dv)])
    g["q_norm"], g["k_norm"] = dqn, dkn
    dshift = jnp.concatenate([dprev[1:], jnp.zeros_like(dprev[:1])], axis=0)
    dps = [jnp.concatenate(dqkv[j] + [tail], axis=1) for j, tail in enumerate((dcur, dshift, jnp.zeros_like(dcur)))]
    dx1, g["mix_norm"], g["w_in"] = _proj_bwd(x1, w["mix_norm"], w["w_in"], *dps, dx2)
    dep = ex.send_mix({n: g[n] for n in ("w_in", "w_out") + RWKV_MAT}, (dx1,))
    dx, g["ffn1_norm"], g["ffn1_w_gate"], g["ffn1_w_up"], g["ffn1_w_down"] = _ffn_bwd(
        x, w["ffn1_norm"], w["ffn1_w_gate"], w["ffn1_w_up"], w["ffn1_w_down"], dx1, dep, "ffn1_bwd")
    return loss, dx, g


N_SHARDS = 4


def _place():
    return lax.axis_index("x"), lax.axis_index("y"), lax.axis_index("c")


def _chip_peers(x, y):
    return [(1 - x, y), (x, 1 - y), (1 - x, 1 - y)]


def _gather_xy(shards):
    n = len(shards)

    def body(*refs):
        ins, outs = refs[:n], refs[n:2 * n]
        send_sems, recv_sems, local_sems = refs[2 * n:]
        x, y, c = _place()
        me = 2 * x + y
        copies = []
        for i in range(n):
            own = pltpu.make_async_copy(ins[i], outs[i].at[me], local_sems.at[i])
            own.start()
            copies.append(own)
            for k, (px, py) in enumerate(_chip_peers(x, y)):
                cp = pltpu.make_async_remote_copy(
                    src_ref=ins[i], dst_ref=outs[i].at[me], send_sem=send_sems.at[i, k], recv_sem=recv_sems.at[i, k],
                    device_id=(px, py, c), device_id_type=MESH)
                cp.start()
                copies.append(cp)
        for cp in copies:
            cp.wait()

    return pl.pallas_call(
        body, name="gather_weights",
        out_shape=tuple(jax.ShapeDtypeStruct((N_SHARDS,) + s.shape, s.dtype) for s in shards),
        in_specs=[ANY] * n, out_specs=(ANY,) * n,
        scratch_shapes=[pltpu.SemaphoreType.DMA((n, 3)), pltpu.SemaphoreType.DMA((n, 3)), pltpu.SemaphoreType.DMA((n,))],
    )(*shards)


def _scatter_partials(parts):
    n = len(parts)

    def body(*refs):
        ins, outs = refs[:n], refs[n:2 * n]
        send_sems, recv_sems = refs[2 * n:]
        x, y, c = _place()
        copies = []
        for i in range(n):
            for k, (px, py) in enumerate(_chip_peers(x, y)):
                cp = pltpu.make_async_remote_copy(
                    src_ref=ins[i].at[2 * px + py], dst_ref=outs[i].at[k], send_sem=send_sems.at[i, k],
                    recv_sem=recv_sems.at[i, k], device_id=(px, py, c), device_id_type=MESH)
                cp.start()
                copies.append(cp)
        for cp in copies:
            cp.wait()

    return pl.pallas_call(
        body, name="scatter_partials",
        out_shape=tuple(jax.ShapeDtypeStruct((3,) + p.shape[1:], p.dtype) for p in parts),
        in_specs=[ANY] * n, out_specs=(ANY,) * n,
        scratch_shapes=[pltpu.SemaphoreType.DMA((n, 3)), pltpu.SemaphoreType.DMA((n, 3))],
    )(*parts)


HBM = pl.BlockSpec(memory_space=pltpu.HBM)
SEM = pl.BlockSpec(memory_space=pltpu.SEMAPHORE)
DEP_SHAPE = (8, 128)


def _gather_views(i, srcs, lands, k, px, py, me):
    return (srcs[i], lands[i].at[me]), (srcs[i], lands[i].at[2 * px + py])


def _scatter_views(i, srcs, lands, k, px, py, me):
    return (srcs[i].at[2 * px + py], lands[i].at[k]), (srcs[i].at[me], lands[i].at[k])


def _push_start(srcs, land_shapes, views, own_slot, after, name):
    n = len(srcs)

    def body(*refs):
        src_refs, land_refs = refs[:n], refs[n:2 * n]
        send_sems, recv_sems = refs[2 * n + 1:2 * n + 3]
        token = refs[4 * n + 3]
        x, y, c = _place()
        me = 2 * x + y
        for i in range(n):
            for k, (px, py) in enumerate(_chip_peers(x, y)):
                (src, dst), _ = views(i, src_refs, land_refs, k, px, py, me)
                pltpu.make_async_remote_copy(
                    src_ref=src, dst_ref=dst, send_sem=send_sems.at[3 * i + k], recv_sem=recv_sems.at[3 * i + k],
                    device_id=(px, py, c), device_id_type=MESH).start()
        token[...] = jnp.zeros_like(token)

    sems = pltpu.SemaphoreType.DMA((3 * n,))
    lands = [lax.empty(s.shape, s.dtype) for s in land_shapes]
    if own_slot:
        me = 2 * lax.axis_index("x") + lax.axis_index("y")
        lands = [lax.dynamic_update_index_in_dim(z, s, me, 0) for z, s in zip(lands, srcs)]
    lands = [pltpu.with_memory_space_constraint(z, pltpu.HBM) for z in lands]
    srcs = [pltpu.with_memory_space_constraint(s, pltpu.HBM) for s in srcs]
    outs = pl.pallas_call(
        body, name=name,
        out_shape=(sems, sems, *[pltpu.HBM(s.shape, s.dtype) for s in srcs], *[pltpu.HBM(s.shape, s.dtype) for s in land_shapes],
                   jax.ShapeDtypeStruct(DEP_SHAPE, F32)),
        in_specs=[HBM] * (2 * n) + [ANY], out_specs=(SEM, SEM, *[HBM] * (2 * n), VMEM_FULL),
        input_output_aliases={i: 2 + i for i in range(2 * n)},
        compiler_params=pltpu.CompilerParams(has_side_effects=pltpu.SideEffectType.DATAFLOW_SIDE_EFFECTING),
    )(*srcs, *lands, after)
    return outs[0], outs[1], outs[2:2 + n], outs[2 + n:2 + 2 * n], outs[2 + 2 * n]


def _push_wait(started, views, after, name):
    send_sems, recv_sems, srcs, lands, _ = started
    n = len(srcs)

    def body(*refs):
        src_refs, land_refs = refs[:n], refs[n:2 * n]
        send_sems, recv_sems = refs[2 * n:2 * n + 2]
        x, y, c = _place()
        me = 2 * x + y
        for i in range(n):
            for k, (px, py) in enumerate(_chip_peers(x, y)):
                _, (src, dst) = views(i, src_refs, land_refs, k, px, py, me)
                landing = pltpu.make_async_remote_copy(
                    src_ref=src, dst_ref=dst, send_sem=send_sems.at[3 * i + k], recv_sem=recv_sems.at[3 * i + k],
                    device_id=(px, py, c), device_id_type=MESH)
                landing.wait_send()
                landing.wait_recv()

    outs = pl.pallas_call(
        body, name=name,
        out_shape=tuple(pltpu.HBM(s.shape, s.dtype) for s in (*srcs, *lands)),
        in_specs=[HBM] * (2 * n) + [SEM, SEM] + [ANY] * len(after), out_specs=(HBM,) * (2 * n),
        input_output_aliases={i: i for i in range(2 * n)},
        compiler_params=pltpu.CompilerParams(has_side_effects=pltpu.SideEffectType.DATAFLOW_SIDE_EFFECTING),
    )(*srcs, *lands, send_sems, recv_sems, *after)
    return outs[n:]


def _sibling_swap(arrays):
    n = len(arrays)

    def body(*refs):
        ins, outs = refs[:n], refs[n:2 * n]
        send_sems, recv_sems = refs[2 * n:]
        x, y, c = _place()
        copies = []
        for i in range(n):
            cp = pltpu.make_async_remote_copy(
                src_ref=ins[i], dst_ref=outs[i], send_sem=send_sems.at[i], recv_sem=recv_sems.at[i],
                device_id=(x, y, 1 - c), device_id_type=MESH)
            cp.start()
            copies.append(cp)
        for cp in copies:
            cp.wait()

    return pl.pallas_call(
        body, name="sibling_swap",
        out_shape=tuple(jax.ShapeDtypeStruct(a.shape, a.dtype) for a in arrays),
        in_specs=[ANY] * n, out_specs=(ANY,) * n,
        scratch_shapes=[pltpu.SemaphoreType.DMA((n,)), pltpu.SemaphoreType.DMA((n,))],
    )(*arrays)


N_DEV = 8


def _allreduce_small(pack):
    def body(in_ref, out_ref, buf, send_sems, recv_sems):
        x, y, c = _place()
        me = 4 * x + 2 * y + c
        buf[me] = in_ref[...]

        def copy(j, slot):
            px, py, pc = x ^ (j >> 2), y ^ ((j >> 1) & 1), c ^ (j & 1)
            return pltpu.make_async_remote_copy(
                src_ref=in_ref, dst_ref=buf.at[slot(px, py, pc)], send_sem=send_sems.at[j], recv_sem=recv_sems.at[j],
                device_id=(px, py, pc), device_id_type=MESH)

        for j in range(1, N_DEV):
            copy(j, lambda px, py, pc: me).start()
        for j in range(1, N_DEV):
            landing = copy(j, lambda px, py, pc: 4 * px + 2 * py + pc)
            landing.wait_send()
            landing.wait_recv()
        acc = buf[0]
        for s in range(1, N_DEV):
            acc = acc + buf[s]
        out_ref[...] = acc

    return pl.pallas_call(
        body, name="allreduce_small", out_shape=jax.ShapeDtypeStruct(pack.shape, F32),
        in_specs=[VMEM_FULL], out_specs=VMEM_FULL,
        scratch_shapes=[pltpu.VMEM((N_DEV,) + pack.shape, F32), pltpu.SemaphoreType.DMA((N_DEV,)),
                        pltpu.SemaphoreType.DMA((N_DEV,))],
    )(pack)


ROW_TILE_MAX = 256
BF16_SUBLANES = 16


def _row_tile(rows):
    for tr in range(min(rows, ROW_TILE_MAX), 0, -1):
        if rows % tr == 0 and tr % BF16_SUBLANES == 0:
            return tr
    return rows


def _reduce_own(me, part, recv, name):
    _, r, cols = part.shape
    tr = _row_tile(r)

    def body(me_ref, p_ref, rv_ref, o_ref):
        acc = p_ref[0]
        for k in range(3):
            acc = acc + rv_ref[k].astype(F32)
        o_ref[...] = acc

    return pl.pallas_call(
        body, name=name, out_shape=jax.ShapeDtypeStruct((r, cols), F32),
        grid_spec=pltpu.PrefetchScalarGridSpec(
            num_scalar_prefetch=1, grid=(r // tr,),
            in_specs=[pl.BlockSpec((1, tr, cols), lambda i, me_ref: (me_ref[0], i, 0)),
                      pl.BlockSpec((3, tr, cols), lambda i, me_ref: (0, i, 0))],
            out_specs=pl.BlockSpec((tr, cols), lambda i, me_ref: (i, 0))),
        compiler_params=_params("arbitrary"),
    )(me, part, recv)


def _adamw(w, ga, gb, m, v, name):
    r, cols = w.shape
    tr = _row_tile(r)
    c1 = 1.0 - ADAM_B1 ** ADAM_STEP
    c2 = 1.0 - ADAM_B2 ** ADAM_STEP

    def body(w_ref, ga_ref, gb_ref, m_ref, v_ref, g_out, d_out, m_out, v_out):
        g = ga_ref[...] + gb_ref[...]
        mn = ADAM_B1 * m_ref[...] + (1.0 - ADAM_B1) * g
        vn = ADAM_B2 * v_ref[...] + (1.0 - ADAM_B2) * (g * g)
        g_out[...] = g
        m_out[...] = mn
        v_out[...] = vn
        d_out[...] = -ADAM_LR * ((mn / c1) / (jnp.sqrt(vn / c2) + ADAM_EPS) + ADAM_WD * w_ref[...])

    tile = pl.BlockSpec((tr, cols), lambda i: (i, 0))
    shape = jax.ShapeDtypeStruct((r, cols), F32)
    return pl.pallas_call(
        body, name=name, grid=(r // tr,), out_shape=(shape,) * 4, in_specs=[tile] * 5, out_specs=(tile,) * 4,
        compiler_params=_params("arbitrary"),
    )(w, ga, gb, m, v)


PACK_COLS = 512


def _to_rows(a):
    flat = a.reshape(-1)
    pad = (-flat.shape[0]) % PACK_COLS
    return jnp.pad(flat, (0, pad)).reshape(-1, PACK_COLS)


def _pack(arrays, extra_rows=0):
    rows = [_to_rows(a) for a in arrays]
    n = sum(r.shape[0] for r in rows) + extra_rows
    pad = (-n) % 8
    return jnp.concatenate(rows + [jnp.zeros((extra_rows + pad, PACK_COLS), F32)], axis=0)


def _unpack(pack, like):
    out, at = [], 0
    for a in like:
        n = -(-a.size // PACK_COLS)
        out.append(pack[at:at + n].reshape(-1)[:a.size].reshape(a.shape))
        at += n
    return out


COL_SHARDED = ("ffn1_w_gate", "ffn1_w_up", "w_in", "ffn2_w_gate", "ffn2_w_up", "w2", "a2", "g2")
ROW_SHARDED = ("ffn1_w_down", "ffn2_w_down", "w_out", "w1", "a1", "g1")
CHUNKED = ("ffn1_w_gate", "ffn1_w_up", "ffn1_w_down", "w_in", "ffn2_w_gate", "ffn2_w_up", "ffn2_w_down")
WEIGHTS = ("ffn1_norm", "ffn1_w_gate", "ffn1_w_up", "ffn1_w_down", "mix_norm", "w_in", "q_norm", "k_norm",
           "mu_r", "mu_k", "mu_v", "mu_w", "mu_a", "mu_g", "w0", "w1", "w2", "a0", "a1", "a2", "g1", "g2",
           "k_k", "k_a", "r_k", "ln_x_w", "ln_x_b", "w_out", "ffn2_norm", "ffn2_w_gate", "ffn2_w_up", "ffn2_w_down")


def _full_from_blocks(name, blocks):
    if name in CHUNKED:
        return blocks
    if name in ROW_SHARDED:
        return blocks.reshape(-1, blocks.shape[-1])
    return blocks.transpose(1, 0, 2).reshape(blocks.shape[1], -1)


def _blocks_from_full(name, full):
    if name in CHUNKED:
        return full
    if name in ROW_SHARDED:
        return full.reshape(N_SHARDS, -1, full.shape[-1])
    return full.reshape(full.shape[0], N_SHARDS, -1).transpose(1, 0, 2)


FFN1_GROUP = ("ffn1_w_gate", "ffn1_w_up", "ffn1_w_down")
MIX_GROUP = ("w_in",) + RWKV_MAT
OUT_GROUP = ("w_out", "ffn2_w_gate", "ffn2_w_up", "ffn2_w_down")
FFN2_GROUP = OUT_GROUP[1:]
LATE_GROUP = ("w_in", "w_out") + RWKV_MAT


class _Exchange:
    def __init__(self, given):
        self.given = given
        first = _gather_xy(self._shards(FFN1_GROUP))
        self.first_weights = self._full(FFN1_GROUP, first)
        self.mix = self._gather_start(MIX_GROUP, first[0], "gather_mix_start")
        self.out = self._gather_start(OUT_GROUP, self.mix[4], "gather_out_start")
        self.first_dep = self.out[4]
        self.parts, self.recv = {}, {}

    def _shards(self, names):
        return [self.given[n][0].astype(BF16) for n in names]

    @staticmethod
    def _full(names, blocks):
        out = {}
        for n, b in zip(names, blocks):
            full = _full_from_blocks(n, b)
            out[n] = full.astype(F32) if n in RWKV_MAT else full
        return out

    def _gather_start(self, names, after, name):
        shards = self._shards(names)
        lands = [jax.ShapeDtypeStruct((N_SHARDS,) + s.shape, s.dtype) for s in shards]
        return _push_start(shards, lands, _gather_views, True, after, name)

    def mix_weights(self, after):
        return self._full(MIX_GROUP, _push_wait(self.mix, _gather_views, after, "gather_mix_wait"))

    def out_weights(self, after):
        return self._full(OUT_GROUP, _push_wait(self.out, _gather_views, after, "gather_out_wait"))

    def _scatter_start(self, grads, name):
        names = tuple(grads)
        parts = [_blocks_from_full(n, grads[n]) for n in names]
        self.parts.update(zip(names, parts))
        lands = [jax.ShapeDtypeStruct((3,) + p.shape[1:], BF16) for p in parts]
        return _push_start([p.astype(BF16) for p in parts], lands, _scatter_views, False, parts[0], name)

    def send_ffn2(self, grads):
        self.ffn2 = self._scatter_start(grads, "scatter_ffn2_start")
        return self.ffn2[4]

    def send_mix(self, grads, after):
        self.recv.update(zip(FFN2_GROUP, _push_wait(self.ffn2, _scatter_views, after, "scatter_ffn2_wait")))
        self.late = self._scatter_start(grads, "scatter_late_start")
        return self.late[4]

    def finish(self, grads):
        names = tuple(grads)
        parts = [_blocks_from_full(n, grads[n]) for n in names]
        self.parts.update(zip(names, parts))
        got = _scatter_partials([p.astype(BF16) for p in parts])
        self.recv.update(zip(names, got))
        self.recv.update(zip(LATE_GROUP, _push_wait(self.late, _scatter_views, (got[0],), "scatter_late_wait")))
        return self.parts, self.recv


def kernel(
        x, ffn1_norm, ffn1_w_gate, ffn1_w_up, ffn1_w_down, mix_norm, w_in, q_norm, k_norm, mu_r, mu_k, mu_v, mu_w,
        mu_a, mu_g, w0, w1, w2, a0, a1, a2, g1, g2, k_k, k_a, r_k, ln_x_w, ln_x_b, w_out, ffn2_norm, ffn2_w_gate,
        ffn2_w_up, ffn2_w_down, loss_target, m_ffn1_norm, m_ffn1_w_gate, m_ffn1_w_up, m_ffn1_w_down, m_mix_norm,
        m_w_in, m_q_norm, m_k_norm, m_mu_r, m_mu_k, m_mu_v, m_mu_w, m_mu_a, m_mu_g, m_w0, m_w1, m_w2, m_a0, m_a1,
        m_a2, m_g1, m_g2, m_k_k, m_k_a, m_r_k, m_ln_x_w, m_ln_x_b, m_w_out, m_ffn2_norm, m_ffn2_w_gate, m_ffn2_w_up,
        m_ffn2_w_down, v_ffn1_norm, v_ffn1_w_gate, v_ffn1_w_up, v_ffn1_w_down, v_mix_norm, v_w_in, v_q_norm, v_k_norm,
        v_mu_r, v_mu_k, v_mu_v, v_mu_w, v_mu_a, v_mu_g, v_w0, v_w1, v_w2, v_a0, v_a1, v_a2, v_g1, v_g2, v_k_k, v_k_a,
        v_r_k, v_ln_x_w, v_ln_x_b, v_w_out, v_ffn2_norm, v_ffn2_w_gate, v_ffn2_w_up, v_ffn2_w_down):
    given = dict(locals())
    sharded = COL_SHARDED + ROW_SHARDED
    sharded = tuple(n for n in WEIGHTS if n in sharded)
    small = tuple(n for n in WEIGHTS if n not in sharded)

    ex = _Exchange(given)
    w = {n: given[n] for n in small}
    w.update(ex.first_weights)
    loss, dx, g = _local_step(x[0], loss_target[0], w, ex)
    parts, recv = ex.finish({n: g[n] for n in FFN1_GROUP})

    me = (2 * lax.axis_index("x") + lax.axis_index("y")).astype(jnp.int32).reshape(1)
    mine = []
    for n in sharded:
        p, rv = parts[n], recv[n]
        p2 = p.reshape(N_SHARDS, -1, p.shape[-1])
        mine.append(_reduce_own(me, p2, rv.reshape(3, -1, rv.shape[-1]), f"reduce_{n}"))
    theirs = _sibling_swap(mine)
    out = {}
    for n, a, b in zip(sharded, mine, theirs):
        shape = given[n].shape
        two_d = (-1, shape[-1])
        res = _adamw(given[n].reshape(two_d), a, b, given["m_" + n].reshape(two_d), given["v_" + n].reshape(two_d), f"adamw_{n}")
        out[n] = [r.reshape(shape) for r in res]

    gpack = _pack([g[n] for n in small], extra_rows=1)
    n_rows = sum(-(-given[n].size // PACK_COLS) for n in small)
    gpack = gpack.at[n_rows, :loss.shape[1]].set(loss[0])
    gsum = _allreduce_small(gpack)
    res = _adamw(_pack([given[n] for n in small], 1), gsum, jnp.zeros_like(gsum), _pack([given["m_" + n] for n in small], 1),
                 _pack([given["v_" + n] for n in small], 1), "adamw_small")
    like = [given[n] for n in small]
    for j, r in enumerate(res):
        for n, a in zip(small, _unpack(r, like)):
            out.setdefault(n, [None] * 4)[j] = a
    total_loss = gsum[n_rows, 0]
    return (total_loss, dx[None], *[out[n][0] for n in WEIGHTS], *[out[n][1] for n in WEIGHTS],
            *[out[n][2] for n in WEIGHTS], *[out[n][3] for n in WEIGHTS])
```

```python
import functools

import jax
import jax.numpy as jnp
from jax import lax
from jax.experimental import pallas as pl
from jax.experimental.pallas import tpu as pltpu

F32 = jnp.float32
BF16 = jnp.bfloat16
HIGHEST = lax.Precision.HIGHEST
MESH = pl.DeviceIdType.MESH

RMS_EPS = 1e-6
GN_EPS = 64e-5
NEG_INF = -1e30
FFN_RESIDUAL = 0.5
HEAD_DIM = 64
ATT_BLOCK = 128
DILATIONS = (1, 4, 16)
SCAN_CHUNK = 64
TOKEN_TILE = 256

ADAM_LR = 0.001
ADAM_B1 = 0.9
ADAM_B2 = 0.999
ADAM_EPS = 1e-08
ADAM_WD = 0.01
ADAM_STEP = 10

VMEM_FULL = pl.BlockSpec(memory_space=pltpu.VMEM)
ANY = pl.BlockSpec(memory_space=pl.ANY)


VMEM_LIMIT = 56 * 1024 * 1024


def _params(*sem):
    return pltpu.CompilerParams(dimension_semantics=sem, vmem_limit_bytes=VMEM_LIMIT)


def _dot(a, b, dims):
    return lax.dot_general(a.astype(BF16), b.astype(BF16), (dims, ((), ())), preferred_element_type=F32)


def _dot_nn(a, b):
    return _dot(a, b, ((1,), (0,)))


def _dot_nt(a, b):
    return _dot(a, b, ((1,), (1,)))


def _dot_tn(a, b):
    return _dot(a, b, ((0,), (0,)))


@jax.custom_vjp
def _mm(a, b):
    return _dot_nn(a, b)


def _mm_fwd(a, b):
    return _dot_nn(a, b), (a, b)


def _mm_bwd(res, g):
    a, b = res
    return _dot_nt(g, b).astype(a.dtype), _dot_tn(a, g).astype(b.dtype)


_mm.defvjp(_mm_fwd, _mm_bwd)


def _bdot(a, b, ca, cb):
    return lax.dot_general(a.astype(BF16), b.astype(BF16), (((ca,), (cb,)), ((0,), (0,))), preferred_element_type=F32)


@jax.custom_vjp
def _bmm_nt(a, b):
    return _bdot(a, b, 2, 2)


def _bmm_nt_fwd(a, b):
    return _bdot(a, b, 2, 2), (a, b)


def _bmm_nt_bwd(res, g):
    a, b = res
    return _bdot(g, b, 2, 1), _bdot(g, a, 1, 1)


_bmm_nt.defvjp(_bmm_nt_fwd, _bmm_nt_bwd)


@jax.custom_vjp
def _bmm_nn(a, b):
    return _bdot(a, b, 2, 1)


def _bmm_nn_fwd(a, b):
    return _bdot(a, b, 2, 1), (a, b)


def _bmm_nn_bwd(res, g):
    a, b = res
    return _bdot(g, b, 2, 2), _bdot(a, g, 1, 1)


_bmm_nn.defvjp(_bmm_nn_fwd, _bmm_nn_bwd)


def _hdot(a, b, ca, cb):
    return lax.dot_general(a, b, (((ca,), (cb,)), ((0,), (0,))), precision=lax.Precision.HIGH, preferred_element_type=F32)


def _sigmoid(x):
    return 1.0 / (1.0 + jnp.exp(-x))


def _rms(x):
    return lax.rsqrt(jnp.mean(x * x, axis=-1, keepdims=True) + RMS_EPS)


def _ffn_fwd(x, norm, wg, wu, wd, dep, name):
    t, d = x.shape
    nc = wg.shape[0]
    tm = TOKEN_TILE

    def body(x_ref, n_ref, wg_ref, wu_ref, wd_ref, dep_ref, o_ref):
        xv = x_ref[...]
        h = (xv * _rms(xv) * n_ref[...]).astype(BF16)
        acc = jnp.zeros((tm, d), F32)
        for c in range(nc):
            g = jnp.dot(h, wg_ref[c], preferred_element_type=F32)
            u = jnp.dot(h, wu_ref[c], preferred_element_type=F32)
            a = (g * _sigmoid(g) * u).astype(BF16)
            acc = acc + jnp.dot(a, wd_ref[c], preferred_element_type=F32)
        o_ref[...] = xv + FFN_RESIDUAL * acc

    tile = pl.BlockSpec((tm, d), lambda i: (i, 0))
    return pl.pallas_call(
        body, name=name, grid=(t // tm,), out_shape=jax.ShapeDtypeStruct((t, d), F32),
        in_specs=[tile, pl.BlockSpec((1, d), lambda i: (0, 0)), VMEM_FULL, VMEM_FULL, VMEM_FULL, ANY],
        out_specs=tile, compiler_params=_params("arbitrary"),
    )(x, norm, wg, wu, wd, dep)


def _rmsnorm_bwd(xv, gain, dh):
    rs = _rms(xv)
    xn = xv * rs
    dxn = dh * gain
    dx = rs * (dxn - xn * jnp.mean(dxn * xn, axis=-1, keepdims=True))
    return dx, jnp.sum(dh * xn, axis=0, keepdims=True)


def _ffn_bwd(x, norm, wg, wu, wd, dy, dep, name):
    t, d = x.shape
    nc, _, fc = wg.shape
    tm = TOKEN_TILE
    nt = t // tm

    def body(x_ref, n_ref, wg_ref, wu_ref, wd_ref, dy_ref, dep_ref, dx_ref, dn_ref, dwg_ref, dwu_ref, dwd_ref, dh_ref):
        c, i = pl.program_id(0), pl.program_id(1)
        rows = pl.ds(pl.multiple_of(i * tm, tm), tm)
        xv = x_ref[...]
        gain = n_ref[...]
        h = (xv * _rms(xv) * gain).astype(BF16)
        dy = dy_ref[...]
        dyb = (FFN_RESIDUAL * dy).astype(BF16)
        g = jnp.dot(h, wg_ref[0], preferred_element_type=F32)
        u = jnp.dot(h, wu_ref[0], preferred_element_type=F32)
        sg = _sigmoid(g)
        s = g * sg
        a = (s * u).astype(BF16)
        da = _dot_nt(dyb, wd_ref[0])
        dub = (da * s).astype(BF16)
        dgb = (da * u * (sg * (1.0 + g * (1.0 - sg)))).astype(BF16)
        dwd_c = _dot_tn(a, dyb)
        dwg_c = _dot_tn(h, dgb)
        dwu_c = _dot_tn(h, dub)
        dh_c = _dot_nt(dgb, wg_ref[0]) + _dot_nt(dub, wu_ref[0])

        @pl.when(i == 0)
        def _():
            dwd_ref[0] = dwd_c
            dwg_ref[0] = dwg_c
            dwu_ref[0] = dwu_c

        @pl.when(i > 0)
        def _():
            dwd_ref[0] += dwd_c
            dwg_ref[0] += dwg_c
            dwu_ref[0] += dwu_c

        @pl.when(c == 0)
        def _():
            dh_ref[rows, :] = dh_c

        @pl.when(c > 0)
        def _():
            dh_ref[rows, :] += dh_c

        @pl.when(c == nc - 1)
        def _():
            dx, dn = _rmsnorm_bwd(xv, gain, dh_ref[rows, :])
            dx_ref[...] = dx + dy

            @pl.when(i == 0)
            def _():
                dn_ref[...] = dn

            @pl.when(i > 0)
            def _():
                dn_ref[...] += dn

    tile = pl.BlockSpec((tm, d), lambda c, i: (i, 0))
    row = pl.BlockSpec((1, d), lambda c, i: (0, 0))
    wcol = pl.BlockSpec((1, d, fc), lambda c, i: (c, 0, 0))
    wrow = pl.BlockSpec((1, fc, d), lambda c, i: (c, 0, 0))
    last = pl.BlockSpec((tm, d), lambda c, i: (jnp.where(c == nc - 1, i, 0), 0))
    return pl.pallas_call(
        body, name=name, grid=(nc, nt),
        out_shape=(jax.ShapeDtypeStruct((t, d), F32), jax.ShapeDtypeStruct((1, d), F32),
                   jax.ShapeDtypeStruct(wg.shape, F32), jax.ShapeDtypeStruct(wu.shape, F32),
                   jax.ShapeDtypeStruct(wd.shape, F32)),
        in_specs=[tile, row, wcol, wcol, wrow, tile, ANY],
        out_specs=(last, row, wcol, wcol, wrow),
        scratch_shapes=[pltpu.VMEM((t, d), F32)],
        compiler_params=_params("arbitrary", "arbitrary"),
    )(x, norm, wg, wu, wd, dy, dep)


def _store_heads(ref, v):
    for h in range(ref.shape[0]):
        ref[h] = v[:, h * HEAD_DIM:(h + 1) * HEAD_DIM]


def _load_heads(ref):
    return jnp.concatenate([ref[h] for h in range(ref.shape[0])], axis=-1)


N_HEAD_GROUPS = 3


def _proj_fwd(x, norm, w):
    t, d = x.shape
    ng, _, c = w.shape
    nh = c // HEAD_DIM
    tm = TOKEN_TILE

    def body(x_ref, n_ref, w_ref, q_ref, k_ref, v_ref, cur_ref):
        xv = x_ref[...]
        h = (xv * _rms(xv) * n_ref[...]).astype(BF16)
        for m, ref in enumerate((q_ref, k_ref, v_ref)):
            _store_heads(ref, jnp.dot(h, w_ref[m], preferred_element_type=F32))
        for m in range(N_HEAD_GROUPS, ng):
            j = m - N_HEAD_GROUPS
            cur_ref[:, j * c:(j + 1) * c] = jnp.dot(h, w_ref[m], preferred_element_type=F32)

    heads = pl.BlockSpec((nh, tm, HEAD_DIM), lambda i: (0, i, 0))
    hshape = jax.ShapeDtypeStruct((nh, t, HEAD_DIM), F32)
    wide = (ng - N_HEAD_GROUPS) * c
    return pl.pallas_call(
        body, name="proj_fwd", grid=(t // tm,),
        out_shape=(hshape, hshape, hshape, jax.ShapeDtypeStruct((t, wide), F32)),
        in_specs=[pl.BlockSpec((tm, d), lambda i: (i, 0)), pl.BlockSpec((1, d), lambda i: (0, 0)), VMEM_FULL],
        out_specs=(heads, heads, heads, pl.BlockSpec((tm, wide), lambda i: (i, 0))),
        compiler_params=_params("arbitrary"),
    )(x, norm, w)


def _proj_bwd(x, norm, w, dq, dk, dv, dcur, dres):
    t, d = x.shape
    ng, _, c = w.shape
    nh = c // HEAD_DIM
    tm = TOKEN_TILE

    def body(x_ref, n_ref, w_ref, dq_ref, dk_ref, dv_ref, dcur_ref, dres_ref, dx_ref, dn_ref, dw_ref):
        i = pl.program_id(0)

        @pl.when(i == 0)
        def _():
            dw_ref[...] = jnp.zeros_like(dw_ref)
            dn_ref[...] = jnp.zeros_like(dn_ref)

        xv = x_ref[...]
        gain = n_ref[...]
        h = (xv * _rms(xv) * gain).astype(BF16)
        dh = jnp.zeros((tm, d), F32)
        for m in range(ng):
            j = m - N_HEAD_GROUPS
            dp = _load_heads((dq_ref, dk_ref, dv_ref)[m]) if j < 0 else dcur_ref[:, j * c:(j + 1) * c]
            dp = dp.astype(BF16)
            dw_ref[m] += _dot_tn(h, dp)
            dh = dh + _dot_nt(dp, w_ref[m])
        dx, dn = _rmsnorm_bwd(xv, gain, dh)
        dx_ref[...] = dx + dres_ref[...]
        dn_ref[...] += dn

    tile = pl.BlockSpec((tm, d), lambda i: (i, 0))
    row = pl.BlockSpec((1, d), lambda i: (0, 0))
    heads = pl.BlockSpec((nh, tm, HEAD_DIM), lambda i: (0, i, 0))
    wide = (ng - N_HEAD_GROUPS) * c
    return pl.pallas_call(
        body, name="proj_bwd", grid=(t // tm,),
        out_shape=(jax.ShapeDtypeStruct((t, d), F32), jax.ShapeDtypeStruct((1, d), F32),
                   jax.ShapeDtypeStruct(w.shape, F32)),
        in_specs=[tile, row, VMEM_FULL, heads, heads, heads, pl.BlockSpec((tm, wide), lambda i: (i, 0)), tile],
        out_specs=(tile, row, VMEM_FULL),
        compiler_params=_params("arbitrary"),
    )(x, norm, w, dq, dk, dv, dcur, dres)


def _mixout_fwd(x, att, opg, gate, w):
    t, d = x.shape
    nh = att.shape[0]
    half = gate.shape[1]
    tm = TOKEN_TILE

    def body(x_ref, att_ref, opg_ref, g_ref, w_ref, o_ref):
        mix = jnp.concatenate([_load_heads(att_ref), _load_heads(opg_ref) * g_ref[...]], axis=-1).astype(BF16)
        o_ref[...] = x_ref[...] + jnp.dot(mix, w_ref[...], preferred_element_type=F32)

    tile = pl.BlockSpec((tm, d), lambda i: (i, 0))
    htile = pl.BlockSpec((tm, half), lambda i: (i, 0))
    heads = pl.BlockSpec((nh, tm, HEAD_DIM), lambda i: (0, i, 0))
    return pl.pallas_call(
        body, name="mixout_fwd", grid=(t // tm,), out_shape=jax.ShapeDtypeStruct((t, d), F32),
        in_specs=[tile, heads, heads, htile, VMEM_FULL], out_specs=tile, compiler_params=_params("arbitrary"),
    )(x, att, opg, gate, w)


def _mixout_bwd(att, opg, gate, w, dy, dep):
    nh, t, _ = att.shape
    half = gate.shape[1]
    d = dy.shape[1]
    tm = TOKEN_TILE

    def body(att_ref, opg_ref, g_ref, w_ref, dy_ref, dep_ref, datt_ref, dopg_ref, dg_ref, dw_ref):
        i = pl.program_id(0)
        opg_v, g_v = _load_heads(opg_ref), g_ref[...]
        mix = jnp.concatenate([_load_heads(att_ref), opg_v * g_v], axis=-1).astype(BF16)
        dyb = dy_ref[...].astype(BF16)
        dmix = _dot_nt(dyb, w_ref[...])
        dw = _dot_tn(mix, dyb)
        _store_heads(datt_ref, dmix[:, :half])
        drw = dmix[:, half:]
        _store_heads(dopg_ref, drw * g_v)
        dg_ref[...] = drw * opg_v

        @pl.when(i == 0)
        def _():
            dw_ref[...] = dw

        @pl.when(i > 0)
        def _():
            dw_ref[...] += dw

    tile = pl.BlockSpec((tm, d), lambda i: (i, 0))
    htile = pl.BlockSpec((tm, half), lambda i: (i, 0))
    heads = pl.BlockSpec((nh, tm, HEAD_DIM), lambda i: (0, i, 0))
    hshape = jax.ShapeDtypeStruct((nh, t, HEAD_DIM), F32)
    return pl.pallas_call(
        body, name="mixout_bwd", grid=(t // tm,),
        out_shape=(hshape, hshape, jax.ShapeDtypeStruct((t, half), F32), jax.ShapeDtypeStruct(w.shape, F32)),
        in_specs=[heads, heads, htile, VMEM_FULL, tile, ANY],
        out_specs=(heads, heads, htile, pl.BlockSpec(w.shape, lambda i: (0, 0))),
        compiler_params=_params("arbitrary"),
    )(att, opg, gate, w, dy, dep)


def _loss_head(y, target):
    t, d = y.shape
    tm = TOKEN_TILE

    def body(y_ref, t_ref, dy_ref, loss_ref):
        i = pl.program_id(0)
        err = y_ref[...] - t_ref[...]
        dy_ref[...] = err * (1.0 / d)
        part = 0.5 * jnp.sum(jnp.mean(err * err, axis=-1, keepdims=True), axis=0, keepdims=True)

        @pl.when(i == 0)
        def _():
            loss_ref[...] = jnp.zeros_like(loss_ref)

        loss_ref[...] += jnp.broadcast_to(part, loss_ref.shape)

    tile = pl.BlockSpec((tm, d), lambda i: (i, 0))
    return pl.pallas_call(
        body, name="loss_head", grid=(t // tm,),
        out_shape=(jax.ShapeDtypeStruct((t, d), F32), jax.ShapeDtypeStruct((1, 128), F32)),
        in_specs=[tile, tile], out_specs=(tile, pl.BlockSpec((1, 128), lambda i: (0, 0))),
        compiler_params=_params("arbitrary"),
    )(y, target)


def _att_block(q, kc, vc, qn, kn, kp=None, vp=None, has_prev=True):
    blk = q.shape[1]

    def hn(v, gain):
        return v * _rms(v) * gain

    qh = hn(q, qn)
    scale = HEAD_DIM ** -0.5
    qi = lax.broadcasted_iota(jnp.int32, (blk, blk), 0)
    kj = lax.broadcasted_iota(jnp.int32, (blk, blk), 1)
    sc = jnp.where(kj <= qi, _bmm_nt(qh, hn(kc, kn)) * scale, NEG_INF)
    top = jnp.max(sc, axis=-1, keepdims=True)
    if kp is not None:
        sp = jnp.where((kj >= qi) & has_prev, _bmm_nt(qh, hn(kp, kn)) * scale, NEG_INF)
        top = jnp.maximum(top, jnp.max(sp, axis=-1, keepdims=True))
    m = lax.stop_gradient(top)
    pc = jnp.exp(sc - m)
    den = jnp.sum(pc, axis=-1, keepdims=True)
    acc = _bmm_nn(pc, vc)
    if kp is not None:
        pp = jnp.exp(sp - m)
        den = den + jnp.sum(pp, axis=-1, keepdims=True)
        acc = acc + _bmm_nn(pp, vp)
    o = acc / den
    return o, jnp.broadcast_to(m + jnp.log(den), o.shape)


def _class_rows(n, dil):
    base = n * (ATT_BLOCK * dil)
    if dil == 1:
        return [pl.ds(pl.multiple_of(base, ATT_BLOCK), ATT_BLOCK)]
    return [pl.ds(base + r, ATT_BLOCK, stride=dil) for r in range(dil)]


def _take(ref, rows):
    return jnp.stack([ref[0, r, :] for r in rows])


def _put(ref, rows, val):
    for g, r in enumerate(rows):
        ref[0, r, :] = val[g]


def _put_add(ref, rows, val):
    for g, r in enumerate(rows):
        ref[0, r, :] += val[g]


def _merge_fn(o1, o2, o3, l1, l2, l3):
    m = lax.stop_gradient(jnp.maximum(jnp.maximum(l1, l2), l3))
    e1, e2, e3 = jnp.exp(l1 - m), jnp.exp(l2 - m), jnp.exp(l3 - m)
    return (e1 * o1 + e2 * o2 + e3 * o3) / (e1 + e2 + e3)


def _att_head_specs(t):
    head = pl.BlockSpec((1, t, HEAD_DIM), lambda h: (h, 0, 0))
    gain = pl.BlockSpec((1, 1, HEAD_DIM), lambda h: (0, 0, 0))
    return head, gain


def _for_each_block(t, block):
    carry = None
    for p, dil in enumerate(DILATIONS):
        nb = t // (ATT_BLOCK * dil)
        if dil == 1:
            first = block(p, dil, 0, None, True, carry)
            carry = lax.fori_loop(1, nb, lambda n, c, p=p, dil=dil: block(p, dil, n, n - 1, True, c), first)
        else:
            for n in range(nb):
                carry = block(p, dil, n, n - 1 if n else None, True, carry)
    return carry


def _att_fwd(q, k, v, qn, kn):
    nh, t, dh = q.shape
    head, gain = _att_head_specs(t)

    def body(q_ref, k_ref, v_ref, qn_ref, kn_ref, att_ref, *saved):
        o_refs, l_refs = saved[:3], saved[3:]
        gq, gk = qn_ref[...], kn_ref[...]

        def block(p, dil, n, prev_n, has_prev, carry):
            rows = _class_rows(n, dil)
            args = [_take(q_ref, rows), _take(k_ref, rows), _take(v_ref, rows), gq, gk]
            if prev_n is not None:
                before = _class_rows(prev_n, dil)
                args += [_take(k_ref, before), _take(v_ref, before), has_prev]
            o, lse = _att_block(*args)
            _put(o_refs[p], rows, o)
            _put(l_refs[p], rows, lse)
            return 0

        _for_each_block(t, block)

        def merge(j, carry):
            rows = pl.ds(pl.multiple_of(j * ATT_BLOCK, ATT_BLOCK), ATT_BLOCK)
            att_ref[0, rows, :] = _merge_fn(*[r[0, rows, :] for r in saved])
            return carry

        lax.fori_loop(0, t // ATT_BLOCK, merge, 0)

    return pl.pallas_call(
        body, name="att_fwd", grid=(nh,), out_shape=(jax.ShapeDtypeStruct(q.shape, F32),) * 7,
        in_specs=[head, head, head, gain, gain], out_specs=(head,) * 7, compiler_params=_params("arbitrary"),
    )(q, k, v, qn, kn)


def _att_bwd(q, k, v, qn, kn, saved, datt):
    nh, t, dh = q.shape
    head, gain = _att_head_specs(t)

    def body(q_ref, k_ref, v_ref, qn_ref, kn_ref, o1, o2, o3, l1, l2, l3, datt_ref,
             dq_ref, dk_ref, dv_ref, dqn_ref, dkn_ref):
        for ref in (dq_ref, dk_ref, dv_ref):
            ref[...] = jnp.zeros_like(ref)

        @pl.when(pl.program_id(0) == 0)
        def _():
            dqn_ref[...] = jnp.zeros_like(dqn_ref)
            dkn_ref[...] = jnp.zeros_like(dkn_ref)

        gq, gk = qn_ref[...], kn_ref[...]

        def block(p, dil, n, prev_n, has_prev, carry):
            rows = _class_rows(n, dil)
            _, merge_vjp = jax.vjp(_merge_fn, *[_take(r, rows) for r in (o1, o2, o3, l1, l2, l3)])
            cts = merge_vjp(_take(datt_ref, rows))
            args = [_take(q_ref, rows), _take(k_ref, rows), _take(v_ref, rows), gq, gk]
            if prev_n is not None:
                before = _class_rows(prev_n, dil)
                args += [_take(k_ref, before), _take(v_ref, before)]
            _, block_vjp = jax.vjp(functools.partial(_att_block, has_prev=has_prev), *args)
            grads = block_vjp((cts[p], cts[3 + p]))
            _put_add(dq_ref, rows, grads[0])
            _put_add(dk_ref, rows, grads[1])
            _put_add(dv_ref, rows, grads[2])
            if prev_n is not None:
                _put_add(dk_ref, before, grads[5])
                _put_add(dv_ref, before, grads[6])
            if carry is None:
                return grads[3], grads[4]
            return carry[0] + grads[3], carry[1] + grads[4]

        dgq, dgk = _for_each_block(t, block)
        dqn_ref[...] += dgq
        dkn_ref[...] += dgk

    hshape = jax.ShapeDtypeStruct(q.shape, F32)
    gshape = jax.ShapeDtypeStruct((1, 1, dh), F32)
    return pl.pallas_call(
        body, name="att_bwd", grid=(nh,), out_shape=(hshape, hshape, hshape, gshape, gshape),
        in_specs=[head, head, head, gain, gain] + [head] * 7, out_specs=(head, head, head, gain, gain),
        compiler_params=_params("arbitrary"),
    )(q, k, v, qn, kn, *saved, datt)


RWKV_VEC = ("mu_r", "mu_k", "mu_v", "mu_w", "mu_a", "mu_g", "w0", "a0", "k_k", "k_a")
RWKV_MAT = ("w1", "w2", "a1", "a2", "g1", "g2")


def _rwkv_pre_fn(cur, prev, vec, w1, w2, a1, a2, g1, g2):
    c = cur.shape[1] // 4
    mu_r, mu_k, mu_v, mu_w, mu_a, mu_g, w0, a0, k_k, k_a = (vec[j:j + 1] for j in range(10))

    def lerp(j, mu):
        xc, xp = cur[:, j * c:(j + 1) * c], prev[:, j * c:(j + 1) * c]
        return xc + (xp - xc) * mu

    r, k, v = lerp(0, mu_r), lerp(1, mu_k), lerp(2, mu_v)
    cw, ca, cg = lerp(3, mu_w), lerp(3, mu_a), lerp(3, mu_g)
    z = w0 + _mm(jnp.tanh(_mm(cw, w1)), w2)
    w_log = jnp.minimum(z, 0.0) - jnp.log(1.0 + jnp.exp(-jnp.abs(z))) - 0.5
    lw = -jnp.exp(w_log)
    a = _sigmoid(a0 + _mm(_mm(ca, a1), a2))
    gate = _mm(_sigmoid(_mm(cg, g1)), g2)
    kkraw = k * k_k
    kmod = k * (1.0 + (a - 1.0) * k_a)
    return r, lw, kmod, v, kkraw, a, gate


HALO_ROWS = 8


def _rwkv_pre_specs(c, mats, tile_of):
    tm = TOKEN_TILE
    nh = c // HEAD_DIM
    wide = pl.BlockSpec((tm, 4 * c), lambda j: (tile_of(j), 0))
    halo = pl.BlockSpec((HALO_ROWS, 4 * c), lambda j: (jnp.maximum(tile_of(j) * (tm // HALO_ROWS) - 1, 0), 0))
    one = pl.BlockSpec((tm, c), lambda j: (tile_of(j), 0))
    heads = pl.BlockSpec((nh, tm, HEAD_DIM), lambda j: (0, tile_of(j), 0))
    vec = pl.BlockSpec((10, c), lambda j: (0, 0))
    mspecs = [pl.BlockSpec(m.shape, lambda j: (0, 0)) for m in mats]
    return wide, halo, one, heads, vec, mspecs


def _previous_rows(cur, halo, tile):
    first = jnp.where(tile > 0, halo[HALO_ROWS - 1:HALO_ROWS], 0.0)
    rows = lax.broadcasted_iota(jnp.int32, cur.shape, 0)
    return jnp.where(rows == 0, first, pltpu.roll(cur, 1, axis=0))


def _rwkv_pre_fwd(cur, vec, mats):
    t, c4 = cur.shape
    c = c4 // 4
    wide, halo, one, heads, vspec, mspecs = _rwkv_pre_specs(c, mats, lambda j: j)

    def body(cur_ref, halo_ref, vec_ref, *rest):
        mrefs, outs = rest[:6], rest[6:]
        cur_v = cur_ref[...]
        prev = _previous_rows(cur_v, halo_ref[...], pl.program_id(0))
        vals = _rwkv_pre_fn(cur_v, prev, vec_ref[...], *(m[...] for m in mrefs))
        for ref, val in zip(outs[:6], vals[:6]):
            _store_heads(ref, val)
        outs[6][...] = vals[6]

    hshape = jax.ShapeDtypeStruct((c // HEAD_DIM, t, HEAD_DIM), F32)
    return pl.pallas_call(
        body, name="rwkv_pre_fwd", grid=(t // TOKEN_TILE,), out_shape=(hshape,) * 6 + (jax.ShapeDtypeStruct((t, c), F32),),
        in_specs=[wide, halo, vspec] + mspecs, out_specs=(heads,) * 6 + (one,), compiler_params=_params("arbitrary"),
    )(cur, cur, vec, *mats)


def _rwkv_pre_bwd(cur, vec, mats, cts, dgate):
    t, c4 = cur.shape
    c = c4 // 4
    tm = TOKEN_TILE
    nt = t // tm
    wide, halo, one, heads, vspec, mspecs = _rwkv_pre_specs(c, mats, lambda j: nt - 1 - j)

    def body(cur_ref, halo_ref, vec_ref, *rest):
        mrefs, ctrefs, dgate_ref, outs, carry_ref = rest[:6], rest[6:12], rest[12], rest[13:-1], rest[-1]
        j = pl.program_id(0)

        @pl.when(j == 0)
        def _():
            carry_ref[...] = jnp.zeros_like(carry_ref)
            for ref in outs[1:]:
                ref[...] = jnp.zeros_like(ref)

        cur_v = cur_ref[...]
        prev = _previous_rows(cur_v, halo_ref[...], nt - 1 - j)
        _, vjp = jax.vjp(_rwkv_pre_fn, cur_v, prev, vec_ref[...], *(m[...] for m in mrefs))
        grads = vjp(tuple(_load_heads(r) for r in ctrefs) + (dgate_ref[...],))
        dprev = grads[1]
        rows = lax.broadcasted_iota(jnp.int32, dprev.shape, 0)
        outs[0][...] = grads[0] + jnp.where(rows == tm - 1, carry_ref[0:1], pltpu.roll(dprev, tm - 1, axis=0))
        carry_ref[0:1] = dprev[0:1]
        for ref, val in zip(outs[1:], grads[2:]):
            ref[...] += val

    return pl.pallas_call(
        body, name="rwkv_pre_bwd", grid=(nt,),
        out_shape=(jax.ShapeDtypeStruct(cur.shape, F32), jax.ShapeDtypeStruct(vec.shape, F32))
        + tuple(jax.ShapeDtypeStruct(m.shape, F32) for m in mats),
        in_specs=[wide, halo, vspec] + mspecs + [heads] * 6 + [one], out_specs=(wide, vspec) + tuple(mspecs),
        scratch_shapes=[pltpu.VMEM((HALO_ROWS, c4), F32)], compiler_params=_params("arbitrary"),
    )(cur, cur, vec, *mats, *cts, dgate)


def _scan_chunk_fn(h0, r, lw, k, v, kkraw, a, rk, lnw, lnb):
    n = r.shape[1]
    nrm = jnp.sqrt(jnp.sum(kkraw * kkraw, axis=-1, keepdims=True))
    kk = kkraw / jnp.maximum(nrm, 1e-12)
    av, bv = -kk, kk * a
    ti = lax.broadcasted_iota(jnp.int32, (n, n), 0)
    si = lax.broadcasted_iota(jnp.int32, (n, n), 1)
    incl, strict = ti >= si, ti > si
    ones = jnp.broadcast_to(incl.astype(F32)[None], (r.shape[0], n, n))
    cum = _hdot(ones, lw, 2, 1)
    at, rt = av * jnp.exp(cum - lw), r * jnp.exp(cum)
    inv = jnp.exp(-cum)
    bt, kt = bv * inv, k * inv
    lab = jnp.where(strict, _hdot(at, bt, 2, 2), 0.0)
    lak = jnp.where(strict, _hdot(at, kt, 2, 2), 0.0)
    rb = jnp.where(incl, _hdot(rt, bt, 2, 2), 0.0)
    rkm = jnp.where(incl, _hdot(rt, kt, 2, 2), 0.0)
    u = _hdot(at, h0, 2, 1) + _hdot(lak, v, 2, 1)
    p = lab
    m = 1
    while m < n:
        u = u + _hdot(p, u, 2, 1)
        m *= 2
        if m < n:
            p = _hdot(p, p, 2, 1)
    y = _hdot(rt, h0, 2, 1) + _hdot(rb, u, 2, 1) + _hdot(rkm, v, 2, 1)
    last = jnp.exp(jnp.sum(lw, axis=1, keepdims=True))
    h1 = jnp.swapaxes(last, 1, 2) * (h0 + _hdot(bt, u, 1, 1) + _hdot(kt, v, 1, 1))
    mean = jnp.mean(y, axis=-1, keepdims=True)
    yc = y - mean
    var = jnp.mean(yc * yc, axis=-1, keepdims=True)
    yn = yc * lax.rsqrt(var + GN_EPS) * lnw + lnb
    bonus = jnp.sum(r * k * rk, axis=-1, keepdims=True) * v
    return yn + bonus, h1


def _scan_specs(h, t, dh, rev):
    n = SCAN_CHUNK
    nc = t // n
    pos = (lambda c: (0, nc - 1 - c, 0)) if rev else (lambda c: (0, c, 0))
    st = (lambda c: (nc - 1 - c, 0, 0, 0)) if rev else (lambda c: (c, 0, 0, 0))
    seq = pl.BlockSpec((h, n, dh), pos)
    par = pl.BlockSpec((h, 1, dh), lambda c: (0, 0, 0))
    state = pl.BlockSpec((1, h, dh, dh), st)
    return seq, par, state


def _scan_fwd(seqs, pars):
    h, t, dh = seqs[0].shape
    nc = t // SCAN_CHUNK
    seq, par, state = _scan_specs(h, t, dh, False)

    def body(r, lw, k, v, kkraw, a, rk, lnw, lnb, o_ref, st_ref, h_ref):
        @pl.when(pl.program_id(0) == 0)
        def _():
            h_ref[...] = jnp.zeros_like(h_ref)

        h0 = h_ref[...]
        st_ref[0] = h0
        o, h1 = _scan_chunk_fn(h0, r[...], lw[...], k[...], v[...], kkraw[...], a[...], rk[...], lnw[...], lnb[...])
        o_ref[...] = o
        h_ref[...] = h1

    return pl.pallas_call(
        body, name="rwkv_scan_fwd", grid=(nc,),
        out_shape=(jax.ShapeDtypeStruct((h, t, dh), F32), jax.ShapeDtypeStruct((nc, h, dh, dh), F32)),
        in_specs=[seq] * 6 + [par] * 3, out_specs=(seq, state),
        scratch_shapes=[pltpu.VMEM((h, dh, dh), F32)], compiler_params=_params("arbitrary"),
    )(*seqs, *pars)


def _scan_bwd(seqs, pars, states, do):
    h, t, dh = seqs[0].shape
    nc = t // SCAN_CHUNK
    seq, par, state = _scan_specs(h, t, dh, True)

    def body(r, lw, k, v, kkraw, a, rk, lnw, lnb, st_ref, do_ref, *rest):
        douts, dpars, dh_ref = rest[:6], rest[6:9], rest[9]
        first = pl.program_id(0) == 0

        @pl.when(first)
        def _():
            dh_ref[...] = jnp.zeros_like(dh_ref)

        _, vjp = jax.vjp(_scan_chunk_fn, st_ref[0], r[...], lw[...], k[...], v[...], kkraw[...], a[...],
                         rk[...], lnw[...], lnb[...])
        grads = vjp((do_ref[...], dh_ref[...]))
        dh_ref[...] = grads[0]
        for ref, val in zip(douts, grads[1:7]):
            ref[...] = val

        @pl.when(first)
        def _():
            for ref, val in zip(dpars, grads[7:]):
                ref[...] = val

        @pl.when(jnp.logical_not(first))
        def _():
            for ref, val in zip(dpars, grads[7:]):
                ref[...] += val

    sshape = jax.ShapeDtypeStruct((h, t, dh), F32)
    pshape = jax.ShapeDtypeStruct((h, 1, dh), F32)
    return pl.pallas_call(
        body, name="rwkv_scan_bwd", grid=(nc,), out_shape=(sshape,) * 6 + (pshape,) * 3,
        in_specs=[seq] * 6 + [par] * 3 + [state, seq], out_specs=(seq,) * 6 + (par,) * 3,
        scratch_shapes=[pltpu.VMEM((h, dh, dh), F32)], compiler_params=_params("arbitrary"),
    )(*seqs, *pars, states, do)


def _local_step(x, target, w, ex):
    w = dict(w)
    c = w["mu_r"].shape[-1]
    qn, kn = w["q_norm"].reshape(1, 1, HEAD_DIM), w["k_norm"].reshape(1, 1, HEAD_DIM)
    vec = jnp.concatenate([w[n].reshape(1, c) for n in RWKV_VEC], axis=0)
    pars = [w[n].reshape(-1, 1, HEAD_DIM) for n in ("r_k", "ln_x_w", "ln_x_b")]
    no_dep = jnp.zeros(DEP_SHAPE, F32)

    x1 = _ffn_fwd(x, w["ffn1_norm"], w["ffn1_w_gate"], w["ffn1_w_up"], w["ffn1_w_down"], ex.first_dep, "ffn1_fwd")
    w.update(ex.mix_weights((x1,)))
    mats = [w[n] for n in RWKV_MAT]
    q, k, v, cur = _proj_fwd(x1, w["mix_norm"], w["w_in"])
    att, *saved = _att_fwd(q, k, v, qn, kn)
    pre = _rwkv_pre_fwd(cur, vec, mats)
    seqs, gate = pre[:6], pre[6]
    opg, states = _scan_fwd(seqs, pars)
    w.update(ex.out_weights((att, opg)))
    x2 = _mixout_fwd(x1, att, opg, gate, w["w_out"])
    x3 = _ffn_fwd(x2, w["ffn2_norm"], w["ffn2_w_gate"], w["ffn2_w_up"], w["ffn2_w_down"], no_dep, "ffn2_fwd")
    dy, loss = _loss_head(x3, target)

    g = {}
    dx2, g["ffn2_norm"], g["ffn2_w_gate"], g["ffn2_w_up"], g["ffn2_w_down"] = _ffn_bwd(
        x2, w["ffn2_norm"], w["ffn2_w_gate"], w["ffn2_w_up"], w["ffn2_w_down"], dy, no_dep, "ffn2_bwd")
    dep = ex.send_ffn2({n: g[n] for n in ("ffn2_w_gate", "ffn2_w_up", "ffn2_w_down")})
    datt, dopg, dgate, g["w_out"] = _mixout_bwd(att, opg, gate, w["w_out"], dx2, dep)
    dscan = _scan_bwd(seqs, pars, states, dopg)
    for n, d in zip(("r_k", "ln_x_w", "ln_x_b"), dscan[6:]):
        g[n] = d
    dcur, dvec, *dmats = _rwkv_pre_bwd(cur, vec, mats, dscan[:6], dgate)
    for n, d in zip(RWKV_MAT, dmats):
        g[n] = d
    for j, n in enumerate(RWKV_VEC):
        g[n] = dvec[j:j + 1]
    dq, dk, dv, g["q_norm"], g["k_norm"] = _att_bwd(q, k, v, qn, kn, saved, datt)
    dx1, g["mix_norm"], g["w_in"] = _proj_bwd(x1, w["mix_norm"], w["w_in"], dq, dk, dv, dcur, dx2)
    dep = ex.send_mix({n: g[n] for n in ("w_in", "w_out") + RWKV_MAT}, (dx1,))
    dx, g["ffn1_norm"], g["ffn1_w_gate"], g["ffn1_w_up"], g["ffn1_w_down"] = _ffn_bwd(
        x, w["ffn1_norm"], w["ffn1_w_gate"], w["ffn1_w_up"], w["ffn1_w_down"], dx1, dep, "ffn1_bwd")
    return loss, dx, g


N_SHARDS = 4


def _place():
    return lax.axis_index("x"), lax.axis_index("y"), lax.axis_index("c")


def _chip_peers(x, y):
    return [(1 - x, y), (x, 1 - y), (1 - x, 1 - y)]


def _gather_xy(shards):
    n = len(shards)

    def body(*refs):
        ins, outs = refs[:n], refs[n:2 * n]
        send_sems, recv_sems, local_sems = refs[2 * n:]
        x, y, c = _place()
        me = 2 * x + y
        copies = []
        for i in range(n):
            own = pltpu.make_async_copy(ins[i], outs[i].at[me], local_sems.at[i])
            own.start()
            copies.append(own)
            for k, (px, py) in enumerate(_chip_peers(x, y)):
                cp = pltpu.make_async_remote_copy(
                    src_ref=ins[i], dst_ref=outs[i].at[me], send_sem=send_sems.at[i, k], recv_sem=recv_sems.at[i, k],
                    device_id=(px, py, c), device_id_type=MESH)
                cp.start()
                copies.append(cp)
        for cp in copies:
            cp.wait()

    return pl.pallas_call(
        body, name="gather_weights",
        out_shape=tuple(jax.ShapeDtypeStruct((N_SHARDS,) + s.shape, s.dtype) for s in shards),
        in_specs=[ANY] * n, out_specs=(ANY,) * n,
        scratch_shapes=[pltpu.SemaphoreType.DMA((n, 3)), pltpu.SemaphoreType.DMA((n, 3)), pltpu.SemaphoreType.DMA((n,))],
    )(*shards)


def _scatter_partials(parts):
    n = len(parts)

    def body(*refs):
        ins, outs = refs[:n], refs[n:2 * n]
        send_sems, recv_sems = refs[2 * n:]
        x, y, c = _place()
        copies = []
        for i in range(n):
            for k, (px, py) in enumerate(_chip_peers(x, y)):
                cp = pltpu.make_async_remote_copy(
                    src_ref=ins[i].at[2 * px + py], dst_ref=outs[i].at[k], send_sem=send_sems.at[i, k],
                    recv_sem=recv_sems.at[i, k], device_id=(px, py, c), device_id_type=MESH)
                cp.start()
                copies.append(cp)
        for cp in copies:
            cp.wait()

    return pl.pallas_call(
        body, name="scatter_partials",
        out_shape=tuple(jax.ShapeDtypeStruct((3,) + p.shape[1:], p.dtype) for p in parts),
        in_specs=[ANY] * n, out_specs=(ANY,) * n,
        scratch_shapes=[pltpu.SemaphoreType.DMA((n, 3)), pltpu.SemaphoreType.DMA((n, 3))],
    )(*parts)


HBM = pl.BlockSpec(memory_space=pltpu.HBM)
SEM = pl.BlockSpec(memory_space=pltpu.SEMAPHORE)
DEP_SHAPE = (8, 128)


def _gather_views(i, srcs, lands, k, px, py, me):
    return (srcs[i], lands[i].at[me]), (srcs[i], lands[i].at[2 * px + py])


def _scatter_views(i, srcs, lands, k, px, py, me):
    return (srcs[i].at[2 * px + py], lands[i].at[k]), (srcs[i].at[me], lands[i].at[k])


def _push_start(srcs, land_shapes, views, own_slot, after, name):
    n = len(srcs)

    def body(*refs):
        src_refs, land_refs = refs[:n], refs[n:2 * n]
        send_sems, recv_sems = refs[2 * n + 1:2 * n + 3]
        token = refs[4 * n + 3]
        x, y, c = _place()
        me = 2 * x + y
        for i in range(n):
            for k, (px, py) in enumerate(_chip_peers(x, y)):
                (src, dst), _ = views(i, src_refs, land_refs, k, px, py, me)
                pltpu.make_async_remote_copy(
                    src_ref=src, dst_ref=dst, send_sem=send_sems.at[3 * i + k], recv_sem=recv_sems.at[3 * i + k],
                    device_id=(px, py, c), device_id_type=MESH).start()
        token[...] = jnp.zeros_like(token)

    sems = pltpu.SemaphoreType.DMA((3 * n,))
    lands = [lax.empty(s.shape, s.dtype) for s in land_shapes]
    if own_slot:
        me = 2 * lax.axis_index("x") + lax.axis_index("y")
        lands = [lax.dynamic_update_index_in_dim(z, s, me, 0) for z, s in zip(lands, srcs)]
    lands = [pltpu.with_memory_space_constraint(z, pltpu.HBM) for z in lands]
    srcs = [pltpu.with_memory_space_constraint(s, pltpu.HBM) for s in srcs]
    outs = pl.pallas_call(
        body, name=name,
        out_shape=(sems, sems, *[pltpu.HBM(s.shape, s.dtype) for s in srcs], *[pltpu.HBM(s.shape, s.dtype) for s in land_shapes],
                   jax.ShapeDtypeStruct(DEP_SHAPE, F32)),
        in_specs=[HBM] * (2 * n) + [ANY], out_specs=(SEM, SEM, *[HBM] * (2 * n), VMEM_FULL),
        input_output_aliases={i: 2 + i for i in range(2 * n)},
        compiler_params=pltpu.CompilerParams(has_side_effects=pltpu.SideEffectType.DATAFLOW_SIDE_EFFECTING),
    )(*srcs, *lands, after)
    return outs[0], outs[1], outs[2:2 + n], outs[2 + n:2 + 2 * n], outs[2 + 2 * n]


def _push_wait(started, views, after, name):
    send_sems, recv_sems, srcs, lands, _ = started
    n = len(srcs)

    def body(*refs):
        src_refs, land_refs = refs[:n], refs[n:2 * n]
        send_sems, recv_sems = refs[2 * n:2 * n + 2]
        x, y, c = _place()
        me = 2 * x + y
        for i in range(n):
            for k, (px, py) in enumerate(_chip_peers(x, y)):
                _, (src, dst) = views(i, src_refs, land_refs, k, px, py, me)
                landing = pltpu.make_async_remote_copy(
                    src_ref=src, dst_ref=dst, send_sem=send_sems.at[3 * i + k], recv_sem=recv_sems.at[3 * i + k],
                    device_id=(px, py, c), device_id_type=MESH)
                landing.wait_send()
                landing.wait_recv()

    outs = pl.pallas_call(
        body, name=name,
        out_shape=tuple(pltpu.HBM(s.shape, s.dtype) for s in (*srcs, *lands)),
        in_specs=[HBM] * (2 * n) + [SEM, SEM] + [ANY] * len(after), out_specs=(HBM,) * (2 * n),
        input_output_aliases={i: i for i in range(2 * n)},
        compiler_params=pltpu.CompilerParams(has_side_effects=pltpu.SideEffectType.DATAFLOW_SIDE_EFFECTING),
    )(*srcs, *lands, send_sems, recv_sems, *after)
    return outs[n:]


def _sibling_swap(arrays):
    n = len(arrays)

    def body(*refs):
        ins, outs = refs[:n], refs[n:2 * n]
        send_sems, recv_sems = refs[2 * n:]
        x, y, c = _place()
        copies = []
        for i in range(n):
            cp = pltpu.make_async_remote_copy(
                src_ref=ins[i], dst_ref=outs[i], send_sem=send_sems.at[i], recv_sem=recv_sems.at[i],
                device_id=(x, y, 1 - c), device_id_type=MESH)
            cp.start()
            copies.append(cp)
        for cp in copies:
            cp.wait()

    return pl.pallas_call(
        body, name="sibling_swap",
        out_shape=tuple(jax.ShapeDtypeStruct(a.shape, a.dtype) for a in arrays),
        in_specs=[ANY] * n, out_specs=(ANY,) * n,
        scratch_shapes=[pltpu.SemaphoreType.DMA((n,)), pltpu.SemaphoreType.DMA((n,))],
    )(*arrays)


N_DEV = 8


def _allreduce_small(pack):
    def body(in_ref, out_ref, buf, send_sems, recv_sems):
        x, y, c = _place()
        me = 4 * x + 2 * y + c
        buf[me] = in_ref[...]

        def copy(j, slot):
            px, py, pc = x ^ (j >> 2), y ^ ((j >> 1) & 1), c ^ (j & 1)
            return pltpu.make_async_remote_copy(
                src_ref=in_ref, dst_ref=buf.at[slot(px, py, pc)], send_sem=send_sems.at[j], recv_sem=recv_sems.at[j],
                device_id=(px, py, pc), device_id_type=MESH)

        for j in range(1, N_DEV):
            copy(j, lambda px, py, pc: me).start()
        for j in range(1, N_DEV):
            landing = copy(j, lambda px, py, pc: 4 * px + 2 * py + pc)
            landing.wait_send()
            landing.wait_recv()
        acc = buf[0]
        for s in range(1, N_DEV):
            acc = acc + buf[s]
        out_ref[...] = acc

    return pl.pallas_call(
        body, name="allreduce_small", out_shape=jax.ShapeDtypeStruct(pack.shape, F32),
        in_specs=[VMEM_FULL], out_specs=VMEM_FULL,
        scratch_shapes=[pltpu.VMEM((N_DEV,) + pack.shape, F32), pltpu.SemaphoreType.DMA((N_DEV,)),
                        pltpu.SemaphoreType.DMA((N_DEV,))],
    )(pack)


ROW_TILE_MAX = 256
BF16_SUBLANES = 16


def _row_tile(rows):
    for tr in range(min(rows, ROW_TILE_MAX), 0, -1):
        if rows % tr == 0 and tr % BF16_SUBLANES == 0:
            return tr
    return rows


def _reduce_own(me, part, recv, name):
    _, r, cols = part.shape
    tr = _row_tile(r)

    def body(me_ref, p_ref, rv_ref, o_ref):
        acc = p_ref[0]
        for k in range(3):
            acc = acc + rv_ref[k].astype(F32)
        o_ref[...] = acc

    return pl.pallas_call(
        body, name=name, out_shape=jax.ShapeDtypeStruct((r, cols), F32),
        grid_spec=pltpu.PrefetchScalarGridSpec(
            num_scalar_prefetch=1, grid=(r // tr,),
            in_specs=[pl.BlockSpec((1, tr, cols), lambda i, me_ref: (me_ref[0], i, 0)),
                      pl.BlockSpec((3, tr, cols), lambda i, me_ref: (0, i, 0))],
            out_specs=pl.BlockSpec((tr, cols), lambda i, me_ref: (i, 0))),
        compiler_params=_params("arbitrary"),
    )(me, part, recv)


def _adamw(w, ga, gb, m, v, name):
    r, cols = w.shape
    tr = _row_tile(r)
    c1 = 1.0 - ADAM_B1 ** ADAM_STEP
    c2 = 1.0 - ADAM_B2 ** ADAM_STEP

    def body(w_ref, ga_ref, gb_ref, m_ref, v_ref, g_out, d_out, m_out, v_out):
        g = ga_ref[...] + gb_ref[...]
        mn = ADAM_B1 * m_ref[...] + (1.0 - ADAM_B1) * g
        vn = ADAM_B2 * v_ref[...] + (1.0 - ADAM_B2) * (g * g)
        g_out[...] = g
        m_out[...] = mn
        v_out[...] = vn
        d_out[...] = -ADAM_LR * ((mn / c1) / (jnp.sqrt(vn / c2) + ADAM_EPS) + ADAM_WD * w_ref[...])

    tile = pl.BlockSpec((tr, cols), lambda i: (i, 0))
    shape = jax.ShapeDtypeStruct((r, cols), F32)
    return pl.pallas_call(
        body, name=name, grid=(r // tr,), out_shape=(shape,) * 4, in_specs=[tile] * 5, out_specs=(tile,) * 4,
        compiler_params=_params("arbitrary"),
    )(w, ga, gb, m, v)


PACK_COLS = 512


def _to_rows(a):
    flat = a.reshape(-1)
    pad = (-flat.shape[0]) % PACK_COLS
    return jnp.pad(flat, (0, pad)).reshape(-1, PACK_COLS)


def _pack(arrays, extra_rows=0):
    rows = [_to_rows(a) for a in arrays]
    n = sum(r.shape[0] for r in rows) + extra_rows
    pad = (-n) % 8
    return jnp.concatenate(rows + [jnp.zeros((extra_rows + pad, PACK_COLS), F32)], axis=0)


def _unpack(pack, like):
    out, at = [], 0
    for a in like:
        n = -(-a.size // PACK_COLS)
        out.append(pack[at:at + n].reshape(-1)[:a.size].reshape(a.shape))
        at += n
    return out


COL_SHARDED = ("ffn1_w_gate", "ffn1_w_up", "w_in", "ffn2_w_gate", "ffn2_w_up", "w2", "a2", "g2")
ROW_SHARDED = ("ffn1_w_down", "ffn2_w_down", "w_out", "w1", "a1", "g1")
CHUNKED = ("ffn1_w_gate", "ffn1_w_up", "ffn1_w_down", "ffn2_w_gate", "ffn2_w_up", "ffn2_w_down")
WEIGHTS = ("ffn1_norm", "ffn1_w_gate", "ffn1_w_up", "ffn1_w_down", "mix_norm", "w_in", "q_norm", "k_norm",
           "mu_r", "mu_k", "mu_v", "mu_w", "mu_a", "mu_g", "w0", "w1", "w2", "a0", "a1", "a2", "g1", "g2",
           "k_k", "k_a", "r_k", "ln_x_w", "ln_x_b", "w_out", "ffn2_norm", "ffn2_w_gate", "ffn2_w_up", "ffn2_w_down")


W_IN_GROUPS = 7


def _full_from_blocks(name, blocks):
    if name in CHUNKED:
        return blocks
    if name in ROW_SHARDED:
        return blocks.reshape(-1, blocks.shape[-1])
    full = blocks.transpose(1, 0, 2).reshape(blocks.shape[1], -1)
    if name == "w_in":
        return full.reshape(full.shape[0], W_IN_GROUPS, -1).transpose(1, 0, 2)
    return full


def _blocks_from_full(name, full):
    if name in CHUNKED:
        return full
    if name in ROW_SHARDED:
        return full.reshape(N_SHARDS, -1, full.shape[-1])
    if name == "w_in":
        full = full.transpose(1, 0, 2).reshape(full.shape[1], -1)
    return full.reshape(full.shape[0], N_SHARDS, -1).transpose(1, 0, 2)


FFN1_GROUP = ("ffn1_w_gate", "ffn1_w_up", "ffn1_w_down")
MIX_GROUP = ("w_in",) + RWKV_MAT
OUT_GROUP = ("w_out", "ffn2_w_gate", "ffn2_w_up", "ffn2_w_down")
FFN2_GROUP = OUT_GROUP[1:]
LATE_GROUP = ("w_in", "w_out") + RWKV_MAT


class _Exchange:
    def __init__(self, given):
        self.given = given
        first = _gather_xy(self._shards(FFN1_GROUP))
        self.first_weights = self._full(FFN1_GROUP, first)
        self.mix = self._gather_start(MIX_GROUP, first[0], "gather_mix_start")
        self.out = self._gather_start(OUT_GROUP, self.mix[4], "gather_out_start")
        self.first_dep = self.out[4]
        self.parts, self.recv = {}, {}

    def _shards(self, names):
        return [self.given[n][0].astype(BF16) for n in names]

    @staticmethod
    def _full(names, blocks):
        out = {}
        for n, b in zip(names, blocks):
            full = _full_from_blocks(n, b)
            out[n] = full.astype(F32) if n in RWKV_MAT else full
        return out

    def _gather_start(self, names, after, name):
        shards = self._shards(names)
        lands = [jax.ShapeDtypeStruct((N_SHARDS,) + s.shape, s.dtype) for s in shards]
        return _push_start(shards, lands, _gather_views, True, after, name)

    def mix_weights(self, after):
        return self._full(MIX_GROUP, _push_wait(self.mix, _gather_views, after, "gather_mix_wait"))

    def out_weights(self, after):
        return self._full(OUT_GROUP, _push_wait(self.out, _gather_views, after, "gather_out_wait"))

    def _scatter_start(self, grads, name):
        names = tuple(grads)
        parts = [_blocks_from_full(n, grads[n]) for n in names]
        self.parts.update(zip(names, parts))
        lands = [jax.ShapeDtypeStruct((3,) + p.shape[1:], BF16) for p in parts]
        return _push_start([p.astype(BF16) for p in parts], lands, _scatter_views, False, parts[0], name)

    def send_ffn2(self, grads):
        self.ffn2 = self._scatter_start(grads, "scatter_ffn2_start")
        return self.ffn2[4]

    def send_mix(self, grads, after):
        self.recv.update(zip(FFN2_GROUP, _push_wait(self.ffn2, _scatter_views, after, "scatter_ffn2_wait")))
        self.late = self._scatter_start(grads, "scatter_late_start")
        return self.late[4]

    def finish(self, grads):
        names = tuple(grads)
        parts = [_blocks_from_full(n, grads[n]) for n in names]
        self.parts.update(zip(names, parts))
        got = _scatter_partials([p.astype(BF16) for p in parts])
        self.recv.update(zip(names, got))
        self.recv.update(zip(LATE_GROUP, _push_wait(self.late, _scatter_views, (got[0],), "scatter_late_wait")))
        return self.parts, self.recv


def kernel(
        x, ffn1_norm, ffn1_w_gate, ffn1_w_up, ffn1_w_down, mix_norm, w_in, q_norm, k_norm, mu_r, mu_k, mu_v, mu_w,
        mu_a, mu_g, w0, w1, w2, a0, a1, a2, g1, g2, k_k, k_a, r_k, ln_x_w, ln_x_b, w_out, ffn2_norm, ffn2_w_gate,
        ffn2_w_up, ffn2_w_down, loss_target, m_ffn1_norm, m_ffn1_w_gate, m_ffn1_w_up, m_ffn1_w_down, m_mix_norm,
        m_w_in, m_q_norm, m_k_norm, m_mu_r, m_mu_k, m_mu_v, m_mu_w, m_mu_a, m_mu_g, m_w0, m_w1, m_w2, m_a0, m_a1,
        m_a2, m_g1, m_g2, m_k_k, m_k_a, m_r_k, m_ln_x_w, m_ln_x_b, m_w_out, m_ffn2_norm, m_ffn2_w_gate, m_ffn2_w_up,
        m_ffn2_w_down, v_ffn1_norm, v_ffn1_w_gate, v_ffn1_w_up, v_ffn1_w_down, v_mix_norm, v_w_in, v_q_norm, v_k_norm,
        v_mu_r, v_mu_k, v_mu_v, v_mu_w, v_mu_a, v_mu_g, v_w0, v_w1, v_w2, v_a0, v_a1, v_a2, v_g1, v_g2, v_k_k, v_k_a,
        v_r_k, v_ln_x_w, v_ln_x_b, v_w_out, v_ffn2_norm, v_ffn2_w_gate, v_ffn2_w_up, v_ffn2_w_down):
    given = dict(locals())
    sharded = COL_SHARDED + ROW_SHARDED
    sharded = tuple(n for n in WEIGHTS if n in sharded)
    small = tuple(n for n in WEIGHTS if n not in sharded)

    ex = _Exchange(given)
    w = {n: given[n] for n in small}
    w.update(ex.first_weights)
    loss, dx, g = _local_step(x[0], loss_target[0], w, ex)
    parts, recv = ex.finish({n: g[n] for n in FFN1_GROUP})

    me = (2 * lax.axis_index("x") + lax.axis_index("y")).astype(jnp.int32).reshape(1)
    mine = []
    for n in sharded:
        p, rv = parts[n], recv[n]
        p2 = p.reshape(N_SHARDS, -1, p.shape[-1])
        mine.append(_reduce_own(me, p2, rv.reshape(3, -1, rv.shape[-1]), f"reduce_{n}"))
    theirs = _sibling_swap(mine)
    out = {}
    for n, a, b in zip(sharded, mine, theirs):
        shape = given[n].shape
        two_d = (-1, shape[-1])
        res = _adamw(given[n].reshape(two_d), a, b, given["m_" + n].reshape(two_d), given["v_" + n].reshape(two_d), f"adamw_{n}")
        out[n] = [r.reshape(shape) for r in res]

    gpack = _pack([g[n] for n in small], extra_rows=1)
    n_rows = sum(-(-given[n].size // PACK_COLS) for n in small)
    gpack = gpack.at[n_rows, :loss.shape[1]].set(loss[0])
    gsum = _allreduce_small(gpack)
    res = _adamw(_pack([given[n] for n in small], 1), gsum, jnp.zeros_like(gsum), _pack([given["m_" + n] for n in small], 1),
                 _pack([given["v_" + n] for n in small], 1), "adamw_small")
    like = [given[n] for n in small]
    for j, r in enumerate(res):
        for n, a in zip(small, _unpack(r, like)):
            out.setdefault(n, [None] * 4)[j] = a
    total_loss = gsum[n_rows, 0]
    return (total_loss, dx[None], *[out[n][0] for n in WEIGHTS], *[out[n][1] for n in WEIGHTS],
            *[out[n][2] for n in WEIGHTS], *[out[n][3] for n in WEIGHTS])
```

```python
import functools

import jax
import jax.numpy as jnp
from jax import lax
from jax.experimental import pallas as pl
from jax.experimental.pallas import tpu as pltpu

F32 = jnp.float32
BF16 = jnp.bfloat16
MESH = pl.DeviceIdType.MESH

RMS_EPS = 1e-6
GN_EPS = 64e-5
NEG_INF = -1e30
FFN_RESIDUAL = 0.5
HEAD_DIM = 64
ATT_BLOCK = 128
DILATIONS = (1, 4, 16)
SCAN_CHUNK = 64
TOKEN_TILE = 256

ADAM_LR = 0.001
ADAM_B1 = 0.9
ADAM_B2 = 0.999
ADAM_EPS = 1e-08
ADAM_WD = 0.01
ADAM_STEP = 10

VMEM_FULL = pl.BlockSpec(memory_space=pltpu.VMEM)
ANY = pl.BlockSpec(memory_space=pl.ANY)


VMEM_LIMIT = 56 * 1024 * 1024


def _params(*sem):
    return pltpu.CompilerParams(dimension_semantics=sem, vmem_limit_bytes=VMEM_LIMIT)


def _dot(a, b, dims):
    return lax.dot_general(a.astype(BF16), b.astype(BF16), (dims, ((), ())), preferred_element_type=F32)


def _dot_nn(a, b):
    return _dot(a, b, ((1,), (0,)))


def _dot_nt(a, b):
    return _dot(a, b, ((1,), (1,)))


def _dot_tn(a, b):
    return _dot(a, b, ((0,), (0,)))


@jax.custom_vjp
def _mm(a, b):
    return _dot_nn(a, b)


def _mm_fwd(a, b):
    return _dot_nn(a, b), (a, b)


def _mm_bwd(res, g):
    a, b = res
    return _dot_nt(g, b).astype(a.dtype), _dot_tn(a, g).astype(b.dtype)


_mm.defvjp(_mm_fwd, _mm_bwd)


def _bdot(a, b, ca, cb):
    return lax.dot_general(a.astype(BF16), b.astype(BF16), (((ca,), (cb,)), ((0,), (0,))), preferred_element_type=F32)


@jax.custom_vjp
def _bmm_nt(a, b):
    return _bdot(a, b, 2, 2)


def _bmm_nt_fwd(a, b):
    return _bdot(a, b, 2, 2), (a, b)


def _bmm_nt_bwd(res, g):
    a, b = res
    return _bdot(g, b, 2, 1), _bdot(g, a, 1, 1)


_bmm_nt.defvjp(_bmm_nt_fwd, _bmm_nt_bwd)


@jax.custom_vjp
def _bmm_nn(a, b):
    return _bdot(a, b, 2, 1)


def _bmm_nn_fwd(a, b):
    return _bdot(a, b, 2, 1), (a, b)


def _bmm_nn_bwd(res, g):
    a, b = res
    return _bdot(g, b, 2, 2), _bdot(a, g, 1, 1)


_bmm_nn.defvjp(_bmm_nn_fwd, _bmm_nn_bwd)


@jax.custom_vjp
def _bmm_tn(a, b):
    return _bdot(a, b, 1, 1)


def _bmm_tn_fwd(a, b):
    return _bdot(a, b, 1, 1), (a, b)


def _bmm_tn_bwd(res, g):
    a, b = res
    return _bdot(b, g, 2, 2), _bdot(a, g, 2, 1)


_bmm_tn.defvjp(_bmm_tn_fwd, _bmm_tn_bwd)


def _hdot(a, b, ca, cb):
    return lax.dot_general(a, b, (((ca,), (cb,)), ((0,), (0,))), precision=lax.Precision.HIGH, preferred_element_type=F32)


def _sigmoid(x):
    return 1.0 / (1.0 + jnp.exp(-x))


def _rms(x):
    return lax.rsqrt(jnp.mean(x * x, axis=-1, keepdims=True) + RMS_EPS)


def _ffn_fwd(x, norm, wg, wu, wd, dep, name):
    t, d = x.shape
    nc = wg.shape[0]
    tm = TOKEN_TILE

    def body(x_ref, n_ref, wg_ref, wu_ref, wd_ref, dep_ref, o_ref):
        xv = x_ref[...]
        h = (xv * _rms(xv) * n_ref[...]).astype(BF16)
        acc = jnp.zeros((tm, d), F32)
        for c in range(nc):
            g = jnp.dot(h, wg_ref[c], preferred_element_type=F32)
            u = jnp.dot(h, wu_ref[c], preferred_element_type=F32)
            a = (g * _sigmoid(g) * u).astype(BF16)
            acc = acc + jnp.dot(a, wd_ref[c], preferred_element_type=F32)
        o_ref[...] = xv + FFN_RESIDUAL * acc

    tile = pl.BlockSpec((tm, d), lambda i: (i, 0))
    return pl.pallas_call(
        body, name=name, grid=(t // tm,), out_shape=jax.ShapeDtypeStruct((t, d), F32),
        in_specs=[tile, pl.BlockSpec((1, d), lambda i: (0, 0)), VMEM_FULL, VMEM_FULL, VMEM_FULL, ANY],
        out_specs=tile, compiler_params=_params("arbitrary"),
    )(x, norm, wg, wu, wd, dep)


def _rmsnorm_bwd(xv, gain, dh):
    rs = _rms(xv)
    xn = xv * rs
    dxn = dh * gain
    dx = rs * (dxn - xn * jnp.mean(dxn * xn, axis=-1, keepdims=True))
    return dx, jnp.sum(dh * xn, axis=0, keepdims=True)


def _ffn_bwd(x, norm, wg, wu, wd, dy, dep, name):
    t, d = x.shape
    nc, _, fc = wg.shape
    tm = TOKEN_TILE
    nt = t // tm

    def body(x_ref, n_ref, wg_ref, wu_ref, wd_ref, dy_ref, dep_ref, dx_ref, dn_ref, dwg_ref, dwu_ref, dwd_ref, dh_ref):
        c, i = pl.program_id(0), pl.program_id(1)
        rows = pl.ds(pl.multiple_of(i * tm, tm), tm)
        xv = x_ref[...]
        gain = n_ref[...]
        h = (xv * _rms(xv) * gain).astype(BF16)
        dy = dy_ref[...]
        dyb = (FFN_RESIDUAL * dy).astype(BF16)
        g = jnp.dot(h, wg_ref[0], preferred_element_type=F32)
        u = jnp.dot(h, wu_ref[0], preferred_element_type=F32)
        sg = _sigmoid(g)
        s = g * sg
        a = (s * u).astype(BF16)
        da = _dot_nt(dyb, wd_ref[0])
        dub = (da * s).astype(BF16)
        dgb = (da * u * (sg * (1.0 + g * (1.0 - sg)))).astype(BF16)
        dwd_c = _dot_tn(a, dyb)
        dwg_c = _dot_tn(h, dgb)
        dwu_c = _dot_tn(h, dub)
        dh_c = _dot_nt(dgb, wg_ref[0]) + _dot_nt(dub, wu_ref[0])

        @pl.when(i == 0)
        def _():
            dwd_ref[0] = dwd_c
            dwg_ref[0] = dwg_c
            dwu_ref[0] = dwu_c

        @pl.when(i > 0)
        def _():
            dwd_ref[0] += dwd_c
            dwg_ref[0] += dwg_c
            dwu_ref[0] += dwu_c

        @pl.when(c == 0)
        def _():
            dh_ref[rows, :] = dh_c

        @pl.when(c > 0)
        def _():
            dh_ref[rows, :] += dh_c

        @pl.when(c == nc - 1)
        def _():
            dx, dn = _rmsnorm_bwd(xv, gain, dh_ref[rows, :])
            dx_ref[...] = dx + dy

            @pl.when(i == 0)
            def _():
                dn_ref[...] = dn

            @pl.when(i > 0)
            def _():
                dn_ref[...] += dn

    tile = pl.BlockSpec((tm, d), lambda c, i: (i, 0))
    row = pl.BlockSpec((1, d), lambda c, i: (0, 0))
    wcol = pl.BlockSpec((1, d, fc), lambda c, i: (c, 0, 0))
    wrow = pl.BlockSpec((1, fc, d), lambda c, i: (c, 0, 0))
    last = pl.BlockSpec((tm, d), lambda c, i: (jnp.where(c == nc - 1, i, 0), 0))
    return pl.pallas_call(
        body, name=name, grid=(nc, nt),
        out_shape=(jax.ShapeDtypeStruct((t, d), F32), jax.ShapeDtypeStruct((1, d), F32),
                   jax.ShapeDtypeStruct(wg.shape, F32), jax.ShapeDtypeStruct(wu.shape, F32),
                   jax.ShapeDtypeStruct(wd.shape, F32)),
        in_specs=[tile, row, wcol, wcol, wrow, tile, ANY],
        out_specs=(last, row, wcol, wcol, wrow),
        scratch_shapes=[pltpu.VMEM((t, d), F32)],
        compiler_params=_params("arbitrary", "arbitrary"),
    )(x, norm, wg, wu, wd, dy, dep)


def _store_heads(ref, v):
    for h in range(ref.shape[0]):
        ref[h] = v[:, h * HEAD_DIM:(h + 1) * HEAD_DIM]


def _load_heads(ref):
    return jnp.concatenate([ref[h] for h in range(ref.shape[0])], axis=-1)


N_HEAD_GROUPS = 3


def _proj_fwd(x, norm, w):
    t, d = x.shape
    ng, _, c = w.shape
    nh = c // HEAD_DIM
    tm = TOKEN_TILE

    def body(x_ref, n_ref, w_ref, q_ref, k_ref, v_ref, cur_ref):
        xv = x_ref[...]
        h = (xv * _rms(xv) * n_ref[...]).astype(BF16)
        for m, ref in enumerate((q_ref, k_ref, v_ref)):
            _store_heads(ref, jnp.dot(h, w_ref[m], preferred_element_type=F32))
        for m in range(N_HEAD_GROUPS, ng):
            j = m - N_HEAD_GROUPS
            cur_ref[:, j * c:(j + 1) * c] = jnp.dot(h, w_ref[m], preferred_element_type=F32)

    heads = pl.BlockSpec((nh, tm, HEAD_DIM), lambda i: (0, i, 0))
    hshape = jax.ShapeDtypeStruct((nh, t, HEAD_DIM), F32)
    wide = (ng - N_HEAD_GROUPS) * c
    return pl.pallas_call(
        body, name="proj_fwd", grid=(t // tm,),
        out_shape=(hshape, hshape, hshape, jax.ShapeDtypeStruct((t, wide), F32)),
        in_specs=[pl.BlockSpec((tm, d), lambda i: (i, 0)), pl.BlockSpec((1, d), lambda i: (0, 0)), VMEM_FULL],
        out_specs=(heads, heads, heads, pl.BlockSpec((tm, wide), lambda i: (i, 0))),
        compiler_params=_params("arbitrary"),
    )(x, norm, w)


def _proj_bwd(x, norm, w, dq, dk, dv, dcur, dres):
    t, d = x.shape
    ng, _, c = w.shape
    nh = c // HEAD_DIM
    tm = TOKEN_TILE

    def body(x_ref, n_ref, w_ref, dq_ref, dk_ref, dv_ref, dcur_ref, dres_ref, dx_ref, dn_ref, dw_ref):
        i = pl.program_id(0)

        @pl.when(i == 0)
        def _():
            dw_ref[...] = jnp.zeros_like(dw_ref)
            dn_ref[...] = jnp.zeros_like(dn_ref)

        xv = x_ref[...]
        gain = n_ref[...]
        h = (xv * _rms(xv) * gain).astype(BF16)
        dh = jnp.zeros((tm, d), F32)
        for m in range(ng):
            j = m - N_HEAD_GROUPS
            dp = _load_heads((dq_ref, dk_ref, dv_ref)[m]) if j < 0 else dcur_ref[:, j * c:(j + 1) * c]
            dp = dp.astype(BF16)
            dw_ref[m] += _dot_tn(h, dp)
            dh = dh + _dot_nt(dp, w_ref[m])
        dx, dn = _rmsnorm_bwd(xv, gain, dh)
        dx_ref[...] = dx + dres_ref[...]
        dn_ref[...] += dn

    tile = pl.BlockSpec((tm, d), lambda i: (i, 0))
    row = pl.BlockSpec((1, d), lambda i: (0, 0))
    heads = pl.BlockSpec((nh, tm, HEAD_DIM), lambda i: (0, i, 0))
    wide = (ng - N_HEAD_GROUPS) * c
    return pl.pallas_call(
        body, name="proj_bwd", grid=(t // tm,),
        out_shape=(jax.ShapeDtypeStruct((t, d), F32), jax.ShapeDtypeStruct((1, d), F32),
                   jax.ShapeDtypeStruct(w.shape, F32)),
        in_specs=[tile, row, VMEM_FULL, heads, heads, heads, pl.BlockSpec((tm, wide), lambda i: (i, 0)), tile],
        out_specs=(tile, row, VMEM_FULL),
        compiler_params=_params("arbitrary"),
    )(x, norm, w, dq, dk, dv, dcur, dres)


def _mixout_fwd(x, att, opg, gate, w):
    t, d = x.shape
    nh = att.shape[0]
    half = gate.shape[1]
    tm = TOKEN_TILE

    def body(x_ref, att_ref, opg_ref, g_ref, w_ref, o_ref):
        mix = jnp.concatenate([_load_heads(att_ref), _load_heads(opg_ref) * g_ref[...]], axis=-1).astype(BF16)
        o_ref[...] = x_ref[...] + jnp.dot(mix, w_ref[...], preferred_element_type=F32)

    tile = pl.BlockSpec((tm, d), lambda i: (i, 0))
    htile = pl.BlockSpec((tm, half), lambda i: (i, 0))
    heads = pl.BlockSpec((nh, tm, HEAD_DIM), lambda i: (0, i, 0))
    return pl.pallas_call(
        body, name="mixout_fwd", grid=(t // tm,), out_shape=jax.ShapeDtypeStruct((t, d), F32),
        in_specs=[tile, heads, heads, htile, VMEM_FULL], out_specs=tile, compiler_params=_params("arbitrary"),
    )(x, att, opg, gate, w)


def _mixout_bwd(att, opg, gate, w, dy, dep):
    nh, t, _ = att.shape
    half = gate.shape[1]
    d = dy.shape[1]
    tm = TOKEN_TILE

    def body(att_ref, opg_ref, g_ref, w_ref, dy_ref, dep_ref, datt_ref, dopg_ref, dg_ref, dw_ref):
        i = pl.program_id(0)
        opg_v, g_v = _load_heads(opg_ref), g_ref[...]
        mix = jnp.concatenate([_load_heads(att_ref), opg_v * g_v], axis=-1).astype(BF16)
        dyb = dy_ref[...].astype(BF16)
        dmix = _dot_nt(dyb, w_ref[...])
        dw = _dot_tn(mix, dyb)
        _store_heads(datt_ref, dmix[:, :half])
        drw = dmix[:, half:]
        _store_heads(dopg_ref, drw * g_v)
        dg_ref[...] = drw * opg_v

        @pl.when(i == 0)
        def _():
            dw_ref[...] = dw

        @pl.when(i > 0)
        def _():
            dw_ref[...] += dw

    tile = pl.BlockSpec((tm, d), lambda i: (i, 0))
    htile = pl.BlockSpec((tm, half), lambda i: (i, 0))
    heads = pl.BlockSpec((nh, tm, HEAD_DIM), lambda i: (0, i, 0))
    hshape = jax.ShapeDtypeStruct((nh, t, HEAD_DIM), F32)
    return pl.pallas_call(
        body, name="mixout_bwd", grid=(t // tm,),
        out_shape=(hshape, hshape, jax.ShapeDtypeStruct((t, half), F32), jax.ShapeDtypeStruct(w.shape, F32)),
        in_specs=[heads, heads, htile, VMEM_FULL, tile, ANY],
        out_specs=(heads, heads, htile, pl.BlockSpec(w.shape, lambda i: (0, 0))),
        compiler_params=_params("arbitrary"),
    )(att, opg, gate, w, dy, dep)


def _loss_head(y, target):
    t, d = y.shape
    tm = TOKEN_TILE

    def body(y_ref, t_ref, dy_ref, loss_ref):
        i = pl.program_id(0)
        err = y_ref[...] - t_ref[...]
        dy_ref[...] = err * (1.0 / d)
        part = 0.5 * jnp.sum(jnp.mean(err * err, axis=-1, keepdims=True), axis=0, keepdims=True)

        @pl.when(i == 0)
        def _():
            loss_ref[...] = jnp.zeros_like(loss_ref)

        loss_ref[...] += jnp.broadcast_to(part, loss_ref.shape)

    tile = pl.BlockSpec((tm, d), lambda i: (i, 0))
    return pl.pallas_call(
        body, name="loss_head", grid=(t // tm,),
        out_shape=(jax.ShapeDtypeStruct((t, d), F32), jax.ShapeDtypeStruct((1, 128), F32)),
        in_specs=[tile, tile], out_specs=(tile, pl.BlockSpec((1, 128), lambda i: (0, 0))),
        compiler_params=_params("arbitrary"),
    )(y, target)


def _att_block(q, kc, vc, qn, kn, kp=None, vp=None, has_prev=True):
    blk = q.shape[1]

    def hn(v, gain):
        return v * _rms(v) * gain

    qh = hn(q, qn)
    scale = HEAD_DIM ** -0.5
    qi = lax.broadcasted_iota(jnp.int32, (blk, blk), 0)
    kj = lax.broadcasted_iota(jnp.int32, (blk, blk), 1)
    sc = jnp.where(kj <= qi, _bmm_nt(qh, hn(kc, kn)) * scale, NEG_INF)
    top = jnp.max(sc, axis=-1, keepdims=True)
    if kp is not None:
        sp = jnp.where((kj >= qi) & has_prev, _bmm_nt(qh, hn(kp, kn)) * scale, NEG_INF)
        top = jnp.maximum(top, jnp.max(sp, axis=-1, keepdims=True))
    m = lax.stop_gradient(top)
    pc = jnp.exp(sc - m)
    den = jnp.sum(pc, axis=-1, keepdims=True)
    acc = _bmm_nn(pc, vc)
    if kp is not None:
        pp = jnp.exp(sp - m)
        den = den + jnp.sum(pp, axis=-1, keepdims=True)
        acc = acc + _bmm_nn(pp, vp)
    o = acc / den
    return o, jnp.broadcast_to(m + jnp.log(den), o.shape)


def _class_rows(n, dil):
    base = n * (ATT_BLOCK * dil)
    if dil == 1:
        return [pl.ds(pl.multiple_of(base, ATT_BLOCK), ATT_BLOCK)]
    return [pl.ds(base + r, ATT_BLOCK, stride=dil) for r in range(dil)]


def _take(ref, rows):
    return jnp.stack([ref[0, r, :] for r in rows])


def _put(ref, rows, val):
    for g, r in enumerate(rows):
        ref[0, r, :] = val[g]


def _put_add(ref, rows, val):
    for g, r in enumerate(rows):
        ref[0, r, :] += val[g]


def _merge_fn(o1, o2, o3, l1, l2, l3):
    m = lax.stop_gradient(jnp.maximum(jnp.maximum(l1, l2), l3))
    e1, e2, e3 = jnp.exp(l1 - m), jnp.exp(l2 - m), jnp.exp(l3 - m)
    return (e1 * o1 + e2 * o2 + e3 * o3) / (e1 + e2 + e3)


def _att_head_specs(t):
    head = pl.BlockSpec((1, t, HEAD_DIM), lambda h: (h, 0, 0))
    gain = pl.BlockSpec((1, 1, HEAD_DIM), lambda h: (0, 0, 0))
    return head, gain


def _for_each_block(t, block):
    carry = None
    for p, dil in enumerate(DILATIONS):
        nb = t // (ATT_BLOCK * dil)
        if dil == 1:
            first = block(p, dil, 0, None, True, carry)
            carry = lax.fori_loop(1, nb, lambda n, c, p=p, dil=dil: block(p, dil, n, n - 1, True, c), first)
        else:
            for n in range(nb):
                carry = block(p, dil, n, n - 1 if n else None, True, carry)
    return carry


def _att_fwd(q, k, v, qn, kn):
    nh, t, dh = q.shape
    head, gain = _att_head_specs(t)

    def body(q_ref, k_ref, v_ref, qn_ref, kn_ref, att_ref, *saved):
        o_refs, l_refs = saved[:3], saved[3:]
        gq, gk = qn_ref[...], kn_ref[...]

        def block(p, dil, n, prev_n, has_prev, carry):
            rows = _class_rows(n, dil)
            args = [_take(q_ref, rows), _take(k_ref, rows), _take(v_ref, rows), gq, gk]
            if prev_n is not None:
                before = _class_rows(prev_n, dil)
                args += [_take(k_ref, before), _take(v_ref, before), has_prev]
            o, lse = _att_block(*args)
            _put(o_refs[p], rows, o)
            _put(l_refs[p], rows, lse)
            return 0

        _for_each_block(t, block)

        def merge(j, carry):
            rows = pl.ds(pl.multiple_of(j * ATT_BLOCK, ATT_BLOCK), ATT_BLOCK)
            att_ref[0, rows, :] = _merge_fn(*[r[0, rows, :] for r in saved])
            return carry

        lax.fori_loop(0, t // ATT_BLOCK, merge, 0)

    return pl.pallas_call(
        body, name="att_fwd", grid=(nh,), out_shape=(jax.ShapeDtypeStruct(q.shape, F32),) * 7,
        in_specs=[head, head, head, gain, gain], out_specs=(head,) * 7, compiler_params=_params("arbitrary"),
    )(q, k, v, qn, kn)


def _att_bwd(q, k, v, qn, kn, saved, datt):
    nh, t, dh = q.shape
    head, gain = _att_head_specs(t)

    def body(q_ref, k_ref, v_ref, qn_ref, kn_ref, o1, o2, o3, l1, l2, l3, datt_ref,
             dq_ref, dk_ref, dv_ref, dqn_ref, dkn_ref):
        for ref in (dq_ref, dk_ref, dv_ref):
            ref[...] = jnp.zeros_like(ref)

        @pl.when(pl.program_id(0) == 0)
        def _():
            dqn_ref[...] = jnp.zeros_like(dqn_ref)
            dkn_ref[...] = jnp.zeros_like(dkn_ref)

        gq, gk = qn_ref[...], kn_ref[...]

        def block(p, dil, n, prev_n, has_prev, carry):
            rows = _class_rows(n, dil)
            _, merge_vjp = jax.vjp(_merge_fn, *[_take(r, rows) for r in (o1, o2, o3, l1, l2, l3)])
            cts = merge_vjp(_take(datt_ref, rows))
            args = [_take(q_ref, rows), _take(k_ref, rows), _take(v_ref, rows), gq, gk]
            if prev_n is not None:
                before = _class_rows(prev_n, dil)
                args += [_take(k_ref, before), _take(v_ref, before)]
            _, block_vjp = jax.vjp(functools.partial(_att_block, has_prev=has_prev), *args)
            grads = block_vjp((cts[p], cts[3 + p]))
            _put_add(dq_ref, rows, grads[0])
            _put_add(dk_ref, rows, grads[1])
            _put_add(dv_ref, rows, grads[2])
            if prev_n is not None:
                _put_add(dk_ref, before, grads[5])
                _put_add(dv_ref, before, grads[6])
            if carry is None:
                return grads[3], grads[4]
            return carry[0] + grads[3], carry[1] + grads[4]

        dgq, dgk = _for_each_block(t, block)
        dqn_ref[...] += dgq
        dkn_ref[...] += dgk

    hshape = jax.ShapeDtypeStruct(q.shape, F32)
    gshape = jax.ShapeDtypeStruct((1, 1, dh), F32)
    return pl.pallas_call(
        body, name="att_bwd", grid=(nh,), out_shape=(hshape, hshape, hshape, gshape, gshape),
        in_specs=[head, head, head, gain, gain] + [head] * 7, out_specs=(head, head, head, gain, gain),
        compiler_params=_params("arbitrary"),
    )(q, k, v, qn, kn, *saved, datt)


RWKV_VEC = ("mu_r", "mu_k", "mu_v", "mu_w", "mu_a", "mu_g", "w0", "a0", "k_k", "k_a")
RWKV_MAT = ("w1", "w2", "a1", "a2", "g1", "g2")


def _rwkv_pre_fn(cur, prev, vec, w1, w2, a1, a2, g1, g2):
    c = cur.shape[1] // 4
    mu_r, mu_k, mu_v, mu_w, mu_a, mu_g, w0, a0, k_k, k_a = (vec[j:j + 1] for j in range(10))

    def lerp(j, mu):
        xc, xp = cur[:, j * c:(j + 1) * c], prev[:, j * c:(j + 1) * c]
        return xc + (xp - xc) * mu

    r, k, v = lerp(0, mu_r), lerp(1, mu_k), lerp(2, mu_v)
    cw, ca, cg = lerp(3, mu_w), lerp(3, mu_a), lerp(3, mu_g)
    z = w0 + _mm(jnp.tanh(_mm(cw, w1)), w2)
    w_log = jnp.minimum(z, 0.0) - jnp.log(1.0 + jnp.exp(-jnp.abs(z))) - 0.5
    lw = -jnp.exp(w_log)
    a = _sigmoid(a0 + _mm(_mm(ca, a1), a2))
    gate = _mm(_sigmoid(_mm(cg, g1)), g2)
    kkraw = k * k_k
    kmod = k * (1.0 + (a - 1.0) * k_a)
    return r, lw, kmod, v, kkraw, a, gate


HALO_ROWS = 8


def _rwkv_pre_specs(c, mats, tile_of):
    tm = TOKEN_TILE
    nh = c // HEAD_DIM
    wide = pl.BlockSpec((tm, 4 * c), lambda j: (tile_of(j), 0))
    halo = pl.BlockSpec((HALO_ROWS, 4 * c), lambda j: (jnp.maximum(tile_of(j) * (tm // HALO_ROWS) - 1, 0), 0))
    one = pl.BlockSpec((tm, c), lambda j: (tile_of(j), 0))
    heads = pl.BlockSpec((nh, tm, HEAD_DIM), lambda j: (0, tile_of(j), 0))
    vec = pl.BlockSpec((10, c), lambda j: (0, 0))
    mspecs = [pl.BlockSpec(m.shape, lambda j: (0, 0)) for m in mats]
    return wide, halo, one, heads, vec, mspecs


def _previous_rows(cur, halo, tile):
    first = jnp.where(tile > 0, halo[HALO_ROWS - 1:HALO_ROWS], 0.0)
    rows = lax.broadcasted_iota(jnp.int32, cur.shape, 0)
    return jnp.where(rows == 0, first, pltpu.roll(cur, 1, axis=0))


def _rwkv_pre_fwd(cur, vec, mats):
    t, c4 = cur.shape
    c = c4 // 4
    wide, halo, one, heads, vspec, mspecs = _rwkv_pre_specs(c, mats, lambda j: j)

    def body(cur_ref, halo_ref, vec_ref, *rest):
        mrefs, outs = rest[:6], rest[6:]
        cur_v = cur_ref[...]
        prev = _previous_rows(cur_v, halo_ref[...], pl.program_id(0))
        vals = _rwkv_pre_fn(cur_v, prev, vec_ref[...], *(m[...] for m in mrefs))
        for ref, val in zip(outs[:6], vals[:6]):
            _store_heads(ref, val)
        outs[6][...] = vals[6]

    hshape = jax.ShapeDtypeStruct((c // HEAD_DIM, t, HEAD_DIM), F32)
    return pl.pallas_call(
        body, name="rwkv_pre_fwd", grid=(t // TOKEN_TILE,), out_shape=(hshape,) * 6 + (jax.ShapeDtypeStruct((t, c), F32),),
        in_specs=[wide, halo, vspec] + mspecs, out_specs=(heads,) * 6 + (one,), compiler_params=_params("arbitrary"),
    )(cur, cur, vec, *mats)


def _rwkv_pre_bwd(cur, vec, mats, cts, dgate):
    t, c4 = cur.shape
    c = c4 // 4
    tm = TOKEN_TILE
    nt = t // tm
    wide, halo, one, heads, vspec, mspecs = _rwkv_pre_specs(c, mats, lambda j: nt - 1 - j)

    def body(cur_ref, halo_ref, vec_ref, *rest):
        mrefs, ctrefs, dgate_ref, outs, carry_ref = rest[:6], rest[6:12], rest[12], rest[13:-1], rest[-1]
        j = pl.program_id(0)

        @pl.when(j == 0)
        def _():
            carry_ref[...] = jnp.zeros_like(carry_ref)
            for ref in outs[1:]:
                ref[...] = jnp.zeros_like(ref)

        cur_v = cur_ref[...]
        prev = _previous_rows(cur_v, halo_ref[...], nt - 1 - j)
        _, vjp = jax.vjp(_rwkv_pre_fn, cur_v, prev, vec_ref[...], *(m[...] for m in mrefs))
        grads = vjp(tuple(_load_heads(r) for r in ctrefs) + (dgate_ref[...],))
        dprev = grads[1]
        rows = lax.broadcasted_iota(jnp.int32, dprev.shape, 0)
        outs[0][...] = grads[0] + jnp.where(rows == tm - 1, carry_ref[0:1], pltpu.roll(dprev, tm - 1, axis=0))
        carry_ref[0:1] = dprev[0:1]
        for ref, val in zip(outs[1:], grads[2:]):
            ref[...] += val

    return pl.pallas_call(
        body, name="rwkv_pre_bwd", grid=(nt,),
        out_shape=(jax.ShapeDtypeStruct(cur.shape, F32), jax.ShapeDtypeStruct(vec.shape, F32))
        + tuple(jax.ShapeDtypeStruct(m.shape, F32) for m in mats),
        in_specs=[wide, halo, vspec] + mspecs + [heads] * 6 + [one], out_specs=(wide, vspec) + tuple(mspecs),
        scratch_shapes=[pltpu.VMEM((HALO_ROWS, c4), F32)], compiler_params=_params("arbitrary"),
    )(cur, cur, vec, *mats, *cts, dgate)


def _scan_chunk_fn(h0, r, lw, k, v, kkraw, a, rk, lnw, lnb):
    n = r.shape[1]
    nrm = jnp.sqrt(jnp.sum(kkraw * kkraw, axis=-1, keepdims=True))
    kk = kkraw / jnp.maximum(nrm, 1e-12)
    av, bv = -kk, kk * a
    ti = lax.broadcasted_iota(jnp.int32, (n, n), 0)
    si = lax.broadcasted_iota(jnp.int32, (n, n), 1)
    incl, strict = ti >= si, ti > si
    ones = jnp.broadcast_to(incl.astype(F32)[None], (r.shape[0], n, n))
    cum = _hdot(ones, lw, 2, 1)
    at, rt = av * jnp.exp(cum - lw), r * jnp.exp(cum)
    inv = jnp.exp(-cum)
    bt, kt = bv * inv, k * inv
    lab = jnp.where(strict, _hdot(at, bt, 2, 2), 0.0)
    lak = jnp.where(strict, _hdot(at, kt, 2, 2), 0.0)
    rb = jnp.where(incl, _hdot(rt, bt, 2, 2), 0.0)
    rkm = jnp.where(incl, _hdot(rt, kt, 2, 2), 0.0)
    u = _bmm_nn(at, h0) + _bmm_nn(lak, v)
    p = lab
    m = 1
    while m < n:
        u = u + _bmm_nn(p, u)
        m *= 2
        if m < n:
            p = _bmm_nn(p, p)
    y = _bmm_nn(rt, h0) + _bmm_nn(rb, u) + _bmm_nn(rkm, v)
    last = jnp.exp(jnp.sum(lw, axis=1, keepdims=True))
    h1 = jnp.swapaxes(last, 1, 2) * (h0 + _bmm_tn(bt, u) + _bmm_tn(kt, v))
    mean = jnp.mean(y, axis=-1, keepdims=True)
    yc = y - mean
    var = jnp.mean(yc * yc, axis=-1, keepdims=True)
    yn = yc * lax.rsqrt(var + GN_EPS) * lnw + lnb
    bonus = jnp.sum(r * k * rk, axis=-1, keepdims=True) * v
    return yn + bonus, h1


def _scan_specs(h, t, dh, rev):
    n = SCAN_CHUNK
    nc = t // n
    pos = (lambda c: (0, nc - 1 - c, 0)) if rev else (lambda c: (0, c, 0))
    st = (lambda c: (nc - 1 - c, 0, 0, 0)) if rev else (lambda c: (c, 0, 0, 0))
    seq = pl.BlockSpec((h, n, dh), pos)
    par = pl.BlockSpec((h, 1, dh), lambda c: (0, 0, 0))
    state = pl.BlockSpec((1, h, dh, dh), st)
    return seq, par, state


def _scan_fwd(seqs, pars):
    h, t, dh = seqs[0].shape
    nc = t // SCAN_CHUNK
    seq, par, state = _scan_specs(h, t, dh, False)

    def body(r, lw, k, v, kkraw, a, rk, lnw, lnb, o_ref, st_ref, h_ref):
        @pl.when(pl.program_id(0) == 0)
        def _():
            h_ref[...] = jnp.zeros_like(h_ref)

        h0 = h_ref[...]
        st_ref[0] = h0
        o, h1 = _scan_chunk_fn(h0, r[...], lw[...], k[...], v[...], kkraw[...], a[...], rk[...], lnw[...], lnb[...])
        o_ref[...] = o
        h_ref[...] = h1

    return pl.pallas_call(
        body, name="rwkv_scan_fwd", grid=(nc,),
        out_shape=(jax.ShapeDtypeStruct((h, t, dh), F32), jax.ShapeDtypeStruct((nc, h, dh, dh), F32)),
        in_specs=[seq] * 6 + [par] * 3, out_specs=(seq, state),
        scratch_shapes=[pltpu.VMEM((h, dh, dh), F32)], compiler_params=_params("arbitrary"),
    )(*seqs, *pars)


def _scan_bwd(seqs, pars, states, do):
    h, t, dh = seqs[0].shape
    nc = t // SCAN_CHUNK
    seq, par, state = _scan_specs(h, t, dh, True)

    def body(r, lw, k, v, kkraw, a, rk, lnw, lnb, st_ref, do_ref, *rest):
        douts, dpars, dh_ref = rest[:6], rest[6:9], rest[9]
        first = pl.program_id(0) == 0

        @pl.when(first)
        def _():
            dh_ref[...] = jnp.zeros_like(dh_ref)

        _, vjp = jax.vjp(_scan_chunk_fn, st_ref[0], r[...], lw[...], k[...], v[...], kkraw[...], a[...],
                         rk[...], lnw[...], lnb[...])
        grads = vjp((do_ref[...], dh_ref[...]))
        dh_ref[...] = grads[0]
        for ref, val in zip(douts, grads[1:7]):
            ref[...] = val

        @pl.when(first)
        def _():
            for ref, val in zip(dpars, grads[7:]):
                ref[...] = val

        @pl.when(jnp.logical_not(first))
        def _():
            for ref, val in zip(dpars, grads[7:]):
                ref[...] += val

    sshape = jax.ShapeDtypeStruct((h, t, dh), F32)
    pshape = jax.ShapeDtypeStruct((h, 1, dh), F32)
    return pl.pallas_call(
        body, name="rwkv_scan_bwd", grid=(nc,), out_shape=(sshape,) * 6 + (pshape,) * 3,
        in_specs=[seq] * 6 + [par] * 3 + [state, seq], out_specs=(seq,) * 6 + (par,) * 3,
        scratch_shapes=[pltpu.VMEM((h, dh, dh), F32)], compiler_params=_params("arbitrary"),
    )(*seqs, *pars, states, do)


def _local_step(x, target, w, ex):
    w = dict(w)
    c = w["mu_r"].shape[-1]
    qn, kn = w["q_norm"].reshape(1, 1, HEAD_DIM), w["k_norm"].reshape(1, 1, HEAD_DIM)
    vec = jnp.concatenate([w[n].reshape(1, c) for n in RWKV_VEC], axis=0)
    pars = [w[n].reshape(-1, 1, HEAD_DIM) for n in ("r_k", "ln_x_w", "ln_x_b")]
    no_dep = jnp.zeros(DEP_SHAPE, F32)

    x1 = _ffn_fwd(x, w["ffn1_norm"], w["ffn1_w_gate"], w["ffn1_w_up"], w["ffn1_w_down"], ex.first_dep, "ffn1_fwd")
    w.update(ex.mix_weights((x1,)))
    mats = [w[n] for n in RWKV_MAT]
    q, k, v, cur = _proj_fwd(x1, w["mix_norm"], w["w_in"])
    att, *saved = _att_fwd(q, k, v, qn, kn)
    pre = _rwkv_pre_fwd(cur, vec, mats)
    seqs, gate = pre[:6], pre[6]
    opg, states = _scan_fwd(seqs, pars)
    w.update(ex.out_weights((att, opg)))
    x2 = _mixout_fwd(x1, att, opg, gate, w["w_out"])
    x3 = _ffn_fwd(x2, w["ffn2_norm"], w["ffn2_w_gate"], w["ffn2_w_up"], w["ffn2_w_down"], no_dep, "ffn2_fwd")
    dy, loss = _loss_head(x3, target)

    g = {}
    dx2, g["ffn2_norm"], g["ffn2_w_gate"], g["ffn2_w_up"], g["ffn2_w_down"] = _ffn_bwd(
        x2, w["ffn2_norm"], w["ffn2_w_gate"], w["ffn2_w_up"], w["ffn2_w_down"], dy, no_dep, "ffn2_bwd")
    dep = ex.send_ffn2({n: g[n] for n in ("ffn2_w_gate", "ffn2_w_up", "ffn2_w_down")})
    datt, dopg, dgate, g["w_out"] = _mixout_bwd(att, opg, gate, w["w_out"], dx2, dep)
    dscan = _scan_bwd(seqs, pars, states, dopg)
    for n, d in zip(("r_k", "ln_x_w", "ln_x_b"), dscan[6:]):
        g[n] = d
    dcur, dvec, *dmats = _rwkv_pre_bwd(cur, vec, mats, dscan[:6], dgate)
    for n, d in zip(RWKV_MAT, dmats):
        g[n] = d
    for j, n in enumerate(RWKV_VEC):
        g[n] = dvec[j:j + 1]
    dq, dk, dv, g["q_norm"], g["k_norm"] = _att_bwd(q, k, v, qn, kn, saved, datt)
    dx1, g["mix_norm"], g["w_in"] = _proj_bwd(x1, w["mix_norm"], w["w_in"], dq, dk, dv, dcur, dx2)
    dep = ex.send_mix({n: g[n] for n in ("w_in", "w_out") + RWKV_MAT}, (dx1,))
    dx, g["ffn1_norm"], g["ffn1_w_gate"], g["ffn1_w_up"], g["ffn1_w_down"] = _ffn_bwd(
        x, w["ffn1_norm"], w["ffn1_w_gate"], w["ffn1_w_up"], w["ffn1_w_down"], dx1, dep, "ffn1_bwd")
    return loss, dx, g


N_SHARDS = 4


def _place():
    return lax.axis_index("x"), lax.axis_index("y"), lax.axis_index("c")


def _chip_peers(x, y):
    return [(1 - x, y), (x, 1 - y), (1 - x, 1 - y)]


def _gather_xy(shards):
    n = len(shards)

    def body(*refs):
        ins, outs = refs[:n], refs[n:2 * n]
        send_sems, recv_sems, local_sems = refs[2 * n:]
        x, y, c = _place()
        me = 2 * x + y
        copies = []
        for i in range(n):
            own = pltpu.make_async_copy(ins[i], outs[i].at[me], local_sems.at[i])
            own.start()
            copies.append(own)
            for k, (px, py) in enumerate(_chip_peers(x, y)):
                cp = pltpu.make_async_remote_copy(
                    src_ref=ins[i], dst_ref=outs[i].at[me], send_sem=send_sems.at[i, k], recv_sem=recv_sems.at[i, k],
                    device_id=(px, py, c), device_id_type=MESH)
                cp.start()
                copies.append(cp)
        for cp in copies:
            cp.wait()

    return pl.pallas_call(
        body, name="gather_weights",
        out_shape=tuple(jax.ShapeDtypeStruct((N_SHARDS,) + s.shape, s.dtype) for s in shards),
        in_specs=[ANY] * n, out_specs=(ANY,) * n,
        scratch_shapes=[pltpu.SemaphoreType.DMA((n, 3)), pltpu.SemaphoreType.DMA((n, 3)), pltpu.SemaphoreType.DMA((n,))],
    )(*shards)


def _scatter_partials(parts):
    n = len(parts)

    def body(*refs):
        ins, outs = refs[:n], refs[n:2 * n]
        send_sems, recv_sems = refs[2 * n:]
        x, y, c = _place()
        copies = []
        for i in range(n):
            for k, (px, py) in enumerate(_chip_peers(x, y)):
                cp = pltpu.make_async_remote_copy(
                    src_ref=ins[i].at[2 * px + py], dst_ref=outs[i].at[k], send_sem=send_sems.at[i, k],
                    recv_sem=recv_sems.at[i, k], device_id=(px, py, c), device_id_type=MESH)
                cp.start()
                copies.append(cp)
        for cp in copies:
            cp.wait()

    return pl.pallas_call(
        body, name="scatter_partials",
        out_shape=tuple(jax.ShapeDtypeStruct((3,) + p.shape[1:], p.dtype) for p in parts),
        in_specs=[ANY] * n, out_specs=(ANY,) * n,
        scratch_shapes=[pltpu.SemaphoreType.DMA((n, 3)), pltpu.SemaphoreType.DMA((n, 3))],
    )(*parts)


HBM = pl.BlockSpec(memory_space=pltpu.HBM)
SEM = pl.BlockSpec(memory_space=pltpu.SEMAPHORE)
DEP_SHAPE = (8, 128)


def _gather_views(i, srcs, lands, k, px, py, me):
    return (srcs[i], lands[i].at[me]), (srcs[i], lands[i].at[2 * px + py])


def _scatter_views(i, srcs, lands, k, px, py, me):
    return (srcs[i].at[2 * px + py], lands[i].at[k]), (srcs[i].at[me], lands[i].at[k])


def _push_start(srcs, land_shapes, views, own_slot, after, name):
    n = len(srcs)

    def body(*refs):
        src_refs, land_refs = refs[:n], refs[n:2 * n]
        send_sems, recv_sems = refs[2 * n + 1:2 * n + 3]
        token = refs[4 * n + 3]
        x, y, c = _place()
        me = 2 * x + y
        for i in range(n):
            for k, (px, py) in enumerate(_chip_peers(x, y)):
                (src, dst), _ = views(i, src_refs, land_refs, k, px, py, me)
                pltpu.make_async_remote_copy(
                    src_ref=src, dst_ref=dst, send_sem=send_sems.at[3 * i + k], recv_sem=recv_sems.at[3 * i + k],
                    device_id=(px, py, c), device_id_type=MESH).start()
        token[...] = jnp.zeros_like(token)

    sems = pltpu.SemaphoreType.DMA((3 * n,))
    lands = [lax.empty(s.shape, s.dtype) for s in land_shapes]
    if own_slot:
        me = 2 * lax.axis_index("x") + lax.axis_index("y")
        lands = [lax.dynamic_update_index_in_dim(z, s, me, 0) for z, s in zip(lands, srcs)]
    lands = [pltpu.with_memory_space_constraint(z, pltpu.HBM) for z in lands]
    srcs = [pltpu.with_memory_space_constraint(s, pltpu.HBM) for s in srcs]
    outs = pl.pallas_call(
        body, name=name,
        out_shape=(sems, sems, *[pltpu.HBM(s.shape, s.dtype) for s in srcs], *[pltpu.HBM(s.shape, s.dtype) for s in land_shapes],
                   jax.ShapeDtypeStruct(DEP_SHAPE, F32)),
        in_specs=[HBM] * (2 * n) + [ANY], out_specs=(SEM, SEM, *[HBM] * (2 * n), VMEM_FULL),
        input_output_aliases={i: 2 + i for i in range(2 * n)},
        compiler_params=pltpu.CompilerParams(has_side_effects=pltpu.SideEffectType.DATAFLOW_SIDE_EFFECTING),
    )(*srcs, *lands, after)
    return outs[0], outs[1], outs[2:2 + n], outs[2 + n:2 + 2 * n], outs[2 + 2 * n]


def _push_wait(started, views, after, name):
    send_sems, recv_sems, srcs, lands, _ = started
    n = len(srcs)

    def body(*refs):
        src_refs, land_refs = refs[:n], refs[n:2 * n]
        send_sems, recv_sems = refs[2 * n:2 * n + 2]
        x, y, c = _place()
        me = 2 * x + y
        for i in range(n):
            for k, (px, py) in enumerate(_chip_peers(x, y)):
                _, (src, dst) = views(i, src_refs, land_refs, k, px, py, me)
                landing = pltpu.make_async_remote_copy(
                    src_ref=src, dst_ref=dst, send_sem=send_sems.at[3 * i + k], recv_sem=recv_sems.at[3 * i + k],
                    device_id=(px, py, c), device_id_type=MESH)
                landing.wait_send()
                landing.wait_recv()

    outs = pl.pallas_call(
        body, name=name,
        out_shape=tuple(pltpu.HBM(s.shape, s.dtype) for s in (*srcs, *lands)),
        in_specs=[HBM] * (2 * n) + [SEM, SEM] + [ANY] * len(after), out_specs=(HBM,) * (2 * n),
        input_output_aliases={i: i for i in range(2 * n)},
        compiler_params=pltpu.CompilerParams(has_side_effects=pltpu.SideEffectType.DATAFLOW_SIDE_EFFECTING),
    )(*srcs, *lands, send_sems, recv_sems, *after)
    return outs[n:]


def _sibling_swap(arrays):
    n = len(arrays)

    def body(*refs):
        ins, outs = refs[:n], refs[n:2 * n]
        send_sems, recv_sems = refs[2 * n:]
        x, y, c = _place()
        copies = []
        for i in range(n):
            cp = pltpu.make_async_remote_copy(
                src_ref=ins[i], dst_ref=outs[i], send_sem=send_sems.at[i], recv_sem=recv_sems.at[i],
                device_id=(x, y, 1 - c), device_id_type=MESH)
            cp.start()
            copies.append(cp)
        for cp in copies:
            cp.wait()

    return pl.pallas_call(
        body, name="sibling_swap",
        out_shape=tuple(jax.ShapeDtypeStruct(a.shape, a.dtype) for a in arrays),
        in_specs=[ANY] * n, out_specs=(ANY,) * n,
        scratch_shapes=[pltpu.SemaphoreType.DMA((n,)), pltpu.SemaphoreType.DMA((n,))],
    )(*arrays)


N_DEV = 8


def _allreduce_small(pack):
    def body(in_ref, out_ref, buf, send_sems, recv_sems):
        x, y, c = _place()
        me = 4 * x + 2 * y + c
        buf[me] = in_ref[...]

        def copy(j, slot):
            px, py, pc = x ^ (j >> 2), y ^ ((j >> 1) & 1), c ^ (j & 1)
            return pltpu.make_async_remote_copy(
                src_ref=in_ref, dst_ref=buf.at[slot(px, py, pc)], send_sem=send_sems.at[j], recv_sem=recv_sems.at[j],
                device_id=(px, py, pc), device_id_type=MESH)

        for j in range(1, N_DEV):
            copy(j, lambda px, py, pc: me).start()
        for j in range(1, N_DEV):
            landing = copy(j, lambda px, py, pc: 4 * px + 2 * py + pc)
            landing.wait_send()
            landing.wait_recv()
        acc = buf[0]
        for s in range(1, N_DEV):
            acc = acc + buf[s]
        out_ref[...] = acc

    return pl.pallas_call(
        body, name="allreduce_small", out_shape=jax.ShapeDtypeStruct(pack.shape, F32),
        in_specs=[VMEM_FULL], out_specs=VMEM_FULL,
        scratch_shapes=[pltpu.VMEM((N_DEV,) + pack.shape, F32), pltpu.SemaphoreType.DMA((N_DEV,)),
                        pltpu.SemaphoreType.DMA((N_DEV,))],
    )(pack)


ROW_TILE_MAX = 256
BF16_SUBLANES = 16


def _row_tile(rows):
    for tr in range(min(rows, ROW_TILE_MAX), 0, -1):
        if rows % tr == 0 and tr % BF16_SUBLANES == 0:
            return tr
    return rows


def _reduce_own(me, part, recv, name):
    _, r, cols = part.shape
    tr = _row_tile(r)

    def body(me_ref, p_ref, rv_ref, o_ref):
        acc = p_ref[0]
        for k in range(3):
            acc = acc + rv_ref[k].astype(F32)
        o_ref[...] = acc

    return pl.pallas_call(
        body, name=name, out_shape=jax.ShapeDtypeStruct((r, cols), F32),
        grid_spec=pltpu.PrefetchScalarGridSpec(
            num_scalar_prefetch=1, grid=(r // tr,),
            in_specs=[pl.BlockSpec((1, tr, cols), lambda i, me_ref: (me_ref[0], i, 0)),
                      pl.BlockSpec((3, tr, cols), lambda i, me_ref: (0, i, 0))],
            out_specs=pl.BlockSpec((tr, cols), lambda i, me_ref: (i, 0))),
        compiler_params=_params("arbitrary"),
    )(me, part, recv)


def _adamw(w, ga, gb, m, v, name):
    r, cols = w.shape
    tr = _row_tile(r)
    c1 = 1.0 - ADAM_B1 ** ADAM_STEP
    c2 = 1.0 - ADAM_B2 ** ADAM_STEP

    def body(w_ref, ga_ref, gb_ref, m_ref, v_ref, g_out, d_out, m_out, v_out):
        g = ga_ref[...] + gb_ref[...]
        mn = ADAM_B1 * m_ref[...] + (1.0 - ADAM_B1) * g
        vn = ADAM_B2 * v_ref[...] + (1.0 - ADAM_B2) * (g * g)
        g_out[...] = g
        m_out[...] = mn
        v_out[...] = vn
        d_out[...] = -ADAM_LR * ((mn / c1) / (jnp.sqrt(vn / c2) + ADAM_EPS) + ADAM_WD * w_ref[...])

    tile = pl.BlockSpec((tr, cols), lambda i: (i, 0))
    shape = jax.ShapeDtypeStruct((r, cols), F32)
    return pl.pallas_call(
        body, name=name, grid=(r // tr,), out_shape=(shape,) * 4, in_specs=[tile] * 5, out_specs=(tile,) * 4,
        compiler_params=_params("arbitrary"),
    )(w, ga, gb, m, v)


PACK_COLS = 512


def _to_rows(a):
    flat = a.reshape(-1)
    pad = (-flat.shape[0]) % PACK_COLS
    return jnp.pad(flat, (0, pad)).reshape(-1, PACK_COLS)


def _pack(arrays, extra_rows=0):
    rows = [_to_rows(a) for a in arrays]
    n = sum(r.shape[0] for r in rows) + extra_rows
    pad = (-n) % 8
    return jnp.concatenate(rows + [jnp.zeros((extra_rows + pad, PACK_COLS), F32)], axis=0)


def _unpack(pack, like):
    out, at = [], 0
    for a in like:
        n = -(-a.size // PACK_COLS)
        out.append(pack[at:at + n].reshape(-1)[:a.size].reshape(a.shape))
        at += n
    return out


COL_SHARDED = ("ffn1_w_gate", "ffn1_w_up", "w_in", "ffn2_w_gate", "ffn2_w_up", "w2", "a2", "g2")
ROW_SHARDED = ("ffn1_w_down", "ffn2_w_down", "w_out", "w1", "a1", "g1")
CHUNKED = ("ffn1_w_gate", "ffn1_w_up", "ffn1_w_down", "ffn2_w_gate", "ffn2_w_up", "ffn2_w_down")
WEIGHTS = ("ffn1_norm", "ffn1_w_gate", "ffn1_w_up", "ffn1_w_down", "mix_norm", "w_in", "q_norm", "k_norm",
           "mu_r", "mu_k", "mu_v", "mu_w", "mu_a", "mu_g", "w0", "w1", "w2", "a0", "a1", "a2", "g1", "g2",
           "k_k", "k_a", "r_k", "ln_x_w", "ln_x_b", "w_out", "ffn2_norm", "ffn2_w_gate", "ffn2_w_up", "ffn2_w_down")


W_IN_GROUPS = 7


def _full_from_blocks(name, blocks):
    if name in CHUNKED:
        return blocks
    if name in ROW_SHARDED:
        return blocks.reshape(-1, blocks.shape[-1])
    full = blocks.transpose(1, 0, 2).reshape(blocks.shape[1], -1)
    if name == "w_in":
        return full.reshape(full.shape[0], W_IN_GROUPS, -1).transpose(1, 0, 2)
    return full


def _blocks_from_full(name, full):
    if name in CHUNKED:
        return full
    if name in ROW_SHARDED:
        return full.reshape(N_SHARDS, -1, full.shape[-1])
    if name == "w_in":
        full = full.transpose(1, 0, 2).reshape(full.shape[1], -1)
    return full.reshape(full.shape[0], N_SHARDS, -1).transpose(1, 0, 2)


FFN1_GROUP = ("ffn1_w_gate", "ffn1_w_up", "ffn1_w_down")
MIX_GROUP = ("w_in",) + RWKV_MAT
OUT_GROUP = ("w_out", "ffn2_w_gate", "ffn2_w_up", "ffn2_w_down")
FFN2_GROUP = OUT_GROUP[1:]
LATE_GROUP = ("w_in", "w_out") + RWKV_MAT


class _Exchange:
    def __init__(self, given):
        self.given = given
        first = _gather_xy(self._shards(FFN1_GROUP))
        self.first_weights = self._full(FFN1_GROUP, first)
        self.mix = self._gather_start(MIX_GROUP, first[0], "gather_mix_start")
        self.out = self._gather_start(OUT_GROUP, self.mix[4], "gather_out_start")
        self.first_dep = self.out[4]
        self.parts, self.recv = {}, {}

    def _shards(self, names):
        return [self.given[n][0].astype(BF16) for n in names]

    @staticmethod
    def _full(names, blocks):
        out = {}
        for n, b in zip(names, blocks):
            full = _full_from_blocks(n, b)
            out[n] = full.astype(F32) if n in RWKV_MAT else full
        return out

    def _gather_start(self, names, after, name):
        shards = self._shards(names)
        lands = [jax.ShapeDtypeStruct((N_SHARDS,) + s.shape, s.dtype) for s in shards]
        return _push_start(shards, lands, _gather_views, True, after, name)

    def mix_weights(self, after):
        return self._full(MIX_GROUP, _push_wait(self.mix, _gather_views, after, "gather_mix_wait"))

    def out_weights(self, after):
        return self._full(OUT_GROUP, _push_wait(self.out, _gather_views, after, "gather_out_wait"))

    def _scatter_start(self, grads, name):
        names = tuple(grads)
        parts = [_blocks_from_full(n, grads[n]) for n in names]
        self.parts.update(zip(names, parts))
        lands = [jax.ShapeDtypeStruct((3,) + p.shape[1:], BF16) for p in parts]
        return _push_start([p.astype(BF16) for p in parts], lands, _scatter_views, False, parts[0], name)

    def send_ffn2(self, grads):
        self.ffn2 = self._scatter_start(grads, "scatter_ffn2_start")
        return self.ffn2[4]

    def send_mix(self, grads, after):
        self.recv.update(zip(FFN2_GROUP, _push_wait(self.ffn2, _scatter_views, after, "scatter_ffn2_wait")))
        self.late = self._scatter_start(grads, "scatter_late_start")
        return self.late[4]

    def finish(self, grads):
        names = tuple(grads)
        parts = [_blocks_from_full(n, grads[n]) for n in names]
        self.parts.update(zip(names, parts))
        got = _scatter_partials([p.astype(BF16) for p in parts])
        self.recv.update(zip(names, got))
        self.recv.update(zip(LATE_GROUP, _push_wait(self.late, _scatter_views, (got[0],), "scatter_late_wait")))
        return self.parts, self.recv


def kernel(
        x, ffn1_norm, ffn1_w_gate, ffn1_w_up, ffn1_w_down, mix_norm, w_in, q_norm, k_norm, mu_r, mu_k, mu_v, mu_w,
        mu_a, mu_g, w0, w1, w2, a0, a1, a2, g1, g2, k_k, k_a, r_k, ln_x_w, ln_x_b, w_out, ffn2_norm, ffn2_w_gate,
        ffn2_w_up, ffn2_w_down, loss_target, m_ffn1_norm, m_ffn1_w_gate, m_ffn1_w_up, m_ffn1_w_down, m_mix_norm,
        m_w_in, m_q_norm, m_k_norm, m_mu_r, m_mu_k, m_mu_v, m_mu_w, m_mu_a, m_mu_g, m_w0, m_w1, m_w2, m_a0, m_a1,
        m_a2, m_g1, m_g2, m_k_k, m_k_a, m_r_k, m_ln_x_w, m_ln_x_b, m_w_out, m_ffn2_norm, m_ffn2_w_gate, m_ffn2_w_up,
        m_ffn2_w_down, v_ffn1_norm, v_ffn1_w_gate, v_ffn1_w_up, v_ffn1_w_down, v_mix_norm, v_w_in, v_q_norm, v_k_norm,
        v_mu_r, v_mu_k, v_mu_v, v_mu_w, v_mu_a, v_mu_g, v_w0, v_w1, v_w2, v_a0, v_a1, v_a2, v_g1, v_g2, v_k_k, v_k_a,
        v_r_k, v_ln_x_w, v_ln_x_b, v_w_out, v_ffn2_norm, v_ffn2_w_gate, v_ffn2_w_up, v_ffn2_w_down):
    given = dict(locals())
    sharded = COL_SHARDED + ROW_SHARDED
    sharded = tuple(n for n in WEIGHTS if n in sharded)
    small = tuple(n for n in WEIGHTS if n not in sharded)

    ex = _Exchange(given)
    w = {n: given[n] for n in small}
    w.update(ex.first_weights)
    loss, dx, g = _local_step(x[0], loss_target[0], w, ex)
    parts, recv = ex.finish({n: g[n] for n in FFN1_GROUP})

    me = (2 * lax.axis_index("x") + lax.axis_index("y")).astype(jnp.int32).reshape(1)
    mine = []
    for n in sharded:
        p, rv = parts[n], recv[n]
        p2 = p.reshape(N_SHARDS, -1, p.shape[-1])
        mine.append(_reduce_own(me, p2, rv.reshape(3, -1, rv.shape[-1]), f"reduce_{n}"))
    theirs = _sibling_swap(mine)
    out = {}
    for n, a, b in zip(sharded, mine, theirs):
        shape = given[n].shape
        two_d = (-1, shape[-1])
        res = _adamw(given[n].reshape(two_d), a, b, given["m_" + n].reshape(two_d), given["v_" + n].reshape(two_d), f"adamw_{n}")
        out[n] = [r.reshape(shape) for r in res]

    gpack = _pack([g[n] for n in small], extra_rows=1)
    n_rows = sum(-(-given[n].size // PACK_COLS) for n in small)
    gpack = gpack.at[n_rows, :loss.shape[1]].set(loss[0])
    gsum = _allreduce_small(gpack)
    res = _adamw(_pack([given[n] for n in small], 1), gsum, jnp.zeros_like(gsum), _pack([given["m_" + n] for n in small], 1),
                 _pack([given["v_" + n] for n in small], 1), "adamw_small")
    like = [given[n] for n in small]
    for j, r in enumerate(res):
        for n, a in zip(small, _unpack(r, like)):
            out.setdefault(n, [None] * 4)[j] = a
    total_loss = gsum[n_rows, 0]
    return (total_loss, dx[None], *[out[n][0] for n in WEIGHTS], *[out[n][1] for n in WEIGHTS],
            *[out[n][2] for n in WEIGHTS], *[out[n][3] for n in WEIGHTS])
```

```python
import functools

import jax
import jax.numpy as jnp
from jax import lax
from jax.experimental import pallas as pl
from jax.experimental.pallas import tpu as pltpu

F32 = jnp.float32
BF16 = jnp.bfloat16
MESH = pl.DeviceIdType.MESH

RMS_EPS = 1e-6
GN_EPS = 64e-5
NEG_INF = -1e30
FFN_RESIDUAL = 0.5
HEAD_DIM = 64
ATT_BLOCK = 128
DILATIONS = (1, 4, 16)
SCAN_CHUNK = 64
TOKEN_TILE = 256

ADAM_LR = 0.001
ADAM_B1 = 0.9
ADAM_B2 = 0.999
ADAM_EPS = 1e-08
ADAM_WD = 0.01
ADAM_STEP = 10

VMEM_FULL = pl.BlockSpec(memory_space=pltpu.VMEM)
ANY = pl.BlockSpec(memory_space=pl.ANY)


VMEM_LIMIT = 56 * 1024 * 1024


def _params(*sem):
    return pltpu.CompilerParams(dimension_semantics=sem, vmem_limit_bytes=VMEM_LIMIT)


def _dot(a, b, dims):
    return lax.dot_general(a.astype(BF16), b.astype(BF16), (dims, ((), ())), preferred_element_type=F32)


def _dot_nn(a, b):
    return _dot(a, b, ((1,), (0,)))


def _dot_nt(a, b):
    return _dot(a, b, ((1,), (1,)))


def _dot_tn(a, b):
    return _dot(a, b, ((0,), (0,)))


@jax.custom_vjp
def _mm(a, b):
    return _dot_nn(a, b)


def _mm_fwd(a, b):
    return _dot_nn(a, b), (a, b)


def _mm_bwd(res, g):
    a, b = res
    return _dot_nt(g, b).astype(a.dtype), _dot_tn(a, g).astype(b.dtype)


_mm.defvjp(_mm_fwd, _mm_bwd)


def _bdot(a, b, ca, cb):
    return lax.dot_general(a.astype(BF16), b.astype(BF16), (((ca,), (cb,)), ((0,), (0,))), preferred_element_type=F32)


@jax.custom_vjp
def _bmm_nt(a, b):
    return _bdot(a, b, 2, 2)


def _bmm_nt_fwd(a, b):
    return _bdot(a, b, 2, 2), (a, b)


def _bmm_nt_bwd(res, g):
    a, b = res
    return _bdot(g, b, 2, 1), _bdot(g, a, 1, 1)


_bmm_nt.defvjp(_bmm_nt_fwd, _bmm_nt_bwd)


@jax.custom_vjp
def _bmm_nn(a, b):
    return _bdot(a, b, 2, 1)


def _bmm_nn_fwd(a, b):
    return _bdot(a, b, 2, 1), (a, b)


def _bmm_nn_bwd(res, g):
    a, b = res
    return _bdot(g, b, 2, 2), _bdot(a, g, 1, 1)


_bmm_nn.defvjp(_bmm_nn_fwd, _bmm_nn_bwd)


@jax.custom_vjp
def _bmm_tn(a, b):
    return _bdot(a, b, 1, 1)


def _bmm_tn_fwd(a, b):
    return _bdot(a, b, 1, 1), (a, b)


def _bmm_tn_bwd(res, g):
    a, b = res
    return _bdot(b, g, 2, 2), _bdot(a, g, 2, 1)


_bmm_tn.defvjp(_bmm_tn_fwd, _bmm_tn_bwd)


def _hdot(a, b, ca, cb):
    return lax.dot_general(a, b, (((ca,), (cb,)), ((0,), (0,))), precision=lax.Precision.HIGH, preferred_element_type=F32)


def _sigmoid(x):
    return 1.0 / (1.0 + jnp.exp(-x))


def _rms(x):
    return lax.rsqrt(jnp.mean(x * x, axis=-1, keepdims=True) + RMS_EPS)


def _ffn_fwd(x, norm, wg, wu, wd, dep, name):
    t, d = x.shape
    nc = wg.shape[0]
    tm = TOKEN_TILE

    def body(x_ref, n_ref, wg_ref, wu_ref, wd_ref, dep_ref, o_ref):
        xv = x_ref[...]
        h = (xv * _rms(xv) * n_ref[...]).astype(BF16)
        acc = jnp.zeros((tm, d), F32)
        for c in range(nc):
            g = jnp.dot(h, wg_ref[c], preferred_element_type=F32)
            u = jnp.dot(h, wu_ref[c], preferred_element_type=F32)
            a = (g * _sigmoid(g) * u).astype(BF16)
            acc = acc + jnp.dot(a, wd_ref[c], preferred_element_type=F32)
        o_ref[...] = xv + FFN_RESIDUAL * acc

    tile = pl.BlockSpec((tm, d), lambda i: (i, 0))
    return pl.pallas_call(
        body, name=name, grid=(t // tm,), out_shape=jax.ShapeDtypeStruct((t, d), F32),
        in_specs=[tile, pl.BlockSpec((1, d), lambda i: (0, 0)), VMEM_FULL, VMEM_FULL, VMEM_FULL, ANY],
        out_specs=tile, compiler_params=_params("arbitrary"),
    )(x, norm, wg, wu, wd, dep)


def _rmsnorm_bwd(xv, gain, dh):
    rs = _rms(xv)
    xn = xv * rs
    dxn = dh * gain
    dx = rs * (dxn - xn * jnp.mean(dxn * xn, axis=-1, keepdims=True))
    return dx, jnp.sum(dh * xn, axis=0, keepdims=True)


def _ffn_bwd(x, norm, wg, wu, wd, dy, dep, name):
    t, d = x.shape
    nc, _, fc = wg.shape
    tm = TOKEN_TILE
    nt = t // tm

    def body(x_ref, n_ref, wg_ref, wu_ref, wd_ref, dy_ref, dep_ref, dx_ref, dn_ref, dwg_ref, dwu_ref, dwd_ref, dh_ref):
        c, i = pl.program_id(0), pl.program_id(1)
        rows = pl.ds(pl.multiple_of(i * tm, tm), tm)
        xv = x_ref[...]
        gain = n_ref[...]
        h = (xv * _rms(xv) * gain).astype(BF16)
        dy = dy_ref[...]
        dyb = (FFN_RESIDUAL * dy).astype(BF16)
        g = jnp.dot(h, wg_ref[0], preferred_element_type=F32)
        u = jnp.dot(h, wu_ref[0], preferred_element_type=F32)
        sg = _sigmoid(g)
        s = g * sg
        a = (s * u).astype(BF16)
        da = _dot_nt(dyb, wd_ref[0])
        dub = (da * s).astype(BF16)
        dgb = (da * u * (sg * (1.0 + g * (1.0 - sg)))).astype(BF16)
        dwd_c = _dot_tn(a, dyb)
        dwg_c = _dot_tn(h, dgb)
        dwu_c = _dot_tn(h, dub)
        dh_c = _dot_nt(dgb, wg_ref[0]) + _dot_nt(dub, wu_ref[0])

        @pl.when(i == 0)
        def _():
            dwd_ref[0] = dwd_c
            dwg_ref[0] = dwg_c
            dwu_ref[0] = dwu_c

        @pl.when(i > 0)
        def _():
            dwd_ref[0] += dwd_c
            dwg_ref[0] += dwg_c
            dwu_ref[0] += dwu_c

        @pl.when(c == 0)
        def _():
            dh_ref[rows, :] = dh_c

        @pl.when(c > 0)
        def _():
            dh_ref[rows, :] += dh_c

        @pl.when(c == nc - 1)
        def _():
            dx, dn = _rmsnorm_bwd(xv, gain, dh_ref[rows, :])
            dx_ref[...] = dx + dy

            @pl.when(i == 0)
            def _():
                dn_ref[...] = dn

            @pl.when(i > 0)
            def _():
                dn_ref[...] += dn

    tile = pl.BlockSpec((tm, d), lambda c, i: (i, 0))
    row = pl.BlockSpec((1, d), lambda c, i: (0, 0))
    wcol = pl.BlockSpec((1, d, fc), lambda c, i: (c, 0, 0))
    wrow = pl.BlockSpec((1, fc, d), lambda c, i: (c, 0, 0))
    last = pl.BlockSpec((tm, d), lambda c, i: (jnp.where(c == nc - 1, i, 0), 0))
    return pl.pallas_call(
        body, name=name, grid=(nc, nt),
        out_shape=(jax.ShapeDtypeStruct((t, d), F32), jax.ShapeDtypeStruct((1, d), F32),
                   jax.ShapeDtypeStruct(wg.shape, F32), jax.ShapeDtypeStruct(wu.shape, F32),
                   jax.ShapeDtypeStruct(wd.shape, F32)),
        in_specs=[tile, row, wcol, wcol, wrow, tile, ANY],
        out_specs=(last, row, wcol, wcol, wrow),
        scratch_shapes=[pltpu.VMEM((t, d), F32)],
        compiler_params=_params("arbitrary", "arbitrary"),
    )(x, norm, wg, wu, wd, dy, dep)


def _store_heads(ref, v):
    for h in range(ref.shape[0]):
        ref[h] = v[:, h * HEAD_DIM:(h + 1) * HEAD_DIM]


def _load_heads(ref):
    return jnp.concatenate([ref[h] for h in range(ref.shape[0])], axis=-1)


N_HEAD_GROUPS = 3


def _proj_fwd(x, norm, w):
    t, d = x.shape
    ng, _, c = w.shape
    nh = c // HEAD_DIM
    tm = TOKEN_TILE

    def body(x_ref, n_ref, w_ref, q_ref, k_ref, v_ref, cur_ref):
        xv = x_ref[...]
        h = (xv * _rms(xv) * n_ref[...]).astype(BF16)
        for m, ref in enumerate((q_ref, k_ref, v_ref)):
            _store_heads(ref, jnp.dot(h, w_ref[m], preferred_element_type=F32))
        for m in range(N_HEAD_GROUPS, ng):
            j = m - N_HEAD_GROUPS
            cur_ref[:, j * c:(j + 1) * c] = jnp.dot(h, w_ref[m], preferred_element_type=F32)

    heads = pl.BlockSpec((nh, tm, HEAD_DIM), lambda i: (0, i, 0))
    hshape = jax.ShapeDtypeStruct((nh, t, HEAD_DIM), F32)
    wide = (ng - N_HEAD_GROUPS) * c
    return pl.pallas_call(
        body, name="proj_fwd", grid=(t // tm,),
        out_shape=(hshape, hshape, hshape, jax.ShapeDtypeStruct((t, wide), F32)),
        in_specs=[pl.BlockSpec((tm, d), lambda i: (i, 0)), pl.BlockSpec((1, d), lambda i: (0, 0)), VMEM_FULL],
        out_specs=(heads, heads, heads, pl.BlockSpec((tm, wide), lambda i: (i, 0))),
        compiler_params=_params("arbitrary"),
    )(x, norm, w)


def _proj_bwd(x, norm, w, dq, dk, dv, dcur, dres):
    t, d = x.shape
    ng, _, c = w.shape
    nh = c // HEAD_DIM
    tm = TOKEN_TILE

    def body(x_ref, n_ref, w_ref, dq_ref, dk_ref, dv_ref, dcur_ref, dres_ref, dx_ref, dn_ref, dw_ref):
        i = pl.program_id(0)

        @pl.when(i == 0)
        def _():
            dw_ref[...] = jnp.zeros_like(dw_ref)
            dn_ref[...] = jnp.zeros_like(dn_ref)

        xv = x_ref[...]
        gain = n_ref[...]
        h = (xv * _rms(xv) * gain).astype(BF16)
        dh = jnp.zeros((tm, d), F32)
        for m in range(ng):
            j = m - N_HEAD_GROUPS
            dp = _load_heads((dq_ref, dk_ref, dv_ref)[m]) if j < 0 else dcur_ref[:, j * c:(j + 1) * c]
            dp = dp.astype(BF16)
            dw_ref[m] += _dot_tn(h, dp)
            dh = dh + _dot_nt(dp, w_ref[m])
        dx, dn = _rmsnorm_bwd(xv, gain, dh)
        dx_ref[...] = dx + dres_ref[...]
        dn_ref[...] += dn

    tile = pl.BlockSpec((tm, d), lambda i: (i, 0))
    row = pl.BlockSpec((1, d), lambda i: (0, 0))
    heads = pl.BlockSpec((nh, tm, HEAD_DIM), lambda i: (0, i, 0))
    wide = (ng - N_HEAD_GROUPS) * c
    return pl.pallas_call(
        body, name="proj_bwd", grid=(t // tm,),
        out_shape=(jax.ShapeDtypeStruct((t, d), F32), jax.ShapeDtypeStruct((1, d), F32),
                   jax.ShapeDtypeStruct(w.shape, F32)),
        in_specs=[tile, row, VMEM_FULL, heads, heads, heads, pl.BlockSpec((tm, wide), lambda i: (i, 0)), tile],
        out_specs=(tile, row, VMEM_FULL),
        compiler_params=_params("arbitrary"),
    )(x, norm, w, dq, dk, dv, dcur, dres)


def _mixout_fwd(x, att, opg, gate, w):
    t, d = x.shape
    nh = att.shape[0]
    half = gate.shape[1]
    tm = TOKEN_TILE

    def body(x_ref, att_ref, opg_ref, g_ref, w_ref, o_ref):
        mix = jnp.concatenate([_load_heads(att_ref), _load_heads(opg_ref) * g_ref[...]], axis=-1).astype(BF16)
        o_ref[...] = x_ref[...] + jnp.dot(mix, w_ref[...], preferred_element_type=F32)

    tile = pl.BlockSpec((tm, d), lambda i: (i, 0))
    htile = pl.BlockSpec((tm, half), lambda i: (i, 0))
    heads = pl.BlockSpec((nh, tm, HEAD_DIM), lambda i: (0, i, 0))
    return pl.pallas_call(
        body, name="mixout_fwd", grid=(t // tm,), out_shape=jax.ShapeDtypeStruct((t, d), F32),
        in_specs=[tile, heads, heads, htile, VMEM_FULL], out_specs=tile, compiler_params=_params("arbitrary"),
    )(x, att, opg, gate, w)


def _mixout_bwd(att, opg, gate, w, dy, dep):
    nh, t, _ = att.shape
    half = gate.shape[1]
    d = dy.shape[1]
    tm = TOKEN_TILE

    def body(att_ref, opg_ref, g_ref, w_ref, dy_ref, dep_ref, datt_ref, dopg_ref, dg_ref, dw_ref):
        i = pl.program_id(0)
        opg_v, g_v = _load_heads(opg_ref), g_ref[...]
        mix = jnp.concatenate([_load_heads(att_ref), opg_v * g_v], axis=-1).astype(BF16)
        dyb = dy_ref[...].astype(BF16)
        dmix = _dot_nt(dyb, w_ref[...])
        dw = _dot_tn(mix, dyb)
        _store_heads(datt_ref, dmix[:, :half])
        drw = dmix[:, half:]
        _store_heads(dopg_ref, drw * g_v)
        dg_ref[...] = drw * opg_v

        @pl.when(i == 0)
        def _():
            dw_ref[...] = dw

        @pl.when(i > 0)
        def _():
            dw_ref[...] += dw

    tile = pl.BlockSpec((tm, d), lambda i: (i, 0))
    htile = pl.BlockSpec((tm, half), lambda i: (i, 0))
    heads = pl.BlockSpec((nh, tm, HEAD_DIM), lambda i: (0, i, 0))
    hshape = jax.ShapeDtypeStruct((nh, t, HEAD_DIM), F32)
    return pl.pallas_call(
        body, name="mixout_bwd", grid=(t // tm,),
        out_shape=(hshape, hshape, jax.ShapeDtypeStruct((t, half), F32), jax.ShapeDtypeStruct(w.shape, F32)),
        in_specs=[heads, heads, htile, VMEM_FULL, tile, ANY],
        out_specs=(heads, heads, htile, pl.BlockSpec(w.shape, lambda i: (0, 0))),
        compiler_params=_params("arbitrary"),
    )(att, opg, gate, w, dy, dep)


def _loss_head(y, target):
    t, d = y.shape
    tm = TOKEN_TILE

    def body(y_ref, t_ref, dy_ref, loss_ref):
        i = pl.program_id(0)
        err = y_ref[...] - t_ref[...]
        dy_ref[...] = err * (1.0 / d)
        part = 0.5 * jnp.sum(jnp.mean(err * err, axis=-1, keepdims=True), axis=0, keepdims=True)

        @pl.when(i == 0)
        def _():
            loss_ref[...] = jnp.zeros_like(loss_ref)

        loss_ref[...] += jnp.broadcast_to(part, loss_ref.shape)

    tile = pl.BlockSpec((tm, d), lambda i: (i, 0))
    return pl.pallas_call(
        body, name="loss_head", grid=(t // tm,),
        out_shape=(jax.ShapeDtypeStruct((t, d), F32), jax.ShapeDtypeStruct((1, 128), F32)),
        in_specs=[tile, tile], out_specs=(tile, pl.BlockSpec((1, 128), lambda i: (0, 0))),
        compiler_params=_params("arbitrary"),
    )(y, target)


def _att_block(q, kc, vc, qn, kn, kp=None, vp=None, has_prev=True):
    blk = q.shape[1]

    def hn(v, gain):
        return v * _rms(v) * gain

    qh = hn(q, qn)
    scale = HEAD_DIM ** -0.5
    qi = lax.broadcasted_iota(jnp.int32, (blk, blk), 0)
    kj = lax.broadcasted_iota(jnp.int32, (blk, blk), 1)
    sc = jnp.where(kj <= qi, _bmm_nt(qh, hn(kc, kn)) * scale, NEG_INF)
    top = jnp.max(sc, axis=-1, keepdims=True)
    if kp is not None:
        sp = jnp.where((kj >= qi) & has_prev, _bmm_nt(qh, hn(kp, kn)) * scale, NEG_INF)
        top = jnp.maximum(top, jnp.max(sp, axis=-1, keepdims=True))
    m = lax.stop_gradient(top)
    pc = jnp.exp(sc - m)
    den = jnp.sum(pc, axis=-1, keepdims=True)
    acc = _bmm_nn(pc, vc)
    if kp is not None:
        pp = jnp.exp(sp - m)
        den = den + jnp.sum(pp, axis=-1, keepdims=True)
        acc = acc + _bmm_nn(pp, vp)
    o = acc / den
    return o, jnp.broadcast_to(m + jnp.log(den), o.shape)


def _class_rows(n, dil):
    base = n * (ATT_BLOCK * dil)
    if dil == 1:
        return [pl.ds(pl.multiple_of(base, ATT_BLOCK), ATT_BLOCK)]
    return [pl.ds(base + r, ATT_BLOCK, stride=dil) for r in range(dil)]


def _take(ref, rows):
    return jnp.stack([ref[0, r, :] for r in rows])


def _put(ref, rows, val):
    for g, r in enumerate(rows):
        ref[0, r, :] = val[g]


def _put_add(ref, rows, val):
    for g, r in enumerate(rows):
        ref[0, r, :] += val[g]


def _merge_fn(o1, o2, o3, l1, l2, l3):
    m = lax.stop_gradient(jnp.maximum(jnp.maximum(l1, l2), l3))
    e1, e2, e3 = jnp.exp(l1 - m), jnp.exp(l2 - m), jnp.exp(l3 - m)
    return (e1 * o1 + e2 * o2 + e3 * o3) / (e1 + e2 + e3)


def _att_head_specs(t):
    head = pl.BlockSpec((1, t, HEAD_DIM), lambda h: (h, 0, 0))
    gain = pl.BlockSpec((1, 1, HEAD_DIM), lambda h: (0, 0, 0))
    return head, gain


def _for_each_block(t, block):
    carry = None
    for p, dil in enumerate(DILATIONS):
        nb = t // (ATT_BLOCK * dil)
        if dil == 1:
            first = block(p, dil, 0, None, True, carry)
            carry = lax.fori_loop(1, nb, lambda n, c, p=p, dil=dil: block(p, dil, n, n - 1, True, c), first)
        else:
            for n in range(nb):
                carry = block(p, dil, n, n - 1 if n else None, True, carry)
    return carry


def _att_fwd(q, k, v, qn, kn):
    nh, t, dh = q.shape
    head, gain = _att_head_specs(t)

    def body(q_ref, k_ref, v_ref, qn_ref, kn_ref, att_ref, *saved):
        o_refs, l_refs = saved[:3], saved[3:]
        gq, gk = qn_ref[...], kn_ref[...]

        def block(p, dil, n, prev_n, has_prev, carry):
            rows = _class_rows(n, dil)
            args = [_take(q_ref, rows), _take(k_ref, rows), _take(v_ref, rows), gq, gk]
            if prev_n is not None:
                before = _class_rows(prev_n, dil)
                args += [_take(k_ref, before), _take(v_ref, before), has_prev]
            o, lse = _att_block(*args)
            _put(o_refs[p], rows, o)
            _put(l_refs[p], rows, lse)
            return 0

        _for_each_block(t, block)

        def merge(j, carry):
            rows = pl.ds(pl.multiple_of(j * ATT_BLOCK, ATT_BLOCK), ATT_BLOCK)
            att_ref[0, rows, :] = _merge_fn(*[r[0, rows, :] for r in saved])
            return carry

        lax.fori_loop(0, t // ATT_BLOCK, merge, 0)

    return pl.pallas_call(
        body, name="att_fwd", grid=(nh,), out_shape=(jax.ShapeDtypeStruct(q.shape, F32),) * 7,
        in_specs=[head, head, head, gain, gain], out_specs=(head,) * 7, compiler_params=_params("arbitrary"),
    )(q, k, v, qn, kn)


def _att_bwd(q, k, v, qn, kn, saved, datt):
    nh, t, dh = q.shape
    head, gain = _att_head_specs(t)

    def body(q_ref, k_ref, v_ref, qn_ref, kn_ref, o1, o2, o3, l1, l2, l3, datt_ref,
             dq_ref, dk_ref, dv_ref, dqn_ref, dkn_ref):
        for ref in (dq_ref, dk_ref, dv_ref):
            ref[...] = jnp.zeros_like(ref)

        @pl.when(pl.program_id(0) == 0)
        def _():
            dqn_ref[...] = jnp.zeros_like(dqn_ref)
            dkn_ref[...] = jnp.zeros_like(dkn_ref)

        gq, gk = qn_ref[...], kn_ref[...]

        def block(p, dil, n, prev_n, has_prev, carry):
            rows = _class_rows(n, dil)
            _, merge_vjp = jax.vjp(_merge_fn, *[_take(r, rows) for r in (o1, o2, o3, l1, l2, l3)])
            cts = merge_vjp(_take(datt_ref, rows))
            args = [_take(q_ref, rows), _take(k_ref, rows), _take(v_ref, rows), gq, gk]
            if prev_n is not None:
                before = _class_rows(prev_n, dil)
                args += [_take(k_ref, before), _take(v_ref, before)]
            _, block_vjp = jax.vjp(functools.partial(_att_block, has_prev=has_prev), *args)
            grads = block_vjp((cts[p], cts[3 + p]))
            _put_add(dq_ref, rows, grads[0])
            _put_add(dk_ref, rows, grads[1])
            _put_add(dv_ref, rows, grads[2])
            if prev_n is not None:
                _put_add(dk_ref, before, grads[5])
                _put_add(dv_ref, before, grads[6])
            if carry is None:
                return grads[3], grads[4]
            return carry[0] + grads[3], carry[1] + grads[4]

        dgq, dgk = _for_each_block(t, block)
        dqn_ref[...] += dgq
        dkn_ref[...] += dgk

    hshape = jax.ShapeDtypeStruct(q.shape, F32)
    gshape = jax.ShapeDtypeStruct((1, 1, dh), F32)
    return pl.pallas_call(
        body, name="att_bwd", grid=(nh,), out_shape=(hshape, hshape, hshape, gshape, gshape),
        in_specs=[head, head, head, gain, gain] + [head] * 7, out_specs=(head, head, head, gain, gain),
        compiler_params=_params("arbitrary"),
    )(q, k, v, qn, kn, *saved, datt)


RWKV_VEC = ("mu_r", "mu_k", "mu_v", "mu_w", "mu_a", "mu_g", "w0", "a0", "k_k", "k_a")
RWKV_MAT = ("w1", "w2", "a1", "a2", "g1", "g2")


def _rwkv_pre_fn(cur, prev, vec, w1, w2, a1, a2, g1, g2):
    c = cur.shape[1] // 4
    mu_r, mu_k, mu_v, mu_w, mu_a, mu_g, w0, a0, k_k, k_a = (vec[j:j + 1] for j in range(10))

    def lerp(j, mu):
        xc, xp = cur[:, j * c:(j + 1) * c], prev[:, j * c:(j + 1) * c]
        return xc + (xp - xc) * mu

    r, k, v = lerp(0, mu_r), lerp(1, mu_k), lerp(2, mu_v)
    cw, ca, cg = lerp(3, mu_w), lerp(3, mu_a), lerp(3, mu_g)
    z = w0 + _mm(jnp.tanh(_mm(cw, w1)), w2)
    w_log = jnp.minimum(z, 0.0) - jnp.log(1.0 + jnp.exp(-jnp.abs(z))) - 0.5
    lw = -jnp.exp(w_log)
    a = _sigmoid(a0 + _mm(_mm(ca, a1), a2))
    gate = _mm(_sigmoid(_mm(cg, g1)), g2)
    kkraw = k * k_k
    kmod = k * (1.0 + (a - 1.0) * k_a)
    return r, lw, kmod, v, kkraw, a, gate


HALO_ROWS = 8


def _rwkv_pre_specs(c, mats, tile_of):
    tm = TOKEN_TILE
    nh = c // HEAD_DIM
    wide = pl.BlockSpec((tm, 4 * c), lambda j: (tile_of(j), 0))
    halo = pl.BlockSpec((HALO_ROWS, 4 * c), lambda j: (jnp.maximum(tile_of(j) * (tm // HALO_ROWS) - 1, 0), 0))
    one = pl.BlockSpec((tm, c), lambda j: (tile_of(j), 0))
    heads = pl.BlockSpec((nh, tm, HEAD_DIM), lambda j: (0, tile_of(j), 0))
    vec = pl.BlockSpec((10, c), lambda j: (0, 0))
    mspecs = [pl.BlockSpec(m.shape, lambda j: (0, 0)) for m in mats]
    return wide, halo, one, heads, vec, mspecs


def _previous_rows(cur, halo, tile):
    first = jnp.where(tile > 0, halo[HALO_ROWS - 1:HALO_ROWS], 0.0)
    rows = lax.broadcasted_iota(jnp.int32, cur.shape, 0)
    return jnp.where(rows == 0, first, pltpu.roll(cur, 1, axis=0))


def _rwkv_pre_fwd(cur, vec, mats):
    t, c4 = cur.shape
    c = c4 // 4
    wide, halo, one, heads, vspec, mspecs = _rwkv_pre_specs(c, mats, lambda j: j)

    def body(cur_ref, halo_ref, vec_ref, *rest):
        mrefs, outs = rest[:6], rest[6:]
        cur_v = cur_ref[...]
        prev = _previous_rows(cur_v, halo_ref[...], pl.program_id(0))
        vals = _rwkv_pre_fn(cur_v, prev, vec_ref[...], *(m[...] for m in mrefs))
        for ref, val in zip(outs[:6], vals[:6]):
            _store_heads(ref, val)
        outs[6][...] = vals[6]

    hshape = jax.ShapeDtypeStruct((c // HEAD_DIM, t, HEAD_DIM), F32)
    return pl.pallas_call(
        body, name="rwkv_pre_fwd", grid=(t // TOKEN_TILE,), out_shape=(hshape,) * 6 + (jax.ShapeDtypeStruct((t, c), F32),),
        in_specs=[wide, halo, vspec] + mspecs, out_specs=(heads,) * 6 + (one,), compiler_params=_params("arbitrary"),
    )(cur, cur, vec, *mats)


def _rwkv_pre_bwd(cur, vec, mats, cts, dgate):
    t, c4 = cur.shape
    c = c4 // 4
    tm = TOKEN_TILE
    nt = t // tm
    wide, halo, one, heads, vspec, mspecs = _rwkv_pre_specs(c, mats, lambda j: nt - 1 - j)

    def body(cur_ref, halo_ref, vec_ref, *rest):
        mrefs, ctrefs, dgate_ref, outs, carry_ref = rest[:6], rest[6:12], rest[12], rest[13:-1], rest[-1]
        j = pl.program_id(0)

        @pl.when(j == 0)
        def _():
            carry_ref[...] = jnp.zeros_like(carry_ref)
            for ref in outs[1:]:
                ref[...] = jnp.zeros_like(ref)

        cur_v = cur_ref[...]
        prev = _previous_rows(cur_v, halo_ref[...], nt - 1 - j)
        _, vjp = jax.vjp(_rwkv_pre_fn, cur_v, prev, vec_ref[...], *(m[...] for m in mrefs))
        grads = vjp(tuple(_load_heads(r) for r in ctrefs) + (dgate_ref[...],))
        dprev = grads[1]
        rows = lax.broadcasted_iota(jnp.int32, dprev.shape, 0)
        outs[0][...] = grads[0] + jnp.where(rows == tm - 1, carry_ref[0:1], pltpu.roll(dprev, tm - 1, axis=0))
        carry_ref[0:1] = dprev[0:1]
        for ref, val in zip(outs[1:], grads[2:]):
            ref[...] += val

    return pl.pallas_call(
        body, name="rwkv_pre_bwd", grid=(nt,),
        out_shape=(jax.ShapeDtypeStruct(cur.shape, F32), jax.ShapeDtypeStruct(vec.shape, F32))
        + tuple(jax.ShapeDtypeStruct(m.shape, F32) for m in mats),
        in_specs=[wide, halo, vspec] + mspecs + [heads] * 6 + [one], out_specs=(wide, vspec) + tuple(mspecs),
        scratch_shapes=[pltpu.VMEM((HALO_ROWS, c4), F32)], compiler_params=_params("arbitrary"),
    )(cur, cur, vec, *mats, *cts, dgate)


def _scan_chunk_fn(h0, r, lw, k, v, kkraw, a, rk, lnw, lnb):
    n = r.shape[1]
    nrm = jnp.sqrt(jnp.sum(kkraw * kkraw, axis=-1, keepdims=True))
    kk = kkraw / jnp.maximum(nrm, 1e-12)
    av, bv = -kk, kk * a
    ti = lax.broadcasted_iota(jnp.int32, (n, n), 0)
    si = lax.broadcasted_iota(jnp.int32, (n, n), 1)
    incl, strict = ti >= si, ti > si
    ones = jnp.broadcast_to(incl.astype(F32)[None], (r.shape[0], n, n))
    cum = _hdot(ones, lw, 2, 1)
    at, rt = av * jnp.exp(cum - lw), r * jnp.exp(cum)
    inv = jnp.exp(-cum)
    bt, kt = bv * inv, k * inv
    lab = jnp.where(strict, _hdot(at, bt, 2, 2), 0.0)
    lak = jnp.where(strict, _hdot(at, kt, 2, 2), 0.0)
    rb = jnp.where(incl, _hdot(rt, bt, 2, 2), 0.0)
    rkm = jnp.where(incl, _hdot(rt, kt, 2, 2), 0.0)
    u = _bmm_nn(at, h0) + _bmm_nn(lak, v)
    p = lab
    m = 1
    while m < n:
        u = u + _bmm_nn(p, u)
        m *= 2
        if m < n:
            p = _bmm_nn(p, p)
    y = _bmm_nn(rt, h0) + _bmm_nn(rb, u) + _bmm_nn(rkm, v)
    last = jnp.exp(jnp.sum(lw, axis=1, keepdims=True))
    h1 = jnp.swapaxes(last, 1, 2) * (h0 + _bmm_tn(bt, u) + _bmm_tn(kt, v))
    mean = jnp.mean(y, axis=-1, keepdims=True)
    yc = y - mean
    var = jnp.mean(yc * yc, axis=-1, keepdims=True)
    yn = yc * lax.rsqrt(var + GN_EPS) * lnw + lnb
    bonus = jnp.sum(r * k * rk, axis=-1, keepdims=True) * v
    return yn + bonus, h1


def _scan_specs(h, t, dh, rev):
    n = SCAN_CHUNK
    nc = t // n
    pos = (lambda c: (0, nc - 1 - c, 0)) if rev else (lambda c: (0, c, 0))
    st = (lambda c: (nc - 1 - c, 0, 0, 0)) if rev else (lambda c: (c, 0, 0, 0))
    seq = pl.BlockSpec((h, n, dh), pos)
    par = pl.BlockSpec((h, 1, dh), lambda c: (0, 0, 0))
    state = pl.BlockSpec((1, h, dh, dh), st)
    return seq, par, state


def _scan_fwd(seqs, pars):
    h, t, dh = seqs[0].shape
    nc = t // SCAN_CHUNK
    seq, par, state = _scan_specs(h, t, dh, False)

    def body(r, lw, k, v, kkraw, a, rk, lnw, lnb, o_ref, st_ref, h_ref):
        @pl.when(pl.program_id(0) == 0)
        def _():
            h_ref[...] = jnp.zeros_like(h_ref)

        h0 = h_ref[...]
        st_ref[0] = h0
        o, h1 = _scan_chunk_fn(h0, r[...], lw[...], k[...], v[...], kkraw[...], a[...], rk[...], lnw[...], lnb[...])
        o_ref[...] = o
        h_ref[...] = h1

    return pl.pallas_call(
        body, name="rwkv_scan_fwd", grid=(nc,),
        out_shape=(jax.ShapeDtypeStruct((h, t, dh), F32), jax.ShapeDtypeStruct((nc, h, dh, dh), F32)),
        in_specs=[seq] * 6 + [par] * 3, out_specs=(seq, state),
        scratch_shapes=[pltpu.VMEM((h, dh, dh), F32)], compiler_params=_params("arbitrary"),
    )(*seqs, *pars)


def _scan_bwd(seqs, pars, states, do):
    h, t, dh = seqs[0].shape
    nc = t // SCAN_CHUNK
    seq, par, state = _scan_specs(h, t, dh, True)

    def body(r, lw, k, v, kkraw, a, rk, lnw, lnb, st_ref, do_ref, *rest):
        douts, dpars, dh_ref = rest[:6], rest[6:9], rest[9]
        first = pl.program_id(0) == 0

        @pl.when(first)
        def _():
            dh_ref[...] = jnp.zeros_like(dh_ref)

        _, vjp = jax.vjp(_scan_chunk_fn, st_ref[0], r[...], lw[...], k[...], v[...], kkraw[...], a[...],
                         rk[...], lnw[...], lnb[...])
        grads = vjp((do_ref[...], dh_ref[...]))
        dh_ref[...] = grads[0]
        for ref, val in zip(douts, grads[1:7]):
            ref[...] = val

        @pl.when(first)
        def _():
            for ref, val in zip(dpars, grads[7:]):
                ref[...] = val

        @pl.when(jnp.logical_not(first))
        def _():
            for ref, val in zip(dpars, grads[7:]):
                ref[...] += val

    sshape = jax.ShapeDtypeStruct((h, t, dh), F32)
    pshape = jax.ShapeDtypeStruct((h, 1, dh), F32)
    return pl.pallas_call(
        body, name="rwkv_scan_bwd", grid=(nc,), out_shape=(sshape,) * 6 + (pshape,) * 3,
        in_specs=[seq] * 6 + [par] * 3 + [state, seq], out_specs=(seq,) * 6 + (par,) * 3,
        scratch_shapes=[pltpu.VMEM((h, dh, dh), F32)], compiler_params=_params("arbitrary"),
    )(*seqs, *pars, states, do)


def _local_step(x, target, w, ex):
    w = dict(w)
    c = w["mu_r"].shape[-1]
    qn, kn = w["q_norm"].reshape(1, 1, HEAD_DIM), w["k_norm"].reshape(1, 1, HEAD_DIM)
    vec = jnp.concatenate([w[n].reshape(1, c) for n in RWKV_VEC], axis=0)
    pars = [w[n].reshape(-1, 1, HEAD_DIM) for n in ("r_k", "ln_x_w", "ln_x_b")]
    no_dep = jnp.zeros(DEP_SHAPE, F32)

    x1 = _ffn_fwd(x, w["ffn1_norm"], w["ffn1_w_gate"], w["ffn1_w_up"], w["ffn1_w_down"], ex.first_dep, "ffn1_fwd")
    w.update(ex.mix_weights((x1,)))
    mats = [w[n] for n in RWKV_MAT]
    q, k, v, cur = _proj_fwd(x1, w["mix_norm"], w["w_in"])
    att, *saved = _att_fwd(q, k, v, qn, kn)
    pre = _rwkv_pre_fwd(cur, vec, mats)
    seqs, gate = pre[:6], pre[6]
    opg, states = _scan_fwd(seqs, pars)
    w.update(ex.out_weights((att, opg)))
    x2 = _mixout_fwd(x1, att, opg, gate, w["w_out"])
    x3 = _ffn_fwd(x2, w["ffn2_norm"], w["ffn2_w_gate"], w["ffn2_w_up"], w["ffn2_w_down"], no_dep, "ffn2_fwd")
    dy, loss = _loss_head(x3, target)

    g = {}
    dx2, g["ffn2_norm"], g["ffn2_w_gate"], g["ffn2_w_up"], g["ffn2_w_down"] = _ffn_bwd(
        x2, w["ffn2_norm"], w["ffn2_w_gate"], w["ffn2_w_up"], w["ffn2_w_down"], dy, no_dep, "ffn2_bwd")
    dep = ex.send_ffn2({n: g[n] for n in ("ffn2_w_gate", "ffn2_w_up", "ffn2_w_down")})
    datt, dopg, dgate, g["w_out"] = _mixout_bwd(att, opg, gate, w["w_out"], dx2, dep)
    dscan = _scan_bwd(seqs, pars, states, dopg)
    for n, d in zip(("r_k", "ln_x_w", "ln_x_b"), dscan[6:]):
        g[n] = d
    dcur, dvec, *dmats = _rwkv_pre_bwd(cur, vec, mats, dscan[:6], dgate)
    for n, d in zip(RWKV_MAT, dmats):
        g[n] = d
    for j, n in enumerate(RWKV_VEC):
        g[n] = dvec[j:j + 1]
    dq, dk, dv, g["q_norm"], g["k_norm"] = _att_bwd(q, k, v, qn, kn, saved, datt)
    dx1, g["mix_norm"], g["w_in"] = _proj_bwd(x1, w["mix_norm"], w["w_in"], dq, dk, dv, dcur, dx2)
    dep = ex.send_mix({n: g[n] for n in ("w_in", "w_out") + RWKV_MAT}, (dx1,))
    dx, g["ffn1_norm"], g["ffn1_w_gate"], g["ffn1_w_up"], g["ffn1_w_down"] = _ffn_bwd(
        x, w["ffn1_norm"], w["ffn1_w_gate"], w["ffn1_w_up"], w["ffn1_w_down"], dx1, dep, "ffn1_bwd")
    return loss, dx, g


N_SHARDS = 4


def _place():
    return lax.axis_index("x"), lax.axis_index("y"), lax.axis_index("c")


def _chip_peers(x, y):
    return [(1 - x, y), (x, 1 - y), (1 - x, 1 - y)]


HBM = pl.BlockSpec(memory_space=pltpu.HBM)
SEM = pl.BlockSpec(memory_space=pltpu.SEMAPHORE)
DEP_SHAPE = (8, 128)


def _gather_views(i, srcs, lands, k, px, py, me):
    return (srcs[i], lands[i].at[me]), (srcs[i], lands[i].at[2 * px + py])


def _scatter_views(i, srcs, lands, k, px, py, me):
    return (srcs[i].at[2 * px + py], lands[i].at[k]), (srcs[i].at[me], lands[i].at[k])


def _push_start(srcs, land_shapes, views, own_slot, after, name):
    n = len(srcs)

    def body(*refs):
        src_refs, land_refs = refs[:n], refs[n:2 * n]
        send_sems, recv_sems = refs[2 * n + 1:2 * n + 3]
        token = refs[4 * n + 3]
        x, y, c = _place()
        me = 2 * x + y
        for i in range(n):
            for k, (px, py) in enumerate(_chip_peers(x, y)):
                (src, dst), _ = views(i, src_refs, land_refs, k, px, py, me)
                pltpu.make_async_remote_copy(
                    src_ref=src, dst_ref=dst, send_sem=send_sems.at[3 * i + k], recv_sem=recv_sems.at[3 * i + k],
                    device_id=(px, py, c), device_id_type=MESH).start()
        token[...] = jnp.zeros_like(token)

    sems = pltpu.SemaphoreType.DMA((3 * n,))
    lands = [lax.empty(s.shape, s.dtype) for s in land_shapes]
    if own_slot:
        me = 2 * lax.axis_index("x") + lax.axis_index("y")
        lands = [lax.dynamic_update_index_in_dim(z, s, me, 0) for z, s in zip(lands, srcs)]
    lands = [pltpu.with_memory_space_constraint(z, pltpu.HBM) for z in lands]
    srcs = [pltpu.with_memory_space_constraint(s, pltpu.HBM) for s in srcs]
    outs = pl.pallas_call(
        body, name=name,
        out_shape=(sems, sems, *[pltpu.HBM(s.shape, s.dtype) for s in srcs], *[pltpu.HBM(s.shape, s.dtype) for s in land_shapes],
                   jax.ShapeDtypeStruct(DEP_SHAPE, F32)),
        in_specs=[HBM] * (2 * n) + [ANY], out_specs=(SEM, SEM, *[HBM] * (2 * n), VMEM_FULL),
        input_output_aliases={i: 2 + i for i in range(2 * n)},
        compiler_params=pltpu.CompilerParams(has_side_effects=pltpu.SideEffectType.DATAFLOW_SIDE_EFFECTING),
    )(*srcs, *lands, after)
    return outs[0], outs[1], outs[2:2 + n], outs[2 + n:2 + 2 * n], outs[2 + 2 * n]


def _push_wait(started, views, after, name):
    send_sems, recv_sems, srcs, lands, _ = started
    n = len(srcs)

    def body(*refs):
        src_refs, land_refs = refs[:n], refs[n:2 * n]
        send_sems, recv_sems = refs[2 * n:2 * n + 2]
        x, y, c = _place()
        me = 2 * x + y
        for i in range(n):
            for k, (px, py) in enumerate(_chip_peers(x, y)):
                _, (src, dst) = views(i, src_refs, land_refs, k, px, py, me)
                landing = pltpu.make_async_remote_copy(
                    src_ref=src, dst_ref=dst, send_sem=send_sems.at[3 * i + k], recv_sem=recv_sems.at[3 * i + k],
                    device_id=(px, py, c), device_id_type=MESH)
                landing.wait_send()
                landing.wait_recv()

    outs = pl.pallas_call(
        body, name=name,
        out_shape=tuple(pltpu.HBM(s.shape, s.dtype) for s in (*srcs, *lands)),
        in_specs=[HBM] * (2 * n) + [SEM, SEM] + [ANY] * len(after), out_specs=(HBM,) * (2 * n),
        input_output_aliases={i: i for i in range(2 * n)},
        compiler_params=pltpu.CompilerParams(has_side_effects=pltpu.SideEffectType.DATAFLOW_SIDE_EFFECTING),
    )(*srcs, *lands, send_sems, recv_sems, *after)
    return outs[n:]


def _sibling_swap(arrays, name):
    n = len(arrays)

    def body(*refs):
        ins, outs = refs[:n], refs[n:2 * n]
        send_sems, recv_sems = refs[2 * n:]
        x, y, c = _place()
        copies = []
        for i in range(n):
            cp = pltpu.make_async_remote_copy(
                src_ref=ins[i], dst_ref=outs[i], send_sem=send_sems.at[i], recv_sem=recv_sems.at[i],
                device_id=(x, y, 1 - c), device_id_type=MESH)
            cp.start()
            copies.append(cp)
        for cp in copies:
            cp.wait()

    return pl.pallas_call(
        body, name=name,
        out_shape=tuple(jax.ShapeDtypeStruct(a.shape, a.dtype) for a in arrays),
        in_specs=[ANY] * n, out_specs=(ANY,) * n,
        scratch_shapes=[pltpu.SemaphoreType.DMA((n,)), pltpu.SemaphoreType.DMA((n,))],
    )(*arrays)


N_DEV = 8


def _allreduce_small(pack):
    def body(in_ref, out_ref, buf, send_sems, recv_sems):
        x, y, c = _place()
        me = 4 * x + 2 * y + c
        buf[me] = in_ref[...]

        def copy(j, slot):
            px, py, pc = x ^ (j >> 2), y ^ ((j >> 1) & 1), c ^ (j & 1)
            return pltpu.make_async_remote_copy(
                src_ref=in_ref, dst_ref=buf.at[slot(px, py, pc)], send_sem=send_sems.at[j], recv_sem=recv_sems.at[j],
                device_id=(px, py, pc), device_id_type=MESH)

        for j in range(1, N_DEV):
            copy(j, lambda px, py, pc: me).start()
        for j in range(1, N_DEV):
            landing = copy(j, lambda px, py, pc: 4 * px + 2 * py + pc)
            landing.wait_send()
            landing.wait_recv()
        acc = buf[0]
        for s in range(1, N_DEV):
            acc = acc + buf[s]
        out_ref[...] = acc

    return pl.pallas_call(
        body, name="allreduce_small", out_shape=jax.ShapeDtypeStruct(pack.shape, F32),
        in_specs=[VMEM_FULL], out_specs=VMEM_FULL,
        scratch_shapes=[pltpu.VMEM((N_DEV,) + pack.shape, F32), pltpu.SemaphoreType.DMA((N_DEV,)),
                        pltpu.SemaphoreType.DMA((N_DEV,))],
    )(pack)


ROW_TILE_MAX = 256
BF16_SUBLANES = 16


def _row_tile(rows):
    for tr in range(min(rows, ROW_TILE_MAX), 0, -1):
        if rows % tr == 0 and tr % BF16_SUBLANES == 0:
            return tr
    return rows


def _reduce_own(me, part, recv, dep, name):
    _, r, cols = part.shape
    tr = _row_tile(r)

    def body(me_ref, p_ref, rv_ref, dep_ref, o_ref):
        acc = p_ref[0]
        for k in range(3):
            acc = acc + rv_ref[k].astype(F32)
        o_ref[...] = acc

    return pl.pallas_call(
        body, name=name, out_shape=jax.ShapeDtypeStruct((r, cols), F32),
        grid_spec=pltpu.PrefetchScalarGridSpec(
            num_scalar_prefetch=1, grid=(r // tr,),
            in_specs=[pl.BlockSpec((1, tr, cols), lambda i, me_ref: (me_ref[0], i, 0)),
                      pl.BlockSpec((3, tr, cols), lambda i, me_ref: (0, i, 0)), ANY],
            out_specs=pl.BlockSpec((tr, cols), lambda i, me_ref: (i, 0))),
        compiler_params=_params("arbitrary"),
    )(me, part, recv, dep)


def _adamw(w, ga, gb, m, v, name):
    r, cols = w.shape
    tr = _row_tile(r)
    c1 = 1.0 - ADAM_B1 ** ADAM_STEP
    c2 = 1.0 - ADAM_B2 ** ADAM_STEP

    def body(w_ref, ga_ref, gb_ref, m_ref, v_ref, g_out, d_out, m_out, v_out):
        g = ga_ref[...] + gb_ref[...]
        mn = ADAM_B1 * m_ref[...] + (1.0 - ADAM_B1) * g
        vn = ADAM_B2 * v_ref[...] + (1.0 - ADAM_B2) * (g * g)
        g_out[...] = g
        m_out[...] = mn
        v_out[...] = vn
        d_out[...] = -ADAM_LR * ((mn / c1) / (jnp.sqrt(vn / c2) + ADAM_EPS) + ADAM_WD * w_ref[...])

    tile = pl.BlockSpec((tr, cols), lambda i: (i, 0))
    shape = jax.ShapeDtypeStruct((r, cols), F32)
    return pl.pallas_call(
        body, name=name, grid=(r // tr,), out_shape=(shape,) * 4, in_specs=[tile] * 5, out_specs=(tile,) * 4,
        compiler_params=_params("arbitrary"),
    )(w, ga, gb, m, v)


PACK_COLS = 512


def _to_rows(a):
    flat = a.reshape(-1)
    pad = (-flat.shape[0]) % PACK_COLS
    return jnp.pad(flat, (0, pad)).reshape(-1, PACK_COLS)


def _pack(arrays, extra_rows=0):
    rows = [_to_rows(a) for a in arrays]
    n = sum(r.shape[0] for r in rows) + extra_rows
    pad = (-n) % 8
    return jnp.concatenate(rows + [jnp.zeros((extra_rows + pad, PACK_COLS), F32)], axis=0)


def _unpack(pack, like):
    out, at = [], 0
    for a in like:
        n = -(-a.size // PACK_COLS)
        out.append(pack[at:at + n].reshape(-1)[:a.size].reshape(a.shape))
        at += n
    return out


COL_SHARDED = ("ffn1_w_gate", "ffn1_w_up", "w_in", "ffn2_w_gate", "ffn2_w_up", "w2", "a2", "g2")
ROW_SHARDED = ("ffn1_w_down", "ffn2_w_down", "w_out", "w1", "a1", "g1")
CHUNKED = ("ffn1_w_gate", "ffn1_w_up", "ffn1_w_down", "ffn2_w_gate", "ffn2_w_up", "ffn2_w_down")
WEIGHTS = ("ffn1_norm", "ffn1_w_gate", "ffn1_w_up", "ffn1_w_down", "mix_norm", "w_in", "q_norm", "k_norm",
           "mu_r", "mu_k", "mu_v", "mu_w", "mu_a", "mu_g", "w0", "w1", "w2", "a0", "a1", "a2", "g1", "g2",
           "k_k", "k_a", "r_k", "ln_x_w", "ln_x_b", "w_out", "ffn2_norm", "ffn2_w_gate", "ffn2_w_up", "ffn2_w_down")


W_IN_GROUPS = 7


def _full_from_blocks(name, blocks):
    if name in CHUNKED:
        return blocks
    if name in ROW_SHARDED:
        return blocks.reshape(-1, blocks.shape[-1])
    full = blocks.transpose(1, 0, 2).reshape(blocks.shape[1], -1)
    if name == "w_in":
        return full.reshape(full.shape[0], W_IN_GROUPS, -1).transpose(1, 0, 2)
    return full


def _blocks_from_full(name, full):
    if name in CHUNKED:
        return full
    if name in ROW_SHARDED:
        return full.reshape(N_SHARDS, -1, full.shape[-1])
    if name == "w_in":
        full = full.transpose(1, 0, 2).reshape(full.shape[1], -1)
    return full.reshape(full.shape[0], N_SHARDS, -1).transpose(1, 0, 2)


FFN1_GROUP = ("ffn1_w_gate", "ffn1_w_up", "ffn1_w_down")
MIX_GROUP = ("w_in",) + RWKV_MAT
OUT_GROUP = ("w_out", "ffn2_w_gate", "ffn2_w_up", "ffn2_w_down")
FFN2_GROUP = OUT_GROUP[1:]
LATE_GROUP = ("w_in", "w_out") + RWKV_MAT


class _Exchange:
    def __init__(self, given):
        self.given = given
        first = self._gather_start(FFN1_GROUP, jnp.zeros(DEP_SHAPE, F32), "gather_ffn1_start")
        self.mix = self._gather_start(MIX_GROUP, first[4], "gather_mix_start")
        self.out = self._gather_start(OUT_GROUP, self.mix[4], "gather_out_start")
        self.first_dep = self.out[4]
        self.first_weights = self._full(FFN1_GROUP, _push_wait(first, _gather_views, (self.first_dep,), "gather_ffn1_wait"))
        self.parts, self.recv = {}, {}

    def _shards(self, names):
        return [self.given[n][0].astype(BF16) for n in names]

    @staticmethod
    def _full(names, blocks):
        out = {}
        for n, b in zip(names, blocks):
            full = _full_from_blocks(n, b)
            out[n] = full.astype(F32) if n in RWKV_MAT else full
        return out

    def _gather_start(self, names, after, name):
        shards = self._shards(names)
        lands = [jax.ShapeDtypeStruct((N_SHARDS,) + s.shape, s.dtype) for s in shards]
        return _push_start(shards, lands, _gather_views, True, after, name)

    def mix_weights(self, after):
        return self._full(MIX_GROUP, _push_wait(self.mix, _gather_views, after, "gather_mix_wait"))

    def out_weights(self, after):
        return self._full(OUT_GROUP, _push_wait(self.out, _gather_views, after, "gather_out_wait"))

    def _scatter_start(self, grads, name):
        names = tuple(grads)
        parts = [_blocks_from_full(n, grads[n]) for n in names]
        self.parts.update(zip(names, parts))
        lands = [jax.ShapeDtypeStruct((3,) + p.shape[1:], BF16) for p in parts]
        return _push_start([p.astype(BF16) for p in parts], lands, _scatter_views, False, parts[0], name)

    def send_ffn2(self, grads):
        self.ffn2 = self._scatter_start(grads, "scatter_ffn2_start")
        return self.ffn2[4]

    def send_mix(self, grads, after):
        self.recv.update(zip(FFN2_GROUP, _push_wait(self.ffn2, _scatter_views, after, "scatter_ffn2_wait")))
        self.late = self._scatter_start(grads, "scatter_late_start")
        return self.late[4]

    def send_ffn1(self, grads):
        self.ffn1 = self._scatter_start(grads, "scatter_ffn1_start")
        return self.ffn1[4]

    def late_received(self, after):
        self.recv.update(zip(LATE_GROUP, _push_wait(self.late, _scatter_views, after, "scatter_late_wait")))

    def ffn1_received(self, after):
        self.recv.update(zip(FFN1_GROUP, _push_wait(self.ffn1, _scatter_views, after, "scatter_ffn1_wait")))


def kernel(
        x, ffn1_norm, ffn1_w_gate, ffn1_w_up, ffn1_w_down, mix_norm, w_in, q_norm, k_norm, mu_r, mu_k, mu_v, mu_w,
        mu_a, mu_g, w0, w1, w2, a0, a1, a2, g1, g2, k_k, k_a, r_k, ln_x_w, ln_x_b, w_out, ffn2_norm, ffn2_w_gate,
        ffn2_w_up, ffn2_w_down, loss_target, m_ffn1_norm, m_ffn1_w_gate, m_ffn1_w_up, m_ffn1_w_down, m_mix_norm,
        m_w_in, m_q_norm, m_k_norm, m_mu_r, m_mu_k, m_mu_v, m_mu_w, m_mu_a, m_mu_g, m_w0, m_w1, m_w2, m_a0, m_a1,
        m_a2, m_g1, m_g2, m_k_k, m_k_a, m_r_k, m_ln_x_w, m_ln_x_b, m_w_out, m_ffn2_norm, m_ffn2_w_gate, m_ffn2_w_up,
        m_ffn2_w_down, v_ffn1_norm, v_ffn1_w_gate, v_ffn1_w_up, v_ffn1_w_down, v_mix_norm, v_w_in, v_q_norm, v_k_norm,
        v_mu_r, v_mu_k, v_mu_v, v_mu_w, v_mu_a, v_mu_g, v_w0, v_w1, v_w2, v_a0, v_a1, v_a2, v_g1, v_g2, v_k_k, v_k_a,
        v_r_k, v_ln_x_w, v_ln_x_b, v_w_out, v_ffn2_norm, v_ffn2_w_gate, v_ffn2_w_up, v_ffn2_w_down):
    given = dict(locals())
    sharded = COL_SHARDED + ROW_SHARDED
    sharded = tuple(n for n in WEIGHTS if n in sharded)
    small = tuple(n for n in WEIGHTS if n not in sharded)

    ex = _Exchange(given)
    w = {n: given[n] for n in small}
    w.update(ex.first_weights)
    loss, dx, g = _local_step(x[0], loss_target[0], w, ex)
    dep = ex.send_ffn1({n: g[n] for n in FFN1_GROUP})

    me = (2 * lax.axis_index("x") + lax.axis_index("y")).astype(jnp.int32).reshape(1)
    out = {}

    def settle(names, dep, tag):
        mine = []
        for n in names:
            p, rv = ex.parts[n], ex.recv[n]
            p2 = p.reshape(N_SHARDS, -1, p.shape[-1])
            mine.append(_reduce_own(me, p2, rv.reshape(3, -1, rv.shape[-1]), dep, f"reduce_{n}"))
        theirs = _sibling_swap(mine, f"sibling_swap_{tag}")
        for n, a, b in zip(names, mine, theirs):
            shape = given[n].shape
            two_d = (-1, shape[-1])
            res = _adamw(given[n].reshape(two_d), a, b, given["m_" + n].reshape(two_d), given["v_" + n].reshape(two_d), f"adamw_{n}")
            out[n] = [r.reshape(shape) for r in res]
        return out[names[-1]][1]

    ex.late_received((dep,))
    last = settle(tuple(n for n in sharded if n not in FFN1_GROUP), dep, "rest")

    gpack = _pack([g[n] for n in small], extra_rows=1)
    n_rows = sum(-(-given[n].size // PACK_COLS) for n in small)
    gpack = gpack.at[n_rows, :loss.shape[1]].set(loss[0])
    gsum = _allreduce_small(gpack)
    res = _adamw(_pack([given[n] for n in small], 1), gsum, jnp.zeros_like(gsum), _pack([given["m_" + n] for n in small], 1),
                 _pack([given["v_" + n] for n in small], 1), "adamw_small")
    like = [given[n] for n in small]
    for j, r in enumerate(res):
        for n, a in zip(small, _unpack(r, like)):
            out.setdefault(n, [None] * 4)[j] = a
    total_loss = gsum[n_rows, 0]

    ex.ffn1_received((last, res[1]))
    settle(FFN1_GROUP, jnp.zeros(DEP_SHAPE, F32), "ffn1")
    return (total_loss, dx[None], *[out[n][0] for n in WEIGHTS], *[out[n][1] for n in WEIGHTS],
            *[out[n][2] for n in WEIGHTS], *[out[n][3] for n in WEIGHTS])
```

```python
import functools

import jax
import jax.numpy as jnp
from jax import lax
from jax.experimental import pallas as pl
from jax.experimental.pallas import tpu as pltpu

F32 = jnp.float32
BF16 = jnp.bfloat16
MESH = pl.DeviceIdType.MESH

RMS_EPS = 1e-6
GN_EPS = 64e-5
NEG_INF = -1e30
FFN_RESIDUAL = 0.5
HEAD_DIM = 64
ATT_BLOCK = 128
DILATIONS = (1, 4, 16)
SCAN_CHUNK = 64
TOKEN_TILE = 256

ADAM_LR = 0.001
ADAM_B1 = 0.9
ADAM_B2 = 0.999
ADAM_EPS = 1e-08
ADAM_WD = 0.01
ADAM_STEP = 10

VMEM_FULL = pl.BlockSpec(memory_space=pltpu.VMEM)
ANY = pl.BlockSpec(memory_space=pl.ANY)


VMEM_LIMIT = 56 * 1024 * 1024


def _params(*sem):
    return pltpu.CompilerParams(dimension_semantics=sem, vmem_limit_bytes=VMEM_LIMIT)


def _dot(a, b, dims):
    return lax.dot_general(a.astype(BF16), b.astype(BF16), (dims, ((), ())), preferred_element_type=F32)


def _dot_nn(a, b):
    return _dot(a, b, ((1,), (0,)))


def _dot_nt(a, b):
    return _dot(a, b, ((1,), (1,)))


def _dot_tn(a, b):
    return _dot(a, b, ((0,), (0,)))


@jax.custom_vjp
def _mm(a, b):
    return _dot_nn(a, b)


def _mm_fwd(a, b):
    return _dot_nn(a, b), (a, b)


def _mm_bwd(res, g):
    a, b = res
    return _dot_nt(g, b).astype(a.dtype), _dot_tn(a, g).astype(b.dtype)


_mm.defvjp(_mm_fwd, _mm_bwd)


def _bdot(a, b, ca, cb):
    return lax.dot_general(a.astype(BF16), b.astype(BF16), (((ca,), (cb,)), ((0,), (0,))), preferred_element_type=F32)


@jax.custom_vjp
def _bmm_nt(a, b):
    return _bdot(a, b, 2, 2)


def _bmm_nt_fwd(a, b):
    return _bdot(a, b, 2, 2), (a, b)


def _bmm_nt_bwd(res, g):
    a, b = res
    return _bdot(g, b, 2, 1), _bdot(g, a, 1, 1)


_bmm_nt.defvjp(_bmm_nt_fwd, _bmm_nt_bwd)


@jax.custom_vjp
def _bmm_nn(a, b):
    return _bdot(a, b, 2, 1)


def _bmm_nn_fwd(a, b):
    return _bdot(a, b, 2, 1), (a, b)


def _bmm_nn_bwd(res, g):
    a, b = res
    return _bdot(g, b, 2, 2), _bdot(a, g, 1, 1)


_bmm_nn.defvjp(_bmm_nn_fwd, _bmm_nn_bwd)


@jax.custom_vjp
def _bmm_tn(a, b):
    return _bdot(a, b, 1, 1)


def _bmm_tn_fwd(a, b):
    return _bdot(a, b, 1, 1), (a, b)


def _bmm_tn_bwd(res, g):
    a, b = res
    return _bdot(b, g, 2, 2), _bdot(a, g, 2, 1)


_bmm_tn.defvjp(_bmm_tn_fwd, _bmm_tn_bwd)


def _hdot(a, b, ca, cb):
    return lax.dot_general(a, b, (((ca,), (cb,)), ((0,), (0,))), precision=lax.Precision.HIGH, preferred_element_type=F32)


def _sigmoid(x):
    return 1.0 / (1.0 + jnp.exp(-x))


def _rms(x):
    return lax.rsqrt(jnp.mean(x * x, axis=-1, keepdims=True) + RMS_EPS)


def _ffn_fwd(x, norm, wg, wu, wd, dep, name):
    t, d = x.shape
    nc = wg.shape[0]
    tm = TOKEN_TILE

    def body(x_ref, n_ref, wg_ref, wu_ref, wd_ref, dep_ref, o_ref):
        xv = x_ref[...]
        h = (xv * _rms(xv) * n_ref[...]).astype(BF16)
        acc = jnp.zeros((tm, d), F32)
        for c in range(nc):
            g = jnp.dot(h, wg_ref[c], preferred_element_type=F32)
            u = jnp.dot(h, wu_ref[c], preferred_element_type=F32)
            a = (g * _sigmoid(g) * u).astype(BF16)
            acc = acc + jnp.dot(a, wd_ref[c], preferred_element_type=F32)
        o_ref[...] = xv + FFN_RESIDUAL * acc

    tile = pl.BlockSpec((tm, d), lambda i: (i, 0))
    return pl.pallas_call(
        body, name=name, grid=(t // tm,), out_shape=jax.ShapeDtypeStruct((t, d), F32),
        in_specs=[tile, pl.BlockSpec((1, d), lambda i: (0, 0)), VMEM_FULL, VMEM_FULL, VMEM_FULL, ANY],
        out_specs=tile, compiler_params=_params("arbitrary"),
    )(x, norm, wg, wu, wd, dep)


def _rmsnorm_bwd(xv, gain, dh):
    rs = _rms(xv)
    xn = xv * rs
    dxn = dh * gain
    dx = rs * (dxn - xn * jnp.mean(dxn * xn, axis=-1, keepdims=True))
    return dx, jnp.sum(dh * xn, axis=0, keepdims=True)


def _ffn_bwd(x, norm, wg, wu, wd, dy, dep, name):
    t, d = x.shape
    nc, _, fc = wg.shape
    tm = TOKEN_TILE
    nt = t // tm

    def body(x_ref, n_ref, wg_ref, wu_ref, wd_ref, dy_ref, dep_ref, dx_ref, dn_ref, dwg_ref, dwu_ref, dwd_ref, dh_ref):
        c, i = pl.program_id(0), pl.program_id(1)
        rows = pl.ds(pl.multiple_of(i * tm, tm), tm)
        xv = x_ref[...]
        gain = n_ref[...]
        h = (xv * _rms(xv) * gain).astype(BF16)
        dy = dy_ref[...]
        dyb = (FFN_RESIDUAL * dy).astype(BF16)
        g = jnp.dot(h, wg_ref[0], preferred_element_type=F32)
        u = jnp.dot(h, wu_ref[0], preferred_element_type=F32)
        sg = _sigmoid(g)
        s = g * sg
        a = (s * u).astype(BF16)
        da = _dot_nt(dyb, wd_ref[0])
        dub = (da * s).astype(BF16)
        dgb = (da * u * (sg * (1.0 + g * (1.0 - sg)))).astype(BF16)
        dwd_c = _dot_tn(a, dyb)
        dwg_c = _dot_tn(h, dgb)
        dwu_c = _dot_tn(h, dub)
        dh_c = _dot_nt(dgb, wg_ref[0]) + _dot_nt(dub, wu_ref[0])

        @pl.when(i == 0)
        def _():
            dwd_ref[0] = dwd_c
            dwg_ref[0] = dwg_c
            dwu_ref[0] = dwu_c

        @pl.when(i > 0)
        def _():
            dwd_ref[0] += dwd_c
            dwg_ref[0] += dwg_c
            dwu_ref[0] += dwu_c

        @pl.when(c == 0)
        def _():
            dh_ref[rows, :] = dh_c

        @pl.when(c > 0)
        def _():
            dh_ref[rows, :] += dh_c

        @pl.when(c == nc - 1)
        def _():
            dx, dn = _rmsnorm_bwd(xv, gain, dh_ref[rows, :])
            dx_ref[...] = dx + dy

            @pl.when(i == 0)
            def _():
                dn_ref[...] = dn

            @pl.when(i > 0)
            def _():
                dn_ref[...] += dn

    tile = pl.BlockSpec((tm, d), lambda c, i: (i, 0))
    row = pl.BlockSpec((1, d), lambda c, i: (0, 0))
    wcol = pl.BlockSpec((1, d, fc), lambda c, i: (c, 0, 0))
    wrow = pl.BlockSpec((1, fc, d), lambda c, i: (c, 0, 0))
    last = pl.BlockSpec((tm, d), lambda c, i: (jnp.where(c == nc - 1, i, 0), 0))
    return pl.pallas_call(
        body, name=name, grid=(nc, nt),
        out_shape=(jax.ShapeDtypeStruct((t, d), F32), jax.ShapeDtypeStruct((1, d), F32),
                   jax.ShapeDtypeStruct(wg.shape, F32), jax.ShapeDtypeStruct(wu.shape, F32),
                   jax.ShapeDtypeStruct(wd.shape, F32)),
        in_specs=[tile, row, wcol, wcol, wrow, tile, ANY],
        out_specs=(last, row, wcol, wcol, wrow),
        scratch_shapes=[pltpu.VMEM((t, d), F32)],
        compiler_params=_params("arbitrary", "arbitrary"),
    )(x, norm, wg, wu, wd, dy, dep)


def _store_heads(ref, v):
    for h in range(ref.shape[0]):
        ref[h] = v[:, h * HEAD_DIM:(h + 1) * HEAD_DIM]


def _load_heads(ref):
    return jnp.concatenate([ref[h] for h in range(ref.shape[0])], axis=-1)


N_HEAD_GROUPS = 3


def _proj_fwd(x, norm, w):
    t, d = x.shape
    ng, _, c = w.shape
    nh = c // HEAD_DIM
    tm = TOKEN_TILE

    def body(x_ref, n_ref, w_ref, q_ref, k_ref, v_ref, cur_ref):
        xv = x_ref[...]
        h = (xv * _rms(xv) * n_ref[...]).astype(BF16)
        for m, ref in enumerate((q_ref, k_ref, v_ref)):
            _store_heads(ref, jnp.dot(h, w_ref[m], preferred_element_type=F32))
        for m in range(N_HEAD_GROUPS, ng):
            j = m - N_HEAD_GROUPS
            cur_ref[:, j * c:(j + 1) * c] = jnp.dot(h, w_ref[m], preferred_element_type=F32)

    heads = pl.BlockSpec((nh, tm, HEAD_DIM), lambda i: (0, i, 0))
    hshape = jax.ShapeDtypeStruct((nh, t, HEAD_DIM), F32)
    wide = (ng - N_HEAD_GROUPS) * c
    return pl.pallas_call(
        body, name="proj_fwd", grid=(t // tm,),
        out_shape=(hshape, hshape, hshape, jax.ShapeDtypeStruct((t, wide), F32)),
        in_specs=[pl.BlockSpec((tm, d), lambda i: (i, 0)), pl.BlockSpec((1, d), lambda i: (0, 0)), VMEM_FULL],
        out_specs=(heads, heads, heads, pl.BlockSpec((tm, wide), lambda i: (i, 0))),
        compiler_params=_params("arbitrary"),
    )(x, norm, w)


def _proj_bwd(x, norm, w, dq, dk, dv, dcur, dres):
    t, d = x.shape
    ng, _, c = w.shape
    nh = c // HEAD_DIM
    tm = TOKEN_TILE

    def body(x_ref, n_ref, w_ref, dq_ref, dk_ref, dv_ref, dcur_ref, dres_ref, dx_ref, dn_ref, dw_ref):
        i = pl.program_id(0)

        @pl.when(i == 0)
        def _():
            dw_ref[...] = jnp.zeros_like(dw_ref)
            dn_ref[...] = jnp.zeros_like(dn_ref)

        xv = x_ref[...]
        gain = n_ref[...]
        h = (xv * _rms(xv) * gain).astype(BF16)
        dh = jnp.zeros((tm, d), F32)
        for m in range(ng):
            j = m - N_HEAD_GROUPS
            dp = _load_heads((dq_ref, dk_ref, dv_ref)[m]) if j < 0 else dcur_ref[:, j * c:(j + 1) * c]
            dp = dp.astype(BF16)
            dw_ref[m] += _dot_tn(h, dp)
            dh = dh + _dot_nt(dp, w_ref[m])
        dx, dn = _rmsnorm_bwd(xv, gain, dh)
        dx_ref[...] = dx + dres_ref[...]
        dn_ref[...] += dn

    tile = pl.BlockSpec((tm, d), lambda i: (i, 0))
    row = pl.BlockSpec((1, d), lambda i: (0, 0))
    heads = pl.BlockSpec((nh, tm, HEAD_DIM), lambda i: (0, i, 0))
    wide = (ng - N_HEAD_GROUPS) * c
    return pl.pallas_call(
        body, name="proj_bwd", grid=(t // tm,),
        out_shape=(jax.ShapeDtypeStruct((t, d), F32), jax.ShapeDtypeStruct((1, d), F32),
                   jax.ShapeDtypeStruct(w.shape, F32)),
        in_specs=[tile, row, VMEM_FULL, heads, heads, heads, pl.BlockSpec((tm, wide), lambda i: (i, 0)), tile],
        out_specs=(tile, row, VMEM_FULL),
        compiler_params=_params("arbitrary"),
    )(x, norm, w, dq, dk, dv, dcur, dres)


def _mixout_fwd(x, att, opg, gate, w):
    t, d = x.shape
    nh = att.shape[0]
    half = gate.shape[1]
    tm = TOKEN_TILE

    def body(x_ref, att_ref, opg_ref, g_ref, w_ref, o_ref):
        mix = jnp.concatenate([_load_heads(att_ref), _load_heads(opg_ref) * g_ref[...]], axis=-1).astype(BF16)
        o_ref[...] = x_ref[...] + jnp.dot(mix, w_ref[...], preferred_element_type=F32)

    tile = pl.BlockSpec((tm, d), lambda i: (i, 0))
    htile = pl.BlockSpec((tm, half), lambda i: (i, 0))
    heads = pl.BlockSpec((nh, tm, HEAD_DIM), lambda i: (0, i, 0))
    return pl.pallas_call(
        body, name="mixout_fwd", grid=(t // tm,), out_shape=jax.ShapeDtypeStruct((t, d), F32),
        in_specs=[tile, heads, heads, htile, VMEM_FULL], out_specs=tile, compiler_params=_params("arbitrary"),
    )(x, att, opg, gate, w)


def _mixout_bwd(att, opg, gate, w, dy, dep):
    nh, t, _ = att.shape
    half = gate.shape[1]
    d = dy.shape[1]
    tm = TOKEN_TILE

    def body(att_ref, opg_ref, g_ref, w_ref, dy_ref, dep_ref, datt_ref, dopg_ref, dg_ref, dw_ref):
        i = pl.program_id(0)
        opg_v, g_v = _load_heads(opg_ref), g_ref[...]
        mix = jnp.concatenate([_load_heads(att_ref), opg_v * g_v], axis=-1).astype(BF16)
        dyb = dy_ref[...].astype(BF16)
        dmix = _dot_nt(dyb, w_ref[...])
        dw = _dot_tn(mix, dyb)
        _store_heads(datt_ref, dmix[:, :half])
        drw = dmix[:, half:]
        _store_heads(dopg_ref, drw * g_v)
        dg_ref[...] = drw * opg_v

        @pl.when(i == 0)
        def _():
            dw_ref[...] = dw

        @pl.when(i > 0)
        def _():
            dw_ref[...] += dw

    tile = pl.BlockSpec((tm, d), lambda i: (i, 0))
    htile = pl.BlockSpec((tm, half), lambda i: (i, 0))
    heads = pl.BlockSpec((nh, tm, HEAD_DIM), lambda i: (0, i, 0))
    hshape = jax.ShapeDtypeStruct((nh, t, HEAD_DIM), F32)
    return pl.pallas_call(
        body, name="mixout_bwd", grid=(t // tm,),
        out_shape=(hshape, hshape, jax.ShapeDtypeStruct((t, half), F32), jax.ShapeDtypeStruct(w.shape, F32)),
        in_specs=[heads, heads, htile, VMEM_FULL, tile, ANY],
        out_specs=(heads, heads, htile, pl.BlockSpec(w.shape, lambda i: (0, 0))),
        compiler_params=_params("arbitrary"),
    )(att, opg, gate, w, dy, dep)


def _loss_head(y, target):
    t, d = y.shape
    tm = TOKEN_TILE

    def body(y_ref, t_ref, dy_ref, loss_ref):
        i = pl.program_id(0)
        err = y_ref[...] - t_ref[...]
        dy_ref[...] = err * (1.0 / d)
        part = 0.5 * jnp.sum(jnp.mean(err * err, axis=-1, keepdims=True), axis=0, keepdims=True)

        @pl.when(i == 0)
        def _():
            loss_ref[...] = jnp.zeros_like(loss_ref)

        loss_ref[...] += jnp.broadcast_to(part, loss_ref.shape)

    tile = pl.BlockSpec((tm, d), lambda i: (i, 0))
    return pl.pallas_call(
        body, name="loss_head", grid=(t // tm,),
        out_shape=(jax.ShapeDtypeStruct((t, d), F32), jax.ShapeDtypeStruct((1, 128), F32)),
        in_specs=[tile, tile], out_specs=(tile, pl.BlockSpec((1, 128), lambda i: (0, 0))),
        compiler_params=_params("arbitrary"),
    )(y, target)


def _att_block(q, kc, vc, qn, kn, kp=None, vp=None, has_prev=True):
    blk = q.shape[1]

    def hn(v, gain):
        return v * _rms(v) * gain

    qh = hn(q, qn)
    scale = HEAD_DIM ** -0.5
    qi = lax.broadcasted_iota(jnp.int32, (blk, blk), 0)
    kj = lax.broadcasted_iota(jnp.int32, (blk, blk), 1)
    sc = jnp.where(kj <= qi, _bmm_nt(qh, hn(kc, kn)) * scale, NEG_INF)
    top = jnp.max(sc, axis=-1, keepdims=True)
    if kp is not None:
        sp = jnp.where((kj >= qi) & has_prev, _bmm_nt(qh, hn(kp, kn)) * scale, NEG_INF)
        top = jnp.maximum(top, jnp.max(sp, axis=-1, keepdims=True))
    m = lax.stop_gradient(top)
    pc = jnp.exp(sc - m)
    den = jnp.sum(pc, axis=-1, keepdims=True)
    acc = _bmm_nn(pc, vc)
    if kp is not None:
        pp = jnp.exp(sp - m)
        den = den + jnp.sum(pp, axis=-1, keepdims=True)
        acc = acc + _bmm_nn(pp, vp)
    o = acc / den
    return o, jnp.broadcast_to(m + jnp.log(den), o.shape)


def _class_rows(n, dil):
    base = n * (ATT_BLOCK * dil)
    if dil == 1:
        return [pl.ds(pl.multiple_of(base, ATT_BLOCK), ATT_BLOCK)]
    return [pl.ds(base + r, ATT_BLOCK, stride=dil) for r in range(dil)]


def _take(ref, rows):
    return jnp.stack([ref[0, r, :] for r in rows])


def _put(ref, rows, val):
    for g, r in enumerate(rows):
        ref[0, r, :] = val[g]


def _put_add(ref, rows, val):
    for g, r in enumerate(rows):
        ref[0, r, :] += val[g]


def _merge_fn(o1, o2, o3, l1, l2, l3):
    m = lax.stop_gradient(jnp.maximum(jnp.maximum(l1, l2), l3))
    e1, e2, e3 = jnp.exp(l1 - m), jnp.exp(l2 - m), jnp.exp(l3 - m)
    return (e1 * o1 + e2 * o2 + e3 * o3) / (e1 + e2 + e3)


def _att_head_specs(t):
    head = pl.BlockSpec((1, t, HEAD_DIM), lambda h: (h, 0, 0))
    gain = pl.BlockSpec((1, 1, HEAD_DIM), lambda h: (0, 0, 0))
    return head, gain


def _for_each_block(t, block):
    carry = None
    for p, dil in enumerate(DILATIONS):
        nb = t // (ATT_BLOCK * dil)
        if dil == 1:
            first = block(p, dil, 0, None, True, carry)
            carry = lax.fori_loop(1, nb, lambda n, c, p=p, dil=dil: block(p, dil, n, n - 1, True, c), first)
        else:
            for n in range(nb):
                carry = block(p, dil, n, n - 1 if n else None, True, carry)
    return carry


def _att_fwd(q, k, v, qn, kn):
    nh, t, dh = q.shape
    head, gain = _att_head_specs(t)

    def body(q_ref, k_ref, v_ref, qn_ref, kn_ref, att_ref, *saved):
        o_refs, l_refs = saved[:3], saved[3:]
        gq, gk = qn_ref[...], kn_ref[...]

        def block(p, dil, n, prev_n, has_prev, carry):
            rows = _class_rows(n, dil)
            args = [_take(q_ref, rows), _take(k_ref, rows), _take(v_ref, rows), gq, gk]
            if prev_n is not None:
                before = _class_rows(prev_n, dil)
                args += [_take(k_ref, before), _take(v_ref, before), has_prev]
            o, lse = _att_block(*args)
            _put(o_refs[p], rows, o)
            _put(l_refs[p], rows, lse)
            return 0

        _for_each_block(t, block)

        def merge(j, carry):
            rows = pl.ds(pl.multiple_of(j * ATT_BLOCK, ATT_BLOCK), ATT_BLOCK)
            att_ref[0, rows, :] = _merge_fn(*[r[0, rows, :] for r in saved])
            return carry

        lax.fori_loop(0, t // ATT_BLOCK, merge, 0)

    return pl.pallas_call(
        body, name="att_fwd", grid=(nh,), out_shape=(jax.ShapeDtypeStruct(q.shape, F32),) * 7,
        in_specs=[head, head, head, gain, gain], out_specs=(head,) * 7, compiler_params=_params("arbitrary"),
    )(q, k, v, qn, kn)


def _att_bwd(q, k, v, qn, kn, saved, datt):
    nh, t, dh = q.shape
    head, gain = _att_head_specs(t)

    def body(q_ref, k_ref, v_ref, qn_ref, kn_ref, o1, o2, o3, l1, l2, l3, datt_ref,
             dq_ref, dk_ref, dv_ref, dqn_ref, dkn_ref):
        for ref in (dq_ref, dk_ref, dv_ref):
            ref[...] = jnp.zeros_like(ref)

        @pl.when(pl.program_id(0) == 0)
        def _():
            dqn_ref[...] = jnp.zeros_like(dqn_ref)
            dkn_ref[...] = jnp.zeros_like(dkn_ref)

        gq, gk = qn_ref[...], kn_ref[...]

        def block(p, dil, n, prev_n, has_prev, carry):
            rows = _class_rows(n, dil)
            _, merge_vjp = jax.vjp(_merge_fn, *[_take(r, rows) for r in (o1, o2, o3, l1, l2, l3)])
            cts = merge_vjp(_take(datt_ref, rows))
            args = [_take(q_ref, rows), _take(k_ref, rows), _take(v_ref, rows), gq, gk]
            if prev_n is not None:
                before = _class_rows(prev_n, dil)
                args += [_take(k_ref, before), _take(v_ref, before)]
            _, block_vjp = jax.vjp(functools.partial(_att_block, has_prev=has_prev), *args)
            grads = block_vjp((cts[p], cts[3 + p]))
            _put_add(dq_ref, rows, grads[0])
            _put_add(dk_ref, rows, grads[1])
            _put_add(dv_ref, rows, grads[2])
            if prev_n is not None:
                _put_add(dk_ref, before, grads[5])
                _put_add(dv_ref, before, grads[6])
            if carry is None:
                return grads[3], grads[4]
            return carry[0] + grads[3], carry[1] + grads[4]

        dgq, dgk = _for_each_block(t, block)
        dqn_ref[...] += dgq
        dkn_ref[...] += dgk

    hshape = jax.ShapeDtypeStruct(q.shape, F32)
    gshape = jax.ShapeDtypeStruct((1, 1, dh), F32)
    return pl.pallas_call(
        body, name="att_bwd", grid=(nh,), out_shape=(hshape, hshape, hshape, gshape, gshape),
        in_specs=[head, head, head, gain, gain] + [head] * 7, out_specs=(head, head, head, gain, gain),
        compiler_params=_params("arbitrary"),
    )(q, k, v, qn, kn, *saved, datt)


RWKV_VEC = ("mu_r", "mu_k", "mu_v", "mu_w", "mu_a", "mu_g", "w0", "a0", "k_k", "k_a")
RWKV_MAT = ("w1", "w2", "a1", "a2", "g1", "g2")


def _rwkv_pre_fn(cur, prev, vec, w1, w2, a1, a2, g1, g2):
    c = cur.shape[1] // 4
    mu_r, mu_k, mu_v, mu_w, mu_a, mu_g, w0, a0, k_k, k_a = (vec[j:j + 1] for j in range(10))

    def lerp(j, mu):
        xc, xp = cur[:, j * c:(j + 1) * c], prev[:, j * c:(j + 1) * c]
        return xc + (xp - xc) * mu

    r, k, v = lerp(0, mu_r), lerp(1, mu_k), lerp(2, mu_v)
    cw, ca, cg = lerp(3, mu_w), lerp(3, mu_a), lerp(3, mu_g)
    z = w0 + _mm(jnp.tanh(_mm(cw, w1)), w2)
    w_log = jnp.minimum(z, 0.0) - jnp.log(1.0 + jnp.exp(-jnp.abs(z))) - 0.5
    lw = -jnp.exp(w_log)
    a = _sigmoid(a0 + _mm(_mm(ca, a1), a2))
    gate = _mm(_sigmoid(_mm(cg, g1)), g2)
    kkraw = k * k_k
    kmod = k * (1.0 + (a - 1.0) * k_a)
    return r, lw, kmod, v, kkraw, a, gate


HALO_ROWS = 8


def _rwkv_pre_specs(c, mats, tile_of):
    tm = TOKEN_TILE
    nh = c // HEAD_DIM
    wide = pl.BlockSpec((tm, 4 * c), lambda j: (tile_of(j), 0))
    halo = pl.BlockSpec((HALO_ROWS, 4 * c), lambda j: (jnp.maximum(tile_of(j) * (tm // HALO_ROWS) - 1, 0), 0))
    one = pl.BlockSpec((tm, c), lambda j: (tile_of(j), 0))
    heads = pl.BlockSpec((nh, tm, HEAD_DIM), lambda j: (0, tile_of(j), 0))
    vec = pl.BlockSpec((10, c), lambda j: (0, 0))
    mspecs = [pl.BlockSpec(m.shape, lambda j: (0, 0)) for m in mats]
    return wide, halo, one, heads, vec, mspecs


def _previous_rows(cur, halo, tile):
    first = jnp.where(tile > 0, halo[HALO_ROWS - 1:HALO_ROWS], 0.0)
    rows = lax.broadcasted_iota(jnp.int32, cur.shape, 0)
    return jnp.where(rows == 0, first, pltpu.roll(cur, 1, axis=0))


def _rwkv_pre_fwd(cur, vec, mats):
    t, c4 = cur.shape
    c = c4 // 4
    wide, halo, one, heads, vspec, mspecs = _rwkv_pre_specs(c, mats, lambda j: j)

    def body(cur_ref, halo_ref, vec_ref, *rest):
        mrefs, outs = rest[:6], rest[6:]
        cur_v = cur_ref[...]
        prev = _previous_rows(cur_v, halo_ref[...], pl.program_id(0))
        vals = _rwkv_pre_fn(cur_v, prev, vec_ref[...], *(m[...] for m in mrefs))
        for ref, val in zip(outs[:6], vals[:6]):
            _store_heads(ref, val)
        outs[6][...] = vals[6]

    hshape = jax.ShapeDtypeStruct((c // HEAD_DIM, t, HEAD_DIM), F32)
    return pl.pallas_call(
        body, name="rwkv_pre_fwd", grid=(t // TOKEN_TILE,), out_shape=(hshape,) * 6 + (jax.ShapeDtypeStruct((t, c), F32),),
        in_specs=[wide, halo, vspec] + mspecs, out_specs=(heads,) * 6 + (one,), compiler_params=_params("arbitrary"),
    )(cur, cur, vec, *mats)


def _rwkv_pre_bwd(cur, vec, mats, cts, dgate):
    t, c4 = cur.shape
    c = c4 // 4
    tm = TOKEN_TILE
    nt = t // tm
    wide, halo, one, heads, vspec, mspecs = _rwkv_pre_specs(c, mats, lambda j: nt - 1 - j)

    def body(cur_ref, halo_ref, vec_ref, *rest):
        mrefs, ctrefs, dgate_ref, outs, carry_ref = rest[:6], rest[6:12], rest[12], rest[13:-1], rest[-1]
        j = pl.program_id(0)

        @pl.when(j == 0)
        def _():
            carry_ref[...] = jnp.zeros_like(carry_ref)
            for ref in outs[1:]:
                ref[...] = jnp.zeros_like(ref)

        cur_v = cur_ref[...]
        prev = _previous_rows(cur_v, halo_ref[...], nt - 1 - j)
        _, vjp = jax.vjp(_rwkv_pre_fn, cur_v, prev, vec_ref[...], *(m[...] for m in mrefs))
        grads = vjp(tuple(_load_heads(r) for r in ctrefs) + (dgate_ref[...],))
        dprev = grads[1]
        rows = lax.broadcasted_iota(jnp.int32, dprev.shape, 0)
        outs[0][...] = grads[0] + jnp.where(rows == tm - 1, carry_ref[0:1], pltpu.roll(dprev, tm - 1, axis=0))
        carry_ref[0:1] = dprev[0:1]
        for ref, val in zip(outs[1:], grads[2:]):
            ref[...] += val

    return pl.pallas_call(
        body, name="rwkv_pre_bwd", grid=(nt,),
        out_shape=(jax.ShapeDtypeStruct(cur.shape, F32), jax.ShapeDtypeStruct(vec.shape, F32))
        + tuple(jax.ShapeDtypeStruct(m.shape, F32) for m in mats),
        in_specs=[wide, halo, vspec] + mspecs + [heads] * 6 + [one], out_specs=(wide, vspec) + tuple(mspecs),
        scratch_shapes=[pltpu.VMEM((HALO_ROWS, c4), F32)], compiler_params=_params("arbitrary"),
    )(cur, cur, vec, *mats, *cts, dgate)


def _scan_chunk_fn(h0, r, lw, k, v, kkraw, a, rk, lnw, lnb):
    n = r.shape[1]
    nrm = jnp.sqrt(jnp.sum(kkraw * kkraw, axis=-1, keepdims=True))
    kk = kkraw / jnp.maximum(nrm, 1e-12)
    av, bv = -kk, kk * a
    ti = lax.broadcasted_iota(jnp.int32, (n, n), 0)
    si = lax.broadcasted_iota(jnp.int32, (n, n), 1)
    incl, strict = ti >= si, ti > si
    ones = jnp.broadcast_to(incl.astype(F32)[None], (r.shape[0], n, n))
    cum = _hdot(ones, lw, 2, 1)
    at, rt = av * jnp.exp(cum - lw), r * jnp.exp(cum)
    inv = jnp.exp(-cum)
    bt, kt = bv * inv, k * inv
    lab = jnp.where(strict, _hdot(at, bt, 2, 2), 0.0)
    lak = jnp.where(strict, _hdot(at, kt, 2, 2), 0.0)
    rb = jnp.where(incl, _hdot(rt, bt, 2, 2), 0.0)
    rkm = jnp.where(incl, _hdot(rt, kt, 2, 2), 0.0)
    u = _bmm_nn(at, h0) + _bmm_nn(lak, v)
    p = lab
    m = 1
    while m < n:
        u = u + _bmm_nn(p, u)
        m *= 2
        if m < n:
            p = _bmm_nn(p, p)
    y = _bmm_nn(rt, h0) + _bmm_nn(rb, u) + _bmm_nn(rkm, v)
    last = jnp.exp(jnp.sum(lw, axis=1, keepdims=True))
    h1 = jnp.swapaxes(last, 1, 2) * (h0 + _bmm_tn(bt, u) + _bmm_tn(kt, v))
    mean = jnp.mean(y, axis=-1, keepdims=True)
    yc = y - mean
    var = jnp.mean(yc * yc, axis=-1, keepdims=True)
    yn = yc * lax.rsqrt(var + GN_EPS) * lnw + lnb
    bonus = jnp.sum(r * k * rk, axis=-1, keepdims=True) * v
    return yn + bonus, h1


def _scan_specs(h, t, dh, rev):
    n = SCAN_CHUNK
    nc = t // n
    pos = (lambda c: (0, nc - 1 - c, 0)) if rev else (lambda c: (0, c, 0))
    st = (lambda c: (nc - 1 - c, 0, 0, 0)) if rev else (lambda c: (c, 0, 0, 0))
    seq = pl.BlockSpec((h, n, dh), pos)
    par = pl.BlockSpec((h, 1, dh), lambda c: (0, 0, 0))
    state = pl.BlockSpec((1, h, dh, dh), st)
    return seq, par, state


def _scan_fwd(seqs, pars):
    h, t, dh = seqs[0].shape
    nc = t // SCAN_CHUNK
    seq, par, state = _scan_specs(h, t, dh, False)

    def body(r, lw, k, v, kkraw, a, rk, lnw, lnb, o_ref, st_ref, h_ref):
        @pl.when(pl.program_id(0) == 0)
        def _():
            h_ref[...] = jnp.zeros_like(h_ref)

        h0 = h_ref[...]
        st_ref[0] = h0
        o, h1 = _scan_chunk_fn(h0, r[...], lw[...], k[...], v[...], kkraw[...], a[...], rk[...], lnw[...], lnb[...])
        o_ref[...] = o
        h_ref[...] = h1

    return pl.pallas_call(
        body, name="rwkv_scan_fwd", grid=(nc,),
        out_shape=(jax.ShapeDtypeStruct((h, t, dh), F32), jax.ShapeDtypeStruct((nc, h, dh, dh), F32)),
        in_specs=[seq] * 6 + [par] * 3, out_specs=(seq, state),
        scratch_shapes=[pltpu.VMEM((h, dh, dh), F32)], compiler_params=_params("arbitrary"),
    )(*seqs, *pars)


def _scan_bwd(seqs, pars, states, do):
    h, t, dh = seqs[0].shape
    nc = t // SCAN_CHUNK
    seq, par, state = _scan_specs(h, t, dh, True)

    def body(r, lw, k, v, kkraw, a, rk, lnw, lnb, st_ref, do_ref, *rest):
        douts, dpars, dh_ref = rest[:6], rest[6:9], rest[9]
        first = pl.program_id(0) == 0

        @pl.when(first)
        def _():
            dh_ref[...] = jnp.zeros_like(dh_ref)

        _, vjp = jax.vjp(_scan_chunk_fn, st_ref[0], r[...], lw[...], k[...], v[...], kkraw[...], a[...],
                         rk[...], lnw[...], lnb[...])
        grads = vjp((do_ref[...], dh_ref[...]))
        dh_ref[...] = grads[0]
        for ref, val in zip(douts, grads[1:7]):
            ref[...] = val

        @pl.when(first)
        def _():
            for ref, val in zip(dpars, grads[7:]):
                ref[...] = val

        @pl.when(jnp.logical_not(first))
        def _():
            for ref, val in zip(dpars, grads[7:]):
                ref[...] += val

    sshape = jax.ShapeDtypeStruct((h, t, dh), F32)
    pshape = jax.ShapeDtypeStruct((h, 1, dh), F32)
    return pl.pallas_call(
        body, name="rwkv_scan_bwd", grid=(nc,), out_shape=(sshape,) * 6 + (pshape,) * 3,
        in_specs=[seq] * 6 + [par] * 3 + [state, seq], out_specs=(seq,) * 6 + (par,) * 3,
        scratch_shapes=[pltpu.VMEM((h, dh, dh), F32)], compiler_params=_params("arbitrary"),
    )(*seqs, *pars, states, do)


def _local_step(x, target, w, ex):
    w = dict(w)
    c = w["mu_r"].shape[-1]
    qn, kn = w["q_norm"].reshape(1, 1, HEAD_DIM), w["k_norm"].reshape(1, 1, HEAD_DIM)
    vec = jnp.concatenate([w[n].reshape(1, c) for n in RWKV_VEC], axis=0)
    pars = [w[n].reshape(-1, 1, HEAD_DIM) for n in ("r_k", "ln_x_w", "ln_x_b")]
    no_dep = jnp.zeros(DEP_SHAPE, F32)

    x1 = _ffn_fwd(x, w["ffn1_norm"], w["ffn1_w_gate"], w["ffn1_w_up"], w["ffn1_w_down"], ex.first_dep, "ffn1_fwd")
    w.update(ex.mix_weights((x1,)))
    mats = [w[n] for n in RWKV_MAT]
    q, k, v, cur = _proj_fwd(x1, w["mix_norm"], w["w_in"])
    att, *saved = _att_fwd(q, k, v, qn, kn)
    pre = _rwkv_pre_fwd(cur, vec, mats)
    seqs, gate = pre[:6], pre[6]
    opg, states = _scan_fwd(seqs, pars)
    w.update(ex.out_weights((att, opg)))
    x2 = _mixout_fwd(x1, att, opg, gate, w["w_out"])
    x3 = _ffn_fwd(x2, w["ffn2_norm"], w["ffn2_w_gate"], w["ffn2_w_up"], w["ffn2_w_down"], no_dep, "ffn2_fwd")
    dy, loss = _loss_head(x3, target)

    g = {}
    dx2, g["ffn2_norm"], g["ffn2_w_gate"], g["ffn2_w_up"], g["ffn2_w_down"] = _ffn_bwd(
        x2, w["ffn2_norm"], w["ffn2_w_gate"], w["ffn2_w_up"], w["ffn2_w_down"], dy, no_dep, "ffn2_bwd")
    dep = ex.send_ffn2({n: g[n] for n in ("ffn2_w_gate", "ffn2_w_up", "ffn2_w_down")})
    datt, dopg, dgate, g["w_out"] = _mixout_bwd(att, opg, gate, w["w_out"], dx2, dep)
    dscan = _scan_bwd(seqs, pars, states, dopg)
    for n, d in zip(("r_k", "ln_x_w", "ln_x_b"), dscan[6:]):
        g[n] = d
    dcur, dvec, *dmats = _rwkv_pre_bwd(cur, vec, mats, dscan[:6], dgate)
    for n, d in zip(RWKV_MAT, dmats):
        g[n] = d
    for j, n in enumerate(RWKV_VEC):
        g[n] = dvec[j:j + 1]
    dq, dk, dv, g["q_norm"], g["k_norm"] = _att_bwd(q, k, v, qn, kn, saved, datt)
    dx1, g["mix_norm"], g["w_in"] = _proj_bwd(x1, w["mix_norm"], w["w_in"], dq, dk, dv, dcur, dx2)
    dep = ex.send_mix({n: g[n] for n in ("w_in", "w_out") + RWKV_MAT}, (dx1,))
    dx, g["ffn1_norm"], g["ffn1_w_gate"], g["ffn1_w_up"], g["ffn1_w_down"] = _ffn_bwd(
        x, w["ffn1_norm"], w["ffn1_w_gate"], w["ffn1_w_up"], w["ffn1_w_down"], dx1, dep, "ffn1_bwd")
    return loss, dx, g


N_SHARDS = 4


def _place():
    return lax.axis_index("x"), lax.axis_index("y"), lax.axis_index("c")


def _chip_peers(x, y):
    return [(1 - x, y), (x, 1 - y), (1 - x, 1 - y)]


HBM = pl.BlockSpec(memory_space=pltpu.HBM)
SEM = pl.BlockSpec(memory_space=pltpu.SEMAPHORE)
DEP_SHAPE = (8, 128)


class _Views:
    to_sibling = False


class _GatherViews(_Views):
    @staticmethod
    def send(i, srcs, lands, k, at):
        return srcs[i], lands[i].at[at[3]]

    @staticmethod
    def landing(i, srcs, lands, k, at):
        return srcs[i], lands[i].at[2 * at[4] + at[5]]


class _ScatterViews(_Views):
    @staticmethod
    def send(i, srcs, lands, k, at):
        return srcs[i].at[2 * at[4] + at[5]], lands[i].at[k]

    @staticmethod
    def landing(i, srcs, lands, k, at):
        return srcs[i].at[at[3]], lands[i].at[k]


def _half_rows(ref, slot, half):
    rows = ref.shape[1] // 2
    return ref.at[slot, pl.ds(pl.multiple_of(half * rows, BF16_SUBLANES), rows)]


class _HalfGatherViews(_Views):
    @staticmethod
    def send(i, srcs, lands, k, at):
        rows = srcs[i].shape[0] // 2
        return srcs[i].at[pl.ds(pl.multiple_of(at[2] * rows, BF16_SUBLANES), rows)], _half_rows(lands[i], at[3], at[2])

    @staticmethod
    def landing(i, srcs, lands, k, at):
        rows = srcs[i].shape[0] // 2
        return srcs[i].at[pl.ds(pl.multiple_of(at[2] * rows, BF16_SUBLANES), rows)], _half_rows(lands[i], 2 * at[4] + at[5], at[2])


class _ForwardViews(_Views):
    to_sibling = True

    @staticmethod
    def send(i, srcs, lands, k, at):
        mine = _half_rows(lands[i], 2 * at[4] + at[5], at[2])
        return mine, mine

    @staticmethod
    def landing(i, srcs, lands, k, at):
        theirs = _half_rows(lands[i], 2 * at[4] + at[5], 1 - at[2])
        return theirs, theirs


def _push_start(srcs, lands, views, after, name):
    ns, nl = len(srcs), len(lands)

    def body(*refs):
        src_refs, land_refs = refs[:ns], refs[ns:ns + nl]
        send_sems, recv_sems = refs[ns + nl + 1:ns + nl + 3]
        token = refs[2 * (ns + nl) + 3]
        x, y, c = _place()
        for i in range(nl):
            for k, (px, py) in enumerate(_chip_peers(x, y)):
                src, dst = views.send(i, src_refs, land_refs, k, (x, y, c, 2 * x + y, px, py))
                pltpu.make_async_remote_copy(
                    src_ref=src, dst_ref=dst, send_sem=send_sems.at[3 * i + k], recv_sem=recv_sems.at[3 * i + k],
                    device_id=(x, y, 1 - c) if views.to_sibling else (px, py, c), device_id_type=MESH).start()
        token[...] = jnp.zeros_like(token)

    sems = pltpu.SemaphoreType.DMA((3 * nl,))
    both = [pltpu.with_memory_space_constraint(a, pltpu.HBM) for a in (*srcs, *lands)]
    outs = pl.pallas_call(
        body, name=name,
        out_shape=(sems, sems, *[pltpu.HBM(a.shape, a.dtype) for a in both], jax.ShapeDtypeStruct(DEP_SHAPE, F32)),
        in_specs=[HBM] * (ns + nl) + [ANY], out_specs=(SEM, SEM, *[HBM] * (ns + nl), VMEM_FULL),
        input_output_aliases={i: 2 + i for i in range(ns + nl)},
        compiler_params=pltpu.CompilerParams(has_side_effects=pltpu.SideEffectType.DATAFLOW_SIDE_EFFECTING),
    )(*both, after)
    return outs[0], outs[1], outs[2:2 + ns], outs[2 + ns:2 + ns + nl], outs[2 + ns + nl]


def _push_wait(started, views, after, name):
    send_sems, recv_sems, srcs, lands, _ = started
    ns, nl = len(srcs), len(lands)

    def body(*refs):
        src_refs, land_refs = refs[:ns], refs[ns:ns + nl]
        send_sems, recv_sems = refs[ns + nl:ns + nl + 2]
        x, y, c = _place()
        for i in range(nl):
            for k, (px, py) in enumerate(_chip_peers(x, y)):
                src, dst = views.landing(i, src_refs, land_refs, k, (x, y, c, 2 * x + y, px, py))
                landing = pltpu.make_async_remote_copy(
                    src_ref=src, dst_ref=dst, send_sem=send_sems.at[3 * i + k], recv_sem=recv_sems.at[3 * i + k],
                    device_id=(x, y, 1 - c) if views.to_sibling else (px, py, c), device_id_type=MESH)
                landing.wait_send()
                landing.wait_recv()

    outs = pl.pallas_call(
        body, name=name,
        out_shape=tuple(pltpu.HBM(a.shape, a.dtype) for a in (*srcs, *lands)),
        in_specs=[HBM] * (ns + nl) + [SEM, SEM] + [ANY] * len(after), out_specs=(HBM,) * (ns + nl),
        input_output_aliases={i: i for i in range(ns + nl)},
        compiler_params=pltpu.CompilerParams(has_side_effects=pltpu.SideEffectType.DATAFLOW_SIDE_EFFECTING),
    )(*srcs, *lands, send_sems, recv_sems, *after)
    return outs[ns:]


def _empty_lands(shards, slots, own_slot):
    lands = [lax.empty((slots,) + s.shape, s.dtype) for s in shards]
    if own_slot:
        me = 2 * lax.axis_index("x") + lax.axis_index("y")
        lands = [lax.dynamic_update_index_in_dim(z, s, me, 0) for z, s in zip(lands, shards)]
    return lands


def _sibling_swap(arrays, name):
    n = len(arrays)

    def body(*refs):
        ins, outs = refs[:n], refs[n:2 * n]
        send_sems, recv_sems = refs[2 * n:]
        x, y, c = _place()
        copies = []
        for i in range(n):
            cp = pltpu.make_async_remote_copy(
                src_ref=ins[i], dst_ref=outs[i], send_sem=send_sems.at[i], recv_sem=recv_sems.at[i],
                device_id=(x, y, 1 - c), device_id_type=MESH)
            cp.start()
            copies.append(cp)
        for cp in copies:
            cp.wait()

    return pl.pallas_call(
        body, name=name,
        out_shape=tuple(jax.ShapeDtypeStruct(a.shape, a.dtype) for a in arrays),
        in_specs=[ANY] * n, out_specs=(ANY,) * n,
        scratch_shapes=[pltpu.SemaphoreType.DMA((n,)), pltpu.SemaphoreType.DMA((n,))],
    )(*arrays)


N_DEV = 8


def _allreduce_small(pack):
    def body(in_ref, out_ref, buf, send_sems, recv_sems):
        x, y, c = _place()
        me = 4 * x + 2 * y + c
        buf[me] = in_ref[...]

        def copy(j, slot):
            px, py, pc = x ^ (j >> 2), y ^ ((j >> 1) & 1), c ^ (j & 1)
            return pltpu.make_async_remote_copy(
                src_ref=in_ref, dst_ref=buf.at[slot(px, py, pc)], send_sem=send_sems.at[j], recv_sem=recv_sems.at[j],
                device_id=(px, py, pc), device_id_type=MESH)

        for j in range(1, N_DEV):
            copy(j, lambda px, py, pc: me).start()
        for j in range(1, N_DEV):
            landing = copy(j, lambda px, py, pc: 4 * px + 2 * py + pc)
            landing.wait_send()
            landing.wait_recv()
        acc = buf[0]
        for s in range(1, N_DEV):
            acc = acc + buf[s]
        out_ref[...] = acc

    return pl.pallas_call(
        body, name="allreduce_small", out_shape=jax.ShapeDtypeStruct(pack.shape, F32),
        in_specs=[VMEM_FULL], out_specs=VMEM_FULL,
        scratch_shapes=[pltpu.VMEM((N_DEV,) + pack.shape, F32), pltpu.SemaphoreType.DMA((N_DEV,)),
                        pltpu.SemaphoreType.DMA((N_DEV,))],
    )(pack)


ROW_TILE_MAX = 256
BF16_SUBLANES = 16


def _row_tile(rows):
    for tr in range(min(rows, ROW_TILE_MAX), 0, -1):
        if rows % tr == 0 and tr % BF16_SUBLANES == 0:
            return tr
    return rows


def _reduce_own(me, part, recv, dep, name):
    _, r, cols = part.shape
    tr = _row_tile(r)

    def body(me_ref, p_ref, rv_ref, dep_ref, o_ref):
        acc = p_ref[0]
        for k in range(3):
            acc = acc + rv_ref[k].astype(F32)
        o_ref[...] = acc

    return pl.pallas_call(
        body, name=name, out_shape=jax.ShapeDtypeStruct((r, cols), F32),
        grid_spec=pltpu.PrefetchScalarGridSpec(
            num_scalar_prefetch=1, grid=(r // tr,),
            in_specs=[pl.BlockSpec((1, tr, cols), lambda i, me_ref: (me_ref[0], i, 0)),
                      pl.BlockSpec((3, tr, cols), lambda i, me_ref: (0, i, 0)), ANY],
            out_specs=pl.BlockSpec((tr, cols), lambda i, me_ref: (i, 0))),
        compiler_params=_params("arbitrary"),
    )(me, part, recv, dep)


def _adamw(w, ga, gb, m, v, name):
    r, cols = w.shape
    tr = _row_tile(r)
    c1 = 1.0 - ADAM_B1 ** ADAM_STEP
    c2 = 1.0 - ADAM_B2 ** ADAM_STEP

    def body(w_ref, ga_ref, gb_ref, m_ref, v_ref, g_out, d_out, m_out, v_out):
        g = ga_ref[...] + gb_ref[...]
        mn = ADAM_B1 * m_ref[...] + (1.0 - ADAM_B1) * g
        vn = ADAM_B2 * v_ref[...] + (1.0 - ADAM_B2) * (g * g)
        g_out[...] = g
        m_out[...] = mn
        v_out[...] = vn
        d_out[...] = -ADAM_LR * ((mn / c1) / (jnp.sqrt(vn / c2) + ADAM_EPS) + ADAM_WD * w_ref[...])

    tile = pl.BlockSpec((tr, cols), lambda i: (i, 0))
    shape = jax.ShapeDtypeStruct((r, cols), F32)
    return pl.pallas_call(
        body, name=name, grid=(r // tr,), out_shape=(shape,) * 4, in_specs=[tile] * 5, out_specs=(tile,) * 4,
        compiler_params=_params("arbitrary"),
    )(w, ga, gb, m, v)


PACK_COLS = 512


def _to_rows(a):
    flat = a.reshape(-1)
    pad = (-flat.shape[0]) % PACK_COLS
    return jnp.pad(flat, (0, pad)).reshape(-1, PACK_COLS)


def _pack(arrays, extra_rows=0):
    rows = [_to_rows(a) for a in arrays]
    n = sum(r.shape[0] for r in rows) + extra_rows
    pad = (-n) % 8
    return jnp.concatenate(rows + [jnp.zeros((extra_rows + pad, PACK_COLS), F32)], axis=0)


def _unpack(pack, like):
    out, at = [], 0
    for a in like:
        n = -(-a.size // PACK_COLS)
        out.append(pack[at:at + n].reshape(-1)[:a.size].reshape(a.shape))
        at += n
    return out


COL_SHARDED = ("ffn1_w_gate", "ffn1_w_up", "w_in", "ffn2_w_gate", "ffn2_w_up", "w2", "a2", "g2")
ROW_SHARDED = ("ffn1_w_down", "ffn2_w_down", "w_out", "w1", "a1", "g1")
CHUNKED = ("ffn1_w_gate", "ffn1_w_up", "ffn1_w_down", "ffn2_w_gate", "ffn2_w_up", "ffn2_w_down")
WEIGHTS = ("ffn1_norm", "ffn1_w_gate", "ffn1_w_up", "ffn1_w_down", "mix_norm", "w_in", "q_norm", "k_norm",
           "mu_r", "mu_k", "mu_v", "mu_w", "mu_a", "mu_g", "w0", "w1", "w2", "a0", "a1", "a2", "g1", "g2",
           "k_k", "k_a", "r_k", "ln_x_w", "ln_x_b", "w_out", "ffn2_norm", "ffn2_w_gate", "ffn2_w_up", "ffn2_w_down")


W_IN_GROUPS = 7


def _full_from_blocks(name, blocks):
    if name in CHUNKED:
        return blocks
    if name in ROW_SHARDED:
        return blocks.reshape(-1, blocks.shape[-1])
    full = blocks.transpose(1, 0, 2).reshape(blocks.shape[1], -1)
    if name == "w_in":
        return full.reshape(full.shape[0], W_IN_GROUPS, -1).transpose(1, 0, 2)
    return full


def _blocks_from_full(name, full):
    if name in CHUNKED:
        return full
    if name in ROW_SHARDED:
        return full.reshape(N_SHARDS, -1, full.shape[-1])
    if name == "w_in":
        full = full.transpose(1, 0, 2).reshape(full.shape[1], -1)
    return full.reshape(full.shape[0], N_SHARDS, -1).transpose(1, 0, 2)


FFN1_GROUP = ("ffn1_w_gate", "ffn1_w_up", "ffn1_w_down")
MIX_GROUP = ("w_in",) + RWKV_MAT
OUT_GROUP = ("w_out", "ffn2_w_gate", "ffn2_w_up", "ffn2_w_down")
FFN2_GROUP = OUT_GROUP[1:]
LATE_GROUP = ("w_in", "w_out") + RWKV_MAT


class _Exchange:
    def __init__(self, given):
        self.given = given
        first = self._gather_start(FFN1_GROUP, _HalfGatherViews, jnp.zeros(DEP_SHAPE, F32), "gather_ffn1_start")
        self.mix = self._gather_start(MIX_GROUP, _GatherViews, first[4], "gather_mix_start")
        self.out = self._gather_start(OUT_GROUP, _GatherViews, self.mix[4], "gather_out_start")
        self.first_dep = self.out[4]
        halves = _push_wait(first, _HalfGatherViews, (self.first_dep,), "gather_ffn1_wait")
        passed = _push_start([], halves, _ForwardViews, halves[0], "gather_ffn1_pass_start")
        self.first_weights = self._full(FFN1_GROUP, _push_wait(passed, _ForwardViews, (passed[4],), "gather_ffn1_pass_wait"))
        self.parts, self.recv = {}, {}

    def _shards(self, names):
        return [self.given[n][0].astype(BF16) for n in names]

    @staticmethod
    def _full(names, blocks):
        out = {}
        for n, b in zip(names, blocks):
            full = _full_from_blocks(n, b)
            out[n] = full.astype(F32) if n in RWKV_MAT else full
        return out

    def _gather_start(self, names, views, after, name):
        shards = self._shards(names)
        return _push_start(shards, _empty_lands(shards, N_SHARDS, True), views, after, name)

    def mix_weights(self, after):
        return self._full(MIX_GROUP, _push_wait(self.mix, _GatherViews, after, "gather_mix_wait"))

    def out_weights(self, after):
        return self._full(OUT_GROUP, _push_wait(self.out, _GatherViews, after, "gather_out_wait"))

    def _scatter_start(self, grads, name):
        names = tuple(grads)
        parts = [_blocks_from_full(n, grads[n]) for n in names]
        self.parts.update(zip(names, parts))
        lands = [lax.empty((3,) + p.shape[1:], BF16) for p in parts]
        return _push_start([p.astype(BF16) for p in parts], lands, _ScatterViews, parts[0], name)

    def send_ffn2(self, grads):
        self.ffn2 = self._scatter_start(grads, "scatter_ffn2_start")
        return self.ffn2[4]

    def send_mix(self, grads, after):
        self.recv.update(zip(FFN2_GROUP, _push_wait(self.ffn2, _ScatterViews, after, "scatter_ffn2_wait")))
        self.late = self._scatter_start(grads, "scatter_late_start")
        return self.late[4]

    def send_ffn1(self, grads):
        self.ffn1 = self._scatter_start(grads, "scatter_ffn1_start")
        return self.ffn1[4]

    def late_received(self, after):
        self.recv.update(zip(LATE_GROUP, _push_wait(self.late, _ScatterViews, after, "scatter_late_wait")))

    def ffn1_received(self, after):
        self.recv.update(zip(FFN1_GROUP, _push_wait(self.ffn1, _ScatterViews, after, "scatter_ffn1_wait")))


def kernel(
        x, ffn1_norm, ffn1_w_gate, ffn1_w_up, ffn1_w_down, mix_norm, w_in, q_norm, k_norm, mu_r, mu_k, mu_v, mu_w,
        mu_a, mu_g, w0, w1, w2, a0, a1, a2, g1, g2, k_k, k_a, r_k, ln_x_w, ln_x_b, w_out, ffn2_norm, ffn2_w_gate,
        ffn2_w_up, ffn2_w_down, loss_target, m_ffn1_norm, m_ffn1_w_gate, m_ffn1_w_up, m_ffn1_w_down, m_mix_norm,
        m_w_in, m_q_norm, m_k_norm, m_mu_r, m_mu_k, m_mu_v, m_mu_w, m_mu_a, m_mu_g, m_w0, m_w1, m_w2, m_a0, m_a1,
        m_a2, m_g1, m_g2, m_k_k, m_k_a, m_r_k, m_ln_x_w, m_ln_x_b, m_w_out, m_ffn2_norm, m_ffn2_w_gate, m_ffn2_w_up,
        m_ffn2_w_down, v_ffn1_norm, v_ffn1_w_gate, v_ffn1_w_up, v_ffn1_w_down, v_mix_norm, v_w_in, v_q_norm, v_k_norm,
        v_mu_r, v_mu_k, v_mu_v, v_mu_w, v_mu_a, v_mu_g, v_w0, v_w1, v_w2, v_a0, v_a1, v_a2, v_g1, v_g2, v_k_k, v_k_a,
        v_r_k, v_ln_x_w, v_ln_x_b, v_w_out, v_ffn2_norm, v_ffn2_w_gate, v_ffn2_w_up, v_ffn2_w_down):
    given = dict(locals())
    sharded = COL_SHARDED + ROW_SHARDED
    sharded = tuple(n for n in WEIGHTS if n in sharded)
    small = tuple(n for n in WEIGHTS if n not in sharded)

    ex = _Exchange(given)
    w = {n: given[n] for n in small}
    w.update(ex.first_weights)
    loss, dx, g = _local_step(x[0], loss_target[0], w, ex)
    dep = ex.send_ffn1({n: g[n] for n in FFN1_GROUP})

    me = (2 * lax.axis_index("x") + lax.axis_index("y")).astype(jnp.int32).reshape(1)
    out = {}

    def settle(names, dep, tag):
        mine = []
        for n in names:
            p, rv = ex.parts[n], ex.recv[n]
            p2 = p.reshape(N_SHARDS, -1, p.shape[-1])
            mine.append(_reduce_own(me, p2, rv.reshape(3, -1, rv.shape[-1]), dep, f"reduce_{n}"))
        theirs = _sibling_swap(mine, f"sibling_swap_{tag}")
        for n, a, b in zip(names, mine, theirs):
            shape = given[n].shape
            two_d = (-1, shape[-1])
            res = _adamw(given[n].reshape(two_d), a, b, given["m_" + n].reshape(two_d), given["v_" + n].reshape(two_d), f"adamw_{n}")
            out[n] = [r.reshape(shape) for r in res]
        return tuple(out[n][1] for n in names)

    ex.late_received((dep,))
    last = settle(tuple(n for n in sharded if n not in FFN1_GROUP), dep, "rest")

    gpack = _pack([g[n] for n in small], extra_rows=1)
    n_rows = sum(-(-given[n].size // PACK_COLS) for n in small)
    gpack = gpack.at[n_rows, :loss.shape[1]].set(loss[0])
    gsum = _allreduce_small(gpack)
    res = _adamw(_pack([given[n] for n in small], 1), gsum, jnp.zeros_like(gsum), _pack([given["m_" + n] for n in small], 1),
                 _pack([given["v_" + n] for n in small], 1), "adamw_small")
    like = [given[n] for n in small]
    for j, r in enumerate(res):
        for n, a in zip(small, _unpack(r, like)):
            out.setdefault(n, [None] * 4)[j] = a
    total_loss = gsum[n_rows, 0]

    ex.ffn1_received((*last, res[1]))
    settle(FFN1_GROUP, jnp.zeros(DEP_SHAPE, F32), "ffn1")
    return (total_loss, dx[None], *[out[n][0] for n in WEIGHTS], *[out[n][1] for n in WEIGHTS],
            *[out[n][2] for n in WEIGHTS], *[out[n][3] for n in WEIGHTS])
```

```python
import functools

import jax
import jax.numpy as jnp
from jax import lax
from jax.experimental import pallas as pl
from jax.experimental.pallas import tpu as pltpu

F32 = jnp.float32
BF16 = jnp.bfloat16
MESH = pl.DeviceIdType.MESH

RMS_EPS = 1e-6
GN_EPS = 64e-5
NEG_INF = -1e30
FFN_RESIDUAL = 0.5
HEAD_DIM = 64
ATT_BLOCK = 128
DILATIONS = (1, 4, 16)
SCAN_CHUNK = 64
TOKEN_TILE = 256

ADAM_LR = 0.001
ADAM_B1 = 0.9
ADAM_B2 = 0.999
ADAM_EPS = 1e-08
ADAM_WD = 0.01
ADAM_STEP = 10

VMEM_FULL = pl.BlockSpec(memory_space=pltpu.VMEM)
ANY = pl.BlockSpec(memory_space=pl.ANY)


VMEM_LIMIT = 56 * 1024 * 1024


def _params(*sem):
    return pltpu.CompilerParams(dimension_semantics=sem, vmem_limit_bytes=VMEM_LIMIT)


def _dot(a, b, dims):
    return lax.dot_general(a.astype(BF16), b.astype(BF16), (dims, ((), ())), preferred_element_type=F32)


def _dot_nn(a, b):
    return _dot(a, b, ((1,), (0,)))


def _dot_nt(a, b):
    return _dot(a, b, ((1,), (1,)))


def _dot_tn(a, b):
    return _dot(a, b, ((0,), (0,)))


@jax.custom_vjp
def _mm(a, b):
    return _dot_nn(a, b)


def _mm_fwd(a, b):
    return _dot_nn(a, b), (a, b)


def _mm_bwd(res, g):
    a, b = res
    return _dot_nt(g, b).astype(a.dtype), _dot_tn(a, g).astype(b.dtype)


_mm.defvjp(_mm_fwd, _mm_bwd)


def _bdot(a, b, ca, cb):
    return lax.dot_general(a.astype(BF16), b.astype(BF16), (((ca,), (cb,)), ((0,), (0,))), preferred_element_type=F32)


@jax.custom_vjp
def _bmm_nt(a, b):
    return _bdot(a, b, 2, 2)


def _bmm_nt_fwd(a, b):
    return _bdot(a, b, 2, 2), (a, b)


def _bmm_nt_bwd(res, g):
    a, b = res
    return _bdot(g, b, 2, 1), _bdot(g, a, 1, 1)


_bmm_nt.defvjp(_bmm_nt_fwd, _bmm_nt_bwd)


@jax.custom_vjp
def _bmm_nn(a, b):
    return _bdot(a, b, 2, 1)


def _bmm_nn_fwd(a, b):
    return _bdot(a, b, 2, 1), (a, b)


def _bmm_nn_bwd(res, g):
    a, b = res
    return _bdot(g, b, 2, 2), _bdot(a, g, 1, 1)


_bmm_nn.defvjp(_bmm_nn_fwd, _bmm_nn_bwd)


@jax.custom_vjp
def _bmm_tn(a, b):
    return _bdot(a, b, 1, 1)


def _bmm_tn_fwd(a, b):
    return _bdot(a, b, 1, 1), (a, b)


def _bmm_tn_bwd(res, g):
    a, b = res
    return _bdot(b, g, 2, 2), _bdot(a, g, 2, 1)


_bmm_tn.defvjp(_bmm_tn_fwd, _bmm_tn_bwd)


def _hdot(a, b, ca, cb):
    return lax.dot_general(a, b, (((ca,), (cb,)), ((0,), (0,))), precision=lax.Precision.HIGH, preferred_element_type=F32)


def _sigmoid(x):
    return 1.0 / (1.0 + jnp.exp(-x))


def _rms(x):
    return lax.rsqrt(jnp.mean(x * x, axis=-1, keepdims=True) + RMS_EPS)


def _ffn_fwd(x, norm, wg, wu, wd, dep, name):
    t, d = x.shape
    nc = wg.shape[0]
    tm = TOKEN_TILE

    def body(x_ref, n_ref, wg_ref, wu_ref, wd_ref, dep_ref, o_ref):
        xv = x_ref[...]
        h = (xv * _rms(xv) * n_ref[...]).astype(BF16)
        acc = jnp.zeros((tm, d), F32)
        for c in range(nc):
            g = jnp.dot(h, wg_ref[c], preferred_element_type=F32)
            u = jnp.dot(h, wu_ref[c], preferred_element_type=F32)
            a = (g * _sigmoid(g) * u).astype(BF16)
            acc = acc + jnp.dot(a, wd_ref[c], preferred_element_type=F32)
        o_ref[...] = xv + FFN_RESIDUAL * acc

    tile = pl.BlockSpec((tm, d), lambda i: (i, 0))
    return pl.pallas_call(
        body, name=name, grid=(t // tm,), out_shape=jax.ShapeDtypeStruct((t, d), F32),
        in_specs=[tile, pl.BlockSpec((1, d), lambda i: (0, 0)), VMEM_FULL, VMEM_FULL, VMEM_FULL, ANY],
        out_specs=tile, compiler_params=_params("arbitrary"),
    )(x, norm, wg, wu, wd, dep)


def _rmsnorm_bwd(xv, gain, dh):
    rs = _rms(xv)
    xn = xv * rs
    dxn = dh * gain
    dx = rs * (dxn - xn * jnp.mean(dxn * xn, axis=-1, keepdims=True))
    return dx, jnp.sum(dh * xn, axis=0, keepdims=True)


def _ffn_bwd(x, norm, wg, wu, wd, dy, dep, name):
    t, d = x.shape
    nc, _, fc = wg.shape
    tm = TOKEN_TILE
    nt = t // tm

    def body(x_ref, n_ref, wg_ref, wu_ref, wd_ref, dy_ref, dep_ref, dx_ref, dn_ref, dwg_ref, dwu_ref, dwd_ref, dh_ref):
        c, i = pl.program_id(0), pl.program_id(1)
        rows = pl.ds(pl.multiple_of(i * tm, tm), tm)
        xv = x_ref[...]
        gain = n_ref[...]
        h = (xv * _rms(xv) * gain).astype(BF16)
        dy = dy_ref[...]
        dyb = (FFN_RESIDUAL * dy).astype(BF16)
        g = jnp.dot(h, wg_ref[0], preferred_element_type=F32)
        u = jnp.dot(h, wu_ref[0], preferred_element_type=F32)
        sg = _sigmoid(g)
        s = g * sg
        a = (s * u).astype(BF16)
        da = _dot_nt(dyb, wd_ref[0])
        dub = (da * s).astype(BF16)
        dgb = (da * u * (sg * (1.0 + g * (1.0 - sg)))).astype(BF16)
        dwd_c = _dot_tn(a, dyb)
        dwg_c = _dot_tn(h, dgb)
        dwu_c = _dot_tn(h, dub)
        dh_c = _dot_nt(dgb, wg_ref[0]) + _dot_nt(dub, wu_ref[0])

        @pl.when(i == 0)
        def _():
            dwd_ref[0] = dwd_c
            dwg_ref[0] = dwg_c
            dwu_ref[0] = dwu_c

        @pl.when(i > 0)
        def _():
            dwd_ref[0] += dwd_c
            dwg_ref[0] += dwg_c
            dwu_ref[0] += dwu_c

        @pl.when(c == 0)
        def _():
            dh_ref[rows, :] = dh_c

        @pl.when(c > 0)
        def _():
            dh_ref[rows, :] += dh_c

        @pl.when(c == nc - 1)
        def _():
            dx, dn = _rmsnorm_bwd(xv, gain, dh_ref[rows, :])
            dx_ref[...] = dx + dy

            @pl.when(i == 0)
            def _():
                dn_ref[...] = dn

            @pl.when(i > 0)
            def _():
                dn_ref[...] += dn

    tile = pl.BlockSpec((tm, d), lambda c, i: (i, 0))
    row = pl.BlockSpec((1, d), lambda c, i: (0, 0))
    wcol = pl.BlockSpec((1, d, fc), lambda c, i: (c, 0, 0))
    wrow = pl.BlockSpec((1, fc, d), lambda c, i: (c, 0, 0))
    last = pl.BlockSpec((tm, d), lambda c, i: (jnp.where(c == nc - 1, i, 0), 0))
    return pl.pallas_call(
        body, name=name, grid=(nc, nt),
        out_shape=(jax.ShapeDtypeStruct((t, d), F32), jax.ShapeDtypeStruct((1, d), F32),
                   jax.ShapeDtypeStruct(wg.shape, F32), jax.ShapeDtypeStruct(wu.shape, F32),
                   jax.ShapeDtypeStruct(wd.shape, F32)),
        in_specs=[tile, row, wcol, wcol, wrow, tile, ANY],
        out_specs=(last, row, wcol, wcol, wrow),
        scratch_shapes=[pltpu.VMEM((t, d), F32)],
        compiler_params=_params("arbitrary", "arbitrary"),
    )(x, norm, wg, wu, wd, dy, dep)


def _store_heads(ref, v):
    for h in range(ref.shape[0]):
        ref[h] = v[:, h * HEAD_DIM:(h + 1) * HEAD_DIM]


def _load_heads(ref):
    return jnp.concatenate([ref[h] for h in range(ref.shape[0])], axis=-1)


N_HEAD_GROUPS = 3


def _proj_fwd(x, norm, w):
    t, d = x.shape
    ng, _, c = w.shape
    nh = c // HEAD_DIM
    tm = TOKEN_TILE

    def body(x_ref, n_ref, w_ref, q_ref, k_ref, v_ref, cur_ref):
        xv = x_ref[...]
        h = (xv * _rms(xv) * n_ref[...]).astype(BF16)
        for m, ref in enumerate((q_ref, k_ref, v_ref)):
            _store_heads(ref, jnp.dot(h, w_ref[m], preferred_element_type=F32))
        for m in range(N_HEAD_GROUPS, ng):
            j = m - N_HEAD_GROUPS
            cur_ref[:, j * c:(j + 1) * c] = jnp.dot(h, w_ref[m], preferred_element_type=F32)

    heads = pl.BlockSpec((nh, tm, HEAD_DIM), lambda i: (0, i, 0))
    hshape = jax.ShapeDtypeStruct((nh, t, HEAD_DIM), F32)
    wide = (ng - N_HEAD_GROUPS) * c
    return pl.pallas_call(
        body, name="proj_fwd", grid=(t // tm,),
        out_shape=(hshape, hshape, hshape, jax.ShapeDtypeStruct((t, wide), F32)),
        in_specs=[pl.BlockSpec((tm, d), lambda i: (i, 0)), pl.BlockSpec((1, d), lambda i: (0, 0)), VMEM_FULL],
        out_specs=(heads, heads, heads, pl.BlockSpec((tm, wide), lambda i: (i, 0))),
        compiler_params=_params("arbitrary"),
    )(x, norm, w)


def _proj_bwd(x, norm, w, dq, dk, dv, dcur, dres):
    t, d = x.shape
    ng, _, c = w.shape
    nh = c // HEAD_DIM
    tm = TOKEN_TILE

    def body(x_ref, n_ref, w_ref, dq_ref, dk_ref, dv_ref, dcur_ref, dres_ref, dx_ref, dn_ref, dw_ref):
        i = pl.program_id(0)

        @pl.when(i == 0)
        def _():
            dw_ref[...] = jnp.zeros_like(dw_ref)
            dn_ref[...] = jnp.zeros_like(dn_ref)

        xv = x_ref[...]
        gain = n_ref[...]
        h = (xv * _rms(xv) * gain).astype(BF16)
        dh = jnp.zeros((tm, d), F32)
        for m in range(ng):
            j = m - N_HEAD_GROUPS
            dp = _load_heads((dq_ref, dk_ref, dv_ref)[m]) if j < 0 else dcur_ref[:, j * c:(j + 1) * c]
            dp = dp.astype(BF16)
            dw_ref[m] += _dot_tn(h, dp)
            dh = dh + _dot_nt(dp, w_ref[m])
        dx, dn = _rmsnorm_bwd(xv, gain, dh)
        dx_ref[...] = dx + dres_ref[...]
        dn_ref[...] += dn

    tile = pl.BlockSpec((tm, d), lambda i: (i, 0))
    row = pl.BlockSpec((1, d), lambda i: (0, 0))
    heads = pl.BlockSpec((nh, tm, HEAD_DIM), lambda i: (0, i, 0))
    wide = (ng - N_HEAD_GROUPS) * c
    return pl.pallas_call(
        body, name="proj_bwd", grid=(t // tm,),
        out_shape=(jax.ShapeDtypeStruct((t, d), F32), jax.ShapeDtypeStruct((1, d), F32),
                   jax.ShapeDtypeStruct(w.shape, F32)),
        in_specs=[tile, row, VMEM_FULL, heads, heads, heads, pl.BlockSpec((tm, wide), lambda i: (i, 0)), tile],
        out_specs=(tile, row, VMEM_FULL),
        compiler_params=_params("arbitrary"),
    )(x, norm, w, dq, dk, dv, dcur, dres)


def _mixout_fwd(x, att, opg, gate, w):
    t, d = x.shape
    nh = att.shape[0]
    half = gate.shape[1]
    tm = TOKEN_TILE

    def body(x_ref, att_ref, opg_ref, g_ref, w_ref, o_ref):
        mix = jnp.concatenate([_load_heads(att_ref), _load_heads(opg_ref) * g_ref[...]], axis=-1).astype(BF16)
        o_ref[...] = x_ref[...] + jnp.dot(mix, w_ref[...], preferred_element_type=F32)

    tile = pl.BlockSpec((tm, d), lambda i: (i, 0))
    htile = pl.BlockSpec((tm, half), lambda i: (i, 0))
    heads = pl.BlockSpec((nh, tm, HEAD_DIM), lambda i: (0, i, 0))
    return pl.pallas_call(
        body, name="mixout_fwd", grid=(t // tm,), out_shape=jax.ShapeDtypeStruct((t, d), F32),
        in_specs=[tile, heads, heads, htile, VMEM_FULL], out_specs=tile, compiler_params=_params("arbitrary"),
    )(x, att, opg, gate, w)


def _mixout_bwd(att, opg, gate, w, dy, dep):
    nh, t, _ = att.shape
    half = gate.shape[1]
    d = dy.shape[1]
    tm = TOKEN_TILE

    def body(att_ref, opg_ref, g_ref, w_ref, dy_ref, dep_ref, datt_ref, dopg_ref, dg_ref, dw_ref):
        i = pl.program_id(0)
        opg_v, g_v = _load_heads(opg_ref), g_ref[...]
        mix = jnp.concatenate([_load_heads(att_ref), opg_v * g_v], axis=-1).astype(BF16)
        dyb = dy_ref[...].astype(BF16)
        dmix = _dot_nt(dyb, w_ref[...])
        dw = _dot_tn(mix, dyb)
        _store_heads(datt_ref, dmix[:, :half])
        drw = dmix[:, half:]
        _store_heads(dopg_ref, drw * g_v)
        dg_ref[...] = drw * opg_v

        @pl.when(i == 0)
        def _():
            dw_ref[...] = dw

        @pl.when(i > 0)
        def _():
            dw_ref[...] += dw

    tile = pl.BlockSpec((tm, d), lambda i: (i, 0))
    htile = pl.BlockSpec((tm, half), lambda i: (i, 0))
    heads = pl.BlockSpec((nh, tm, HEAD_DIM), lambda i: (0, i, 0))
    hshape = jax.ShapeDtypeStruct((nh, t, HEAD_DIM), F32)
    return pl.pallas_call(
        body, name="mixout_bwd", grid=(t // tm,),
        out_shape=(hshape, hshape, jax.ShapeDtypeStruct((t, half), F32), jax.ShapeDtypeStruct(w.shape, F32)),
        in_specs=[heads, heads, htile, VMEM_FULL, tile, ANY],
        out_specs=(heads, heads, htile, pl.BlockSpec(w.shape, lambda i: (0, 0))),
        compiler_params=_params("arbitrary"),
    )(att, opg, gate, w, dy, dep)


def _loss_head(y, target):
    t, d = y.shape
    tm = TOKEN_TILE

    def body(y_ref, t_ref, dy_ref, loss_ref):
        i = pl.program_id(0)
        err = y_ref[...] - t_ref[...]
        dy_ref[...] = err * (1.0 / d)
        part = 0.5 * jnp.sum(jnp.mean(err * err, axis=-1, keepdims=True), axis=0, keepdims=True)

        @pl.when(i == 0)
        def _():
            loss_ref[...] = jnp.zeros_like(loss_ref)

        loss_ref[...] += jnp.broadcast_to(part, loss_ref.shape)

    tile = pl.BlockSpec((tm, d), lambda i: (i, 0))
    return pl.pallas_call(
        body, name="loss_head", grid=(t // tm,),
        out_shape=(jax.ShapeDtypeStruct((t, d), F32), jax.ShapeDtypeStruct((1, 128), F32)),
        in_specs=[tile, tile], out_specs=(tile, pl.BlockSpec((1, 128), lambda i: (0, 0))),
        compiler_params=_params("arbitrary"),
    )(y, target)


def _att_pattern(q, k, v, qn, kn, nb):
    g, blk, _ = q.shape
    qh = q * _rms(q) * qn
    kh = k * _rms(k) * kn
    scale = HEAD_DIM ** -0.5
    qi = lax.broadcasted_iota(jnp.int32, (blk, blk), 0)
    kj = lax.broadcasted_iota(jnp.int32, (blk, blk), 1)
    sc = jnp.where(kj <= qi, _bmm_nt(qh, kh) * scale, NEG_INF)
    top = jnp.max(sc, axis=-1, keepdims=True)
    if nb > 1:
        khp = jnp.concatenate([kh[:1], kh[:-1]], axis=0)
        vp = jnp.concatenate([v[:1], v[:-1]], axis=0)
        has_prev = lax.broadcasted_iota(jnp.int32, (g, 1, 1), 0) % nb != 0
        sp = jnp.where((kj >= qi) & has_prev, _bmm_nt(qh, khp) * scale, NEG_INF)
        top = jnp.maximum(top, jnp.max(sp, axis=-1, keepdims=True))
    m = lax.stop_gradient(top)
    pc = jnp.exp(sc - m)
    den = jnp.sum(pc, axis=-1, keepdims=True)
    acc = _bmm_nn(pc, v)
    if nb > 1:
        pp = jnp.exp(sp - m)
        den = den + jnp.sum(pp, axis=-1, keepdims=True)
        acc = acc + _bmm_nn(pp, vp)
    o = acc / den
    return o, jnp.broadcast_to(m + jnp.log(den), o.shape)


def _pattern_rows(t, dil):
    nb = t // (ATT_BLOCK * dil)
    starts = [n * ATT_BLOCK * dil + r for r in range(dil) for n in range(nb)]
    return [pl.ds(s, ATT_BLOCK, stride=dil) if dil > 1 else pl.ds(s, ATT_BLOCK) for s in starts], nb


def _take(ref, rows):
    return jnp.stack([ref[0, r, :] for r in rows])


def _put(ref, rows, val):
    for g, r in enumerate(rows):
        ref[0, r, :] = val[g]


def _put_add(ref, rows, val):
    for g, r in enumerate(rows):
        ref[0, r, :] += val[g]


def _merge_fn(o1, o2, o3, l1, l2, l3):
    m = lax.stop_gradient(jnp.maximum(jnp.maximum(l1, l2), l3))
    e1, e2, e3 = jnp.exp(l1 - m), jnp.exp(l2 - m), jnp.exp(l3 - m)
    return (e1 * o1 + e2 * o2 + e3 * o3) / (e1 + e2 + e3)


def _att_head_specs(t):
    head = pl.BlockSpec((1, t, HEAD_DIM), lambda h: (h, 0, 0))
    gain = pl.BlockSpec((1, 1, HEAD_DIM), lambda h: (0, 0, 0))
    return head, gain


def _att_fwd(q, k, v, qn, kn):
    nh, t, dh = q.shape
    head, gain = _att_head_specs(t)

    def body(q_ref, k_ref, v_ref, qn_ref, kn_ref, att_ref, *saved):
        o_refs, l_refs = saved[:3], saved[3:]
        for p, dil in enumerate(DILATIONS):
            rows, nb = _pattern_rows(t, dil)
            o, lse = _att_pattern(_take(q_ref, rows), _take(k_ref, rows), _take(v_ref, rows), qn_ref[...], kn_ref[...], nb)
            _put(o_refs[p], rows, o)
            _put(l_refs[p], rows, lse)

        def merge(j, carry):
            rows = pl.ds(pl.multiple_of(j * ATT_BLOCK, ATT_BLOCK), ATT_BLOCK)
            att_ref[0, rows, :] = _merge_fn(*[r[0, rows, :] for r in saved])
            return carry

        lax.fori_loop(0, t // ATT_BLOCK, merge, 0)

    return pl.pallas_call(
        body, name="att_fwd", grid=(nh,), out_shape=(jax.ShapeDtypeStruct(q.shape, F32),) * 7,
        in_specs=[head, head, head, gain, gain], out_specs=(head,) * 7, compiler_params=_params("arbitrary"),
    )(q, k, v, qn, kn)


def _att_bwd(q, k, v, qn, kn, saved, datt):
    nh, t, dh = q.shape
    head, gain = _att_head_specs(t)

    def body(q_ref, k_ref, v_ref, qn_ref, kn_ref, o1, o2, o3, l1, l2, l3, datt_ref,
             dq_ref, dk_ref, dv_ref, dqn_ref, dkn_ref):
        for ref in (dq_ref, dk_ref, dv_ref):
            ref[...] = jnp.zeros_like(ref)

        @pl.when(pl.program_id(0) == 0)
        def _():
            dqn_ref[...] = jnp.zeros_like(dqn_ref)
            dkn_ref[...] = jnp.zeros_like(dkn_ref)

        for p, dil in enumerate(DILATIONS):
            rows, nb = _pattern_rows(t, dil)
            _, merge_vjp = jax.vjp(_merge_fn, *[_take(r, rows) for r in (o1, o2, o3, l1, l2, l3)])
            cts = merge_vjp(_take(datt_ref, rows))
            _, pattern_vjp = jax.vjp(functools.partial(_att_pattern, nb=nb), _take(q_ref, rows), _take(k_ref, rows),
                                     _take(v_ref, rows), qn_ref[...], kn_ref[...])
            dq, dk, dv, dgq, dgk = pattern_vjp((cts[p], cts[3 + p]))
            _put_add(dq_ref, rows, dq)
            _put_add(dk_ref, rows, dk)
            _put_add(dv_ref, rows, dv)
            dqn_ref[...] += dgq
            dkn_ref[...] += dgk

    hshape = jax.ShapeDtypeStruct(q.shape, F32)
    gshape = jax.ShapeDtypeStruct((1, 1, dh), F32)
    return pl.pallas_call(
        body, name="att_bwd", grid=(nh,), out_shape=(hshape, hshape, hshape, gshape, gshape),
        in_specs=[head, head, head, gain, gain] + [head] * 7, out_specs=(head, head, head, gain, gain),
        compiler_params=_params("arbitrary"),
    )(q, k, v, qn, kn, *saved, datt)


RWKV_VEC = ("mu_r", "mu_k", "mu_v", "mu_w", "mu_a", "mu_g", "w0", "a0", "k_k", "k_a")
RWKV_MAT = ("w1", "w2", "a1", "a2", "g1", "g2")


def _rwkv_pre_fn(cur, prev, vec, w1, w2, a1, a2, g1, g2):
    c = cur.shape[1] // 4
    mu_r, mu_k, mu_v, mu_w, mu_a, mu_g, w0, a0, k_k, k_a = (vec[j:j + 1] for j in range(10))

    def lerp(j, mu):
        xc, xp = cur[:, j * c:(j + 1) * c], prev[:, j * c:(j + 1) * c]
        return xc + (xp - xc) * mu

    r, k, v = lerp(0, mu_r), lerp(1, mu_k), lerp(2, mu_v)
    cw, ca, cg = lerp(3, mu_w), lerp(3, mu_a), lerp(3, mu_g)
    z = w0 + _mm(jnp.tanh(_mm(cw, w1)), w2)
    w_log = jnp.minimum(z, 0.0) - jnp.log(1.0 + jnp.exp(-jnp.abs(z))) - 0.5
    lw = -jnp.exp(w_log)
    a = _sigmoid(a0 + _mm(_mm(ca, a1), a2))
    gate = _mm(_sigmoid(_mm(cg, g1)), g2)
    kkraw = k * k_k
    kmod = k * (1.0 + (a - 1.0) * k_a)
    return r, lw, kmod, v, kkraw, a, gate


HALO_ROWS = 8


def _rwkv_pre_specs(c, mats, tile_of):
    tm = TOKEN_TILE
    nh = c // HEAD_DIM
    wide = pl.BlockSpec((tm, 4 * c), lambda j: (tile_of(j), 0))
    halo = pl.BlockSpec((HALO_ROWS, 4 * c), lambda j: (jnp.maximum(tile_of(j) * (tm // HALO_ROWS) - 1, 0), 0))
    one = pl.BlockSpec((tm, c), lambda j: (tile_of(j), 0))
    heads = pl.BlockSpec((nh, tm, HEAD_DIM), lambda j: (0, tile_of(j), 0))
    vec = pl.BlockSpec((10, c), lambda j: (0, 0))
    mspecs = [pl.BlockSpec(m.shape, lambda j: (0, 0)) for m in mats]
    return wide, halo, one, heads, vec, mspecs


def _previous_rows(cur, halo, tile):
    first = jnp.where(tile > 0, halo[HALO_ROWS - 1:HALO_ROWS], 0.0)
    rows = lax.broadcasted_iota(jnp.int32, cur.shape, 0)
    return jnp.where(rows == 0, first, pltpu.roll(cur, 1, axis=0))


def _rwkv_pre_fwd(cur, vec, mats):
    t, c4 = cur.shape
    c = c4 // 4
    wide, halo, one, heads, vspec, mspecs = _rwkv_pre_specs(c, mats, lambda j: j)

    def body(cur_ref, halo_ref, vec_ref, *rest):
        mrefs, outs = rest[:6], rest[6:]
        cur_v = cur_ref[...]
        prev = _previous_rows(cur_v, halo_ref[...], pl.program_id(0))
        vals = _rwkv_pre_fn(cur_v, prev, vec_ref[...], *(m[...] for m in mrefs))
        for ref, val in zip(outs[:6], vals[:6]):
            _store_heads(ref, val)
        outs[6][...] = vals[6]

    hshape = jax.ShapeDtypeStruct((c // HEAD_DIM, t, HEAD_DIM), F32)
    return pl.pallas_call(
        body, name="rwkv_pre_fwd", grid=(t // TOKEN_TILE,), out_shape=(hshape,) * 6 + (jax.ShapeDtypeStruct((t, c), F32),),
        in_specs=[wide, halo, vspec] + mspecs, out_specs=(heads,) * 6 + (one,), compiler_params=_params("arbitrary"),
    )(cur, cur, vec, *mats)


def _rwkv_pre_bwd(cur, vec, mats, cts, dgate):
    t, c4 = cur.shape
    c = c4 // 4
    tm = TOKEN_TILE
    nt = t // tm
    wide, halo, one, heads, vspec, mspecs = _rwkv_pre_specs(c, mats, lambda j: nt - 1 - j)

    def body(cur_ref, halo_ref, vec_ref, *rest):
        mrefs, ctrefs, dgate_ref, outs, carry_ref = rest[:6], rest[6:12], rest[12], rest[13:-1], rest[-1]
        j = pl.program_id(0)

        @pl.when(j == 0)
        def _():
            carry_ref[...] = jnp.zeros_like(carry_ref)
            for ref in outs[1:]:
                ref[...] = jnp.zeros_like(ref)

        cur_v = cur_ref[...]
        prev = _previous_rows(cur_v, halo_ref[...], nt - 1 - j)
        _, vjp = jax.vjp(_rwkv_pre_fn, cur_v, prev, vec_ref[...], *(m[...] for m in mrefs))
        grads = vjp(tuple(_load_heads(r) for r in ctrefs) + (dgate_ref[...],))
        dprev = grads[1]
        rows = lax.broadcasted_iota(jnp.int32, dprev.shape, 0)
        outs[0][...] = grads[0] + jnp.where(rows == tm - 1, carry_ref[0:1], pltpu.roll(dprev, tm - 1, axis=0))
        carry_ref[0:1] = dprev[0:1]
        for ref, val in zip(outs[1:], grads[2:]):
            ref[...] += val

    return pl.pallas_call(
        body, name="rwkv_pre_bwd", grid=(nt,),
        out_shape=(jax.ShapeDtypeStruct(cur.shape, F32), jax.ShapeDtypeStruct(vec.shape, F32))
        + tuple(jax.ShapeDtypeStruct(m.shape, F32) for m in mats),
        in_specs=[wide, halo, vspec] + mspecs + [heads] * 6 + [one], out_specs=(wide, vspec) + tuple(mspecs),
        scratch_shapes=[pltpu.VMEM((HALO_ROWS, c4), F32)], compiler_params=_params("arbitrary"),
    )(cur, cur, vec, *mats, *cts, dgate)


def _scan_chunk_fn(h0, r, lw, k, v, kkraw, a, rk, lnw, lnb):
    n = r.shape[1]
    nrm = jnp.sqrt(jnp.sum(kkraw * kkraw, axis=-1, keepdims=True))
    kk = kkraw / jnp.maximum(nrm, 1e-12)
    av, bv = -kk, kk * a
    ti = lax.broadcasted_iota(jnp.int32, (n, n), 0)
    si = lax.broadcasted_iota(jnp.int32, (n, n), 1)
    incl, strict = ti >= si, ti > si
    ones = jnp.broadcast_to(incl.astype(F32)[None], (r.shape[0], n, n))
    cum = _hdot(ones, lw, 2, 1)
    at, rt = av * jnp.exp(cum - lw), r * jnp.exp(cum)
    inv = jnp.exp(-cum)
    bt, kt = bv * inv, k * inv
    lab = jnp.where(strict, _hdot(at, bt, 2, 2), 0.0)
    lak = jnp.where(strict, _hdot(at, kt, 2, 2), 0.0)
    rb = jnp.where(incl, _hdot(rt, bt, 2, 2), 0.0)
    rkm = jnp.where(incl, _hdot(rt, kt, 2, 2), 0.0)
    u = _bmm_nn(at, h0) + _bmm_nn(lak, v)
    p = lab
    m = 1
    while m < n:
        u = u + _bmm_nn(p, u)
        m *= 2
        if m < n:
            p = _bmm_nn(p, p)
    y = _bmm_nn(rt, h0) + _bmm_nn(rb, u) + _bmm_nn(rkm, v)
    last = jnp.exp(jnp.sum(lw, axis=1, keepdims=True))
    h1 = jnp.swapaxes(last, 1, 2) * (h0 + _bmm_tn(bt, u) + _bmm_tn(kt, v))
    mean = jnp.mean(y, axis=-1, keepdims=True)
    yc = y - mean
    var = jnp.mean(yc * yc, axis=-1, keepdims=True)
    yn = yc * lax.rsqrt(var + GN_EPS) * lnw + lnb
    bonus = jnp.sum(r * k * rk, axis=-1, keepdims=True) * v
    return yn + bonus, h1


def _scan_specs(h, t, dh, rev):
    n = SCAN_CHUNK
    nc = t // n
    pos = (lambda c: (0, nc - 1 - c, 0)) if rev else (lambda c: (0, c, 0))
    st = (lambda c: (nc - 1 - c, 0, 0, 0)) if rev else (lambda c: (c, 0, 0, 0))
    seq = pl.BlockSpec((h, n, dh), pos)
    par = pl.BlockSpec((h, 1, dh), lambda c: (0, 0, 0))
    state = pl.BlockSpec((1, h, dh, dh), st)
    return seq, par, state


def _scan_fwd(seqs, pars):
    h, t, dh = seqs[0].shape
    nc = t // SCAN_CHUNK
    seq, par, state = _scan_specs(h, t, dh, False)

    def body(r, lw, k, v, kkraw, a, rk, lnw, lnb, o_ref, st_ref, h_ref):
        @pl.when(pl.program_id(0) == 0)
        def _():
            h_ref[...] = jnp.zeros_like(h_ref)

        h0 = h_ref[...]
        st_ref[0] = h0
        o, h1 = _scan_chunk_fn(h0, r[...], lw[...], k[...], v[...], kkraw[...], a[...], rk[...], lnw[...], lnb[...])
        o_ref[...] = o
        h_ref[...] = h1

    return pl.pallas_call(
        body, name="rwkv_scan_fwd", grid=(nc,),
        out_shape=(jax.ShapeDtypeStruct((h, t, dh), F32), jax.ShapeDtypeStruct((nc, h, dh, dh), F32)),
        in_specs=[seq] * 6 + [par] * 3, out_specs=(seq, state),
        scratch_shapes=[pltpu.VMEM((h, dh, dh), F32)], compiler_params=_params("arbitrary"),
    )(*seqs, *pars)


def _scan_bwd(seqs, pars, states, do):
    h, t, dh = seqs[0].shape
    nc = t // SCAN_CHUNK
    seq, par, state = _scan_specs(h, t, dh, True)

    def body(r, lw, k, v, kkraw, a, rk, lnw, lnb, st_ref, do_ref, *rest):
        douts, dpars, dh_ref = rest[:6], rest[6:9], rest[9]
        first = pl.program_id(0) == 0

        @pl.when(first)
        def _():
            dh_ref[...] = jnp.zeros_like(dh_ref)

        _, vjp = jax.vjp(_scan_chunk_fn, st_ref[0], r[...], lw[...], k[...], v[...], kkraw[...], a[...],
                         rk[...], lnw[...], lnb[...])
        grads = vjp((do_ref[...], dh_ref[...]))
        dh_ref[...] = grads[0]
        for ref, val in zip(douts, grads[1:7]):
            ref[...] = val

        @pl.when(first)
        def _():
            for ref, val in zip(dpars, grads[7:]):
                ref[...] = val

        @pl.when(jnp.logical_not(first))
        def _():
            for ref, val in zip(dpars, grads[7:]):
                ref[...] += val

    sshape = jax.ShapeDtypeStruct((h, t, dh), F32)
    pshape = jax.ShapeDtypeStruct((h, 1, dh), F32)
    return pl.pallas_call(
        body, name="rwkv_scan_bwd", grid=(nc,), out_shape=(sshape,) * 6 + (pshape,) * 3,
        in_specs=[seq] * 6 + [par] * 3 + [state, seq], out_specs=(seq,) * 6 + (par,) * 3,
        scratch_shapes=[pltpu.VMEM((h, dh, dh), F32)], compiler_params=_params("arbitrary"),
    )(*seqs, *pars, states, do)


def _local_step(x, target, w, ex):
    w = dict(w)
    c = w["mu_r"].shape[-1]
    qn, kn = w["q_norm"].reshape(1, 1, HEAD_DIM), w["k_norm"].reshape(1, 1, HEAD_DIM)
    vec = jnp.concatenate([w[n].reshape(1, c) for n in RWKV_VEC], axis=0)
    pars = [w[n].reshape(-1, 1, HEAD_DIM) for n in ("r_k", "ln_x_w", "ln_x_b")]
    no_dep = jnp.zeros(DEP_SHAPE, F32)

    x1 = _ffn_fwd(x, w["ffn1_norm"], w["ffn1_w_gate"], w["ffn1_w_up"], w["ffn1_w_down"], ex.first_dep, "ffn1_fwd")
    w.update(ex.mix_weights((x1,)))
    mats = [w[n] for n in RWKV_MAT]
    q, k, v, cur = _proj_fwd(x1, w["mix_norm"], w["w_in"])
    att, *saved = _att_fwd(q, k, v, qn, kn)
    pre = _rwkv_pre_fwd(cur, vec, mats)
    seqs, gate = pre[:6], pre[6]
    opg, states = _scan_fwd(seqs, pars)
    w.update(ex.out_weights((att, opg)))
    x2 = _mixout_fwd(x1, att, opg, gate, w["w_out"])
    x3 = _ffn_fwd(x2, w["ffn2_norm"], w["ffn2_w_gate"], w["ffn2_w_up"], w["ffn2_w_down"], no_dep, "ffn2_fwd")
    dy, loss = _loss_head(x3, target)

    g = {}
    dx2, g["ffn2_norm"], g["ffn2_w_gate"], g["ffn2_w_up"], g["ffn2_w_down"] = _ffn_bwd(
        x2, w["ffn2_norm"], w["ffn2_w_gate"], w["ffn2_w_up"], w["ffn2_w_down"], dy, no_dep, "ffn2_bwd")
    dep = ex.send_ffn2({n: g[n] for n in ("ffn2_w_gate", "ffn2_w_up", "ffn2_w_down")})
    datt, dopg, dgate, g["w_out"] = _mixout_bwd(att, opg, gate, w["w_out"], dx2, dep)
    dscan = _scan_bwd(seqs, pars, states, dopg)
    for n, d in zip(("r_k", "ln_x_w", "ln_x_b"), dscan[6:]):
        g[n] = d
    dcur, dvec, *dmats = _rwkv_pre_bwd(cur, vec, mats, dscan[:6], dgate)
    for n, d in zip(RWKV_MAT, dmats):
        g[n] = d
    for j, n in enumerate(RWKV_VEC):
        g[n] = dvec[j:j + 1]
    dq, dk, dv, g["q_norm"], g["k_norm"] = _att_bwd(q, k, v, qn, kn, saved, datt)
    dx1, g["mix_norm"], g["w_in"] = _proj_bwd(x1, w["mix_norm"], w["w_in"], dq, dk, dv, dcur, dx2)
    dep = ex.send_mix({n: g[n] for n in ("w_in", "w_out") + RWKV_MAT}, (dx1,))
    dx, g["ffn1_norm"], g["ffn1_w_gate"], g["ffn1_w_up"], g["ffn1_w_down"] = _ffn_bwd(
        x, w["ffn1_norm"], w["ffn1_w_gate"], w["ffn1_w_up"], w["ffn1_w_down"], dx1, dep, "ffn1_bwd")
    return loss, dx, g


N_SHARDS = 4


def _place():
    return lax.axis_index("x"), lax.axis_index("y"), lax.axis_index("c")


def _chip_peers(x, y):
    return [(1 - x, y), (x, 1 - y), (1 - x, 1 - y)]


HBM = pl.BlockSpec(memory_space=pltpu.HBM)
SEM = pl.BlockSpec(memory_space=pltpu.SEMAPHORE)
DEP_SHAPE = (8, 128)


class _Views:
    to_sibling = False


class _GatherViews(_Views):
    @staticmethod
    def send(i, srcs, lands, k, at):
        return srcs[i], lands[i].at[at[3]]

    @staticmethod
    def landing(i, srcs, lands, k, at):
        return srcs[i], lands[i].at[2 * at[4] + at[5]]


class _ScatterViews(_Views):
    @staticmethod
    def send(i, srcs, lands, k, at):
        return srcs[i].at[2 * at[4] + at[5]], lands[i].at[k]

    @staticmethod
    def landing(i, srcs, lands, k, at):
        return srcs[i].at[at[3]], lands[i].at[k]


def _half_rows(ref, slot, half):
    rows = ref.shape[1] // 2
    return ref.at[slot, pl.ds(pl.multiple_of(half * rows, BF16_SUBLANES), rows)]


class _HalfGatherViews(_Views):
    @staticmethod
    def send(i, srcs, lands, k, at):
        rows = srcs[i].shape[0] // 2
        return srcs[i].at[pl.ds(pl.multiple_of(at[2] * rows, BF16_SUBLANES), rows)], _half_rows(lands[i], at[3], at[2])

    @staticmethod
    def landing(i, srcs, lands, k, at):
        rows = srcs[i].shape[0] // 2
        return srcs[i].at[pl.ds(pl.multiple_of(at[2] * rows, BF16_SUBLANES), rows)], _half_rows(lands[i], 2 * at[4] + at[5], at[2])


class _ForwardViews(_Views):
    to_sibling = True

    @staticmethod
    def send(i, srcs, lands, k, at):
        mine = _half_rows(lands[i], 2 * at[4] + at[5], at[2])
        return mine, mine

    @staticmethod
    def landing(i, srcs, lands, k, at):
        theirs = _half_rows(lands[i], 2 * at[4] + at[5], 1 - at[2])
        return theirs, theirs


def _push_start(srcs, lands, views, after, name):
    ns, nl = len(srcs), len(lands)

    def body(*refs):
        src_refs, land_refs = refs[:ns], refs[ns:ns + nl]
        send_sems, recv_sems = refs[ns + nl + 1:ns + nl + 3]
        token = refs[2 * (ns + nl) + 3]
        x, y, c = _place()
        for i in range(nl):
            for k, (px, py) in enumerate(_chip_peers(x, y)):
                src, dst = views.send(i, src_refs, land_refs, k, (x, y, c, 2 * x + y, px, py))
                pltpu.make_async_remote_copy(
                    src_ref=src, dst_ref=dst, send_sem=send_sems.at[3 * i + k], recv_sem=recv_sems.at[3 * i + k],
                    device_id=(x, y, 1 - c) if views.to_sibling else (px, py, c), device_id_type=MESH).start()
        token[...] = jnp.zeros_like(token)

    sems = pltpu.SemaphoreType.DMA((3 * nl,))
    both = [pltpu.with_memory_space_constraint(a, pltpu.HBM) for a in (*srcs, *lands)]
    outs = pl.pallas_call(
        body, name=name,
        out_shape=(sems, sems, *[pltpu.HBM(a.shape, a.dtype) for a in both], jax.ShapeDtypeStruct(DEP_SHAPE, F32)),
        in_specs=[HBM] * (ns + nl) + [ANY], out_specs=(SEM, SEM, *[HBM] * (ns + nl), VMEM_FULL),
        input_output_aliases={i: 2 + i for i in range(ns + nl)},
        compiler_params=pltpu.CompilerParams(has_side_effects=pltpu.SideEffectType.DATAFLOW_SIDE_EFFECTING),
    )(*both, after)
    return outs[0], outs[1], outs[2:2 + ns], outs[2 + ns:2 + ns + nl], outs[2 + ns + nl]


def _push_wait(started, views, after, name):
    send_sems, recv_sems, srcs, lands, _ = started
    ns, nl = len(srcs), len(lands)

    def body(*refs):
        src_refs, land_refs = refs[:ns], refs[ns:ns + nl]
        send_sems, recv_sems = refs[ns + nl:ns + nl + 2]
        x, y, c = _place()
        for i in range(nl):
            for k, (px, py) in enumerate(_chip_peers(x, y)):
                src, dst = views.landing(i, src_refs, land_refs, k, (x, y, c, 2 * x + y, px, py))
                landing = pltpu.make_async_remote_copy(
                    src_ref=src, dst_ref=dst, send_sem=send_sems.at[3 * i + k], recv_sem=recv_sems.at[3 * i + k],
                    device_id=(x, y, 1 - c) if views.to_sibling else (px, py, c), device_id_type=MESH)
                landing.wait_send()
                landing.wait_recv()

    outs = pl.pallas_call(
        body, name=name,
        out_shape=tuple(pltpu.HBM(a.shape, a.dtype) for a in (*srcs, *lands)),
        in_specs=[HBM] * (ns + nl) + [SEM, SEM] + [ANY] * len(after), out_specs=(HBM,) * (ns + nl),
        input_output_aliases={i: i for i in range(ns + nl)},
        compiler_params=pltpu.CompilerParams(has_side_effects=pltpu.SideEffectType.DATAFLOW_SIDE_EFFECTING),
    )(*srcs, *lands, send_sems, recv_sems, *after)
    return outs[ns:]


def _empty_lands(shards, slots, own_slot):
    lands = [lax.empty((slots,) + s.shape, s.dtype) for s in shards]
    if own_slot:
        me = 2 * lax.axis_index("x") + lax.axis_index("y")
        lands = [lax.dynamic_update_index_in_dim(z, s, me, 0) for z, s in zip(lands, shards)]
    return lands


def _sibling_swap(arrays, name):
    n = len(arrays)

    def body(*refs):
        ins, outs = refs[:n], refs[n:2 * n]
        send_sems, recv_sems = refs[2 * n:]
        x, y, c = _place()
        copies = []
        for i in range(n):
            cp = pltpu.make_async_remote_copy(
                src_ref=ins[i], dst_ref=outs[i], send_sem=send_sems.at[i], recv_sem=recv_sems.at[i],
                device_id=(x, y, 1 - c), device_id_type=MESH)
            cp.start()
            copies.append(cp)
        for cp in copies:
            cp.wait()

    return pl.pallas_call(
        body, name=name,
        out_shape=tuple(jax.ShapeDtypeStruct(a.shape, a.dtype) for a in arrays),
        in_specs=[ANY] * n, out_specs=(ANY,) * n,
        scratch_shapes=[pltpu.SemaphoreType.DMA((n,)), pltpu.SemaphoreType.DMA((n,))],
    )(*arrays)


N_DEV = 8


def _allreduce_small(pack):
    def body(in_ref, out_ref, buf, send_sems, recv_sems):
        x, y, c = _place()
        me = 4 * x + 2 * y + c
        buf[me] = in_ref[...]

        def copy(j, slot):
            px, py, pc = x ^ (j >> 2), y ^ ((j >> 1) & 1), c ^ (j & 1)
            return pltpu.make_async_remote_copy(
                src_ref=in_ref, dst_ref=buf.at[slot(px, py, pc)], send_sem=send_sems.at[j], recv_sem=recv_sems.at[j],
                device_id=(px, py, pc), device_id_type=MESH)

        for j in range(1, N_DEV):
            copy(j, lambda px, py, pc: me).start()
        for j in range(1, N_DEV):
            landing = copy(j, lambda px, py, pc: 4 * px + 2 * py + pc)
            landing.wait_send()
            landing.wait_recv()
        acc = buf[0]
        for s in range(1, N_DEV):
            acc = acc + buf[s]
        out_ref[...] = acc

    return pl.pallas_call(
        body, name="allreduce_small", out_shape=jax.ShapeDtypeStruct(pack.shape, F32),
        in_specs=[VMEM_FULL], out_specs=VMEM_FULL,
        scratch_shapes=[pltpu.VMEM((N_DEV,) + pack.shape, F32), pltpu.SemaphoreType.DMA((N_DEV,)),
                        pltpu.SemaphoreType.DMA((N_DEV,))],
    )(pack)


ROW_TILE_MAX = 256
BF16_SUBLANES = 16


def _row_tile(rows):
    for tr in range(min(rows, ROW_TILE_MAX), 0, -1):
        if rows % tr == 0 and tr % BF16_SUBLANES == 0:
            return tr
    return rows


def _reduce_own(me, part, recv, dep, name):
    _, r, cols = part.shape
    tr = _row_tile(r)

    def body(me_ref, p_ref, rv_ref, dep_ref, o_ref):
        acc = p_ref[0]
        for k in range(3):
            acc = acc + rv_ref[k].astype(F32)
        o_ref[...] = acc

    return pl.pallas_call(
        body, name=name, out_shape=jax.ShapeDtypeStruct((r, cols), F32),
        grid_spec=pltpu.PrefetchScalarGridSpec(
            num_scalar_prefetch=1, grid=(r // tr,),
            in_specs=[pl.BlockSpec((1, tr, cols), lambda i, me_ref: (me_ref[0], i, 0)),
                      pl.BlockSpec((3, tr, cols), lambda i, me_ref: (0, i, 0)), ANY],
            out_specs=pl.BlockSpec((tr, cols), lambda i, me_ref: (i, 0))),
        compiler_params=_params("arbitrary"),
    )(me, part, recv, dep)


def _adamw(w, ga, gb, m, v, name):
    r, cols = w.shape
    tr = _row_tile(r)
    c1 = 1.0 - ADAM_B1 ** ADAM_STEP
    c2 = 1.0 - ADAM_B2 ** ADAM_STEP

    def body(w_ref, ga_ref, gb_ref, m_ref, v_ref, g_out, d_out, m_out, v_out):
        g = ga_ref[...] + gb_ref[...]
        mn = ADAM_B1 * m_ref[...] + (1.0 - ADAM_B1) * g
        vn = ADAM_B2 * v_ref[...] + (1.0 - ADAM_B2) * (g * g)
        g_out[...] = g
        m_out[...] = mn
        v_out[...] = vn
        d_out[...] = -ADAM_LR * ((mn / c1) / (jnp.sqrt(vn / c2) + ADAM_EPS) + ADAM_WD * w_ref[...])

    tile = pl.BlockSpec((tr, cols), lambda i: (i, 0))
    shape = jax.ShapeDtypeStruct((r, cols), F32)
    return pl.pallas_call(
        body, name=name, grid=(r // tr,), out_shape=(shape,) * 4, in_specs=[tile] * 5, out_specs=(tile,) * 4,
        compiler_params=_params("arbitrary"),
    )(w, ga, gb, m, v)


PACK_COLS = 512


def _to_rows(a):
    flat = a.reshape(-1)
    pad = (-flat.shape[0]) % PACK_COLS
    return jnp.pad(flat, (0, pad)).reshape(-1, PACK_COLS)


def _pack(arrays, extra_rows=0):
    rows = [_to_rows(a) for a in arrays]
    n = sum(r.shape[0] for r in rows) + extra_rows
    pad = (-n) % 8
    return jnp.concatenate(rows + [jnp.zeros((extra_rows + pad, PACK_COLS), F32)], axis=0)


def _unpack(pack, like):
    out, at = [], 0
    for a in like:
        n = -(-a.size // PACK_COLS)
        out.append(pack[at:at + n].reshape(-1)[:a.size].reshape(a.shape))
        at += n
    return out


COL_SHARDED = ("ffn1_w_gate", "ffn1_w_up", "w_in", "ffn2_w_gate", "ffn2_w_up", "w2", "a2", "g2")
ROW_SHARDED = ("ffn1_w_down", "ffn2_w_down", "w_out", "w1", "a1", "g1")
CHUNKED = ("ffn1_w_gate", "ffn1_w_up", "ffn1_w_down", "ffn2_w_gate", "ffn2_w_up", "ffn2_w_down")
WEIGHTS = ("ffn1_norm", "ffn1_w_gate", "ffn1_w_up", "ffn1_w_down", "mix_norm", "w_in", "q_norm", "k_norm",
           "mu_r", "mu_k", "mu_v", "mu_w", "mu_a", "mu_g", "w0", "w1", "w2", "a0", "a1", "a2", "g1", "g2",
           "k_k", "k_a", "r_k", "ln_x_w", "ln_x_b", "w_out", "ffn2_norm", "ffn2_w_gate", "ffn2_w_up", "ffn2_w_down")


W_IN_GROUPS = 7


def _full_from_blocks(name, blocks):
    if name in CHUNKED:
        return blocks
    if name in ROW_SHARDED:
        return blocks.reshape(-1, blocks.shape[-1])
    full = blocks.transpose(1, 0, 2).reshape(blocks.shape[1], -1)
    if name == "w_in":
        return full.reshape(full.shape[0], W_IN_GROUPS, -1).transpose(1, 0, 2)
    return full


def _blocks_from_full(name, full):
    if name in CHUNKED:
        return full
    if name in ROW_SHARDED:
        return full.reshape(N_SHARDS, -1, full.shape[-1])
    if name == "w_in":
        full = full.transpose(1, 0, 2).reshape(full.shape[1], -1)
    return full.reshape(full.shape[0], N_SHARDS, -1).transpose(1, 0, 2)


FFN1_GROUP = ("ffn1_w_gate", "ffn1_w_up", "ffn1_w_down")
MIX_GROUP = ("w_in",) + RWKV_MAT
OUT_GROUP = ("w_out", "ffn2_w_gate", "ffn2_w_up", "ffn2_w_down")
FFN2_GROUP = OUT_GROUP[1:]
LATE_GROUP = ("w_in", "w_out") + RWKV_MAT


class _Exchange:
    def __init__(self, given):
        self.given = given
        first = self._gather_start(FFN1_GROUP, _HalfGatherViews, jnp.zeros(DEP_SHAPE, F32), "gather_ffn1_start")
        self.mix = self._gather_start(MIX_GROUP, _GatherViews, first[4], "gather_mix_start")
        self.out = self._gather_start(OUT_GROUP, _GatherViews, self.mix[4], "gather_out_start")
        self.first_dep = self.out[4]
        halves = _push_wait(first, _HalfGatherViews, (self.first_dep,), "gather_ffn1_wait")
        passed = _push_start([], halves, _ForwardViews, halves[0], "gather_ffn1_pass_start")
        self.first_weights = self._full(FFN1_GROUP, _push_wait(passed, _ForwardViews, (passed[4],), "gather_ffn1_pass_wait"))
        self.parts, self.recv = {}, {}

    def _shards(self, names):
        return [self.given[n][0].astype(BF16) for n in names]

    @staticmethod
    def _full(names, blocks):
        out = {}
        for n, b in zip(names, blocks):
            full = _full_from_blocks(n, b)
            out[n] = full.astype(F32) if n in RWKV_MAT else full
        return out

    def _gather_start(self, names, views, after, name):
        shards = self._shards(names)
        return _push_start(shards, _empty_lands(shards, N_SHARDS, True), views, after, name)

    def mix_weights(self, after):
        return self._full(MIX_GROUP, _push_wait(self.mix, _GatherViews, after, "gather_mix_wait"))

    def out_weights(self, after):
        return self._full(OUT_GROUP, _push_wait(self.out, _GatherViews, after, "gather_out_wait"))

    def _scatter_start(self, grads, name):
        names = tuple(grads)
        parts = [_blocks_from_full(n, grads[n]) for n in names]
        self.parts.update(zip(names, parts))
        lands = [lax.empty((3,) + p.shape[1:], BF16) for p in parts]
        return _push_start([p.astype(BF16) for p in parts], lands, _ScatterViews, parts[0], name)

    def send_ffn2(self, grads):
        self.ffn2 = self._scatter_start(grads, "scatter_ffn2_start")
        return self.ffn2[4]

    def send_mix(self, grads, after):
        self.recv.update(zip(FFN2_GROUP, _push_wait(self.ffn2, _ScatterViews, after, "scatter_ffn2_wait")))
        self.late = self._scatter_start(grads, "scatter_late_start")
        return self.late[4]

    def send_ffn1(self, grads):
        self.ffn1 = self._scatter_start(grads, "scatter_ffn1_start")
        return self.ffn1[4]

    def late_received(self, after):
        self.recv.update(zip(LATE_GROUP, _push_wait(self.late, _ScatterViews, after, "scatter_late_wait")))

    def ffn1_received(self, after):
        self.recv.update(zip(FFN1_GROUP, _push_wait(self.ffn1, _ScatterViews, after, "scatter_ffn1_wait")))


def kernel(
        x, ffn1_norm, ffn1_w_gate, ffn1_w_up, ffn1_w_down, mix_norm, w_in, q_norm, k_norm, mu_r, mu_k, mu_v, mu_w,
        mu_a, mu_g, w0, w1, w2, a0, a1, a2, g1, g2, k_k, k_a, r_k, ln_x_w, ln_x_b, w_out, ffn2_norm, ffn2_w_gate,
        ffn2_w_up, ffn2_w_down, loss_target, m_ffn1_norm, m_ffn1_w_gate, m_ffn1_w_up, m_ffn1_w_down, m_mix_norm,
        m_w_in, m_q_norm, m_k_norm, m_mu_r, m_mu_k, m_mu_v, m_mu_w, m_mu_a, m_mu_g, m_w0, m_w1, m_w2, m_a0, m_a1,
        m_a2, m_g1, m_g2, m_k_k, m_k_a, m_r_k, m_ln_x_w, m_ln_x_b, m_w_out, m_ffn2_norm, m_ffn2_w_gate, m_ffn2_w_up,
        m_ffn2_w_down, v_ffn1_norm, v_ffn1_w_gate, v_ffn1_w_up, v_ffn1_w_down, v_mix_norm, v_w_in, v_q_norm, v_k_norm,
        v_mu_r, v_mu_k, v_mu_v, v_mu_w, v_mu_a, v_mu_g, v_w0, v_w1, v_w2, v_a0, v_a1, v_a2, v_g1, v_g2, v_k_k, v_k_a,
        v_r_k, v_ln_x_w, v_ln_x_b, v_w_out, v_ffn2_norm, v_ffn2_w_gate, v_ffn2_w_up, v_ffn2_w_down):
    given = dict(locals())
    sharded = COL_SHARDED + ROW_SHARDED
    sharded = tuple(n for n in WEIGHTS if n in sharded)
    small = tuple(n for n in WEIGHTS if n not in sharded)

    ex = _Exchange(given)
    w = {n: given[n] for n in small}
    w.update(ex.first_weights)
    loss, dx, g = _local_step(x[0], loss_target[0], w, ex)
    dep = ex.send_ffn1({n: g[n] for n in FFN1_GROUP})

    me = (2 * lax.axis_index("x") + lax.axis_index("y")).astype(jnp.int32).reshape(1)
    out = {}

    def settle(names, dep, tag):
        mine = []
        for n in names:
            p, rv = ex.parts[n], ex.recv[n]
            p2 = p.reshape(N_SHARDS, -1, p.shape[-1])
            mine.append(_reduce_own(me, p2, rv.reshape(3, -1, rv.shape[-1]), dep, f"reduce_{n}"))
        theirs = _sibling_swap(mine, f"sibling_swap_{tag}")
        for n, a, b in zip(names, mine, theirs):
            shape = given[n].shape
            two_d = (-1, shape[-1])
            res = _adamw(given[n].reshape(two_d), a, b, given["m_" + n].reshape(two_d), given["v_" + n].reshape(two_d), f"adamw_{n}")
            out[n] = [r.reshape(shape) for r in res]
        return tuple(out[n][1] for n in names)

    ex.late_received((dep,))
    last = settle(tuple(n for n in sharded if n not in FFN1_GROUP), dep, "rest")

    gpack = _pack([g[n] for n in small], extra_rows=1)
    n_rows = sum(-(-given[n].size // PACK_COLS) for n in small)
    gpack = gpack.at[n_rows, :loss.shape[1]].set(loss[0])
    gsum = _allreduce_small(gpack)
    res = _adamw(_pack([given[n] for n in small], 1), gsum, jnp.zeros_like(gsum), _pack([given["m_" + n] for n in small], 1),
                 _pack([given["v_" + n] for n in small], 1), "adamw_small")
    like = [given[n] for n in small]
    for j, r in enumerate(res):
        for n, a in zip(small, _unpack(r, like)):
            out.setdefault(n, [None] * 4)[j] = a
    total_loss = gsum[n_rows, 0]

    ex.ffn1_received((*last, res[1]))
    settle(FFN1_GROUP, jnp.zeros(DEP_SHAPE, F32), "ffn1")
    return (total_loss, dx[None], *[out[n][0] for n in WEIGHTS], *[out[n][1] for n in WEIGHTS],
            *[out[n][2] for n in WEIGHTS], *[out[n][3] for n in WEIGHTS])
```

```python
import functools

import jax
import jax.numpy as jnp
from jax import lax
from jax.experimental import pallas as pl
from jax.experimental.pallas import tpu as pltpu

F32 = jnp.float32
BF16 = jnp.bfloat16
MESH = pl.DeviceIdType.MESH

RMS_EPS = 1e-6
GN_EPS = 64e-5
NEG_INF = -1e30
FFN_RESIDUAL = 0.5
HEAD_DIM = 64
ATT_BLOCK = 128
DILATIONS = (1, 4, 16)
SCAN_CHUNK = 64
TOKEN_TILE = 256

ADAM_LR = 0.001
ADAM_B1 = 0.9
ADAM_B2 = 0.999
ADAM_EPS = 1e-08
ADAM_WD = 0.01
ADAM_STEP = 10

VMEM_FULL = pl.BlockSpec(memory_space=pltpu.VMEM)
ANY = pl.BlockSpec(memory_space=pl.ANY)


VMEM_LIMIT = 56 * 1024 * 1024


def _params(*sem):
    return pltpu.CompilerParams(dimension_semantics=sem, vmem_limit_bytes=VMEM_LIMIT)


def _dot(a, b, dims):
    return lax.dot_general(a.astype(BF16), b.astype(BF16), (dims, ((), ())), preferred_element_type=F32)


def _dot_nn(a, b):
    return _dot(a, b, ((1,), (0,)))


def _dot_nt(a, b):
    return _dot(a, b, ((1,), (1,)))


def _dot_tn(a, b):
    return _dot(a, b, ((0,), (0,)))


@jax.custom_vjp
def _mm(a, b):
    return _dot_nn(a, b)


def _mm_fwd(a, b):
    return _dot_nn(a, b), (a, b)


def _mm_bwd(res, g):
    a, b = res
    return _dot_nt(g, b).astype(a.dtype), _dot_tn(a, g).astype(b.dtype)


_mm.defvjp(_mm_fwd, _mm_bwd)


def _bdot(a, b, ca, cb):
    return lax.dot_general(a.astype(BF16), b.astype(BF16), (((ca,), (cb,)), ((0,), (0,))), preferred_element_type=F32)


@jax.custom_vjp
def _bmm_nt(a, b):
    return _bdot(a, b, 2, 2)


def _bmm_nt_fwd(a, b):
    return _bdot(a, b, 2, 2), (a, b)


def _bmm_nt_bwd(res, g):
    a, b = res
    return _bdot(g, b, 2, 1), _bdot(g, a, 1, 1)


_bmm_nt.defvjp(_bmm_nt_fwd, _bmm_nt_bwd)


@jax.custom_vjp
def _bmm_nn(a, b):
    return _bdot(a, b, 2, 1)


def _bmm_nn_fwd(a, b):
    return _bdot(a, b, 2, 1), (a, b)


def _bmm_nn_bwd(res, g):
    a, b = res
    return _bdot(g, b, 2, 2), _bdot(a, g, 1, 1)


_bmm_nn.defvjp(_bmm_nn_fwd, _bmm_nn_bwd)


@jax.custom_vjp
def _bmm_tn(a, b):
    return _bdot(a, b, 1, 1)


def _bmm_tn_fwd(a, b):
    return _bdot(a, b, 1, 1), (a, b)


def _bmm_tn_bwd(res, g):
    a, b = res
    return _bdot(b, g, 2, 2), _bdot(a, g, 2, 1)


_bmm_tn.defvjp(_bmm_tn_fwd, _bmm_tn_bwd)


def _hdot(a, b, ca, cb):
    return lax.dot_general(a, b, (((ca,), (cb,)), ((0,), (0,))), precision=lax.Precision.HIGH, preferred_element_type=F32)


def _sigmoid(x):
    return 1.0 / (1.0 + jnp.exp(-x))


def _rms(x):
    return lax.rsqrt(jnp.mean(x * x, axis=-1, keepdims=True) + RMS_EPS)


def _ffn_fwd(x, norm, wg, wu, wd, dep, name):
    t, d = x.shape
    nc = wg.shape[0]
    tm = TOKEN_TILE

    def body(x_ref, n_ref, wg_ref, wu_ref, wd_ref, dep_ref, o_ref):
        xv = x_ref[...]
        h = (xv * _rms(xv) * n_ref[...]).astype(BF16)
        acc = jnp.zeros((tm, d), F32)
        for c in range(nc):
            g = _dot_nt(h, wg_ref[c])
            u = _dot_nt(h, wu_ref[c])
            a = (g * _sigmoid(g) * u).astype(BF16)
            acc = acc + jnp.dot(a, wd_ref[c], preferred_element_type=F32)
        o_ref[...] = xv + FFN_RESIDUAL * acc

    tile = pl.BlockSpec((tm, d), lambda i: (i, 0))
    return pl.pallas_call(
        body, name=name, grid=(t // tm,), out_shape=jax.ShapeDtypeStruct((t, d), F32),
        in_specs=[tile, pl.BlockSpec((1, d), lambda i: (0, 0)), VMEM_FULL, VMEM_FULL, VMEM_FULL, ANY],
        out_specs=tile, compiler_params=_params("arbitrary"),
    )(x, norm, wg, wu, wd, dep)


def _rmsnorm_bwd(xv, gain, dh):
    rs = _rms(xv)
    xn = xv * rs
    dxn = dh * gain
    dx = rs * (dxn - xn * jnp.mean(dxn * xn, axis=-1, keepdims=True))
    return dx, jnp.sum(dh * xn, axis=0, keepdims=True)


def _ffn_bwd(x, norm, wg, wu, wd, dy, dep, name):
    t, d = x.shape
    nc, fc, _ = wg.shape
    tm = TOKEN_TILE
    nt = t // tm

    def body(x_ref, n_ref, wg_ref, wu_ref, wd_ref, dy_ref, dep_ref, dx_ref, dn_ref, dwg_ref, dwu_ref, dwd_ref, dh_ref):
        c, i = pl.program_id(0), pl.program_id(1)
        rows = pl.ds(pl.multiple_of(i * tm, tm), tm)
        xv = x_ref[...]
        gain = n_ref[...]
        h = (xv * _rms(xv) * gain).astype(BF16)
        dy = dy_ref[...]
        dyb = (FFN_RESIDUAL * dy).astype(BF16)
        g = _dot_nt(h, wg_ref[0])
        u = _dot_nt(h, wu_ref[0])
        sg = _sigmoid(g)
        s = g * sg
        a = (s * u).astype(BF16)
        da = _dot_nt(dyb, wd_ref[0])
        dub = (da * s).astype(BF16)
        dgb = (da * u * (sg * (1.0 + g * (1.0 - sg)))).astype(BF16)
        dwd_c = _dot_tn(a, dyb)
        dwg_c = _dot_tn(dgb, h)
        dwu_c = _dot_tn(dub, h)
        dh_c = _dot_nn(dgb, wg_ref[0]) + _dot_nn(dub, wu_ref[0])

        @pl.when(i == 0)
        def _():
            dwd_ref[0] = dwd_c
            dwg_ref[0] = dwg_c
            dwu_ref[0] = dwu_c

        @pl.when(i > 0)
        def _():
            dwd_ref[0] += dwd_c
            dwg_ref[0] += dwg_c
            dwu_ref[0] += dwu_c

        @pl.when(c == 0)
        def _():
            dh_ref[rows, :] = dh_c

        @pl.when(c > 0)
        def _():
            dh_ref[rows, :] += dh_c

        @pl.when(c == nc - 1)
        def _():
            dx, dn = _rmsnorm_bwd(xv, gain, dh_ref[rows, :])
            dx_ref[...] = dx + dy

            @pl.when(i == 0)
            def _():
                dn_ref[...] = dn

            @pl.when(i > 0)
            def _():
                dn_ref[...] += dn

    tile = pl.BlockSpec((tm, d), lambda c, i: (i, 0))
    row = pl.BlockSpec((1, d), lambda c, i: (0, 0))
    wrow = pl.BlockSpec((1, fc, d), lambda c, i: (c, 0, 0))
    last = pl.BlockSpec((tm, d), lambda c, i: (jnp.where(c == nc - 1, i, 0), 0))
    return pl.pallas_call(
        body, name=name, grid=(nc, nt),
        out_shape=(jax.ShapeDtypeStruct((t, d), F32), jax.ShapeDtypeStruct((1, d), F32),
                   jax.ShapeDtypeStruct(wg.shape, F32), jax.ShapeDtypeStruct(wu.shape, F32),
                   jax.ShapeDtypeStruct(wd.shape, F32)),
        in_specs=[tile, row, wrow, wrow, wrow, tile, ANY],
        out_specs=(last, row, wrow, wrow, wrow),
        scratch_shapes=[pltpu.VMEM((t, d), F32)],
        compiler_params=_params("arbitrary", "arbitrary"),
    )(x, norm, wg, wu, wd, dy, dep)


def _store_heads(ref, v):
    for h in range(ref.shape[0]):
        ref[h] = v[:, h * HEAD_DIM:(h + 1) * HEAD_DIM]


def _load_heads(ref):
    return jnp.concatenate([ref[h] for h in range(ref.shape[0])], axis=-1)


N_HEAD_GROUPS = 3


def _proj_fwd(x, norm, w):
    t, d = x.shape
    ng, _, c = w.shape
    nh = c // HEAD_DIM
    tm = TOKEN_TILE

    def body(x_ref, n_ref, w_ref, q_ref, k_ref, v_ref, cur_ref):
        xv = x_ref[...]
        h = (xv * _rms(xv) * n_ref[...]).astype(BF16)
        for m, ref in enumerate((q_ref, k_ref, v_ref)):
            _store_heads(ref, jnp.dot(h, w_ref[m], preferred_element_type=F32))
        for m in range(N_HEAD_GROUPS, ng):
            j = m - N_HEAD_GROUPS
            cur_ref[:, j * c:(j + 1) * c] = jnp.dot(h, w_ref[m], preferred_element_type=F32)

    heads = pl.BlockSpec((nh, tm, HEAD_DIM), lambda i: (0, i, 0))
    hshape = jax.ShapeDtypeStruct((nh, t, HEAD_DIM), F32)
    wide = (ng - N_HEAD_GROUPS) * c
    return pl.pallas_call(
        body, name="proj_fwd", grid=(t // tm,),
        out_shape=(hshape, hshape, hshape, jax.ShapeDtypeStruct((t, wide), F32)),
        in_specs=[pl.BlockSpec((tm, d), lambda i: (i, 0)), pl.BlockSpec((1, d), lambda i: (0, 0)), VMEM_FULL],
        out_specs=(heads, heads, heads, pl.BlockSpec((tm, wide), lambda i: (i, 0))),
        compiler_params=_params("arbitrary"),
    )(x, norm, w)


def _proj_bwd(x, norm, w, dq, dk, dv, dcur, dres):
    t, d = x.shape
    ng, _, c = w.shape
    nh = c // HEAD_DIM
    tm = TOKEN_TILE

    def body(x_ref, n_ref, w_ref, dq_ref, dk_ref, dv_ref, dcur_ref, dres_ref, dx_ref, dn_ref, dw_ref):
        i = pl.program_id(0)

        @pl.when(i == 0)
        def _():
            dw_ref[...] = jnp.zeros_like(dw_ref)
            dn_ref[...] = jnp.zeros_like(dn_ref)

        xv = x_ref[...]
        gain = n_ref[...]
        h = (xv * _rms(xv) * gain).astype(BF16)
        dh = jnp.zeros((tm, d), F32)
        for m in range(ng):
            j = m - N_HEAD_GROUPS
            dp = _load_heads((dq_ref, dk_ref, dv_ref)[m]) if j < 0 else dcur_ref[:, j * c:(j + 1) * c]
            dp = dp.astype(BF16)
            dw_ref[m] += _dot_tn(h, dp)
            dh = dh + _dot_nt(dp, w_ref[m])
        dx, dn = _rmsnorm_bwd(xv, gain, dh)
        dx_ref[...] = dx + dres_ref[...]
        dn_ref[...] += dn

    tile = pl.BlockSpec((tm, d), lambda i: (i, 0))
    row = pl.BlockSpec((1, d), lambda i: (0, 0))
    heads = pl.BlockSpec((nh, tm, HEAD_DIM), lambda i: (0, i, 0))
    wide = (ng - N_HEAD_GROUPS) * c
    return pl.pallas_call(
        body, name="proj_bwd", grid=(t // tm,),
        out_shape=(jax.ShapeDtypeStruct((t, d), F32), jax.ShapeDtypeStruct((1, d), F32),
                   jax.ShapeDtypeStruct(w.shape, F32)),
        in_specs=[tile, row, VMEM_FULL, heads, heads, heads, pl.BlockSpec((tm, wide), lambda i: (i, 0)), tile],
        out_specs=(tile, row, VMEM_FULL),
        compiler_params=_params("arbitrary"),
    )(x, norm, w, dq, dk, dv, dcur, dres)


def _mixout_fwd(x, att, opg, gate, w):
    t, d = x.shape
    nh = att.shape[0]
    half = gate.shape[1]
    tm = TOKEN_TILE

    def body(x_ref, att_ref, opg_ref, g_ref, w_ref, o_ref):
        mix = jnp.concatenate([_load_heads(att_ref), _load_heads(opg_ref) * g_ref[...]], axis=-1).astype(BF16)
        o_ref[...] = x_ref[...] + jnp.dot(mix, w_ref[...], preferred_element_type=F32)

    tile = pl.BlockSpec((tm, d), lambda i: (i, 0))
    htile = pl.BlockSpec((tm, half), lambda i: (i, 0))
    heads = pl.BlockSpec((nh, tm, HEAD_DIM), lambda i: (0, i, 0))
    return pl.pallas_call(
        body, name="mixout_fwd", grid=(t // tm,), out_shape=jax.ShapeDtypeStruct((t, d), F32),
        in_specs=[tile, heads, heads, htile, VMEM_FULL], out_specs=tile, compiler_params=_params("arbitrary"),
    )(x, att, opg, gate, w)


def _mixout_bwd(att, opg, gate, w, dy, dep):
    nh, t, _ = att.shape
    half = gate.shape[1]
    d = dy.shape[1]
    tm = TOKEN_TILE

    def body(att_ref, opg_ref, g_ref, w_ref, dy_ref, dep_ref, datt_ref, dopg_ref, dg_ref, dw_ref):
        i = pl.program_id(0)
        opg_v, g_v = _load_heads(opg_ref), g_ref[...]
        mix = jnp.concatenate([_load_heads(att_ref), opg_v * g_v], axis=-1).astype(BF16)
        dyb = dy_ref[...].astype(BF16)
        dmix = _dot_nt(dyb, w_ref[...])
        dw = _dot_tn(mix, dyb)
        _store_heads(datt_ref, dmix[:, :half])
        drw = dmix[:, half:]
        _store_heads(dopg_ref, drw * g_v)
        dg_ref[...] = drw * opg_v

        @pl.when(i == 0)
        def _():
            dw_ref[...] = dw

        @pl.when(i > 0)
        def _():
            dw_ref[...] += dw

    tile = pl.BlockSpec((tm, d), lambda i: (i, 0))
    htile = pl.BlockSpec((tm, half), lambda i: (i, 0))
    heads = pl.BlockSpec((nh, tm, HEAD_DIM), lambda i: (0, i, 0))
    hshape = jax.ShapeDtypeStruct((nh, t, HEAD_DIM), F32)
    return pl.pallas_call(
        body, name="mixout_bwd", grid=(t // tm,),
        out_shape=(hshape, hshape, jax.ShapeDtypeStruct((t, half), F32), jax.ShapeDtypeStruct(w.shape, F32)),
        in_specs=[heads, heads, htile, VMEM_FULL, tile, ANY],
        out_specs=(heads, heads, htile, pl.BlockSpec(w.shape, lambda i: (0, 0))),
        compiler_params=_params("arbitrary"),
    )(att, opg, gate, w, dy, dep)


def _loss_head(y, target):
    t, d = y.shape
    tm = TOKEN_TILE

    def body(y_ref, t_ref, dy_ref, loss_ref):
        i = pl.program_id(0)
        err = y_ref[...] - t_ref[...]
        dy_ref[...] = err * (1.0 / d)
        part = 0.5 * jnp.sum(jnp.mean(err * err, axis=-1, keepdims=True), axis=0, keepdims=True)

        @pl.when(i == 0)
        def _():
            loss_ref[...] = jnp.zeros_like(loss_ref)

        loss_ref[...] += jnp.broadcast_to(part, loss_ref.shape)

    tile = pl.BlockSpec((tm, d), lambda i: (i, 0))
    return pl.pallas_call(
        body, name="loss_head", grid=(t // tm,),
        out_shape=(jax.ShapeDtypeStruct((t, d), F32), jax.ShapeDtypeStruct((1, 128), F32)),
        in_specs=[tile, tile], out_specs=(tile, pl.BlockSpec((1, 128), lambda i: (0, 0))),
        compiler_params=_params("arbitrary"),
    )(y, target)


def _att_pattern(q, k, v, qn, kn, nb):
    g, blk, _ = q.shape
    qh = q * _rms(q) * qn
    kh = k * _rms(k) * kn
    scale = HEAD_DIM ** -0.5
    qi = lax.broadcasted_iota(jnp.int32, (blk, blk), 0)
    kj = lax.broadcasted_iota(jnp.int32, (blk, blk), 1)
    sc = jnp.where(kj <= qi, _bmm_nt(qh, kh) * scale, NEG_INF)
    top = jnp.max(sc, axis=-1, keepdims=True)
    if nb > 1:
        khp = jnp.concatenate([kh[:1], kh[:-1]], axis=0)
        vp = jnp.concatenate([v[:1], v[:-1]], axis=0)
        has_prev = lax.broadcasted_iota(jnp.int32, (g, 1, 1), 0) % nb != 0
        sp = jnp.where((kj >= qi) & has_prev, _bmm_nt(qh, khp) * scale, NEG_INF)
        top = jnp.maximum(top, jnp.max(sp, axis=-1, keepdims=True))
    m = lax.stop_gradient(top)
    pc = jnp.exp(sc - m)
    den = jnp.sum(pc, axis=-1, keepdims=True)
    acc = _bmm_nn(pc, v)
    if nb > 1:
        pp = jnp.exp(sp - m)
        den = den + jnp.sum(pp, axis=-1, keepdims=True)
        acc = acc + _bmm_nn(pp, vp)
    o = acc / den
    return o, jnp.broadcast_to(m + jnp.log(den), o.shape)


def _pattern_rows(t, dil):
    nb = t // (ATT_BLOCK * dil)
    starts = [n * ATT_BLOCK * dil + r for r in range(dil) for n in range(nb)]
    return [pl.ds(s, ATT_BLOCK, stride=dil) if dil > 1 else pl.ds(s, ATT_BLOCK) for s in starts], nb


def _take(ref, rows):
    return jnp.stack([ref[0, r, :] for r in rows])


def _put(ref, rows, val):
    for g, r in enumerate(rows):
        ref[0, r, :] = val[g]


def _put_add(ref, rows, val):
    for g, r in enumerate(rows):
        ref[0, r, :] += val[g]


def _merge_fn(o1, o2, o3, l1, l2, l3):
    m = lax.stop_gradient(jnp.maximum(jnp.maximum(l1, l2), l3))
    e1, e2, e3 = jnp.exp(l1 - m), jnp.exp(l2 - m), jnp.exp(l3 - m)
    return (e1 * o1 + e2 * o2 + e3 * o3) / (e1 + e2 + e3)


def _att_head_specs(t):
    head = pl.BlockSpec((1, t, HEAD_DIM), lambda h: (h, 0, 0))
    gain = pl.BlockSpec((1, 1, HEAD_DIM), lambda h: (0, 0, 0))
    return head, gain


def _att_fwd(q, k, v, qn, kn):
    nh, t, dh = q.shape
    head, gain = _att_head_specs(t)

    def body(q_ref, k_ref, v_ref, qn_ref, kn_ref, att_ref, *saved):
        o_refs, l_refs = saved[:3], saved[3:]
        for p, dil in enumerate(DILATIONS):
            rows, nb = _pattern_rows(t, dil)
            o, lse = _att_pattern(_take(q_ref, rows), _take(k_ref, rows), _take(v_ref, rows), qn_ref[...], kn_ref[...], nb)
            _put(o_refs[p], rows, o)
            _put(l_refs[p], rows, lse)

        def merge(j, carry):
            rows = pl.ds(pl.multiple_of(j * ATT_BLOCK, ATT_BLOCK), ATT_BLOCK)
            att_ref[0, rows, :] = _merge_fn(*[r[0, rows, :] for r in saved])
            return carry

        lax.fori_loop(0, t // ATT_BLOCK, merge, 0)

    return pl.pallas_call(
        body, name="att_fwd", grid=(nh,), out_shape=(jax.ShapeDtypeStruct(q.shape, F32),) * 7,
        in_specs=[head, head, head, gain, gain], out_specs=(head,) * 7, compiler_params=_params("arbitrary"),
    )(q, k, v, qn, kn)


def _att_bwd(q, k, v, qn, kn, saved, datt):
    nh, t, dh = q.shape
    head, gain = _att_head_specs(t)

    def body(q_ref, k_ref, v_ref, qn_ref, kn_ref, o1, o2, o3, l1, l2, l3, datt_ref,
             dq_ref, dk_ref, dv_ref, dqn_ref, dkn_ref):
        for ref in (dq_ref, dk_ref, dv_ref):
            ref[...] = jnp.zeros_like(ref)

        @pl.when(pl.program_id(0) == 0)
        def _():
            dqn_ref[...] = jnp.zeros_like(dqn_ref)
            dkn_ref[...] = jnp.zeros_like(dkn_ref)

        for p, dil in enumerate(DILATIONS):
            rows, nb = _pattern_rows(t, dil)
            _, merge_vjp = jax.vjp(_merge_fn, *[_take(r, rows) for r in (o1, o2, o3, l1, l2, l3)])
            cts = merge_vjp(_take(datt_ref, rows))
            _, pattern_vjp = jax.vjp(functools.partial(_att_pattern, nb=nb), _take(q_ref, rows), _take(k_ref, rows),
                                     _take(v_ref, rows), qn_ref[...], kn_ref[...])
            dq, dk, dv, dgq, dgk = pattern_vjp((cts[p], cts[3 + p]))
            _put_add(dq_ref, rows, dq)
            _put_add(dk_ref, rows, dk)
            _put_add(dv_ref, rows, dv)
            dqn_ref[...] += dgq
            dkn_ref[...] += dgk

    hshape = jax.ShapeDtypeStruct(q.shape, F32)
    gshape = jax.ShapeDtypeStruct((1, 1, dh), F32)
    return pl.pallas_call(
        body, name="att_bwd", grid=(nh,), out_shape=(hshape, hshape, hshape, gshape, gshape),
        in_specs=[head, head, head, gain, gain] + [head] * 7, out_specs=(head, head, head, gain, gain),
        compiler_params=_params("arbitrary"),
    )(q, k, v, qn, kn, *saved, datt)


RWKV_VEC = ("mu_r", "mu_k", "mu_v", "mu_w", "mu_a", "mu_g", "w0", "a0", "k_k", "k_a")
RWKV_MAT = ("w1", "w2", "a1", "a2", "g1", "g2")


def _rwkv_pre_fn(cur, prev, vec, w1, w2, a1, a2, g1, g2):
    c = cur.shape[1] // 4
    mu_r, mu_k, mu_v, mu_w, mu_a, mu_g, w0, a0, k_k, k_a = (vec[j:j + 1] for j in range(10))

    def lerp(j, mu):
        xc, xp = cur[:, j * c:(j + 1) * c], prev[:, j * c:(j + 1) * c]
        return xc + (xp - xc) * mu

    r, k, v = lerp(0, mu_r), lerp(1, mu_k), lerp(2, mu_v)
    cw, ca, cg = lerp(3, mu_w), lerp(3, mu_a), lerp(3, mu_g)
    z = w0 + _mm(jnp.tanh(_mm(cw, w1)), w2)
    w_log = jnp.minimum(z, 0.0) - jnp.log(1.0 + jnp.exp(-jnp.abs(z))) - 0.5
    lw = -jnp.exp(w_log)
    a = _sigmoid(a0 + _mm(_mm(ca, a1), a2))
    gate = _mm(_sigmoid(_mm(cg, g1)), g2)
    kkraw = k * k_k
    kmod = k * (1.0 + (a - 1.0) * k_a)
    return r, lw, kmod, v, kkraw, a, gate


HALO_ROWS = 8


def _rwkv_pre_specs(c, mats, tile_of):
    tm = TOKEN_TILE
    nh = c // HEAD_DIM
    wide = pl.BlockSpec((tm, 4 * c), lambda j: (tile_of(j), 0))
    halo = pl.BlockSpec((HALO_ROWS, 4 * c), lambda j: (jnp.maximum(tile_of(j) * (tm // HALO_ROWS) - 1, 0), 0))
    one = pl.BlockSpec((tm, c), lambda j: (tile_of(j), 0))
    heads = pl.BlockSpec((nh, tm, HEAD_DIM), lambda j: (0, tile_of(j), 0))
    vec = pl.BlockSpec((10, c), lambda j: (0, 0))
    mspecs = [pl.BlockSpec(m.shape, lambda j: (0, 0)) for m in mats]
    return wide, halo, one, heads, vec, mspecs


def _previous_rows(cur, halo, tile):
    first = jnp.where(tile > 0, halo[HALO_ROWS - 1:HALO_ROWS], 0.0)
    rows = lax.broadcasted_iota(jnp.int32, cur.shape, 0)
    return jnp.where(rows == 0, first, pltpu.roll(cur, 1, axis=0))


def _rwkv_pre_fwd(cur, vec, mats):
    t, c4 = cur.shape
    c = c4 // 4
    wide, halo, one, heads, vspec, mspecs = _rwkv_pre_specs(c, mats, lambda j: j)

    def body(cur_ref, halo_ref, vec_ref, *rest):
        mrefs, outs = rest[:6], rest[6:]
        cur_v = cur_ref[...]
        prev = _previous_rows(cur_v, halo_ref[...], pl.program_id(0))
        vals = _rwkv_pre_fn(cur_v, prev, vec_ref[...], *(m[...] for m in mrefs))
        for ref, val in zip(outs[:6], vals[:6]):
            _store_heads(ref, val)
        outs[6][...] = vals[6]

    hshape = jax.ShapeDtypeStruct((c // HEAD_DIM, t, HEAD_DIM), F32)
    return pl.pallas_call(
        body, name="rwkv_pre_fwd", grid=(t // TOKEN_TILE,), out_shape=(hshape,) * 6 + (jax.ShapeDtypeStruct((t, c), F32),),
        in_specs=[wide, halo, vspec] + mspecs, out_specs=(heads,) * 6 + (one,), compiler_params=_params("arbitrary"),
    )(cur, cur, vec, *mats)


def _rwkv_pre_bwd(cur, vec, mats, cts, dgate):
    t, c4 = cur.shape
    c = c4 // 4
    tm = TOKEN_TILE
    nt = t // tm
    wide, halo, one, heads, vspec, mspecs = _rwkv_pre_specs(c, mats, lambda j: nt - 1 - j)

    def body(cur_ref, halo_ref, vec_ref, *rest):
        mrefs, ctrefs, dgate_ref, outs, carry_ref = rest[:6], rest[6:12], rest[12], rest[13:-1], rest[-1]
        j = pl.program_id(0)

        @pl.when(j == 0)
        def _():
            carry_ref[...] = jnp.zeros_like(carry_ref)
            for ref in outs[1:]:
                ref[...] = jnp.zeros_like(ref)

        cur_v = cur_ref[...]
        prev = _previous_rows(cur_v, halo_ref[...], nt - 1 - j)
        _, vjp = jax.vjp(_rwkv_pre_fn, cur_v, prev, vec_ref[...], *(m[...] for m in mrefs))
        grads = vjp(tuple(_load_heads(r) for r in ctrefs) + (dgate_ref[...],))
        dprev = grads[1]
        rows = lax.broadcasted_iota(jnp.int32, dprev.shape, 0)
        outs[0][...] = grads[0] + jnp.where(rows == tm - 1, carry_ref[0:1], pltpu.roll(dprev, tm - 1, axis=0))
        carry_ref[0:1] = dprev[0:1]
        for ref, val in zip(outs[1:], grads[2:]):
            ref[...] += val

    return pl.pallas_call(
        body, name="rwkv_pre_bwd", grid=(nt,),
        out_shape=(jax.ShapeDtypeStruct(cur.shape, F32), jax.ShapeDtypeStruct(vec.shape, F32))
        + tuple(jax.ShapeDtypeStruct(m.shape, F32) for m in mats),
        in_specs=[wide, halo, vspec] + mspecs + [heads] * 6 + [one], out_specs=(wide, vspec) + tuple(mspecs),
        scratch_shapes=[pltpu.VMEM((HALO_ROWS, c4), F32)], compiler_params=_params("arbitrary"),
    )(cur, cur, vec, *mats, *cts, dgate)


def _scan_chunk_fn(h0, r, lw, k, v, kkraw, a, rk, lnw, lnb):
    n = r.shape[1]
    nrm = jnp.sqrt(jnp.sum(kkraw * kkraw, axis=-1, keepdims=True))
    kk = kkraw / jnp.maximum(nrm, 1e-12)
    av, bv = -kk, kk * a
    ti = lax.broadcasted_iota(jnp.int32, (n, n), 0)
    si = lax.broadcasted_iota(jnp.int32, (n, n), 1)
    incl, strict = ti >= si, ti > si
    ones = jnp.broadcast_to(incl.astype(F32)[None], (r.shape[0], n, n))
    cum = _hdot(ones, lw, 2, 1)
    at, rt = av * jnp.exp(cum - lw), r * jnp.exp(cum)
    inv = jnp.exp(-cum)
    bt, kt = bv * inv, k * inv
    lab = jnp.where(strict, _hdot(at, bt, 2, 2), 0.0)
    lak = jnp.where(strict, _hdot(at, kt, 2, 2), 0.0)
    rb = jnp.where(incl, _hdot(rt, bt, 2, 2), 0.0)
    rkm = jnp.where(incl, _hdot(rt, kt, 2, 2), 0.0)
    u = _bmm_nn(at, h0) + _bmm_nn(lak, v)
    p = lab
    m = 1
    while m < n:
        u = u + _bmm_nn(p, u)
        m *= 2
        if m < n:
            p = _bmm_nn(p, p)
    y = _bmm_nn(rt, h0) + _bmm_nn(rb, u) + _bmm_nn(rkm, v)
    last = jnp.exp(jnp.sum(lw, axis=1, keepdims=True))
    h1 = jnp.swapaxes(last, 1, 2) * (h0 + _bmm_tn(bt, u) + _bmm_tn(kt, v))
    mean = jnp.mean(y, axis=-1, keepdims=True)
    yc = y - mean
    var = jnp.mean(yc * yc, axis=-1, keepdims=True)
    yn = yc * lax.rsqrt(var + GN_EPS) * lnw + lnb
    bonus = jnp.sum(r * k * rk, axis=-1, keepdims=True) * v
    return yn + bonus, h1


def _scan_specs(h, t, dh, rev):
    n = SCAN_CHUNK
    nc = t // n
    pos = (lambda c: (0, nc - 1 - c, 0)) if rev else (lambda c: (0, c, 0))
    st = (lambda c: (nc - 1 - c, 0, 0, 0)) if rev else (lambda c: (c, 0, 0, 0))
    seq = pl.BlockSpec((h, n, dh), pos)
    par = pl.BlockSpec((h, 1, dh), lambda c: (0, 0, 0))
    state = pl.BlockSpec((1, h, dh, dh), st)
    return seq, par, state


def _scan_fwd(seqs, pars):
    h, t, dh = seqs[0].shape
    nc = t // SCAN_CHUNK
    seq, par, state = _scan_specs(h, t, dh, False)

    def body(r, lw, k, v, kkraw, a, rk, lnw, lnb, o_ref, st_ref, h_ref):
        @pl.when(pl.program_id(0) == 0)
        def _():
            h_ref[...] = jnp.zeros_like(h_ref)

        h0 = h_ref[...]
        st_ref[0] = h0
        o, h1 = _scan_chunk_fn(h0, r[...], lw[...], k[...], v[...], kkraw[...], a[...], rk[...], lnw[...], lnb[...])
        o_ref[...] = o
        h_ref[...] = h1

    return pl.pallas_call(
        body, name="rwkv_scan_fwd", grid=(nc,),
        out_shape=(jax.ShapeDtypeStruct((h, t, dh), F32), jax.ShapeDtypeStruct((nc, h, dh, dh), F32)),
        in_specs=[seq] * 6 + [par] * 3, out_specs=(seq, state),
        scratch_shapes=[pltpu.VMEM((h, dh, dh), F32)], compiler_params=_params("arbitrary"),
    )(*seqs, *pars)


def _scan_bwd(seqs, pars, states, do):
    h, t, dh = seqs[0].shape
    nc = t // SCAN_CHUNK
    seq, par, state = _scan_specs(h, t, dh, True)

    def body(r, lw, k, v, kkraw, a, rk, lnw, lnb, st_ref, do_ref, *rest):
        douts, dpars, dh_ref = rest[:6], rest[6:9], rest[9]
        first = pl.program_id(0) == 0

        @pl.when(first)
        def _():
            dh_ref[...] = jnp.zeros_like(dh_ref)

        _, vjp = jax.vjp(_scan_chunk_fn, st_ref[0], r[...], lw[...], k[...], v[...], kkraw[...], a[...],
                         rk[...], lnw[...], lnb[...])
        grads = vjp((do_ref[...], dh_ref[...]))
        dh_ref[...] = grads[0]
        for ref, val in zip(douts, grads[1:7]):
            ref[...] = val

        @pl.when(first)
        def _():
            for ref, val in zip(dpars, grads[7:]):
                ref[...] = val

        @pl.when(jnp.logical_not(first))
        def _():
            for ref, val in zip(dpars, grads[7:]):
                ref[...] += val

    sshape = jax.ShapeDtypeStruct((h, t, dh), F32)
    pshape = jax.ShapeDtypeStruct((h, 1, dh), F32)
    return pl.pallas_call(
        body, name="rwkv_scan_bwd", grid=(nc,), out_shape=(sshape,) * 6 + (pshape,) * 3,
        in_specs=[seq] * 6 + [par] * 3 + [state, seq], out_specs=(seq,) * 6 + (par,) * 3,
        scratch_shapes=[pltpu.VMEM((h, dh, dh), F32)], compiler_params=_params("arbitrary"),
    )(*seqs, *pars, states, do)


def _local_step(x, target, w, ex):
    w = dict(w)
    c = w["mu_r"].shape[-1]
    qn, kn = w["q_norm"].reshape(1, 1, HEAD_DIM), w["k_norm"].reshape(1, 1, HEAD_DIM)
    vec = jnp.concatenate([w[n].reshape(1, c) for n in RWKV_VEC], axis=0)
    pars = [w[n].reshape(-1, 1, HEAD_DIM) for n in ("r_k", "ln_x_w", "ln_x_b")]
    no_dep = jnp.zeros(DEP_SHAPE, F32)

    x1 = _ffn_fwd(x, w["ffn1_norm"], w["ffn1_w_gate"], w["ffn1_w_up"], w["ffn1_w_down"], ex.first_dep, "ffn1_fwd")
    w.update(ex.mix_weights((x1,)))
    mats = [w[n] for n in RWKV_MAT]
    q, k, v, cur = _proj_fwd(x1, w["mix_norm"], w["w_in"])
    att, *saved = _att_fwd(q, k, v, qn, kn)
    pre = _rwkv_pre_fwd(cur, vec, mats)
    seqs, gate = pre[:6], pre[6]
    opg, states = _scan_fwd(seqs, pars)
    w.update(ex.out_weights((att, opg)))
    x2 = _mixout_fwd(x1, att, opg, gate, w["w_out"])
    x3 = _ffn_fwd(x2, w["ffn2_norm"], w["ffn2_w_gate"], w["ffn2_w_up"], w["ffn2_w_down"], no_dep, "ffn2_fwd")
    dy, loss = _loss_head(x3, target)

    g = {}
    dx2, g["ffn2_norm"], g["ffn2_w_gate"], g["ffn2_w_up"], g["ffn2_w_down"] = _ffn_bwd(
        x2, w["ffn2_norm"], w["ffn2_w_gate"], w["ffn2_w_up"], w["ffn2_w_down"], dy, no_dep, "ffn2_bwd")
    dep = ex.send_ffn2({n: g[n] for n in ("ffn2_w_gate", "ffn2_w_up", "ffn2_w_down")})
    datt, dopg, dgate, g["w_out"] = _mixout_bwd(att, opg, gate, w["w_out"], dx2, dep)
    dscan = _scan_bwd(seqs, pars, states, dopg)
    for n, d in zip(("r_k", "ln_x_w", "ln_x_b"), dscan[6:]):
        g[n] = d
    dcur, dvec, *dmats = _rwkv_pre_bwd(cur, vec, mats, dscan[:6], dgate)
    for n, d in zip(RWKV_MAT, dmats):
        g[n] = d
    for j, n in enumerate(RWKV_VEC):
        g[n] = dvec[j:j + 1]
    dq, dk, dv, g["q_norm"], g["k_norm"] = _att_bwd(q, k, v, qn, kn, saved, datt)
    dx1, g["mix_norm"], g["w_in"] = _proj_bwd(x1, w["mix_norm"], w["w_in"], dq, dk, dv, dcur, dx2)
    dep = ex.send_mix({n: g[n] for n in ("w_in", "w_out") + RWKV_MAT}, (dx1,))
    dx, g["ffn1_norm"], g["ffn1_w_gate"], g["ffn1_w_up"], g["ffn1_w_down"] = _ffn_bwd(
        x, w["ffn1_norm"], w["ffn1_w_gate"], w["ffn1_w_up"], w["ffn1_w_down"], dx1, dep, "ffn1_bwd")
    return loss, dx, g


N_SHARDS = 4


def _place():
    return lax.axis_index("x"), lax.axis_index("y"), lax.axis_index("c")


def _chip_peers(x, y):
    return [(1 - x, y), (x, 1 - y), (1 - x, 1 - y)]


HBM = pl.BlockSpec(memory_space=pltpu.HBM)
SEM = pl.BlockSpec(memory_space=pltpu.SEMAPHORE)
DEP_SHAPE = (8, 128)


class _Views:
    to_sibling = False


class _GatherViews(_Views):
    @staticmethod
    def send(i, srcs, lands, k, at):
        return srcs[i], lands[i].at[at[3]]

    @staticmethod
    def landing(i, srcs, lands, k, at):
        return srcs[i], lands[i].at[2 * at[4] + at[5]]


class _ScatterViews(_Views):
    @staticmethod
    def send(i, srcs, lands, k, at):
        return srcs[i].at[2 * at[4] + at[5]], lands[i].at[k]

    @staticmethod
    def landing(i, srcs, lands, k, at):
        return srcs[i].at[at[3]], lands[i].at[k]


def _half_rows(ref, slot, half):
    rows = ref.shape[1] // 2
    return ref.at[slot, pl.ds(pl.multiple_of(half * rows, BF16_SUBLANES), rows)]


class _HalfGatherViews(_Views):
    @staticmethod
    def send(i, srcs, lands, k, at):
        rows = srcs[i].shape[0] // 2
        return srcs[i].at[pl.ds(pl.multiple_of(at[2] * rows, BF16_SUBLANES), rows)], _half_rows(lands[i], at[3], at[2])

    @staticmethod
    def landing(i, srcs, lands, k, at):
        rows = srcs[i].shape[0] // 2
        return srcs[i].at[pl.ds(pl.multiple_of(at[2] * rows, BF16_SUBLANES), rows)], _half_rows(lands[i], 2 * at[4] + at[5], at[2])


class _ForwardViews(_Views):
    to_sibling = True

    @staticmethod
    def send(i, srcs, lands, k, at):
        mine = _half_rows(lands[i], 2 * at[4] + at[5], at[2])
        return mine, mine

    @staticmethod
    def landing(i, srcs, lands, k, at):
        theirs = _half_rows(lands[i], 2 * at[4] + at[5], 1 - at[2])
        return theirs, theirs


def _push_start(srcs, lands, views, after, name):
    ns, nl = len(srcs), len(lands)

    def body(*refs):
        src_refs, land_refs = refs[:ns], refs[ns:ns + nl]
        send_sems, recv_sems = refs[ns + nl + 1:ns + nl + 3]
        token = refs[2 * (ns + nl) + 3]
        x, y, c = _place()
        for i in range(nl):
            for k, (px, py) in enumerate(_chip_peers(x, y)):
                src, dst = views.send(i, src_refs, land_refs, k, (x, y, c, 2 * x + y, px, py))
                pltpu.make_async_remote_copy(
                    src_ref=src, dst_ref=dst, send_sem=send_sems.at[3 * i + k], recv_sem=recv_sems.at[3 * i + k],
                    device_id=(x, y, 1 - c) if views.to_sibling else (px, py, c), device_id_type=MESH).start()
        token[...] = jnp.zeros_like(token)

    sems = pltpu.SemaphoreType.DMA((3 * nl,))
    both = [pltpu.with_memory_space_constraint(a, pltpu.HBM) for a in (*srcs, *lands)]
    outs = pl.pallas_call(
        body, name=name,
        out_shape=(sems, sems, *[pltpu.HBM(a.shape, a.dtype) for a in both], jax.ShapeDtypeStruct(DEP_SHAPE, F32)),
        in_specs=[HBM] * (ns + nl) + [ANY], out_specs=(SEM, SEM, *[HBM] * (ns + nl), VMEM_FULL),
        input_output_aliases={i: 2 + i for i in range(ns + nl)},
        compiler_params=pltpu.CompilerParams(has_side_effects=pltpu.SideEffectType.DATAFLOW_SIDE_EFFECTING),
    )(*both, after)
    return outs[0], outs[1], outs[2:2 + ns], outs[2 + ns:2 + ns + nl], outs[2 + ns + nl]


def _push_wait(started, views, after, name):
    send_sems, recv_sems, srcs, lands, _ = started
    ns, nl = len(srcs), len(lands)

    def body(*refs):
        src_refs, land_refs = refs[:ns], refs[ns:ns + nl]
        send_sems, recv_sems = refs[ns + nl:ns + nl + 2]
        x, y, c = _place()
        for i in range(nl):
            for k, (px, py) in enumerate(_chip_peers(x, y)):
                src, dst = views.landing(i, src_refs, land_refs, k, (x, y, c, 2 * x + y, px, py))
                landing = pltpu.make_async_remote_copy(
                    src_ref=src, dst_ref=dst, send_sem=send_sems.at[3 * i + k], recv_sem=recv_sems.at[3 * i + k],
                    device_id=(x, y, 1 - c) if views.to_sibling else (px, py, c), device_id_type=MESH)
                landing.wait_send()
                landing.wait_recv()

    outs = pl.pallas_call(
        body, name=name,
        out_shape=tuple(pltpu.HBM(a.shape, a.dtype) for a in (*srcs, *lands)),
        in_specs=[HBM] * (ns + nl) + [SEM, SEM] + [ANY] * len(after), out_specs=(HBM,) * (ns + nl),
        input_output_aliases={i: i for i in range(ns + nl)},
        compiler_params=pltpu.CompilerParams(has_side_effects=pltpu.SideEffectType.DATAFLOW_SIDE_EFFECTING),
    )(*srcs, *lands, send_sems, recv_sems, *after)
    return outs[ns:]


def _empty_lands(shards, slots, own_slot):
    lands = [lax.empty((slots,) + s.shape, s.dtype) for s in shards]
    if own_slot:
        me = 2 * lax.axis_index("x") + lax.axis_index("y")
        lands = [lax.dynamic_update_index_in_dim(z, s, me, 0) for z, s in zip(lands, shards)]
    return lands


def _sibling_swap(arrays, name):
    n = len(arrays)

    def body(*refs):
        ins, outs = refs[:n], refs[n:2 * n]
        send_sems, recv_sems = refs[2 * n:]
        x, y, c = _place()
        copies = []
        for i in range(n):
            cp = pltpu.make_async_remote_copy(
                src_ref=ins[i], dst_ref=outs[i], send_sem=send_sems.at[i], recv_sem=recv_sems.at[i],
                device_id=(x, y, 1 - c), device_id_type=MESH)
            cp.start()
            copies.append(cp)
        for cp in copies:
            cp.wait()

    return pl.pallas_call(
        body, name=name,
        out_shape=tuple(jax.ShapeDtypeStruct(a.shape, a.dtype) for a in arrays),
        in_specs=[ANY] * n, out_specs=(ANY,) * n,
        scratch_shapes=[pltpu.SemaphoreType.DMA((n,)), pltpu.SemaphoreType.DMA((n,))],
    )(*arrays)


N_DEV = 8


def _allreduce_small(pack):
    def body(in_ref, out_ref, buf, send_sems, recv_sems):
        x, y, c = _place()
        me = 4 * x + 2 * y + c
        buf[me] = in_ref[...]

        def copy(j, slot):
            px, py, pc = x ^ (j >> 2), y ^ ((j >> 1) & 1), c ^ (j & 1)
            return pltpu.make_async_remote_copy(
                src_ref=in_ref, dst_ref=buf.at[slot(px, py, pc)], send_sem=send_sems.at[j], recv_sem=recv_sems.at[j],
                device_id=(px, py, pc), device_id_type=MESH)

        for j in range(1, N_DEV):
            copy(j, lambda px, py, pc: me).start()
        for j in range(1, N_DEV):
            landing = copy(j, lambda px, py, pc: 4 * px + 2 * py + pc)
            landing.wait_send()
            landing.wait_recv()
        acc = buf[0]
        for s in range(1, N_DEV):
            acc = acc + buf[s]
        out_ref[...] = acc

    return pl.pallas_call(
        body, name="allreduce_small", out_shape=jax.ShapeDtypeStruct(pack.shape, F32),
        in_specs=[VMEM_FULL], out_specs=VMEM_FULL,
        scratch_shapes=[pltpu.VMEM((N_DEV,) + pack.shape, F32), pltpu.SemaphoreType.DMA((N_DEV,)),
                        pltpu.SemaphoreType.DMA((N_DEV,))],
    )(pack)


ROW_TILE_MAX = 256
BF16_SUBLANES = 16


def _row_tile(rows):
    for tr in range(min(rows, ROW_TILE_MAX), 0, -1):
        if rows % tr == 0 and tr % BF16_SUBLANES == 0:
            return tr
    return rows


def _reduce_own(me, part, recv, dep, name):
    _, r, cols = part.shape
    tr = _row_tile(r)

    def body(me_ref, p_ref, rv_ref, dep_ref, o_ref):
        acc = p_ref[0]
        for k in range(3):
            acc = acc + rv_ref[k].astype(F32)
        o_ref[...] = acc

    return pl.pallas_call(
        body, name=name, out_shape=jax.ShapeDtypeStruct((r, cols), F32),
        grid_spec=pltpu.PrefetchScalarGridSpec(
            num_scalar_prefetch=1, grid=(r // tr,),
            in_specs=[pl.BlockSpec((1, tr, cols), lambda i, me_ref: (me_ref[0], i, 0)),
                      pl.BlockSpec((3, tr, cols), lambda i, me_ref: (0, i, 0)), ANY],
            out_specs=pl.BlockSpec((tr, cols), lambda i, me_ref: (i, 0))),
        compiler_params=_params("arbitrary"),
    )(me, part, recv, dep)


def _adamw(w, ga, gb, m, v, name):
    r, cols = w.shape
    tr = _row_tile(r)
    c1 = 1.0 - ADAM_B1 ** ADAM_STEP
    c2 = 1.0 - ADAM_B2 ** ADAM_STEP

    def body(w_ref, ga_ref, gb_ref, m_ref, v_ref, g_out, d_out, m_out, v_out):
        g = ga_ref[...] + gb_ref[...]
        mn = ADAM_B1 * m_ref[...] + (1.0 - ADAM_B1) * g
        vn = ADAM_B2 * v_ref[...] + (1.0 - ADAM_B2) * (g * g)
        g_out[...] = g
        m_out[...] = mn
        v_out[...] = vn
        d_out[...] = -ADAM_LR * ((mn / c1) / (jnp.sqrt(vn / c2) + ADAM_EPS) + ADAM_WD * w_ref[...])

    tile = pl.BlockSpec((tr, cols), lambda i: (i, 0))
    shape = jax.ShapeDtypeStruct((r, cols), F32)
    return pl.pallas_call(
        body, name=name, grid=(r // tr,), out_shape=(shape,) * 4, in_specs=[tile] * 5, out_specs=(tile,) * 4,
        compiler_params=_params("arbitrary"),
    )(w, ga, gb, m, v)


PACK_COLS = 512


def _to_rows(a):
    flat = a.reshape(-1)
    pad = (-flat.shape[0]) % PACK_COLS
    return jnp.pad(flat, (0, pad)).reshape(-1, PACK_COLS)


def _pack(arrays, extra_rows=0):
    rows = [_to_rows(a) for a in arrays]
    n = sum(r.shape[0] for r in rows) + extra_rows
    pad = (-n) % 8
    return jnp.concatenate(rows + [jnp.zeros((extra_rows + pad, PACK_COLS), F32)], axis=0)


def _unpack(pack, like):
    out, at = [], 0
    for a in like:
        n = -(-a.size // PACK_COLS)
        out.append(pack[at:at + n].reshape(-1)[:a.size].reshape(a.shape))
        at += n
    return out


COL_SHARDED = ("ffn1_w_gate", "ffn1_w_up", "w_in", "ffn2_w_gate", "ffn2_w_up", "w2", "a2", "g2")
ROW_SHARDED = ("ffn1_w_down", "ffn2_w_down", "w_out", "w1", "a1", "g1")
CHUNKED = ("ffn1_w_gate", "ffn1_w_up", "ffn1_w_down", "ffn2_w_gate", "ffn2_w_up", "ffn2_w_down")
WEIGHTS = ("ffn1_norm", "ffn1_w_gate", "ffn1_w_up", "ffn1_w_down", "mix_norm", "w_in", "q_norm", "k_norm",
           "mu_r", "mu_k", "mu_v", "mu_w", "mu_a", "mu_g", "w0", "w1", "w2", "a0", "a1", "a2", "g1", "g2",
           "k_k", "k_a", "r_k", "ln_x_w", "ln_x_b", "w_out", "ffn2_norm", "ffn2_w_gate", "ffn2_w_up", "ffn2_w_down")


W_IN_GROUPS = 7
TRANSPOSED = ("ffn1_w_gate", "ffn1_w_up", "ffn2_w_gate", "ffn2_w_up")


def _shard_2d(name, a):
    return a[0].T if name in TRANSPOSED else a[0]


def _full_from_blocks(name, blocks):
    if name in CHUNKED:
        return blocks
    if name in ROW_SHARDED:
        return blocks.reshape(-1, blocks.shape[-1])
    full = blocks.transpose(1, 0, 2).reshape(blocks.shape[1], -1)
    if name == "w_in":
        return full.reshape(full.shape[0], W_IN_GROUPS, -1).transpose(1, 0, 2)
    return full


def _blocks_from_full(name, full):
    if name in CHUNKED:
        return full
    if name in ROW_SHARDED:
        return full.reshape(N_SHARDS, -1, full.shape[-1])
    if name == "w_in":
        full = full.transpose(1, 0, 2).reshape(full.shape[1], -1)
    return full.reshape(full.shape[0], N_SHARDS, -1).transpose(1, 0, 2)


FFN1_GROUP = ("ffn1_w_gate", "ffn1_w_up", "ffn1_w_down")
MIX_GROUP = ("w_in",) + RWKV_MAT
OUT_GROUP = ("w_out", "ffn2_w_gate", "ffn2_w_up", "ffn2_w_down")
FFN2_GROUP = OUT_GROUP[1:]
LATE_GROUP = ("w_in", "w_out") + RWKV_MAT


class _Exchange:
    def __init__(self, given):
        self.given = given
        first = self._gather_start(FFN1_GROUP, _HalfGatherViews, jnp.zeros(DEP_SHAPE, F32), "gather_ffn1_start")
        self.mix = self._gather_start(MIX_GROUP, _GatherViews, first[4], "gather_mix_start")
        self.out = self._gather_start(OUT_GROUP, _GatherViews, self.mix[4], "gather_out_start")
        self.first_dep = self.out[4]
        halves = _push_wait(first, _HalfGatherViews, (self.first_dep,), "gather_ffn1_wait")
        passed = _push_start([], halves, _ForwardViews, halves[0], "gather_ffn1_pass_start")
        self.first_weights = self._full(FFN1_GROUP, _push_wait(passed, _ForwardViews, (passed[4],), "gather_ffn1_pass_wait"))
        self.parts, self.recv = {}, {}

    def _shards(self, names):
        return [_shard_2d(n, self.given[n]).astype(BF16) for n in names]

    @staticmethod
    def _full(names, blocks):
        out = {}
        for n, b in zip(names, blocks):
            full = _full_from_blocks(n, b)
            out[n] = full.astype(F32) if n in RWKV_MAT else full
        return out

    def _gather_start(self, names, views, after, name):
        shards = self._shards(names)
        return _push_start(shards, _empty_lands(shards, N_SHARDS, True), views, after, name)

    def mix_weights(self, after):
        return self._full(MIX_GROUP, _push_wait(self.mix, _GatherViews, after, "gather_mix_wait"))

    def out_weights(self, after):
        return self._full(OUT_GROUP, _push_wait(self.out, _GatherViews, after, "gather_out_wait"))

    def _scatter_start(self, grads, name):
        names = tuple(grads)
        parts = [_blocks_from_full(n, grads[n]) for n in names]
        self.parts.update(zip(names, parts))
        lands = [lax.empty((3,) + p.shape[1:], BF16) for p in parts]
        return _push_start([p.astype(BF16) for p in parts], lands, _ScatterViews, parts[0], name)

    def send_ffn2(self, grads):
        self.ffn2 = self._scatter_start(grads, "scatter_ffn2_start")
        return self.ffn2[4]

    def send_mix(self, grads, after):
        self.recv.update(zip(FFN2_GROUP, _push_wait(self.ffn2, _ScatterViews, after, "scatter_ffn2_wait")))
        self.late = self._scatter_start(grads, "scatter_late_start")
        return self.late[4]

    def send_ffn1(self, grads):
        self.ffn1 = self._scatter_start(grads, "scatter_ffn1_start")
        return self.ffn1[4]

    def late_received(self, after):
        self.recv.update(zip(LATE_GROUP, _push_wait(self.late, _ScatterViews, after, "scatter_late_wait")))

    def ffn1_received(self, after):
        self.recv.update(zip(FFN1_GROUP, _push_wait(self.ffn1, _ScatterViews, after, "scatter_ffn1_wait")))


def kernel(
        x, ffn1_norm, ffn1_w_gate, ffn1_w_up, ffn1_w_down, mix_norm, w_in, q_norm, k_norm, mu_r, mu_k, mu_v, mu_w,
        mu_a, mu_g, w0, w1, w2, a0, a1, a2, g1, g2, k_k, k_a, r_k, ln_x_w, ln_x_b, w_out, ffn2_norm, ffn2_w_gate,
        ffn2_w_up, ffn2_w_down, loss_target, m_ffn1_norm, m_ffn1_w_gate, m_ffn1_w_up, m_ffn1_w_down, m_mix_norm,
        m_w_in, m_q_norm, m_k_norm, m_mu_r, m_mu_k, m_mu_v, m_mu_w, m_mu_a, m_mu_g, m_w0, m_w1, m_w2, m_a0, m_a1,
        m_a2, m_g1, m_g2, m_k_k, m_k_a, m_r_k, m_ln_x_w, m_ln_x_b, m_w_out, m_ffn2_norm, m_ffn2_w_gate, m_ffn2_w_up,
        m_ffn2_w_down, v_ffn1_norm, v_ffn1_w_gate, v_ffn1_w_up, v_ffn1_w_down, v_mix_norm, v_w_in, v_q_norm, v_k_norm,
        v_mu_r, v_mu_k, v_mu_v, v_mu_w, v_mu_a, v_mu_g, v_w0, v_w1, v_w2, v_a0, v_a1, v_a2, v_g1, v_g2, v_k_k, v_k_a,
        v_r_k, v_ln_x_w, v_ln_x_b, v_w_out, v_ffn2_norm, v_ffn2_w_gate, v_ffn2_w_up, v_ffn2_w_down):
    given = dict(locals())
    sharded = COL_SHARDED + ROW_SHARDED
    sharded = tuple(n for n in WEIGHTS if n in sharded)
    small = tuple(n for n in WEIGHTS if n not in sharded)

    ex = _Exchange(given)
    w = {n: given[n] for n in small}
    w.update(ex.first_weights)
    loss, dx, g = _local_step(x[0], loss_target[0], w, ex)
    dep = ex.send_ffn1({n: g[n] for n in FFN1_GROUP})

    me = (2 * lax.axis_index("x") + lax.axis_index("y")).astype(jnp.int32).reshape(1)
    out = {}

    def settle(names, dep, tag):
        mine = []
        for n in names:
            p, rv = ex.parts[n], ex.recv[n]
            p2 = p.reshape(N_SHARDS, -1, p.shape[-1])
            mine.append(_reduce_own(me, p2, rv.reshape(3, -1, rv.shape[-1]), dep, f"reduce_{n}"))
        theirs = _sibling_swap(mine, f"sibling_swap_{tag}")
        for n, a, b in zip(names, mine, theirs):
            shape = given[n].shape
            res = _adamw(_shard_2d(n, given[n]), a, b, _shard_2d(n, given["m_" + n]), _shard_2d(n, given["v_" + n]), f"adamw_{n}")
            out[n] = [(r.T if n in TRANSPOSED else r).reshape(shape) for r in res]
        return tuple(out[n][1] for n in names)

    ex.late_received((dep,))
    last = settle(tuple(n for n in sharded if n not in FFN1_GROUP), dep, "rest")

    gpack = _pack([g[n] for n in small], extra_rows=1)
    n_rows = sum(-(-given[n].size // PACK_COLS) for n in small)
    gpack = gpack.at[n_rows, :loss.shape[1]].set(loss[0])
    gsum = _allreduce_small(gpack)
    res = _adamw(_pack([given[n] for n in small], 1), gsum, jnp.zeros_like(gsum), _pack([given["m_" + n] for n in small], 1),
                 _pack([given["v_" + n] for n in small], 1), "adamw_small")
    like = [given[n] for n in small]
    for j, r in enumerate(res):
        for n, a in zip(small, _unpack(r, like)):
            out.setdefault(n, [None] * 4)[j] = a
    total_loss = gsum[n_rows, 0]

    ex.ffn1_received((*last, res[1]))
    settle(FFN1_GROUP, jnp.zeros(DEP_SHAPE, F32), "ffn1")
    return (total_loss, dx[None], *[out[n][0] for n in WEIGHTS], *[out[n][1] for n in WEIGHTS],
            *[out[n][2] for n in WEIGHTS], *[out[n][3] for n in WEIGHTS])
```

```python
import functools

import jax
import jax.numpy as jnp
from jax import lax
from jax.experimental import pallas as pl
from jax.experimental.pallas import tpu as pltpu

F32 = jnp.float32
BF16 = jnp.bfloat16
MESH = pl.DeviceIdType.MESH

RMS_EPS = 1e-6
GN_EPS = 64e-5
NEG_INF = -1e30
FFN_RESIDUAL = 0.5
HEAD_DIM = 64
ATT_BLOCK = 128
DILATIONS = (1, 4, 16)
SCAN_CHUNK = 64
TOKEN_TILE = 256

ADAM_LR = 0.001
ADAM_B1 = 0.9
ADAM_B2 = 0.999
ADAM_EPS = 1e-08
ADAM_WD = 0.01
ADAM_STEP = 10

VMEM_FULL = pl.BlockSpec(memory_space=pltpu.VMEM)
ANY = pl.BlockSpec(memory_space=pl.ANY)


VMEM_LIMIT = 56 * 1024 * 1024


def _params(*sem):
    return pltpu.CompilerParams(dimension_semantics=sem, vmem_limit_bytes=VMEM_LIMIT)


def _dot(a, b, dims):
    return lax.dot_general(a.astype(BF16), b.astype(BF16), (dims, ((), ())), preferred_element_type=F32)


def _dot_nn(a, b):
    return _dot(a, b, ((1,), (0,)))


def _dot_nt(a, b):
    return _dot(a, b, ((1,), (1,)))


def _dot_tn(a, b):
    return _dot(a, b, ((0,), (0,)))


@jax.custom_vjp
def _mm(a, b):
    return _dot_nn(a, b)


def _mm_fwd(a, b):
    return _dot_nn(a, b), (a, b)


def _mm_bwd(res, g):
    a, b = res
    return _dot_nt(g, b).astype(a.dtype), _dot_tn(a, g).astype(b.dtype)


_mm.defvjp(_mm_fwd, _mm_bwd)


def _bdot(a, b, ca, cb):
    return lax.dot_general(a.astype(BF16), b.astype(BF16), (((ca,), (cb,)), ((0,), (0,))), preferred_element_type=F32)


@jax.custom_vjp
def _bmm_nt(a, b):
    return _bdot(a, b, 2, 2)


def _bmm_nt_fwd(a, b):
    return _bdot(a, b, 2, 2), (a, b)


def _bmm_nt_bwd(res, g):
    a, b = res
    return _bdot(g, b, 2, 1), _bdot(g, a, 1, 1)


_bmm_nt.defvjp(_bmm_nt_fwd, _bmm_nt_bwd)


@jax.custom_vjp
def _bmm_nn(a, b):
    return _bdot(a, b, 2, 1)


def _bmm_nn_fwd(a, b):
    return _bdot(a, b, 2, 1), (a, b)


def _bmm_nn_bwd(res, g):
    a, b = res
    return _bdot(g, b, 2, 2), _bdot(a, g, 1, 1)


_bmm_nn.defvjp(_bmm_nn_fwd, _bmm_nn_bwd)


@jax.custom_vjp
def _bmm_tn(a, b):
    return _bdot(a, b, 1, 1)


def _bmm_tn_fwd(a, b):
    return _bdot(a, b, 1, 1), (a, b)


def _bmm_tn_bwd(res, g):
    a, b = res
    return _bdot(b, g, 2, 2), _bdot(a, g, 2, 1)


_bmm_tn.defvjp(_bmm_tn_fwd, _bmm_tn_bwd)


def _hdot(a, b, ca, cb):
    return lax.dot_general(a, b, (((ca,), (cb,)), ((0,), (0,))), precision=lax.Precision.HIGH, preferred_element_type=F32)


def _sigmoid(x):
    return 1.0 / (1.0 + jnp.exp(-x))


def _rms(x):
    return lax.rsqrt(jnp.mean(x * x, axis=-1, keepdims=True) + RMS_EPS)


def _ffn_fwd(x, norm, wg, wu, wd, dep, name):
    t, d = x.shape
    nc = wg.shape[0]
    tm = TOKEN_TILE

    def body(x_ref, n_ref, wg_ref, wu_ref, wd_ref, dep_ref, o_ref):
        xv = x_ref[...]
        h = (xv * _rms(xv) * n_ref[...]).astype(BF16)
        acc = jnp.zeros((tm, d), F32)
        for c in range(nc):
            g = _dot_nt(h, wg_ref[c])
            u = _dot_nt(h, wu_ref[c])
            a = (g * _sigmoid(g) * u).astype(BF16)
            acc = acc + jnp.dot(a, wd_ref[c], preferred_element_type=F32)
        o_ref[...] = xv + FFN_RESIDUAL * acc

    tile = pl.BlockSpec((tm, d), lambda i: (i, 0))
    return pl.pallas_call(
        body, name=name, grid=(t // tm,), out_shape=jax.ShapeDtypeStruct((t, d), F32),
        in_specs=[tile, pl.BlockSpec((1, d), lambda i: (0, 0)), VMEM_FULL, VMEM_FULL, VMEM_FULL, ANY],
        out_specs=tile, compiler_params=_params("arbitrary"),
    )(x, norm, wg, wu, wd, dep)


def _rmsnorm_bwd(xv, gain, dh):
    rs = _rms(xv)
    xn = xv * rs
    dxn = dh * gain
    dx = rs * (dxn - xn * jnp.mean(dxn * xn, axis=-1, keepdims=True))
    return dx, jnp.sum(dh * xn, axis=0, keepdims=True)


def _ffn_bwd(x, norm, wg, wu, wd, dy, dep, name):
    t, d = x.shape
    nc, fc, _ = wg.shape
    tm = TOKEN_TILE
    nt = t // tm

    def body(x_ref, n_ref, wg_ref, wu_ref, wd_ref, dy_ref, dep_ref, dx_ref, dn_ref, dwg_ref, dwu_ref, dwd_ref,
             dh_ref, ag_ref, au_ref, ad_ref):
        c, i = pl.program_id(0), pl.program_id(1)
        rows = pl.ds(pl.multiple_of(i * tm, tm), tm)
        xv = x_ref[...]
        gain = n_ref[...]
        h = (xv * _rms(xv) * gain).astype(BF16)
        dy = dy_ref[...]
        dyb = (FFN_RESIDUAL * dy).astype(BF16)
        g = _dot_nt(h, wg_ref[0])
        u = _dot_nt(h, wu_ref[0])
        sg = _sigmoid(g)
        s = g * sg
        a = (s * u).astype(BF16)
        da = _dot_nt(dyb, wd_ref[0])
        dub = (da * s).astype(BF16)
        dgb = (da * u * (sg * (1.0 + g * (1.0 - sg)))).astype(BF16)
        dwd_c = _dot_tn(a, dyb)
        dwg_c = _dot_tn(dgb, h)
        dwu_c = _dot_tn(dub, h)
        dh_c = _dot_nn(dgb, wg_ref[0]) + _dot_nn(dub, wu_ref[0])

        @pl.when(i == 0)
        def _():
            ad_ref[...] = dwd_c
            ag_ref[...] = dwg_c
            au_ref[...] = dwu_c

        @pl.when(i > 0)
        def _():
            ad_ref[...] += dwd_c
            ag_ref[...] += dwg_c
            au_ref[...] += dwu_c

        @pl.when(i == nt - 1)
        def _():
            dwd_ref[0] = ad_ref[...].astype(BF16)
            dwg_ref[0] = ag_ref[...].astype(BF16)
            dwu_ref[0] = au_ref[...].astype(BF16)

        @pl.when(c == 0)
        def _():
            dh_ref[rows, :] = dh_c

        @pl.when(c > 0)
        def _():
            dh_ref[rows, :] += dh_c

        @pl.when(c == nc - 1)
        def _():
            dx, dn = _rmsnorm_bwd(xv, gain, dh_ref[rows, :])
            dx_ref[...] = dx + dy

            @pl.when(i == 0)
            def _():
                dn_ref[...] = dn

            @pl.when(i > 0)
            def _():
                dn_ref[...] += dn

    tile = pl.BlockSpec((tm, d), lambda c, i: (i, 0))
    row = pl.BlockSpec((1, d), lambda c, i: (0, 0))
    wrow = pl.BlockSpec((1, fc, d), lambda c, i: (c, 0, 0))
    last = pl.BlockSpec((tm, d), lambda c, i: (jnp.where(c == nc - 1, i, 0), 0))
    return pl.pallas_call(
        body, name=name, grid=(nc, nt),
        out_shape=(jax.ShapeDtypeStruct((t, d), F32), jax.ShapeDtypeStruct((1, d), F32),
                   jax.ShapeDtypeStruct(wg.shape, BF16), jax.ShapeDtypeStruct(wu.shape, BF16),
                   jax.ShapeDtypeStruct(wd.shape, BF16)),
        in_specs=[tile, row, wrow, wrow, wrow, tile, ANY],
        out_specs=(last, row, wrow, wrow, wrow),
        scratch_shapes=[pltpu.VMEM((t, d), F32)] + [pltpu.VMEM((fc, d), F32)] * 3,
        compiler_params=_params("arbitrary", "arbitrary"),
    )(x, norm, wg, wu, wd, dy, dep)


def _store_heads(ref, v):
    for h in range(ref.shape[0]):
        ref[h] = v[:, h * HEAD_DIM:(h + 1) * HEAD_DIM]


def _load_heads(ref):
    return jnp.concatenate([ref[h] for h in range(ref.shape[0])], axis=-1)


N_HEAD_GROUPS = 3


def _proj_fwd(x, norm, w):
    t, d = x.shape
    ng, _, c = w.shape
    nh = c // HEAD_DIM
    tm = TOKEN_TILE

    def body(x_ref, n_ref, w_ref, q_ref, k_ref, v_ref, cur_ref):
        xv = x_ref[...]
        h = (xv * _rms(xv) * n_ref[...]).astype(BF16)
        for m, ref in enumerate((q_ref, k_ref, v_ref)):
            _store_heads(ref, jnp.dot(h, w_ref[m], preferred_element_type=F32))
        for m in range(N_HEAD_GROUPS, ng):
            j = m - N_HEAD_GROUPS
            cur_ref[:, j * c:(j + 1) * c] = jnp.dot(h, w_ref[m], preferred_element_type=F32)

    heads = pl.BlockSpec((nh, tm, HEAD_DIM), lambda i: (0, i, 0))
    hshape = jax.ShapeDtypeStruct((nh, t, HEAD_DIM), F32)
    wide = (ng - N_HEAD_GROUPS) * c
    return pl.pallas_call(
        body, name="proj_fwd", grid=(t // tm,),
        out_shape=(hshape, hshape, hshape, jax.ShapeDtypeStruct((t, wide), F32)),
        in_specs=[pl.BlockSpec((tm, d), lambda i: (i, 0)), pl.BlockSpec((1, d), lambda i: (0, 0)), VMEM_FULL],
        out_specs=(heads, heads, heads, pl.BlockSpec((tm, wide), lambda i: (i, 0))),
        compiler_params=_params("arbitrary"),
    )(x, norm, w)


def _proj_bwd(x, norm, w, dq, dk, dv, dcur, dres):
    t, d = x.shape
    ng, _, c = w.shape
    nh = c // HEAD_DIM
    tm = TOKEN_TILE

    def body(x_ref, n_ref, w_ref, dq_ref, dk_ref, dv_ref, dcur_ref, dres_ref, dx_ref, dn_ref, dw_ref):
        i = pl.program_id(0)

        @pl.when(i == 0)
        def _():
            dw_ref[...] = jnp.zeros_like(dw_ref)
            dn_ref[...] = jnp.zeros_like(dn_ref)

        xv = x_ref[...]
        gain = n_ref[...]
        h = (xv * _rms(xv) * gain).astype(BF16)
        dh = jnp.zeros((tm, d), F32)
        for m in range(ng):
            j = m - N_HEAD_GROUPS
            dp = _load_heads((dq_ref, dk_ref, dv_ref)[m]) if j < 0 else dcur_ref[:, j * c:(j + 1) * c]
            dp = dp.astype(BF16)
            dw_ref[m] += _dot_tn(h, dp)
            dh = dh + _dot_nt(dp, w_ref[m])
        dx, dn = _rmsnorm_bwd(xv, gain, dh)
        dx_ref[...] = dx + dres_ref[...]
        dn_ref[...] += dn

    tile = pl.BlockSpec((tm, d), lambda i: (i, 0))
    row = pl.BlockSpec((1, d), lambda i: (0, 0))
    heads = pl.BlockSpec((nh, tm, HEAD_DIM), lambda i: (0, i, 0))
    wide = (ng - N_HEAD_GROUPS) * c
    return pl.pallas_call(
        body, name="proj_bwd", grid=(t // tm,),
        out_shape=(jax.ShapeDtypeStruct((t, d), F32), jax.ShapeDtypeStruct((1, d), F32),
                   jax.ShapeDtypeStruct(w.shape, F32)),
        in_specs=[tile, row, VMEM_FULL, heads, heads, heads, pl.BlockSpec((tm, wide), lambda i: (i, 0)), tile],
        out_specs=(tile, row, VMEM_FULL),
        compiler_params=_params("arbitrary"),
    )(x, norm, w, dq, dk, dv, dcur, dres)


def _mixout_fwd(x, att, opg, gate, w):
    t, d = x.shape
    nh = att.shape[0]
    half = gate.shape[1]
    tm = TOKEN_TILE

    def body(x_ref, att_ref, opg_ref, g_ref, w_ref, o_ref):
        mix = jnp.concatenate([_load_heads(att_ref), _load_heads(opg_ref) * g_ref[...]], axis=-1).astype(BF16)
        o_ref[...] = x_ref[...] + jnp.dot(mix, w_ref[...], preferred_element_type=F32)

    tile = pl.BlockSpec((tm, d), lambda i: (i, 0))
    htile = pl.BlockSpec((tm, half), lambda i: (i, 0))
    heads = pl.BlockSpec((nh, tm, HEAD_DIM), lambda i: (0, i, 0))
    return pl.pallas_call(
        body, name="mixout_fwd", grid=(t // tm,), out_shape=jax.ShapeDtypeStruct((t, d), F32),
        in_specs=[tile, heads, heads, htile, VMEM_FULL], out_specs=tile, compiler_params=_params("arbitrary"),
    )(x, att, opg, gate, w)


def _mixout_bwd(att, opg, gate, w, dy, dep):
    nh, t, _ = att.shape
    half = gate.shape[1]
    d = dy.shape[1]
    tm = TOKEN_TILE

    def body(att_ref, opg_ref, g_ref, w_ref, dy_ref, dep_ref, datt_ref, dopg_ref, dg_ref, dw_ref):
        i = pl.program_id(0)
        opg_v, g_v = _load_heads(opg_ref), g_ref[...]
        mix = jnp.concatenate([_load_heads(att_ref), opg_v * g_v], axis=-1).astype(BF16)
        dyb = dy_ref[...].astype(BF16)
        dmix = _dot_nt(dyb, w_ref[...])
        dw = _dot_tn(mix, dyb)
        _store_heads(datt_ref, dmix[:, :half])
        drw = dmix[:, half:]
        _store_heads(dopg_ref, drw * g_v)
        dg_ref[...] = drw * opg_v

        @pl.when(i == 0)
        def _():
            dw_ref[...] = dw

        @pl.when(i > 0)
        def _():
            dw_ref[...] += dw

    tile = pl.BlockSpec((tm, d), lambda i: (i, 0))
    htile = pl.BlockSpec((tm, half), lambda i: (i, 0))
    heads = pl.BlockSpec((nh, tm, HEAD_DIM), lambda i: (0, i, 0))
    hshape = jax.ShapeDtypeStruct((nh, t, HEAD_DIM), F32)
    return pl.pallas_call(
        body, name="mixout_bwd", grid=(t // tm,),
        out_shape=(hshape, hshape, jax.ShapeDtypeStruct((t, half), F32), jax.ShapeDtypeStruct(w.shape, F32)),
        in_specs=[heads, heads, htile, VMEM_FULL, tile, ANY],
        out_specs=(heads, heads, htile, pl.BlockSpec(w.shape, lambda i: (0, 0))),
        compiler_params=_params("arbitrary"),
    )(att, opg, gate, w, dy, dep)


def _loss_head(y, target):
    t, d = y.shape
    tm = TOKEN_TILE

    def body(y_ref, t_ref, dy_ref, loss_ref):
        i = pl.program_id(0)
        err = y_ref[...] - t_ref[...]
        dy_ref[...] = err * (1.0 / d)
        part = 0.5 * jnp.sum(jnp.mean(err * err, axis=-1, keepdims=True), axis=0, keepdims=True)

        @pl.when(i == 0)
        def _():
            loss_ref[...] = jnp.zeros_like(loss_ref)

        loss_ref[...] += jnp.broadcast_to(part, loss_ref.shape)

    tile = pl.BlockSpec((tm, d), lambda i: (i, 0))
    return pl.pallas_call(
        body, name="loss_head", grid=(t // tm,),
        out_shape=(jax.ShapeDtypeStruct((t, d), F32), jax.ShapeDtypeStruct((1, 128), F32)),
        in_specs=[tile, tile], out_specs=(tile, pl.BlockSpec((1, 128), lambda i: (0, 0))),
        compiler_params=_params("arbitrary"),
    )(y, target)


def _att_pattern(q, k, v, qn, kn, nb):
    g, blk, _ = q.shape
    qh = q * _rms(q) * qn
    kh = k * _rms(k) * kn
    scale = HEAD_DIM ** -0.5
    qi = lax.broadcasted_iota(jnp.int32, (blk, blk), 0)
    kj = lax.broadcasted_iota(jnp.int32, (blk, blk), 1)
    sc = jnp.where(kj <= qi, _bmm_nt(qh, kh) * scale, NEG_INF)
    top = jnp.max(sc, axis=-1, keepdims=True)
    if nb > 1:
        khp = jnp.concatenate([kh[:1], kh[:-1]], axis=0)
        vp = jnp.concatenate([v[:1], v[:-1]], axis=0)
        has_prev = lax.broadcasted_iota(jnp.int32, (g, 1, 1), 0) % nb != 0
        sp = jnp.where((kj >= qi) & has_prev, _bmm_nt(qh, khp) * scale, NEG_INF)
        top = jnp.maximum(top, jnp.max(sp, axis=-1, keepdims=True))
    m = lax.stop_gradient(top)
    pc = jnp.exp(sc - m)
    den = jnp.sum(pc, axis=-1, keepdims=True)
    acc = _bmm_nn(pc, v)
    if nb > 1:
        pp = jnp.exp(sp - m)
        den = den + jnp.sum(pp, axis=-1, keepdims=True)
        acc = acc + _bmm_nn(pp, vp)
    o = acc / den
    return o, jnp.broadcast_to(m + jnp.log(den), o.shape)


def _pattern_rows(t, dil):
    nb = t // (ATT_BLOCK * dil)
    starts = [n * ATT_BLOCK * dil + r for r in range(dil) for n in range(nb)]
    return [pl.ds(s, ATT_BLOCK, stride=dil) if dil > 1 else pl.ds(s, ATT_BLOCK) for s in starts], nb


def _take(ref, rows):
    return jnp.stack([ref[0, r, :] for r in rows])


def _put(ref, rows, val):
    for g, r in enumerate(rows):
        ref[0, r, :] = val[g]


def _put_add(ref, rows, val):
    for g, r in enumerate(rows):
        ref[0, r, :] += val[g]


def _merge_fn(o1, o2, o3, l1, l2, l3):
    m = lax.stop_gradient(jnp.maximum(jnp.maximum(l1, l2), l3))
    e1, e2, e3 = jnp.exp(l1 - m), jnp.exp(l2 - m), jnp.exp(l3 - m)
    return (e1 * o1 + e2 * o2 + e3 * o3) / (e1 + e2 + e3)


def _att_head_specs(t):
    head = pl.BlockSpec((1, t, HEAD_DIM), lambda h: (h, 0, 0))
    gain = pl.BlockSpec((1, 1, HEAD_DIM), lambda h: (0, 0, 0))
    return head, gain


def _att_fwd(q, k, v, qn, kn):
    nh, t, dh = q.shape
    head, gain = _att_head_specs(t)

    def body(q_ref, k_ref, v_ref, qn_ref, kn_ref, att_ref, *saved):
        o_refs, l_refs = saved[:3], saved[3:]
        for p, dil in enumerate(DILATIONS):
            rows, nb = _pattern_rows(t, dil)
            o, lse = _att_pattern(_take(q_ref, rows), _take(k_ref, rows), _take(v_ref, rows), qn_ref[...], kn_ref[...], nb)
            _put(o_refs[p], rows, o)
            _put(l_refs[p], rows, lse)

        def merge(j, carry):
            rows = pl.ds(pl.multiple_of(j * ATT_BLOCK, ATT_BLOCK), ATT_BLOCK)
            att_ref[0, rows, :] = _merge_fn(*[r[0, rows, :] for r in saved])
            return carry

        lax.fori_loop(0, t // ATT_BLOCK, merge, 0)

    return pl.pallas_call(
        body, name="att_fwd", grid=(nh,), out_shape=(jax.ShapeDtypeStruct(q.shape, F32),) * 7,
        in_specs=[head, head, head, gain, gain], out_specs=(head,) * 7, compiler_params=_params("arbitrary"),
    )(q, k, v, qn, kn)


def _att_bwd(q, k, v, qn, kn, saved, datt):
    nh, t, dh = q.shape
    head, gain = _att_head_specs(t)

    def body(q_ref, k_ref, v_ref, qn_ref, kn_ref, o1, o2, o3, l1, l2, l3, datt_ref,
             dq_ref, dk_ref, dv_ref, dqn_ref, dkn_ref):
        for ref in (dq_ref, dk_ref, dv_ref):
            ref[...] = jnp.zeros_like(ref)

        @pl.when(pl.program_id(0) == 0)
        def _():
            dqn_ref[...] = jnp.zeros_like(dqn_ref)
            dkn_ref[...] = jnp.zeros_like(dkn_ref)

        for p, dil in enumerate(DILATIONS):
            rows, nb = _pattern_rows(t, dil)
            _, merge_vjp = jax.vjp(_merge_fn, *[_take(r, rows) for r in (o1, o2, o3, l1, l2, l3)])
            cts = merge_vjp(_take(datt_ref, rows))
            _, pattern_vjp = jax.vjp(functools.partial(_att_pattern, nb=nb), _take(q_ref, rows), _take(k_ref, rows),
                                     _take(v_ref, rows), qn_ref[...], kn_ref[...])
            dq, dk, dv, dgq, dgk = pattern_vjp((cts[p], cts[3 + p]))
            _put_add(dq_ref, rows, dq)
            _put_add(dk_ref, rows, dk)
            _put_add(dv_ref, rows, dv)
            dqn_ref[...] += dgq
            dkn_ref[...] += dgk

    hshape = jax.ShapeDtypeStruct(q.shape, F32)
    gshape = jax.ShapeDtypeStruct((1, 1, dh), F32)
    return pl.pallas_call(
        body, name="att_bwd", grid=(nh,), out_shape=(hshape, hshape, hshape, gshape, gshape),
        in_specs=[head, head, head, gain, gain] + [head] * 7, out_specs=(head, head, head, gain, gain),
        compiler_params=_params("arbitrary"),
    )(q, k, v, qn, kn, *saved, datt)


RWKV_VEC = ("mu_r", "mu_k", "mu_v", "mu_w", "mu_a", "mu_g", "w0", "a0", "k_k", "k_a")
RWKV_MAT = ("w1", "w2", "a1", "a2", "g1", "g2")


def _rwkv_pre_fn(cur, prev, vec, w1, w2, a1, a2, g1, g2):
    c = cur.shape[1] // 4
    mu_r, mu_k, mu_v, mu_w, mu_a, mu_g, w0, a0, k_k, k_a = (vec[j:j + 1] for j in range(10))

    def lerp(j, mu):
        xc, xp = cur[:, j * c:(j + 1) * c], prev[:, j * c:(j + 1) * c]
        return xc + (xp - xc) * mu

    r, k, v = lerp(0, mu_r), lerp(1, mu_k), lerp(2, mu_v)
    cw, ca, cg = lerp(3, mu_w), lerp(3, mu_a), lerp(3, mu_g)
    z = w0 + _mm(jnp.tanh(_mm(cw, w1)), w2)
    w_log = jnp.minimum(z, 0.0) - jnp.log(1.0 + jnp.exp(-jnp.abs(z))) - 0.5
    lw = -jnp.exp(w_log)
    a = _sigmoid(a0 + _mm(_mm(ca, a1), a2))
    gate = _mm(_sigmoid(_mm(cg, g1)), g2)
    kkraw = k * k_k
    kmod = k * (1.0 + (a - 1.0) * k_a)
    return r, lw, kmod, v, kkraw, a, gate


HALO_ROWS = 8


def _rwkv_pre_specs(c, mats, tile_of):
    tm = TOKEN_TILE
    nh = c // HEAD_DIM
    wide = pl.BlockSpec((tm, 4 * c), lambda j: (tile_of(j), 0))
    halo = pl.BlockSpec((HALO_ROWS, 4 * c), lambda j: (jnp.maximum(tile_of(j) * (tm // HALO_ROWS) - 1, 0), 0))
    one = pl.BlockSpec((tm, c), lambda j: (tile_of(j), 0))
    heads = pl.BlockSpec((nh, tm, HEAD_DIM), lambda j: (0, tile_of(j), 0))
    vec = pl.BlockSpec((10, c), lambda j: (0, 0))
    mspecs = [pl.BlockSpec(m.shape, lambda j: (0, 0)) for m in mats]
    return wide, halo, one, heads, vec, mspecs


def _previous_rows(cur, halo, tile):
    first = jnp.where(tile > 0, halo[HALO_ROWS - 1:HALO_ROWS], 0.0)
    rows = lax.broadcasted_iota(jnp.int32, cur.shape, 0)
    return jnp.where(rows == 0, first, pltpu.roll(cur, 1, axis=0))


def _rwkv_pre_fwd(cur, vec, mats):
    t, c4 = cur.shape
    c = c4 // 4
    wide, halo, one, heads, vspec, mspecs = _rwkv_pre_specs(c, mats, lambda j: j)

    def body(cur_ref, halo_ref, vec_ref, *rest):
        mrefs, outs = rest[:6], rest[6:]
        cur_v = cur_ref[...]
        prev = _previous_rows(cur_v, halo_ref[...], pl.program_id(0))
        vals = _rwkv_pre_fn(cur_v, prev, vec_ref[...], *(m[...] for m in mrefs))
        for ref, val in zip(outs[:6], vals[:6]):
            _store_heads(ref, val)
        outs[6][...] = vals[6]

    hshape = jax.ShapeDtypeStruct((c // HEAD_DIM, t, HEAD_DIM), F32)
    return pl.pallas_call(
        body, name="rwkv_pre_fwd", grid=(t // TOKEN_TILE,), out_shape=(hshape,) * 6 + (jax.ShapeDtypeStruct((t, c), F32),),
        in_specs=[wide, halo, vspec] + mspecs, out_specs=(heads,) * 6 + (one,), compiler_params=_params("arbitrary"),
    )(cur, cur, vec, *mats)


def _rwkv_pre_bwd(cur, vec, mats, cts, dgate):
    t, c4 = cur.shape
    c = c4 // 4
    tm = TOKEN_TILE
    nt = t // tm
    wide, halo, one, heads, vspec, mspecs = _rwkv_pre_specs(c, mats, lambda j: nt - 1 - j)

    def body(cur_ref, halo_ref, vec_ref, *rest):
        mrefs, ctrefs, dgate_ref, outs, carry_ref = rest[:6], rest[6:12], rest[12], rest[13:-1], rest[-1]
        j = pl.program_id(0)

        @pl.when(j == 0)
        def _():
            carry_ref[...] = jnp.zeros_like(carry_ref)
            for ref in outs[1:]:
                ref[...] = jnp.zeros_like(ref)

        cur_v = cur_ref[...]
        prev = _previous_rows(cur_v, halo_ref[...], nt - 1 - j)
        _, vjp = jax.vjp(_rwkv_pre_fn, cur_v, prev, vec_ref[...], *(m[...] for m in mrefs))
        grads = vjp(tuple(_load_heads(r) for r in ctrefs) + (dgate_ref[...],))
        dprev = grads[1]
        rows = lax.broadcasted_iota(jnp.int32, dprev.shape, 0)
        outs[0][...] = grads[0] + jnp.where(rows == tm - 1, carry_ref[0:1], pltpu.roll(dprev, tm - 1, axis=0))
        carry_ref[0:1] = dprev[0:1]
        for ref, val in zip(outs[1:], grads[2:]):
            ref[...] += val

    return pl.pallas_call(
        body, name="rwkv_pre_bwd", grid=(nt,),
        out_shape=(jax.ShapeDtypeStruct(cur.shape, F32), jax.ShapeDtypeStruct(vec.shape, F32))
        + tuple(jax.ShapeDtypeStruct(m.shape, F32) for m in mats),
        in_specs=[wide, halo, vspec] + mspecs + [heads] * 6 + [one], out_specs=(wide, vspec) + tuple(mspecs),
        scratch_shapes=[pltpu.VMEM((HALO_ROWS, c4), F32)], compiler_params=_params("arbitrary"),
    )(cur, cur, vec, *mats, *cts, dgate)


def _scan_chunk_fn(h0, r, lw, k, v, kkraw, a, rk, lnw, lnb):
    n = r.shape[1]
    nrm = jnp.sqrt(jnp.sum(kkraw * kkraw, axis=-1, keepdims=True))
    kk = kkraw / jnp.maximum(nrm, 1e-12)
    av, bv = -kk, kk * a
    ti = lax.broadcasted_iota(jnp.int32, (n, n), 0)
    si = lax.broadcasted_iota(jnp.int32, (n, n), 1)
    incl, strict = ti >= si, ti > si
    ones = jnp.broadcast_to(incl.astype(F32)[None], (r.shape[0], n, n))
    cum = _hdot(ones, lw, 2, 1)
    at, rt = av * jnp.exp(cum - lw), r * jnp.exp(cum)
    inv = jnp.exp(-cum)
    bt, kt = bv * inv, k * inv
    lab = jnp.where(strict, _hdot(at, bt, 2, 2), 0.0)
    lak = jnp.where(strict, _hdot(at, kt, 2, 2), 0.0)
    rb = jnp.where(incl, _hdot(rt, bt, 2, 2), 0.0)
    rkm = jnp.where(incl, _hdot(rt, kt, 2, 2), 0.0)
    u = _bmm_nn(at, h0) + _bmm_nn(lak, v)
    p = lab
    m = 1
    while m < n:
        u = u + _bmm_nn(p, u)
        m *= 2
        if m < n:
            p = _bmm_nn(p, p)
    y = _bmm_nn(rt, h0) + _bmm_nn(rb, u) + _bmm_nn(rkm, v)
    last = jnp.exp(jnp.sum(lw, axis=1, keepdims=True))
    h1 = jnp.swapaxes(last, 1, 2) * (h0 + _bmm_tn(bt, u) + _bmm_tn(kt, v))
    mean = jnp.mean(y, axis=-1, keepdims=True)
    yc = y - mean
    var = jnp.mean(yc * yc, axis=-1, keepdims=True)
    yn = yc * lax.rsqrt(var + GN_EPS) * lnw + lnb
    bonus = jnp.sum(r * k * rk, axis=-1, keepdims=True) * v
    return yn + bonus, h1


def _scan_specs(h, t, dh, rev):
    n = SCAN_CHUNK
    nc = t // n
    pos = (lambda c: (0, nc - 1 - c, 0)) if rev else (lambda c: (0, c, 0))
    st = (lambda c: (nc - 1 - c, 0, 0, 0)) if rev else (lambda c: (c, 0, 0, 0))
    seq = pl.BlockSpec((h, n, dh), pos)
    par = pl.BlockSpec((h, 1, dh), lambda c: (0, 0, 0))
    state = pl.BlockSpec((1, h, dh, dh), st)
    return seq, par, state


def _scan_fwd(seqs, pars):
    h, t, dh = seqs[0].shape
    nc = t // SCAN_CHUNK
    seq, par, state = _scan_specs(h, t, dh, False)

    def body(r, lw, k, v, kkraw, a, rk, lnw, lnb, o_ref, st_ref, h_ref):
        @pl.when(pl.program_id(0) == 0)
        def _():
            h_ref[...] = jnp.zeros_like(h_ref)

        h0 = h_ref[...]
        st_ref[0] = h0
        o, h1 = _scan_chunk_fn(h0, r[...], lw[...], k[...], v[...], kkraw[...], a[...], rk[...], lnw[...], lnb[...])
        o_ref[...] = o
        h_ref[...] = h1

    return pl.pallas_call(
        body, name="rwkv_scan_fwd", grid=(nc,),
        out_shape=(jax.ShapeDtypeStruct((h, t, dh), F32), jax.ShapeDtypeStruct((nc, h, dh, dh), F32)),
        in_specs=[seq] * 6 + [par] * 3, out_specs=(seq, state),
        scratch_shapes=[pltpu.VMEM((h, dh, dh), F32)], compiler_params=_params("arbitrary"),
    )(*seqs, *pars)


def _scan_bwd(seqs, pars, states, do):
    h, t, dh = seqs[0].shape
    nc = t // SCAN_CHUNK
    seq, par, state = _scan_specs(h, t, dh, True)

    def body(r, lw, k, v, kkraw, a, rk, lnw, lnb, st_ref, do_ref, *rest):
        douts, dpars, dh_ref = rest[:6], rest[6:9], rest[9]
        first = pl.program_id(0) == 0

        @pl.when(first)
        def _():
            dh_ref[...] = jnp.zeros_like(dh_ref)

        _, vjp = jax.vjp(_scan_chunk_fn, st_ref[0], r[...], lw[...], k[...], v[...], kkraw[...], a[...],
                         rk[...], lnw[...], lnb[...])
        grads = vjp((do_ref[...], dh_ref[...]))
        dh_ref[...] = grads[0]
        for ref, val in zip(douts, grads[1:7]):
            ref[...] = val

        @pl.when(first)
        def _():
            for ref, val in zip(dpars, grads[7:]):
                ref[...] = val

        @pl.when(jnp.logical_not(first))
        def _():
            for ref, val in zip(dpars, grads[7:]):
                ref[...] += val

    sshape = jax.ShapeDtypeStruct((h, t, dh), F32)
    pshape = jax.ShapeDtypeStruct((h, 1, dh), F32)
    return pl.pallas_call(
        body, name="rwkv_scan_bwd", grid=(nc,), out_shape=(sshape,) * 6 + (pshape,) * 3,
        in_specs=[seq] * 6 + [par] * 3 + [state, seq], out_specs=(seq,) * 6 + (par,) * 3,
        scratch_shapes=[pltpu.VMEM((h, dh, dh), F32)], compiler_params=_params("arbitrary"),
    )(*seqs, *pars, states, do)


def _local_step(x, target, w, ex):
    w = dict(w)
    c = w["mu_r"].shape[-1]
    qn, kn = w["q_norm"].reshape(1, 1, HEAD_DIM), w["k_norm"].reshape(1, 1, HEAD_DIM)
    vec = jnp.concatenate([w[n].reshape(1, c) for n in RWKV_VEC], axis=0)
    pars = [w[n].reshape(-1, 1, HEAD_DIM) for n in ("r_k", "ln_x_w", "ln_x_b")]
    no_dep = jnp.zeros(DEP_SHAPE, F32)

    x1 = _ffn_fwd(x, w["ffn1_norm"], w["ffn1_w_gate"], w["ffn1_w_up"], w["ffn1_w_down"], ex.first_dep, "ffn1_fwd")
    w.update(ex.mix_weights((x1,)))
    mats = [w[n] for n in RWKV_MAT]
    q, k, v, cur = _proj_fwd(x1, w["mix_norm"], w["w_in"])
    att, *saved = _att_fwd(q, k, v, qn, kn)
    pre = _rwkv_pre_fwd(cur, vec, mats)
    seqs, gate = pre[:6], pre[6]
    opg, states = _scan_fwd(seqs, pars)
    w.update(ex.out_weights((att, opg)))
    x2 = _mixout_fwd(x1, att, opg, gate, w["w_out"])
    x3 = _ffn_fwd(x2, w["ffn2_norm"], w["ffn2_w_gate"], w["ffn2_w_up"], w["ffn2_w_down"], no_dep, "ffn2_fwd")
    dy, loss = _loss_head(x3, target)

    g = {}
    dx2, g["ffn2_norm"], g["ffn2_w_gate"], g["ffn2_w_up"], g["ffn2_w_down"] = _ffn_bwd(
        x2, w["ffn2_norm"], w["ffn2_w_gate"], w["ffn2_w_up"], w["ffn2_w_down"], dy, no_dep, "ffn2_bwd")
    dep = ex.send_ffn2({n: g[n] for n in ("ffn2_w_gate", "ffn2_w_up", "ffn2_w_down")})
    datt, dopg, dgate, g["w_out"] = _mixout_bwd(att, opg, gate, w["w_out"], dx2, dep)
    dscan = _scan_bwd(seqs, pars, states, dopg)
    for n, d in zip(("r_k", "ln_x_w", "ln_x_b"), dscan[6:]):
        g[n] = d
    dcur, dvec, *dmats = _rwkv_pre_bwd(cur, vec, mats, dscan[:6], dgate)
    for n, d in zip(RWKV_MAT, dmats):
        g[n] = d
    for j, n in enumerate(RWKV_VEC):
        g[n] = dvec[j:j + 1]
    dq, dk, dv, g["q_norm"], g["k_norm"] = _att_bwd(q, k, v, qn, kn, saved, datt)
    dx1, g["mix_norm"], g["w_in"] = _proj_bwd(x1, w["mix_norm"], w["w_in"], dq, dk, dv, dcur, dx2)
    dep = ex.send_mix({n: g[n] for n in ("w_in", "w_out") + RWKV_MAT}, (dx1,))
    dx, g["ffn1_norm"], g["ffn1_w_gate"], g["ffn1_w_up"], g["ffn1_w_down"] = _ffn_bwd(
        x, w["ffn1_norm"], w["ffn1_w_gate"], w["ffn1_w_up"], w["ffn1_w_down"], dx1, dep, "ffn1_bwd")
    return loss, dx, g


N_SHARDS = 4


def _place():
    return lax.axis_index("x"), lax.axis_index("y"), lax.axis_index("c")


def _chip_peers(x, y):
    return [(1 - x, y), (x, 1 - y), (1 - x, 1 - y)]


HBM = pl.BlockSpec(memory_space=pltpu.HBM)
SEM = pl.BlockSpec(memory_space=pltpu.SEMAPHORE)
DEP_SHAPE = (8, 128)


class _Views:
    to_sibling = False


class _GatherViews(_Views):
    @staticmethod
    def send(i, srcs, lands, k, at):
        return srcs[i], lands[i].at[at[3]]

    @staticmethod
    def landing(i, srcs, lands, k, at):
        return srcs[i], lands[i].at[2 * at[4] + at[5]]


class _ScatterViews(_Views):
    @staticmethod
    def send(i, srcs, lands, k, at):
        return srcs[i].at[2 * at[4] + at[5]], lands[i].at[k]

    @staticmethod
    def landing(i, srcs, lands, k, at):
        return srcs[i].at[at[3]], lands[i].at[k]


def _half_rows(ref, slot, half):
    rows = ref.shape[1] // 2
    return ref.at[slot, pl.ds(pl.multiple_of(half * rows, BF16_SUBLANES), rows)]


class _HalfGatherViews(_Views):
    @staticmethod
    def send(i, srcs, lands, k, at):
        rows = srcs[i].shape[0] // 2
        return srcs[i].at[pl.ds(pl.multiple_of(at[2] * rows, BF16_SUBLANES), rows)], _half_rows(lands[i], at[3], at[2])

    @staticmethod
    def landing(i, srcs, lands, k, at):
        rows = srcs[i].shape[0] // 2
        return srcs[i].at[pl.ds(pl.multiple_of(at[2] * rows, BF16_SUBLANES), rows)], _half_rows(lands[i], 2 * at[4] + at[5], at[2])


class _ForwardViews(_Views):
    to_sibling = True

    @staticmethod
    def send(i, srcs, lands, k, at):
        mine = _half_rows(lands[i], 2 * at[4] + at[5], at[2])
        return mine, mine

    @staticmethod
    def landing(i, srcs, lands, k, at):
        theirs = _half_rows(lands[i], 2 * at[4] + at[5], 1 - at[2])
        return theirs, theirs


def _push_start(srcs, lands, views, after, name):
    ns, nl = len(srcs), len(lands)

    def body(*refs):
        src_refs, land_refs = refs[:ns], refs[ns:ns + nl]
        send_sems, recv_sems = refs[ns + nl + 1:ns + nl + 3]
        token = refs[2 * (ns + nl) + 3]
        x, y, c = _place()
        for i in range(nl):
            for k, (px, py) in enumerate(_chip_peers(x, y)):
                src, dst = views.send(i, src_refs, land_refs, k, (x, y, c, 2 * x + y, px, py))
                pltpu.make_async_remote_copy(
                    src_ref=src, dst_ref=dst, send_sem=send_sems.at[3 * i + k], recv_sem=recv_sems.at[3 * i + k],
                    device_id=(x, y, 1 - c) if views.to_sibling else (px, py, c), device_id_type=MESH).start()
        token[...] = jnp.zeros_like(token)

    sems = pltpu.SemaphoreType.DMA((3 * nl,))
    both = [pltpu.with_memory_space_constraint(a, pltpu.HBM) for a in (*srcs, *lands)]
    outs = pl.pallas_call(
        body, name=name,
        out_shape=(sems, sems, *[pltpu.HBM(a.shape, a.dtype) for a in both], jax.ShapeDtypeStruct(DEP_SHAPE, F32)),
        in_specs=[HBM] * (ns + nl) + [ANY], out_specs=(SEM, SEM, *[HBM] * (ns + nl), VMEM_FULL),
        input_output_aliases={i: 2 + i for i in range(ns + nl)},
        compiler_params=pltpu.CompilerParams(has_side_effects=pltpu.SideEffectType.DATAFLOW_SIDE_EFFECTING),
    )(*both, after)
    return outs[0], outs[1], outs[2:2 + ns], outs[2 + ns:2 + ns + nl], outs[2 + ns + nl]


def _push_wait(started, views, after, name):
    send_sems, recv_sems, srcs, lands, _ = started
    ns, nl = len(srcs), len(lands)

    def body(*refs):
        src_refs, land_refs = refs[:ns], refs[ns:ns + nl]
        send_sems, recv_sems = refs[ns + nl:ns + nl + 2]
        x, y, c = _place()
        for i in range(nl):
            for k, (px, py) in enumerate(_chip_peers(x, y)):
                src, dst = views.landing(i, src_refs, land_refs, k, (x, y, c, 2 * x + y, px, py))
                landing = pltpu.make_async_remote_copy(
                    src_ref=src, dst_ref=dst, send_sem=send_sems.at[3 * i + k], recv_sem=recv_sems.at[3 * i + k],
                    device_id=(x, y, 1 - c) if views.to_sibling else (px, py, c), device_id_type=MESH)
                landing.wait_send()
                landing.wait_recv()

    outs = pl.pallas_call(
        body, name=name,
        out_shape=tuple(pltpu.HBM(a.shape, a.dtype) for a in (*srcs, *lands)),
        in_specs=[HBM] * (ns + nl) + [SEM, SEM] + [ANY] * len(after), out_specs=(HBM,) * (ns + nl),
        input_output_aliases={i: i for i in range(ns + nl)},
        compiler_params=pltpu.CompilerParams(has_side_effects=pltpu.SideEffectType.DATAFLOW_SIDE_EFFECTING),
    )(*srcs, *lands, send_sems, recv_sems, *after)
    return outs[ns:]


def _empty_lands(shards, slots, own_slot):
    lands = [lax.empty((slots,) + s.shape, s.dtype) for s in shards]
    if own_slot:
        me = 2 * lax.axis_index("x") + lax.axis_index("y")
        lands = [lax.dynamic_update_index_in_dim(z, s, me, 0) for z, s in zip(lands, shards)]
    return lands


def _sibling_swap(arrays, name):
    n = len(arrays)

    def body(*refs):
        ins, outs = refs[:n], refs[n:2 * n]
        send_sems, recv_sems = refs[2 * n:]
        x, y, c = _place()
        copies = []
        for i in range(n):
            cp = pltpu.make_async_remote_copy(
                src_ref=ins[i], dst_ref=outs[i], send_sem=send_sems.at[i], recv_sem=recv_sems.at[i],
                device_id=(x, y, 1 - c), device_id_type=MESH)
            cp.start()
            copies.append(cp)
        for cp in copies:
            cp.wait()

    return pl.pallas_call(
        body, name=name,
        out_shape=tuple(jax.ShapeDtypeStruct(a.shape, a.dtype) for a in arrays),
        in_specs=[ANY] * n, out_specs=(ANY,) * n,
        scratch_shapes=[pltpu.SemaphoreType.DMA((n,)), pltpu.SemaphoreType.DMA((n,))],
    )(*arrays)


N_DEV = 8


def _allreduce_small(pack):
    def body(in_ref, out_ref, buf, send_sems, recv_sems):
        x, y, c = _place()
        me = 4 * x + 2 * y + c
        buf[me] = in_ref[...]

        def copy(j, slot):
            px, py, pc = x ^ (j >> 2), y ^ ((j >> 1) & 1), c ^ (j & 1)
            return pltpu.make_async_remote_copy(
                src_ref=in_ref, dst_ref=buf.at[slot(px, py, pc)], send_sem=send_sems.at[j], recv_sem=recv_sems.at[j],
                device_id=(px, py, pc), device_id_type=MESH)

        for j in range(1, N_DEV):
            copy(j, lambda px, py, pc: me).start()
        for j in range(1, N_DEV):
            landing = copy(j, lambda px, py, pc: 4 * px + 2 * py + pc)
            landing.wait_send()
            landing.wait_recv()
        acc = buf[0]
        for s in range(1, N_DEV):
            acc = acc + buf[s]
        out_ref[...] = acc

    return pl.pallas_call(
        body, name="allreduce_small", out_shape=jax.ShapeDtypeStruct(pack.shape, F32),
        in_specs=[VMEM_FULL], out_specs=VMEM_FULL,
        scratch_shapes=[pltpu.VMEM((N_DEV,) + pack.shape, F32), pltpu.SemaphoreType.DMA((N_DEV,)),
                        pltpu.SemaphoreType.DMA((N_DEV,))],
    )(pack)


ROW_TILE_MAX = 256
BF16_SUBLANES = 16


def _row_tile(rows):
    for tr in range(min(rows, ROW_TILE_MAX), 0, -1):
        if rows % tr == 0 and tr % BF16_SUBLANES == 0:
            return tr
    return rows


def _reduce_own(me, part, recv, dep, name):
    _, r, cols = part.shape
    tr = _row_tile(r)

    def body(me_ref, p_ref, rv_ref, dep_ref, o_ref):
        acc = p_ref[0].astype(F32)
        for k in range(3):
            acc = acc + rv_ref[k].astype(F32)
        o_ref[...] = acc

    return pl.pallas_call(
        body, name=name, out_shape=jax.ShapeDtypeStruct((r, cols), F32),
        grid_spec=pltpu.PrefetchScalarGridSpec(
            num_scalar_prefetch=1, grid=(r // tr,),
            in_specs=[pl.BlockSpec((1, tr, cols), lambda i, me_ref: (me_ref[0], i, 0)),
                      pl.BlockSpec((3, tr, cols), lambda i, me_ref: (0, i, 0)), ANY],
            out_specs=pl.BlockSpec((tr, cols), lambda i, me_ref: (i, 0))),
        compiler_params=_params("arbitrary"),
    )(me, part, recv, dep)


def _adamw(w, ga, gb, m, v, name):
    r, cols = w.shape
    tr = _row_tile(r)
    c1 = 1.0 - ADAM_B1 ** ADAM_STEP
    c2 = 1.0 - ADAM_B2 ** ADAM_STEP

    def body(w_ref, ga_ref, gb_ref, m_ref, v_ref, g_out, d_out, m_out, v_out):
        g = ga_ref[...] + gb_ref[...]
        mn = ADAM_B1 * m_ref[...] + (1.0 - ADAM_B1) * g
        vn = ADAM_B2 * v_ref[...] + (1.0 - ADAM_B2) * (g * g)
        g_out[...] = g
        m_out[...] = mn
        v_out[...] = vn
        d_out[...] = -ADAM_LR * ((mn / c1) / (jnp.sqrt(vn / c2) + ADAM_EPS) + ADAM_WD * w_ref[...])

    tile = pl.BlockSpec((tr, cols), lambda i: (i, 0))
    shape = jax.ShapeDtypeStruct((r, cols), F32)
    return pl.pallas_call(
        body, name=name, grid=(r // tr,), out_shape=(shape,) * 4, in_specs=[tile] * 5, out_specs=(tile,) * 4,
        compiler_params=_params("arbitrary"),
    )(w, ga, gb, m, v)


PACK_COLS = 512


def _to_rows(a):
    flat = a.reshape(-1)
    pad = (-flat.shape[0]) % PACK_COLS
    return jnp.pad(flat, (0, pad)).reshape(-1, PACK_COLS)


def _pack(arrays, extra_rows=0):
    rows = [_to_rows(a) for a in arrays]
    n = sum(r.shape[0] for r in rows) + extra_rows
    pad = (-n) % 8
    return jnp.concatenate(rows + [jnp.zeros((extra_rows + pad, PACK_COLS), F32)], axis=0)


def _unpack(pack, like):
    out, at = [], 0
    for a in like:
        n = -(-a.size // PACK_COLS)
        out.append(pack[at:at + n].reshape(-1)[:a.size].reshape(a.shape))
        at += n
    return out


COL_SHARDED = ("ffn1_w_gate", "ffn1_w_up", "w_in", "ffn2_w_gate", "ffn2_w_up", "w2", "a2", "g2")
ROW_SHARDED = ("ffn1_w_down", "ffn2_w_down", "w_out", "w1", "a1", "g1")
CHUNKED = ("ffn1_w_gate", "ffn1_w_up", "ffn1_w_down", "ffn2_w_gate", "ffn2_w_up", "ffn2_w_down")
WEIGHTS = ("ffn1_norm", "ffn1_w_gate", "ffn1_w_up", "ffn1_w_down", "mix_norm", "w_in", "q_norm", "k_norm",
           "mu_r", "mu_k", "mu_v", "mu_w", "mu_a", "mu_g", "w0", "w1", "w2", "a0", "a1", "a2", "g1", "g2",
           "k_k", "k_a", "r_k", "ln_x_w", "ln_x_b", "w_out", "ffn2_norm", "ffn2_w_gate", "ffn2_w_up", "ffn2_w_down")


W_IN_GROUPS = 7
TRANSPOSED = ("ffn1_w_gate", "ffn1_w_up", "ffn2_w_gate", "ffn2_w_up")


def _shard_2d(name, a):
    return a[0].T if name in TRANSPOSED else a[0]


def _full_from_blocks(name, blocks):
    if name in CHUNKED:
        return blocks
    if name in ROW_SHARDED:
        return blocks.reshape(-1, blocks.shape[-1])
    full = blocks.transpose(1, 0, 2).reshape(blocks.shape[1], -1)
    if name == "w_in":
        return full.reshape(full.shape[0], W_IN_GROUPS, -1).transpose(1, 0, 2)
    return full


def _blocks_from_full(name, full):
    if name in CHUNKED:
        return full
    if name in ROW_SHARDED:
        return full.reshape(N_SHARDS, -1, full.shape[-1])
    if name == "w_in":
        full = full.transpose(1, 0, 2).reshape(full.shape[1], -1)
    return full.reshape(full.shape[0], N_SHARDS, -1).transpose(1, 0, 2)


FFN1_GROUP = ("ffn1_w_gate", "ffn1_w_up", "ffn1_w_down")
MIX_GROUP = ("w_in",) + RWKV_MAT
OUT_GROUP = ("w_out", "ffn2_w_gate", "ffn2_w_up", "ffn2_w_down")
FFN2_GROUP = OUT_GROUP[1:]
LATE_GROUP = ("w_in", "w_out") + RWKV_MAT


class _Exchange:
    def __init__(self, given):
        self.given = given
        first = self._gather_start(FFN1_GROUP, _HalfGatherViews, jnp.zeros(DEP_SHAPE, F32), "gather_ffn1_start")
        self.mix = self._gather_start(MIX_GROUP, _GatherViews, first[4], "gather_mix_start")
        self.out = self._gather_start(OUT_GROUP, _GatherViews, self.mix[4], "gather_out_start")
        self.first_dep = self.out[4]
        halves = _push_wait(first, _HalfGatherViews, (self.first_dep,), "gather_ffn1_wait")
        passed = _push_start([], halves, _ForwardViews, halves[0], "gather_ffn1_pass_start")
        self.first_weights = self._full(FFN1_GROUP, _push_wait(passed, _ForwardViews, (passed[4],), "gather_ffn1_pass_wait"))
        self.parts, self.recv = {}, {}

    def _shards(self, names):
        return [_shard_2d(n, self.given[n]).astype(BF16) for n in names]

    @staticmethod
    def _full(names, blocks):
        out = {}
        for n, b in zip(names, blocks):
            full = _full_from_blocks(n, b)
            out[n] = full.astype(F32) if n in RWKV_MAT else full
        return out

    def _gather_start(self, names, views, after, name):
        shards = self._shards(names)
        return _push_start(shards, _empty_lands(shards, N_SHARDS, True), views, after, name)

    def mix_weights(self, after):
        return self._full(MIX_GROUP, _push_wait(self.mix, _GatherViews, after, "gather_mix_wait"))

    def out_weights(self, after):
        return self._full(OUT_GROUP, _push_wait(self.out, _GatherViews, after, "gather_out_wait"))

    def _scatter_start(self, grads, name):
        names = tuple(grads)
        parts = [_blocks_from_full(n, grads[n]) for n in names]
        self.parts.update(zip(names, parts))
        lands = [lax.empty((3,) + p.shape[1:], BF16) for p in parts]
        return _push_start([p.astype(BF16) for p in parts], lands, _ScatterViews, parts[0], name)

    def send_ffn2(self, grads):
        self.ffn2 = self._scatter_start(grads, "scatter_ffn2_start")
        return self.ffn2[4]

    def send_mix(self, grads, after):
        self.recv.update(zip(FFN2_GROUP, _push_wait(self.ffn2, _ScatterViews, after, "scatter_ffn2_wait")))
        self.late = self._scatter_start(grads, "scatter_late_start")
        return self.late[4]

    def send_ffn1(self, grads):
        self.ffn1 = self._scatter_start(grads, "scatter_ffn1_start")
        return self.ffn1[4]

    def late_received(self, after):
        self.recv.update(zip(LATE_GROUP, _push_wait(self.late, _ScatterViews, after, "scatter_late_wait")))

    def ffn1_received(self, after):
        self.recv.update(zip(FFN1_GROUP, _push_wait(self.ffn1, _ScatterViews, after, "scatter_ffn1_wait")))


def kernel(
        x, ffn1_norm, ffn1_w_gate, ffn1_w_up, ffn1_w_down, mix_norm, w_in, q_norm, k_norm, mu_r, mu_k, mu_v, mu_w,
        mu_a, mu_g, w0, w1, w2, a0, a1, a2, g1, g2, k_k, k_a, r_k, ln_x_w, ln_x_b, w_out, ffn2_norm, ffn2_w_gate,
        ffn2_w_up, ffn2_w_down, loss_target, m_ffn1_norm, m_ffn1_w_gate, m_ffn1_w_up, m_ffn1_w_down, m_mix_norm,
        m_w_in, m_q_norm, m_k_norm, m_mu_r, m_mu_k, m_mu_v, m_mu_w, m_mu_a, m_mu_g, m_w0, m_w1, m_w2, m_a0, m_a1,
        m_a2, m_g1, m_g2, m_k_k, m_k_a, m_r_k, m_ln_x_w, m_ln_x_b, m_w_out, m_ffn2_norm, m_ffn2_w_gate, m_ffn2_w_up,
        m_ffn2_w_down, v_ffn1_norm, v_ffn1_w_gate, v_ffn1_w_up, v_ffn1_w_down, v_mix_norm, v_w_in, v_q_norm, v_k_norm,
        v_mu_r, v_mu_k, v_mu_v, v_mu_w, v_mu_a, v_mu_g, v_w0, v_w1, v_w2, v_a0, v_a1, v_a2, v_g1, v_g2, v_k_k, v_k_a,
        v_r_k, v_ln_x_w, v_ln_x_b, v_w_out, v_ffn2_norm, v_ffn2_w_gate, v_ffn2_w_up, v_ffn2_w_down):
    given = dict(locals())
    sharded = COL_SHARDED + ROW_SHARDED
    sharded = tuple(n for n in WEIGHTS if n in sharded)
    small = tuple(n for n in WEIGHTS if n not in sharded)

    ex = _Exchange(given)
    w = {n: given[n] for n in small}
    w.update(ex.first_weights)
    loss, dx, g = _local_step(x[0], loss_target[0], w, ex)
    dep = ex.send_ffn1({n: g[n] for n in FFN1_GROUP})

    me = (2 * lax.axis_index("x") + lax.axis_index("y")).astype(jnp.int32).reshape(1)
    out = {}

    def settle(names, dep, tag):
        mine = []
        for n in names:
            p, rv = ex.parts[n], ex.recv[n]
            p2 = p.reshape(N_SHARDS, -1, p.shape[-1])
            mine.append(_reduce_own(me, p2, rv.reshape(3, -1, rv.shape[-1]), dep, f"reduce_{n}"))
        theirs = _sibling_swap(mine, f"sibling_swap_{tag}")
        for n, a, b in zip(names, mine, theirs):
            shape = given[n].shape
            res = _adamw(_shard_2d(n, given[n]), a, b, _shard_2d(n, given["m_" + n]), _shard_2d(n, given["v_" + n]), f"adamw_{n}")
            out[n] = [(r.T if n in TRANSPOSED else r).reshape(shape) for r in res]
        return tuple(out[n][1] for n in names)

    ex.late_received((dep,))
    last = settle(tuple(n for n in sharded if n not in FFN1_GROUP), dep, "rest")

    gpack = _pack([g[n] for n in small], extra_rows=1)
    n_rows = sum(-(-given[n].size // PACK_COLS) for n in small)
    gpack = gpack.at[n_rows, :loss.shape[1]].set(loss[0])
    gsum = _allreduce_small(gpack)
    res = _adamw(_pack([given[n] for n in small], 1), gsum, jnp.zeros_like(gsum), _pack([given["m_" + n] for n in small], 1),
                 _pack([given["v_" + n] for n in small], 1), "adamw_small")
    like = [given[n] for n in small]
    for j, r in enumerate(res):
        for n, a in zip(small, _unpack(r, like)):
            out.setdefault(n, [None] * 4)[j] = a
    total_loss = gsum[n_rows, 0]

    ex.ffn1_received((*last, res[1]))
    settle(FFN1_GROUP, jnp.zeros(DEP_SHAPE, F32), "ffn1")
    return (total_loss, dx[None], *[out[n][0] for n in WEIGHTS], *[out[n][1] for n in WEIGHTS],
            *[out[n][2] for n in WEIGHTS], *[out[n][3] for n in WEIGHTS])
```

```python
import functools

import jax
import jax.numpy as jnp
from jax import lax
from jax.experimental import pallas as pl
from jax.experimental.pallas import tpu as pltpu

F32 = jnp.float32
BF16 = jnp.bfloat16
MESH = pl.DeviceIdType.MESH

RMS_EPS = 1e-6
GN_EPS = 64e-5
NEG_INF = -1e30
FFN_RESIDUAL = 0.5
HEAD_DIM = 64
ATT_BLOCK = 128
DILATIONS = (1, 4, 16)
SCAN_CHUNK = 64
TOKEN_TILE = 256

ADAM_LR = 0.001
ADAM_B1 = 0.9
ADAM_B2 = 0.999
ADAM_EPS = 1e-08
ADAM_WD = 0.01
ADAM_STEP = 10

VMEM_FULL = pl.BlockSpec(memory_space=pltpu.VMEM)
ANY = pl.BlockSpec(memory_space=pl.ANY)


VMEM_LIMIT = 56 * 1024 * 1024


def _params(*sem):
    return pltpu.CompilerParams(dimension_semantics=sem, vmem_limit_bytes=VMEM_LIMIT)


def _dot(a, b, dims):
    return lax.dot_general(a.astype(BF16), b.astype(BF16), (dims, ((), ())), preferred_element_type=F32)


def _dot_nn(a, b):
    return _dot(a, b, ((1,), (0,)))


def _dot_nt(a, b):
    return _dot(a, b, ((1,), (1,)))


def _dot_tn(a, b):
    return _dot(a, b, ((0,), (0,)))


@jax.custom_vjp
def _mm(a, b):
    return _dot_nn(a, b)


def _mm_fwd(a, b):
    return _dot_nn(a, b), (a, b)


def _mm_bwd(res, g):
    a, b = res
    return _dot_nt(g, b).astype(a.dtype), _dot_tn(a, g).astype(b.dtype)


_mm.defvjp(_mm_fwd, _mm_bwd)


def _bdot(a, b, ca, cb):
    return lax.dot_general(a.astype(BF16), b.astype(BF16), (((ca,), (cb,)), ((0,), (0,))), preferred_element_type=F32)


@jax.custom_vjp
def _bmm_nt(a, b):
    return _bdot(a, b, 2, 2)


def _bmm_nt_fwd(a, b):
    return _bdot(a, b, 2, 2), (a, b)


def _bmm_nt_bwd(res, g):
    a, b = res
    return _bdot(g, b, 2, 1), _bdot(g, a, 1, 1)


_bmm_nt.defvjp(_bmm_nt_fwd, _bmm_nt_bwd)


@jax.custom_vjp
def _bmm_nn(a, b):
    return _bdot(a, b, 2, 1)


def _bmm_nn_fwd(a, b):
    return _bdot(a, b, 2, 1), (a, b)


def _bmm_nn_bwd(res, g):
    a, b = res
    return _bdot(g, b, 2, 2), _bdot(a, g, 1, 1)


_bmm_nn.defvjp(_bmm_nn_fwd, _bmm_nn_bwd)


@jax.custom_vjp
def _bmm_tn(a, b):
    return _bdot(a, b, 1, 1)


def _bmm_tn_fwd(a, b):
    return _bdot(a, b, 1, 1), (a, b)


def _bmm_tn_bwd(res, g):
    a, b = res
    return _bdot(b, g, 2, 2), _bdot(a, g, 2, 1)


_bmm_tn.defvjp(_bmm_tn_fwd, _bmm_tn_bwd)


def _hdot(a, b, ca, cb):
    return lax.dot_general(a, b, (((ca,), (cb,)), ((0,), (0,))), precision=lax.Precision.HIGH, preferred_element_type=F32)


def _sigmoid(x):
    return 1.0 / (1.0 + jnp.exp(-x))


def _rms(x):
    return lax.rsqrt(jnp.mean(x * x, axis=-1, keepdims=True) + RMS_EPS)


def _ffn_fwd(x, norm, wg, wu, wd, dep, name):
    t, d = x.shape
    nc = wg.shape[0]
    tm = TOKEN_TILE

    def body(x_ref, n_ref, wg_ref, wu_ref, wd_ref, dep_ref, o_ref):
        xv = x_ref[...]
        h = (xv * _rms(xv) * n_ref[...]).astype(BF16)
        acc = jnp.zeros((tm, d), F32)
        for c in range(nc):
            g = _dot_nt(h, wg_ref[c])
            u = _dot_nt(h, wu_ref[c])
            a = (g * _sigmoid(g) * u).astype(BF16)
            acc = acc + jnp.dot(a, wd_ref[c], preferred_element_type=F32)
        o_ref[...] = xv + FFN_RESIDUAL * acc

    tile = pl.BlockSpec((tm, d), lambda i: (i, 0))
    return pl.pallas_call(
        body, name=name, grid=(t // tm,), out_shape=jax.ShapeDtypeStruct((t, d), F32),
        in_specs=[tile, pl.BlockSpec((1, d), lambda i: (0, 0)), VMEM_FULL, VMEM_FULL, VMEM_FULL, ANY],
        out_specs=tile, compiler_params=_params("arbitrary"),
    )(x, norm, wg, wu, wd, dep)


def _rmsnorm_bwd(xv, gain, dh):
    rs = _rms(xv)
    xn = xv * rs
    dxn = dh * gain
    dx = rs * (dxn - xn * jnp.mean(dxn * xn, axis=-1, keepdims=True))
    return dx, jnp.sum(dh * xn, axis=0, keepdims=True)


def _ffn_bwd(x, norm, wg, wu, wd, dy, dep, name):
    t, d = x.shape
    nc, fc, _ = wg.shape
    tm = TOKEN_TILE
    nt = t // tm

    def body(x_ref, n_ref, wg_ref, wu_ref, wd_ref, dy_ref, dep_ref, dx_ref, dn_ref, dwg_ref, dwu_ref, dwd_ref,
             dh_ref, ag_ref, au_ref, ad_ref):
        c, i = pl.program_id(0), pl.program_id(1)
        rows = pl.ds(pl.multiple_of(i * tm, tm), tm)
        xv = x_ref[...]
        gain = n_ref[...]
        h = (xv * _rms(xv) * gain).astype(BF16)
        dy = dy_ref[...]
        dyb = (FFN_RESIDUAL * dy).astype(BF16)
        g = _dot_nt(h, wg_ref[0])
        u = _dot_nt(h, wu_ref[0])
        sg = _sigmoid(g)
        s = g * sg
        a = (s * u).astype(BF16)
        da = _dot_nt(dyb, wd_ref[0])
        dub = (da * s).astype(BF16)
        dgb = (da * u * (sg * (1.0 + g * (1.0 - sg)))).astype(BF16)
        dwd_c = _dot_tn(a, dyb)
        dwg_c = _dot_tn(dgb, h)
        dwu_c = _dot_tn(dub, h)
        dh_c = _dot_nn(dgb, wg_ref[0]) + _dot_nn(dub, wu_ref[0])

        @pl.when(i == 0)
        def _():
            ad_ref[...] = dwd_c
            ag_ref[...] = dwg_c
            au_ref[...] = dwu_c

        @pl.when(i > 0)
        def _():
            ad_ref[...] += dwd_c
            ag_ref[...] += dwg_c
            au_ref[...] += dwu_c

        @pl.when(i == nt - 1)
        def _():
            dwd_ref[0] = ad_ref[...].astype(BF16)
            dwg_ref[0] = ag_ref[...].astype(BF16)
            dwu_ref[0] = au_ref[...].astype(BF16)

        @pl.when(c == 0)
        def _():
            dh_ref[rows, :] = dh_c

        @pl.when(c > 0)
        def _():
            dh_ref[rows, :] += dh_c

        @pl.when(c == nc - 1)
        def _():
            dx, dn = _rmsnorm_bwd(xv, gain, dh_ref[rows, :])
            dx_ref[...] = dx + dy

            @pl.when(i == 0)
            def _():
                dn_ref[...] = dn

            @pl.when(i > 0)
            def _():
                dn_ref[...] += dn

    tile = pl.BlockSpec((tm, d), lambda c, i: (i, 0))
    row = pl.BlockSpec((1, d), lambda c, i: (0, 0))
    wrow = pl.BlockSpec((1, fc, d), lambda c, i: (c, 0, 0))
    last = pl.BlockSpec((tm, d), lambda c, i: (jnp.where(c == nc - 1, i, 0), 0))
    return pl.pallas_call(
        body, name=name, grid=(nc, nt),
        out_shape=(jax.ShapeDtypeStruct((t, d), F32), jax.ShapeDtypeStruct((1, d), F32),
                   jax.ShapeDtypeStruct(wg.shape, BF16), jax.ShapeDtypeStruct(wu.shape, BF16),
                   jax.ShapeDtypeStruct(wd.shape, BF16)),
        in_specs=[tile, row, wrow, wrow, wrow, tile, ANY],
        out_specs=(last, row, wrow, wrow, wrow),
        scratch_shapes=[pltpu.VMEM((t, d), F32)] + [pltpu.VMEM((fc, d), F32)] * 3,
        compiler_params=_params("arbitrary", "arbitrary"),
    )(x, norm, wg, wu, wd, dy, dep)


def _store_heads(ref, v):
    for h in range(ref.shape[0]):
        ref[h] = v[:, h * HEAD_DIM:(h + 1) * HEAD_DIM]


def _load_heads(ref):
    return jnp.concatenate([ref[h] for h in range(ref.shape[0])], axis=-1)


N_HEAD_GROUPS = 3


def _proj_fwd(x, norm, w):
    t, d = x.shape
    ng, _, c = w.shape
    nh = c // HEAD_DIM
    tm = TOKEN_TILE

    def body(x_ref, n_ref, w_ref, q_ref, k_ref, v_ref, cur_ref):
        xv = x_ref[...]
        h = (xv * _rms(xv) * n_ref[...]).astype(BF16)
        for m, ref in enumerate((q_ref, k_ref, v_ref)):
            _store_heads(ref, jnp.dot(h, w_ref[m], preferred_element_type=F32))
        for m in range(N_HEAD_GROUPS, ng):
            j = m - N_HEAD_GROUPS
            cur_ref[:, j * c:(j + 1) * c] = jnp.dot(h, w_ref[m], preferred_element_type=F32)

    heads = pl.BlockSpec((nh, tm, HEAD_DIM), lambda i: (0, i, 0))
    hshape = jax.ShapeDtypeStruct((nh, t, HEAD_DIM), F32)
    wide = (ng - N_HEAD_GROUPS) * c
    return pl.pallas_call(
        body, name="proj_fwd", grid=(t // tm,),
        out_shape=(hshape, hshape, hshape, jax.ShapeDtypeStruct((t, wide), F32)),
        in_specs=[pl.BlockSpec((tm, d), lambda i: (i, 0)), pl.BlockSpec((1, d), lambda i: (0, 0)), VMEM_FULL],
        out_specs=(heads, heads, heads, pl.BlockSpec((tm, wide), lambda i: (i, 0))),
        compiler_params=_params("arbitrary"),
    )(x, norm, w)


def _proj_bwd(x, norm, w, dq, dk, dv, dcur, dres):
    t, d = x.shape
    ng, _, c = w.shape
    nh = c // HEAD_DIM
    tm = TOKEN_TILE

    def body(x_ref, n_ref, w_ref, dq_ref, dk_ref, dv_ref, dcur_ref, dres_ref, dx_ref, dn_ref, dw_ref):
        i = pl.program_id(0)

        @pl.when(i == 0)
        def _():
            dw_ref[...] = jnp.zeros_like(dw_ref)
            dn_ref[...] = jnp.zeros_like(dn_ref)

        xv = x_ref[...]
        gain = n_ref[...]
        h = (xv * _rms(xv) * gain).astype(BF16)
        dh = jnp.zeros((tm, d), F32)
        for m in range(ng):
            j = m - N_HEAD_GROUPS
            dp = _load_heads((dq_ref, dk_ref, dv_ref)[m]) if j < 0 else dcur_ref[:, j * c:(j + 1) * c]
            dp = dp.astype(BF16)
            dw_ref[m] += _dot_tn(h, dp)
            dh = dh + _dot_nt(dp, w_ref[m])
        dx, dn = _rmsnorm_bwd(xv, gain, dh)
        dx_ref[...] = dx + dres_ref[...]
        dn_ref[...] += dn

    tile = pl.BlockSpec((tm, d), lambda i: (i, 0))
    row = pl.BlockSpec((1, d), lambda i: (0, 0))
    heads = pl.BlockSpec((nh, tm, HEAD_DIM), lambda i: (0, i, 0))
    wide = (ng - N_HEAD_GROUPS) * c
    return pl.pallas_call(
        body, name="proj_bwd", grid=(t // tm,),
        out_shape=(jax.ShapeDtypeStruct((t, d), F32), jax.ShapeDtypeStruct((1, d), F32),
                   jax.ShapeDtypeStruct(w.shape, F32)),
        in_specs=[tile, row, VMEM_FULL, heads, heads, heads, pl.BlockSpec((tm, wide), lambda i: (i, 0)), tile],
        out_specs=(tile, row, VMEM_FULL),
        compiler_params=_params("arbitrary"),
    )(x, norm, w, dq, dk, dv, dcur, dres)


def _mixout_fwd(x, att, opg, gate, w):
    t, d = x.shape
    nh = att.shape[0]
    half = gate.shape[1]
    tm = TOKEN_TILE

    def body(x_ref, att_ref, opg_ref, g_ref, w_ref, o_ref):
        mix = jnp.concatenate([_load_heads(att_ref), _load_heads(opg_ref) * g_ref[...]], axis=-1).astype(BF16)
        o_ref[...] = x_ref[...] + jnp.dot(mix, w_ref[...], preferred_element_type=F32)

    tile = pl.BlockSpec((tm, d), lambda i: (i, 0))
    htile = pl.BlockSpec((tm, half), lambda i: (i, 0))
    heads = pl.BlockSpec((nh, tm, HEAD_DIM), lambda i: (0, i, 0))
    return pl.pallas_call(
        body, name="mixout_fwd", grid=(t // tm,), out_shape=jax.ShapeDtypeStruct((t, d), F32),
        in_specs=[tile, heads, heads, htile, VMEM_FULL], out_specs=tile, compiler_params=_params("arbitrary"),
    )(x, att, opg, gate, w)


def _mixout_bwd(att, opg, gate, w, dy, dep):
    nh, t, _ = att.shape
    half = gate.shape[1]
    d = dy.shape[1]
    tm = TOKEN_TILE

    def body(att_ref, opg_ref, g_ref, w_ref, dy_ref, dep_ref, datt_ref, dopg_ref, dg_ref, dw_ref):
        i = pl.program_id(0)
        opg_v, g_v = _load_heads(opg_ref), g_ref[...]
        mix = jnp.concatenate([_load_heads(att_ref), opg_v * g_v], axis=-1).astype(BF16)
        dyb = dy_ref[...].astype(BF16)
        dmix = _dot_nt(dyb, w_ref[...])
        dw = _dot_tn(mix, dyb)
        _store_heads(datt_ref, dmix[:, :half])
        drw = dmix[:, half:]
        _store_heads(dopg_ref, drw * g_v)
        dg_ref[...] = drw * opg_v

        @pl.when(i == 0)
        def _():
            dw_ref[...] = dw

        @pl.when(i > 0)
        def _():
            dw_ref[...] += dw

    tile = pl.BlockSpec((tm, d), lambda i: (i, 0))
    htile = pl.BlockSpec((tm, half), lambda i: (i, 0))
    heads = pl.BlockSpec((nh, tm, HEAD_DIM), lambda i: (0, i, 0))
    hshape = jax.ShapeDtypeStruct((nh, t, HEAD_DIM), F32)
    return pl.pallas_call(
        body, name="mixout_bwd", grid=(t // tm,),
        out_shape=(hshape, hshape, jax.ShapeDtypeStruct((t, half), F32), jax.ShapeDtypeStruct(w.shape, F32)),
        in_specs=[heads, heads, htile, VMEM_FULL, tile, ANY],
        out_specs=(heads, heads, htile, pl.BlockSpec(w.shape, lambda i: (0, 0))),
        compiler_params=_params("arbitrary"),
    )(att, opg, gate, w, dy, dep)


def _loss_head(y, target):
    t, d = y.shape
    tm = TOKEN_TILE

    def body(y_ref, t_ref, dy_ref, loss_ref):
        i = pl.program_id(0)
        err = y_ref[...] - t_ref[...]
        dy_ref[...] = err * (1.0 / d)
        part = 0.5 * jnp.sum(jnp.mean(err * err, axis=-1, keepdims=True), axis=0, keepdims=True)

        @pl.when(i == 0)
        def _():
            loss_ref[...] = jnp.zeros_like(loss_ref)

        loss_ref[...] += jnp.broadcast_to(part, loss_ref.shape)

    tile = pl.BlockSpec((tm, d), lambda i: (i, 0))
    return pl.pallas_call(
        body, name="loss_head", grid=(t // tm,),
        out_shape=(jax.ShapeDtypeStruct((t, d), F32), jax.ShapeDtypeStruct((1, 128), F32)),
        in_specs=[tile, tile], out_specs=(tile, pl.BlockSpec((1, 128), lambda i: (0, 0))),
        compiler_params=_params("arbitrary"),
    )(y, target)


def _att_pattern(q, k, v, qn, kn, nb):
    g, blk, _ = q.shape
    qh = q * _rms(q) * qn
    kh = k * _rms(k) * kn
    scale = HEAD_DIM ** -0.5
    qi = lax.broadcasted_iota(jnp.int32, (blk, blk), 0)
    kj = lax.broadcasted_iota(jnp.int32, (blk, blk), 1)
    sc = jnp.where(kj <= qi, _bmm_nt(qh, kh) * scale, NEG_INF)
    top = jnp.max(sc, axis=-1, keepdims=True)
    if nb > 1:
        khp = jnp.concatenate([kh[:1], kh[:-1]], axis=0)
        vp = jnp.concatenate([v[:1], v[:-1]], axis=0)
        has_prev = lax.broadcasted_iota(jnp.int32, (g, 1, 1), 0) % nb != 0
        sp = jnp.where((kj >= qi) & has_prev, _bmm_nt(qh, khp) * scale, NEG_INF)
        top = jnp.maximum(top, jnp.max(sp, axis=-1, keepdims=True))
    m = lax.stop_gradient(top)
    pc = jnp.exp(sc - m)
    den = jnp.sum(pc, axis=-1, keepdims=True)
    acc = _bmm_nn(pc, v)
    if nb > 1:
        pp = jnp.exp(sp - m)
        den = den + jnp.sum(pp, axis=-1, keepdims=True)
        acc = acc + _bmm_nn(pp, vp)
    o = acc / den
    return o, jnp.broadcast_to(m + jnp.log(den), o.shape)


def _pattern_rows(t, dil):
    nb = t // (ATT_BLOCK * dil)
    starts = [n * ATT_BLOCK * dil + r for r in range(dil) for n in range(nb)]
    return [pl.ds(s, ATT_BLOCK, stride=dil) if dil > 1 else pl.ds(s, ATT_BLOCK) for s in starts], nb


def _take(ref, rows):
    return jnp.stack([ref[0, r, :] for r in rows])


def _put(ref, rows, val):
    for g, r in enumerate(rows):
        ref[0, r, :] = val[g]


def _put_add(ref, rows, val):
    for g, r in enumerate(rows):
        ref[0, r, :] += val[g]


def _merge_fn(o1, o2, o3, l1, l2, l3):
    m = lax.stop_gradient(jnp.maximum(jnp.maximum(l1, l2), l3))
    e1, e2, e3 = jnp.exp(l1 - m), jnp.exp(l2 - m), jnp.exp(l3 - m)
    return (e1 * o1 + e2 * o2 + e3 * o3) / (e1 + e2 + e3)


def _att_head_specs(t):
    head = pl.BlockSpec((1, t, HEAD_DIM), lambda h: (h, 0, 0))
    gain = pl.BlockSpec((1, 1, HEAD_DIM), lambda h: (0, 0, 0))
    return head, gain


def _att_fwd(q, k, v, qn, kn):
    nh, t, dh = q.shape
    head, gain = _att_head_specs(t)

    def body(q_ref, k_ref, v_ref, qn_ref, kn_ref, att_ref, *saved):
        o_refs, l_refs = saved[:3], saved[3:]
        for p, dil in enumerate(DILATIONS):
            rows, nb = _pattern_rows(t, dil)
            o, lse = _att_pattern(_take(q_ref, rows), _take(k_ref, rows), _take(v_ref, rows), qn_ref[...], kn_ref[...], nb)
            _put(o_refs[p], rows, o)
            _put(l_refs[p], rows, lse)

        def merge(j, carry):
            rows = pl.ds(pl.multiple_of(j * ATT_BLOCK, ATT_BLOCK), ATT_BLOCK)
            att_ref[0, rows, :] = _merge_fn(*[r[0, rows, :] for r in saved])
            return carry

        lax.fori_loop(0, t // ATT_BLOCK, merge, 0)

    return pl.pallas_call(
        body, name="att_fwd", grid=(nh,), out_shape=(jax.ShapeDtypeStruct(q.shape, F32),) * 7,
        in_specs=[head, head, head, gain, gain], out_specs=(head,) * 7, compiler_params=_params("arbitrary"),
    )(q, k, v, qn, kn)


def _att_bwd(q, k, v, qn, kn, saved, datt):
    nh, t, dh = q.shape
    head, gain = _att_head_specs(t)

    def body(q_ref, k_ref, v_ref, qn_ref, kn_ref, o1, o2, o3, l1, l2, l3, datt_ref,
             dq_ref, dk_ref, dv_ref, dqn_ref, dkn_ref):
        for ref in (dq_ref, dk_ref, dv_ref):
            ref[...] = jnp.zeros_like(ref)

        @pl.when(pl.program_id(0) == 0)
        def _():
            dqn_ref[...] = jnp.zeros_like(dqn_ref)
            dkn_ref[...] = jnp.zeros_like(dkn_ref)

        for p, dil in enumerate(DILATIONS):
            rows, nb = _pattern_rows(t, dil)
            _, merge_vjp = jax.vjp(_merge_fn, *[_take(r, rows) for r in (o1, o2, o3, l1, l2, l3)])
            cts = merge_vjp(_take(datt_ref, rows))
            _, pattern_vjp = jax.vjp(functools.partial(_att_pattern, nb=nb), _take(q_ref, rows), _take(k_ref, rows),
                                     _take(v_ref, rows), qn_ref[...], kn_ref[...])
            dq, dk, dv, dgq, dgk = pattern_vjp((cts[p], cts[3 + p]))
            _put_add(dq_ref, rows, dq)
            _put_add(dk_ref, rows, dk)
            _put_add(dv_ref, rows, dv)
            dqn_ref[...] += dgq
            dkn_ref[...] += dgk

    hshape = jax.ShapeDtypeStruct(q.shape, F32)
    gshape = jax.ShapeDtypeStruct((1, 1, dh), F32)
    return pl.pallas_call(
        body, name="att_bwd", grid=(nh,), out_shape=(hshape, hshape, hshape, gshape, gshape),
        in_specs=[head, head, head, gain, gain] + [head] * 7, out_specs=(head, head, head, gain, gain),
        compiler_params=_params("arbitrary"),
    )(q, k, v, qn, kn, *saved, datt)


RWKV_VEC = ("mu_r", "mu_k", "mu_v", "mu_w", "mu_a", "mu_g", "w0", "a0", "k_k", "k_a")
RWKV_MAT = ("w1", "w2", "a1", "a2", "g1", "g2")


def _rwkv_pre_fn(cur, prev, vec, w1, w2, a1, a2, g1, g2):
    c = cur.shape[1] // 4
    mu_r, mu_k, mu_v, mu_w, mu_a, mu_g, w0, a0, k_k, k_a = (vec[j:j + 1] for j in range(10))

    def lerp(j, mu):
        xc, xp = cur[:, j * c:(j + 1) * c], prev[:, j * c:(j + 1) * c]
        return xc + (xp - xc) * mu

    r, k, v = lerp(0, mu_r), lerp(1, mu_k), lerp(2, mu_v)
    cw, ca, cg = lerp(3, mu_w), lerp(3, mu_a), lerp(3, mu_g)
    z = w0 + _mm(jnp.tanh(_mm(cw, w1)), w2)
    w_log = jnp.minimum(z, 0.0) - jnp.log(1.0 + jnp.exp(-jnp.abs(z))) - 0.5
    lw = -jnp.exp(w_log)
    a = _sigmoid(a0 + _mm(_mm(ca, a1), a2))
    gate = _mm(_sigmoid(_mm(cg, g1)), g2)
    kkraw = k * k_k
    kmod = k * (1.0 + (a - 1.0) * k_a)
    return r, lw, kmod, v, kkraw, a, gate


HALO_ROWS = 8


def _rwkv_pre_specs(c, mats, tile_of):
    tm = TOKEN_TILE
    nh = c // HEAD_DIM
    wide = pl.BlockSpec((tm, 4 * c), lambda j: (tile_of(j), 0))
    halo = pl.BlockSpec((HALO_ROWS, 4 * c), lambda j: (jnp.maximum(tile_of(j) * (tm // HALO_ROWS) - 1, 0), 0))
    one = pl.BlockSpec((tm, c), lambda j: (tile_of(j), 0))
    heads = pl.BlockSpec((nh, tm, HEAD_DIM), lambda j: (0, tile_of(j), 0))
    vec = pl.BlockSpec((10, c), lambda j: (0, 0))
    mspecs = [pl.BlockSpec(m.shape, lambda j: (0, 0)) for m in mats]
    return wide, halo, one, heads, vec, mspecs


def _previous_rows(cur, halo, tile):
    first = jnp.where(tile > 0, halo[HALO_ROWS - 1:HALO_ROWS], 0.0)
    rows = lax.broadcasted_iota(jnp.int32, cur.shape, 0)
    return jnp.where(rows == 0, first, pltpu.roll(cur, 1, axis=0))


def _rwkv_pre_fwd(cur, vec, mats):
    t, c4 = cur.shape
    c = c4 // 4
    wide, halo, one, heads, vspec, mspecs = _rwkv_pre_specs(c, mats, lambda j: j)

    def body(cur_ref, halo_ref, vec_ref, *rest):
        mrefs, outs = rest[:6], rest[6:]
        cur_v = cur_ref[...]
        prev = _previous_rows(cur_v, halo_ref[...], pl.program_id(0))
        vals = _rwkv_pre_fn(cur_v, prev, vec_ref[...], *(m[...] for m in mrefs))
        for ref, val in zip(outs[:6], vals[:6]):
            _store_heads(ref, val)
        outs[6][...] = vals[6]

    hshape = jax.ShapeDtypeStruct((c // HEAD_DIM, t, HEAD_DIM), F32)
    return pl.pallas_call(
        body, name="rwkv_pre_fwd", grid=(t // TOKEN_TILE,), out_shape=(hshape,) * 6 + (jax.ShapeDtypeStruct((t, c), F32),),
        in_specs=[wide, halo, vspec] + mspecs, out_specs=(heads,) * 6 + (one,), compiler_params=_params("arbitrary"),
    )(cur, cur, vec, *mats)


def _rwkv_pre_bwd(cur, vec, mats, cts, dgate):
    t, c4 = cur.shape
    c = c4 // 4
    tm = TOKEN_TILE
    nt = t // tm
    wide, halo, one, heads, vspec, mspecs = _rwkv_pre_specs(c, mats, lambda j: nt - 1 - j)

    def body(cur_ref, halo_ref, vec_ref, *rest):
        mrefs, ctrefs, dgate_ref, outs, carry_ref = rest[:6], rest[6:12], rest[12], rest[13:-1], rest[-1]
        j = pl.program_id(0)

        @pl.when(j == 0)
        def _():
            carry_ref[...] = jnp.zeros_like(carry_ref)
            for ref in outs[1:]:
                ref[...] = jnp.zeros_like(ref)

        cur_v = cur_ref[...]
        prev = _previous_rows(cur_v, halo_ref[...], nt - 1 - j)
        _, vjp = jax.vjp(_rwkv_pre_fn, cur_v, prev, vec_ref[...], *(m[...] for m in mrefs))
        grads = vjp(tuple(_load_heads(r) for r in ctrefs) + (dgate_ref[...],))
        dprev = grads[1]
        rows = lax.broadcasted_iota(jnp.int32, dprev.shape, 0)
        outs[0][...] = grads[0] + jnp.where(rows == tm - 1, carry_ref[0:1], pltpu.roll(dprev, tm - 1, axis=0))
        carry_ref[0:1] = dprev[0:1]
        for ref, val in zip(outs[1:], grads[2:]):
            ref[...] += val

    return pl.pallas_call(
        body, name="rwkv_pre_bwd", grid=(nt,),
        out_shape=(jax.ShapeDtypeStruct(cur.shape, F32), jax.ShapeDtypeStruct(vec.shape, F32))
        + tuple(jax.ShapeDtypeStruct(m.shape, F32) for m in mats),
        in_specs=[wide, halo, vspec] + mspecs + [heads] * 6 + [one], out_specs=(wide, vspec) + tuple(mspecs),
        scratch_shapes=[pltpu.VMEM((HALO_ROWS, c4), F32)], compiler_params=_params("arbitrary"),
    )(cur, cur, vec, *mats, *cts, dgate)


def _scan_chunk_fn(h0, r, lw, k, v, kkraw, a, rk, lnw, lnb):
    n = r.shape[1]
    nrm = jnp.sqrt(jnp.sum(kkraw * kkraw, axis=-1, keepdims=True))
    kk = kkraw / jnp.maximum(nrm, 1e-12)
    av, bv = -kk, kk * a
    ti = lax.broadcasted_iota(jnp.int32, (n, n), 0)
    si = lax.broadcasted_iota(jnp.int32, (n, n), 1)
    incl, strict = ti >= si, ti > si
    ones = jnp.broadcast_to(incl.astype(F32)[None], (r.shape[0], n, n))
    cum = _hdot(ones, lw, 2, 1)
    at, rt = av * jnp.exp(cum - lw), r * jnp.exp(cum)
    inv = jnp.exp(-cum)
    bt, kt = bv * inv, k * inv
    lab = jnp.where(strict, _hdot(at, bt, 2, 2), 0.0)
    lak = jnp.where(strict, _hdot(at, kt, 2, 2), 0.0)
    rb = jnp.where(incl, _hdot(rt, bt, 2, 2), 0.0)
    rkm = jnp.where(incl, _hdot(rt, kt, 2, 2), 0.0)
    u = _bmm_nn(at, h0) + _bmm_nn(lak, v)
    p = lab
    m = 1
    while m < n:
        u = u + _bmm_nn(p, u)
        m *= 2
        if m < n:
            p = _bmm_nn(p, p)
    y = _bmm_nn(rt, h0) + _bmm_nn(rb, u) + _bmm_nn(rkm, v)
    last = jnp.exp(jnp.sum(lw, axis=1, keepdims=True))
    h1 = jnp.swapaxes(last, 1, 2) * (h0 + _bmm_tn(bt, u) + _bmm_tn(kt, v))
    mean = jnp.mean(y, axis=-1, keepdims=True)
    yc = y - mean
    var = jnp.mean(yc * yc, axis=-1, keepdims=True)
    yn = yc * lax.rsqrt(var + GN_EPS) * lnw + lnb
    bonus = jnp.sum(r * k * rk, axis=-1, keepdims=True) * v
    return yn + bonus, h1


SCAN_GROUP = 2


def _scan_group_fn(h0, r, lw, k, v, kkraw, a, rk, lnw, lnb):
    outs = []
    for j in range(SCAN_GROUP):
        rows = slice(j * SCAN_CHUNK, (j + 1) * SCAN_CHUNK)
        o, h0 = _scan_chunk_fn(h0, r[:, rows], lw[:, rows], k[:, rows], v[:, rows], kkraw[:, rows], a[:, rows], rk, lnw, lnb)
        outs.append(o)
    return jnp.concatenate(outs, axis=1), h0


def _scan_specs(h, t, dh, rev):
    n = SCAN_CHUNK * SCAN_GROUP
    nc = t // n
    pos = (lambda c: (0, nc - 1 - c, 0)) if rev else (lambda c: (0, c, 0))
    st = (lambda c: (nc - 1 - c, 0, 0, 0)) if rev else (lambda c: (c, 0, 0, 0))
    seq = pl.BlockSpec((h, n, dh), pos)
    par = pl.BlockSpec((h, 1, dh), lambda c: (0, 0, 0))
    state = pl.BlockSpec((1, h, dh, dh), st)
    return seq, par, state


def _scan_fwd(seqs, pars):
    h, t, dh = seqs[0].shape
    nc = t // (SCAN_CHUNK * SCAN_GROUP)
    seq, par, state = _scan_specs(h, t, dh, False)

    def body(r, lw, k, v, kkraw, a, rk, lnw, lnb, o_ref, st_ref, h_ref):
        @pl.when(pl.program_id(0) == 0)
        def _():
            h_ref[...] = jnp.zeros_like(h_ref)

        h0 = h_ref[...]
        st_ref[0] = h0
        o, h1 = _scan_group_fn(h0, r[...], lw[...], k[...], v[...], kkraw[...], a[...], rk[...], lnw[...], lnb[...])
        o_ref[...] = o
        h_ref[...] = h1

    return pl.pallas_call(
        body, name="rwkv_scan_fwd", grid=(nc,),
        out_shape=(jax.ShapeDtypeStruct((h, t, dh), F32), jax.ShapeDtypeStruct((nc, h, dh, dh), F32)),
        in_specs=[seq] * 6 + [par] * 3, out_specs=(seq, state),
        scratch_shapes=[pltpu.VMEM((h, dh, dh), F32)], compiler_params=_params("arbitrary"),
    )(*seqs, *pars)


def _scan_bwd(seqs, pars, states, do):
    h, t, dh = seqs[0].shape
    nc = t // (SCAN_CHUNK * SCAN_GROUP)
    seq, par, state = _scan_specs(h, t, dh, True)

    def body(r, lw, k, v, kkraw, a, rk, lnw, lnb, st_ref, do_ref, *rest):
        douts, dpars, dh_ref = rest[:6], rest[6:9], rest[9]
        first = pl.program_id(0) == 0

        @pl.when(first)
        def _():
            dh_ref[...] = jnp.zeros_like(dh_ref)

        _, vjp = jax.vjp(_scan_group_fn, st_ref[0], r[...], lw[...], k[...], v[...], kkraw[...], a[...],
                         rk[...], lnw[...], lnb[...])
        grads = vjp((do_ref[...], dh_ref[...]))
        dh_ref[...] = grads[0]
        for ref, val in zip(douts, grads[1:7]):
            ref[...] = val

        @pl.when(first)
        def _():
            for ref, val in zip(dpars, grads[7:]):
                ref[...] = val

        @pl.when(jnp.logical_not(first))
        def _():
            for ref, val in zip(dpars, grads[7:]):
                ref[...] += val

    sshape = jax.ShapeDtypeStruct((h, t, dh), F32)
    pshape = jax.ShapeDtypeStruct((h, 1, dh), F32)
    return pl.pallas_call(
        body, name="rwkv_scan_bwd", grid=(nc,), out_shape=(sshape,) * 6 + (pshape,) * 3,
        in_specs=[seq] * 6 + [par] * 3 + [state, seq], out_specs=(seq,) * 6 + (par,) * 3,
        scratch_shapes=[pltpu.VMEM((h, dh, dh), F32)], compiler_params=_params("arbitrary"),
    )(*seqs, *pars, states, do)


def _local_step(x, target, w, ex):
    w = dict(w)
    c = w["mu_r"].shape[-1]
    qn, kn = w["q_norm"].reshape(1, 1, HEAD_DIM), w["k_norm"].reshape(1, 1, HEAD_DIM)
    vec = jnp.concatenate([w[n].reshape(1, c) for n in RWKV_VEC], axis=0)
    pars = [w[n].reshape(-1, 1, HEAD_DIM) for n in ("r_k", "ln_x_w", "ln_x_b")]
    no_dep = jnp.zeros(DEP_SHAPE, F32)

    x1 = _ffn_fwd(x, w["ffn1_norm"], w["ffn1_w_gate"], w["ffn1_w_up"], w["ffn1_w_down"], ex.first_dep, "ffn1_fwd")
    w.update(ex.mix_weights((x1,)))
    mats = [w[n] for n in RWKV_MAT]
    q, k, v, cur = _proj_fwd(x1, w["mix_norm"], w["w_in"])
    att, *saved = _att_fwd(q, k, v, qn, kn)
    pre = _rwkv_pre_fwd(cur, vec, mats)
    seqs, gate = pre[:6], pre[6]
    opg, states = _scan_fwd(seqs, pars)
    w.update(ex.out_weights((att, opg)))
    x2 = _mixout_fwd(x1, att, opg, gate, w["w_out"])
    x3 = _ffn_fwd(x2, w["ffn2_norm"], w["ffn2_w_gate"], w["ffn2_w_up"], w["ffn2_w_down"], no_dep, "ffn2_fwd")
    dy, loss = _loss_head(x3, target)

    g = {}
    dx2, g["ffn2_norm"], g["ffn2_w_gate"], g["ffn2_w_up"], g["ffn2_w_down"] = _ffn_bwd(
        x2, w["ffn2_norm"], w["ffn2_w_gate"], w["ffn2_w_up"], w["ffn2_w_down"], dy, no_dep, "ffn2_bwd")
    dep = ex.send_ffn2({n: g[n] for n in ("ffn2_w_gate", "ffn2_w_up", "ffn2_w_down")})
    datt, dopg, dgate, g["w_out"] = _mixout_bwd(att, opg, gate, w["w_out"], dx2, dep)
    dscan = _scan_bwd(seqs, pars, states, dopg)
    for n, d in zip(("r_k", "ln_x_w", "ln_x_b"), dscan[6:]):
        g[n] = d
    dcur, dvec, *dmats = _rwkv_pre_bwd(cur, vec, mats, dscan[:6], dgate)
    for n, d in zip(RWKV_MAT, dmats):
        g[n] = d
    for j, n in enumerate(RWKV_VEC):
        g[n] = dvec[j:j + 1]
    dq, dk, dv, g["q_norm"], g["k_norm"] = _att_bwd(q, k, v, qn, kn, saved, datt)
    dx1, g["mix_norm"], g["w_in"] = _proj_bwd(x1, w["mix_norm"], w["w_in"], dq, dk, dv, dcur, dx2)
    dep = ex.send_mix({n: g[n] for n in ("w_in", "w_out") + RWKV_MAT}, (dx1,))
    dx, g["ffn1_norm"], g["ffn1_w_gate"], g["ffn1_w_up"], g["ffn1_w_down"] = _ffn_bwd(
        x, w["ffn1_norm"], w["ffn1_w_gate"], w["ffn1_w_up"], w["ffn1_w_down"], dx1, dep, "ffn1_bwd")
    return loss, dx, g


N_SHARDS = 4


def _place():
    return lax.axis_index("x"), lax.axis_index("y"), lax.axis_index("c")


def _chip_peers(x, y):
    return [(1 - x, y), (x, 1 - y), (1 - x, 1 - y)]


HBM = pl.BlockSpec(memory_space=pltpu.HBM)
SEM = pl.BlockSpec(memory_space=pltpu.SEMAPHORE)
DEP_SHAPE = (8, 128)


class _Views:
    to_sibling = False


class _GatherViews(_Views):
    @staticmethod
    def send(i, srcs, lands, k, at):
        return srcs[i], lands[i].at[at[3]]

    @staticmethod
    def landing(i, srcs, lands, k, at):
        return srcs[i], lands[i].at[2 * at[4] + at[5]]


class _ScatterViews(_Views):
    @staticmethod
    def send(i, srcs, lands, k, at):
        return srcs[i].at[2 * at[4] + at[5]], lands[i].at[k]

    @staticmethod
    def landing(i, srcs, lands, k, at):
        return srcs[i].at[at[3]], lands[i].at[k]


def _half_rows(ref, slot, half):
    rows = ref.shape[1] // 2
    return ref.at[slot, pl.ds(pl.multiple_of(half * rows, BF16_SUBLANES), rows)]


class _HalfGatherViews(_Views):
    @staticmethod
    def send(i, srcs, lands, k, at):
        rows = srcs[i].shape[0] // 2
        return srcs[i].at[pl.ds(pl.multiple_of(at[2] * rows, BF16_SUBLANES), rows)], _half_rows(lands[i], at[3], at[2])

    @staticmethod
    def landing(i, srcs, lands, k, at):
        rows = srcs[i].shape[0] // 2
        return srcs[i].at[pl.ds(pl.multiple_of(at[2] * rows, BF16_SUBLANES), rows)], _half_rows(lands[i], 2 * at[4] + at[5], at[2])


class _ForwardViews(_Views):
    to_sibling = True

    @staticmethod
    def send(i, srcs, lands, k, at):
        mine = _half_rows(lands[i], 2 * at[4] + at[5], at[2])
        return mine, mine

    @staticmethod
    def landing(i, srcs, lands, k, at):
        theirs = _half_rows(lands[i], 2 * at[4] + at[5], 1 - at[2])
        return theirs, theirs


def _push_start(srcs, lands, views, after, name):
    ns, nl = len(srcs), len(lands)

    def body(*refs):
        src_refs, land_refs = refs[:ns], refs[ns:ns + nl]
        send_sems, recv_sems = refs[ns + nl + 1:ns + nl + 3]
        token = refs[2 * (ns + nl) + 3]
        x, y, c = _place()
        for i in range(nl):
            for k, (px, py) in enumerate(_chip_peers(x, y)):
                src, dst = views.send(i, src_refs, land_refs, k, (x, y, c, 2 * x + y, px, py))
                pltpu.make_async_remote_copy(
                    src_ref=src, dst_ref=dst, send_sem=send_sems.at[3 * i + k], recv_sem=recv_sems.at[3 * i + k],
                    device_id=(x, y, 1 - c) if views.to_sibling else (px, py, c), device_id_type=MESH).start()
        token[...] = jnp.zeros_like(token)

    sems = pltpu.SemaphoreType.DMA((3 * nl,))
    both = [pltpu.with_memory_space_constraint(a, pltpu.HBM) for a in (*srcs, *lands)]
    outs = pl.pallas_call(
        body, name=name,
        out_shape=(sems, sems, *[pltpu.HBM(a.shape, a.dtype) for a in both], jax.ShapeDtypeStruct(DEP_SHAPE, F32)),
        in_specs=[HBM] * (ns + nl) + [ANY], out_specs=(SEM, SEM, *[HBM] * (ns + nl), VMEM_FULL),
        input_output_aliases={i: 2 + i for i in range(ns + nl)},
        compiler_params=pltpu.CompilerParams(has_side_effects=pltpu.SideEffectType.DATAFLOW_SIDE_EFFECTING),
    )(*both, after)
    return outs[0], outs[1], outs[2:2 + ns], outs[2 + ns:2 + ns + nl], outs[2 + ns + nl]


def _push_wait(started, views, after, name, with_sources=False):
    send_sems, recv_sems, srcs, lands, _ = started
    ns, nl = len(srcs), len(lands)

    def body(*refs):
        src_refs, land_refs = refs[:ns], refs[ns:ns + nl]
        send_sems, recv_sems = refs[ns + nl:ns + nl + 2]
        x, y, c = _place()
        for i in range(nl):
            for k, (px, py) in enumerate(_chip_peers(x, y)):
                src, dst = views.landing(i, src_refs, land_refs, k, (x, y, c, 2 * x + y, px, py))
                landing = pltpu.make_async_remote_copy(
                    src_ref=src, dst_ref=dst, send_sem=send_sems.at[3 * i + k], recv_sem=recv_sems.at[3 * i + k],
                    device_id=(x, y, 1 - c) if views.to_sibling else (px, py, c), device_id_type=MESH)
                landing.wait_send()
                landing.wait_recv()

    outs = pl.pallas_call(
        body, name=name,
        out_shape=tuple(pltpu.HBM(a.shape, a.dtype) for a in (*srcs, *lands)),
        in_specs=[HBM] * (ns + nl) + [SEM, SEM] + [ANY] * len(after), out_specs=(HBM,) * (ns + nl),
        input_output_aliases={i: i for i in range(ns + nl)},
        compiler_params=pltpu.CompilerParams(has_side_effects=pltpu.SideEffectType.DATAFLOW_SIDE_EFFECTING),
    )(*srcs, *lands, send_sems, recv_sems, *after)
    return outs if with_sources else outs[ns:]


def _empty_lands(shards, slots, own_slot):
    lands = [lax.empty((slots,) + s.shape, s.dtype) for s in shards]
    if own_slot:
        me = 2 * lax.axis_index("x") + lax.axis_index("y")
        lands = [lax.dynamic_update_index_in_dim(z, s, me, 0) for z, s in zip(lands, shards)]
    return lands


def _sibling_swap(arrays, name):
    n = len(arrays)

    def body(*refs):
        ins, outs = refs[:n], refs[n:2 * n]
        send_sems, recv_sems = refs[2 * n:]
        x, y, c = _place()
        copies = []
        for i in range(n):
            cp = pltpu.make_async_remote_copy(
                src_ref=ins[i], dst_ref=outs[i], send_sem=send_sems.at[i], recv_sem=recv_sems.at[i],
                device_id=(x, y, 1 - c), device_id_type=MESH)
            cp.start()
            copies.append(cp)
        for cp in copies:
            cp.wait()

    return pl.pallas_call(
        body, name=name,
        out_shape=tuple(jax.ShapeDtypeStruct(a.shape, a.dtype) for a in arrays),
        in_specs=[ANY] * n, out_specs=(ANY,) * n,
        scratch_shapes=[pltpu.SemaphoreType.DMA((n,)), pltpu.SemaphoreType.DMA((n,))],
    )(*arrays)


N_DEV = 8


def _allreduce_small(pack):
    def body(in_ref, out_ref, buf, send_sems, recv_sems):
        x, y, c = _place()
        me = 4 * x + 2 * y + c
        buf[me] = in_ref[...]

        def copy(j, slot):
            px, py, pc = x ^ (j >> 2), y ^ ((j >> 1) & 1), c ^ (j & 1)
            return pltpu.make_async_remote_copy(
                src_ref=in_ref, dst_ref=buf.at[slot(px, py, pc)], send_sem=send_sems.at[j], recv_sem=recv_sems.at[j],
                device_id=(px, py, pc), device_id_type=MESH)

        for j in range(1, N_DEV):
            copy(j, lambda px, py, pc: me).start()
        for j in range(1, N_DEV):
            landing = copy(j, lambda px, py, pc: 4 * px + 2 * py + pc)
            landing.wait_send()
            landing.wait_recv()
        acc = buf[0]
        for s in range(1, N_DEV):
            acc = acc + buf[s]
        out_ref[...] = acc

    return pl.pallas_call(
        body, name="allreduce_small", out_shape=jax.ShapeDtypeStruct(pack.shape, F32),
        in_specs=[VMEM_FULL], out_specs=VMEM_FULL,
        scratch_shapes=[pltpu.VMEM((N_DEV,) + pack.shape, F32), pltpu.SemaphoreType.DMA((N_DEV,)),
                        pltpu.SemaphoreType.DMA((N_DEV,))],
    )(pack)


ROW_TILE_MAX = 256
BF16_SUBLANES = 16


def _row_tile(rows):
    for tr in range(min(rows, ROW_TILE_MAX), 0, -1):
        if rows % tr == 0 and tr % BF16_SUBLANES == 0:
            return tr
    return rows


def _reduce_own(me, part, recv, dep, name):
    _, r, cols = part.shape
    tr = _row_tile(r)

    def body(me_ref, p_ref, rv_ref, dep_ref, o_ref):
        acc = p_ref[0].astype(F32)
        for k in range(3):
            acc = acc + rv_ref[k].astype(F32)
        o_ref[...] = acc

    return pl.pallas_call(
        body, name=name, out_shape=jax.ShapeDtypeStruct((r, cols), F32),
        grid_spec=pltpu.PrefetchScalarGridSpec(
            num_scalar_prefetch=1, grid=(r // tr,),
            in_specs=[pl.BlockSpec((1, tr, cols), lambda i, me_ref: (me_ref[0], i, 0)),
                      pl.BlockSpec((3, tr, cols), lambda i, me_ref: (0, i, 0)), ANY],
            out_specs=pl.BlockSpec((tr, cols), lambda i, me_ref: (i, 0))),
        compiler_params=_params("arbitrary"),
    )(me, part, recv, dep)


def _adamw(w, ga, gb, m, v, name):
    r, cols = w.shape
    tr = _row_tile(r)
    c1 = 1.0 - ADAM_B1 ** ADAM_STEP
    c2 = 1.0 - ADAM_B2 ** ADAM_STEP

    def body(w_ref, ga_ref, gb_ref, m_ref, v_ref, g_out, d_out, m_out, v_out):
        g = ga_ref[...] + gb_ref[...]
        mn = ADAM_B1 * m_ref[...] + (1.0 - ADAM_B1) * g
        vn = ADAM_B2 * v_ref[...] + (1.0 - ADAM_B2) * (g * g)
        g_out[...] = g
        m_out[...] = mn
        v_out[...] = vn
        d_out[...] = -ADAM_LR * ((mn / c1) / (jnp.sqrt(vn / c2) + ADAM_EPS) + ADAM_WD * w_ref[...])

    tile = pl.BlockSpec((tr, cols), lambda i: (i, 0))
    shape = jax.ShapeDtypeStruct((r, cols), F32)
    return pl.pallas_call(
        body, name=name, grid=(r // tr,), out_shape=(shape,) * 4, in_specs=[tile] * 5, out_specs=(tile,) * 4,
        compiler_params=_params("arbitrary"),
    )(w, ga, gb, m, v)


PACK_COLS = 512


def _to_rows(a):
    flat = a.reshape(-1)
    pad = (-flat.shape[0]) % PACK_COLS
    return jnp.pad(flat, (0, pad)).reshape(-1, PACK_COLS)


def _pack(arrays, extra_rows=0):
    rows = [_to_rows(a) for a in arrays]
    n = sum(r.shape[0] for r in rows) + extra_rows
    pad = (-n) % 8
    return jnp.concatenate(rows + [jnp.zeros((extra_rows + pad, PACK_COLS), F32)], axis=0)


def _unpack(pack, like):
    out, at = [], 0
    for a in like:
        n = -(-a.size // PACK_COLS)
        out.append(pack[at:at + n].reshape(-1)[:a.size].reshape(a.shape))
        at += n
    return out


COL_SHARDED = ("ffn1_w_gate", "ffn1_w_up", "w_in", "ffn2_w_gate", "ffn2_w_up", "w2", "a2", "g2")
ROW_SHARDED = ("ffn1_w_down", "ffn2_w_down", "w_out", "w1", "a1", "g1")
CHUNKED = ("ffn1_w_gate", "ffn1_w_up", "ffn1_w_down", "ffn2_w_gate", "ffn2_w_up", "ffn2_w_down")
WEIGHTS = ("ffn1_norm", "ffn1_w_gate", "ffn1_w_up", "ffn1_w_down", "mix_norm", "w_in", "q_norm", "k_norm",
           "mu_r", "mu_k", "mu_v", "mu_w", "mu_a", "mu_g", "w0", "w1", "w2", "a0", "a1", "a2", "g1", "g2",
           "k_k", "k_a", "r_k", "ln_x_w", "ln_x_b", "w_out", "ffn2_norm", "ffn2_w_gate", "ffn2_w_up", "ffn2_w_down")


W_IN_GROUPS = 7
TRANSPOSED = ("ffn1_w_gate", "ffn1_w_up", "ffn2_w_gate", "ffn2_w_up")


def _shard_2d(name, a):
    return a[0].T if name in TRANSPOSED else a[0]


def _full_from_blocks(name, blocks):
    if name in CHUNKED:
        return blocks
    if name in ROW_SHARDED:
        return blocks.reshape(-1, blocks.shape[-1])
    full = blocks.transpose(1, 0, 2).reshape(blocks.shape[1], -1)
    if name == "w_in":
        return full.reshape(full.shape[0], W_IN_GROUPS, -1).transpose(1, 0, 2)
    return full


def _blocks_from_full(name, full):
    if name in CHUNKED:
        return full
    if name in ROW_SHARDED:
        return full.reshape(N_SHARDS, -1, full.shape[-1])
    if name == "w_in":
        full = full.transpose(1, 0, 2).reshape(full.shape[1], -1)
    return full.reshape(full.shape[0], N_SHARDS, -1).transpose(1, 0, 2)


FFN1_GROUP = ("ffn1_w_gate", "ffn1_w_up", "ffn1_w_down")
MIX_GROUP = ("w_in",) + RWKV_MAT
OUT_GROUP = ("w_out", "ffn2_w_gate", "ffn2_w_up", "ffn2_w_down")
FFN2_GROUP = OUT_GROUP[1:]
LATE_GROUP = ("w_in", "w_out") + RWKV_MAT


class _Exchange:
    def __init__(self, given):
        self.given = given
        first = self._gather_start(FFN1_GROUP, _HalfGatherViews, jnp.zeros(DEP_SHAPE, F32), "gather_ffn1_start")
        self.mix = self._gather_start(MIX_GROUP, _GatherViews, first[4], "gather_mix_start")
        self.out = self._gather_start(OUT_GROUP, _GatherViews, self.mix[4], "gather_out_start")
        self.first_dep = self.out[4]
        halves = _push_wait(first, _HalfGatherViews, (self.first_dep,), "gather_ffn1_wait")
        passed = _push_start([], halves, _ForwardViews, halves[0], "gather_ffn1_pass_start")
        self.first_weights = self._full(FFN1_GROUP, _push_wait(passed, _ForwardViews, (passed[4],), "gather_ffn1_pass_wait"))
        self.parts, self.recv = {}, {}

    def _shards(self, names):
        return [_shard_2d(n, self.given[n]).astype(BF16) for n in names]

    @staticmethod
    def _full(names, blocks):
        out = {}
        for n, b in zip(names, blocks):
            full = _full_from_blocks(n, b)
            out[n] = full.astype(F32) if n in RWKV_MAT else full
        return out

    def _gather_start(self, names, views, after, name):
        shards = self._shards(names)
        return _push_start(shards, _empty_lands(shards, N_SHARDS, True), views, after, name)

    def mix_weights(self, after):
        return self._full(MIX_GROUP, _push_wait(self.mix, _GatherViews, after, "gather_mix_wait"))

    def out_weights(self, after):
        return self._full(OUT_GROUP, _push_wait(self.out, _GatherViews, after, "gather_out_wait"))

    def _scatter_start(self, grads, name):
        names = tuple(grads)
        parts = [_blocks_from_full(n, grads[n]) for n in names]
        self.parts.update(zip(names, parts))
        lands = [lax.empty((3,) + p.shape[1:], BF16) for p in parts]
        return _push_start([p.astype(BF16) for p in parts], lands, _ScatterViews, jnp.zeros(DEP_SHAPE, F32), name)

    def _scatter_done(self, started, names, after, name):
        outs = _push_wait(started, _ScatterViews, after, name, with_sources=True)
        for n, sent, got in zip(names, outs[:len(names)], outs[len(names):]):
            self.recv[n] = got
            if self.parts[n].dtype == BF16:
                self.parts[n] = sent

    def send_ffn2(self, grads):
        self.ffn2 = self._scatter_start(grads, "scatter_ffn2_start")
        return self.ffn2[4]

    def send_mix(self, grads, after):
        self._scatter_done(self.ffn2, FFN2_GROUP, after, "scatter_ffn2_wait")
        self.late = self._scatter_start(grads, "scatter_late_start")
        return self.late[4]

    def send_ffn1(self, grads):
        self.ffn1 = self._scatter_start(grads, "scatter_ffn1_start")
        return self.ffn1[4]

    def late_received(self, after):
        self._scatter_done(self.late, LATE_GROUP, after, "scatter_late_wait")

    def ffn1_received(self, after):
        self._scatter_done(self.ffn1, FFN1_GROUP, after, "scatter_ffn1_wait")


def kernel(
        x, ffn1_norm, ffn1_w_gate, ffn1_w_up, ffn1_w_down, mix_norm, w_in, q_norm, k_norm, mu_r, mu_k, mu_v, mu_w,
        mu_a, mu_g, w0, w1, w2, a0, a1, a2, g1, g2, k_k, k_a, r_k, ln_x_w, ln_x_b, w_out, ffn2_norm, ffn2_w_gate,
        ffn2_w_up, ffn2_w_down, loss_target, m_ffn1_norm, m_ffn1_w_gate, m_ffn1_w_up, m_ffn1_w_down, m_mix_norm,
        m_w_in, m_q_norm, m_k_norm, m_mu_r, m_mu_k, m_mu_v, m_mu_w, m_mu_a, m_mu_g, m_w0, m_w1, m_w2, m_a0, m_a1,
        m_a2, m_g1, m_g2, m_k_k, m_k_a, m_r_k, m_ln_x_w, m_ln_x_b, m_w_out, m_ffn2_norm, m_ffn2_w_gate, m_ffn2_w_up,
        m_ffn2_w_down, v_ffn1_norm, v_ffn1_w_gate, v_ffn1_w_up, v_ffn1_w_down, v_mix_norm, v_w_in, v_q_norm, v_k_norm,
        v_mu_r, v_mu_k, v_mu_v, v_mu_w, v_mu_a, v_mu_g, v_w0, v_w1, v_w2, v_a0, v_a1, v_a2, v_g1, v_g2, v_k_k, v_k_a,
        v_r_k, v_ln_x_w, v_ln_x_b, v_w_out, v_ffn2_norm, v_ffn2_w_gate, v_ffn2_w_up, v_ffn2_w_down):
    given = dict(locals())
    sharded = COL_SHARDED + ROW_SHARDED
    sharded = tuple(n for n in WEIGHTS if n in sharded)
    small = tuple(n for n in WEIGHTS if n not in sharded)

    ex = _Exchange(given)
    w = {n: given[n] for n in small}
    w.update(ex.first_weights)
    loss, dx, g = _local_step(x[0], loss_target[0], w, ex)
    dep = ex.send_ffn1({n: g[n] for n in FFN1_GROUP})

    me = (2 * lax.axis_index("x") + lax.axis_index("y")).astype(jnp.int32).reshape(1)
    out = {}

    def settle(names, dep, tag):
        mine = []
        for n in names:
            p, rv = ex.parts[n], ex.recv[n]
            p2 = p.reshape(N_SHARDS, -1, p.shape[-1])
            mine.append(_reduce_own(me, p2, rv.reshape(3, -1, rv.shape[-1]), dep, f"reduce_{n}"))
        theirs = _sibling_swap(mine, f"sibling_swap_{tag}")
        for n, a, b in zip(names, mine, theirs):
            shape = given[n].shape
            res = _adamw(_shard_2d(n, given[n]), a, b, _shard_2d(n, given["m_" + n]), _shard_2d(n, given["v_" + n]), f"adamw_{n}")
            out[n] = [(r.T if n in TRANSPOSED else r).reshape(shape) for r in res]
        return tuple(out[n][1] for n in names)

    ex.late_received((dep,))
    last = settle(tuple(n for n in sharded if n not in FFN1_GROUP), dep, "rest")

    gpack = _pack([g[n] for n in small], extra_rows=1)
    n_rows = sum(-(-given[n].size // PACK_COLS) for n in small)
    gpack = gpack.at[n_rows, :loss.shape[1]].set(loss[0])
    gsum = _allreduce_small(gpack)
    res = _adamw(_pack([given[n] for n in small], 1), gsum, jnp.zeros_like(gsum), _pack([given["m_" + n] for n in small], 1),
                 _pack([given["v_" + n] for n in small], 1), "adamw_small")
    like = [given[n] for n in small]
    for j, r in enumerate(res):
        for n, a in zip(small, _unpack(r, like)):
            out.setdefault(n, [None] * 4)[j] = a
    total_loss = gsum[n_rows, 0]

    ex.ffn1_received((*last, res[1]))
    settle(FFN1_GROUP, jnp.zeros(DEP_SHAPE, F32), "ffn1")
    return (total_loss, dx[None], *[out[n][0] for n in WEIGHTS], *[out[n][1] for n in WEIGHTS],
            *[out[n][2] for n in WEIGHTS], *[out[n][3] for n in WEIGHTS])
```

```python
import functools

import jax
import jax.numpy as jnp
from jax import lax
from jax.experimental import pallas as pl
from jax.experimental.pallas import tpu as pltpu

F32 = jnp.float32
BF16 = jnp.bfloat16
MESH = pl.DeviceIdType.MESH

RMS_EPS = 1e-6
GN_EPS = 64e-5
NEG_INF = -1e30
FFN_RESIDUAL = 0.5
HEAD_DIM = 64
ATT_BLOCK = 128
DILATIONS = (1, 4, 16)
SCAN_CHUNK = 64
TOKEN_TILE = 256

ADAM_LR = 0.001
ADAM_B1 = 0.9
ADAM_B2 = 0.999
ADAM_EPS = 1e-08
ADAM_WD = 0.01
ADAM_STEP = 10

VMEM_FULL = pl.BlockSpec(memory_space=pltpu.VMEM)
ANY = pl.BlockSpec(memory_space=pl.ANY)


VMEM_LIMIT = 56 * 1024 * 1024


def _params(*sem):
    return pltpu.CompilerParams(dimension_semantics=sem, vmem_limit_bytes=VMEM_LIMIT)


def _dot(a, b, dims):
    return lax.dot_general(a.astype(BF16), b.astype(BF16), (dims, ((), ())), preferred_element_type=F32)


def _dot_nn(a, b):
    return _dot(a, b, ((1,), (0,)))


def _dot_nt(a, b):
    return _dot(a, b, ((1,), (1,)))


def _dot_tn(a, b):
    return _dot(a, b, ((0,), (0,)))


@jax.custom_vjp
def _mm(a, b):
    return _dot_nn(a, b)


def _mm_fwd(a, b):
    return _dot_nn(a, b), (a, b)


def _mm_bwd(res, g):
    a, b = res
    return _dot_nt(g, b).astype(a.dtype), _dot_tn(a, g).astype(b.dtype)


_mm.defvjp(_mm_fwd, _mm_bwd)


def _bdot(a, b, ca, cb):
    return lax.dot_general(a.astype(BF16), b.astype(BF16), (((ca,), (cb,)), ((0,), (0,))), preferred_element_type=F32)


@jax.custom_vjp
def _bmm_nt(a, b):
    return _bdot(a, b, 2, 2)


def _bmm_nt_fwd(a, b):
    return _bdot(a, b, 2, 2), (a, b)


def _bmm_nt_bwd(res, g):
    a, b = res
    return _bdot(g, b, 2, 1), _bdot(g, a, 1, 1)


_bmm_nt.defvjp(_bmm_nt_fwd, _bmm_nt_bwd)


@jax.custom_vjp
def _bmm_nn(a, b):
    return _bdot(a, b, 2, 1)


def _bmm_nn_fwd(a, b):
    return _bdot(a, b, 2, 1), (a, b)


def _bmm_nn_bwd(res, g):
    a, b = res
    return _bdot(g, b, 2, 2), _bdot(a, g, 1, 1)


_bmm_nn.defvjp(_bmm_nn_fwd, _bmm_nn_bwd)


@jax.custom_vjp
def _bmm_tn(a, b):
    return _bdot(a, b, 1, 1)


def _bmm_tn_fwd(a, b):
    return _bdot(a, b, 1, 1), (a, b)


def _bmm_tn_bwd(res, g):
    a, b = res
    return _bdot(b, g, 2, 2), _bdot(a, g, 2, 1)


_bmm_tn.defvjp(_bmm_tn_fwd, _bmm_tn_bwd)


def _hdot(a, b, ca, cb):
    return lax.dot_general(a, b, (((ca,), (cb,)), ((0,), (0,))), precision=lax.Precision.HIGH, preferred_element_type=F32)


def _sigmoid(x):
    return 1.0 / (1.0 + jnp.exp(-x))


def _rms(x):
    return lax.rsqrt(jnp.mean(x * x, axis=-1, keepdims=True) + RMS_EPS)


def _ffn_fwd(x, norm, wg, wu, wd, dep, name):
    t, d = x.shape
    nc, fc, _ = wg.shape
    tm = TOKEN_TILE

    def body(x_ref, n_ref, wg_ref, wu_ref, wd_ref, dep_ref, o_ref, g_ref, u_ref):
        xv = x_ref[...]
        h = (xv * _rms(xv) * n_ref[...]).astype(BF16)
        acc = jnp.zeros((tm, d), F32)
        for c in range(nc):
            g = _dot_nt(h, wg_ref[c])
            u = _dot_nt(h, wu_ref[c])
            g_ref[c] = g.astype(BF16)
            u_ref[c] = u.astype(BF16)
            a = (g * _sigmoid(g) * u).astype(BF16)
            acc = acc + jnp.dot(a, wd_ref[c], preferred_element_type=F32)
        o_ref[...] = xv + FFN_RESIDUAL * acc

    tile = pl.BlockSpec((tm, d), lambda i: (i, 0))
    hidden = pl.BlockSpec((nc, tm, fc), lambda i: (0, i, 0))
    hshape = jax.ShapeDtypeStruct((nc, t, fc), BF16)
    return pl.pallas_call(
        body, name=name, grid=(t // tm,), out_shape=(jax.ShapeDtypeStruct((t, d), F32), hshape, hshape),
        in_specs=[tile, pl.BlockSpec((1, d), lambda i: (0, 0)), VMEM_FULL, VMEM_FULL, VMEM_FULL, ANY],
        out_specs=(tile, hidden, hidden), compiler_params=_params("arbitrary"),
    )(x, norm, wg, wu, wd, dep)


def _rmsnorm_bwd(xv, gain, dh):
    rs = _rms(xv)
    xn = xv * rs
    dxn = dh * gain
    dx = rs * (dxn - xn * jnp.mean(dxn * xn, axis=-1, keepdims=True))
    return dx, jnp.sum(dh * xn, axis=0, keepdims=True)


def _ffn_bwd(x, norm, wg, wu, wd, gate, up, dy, dep, name):
    t, d = x.shape
    nc, fc, _ = wg.shape
    tm = TOKEN_TILE
    nt = t // tm

    def body(x_ref, n_ref, wg_ref, wu_ref, wd_ref, g_ref, u_ref, dy_ref, dep_ref, dx_ref, dn_ref, dwg_ref, dwu_ref,
             dwd_ref, dh_ref, ag_ref, au_ref, ad_ref):
        c, i = pl.program_id(0), pl.program_id(1)
        rows = pl.ds(pl.multiple_of(i * tm, tm), tm)
        xv = x_ref[...]
        gain = n_ref[...]
        h = (xv * _rms(xv) * gain).astype(BF16)
        dy = dy_ref[...]
        dyb = (FFN_RESIDUAL * dy).astype(BF16)
        g = g_ref[0].astype(F32)
        u = u_ref[0].astype(F32)
        sg = _sigmoid(g)
        s = g * sg
        a = (s * u).astype(BF16)
        da = _dot_nt(dyb, wd_ref[0])
        dub = (da * s).astype(BF16)
        dgb = (da * u * (sg * (1.0 + g * (1.0 - sg)))).astype(BF16)
        dwd_c = _dot_tn(a, dyb)
        dwg_c = _dot_tn(dgb, h)
        dwu_c = _dot_tn(dub, h)
        dh_c = _dot_nn(dgb, wg_ref[0]) + _dot_nn(dub, wu_ref[0])

        @pl.when(i == 0)
        def _():
            ad_ref[...] = dwd_c
            ag_ref[...] = dwg_c
            au_ref[...] = dwu_c

        @pl.when(i > 0)
        def _():
            ad_ref[...] += dwd_c
            ag_ref[...] += dwg_c
            au_ref[...] += dwu_c

        @pl.when(i == nt - 1)
        def _():
            dwd_ref[0] = ad_ref[...].astype(BF16)
            dwg_ref[0] = ag_ref[...].astype(BF16)
            dwu_ref[0] = au_ref[...].astype(BF16)

        @pl.when(c == 0)
        def _():
            dh_ref[rows, :] = dh_c

        @pl.when(c > 0)
        def _():
            dh_ref[rows, :] += dh_c

        @pl.when(c == nc - 1)
        def _():
            dx, dn = _rmsnorm_bwd(xv, gain, dh_ref[rows, :])
            dx_ref[...] = dx + dy

            @pl.when(i == 0)
            def _():
                dn_ref[...] = dn

            @pl.when(i > 0)
            def _():
                dn_ref[...] += dn

    tile = pl.BlockSpec((tm, d), lambda c, i: (i, 0))
    row = pl.BlockSpec((1, d), lambda c, i: (0, 0))
    wrow = pl.BlockSpec((1, fc, d), lambda c, i: (c, 0, 0))
    hidden = pl.BlockSpec((1, tm, fc), lambda c, i: (c, i, 0))
    last = pl.BlockSpec((tm, d), lambda c, i: (jnp.where(c == nc - 1, i, 0), 0))
    return pl.pallas_call(
        body, name=name, grid=(nc, nt),
        out_shape=(jax.ShapeDtypeStruct((t, d), F32), jax.ShapeDtypeStruct((1, d), F32),
                   jax.ShapeDtypeStruct(wg.shape, BF16), jax.ShapeDtypeStruct(wu.shape, BF16),
                   jax.ShapeDtypeStruct(wd.shape, BF16)),
        in_specs=[tile, row, wrow, wrow, wrow, hidden, hidden, tile, ANY],
        out_specs=(last, row, wrow, wrow, wrow),
        scratch_shapes=[pltpu.VMEM((t, d), F32)] + [pltpu.VMEM((fc, d), F32)] * 3,
        compiler_params=_params("arbitrary", "arbitrary"),
    )(x, norm, wg, wu, wd, gate, up, dy, dep)


def _store_heads(ref, v):
    for h in range(ref.shape[0]):
        ref[h] = v[:, h * HEAD_DIM:(h + 1) * HEAD_DIM]


def _load_heads(ref):
    return jnp.concatenate([ref[h] for h in range(ref.shape[0])], axis=-1)


N_HEAD_GROUPS = 3


def _proj_fwd(x, norm, w):
    t, d = x.shape
    ng, _, c = w.shape
    nh = c // HEAD_DIM
    tm = TOKEN_TILE

    def body(x_ref, n_ref, w_ref, q_ref, k_ref, v_ref, cur_ref):
        xv = x_ref[...]
        h = (xv * _rms(xv) * n_ref[...]).astype(BF16)
        for m, ref in enumerate((q_ref, k_ref, v_ref)):
            _store_heads(ref, jnp.dot(h, w_ref[m], preferred_element_type=F32))
        for m in range(N_HEAD_GROUPS, ng):
            j = m - N_HEAD_GROUPS
            cur_ref[:, j * c:(j + 1) * c] = jnp.dot(h, w_ref[m], preferred_element_type=F32)

    heads = pl.BlockSpec((nh, tm, HEAD_DIM), lambda i: (0, i, 0))
    hshape = jax.ShapeDtypeStruct((nh, t, HEAD_DIM), F32)
    wide = (ng - N_HEAD_GROUPS) * c
    return pl.pallas_call(
        body, name="proj_fwd", grid=(t // tm,),
        out_shape=(hshape, hshape, hshape, jax.ShapeDtypeStruct((t, wide), F32)),
        in_specs=[pl.BlockSpec((tm, d), lambda i: (i, 0)), pl.BlockSpec((1, d), lambda i: (0, 0)), VMEM_FULL],
        out_specs=(heads, heads, heads, pl.BlockSpec((tm, wide), lambda i: (i, 0))),
        compiler_params=_params("arbitrary"),
    )(x, norm, w)


def _proj_bwd(x, norm, w, dq, dk, dv, dcur, dres):
    t, d = x.shape
    ng, _, c = w.shape
    nh = c // HEAD_DIM
    tm = TOKEN_TILE

    def body(x_ref, n_ref, w_ref, dq_ref, dk_ref, dv_ref, dcur_ref, dres_ref, dx_ref, dn_ref, dw_ref):
        i = pl.program_id(0)

        @pl.when(i == 0)
        def _():
            dw_ref[...] = jnp.zeros_like(dw_ref)
            dn_ref[...] = jnp.zeros_like(dn_ref)

        xv = x_ref[...]
        gain = n_ref[...]
        h = (xv * _rms(xv) * gain).astype(BF16)
        dh = jnp.zeros((tm, d), F32)
        for m in range(ng):
            j = m - N_HEAD_GROUPS
            dp = _load_heads((dq_ref, dk_ref, dv_ref)[m]) if j < 0 else dcur_ref[:, j * c:(j + 1) * c]
            dp = dp.astype(BF16)
            dw_ref[m] += _dot_tn(h, dp)
            dh = dh + _dot_nt(dp, w_ref[m])
        dx, dn = _rmsnorm_bwd(xv, gain, dh)
        dx_ref[...] = dx + dres_ref[...]
        dn_ref[...] += dn

    tile = pl.BlockSpec((tm, d), lambda i: (i, 0))
    row = pl.BlockSpec((1, d), lambda i: (0, 0))
    heads = pl.BlockSpec((nh, tm, HEAD_DIM), lambda i: (0, i, 0))
    wide = (ng - N_HEAD_GROUPS) * c
    return pl.pallas_call(
        body, name="proj_bwd", grid=(t // tm,),
        out_shape=(jax.ShapeDtypeStruct((t, d), F32), jax.ShapeDtypeStruct((1, d), F32),
                   jax.ShapeDtypeStruct(w.shape, F32)),
        in_specs=[tile, row, VMEM_FULL, heads, heads, heads, pl.BlockSpec((tm, wide), lambda i: (i, 0)), tile],
        out_specs=(tile, row, VMEM_FULL),
        compiler_params=_params("arbitrary"),
    )(x, norm, w, dq, dk, dv, dcur, dres)


def _mixout_fwd(x, att, opg, gate, w):
    t, d = x.shape
    nh = att.shape[0]
    half = gate.shape[1]
    tm = TOKEN_TILE

    def body(x_ref, att_ref, opg_ref, g_ref, w_ref, o_ref):
        mix = jnp.concatenate([_load_heads(att_ref), _load_heads(opg_ref) * g_ref[...]], axis=-1).astype(BF16)
        o_ref[...] = x_ref[...] + jnp.dot(mix, w_ref[...], preferred_element_type=F32)

    tile = pl.BlockSpec((tm, d), lambda i: (i, 0))
    htile = pl.BlockSpec((tm, half), lambda i: (i, 0))
    heads = pl.BlockSpec((nh, tm, HEAD_DIM), lambda i: (0, i, 0))
    return pl.pallas_call(
        body, name="mixout_fwd", grid=(t // tm,), out_shape=jax.ShapeDtypeStruct((t, d), F32),
        in_specs=[tile, heads, heads, htile, VMEM_FULL], out_specs=tile, compiler_params=_params("arbitrary"),
    )(x, att, opg, gate, w)


def _mixout_bwd(att, opg, gate, w, dy, dep):
    nh, t, _ = att.shape
    half = gate.shape[1]
    d = dy.shape[1]
    tm = TOKEN_TILE

    def body(att_ref, opg_ref, g_ref, w_ref, dy_ref, dep_ref, datt_ref, dopg_ref, dg_ref, dw_ref):
        i = pl.program_id(0)
        opg_v, g_v = _load_heads(opg_ref), g_ref[...]
        mix = jnp.concatenate([_load_heads(att_ref), opg_v * g_v], axis=-1).astype(BF16)
        dyb = dy_ref[...].astype(BF16)
        dmix = _dot_nt(dyb, w_ref[...])
        dw = _dot_tn(mix, dyb)
        _store_heads(datt_ref, dmix[:, :half])
        drw = dmix[:, half:]
        _store_heads(dopg_ref, drw * g_v)
        dg_ref[...] = drw * opg_v

        @pl.when(i == 0)
        def _():
            dw_ref[...] = dw

        @pl.when(i > 0)
        def _():
            dw_ref[...] += dw

    tile = pl.BlockSpec((tm, d), lambda i: (i, 0))
    htile = pl.BlockSpec((tm, half), lambda i: (i, 0))
    heads = pl.BlockSpec((nh, tm, HEAD_DIM), lambda i: (0, i, 0))
    hshape = jax.ShapeDtypeStruct((nh, t, HEAD_DIM), F32)
    return pl.pallas_call(
        body, name="mixout_bwd", grid=(t // tm,),
        out_shape=(hshape, hshape, jax.ShapeDtypeStruct((t, half), F32), jax.ShapeDtypeStruct(w.shape, F32)),
        in_specs=[heads, heads, htile, VMEM_FULL, tile, ANY],
        out_specs=(heads, heads, htile, pl.BlockSpec(w.shape, lambda i: (0, 0))),
        compiler_params=_params("arbitrary"),
    )(att, opg, gate, w, dy, dep)


def _loss_head(y, target):
    t, d = y.shape
    tm = TOKEN_TILE

    def body(y_ref, t_ref, dy_ref, loss_ref):
        i = pl.program_id(0)
        err = y_ref[...] - t_ref[...]
        dy_ref[...] = err * (1.0 / d)
        part = 0.5 * jnp.sum(jnp.mean(err * err, axis=-1, keepdims=True), axis=0, keepdims=True)

        @pl.when(i == 0)
        def _():
            loss_ref[...] = jnp.zeros_like(loss_ref)

        loss_ref[...] += jnp.broadcast_to(part, loss_ref.shape)

    tile = pl.BlockSpec((tm, d), lambda i: (i, 0))
    return pl.pallas_call(
        body, name="loss_head", grid=(t // tm,),
        out_shape=(jax.ShapeDtypeStruct((t, d), F32), jax.ShapeDtypeStruct((1, 128), F32)),
        in_specs=[tile, tile], out_specs=(tile, pl.BlockSpec((1, 128), lambda i: (0, 0))),
        compiler_params=_params("arbitrary"),
    )(y, target)


def _att_pattern(q, k, v, qn, kn, nb):
    g, blk, _ = q.shape
    qh = q * _rms(q) * qn
    kh = k * _rms(k) * kn
    scale = HEAD_DIM ** -0.5
    qi = lax.broadcasted_iota(jnp.int32, (blk, blk), 0)
    kj = lax.broadcasted_iota(jnp.int32, (blk, blk), 1)
    sc = jnp.where(kj <= qi, _bmm_nt(qh, kh) * scale, NEG_INF)
    top = jnp.max(sc, axis=-1, keepdims=True)
    if nb > 1:
        khp = jnp.concatenate([kh[:1], kh[:-1]], axis=0)
        vp = jnp.concatenate([v[:1], v[:-1]], axis=0)
        has_prev = lax.broadcasted_iota(jnp.int32, (g, 1, 1), 0) % nb != 0
        sp = jnp.where((kj >= qi) & has_prev, _bmm_nt(qh, khp) * scale, NEG_INF)
        top = jnp.maximum(top, jnp.max(sp, axis=-1, keepdims=True))
    m = lax.stop_gradient(top)
    pc = jnp.exp(sc - m)
    den = jnp.sum(pc, axis=-1, keepdims=True)
    acc = _bmm_nn(pc, v)
    if nb > 1:
        pp = jnp.exp(sp - m)
        den = den + jnp.sum(pp, axis=-1, keepdims=True)
        acc = acc + _bmm_nn(pp, vp)
    o = acc / den
    return o, jnp.broadcast_to(m + jnp.log(den), o.shape)


def _pattern_rows(t, dil):
    nb = t // (ATT_BLOCK * dil)
    starts = [n * ATT_BLOCK * dil + r for r in range(dil) for n in range(nb)]
    return [pl.ds(s, ATT_BLOCK, stride=dil) if dil > 1 else pl.ds(s, ATT_BLOCK) for s in starts], nb


def _take(ref, rows):
    return jnp.stack([ref[0, r, :] for r in rows])


def _put(ref, rows, val):
    for g, r in enumerate(rows):
        ref[0, r, :] = val[g]


def _put_add(ref, rows, val):
    for g, r in enumerate(rows):
        ref[0, r, :] += val[g]


def _merge_fn(o1, o2, o3, l1, l2, l3):
    m = lax.stop_gradient(jnp.maximum(jnp.maximum(l1, l2), l3))
    e1, e2, e3 = jnp.exp(l1 - m), jnp.exp(l2 - m), jnp.exp(l3 - m)
    return (e1 * o1 + e2 * o2 + e3 * o3) / (e1 + e2 + e3)


def _att_head_specs(t):
    head = pl.BlockSpec((1, t, HEAD_DIM), lambda h: (h, 0, 0))
    gain = pl.BlockSpec((1, 1, HEAD_DIM), lambda h: (0, 0, 0))
    return head, gain


def _att_fwd(q, k, v, qn, kn):
    nh, t, dh = q.shape
    head, gain = _att_head_specs(t)

    def body(q_ref, k_ref, v_ref, qn_ref, kn_ref, att_ref, *saved):
        o_refs, l_refs = saved[:3], saved[3:]
        for p, dil in enumerate(DILATIONS):
            rows, nb = _pattern_rows(t, dil)
            o, lse = _att_pattern(_take(q_ref, rows), _take(k_ref, rows), _take(v_ref, rows), qn_ref[...], kn_ref[...], nb)
            _put(o_refs[p], rows, o)
            _put(l_refs[p], rows, lse)

        def merge(j, carry):
            rows = pl.ds(pl.multiple_of(j * ATT_BLOCK, ATT_BLOCK), ATT_BLOCK)
            att_ref[0, rows, :] = _merge_fn(*[r[0, rows, :] for r in saved])
            return carry

        lax.fori_loop(0, t // ATT_BLOCK, merge, 0)

    return pl.pallas_call(
        body, name="att_fwd", grid=(nh,), out_shape=(jax.ShapeDtypeStruct(q.shape, F32),) * 7,
        in_specs=[head, head, head, gain, gain], out_specs=(head,) * 7, compiler_params=_params("arbitrary"),
    )(q, k, v, qn, kn)


def _att_bwd(q, k, v, qn, kn, saved, datt):
    nh, t, dh = q.shape
    head, gain = _att_head_specs(t)

    def body(q_ref, k_ref, v_ref, qn_ref, kn_ref, o1, o2, o3, l1, l2, l3, datt_ref,
             dq_ref, dk_ref, dv_ref, dqn_ref, dkn_ref, *ct_refs):
        for ref in (dq_ref, dk_ref, dv_ref):
            ref[...] = jnp.zeros_like(ref)

        @pl.when(pl.program_id(0) == 0)
        def _():
            dqn_ref[...] = jnp.zeros_like(dqn_ref)
            dkn_ref[...] = jnp.zeros_like(dkn_ref)

        def merge_cotangents(j, carry):
            rows = pl.ds(pl.multiple_of(j * ATT_BLOCK, ATT_BLOCK), ATT_BLOCK)
            _, merge_vjp = jax.vjp(_merge_fn, *[r[0, rows, :] for r in (o1, o2, o3, l1, l2, l3)])
            for ref, val in zip(ct_refs, merge_vjp(datt_ref[0, rows, :])):
                ref[0, rows, :] = val
            return carry

        lax.fori_loop(0, t // ATT_BLOCK, merge_cotangents, 0)

        for p, dil in enumerate(DILATIONS):
            rows, nb = _pattern_rows(t, dil)
            cts = [_take(ct_refs[p], rows), _take(ct_refs[3 + p], rows)]
            _, pattern_vjp = jax.vjp(functools.partial(_att_pattern, nb=nb), _take(q_ref, rows), _take(k_ref, rows),
                                     _take(v_ref, rows), qn_ref[...], kn_ref[...])
            dq, dk, dv, dgq, dgk = pattern_vjp((cts[0], cts[1]))
            _put_add(dq_ref, rows, dq)
            _put_add(dk_ref, rows, dk)
            _put_add(dv_ref, rows, dv)
            dqn_ref[...] += dgq
            dkn_ref[...] += dgk

    hshape = jax.ShapeDtypeStruct(q.shape, F32)
    gshape = jax.ShapeDtypeStruct((1, 1, dh), F32)
    return pl.pallas_call(
        body, name="att_bwd", grid=(nh,), out_shape=(hshape, hshape, hshape, gshape, gshape),
        in_specs=[head, head, head, gain, gain] + [head] * 7, out_specs=(head, head, head, gain, gain),
        scratch_shapes=[pltpu.VMEM((1, t, dh), F32)] * 6, compiler_params=_params("arbitrary"),
    )(q, k, v, qn, kn, *saved, datt)


RWKV_VEC = ("mu_r", "mu_k", "mu_v", "mu_w", "mu_a", "mu_g", "w0", "a0", "k_k", "k_a")
RWKV_MAT = ("w1", "w2", "a1", "a2", "g1", "g2")


def _rwkv_pre_fn(cur, prev, vec, w1, w2, a1, a2, g1, g2):
    c = cur.shape[1] // 4
    mu_r, mu_k, mu_v, mu_w, mu_a, mu_g, w0, a0, k_k, k_a = (vec[j:j + 1] for j in range(10))

    def lerp(j, mu):
        xc, xp = cur[:, j * c:(j + 1) * c], prev[:, j * c:(j + 1) * c]
        return xc + (xp - xc) * mu

    r, k, v = lerp(0, mu_r), lerp(1, mu_k), lerp(2, mu_v)
    cw, ca, cg = lerp(3, mu_w), lerp(3, mu_a), lerp(3, mu_g)
    z = w0 + _mm(jnp.tanh(_mm(cw, w1)), w2)
    w_log = jnp.minimum(z, 0.0) - jnp.log(1.0 + jnp.exp(-jnp.abs(z))) - 0.5
    lw = -jnp.exp(w_log)
    a = _sigmoid(a0 + _mm(_mm(ca, a1), a2))
    gate = _mm(_sigmoid(_mm(cg, g1)), g2)
    kkraw = k * k_k
    kmod = k * (1.0 + (a - 1.0) * k_a)
    return r, lw, kmod, v, kkraw, a, gate


HALO_ROWS = 8


def _rwkv_pre_specs(c, mats, tile_of):
    tm = TOKEN_TILE
    nh = c // HEAD_DIM
    wide = pl.BlockSpec((tm, 4 * c), lambda j: (tile_of(j), 0))
    halo = pl.BlockSpec((HALO_ROWS, 4 * c), lambda j: (jnp.maximum(tile_of(j) * (tm // HALO_ROWS) - 1, 0), 0))
    one = pl.BlockSpec((tm, c), lambda j: (tile_of(j), 0))
    heads = pl.BlockSpec((nh, tm, HEAD_DIM), lambda j: (0, tile_of(j), 0))
    vec = pl.BlockSpec((10, c), lambda j: (0, 0))
    mspecs = [pl.BlockSpec(m.shape, lambda j: (0, 0)) for m in mats]
    return wide, halo, one, heads, vec, mspecs


def _previous_rows(cur, halo, tile):
    first = jnp.where(tile > 0, halo[HALO_ROWS - 1:HALO_ROWS], 0.0)
    rows = lax.broadcasted_iota(jnp.int32, cur.shape, 0)
    return jnp.where(rows == 0, first, pltpu.roll(cur, 1, axis=0))


def _rwkv_pre_fwd(cur, vec, mats):
    t, c4 = cur.shape
    c = c4 // 4
    wide, halo, one, heads, vspec, mspecs = _rwkv_pre_specs(c, mats, lambda j: j)

    def body(cur_ref, halo_ref, vec_ref, *rest):
        mrefs, outs = rest[:6], rest[6:]
        cur_v = cur_ref[...]
        prev = _previous_rows(cur_v, halo_ref[...], pl.program_id(0))
        vals = _rwkv_pre_fn(cur_v, prev, vec_ref[...], *(m[...] for m in mrefs))
        for ref, val in zip(outs[:6], vals[:6]):
            _store_heads(ref, val)
        outs[6][...] = vals[6]

    hshape = jax.ShapeDtypeStruct((c // HEAD_DIM, t, HEAD_DIM), F32)
    return pl.pallas_call(
        body, name="rwkv_pre_fwd", grid=(t // TOKEN_TILE,), out_shape=(hshape,) * 6 + (jax.ShapeDtypeStruct((t, c), F32),),
        in_specs=[wide, halo, vspec] + mspecs, out_specs=(heads,) * 6 + (one,), compiler_params=_params("arbitrary"),
    )(cur, cur, vec, *mats)


def _rwkv_pre_bwd(cur, vec, mats, cts, dgate):
    t, c4 = cur.shape
    c = c4 // 4
    tm = TOKEN_TILE
    nt = t // tm
    wide, halo, one, heads, vspec, mspecs = _rwkv_pre_specs(c, mats, lambda j: nt - 1 - j)

    def body(cur_ref, halo_ref, vec_ref, *rest):
        mrefs, ctrefs, dgate_ref, outs, carry_ref = rest[:6], rest[6:12], rest[12], rest[13:-1], rest[-1]
        j = pl.program_id(0)

        @pl.when(j == 0)
        def _():
            carry_ref[...] = jnp.zeros_like(carry_ref)
            for ref in outs[1:]:
                ref[...] = jnp.zeros_like(ref)

        cur_v = cur_ref[...]
        prev = _previous_rows(cur_v, halo_ref[...], nt - 1 - j)
        _, vjp = jax.vjp(_rwkv_pre_fn, cur_v, prev, vec_ref[...], *(m[...] for m in mrefs))
        grads = vjp(tuple(_load_heads(r) for r in ctrefs) + (dgate_ref[...],))
        dprev = grads[1]
        rows = lax.broadcasted_iota(jnp.int32, dprev.shape, 0)
        outs[0][...] = grads[0] + jnp.where(rows == tm - 1, carry_ref[0:1], pltpu.roll(dprev, tm - 1, axis=0))
        carry_ref[0:1] = dprev[0:1]
        for ref, val in zip(outs[1:], grads[2:]):
            ref[...] += val

    return pl.pallas_call(
        body, name="rwkv_pre_bwd", grid=(nt,),
        out_shape=(jax.ShapeDtypeStruct(cur.shape, F32), jax.ShapeDtypeStruct(vec.shape, F32))
        + tuple(jax.ShapeDtypeStruct(m.shape, F32) for m in mats),
        in_specs=[wide, halo, vspec] + mspecs + [heads] * 6 + [one], out_specs=(wide, vspec) + tuple(mspecs),
        scratch_shapes=[pltpu.VMEM((HALO_ROWS, c4), F32)], compiler_params=_params("arbitrary"),
    )(cur, cur, vec, *mats, *cts, dgate)


def _scan_chunk_fn(h0, r, lw, k, v, kkraw, a, rk, lnw, lnb):
    n = r.shape[1]
    nrm = jnp.sqrt(jnp.sum(kkraw * kkraw, axis=-1, keepdims=True))
    kk = kkraw / jnp.maximum(nrm, 1e-12)
    av, bv = -kk, kk * a
    ti = lax.broadcasted_iota(jnp.int32, (n, n), 0)
    si = lax.broadcasted_iota(jnp.int32, (n, n), 1)
    incl, strict = ti >= si, ti > si
    ones = jnp.broadcast_to(incl.astype(F32)[None], (r.shape[0], n, n))
    cum = _hdot(ones, lw, 2, 1)
    at, rt = av * jnp.exp(cum - lw), r * jnp.exp(cum)
    inv = jnp.exp(-cum)
    bt, kt = bv * inv, k * inv
    lab = jnp.where(strict, _hdot(at, bt, 2, 2), 0.0)
    lak = jnp.where(strict, _hdot(at, kt, 2, 2), 0.0)
    rb = jnp.where(incl, _hdot(rt, bt, 2, 2), 0.0)
    rkm = jnp.where(incl, _hdot(rt, kt, 2, 2), 0.0)
    u = _bmm_nn(at, h0) + _bmm_nn(lak, v)
    p = lab
    m = 1
    while m < n:
        u = u + _bmm_nn(p, u)
        m *= 2
        if m < n:
            p = _bmm_nn(p, p)
    y = _bmm_nn(rt, h0) + _bmm_nn(rb, u) + _bmm_nn(rkm, v)
    last = jnp.exp(jnp.sum(lw, axis=1, keepdims=True))
    h1 = jnp.swapaxes(last, 1, 2) * (h0 + _bmm_tn(bt, u) + _bmm_tn(kt, v))
    mean = jnp.mean(y, axis=-1, keepdims=True)
    yc = y - mean
    var = jnp.mean(yc * yc, axis=-1, keepdims=True)
    yn = yc * lax.rsqrt(var + GN_EPS) * lnw + lnb
    bonus = jnp.sum(r * k * rk, axis=-1, keepdims=True) * v
    return yn + bonus, h1


SCAN_GROUP = 2


def _scan_group_fn(h0, r, lw, k, v, kkraw, a, rk, lnw, lnb):
    outs = []
    for j in range(SCAN_GROUP):
        rows = slice(j * SCAN_CHUNK, (j + 1) * SCAN_CHUNK)
        o, h0 = _scan_chunk_fn(h0, r[:, rows], lw[:, rows], k[:, rows], v[:, rows], kkraw[:, rows], a[:, rows], rk, lnw, lnb)
        outs.append(o)
    return jnp.concatenate(outs, axis=1), h0


def _scan_specs(h, t, dh, rev):
    n = SCAN_CHUNK * SCAN_GROUP
    nc = t // n
    pos = (lambda c: (0, nc - 1 - c, 0)) if rev else (lambda c: (0, c, 0))
    st = (lambda c: (nc - 1 - c, 0, 0, 0)) if rev else (lambda c: (c, 0, 0, 0))
    seq = pl.BlockSpec((h, n, dh), pos)
    par = pl.BlockSpec((h, 1, dh), lambda c: (0, 0, 0))
    state = pl.BlockSpec((1, h, dh, dh), st)
    return seq, par, state


def _scan_fwd(seqs, pars):
    h, t, dh = seqs[0].shape
    nc = t // (SCAN_CHUNK * SCAN_GROUP)
    seq, par, state = _scan_specs(h, t, dh, False)

    def body(r, lw, k, v, kkraw, a, rk, lnw, lnb, o_ref, st_ref, h_ref):
        @pl.when(pl.program_id(0) == 0)
        def _():
            h_ref[...] = jnp.zeros_like(h_ref)

        h0 = h_ref[...]
        st_ref[0] = h0
        o, h1 = _scan_group_fn(h0, r[...], lw[...], k[...], v[...], kkraw[...], a[...], rk[...], lnw[...], lnb[...])
        o_ref[...] = o
        h_ref[...] = h1

    return pl.pallas_call(
        body, name="rwkv_scan_fwd", grid=(nc,),
        out_shape=(jax.ShapeDtypeStruct((h, t, dh), F32), jax.ShapeDtypeStruct((nc, h, dh, dh), F32)),
        in_specs=[seq] * 6 + [par] * 3, out_specs=(seq, state),
        scratch_shapes=[pltpu.VMEM((h, dh, dh), F32)], compiler_params=_params("arbitrary"),
    )(*seqs, *pars)


def _scan_bwd(seqs, pars, states, do):
    h, t, dh = seqs[0].shape
    nc = t // (SCAN_CHUNK * SCAN_GROUP)
    seq, par, state = _scan_specs(h, t, dh, True)

    def body(r, lw, k, v, kkraw, a, rk, lnw, lnb, st_ref, do_ref, *rest):
        douts, dpars, dh_ref = rest[:6], rest[6:9], rest[9]
        first = pl.program_id(0) == 0

        @pl.when(first)
        def _():
            dh_ref[...] = jnp.zeros_like(dh_ref)

        _, vjp = jax.vjp(_scan_group_fn, st_ref[0], r[...], lw[...], k[...], v[...], kkraw[...], a[...],
                         rk[...], lnw[...], lnb[...])
        grads = vjp((do_ref[...], dh_ref[...]))
        dh_ref[...] = grads[0]
        for ref, val in zip(douts, grads[1:7]):
            ref[...] = val

        @pl.when(first)
        def _():
            for ref, val in zip(dpars, grads[7:]):
                ref[...] = val

        @pl.when(jnp.logical_not(first))
        def _():
            for ref, val in zip(dpars, grads[7:]):
                ref[...] += val

    sshape = jax.ShapeDtypeStruct((h, t, dh), F32)
    pshape = jax.ShapeDtypeStruct((h, 1, dh), F32)
    return pl.pallas_call(
        body, name="rwkv_scan_bwd", grid=(nc,), out_shape=(sshape,) * 6 + (pshape,) * 3,
        in_specs=[seq] * 6 + [par] * 3 + [state, seq], out_specs=(seq,) * 6 + (par,) * 3,
        scratch_shapes=[pltpu.VMEM((h, dh, dh), F32)], compiler_params=_params("arbitrary"),
    )(*seqs, *pars, states, do)


def _local_step(x, target, w, ex):
    w = dict(w)
    c = w["mu_r"].shape[-1]
    qn, kn = w["q_norm"].reshape(1, 1, HEAD_DIM), w["k_norm"].reshape(1, 1, HEAD_DIM)
    vec = jnp.concatenate([w[n].reshape(1, c) for n in RWKV_VEC], axis=0)
    pars = [w[n].reshape(-1, 1, HEAD_DIM) for n in ("r_k", "ln_x_w", "ln_x_b")]
    no_dep = jnp.zeros(DEP_SHAPE, F32)

    x1, gate1, up1 = _ffn_fwd(x, w["ffn1_norm"], w["ffn1_w_gate"], w["ffn1_w_up"], w["ffn1_w_down"], ex.first_dep, "ffn1_fwd")
    w.update(ex.mix_weights((x1,)))
    mats = [w[n] for n in RWKV_MAT]
    q, k, v, cur = _proj_fwd(x1, w["mix_norm"], w["w_in"])
    att, *saved = _att_fwd(q, k, v, qn, kn)
    pre = _rwkv_pre_fwd(cur, vec, mats)
    seqs, gate = pre[:6], pre[6]
    opg, states = _scan_fwd(seqs, pars)
    w.update(ex.out_weights((att, opg)))
    x2 = _mixout_fwd(x1, att, opg, gate, w["w_out"])
    x3, gate2, up2 = _ffn_fwd(x2, w["ffn2_norm"], w["ffn2_w_gate"], w["ffn2_w_up"], w["ffn2_w_down"], no_dep, "ffn2_fwd")
    dy, loss = _loss_head(x3, target)

    g = {}
    dx2, g["ffn2_norm"], g["ffn2_w_gate"], g["ffn2_w_up"], g["ffn2_w_down"] = _ffn_bwd(
        x2, w["ffn2_norm"], w["ffn2_w_gate"], w["ffn2_w_up"], w["ffn2_w_down"], gate2, up2, dy, no_dep, "ffn2_bwd")
    dep = ex.send_ffn2({n: g[n] for n in ("ffn2_w_gate", "ffn2_w_up", "ffn2_w_down")})
    datt, dopg, dgate, g["w_out"] = _mixout_bwd(att, opg, gate, w["w_out"], dx2, dep)
    dscan = _scan_bwd(seqs, pars, states, dopg)
    for n, d in zip(("r_k", "ln_x_w", "ln_x_b"), dscan[6:]):
        g[n] = d
    dcur, dvec, *dmats = _rwkv_pre_bwd(cur, vec, mats, dscan[:6], dgate)
    for n, d in zip(RWKV_MAT, dmats):
        g[n] = d
    for j, n in enumerate(RWKV_VEC):
        g[n] = dvec[j:j + 1]
    dq, dk, dv, g["q_norm"], g["k_norm"] = _att_bwd(q, k, v, qn, kn, saved, datt)
    dx1, g["mix_norm"], g["w_in"] = _proj_bwd(x1, w["mix_norm"], w["w_in"], dq, dk, dv, dcur, dx2)
    dep = ex.send_mix({n: g[n] for n in ("w_in", "w_out") + RWKV_MAT}, (dx1,))
    dx, g["ffn1_norm"], g["ffn1_w_gate"], g["ffn1_w_up"], g["ffn1_w_down"] = _ffn_bwd(
        x, w["ffn1_norm"], w["ffn1_w_gate"], w["ffn1_w_up"], w["ffn1_w_down"], gate1, up1, dx1, dep, "ffn1_bwd")
    return loss, dx, g


N_SHARDS = 4


def _place():
    return lax.axis_index("x"), lax.axis_index("y"), lax.axis_index("c")


def _chip_peers(x, y):
    return [(1 - x, y), (x, 1 - y), (1 - x, 1 - y)]


HBM = pl.BlockSpec(memory_space=pltpu.HBM)
SEM = pl.BlockSpec(memory_space=pltpu.SEMAPHORE)
DEP_SHAPE = (8, 128)


class _Views:
    to_sibling = False


class _GatherViews(_Views):
    @staticmethod
    def send(i, srcs, lands, k, at):
        return srcs[i], lands[i].at[at[3]]

    @staticmethod
    def landing(i, srcs, lands, k, at):
        return srcs[i], lands[i].at[2 * at[4] + at[5]]


class _ScatterViews(_Views):
    @staticmethod
    def send(i, srcs, lands, k, at):
        return srcs[i].at[2 * at[4] + at[5]], lands[i].at[k]

    @staticmethod
    def landing(i, srcs, lands, k, at):
        return srcs[i].at[at[3]], lands[i].at[k]


def _half_rows(ref, slot, half):
    rows = ref.shape[1] // 2
    return ref.at[slot, pl.ds(pl.multiple_of(half * rows, BF16_SUBLANES), rows)]


class _HalfGatherViews(_Views):
    @staticmethod
    def send(i, srcs, lands, k, at):
        rows = srcs[i].shape[0] // 2
        return srcs[i].at[pl.ds(pl.multiple_of(at[2] * rows, BF16_SUBLANES), rows)], _half_rows(lands[i], at[3], at[2])

    @staticmethod
    def landing(i, srcs, lands, k, at):
        rows = srcs[i].shape[0] // 2
        return srcs[i].at[pl.ds(pl.multiple_of(at[2] * rows, BF16_SUBLANES), rows)], _half_rows(lands[i], 2 * at[4] + at[5], at[2])


class _ForwardViews(_Views):
    to_sibling = True

    @staticmethod
    def send(i, srcs, lands, k, at):
        mine = _half_rows(lands[i], 2 * at[4] + at[5], at[2])
        return mine, mine

    @staticmethod
    def landing(i, srcs, lands, k, at):
        theirs = _half_rows(lands[i], 2 * at[4] + at[5], 1 - at[2])
        return theirs, theirs


def _push_start(srcs, lands, views, after, name):
    ns, nl = len(srcs), len(lands)

    def body(*refs):
        src_refs, land_refs = refs[:ns], refs[ns:ns + nl]
        send_sems, recv_sems = refs[ns + nl + 1:ns + nl + 3]
        token = refs[2 * (ns + nl) + 3]
        x, y, c = _place()
        for i in range(nl):
            for k, (px, py) in enumerate(_chip_peers(x, y)):
                src, dst = views.send(i, src_refs, land_refs, k, (x, y, c, 2 * x + y, px, py))
                pltpu.make_async_remote_copy(
                    src_ref=src, dst_ref=dst, send_sem=send_sems.at[3 * i + k], recv_sem=recv_sems.at[3 * i + k],
                    device_id=(x, y, 1 - c) if views.to_sibling else (px, py, c), device_id_type=MESH).start()
        token[...] = jnp.zeros_like(token)

    sems = pltpu.SemaphoreType.DMA((3 * nl,))
    both = [pltpu.with_memory_space_constraint(a, pltpu.HBM) for a in (*srcs, *lands)]
    outs = pl.pallas_call(
        body, name=name,
        out_shape=(sems, sems, *[pltpu.HBM(a.shape, a.dtype) for a in both], jax.ShapeDtypeStruct(DEP_SHAPE, F32)),
        in_specs=[HBM] * (ns + nl) + [ANY], out_specs=(SEM, SEM, *[HBM] * (ns + nl), VMEM_FULL),
        input_output_aliases={i: 2 + i for i in range(ns + nl)},
        compiler_params=pltpu.CompilerParams(has_side_effects=pltpu.SideEffectType.DATAFLOW_SIDE_EFFECTING),
    )(*both, after)
    return outs[0], outs[1], outs[2:2 + ns], outs[2 + ns:2 + ns + nl], outs[2 + ns + nl]


def _push_wait(started, views, after, name, with_sources=False):
    send_sems, recv_sems, srcs, lands, _ = started
    ns, nl = len(srcs), len(lands)

    def body(*refs):
        src_refs, land_refs = refs[:ns], refs[ns:ns + nl]
        send_sems, recv_sems = refs[ns + nl:ns + nl + 2]
        x, y, c = _place()
        for i in range(nl):
            for k, (px, py) in enumerate(_chip_peers(x, y)):
                src, dst = views.landing(i, src_refs, land_refs, k, (x, y, c, 2 * x + y, px, py))
                landing = pltpu.make_async_remote_copy(
                    src_ref=src, dst_ref=dst, send_sem=send_sems.at[3 * i + k], recv_sem=recv_sems.at[3 * i + k],
                    device_id=(x, y, 1 - c) if views.to_sibling else (px, py, c), device_id_type=MESH)
                landing.wait_send()
                landing.wait_recv()

    outs = pl.pallas_call(
        body, name=name,
        out_shape=tuple(pltpu.HBM(a.shape, a.dtype) for a in (*srcs, *lands)),
        in_specs=[HBM] * (ns + nl) + [SEM, SEM] + [ANY] * len(after), out_specs=(HBM,) * (ns + nl),
        input_output_aliases={i: i for i in range(ns + nl)},
        compiler_params=pltpu.CompilerParams(has_side_effects=pltpu.SideEffectType.DATAFLOW_SIDE_EFFECTING),
    )(*srcs, *lands, send_sems, recv_sems, *after)
    return outs if with_sources else outs[ns:]


def _empty_lands(shards, slots, own_slot):
    lands = [lax.empty((slots,) + s.shape, s.dtype) for s in shards]
    if own_slot:
        me = 2 * lax.axis_index("x") + lax.axis_index("y")
        lands = [lax.dynamic_update_index_in_dim(z, s, me, 0) for z, s in zip(lands, shards)]
    return lands


def _sibling_swap(arrays, name):
    n = len(arrays)

    def body(*refs):
        ins, outs = refs[:n], refs[n:2 * n]
        send_sems, recv_sems = refs[2 * n:]
        x, y, c = _place()
        copies = []
        for i in range(n):
            cp = pltpu.make_async_remote_copy(
                src_ref=ins[i], dst_ref=outs[i], send_sem=send_sems.at[i], recv_sem=recv_sems.at[i],
                device_id=(x, y, 1 - c), device_id_type=MESH)
            cp.start()
            copies.append(cp)
        for cp in copies:
            cp.wait()

    return pl.pallas_call(
        body, name=name,
        out_shape=tuple(jax.ShapeDtypeStruct(a.shape, a.dtype) for a in arrays),
        in_specs=[ANY] * n, out_specs=(ANY,) * n,
        scratch_shapes=[pltpu.SemaphoreType.DMA((n,)), pltpu.SemaphoreType.DMA((n,))],
    )(*arrays)


N_DEV = 8


def _allreduce_small(pack):
    def body(in_ref, out_ref, buf, send_sems, recv_sems):
        x, y, c = _place()
        me = 4 * x + 2 * y + c
        buf[me] = in_ref[...]

        def copy(j, slot):
            px, py, pc = x ^ (j >> 2), y ^ ((j >> 1) & 1), c ^ (j & 1)
            return pltpu.make_async_remote_copy(
                src_ref=in_ref, dst_ref=buf.at[slot(px, py, pc)], send_sem=send_sems.at[j], recv_sem=recv_sems.at[j],
                device_id=(px, py, pc), device_id_type=MESH)

        for j in range(1, N_DEV):
            copy(j, lambda px, py, pc: me).start()
        for j in range(1, N_DEV):
            landing = copy(j, lambda px, py, pc: 4 * px + 2 * py + pc)
            landing.wait_send()
            landing.wait_recv()
        acc = buf[0]
        for s in range(1, N_DEV):
            acc = acc + buf[s]
        out_ref[...] = acc

    return pl.pallas_call(
        body, name="allreduce_small", out_shape=jax.ShapeDtypeStruct(pack.shape, F32),
        in_specs=[VMEM_FULL], out_specs=VMEM_FULL,
        scratch_shapes=[pltpu.VMEM((N_DEV,) + pack.shape, F32), pltpu.SemaphoreType.DMA((N_DEV,)),
                        pltpu.SemaphoreType.DMA((N_DEV,))],
    )(pack)


ROW_TILE_MAX = 256
BF16_SUBLANES = 16


def _row_tile(rows):
    for tr in range(min(rows, ROW_TILE_MAX), 0, -1):
        if rows % tr == 0 and tr % BF16_SUBLANES == 0:
            return tr
    return rows


def _reduce_own(me, part, recv, dep, name):
    _, r, cols = part.shape
    tr = _row_tile(r)

    def body(me_ref, p_ref, rv_ref, dep_ref, o_ref):
        acc = p_ref[0].astype(F32)
        for k in range(3):
            acc = acc + rv_ref[k].astype(F32)
        o_ref[...] = acc

    return pl.pallas_call(
        body, name=name, out_shape=jax.ShapeDtypeStruct((r, cols), F32),
        grid_spec=pltpu.PrefetchScalarGridSpec(
            num_scalar_prefetch=1, grid=(r // tr,),
            in_specs=[pl.BlockSpec((1, tr, cols), lambda i, me_ref: (me_ref[0], i, 0)),
                      pl.BlockSpec((3, tr, cols), lambda i, me_ref: (0, i, 0)), ANY],
            out_specs=pl.BlockSpec((tr, cols), lambda i, me_ref: (i, 0))),
        compiler_params=_params("arbitrary"),
    )(me, part, recv, dep)


def _adamw(w, ga, gb, m, v, name):
    r, cols = w.shape
    tr = _row_tile(r)
    c1 = 1.0 - ADAM_B1 ** ADAM_STEP
    c2 = 1.0 - ADAM_B2 ** ADAM_STEP

    def body(w_ref, ga_ref, gb_ref, m_ref, v_ref, g_out, d_out, m_out, v_out):
        g = ga_ref[...] + gb_ref[...]
        mn = ADAM_B1 * m_ref[...] + (1.0 - ADAM_B1) * g
        vn = ADAM_B2 * v_ref[...] + (1.0 - ADAM_B2) * (g * g)
        g_out[...] = g
        m_out[...] = mn
        v_out[...] = vn
        d_out[...] = -ADAM_LR * ((mn / c1) / (jnp.sqrt(vn / c2) + ADAM_EPS) + ADAM_WD * w_ref[...])

    tile = pl.BlockSpec((tr, cols), lambda i: (i, 0))
    shape = jax.ShapeDtypeStruct((r, cols), F32)
    return pl.pallas_call(
        body, name=name, grid=(r // tr,), out_shape=(shape,) * 4, in_specs=[tile] * 5, out_specs=(tile,) * 4,
        compiler_params=_params("arbitrary"),
    )(w, ga, gb, m, v)


PACK_COLS = 512


def _to_rows(a):
    flat = a.reshape(-1)
    pad = (-flat.shape[0]) % PACK_COLS
    return jnp.pad(flat, (0, pad)).reshape(-1, PACK_COLS)


def _pack(arrays, extra_rows=0):
    rows = [_to_rows(a) for a in arrays]
    n = sum(r.shape[0] for r in rows) + extra_rows
    pad = (-n) % 8
    return jnp.concatenate(rows + [jnp.zeros((extra_rows + pad, PACK_COLS), F32)], axis=0)


def _unpack(pack, like):
    out, at = [], 0
    for a in like:
        n = -(-a.size // PACK_COLS)
        out.append(pack[at:at + n].reshape(-1)[:a.size].reshape(a.shape))
        at += n
    return out


COL_SHARDED = ("ffn1_w_gate", "ffn1_w_up", "w_in", "ffn2_w_gate", "ffn2_w_up", "w2", "a2", "g2")
ROW_SHARDED = ("ffn1_w_down", "ffn2_w_down", "w_out", "w1", "a1", "g1")
CHUNKED = ("ffn1_w_gate", "ffn1_w_up", "ffn1_w_down", "ffn2_w_gate", "ffn2_w_up", "ffn2_w_down")
WEIGHTS = ("ffn1_norm", "ffn1_w_gate", "ffn1_w_up", "ffn1_w_down", "mix_norm", "w_in", "q_norm", "k_norm",
           "mu_r", "mu_k", "mu_v", "mu_w", "mu_a", "mu_g", "w0", "w1", "w2", "a0", "a1", "a2", "g1", "g2",
           "k_k", "k_a", "r_k", "ln_x_w", "ln_x_b", "w_out", "ffn2_norm", "ffn2_w_gate", "ffn2_w_up", "ffn2_w_down")


W_IN_GROUPS = 7
TRANSPOSED = ("ffn1_w_gate", "ffn1_w_up", "ffn2_w_gate", "ffn2_w_up")


def _shard_2d(name, a):
    return a[0].T if name in TRANSPOSED else a[0]


def _full_from_blocks(name, blocks):
    if name in CHUNKED:
        return blocks
    if name in ROW_SHARDED:
        return blocks.reshape(-1, blocks.shape[-1])
    full = blocks.transpose(1, 0, 2).reshape(blocks.shape[1], -1)
    if name == "w_in":
        return full.reshape(full.shape[0], W_IN_GROUPS, -1).transpose(1, 0, 2)
    return full


def _blocks_from_full(name, full):
    if name in CHUNKED:
        return full
    if name in ROW_SHARDED:
        return full.reshape(N_SHARDS, -1, full.shape[-1])
    if name == "w_in":
        full = full.transpose(1, 0, 2).reshape(full.shape[1], -1)
    return full.reshape(full.shape[0], N_SHARDS, -1).transpose(1, 0, 2)


FFN1_GROUP = ("ffn1_w_gate", "ffn1_w_up", "ffn1_w_down")
MIX_GROUP = ("w_in",) + RWKV_MAT
OUT_GROUP = ("w_out", "ffn2_w_gate", "ffn2_w_up", "ffn2_w_down")
FFN2_GROUP = OUT_GROUP[1:]
LATE_GROUP = ("w_in", "w_out") + RWKV_MAT


class _Exchange:
    def __init__(self, given):
        self.given = given
        first = self._gather_start(FFN1_GROUP, _HalfGatherViews, jnp.zeros(DEP_SHAPE, F32), "gather_ffn1_start")
        self.mix = self._gather_start(MIX_GROUP, _GatherViews, first[4], "gather_mix_start")
        self.out = self._gather_start(OUT_GROUP, _GatherViews, self.mix[4], "gather_out_start")
        self.first_dep = self.out[4]
        halves = _push_wait(first, _HalfGatherViews, (self.first_dep,), "gather_ffn1_wait")
        passed = _push_start([], halves, _ForwardViews, halves[0], "gather_ffn1_pass_start")
        self.first_weights = self._full(FFN1_GROUP, _push_wait(passed, _ForwardViews, (passed[4],), "gather_ffn1_pass_wait"))
        self.parts, self.recv = {}, {}

    def _shards(self, names):
        return [_shard_2d(n, self.given[n]).astype(BF16) for n in names]

    @staticmethod
    def _full(names, blocks):
        out = {}
        for n, b in zip(names, blocks):
            full = _full_from_blocks(n, b)
            out[n] = full.astype(F32) if n in RWKV_MAT else full
        return out

    def _gather_start(self, names, views, after, name):
        shards = self._shards(names)
        return _push_start(shards, _empty_lands(shards, N_SHARDS, True), views, after, name)

    def mix_weights(self, after):
        return self._full(MIX_GROUP, _push_wait(self.mix, _GatherViews, after, "gather_mix_wait"))

    def out_weights(self, after):
        return self._full(OUT_GROUP, _push_wait(self.out, _GatherViews, after, "gather_out_wait"))

    def _scatter_start(self, grads, name):
        names = tuple(grads)
        parts = [_blocks_from_full(n, grads[n]) for n in names]
        self.parts.update(zip(names, parts))
        lands = [lax.empty((3,) + p.shape[1:], BF16) for p in parts]
        return _push_start([p.astype(BF16) for p in parts], lands, _ScatterViews, jnp.zeros(DEP_SHAPE, F32), name)

    def _scatter_done(self, started, names, after, name):
        outs = _push_wait(started, _ScatterViews, after, name, with_sources=True)
        for n, sent, got in zip(names, outs[:len(names)], outs[len(names):]):
            self.recv[n] = got
            if self.parts[n].dtype == BF16:
                self.parts[n] = sent

    def send_ffn2(self, grads):
        self.ffn2 = self._scatter_start(grads, "scatter_ffn2_start")
        return self.ffn2[4]

    def send_mix(self, grads, after):
        self._scatter_done(self.ffn2, FFN2_GROUP, after, "scatter_ffn2_wait")
        self.late = self._scatter_start(grads, "scatter_late_start")
        return self.late[4]

    def send_ffn1(self, grads):
        self.ffn1 = self._scatter_start(grads, "scatter_ffn1_start")
        return self.ffn1[4]

    def late_received(self, after):
        self._scatter_done(self.late, LATE_GROUP, after, "scatter_late_wait")

    def ffn1_received(self, after):
        self._scatter_done(self.ffn1, FFN1_GROUP, after, "scatter_ffn1_wait")


def kernel(
        x, ffn1_norm, ffn1_w_gate, ffn1_w_up, ffn1_w_down, mix_norm, w_in, q_norm, k_norm, mu_r, mu_k, mu_v, mu_w,
        mu_a, mu_g, w0, w1, w2, a0, a1, a2, g1, g2, k_k, k_a, r_k, ln_x_w, ln_x_b, w_out, ffn2_norm, ffn2_w_gate,
        ffn2_w_up, ffn2_w_down, loss_target, m_ffn1_norm, m_ffn1_w_gate, m_ffn1_w_up, m_ffn1_w_down, m_mix_norm,
        m_w_in, m_q_norm, m_k_norm, m_mu_r, m_mu_k, m_mu_v, m_mu_w, m_mu_a, m_mu_g, m_w0, m_w1, m_w2, m_a0, m_a1,
        m_a2, m_g1, m_g2, m_k_k, m_k_a, m_r_k, m_ln_x_w, m_ln_x_b, m_w_out, m_ffn2_norm, m_ffn2_w_gate, m_ffn2_w_up,
        m_ffn2_w_down, v_ffn1_norm, v_ffn1_w_gate, v_ffn1_w_up, v_ffn1_w_down, v_mix_norm, v_w_in, v_q_norm, v_k_norm,
        v_mu_r, v_mu_k, v_mu_v, v_mu_w, v_mu_a, v_mu_g, v_w0, v_w1, v_w2, v_a0, v_a1, v_a2, v_g1, v_g2, v_k_k, v_k_a,
        v_r_k, v_ln_x_w, v_ln_x_b, v_w_out, v_ffn2_norm, v_ffn2_w_gate, v_ffn2_w_up, v_ffn2_w_down):
    given = dict(locals())
    sharded = COL_SHARDED + ROW_SHARDED
    sharded = tuple(n for n in WEIGHTS if n in sharded)
    small = tuple(n for n in WEIGHTS if n not in sharded)

    ex = _Exchange(given)
    w = {n: given[n] for n in small}
    w.update(ex.first_weights)
    loss, dx, g = _local_step(x[0], loss_target[0], w, ex)
    dep = ex.send_ffn1({n: g[n] for n in FFN1_GROUP})

    me = (2 * lax.axis_index("x") + lax.axis_index("y")).astype(jnp.int32).reshape(1)
    out = {}

    def settle(names, dep, tag):
        mine = []
        for n in names:
            p, rv = ex.parts[n], ex.recv[n]
            p2 = p.reshape(N_SHARDS, -1, p.shape[-1])
            mine.append(_reduce_own(me, p2, rv.reshape(3, -1, rv.shape[-1]), dep, f"reduce_{n}"))
        theirs = _sibling_swap(mine, f"sibling_swap_{tag}")
        for n, a, b in zip(names, mine, theirs):
            shape = given[n].shape
            res = _adamw(_shard_2d(n, given[n]), a, b, _shard_2d(n, given["m_" + n]), _shard_2d(n, given["v_" + n]), f"adamw_{n}")
            out[n] = [(r.T if n in TRANSPOSED else r).reshape(shape) for r in res]
        return tuple(out[n][1] for n in names)

    ex.late_received((dep,))
    last = settle(tuple(n for n in sharded if n not in FFN1_GROUP), dep, "rest")

    gpack = _pack([g[n] for n in small], extra_rows=1)
    n_rows = sum(-(-given[n].size // PACK_COLS) for n in small)
    gpack = gpack.at[n_rows, :loss.shape[1]].set(loss[0])
    gsum = _allreduce_small(gpack)
    res = _adamw(_pack([given[n] for n in small], 1), gsum, jnp.zeros_like(gsum), _pack([given["m_" + n] for n in small], 1),
                 _pack([given["v_" + n] for n in small], 1), "adamw_small")
    like = [given[n] for n in small]
    for j, r in enumerate(res):
        for n, a in zip(small, _unpack(r, like)):
            out.setdefault(n, [None] * 4)[j] = a
    total_loss = gsum[n_rows, 0]

    ex.ffn1_received((*last, res[1]))
    settle(FFN1_GROUP, jnp.zeros(DEP_SHAPE, F32), "ffn1")
    return (total_loss, dx[None], *[out[n][0] for n in WEIGHTS], *[out[n][1] for n in WEIGHTS],
            *[out[n][2] for n in WEIGHTS], *[out[n][3] for n in WEIGHTS])
```

```python
import functools

import jax
import jax.numpy as jnp
from jax import lax
from jax.experimental import pallas as pl
from jax.experimental.pallas import tpu as pltpu

F32 = jnp.float32
BF16 = jnp.bfloat16
MESH = pl.DeviceIdType.MESH

RMS_EPS = 1e-6
GN_EPS = 64e-5
NEG_INF = -1e30
FFN_RESIDUAL = 0.5
HEAD_DIM = 64
ATT_BLOCK = 128
DILATIONS = (1, 4, 16)
SCAN_CHUNK = 64
TOKEN_TILE = 256
FFN_BWD_TILE = 512

ADAM_LR = 0.001
ADAM_B1 = 0.9
ADAM_B2 = 0.999
ADAM_EPS = 1e-08
ADAM_WD = 0.01
ADAM_STEP = 10

VMEM_FULL = pl.BlockSpec(memory_space=pltpu.VMEM)
ANY = pl.BlockSpec(memory_space=pl.ANY)


VMEM_LIMIT = 56 * 1024 * 1024


def _params(*sem):
    return pltpu.CompilerParams(dimension_semantics=sem, vmem_limit_bytes=VMEM_LIMIT)


def _dot(a, b, dims):
    return lax.dot_general(a.astype(BF16), b.astype(BF16), (dims, ((), ())), preferred_element_type=F32)


def _dot_nn(a, b):
    return _dot(a, b, ((1,), (0,)))


def _dot_nt(a, b):
    return _dot(a, b, ((1,), (1,)))


def _dot_tn(a, b):
    return _dot(a, b, ((0,), (0,)))


@jax.custom_vjp
def _mm(a, b):
    return _dot_nn(a, b)


def _mm_fwd(a, b):
    return _dot_nn(a, b), (a, b)


def _mm_bwd(res, g):
    a, b = res
    return _dot_nt(g, b).astype(a.dtype), _dot_tn(a, g).astype(b.dtype)


_mm.defvjp(_mm_fwd, _mm_bwd)


def _bdot(a, b, ca, cb):
    return lax.dot_general(a.astype(BF16), b.astype(BF16), (((ca,), (cb,)), ((0,), (0,))), preferred_element_type=F32)


@jax.custom_vjp
def _bmm_nt(a, b):
    return _bdot(a, b, 2, 2)


def _bmm_nt_fwd(a, b):
    return _bdot(a, b, 2, 2), (a, b)


def _bmm_nt_bwd(res, g):
    a, b = res
    return _bdot(g, b, 2, 1), _bdot(g, a, 1, 1)


_bmm_nt.defvjp(_bmm_nt_fwd, _bmm_nt_bwd)


@jax.custom_vjp
def _bmm_nn(a, b):
    return _bdot(a, b, 2, 1)


def _bmm_nn_fwd(a, b):
    return _bdot(a, b, 2, 1), (a, b)


def _bmm_nn_bwd(res, g):
    a, b = res
    return _bdot(g, b, 2, 2), _bdot(a, g, 1, 1)


_bmm_nn.defvjp(_bmm_nn_fwd, _bmm_nn_bwd)


@jax.custom_vjp
def _bmm_tn(a, b):
    return _bdot(a, b, 1, 1)


def _bmm_tn_fwd(a, b):
    return _bdot(a, b, 1, 1), (a, b)


def _bmm_tn_bwd(res, g):
    a, b = res
    return _bdot(b, g, 2, 2), _bdot(a, g, 2, 1)


_bmm_tn.defvjp(_bmm_tn_fwd, _bmm_tn_bwd)


def _hdot(a, b, ca, cb):
    return lax.dot_general(a, b, (((ca,), (cb,)), ((0,), (0,))), precision=lax.Precision.HIGH, preferred_element_type=F32)


def _sigmoid(x):
    return 1.0 / (1.0 + jnp.exp(-x))


def _rms(x):
    return lax.rsqrt(jnp.mean(x * x, axis=-1, keepdims=True) + RMS_EPS)


def _ffn_fwd(x, norm, wg, wu, wd, dep, name):
    t, d = x.shape
    nc, fc, _ = wg.shape
    tm = TOKEN_TILE

    def body(x_ref, n_ref, wg_ref, wu_ref, wd_ref, dep_ref, o_ref, g_ref, u_ref):
        xv = x_ref[...]
        h = (xv * _rms(xv) * n_ref[...]).astype(BF16)
        acc = jnp.zeros((tm, d), F32)
        for c in range(nc):
            g = _dot_nt(h, wg_ref[c])
            u = _dot_nt(h, wu_ref[c])
            g_ref[c] = g.astype(BF16)
            u_ref[c] = u.astype(BF16)
            a = (g * _sigmoid(g) * u).astype(BF16)
            acc = acc + jnp.dot(a, wd_ref[c], preferred_element_type=F32)
        o_ref[...] = xv + FFN_RESIDUAL * acc

    tile = pl.BlockSpec((tm, d), lambda i: (i, 0))
    hidden = pl.BlockSpec((nc, tm, fc), lambda i: (0, i, 0))
    hshape = jax.ShapeDtypeStruct((nc, t, fc), BF16)
    return pl.pallas_call(
        body, name=name, grid=(t // tm,), out_shape=(jax.ShapeDtypeStruct((t, d), F32), hshape, hshape),
        in_specs=[tile, pl.BlockSpec((1, d), lambda i: (0, 0)), VMEM_FULL, VMEM_FULL, VMEM_FULL, ANY],
        out_specs=(tile, hidden, hidden), compiler_params=_params("arbitrary"),
    )(x, norm, wg, wu, wd, dep)


def _rmsnorm_bwd(xv, gain, dh):
    rs = _rms(xv)
    xn = xv * rs
    dxn = dh * gain
    dx = rs * (dxn - xn * jnp.mean(dxn * xn, axis=-1, keepdims=True))
    return dx, jnp.sum(dh * xn, axis=0, keepdims=True)


def _ffn_bwd(x, norm, wg, wu, wd, gate, up, dy, dep, name):
    t, d = x.shape
    nc, fc, _ = wg.shape
    tm = FFN_BWD_TILE
    nt = t // tm

    def body(x_ref, n_ref, wg_ref, wu_ref, wd_ref, g_ref, u_ref, dy_ref, dep_ref, dx_ref, dn_ref, dwg_ref, dwu_ref,
             dwd_ref, dh_ref, ag_ref, au_ref, ad_ref):
        c, i = pl.program_id(0), pl.program_id(1)
        rows = pl.ds(pl.multiple_of(i * tm, tm), tm)
        xv = x_ref[...]
        gain = n_ref[...]
        h = (xv * _rms(xv) * gain).astype(BF16)
        dy = dy_ref[...]
        dyb = (FFN_RESIDUAL * dy).astype(BF16)
        g = g_ref[0].astype(F32)
        u = u_ref[0].astype(F32)
        sg = _sigmoid(g)
        s = g * sg
        a = (s * u).astype(BF16)
        da = _dot_nt(dyb, wd_ref[0])
        dub = (da * s).astype(BF16)
        dgb = (da * u * (sg * (1.0 + g * (1.0 - sg)))).astype(BF16)
        dwd_c = _dot_tn(a, dyb)
        dwg_c = _dot_tn(dgb, h)
        dwu_c = _dot_tn(dub, h)
        dh_c = _dot_nn(dgb, wg_ref[0]) + _dot_nn(dub, wu_ref[0])

        @pl.when(i == 0)
        def _():
            ad_ref[...] = dwd_c
            ag_ref[...] = dwg_c
            au_ref[...] = dwu_c

        @pl.when(i > 0)
        def _():
            ad_ref[...] += dwd_c
            ag_ref[...] += dwg_c
            au_ref[...] += dwu_c

        @pl.when(i == nt - 1)
        def _():
            dwd_ref[0] = ad_ref[...].astype(BF16)
            dwg_ref[0] = ag_ref[...].astype(BF16)
            dwu_ref[0] = au_ref[...].astype(BF16)

        @pl.when(c == 0)
        def _():
            dh_ref[rows, :] = dh_c

        @pl.when(c > 0)
        def _():
            dh_ref[rows, :] += dh_c

        @pl.when(c == nc - 1)
        def _():
            dx, dn = _rmsnorm_bwd(xv, gain, dh_ref[rows, :])
            dx_ref[...] = dx + dy

            @pl.when(i == 0)
            def _():
                dn_ref[...] = dn

            @pl.when(i > 0)
            def _():
                dn_ref[...] += dn

    tile = pl.BlockSpec((tm, d), lambda c, i: (i, 0))
    row = pl.BlockSpec((1, d), lambda c, i: (0, 0))
    wrow = pl.BlockSpec((1, fc, d), lambda c, i: (c, 0, 0), pipeline_mode=pl.Buffered(1))
    hidden = pl.BlockSpec((1, tm, fc), lambda c, i: (c, i, 0))
    last = pl.BlockSpec((tm, d), lambda c, i: (jnp.where(c == nc - 1, i, 0), 0))
    return pl.pallas_call(
        body, name=name, grid=(nc, nt),
        out_shape=(jax.ShapeDtypeStruct((t, d), F32), jax.ShapeDtypeStruct((1, d), F32),
                   jax.ShapeDtypeStruct(wg.shape, BF16), jax.ShapeDtypeStruct(wu.shape, BF16),
                   jax.ShapeDtypeStruct(wd.shape, BF16)),
        in_specs=[tile, row, wrow, wrow, wrow, hidden, hidden, tile, ANY],
        out_specs=(last, row, wrow, wrow, wrow),
        scratch_shapes=[pltpu.VMEM((t, d), F32)] + [pltpu.VMEM((fc, d), F32)] * 3,
        compiler_params=_params("arbitrary", "arbitrary"),
    )(x, norm, wg, wu, wd, gate, up, dy, dep)


def _store_heads(ref, v):
    for h in range(ref.shape[0]):
        ref[h] = v[:, h * HEAD_DIM:(h + 1) * HEAD_DIM]


def _load_heads(ref):
    return jnp.concatenate([ref[h] for h in range(ref.shape[0])], axis=-1)


N_HEAD_GROUPS = 3


def _proj_fwd(x, norm, w):
    t, d = x.shape
    ng, _, c = w.shape
    nh = c // HEAD_DIM
    tm = TOKEN_TILE

    def body(x_ref, n_ref, w_ref, q_ref, k_ref, v_ref, cur_ref):
        xv = x_ref[...]
        h = (xv * _rms(xv) * n_ref[...]).astype(BF16)
        for m, ref in enumerate((q_ref, k_ref, v_ref)):
            _store_heads(ref, jnp.dot(h, w_ref[m], preferred_element_type=F32))
        for m in range(N_HEAD_GROUPS, ng):
            j = m - N_HEAD_GROUPS
            cur_ref[:, j * c:(j + 1) * c] = jnp.dot(h, w_ref[m], preferred_element_type=F32)

    heads = pl.BlockSpec((nh, tm, HEAD_DIM), lambda i: (0, i, 0))
    hshape = jax.ShapeDtypeStruct((nh, t, HEAD_DIM), F32)
    wide = (ng - N_HEAD_GROUPS) * c
    return pl.pallas_call(
        body, name="proj_fwd", grid=(t // tm,),
        out_shape=(hshape, hshape, hshape, jax.ShapeDtypeStruct((t, wide), F32)),
        in_specs=[pl.BlockSpec((tm, d), lambda i: (i, 0)), pl.BlockSpec((1, d), lambda i: (0, 0)), VMEM_FULL],
        out_specs=(heads, heads, heads, pl.BlockSpec((tm, wide), lambda i: (i, 0))),
        compiler_params=_params("arbitrary"),
    )(x, norm, w)


def _proj_bwd(x, norm, w, dq, dk, dv, dcur, dres):
    t, d = x.shape
    ng, _, c = w.shape
    nh = c // HEAD_DIM
    tm = TOKEN_TILE

    def body(x_ref, n_ref, w_ref, dq_ref, dk_ref, dv_ref, dcur_ref, dres_ref, dx_ref, dn_ref, dw_ref):
        i = pl.program_id(0)

        @pl.when(i == 0)
        def _():
            dw_ref[...] = jnp.zeros_like(dw_ref)
            dn_ref[...] = jnp.zeros_like(dn_ref)

        xv = x_ref[...]
        gain = n_ref[...]
        h = (xv * _rms(xv) * gain).astype(BF16)
        dh = jnp.zeros((tm, d), F32)
        for m in range(ng):
            j = m - N_HEAD_GROUPS
            dp = _load_heads((dq_ref, dk_ref, dv_ref)[m]) if j < 0 else dcur_ref[:, j * c:(j + 1) * c]
            dp = dp.astype(BF16)
            dw_ref[m] += _dot_tn(h, dp)
            dh = dh + _dot_nt(dp, w_ref[m])
        dx, dn = _rmsnorm_bwd(xv, gain, dh)
        dx_ref[...] = dx + dres_ref[...]
        dn_ref[...] += dn

    tile = pl.BlockSpec((tm, d), lambda i: (i, 0))
    row = pl.BlockSpec((1, d), lambda i: (0, 0))
    heads = pl.BlockSpec((nh, tm, HEAD_DIM), lambda i: (0, i, 0))
    wide = (ng - N_HEAD_GROUPS) * c
    return pl.pallas_call(
        body, name="proj_bwd", grid=(t // tm,),
        out_shape=(jax.ShapeDtypeStruct((t, d), F32), jax.ShapeDtypeStruct((1, d), F32),
                   jax.ShapeDtypeStruct(w.shape, F32)),
        in_specs=[tile, row, VMEM_FULL, heads, heads, heads, pl.BlockSpec((tm, wide), lambda i: (i, 0)), tile],
        out_specs=(tile, row, VMEM_FULL),
        compiler_params=_params("arbitrary"),
    )(x, norm, w, dq, dk, dv, dcur, dres)


def _mixout_fwd(x, att, opg, gate, w):
    t, d = x.shape
    nh = att.shape[0]
    half = gate.shape[1]
    tm = TOKEN_TILE

    def body(x_ref, att_ref, opg_ref, g_ref, w_ref, o_ref):
        mix = jnp.concatenate([_load_heads(att_ref), _load_heads(opg_ref) * g_ref[...]], axis=-1).astype(BF16)
        o_ref[...] = x_ref[...] + jnp.dot(mix, w_ref[...], preferred_element_type=F32)

    tile = pl.BlockSpec((tm, d), lambda i: (i, 0))
    htile = pl.BlockSpec((tm, half), lambda i: (i, 0))
    heads = pl.BlockSpec((nh, tm, HEAD_DIM), lambda i: (0, i, 0))
    return pl.pallas_call(
        body, name="mixout_fwd", grid=(t // tm,), out_shape=jax.ShapeDtypeStruct((t, d), F32),
        in_specs=[tile, heads, heads, htile, VMEM_FULL], out_specs=tile, compiler_params=_params("arbitrary"),
    )(x, att, opg, gate, w)


def _mixout_bwd(att, opg, gate, w, dy, dep):
    nh, t, _ = att.shape
    half = gate.shape[1]
    d = dy.shape[1]
    tm = TOKEN_TILE

    def body(att_ref, opg_ref, g_ref, w_ref, dy_ref, dep_ref, datt_ref, dopg_ref, dg_ref, dw_ref):
        i = pl.program_id(0)
        opg_v, g_v = _load_heads(opg_ref), g_ref[...]
        mix = jnp.concatenate([_load_heads(att_ref), opg_v * g_v], axis=-1).astype(BF16)
        dyb = dy_ref[...].astype(BF16)
        dmix = _dot_nt(dyb, w_ref[...])
        dw = _dot_tn(mix, dyb)
        _store_heads(datt_ref, dmix[:, :half])
        drw = dmix[:, half:]
        _store_heads(dopg_ref, drw * g_v)
        dg_ref[...] = drw * opg_v

        @pl.when(i == 0)
        def _():
            dw_ref[...] = dw

        @pl.when(i > 0)
        def _():
            dw_ref[...] += dw

    tile = pl.BlockSpec((tm, d), lambda i: (i, 0))
    htile = pl.BlockSpec((tm, half), lambda i: (i, 0))
    heads = pl.BlockSpec((nh, tm, HEAD_DIM), lambda i: (0, i, 0))
    hshape = jax.ShapeDtypeStruct((nh, t, HEAD_DIM), F32)
    return pl.pallas_call(
        body, name="mixout_bwd", grid=(t // tm,),
        out_shape=(hshape, hshape, jax.ShapeDtypeStruct((t, half), F32), jax.ShapeDtypeStruct(w.shape, F32)),
        in_specs=[heads, heads, htile, VMEM_FULL, tile, ANY],
        out_specs=(heads, heads, htile, pl.BlockSpec(w.shape, lambda i: (0, 0))),
        compiler_params=_params("arbitrary"),
    )(att, opg, gate, w, dy, dep)


def _loss_head(y, target):
    t, d = y.shape
    tm = TOKEN_TILE

    def body(y_ref, t_ref, dy_ref, loss_ref):
        i = pl.program_id(0)
        err = y_ref[...] - t_ref[...]
        dy_ref[...] = err * (1.0 / d)
        part = 0.5 * jnp.sum(jnp.mean(err * err, axis=-1, keepdims=True), axis=0, keepdims=True)

        @pl.when(i == 0)
        def _():
            loss_ref[...] = jnp.zeros_like(loss_ref)

        loss_ref[...] += jnp.broadcast_to(part, loss_ref.shape)

    tile = pl.BlockSpec((tm, d), lambda i: (i, 0))
    return pl.pallas_call(
        body, name="loss_head", grid=(t // tm,),
        out_shape=(jax.ShapeDtypeStruct((t, d), F32), jax.ShapeDtypeStruct((1, 128), F32)),
        in_specs=[tile, tile], out_specs=(tile, pl.BlockSpec((1, 128), lambda i: (0, 0))),
        compiler_params=_params("arbitrary"),
    )(y, target)


def _att_pattern(q, k, v, qn, kn, nb):
    g, blk, _ = q.shape
    qh = q * _rms(q) * qn
    kh = k * _rms(k) * kn
    scale = HEAD_DIM ** -0.5
    qi = lax.broadcasted_iota(jnp.int32, (blk, blk), 0)
    kj = lax.broadcasted_iota(jnp.int32, (blk, blk), 1)
    sc = jnp.where(kj <= qi, _bmm_nt(qh, kh) * scale, NEG_INF)
    top = jnp.max(sc, axis=-1, keepdims=True)
    if nb > 1:
        khp = jnp.concatenate([kh[:1], kh[:-1]], axis=0)
        vp = jnp.concatenate([v[:1], v[:-1]], axis=0)
        has_prev = lax.broadcasted_iota(jnp.int32, (g, 1, 1), 0) % nb != 0
        sp = jnp.where((kj >= qi) & has_prev, _bmm_nt(qh, khp) * scale, NEG_INF)
        top = jnp.maximum(top, jnp.max(sp, axis=-1, keepdims=True))
    m = lax.stop_gradient(top)
    pc = jnp.exp(sc - m)
    den = jnp.sum(pc, axis=-1, keepdims=True)
    acc = _bmm_nn(pc, v)
    if nb > 1:
        pp = jnp.exp(sp - m)
        den = den + jnp.sum(pp, axis=-1, keepdims=True)
        acc = acc + _bmm_nn(pp, vp)
    o = acc / den
    return o, jnp.broadcast_to(m + jnp.log(den), o.shape)


def _pattern_rows(t, dil):
    nb = t // (ATT_BLOCK * dil)
    starts = [n * ATT_BLOCK * dil + r for r in range(dil) for n in range(nb)]
    return [pl.ds(s, ATT_BLOCK, stride=dil) if dil > 1 else pl.ds(s, ATT_BLOCK) for s in starts], nb


def _take(ref, rows):
    return jnp.stack([ref[0, r, :] for r in rows])


def _put(ref, rows, val):
    for g, r in enumerate(rows):
        ref[0, r, :] = val[g]


def _put_add(ref, rows, val):
    for g, r in enumerate(rows):
        ref[0, r, :] += val[g]


def _merge_fn(o1, o2, o3, l1, l2, l3):
    m = lax.stop_gradient(jnp.maximum(jnp.maximum(l1, l2), l3))
    e1, e2, e3 = jnp.exp(l1 - m), jnp.exp(l2 - m), jnp.exp(l3 - m)
    return (e1 * o1 + e2 * o2 + e3 * o3) / (e1 + e2 + e3)


def _att_head_specs(t):
    head = pl.BlockSpec((1, t, HEAD_DIM), lambda h: (h, 0, 0))
    gain = pl.BlockSpec((1, 1, HEAD_DIM), lambda h: (0, 0, 0))
    return head, gain


def _att_fwd(q, k, v, qn, kn):
    nh, t, dh = q.shape
    head, gain = _att_head_specs(t)

    def body(q_ref, k_ref, v_ref, qn_ref, kn_ref, att_ref, *saved):
        o_refs, l_refs = saved[:3], saved[3:]
        for p, dil in enumerate(DILATIONS):
            rows, nb = _pattern_rows(t, dil)
            o, lse = _att_pattern(_take(q_ref, rows), _take(k_ref, rows), _take(v_ref, rows), qn_ref[...], kn_ref[...], nb)
            _put(o_refs[p], rows, o)
            _put(l_refs[p], rows, lse)

        def merge(j, carry):
            rows = pl.ds(pl.multiple_of(j * ATT_BLOCK, ATT_BLOCK), ATT_BLOCK)
            att_ref[0, rows, :] = _merge_fn(*[r[0, rows, :] for r in saved])
            return carry

        lax.fori_loop(0, t // ATT_BLOCK, merge, 0)

    return pl.pallas_call(
        body, name="att_fwd", grid=(nh,), out_shape=(jax.ShapeDtypeStruct(q.shape, F32),) * 7,
        in_specs=[head, head, head, gain, gain], out_specs=(head,) * 7, compiler_params=_params("arbitrary"),
    )(q, k, v, qn, kn)


def _att_bwd(q, k, v, qn, kn, saved, datt):
    nh, t, dh = q.shape
    head, gain = _att_head_specs(t)

    def body(q_ref, k_ref, v_ref, qn_ref, kn_ref, o1, o2, o3, l1, l2, l3, datt_ref,
             dq_ref, dk_ref, dv_ref, dqn_ref, dkn_ref, *ct_refs):
        for ref in (dq_ref, dk_ref, dv_ref):
            ref[...] = jnp.zeros_like(ref)

        @pl.when(pl.program_id(0) == 0)
        def _():
            dqn_ref[...] = jnp.zeros_like(dqn_ref)
            dkn_ref[...] = jnp.zeros_like(dkn_ref)

        def merge_cotangents(j, carry):
            rows = pl.ds(pl.multiple_of(j * ATT_BLOCK, ATT_BLOCK), ATT_BLOCK)
            _, merge_vjp = jax.vjp(_merge_fn, *[r[0, rows, :] for r in (o1, o2, o3, l1, l2, l3)])
            for ref, val in zip(ct_refs, merge_vjp(datt_ref[0, rows, :])):
                ref[0, rows, :] = val
            return carry

        lax.fori_loop(0, t // ATT_BLOCK, merge_cotangents, 0)

        for p, dil in enumerate(DILATIONS):
            rows, nb = _pattern_rows(t, dil)
            cts = [_take(ct_refs[p], rows), _take(ct_refs[3 + p], rows)]
            _, pattern_vjp = jax.vjp(functools.partial(_att_pattern, nb=nb), _take(q_ref, rows), _take(k_ref, rows),
                                     _take(v_ref, rows), qn_ref[...], kn_ref[...])
            dq, dk, dv, dgq, dgk = pattern_vjp((cts[0], cts[1]))
            _put_add(dq_ref, rows, dq)
            _put_add(dk_ref, rows, dk)
            _put_add(dv_ref, rows, dv)
            dqn_ref[...] += dgq
            dkn_ref[...] += dgk

    hshape = jax.ShapeDtypeStruct(q.shape, F32)
    gshape = jax.ShapeDtypeStruct((1, 1, dh), F32)
    return pl.pallas_call(
        body, name="att_bwd", grid=(nh,), out_shape=(hshape, hshape, hshape, gshape, gshape),
        in_specs=[head, head, head, gain, gain] + [head] * 7, out_specs=(head, head, head, gain, gain),
        scratch_shapes=[pltpu.VMEM((1, t, dh), F32)] * 6, compiler_params=_params("arbitrary"),
    )(q, k, v, qn, kn, *saved, datt)


RWKV_VEC = ("mu_r", "mu_k", "mu_v", "mu_w", "mu_a", "mu_g", "w0", "a0", "k_k", "k_a")
RWKV_MAT = ("w1", "w2", "a1", "a2", "g1", "g2")


def _rwkv_pre_fn(cur, prev, vec, w1, w2, a1, a2, g1, g2):
    c = cur.shape[1] // 4
    mu_r, mu_k, mu_v, mu_w, mu_a, mu_g, w0, a0, k_k, k_a = (vec[j:j + 1] for j in range(10))

    def lerp(j, mu):
        xc, xp = cur[:, j * c:(j + 1) * c], prev[:, j * c:(j + 1) * c]
        return xc + (xp - xc) * mu

    r, k, v = lerp(0, mu_r), lerp(1, mu_k), lerp(2, mu_v)
    cw, ca, cg = lerp(3, mu_w), lerp(3, mu_a), lerp(3, mu_g)
    z = w0 + _mm(jnp.tanh(_mm(cw, w1)), w2)
    w_log = jnp.minimum(z, 0.0) - jnp.log(1.0 + jnp.exp(-jnp.abs(z))) - 0.5
    lw = -jnp.exp(w_log)
    a = _sigmoid(a0 + _mm(_mm(ca, a1), a2))
    gate = _mm(_sigmoid(_mm(cg, g1)), g2)
    kkraw = k * k_k
    kmod = k * (1.0 + (a - 1.0) * k_a)
    return r, lw, kmod, v, kkraw, a, gate


HALO_ROWS = 8


def _rwkv_pre_specs(c, mats, tile_of):
    tm = TOKEN_TILE
    nh = c // HEAD_DIM
    wide = pl.BlockSpec((tm, 4 * c), lambda j: (tile_of(j), 0))
    halo = pl.BlockSpec((HALO_ROWS, 4 * c), lambda j: (jnp.maximum(tile_of(j) * (tm // HALO_ROWS) - 1, 0), 0))
    one = pl.BlockSpec((tm, c), lambda j: (tile_of(j), 0))
    heads = pl.BlockSpec((nh, tm, HEAD_DIM), lambda j: (0, tile_of(j), 0))
    vec = pl.BlockSpec((10, c), lambda j: (0, 0))
    mspecs = [pl.BlockSpec(m.shape, lambda j: (0, 0)) for m in mats]
    return wide, halo, one, heads, vec, mspecs


def _previous_rows(cur, halo, tile):
    first = jnp.where(tile > 0, halo[HALO_ROWS - 1:HALO_ROWS], 0.0)
    rows = lax.broadcasted_iota(jnp.int32, cur.shape, 0)
    return jnp.where(rows == 0, first, pltpu.roll(cur, 1, axis=0))


def _rwkv_pre_fwd(cur, vec, mats):
    t, c4 = cur.shape
    c = c4 // 4
    wide, halo, one, heads, vspec, mspecs = _rwkv_pre_specs(c, mats, lambda j: j)

    def body(cur_ref, halo_ref, vec_ref, *rest):
        mrefs, outs = rest[:6], rest[6:]
        cur_v = cur_ref[...]
        prev = _previous_rows(cur_v, halo_ref[...], pl.program_id(0))
        vals = _rwkv_pre_fn(cur_v, prev, vec_ref[...], *(m[...] for m in mrefs))
        for ref, val in zip(outs[:6], vals[:6]):
            _store_heads(ref, val)
        outs[6][...] = vals[6]

    hshape = jax.ShapeDtypeStruct((c // HEAD_DIM, t, HEAD_DIM), F32)
    return pl.pallas_call(
        body, name="rwkv_pre_fwd", grid=(t // TOKEN_TILE,), out_shape=(hshape,) * 6 + (jax.ShapeDtypeStruct((t, c), F32),),
        in_specs=[wide, halo, vspec] + mspecs, out_specs=(heads,) * 6 + (one,), compiler_params=_params("arbitrary"),
    )(cur, cur, vec, *mats)


def _rwkv_pre_bwd(cur, vec, mats, cts, dgate):
    t, c4 = cur.shape
    c = c4 // 4
    tm = TOKEN_TILE
    nt = t // tm
    wide, halo, one, heads, vspec, mspecs = _rwkv_pre_specs(c, mats, lambda j: nt - 1 - j)

    def body(cur_ref, halo_ref, vec_ref, *rest):
        mrefs, ctrefs, dgate_ref, outs, carry_ref = rest[:6], rest[6:12], rest[12], rest[13:-1], rest[-1]
        j = pl.program_id(0)

        @pl.when(j == 0)
        def _():
            carry_ref[...] = jnp.zeros_like(carry_ref)
            for ref in outs[1:]:
                ref[...] = jnp.zeros_like(ref)

        cur_v = cur_ref[...]
        prev = _previous_rows(cur_v, halo_ref[...], nt - 1 - j)
        _, vjp = jax.vjp(_rwkv_pre_fn, cur_v, prev, vec_ref[...], *(m[...] for m in mrefs))
        grads = vjp(tuple(_load_heads(r) for r in ctrefs) + (dgate_ref[...],))
        dprev = grads[1]
        rows = lax.broadcasted_iota(jnp.int32, dprev.shape, 0)
        outs[0][...] = grads[0] + jnp.where(rows == tm - 1, carry_ref[0:1], pltpu.roll(dprev, tm - 1, axis=0))
        carry_ref[0:1] = dprev[0:1]
        for ref, val in zip(outs[1:], grads[2:]):
            ref[...] += val

    return pl.pallas_call(
        body, name="rwkv_pre_bwd", grid=(nt,),
        out_shape=(jax.ShapeDtypeStruct(cur.shape, F32), jax.ShapeDtypeStruct(vec.shape, F32))
        + tuple(jax.ShapeDtypeStruct(m.shape, F32) for m in mats),
        in_specs=[wide, halo, vspec] + mspecs + [heads] * 6 + [one], out_specs=(wide, vspec) + tuple(mspecs),
        scratch_shapes=[pltpu.VMEM((HALO_ROWS, c4), F32)], compiler_params=_params("arbitrary"),
    )(cur, cur, vec, *mats, *cts, dgate)


def _scan_chunk_fn(h0, r, lw, k, v, kkraw, a, rk, lnw, lnb):
    n = r.shape[1]
    nrm = jnp.sqrt(jnp.sum(kkraw * kkraw, axis=-1, keepdims=True))
    kk = kkraw / jnp.maximum(nrm, 1e-12)
    av, bv = -kk, kk * a
    ti = lax.broadcasted_iota(jnp.int32, (n, n), 0)
    si = lax.broadcasted_iota(jnp.int32, (n, n), 1)
    incl, strict = ti >= si, ti > si
    ones = jnp.broadcast_to(incl.astype(F32)[None], (r.shape[0], n, n))
    cum = _hdot(ones, lw, 2, 1)
    at, rt = av * jnp.exp(cum - lw), r * jnp.exp(cum)
    inv = jnp.exp(-cum)
    bt, kt = bv * inv, k * inv
    lab = jnp.where(strict, _hdot(at, bt, 2, 2), 0.0)
    lak = jnp.where(strict, _hdot(at, kt, 2, 2), 0.0)
    rb = jnp.where(incl, _hdot(rt, bt, 2, 2), 0.0)
    rkm = jnp.where(incl, _hdot(rt, kt, 2, 2), 0.0)
    u = _bmm_nn(at, h0) + _bmm_nn(lak, v)
    p = lab
    m = 1
    while m < n:
        u = u + _bmm_nn(p, u)
        m *= 2
        if m < n:
            p = _bmm_nn(p, p)
    y = _bmm_nn(rt, h0) + _bmm_nn(rb, u) + _bmm_nn(rkm, v)
    last = jnp.exp(jnp.sum(lw, axis=1, keepdims=True))
    h1 = jnp.swapaxes(last, 1, 2) * (h0 + _bmm_tn(bt, u) + _bmm_tn(kt, v))
    mean = jnp.mean(y, axis=-1, keepdims=True)
    yc = y - mean
    var = jnp.mean(yc * yc, axis=-1, keepdims=True)
    yn = yc * lax.rsqrt(var + GN_EPS) * lnw + lnb
    bonus = jnp.sum(r * k * rk, axis=-1, keepdims=True) * v
    return yn + bonus, h1


SCAN_GROUP = 2


def _scan_group_fn(h0, r, lw, k, v, kkraw, a, rk, lnw, lnb):
    outs = []
    for j in range(SCAN_GROUP):
        rows = slice(j * SCAN_CHUNK, (j + 1) * SCAN_CHUNK)
        o, h0 = _scan_chunk_fn(h0, r[:, rows], lw[:, rows], k[:, rows], v[:, rows], kkraw[:, rows], a[:, rows], rk, lnw, lnb)
        outs.append(o)
    return jnp.concatenate(outs, axis=1), h0


def _scan_specs(h, t, dh, rev):
    n = SCAN_CHUNK * SCAN_GROUP
    nc = t // n
    pos = (lambda c: (0, nc - 1 - c, 0)) if rev else (lambda c: (0, c, 0))
    st = (lambda c: (nc - 1 - c, 0, 0, 0)) if rev else (lambda c: (c, 0, 0, 0))
    seq = pl.BlockSpec((h, n, dh), pos)
    par = pl.BlockSpec((h, 1, dh), lambda c: (0, 0, 0))
    state = pl.BlockSpec((1, h, dh, dh), st)
    return seq, par, state


def _scan_fwd(seqs, pars):
    h, t, dh = seqs[0].shape
    nc = t // (SCAN_CHUNK * SCAN_GROUP)
    seq, par, state = _scan_specs(h, t, dh, False)

    def body(r, lw, k, v, kkraw, a, rk, lnw, lnb, o_ref, st_ref, h_ref):
        @pl.when(pl.program_id(0) == 0)
        def _():
            h_ref[...] = jnp.zeros_like(h_ref)

        h0 = h_ref[...]
        st_ref[0] = h0
        o, h1 = _scan_group_fn(h0, r[...], lw[...], k[...], v[...], kkraw[...], a[...], rk[...], lnw[...], lnb[...])
        o_ref[...] = o
        h_ref[...] = h1

    return pl.pallas_call(
        body, name="rwkv_scan_fwd", grid=(nc,),
        out_shape=(jax.ShapeDtypeStruct((h, t, dh), F32), jax.ShapeDtypeStruct((nc, h, dh, dh), F32)),
        in_specs=[seq] * 6 + [par] * 3, out_specs=(seq, state),
        scratch_shapes=[pltpu.VMEM((h, dh, dh), F32)], compiler_params=_params("arbitrary"),
    )(*seqs, *pars)


def _scan_bwd(seqs, pars, states, do):
    h, t, dh = seqs[0].shape
    nc = t // (SCAN_CHUNK * SCAN_GROUP)
    seq, par, state = _scan_specs(h, t, dh, True)

    def body(r, lw, k, v, kkraw, a, rk, lnw, lnb, st_ref, do_ref, *rest):
        douts, dpars, dh_ref = rest[:6], rest[6:9], rest[9]
        first = pl.program_id(0) == 0

        @pl.when(first)
        def _():
            dh_ref[...] = jnp.zeros_like(dh_ref)

        _, vjp = jax.vjp(_scan_group_fn, st_ref[0], r[...], lw[...], k[...], v[...], kkraw[...], a[...],
                         rk[...], lnw[...], lnb[...])
        grads = vjp((do_ref[...], dh_ref[...]))
        dh_ref[...] = grads[0]
        for ref, val in zip(douts, grads[1:7]):
            ref[...] = val

        @pl.when(first)
        def _():
            for ref, val in zip(dpars, grads[7:]):
                ref[...] = val

        @pl.when(jnp.logical_not(first))
        def _():
            for ref, val in zip(dpars, grads[7:]):
                ref[...] += val

    sshape = jax.ShapeDtypeStruct((h, t, dh), F32)
    pshape = jax.ShapeDtypeStruct((h, 1, dh), F32)
    return pl.pallas_call(
        body, name="rwkv_scan_bwd", grid=(nc,), out_shape=(sshape,) * 6 + (pshape,) * 3,
        in_specs=[seq] * 6 + [par] * 3 + [state, seq], out_specs=(seq,) * 6 + (par,) * 3,
        scratch_shapes=[pltpu.VMEM((h, dh, dh), F32)], compiler_params=_params("arbitrary"),
    )(*seqs, *pars, states, do)


def _local_step(x, target, w, ex):
    w = dict(w)
    c = w["mu_r"].shape[-1]
    qn, kn = w["q_norm"].reshape(1, 1, HEAD_DIM), w["k_norm"].reshape(1, 1, HEAD_DIM)
    vec = jnp.concatenate([w[n].reshape(1, c) for n in RWKV_VEC], axis=0)
    pars = [w[n].reshape(-1, 1, HEAD_DIM) for n in ("r_k", "ln_x_w", "ln_x_b")]
    no_dep = jnp.zeros(DEP_SHAPE, F32)

    x1, gate1, up1 = _ffn_fwd(x, w["ffn1_norm"], w["ffn1_w_gate"], w["ffn1_w_up"], w["ffn1_w_down"], ex.first_dep, "ffn1_fwd")
    w.update(ex.mix_weights((x1,)))
    mats = [w[n] for n in RWKV_MAT]
    q, k, v, cur = _proj_fwd(x1, w["mix_norm"], w["w_in"])
    att, *saved = _att_fwd(q, k, v, qn, kn)
    pre = _rwkv_pre_fwd(cur, vec, mats)
    seqs, gate = pre[:6], pre[6]
    opg, states = _scan_fwd(seqs, pars)
    w.update(ex.out_weights((att, opg)))
    x2 = _mixout_fwd(x1, att, opg, gate, w["w_out"])
    x3, gate2, up2 = _ffn_fwd(x2, w["ffn2_norm"], w["ffn2_w_gate"], w["ffn2_w_up"], w["ffn2_w_down"], no_dep, "ffn2_fwd")
    dy, loss = _loss_head(x3, target)

    g = {}
    dx2, g["ffn2_norm"], g["ffn2_w_gate"], g["ffn2_w_up"], g["ffn2_w_down"] = _ffn_bwd(
        x2, w["ffn2_norm"], w["ffn2_w_gate"], w["ffn2_w_up"], w["ffn2_w_down"], gate2, up2, dy, no_dep, "ffn2_bwd")
    dep = ex.send_ffn2({n: g[n] for n in ("ffn2_w_gate", "ffn2_w_up", "ffn2_w_down")})
    datt, dopg, dgate, g["w_out"] = _mixout_bwd(att, opg, gate, w["w_out"], dx2, dep)
    dscan = _scan_bwd(seqs, pars, states, dopg)
    for n, d in zip(("r_k", "ln_x_w", "ln_x_b"), dscan[6:]):
        g[n] = d
    dcur, dvec, *dmats = _rwkv_pre_bwd(cur, vec, mats, dscan[:6], dgate)
    for n, d in zip(RWKV_MAT, dmats):
        g[n] = d
    for j, n in enumerate(RWKV_VEC):
        g[n] = dvec[j:j + 1]
    dq, dk, dv, g["q_norm"], g["k_norm"] = _att_bwd(q, k, v, qn, kn, saved, datt)
    dx1, g["mix_norm"], g["w_in"] = _proj_bwd(x1, w["mix_norm"], w["w_in"], dq, dk, dv, dcur, dx2)
    dep = ex.send_mix({n: g[n] for n in ("w_in", "w_out") + RWKV_MAT}, (dx1,))
    dx, g["ffn1_norm"], g["ffn1_w_gate"], g["ffn1_w_up"], g["ffn1_w_down"] = _ffn_bwd(
        x, w["ffn1_norm"], w["ffn1_w_gate"], w["ffn1_w_up"], w["ffn1_w_down"], gate1, up1, dx1, dep, "ffn1_bwd")
    return loss, dx, g


N_SHARDS = 4


def _place():
    return lax.axis_index("x"), lax.axis_index("y"), lax.axis_index("c")


def _chip_peers(x, y):
    return [(1 - x, y), (x, 1 - y), (1 - x, 1 - y)]


HBM = pl.BlockSpec(memory_space=pltpu.HBM)
SEM = pl.BlockSpec(memory_space=pltpu.SEMAPHORE)
DEP_SHAPE = (8, 128)


class _Views:
    to_sibling = False


class _GatherViews(_Views):
    @staticmethod
    def send(i, srcs, lands, k, at):
        return srcs[i], lands[i].at[at[3]]

    @staticmethod
    def landing(i, srcs, lands, k, at):
        return srcs[i], lands[i].at[2 * at[4] + at[5]]


class _ScatterViews(_Views):
    @staticmethod
    def send(i, srcs, lands, k, at):
        return srcs[i].at[2 * at[4] + at[5]], lands[i].at[k]

    @staticmethod
    def landing(i, srcs, lands, k, at):
        return srcs[i].at[at[3]], lands[i].at[k]


def _half_rows(ref, slot, half):
    rows = ref.shape[1] // 2
    return ref.at[slot, pl.ds(pl.multiple_of(half * rows, BF16_SUBLANES), rows)]


class _HalfGatherViews(_Views):
    @staticmethod
    def send(i, srcs, lands, k, at):
        rows = srcs[i].shape[0] // 2
        return srcs[i].at[pl.ds(pl.multiple_of(at[2] * rows, BF16_SUBLANES), rows)], _half_rows(lands[i], at[3], at[2])

    @staticmethod
    def landing(i, srcs, lands, k, at):
        rows = srcs[i].shape[0] // 2
        return srcs[i].at[pl.ds(pl.multiple_of(at[2] * rows, BF16_SUBLANES), rows)], _half_rows(lands[i], 2 * at[4] + at[5], at[2])


class _ForwardViews(_Views):
    to_sibling = True

    @staticmethod
    def send(i, srcs, lands, k, at):
        mine = _half_rows(lands[i], 2 * at[4] + at[5], at[2])
        return mine, mine

    @staticmethod
    def landing(i, srcs, lands, k, at):
        theirs = _half_rows(lands[i], 2 * at[4] + at[5], 1 - at[2])
        return theirs, theirs


def _push_start(srcs, lands, views, after, name):
    ns, nl = len(srcs), len(lands)

    def body(*refs):
        src_refs, land_refs = refs[:ns], refs[ns:ns + nl]
        send_sems, recv_sems = refs[ns + nl + 1:ns + nl + 3]
        token = refs[2 * (ns + nl) + 3]
        x, y, c = _place()
        for i in range(nl):
            for k, (px, py) in enumerate(_chip_peers(x, y)):
                src, dst = views.send(i, src_refs, land_refs, k, (x, y, c, 2 * x + y, px, py))
                pltpu.make_async_remote_copy(
                    src_ref=src, dst_ref=dst, send_sem=send_sems.at[3 * i + k], recv_sem=recv_sems.at[3 * i + k],
                    device_id=(x, y, 1 - c) if views.to_sibling else (px, py, c), device_id_type=MESH).start()
        token[...] = jnp.zeros_like(token)

    sems = pltpu.SemaphoreType.DMA((3 * nl,))
    both = [pltpu.with_memory_space_constraint(a, pltpu.HBM) for a in (*srcs, *lands)]
    outs = pl.pallas_call(
        body, name=name,
        out_shape=(sems, sems, *[pltpu.HBM(a.shape, a.dtype) for a in both], jax.ShapeDtypeStruct(DEP_SHAPE, F32)),
        in_specs=[HBM] * (ns + nl) + [ANY], out_specs=(SEM, SEM, *[HBM] * (ns + nl), VMEM_FULL),
        input_output_aliases={i: 2 + i for i in range(ns + nl)},
        compiler_params=pltpu.CompilerParams(has_side_effects=pltpu.SideEffectType.DATAFLOW_SIDE_EFFECTING),
    )(*both, after)
    return outs[0], outs[1], outs[2:2 + ns], outs[2 + ns:2 + ns + nl], outs[2 + ns + nl]


def _push_wait(started, views, after, name, with_sources=False):
    send_sems, recv_sems, srcs, lands, _ = started
    ns, nl = len(srcs), len(lands)

    def body(*refs):
        src_refs, land_refs = refs[:ns], refs[ns:ns + nl]
        send_sems, recv_sems = refs[ns + nl:ns + nl + 2]
        x, y, c = _place()
        for i in range(nl):
            for k, (px, py) in enumerate(_chip_peers(x, y)):
                src, dst = views.landing(i, src_refs, land_refs, k, (x, y, c, 2 * x + y, px, py))
                landing = pltpu.make_async_remote_copy(
                    src_ref=src, dst_ref=dst, send_sem=send_sems.at[3 * i + k], recv_sem=recv_sems.at[3 * i + k],
                    device_id=(x, y, 1 - c) if views.to_sibling else (px, py, c), device_id_type=MESH)
                landing.wait_send()
                landing.wait_recv()

    outs = pl.pallas_call(
        body, name=name,
        out_shape=tuple(pltpu.HBM(a.shape, a.dtype) for a in (*srcs, *lands)),
        in_specs=[HBM] * (ns + nl) + [SEM, SEM] + [ANY] * len(after), out_specs=(HBM,) * (ns + nl),
        input_output_aliases={i: i for i in range(ns + nl)},
        compiler_params=pltpu.CompilerParams(has_side_effects=pltpu.SideEffectType.DATAFLOW_SIDE_EFFECTING),
    )(*srcs, *lands, send_sems, recv_sems, *after)
    return outs if with_sources else outs[ns:]


def _empty_lands(shards, slots, own_slot):
    lands = [lax.empty((slots,) + s.shape, s.dtype) for s in shards]
    if own_slot:
        me = 2 * lax.axis_index("x") + lax.axis_index("y")
        lands = [lax.dynamic_update_index_in_dim(z, s, me, 0) for z, s in zip(lands, shards)]
    return lands


def _sibling_swap(arrays, name):
    n = len(arrays)

    def body(*refs):
        ins, outs = refs[:n], refs[n:2 * n]
        send_sems, recv_sems = refs[2 * n:]
        x, y, c = _place()
        copies = []
        for i in range(n):
            cp = pltpu.make_async_remote_copy(
                src_ref=ins[i], dst_ref=outs[i], send_sem=send_sems.at[i], recv_sem=recv_sems.at[i],
                device_id=(x, y, 1 - c), device_id_type=MESH)
            cp.start()
            copies.append(cp)
        for cp in copies:
            cp.wait()

    return pl.pallas_call(
        body, name=name,
        out_shape=tuple(jax.ShapeDtypeStruct(a.shape, a.dtype) for a in arrays),
        in_specs=[ANY] * n, out_specs=(ANY,) * n,
        scratch_shapes=[pltpu.SemaphoreType.DMA((n,)), pltpu.SemaphoreType.DMA((n,))],
    )(*arrays)


N_DEV = 8


def _allreduce_small(pack):
    def body(in_ref, out_ref, buf, send_sems, recv_sems):
        x, y, c = _place()
        me = 4 * x + 2 * y + c
        buf[me] = in_ref[...]

        def copy(j, slot):
            px, py, pc = x ^ (j >> 2), y ^ ((j >> 1) & 1), c ^ (j & 1)
            return pltpu.make_async_remote_copy(
                src_ref=in_ref, dst_ref=buf.at[slot(px, py, pc)], send_sem=send_sems.at[j], recv_sem=recv_sems.at[j],
                device_id=(px, py, pc), device_id_type=MESH)

        for j in range(1, N_DEV):
            copy(j, lambda px, py, pc: me).start()
        for j in range(1, N_DEV):
            landing = copy(j, lambda px, py, pc: 4 * px + 2 * py + pc)
            landing.wait_send()
            landing.wait_recv()
        acc = buf[0]
        for s in range(1, N_DEV):
            acc = acc + buf[s]
        out_ref[...] = acc

    return pl.pallas_call(
        body, name="allreduce_small", out_shape=jax.ShapeDtypeStruct(pack.shape, F32),
        in_specs=[VMEM_FULL], out_specs=VMEM_FULL,
        scratch_shapes=[pltpu.VMEM((N_DEV,) + pack.shape, F32), pltpu.SemaphoreType.DMA((N_DEV,)),
                        pltpu.SemaphoreType.DMA((N_DEV,))],
    )(pack)


ROW_TILE_MAX = 256
BF16_SUBLANES = 16


def _row_tile(rows):
    for tr in range(min(rows, ROW_TILE_MAX), 0, -1):
        if rows % tr == 0 and tr % BF16_SUBLANES == 0:
            return tr
    return rows


def _reduce_own(me, part, recv, dep, name):
    _, r, cols = part.shape
    tr = _row_tile(r)

    def body(me_ref, p_ref, rv_ref, dep_ref, o_ref):
        acc = p_ref[0].astype(F32)
        for k in range(3):
            acc = acc + rv_ref[k].astype(F32)
        o_ref[...] = acc

    return pl.pallas_call(
        body, name=name, out_shape=jax.ShapeDtypeStruct((r, cols), F32),
        grid_spec=pltpu.PrefetchScalarGridSpec(
            num_scalar_prefetch=1, grid=(r // tr,),
            in_specs=[pl.BlockSpec((1, tr, cols), lambda i, me_ref: (me_ref[0], i, 0)),
                      pl.BlockSpec((3, tr, cols), lambda i, me_ref: (0, i, 0)), ANY],
            out_specs=pl.BlockSpec((tr, cols), lambda i, me_ref: (i, 0))),
        compiler_params=_params("arbitrary"),
    )(me, part, recv, dep)


def _adamw(w, ga, gb, m, v, name):
    r, cols = w.shape
    tr = _row_tile(r)
    c1 = 1.0 - ADAM_B1 ** ADAM_STEP
    c2 = 1.0 - ADAM_B2 ** ADAM_STEP

    def body(w_ref, ga_ref, gb_ref, m_ref, v_ref, g_out, d_out, m_out, v_out):
        g = ga_ref[...] + gb_ref[...]
        mn = ADAM_B1 * m_ref[...] + (1.0 - ADAM_B1) * g
        vn = ADAM_B2 * v_ref[...] + (1.0 - ADAM_B2) * (g * g)
        g_out[...] = g
        m_out[...] = mn
        v_out[...] = vn
        d_out[...] = -ADAM_LR * ((mn / c1) / (jnp.sqrt(vn / c2) + ADAM_EPS) + ADAM_WD * w_ref[...])

    tile = pl.BlockSpec((tr, cols), lambda i: (i, 0))
    shape = jax.ShapeDtypeStruct((r, cols), F32)
    return pl.pallas_call(
        body, name=name, grid=(r // tr,), out_shape=(shape,) * 4, in_specs=[tile] * 5, out_specs=(tile,) * 4,
        compiler_params=_params("arbitrary"),
    )(w, ga, gb, m, v)


PACK_COLS = 512


def _to_rows(a):
    flat = a.reshape(-1)
    pad = (-flat.shape[0]) % PACK_COLS
    return jnp.pad(flat, (0, pad)).reshape(-1, PACK_COLS)


def _pack(arrays, extra_rows=0):
    rows = [_to_rows(a) for a in arrays]
    n = sum(r.shape[0] for r in rows) + extra_rows
    pad = (-n) % 8
    return jnp.concatenate(rows + [jnp.zeros((extra_rows + pad, PACK_COLS), F32)], axis=0)


def _unpack(pack, like):
    out, at = [], 0
    for a in like:
        n = -(-a.size // PACK_COLS)
        out.append(pack[at:at + n].reshape(-1)[:a.size].reshape(a.shape))
        at += n
    return out


COL_SHARDED = ("ffn1_w_gate", "ffn1_w_up", "w_in", "ffn2_w_gate", "ffn2_w_up", "w2", "a2", "g2")
ROW_SHARDED = ("ffn1_w_down", "ffn2_w_down", "w_out", "w1", "a1", "g1")
CHUNKED = ("ffn1_w_gate", "ffn1_w_up", "ffn1_w_down", "ffn2_w_gate", "ffn2_w_up", "ffn2_w_down")
WEIGHTS = ("ffn1_norm", "ffn1_w_gate", "ffn1_w_up", "ffn1_w_down", "mix_norm", "w_in", "q_norm", "k_norm",
           "mu_r", "mu_k", "mu_v", "mu_w", "mu_a", "mu_g", "w0", "w1", "w2", "a0", "a1", "a2", "g1", "g2",
           "k_k", "k_a", "r_k", "ln_x_w", "ln_x_b", "w_out", "ffn2_norm", "ffn2_w_gate", "ffn2_w_up", "ffn2_w_down")


W_IN_GROUPS = 7
TRANSPOSED = ("ffn1_w_gate", "ffn1_w_up", "ffn2_w_gate", "ffn2_w_up")


def _shard_2d(name, a):
    return a[0].T if name in TRANSPOSED else a[0]


def _full_from_blocks(name, blocks):
    if name in CHUNKED:
        return blocks
    if name in ROW_SHARDED:
        return blocks.reshape(-1, blocks.shape[-1])
    full = blocks.transpose(1, 0, 2).reshape(blocks.shape[1], -1)
    if name == "w_in":
        return full.reshape(full.shape[0], W_IN_GROUPS, -1).transpose(1, 0, 2)
    return full


def _blocks_from_full(name, full):
    if name in CHUNKED:
        return full
    if name in ROW_SHARDED:
        return full.reshape(N_SHARDS, -1, full.shape[-1])
    if name == "w_in":
        full = full.transpose(1, 0, 2).reshape(full.shape[1], -1)
    return full.reshape(full.shape[0], N_SHARDS, -1).transpose(1, 0, 2)


FFN1_GROUP = ("ffn1_w_gate", "ffn1_w_up", "ffn1_w_down")
MIX_GROUP = ("w_in",) + RWKV_MAT
OUT_GROUP = ("w_out", "ffn2_w_gate", "ffn2_w_up", "ffn2_w_down")
FFN2_GROUP = OUT_GROUP[1:]
LATE_GROUP = ("w_in", "w_out") + RWKV_MAT


class _Exchange:
    def __init__(self, given):
        self.given = given
        first = self._gather_start(FFN1_GROUP, _HalfGatherViews, jnp.zeros(DEP_SHAPE, F32), "gather_ffn1_start")
        self.mix = self._gather_start(MIX_GROUP, _GatherViews, first[4], "gather_mix_start")
        self.out = self._gather_start(OUT_GROUP, _GatherViews, self.mix[4], "gather_out_start")
        self.first_dep = self.out[4]
        halves = _push_wait(first, _HalfGatherViews, (self.first_dep,), "gather_ffn1_wait")
        passed = _push_start([], halves, _ForwardViews, halves[0], "gather_ffn1_pass_start")
        self.first_weights = self._full(FFN1_GROUP, _push_wait(passed, _ForwardViews, (passed[4],), "gather_ffn1_pass_wait"))
        self.parts, self.recv = {}, {}

    def _shards(self, names):
        return [_shard_2d(n, self.given[n]).astype(BF16) for n in names]

    @staticmethod
    def _full(names, blocks):
        out = {}
        for n, b in zip(names, blocks):
            full = _full_from_blocks(n, b)
            out[n] = full.astype(F32) if n in RWKV_MAT else full
        return out

    def _gather_start(self, names, views, after, name):
        shards = self._shards(names)
        return _push_start(shards, _empty_lands(shards, N_SHARDS, True), views, after, name)

    def mix_weights(self, after):
        return self._full(MIX_GROUP, _push_wait(self.mix, _GatherViews, after, "gather_mix_wait"))

    def out_weights(self, after):
        return self._full(OUT_GROUP, _push_wait(self.out, _GatherViews, after, "gather_out_wait"))

    def _scatter_start(self, grads, name):
        names = tuple(grads)
        parts = [_blocks_from_full(n, grads[n]) for n in names]
        self.parts.update(zip(names, parts))
        lands = [lax.empty((3,) + p.shape[1:], BF16) for p in parts]
        return _push_start([p.astype(BF16) for p in parts], lands, _ScatterViews, jnp.zeros(DEP_SHAPE, F32), name)

    def _scatter_done(self, started, names, after, name):
        outs = _push_wait(started, _ScatterViews, after, name, with_sources=True)
        for n, sent, got in zip(names, outs[:len(names)], outs[len(names):]):
            self.recv[n] = got
            if self.parts[n].dtype == BF16:
                self.parts[n] = sent

    def send_ffn2(self, grads):
        self.ffn2 = self._scatter_start(grads, "scatter_ffn2_start")
        return self.ffn2[4]

    def send_mix(self, grads, after):
        self._scatter_done(self.ffn2, FFN2_GROUP, after, "scatter_ffn2_wait")
        self.late = self._scatter_start(grads, "scatter_late_start")
        return self.late[4]

    def send_ffn1(self, grads):
        self.ffn1 = self._scatter_start(grads, "scatter_ffn1_start")
        return self.ffn1[4]

    def late_received(self, after):
        self._scatter_done(self.late, LATE_GROUP, after, "scatter_late_wait")

    def ffn1_received(self, after):
        self._scatter_done(self.ffn1, FFN1_GROUP, after, "scatter_ffn1_wait")


def kernel(
        x, ffn1_norm, ffn1_w_gate, ffn1_w_up, ffn1_w_down, mix_norm, w_in, q_norm, k_norm, mu_r, mu_k, mu_v, mu_w,
        mu_a, mu_g, w0, w1, w2, a0, a1, a2, g1, g2, k_k, k_a, r_k, ln_x_w, ln_x_b, w_out, ffn2_norm, ffn2_w_gate,
        ffn2_w_up, ffn2_w_down, loss_target, m_ffn1_norm, m_ffn1_w_gate, m_ffn1_w_up, m_ffn1_w_down, m_mix_norm,
        m_w_in, m_q_norm, m_k_norm, m_mu_r, m_mu_k, m_mu_v, m_mu_w, m_mu_a, m_mu_g, m_w0, m_w1, m_w2, m_a0, m_a1,
        m_a2, m_g1, m_g2, m_k_k, m_k_a, m_r_k, m_ln_x_w, m_ln_x_b, m_w_out, m_ffn2_norm, m_ffn2_w_gate, m_ffn2_w_up,
        m_ffn2_w_down, v_ffn1_norm, v_ffn1_w_gate, v_ffn1_w_up, v_ffn1_w_down, v_mix_norm, v_w_in, v_q_norm, v_k_norm,
        v_mu_r, v_mu_k, v_mu_v, v_mu_w, v_mu_a, v_mu_g, v_w0, v_w1, v_w2, v_a0, v_a1, v_a2, v_g1, v_g2, v_k_k, v_k_a,
        v_r_k, v_ln_x_w, v_ln_x_b, v_w_out, v_ffn2_norm, v_ffn2_w_gate, v_ffn2_w_up, v_ffn2_w_down):
    given = dict(locals())
    sharded = COL_SHARDED + ROW_SHARDED
    sharded = tuple(n for n in WEIGHTS if n in sharded)
    small = tuple(n for n in WEIGHTS if n not in sharded)

    ex = _Exchange(given)
    w = {n: given[n] for n in small}
    w.update(ex.first_weights)
    loss, dx, g = _local_step(x[0], loss_target[0], w, ex)
    dep = ex.send_ffn1({n: g[n] for n in FFN1_GROUP})

    me = (2 * lax.axis_index("x") + lax.axis_index("y")).astype(jnp.int32).reshape(1)
    out = {}

    def settle(names, dep, tag):
        mine = []
        for n in names:
            p, rv = ex.parts[n], ex.recv[n]
            p2 = p.reshape(N_SHARDS, -1, p.shape[-1])
            mine.append(_reduce_own(me, p2, rv.reshape(3, -1, rv.shape[-1]), dep, f"reduce_{n}"))
        theirs = _sibling_swap(mine, f"sibling_swap_{tag}")
        for n, a, b in zip(names, mine, theirs):
            shape = given[n].shape
            res = _adamw(_shard_2d(n, given[n]), a, b, _shard_2d(n, given["m_" + n]), _shard_2d(n, given["v_" + n]), f"adamw_{n}")
            out[n] = [(r.T if n in TRANSPOSED else r).reshape(shape) for r in res]
        return tuple(out[n][1] for n in names)

    ex.late_received((dep,))
    last = settle(tuple(n for n in sharded if n not in FFN1_GROUP), dep, "rest")

    gpack = _pack([g[n] for n in small], extra_rows=1)
    n_rows = sum(-(-given[n].size // PACK_COLS) for n in small)
    gpack = gpack.at[n_rows, :loss.shape[1]].set(loss[0])
    gsum = _allreduce_small(gpack)
    res = _adamw(_pack([given[n] for n in small], 1), gsum, jnp.zeros_like(gsum), _pack([given["m_" + n] for n in small], 1),
                 _pack([given["v_" + n] for n in small], 1), "adamw_small")
    like = [given[n] for n in small]
    for j, r in enumerate(res):
        for n, a in zip(small, _unpack(r, like)):
            out.setdefault(n, [None] * 4)[j] = a
    total_loss = gsum[n_rows, 0]

    ex.ffn1_received((*last, res[1]))
    settle(FFN1_GROUP, jnp.zeros(DEP_SHAPE, F32), "ffn1")
    return (total_loss, dx[None], *[out[n][0] for n in WEIGHTS], *[out[n][1] for n in WEIGHTS],
            *[out[n][2] for n in WEIGHTS], *[out[n][3] for n in WEIGHTS])
```

```python
import functools

import jax
import jax.numpy as jnp
from jax import lax
from jax.experimental import pallas as pl
from jax.experimental.pallas import tpu as pltpu

F32 = jnp.float32
BF16 = jnp.bfloat16
MESH = pl.DeviceIdType.MESH

RMS_EPS = 1e-6
GN_EPS = 64e-5
NEG_INF = -1e30
FFN_RESIDUAL = 0.5
HEAD_DIM = 64
ATT_BLOCK = 128
DILATIONS = (1, 4, 16)
SCAN_CHUNK = 64
TOKEN_TILE = 256
FFN_BWD_TILE = 512

ADAM_LR = 0.001
ADAM_B1 = 0.9
ADAM_B2 = 0.999
ADAM_EPS = 1e-08
ADAM_WD = 0.01
ADAM_STEP = 10

VMEM_FULL = pl.BlockSpec(memory_space=pltpu.VMEM)
ANY = pl.BlockSpec(memory_space=pl.ANY)


VMEM_LIMIT = 56 * 1024 * 1024


def _params(*sem):
    return pltpu.CompilerParams(dimension_semantics=sem, vmem_limit_bytes=VMEM_LIMIT)


def _dot(a, b, dims):
    return lax.dot_general(a.astype(BF16), b.astype(BF16), (dims, ((), ())), preferred_element_type=F32)


def _dot_nn(a, b):
    return _dot(a, b, ((1,), (0,)))


def _dot_nt(a, b):
    return _dot(a, b, ((1,), (1,)))


def _dot_tn(a, b):
    return _dot(a, b, ((0,), (0,)))


@jax.custom_vjp
def _mm(a, b):
    return _dot_nn(a, b)


def _mm_fwd(a, b):
    return _dot_nn(a, b), (a, b)


def _mm_bwd(res, g):
    a, b = res
    return _dot_nt(g, b).astype(a.dtype), _dot_tn(a, g).astype(b.dtype)


_mm.defvjp(_mm_fwd, _mm_bwd)


def _bdot(a, b, ca, cb):
    return lax.dot_general(a.astype(BF16), b.astype(BF16), (((ca,), (cb,)), ((0,), (0,))), preferred_element_type=F32)


@jax.custom_vjp
def _bmm_nt(a, b):
    return _bdot(a, b, 2, 2)


def _bmm_nt_fwd(a, b):
    return _bdot(a, b, 2, 2), (a, b)


def _bmm_nt_bwd(res, g):
    a, b = res
    return _bdot(g, b, 2, 1), _bdot(g, a, 1, 1)


_bmm_nt.defvjp(_bmm_nt_fwd, _bmm_nt_bwd)


@jax.custom_vjp
def _bmm_nn(a, b):
    return _bdot(a, b, 2, 1)


def _bmm_nn_fwd(a, b):
    return _bdot(a, b, 2, 1), (a, b)


def _bmm_nn_bwd(res, g):
    a, b = res
    return _bdot(g, b, 2, 2), _bdot(a, g, 1, 1)


_bmm_nn.defvjp(_bmm_nn_fwd, _bmm_nn_bwd)


@jax.custom_vjp
def _bmm_tn(a, b):
    return _bdot(a, b, 1, 1)


def _bmm_tn_fwd(a, b):
    return _bdot(a, b, 1, 1), (a, b)


def _bmm_tn_bwd(res, g):
    a, b = res
    return _bdot(b, g, 2, 2), _bdot(a, g, 2, 1)


_bmm_tn.defvjp(_bmm_tn_fwd, _bmm_tn_bwd)


def _hdot(a, b, ca, cb):
    return lax.dot_general(a, b, (((ca,), (cb,)), ((0,), (0,))), precision=lax.Precision.HIGH, preferred_element_type=F32)


def _sigmoid(x):
    return 1.0 / (1.0 + jnp.exp(-x))


def _rms(x):
    return lax.rsqrt(jnp.mean(x * x, axis=-1, keepdims=True) + RMS_EPS)


def _ffn_fwd(x, norm, wg, wu, wd, dep, name):
    t, d = x.shape
    nc, fc, _ = wg.shape
    tm = TOKEN_TILE

    def body(x_ref, n_ref, wg_ref, wu_ref, wd_ref, dep_ref, o_ref, g_ref, u_ref):
        xv = x_ref[...]
        h = (xv * _rms(xv) * n_ref[...]).astype(BF16)
        acc = jnp.zeros((tm, d), F32)
        for c in range(nc):
            g = _dot_nt(h, wg_ref[c])
            u = _dot_nt(h, wu_ref[c])
            g_ref[c] = g.astype(BF16)
            u_ref[c] = u.astype(BF16)
            a = (g * _sigmoid(g) * u).astype(BF16)
            acc = acc + jnp.dot(a, wd_ref[c], preferred_element_type=F32)
        o_ref[...] = xv + FFN_RESIDUAL * acc

    tile = pl.BlockSpec((tm, d), lambda i: (i, 0))
    hidden = pl.BlockSpec((nc, tm, fc), lambda i: (0, i, 0))
    hshape = jax.ShapeDtypeStruct((nc, t, fc), BF16)
    return pl.pallas_call(
        body, name=name, grid=(t // tm,), out_shape=(jax.ShapeDtypeStruct((t, d), F32), hshape, hshape),
        in_specs=[tile, pl.BlockSpec((1, d), lambda i: (0, 0)), VMEM_FULL, VMEM_FULL, VMEM_FULL, ANY],
        out_specs=(tile, hidden, hidden), compiler_params=_params("arbitrary"),
    )(x, norm, wg, wu, wd, dep)


def _rmsnorm_bwd(xv, gain, dh):
    rs = _rms(xv)
    xn = xv * rs
    dxn = dh * gain
    dx = rs * (dxn - xn * jnp.mean(dxn * xn, axis=-1, keepdims=True))
    return dx, jnp.sum(dh * xn, axis=0, keepdims=True)


def _ffn_bwd(x, norm, wg, wu, wd, gate, up, dy, dep, name):
    t, d = x.shape
    nc, fc, _ = wg.shape
    tm = FFN_BWD_TILE
    nt = t // tm

    def body(x_ref, n_ref, wg_ref, wu_ref, wd_ref, g_ref, u_ref, dy_ref, dep_ref, dx_ref, dn_ref, dwg_ref, dwu_ref,
             dwd_ref, dh_ref, ag_ref, au_ref, ad_ref):
        c, i = pl.program_id(0), pl.program_id(1)
        rows = pl.ds(pl.multiple_of(i * tm, tm), tm)
        xv = x_ref[...]
        gain = n_ref[...]
        h = (xv * _rms(xv) * gain).astype(BF16)
        dy = dy_ref[...]
        dyb = (FFN_RESIDUAL * dy).astype(BF16)
        g = g_ref[0].astype(F32)
        u = u_ref[0].astype(F32)
        sg = _sigmoid(g)
        s = g * sg
        a = (s * u).astype(BF16)
        da = _dot_nt(dyb, wd_ref[0])
        dub = (da * s).astype(BF16)
        dgb = (da * u * (sg * (1.0 + g * (1.0 - sg)))).astype(BF16)
        dwd_c = _dot_tn(a, dyb)
        dwg_c = _dot_tn(dgb, h)
        dwu_c = _dot_tn(dub, h)
        dh_c = _dot_nn(dgb, wg_ref[0]) + _dot_nn(dub, wu_ref[0])

        @pl.when(i == 0)
        def _():
            ad_ref[...] = dwd_c
            ag_ref[...] = dwg_c
            au_ref[...] = dwu_c

        @pl.when(i > 0)
        def _():
            ad_ref[...] += dwd_c
            ag_ref[...] += dwg_c
            au_ref[...] += dwu_c

        @pl.when(i == nt - 1)
        def _():
            dwd_ref[0] = ad_ref[...].astype(BF16)
            dwg_ref[0] = ag_ref[...].astype(BF16)
            dwu_ref[0] = au_ref[...].astype(BF16)

        @pl.when(c == 0)
        def _():
            dh_ref[rows, :] = dh_c

        @pl.when(c > 0)
        def _():
            dh_ref[rows, :] += dh_c

        @pl.when(c == nc - 1)
        def _():
            dx, dn = _rmsnorm_bwd(xv, gain, dh_ref[rows, :])
            dx_ref[...] = dx + dy

            @pl.when(i == 0)
            def _():
                dn_ref[...] = dn

            @pl.when(i > 0)
            def _():
                dn_ref[...] += dn

    tile = pl.BlockSpec((tm, d), lambda c, i: (i, 0))
    row = pl.BlockSpec((1, d), lambda c, i: (0, 0))
    wrow = pl.BlockSpec((1, fc, d), lambda c, i: (c, 0, 0), pipeline_mode=pl.Buffered(1))
    hidden = pl.BlockSpec((1, tm, fc), lambda c, i: (c, i, 0))
    last = pl.BlockSpec((tm, d), lambda c, i: (jnp.where(c == nc - 1, i, 0), 0))
    return pl.pallas_call(
        body, name=name, grid=(nc, nt),
        out_shape=(jax.ShapeDtypeStruct((t, d), F32), jax.ShapeDtypeStruct((1, d), F32),
                   jax.ShapeDtypeStruct(wg.shape, BF16), jax.ShapeDtypeStruct(wu.shape, BF16),
                   jax.ShapeDtypeStruct(wd.shape, BF16)),
        in_specs=[tile, row, wrow, wrow, wrow, hidden, hidden, tile, ANY],
        out_specs=(last, row, wrow, wrow, wrow),
        scratch_shapes=[pltpu.VMEM((t, d), F32)] + [pltpu.VMEM((fc, d), F32)] * 3,
        compiler_params=_params("arbitrary", "arbitrary"),
    )(x, norm, wg, wu, wd, gate, up, dy, dep)


def _store_heads(ref, v):
    for h in range(ref.shape[0]):
        ref[h] = v[:, h * HEAD_DIM:(h + 1) * HEAD_DIM]


def _load_heads(ref):
    return jnp.concatenate([ref[h] for h in range(ref.shape[0])], axis=-1)


N_HEAD_GROUPS = 3


def _proj_fwd(x, norm, w):
    t, d = x.shape
    ng, _, c = w.shape
    nh = c // HEAD_DIM
    tm = TOKEN_TILE

    def body(x_ref, n_ref, w_ref, q_ref, k_ref, v_ref, cur_ref):
        xv = x_ref[...]
        h = (xv * _rms(xv) * n_ref[...]).astype(BF16)
        for m, ref in enumerate((q_ref, k_ref, v_ref)):
            _store_heads(ref, jnp.dot(h, w_ref[m], preferred_element_type=F32))
        for m in range(N_HEAD_GROUPS, ng):
            j = m - N_HEAD_GROUPS
            cur_ref[:, j * c:(j + 1) * c] = jnp.dot(h, w_ref[m], preferred_element_type=F32)

    heads = pl.BlockSpec((nh, tm, HEAD_DIM), lambda i: (0, i, 0))
    hshape = jax.ShapeDtypeStruct((nh, t, HEAD_DIM), F32)
    wide = (ng - N_HEAD_GROUPS) * c
    return pl.pallas_call(
        body, name="proj_fwd", grid=(t // tm,),
        out_shape=(hshape, hshape, hshape, jax.ShapeDtypeStruct((t, wide), F32)),
        in_specs=[pl.BlockSpec((tm, d), lambda i: (i, 0)), pl.BlockSpec((1, d), lambda i: (0, 0)), VMEM_FULL],
        out_specs=(heads, heads, heads, pl.BlockSpec((tm, wide), lambda i: (i, 0))),
        compiler_params=_params("arbitrary"),
    )(x, norm, w)


def _proj_bwd(x, norm, w, dq, dk, dv, dcur, dres):
    t, d = x.shape
    ng, _, c = w.shape
    nh = c // HEAD_DIM
    tm = TOKEN_TILE

    def body(x_ref, n_ref, w_ref, dq_ref, dk_ref, dv_ref, dcur_ref, dres_ref, dx_ref, dn_ref, dw_ref):
        i = pl.program_id(0)

        @pl.when(i == 0)
        def _():
            dw_ref[...] = jnp.zeros_like(dw_ref)
            dn_ref[...] = jnp.zeros_like(dn_ref)

        xv = x_ref[...]
        gain = n_ref[...]
        h = (xv * _rms(xv) * gain).astype(BF16)
        dh = jnp.zeros((tm, d), F32)
        for m in range(ng):
            j = m - N_HEAD_GROUPS
            dp = _load_heads((dq_ref, dk_ref, dv_ref)[m]) if j < 0 else dcur_ref[:, j * c:(j + 1) * c]
            dp = dp.astype(BF16)
            dw_ref[m] += _dot_tn(h, dp)
            dh = dh + _dot_nt(dp, w_ref[m])
        dx, dn = _rmsnorm_bwd(xv, gain, dh)
        dx_ref[...] = dx + dres_ref[...]
        dn_ref[...] += dn

    tile = pl.BlockSpec((tm, d), lambda i: (i, 0))
    row = pl.BlockSpec((1, d), lambda i: (0, 0))
    heads = pl.BlockSpec((nh, tm, HEAD_DIM), lambda i: (0, i, 0))
    wide = (ng - N_HEAD_GROUPS) * c
    return pl.pallas_call(
        body, name="proj_bwd", grid=(t // tm,),
        out_shape=(jax.ShapeDtypeStruct((t, d), F32), jax.ShapeDtypeStruct((1, d), F32),
                   jax.ShapeDtypeStruct(w.shape, F32)),
        in_specs=[tile, row, VMEM_FULL, heads, heads, heads, pl.BlockSpec((tm, wide), lambda i: (i, 0)), tile],
        out_specs=(tile, row, VMEM_FULL),
        compiler_params=_params("arbitrary"),
    )(x, norm, w, dq, dk, dv, dcur, dres)


def _mixout_fwd(x, att, opg, gate, w):
    t, d = x.shape
    nh = att.shape[0]
    half = gate.shape[1]
    tm = TOKEN_TILE

    def body(x_ref, att_ref, opg_ref, g_ref, w_ref, o_ref):
        mix = jnp.concatenate([_load_heads(att_ref), _load_heads(opg_ref) * g_ref[...]], axis=-1).astype(BF16)
        o_ref[...] = x_ref[...] + jnp.dot(mix, w_ref[...], preferred_element_type=F32)

    tile = pl.BlockSpec((tm, d), lambda i: (i, 0))
    htile = pl.BlockSpec((tm, half), lambda i: (i, 0))
    heads = pl.BlockSpec((nh, tm, HEAD_DIM), lambda i: (0, i, 0))
    return pl.pallas_call(
        body, name="mixout_fwd", grid=(t // tm,), out_shape=jax.ShapeDtypeStruct((t, d), F32),
        in_specs=[tile, heads, heads, htile, VMEM_FULL], out_specs=tile, compiler_params=_params("arbitrary"),
    )(x, att, opg, gate, w)


def _mixout_bwd(att, opg, gate, w, dy, dep):
    nh, t, _ = att.shape
    half = gate.shape[1]
    d = dy.shape[1]
    tm = TOKEN_TILE

    def body(att_ref, opg_ref, g_ref, w_ref, dy_ref, dep_ref, datt_ref, dopg_ref, dg_ref, dw_ref):
        i = pl.program_id(0)
        opg_v, g_v = _load_heads(opg_ref), g_ref[...]
        mix = jnp.concatenate([_load_heads(att_ref), opg_v * g_v], axis=-1).astype(BF16)
        dyb = dy_ref[...].astype(BF16)
        dmix = _dot_nt(dyb, w_ref[...])
        dw = _dot_tn(mix, dyb)
        _store_heads(datt_ref, dmix[:, :half])
        drw = dmix[:, half:]
        _store_heads(dopg_ref, drw * g_v)
        dg_ref[...] = drw * opg_v

        @pl.when(i == 0)
        def _():
            dw_ref[...] = dw

        @pl.when(i > 0)
        def _():
            dw_ref[...] += dw

    tile = pl.BlockSpec((tm, d), lambda i: (i, 0))
    htile = pl.BlockSpec((tm, half), lambda i: (i, 0))
    heads = pl.BlockSpec((nh, tm, HEAD_DIM), lambda i: (0, i, 0))
    hshape = jax.ShapeDtypeStruct((nh, t, HEAD_DIM), F32)
    return pl.pallas_call(
        body, name="mixout_bwd", grid=(t // tm,),
        out_shape=(hshape, hshape, jax.ShapeDtypeStruct((t, half), F32), jax.ShapeDtypeStruct(w.shape, F32)),
        in_specs=[heads, heads, htile, VMEM_FULL, tile, ANY],
        out_specs=(heads, heads, htile, pl.BlockSpec(w.shape, lambda i: (0, 0))),
        compiler_params=_params("arbitrary"),
    )(att, opg, gate, w, dy, dep)


def _loss_head(y, target):
    t, d = y.shape
    tm = TOKEN_TILE

    def body(y_ref, t_ref, dy_ref, loss_ref):
        i = pl.program_id(0)
        err = y_ref[...] - t_ref[...]
        dy_ref[...] = err * (1.0 / d)
        part = 0.5 * jnp.sum(jnp.mean(err * err, axis=-1, keepdims=True), axis=0, keepdims=True)

        @pl.when(i == 0)
        def _():
            loss_ref[...] = jnp.zeros_like(loss_ref)

        loss_ref[...] += jnp.broadcast_to(part, loss_ref.shape)

    tile = pl.BlockSpec((tm, d), lambda i: (i, 0))
    return pl.pallas_call(
        body, name="loss_head", grid=(t // tm,),
        out_shape=(jax.ShapeDtypeStruct((t, d), F32), jax.ShapeDtypeStruct((1, 128), F32)),
        in_specs=[tile, tile], out_specs=(tile, pl.BlockSpec((1, 128), lambda i: (0, 0))),
        compiler_params=_params("arbitrary"),
    )(y, target)


def _att_pattern(q, k, v, qn, kn, nb):
    g, blk, _ = q.shape
    qh = q * _rms(q) * qn
    kh = k * _rms(k) * kn
    scale = HEAD_DIM ** -0.5
    qi = lax.broadcasted_iota(jnp.int32, (blk, blk), 0)
    kj = lax.broadcasted_iota(jnp.int32, (blk, blk), 1)
    sc = jnp.where(kj <= qi, _bmm_nt(qh, kh) * scale, NEG_INF)
    top = jnp.max(sc, axis=-1, keepdims=True)
    if nb > 1:
        khp = jnp.concatenate([kh[:1], kh[:-1]], axis=0)
        vp = jnp.concatenate([v[:1], v[:-1]], axis=0)
        has_prev = lax.broadcasted_iota(jnp.int32, (g, 1, 1), 0) % nb != 0
        sp = jnp.where((kj >= qi) & has_prev, _bmm_nt(qh, khp) * scale, NEG_INF)
        top = jnp.maximum(top, jnp.max(sp, axis=-1, keepdims=True))
    m = lax.stop_gradient(top)
    pc = jnp.exp(sc - m)
    den = jnp.sum(pc, axis=-1, keepdims=True)
    acc = _bmm_nn(pc, v)
    if nb > 1:
        pp = jnp.exp(sp - m)
        den = den + jnp.sum(pp, axis=-1, keepdims=True)
        acc = acc + _bmm_nn(pp, vp)
    o = acc / den
    return o, jnp.broadcast_to(m + jnp.log(den), o.shape)


def _pattern_rows(t, dil):
    nb = t // (ATT_BLOCK * dil)
    starts = [n * ATT_BLOCK * dil + r for r in range(dil) for n in range(nb)]
    return [pl.ds(s, ATT_BLOCK, stride=dil) if dil > 1 else pl.ds(s, ATT_BLOCK) for s in starts], nb


def _take(ref, rows):
    return jnp.stack([ref[0, r, :] for r in rows])


def _put(ref, rows, val):
    for g, r in enumerate(rows):
        ref[0, r, :] = val[g]


def _put_add(ref, rows, val):
    for g, r in enumerate(rows):
        ref[0, r, :] += val[g]


def _merge_fn(o1, o2, o3, l1, l2, l3):
    m = lax.stop_gradient(jnp.maximum(jnp.maximum(l1, l2), l3))
    e1, e2, e3 = jnp.exp(l1 - m), jnp.exp(l2 - m), jnp.exp(l3 - m)
    return (e1 * o1 + e2 * o2 + e3 * o3) / (e1 + e2 + e3)


def _att_head_specs(t):
    head = pl.BlockSpec((1, t, HEAD_DIM), lambda h: (h, 0, 0))
    gain = pl.BlockSpec((1, 1, HEAD_DIM), lambda h: (0, 0, 0))
    return head, gain


def _att_fwd(q, k, v, qn, kn):
    nh, t, dh = q.shape
    head, gain = _att_head_specs(t)

    def body(q_ref, k_ref, v_ref, qn_ref, kn_ref, att_ref, *saved):
        o_refs, l_refs = saved[:3], saved[3:]
        for p, dil in enumerate(DILATIONS):
            rows, nb = _pattern_rows(t, dil)
            o, lse = _att_pattern(_take(q_ref, rows), _take(k_ref, rows), _take(v_ref, rows), qn_ref[...], kn_ref[...], nb)
            _put(o_refs[p], rows, o)
            _put(l_refs[p], rows, lse)

        def merge(j, carry):
            rows = pl.ds(pl.multiple_of(j * ATT_BLOCK, ATT_BLOCK), ATT_BLOCK)
            att_ref[0, rows, :] = _merge_fn(*[r[0, rows, :] for r in saved])
            return carry

        lax.fori_loop(0, t // ATT_BLOCK, merge, 0)

    return pl.pallas_call(
        body, name="att_fwd", grid=(nh,), out_shape=(jax.ShapeDtypeStruct(q.shape, F32),) * 7,
        in_specs=[head, head, head, gain, gain], out_specs=(head,) * 7, compiler_params=_params("arbitrary"),
    )(q, k, v, qn, kn)


def _att_bwd(q, k, v, qn, kn, saved, datt):
    nh, t, dh = q.shape
    head, gain = _att_head_specs(t)

    def body(q_ref, k_ref, v_ref, qn_ref, kn_ref, o1, o2, o3, l1, l2, l3, datt_ref,
             dq_ref, dk_ref, dv_ref, dqn_ref, dkn_ref, *ct_refs):
        for ref in (dq_ref, dk_ref, dv_ref):
            ref[...] = jnp.zeros_like(ref)

        @pl.when(pl.program_id(0) == 0)
        def _():
            dqn_ref[...] = jnp.zeros_like(dqn_ref)
            dkn_ref[...] = jnp.zeros_like(dkn_ref)

        def merge_cotangents(j, carry):
            rows = pl.ds(pl.multiple_of(j * ATT_BLOCK, ATT_BLOCK), ATT_BLOCK)
            _, merge_vjp = jax.vjp(_merge_fn, *[r[0, rows, :] for r in (o1, o2, o3, l1, l2, l3)])
            for ref, val in zip(ct_refs, merge_vjp(datt_ref[0, rows, :])):
                ref[0, rows, :] = val
            return carry

        lax.fori_loop(0, t // ATT_BLOCK, merge_cotangents, 0)

        for p, dil in enumerate(DILATIONS):
            rows, nb = _pattern_rows(t, dil)
            cts = [_take(ct_refs[p], rows), _take(ct_refs[3 + p], rows)]
            _, pattern_vjp = jax.vjp(functools.partial(_att_pattern, nb=nb), _take(q_ref, rows), _take(k_ref, rows),
                                     _take(v_ref, rows), qn_ref[...], kn_ref[...])
            dq, dk, dv, dgq, dgk = pattern_vjp((cts[0], cts[1]))
            _put_add(dq_ref, rows, dq)
            _put_add(dk_ref, rows, dk)
            _put_add(dv_ref, rows, dv)
            dqn_ref[...] += dgq
            dkn_ref[...] += dgk

    hshape = jax.ShapeDtypeStruct(q.shape, F32)
    gshape = jax.ShapeDtypeStruct((1, 1, dh), F32)
    return pl.pallas_call(
        body, name="att_bwd", grid=(nh,), out_shape=(hshape, hshape, hshape, gshape, gshape),
        in_specs=[head, head, head, gain, gain] + [head] * 7, out_specs=(head, head, head, gain, gain),
        scratch_shapes=[pltpu.VMEM((1, t, dh), F32)] * 6, compiler_params=_params("arbitrary"),
    )(q, k, v, qn, kn, *saved, datt)


RWKV_VEC = ("mu_r", "mu_k", "mu_v", "mu_w", "mu_a", "mu_g", "w0", "a0", "k_k", "k_a")
RWKV_MAT = ("w1", "w2", "a1", "a2", "g1", "g2")


def _rwkv_pre_fn(cur, prev, vec, w1, w2, a1, a2, g1, g2):
    c = cur.shape[1] // 4
    mu_r, mu_k, mu_v, mu_w, mu_a, mu_g, w0, a0, k_k, k_a = (vec[j:j + 1] for j in range(10))

    def lerp(j, mu):
        xc, xp = cur[:, j * c:(j + 1) * c], prev[:, j * c:(j + 1) * c]
        return xc + (xp - xc) * mu

    r, k, v = lerp(0, mu_r), lerp(1, mu_k), lerp(2, mu_v)
    cw, ca, cg = lerp(3, mu_w), lerp(3, mu_a), lerp(3, mu_g)
    z = w0 + _mm(jnp.tanh(_mm(cw, w1)), w2)
    w_log = jnp.minimum(z, 0.0) - jnp.log(1.0 + jnp.exp(-jnp.abs(z))) - 0.5
    lw = -jnp.exp(w_log)
    a = _sigmoid(a0 + _mm(_mm(ca, a1), a2))
    gate = _mm(_sigmoid(_mm(cg, g1)), g2)
    kkraw = k * k_k
    kmod = k * (1.0 + (a - 1.0) * k_a)
    return r, lw, kmod, v, kkraw, a, gate


HALO_ROWS = 8


def _rwkv_pre_specs(c, mats, tile_of):
    tm = TOKEN_TILE
    nh = c // HEAD_DIM
    wide = pl.BlockSpec((tm, 4 * c), lambda j: (tile_of(j), 0))
    halo = pl.BlockSpec((HALO_ROWS, 4 * c), lambda j: (jnp.maximum(tile_of(j) * (tm // HALO_ROWS) - 1, 0), 0))
    one = pl.BlockSpec((tm, c), lambda j: (tile_of(j), 0))
    heads = pl.BlockSpec((nh, tm, HEAD_DIM), lambda j: (0, tile_of(j), 0))
    vec = pl.BlockSpec((10, c), lambda j: (0, 0))
    mspecs = [pl.BlockSpec(m.shape, lambda j: (0, 0)) for m in mats]
    return wide, halo, one, heads, vec, mspecs


def _previous_rows(cur, halo, tile):
    first = jnp.where(tile > 0, halo[HALO_ROWS - 1:HALO_ROWS], 0.0)
    rows = lax.broadcasted_iota(jnp.int32, cur.shape, 0)
    return jnp.where(rows == 0, first, pltpu.roll(cur, 1, axis=0))


def _rwkv_pre_fwd(cur, vec, mats):
    t, c4 = cur.shape
    c = c4 // 4
    wide, halo, one, heads, vspec, mspecs = _rwkv_pre_specs(c, mats, lambda j: j)

    def body(cur_ref, halo_ref, vec_ref, *rest):
        mrefs, outs = rest[:6], rest[6:]
        cur_v = cur_ref[...]
        prev = _previous_rows(cur_v, halo_ref[...], pl.program_id(0))
        vals = _rwkv_pre_fn(cur_v, prev, vec_ref[...], *(m[...] for m in mrefs))
        for ref, val in zip(outs[:6], vals[:6]):
            _store_heads(ref, val)
        outs[6][...] = vals[6]

    hshape = jax.ShapeDtypeStruct((c // HEAD_DIM, t, HEAD_DIM), F32)
    return pl.pallas_call(
        body, name="rwkv_pre_fwd", grid=(t // TOKEN_TILE,), out_shape=(hshape,) * 6 + (jax.ShapeDtypeStruct((t, c), F32),),
        in_specs=[wide, halo, vspec] + mspecs, out_specs=(heads,) * 6 + (one,), compiler_params=_params("arbitrary"),
    )(cur, cur, vec, *mats)


def _rwkv_pre_bwd(cur, vec, mats, cts, dgate):
    t, c4 = cur.shape
    c = c4 // 4
    tm = TOKEN_TILE
    nt = t // tm
    wide, halo, one, heads, vspec, mspecs = _rwkv_pre_specs(c, mats, lambda j: nt - 1 - j)

    def body(cur_ref, halo_ref, vec_ref, *rest):
        mrefs, ctrefs, dgate_ref, outs, carry_ref = rest[:6], rest[6:12], rest[12], rest[13:-1], rest[-1]
        j = pl.program_id(0)

        @pl.when(j == 0)
        def _():
            carry_ref[...] = jnp.zeros_like(carry_ref)
            for ref in outs[1:]:
                ref[...] = jnp.zeros_like(ref)

        cur_v = cur_ref[...]
        prev = _previous_rows(cur_v, halo_ref[...], nt - 1 - j)
        _, vjp = jax.vjp(_rwkv_pre_fn, cur_v, prev, vec_ref[...], *(m[...] for m in mrefs))
        grads = vjp(tuple(_load_heads(r) for r in ctrefs) + (dgate_ref[...],))
        dprev = grads[1]
        rows = lax.broadcasted_iota(jnp.int32, dprev.shape, 0)
        outs[0][...] = grads[0] + jnp.where(rows == tm - 1, carry_ref[0:1], pltpu.roll(dprev, tm - 1, axis=0))
        carry_ref[0:1] = dprev[0:1]
        for ref, val in zip(outs[1:], grads[2:]):
            ref[...] += val

    return pl.pallas_call(
        body, name="rwkv_pre_bwd", grid=(nt,),
        out_shape=(jax.ShapeDtypeStruct(cur.shape, F32), jax.ShapeDtypeStruct(vec.shape, F32))
        + tuple(jax.ShapeDtypeStruct(m.shape, F32) for m in mats),
        in_specs=[wide, halo, vspec] + mspecs + [heads] * 6 + [one], out_specs=(wide, vspec) + tuple(mspecs),
        scratch_shapes=[pltpu.VMEM((HALO_ROWS, c4), F32)], compiler_params=_params("arbitrary"),
    )(cur, cur, vec, *mats, *cts, dgate)


def _scan_chunk_fn(h0, r, lw, k, v, kkraw, a, rk, lnw, lnb):
    n = r.shape[1]
    nrm = jnp.sqrt(jnp.sum(kkraw * kkraw, axis=-1, keepdims=True))
    kk = kkraw / jnp.maximum(nrm, 1e-12)
    av, bv = -kk, kk * a
    ti = lax.broadcasted_iota(jnp.int32, (n, n), 0)
    si = lax.broadcasted_iota(jnp.int32, (n, n), 1)
    incl, strict = ti >= si, ti > si
    ones = jnp.broadcast_to(incl.astype(F32)[None], (r.shape[0], n, n))
    cum = _hdot(ones, lw, 2, 1)
    at, rt = av * jnp.exp(cum - lw), r * jnp.exp(cum)
    inv = jnp.exp(-cum)
    bt, kt = bv * inv, k * inv
    gram = _hdot(jnp.concatenate([at, rt], axis=1), jnp.concatenate([bt, kt], axis=1), 2, 2)
    lab = jnp.where(strict, gram[:, :n, :n], 0.0)
    lak = jnp.where(strict, gram[:, :n, n:], 0.0)
    rb = jnp.where(incl, gram[:, n:, :n], 0.0)
    rkm = jnp.where(incl, gram[:, n:, n:], 0.0)
    nv = v.shape[2]
    u = _bmm_nn(jnp.concatenate([at, lak], axis=2), jnp.concatenate([h0, v], axis=1))
    p = lab
    m = 2
    while m < n:
        both = _bmm_nn(p, jnp.concatenate([u, p], axis=2))
        u, p = u + both[:, :, :nv], both[:, :, nv:]
        m *= 2
    u = u + _bmm_nn(p, u)
    y = _bmm_nn(jnp.concatenate([rt, rb, rkm], axis=2), jnp.concatenate([h0, u, v], axis=1))
    last = jnp.exp(jnp.sum(lw, axis=1, keepdims=True))
    h1 = jnp.swapaxes(last, 1, 2) * (h0 + _bmm_tn(jnp.concatenate([bt, kt], axis=1), jnp.concatenate([u, v], axis=1)))
    mean = jnp.mean(y, axis=-1, keepdims=True)
    yc = y - mean
    var = jnp.mean(yc * yc, axis=-1, keepdims=True)
    yn = yc * lax.rsqrt(var + GN_EPS) * lnw + lnb
    bonus = jnp.sum(r * k * rk, axis=-1, keepdims=True) * v
    return yn + bonus, h1


SCAN_GROUP = 2


def _scan_group_fn(h0, r, lw, k, v, kkraw, a, rk, lnw, lnb):
    outs = []
    for j in range(SCAN_GROUP):
        rows = slice(j * SCAN_CHUNK, (j + 1) * SCAN_CHUNK)
        o, h0 = _scan_chunk_fn(h0, r[:, rows], lw[:, rows], k[:, rows], v[:, rows], kkraw[:, rows], a[:, rows], rk, lnw, lnb)
        outs.append(o)
    return jnp.concatenate(outs, axis=1), h0


def _scan_specs(h, t, dh, rev):
    n = SCAN_CHUNK * SCAN_GROUP
    nc = t // n
    pos = (lambda c: (0, nc - 1 - c, 0)) if rev else (lambda c: (0, c, 0))
    st = (lambda c: (nc - 1 - c, 0, 0, 0)) if rev else (lambda c: (c, 0, 0, 0))
    seq = pl.BlockSpec((h, n, dh), pos)
    par = pl.BlockSpec((h, 1, dh), lambda c: (0, 0, 0))
    state = pl.BlockSpec((1, h, dh, dh), st)
    return seq, par, state


def _scan_fwd(seqs, pars):
    h, t, dh = seqs[0].shape
    nc = t // (SCAN_CHUNK * SCAN_GROUP)
    seq, par, state = _scan_specs(h, t, dh, False)

    def body(r, lw, k, v, kkraw, a, rk, lnw, lnb, o_ref, st_ref, h_ref):
        @pl.when(pl.program_id(0) == 0)
        def _():
            h_ref[...] = jnp.zeros_like(h_ref)

        h0 = h_ref[...]
        st_ref[0] = h0
        o, h1 = _scan_group_fn(h0, r[...], lw[...], k[...], v[...], kkraw[...], a[...], rk[...], lnw[...], lnb[...])
        o_ref[...] = o
        h_ref[...] = h1

    return pl.pallas_call(
        body, name="rwkv_scan_fwd", grid=(nc,),
        out_shape=(jax.ShapeDtypeStruct((h, t, dh), F32), jax.ShapeDtypeStruct((nc, h, dh, dh), F32)),
        in_specs=[seq] * 6 + [par] * 3, out_specs=(seq, state),
        scratch_shapes=[pltpu.VMEM((h, dh, dh), F32)], compiler_params=_params("arbitrary"),
    )(*seqs, *pars)


def _scan_bwd(seqs, pars, states, do):
    h, t, dh = seqs[0].shape
    nc = t // (SCAN_CHUNK * SCAN_GROUP)
    seq, par, state = _scan_specs(h, t, dh, True)

    def body(r, lw, k, v, kkraw, a, rk, lnw, lnb, st_ref, do_ref, *rest):
        douts, dpars, dh_ref = rest[:6], rest[6:9], rest[9]
        first = pl.program_id(0) == 0

        @pl.when(first)
        def _():
            dh_ref[...] = jnp.zeros_like(dh_ref)

        _, vjp = jax.vjp(_scan_group_fn, st_ref[0], r[...], lw[...], k[...], v[...], kkraw[...], a[...],
                         rk[...], lnw[...], lnb[...])
        grads = vjp((do_ref[...], dh_ref[...]))
        dh_ref[...] = grads[0]
        for ref, val in zip(douts, grads[1:7]):
            ref[...] = val

        @pl.when(first)
        def _():
            for ref, val in zip(dpars, grads[7:]):
                ref[...] = val

        @pl.when(jnp.logical_not(first))
        def _():
            for ref, val in zip(dpars, grads[7:]):
                ref[...] += val

    sshape = jax.ShapeDtypeStruct((h, t, dh), F32)
    pshape = jax.ShapeDtypeStruct((h, 1, dh), F32)
    return pl.pallas_call(
        body, name="rwkv_scan_bwd", grid=(nc,), out_shape=(sshape,) * 6 + (pshape,) * 3,
        in_specs=[seq] * 6 + [par] * 3 + [state, seq], out_specs=(seq,) * 6 + (par,) * 3,
        scratch_shapes=[pltpu.VMEM((h, dh, dh), F32)], compiler_params=_params("arbitrary"),
    )(*seqs, *pars, states, do)


def _local_step(x, target, w, ex):
    w = dict(w)
    c = w["mu_r"].shape[-1]
    qn, kn = w["q_norm"].reshape(1, 1, HEAD_DIM), w["k_norm"].reshape(1, 1, HEAD_DIM)
    vec = jnp.concatenate([w[n].reshape(1, c) for n in RWKV_VEC], axis=0)
    pars = [w[n].reshape(-1, 1, HEAD_DIM) for n in ("r_k", "ln_x_w", "ln_x_b")]
    no_dep = jnp.zeros(DEP_SHAPE, F32)

    x1, gate1, up1 = _ffn_fwd(x, w["ffn1_norm"], w["ffn1_w_gate"], w["ffn1_w_up"], w["ffn1_w_down"], ex.first_dep, "ffn1_fwd")
    w.update(ex.mix_weights((x1,)))
    mats = [w[n] for n in RWKV_MAT]
    q, k, v, cur = _proj_fwd(x1, w["mix_norm"], w["w_in"])
    att, *saved = _att_fwd(q, k, v, qn, kn)
    pre = _rwkv_pre_fwd(cur, vec, mats)
    seqs, gate = pre[:6], pre[6]
    opg, states = _scan_fwd(seqs, pars)
    w.update(ex.out_weights((att, opg)))
    x2 = _mixout_fwd(x1, att, opg, gate, w["w_out"])
    x3, gate2, up2 = _ffn_fwd(x2, w["ffn2_norm"], w["ffn2_w_gate"], w["ffn2_w_up"], w["ffn2_w_down"], no_dep, "ffn2_fwd")
    dy, loss = _loss_head(x3, target)

    g = {}
    dx2, g["ffn2_norm"], g["ffn2_w_gate"], g["ffn2_w_up"], g["ffn2_w_down"] = _ffn_bwd(
        x2, w["ffn2_norm"], w["ffn2_w_gate"], w["ffn2_w_up"], w["ffn2_w_down"], gate2, up2, dy, no_dep, "ffn2_bwd")
    dep = ex.send_ffn2({n: g[n] for n in ("ffn2_w_gate", "ffn2_w_up", "ffn2_w_down")})
    datt, dopg, dgate, g["w_out"] = _mixout_bwd(att, opg, gate, w["w_out"], dx2, dep)
    dscan = _scan_bwd(seqs, pars, states, dopg)
    for n, d in zip(("r_k", "ln_x_w", "ln_x_b"), dscan[6:]):
        g[n] = d
    dcur, dvec, *dmats = _rwkv_pre_bwd(cur, vec, mats, dscan[:6], dgate)
    for n, d in zip(RWKV_MAT, dmats):
        g[n] = d
    for j, n in enumerate(RWKV_VEC):
        g[n] = dvec[j:j + 1]
    dq, dk, dv, g["q_norm"], g["k_norm"] = _att_bwd(q, k, v, qn, kn, saved, datt)
    dx1, g["mix_norm"], g["w_in"] = _proj_bwd(x1, w["mix_norm"], w["w_in"], dq, dk, dv, dcur, dx2)
    dep = ex.send_mix({n: g[n] for n in ("w_in", "w_out") + RWKV_MAT}, (dx1,))
    dx, g["ffn1_norm"], g["ffn1_w_gate"], g["ffn1_w_up"], g["ffn1_w_down"] = _ffn_bwd(
        x, w["ffn1_norm"], w["ffn1_w_gate"], w["ffn1_w_up"], w["ffn1_w_down"], gate1, up1, dx1, dep, "ffn1_bwd")
    return loss, dx, g


N_SHARDS = 4


def _place():
    return lax.axis_index("x"), lax.axis_index("y"), lax.axis_index("c")


def _chip_peers(x, y):
    return [(1 - x, y), (x, 1 - y), (1 - x, 1 - y)]


HBM = pl.BlockSpec(memory_space=pltpu.HBM)
SEM = pl.BlockSpec(memory_space=pltpu.SEMAPHORE)
DEP_SHAPE = (8, 128)


class _Views:
    to_sibling = False


class _GatherViews(_Views):
    @staticmethod
    def send(i, srcs, lands, k, at):
        return srcs[i], lands[i].at[at[3]]

    @staticmethod
    def landing(i, srcs, lands, k, at):
        return srcs[i], lands[i].at[2 * at[4] + at[5]]


class _ScatterViews(_Views):
    @staticmethod
    def send(i, srcs, lands, k, at):
        return srcs[i].at[2 * at[4] + at[5]], lands[i].at[k]

    @staticmethod
    def landing(i, srcs, lands, k, at):
        return srcs[i].at[at[3]], lands[i].at[k]


def _half_rows(ref, slot, half):
    rows = ref.shape[1] // 2
    return ref.at[slot, pl.ds(pl.multiple_of(half * rows, BF16_SUBLANES), rows)]


class _HalfGatherViews(_Views):
    @staticmethod
    def send(i, srcs, lands, k, at):
        rows = srcs[i].shape[0] // 2
        return srcs[i].at[pl.ds(pl.multiple_of(at[2] * rows, BF16_SUBLANES), rows)], _half_rows(lands[i], at[3], at[2])

    @staticmethod
    def landing(i, srcs, lands, k, at):
        rows = srcs[i].shape[0] // 2
        return srcs[i].at[pl.ds(pl.multiple_of(at[2] * rows, BF16_SUBLANES), rows)], _half_rows(lands[i], 2 * at[4] + at[5], at[2])


class _ForwardViews(_Views):
    to_sibling = True

    @staticmethod
    def send(i, srcs, lands, k, at):
        mine = _half_rows(lands[i], 2 * at[4] + at[5], at[2])
        return mine, mine

    @staticmethod
    def landing(i, srcs, lands, k, at):
        theirs = _half_rows(lands[i], 2 * at[4] + at[5], 1 - at[2])
        return theirs, theirs


def _push_start(srcs, lands, views, after, name):
    ns, nl = len(srcs), len(lands)

    def body(*refs):
        src_refs, land_refs = refs[:ns], refs[ns:ns + nl]
        send_sems, recv_sems = refs[ns + nl + 1:ns + nl + 3]
        token = refs[2 * (ns + nl) + 3]
        x, y, c = _place()
        for i in range(nl):
            for k, (px, py) in enumerate(_chip_peers(x, y)):
                src, dst = views.send(i, src_refs, land_refs, k, (x, y, c, 2 * x + y, px, py))
                pltpu.make_async_remote_copy(
                    src_ref=src, dst_ref=dst, send_sem=send_sems.at[3 * i + k], recv_sem=recv_sems.at[3 * i + k],
                    device_id=(x, y, 1 - c) if views.to_sibling else (px, py, c), device_id_type=MESH).start()
        token[...] = jnp.zeros_like(token)

    sems = pltpu.SemaphoreType.DMA((3 * nl,))
    both = [pltpu.with_memory_space_constraint(a, pltpu.HBM) for a in (*srcs, *lands)]
    outs = pl.pallas_call(
        body, name=name,
        out_shape=(sems, sems, *[pltpu.HBM(a.shape, a.dtype) for a in both], jax.ShapeDtypeStruct(DEP_SHAPE, F32)),
        in_specs=[HBM] * (ns + nl) + [ANY], out_specs=(SEM, SEM, *[HBM] * (ns + nl), VMEM_FULL),
        input_output_aliases={i: 2 + i for i in range(ns + nl)},
        compiler_params=pltpu.CompilerParams(has_side_effects=pltpu.SideEffectType.DATAFLOW_SIDE_EFFECTING),
    )(*both, after)
    return outs[0], outs[1], outs[2:2 + ns], outs[2 + ns:2 + ns + nl], outs[2 + ns + nl]


def _push_wait(started, views, after, name, with_sources=False):
    send_sems, recv_sems, srcs, lands, _ = started
    ns, nl = len(srcs), len(lands)

    def body(*refs):
        src_refs, land_refs = refs[:ns], refs[ns:ns + nl]
        send_sems, recv_sems = refs[ns + nl:ns + nl + 2]
        x, y, c = _place()
        for i in range(nl):
            for k, (px, py) in enumerate(_chip_peers(x, y)):
                src, dst = views.landing(i, src_refs, land_refs, k, (x, y, c, 2 * x + y, px, py))
                landing = pltpu.make_async_remote_copy(
                    src_ref=src, dst_ref=dst, send_sem=send_sems.at[3 * i + k], recv_sem=recv_sems.at[3 * i + k],
                    device_id=(x, y, 1 - c) if views.to_sibling else (px, py, c), device_id_type=MESH)
                landing.wait_send()
                landing.wait_recv()

    outs = pl.pallas_call(
        body, name=name,
        out_shape=tuple(pltpu.HBM(a.shape, a.dtype) for a in (*srcs, *lands)),
        in_specs=[HBM] * (ns + nl) + [SEM, SEM] + [ANY] * len(after), out_specs=(HBM,) * (ns + nl),
        input_output_aliases={i: i for i in range(ns + nl)},
        compiler_params=pltpu.CompilerParams(has_side_effects=pltpu.SideEffectType.DATAFLOW_SIDE_EFFECTING),
    )(*srcs, *lands, send_sems, recv_sems, *after)
    return outs if with_sources else outs[ns:]


def _empty_lands(shards, slots, own_slot):
    lands = [lax.empty((slots,) + s.shape, s.dtype) for s in shards]
    if own_slot:
        me = 2 * lax.axis_index("x") + lax.axis_index("y")
        lands = [lax.dynamic_update_index_in_dim(z, s, me, 0) for z, s in zip(lands, shards)]
    return lands


def _sibling_swap(arrays, name):
    n = len(arrays)

    def body(*refs):
        ins, outs = refs[:n], refs[n:2 * n]
        send_sems, recv_sems = refs[2 * n:]
        x, y, c = _place()
        copies = []
        for i in range(n):
            cp = pltpu.make_async_remote_copy(
                src_ref=ins[i], dst_ref=outs[i], send_sem=send_sems.at[i], recv_sem=recv_sems.at[i],
                device_id=(x, y, 1 - c), device_id_type=MESH)
            cp.start()
            copies.append(cp)
        for cp in copies:
            cp.wait()

    return pl.pallas_call(
        body, name=name,
        out_shape=tuple(jax.ShapeDtypeStruct(a.shape, a.dtype) for a in arrays),
        in_specs=[ANY] * n, out_specs=(ANY,) * n,
        scratch_shapes=[pltpu.SemaphoreType.DMA((n,)), pltpu.SemaphoreType.DMA((n,))],
    )(*arrays)


N_DEV = 8


def _allreduce_small(pack):
    def body(in_ref, out_ref, buf, send_sems, recv_sems):
        x, y, c = _place()
        me = 4 * x + 2 * y + c
        buf[me] = in_ref[...]

        def copy(j, slot):
            px, py, pc = x ^ (j >> 2), y ^ ((j >> 1) & 1), c ^ (j & 1)
            return pltpu.make_async_remote_copy(
                src_ref=in_ref, dst_ref=buf.at[slot(px, py, pc)], send_sem=send_sems.at[j], recv_sem=recv_sems.at[j],
                device_id=(px, py, pc), device_id_type=MESH)

        for j in range(1, N_DEV):
            copy(j, lambda px, py, pc: me).start()
        for j in range(1, N_DEV):
            landing = copy(j, lambda px, py, pc: 4 * px + 2 * py + pc)
            landing.wait_send()
            landing.wait_recv()
        acc = buf[0]
        for s in range(1, N_DEV):
            acc = acc + buf[s]
        out_ref[...] = acc

    return pl.pallas_call(
        body, name="allreduce_small", out_shape=jax.ShapeDtypeStruct(pack.shape, F32),
        in_specs=[VMEM_FULL], out_specs=VMEM_FULL,
        scratch_shapes=[pltpu.VMEM((N_DEV,) + pack.shape, F32), pltpu.SemaphoreType.DMA((N_DEV,)),
                        pltpu.SemaphoreType.DMA((N_DEV,))],
    )(pack)


ROW_TILE_MAX = 256
BF16_SUBLANES = 16


def _row_tile(rows):
    for tr in range(min(rows, ROW_TILE_MAX), 0, -1):
        if rows % tr == 0 and tr % BF16_SUBLANES == 0:
            return tr
    return rows


def _reduce_own(me, part, recv, dep, name):
    _, r, cols = part.shape
    tr = _row_tile(r)

    def body(me_ref, p_ref, rv_ref, dep_ref, o_ref):
        acc = p_ref[0].astype(F32)
        for k in range(3):
            acc = acc + rv_ref[k].astype(F32)
        o_ref[...] = acc

    return pl.pallas_call(
        body, name=name, out_shape=jax.ShapeDtypeStruct((r, cols), F32),
        grid_spec=pltpu.PrefetchScalarGridSpec(
            num_scalar_prefetch=1, grid=(r // tr,),
            in_specs=[pl.BlockSpec((1, tr, cols), lambda i, me_ref: (me_ref[0], i, 0)),
                      pl.BlockSpec((3, tr, cols), lambda i, me_ref: (0, i, 0)), ANY],
            out_specs=pl.BlockSpec((tr, cols), lambda i, me_ref: (i, 0))),
        compiler_params=_params("arbitrary"),
    )(me, part, recv, dep)


def _adamw(w, ga, gb, m, v, name):
    r, cols = w.shape
    tr = _row_tile(r)
    c1 = 1.0 - ADAM_B1 ** ADAM_STEP
    c2 = 1.0 - ADAM_B2 ** ADAM_STEP

    def body(w_ref, ga_ref, gb_ref, m_ref, v_ref, g_out, d_out, m_out, v_out):
        g = ga_ref[...] + gb_ref[...]
        mn = ADAM_B1 * m_ref[...] + (1.0 - ADAM_B1) * g
        vn = ADAM_B2 * v_ref[...] + (1.0 - ADAM_B2) * (g * g)
        g_out[...] = g
        m_out[...] = mn
        v_out[...] = vn
        d_out[...] = -ADAM_LR * ((mn / c1) / (jnp.sqrt(vn / c2) + ADAM_EPS) + ADAM_WD * w_ref[...])

    tile = pl.BlockSpec((tr, cols), lambda i: (i, 0))
    shape = jax.ShapeDtypeStruct((r, cols), F32)
    return pl.pallas_call(
        body, name=name, grid=(r // tr,), out_shape=(shape,) * 4, in_specs=[tile] * 5, out_specs=(tile,) * 4,
        compiler_params=_params("arbitrary"),
    )(w, ga, gb, m, v)


PACK_COLS = 512


def _to_rows(a):
    flat = a.reshape(-1)
    pad = (-flat.shape[0]) % PACK_COLS
    return jnp.pad(flat, (0, pad)).reshape(-1, PACK_COLS)


def _pack(arrays, extra_rows=0):
    rows = [_to_rows(a) for a in arrays]
    n = sum(r.shape[0] for r in rows) + extra_rows
    pad = (-n) % 8
    return jnp.concatenate(rows + [jnp.zeros((extra_rows + pad, PACK_COLS), F32)], axis=0)


def _unpack(pack, like):
    out, at = [], 0
    for a in like:
        n = -(-a.size // PACK_COLS)
        out.append(pack[at:at + n].reshape(-1)[:a.size].reshape(a.shape))
        at += n
    return out


COL_SHARDED = ("ffn1_w_gate", "ffn1_w_up", "w_in", "ffn2_w_gate", "ffn2_w_up", "w2", "a2", "g2")
ROW_SHARDED = ("ffn1_w_down", "ffn2_w_down", "w_out", "w1", "a1", "g1")
CHUNKED = ("ffn1_w_gate", "ffn1_w_up", "ffn1_w_down", "ffn2_w_gate", "ffn2_w_up", "ffn2_w_down")
WEIGHTS = ("ffn1_norm", "ffn1_w_gate", "ffn1_w_up", "ffn1_w_down", "mix_norm", "w_in", "q_norm", "k_norm",
           "mu_r", "mu_k", "mu_v", "mu_w", "mu_a", "mu_g", "w0", "w1", "w2", "a0", "a1", "a2", "g1", "g2",
           "k_k", "k_a", "r_k", "ln_x_w", "ln_x_b", "w_out", "ffn2_norm", "ffn2_w_gate", "ffn2_w_up", "ffn2_w_down")


W_IN_GROUPS = 7
TRANSPOSED = ("ffn1_w_gate", "ffn1_w_up", "ffn2_w_gate", "ffn2_w_up")


def _shard_2d(name, a):
    return a[0].T if name in TRANSPOSED else a[0]


def _full_from_blocks(name, blocks):
    if name in CHUNKED:
        return blocks
    if name in ROW_SHARDED:
        return blocks.reshape(-1, blocks.shape[-1])
    full = blocks.transpose(1, 0, 2).reshape(blocks.shape[1], -1)
    if name == "w_in":
        return full.reshape(full.shape[0], W_IN_GROUPS, -1).transpose(1, 0, 2)
    return full


def _blocks_from_full(name, full):
    if name in CHUNKED:
        return full
    if name in ROW_SHARDED:
        return full.reshape(N_SHARDS, -1, full.shape[-1])
    if name == "w_in":
        full = full.transpose(1, 0, 2).reshape(full.shape[1], -1)
    return full.reshape(full.shape[0], N_SHARDS, -1).transpose(1, 0, 2)


FFN1_GROUP = ("ffn1_w_gate", "ffn1_w_up", "ffn1_w_down")
MIX_GROUP = ("w_in",) + RWKV_MAT
OUT_GROUP = ("w_out", "ffn2_w_gate", "ffn2_w_up", "ffn2_w_down")
FFN2_GROUP = OUT_GROUP[1:]
LATE_GROUP = ("w_in", "w_out") + RWKV_MAT


class _Exchange:
    def __init__(self, given):
        self.given = given
        first = self._gather_start(FFN1_GROUP, _HalfGatherViews, jnp.zeros(DEP_SHAPE, F32), "gather_ffn1_start")
        self.mix = self._gather_start(MIX_GROUP, _GatherViews, first[4], "gather_mix_start")
        self.out = self._gather_start(OUT_GROUP, _GatherViews, self.mix[4], "gather_out_start")
        self.first_dep = self.out[4]
        halves = _push_wait(first, _HalfGatherViews, (self.first_dep,), "gather_ffn1_wait")
        passed = _push_start([], halves, _ForwardViews, halves[0], "gather_ffn1_pass_start")
        self.first_weights = self._full(FFN1_GROUP, _push_wait(passed, _ForwardViews, (passed[4],), "gather_ffn1_pass_wait"))
        self.parts, self.recv = {}, {}

    def _shards(self, names):
        return [_shard_2d(n, self.given[n]).astype(BF16) for n in names]

    @staticmethod
    def _full(names, blocks):
        out = {}
        for n, b in zip(names, blocks):
            full = _full_from_blocks(n, b)
            out[n] = full.astype(F32) if n in RWKV_MAT else full
        return out

    def _gather_start(self, names, views, after, name):
        shards = self._shards(names)
        return _push_start(shards, _empty_lands(shards, N_SHARDS, True), views, after, name)

    def mix_weights(self, after):
        return self._full(MIX_GROUP, _push_wait(self.mix, _GatherViews, after, "gather_mix_wait"))

    def out_weights(self, after):
        return self._full(OUT_GROUP, _push_wait(self.out, _GatherViews, after, "gather_out_wait"))

    def _scatter_start(self, grads, name):
        names = tuple(grads)
        parts = [_blocks_from_full(n, grads[n]) for n in names]
        self.parts.update(zip(names, parts))
        lands = [lax.empty((3,) + p.shape[1:], BF16) for p in parts]
        return _push_start([p.astype(BF16) for p in parts], lands, _ScatterViews, jnp.zeros(DEP_SHAPE, F32), name)

    def _scatter_done(self, started, names, after, name):
        outs = _push_wait(started, _ScatterViews, after, name, with_sources=True)
        for n, sent, got in zip(names, outs[:len(names)], outs[len(names):]):
            self.recv[n] = got
            if self.parts[n].dtype == BF16:
                self.parts[n] = sent

    def send_ffn2(self, grads):
        self.ffn2 = self._scatter_start(grads, "scatter_ffn2_start")
        return self.ffn2[4]

    def send_mix(self, grads, after):
        self._scatter_done(self.ffn2, FFN2_GROUP, after, "scatter_ffn2_wait")
        self.late = self._scatter_start(grads, "scatter_late_start")
        return self.late[4]

    def send_ffn1(self, grads):
        self.ffn1 = self._scatter_start(grads, "scatter_ffn1_start")
        return self.ffn1[4]

    def late_received(self, after):
        self._scatter_done(self.late, LATE_GROUP, after, "scatter_late_wait")

    def ffn1_received(self, after):
        self._scatter_done(self.ffn1, FFN1_GROUP, after, "scatter_ffn1_wait")


def kernel(
        x, ffn1_norm, ffn1_w_gate, ffn1_w_up, ffn1_w_down, mix_norm, w_in, q_norm, k_norm, mu_r, mu_k, mu_v, mu_w,
        mu_a, mu_g, w0, w1, w2, a0, a1, a2, g1, g2, k_k, k_a, r_k, ln_x_w, ln_x_b, w_out, ffn2_norm, ffn2_w_gate,
        ffn2_w_up, ffn2_w_down, loss_target, m_ffn1_norm, m_ffn1_w_gate, m_ffn1_w_up, m_ffn1_w_down, m_mix_norm,
        m_w_in, m_q_norm, m_k_norm, m_mu_r, m_mu_k, m_mu_v, m_mu_w, m_mu_a, m_mu_g, m_w0, m_w1, m_w2, m_a0, m_a1,
        m_a2, m_g1, m_g2, m_k_k, m_k_a, m_r_k, m_ln_x_w, m_ln_x_b, m_w_out, m_ffn2_norm, m_ffn2_w_gate, m_ffn2_w_up,
        m_ffn2_w_down, v_ffn1_norm, v_ffn1_w_gate, v_ffn1_w_up, v_ffn1_w_down, v_mix_norm, v_w_in, v_q_norm, v_k_norm,
        v_mu_r, v_mu_k, v_mu_v, v_mu_w, v_mu_a, v_mu_g, v_w0, v_w1, v_w2, v_a0, v_a1, v_a2, v_g1, v_g2, v_k_k, v_k_a,
        v_r_k, v_ln_x_w, v_ln_x_b, v_w_out, v_ffn2_norm, v_ffn2_w_gate, v_ffn2_w_up, v_ffn2_w_down):
    given = dict(locals())
    sharded = COL_SHARDED + ROW_SHARDED
    sharded = tuple(n for n in WEIGHTS if n in sharded)
    small = tuple(n for n in WEIGHTS if n not in sharded)

    ex = _Exchange(given)
    w = {n: given[n] for n in small}
    w.update(ex.first_weights)
    loss, dx, g = _local_step(x[0], loss_target[0], w, ex)
    dep = ex.send_ffn1({n: g[n] for n in FFN1_GROUP})

    me = (2 * lax.axis_index("x") + lax.axis_index("y")).astype(jnp.int32).reshape(1)
    out = {}

    def settle(names, dep, tag):
        mine = []
        for n in names:
            p, rv = ex.parts[n], ex.recv[n]
            p2 = p.reshape(N_SHARDS, -1, p.shape[-1])
            mine.append(_reduce_own(me, p2, rv.reshape(3, -1, rv.shape[-1]), dep, f"reduce_{n}"))
        theirs = _sibling_swap(mine, f"sibling_swap_{tag}")
        for n, a, b in zip(names, mine, theirs):
            shape = given[n].shape
            res = _adamw(_shard_2d(n, given[n]), a, b, _shard_2d(n, given["m_" + n]), _shard_2d(n, given["v_" + n]), f"adamw_{n}")
            out[n] = [(r.T if n in TRANSPOSED else r).reshape(shape) for r in res]
        return tuple(out[n][1] for n in names)

    ex.late_received((dep,))
    last = settle(tuple(n for n in sharded if n not in FFN1_GROUP), dep, "rest")

    gpack = _pack([g[n] for n in small], extra_rows=1)
    n_rows = sum(-(-given[n].size // PACK_COLS) for n in small)
    gpack = gpack.at[n_rows, :loss.shape[1]].set(loss[0])
    gsum = _allreduce_small(gpack)
    res = _adamw(_pack([given[n] for n in small], 1), gsum, jnp.zeros_like(gsum), _pack([given["m_" + n] for n in small], 1),
                 _pack([given["v_" + n] for n in small], 1), "adamw_small")
    like = [given[n] for n in small]
    for j, r in enumerate(res):
        for n, a in zip(small, _unpack(r, like)):
            out.setdefault(n, [None] * 4)[j] = a
    total_loss = gsum[n_rows, 0]

    ex.ffn1_received((*last, res[1]))
    settle(FFN1_GROUP, jnp.zeros(DEP_SHAPE, F32), "ffn1")
    return (total_loss, dx[None], *[out[n][0] for n in WEIGHTS], *[out[n][1] for n in WEIGHTS],
            *[out[n][2] for n in WEIGHTS], *[out[n][3] for n in WEIGHTS])
```

```python
import functools

import jax
import jax.numpy as jnp
from jax import lax
from jax.experimental import pallas as pl
from jax.experimental.pallas import tpu as pltpu

F32 = jnp.float32
BF16 = jnp.bfloat16
MESH = pl.DeviceIdType.MESH

RMS_EPS = 1e-6
GN_EPS = 64e-5
NEG_INF = -1e30
FFN_RESIDUAL = 0.5
HEAD_DIM = 64
ATT_BLOCK = 128
DILATIONS = (1, 4, 16)
SCAN_CHUNK = 64
TOKEN_TILE = 256
FFN_BWD_TILE = 512

ADAM_LR = 0.001
ADAM_B1 = 0.9
ADAM_B2 = 0.999
ADAM_EPS = 1e-08
ADAM_WD = 0.01
ADAM_STEP = 10

VMEM_FULL = pl.BlockSpec(memory_space=pltpu.VMEM)
ANY = pl.BlockSpec(memory_space=pl.ANY)


VMEM_LIMIT = 56 * 1024 * 1024


def _params(*sem):
    return pltpu.CompilerParams(dimension_semantics=sem, vmem_limit_bytes=VMEM_LIMIT)


def _dot(a, b, dims):
    return lax.dot_general(a.astype(BF16), b.astype(BF16), (dims, ((), ())), preferred_element_type=F32)


def _dot_nn(a, b):
    return _dot(a, b, ((1,), (0,)))


def _dot_nt(a, b):
    return _dot(a, b, ((1,), (1,)))


def _dot_tn(a, b):
    return _dot(a, b, ((0,), (0,)))


@jax.custom_vjp
def _mm(a, b):
    return _dot_nn(a, b)


def _mm_fwd(a, b):
    return _dot_nn(a, b), (a, b)


def _mm_bwd(res, g):
    a, b = res
    return _dot_nt(g, b).astype(a.dtype), _dot_tn(a, g).astype(b.dtype)


_mm.defvjp(_mm_fwd, _mm_bwd)


def _bdot(a, b, ca, cb):
    return lax.dot_general(a.astype(BF16), b.astype(BF16), (((ca,), (cb,)), ((0,), (0,))), preferred_element_type=F32)


@jax.custom_vjp
def _bmm_nt(a, b):
    return _bdot(a, b, 2, 2)


def _bmm_nt_fwd(a, b):
    return _bdot(a, b, 2, 2), (a, b)


def _bmm_nt_bwd(res, g):
    a, b = res
    return _bdot(g, b, 2, 1), _bdot(g, a, 1, 1)


_bmm_nt.defvjp(_bmm_nt_fwd, _bmm_nt_bwd)


@jax.custom_vjp
def _bmm_nn(a, b):
    return _bdot(a, b, 2, 1)


def _bmm_nn_fwd(a, b):
    return _bdot(a, b, 2, 1), (a, b)


def _bmm_nn_bwd(res, g):
    a, b = res
    return _bdot(g, b, 2, 2), _bdot(a, g, 1, 1)


_bmm_nn.defvjp(_bmm_nn_fwd, _bmm_nn_bwd)


@jax.custom_vjp
def _bmm_tn(a, b):
    return _bdot(a, b, 1, 1)


def _bmm_tn_fwd(a, b):
    return _bdot(a, b, 1, 1), (a, b)


def _bmm_tn_bwd(res, g):
    a, b = res
    return _bdot(b, g, 2, 2), _bdot(a, g, 2, 1)


_bmm_tn.defvjp(_bmm_tn_fwd, _bmm_tn_bwd)


def _hdot(a, b, ca, cb):
    return lax.dot_general(a, b, (((ca,), (cb,)), ((0,), (0,))), precision=lax.Precision.HIGH, preferred_element_type=F32)


def _sigmoid(x):
    return 1.0 / (1.0 + jnp.exp(-x))


def _rms(x):
    return lax.rsqrt(jnp.mean(x * x, axis=-1, keepdims=True) + RMS_EPS)


def _ffn_fwd(x, norm, wg, wu, wd, dep, name):
    t, d = x.shape
    nc, fc, _ = wg.shape
    tm = TOKEN_TILE

    def body(x_ref, n_ref, wg_ref, wu_ref, wd_ref, dep_ref, o_ref, g_ref, u_ref):
        xv = x_ref[...]
        h = (xv * _rms(xv) * n_ref[...]).astype(BF16)
        acc = jnp.zeros((tm, d), F32)
        for c in range(nc):
            g = _dot_nt(h, wg_ref[c])
            u = _dot_nt(h, wu_ref[c])
            g_ref[c] = g.astype(BF16)
            u_ref[c] = u.astype(BF16)
            a = (g * _sigmoid(g) * u).astype(BF16)
            acc = acc + jnp.dot(a, wd_ref[c], preferred_element_type=F32)
        o_ref[...] = xv + FFN_RESIDUAL * acc

    tile = pl.BlockSpec((tm, d), lambda i: (i, 0))
    hidden = pl.BlockSpec((nc, tm, fc), lambda i: (0, i, 0))
    hshape = jax.ShapeDtypeStruct((nc, t, fc), BF16)
    return pl.pallas_call(
        body, name=name, grid=(t // tm,), out_shape=(jax.ShapeDtypeStruct((t, d), F32), hshape, hshape),
        in_specs=[tile, pl.BlockSpec((1, d), lambda i: (0, 0)), VMEM_FULL, VMEM_FULL, VMEM_FULL, ANY],
        out_specs=(tile, hidden, hidden), compiler_params=_params("arbitrary"),
    )(x, norm, wg, wu, wd, dep)


def _rmsnorm_bwd(xv, gain, dh):
    rs = _rms(xv)
    xn = xv * rs
    dxn = dh * gain
    dx = rs * (dxn - xn * jnp.mean(dxn * xn, axis=-1, keepdims=True))
    return dx, jnp.sum(dh * xn, axis=0, keepdims=True)


def _ffn_bwd(x, norm, wg, wu, wd, gate, up, dy, dep, name):
    t, d = x.shape
    nc, fc, _ = wg.shape
    tm = FFN_BWD_TILE
    nt = t // tm

    def body(x_ref, n_ref, wg_ref, wu_ref, wd_ref, g_ref, u_ref, dy_ref, dep_ref, dx_ref, dn_ref, dwg_ref, dwu_ref,
             dwd_ref, dh_ref, ag_ref, au_ref, ad_ref):
        c, i = pl.program_id(0), pl.program_id(1)
        rows = pl.ds(pl.multiple_of(i * tm, tm), tm)
        xv = x_ref[...]
        gain = n_ref[...]
        h = (xv * _rms(xv) * gain).astype(BF16)
        dy = dy_ref[...]
        dyb = (FFN_RESIDUAL * dy).astype(BF16)
        g = g_ref[0].astype(F32)
        u = u_ref[0].astype(F32)
        sg = _sigmoid(g)
        s = g * sg
        a = (s * u).astype(BF16)
        da = _dot_nt(dyb, wd_ref[0])
        dub = (da * s).astype(BF16)
        dgb = (da * u * (sg * (1.0 + g * (1.0 - sg)))).astype(BF16)
        dwd_c = _dot_tn(a, dyb)
        dwg_c = _dot_tn(dgb, h)
        dwu_c = _dot_tn(dub, h)
        dh_c = _dot_nn(dgb, wg_ref[0]) + _dot_nn(dub, wu_ref[0])

        @pl.when(i == 0)
        def _():
            ad_ref[...] = dwd_c
            ag_ref[...] = dwg_c
            au_ref[...] = dwu_c

        @pl.when(i > 0)
        def _():
            ad_ref[...] += dwd_c
            ag_ref[...] += dwg_c
            au_ref[...] += dwu_c

        @pl.when(i == nt - 1)
        def _():
            dwd_ref[0] = ad_ref[...].astype(BF16)
            dwg_ref[0] = ag_ref[...].astype(BF16)
            dwu_ref[0] = au_ref[...].astype(BF16)

        @pl.when(c == 0)
        def _():
            dh_ref[rows, :] = dh_c

        @pl.when(c > 0)
        def _():
            dh_ref[rows, :] += dh_c

        @pl.when(c == nc - 1)
        def _():
            dx, dn = _rmsnorm_bwd(xv, gain, dh_ref[rows, :])
            dx_ref[...] = dx + dy

            @pl.when(i == 0)
            def _():
                dn_ref[...] = dn

            @pl.when(i > 0)
            def _():
                dn_ref[...] += dn

    tile = pl.BlockSpec((tm, d), lambda c, i: (i, 0))
    row = pl.BlockSpec((1, d), lambda c, i: (0, 0))
    wrow = pl.BlockSpec((1, fc, d), lambda c, i: (c, 0, 0), pipeline_mode=pl.Buffered(1))
    hidden = pl.BlockSpec((1, tm, fc), lambda c, i: (c, i, 0))
    last = pl.BlockSpec((tm, d), lambda c, i: (jnp.where(c == nc - 1, i, 0), 0))
    return pl.pallas_call(
        body, name=name, grid=(nc, nt),
        out_shape=(jax.ShapeDtypeStruct((t, d), F32), jax.ShapeDtypeStruct((1, d), F32),
                   jax.ShapeDtypeStruct(wg.shape, BF16), jax.ShapeDtypeStruct(wu.shape, BF16),
                   jax.ShapeDtypeStruct(wd.shape, BF16)),
        in_specs=[tile, row, wrow, wrow, wrow, hidden, hidden, tile, ANY],
        out_specs=(last, row, wrow, wrow, wrow),
        scratch_shapes=[pltpu.VMEM((t, d), F32)] + [pltpu.VMEM((fc, d), F32)] * 3,
        compiler_params=_params("arbitrary", "arbitrary"),
    )(x, norm, wg, wu, wd, gate, up, dy, dep)


def _store_heads(ref, v):
    for h in range(ref.shape[0]):
        ref[h] = v[:, h * HEAD_DIM:(h + 1) * HEAD_DIM]


def _load_heads(ref):
    return jnp.concatenate([ref[h] for h in range(ref.shape[0])], axis=-1)


N_HEAD_GROUPS = 3


def _proj_fwd(x, norm, w):
    t, d = x.shape
    ng, _, c = w.shape
    nh = c // HEAD_DIM
    tm = TOKEN_TILE

    def body(x_ref, n_ref, w_ref, q_ref, k_ref, v_ref, cur_ref):
        xv = x_ref[...]
        h = (xv * _rms(xv) * n_ref[...]).astype(BF16)
        for m, ref in enumerate((q_ref, k_ref, v_ref)):
            _store_heads(ref, jnp.dot(h, w_ref[m], preferred_element_type=F32))
        for m in range(N_HEAD_GROUPS, ng):
            j = m - N_HEAD_GROUPS
            cur_ref[:, j * c:(j + 1) * c] = jnp.dot(h, w_ref[m], preferred_element_type=F32)

    heads = pl.BlockSpec((nh, tm, HEAD_DIM), lambda i: (0, i, 0))
    hshape = jax.ShapeDtypeStruct((nh, t, HEAD_DIM), F32)
    wide = (ng - N_HEAD_GROUPS) * c
    return pl.pallas_call(
        body, name="proj_fwd", grid=(t // tm,),
        out_shape=(hshape, hshape, hshape, jax.ShapeDtypeStruct((t, wide), F32)),
        in_specs=[pl.BlockSpec((tm, d), lambda i: (i, 0)), pl.BlockSpec((1, d), lambda i: (0, 0)), VMEM_FULL],
        out_specs=(heads, heads, heads, pl.BlockSpec((tm, wide), lambda i: (i, 0))),
        compiler_params=_params("arbitrary"),
    )(x, norm, w)


def _proj_bwd(x, norm, w, dq, dk, dv, dcur, dres):
    t, d = x.shape
    ng, _, c = w.shape
    nh = c // HEAD_DIM
    tm = TOKEN_TILE

    def body(x_ref, n_ref, w_ref, dq_ref, dk_ref, dv_ref, dcur_ref, dres_ref, dx_ref, dn_ref, dw_ref):
        i = pl.program_id(0)

        @pl.when(i == 0)
        def _():
            dw_ref[...] = jnp.zeros_like(dw_ref)
            dn_ref[...] = jnp.zeros_like(dn_ref)

        xv = x_ref[...]
        gain = n_ref[...]
        h = (xv * _rms(xv) * gain).astype(BF16)
        dh = jnp.zeros((tm, d), F32)
        for m in range(ng):
            j = m - N_HEAD_GROUPS
            dp = _load_heads((dq_ref, dk_ref, dv_ref)[m]) if j < 0 else dcur_ref[:, j * c:(j + 1) * c]
            dp = dp.astype(BF16)
            dw_ref[m] += _dot_tn(h, dp)
            dh = dh + _dot_nt(dp, w_ref[m])
        dx, dn = _rmsnorm_bwd(xv, gain, dh)
        dx_ref[...] = dx + dres_ref[...]
        dn_ref[...] += dn

    tile = pl.BlockSpec((tm, d), lambda i: (i, 0))
    row = pl.BlockSpec((1, d), lambda i: (0, 0))
    heads = pl.BlockSpec((nh, tm, HEAD_DIM), lambda i: (0, i, 0))
    wide = (ng - N_HEAD_GROUPS) * c
    return pl.pallas_call(
        body, name="proj_bwd", grid=(t // tm,),
        out_shape=(jax.ShapeDtypeStruct((t, d), F32), jax.ShapeDtypeStruct((1, d), F32),
                   jax.ShapeDtypeStruct(w.shape, F32)),
        in_specs=[tile, row, VMEM_FULL, heads, heads, heads, pl.BlockSpec((tm, wide), lambda i: (i, 0)), tile],
        out_specs=(tile, row, VMEM_FULL),
        compiler_params=_params("arbitrary"),
    )(x, norm, w, dq, dk, dv, dcur, dres)


def _mixout_fwd(x, att, opg, gate, w):
    t, d = x.shape
    nh = att.shape[0]
    half = gate.shape[1]
    tm = TOKEN_TILE

    def body(x_ref, att_ref, opg_ref, g_ref, w_ref, o_ref):
        mix = jnp.concatenate([_load_heads(att_ref), _load_heads(opg_ref) * g_ref[...]], axis=-1).astype(BF16)
        o_ref[...] = x_ref[...] + jnp.dot(mix, w_ref[...], preferred_element_type=F32)

    tile = pl.BlockSpec((tm, d), lambda i: (i, 0))
    htile = pl.BlockSpec((tm, half), lambda i: (i, 0))
    heads = pl.BlockSpec((nh, tm, HEAD_DIM), lambda i: (0, i, 0))
    return pl.pallas_call(
        body, name="mixout_fwd", grid=(t // tm,), out_shape=jax.ShapeDtypeStruct((t, d), F32),
        in_specs=[tile, heads, heads, htile, VMEM_FULL], out_specs=tile, compiler_params=_params("arbitrary"),
    )(x, att, opg, gate, w)


def _mixout_bwd(att, opg, gate, w, dy, dep):
    nh, t, _ = att.shape
    half = gate.shape[1]
    d = dy.shape[1]
    tm = TOKEN_TILE

    def body(att_ref, opg_ref, g_ref, w_ref, dy_ref, dep_ref, datt_ref, dopg_ref, dg_ref, dw_ref):
        i = pl.program_id(0)
        opg_v, g_v = _load_heads(opg_ref), g_ref[...]
        mix = jnp.concatenate([_load_heads(att_ref), opg_v * g_v], axis=-1).astype(BF16)
        dyb = dy_ref[...].astype(BF16)
        dmix = _dot_nt(dyb, w_ref[...])
        dw = _dot_tn(mix, dyb)
        _store_heads(datt_ref, dmix[:, :half])
        drw = dmix[:, half:]
        _store_heads(dopg_ref, drw * g_v)
        dg_ref[...] = drw * opg_v

        @pl.when(i == 0)
        def _():
            dw_ref[...] = dw

        @pl.when(i > 0)
        def _():
            dw_ref[...] += dw

    tile = pl.BlockSpec((tm, d), lambda i: (i, 0))
    htile = pl.BlockSpec((tm, half), lambda i: (i, 0))
    heads = pl.BlockSpec((nh, tm, HEAD_DIM), lambda i: (0, i, 0))
    hshape = jax.ShapeDtypeStruct((nh, t, HEAD_DIM), F32)
    return pl.pallas_call(
        body, name="mixout_bwd", grid=(t // tm,),
        out_shape=(hshape, hshape, jax.ShapeDtypeStruct((t, half), F32), jax.ShapeDtypeStruct(w.shape, F32)),
        in_specs=[heads, heads, htile, VMEM_FULL, tile, ANY],
        out_specs=(heads, heads, htile, pl.BlockSpec(w.shape, lambda i: (0, 0))),
        compiler_params=_params("arbitrary"),
    )(att, opg, gate, w, dy, dep)


def _loss_head(y, target):
    t, d = y.shape
    tm = TOKEN_TILE

    def body(y_ref, t_ref, dy_ref, loss_ref):
        i = pl.program_id(0)
        err = y_ref[...] - t_ref[...]
        dy_ref[...] = err * (1.0 / d)
        part = 0.5 * jnp.sum(jnp.mean(err * err, axis=-1, keepdims=True), axis=0, keepdims=True)

        @pl.when(i == 0)
        def _():
            loss_ref[...] = jnp.zeros_like(loss_ref)

        loss_ref[...] += jnp.broadcast_to(part, loss_ref.shape)

    tile = pl.BlockSpec((tm, d), lambda i: (i, 0))
    return pl.pallas_call(
        body, name="loss_head", grid=(t // tm,),
        out_shape=(jax.ShapeDtypeStruct((t, d), F32), jax.ShapeDtypeStruct((1, 128), F32)),
        in_specs=[tile, tile], out_specs=(tile, pl.BlockSpec((1, 128), lambda i: (0, 0))),
        compiler_params=_params("arbitrary"),
    )(y, target)


def _head_norm(x, gain):
    return x * _rms(x) * gain


def _att_pattern(qh, kh, v, nb):
    g, blk, _ = qh.shape
    scale = HEAD_DIM ** -0.5
    qi = lax.broadcasted_iota(jnp.int32, (blk, blk), 0)
    kj = lax.broadcasted_iota(jnp.int32, (blk, blk), 1)
    sc = jnp.where(kj <= qi, _bmm_nt(qh, kh) * scale, NEG_INF)
    top = jnp.max(sc, axis=-1, keepdims=True)
    if nb > 1:
        khp = jnp.concatenate([kh[:1], kh[:-1]], axis=0)
        vp = jnp.concatenate([v[:1], v[:-1]], axis=0)
        has_prev = lax.broadcasted_iota(jnp.int32, (g, 1, 1), 0) % nb != 0
        sp = jnp.where((kj >= qi) & has_prev, _bmm_nt(qh, khp) * scale, NEG_INF)
        top = jnp.maximum(top, jnp.max(sp, axis=-1, keepdims=True))
    m = lax.stop_gradient(top)
    pc = jnp.exp(sc - m)
    den = jnp.sum(pc, axis=-1, keepdims=True)
    acc = _bmm_nn(pc, v)
    if nb > 1:
        pp = jnp.exp(sp - m)
        den = den + jnp.sum(pp, axis=-1, keepdims=True)
        acc = acc + _bmm_nn(pp, vp)
    o = acc / den
    return o, jnp.broadcast_to(m + jnp.log(den), o.shape)


def _pattern_rows(t, dil):
    nb = t // (ATT_BLOCK * dil)
    starts = [n * ATT_BLOCK * dil + r for r in range(dil) for n in range(nb)]
    return [pl.ds(s, ATT_BLOCK, stride=dil) if dil > 1 else pl.ds(s, ATT_BLOCK) for s in starts], nb


def _take(ref, rows):
    return jnp.stack([ref[0, r, :] for r in rows])


def _put(ref, rows, val):
    for g, r in enumerate(rows):
        ref[0, r, :] = val[g]


def _put_add(ref, rows, val):
    for g, r in enumerate(rows):
        ref[0, r, :] += val[g]


def _merge_fn(o1, o2, o3, l1, l2, l3):
    m = lax.stop_gradient(jnp.maximum(jnp.maximum(l1, l2), l3))
    e1, e2, e3 = jnp.exp(l1 - m), jnp.exp(l2 - m), jnp.exp(l3 - m)
    return (e1 * o1 + e2 * o2 + e3 * o3) / (e1 + e2 + e3)


def _token_rows(j):
    return pl.ds(pl.multiple_of(j * ATT_BLOCK, ATT_BLOCK), ATT_BLOCK)


def _norm_rows(t, q_ref, k_ref, gq, gk, qh_ref, kh_ref):
    def step(j, carry):
        rows = _token_rows(j)
        qh_ref[0, rows, :] = _head_norm(q_ref[0, rows, :], gq[0])
        kh_ref[0, rows, :] = _head_norm(k_ref[0, rows, :], gk[0])
        return carry

    lax.fori_loop(0, t // ATT_BLOCK, step, 0)


def _att_head_specs(t):
    head = pl.BlockSpec((1, t, HEAD_DIM), lambda h: (h, 0, 0))
    gain = pl.BlockSpec((1, 1, HEAD_DIM), lambda h: (0, 0, 0))
    return head, gain


def _att_fwd(q, k, v, qn, kn):
    nh, t, dh = q.shape
    head, gain = _att_head_specs(t)

    def body(q_ref, k_ref, v_ref, qn_ref, kn_ref, att_ref, o1, o2, o3, l1, l2, l3, qh_ref, kh_ref):
        saved = (o1, o2, o3, l1, l2, l3)
        _norm_rows(t, q_ref, k_ref, qn_ref[...], kn_ref[...], qh_ref, kh_ref)
        for p, dil in enumerate(DILATIONS):
            rows, nb = _pattern_rows(t, dil)
            o, lse = _att_pattern(_take(qh_ref, rows), _take(kh_ref, rows), _take(v_ref, rows), nb)
            _put(saved[p], rows, o)
            _put(saved[3 + p], rows, lse)

        def merge(j, carry):
            rows = _token_rows(j)
            att_ref[0, rows, :] = _merge_fn(*[r[0, rows, :] for r in saved])
            return carry

        lax.fori_loop(0, t // ATT_BLOCK, merge, 0)

    return pl.pallas_call(
        body, name="att_fwd", grid=(nh,), out_shape=(jax.ShapeDtypeStruct(q.shape, F32),) * 7,
        in_specs=[head, head, head, gain, gain], out_specs=(head,) * 7,
        scratch_shapes=[pltpu.VMEM((1, t, dh), F32)] * 2, compiler_params=_params("arbitrary"),
    )(q, k, v, qn, kn)


def _att_bwd(q, k, v, qn, kn, saved, datt):
    nh, t, dh = q.shape
    head, gain = _att_head_specs(t)

    def body(q_ref, k_ref, v_ref, qn_ref, kn_ref, o1, o2, o3, l1, l2, l3, datt_ref,
             dq_ref, dk_ref, dv_ref, dqn_ref, dkn_ref, qh_ref, kh_ref, dqh_ref, dkh_ref, *ct_refs):
        for ref in (dqh_ref, dkh_ref, dv_ref):
            ref[...] = jnp.zeros_like(ref)

        @pl.when(pl.program_id(0) == 0)
        def _():
            dqn_ref[...] = jnp.zeros_like(dqn_ref)
            dkn_ref[...] = jnp.zeros_like(dkn_ref)

        gq, gk = qn_ref[...], kn_ref[...]
        _norm_rows(t, q_ref, k_ref, gq, gk, qh_ref, kh_ref)

        def merge_cotangents(j, carry):
            rows = _token_rows(j)
            _, merge_vjp = jax.vjp(_merge_fn, *[r[0, rows, :] for r in (o1, o2, o3, l1, l2, l3)])
            for ref, val in zip(ct_refs, merge_vjp(datt_ref[0, rows, :])):
                ref[0, rows, :] = val
            return carry

        lax.fori_loop(0, t // ATT_BLOCK, merge_cotangents, 0)

        for p, dil in enumerate(DILATIONS):
            rows, nb = _pattern_rows(t, dil)
            _, pattern_vjp = jax.vjp(functools.partial(_att_pattern, nb=nb), _take(qh_ref, rows), _take(kh_ref, rows),
                                     _take(v_ref, rows))
            dqh, dkh, dv = pattern_vjp((_take(ct_refs[p], rows), _take(ct_refs[3 + p], rows)))
            _put_add(dqh_ref, rows, dqh)
            _put_add(dkh_ref, rows, dkh)
            _put_add(dv_ref, rows, dv)

        def norm_cotangents(j, carry):
            rows = _token_rows(j)
            out = []
            for x_ref, gain, dh_ref, dx_ref, acc in ((q_ref, gq, dqh_ref, dq_ref, carry[0]), (k_ref, gk, dkh_ref, dk_ref, carry[1])):
                _, norm_vjp = jax.vjp(_head_norm, x_ref[0, rows, :], gain[0])
                dx, dgain = norm_vjp(dh_ref[0, rows, :])
                dx_ref[0, rows, :] = dx
                out.append(acc + dgain)
            return tuple(out)

        zero = jnp.zeros((1, dh), F32)
        dgq, dgk = lax.fori_loop(0, t // ATT_BLOCK, norm_cotangents, (zero, zero))
        dqn_ref[0] += dgq
        dkn_ref[0] += dgk

    hshape = jax.ShapeDtypeStruct(q.shape, F32)
    gshape = jax.ShapeDtypeStruct((1, 1, dh), F32)
    return pl.pallas_call(
        body, name="att_bwd", grid=(nh,), out_shape=(hshape, hshape, hshape, gshape, gshape),
        in_specs=[head, head, head, gain, gain] + [head] * 7, out_specs=(head, head, head, gain, gain),
        scratch_shapes=[pltpu.VMEM((1, t, dh), F32)] * 10, compiler_params=_params("arbitrary"),
    )(q, k, v, qn, kn, *saved, datt)


RWKV_VEC = ("mu_r", "mu_k", "mu_v", "mu_w", "mu_a", "mu_g", "w0", "a0", "k_k", "k_a")
RWKV_MAT = ("w1", "w2", "a1", "a2", "g1", "g2")


def _rwkv_pre_fn(cur, prev, vec, w1, w2, a1, a2, g1, g2):
    c = cur.shape[1] // 4
    mu_r, mu_k, mu_v, mu_w, mu_a, mu_g, w0, a0, k_k, k_a = (vec[j:j + 1] for j in range(10))

    def lerp(j, mu):
        xc, xp = cur[:, j * c:(j + 1) * c], prev[:, j * c:(j + 1) * c]
        return xc + (xp - xc) * mu

    r, k, v = lerp(0, mu_r), lerp(1, mu_k), lerp(2, mu_v)
    cw, ca, cg = lerp(3, mu_w), lerp(3, mu_a), lerp(3, mu_g)
    z = w0 + _mm(jnp.tanh(_mm(cw, w1)), w2)
    w_log = jnp.minimum(z, 0.0) - jnp.log(1.0 + jnp.exp(-jnp.abs(z))) - 0.5
    lw = -jnp.exp(w_log)
    a = _sigmoid(a0 + _mm(_mm(ca, a1), a2))
    gate = _mm(_sigmoid(_mm(cg, g1)), g2)
    kkraw = k * k_k
    kmod = k * (1.0 + (a - 1.0) * k_a)
    return r, lw, kmod, v, kkraw, a, gate


HALO_ROWS = 8


def _rwkv_pre_specs(c, mats, tile_of):
    tm = TOKEN_TILE
    nh = c // HEAD_DIM
    wide = pl.BlockSpec((tm, 4 * c), lambda j: (tile_of(j), 0))
    halo = pl.BlockSpec((HALO_ROWS, 4 * c), lambda j: (jnp.maximum(tile_of(j) * (tm // HALO_ROWS) - 1, 0), 0))
    one = pl.BlockSpec((tm, c), lambda j: (tile_of(j), 0))
    heads = pl.BlockSpec((nh, tm, HEAD_DIM), lambda j: (0, tile_of(j), 0))
    vec = pl.BlockSpec((10, c), lambda j: (0, 0))
    mspecs = [pl.BlockSpec(m.shape, lambda j: (0, 0)) for m in mats]
    return wide, halo, one, heads, vec, mspecs


def _previous_rows(cur, halo, tile):
    first = jnp.where(tile > 0, halo[HALO_ROWS - 1:HALO_ROWS], 0.0)
    rows = lax.broadcasted_iota(jnp.int32, cur.shape, 0)
    return jnp.where(rows == 0, first, pltpu.roll(cur, 1, axis=0))


def _rwkv_pre_fwd(cur, vec, mats):
    t, c4 = cur.shape
    c = c4 // 4
    wide, halo, one, heads, vspec, mspecs = _rwkv_pre_specs(c, mats, lambda j: j)

    def body(cur_ref, halo_ref, vec_ref, *rest):
        mrefs, outs = rest[:6], rest[6:]
        cur_v = cur_ref[...]
        prev = _previous_rows(cur_v, halo_ref[...], pl.program_id(0))
        vals = _rwkv_pre_fn(cur_v, prev, vec_ref[...], *(m[...] for m in mrefs))
        for ref, val in zip(outs[:6], vals[:6]):
            _store_heads(ref, val)
        outs[6][...] = vals[6]

    hshape = jax.ShapeDtypeStruct((c // HEAD_DIM, t, HEAD_DIM), F32)
    return pl.pallas_call(
        body, name="rwkv_pre_fwd", grid=(t // TOKEN_TILE,), out_shape=(hshape,) * 6 + (jax.ShapeDtypeStruct((t, c), F32),),
        in_specs=[wide, halo, vspec] + mspecs, out_specs=(heads,) * 6 + (one,), compiler_params=_params("arbitrary"),
    )(cur, cur, vec, *mats)


def _rwkv_pre_bwd(cur, vec, mats, cts, dgate):
    t, c4 = cur.shape
    c = c4 // 4
    tm = TOKEN_TILE
    nt = t // tm
    wide, halo, one, heads, vspec, mspecs = _rwkv_pre_specs(c, mats, lambda j: nt - 1 - j)

    def body(cur_ref, halo_ref, vec_ref, *rest):
        mrefs, ctrefs, dgate_ref, outs, carry_ref = rest[:6], rest[6:12], rest[12], rest[13:-1], rest[-1]
        j = pl.program_id(0)

        @pl.when(j == 0)
        def _():
            carry_ref[...] = jnp.zeros_like(carry_ref)
            for ref in outs[1:]:
                ref[...] = jnp.zeros_like(ref)

        cur_v = cur_ref[...]
        prev = _previous_rows(cur_v, halo_ref[...], nt - 1 - j)
        _, vjp = jax.vjp(_rwkv_pre_fn, cur_v, prev, vec_ref[...], *(m[...] for m in mrefs))
        grads = vjp(tuple(_load_heads(r) for r in ctrefs) + (dgate_ref[...],))
        dprev = grads[1]
        rows = lax.broadcasted_iota(jnp.int32, dprev.shape, 0)
        outs[0][...] = grads[0] + jnp.where(rows == tm - 1, carry_ref[0:1], pltpu.roll(dprev, tm - 1, axis=0))
        carry_ref[0:1] = dprev[0:1]
        for ref, val in zip(outs[1:], grads[2:]):
            ref[...] += val

    return pl.pallas_call(
        body, name="rwkv_pre_bwd", grid=(nt,),
        out_shape=(jax.ShapeDtypeStruct(cur.shape, F32), jax.ShapeDtypeStruct(vec.shape, F32))
        + tuple(jax.ShapeDtypeStruct(m.shape, F32) for m in mats),
        in_specs=[wide, halo, vspec] + mspecs + [heads] * 6 + [one], out_specs=(wide, vspec) + tuple(mspecs),
        scratch_shapes=[pltpu.VMEM((HALO_ROWS, c4), F32)], compiler_params=_params("arbitrary"),
    )(cur, cur, vec, *mats, *cts, dgate)


def _scan_chunk_fn(h0, r, lw, k, v, kkraw, a, rk, lnw, lnb):
    n = r.shape[1]
    nrm = jnp.sqrt(jnp.sum(kkraw * kkraw, axis=-1, keepdims=True))
    kk = kkraw / jnp.maximum(nrm, 1e-12)
    av, bv = -kk, kk * a
    ti = lax.broadcasted_iota(jnp.int32, (n, n), 0)
    si = lax.broadcasted_iota(jnp.int32, (n, n), 1)
    incl, strict = ti >= si, ti > si
    ones = jnp.broadcast_to(incl.astype(F32)[None], (r.shape[0], n, n))
    cum = _hdot(ones, lw, 2, 1)
    at, rt = av * jnp.exp(cum - lw), r * jnp.exp(cum)
    inv = jnp.exp(-cum)
    bt, kt = bv * inv, k * inv
    gram = _hdot(jnp.concatenate([at, rt], axis=1), jnp.concatenate([bt, kt], axis=1), 2, 2)
    lab = jnp.where(strict, gram[:, :n, :n], 0.0)
    lak = jnp.where(strict, gram[:, :n, n:], 0.0)
    rb = jnp.where(incl, gram[:, n:, :n], 0.0)
    rkm = jnp.where(incl, gram[:, n:, n:], 0.0)
    nv = v.shape[2]
    u = _bmm_nn(jnp.concatenate([at, lak], axis=2), jnp.concatenate([h0, v], axis=1))
    p = lab
    m = 2
    while m < n:
        both = _bmm_nn(p, jnp.concatenate([u, p], axis=2))
        u, p = u + both[:, :, :nv], both[:, :, nv:]
        m *= 2
    u = u + _bmm_nn(p, u)
    y = _bmm_nn(jnp.concatenate([rt, rb, rkm], axis=2), jnp.concatenate([h0, u, v], axis=1))
    last = jnp.exp(jnp.sum(lw, axis=1, keepdims=True))
    h1 = jnp.swapaxes(last, 1, 2) * (h0 + _bmm_tn(jnp.concatenate([bt, kt], axis=1), jnp.concatenate([u, v], axis=1)))
    mean = jnp.mean(y, axis=-1, keepdims=True)
    yc = y - mean
    var = jnp.mean(yc * yc, axis=-1, keepdims=True)
    yn = yc * lax.rsqrt(var + GN_EPS) * lnw + lnb
    bonus = jnp.sum(r * k * rk, axis=-1, keepdims=True) * v
    return yn + bonus, h1


SCAN_GROUP = 2


def _scan_group_fn(h0, r, lw, k, v, kkraw, a, rk, lnw, lnb):
    outs = []
    for j in range(SCAN_GROUP):
        rows = slice(j * SCAN_CHUNK, (j + 1) * SCAN_CHUNK)
        o, h0 = _scan_chunk_fn(h0, r[:, rows], lw[:, rows], k[:, rows], v[:, rows], kkraw[:, rows], a[:, rows], rk, lnw, lnb)
        outs.append(o)
    return jnp.concatenate(outs, axis=1), h0


def _scan_specs(h, t, dh, rev):
    n = SCAN_CHUNK * SCAN_GROUP
    nc = t // n
    pos = (lambda c: (0, nc - 1 - c, 0)) if rev else (lambda c: (0, c, 0))
    st = (lambda c: (nc - 1 - c, 0, 0, 0)) if rev else (lambda c: (c, 0, 0, 0))
    seq = pl.BlockSpec((h, n, dh), pos)
    par = pl.BlockSpec((h, 1, dh), lambda c: (0, 0, 0))
    state = pl.BlockSpec((1, h, dh, dh), st)
    return seq, par, state


def _scan_fwd(seqs, pars):
    h, t, dh = seqs[0].shape
    nc = t // (SCAN_CHUNK * SCAN_GROUP)
    seq, par, state = _scan_specs(h, t, dh, False)

    def body(r, lw, k, v, kkraw, a, rk, lnw, lnb, o_ref, st_ref, h_ref):
        @pl.when(pl.program_id(0) == 0)
        def _():
            h_ref[...] = jnp.zeros_like(h_ref)

        h0 = h_ref[...]
        st_ref[0] = h0
        o, h1 = _scan_group_fn(h0, r[...], lw[...], k[...], v[...], kkraw[...], a[...], rk[...], lnw[...], lnb[...])
        o_ref[...] = o
        h_ref[...] = h1

    return pl.pallas_call(
        body, name="rwkv_scan_fwd", grid=(nc,),
        out_shape=(jax.ShapeDtypeStruct((h, t, dh), F32), jax.ShapeDtypeStruct((nc, h, dh, dh), F32)),
        in_specs=[seq] * 6 + [par] * 3, out_specs=(seq, state),
        scratch_shapes=[pltpu.VMEM((h, dh, dh), F32)], compiler_params=_params("arbitrary"),
    )(*seqs, *pars)


def _scan_bwd(seqs, pars, states, do):
    h, t, dh = seqs[0].shape
    nc = t // (SCAN_CHUNK * SCAN_GROUP)
    seq, par, state = _scan_specs(h, t, dh, True)

    def body(r, lw, k, v, kkraw, a, rk, lnw, lnb, st_ref, do_ref, *rest):
        douts, dpars, dh_ref = rest[:6], rest[6:9], rest[9]
        first = pl.program_id(0) == 0

        @pl.when(first)
        def _():
            dh_ref[...] = jnp.zeros_like(dh_ref)

        _, vjp = jax.vjp(_scan_group_fn, st_ref[0], r[...], lw[...], k[...], v[...], kkraw[...], a[...],
                         rk[...], lnw[...], lnb[...])
        grads = vjp((do_ref[...], dh_ref[...]))
        dh_ref[...] = grads[0]
        for ref, val in zip(douts, grads[1:7]):
            ref[...] = val

        @pl.when(first)
        def _():
            for ref, val in zip(dpars, grads[7:]):
                ref[...] = val

        @pl.when(jnp.logical_not(first))
        def _():
            for ref, val in zip(dpars, grads[7:]):
                ref[...] += val

    sshape = jax.ShapeDtypeStruct((h, t, dh), F32)
    pshape = jax.ShapeDtypeStruct((h, 1, dh), F32)
    return pl.pallas_call(
        body, name="rwkv_scan_bwd", grid=(nc,), out_shape=(sshape,) * 6 + (pshape,) * 3,
        in_specs=[seq] * 6 + [par] * 3 + [state, seq], out_specs=(seq,) * 6 + (par,) * 3,
        scratch_shapes=[pltpu.VMEM((h, dh, dh), F32)], compiler_params=_params("arbitrary"),
    )(*seqs, *pars, states, do)


def _local_step(x, target, w, ex):
    w = dict(w)
    c = w["mu_r"].shape[-1]
    qn, kn = w["q_norm"].reshape(1, 1, HEAD_DIM), w["k_norm"].reshape(1, 1, HEAD_DIM)
    vec = jnp.concatenate([w[n].reshape(1, c) for n in RWKV_VEC], axis=0)
    pars = [w[n].reshape(-1, 1, HEAD_DIM) for n in ("r_k", "ln_x_w", "ln_x_b")]
    no_dep = jnp.zeros(DEP_SHAPE, F32)

    x1, gate1, up1 = _ffn_fwd(x, w["ffn1_norm"], w["ffn1_w_gate"], w["ffn1_w_up"], w["ffn1_w_down"], ex.first_dep, "ffn1_fwd")
    w.update(ex.mix_weights((x1,)))
    mats = [w[n] for n in RWKV_MAT]
    q, k, v, cur = _proj_fwd(x1, w["mix_norm"], w["w_in"])
    att, *saved = _att_fwd(q, k, v, qn, kn)
    pre = _rwkv_pre_fwd(cur, vec, mats)
    seqs, gate = pre[:6], pre[6]
    opg, states = _scan_fwd(seqs, pars)
    w.update(ex.out_weights((att, opg)))
    x2 = _mixout_fwd(x1, att, opg, gate, w["w_out"])
    x3, gate2, up2 = _ffn_fwd(x2, w["ffn2_norm"], w["ffn2_w_gate"], w["ffn2_w_up"], w["ffn2_w_down"], no_dep, "ffn2_fwd")
    dy, loss = _loss_head(x3, target)

    g = {}
    dx2, g["ffn2_norm"], g["ffn2_w_gate"], g["ffn2_w_up"], g["ffn2_w_down"] = _ffn_bwd(
        x2, w["ffn2_norm"], w["ffn2_w_gate"], w["ffn2_w_up"], w["ffn2_w_down"], gate2, up2, dy, no_dep, "ffn2_bwd")
    dep = ex.send_ffn2({n: g[n] for n in ("ffn2_w_gate", "ffn2_w_up", "ffn2_w_down")})
    datt, dopg, dgate, g["w_out"] = _mixout_bwd(att, opg, gate, w["w_out"], dx2, dep)
    dscan = _scan_bwd(seqs, pars, states, dopg)
    for n, d in zip(("r_k", "ln_x_w", "ln_x_b"), dscan[6:]):
        g[n] = d
    dcur, dvec, *dmats = _rwkv_pre_bwd(cur, vec, mats, dscan[:6], dgate)
    for n, d in zip(RWKV_MAT, dmats):
        g[n] = d
    for j, n in enumerate(RWKV_VEC):
        g[n] = dvec[j:j + 1]
    dq, dk, dv, g["q_norm"], g["k_norm"] = _att_bwd(q, k, v, qn, kn, saved, datt)
    dx1, g["mix_norm"], g["w_in"] = _proj_bwd(x1, w["mix_norm"], w["w_in"], dq, dk, dv, dcur, dx2)
    dep = ex.send_mix({n: g[n] for n in ("w_in", "w_out") + RWKV_MAT}, (dx1,))
    dx, g["ffn1_norm"], g["ffn1_w_gate"], g["ffn1_w_up"], g["ffn1_w_down"] = _ffn_bwd(
        x, w["ffn1_norm"], w["ffn1_w_gate"], w["ffn1_w_up"], w["ffn1_w_down"], gate1, up1, dx1, dep, "ffn1_bwd")
    return loss, dx, g


N_SHARDS = 4


def _place():
    return lax.axis_index("x"), lax.axis_index("y"), lax.axis_index("c")


def _chip_peers(x, y):
    return [(1 - x, y), (x, 1 - y), (1 - x, 1 - y)]


HBM = pl.BlockSpec(memory_space=pltpu.HBM)
SEM = pl.BlockSpec(memory_space=pltpu.SEMAPHORE)
DEP_SHAPE = (8, 128)


class _Views:
    to_sibling = False


class _GatherViews(_Views):
    @staticmethod
    def send(i, srcs, lands, k, at):
        return srcs[i], lands[i].at[at[3]]

    @staticmethod
    def landing(i, srcs, lands, k, at):
        return srcs[i], lands[i].at[2 * at[4] + at[5]]


class _ScatterViews(_Views):
    @staticmethod
    def send(i, srcs, lands, k, at):
        return srcs[i].at[2 * at[4] + at[5]], lands[i].at[k]

    @staticmethod
    def landing(i, srcs, lands, k, at):
        return srcs[i].at[at[3]], lands[i].at[k]


def _half_rows(ref, slot, half):
    rows = ref.shape[1] // 2
    return ref.at[slot, pl.ds(pl.multiple_of(half * rows, BF16_SUBLANES), rows)]


class _HalfGatherViews(_Views):
    @staticmethod
    def send(i, srcs, lands, k, at):
        rows = srcs[i].shape[0] // 2
        return srcs[i].at[pl.ds(pl.multiple_of(at[2] * rows, BF16_SUBLANES), rows)], _half_rows(lands[i], at[3], at[2])

    @staticmethod
    def landing(i, srcs, lands, k, at):
        rows = srcs[i].shape[0] // 2
        return srcs[i].at[pl.ds(pl.multiple_of(at[2] * rows, BF16_SUBLANES), rows)], _half_rows(lands[i], 2 * at[4] + at[5], at[2])


class _ForwardViews(_Views):
    to_sibling = True

    @staticmethod
    def send(i, srcs, lands, k, at):
        mine = _half_rows(lands[i], 2 * at[4] + at[5], at[2])
        return mine, mine

    @staticmethod
    def landing(i, srcs, lands, k, at):
        theirs = _half_rows(lands[i], 2 * at[4] + at[5], 1 - at[2])
        return theirs, theirs


def _push_start(srcs, lands, views, after, name):
    ns, nl = len(srcs), len(lands)

    def body(*refs):
        src_refs, land_refs = refs[:ns], refs[ns:ns + nl]
        send_sems, recv_sems = refs[ns + nl + 1:ns + nl + 3]
        token = refs[2 * (ns + nl) + 3]
        x, y, c = _place()
        for i in range(nl):
            for k, (px, py) in enumerate(_chip_peers(x, y)):
                src, dst = views.send(i, src_refs, land_refs, k, (x, y, c, 2 * x + y, px, py))
                pltpu.make_async_remote_copy(
                    src_ref=src, dst_ref=dst, send_sem=send_sems.at[3 * i + k], recv_sem=recv_sems.at[3 * i + k],
                    device_id=(x, y, 1 - c) if views.to_sibling else (px, py, c), device_id_type=MESH).start()
        token[...] = jnp.zeros_like(token)

    sems = pltpu.SemaphoreType.DMA((3 * nl,))
    both = [pltpu.with_memory_space_constraint(a, pltpu.HBM) for a in (*srcs, *lands)]
    outs = pl.pallas_call(
        body, name=name,
        out_shape=(sems, sems, *[pltpu.HBM(a.shape, a.dtype) for a in both], jax.ShapeDtypeStruct(DEP_SHAPE, F32)),
        in_specs=[HBM] * (ns + nl) + [ANY], out_specs=(SEM, SEM, *[HBM] * (ns + nl), VMEM_FULL),
        input_output_aliases={i: 2 + i for i in range(ns + nl)},
        compiler_params=pltpu.CompilerParams(has_side_effects=pltpu.SideEffectType.DATAFLOW_SIDE_EFFECTING),
    )(*both, after)
    return outs[0], outs[1], outs[2:2 + ns], outs[2 + ns:2 + ns + nl], outs[2 + ns + nl]


def _push_wait(started, views, after, name, with_sources=False):
    send_sems, recv_sems, srcs, lands, _ = started
    ns, nl = len(srcs), len(lands)

    def body(*refs):
        src_refs, land_refs = refs[:ns], refs[ns:ns + nl]
        send_sems, recv_sems = refs[ns + nl:ns + nl + 2]
        x, y, c = _place()
        for i in range(nl):
            for k, (px, py) in enumerate(_chip_peers(x, y)):
                src, dst = views.landing(i, src_refs, land_refs, k, (x, y, c, 2 * x + y, px, py))
                landing = pltpu.make_async_remote_copy(
                    src_ref=src, dst_ref=dst, send_sem=send_sems.at[3 * i + k], recv_sem=recv_sems.at[3 * i + k],
                    device_id=(x, y, 1 - c) if views.to_sibling else (px, py, c), device_id_type=MESH)
                landing.wait_send()
                landing.wait_recv()

    outs = pl.pallas_call(
        body, name=name,
        out_shape=tuple(pltpu.HBM(a.shape, a.dtype) for a in (*srcs, *lands)),
        in_specs=[HBM] * (ns + nl) + [SEM, SEM] + [ANY] * len(after), out_specs=(HBM,) * (ns + nl),
        input_output_aliases={i: i for i in range(ns + nl)},
        compiler_params=pltpu.CompilerParams(has_side_effects=pltpu.SideEffectType.DATAFLOW_SIDE_EFFECTING),
    )(*srcs, *lands, send_sems, recv_sems, *after)
    return outs if with_sources else outs[ns:]


def _empty_lands(shards, slots, own_slot):
    lands = [lax.empty((slots,) + s.shape, s.dtype) for s in shards]
    if own_slot:
        me = 2 * lax.axis_index("x") + lax.axis_index("y")
        lands = [lax.dynamic_update_index_in_dim(z, s, me, 0) for z, s in zip(lands, shards)]
    return lands


def _sibling_swap(arrays, name):
    n = len(arrays)

    def body(*refs):
        ins, outs = refs[:n], refs[n:2 * n]
        send_sems, recv_sems = refs[2 * n:]
        x, y, c = _place()
        copies = []
        for i in range(n):
            cp = pltpu.make_async_remote_copy(
                src_ref=ins[i], dst_ref=outs[i], send_sem=send_sems.at[i], recv_sem=recv_sems.at[i],
                device_id=(x, y, 1 - c), device_id_type=MESH)
            cp.start()
            copies.append(cp)
        for cp in copies:
            cp.wait()

    return pl.pallas_call(
        body, name=name,
        out_shape=tuple(jax.ShapeDtypeStruct(a.shape, a.dtype) for a in arrays),
        in_specs=[ANY] * n, out_specs=(ANY,) * n,
        scratch_shapes=[pltpu.SemaphoreType.DMA((n,)), pltpu.SemaphoreType.DMA((n,))],
    )(*arrays)


N_DEV = 8


def _allreduce_small(pack):
    def body(in_ref, out_ref, buf, send_sems, recv_sems):
        x, y, c = _place()
        me = 4 * x + 2 * y + c
        buf[me] = in_ref[...]

        def copy(j, slot):
            px, py, pc = x ^ (j >> 2), y ^ ((j >> 1) & 1), c ^ (j & 1)
            return pltpu.make_async_remote_copy(
                src_ref=in_ref, dst_ref=buf.at[slot(px, py, pc)], send_sem=send_sems.at[j], recv_sem=recv_sems.at[j],
                device_id=(px, py, pc), device_id_type=MESH)

        for j in range(1, N_DEV):
            copy(j, lambda px, py, pc: me).start()
        for j in range(1, N_DEV):
            landing = copy(j, lambda px, py, pc: 4 * px + 2 * py + pc)
            landing.wait_send()
            landing.wait_recv()
        acc = buf[0]
        for s in range(1, N_DEV):
            acc = acc + buf[s]
        out_ref[...] = acc

    return pl.pallas_call(
        body, name="allreduce_small", out_shape=jax.ShapeDtypeStruct(pack.shape, F32),
        in_specs=[VMEM_FULL], out_specs=VMEM_FULL,
        scratch_shapes=[pltpu.VMEM((N_DEV,) + pack.shape, F32), pltpu.SemaphoreType.DMA((N_DEV,)),
                        pltpu.SemaphoreType.DMA((N_DEV,))],
    )(pack)


ROW_TILE_MAX = 256
BF16_SUBLANES = 16


def _row_tile(rows):
    for tr in range(min(rows, ROW_TILE_MAX), 0, -1):
        if rows % tr == 0 and tr % BF16_SUBLANES == 0:
            return tr
    return rows


def _reduce_own(me, part, recv, dep, name):
    _, r, cols = part.shape
    tr = _row_tile(r)

    def body(me_ref, p_ref, rv_ref, dep_ref, o_ref):
        acc = p_ref[0].astype(F32)
        for k in range(3):
            acc = acc + rv_ref[k].astype(F32)
        o_ref[...] = acc

    return pl.pallas_call(
        body, name=name, out_shape=jax.ShapeDtypeStruct((r, cols), F32),
        grid_spec=pltpu.PrefetchScalarGridSpec(
            num_scalar_prefetch=1, grid=(r // tr,),
            in_specs=[pl.BlockSpec((1, tr, cols), lambda i, me_ref: (me_ref[0], i, 0)),
                      pl.BlockSpec((3, tr, cols), lambda i, me_ref: (0, i, 0)), ANY],
            out_specs=pl.BlockSpec((tr, cols), lambda i, me_ref: (i, 0))),
        compiler_params=_params("arbitrary"),
    )(me, part, recv, dep)


def _adamw(w, ga, gb, m, v, name):
    r, cols = w.shape
    tr = _row_tile(r)
    c1 = 1.0 - ADAM_B1 ** ADAM_STEP
    c2 = 1.0 - ADAM_B2 ** ADAM_STEP

    def body(w_ref, ga_ref, gb_ref, m_ref, v_ref, g_out, d_out, m_out, v_out):
        g = ga_ref[...] + gb_ref[...]
        mn = ADAM_B1 * m_ref[...] + (1.0 - ADAM_B1) * g
        vn = ADAM_B2 * v_ref[...] + (1.0 - ADAM_B2) * (g * g)
        g_out[...] = g
        m_out[...] = mn
        v_out[...] = vn
        d_out[...] = -ADAM_LR * ((mn / c1) / (jnp.sqrt(vn / c2) + ADAM_EPS) + ADAM_WD * w_ref[...])

    tile = pl.BlockSpec((tr, cols), lambda i: (i, 0))
    shape = jax.ShapeDtypeStruct((r, cols), F32)
    return pl.pallas_call(
        body, name=name, grid=(r // tr,), out_shape=(shape,) * 4, in_specs=[tile] * 5, out_specs=(tile,) * 4,
        compiler_params=_params("arbitrary"),
    )(w, ga, gb, m, v)


PACK_COLS = 512


def _to_rows(a):
    flat = a.reshape(-1)
    pad = (-flat.shape[0]) % PACK_COLS
    return jnp.pad(flat, (0, pad)).reshape(-1, PACK_COLS)


def _pack(arrays, extra_rows=0):
    rows = [_to_rows(a) for a in arrays]
    n = sum(r.shape[0] for r in rows) + extra_rows
    pad = (-n) % 8
    return jnp.concatenate(rows + [jnp.zeros((extra_rows + pad, PACK_COLS), F32)], axis=0)


def _unpack(pack, like):
    out, at = [], 0
    for a in like:
        n = -(-a.size // PACK_COLS)
        out.append(pack[at:at + n].reshape(-1)[:a.size].reshape(a.shape))
        at += n
    return out


COL_SHARDED = ("ffn1_w_gate", "ffn1_w_up", "w_in", "ffn2_w_gate", "ffn2_w_up", "w2", "a2", "g2")
ROW_SHARDED = ("ffn1_w_down", "ffn2_w_down", "w_out", "w1", "a1", "g1")
CHUNKED = ("ffn1_w_gate", "ffn1_w_up", "ffn1_w_down", "ffn2_w_gate", "ffn2_w_up", "ffn2_w_down")
WEIGHTS = ("ffn1_norm", "ffn1_w_gate", "ffn1_w_up", "ffn1_w_down", "mix_norm", "w_in", "q_norm", "k_norm",
           "mu_r", "mu_k", "mu_v", "mu_w", "mu_a", "mu_g", "w0", "w1", "w2", "a0", "a1", "a2", "g1", "g2",
           "k_k", "k_a", "r_k", "ln_x_w", "ln_x_b", "w_out", "ffn2_norm", "ffn2_w_gate", "ffn2_w_up", "ffn2_w_down")


W_IN_GROUPS = 7
TRANSPOSED = ("ffn1_w_gate", "ffn1_w_up", "ffn2_w_gate", "ffn2_w_up")


def _shard_2d(name, a):
    return a[0].T if name in TRANSPOSED else a[0]


def _full_from_blocks(name, blocks):
    if name in CHUNKED:
        return blocks
    if name in ROW_SHARDED:
        return blocks.reshape(-1, blocks.shape[-1])
    full = blocks.transpose(1, 0, 2).reshape(blocks.shape[1], -1)
    if name == "w_in":
        return full.reshape(full.shape[0], W_IN_GROUPS, -1).transpose(1, 0, 2)
    return full


def _blocks_from_full(name, full):
    if name in CHUNKED:
        return full
    if name in ROW_SHARDED:
        return full.reshape(N_SHARDS, -1, full.shape[-1])
    if name == "w_in":
        full = full.transpose(1, 0, 2).reshape(full.shape[1], -1)
    return full.reshape(full.shape[0], N_SHARDS, -1).transpose(1, 0, 2)


FFN1_GROUP = ("ffn1_w_gate", "ffn1_w_up", "ffn1_w_down")
MIX_GROUP = ("w_in",) + RWKV_MAT
OUT_GROUP = ("w_out", "ffn2_w_gate", "ffn2_w_up", "ffn2_w_down")
FFN2_GROUP = OUT_GROUP[1:]
LATE_GROUP = ("w_in", "w_out") + RWKV_MAT


class _Exchange:
    def __init__(self, given):
        self.given = given
        first = self._gather_start(FFN1_GROUP, _HalfGatherViews, jnp.zeros(DEP_SHAPE, F32), "gather_ffn1_start")
        self.mix = self._gather_start(MIX_GROUP, _GatherViews, first[4], "gather_mix_start")
        self.out = self._gather_start(OUT_GROUP, _GatherViews, self.mix[4], "gather_out_start")
        self.first_dep = self.out[4]
        halves = _push_wait(first, _HalfGatherViews, (self.first_dep,), "gather_ffn1_wait")
        passed = _push_start([], halves, _ForwardViews, halves[0], "gather_ffn1_pass_start")
        self.first_weights = self._full(FFN1_GROUP, _push_wait(passed, _ForwardViews, (passed[4],), "gather_ffn1_pass_wait"))
        self.parts, self.recv = {}, {}

    def _shards(self, names):
        return [_shard_2d(n, self.given[n]).astype(BF16) for n in names]

    @staticmethod
    def _full(names, blocks):
        out = {}
        for n, b in zip(names, blocks):
            full = _full_from_blocks(n, b)
            out[n] = full.astype(F32) if n in RWKV_MAT else full
        return out

    def _gather_start(self, names, views, after, name):
        shards = self._shards(names)
        return _push_start(shards, _empty_lands(shards, N_SHARDS, True), views, after, name)

    def mix_weights(self, after):
        return self._full(MIX_GROUP, _push_wait(self.mix, _GatherViews, after, "gather_mix_wait"))

    def out_weights(self, after):
        return self._full(OUT_GROUP, _push_wait(self.out, _GatherViews, after, "gather_out_wait"))

    def _scatter_start(self, grads, name):
        names = tuple(grads)
        parts = [_blocks_from_full(n, grads[n]) for n in names]
        self.parts.update(zip(names, parts))
        lands = [lax.empty((3,) + p.shape[1:], BF16) for p in parts]
        return _push_start([p.astype(BF16) for p in parts], lands, _ScatterViews, jnp.zeros(DEP_SHAPE, F32), name)

    def _scatter_done(self, started, names, after, name):
        outs = _push_wait(started, _ScatterViews, after, name, with_sources=True)
        for n, sent, got in zip(names, outs[:len(names)], outs[len(names):]):
            self.recv[n] = got
            if self.parts[n].dtype == BF16:
                self.parts[n] = sent

    def send_ffn2(self, grads):
        self.ffn2 = self._scatter_start(grads, "scatter_ffn2_start")
        return self.ffn2[4]

    def send_mix(self, grads, after):
        self._scatter_done(self.ffn2, FFN2_GROUP, after, "scatter_ffn2_wait")
        self.late = self._scatter_start(grads, "scatter_late_start")
        return self.late[4]

    def send_ffn1(self, grads):
        self.ffn1 = self._scatter_start(grads, "scatter_ffn1_start")
        return self.ffn1[4]

    def late_received(self, after):
        self._scatter_done(self.late, LATE_GROUP, after, "scatter_late_wait")

    def ffn1_received(self, after):
        self._scatter_done(self.ffn1, FFN1_GROUP, after, "scatter_ffn1_wait")


def kernel(
        x, ffn1_norm, ffn1_w_gate, ffn1_w_up, ffn1_w_down, mix_norm, w_in, q_norm, k_norm, mu_r, mu_k, mu_v, mu_w,
        mu_a, mu_g, w0, w1, w2, a0, a1, a2, g1, g2, k_k, k_a, r_k, ln_x_w, ln_x_b, w_out, ffn2_norm, ffn2_w_gate,
        ffn2_w_up, ffn2_w_down, loss_target, m_ffn1_norm, m_ffn1_w_gate, m_ffn1_w_up, m_ffn1_w_down, m_mix_norm,
        m_w_in, m_q_norm, m_k_norm, m_mu_r, m_mu_k, m_mu_v, m_mu_w, m_mu_a, m_mu_g, m_w0, m_w1, m_w2, m_a0, m_a1,
        m_a2, m_g1, m_g2, m_k_k, m_k_a, m_r_k, m_ln_x_w, m_ln_x_b, m_w_out, m_ffn2_norm, m_ffn2_w_gate, m_ffn2_w_up,
        m_ffn2_w_down, v_ffn1_norm, v_ffn1_w_gate, v_ffn1_w_up, v_ffn1_w_down, v_mix_norm, v_w_in, v_q_norm, v_k_norm,
        v_mu_r, v_mu_k, v_mu_v, v_mu_w, v_mu_a, v_mu_g, v_w0, v_w1, v_w2, v_a0, v_a1, v_a2, v_g1, v_g2, v_k_k, v_k_a,
        v_r_k, v_ln_x_w, v_ln_x_b, v_w_out, v_ffn2_norm, v_ffn2_w_gate, v_ffn2_w_up, v_ffn2_w_down):
    given = dict(locals())
    sharded = COL_SHARDED + ROW_SHARDED
    sharded = tuple(n for n in WEIGHTS if n in sharded)
    small = tuple(n for n in WEIGHTS if n not in sharded)

    ex = _Exchange(given)
    w = {n: given[n] for n in small}
    w.update(ex.first_weights)
    loss, dx, g = _local_step(x[0], loss_target[0], w, ex)
    dep = ex.send_ffn1({n: g[n] for n in FFN1_GROUP})

    me = (2 * lax.axis_index("x") + lax.axis_index("y")).astype(jnp.int32).reshape(1)
    out = {}

    def settle(names, dep, tag):
        mine = []
        for n in names:
            p, rv = ex.parts[n], ex.recv[n]
            p2 = p.reshape(N_SHARDS, -1, p.shape[-1])
            mine.append(_reduce_own(me, p2, rv.reshape(3, -1, rv.shape[-1]), dep, f"reduce_{n}"))
        theirs = _sibling_swap(mine, f"sibling_swap_{tag}")
        for n, a, b in zip(names, mine, theirs):
            shape = given[n].shape
            res = _adamw(_shard_2d(n, given[n]), a, b, _shard_2d(n, given["m_" + n]), _shard_2d(n, given["v_" + n]), f"adamw_{n}")
            out[n] = [(r.T if n in TRANSPOSED else r).reshape(shape) for r in res]
        return tuple(out[n][1] for n in names)

    ex.late_received((dep,))
    last = settle(tuple(n for n in sharded if n not in FFN1_GROUP), dep, "rest")

    gpack = _pack([g[n] for n in small], extra_rows=1)
    n_rows = sum(-(-given[n].size // PACK_COLS) for n in small)
    gpack = gpack.at[n_rows, :loss.shape[1]].set(loss[0])
    gsum = _allreduce_small(gpack)
    res = _adamw(_pack([given[n] for n in small], 1), gsum, jnp.zeros_like(gsum), _pack([given["m_" + n] for n in small], 1),
                 _pack([given["v_" + n] for n in small], 1), "adamw_small")
    like = [given[n] for n in small]
    for j, r in enumerate(res):
        for n, a in zip(small, _unpack(r, like)):
            out.setdefault(n, [None] * 4)[j] = a
    total_loss = gsum[n_rows, 0]

    ex.ffn1_received((*last, res[1]))
    settle(FFN1_GROUP, jnp.zeros(DEP_SHAPE, F32), "ffn1")
    return (total_loss, dx[None], *[out[n][0] for n in WEIGHTS], *[out[n][1] for n in WEIGHTS],
            *[out[n][2] for n in WEIGHTS], *[out[n][3] for n in WEIGHTS])
```

```python
import functools

import jax
import jax.numpy as jnp
from jax import lax
from jax.experimental import pallas as pl
from jax.experimental.pallas import tpu as pltpu

F32 = jnp.float32
BF16 = jnp.bfloat16
MESH = pl.DeviceIdType.MESH

RMS_EPS = 1e-6
GN_EPS = 64e-5
NEG_INF = -1e30
FFN_RESIDUAL = 0.5
HEAD_DIM = 64
ATT_BLOCK = 128
DILATIONS = (1, 4, 16)
SCAN_CHUNK = 64
TOKEN_TILE = 256
FFN_BWD_TILE = 512

ADAM_LR = 0.001
ADAM_B1 = 0.9
ADAM_B2 = 0.999
ADAM_EPS = 1e-08
ADAM_WD = 0.01
ADAM_STEP = 10

VMEM_FULL = pl.BlockSpec(memory_space=pltpu.VMEM)
ANY = pl.BlockSpec(memory_space=pl.ANY)


VMEM_LIMIT = 56 * 1024 * 1024


def _params(*sem):
    return pltpu.CompilerParams(dimension_semantics=sem, vmem_limit_bytes=VMEM_LIMIT)


def _dot(a, b, dims):
    return lax.dot_general(a.astype(BF16), b.astype(BF16), (dims, ((), ())), preferred_element_type=F32)


def _dot_nn(a, b):
    return _dot(a, b, ((1,), (0,)))


def _dot_nt(a, b):
    return _dot(a, b, ((1,), (1,)))


def _dot_tn(a, b):
    return _dot(a, b, ((0,), (0,)))


@jax.custom_vjp
def _mm(a, b):
    return _dot_nn(a, b)


def _mm_fwd(a, b):
    return _dot_nn(a, b), (a, b)


def _mm_bwd(res, g):
    a, b = res
    return _dot_nt(g, b).astype(a.dtype), _dot_tn(a, g).astype(b.dtype)


_mm.defvjp(_mm_fwd, _mm_bwd)


def _bdot(a, b, ca, cb):
    return lax.dot_general(a.astype(BF16), b.astype(BF16), (((ca,), (cb,)), ((0,), (0,))), preferred_element_type=F32)


@jax.custom_vjp
def _bmm_nt(a, b):
    return _bdot(a, b, 2, 2)


def _bmm_nt_fwd(a, b):
    return _bdot(a, b, 2, 2), (a, b)


def _bmm_nt_bwd(res, g):
    a, b = res
    return _bdot(g, b, 2, 1), _bdot(g, a, 1, 1)


_bmm_nt.defvjp(_bmm_nt_fwd, _bmm_nt_bwd)


@jax.custom_vjp
def _bmm_nn(a, b):
    return _bdot(a, b, 2, 1)


def _bmm_nn_fwd(a, b):
    return _bdot(a, b, 2, 1), (a, b)


def _bmm_nn_bwd(res, g):
    a, b = res
    return _bdot(g, b, 2, 2), _bdot(a, g, 1, 1)


_bmm_nn.defvjp(_bmm_nn_fwd, _bmm_nn_bwd)


@jax.custom_vjp
def _bmm_tn(a, b):
    return _bdot(a, b, 1, 1)


def _bmm_tn_fwd(a, b):
    return _bdot(a, b, 1, 1), (a, b)


def _bmm_tn_bwd(res, g):
    a, b = res
    return _bdot(b, g, 2, 2), _bdot(a, g, 2, 1)


_bmm_tn.defvjp(_bmm_tn_fwd, _bmm_tn_bwd)


def _hdot(a, b, ca, cb):
    return lax.dot_general(a, b, (((ca,), (cb,)), ((0,), (0,))), precision=lax.Precision.HIGH, preferred_element_type=F32)


def _sigmoid(x):
    return 1.0 / (1.0 + jnp.exp(-x))


def _rms(x):
    return lax.rsqrt(jnp.mean(x * x, axis=-1, keepdims=True) + RMS_EPS)


def _ffn_fwd(x, norm, wg, wu, wd, dep, name):
    t, d = x.shape
    nc, fc, _ = wg.shape
    tm = TOKEN_TILE

    def body(x_ref, n_ref, wg_ref, wu_ref, wd_ref, dep_ref, o_ref, g_ref, u_ref):
        xv = x_ref[...]
        h = (xv * _rms(xv) * n_ref[...]).astype(BF16)
        acc = jnp.zeros((tm, d), F32)
        for c in range(nc):
            g = _dot_nt(h, wg_ref[c])
            u = _dot_nt(h, wu_ref[c])
            g_ref[c] = g.astype(BF16)
            u_ref[c] = u.astype(BF16)
            a = (g * _sigmoid(g) * u).astype(BF16)
            acc = acc + jnp.dot(a, wd_ref[c], preferred_element_type=F32)
        o_ref[...] = xv + FFN_RESIDUAL * acc

    tile = pl.BlockSpec((tm, d), lambda i: (i, 0))
    hidden = pl.BlockSpec((nc, tm, fc), lambda i: (0, i, 0))
    hshape = jax.ShapeDtypeStruct((nc, t, fc), BF16)
    return pl.pallas_call(
        body, name=name, grid=(t // tm,), out_shape=(jax.ShapeDtypeStruct((t, d), F32), hshape, hshape),
        in_specs=[tile, pl.BlockSpec((1, d), lambda i: (0, 0)), VMEM_FULL, VMEM_FULL, VMEM_FULL, ANY],
        out_specs=(tile, hidden, hidden), compiler_params=_params("arbitrary"),
    )(x, norm, wg, wu, wd, dep)


def _rmsnorm_bwd(xv, gain, dh):
    rs = _rms(xv)
    xn = xv * rs
    dxn = dh * gain
    dx = rs * (dxn - xn * jnp.mean(dxn * xn, axis=-1, keepdims=True))
    return dx, jnp.sum(dh * xn, axis=0, keepdims=True)


def _ffn_bwd(x, norm, wg, wu, wd, gate, up, dy, dep, name):
    t, d = x.shape
    nc, fc, _ = wg.shape
    tm = FFN_BWD_TILE
    nt = t // tm

    def body(x_ref, n_ref, wg_ref, wu_ref, wd_ref, g_ref, u_ref, dy_ref, dep_ref, dx_ref, dn_ref, dwg_ref, dwu_ref,
             dwd_ref, dh_ref, ag_ref, au_ref, ad_ref):
        c, i = pl.program_id(0), pl.program_id(1)
        rows = pl.ds(pl.multiple_of(i * tm, tm), tm)
        xv = x_ref[...]
        gain = n_ref[...]
        h = (xv * _rms(xv) * gain).astype(BF16)
        dy = dy_ref[...]
        dyb = (FFN_RESIDUAL * dy).astype(BF16)
        g = g_ref[0].astype(F32)
        u = u_ref[0].astype(F32)
        sg = _sigmoid(g)
        s = g * sg
        a = (s * u).astype(BF16)
        da = _dot_nt(dyb, wd_ref[0])
        dub = (da * s).astype(BF16)
        dgb = (da * u * (sg * (1.0 + g * (1.0 - sg)))).astype(BF16)
        dwd_c = _dot_tn(a, dyb)
        dwg_c = _dot_tn(dgb, h)
        dwu_c = _dot_tn(dub, h)
        dh_c = _dot_nn(dgb, wg_ref[0]) + _dot_nn(dub, wu_ref[0])

        @pl.when(i == 0)
        def _():
            ad_ref[...] = dwd_c
            ag_ref[...] = dwg_c
            au_ref[...] = dwu_c

        @pl.when(i > 0)
        def _():
            ad_ref[...] += dwd_c
            ag_ref[...] += dwg_c
            au_ref[...] += dwu_c

        @pl.when(i == nt - 1)
        def _():
            dwd_ref[0] = ad_ref[...].astype(BF16)
            dwg_ref[0] = ag_ref[...].astype(BF16)
            dwu_ref[0] = au_ref[...].astype(BF16)

        @pl.when(c == 0)
        def _():
            dh_ref[rows, :] = dh_c

        @pl.when(c > 0)
        def _():
            dh_ref[rows, :] += dh_c

        @pl.when(c == nc - 1)
        def _():
            dx, dn = _rmsnorm_bwd(xv, gain, dh_ref[rows, :])
            dx_ref[...] = dx + dy

            @pl.when(i == 0)
            def _():
                dn_ref[...] = dn

            @pl.when(i > 0)
            def _():
                dn_ref[...] += dn

    tile = pl.BlockSpec((tm, d), lambda c, i: (i, 0))
    row = pl.BlockSpec((1, d), lambda c, i: (0, 0))
    wrow = pl.BlockSpec((1, fc, d), lambda c, i: (c, 0, 0), pipeline_mode=pl.Buffered(1))
    hidden = pl.BlockSpec((1, tm, fc), lambda c, i: (c, i, 0))
    last = pl.BlockSpec((tm, d), lambda c, i: (jnp.where(c == nc - 1, i, 0), 0))
    return pl.pallas_call(
        body, name=name, grid=(nc, nt),
        out_shape=(jax.ShapeDtypeStruct((t, d), F32), jax.ShapeDtypeStruct((1, d), F32),
                   jax.ShapeDtypeStruct(wg.shape, BF16), jax.ShapeDtypeStruct(wu.shape, BF16),
                   jax.ShapeDtypeStruct(wd.shape, BF16)),
        in_specs=[tile, row, wrow, wrow, wrow, hidden, hidden, tile, ANY],
        out_specs=(last, row, wrow, wrow, wrow),
        scratch_shapes=[pltpu.VMEM((t, d), F32)] + [pltpu.VMEM((fc, d), F32)] * 3,
        compiler_params=_params("arbitrary", "arbitrary"),
    )(x, norm, wg, wu, wd, gate, up, dy, dep)


def _store_heads(ref, v):
    for h in range(ref.shape[0]):
        ref[h] = v[:, h * HEAD_DIM:(h + 1) * HEAD_DIM]


def _load_heads(ref):
    return jnp.concatenate([ref[h] for h in range(ref.shape[0])], axis=-1)


N_HEAD_GROUPS = 3


def _proj_fwd(x, norm, w):
    t, d = x.shape
    ng, _, c = w.shape
    nh = c // HEAD_DIM
    tm = TOKEN_TILE

    def body(x_ref, n_ref, w_ref, q_ref, k_ref, v_ref, cur_ref):
        xv = x_ref[...]
        h = (xv * _rms(xv) * n_ref[...]).astype(BF16)
        for m, ref in enumerate((q_ref, k_ref, v_ref)):
            _store_heads(ref, jnp.dot(h, w_ref[m], preferred_element_type=F32))
        for m in range(N_HEAD_GROUPS, ng):
            j = m - N_HEAD_GROUPS
            cur_ref[:, j * c:(j + 1) * c] = jnp.dot(h, w_ref[m], preferred_element_type=F32)

    heads = pl.BlockSpec((nh, tm, HEAD_DIM), lambda i: (0, i, 0))
    hshape = jax.ShapeDtypeStruct((nh, t, HEAD_DIM), F32)
    wide = (ng - N_HEAD_GROUPS) * c
    return pl.pallas_call(
        body, name="proj_fwd", grid=(t // tm,),
        out_shape=(hshape, hshape, hshape, jax.ShapeDtypeStruct((t, wide), F32)),
        in_specs=[pl.BlockSpec((tm, d), lambda i: (i, 0)), pl.BlockSpec((1, d), lambda i: (0, 0)), VMEM_FULL],
        out_specs=(heads, heads, heads, pl.BlockSpec((tm, wide), lambda i: (i, 0))),
        compiler_params=_params("arbitrary"),
    )(x, norm, w)


def _proj_bwd(x, norm, w, dq, dk, dv, dcur, dres):
    t, d = x.shape
    ng, _, c = w.shape
    nh = c // HEAD_DIM
    tm = TOKEN_TILE

    def body(x_ref, n_ref, w_ref, dq_ref, dk_ref, dv_ref, dcur_ref, dres_ref, dx_ref, dn_ref, dw_ref):
        i = pl.program_id(0)

        @pl.when(i == 0)
        def _():
            dw_ref[...] = jnp.zeros_like(dw_ref)
            dn_ref[...] = jnp.zeros_like(dn_ref)

        xv = x_ref[...]
        gain = n_ref[...]
        h = (xv * _rms(xv) * gain).astype(BF16)
        dh = jnp.zeros((tm, d), F32)
        for m in range(ng):
            j = m - N_HEAD_GROUPS
            dp = _load_heads((dq_ref, dk_ref, dv_ref)[m]) if j < 0 else dcur_ref[:, j * c:(j + 1) * c]
            dp = dp.astype(BF16)
            dw_ref[m] += _dot_tn(h, dp)
            dh = dh + _dot_nt(dp, w_ref[m])
        dx, dn = _rmsnorm_bwd(xv, gain, dh)
        dx_ref[...] = dx + dres_ref[...]
        dn_ref[...] += dn

    tile = pl.BlockSpec((tm, d), lambda i: (i, 0))
    row = pl.BlockSpec((1, d), lambda i: (0, 0))
    heads = pl.BlockSpec((nh, tm, HEAD_DIM), lambda i: (0, i, 0))
    wide = (ng - N_HEAD_GROUPS) * c
    return pl.pallas_call(
        body, name="proj_bwd", grid=(t // tm,),
        out_shape=(jax.ShapeDtypeStruct((t, d), F32), jax.ShapeDtypeStruct((1, d), F32),
                   jax.ShapeDtypeStruct(w.shape, F32)),
        in_specs=[tile, row, VMEM_FULL, heads, heads, heads, pl.BlockSpec((tm, wide), lambda i: (i, 0)), tile],
        out_specs=(tile, row, VMEM_FULL),
        compiler_params=_params("arbitrary"),
    )(x, norm, w, dq, dk, dv, dcur, dres)


def _mixout_fwd(x, att, opg, gate, w):
    t, d = x.shape
    nh = att.shape[0]
    half = gate.shape[1]
    tm = TOKEN_TILE

    def body(x_ref, att_ref, opg_ref, g_ref, w_ref, o_ref):
        mix = jnp.concatenate([_load_heads(att_ref), _load_heads(opg_ref) * g_ref[...]], axis=-1).astype(BF16)
        o_ref[...] = x_ref[...] + jnp.dot(mix, w_ref[...], preferred_element_type=F32)

    tile = pl.BlockSpec((tm, d), lambda i: (i, 0))
    htile = pl.BlockSpec((tm, half), lambda i: (i, 0))
    heads = pl.BlockSpec((nh, tm, HEAD_DIM), lambda i: (0, i, 0))
    return pl.pallas_call(
        body, name="mixout_fwd", grid=(t // tm,), out_shape=jax.ShapeDtypeStruct((t, d), F32),
        in_specs=[tile, heads, heads, htile, VMEM_FULL], out_specs=tile, compiler_params=_params("arbitrary"),
    )(x, att, opg, gate, w)


def _mixout_bwd(att, opg, gate, w, dy, dep):
    nh, t, _ = att.shape
    half = gate.shape[1]
    d = dy.shape[1]
    tm = TOKEN_TILE

    def body(att_ref, opg_ref, g_ref, w_ref, dy_ref, dep_ref, datt_ref, dopg_ref, dg_ref, dw_ref):
        i = pl.program_id(0)
        opg_v, g_v = _load_heads(opg_ref), g_ref[...]
        mix = jnp.concatenate([_load_heads(att_ref), opg_v * g_v], axis=-1).astype(BF16)
        dyb = dy_ref[...].astype(BF16)
        dmix = _dot_nt(dyb, w_ref[...])
        dw = _dot_tn(mix, dyb)
        _store_heads(datt_ref, dmix[:, :half])
        drw = dmix[:, half:]
        _store_heads(dopg_ref, drw * g_v)
        dg_ref[...] = drw * opg_v

        @pl.when(i == 0)
        def _():
            dw_ref[...] = dw

        @pl.when(i > 0)
        def _():
            dw_ref[...] += dw

    tile = pl.BlockSpec((tm, d), lambda i: (i, 0))
    htile = pl.BlockSpec((tm, half), lambda i: (i, 0))
    heads = pl.BlockSpec((nh, tm, HEAD_DIM), lambda i: (0, i, 0))
    hshape = jax.ShapeDtypeStruct((nh, t, HEAD_DIM), F32)
    return pl.pallas_call(
        body, name="mixout_bwd", grid=(t // tm,),
        out_shape=(hshape, hshape, jax.ShapeDtypeStruct((t, half), F32), jax.ShapeDtypeStruct(w.shape, F32)),
        in_specs=[heads, heads, htile, VMEM_FULL, tile, ANY],
        out_specs=(heads, heads, htile, pl.BlockSpec(w.shape, lambda i: (0, 0))),
        compiler_params=_params("arbitrary"),
    )(att, opg, gate, w, dy, dep)


def _loss_head(y, target):
    t, d = y.shape
    tm = TOKEN_TILE

    def body(y_ref, t_ref, dy_ref, loss_ref):
        i = pl.program_id(0)
        err = y_ref[...] - t_ref[...]
        dy_ref[...] = err * (1.0 / d)
        part = 0.5 * jnp.sum(jnp.mean(err * err, axis=-1, keepdims=True), axis=0, keepdims=True)

        @pl.when(i == 0)
        def _():
            loss_ref[...] = jnp.zeros_like(loss_ref)

        loss_ref[...] += jnp.broadcast_to(part, loss_ref.shape)

    tile = pl.BlockSpec((tm, d), lambda i: (i, 0))
    return pl.pallas_call(
        body, name="loss_head", grid=(t // tm,),
        out_shape=(jax.ShapeDtypeStruct((t, d), F32), jax.ShapeDtypeStruct((1, 128), F32)),
        in_specs=[tile, tile], out_specs=(tile, pl.BlockSpec((1, 128), lambda i: (0, 0))),
        compiler_params=_params("arbitrary"),
    )(y, target)


def _head_norm(x, gain):
    return x * _rms(x) * gain


def _att_pattern(qh, kh, v, nb):
    g, blk, _ = qh.shape
    scale = HEAD_DIM ** -0.5
    qi = lax.broadcasted_iota(jnp.int32, (blk, blk), 0)
    kj = lax.broadcasted_iota(jnp.int32, (blk, blk), 1)
    sc = jnp.where(kj <= qi, _bmm_nt(qh, kh) * scale, NEG_INF)
    top = jnp.max(sc, axis=-1, keepdims=True)
    if nb > 1:
        khp = jnp.concatenate([kh[:1], kh[:-1]], axis=0)
        vp = jnp.concatenate([v[:1], v[:-1]], axis=0)
        has_prev = lax.broadcasted_iota(jnp.int32, (g, 1, 1), 0) % nb != 0
        sp = jnp.where((kj >= qi) & has_prev, _bmm_nt(qh, khp) * scale, NEG_INF)
        top = jnp.maximum(top, jnp.max(sp, axis=-1, keepdims=True))
    m = lax.stop_gradient(top)
    pc = jnp.exp(sc - m)
    den = jnp.sum(pc, axis=-1, keepdims=True)
    acc = _bmm_nn(pc, v)
    if nb > 1:
        pp = jnp.exp(sp - m)
        den = den + jnp.sum(pp, axis=-1, keepdims=True)
        acc = acc + _bmm_nn(pp, vp)
    o = acc / den
    return o, jnp.broadcast_to(m + jnp.log(den), o.shape)


def _pattern_rows(t, dil):
    nb = t // (ATT_BLOCK * dil)
    starts = [n * ATT_BLOCK * dil + r for r in range(dil) for n in range(nb)]
    return [pl.ds(s, ATT_BLOCK, stride=dil) if dil > 1 else pl.ds(s, ATT_BLOCK) for s in starts], nb


def _take(ref, rows):
    return jnp.stack([ref[0, r, :] for r in rows])


def _put(ref, rows, val):
    for g, r in enumerate(rows):
        ref[0, r, :] = val[g]


def _put_add(ref, rows, val):
    for g, r in enumerate(rows):
        ref[0, r, :] += val[g]


def _merge_fn(o1, o2, o3, l1, l2, l3):
    m = lax.stop_gradient(jnp.maximum(jnp.maximum(l1, l2), l3))
    e1, e2, e3 = jnp.exp(l1 - m), jnp.exp(l2 - m), jnp.exp(l3 - m)
    return (e1 * o1 + e2 * o2 + e3 * o3) / (e1 + e2 + e3)


def _token_rows(j):
    return pl.ds(pl.multiple_of(j * ATT_BLOCK, ATT_BLOCK), ATT_BLOCK)


def _norm_rows(t, q_ref, k_ref, gq, gk, qh_ref, kh_ref):
    def step(j, carry):
        rows = _token_rows(j)
        qh_ref[0, rows, :] = _head_norm(q_ref[0, rows, :], gq[0])
        kh_ref[0, rows, :] = _head_norm(k_ref[0, rows, :], gk[0])
        return carry

    lax.fori_loop(0, t // ATT_BLOCK, step, 0)


def _att_head_specs(t):
    head = pl.BlockSpec((1, t, HEAD_DIM), lambda h: (h, 0, 0))
    gain = pl.BlockSpec((1, 1, HEAD_DIM), lambda h: (0, 0, 0))
    return head, gain


def _att_fwd(q, k, v, qn, kn):
    nh, t, dh = q.shape
    head, gain = _att_head_specs(t)

    def body(q_ref, k_ref, v_ref, qn_ref, kn_ref, att_ref, o1, o2, o3, l1, l2, l3, qh_ref, kh_ref):
        saved = (o1, o2, o3, l1, l2, l3)
        _norm_rows(t, q_ref, k_ref, qn_ref[...], kn_ref[...], qh_ref, kh_ref)
        for p, dil in enumerate(DILATIONS):
            rows, nb = _pattern_rows(t, dil)
            o, lse = _att_pattern(_take(qh_ref, rows), _take(kh_ref, rows), _take(v_ref, rows), nb)
            _put(saved[p], rows, o)
            _put(saved[3 + p], rows, lse)

        def merge(j, carry):
            rows = _token_rows(j)
            att_ref[0, rows, :] = _merge_fn(*[r[0, rows, :] for r in saved])
            return carry

        lax.fori_loop(0, t // ATT_BLOCK, merge, 0)

    return pl.pallas_call(
        body, name="att_fwd", grid=(nh,), out_shape=(jax.ShapeDtypeStruct(q.shape, F32),) * 7,
        in_specs=[head, head, head, gain, gain], out_specs=(head,) * 7,
        scratch_shapes=[pltpu.VMEM((1, t, dh), F32)] * 2, compiler_params=_params("arbitrary"),
    )(q, k, v, qn, kn)


def _att_bwd(q, k, v, qn, kn, saved, datt):
    nh, t, dh = q.shape
    head, gain = _att_head_specs(t)

    def body(q_ref, k_ref, v_ref, qn_ref, kn_ref, o1, o2, o3, l1, l2, l3, datt_ref,
             dq_ref, dk_ref, dv_ref, dqn_ref, dkn_ref, qh_ref, kh_ref, dqh_ref, dkh_ref, *ct_refs):
        for ref in (dqh_ref, dkh_ref, dv_ref):
            ref[...] = jnp.zeros_like(ref)

        @pl.when(pl.program_id(0) == 0)
        def _():
            dqn_ref[...] = jnp.zeros_like(dqn_ref)
            dkn_ref[...] = jnp.zeros_like(dkn_ref)

        gq, gk = qn_ref[...], kn_ref[...]
        _norm_rows(t, q_ref, k_ref, gq, gk, qh_ref, kh_ref)

        def merge_cotangents(j, carry):
            rows = _token_rows(j)
            _, merge_vjp = jax.vjp(_merge_fn, *[r[0, rows, :] for r in (o1, o2, o3, l1, l2, l3)])
            for ref, val in zip(ct_refs, merge_vjp(datt_ref[0, rows, :])):
                ref[0, rows, :] = val
            return carry

        lax.fori_loop(0, t // ATT_BLOCK, merge_cotangents, 0)

        for p, dil in enumerate(DILATIONS):
            rows, nb = _pattern_rows(t, dil)
            _, pattern_vjp = jax.vjp(functools.partial(_att_pattern, nb=nb), _take(qh_ref, rows), _take(kh_ref, rows),
                                     _take(v_ref, rows))
            dqh, dkh, dv = pattern_vjp((_take(ct_refs[p], rows), _take(ct_refs[3 + p], rows)))
            _put_add(dqh_ref, rows, dqh)
            _put_add(dkh_ref, rows, dkh)
            _put_add(dv_ref, rows, dv)

        def norm_cotangents(j, carry):
            rows = _token_rows(j)
            out = []
            for x_ref, gain, dh_ref, dx_ref, acc in ((q_ref, gq, dqh_ref, dq_ref, carry[0]), (k_ref, gk, dkh_ref, dk_ref, carry[1])):
                _, norm_vjp = jax.vjp(_head_norm, x_ref[0, rows, :], gain[0])
                dx, dgain = norm_vjp(dh_ref[0, rows, :])
                dx_ref[0, rows, :] = dx
                out.append(acc + dgain)
            return tuple(out)

        zero = jnp.zeros((1, dh), F32)
        dgq, dgk = lax.fori_loop(0, t // ATT_BLOCK, norm_cotangents, (zero, zero))
        dqn_ref[0] += dgq
        dkn_ref[0] += dgk

    hshape = jax.ShapeDtypeStruct(q.shape, F32)
    gshape = jax.ShapeDtypeStruct((1, 1, dh), F32)
    return pl.pallas_call(
        body, name="att_bwd", grid=(nh,), out_shape=(hshape, hshape, hshape, gshape, gshape),
        in_specs=[head, head, head, gain, gain] + [head] * 7, out_specs=(head, head, head, gain, gain),
        scratch_shapes=[pltpu.VMEM((1, t, dh), F32)] * 10, compiler_params=_params("arbitrary"),
    )(q, k, v, qn, kn, *saved, datt)


RWKV_VEC = ("mu_r", "mu_k", "mu_v", "mu_w", "mu_a", "mu_g", "w0", "a0", "k_k", "k_a")
RWKV_MAT = ("w1", "w2", "a1", "a2", "g1", "g2")


def _rwkv_pre_fn(cur, prev, vec, w1, w2, a1, a2, g1, g2):
    c = cur.shape[1] // 4
    mu_r, mu_k, mu_v, mu_w, mu_a, mu_g, w0, a0, k_k, k_a = (vec[j:j + 1] for j in range(10))

    def lerp(j, mu):
        xc, xp = cur[:, j * c:(j + 1) * c], prev[:, j * c:(j + 1) * c]
        return xc + (xp - xc) * mu

    r, k, v = lerp(0, mu_r), lerp(1, mu_k), lerp(2, mu_v)
    cw, ca, cg = lerp(3, mu_w), lerp(3, mu_a), lerp(3, mu_g)
    z = w0 + _mm(jnp.tanh(_mm(cw, w1)), w2)
    w_log = jnp.minimum(z, 0.0) - jnp.log(1.0 + jnp.exp(-jnp.abs(z))) - 0.5
    lw = -jnp.exp(w_log)
    a = _sigmoid(a0 + _mm(_mm(ca, a1), a2))
    gate = _mm(_sigmoid(_mm(cg, g1)), g2)
    kkraw = k * k_k
    kmod = k * (1.0 + (a - 1.0) * k_a)
    return r, lw, kmod, v, kkraw, a, gate


HALO_ROWS = 8


def _rwkv_pre_specs(c, mats, tile_of):
    tm = TOKEN_TILE
    nh = c // HEAD_DIM
    wide = pl.BlockSpec((tm, 4 * c), lambda j: (tile_of(j), 0))
    halo = pl.BlockSpec((HALO_ROWS, 4 * c), lambda j: (jnp.maximum(tile_of(j) * (tm // HALO_ROWS) - 1, 0), 0))
    one = pl.BlockSpec((tm, c), lambda j: (tile_of(j), 0))
    heads = pl.BlockSpec((nh, tm, HEAD_DIM), lambda j: (0, tile_of(j), 0))
    vec = pl.BlockSpec((10, c), lambda j: (0, 0))
    mspecs = [pl.BlockSpec(m.shape, lambda j: (0, 0)) for m in mats]
    return wide, halo, one, heads, vec, mspecs


def _previous_rows(cur, halo, tile):
    first = jnp.where(tile > 0, halo[HALO_ROWS - 1:HALO_ROWS], 0.0)
    rows = lax.broadcasted_iota(jnp.int32, cur.shape, 0)
    return jnp.where(rows == 0, first, pltpu.roll(cur, 1, axis=0))


def _rwkv_pre_fwd(cur, vec, mats):
    t, c4 = cur.shape
    c = c4 // 4
    wide, halo, one, heads, vspec, mspecs = _rwkv_pre_specs(c, mats, lambda j: j)

    def body(cur_ref, halo_ref, vec_ref, *rest):
        mrefs, outs = rest[:6], rest[6:]
        cur_v = cur_ref[...]
        prev = _previous_rows(cur_v, halo_ref[...], pl.program_id(0))
        vals = _rwkv_pre_fn(cur_v, prev, vec_ref[...], *(m[...] for m in mrefs))
        for ref, val in zip(outs[:6], vals[:6]):
            _store_heads(ref, val)
        outs[6][...] = vals[6]

    hshape = jax.ShapeDtypeStruct((c // HEAD_DIM, t, HEAD_DIM), F32)
    return pl.pallas_call(
        body, name="rwkv_pre_fwd", grid=(t // TOKEN_TILE,), out_shape=(hshape,) * 6 + (jax.ShapeDtypeStruct((t, c), F32),),
        in_specs=[wide, halo, vspec] + mspecs, out_specs=(heads,) * 6 + (one,), compiler_params=_params("arbitrary"),
    )(cur, cur, vec, *mats)


def _rwkv_pre_bwd(cur, vec, mats, cts, dgate):
    t, c4 = cur.shape
    c = c4 // 4
    tm = TOKEN_TILE
    nt = t // tm
    wide, halo, one, heads, vspec, mspecs = _rwkv_pre_specs(c, mats, lambda j: nt - 1 - j)

    def body(cur_ref, halo_ref, vec_ref, *rest):
        mrefs, ctrefs, dgate_ref, outs, carry_ref = rest[:6], rest[6:12], rest[12], rest[13:-1], rest[-1]
        j = pl.program_id(0)

        @pl.when(j == 0)
        def _():
            carry_ref[...] = jnp.zeros_like(carry_ref)
            for ref in outs[1:]:
                ref[...] = jnp.zeros_like(ref)

        cur_v = cur_ref[...]
        prev = _previous_rows(cur_v, halo_ref[...], nt - 1 - j)
        _, vjp = jax.vjp(_rwkv_pre_fn, cur_v, prev, vec_ref[...], *(m[...] for m in mrefs))
        grads = vjp(tuple(_load_heads(r) for r in ctrefs) + (dgate_ref[...],))
        dprev = grads[1]
        rows = lax.broadcasted_iota(jnp.int32, dprev.shape, 0)
        outs[0][...] = grads[0] + jnp.where(rows == tm - 1, carry_ref[0:1], pltpu.roll(dprev, tm - 1, axis=0))
        carry_ref[0:1] = dprev[0:1]
        for ref, val in zip(outs[1:], grads[2:]):
            ref[...] += val

    return pl.pallas_call(
        body, name="rwkv_pre_bwd", grid=(nt,),
        out_shape=(jax.ShapeDtypeStruct(cur.shape, F32), jax.ShapeDtypeStruct(vec.shape, F32))
        + tuple(jax.ShapeDtypeStruct(m.shape, F32) for m in mats),
        in_specs=[wide, halo, vspec] + mspecs + [heads] * 6 + [one], out_specs=(wide, vspec) + tuple(mspecs),
        scratch_shapes=[pltpu.VMEM((HALO_ROWS, c4), F32)], compiler_params=_params("arbitrary"),
    )(cur, cur, vec, *mats, *cts, dgate)


def _scan_chunk_fn(h0, r, lw, k, v, kkraw, a, rk, lnw, lnb):
    n = r.shape[1]
    nrm = jnp.sqrt(jnp.sum(kkraw * kkraw, axis=-1, keepdims=True))
    kk = kkraw / jnp.maximum(nrm, 1e-12)
    av, bv = -kk, kk * a
    ti = lax.broadcasted_iota(jnp.int32, (n, n), 0)
    si = lax.broadcasted_iota(jnp.int32, (n, n), 1)
    incl, strict = ti >= si, ti > si
    ones = jnp.broadcast_to(incl.astype(F32)[None], (r.shape[0], n, n))
    cum = _hdot(ones, lw, 2, 1)
    at, rt = av * jnp.exp(cum - lw), r * jnp.exp(cum)
    inv = jnp.exp(-cum)
    bt, kt = bv * inv, k * inv
    gram = _hdot(jnp.concatenate([at, rt], axis=1), jnp.concatenate([bt, kt], axis=1), 2, 2)
    lab = jnp.where(strict, gram[:, :n, :n], 0.0)
    lak = jnp.where(strict, gram[:, :n, n:], 0.0)
    rb = jnp.where(incl, gram[:, n:, :n], 0.0)
    rkm = jnp.where(incl, gram[:, n:, n:], 0.0)
    nv = v.shape[2]
    u = _bmm_nn(jnp.concatenate([at, lak], axis=2), jnp.concatenate([h0, v], axis=1))
    p = lab
    m = 2
    while m < n:
        both = _bmm_nn(p, jnp.concatenate([u, p], axis=2))
        u, p = u + both[:, :, :nv], both[:, :, nv:]
        m *= 2
    u = u + _bmm_nn(p, u)
    y = _bmm_nn(jnp.concatenate([rt, rb, rkm], axis=2), jnp.concatenate([h0, u, v], axis=1))
    last = jnp.exp(jnp.sum(lw, axis=1, keepdims=True))
    h1 = jnp.swapaxes(last, 1, 2) * (h0 + _bmm_tn(jnp.concatenate([bt, kt], axis=1), jnp.concatenate([u, v], axis=1)))
    mean = jnp.mean(y, axis=-1, keepdims=True)
    yc = y - mean
    var = jnp.mean(yc * yc, axis=-1, keepdims=True)
    yn = yc * lax.rsqrt(var + GN_EPS) * lnw + lnb
    bonus = jnp.sum(r * k * rk, axis=-1, keepdims=True) * v
    return yn + bonus, h1


SCAN_GROUP = 2


def _scan_group_fn(h0, r, lw, k, v, kkraw, a, rk, lnw, lnb):
    outs = []
    for j in range(SCAN_GROUP):
        rows = slice(j * SCAN_CHUNK, (j + 1) * SCAN_CHUNK)
        o, h0 = _scan_chunk_fn(h0, r[:, rows], lw[:, rows], k[:, rows], v[:, rows], kkraw[:, rows], a[:, rows], rk, lnw, lnb)
        outs.append(o)
    return jnp.concatenate(outs, axis=1), h0


def _scan_specs(h, t, dh, rev):
    n = SCAN_CHUNK * SCAN_GROUP
    nc = t // n
    pos = (lambda c: (0, nc - 1 - c, 0)) if rev else (lambda c: (0, c, 0))
    st = (lambda c: (nc - 1 - c, 0, 0, 0)) if rev else (lambda c: (c, 0, 0, 0))
    seq = pl.BlockSpec((h, n, dh), pos)
    par = pl.BlockSpec((h, 1, dh), lambda c: (0, 0, 0))
    state = pl.BlockSpec((1, h, dh, dh), st)
    return seq, par, state


def _scan_fwd(seqs, pars):
    h, t, dh = seqs[0].shape
    nc = t // (SCAN_CHUNK * SCAN_GROUP)
    seq, par, state = _scan_specs(h, t, dh, False)

    def body(r, lw, k, v, kkraw, a, rk, lnw, lnb, o_ref, st_ref, h_ref):
        @pl.when(pl.program_id(0) == 0)
        def _():
            h_ref[...] = jnp.zeros_like(h_ref)

        h0 = h_ref[...]
        st_ref[0] = h0
        o, h1 = _scan_group_fn(h0, r[...], lw[...], k[...], v[...], kkraw[...], a[...], rk[...], lnw[...], lnb[...])
        o_ref[...] = o
        h_ref[...] = h1

    return pl.pallas_call(
        body, name="rwkv_scan_fwd", grid=(nc,),
        out_shape=(jax.ShapeDtypeStruct((h, t, dh), F32), jax.ShapeDtypeStruct((nc, h, dh, dh), F32)),
        in_specs=[seq] * 6 + [par] * 3, out_specs=(seq, state),
        scratch_shapes=[pltpu.VMEM((h, dh, dh), F32)], compiler_params=_params("arbitrary"),
    )(*seqs, *pars)


def _scan_bwd(seqs, pars, states, do):
    h, t, dh = seqs[0].shape
    nc = t // (SCAN_CHUNK * SCAN_GROUP)
    seq, par, state = _scan_specs(h, t, dh, True)

    def body(r, lw, k, v, kkraw, a, rk, lnw, lnb, st_ref, do_ref, *rest):
        douts, dpars, dh_ref = rest[:6], rest[6:9], rest[9]
        first = pl.program_id(0) == 0

        @pl.when(first)
        def _():
            dh_ref[...] = jnp.zeros_like(dh_ref)

        _, vjp = jax.vjp(_scan_group_fn, st_ref[0], r[...], lw[...], k[...], v[...], kkraw[...], a[...],
                         rk[...], lnw[...], lnb[...])
        grads = vjp((do_ref[...], dh_ref[...]))
        dh_ref[...] = grads[0]
        for ref, val in zip(douts, grads[1:7]):
            ref[...] = val

        @pl.when(first)
        def _():
            for ref, val in zip(dpars, grads[7:]):
                ref[...] = val

        @pl.when(jnp.logical_not(first))
        def _():
            for ref, val in zip(dpars, grads[7:]):
                ref[...] += val

    sshape = jax.ShapeDtypeStruct((h, t, dh), F32)
    pshape = jax.ShapeDtypeStruct((h, 1, dh), F32)
    return pl.pallas_call(
        body, name="rwkv_scan_bwd", grid=(nc,), out_shape=(sshape,) * 6 + (pshape,) * 3,
        in_specs=[seq] * 6 + [par] * 3 + [state, seq], out_specs=(seq,) * 6 + (par,) * 3,
        scratch_shapes=[pltpu.VMEM((h, dh, dh), F32)], compiler_params=_params("arbitrary"),
    )(*seqs, *pars, states, do)


def _local_step(x, target, w, ex):
    w = dict(w)
    c = w["mu_r"].shape[-1]
    qn, kn = w["q_norm"].reshape(1, 1, HEAD_DIM), w["k_norm"].reshape(1, 1, HEAD_DIM)
    vec = jnp.concatenate([w[n].reshape(1, c) for n in RWKV_VEC], axis=0)
    pars = [w[n].reshape(-1, 1, HEAD_DIM) for n in ("r_k", "ln_x_w", "ln_x_b")]
    no_dep = jnp.zeros(DEP_SHAPE, F32)

    x1, gate1, up1 = _ffn_fwd(x, w["ffn1_norm"], w["ffn1_w_gate"], w["ffn1_w_up"], w["ffn1_w_down"], ex.first_dep, "ffn1_fwd")
    w.update(ex.mix_weights((x1,)))
    mats = [w[n] for n in RWKV_MAT]
    q, k, v, cur = _proj_fwd(x1, w["mix_norm"], w["w_in"])
    att, *saved = _att_fwd(q, k, v, qn, kn)
    pre = _rwkv_pre_fwd(cur, vec, mats)
    seqs, gate = pre[:6], pre[6]
    opg, states = _scan_fwd(seqs, pars)
    w.update(ex.out_weights((att, opg)))
    x2 = _mixout_fwd(x1, att, opg, gate, w["w_out"])
    x3, gate2, up2 = _ffn_fwd(x2, w["ffn2_norm"], w["ffn2_w_gate"], w["ffn2_w_up"], w["ffn2_w_down"], no_dep, "ffn2_fwd")
    dy, loss = _loss_head(x3, target)

    g = {}
    dx2, g["ffn2_norm"], g["ffn2_w_gate"], g["ffn2_w_up"], g["ffn2_w_down"] = _ffn_bwd(
        x2, w["ffn2_norm"], w["ffn2_w_gate"], w["ffn2_w_up"], w["ffn2_w_down"], gate2, up2, dy, no_dep, "ffn2_bwd")
    dep = ex.send_ffn2({n: g[n] for n in ("ffn2_w_gate", "ffn2_w_up", "ffn2_w_down")})
    datt, dopg, dgate, g["w_out"] = _mixout_bwd(att, opg, gate, w["w_out"], dx2, dep)
    dscan = _scan_bwd(seqs, pars, states, dopg)
    for n, d in zip(("r_k", "ln_x_w", "ln_x_b"), dscan[6:]):
        g[n] = d
    dcur, dvec, *dmats = _rwkv_pre_bwd(cur, vec, mats, dscan[:6], dgate)
    for n, d in zip(RWKV_MAT, dmats):
        g[n] = d
    for j, n in enumerate(RWKV_VEC):
        g[n] = dvec[j:j + 1]
    dq, dk, dv, g["q_norm"], g["k_norm"] = _att_bwd(q, k, v, qn, kn, saved, datt)
    dx1, g["mix_norm"], g["w_in"] = _proj_bwd(x1, w["mix_norm"], w["w_in"], dq, dk, dv, dcur, dx2)
    dep = ex.send_mix({n: g[n] for n in ("w_in", "w_out") + RWKV_MAT}, (dx1,))
    dx, g["ffn1_norm"], g["ffn1_w_gate"], g["ffn1_w_up"], g["ffn1_w_down"] = _ffn_bwd(
        x, w["ffn1_norm"], w["ffn1_w_gate"], w["ffn1_w_up"], w["ffn1_w_down"], gate1, up1, dx1, dep, "ffn1_bwd")
    return loss, dx, g


N_SHARDS = 4


def _place():
    return lax.axis_index("x"), lax.axis_index("y"), lax.axis_index("c")


def _chip_peers(x, y):
    return [(1 - x, y), (x, 1 - y), (1 - x, 1 - y)]


HBM = pl.BlockSpec(memory_space=pltpu.HBM)
SEM = pl.BlockSpec(memory_space=pltpu.SEMAPHORE)
DEP_SHAPE = (8, 128)


class _Views:
    to_sibling = False


class _GatherViews(_Views):
    @staticmethod
    def send(i, srcs, lands, k, at):
        return srcs[i], lands[i].at[at[3]]

    @staticmethod
    def landing(i, srcs, lands, k, at):
        return srcs[i], lands[i].at[2 * at[4] + at[5]]


class _ScatterViews(_Views):
    @staticmethod
    def send(i, srcs, lands, k, at):
        return srcs[i].at[2 * at[4] + at[5]], lands[i].at[k]

    @staticmethod
    def landing(i, srcs, lands, k, at):
        return srcs[i].at[at[3]], lands[i].at[k]


def _half_rows(ref, slot, half):
    rows = ref.shape[1] // 2
    return ref.at[slot, pl.ds(pl.multiple_of(half * rows, BF16_SUBLANES), rows)]


class _HalfGatherViews(_Views):
    @staticmethod
    def send(i, srcs, lands, k, at):
        rows = srcs[i].shape[0] // 2
        return srcs[i].at[pl.ds(pl.multiple_of(at[2] * rows, BF16_SUBLANES), rows)], _half_rows(lands[i], at[3], at[2])

    @staticmethod
    def landing(i, srcs, lands, k, at):
        rows = srcs[i].shape[0] // 2
        return srcs[i].at[pl.ds(pl.multiple_of(at[2] * rows, BF16_SUBLANES), rows)], _half_rows(lands[i], 2 * at[4] + at[5], at[2])


class _ForwardViews(_Views):
    to_sibling = True

    @staticmethod
    def send(i, srcs, lands, k, at):
        mine = _half_rows(lands[i], 2 * at[4] + at[5], at[2])
        return mine, mine

    @staticmethod
    def landing(i, srcs, lands, k, at):
        theirs = _half_rows(lands[i], 2 * at[4] + at[5], 1 - at[2])
        return theirs, theirs


def _push_start(srcs, lands, views, after, name):
    ns, nl = len(srcs), len(lands)

    def body(*refs):
        src_refs, land_refs = refs[:ns], refs[ns:ns + nl]
        send_sems, recv_sems = refs[ns + nl + 1:ns + nl + 3]
        token = refs[2 * (ns + nl) + 3]
        x, y, c = _place()
        for i in range(nl):
            for k, (px, py) in enumerate(_chip_peers(x, y)):
                src, dst = views.send(i, src_refs, land_refs, k, (x, y, c, 2 * x + y, px, py))
                pltpu.make_async_remote_copy(
                    src_ref=src, dst_ref=dst, send_sem=send_sems.at[3 * i + k], recv_sem=recv_sems.at[3 * i + k],
                    device_id=(x, y, 1 - c) if views.to_sibling else (px, py, c), device_id_type=MESH).start()
        token[...] = jnp.zeros_like(token)

    sems = pltpu.SemaphoreType.DMA((3 * nl,))
    both = [pltpu.with_memory_space_constraint(a, pltpu.HBM) for a in (*srcs, *lands)]
    outs = pl.pallas_call(
        body, name=name,
        out_shape=(sems, sems, *[pltpu.HBM(a.shape, a.dtype) for a in both], jax.ShapeDtypeStruct(DEP_SHAPE, F32)),
        in_specs=[HBM] * (ns + nl) + [ANY], out_specs=(SEM, SEM, *[HBM] * (ns + nl), VMEM_FULL),
        input_output_aliases={i: 2 + i for i in range(ns + nl)},
        compiler_params=pltpu.CompilerParams(has_side_effects=pltpu.SideEffectType.DATAFLOW_SIDE_EFFECTING),
    )(*both, after)
    return outs[0], outs[1], outs[2:2 + ns], outs[2 + ns:2 + ns + nl], outs[2 + ns + nl]


def _push_wait(started, views, after, name, with_sources=False):
    send_sems, recv_sems, srcs, lands, _ = started
    ns, nl = len(srcs), len(lands)

    def body(*refs):
        src_refs, land_refs = refs[:ns], refs[ns:ns + nl]
        send_sems, recv_sems = refs[ns + nl:ns + nl + 2]
        x, y, c = _place()
        for i in range(nl):
            for k, (px, py) in enumerate(_chip_peers(x, y)):
                src, dst = views.landing(i, src_refs, land_refs, k, (x, y, c, 2 * x + y, px, py))
                landing = pltpu.make_async_remote_copy(
                    src_ref=src, dst_ref=dst, send_sem=send_sems.at[3 * i + k], recv_sem=recv_sems.at[3 * i + k],
                    device_id=(x, y, 1 - c) if views.to_sibling else (px, py, c), device_id_type=MESH)
                landing.wait_send()
                landing.wait_recv()

    outs = pl.pallas_call(
        body, name=name,
        out_shape=tuple(pltpu.HBM(a.shape, a.dtype) for a in (*srcs, *lands)),
        in_specs=[HBM] * (ns + nl) + [SEM, SEM] + [ANY] * len(after), out_specs=(HBM,) * (ns + nl),
        input_output_aliases={i: i for i in range(ns + nl)},
        compiler_params=pltpu.CompilerParams(has_side_effects=pltpu.SideEffectType.DATAFLOW_SIDE_EFFECTING),
    )(*srcs, *lands, send_sems, recv_sems, *after)
    return outs if with_sources else outs[ns:]


def _empty_lands(shards, slots, own_slot):
    lands = [lax.empty((slots,) + s.shape, s.dtype) for s in shards]
    if own_slot:
        me = 2 * lax.axis_index("x") + lax.axis_index("y")
        lands = [lax.dynamic_update_index_in_dim(z, s, me, 0) for z, s in zip(lands, shards)]
    return lands


def _sibling_swap(arrays, name):
    n = len(arrays)

    def body(*refs):
        ins, outs = refs[:n], refs[n:2 * n]
        send_sems, recv_sems = refs[2 * n:]
        x, y, c = _place()
        copies = []
        for i in range(n):
            cp = pltpu.make_async_remote_copy(
                src_ref=ins[i], dst_ref=outs[i], send_sem=send_sems.at[i], recv_sem=recv_sems.at[i],
                device_id=(x, y, 1 - c), device_id_type=MESH)
            cp.start()
            copies.append(cp)
        for cp in copies:
            cp.wait()

    return pl.pallas_call(
        body, name=name,
        out_shape=tuple(jax.ShapeDtypeStruct(a.shape, a.dtype) for a in arrays),
        in_specs=[ANY] * n, out_specs=(ANY,) * n,
        scratch_shapes=[pltpu.SemaphoreType.DMA((n,)), pltpu.SemaphoreType.DMA((n,))],
    )(*arrays)


N_DEV = 8


def _allreduce_small(pack):
    def body(in_ref, out_ref, buf, send_sems, recv_sems):
        x, y, c = _place()
        me = 4 * x + 2 * y + c
        buf[me] = in_ref[...]

        def copy(j, slot):
            px, py, pc = x ^ (j >> 2), y ^ ((j >> 1) & 1), c ^ (j & 1)
            return pltpu.make_async_remote_copy(
                src_ref=in_ref, dst_ref=buf.at[slot(px, py, pc)], send_sem=send_sems.at[j], recv_sem=recv_sems.at[j],
                device_id=(px, py, pc), device_id_type=MESH)

        for j in range(1, N_DEV):
            copy(j, lambda px, py, pc: me).start()
        for j in range(1, N_DEV):
            landing = copy(j, lambda px, py, pc: 4 * px + 2 * py + pc)
            landing.wait_send()
            landing.wait_recv()
        acc = buf[0]
        for s in range(1, N_DEV):
            acc = acc + buf[s]
        out_ref[...] = acc

    return pl.pallas_call(
        body, name="allreduce_small", out_shape=jax.ShapeDtypeStruct(pack.shape, F32),
        in_specs=[VMEM_FULL], out_specs=VMEM_FULL,
        scratch_shapes=[pltpu.VMEM((N_DEV,) + pack.shape, F32), pltpu.SemaphoreType.DMA((N_DEV,)),
                        pltpu.SemaphoreType.DMA((N_DEV,))],
    )(pack)


BF16_SUBLANES = 16


def _reduce_own(me, parts, recvs, dep, steps, name):
    n = len(parts)

    def body(me_ref, *refs):
        for p_ref, rv_ref, o_ref in zip(refs[:n], refs[n:2 * n], refs[2 * n + 1:]):
            acc = p_ref[0].astype(F32)
            for k in range(3):
                acc = acc + rv_ref[k].astype(F32)
            o_ref[...] = acc

    shapes = [(p.shape[1] // steps, p.shape[2]) for p in parts]
    return pl.pallas_call(
        body, name=name, out_shape=tuple(jax.ShapeDtypeStruct(p.shape[1:], F32) for p in parts),
        grid_spec=pltpu.PrefetchScalarGridSpec(
            num_scalar_prefetch=1, grid=(steps,),
            in_specs=[pl.BlockSpec((1, tr, c), lambda i, me_ref: (me_ref[0], i, 0)) for tr, c in shapes]
            + [pl.BlockSpec((3, tr, c), lambda i, me_ref: (0, i, 0)) for tr, c in shapes] + [ANY],
            out_specs=tuple(pl.BlockSpec((tr, c), lambda i, me_ref: (i, 0)) for tr, c in shapes)),
        compiler_params=_params("arbitrary"),
    )(me, *parts, *recvs, dep)


def _adamw(ws, gas, gbs, ms, vs, steps, name):
    n = len(ws)
    c1 = 1.0 - ADAM_B1 ** ADAM_STEP
    c2 = 1.0 - ADAM_B2 ** ADAM_STEP

    def body(*refs):
        ins, outs = refs[:5 * n], refs[5 * n:]
        for j in range(n):
            w_ref, ga_ref, gb_ref, m_ref, v_ref = ins[j::n]
            g_out, d_out, m_out, v_out = outs[j::n]
            g = ga_ref[...] + gb_ref[...]
            mn = ADAM_B1 * m_ref[...] + (1.0 - ADAM_B1) * g
            vn = ADAM_B2 * v_ref[...] + (1.0 - ADAM_B2) * (g * g)
            g_out[...] = g
            m_out[...] = mn
            v_out[...] = vn
            d_out[...] = -ADAM_LR * ((mn / c1) / (jnp.sqrt(vn / c2) + ADAM_EPS) + ADAM_WD * w_ref[...])

    tiles = [pl.BlockSpec((w.shape[0] // steps, w.shape[1]), lambda i: (i, 0)) for w in ws]
    shapes = [jax.ShapeDtypeStruct(w.shape, F32) for w in ws]
    outs = pl.pallas_call(
        body, name=name, grid=(steps,), out_shape=tuple(shapes * 4), in_specs=tiles * 5, out_specs=tuple(tiles * 4),
        compiler_params=_params("arbitrary"),
    )(*ws, *gas, *gbs, *ms, *vs)
    return [outs[j::n] for j in range(n)]


PACK_COLS = 512


def _to_rows(a):
    flat = a.reshape(-1)
    pad = (-flat.shape[0]) % PACK_COLS
    return jnp.pad(flat, (0, pad)).reshape(-1, PACK_COLS)


def _pack(arrays, extra_rows=0):
    rows = [_to_rows(a) for a in arrays]
    n = sum(r.shape[0] for r in rows) + extra_rows
    pad = (-n) % 8
    return jnp.concatenate(rows + [jnp.zeros((extra_rows + pad, PACK_COLS), F32)], axis=0)


def _unpack(pack, like):
    out, at = [], 0
    for a in like:
        n = -(-a.size // PACK_COLS)
        out.append(pack[at:at + n].reshape(-1)[:a.size].reshape(a.shape))
        at += n
    return out


COL_SHARDED = ("ffn1_w_gate", "ffn1_w_up", "w_in", "ffn2_w_gate", "ffn2_w_up", "w2", "a2", "g2")
ROW_SHARDED = ("ffn1_w_down", "ffn2_w_down", "w_out", "w1", "a1", "g1")
CHUNKED = ("ffn1_w_gate", "ffn1_w_up", "ffn1_w_down", "ffn2_w_gate", "ffn2_w_up", "ffn2_w_down")
WEIGHTS = ("ffn1_norm", "ffn1_w_gate", "ffn1_w_up", "ffn1_w_down", "mix_norm", "w_in", "q_norm", "k_norm",
           "mu_r", "mu_k", "mu_v", "mu_w", "mu_a", "mu_g", "w0", "w1", "w2", "a0", "a1", "a2", "g1", "g2",
           "k_k", "k_a", "r_k", "ln_x_w", "ln_x_b", "w_out", "ffn2_norm", "ffn2_w_gate", "ffn2_w_up", "ffn2_w_down")


W_IN_GROUPS = 7
TRANSPOSED = ("ffn1_w_gate", "ffn1_w_up", "ffn2_w_gate", "ffn2_w_up")


def _shard_2d(name, a):
    return a[0].T if name in TRANSPOSED else a[0]


def _full_from_blocks(name, blocks):
    if name in CHUNKED:
        return blocks
    if name in ROW_SHARDED:
        return blocks.reshape(-1, blocks.shape[-1])
    full = blocks.transpose(1, 0, 2).reshape(blocks.shape[1], -1)
    if name == "w_in":
        return full.reshape(full.shape[0], W_IN_GROUPS, -1).transpose(1, 0, 2)
    return full


def _blocks_from_full(name, full):
    if name in CHUNKED:
        return full
    if name in ROW_SHARDED:
        return full.reshape(N_SHARDS, -1, full.shape[-1])
    if name == "w_in":
        full = full.transpose(1, 0, 2).reshape(full.shape[1], -1)
    return full.reshape(full.shape[0], N_SHARDS, -1).transpose(1, 0, 2)


FFN1_GROUP = ("ffn1_w_gate", "ffn1_w_up", "ffn1_w_down")
MIX_GROUP = ("w_in",) + RWKV_MAT
OUT_GROUP = ("w_out", "ffn2_w_gate", "ffn2_w_up", "ffn2_w_down")
FFN2_GROUP = OUT_GROUP[1:]
LATE_GROUP = ("w_in", "w_out") + RWKV_MAT


class _Exchange:
    def __init__(self, given):
        self.given = given
        first = self._gather_start(FFN1_GROUP, _HalfGatherViews, jnp.zeros(DEP_SHAPE, F32), "gather_ffn1_start")
        self.mix = self._gather_start(MIX_GROUP, _GatherViews, first[4], "gather_mix_start")
        self.out = self._gather_start(OUT_GROUP, _GatherViews, self.mix[4], "gather_out_start")
        self.first_dep = self.out[4]
        halves = _push_wait(first, _HalfGatherViews, (self.first_dep,), "gather_ffn1_wait")
        passed = _push_start([], halves, _ForwardViews, halves[0], "gather_ffn1_pass_start")
        self.first_weights = self._full(FFN1_GROUP, _push_wait(passed, _ForwardViews, (passed[4],), "gather_ffn1_pass_wait"))
        self.parts, self.recv = {}, {}

    def _shards(self, names):
        return [_shard_2d(n, self.given[n]).astype(BF16) for n in names]

    @staticmethod
    def _full(names, blocks):
        out = {}
        for n, b in zip(names, blocks):
            full = _full_from_blocks(n, b)
            out[n] = full.astype(F32) if n in RWKV_MAT else full
        return out

    def _gather_start(self, names, views, after, name):
        shards = self._shards(names)
        return _push_start(shards, _empty_lands(shards, N_SHARDS, True), views, after, name)

    def mix_weights(self, after):
        return self._full(MIX_GROUP, _push_wait(self.mix, _GatherViews, after, "gather_mix_wait"))

    def out_weights(self, after):
        return self._full(OUT_GROUP, _push_wait(self.out, _GatherViews, after, "gather_out_wait"))

    def _scatter_start(self, grads, name):
        names = tuple(grads)
        parts = [_blocks_from_full(n, grads[n]) for n in names]
        self.parts.update(zip(names, parts))
        lands = [lax.empty((3,) + p.shape[1:], BF16) for p in parts]
        return _push_start([p.astype(BF16) for p in parts], lands, _ScatterViews, jnp.zeros(DEP_SHAPE, F32), name)

    def _scatter_done(self, started, names, after, name):
        outs = _push_wait(started, _ScatterViews, after, name, with_sources=True)
        for n, sent, got in zip(names, outs[:len(names)], outs[len(names):]):
            self.recv[n] = got
            if self.parts[n].dtype == BF16:
                self.parts[n] = sent

    def send_ffn2(self, grads):
        self.ffn2 = self._scatter_start(grads, "scatter_ffn2_start")
        return self.ffn2[4]

    def send_mix(self, grads, after):
        self._scatter_done(self.ffn2, FFN2_GROUP, after, "scatter_ffn2_wait")
        self.late = self._scatter_start(grads, "scatter_late_start")
        return self.late[4]

    def send_ffn1(self, grads):
        self.ffn1 = self._scatter_start(grads, "scatter_ffn1_start")
        return self.ffn1[4]

    def late_received(self, after):
        self._scatter_done(self.late, LATE_GROUP, after, "scatter_late_wait")

    def ffn1_received(self, after):
        self._scatter_done(self.ffn1, FFN1_GROUP, after, "scatter_ffn1_wait")


def kernel(
        x, ffn1_norm, ffn1_w_gate, ffn1_w_up, ffn1_w_down, mix_norm, w_in, q_norm, k_norm, mu_r, mu_k, mu_v, mu_w,
        mu_a, mu_g, w0, w1, w2, a0, a1, a2, g1, g2, k_k, k_a, r_k, ln_x_w, ln_x_b, w_out, ffn2_norm, ffn2_w_gate,
        ffn2_w_up, ffn2_w_down, loss_target, m_ffn1_norm, m_ffn1_w_gate, m_ffn1_w_up, m_ffn1_w_down, m_mix_norm,
        m_w_in, m_q_norm, m_k_norm, m_mu_r, m_mu_k, m_mu_v, m_mu_w, m_mu_a, m_mu_g, m_w0, m_w1, m_w2, m_a0, m_a1,
        m_a2, m_g1, m_g2, m_k_k, m_k_a, m_r_k, m_ln_x_w, m_ln_x_b, m_w_out, m_ffn2_norm, m_ffn2_w_gate, m_ffn2_w_up,
        m_ffn2_w_down, v_ffn1_norm, v_ffn1_w_gate, v_ffn1_w_up, v_ffn1_w_down, v_mix_norm, v_w_in, v_q_norm, v_k_norm,
        v_mu_r, v_mu_k, v_mu_v, v_mu_w, v_mu_a, v_mu_g, v_w0, v_w1, v_w2, v_a0, v_a1, v_a2, v_g1, v_g2, v_k_k, v_k_a,
        v_r_k, v_ln_x_w, v_ln_x_b, v_w_out, v_ffn2_norm, v_ffn2_w_gate, v_ffn2_w_up, v_ffn2_w_down):
    given = dict(locals())
    sharded = COL_SHARDED + ROW_SHARDED
    sharded = tuple(n for n in WEIGHTS if n in sharded)
    small = tuple(n for n in WEIGHTS if n not in sharded)

    ex = _Exchange(given)
    w = {n: given[n] for n in small}
    w.update(ex.first_weights)
    loss, dx, g = _local_step(x[0], loss_target[0], w, ex)
    dep = ex.send_ffn1({n: g[n] for n in FFN1_GROUP})

    me = (2 * lax.axis_index("x") + lax.axis_index("y")).astype(jnp.int32).reshape(1)
    out = {}

    def settle(names, dep, tag):
        done = []
        for kind, sub, r_steps, a_steps in (("large", tuple(n for n in names if n not in RWKV_MAT), 4, 8),
                                            ("small", tuple(n for n in names if n in RWKV_MAT), 1, 1)):
            if not sub:
                continue
            parts = [ex.parts[n].reshape(N_SHARDS, -1, ex.parts[n].shape[-1]) for n in sub]
            recvs = [ex.recv[n].reshape(3, -1, ex.recv[n].shape[-1]) for n in sub]
            mine = _reduce_own(me, parts, recvs, dep, r_steps, f"reduce_{tag}_{kind}")
            theirs = _sibling_swap(mine, f"sibling_swap_{tag}_{kind}")
            res = _adamw([_shard_2d(n, given[n]) for n in sub], mine, theirs, [_shard_2d(n, given["m_" + n]) for n in sub],
                         [_shard_2d(n, given["v_" + n]) for n in sub], a_steps, f"adamw_{tag}_{kind}")
            for n, rs in zip(sub, res):
                out[n] = [(r.T if n in TRANSPOSED else r).reshape(given[n].shape) for r in rs]
                done.append(out[n][1])
        return tuple(done)

    ex.late_received((dep,))
    last = settle(tuple(n for n in sharded if n not in FFN1_GROUP), dep, "rest")

    gpack = _pack([g[n] for n in small], extra_rows=1)
    n_rows = sum(-(-given[n].size // PACK_COLS) for n in small)
    gpack = gpack.at[n_rows, :loss.shape[1]].set(loss[0])
    gsum = _allreduce_small(gpack)
    res = _adamw([_pack([given[n] for n in small], 1)], [gsum], [jnp.zeros_like(gsum)], [_pack([given["m_" + n] for n in small], 1)],
                 [_pack([given["v_" + n] for n in small], 1)], 1, "adamw_replicated")[0]
    like = [given[n] for n in small]
    for j, r in enumerate(res):
        for n, a in zip(small, _unpack(r, like)):
            out.setdefault(n, [None] * 4)[j] = a
    total_loss = gsum[n_rows, 0]

    ex.ffn1_received((*last, res[1]))
    settle(FFN1_GROUP, jnp.zeros(DEP_SHAPE, F32), "ffn1")
    return (total_loss, dx[None], *[out[n][0] for n in WEIGHTS], *[out[n][1] for n in WEIGHTS],
            *[out[n][2] for n in WEIGHTS], *[out[n][3] for n in WEIGHTS])
```

```python
import functools

import jax
import jax.numpy as jnp
from jax import lax
from jax.experimental import pallas as pl
from jax.experimental.pallas import tpu as pltpu

F32 = jnp.float32
BF16 = jnp.bfloat16
MESH = pl.DeviceIdType.MESH

RMS_EPS = 1e-6
GN_EPS = 64e-5
NEG_INF = -1e30
FFN_RESIDUAL = 0.5
HEAD_DIM = 64
ATT_BLOCK = 128
DILATIONS = (1, 4, 16)
SCAN_CHUNK = 64
TOKEN_TILE = 256
FFN_BWD_TILE = 512

ADAM_LR = 0.001
ADAM_B1 = 0.9
ADAM_B2 = 0.999
ADAM_EPS = 1e-08
ADAM_WD = 0.01
ADAM_STEP = 10

VMEM_FULL = pl.BlockSpec(memory_space=pltpu.VMEM)
ANY = pl.BlockSpec(memory_space=pl.ANY)


VMEM_LIMIT = 56 * 1024 * 1024


def _params(*sem):
    return pltpu.CompilerParams(dimension_semantics=sem, vmem_limit_bytes=VMEM_LIMIT)


def _dot(a, b, dims):
    return lax.dot_general(a.astype(BF16), b.astype(BF16), (dims, ((), ())), preferred_element_type=F32)


def _dot_nn(a, b):
    return _dot(a, b, ((1,), (0,)))


def _dot_nt(a, b):
    return _dot(a, b, ((1,), (1,)))


def _dot_tn(a, b):
    return _dot(a, b, ((0,), (0,)))


@jax.custom_vjp
def _mm(a, b):
    return _dot_nn(a, b)


def _mm_fwd(a, b):
    return _dot_nn(a, b), (a, b)


def _mm_bwd(res, g):
    a, b = res
    return _dot_nt(g, b).astype(a.dtype), _dot_tn(a, g).astype(b.dtype)


_mm.defvjp(_mm_fwd, _mm_bwd)


def _bdot(a, b, ca, cb):
    return lax.dot_general(a.astype(BF16), b.astype(BF16), (((ca,), (cb,)), ((0,), (0,))), preferred_element_type=F32)


@jax.custom_vjp
def _bmm_nt(a, b):
    return _bdot(a, b, 2, 2)


def _bmm_nt_fwd(a, b):
    return _bdot(a, b, 2, 2), (a, b)


def _bmm_nt_bwd(res, g):
    a, b = res
    return _bdot(g, b, 2, 1), _bdot(g, a, 1, 1)


_bmm_nt.defvjp(_bmm_nt_fwd, _bmm_nt_bwd)


@jax.custom_vjp
def _bmm_nn(a, b):
    return _bdot(a, b, 2, 1)


def _bmm_nn_fwd(a, b):
    return _bdot(a, b, 2, 1), (a, b)


def _bmm_nn_bwd(res, g):
    a, b = res
    return _bdot(g, b, 2, 2), _bdot(a, g, 1, 1)


_bmm_nn.defvjp(_bmm_nn_fwd, _bmm_nn_bwd)


@jax.custom_vjp
def _bmm_tn(a, b):
    return _bdot(a, b, 1, 1)


def _bmm_tn_fwd(a, b):
    return _bdot(a, b, 1, 1), (a, b)


def _bmm_tn_bwd(res, g):
    a, b = res
    return _bdot(b, g, 2, 2), _bdot(a, g, 2, 1)


_bmm_tn.defvjp(_bmm_tn_fwd, _bmm_tn_bwd)


def _hdot(a, b, ca, cb):
    return lax.dot_general(a, b, (((ca,), (cb,)), ((0,), (0,))), precision=lax.Precision.HIGH, preferred_element_type=F32)


def _sigmoid(x):
    return 1.0 / (1.0 + jnp.exp(-x))


def _rms(x):
    return lax.rsqrt(jnp.mean(x * x, axis=-1, keepdims=True) + RMS_EPS)


def _ffn_fwd(x, norm, wg, wu, wd, dep, name):
    t, d = x.shape
    nc, fc, _ = wg.shape
    tm = TOKEN_TILE

    def body(x_ref, n_ref, wg_ref, wu_ref, wd_ref, dep_ref, o_ref, g_ref, u_ref):
        xv = x_ref[...]
        h = (xv * _rms(xv) * n_ref[...]).astype(BF16)
        acc = jnp.zeros((tm, d), F32)
        for c in range(nc):
            g = _dot_nt(h, wg_ref[c])
            u = _dot_nt(h, wu_ref[c])
            g_ref[c] = g.astype(BF16)
            u_ref[c] = u.astype(BF16)
            a = (g * _sigmoid(g) * u).astype(BF16)
            acc = acc + jnp.dot(a, wd_ref[c], preferred_element_type=F32)
        o_ref[...] = xv + FFN_RESIDUAL * acc

    tile = pl.BlockSpec((tm, d), lambda i: (i, 0))
    hidden = pl.BlockSpec((nc, tm, fc), lambda i: (0, i, 0))
    hshape = jax.ShapeDtypeStruct((nc, t, fc), BF16)
    return pl.pallas_call(
        body, name=name, grid=(t // tm,), out_shape=(jax.ShapeDtypeStruct((t, d), F32), hshape, hshape),
        in_specs=[tile, pl.BlockSpec((1, d), lambda i: (0, 0)), VMEM_FULL, VMEM_FULL, VMEM_FULL, ANY],
        out_specs=(tile, hidden, hidden), compiler_params=_params("arbitrary"),
    )(x, norm, wg, wu, wd, dep)


def _rmsnorm_bwd(xv, gain, dh):
    rs = _rms(xv)
    xn = xv * rs
    dxn = dh * gain
    dx = rs * (dxn - xn * jnp.mean(dxn * xn, axis=-1, keepdims=True))
    return dx, jnp.sum(dh * xn, axis=0, keepdims=True)


def _ffn_bwd(x, norm, wg, wu, wd, gate, up, dy, dep, name):
    t, d = x.shape
    nc, fc, _ = wg.shape
    tm = FFN_BWD_TILE
    nt = t // tm

    def body(x_ref, n_ref, wg_ref, wu_ref, wd_ref, g_ref, u_ref, dy_ref, dep_ref, dx_ref, dn_ref, dwg_ref, dwu_ref,
             dwd_ref, dh_ref, ag_ref, au_ref, ad_ref):
        c, i = pl.program_id(0), pl.program_id(1)
        rows = pl.ds(pl.multiple_of(i * tm, tm), tm)
        xv = x_ref[...]
        gain = n_ref[...]
        h = (xv * _rms(xv) * gain).astype(BF16)
        dy = dy_ref[...]
        dyb = (FFN_RESIDUAL * dy).astype(BF16)
        g = g_ref[0].astype(F32)
        u = u_ref[0].astype(F32)
        sg = _sigmoid(g)
        s = g * sg
        a = (s * u).astype(BF16)
        da = _dot_nt(dyb, wd_ref[0])
        dub = (da * s).astype(BF16)
        dgb = (da * u * (sg * (1.0 + g * (1.0 - sg)))).astype(BF16)
        dwd_c = _dot_tn(a, dyb)
        dwg_c = _dot_tn(dgb, h)
        dwu_c = _dot_tn(dub, h)
        dh_c = _dot_nn(dgb, wg_ref[0]) + _dot_nn(dub, wu_ref[0])

        @pl.when(i == 0)
        def _():
            ad_ref[...] = dwd_c
            ag_ref[...] = dwg_c
            au_ref[...] = dwu_c

        @pl.when(i > 0)
        def _():
            ad_ref[...] += dwd_c
            ag_ref[...] += dwg_c
            au_ref[...] += dwu_c

        @pl.when(i == nt - 1)
        def _():
            dwd_ref[0] = ad_ref[...].astype(BF16)
            dwg_ref[0] = ag_ref[...].astype(BF16)
            dwu_ref[0] = au_ref[...].astype(BF16)

        @pl.when(c == 0)
        def _():
            dh_ref[rows, :] = dh_c

        @pl.when(c > 0)
        def _():
            dh_ref[rows, :] += dh_c

        @pl.when(c == nc - 1)
        def _():
            dx, dn = _rmsnorm_bwd(xv, gain, dh_ref[rows, :])
            dx_ref[...] = dx + dy

            @pl.when(i == 0)
            def _():
                dn_ref[...] = dn

            @pl.when(i > 0)
            def _():
                dn_ref[...] += dn

    tile = pl.BlockSpec((tm, d), lambda c, i: (i, 0))
    row = pl.BlockSpec((1, d), lambda c, i: (0, 0))
    wrow = pl.BlockSpec((1, fc, d), lambda c, i: (c, 0, 0), pipeline_mode=pl.Buffered(1))
    hidden = pl.BlockSpec((1, tm, fc), lambda c, i: (c, i, 0))
    last = pl.BlockSpec((tm, d), lambda c, i: (jnp.where(c == nc - 1, i, 0), 0))
    return pl.pallas_call(
        body, name=name, grid=(nc, nt),
        out_shape=(jax.ShapeDtypeStruct((t, d), F32), jax.ShapeDtypeStruct((1, d), F32),
                   jax.ShapeDtypeStruct(wg.shape, BF16), jax.ShapeDtypeStruct(wu.shape, BF16),
                   jax.ShapeDtypeStruct(wd.shape, BF16)),
        in_specs=[tile, row, wrow, wrow, wrow, hidden, hidden, tile, ANY],
        out_specs=(last, row, wrow, wrow, wrow),
        scratch_shapes=[pltpu.VMEM((t, d), F32)] + [pltpu.VMEM((fc, d), F32)] * 3,
        compiler_params=_params("arbitrary", "arbitrary"),
    )(x, norm, wg, wu, wd, gate, up, dy, dep)


def _store_heads(ref, v):
    for h in range(ref.shape[0]):
        ref[h] = v[:, h * HEAD_DIM:(h + 1) * HEAD_DIM]


def _load_heads(ref):
    return jnp.concatenate([ref[h] for h in range(ref.shape[0])], axis=-1)


N_HEAD_GROUPS = 3


def _proj_fwd(x, norm, w, c):
    t, d = x.shape
    nc, _, ncol = w.shape
    nh = c // HEAD_DIM
    tm = TOKEN_TILE
    wide = nc * ncol - N_HEAD_GROUPS * c

    def body(x_ref, n_ref, w_ref, q_ref, k_ref, v_ref, cur_ref):
        xv = x_ref[...]
        h = (xv * _rms(xv) * n_ref[...]).astype(BF16)
        full = jnp.concatenate([jnp.dot(h, w_ref[s], preferred_element_type=F32) for s in range(nc)], axis=1)
        for m, ref in enumerate((q_ref, k_ref, v_ref)):
            _store_heads(ref, full[:, m * c:(m + 1) * c])
        cur_ref[...] = full[:, N_HEAD_GROUPS * c:]

    heads = pl.BlockSpec((nh, tm, HEAD_DIM), lambda i: (0, i, 0))
    hshape = jax.ShapeDtypeStruct((nh, t, HEAD_DIM), F32)
    return pl.pallas_call(
        body, name="proj_fwd", grid=(t // tm,),
        out_shape=(hshape, hshape, hshape, jax.ShapeDtypeStruct((t, wide), F32)),
        in_specs=[pl.BlockSpec((tm, d), lambda i: (i, 0)), pl.BlockSpec((1, d), lambda i: (0, 0)), VMEM_FULL],
        out_specs=(heads, heads, heads, pl.BlockSpec((tm, wide), lambda i: (i, 0))),
        compiler_params=_params("arbitrary"),
    )(x, norm, w)


def _proj_bwd(x, norm, w, dq, dk, dv, dcur, dres):
    t, d = x.shape
    nc, _, ncol = w.shape
    nh = dq.shape[0]
    tm = TOKEN_TILE
    nt = t // tm
    wide = dcur.shape[1]

    def body(x_ref, n_ref, w_ref, dq_ref, dk_ref, dv_ref, dcur_ref, dres_ref, dx_ref, dn_ref, dw_ref, acc_ref):
        i = pl.program_id(0)

        @pl.when(i == 0)
        def _():
            acc_ref[...] = jnp.zeros_like(acc_ref)
            dn_ref[...] = jnp.zeros_like(dn_ref)

        xv = x_ref[...]
        gain = n_ref[...]
        h = (xv * _rms(xv) * gain).astype(BF16)
        dp = jnp.concatenate([_load_heads(dq_ref), _load_heads(dk_ref), _load_heads(dv_ref), dcur_ref[...]], axis=1).astype(BF16)
        dh = jnp.zeros((tm, d), F32)
        for s in range(nc):
            dps = dp[:, s * ncol:(s + 1) * ncol]
            acc_ref[s] += _dot_tn(h, dps)
            dh = dh + _dot_nt(dps, w_ref[s])
        dx, dn = _rmsnorm_bwd(xv, gain, dh)
        dx_ref[...] = dx + dres_ref[...]
        dn_ref[...] += dn

        @pl.when(i == nt - 1)
        def _():
            dw_ref[...] = acc_ref[...].astype(BF16)

    tile = pl.BlockSpec((tm, d), lambda i: (i, 0))
    row = pl.BlockSpec((1, d), lambda i: (0, 0))
    heads = pl.BlockSpec((nh, tm, HEAD_DIM), lambda i: (0, i, 0))
    return pl.pallas_call(
        body, name="proj_bwd", grid=(nt,),
        out_shape=(jax.ShapeDtypeStruct((t, d), F32), jax.ShapeDtypeStruct((1, d), F32),
                   jax.ShapeDtypeStruct(w.shape, BF16)),
        in_specs=[tile, row, VMEM_FULL, heads, heads, heads, pl.BlockSpec((tm, wide), lambda i: (i, 0)), tile],
        out_specs=(tile, row, VMEM_FULL),
        scratch_shapes=[pltpu.VMEM(w.shape, F32)], compiler_params=_params("arbitrary"),
    )(x, norm, w, dq, dk, dv, dcur, dres)


def _mixout_fwd(x, att, opg, gate, w):
    t, d = x.shape
    nh = att.shape[0]
    half = gate.shape[1]
    tm = TOKEN_TILE

    def body(x_ref, att_ref, opg_ref, g_ref, w_ref, o_ref):
        mix = jnp.concatenate([_load_heads(att_ref), _load_heads(opg_ref) * g_ref[...]], axis=-1).astype(BF16)
        o_ref[...] = x_ref[...] + jnp.dot(mix, w_ref[...], preferred_element_type=F32)

    tile = pl.BlockSpec((tm, d), lambda i: (i, 0))
    htile = pl.BlockSpec((tm, half), lambda i: (i, 0))
    heads = pl.BlockSpec((nh, tm, HEAD_DIM), lambda i: (0, i, 0))
    return pl.pallas_call(
        body, name="mixout_fwd", grid=(t // tm,), out_shape=jax.ShapeDtypeStruct((t, d), F32),
        in_specs=[tile, heads, heads, htile, VMEM_FULL], out_specs=tile, compiler_params=_params("arbitrary"),
    )(x, att, opg, gate, w)


def _mixout_bwd(att, opg, gate, w, dy, dep):
    nh, t, _ = att.shape
    half = gate.shape[1]
    d = dy.shape[1]
    tm = TOKEN_TILE

    def body(att_ref, opg_ref, g_ref, w_ref, dy_ref, dep_ref, datt_ref, dopg_ref, dg_ref, dw_ref):
        i = pl.program_id(0)
        opg_v, g_v = _load_heads(opg_ref), g_ref[...]
        mix = jnp.concatenate([_load_heads(att_ref), opg_v * g_v], axis=-1).astype(BF16)
        dyb = dy_ref[...].astype(BF16)
        dmix = _dot_nt(dyb, w_ref[...])
        dw = _dot_tn(mix, dyb)
        _store_heads(datt_ref, dmix[:, :half])
        drw = dmix[:, half:]
        _store_heads(dopg_ref, drw * g_v)
        dg_ref[...] = drw * opg_v

        @pl.when(i == 0)
        def _():
            dw_ref[...] = dw

        @pl.when(i > 0)
        def _():
            dw_ref[...] += dw

    tile = pl.BlockSpec((tm, d), lambda i: (i, 0))
    htile = pl.BlockSpec((tm, half), lambda i: (i, 0))
    heads = pl.BlockSpec((nh, tm, HEAD_DIM), lambda i: (0, i, 0))
    hshape = jax.ShapeDtypeStruct((nh, t, HEAD_DIM), F32)
    return pl.pallas_call(
        body, name="mixout_bwd", grid=(t // tm,),
        out_shape=(hshape, hshape, jax.ShapeDtypeStruct((t, half), F32), jax.ShapeDtypeStruct(w.shape, F32)),
        in_specs=[heads, heads, htile, VMEM_FULL, tile, ANY],
        out_specs=(heads, heads, htile, pl.BlockSpec(w.shape, lambda i: (0, 0))),
        compiler_params=_params("arbitrary"),
    )(att, opg, gate, w, dy, dep)


def _loss_head(y, target):
    t, d = y.shape
    tm = TOKEN_TILE

    def body(y_ref, t_ref, dy_ref, loss_ref):
        i = pl.program_id(0)
        err = y_ref[...] - t_ref[...]
        dy_ref[...] = err * (1.0 / d)
        part = 0.5 * jnp.sum(jnp.mean(err * err, axis=-1, keepdims=True), axis=0, keepdims=True)

        @pl.when(i == 0)
        def _():
            loss_ref[...] = jnp.zeros_like(loss_ref)

        loss_ref[...] += jnp.broadcast_to(part, loss_ref.shape)

    tile = pl.BlockSpec((tm, d), lambda i: (i, 0))
    return pl.pallas_call(
        body, name="loss_head", grid=(t // tm,),
        out_shape=(jax.ShapeDtypeStruct((t, d), F32), jax.ShapeDtypeStruct((1, 128), F32)),
        in_specs=[tile, tile], out_specs=(tile, pl.BlockSpec((1, 128), lambda i: (0, 0))),
        compiler_params=_params("arbitrary"),
    )(y, target)


def _head_norm(x, gain):
    return x * _rms(x) * gain


def _att_pattern(qh, kh, v, nb):
    g, blk, _ = qh.shape
    scale = HEAD_DIM ** -0.5
    qi = lax.broadcasted_iota(jnp.int32, (blk, blk), 0)
    kj = lax.broadcasted_iota(jnp.int32, (blk, blk), 1)
    sc = jnp.where(kj <= qi, _bmm_nt(qh, kh) * scale, NEG_INF)
    top = jnp.max(sc, axis=-1, keepdims=True)
    if nb > 1:
        khp = jnp.concatenate([kh[:1], kh[:-1]], axis=0)
        vp = jnp.concatenate([v[:1], v[:-1]], axis=0)
        has_prev = lax.broadcasted_iota(jnp.int32, (g, 1, 1), 0) % nb != 0
        sp = jnp.where((kj >= qi) & has_prev, _bmm_nt(qh, khp) * scale, NEG_INF)
        top = jnp.maximum(top, jnp.max(sp, axis=-1, keepdims=True))
    m = lax.stop_gradient(top)
    pc = jnp.exp(sc - m)
    den = jnp.sum(pc, axis=-1, keepdims=True)
    acc = _bmm_nn(pc, v)
    if nb > 1:
        pp = jnp.exp(sp - m)
        den = den + jnp.sum(pp, axis=-1, keepdims=True)
        acc = acc + _bmm_nn(pp, vp)
    o = acc / den
    return o, jnp.broadcast_to(m + jnp.log(den), o.shape)


def _pattern_rows(t, dil):
    nb = t // (ATT_BLOCK * dil)
    starts = [n * ATT_BLOCK * dil + r for r in range(dil) for n in range(nb)]
    return [pl.ds(s, ATT_BLOCK, stride=dil) if dil > 1 else pl.ds(s, ATT_BLOCK) for s in starts], nb


def _take(ref, rows):
    return jnp.stack([ref[0, r, :] for r in rows])


def _put(ref, rows, val):
    for g, r in enumerate(rows):
        ref[0, r, :] = val[g]


def _put_add(ref, rows, val):
    for g, r in enumerate(rows):
        ref[0, r, :] += val[g]


def _merge_fn(o1, o2, o3, l1, l2, l3):
    m = lax.stop_gradient(jnp.maximum(jnp.maximum(l1, l2), l3))
    e1, e2, e3 = jnp.exp(l1 - m), jnp.exp(l2 - m), jnp.exp(l3 - m)
    return (e1 * o1 + e2 * o2 + e3 * o3) / (e1 + e2 + e3)


def _token_rows(j):
    return pl.ds(pl.multiple_of(j * ATT_BLOCK, ATT_BLOCK), ATT_BLOCK)


def _norm_rows(t, q_ref, k_ref, gq, gk, qh_ref, kh_ref):
    def step(j, carry):
        rows = _token_rows(j)
        qh_ref[0, rows, :] = _head_norm(q_ref[0, rows, :], gq[0])
        kh_ref[0, rows, :] = _head_norm(k_ref[0, rows, :], gk[0])
        return carry

    lax.fori_loop(0, t // ATT_BLOCK, step, 0)


def _att_head_specs(t):
    head = pl.BlockSpec((1, t, HEAD_DIM), lambda h: (h, 0, 0))
    gain = pl.BlockSpec((1, 1, HEAD_DIM), lambda h: (0, 0, 0))
    return head, gain


def _att_fwd(q, k, v, qn, kn):
    nh, t, dh = q.shape
    head, gain = _att_head_specs(t)

    def body(q_ref, k_ref, v_ref, qn_ref, kn_ref, att_ref, o1, o2, o3, l1, l2, l3, qh_ref, kh_ref):
        saved = (o1, o2, o3, l1, l2, l3)
        _norm_rows(t, q_ref, k_ref, qn_ref[...], kn_ref[...], qh_ref, kh_ref)
        for p, dil in enumerate(DILATIONS):
            rows, nb = _pattern_rows(t, dil)
            o, lse = _att_pattern(_take(qh_ref, rows), _take(kh_ref, rows), _take(v_ref, rows), nb)
            _put(saved[p], rows, o)
            _put(saved[3 + p], rows, lse)

        def merge(j, carry):
            rows = _token_rows(j)
            att_ref[0, rows, :] = _merge_fn(*[r[0, rows, :] for r in saved])
            return carry

        lax.fori_loop(0, t // ATT_BLOCK, merge, 0)

    return pl.pallas_call(
        body, name="att_fwd", grid=(nh,), out_shape=(jax.ShapeDtypeStruct(q.shape, F32),) * 7,
        in_specs=[head, head, head, gain, gain], out_specs=(head,) * 7,
        scratch_shapes=[pltpu.VMEM((1, t, dh), F32)] * 2, compiler_params=_params("arbitrary"),
    )(q, k, v, qn, kn)


def _att_bwd(q, k, v, qn, kn, saved, datt):
    nh, t, dh = q.shape
    head, gain = _att_head_specs(t)

    def body(q_ref, k_ref, v_ref, qn_ref, kn_ref, o1, o2, o3, l1, l2, l3, datt_ref,
             dq_ref, dk_ref, dv_ref, dqn_ref, dkn_ref, qh_ref, kh_ref, dqh_ref, dkh_ref, *ct_refs):
        for ref in (dqh_ref, dkh_ref, dv_ref):
            ref[...] = jnp.zeros_like(ref)

        @pl.when(pl.program_id(0) == 0)
        def _():
            dqn_ref[...] = jnp.zeros_like(dqn_ref)
            dkn_ref[...] = jnp.zeros_like(dkn_ref)

        gq, gk = qn_ref[...], kn_ref[...]
        _norm_rows(t, q_ref, k_ref, gq, gk, qh_ref, kh_ref)

        def merge_cotangents(j, carry):
            rows = _token_rows(j)
            _, merge_vjp = jax.vjp(_merge_fn, *[r[0, rows, :] for r in (o1, o2, o3, l1, l2, l3)])
            for ref, val in zip(ct_refs, merge_vjp(datt_ref[0, rows, :])):
                ref[0, rows, :] = val
            return carry

        lax.fori_loop(0, t // ATT_BLOCK, merge_cotangents, 0)

        for p, dil in enumerate(DILATIONS):
            rows, nb = _pattern_rows(t, dil)
            _, pattern_vjp = jax.vjp(functools.partial(_att_pattern, nb=nb), _take(qh_ref, rows), _take(kh_ref, rows),
                                     _take(v_ref, rows))
            dqh, dkh, dv = pattern_vjp((_take(ct_refs[p], rows), _take(ct_refs[3 + p], rows)))
            _put_add(dqh_ref, rows, dqh)
            _put_add(dkh_ref, rows, dkh)
            _put_add(dv_ref, rows, dv)

        def norm_cotangents(j, carry):
            rows = _token_rows(j)
            out = []
            for x_ref, gain, dh_ref, dx_ref, acc in ((q_ref, gq, dqh_ref, dq_ref, carry[0]), (k_ref, gk, dkh_ref, dk_ref, carry[1])):
                _, norm_vjp = jax.vjp(_head_norm, x_ref[0, rows, :], gain[0])
                dx, dgain = norm_vjp(dh_ref[0, rows, :])
                dx_ref[0, rows, :] = dx
                out.append(acc + dgain)
            return tuple(out)

        zero = jnp.zeros((1, dh), F32)
        dgq, dgk = lax.fori_loop(0, t // ATT_BLOCK, norm_cotangents, (zero, zero))
        dqn_ref[0] += dgq
        dkn_ref[0] += dgk

    hshape = jax.ShapeDtypeStruct(q.shape, F32)
    gshape = jax.ShapeDtypeStruct((1, 1, dh), F32)
    return pl.pallas_call(
        body, name="att_bwd", grid=(nh,), out_shape=(hshape, hshape, hshape, gshape, gshape),
        in_specs=[head, head, head, gain, gain] + [head] * 7, out_specs=(head, head, head, gain, gain),
        scratch_shapes=[pltpu.VMEM((1, t, dh), F32)] * 10, compiler_params=_params("arbitrary"),
    )(q, k, v, qn, kn, *saved, datt)


RWKV_VEC = ("mu_r", "mu_k", "mu_v", "mu_w", "mu_a", "mu_g", "w0", "a0", "k_k", "k_a")
RWKV_MAT = ("w1", "w2", "a1", "a2", "g1", "g2")


def _rwkv_pre_fn(cur, prev, vec, w1, w2, a1, a2, g1, g2):
    c = cur.shape[1] // 4
    mu_r, mu_k, mu_v, mu_w, mu_a, mu_g, w0, a0, k_k, k_a = (vec[j:j + 1] for j in range(10))

    def lerp(j, mu):
        xc, xp = cur[:, j * c:(j + 1) * c], prev[:, j * c:(j + 1) * c]
        return xc + (xp - xc) * mu

    r, k, v = lerp(0, mu_r), lerp(1, mu_k), lerp(2, mu_v)
    cw, ca, cg = lerp(3, mu_w), lerp(3, mu_a), lerp(3, mu_g)
    z = w0 + _mm(jnp.tanh(_mm(cw, w1)), w2)
    w_log = jnp.minimum(z, 0.0) - jnp.log(1.0 + jnp.exp(-jnp.abs(z))) - 0.5
    lw = -jnp.exp(w_log)
    a = _sigmoid(a0 + _mm(_mm(ca, a1), a2))
    gate = _mm(_sigmoid(_mm(cg, g1)), g2)
    kkraw = k * k_k
    kmod = k * (1.0 + (a - 1.0) * k_a)
    return r, lw, kmod, v, kkraw, a, gate


HALO_ROWS = 8


def _rwkv_pre_specs(c, mats, tile_of):
    tm = TOKEN_TILE
    nh = c // HEAD_DIM
    wide = pl.BlockSpec((tm, 4 * c), lambda j: (tile_of(j), 0))
    halo = pl.BlockSpec((HALO_ROWS, 4 * c), lambda j: (jnp.maximum(tile_of(j) * (tm // HALO_ROWS) - 1, 0), 0))
    one = pl.BlockSpec((tm, c), lambda j: (tile_of(j), 0))
    heads = pl.BlockSpec((nh, tm, HEAD_DIM), lambda j: (0, tile_of(j), 0))
    vec = pl.BlockSpec((10, c), lambda j: (0, 0))
    mspecs = [pl.BlockSpec(m.shape, lambda j: (0, 0)) for m in mats]
    return wide, halo, one, heads, vec, mspecs


def _previous_rows(cur, halo, tile):
    first = jnp.where(tile > 0, halo[HALO_ROWS - 1:HALO_ROWS], 0.0)
    rows = lax.broadcasted_iota(jnp.int32, cur.shape, 0)
    return jnp.where(rows == 0, first, pltpu.roll(cur, 1, axis=0))


def _rwkv_pre_fwd(cur, vec, mats):
    t, c4 = cur.shape
    c = c4 // 4
    wide, halo, one, heads, vspec, mspecs = _rwkv_pre_specs(c, mats, lambda j: j)

    def body(cur_ref, halo_ref, vec_ref, *rest):
        mrefs, outs = rest[:6], rest[6:]
        cur_v = cur_ref[...]
        prev = _previous_rows(cur_v, halo_ref[...], pl.program_id(0))
        vals = _rwkv_pre_fn(cur_v, prev, vec_ref[...], *(m[...] for m in mrefs))
        for ref, val in zip(outs[:6], vals[:6]):
            _store_heads(ref, val)
        outs[6][...] = vals[6]

    hshape = jax.ShapeDtypeStruct((c // HEAD_DIM, t, HEAD_DIM), F32)
    return pl.pallas_call(
        body, name="rwkv_pre_fwd", grid=(t // TOKEN_TILE,), out_shape=(hshape,) * 6 + (jax.ShapeDtypeStruct((t, c), F32),),
        in_specs=[wide, halo, vspec] + mspecs, out_specs=(heads,) * 6 + (one,), compiler_params=_params("arbitrary"),
    )(cur, cur, vec, *mats)


def _rwkv_pre_bwd(cur, vec, mats, cts, dgate):
    t, c4 = cur.shape
    c = c4 // 4
    tm = TOKEN_TILE
    nt = t // tm
    wide, halo, one, heads, vspec, mspecs = _rwkv_pre_specs(c, mats, lambda j: nt - 1 - j)

    def body(cur_ref, halo_ref, vec_ref, *rest):
        mrefs, ctrefs, dgate_ref, outs, carry_ref = rest[:6], rest[6:12], rest[12], rest[13:-1], rest[-1]
        j = pl.program_id(0)

        @pl.when(j == 0)
        def _():
            carry_ref[...] = jnp.zeros_like(carry_ref)
            for ref in outs[1:]:
                ref[...] = jnp.zeros_like(ref)

        cur_v = cur_ref[...]
        prev = _previous_rows(cur_v, halo_ref[...], nt - 1 - j)
        _, vjp = jax.vjp(_rwkv_pre_fn, cur_v, prev, vec_ref[...], *(m[...] for m in mrefs))
        grads = vjp(tuple(_load_heads(r) for r in ctrefs) + (dgate_ref[...],))
        dprev = grads[1]
        rows = lax.broadcasted_iota(jnp.int32, dprev.shape, 0)
        outs[0][...] = grads[0] + jnp.where(rows == tm - 1, carry_ref[0:1], pltpu.roll(dprev, tm - 1, axis=0))
        carry_ref[0:1] = dprev[0:1]
        for ref, val in zip(outs[1:], grads[2:]):
            ref[...] += val

    return pl.pallas_call(
        body, name="rwkv_pre_bwd", grid=(nt,),
        out_shape=(jax.ShapeDtypeStruct(cur.shape, F32), jax.ShapeDtypeStruct(vec.shape, F32))
        + tuple(jax.ShapeDtypeStruct(m.shape, F32) for m in mats),
        in_specs=[wide, halo, vspec] + mspecs + [heads] * 6 + [one], out_specs=(wide, vspec) + tuple(mspecs),
        scratch_shapes=[pltpu.VMEM((HALO_ROWS, c4), F32)], compiler_params=_params("arbitrary"),
    )(cur, cur, vec, *mats, *cts, dgate)


def _scan_chunk_fn(h0, r, lw, k, v, kkraw, a, rk, lnw, lnb):
    n = r.shape[1]
    nrm = jnp.sqrt(jnp.sum(kkraw * kkraw, axis=-1, keepdims=True))
    kk = kkraw / jnp.maximum(nrm, 1e-12)
    av, bv = -kk, kk * a
    ti = lax.broadcasted_iota(jnp.int32, (n, n), 0)
    si = lax.broadcasted_iota(jnp.int32, (n, n), 1)
    incl, strict = ti >= si, ti > si
    ones = jnp.broadcast_to(incl.astype(F32)[None], (r.shape[0], n, n))
    cum = _hdot(ones, lw, 2, 1)
    at, rt = av * jnp.exp(cum - lw), r * jnp.exp(cum)
    inv = jnp.exp(-cum)
    bt, kt = bv * inv, k * inv
    gram = _hdot(jnp.concatenate([at, rt], axis=1), jnp.concatenate([bt, kt], axis=1), 2, 2)
    lab = jnp.where(strict, gram[:, :n, :n], 0.0)
    lak = jnp.where(strict, gram[:, :n, n:], 0.0)
    rb = jnp.where(incl, gram[:, n:, :n], 0.0)
    rkm = jnp.where(incl, gram[:, n:, n:], 0.0)
    nv = v.shape[2]
    u = _bmm_nn(jnp.concatenate([at, lak], axis=2), jnp.concatenate([h0, v], axis=1))
    p = lab
    m = 2
    while m < n:
        both = _bmm_nn(p, jnp.concatenate([u, p], axis=2))
        u, p = u + both[:, :, :nv], both[:, :, nv:]
        m *= 2
    u = u + _bmm_nn(p, u)
    y = _bmm_nn(jnp.concatenate([rt, rb, rkm], axis=2), jnp.concatenate([h0, u, v], axis=1))
    last = jnp.exp(jnp.sum(lw, axis=1, keepdims=True))
    h1 = jnp.swapaxes(last, 1, 2) * (h0 + _bmm_tn(jnp.concatenate([bt, kt], axis=1), jnp.concatenate([u, v], axis=1)))
    mean = jnp.mean(y, axis=-1, keepdims=True)
    yc = y - mean
    var = jnp.mean(yc * yc, axis=-1, keepdims=True)
    yn = yc * lax.rsqrt(var + GN_EPS) * lnw + lnb
    bonus = jnp.sum(r * k * rk, axis=-1, keepdims=True) * v
    return yn + bonus, h1


SCAN_GROUP = 2


def _scan_group_fn(h0, r, lw, k, v, kkraw, a, rk, lnw, lnb):
    outs = []
    for j in range(SCAN_GROUP):
        rows = slice(j * SCAN_CHUNK, (j + 1) * SCAN_CHUNK)
        o, h0 = _scan_chunk_fn(h0, r[:, rows], lw[:, rows], k[:, rows], v[:, rows], kkraw[:, rows], a[:, rows], rk, lnw, lnb)
        outs.append(o)
    return jnp.concatenate(outs, axis=1), h0


def _scan_specs(h, t, dh, rev):
    n = SCAN_CHUNK * SCAN_GROUP
    nc = t // n
    pos = (lambda c: (0, nc - 1 - c, 0)) if rev else (lambda c: (0, c, 0))
    st = (lambda c: (nc - 1 - c, 0, 0, 0)) if rev else (lambda c: (c, 0, 0, 0))
    seq = pl.BlockSpec((h, n, dh), pos)
    par = pl.BlockSpec((h, 1, dh), lambda c: (0, 0, 0))
    state = pl.BlockSpec((1, h, dh, dh), st)
    return seq, par, state


def _scan_fwd(seqs, pars):
    h, t, dh = seqs[0].shape
    nc = t // (SCAN_CHUNK * SCAN_GROUP)
    seq, par, state = _scan_specs(h, t, dh, False)

    def body(r, lw, k, v, kkraw, a, rk, lnw, lnb, o_ref, st_ref, h_ref):
        @pl.when(pl.program_id(0) == 0)
        def _():
            h_ref[...] = jnp.zeros_like(h_ref)

        h0 = h_ref[...]
        st_ref[0] = h0
        o, h1 = _scan_group_fn(h0, r[...], lw[...], k[...], v[...], kkraw[...], a[...], rk[...], lnw[...], lnb[...])
        o_ref[...] = o
        h_ref[...] = h1

    return pl.pallas_call(
        body, name="rwkv_scan_fwd", grid=(nc,),
        out_shape=(jax.ShapeDtypeStruct((h, t, dh), F32), jax.ShapeDtypeStruct((nc, h, dh, dh), F32)),
        in_specs=[seq] * 6 + [par] * 3, out_specs=(seq, state),
        scratch_shapes=[pltpu.VMEM((h, dh, dh), F32)], compiler_params=_params("arbitrary"),
    )(*seqs, *pars)


def _scan_bwd(seqs, pars, states, do):
    h, t, dh = seqs[0].shape
    nc = t // (SCAN_CHUNK * SCAN_GROUP)
    seq, par, state = _scan_specs(h, t, dh, True)

    def body(r, lw, k, v, kkraw, a, rk, lnw, lnb, st_ref, do_ref, *rest):
        douts, dpars, dh_ref = rest[:6], rest[6:9], rest[9]
        first = pl.program_id(0) == 0

        @pl.when(first)
        def _():
            dh_ref[...] = jnp.zeros_like(dh_ref)

        _, vjp = jax.vjp(_scan_group_fn, st_ref[0], r[...], lw[...], k[...], v[...], kkraw[...], a[...],
                         rk[...], lnw[...], lnb[...])
        grads = vjp((do_ref[...], dh_ref[...]))
        dh_ref[...] = grads[0]
        for ref, val in zip(douts, grads[1:7]):
            ref[...] = val

        @pl.when(first)
        def _():
            for ref, val in zip(dpars, grads[7:]):
                ref[...] = val

        @pl.when(jnp.logical_not(first))
        def _():
            for ref, val in zip(dpars, grads[7:]):
                ref[...] += val

    sshape = jax.ShapeDtypeStruct((h, t, dh), F32)
    pshape = jax.ShapeDtypeStruct((h, 1, dh), F32)
    return pl.pallas_call(
        body, name="rwkv_scan_bwd", grid=(nc,), out_shape=(sshape,) * 6 + (pshape,) * 3,
        in_specs=[seq] * 6 + [par] * 3 + [state, seq], out_specs=(seq,) * 6 + (par,) * 3,
        scratch_shapes=[pltpu.VMEM((h, dh, dh), F32)], compiler_params=_params("arbitrary"),
    )(*seqs, *pars, states, do)


def _local_step(x, target, w, ex):
    w = dict(w)
    c = w["mu_r"].shape[-1]
    qn, kn = w["q_norm"].reshape(1, 1, HEAD_DIM), w["k_norm"].reshape(1, 1, HEAD_DIM)
    vec = jnp.concatenate([w[n].reshape(1, c) for n in RWKV_VEC], axis=0)
    pars = [w[n].reshape(-1, 1, HEAD_DIM) for n in ("r_k", "ln_x_w", "ln_x_b")]
    no_dep = jnp.zeros(DEP_SHAPE, F32)

    x1, gate1, up1 = _ffn_fwd(x, w["ffn1_norm"], w["ffn1_w_gate"], w["ffn1_w_up"], w["ffn1_w_down"], ex.first_dep, "ffn1_fwd")
    w.update(ex.mix_weights((x1,)))
    mats = [w[n] for n in RWKV_MAT]
    q, k, v, cur = _proj_fwd(x1, w["mix_norm"], w["w_in"], c)
    att, *saved = _att_fwd(q, k, v, qn, kn)
    pre = _rwkv_pre_fwd(cur, vec, mats)
    seqs, gate = pre[:6], pre[6]
    opg, states = _scan_fwd(seqs, pars)
    w.update(ex.out_weights((att, opg)))
    x2 = _mixout_fwd(x1, att, opg, gate, w["w_out"])
    x3, gate2, up2 = _ffn_fwd(x2, w["ffn2_norm"], w["ffn2_w_gate"], w["ffn2_w_up"], w["ffn2_w_down"], no_dep, "ffn2_fwd")
    dy, loss = _loss_head(x3, target)

    g = {}
    dx2, g["ffn2_norm"], g["ffn2_w_gate"], g["ffn2_w_up"], g["ffn2_w_down"] = _ffn_bwd(
        x2, w["ffn2_norm"], w["ffn2_w_gate"], w["ffn2_w_up"], w["ffn2_w_down"], gate2, up2, dy, no_dep, "ffn2_bwd")
    dep = ex.send_ffn2({n: g[n] for n in ("ffn2_w_gate", "ffn2_w_up", "ffn2_w_down")})
    datt, dopg, dgate, g["w_out"] = _mixout_bwd(att, opg, gate, w["w_out"], dx2, dep)
    dscan = _scan_bwd(seqs, pars, states, dopg)
    for n, d in zip(("r_k", "ln_x_w", "ln_x_b"), dscan[6:]):
        g[n] = d
    dcur, dvec, *dmats = _rwkv_pre_bwd(cur, vec, mats, dscan[:6], dgate)
    for n, d in zip(RWKV_MAT, dmats):
        g[n] = d
    for j, n in enumerate(RWKV_VEC):
        g[n] = dvec[j:j + 1]
    dq, dk, dv, g["q_norm"], g["k_norm"] = _att_bwd(q, k, v, qn, kn, saved, datt)
    dx1, g["mix_norm"], g["w_in"] = _proj_bwd(x1, w["mix_norm"], w["w_in"], dq, dk, dv, dcur, dx2)
    dep = ex.send_mix({n: g[n] for n in ("w_in", "w_out") + RWKV_MAT}, (dx1,))
    dx, g["ffn1_norm"], g["ffn1_w_gate"], g["ffn1_w_up"], g["ffn1_w_down"] = _ffn_bwd(
        x, w["ffn1_norm"], w["ffn1_w_gate"], w["ffn1_w_up"], w["ffn1_w_down"], gate1, up1, dx1, dep, "ffn1_bwd")
    return loss, dx, g


N_SHARDS = 4


def _place():
    return lax.axis_index("x"), lax.axis_index("y"), lax.axis_index("c")


def _chip_peers(x, y):
    return [(1 - x, y), (x, 1 - y), (1 - x, 1 - y)]


HBM = pl.BlockSpec(memory_space=pltpu.HBM)
SEM = pl.BlockSpec(memory_space=pltpu.SEMAPHORE)
DEP_SHAPE = (8, 128)


class _Views:
    to_sibling = False


class _GatherViews(_Views):
    @staticmethod
    def send(i, srcs, lands, k, at):
        return srcs[i], lands[i].at[at[3]]

    @staticmethod
    def landing(i, srcs, lands, k, at):
        return srcs[i], lands[i].at[2 * at[4] + at[5]]


class _ScatterViews(_Views):
    @staticmethod
    def send(i, srcs, lands, k, at):
        return srcs[i].at[2 * at[4] + at[5]], lands[i].at[k]

    @staticmethod
    def landing(i, srcs, lands, k, at):
        return srcs[i].at[at[3]], lands[i].at[k]


def _half_rows(ref, slot, half):
    rows = ref.shape[1] // 2
    return ref.at[slot, pl.ds(pl.multiple_of(half * rows, BF16_SUBLANES), rows)]


class _HalfGatherViews(_Views):
    @staticmethod
    def send(i, srcs, lands, k, at):
        rows = srcs[i].shape[0] // 2
        return srcs[i].at[pl.ds(pl.multiple_of(at[2] * rows, BF16_SUBLANES), rows)], _half_rows(lands[i], at[3], at[2])

    @staticmethod
    def landing(i, srcs, lands, k, at):
        rows = srcs[i].shape[0] // 2
        return srcs[i].at[pl.ds(pl.multiple_of(at[2] * rows, BF16_SUBLANES), rows)], _half_rows(lands[i], 2 * at[4] + at[5], at[2])


class _ForwardViews(_Views):
    to_sibling = True

    @staticmethod
    def send(i, srcs, lands, k, at):
        mine = _half_rows(lands[i], 2 * at[4] + at[5], at[2])
        return mine, mine

    @staticmethod
    def landing(i, srcs, lands, k, at):
        theirs = _half_rows(lands[i], 2 * at[4] + at[5], 1 - at[2])
        return theirs, theirs


def _push_start(srcs, lands, views, after, name):
    ns, nl = len(srcs), len(lands)

    def body(*refs):
        src_refs, land_refs = refs[:ns], refs[ns:ns + nl]
        send_sems, recv_sems = refs[ns + nl + 1:ns + nl + 3]
        token = refs[2 * (ns + nl) + 3]
        x, y, c = _place()
        for i in range(nl):
            for k, (px, py) in enumerate(_chip_peers(x, y)):
                src, dst = views.send(i, src_refs, land_refs, k, (x, y, c, 2 * x + y, px, py))
                pltpu.make_async_remote_copy(
                    src_ref=src, dst_ref=dst, send_sem=send_sems.at[3 * i + k], recv_sem=recv_sems.at[3 * i + k],
                    device_id=(x, y, 1 - c) if views.to_sibling else (px, py, c), device_id_type=MESH).start()
        token[...] = jnp.zeros_like(token)

    sems = pltpu.SemaphoreType.DMA((3 * nl,))
    both = [pltpu.with_memory_space_constraint(a, pltpu.HBM) for a in (*srcs, *lands)]
    outs = pl.pallas_call(
        body, name=name,
        out_shape=(sems, sems, *[pltpu.HBM(a.shape, a.dtype) for a in both], jax.ShapeDtypeStruct(DEP_SHAPE, F32)),
        in_specs=[HBM] * (ns + nl) + [ANY], out_specs=(SEM, SEM, *[HBM] * (ns + nl), VMEM_FULL),
        input_output_aliases={i: 2 + i for i in range(ns + nl)},
        compiler_params=pltpu.CompilerParams(has_side_effects=pltpu.SideEffectType.DATAFLOW_SIDE_EFFECTING),
    )(*both, after)
    return outs[0], outs[1], outs[2:2 + ns], outs[2 + ns:2 + ns + nl], outs[2 + ns + nl]


def _push_wait(started, views, after, name, with_sources=False):
    send_sems, recv_sems, srcs, lands, _ = started
    ns, nl = len(srcs), len(lands)

    def body(*refs):
        src_refs, land_refs = refs[:ns], refs[ns:ns + nl]
        send_sems, recv_sems = refs[ns + nl:ns + nl + 2]
        x, y, c = _place()
        for i in range(nl):
            for k, (px, py) in enumerate(_chip_peers(x, y)):
                src, dst = views.landing(i, src_refs, land_refs, k, (x, y, c, 2 * x + y, px, py))
                landing = pltpu.make_async_remote_copy(
                    src_ref=src, dst_ref=dst, send_sem=send_sems.at[3 * i + k], recv_sem=recv_sems.at[3 * i + k],
                    device_id=(x, y, 1 - c) if views.to_sibling else (px, py, c), device_id_type=MESH)
                landing.wait_send()
                landing.wait_recv()

    outs = pl.pallas_call(
        body, name=name,
        out_shape=tuple(pltpu.HBM(a.shape, a.dtype) for a in (*srcs, *lands)),
        in_specs=[HBM] * (ns + nl) + [SEM, SEM] + [ANY] * len(after), out_specs=(HBM,) * (ns + nl),
        input_output_aliases={i: i for i in range(ns + nl)},
        compiler_params=pltpu.CompilerParams(has_side_effects=pltpu.SideEffectType.DATAFLOW_SIDE_EFFECTING),
    )(*srcs, *lands, send_sems, recv_sems, *after)
    return outs if with_sources else outs[ns:]


def _empty_lands(shards, slots, own_slot):
    lands = [lax.empty((slots,) + s.shape, s.dtype) for s in shards]
    if own_slot:
        me = 2 * lax.axis_index("x") + lax.axis_index("y")
        lands = [lax.dynamic_update_index_in_dim(z, s, me, 0) for z, s in zip(lands, shards)]
    return lands


def _sibling_swap(arrays, name):
    n = len(arrays)

    def body(*refs):
        ins, outs = refs[:n], refs[n:2 * n]
        send_sems, recv_sems = refs[2 * n:]
        x, y, c = _place()
        copies = []
        for i in range(n):
            cp = pltpu.make_async_remote_copy(
                src_ref=ins[i], dst_ref=outs[i], send_sem=send_sems.at[i], recv_sem=recv_sems.at[i],
                device_id=(x, y, 1 - c), device_id_type=MESH)
            cp.start()
            copies.append(cp)
        for cp in copies:
            cp.wait()

    return pl.pallas_call(
        body, name=name,
        out_shape=tuple(jax.ShapeDtypeStruct(a.shape, a.dtype) for a in arrays),
        in_specs=[ANY] * n, out_specs=(ANY,) * n,
        scratch_shapes=[pltpu.SemaphoreType.DMA((n,)), pltpu.SemaphoreType.DMA((n,))],
    )(*arrays)


N_DEV = 8


def _allreduce_small(pack):
    def body(in_ref, out_ref, buf, send_sems, recv_sems):
        x, y, c = _place()
        me = 4 * x + 2 * y + c
        buf[me] = in_ref[...]

        def copy(j, slot):
            px, py, pc = x ^ (j >> 2), y ^ ((j >> 1) & 1), c ^ (j & 1)
            return pltpu.make_async_remote_copy(
                src_ref=in_ref, dst_ref=buf.at[slot(px, py, pc)], send_sem=send_sems.at[j], recv_sem=recv_sems.at[j],
                device_id=(px, py, pc), device_id_type=MESH)

        for j in range(1, N_DEV):
            copy(j, lambda px, py, pc: me).start()
        for j in range(1, N_DEV):
            landing = copy(j, lambda px, py, pc: 4 * px + 2 * py + pc)
            landing.wait_send()
            landing.wait_recv()
        acc = buf[0]
        for s in range(1, N_DEV):
            acc = acc + buf[s]
        out_ref[...] = acc

    return pl.pallas_call(
        body, name="allreduce_small", out_shape=jax.ShapeDtypeStruct(pack.shape, F32),
        in_specs=[VMEM_FULL], out_specs=VMEM_FULL,
        scratch_shapes=[pltpu.VMEM((N_DEV,) + pack.shape, F32), pltpu.SemaphoreType.DMA((N_DEV,)),
                        pltpu.SemaphoreType.DMA((N_DEV,))],
    )(pack)


BF16_SUBLANES = 16


def _reduce_own(me, parts, recvs, dep, steps, name):
    n = len(parts)

    def body(me_ref, *refs):
        for p_ref, rv_ref, o_ref in zip(refs[:n], refs[n:2 * n], refs[2 * n + 1:]):
            acc = p_ref[0].astype(F32)
            for k in range(3):
                acc = acc + rv_ref[k].astype(F32)
            o_ref[...] = acc

    shapes = [(p.shape[1] // steps, p.shape[2]) for p in parts]
    return pl.pallas_call(
        body, name=name, out_shape=tuple(jax.ShapeDtypeStruct(p.shape[1:], F32) for p in parts),
        grid_spec=pltpu.PrefetchScalarGridSpec(
            num_scalar_prefetch=1, grid=(steps,),
            in_specs=[pl.BlockSpec((1, tr, c), lambda i, me_ref: (me_ref[0], i, 0)) for tr, c in shapes]
            + [pl.BlockSpec((3, tr, c), lambda i, me_ref: (0, i, 0)) for tr, c in shapes] + [ANY],
            out_specs=tuple(pl.BlockSpec((tr, c), lambda i, me_ref: (i, 0)) for tr, c in shapes)),
        compiler_params=_params("arbitrary"),
    )(me, *parts, *recvs, dep)


def _adamw(ws, gas, gbs, ms, vs, steps, name):
    n = len(ws)
    c1 = 1.0 - ADAM_B1 ** ADAM_STEP
    c2 = 1.0 - ADAM_B2 ** ADAM_STEP

    def body(*refs):
        ins, outs = refs[:5 * n], refs[5 * n:]
        for j in range(n):
            w_ref, ga_ref, gb_ref, m_ref, v_ref = ins[j::n]
            g_out, d_out, m_out, v_out = outs[j::n]
            g = ga_ref[...] + gb_ref[...]
            mn = ADAM_B1 * m_ref[...] + (1.0 - ADAM_B1) * g
            vn = ADAM_B2 * v_ref[...] + (1.0 - ADAM_B2) * (g * g)
            g_out[...] = g
            m_out[...] = mn
            v_out[...] = vn
            d_out[...] = -ADAM_LR * ((mn / c1) / (jnp.sqrt(vn / c2) + ADAM_EPS) + ADAM_WD * w_ref[...])

    tiles = [pl.BlockSpec((w.shape[0] // steps, w.shape[1]), lambda i: (i, 0)) for w in ws]
    shapes = [jax.ShapeDtypeStruct(w.shape, F32) for w in ws]
    outs = pl.pallas_call(
        body, name=name, grid=(steps,), out_shape=tuple(shapes * 4), in_specs=tiles * 5, out_specs=tuple(tiles * 4),
        compiler_params=_params("arbitrary"),
    )(*ws, *gas, *gbs, *ms, *vs)
    return [outs[j::n] for j in range(n)]


PACK_COLS = 512


def _to_rows(a):
    flat = a.reshape(-1)
    pad = (-flat.shape[0]) % PACK_COLS
    return jnp.pad(flat, (0, pad)).reshape(-1, PACK_COLS)


def _pack(arrays, extra_rows=0):
    rows = [_to_rows(a) for a in arrays]
    n = sum(r.shape[0] for r in rows) + extra_rows
    pad = (-n) % 8
    return jnp.concatenate(rows + [jnp.zeros((extra_rows + pad, PACK_COLS), F32)], axis=0)


def _unpack(pack, like):
    out, at = [], 0
    for a in like:
        n = -(-a.size // PACK_COLS)
        out.append(pack[at:at + n].reshape(-1)[:a.size].reshape(a.shape))
        at += n
    return out


COL_SHARDED = ("ffn1_w_gate", "ffn1_w_up", "w_in", "ffn2_w_gate", "ffn2_w_up", "w2", "a2", "g2")
ROW_SHARDED = ("ffn1_w_down", "ffn2_w_down", "w_out", "w1", "a1", "g1")
CHUNKED = ("ffn1_w_gate", "ffn1_w_up", "ffn1_w_down", "w_in", "ffn2_w_gate", "ffn2_w_up", "ffn2_w_down")
WEIGHTS = ("ffn1_norm", "ffn1_w_gate", "ffn1_w_up", "ffn1_w_down", "mix_norm", "w_in", "q_norm", "k_norm",
           "mu_r", "mu_k", "mu_v", "mu_w", "mu_a", "mu_g", "w0", "w1", "w2", "a0", "a1", "a2", "g1", "g2",
           "k_k", "k_a", "r_k", "ln_x_w", "ln_x_b", "w_out", "ffn2_norm", "ffn2_w_gate", "ffn2_w_up", "ffn2_w_down")


TRANSPOSED = ("ffn1_w_gate", "ffn1_w_up", "ffn2_w_gate", "ffn2_w_up")


def _shard_2d(name, a):
    return a[0].T if name in TRANSPOSED else a[0]


def _full_from_blocks(name, blocks):
    if name in CHUNKED:
        return blocks
    if name in ROW_SHARDED:
        return blocks.reshape(-1, blocks.shape[-1])
    return blocks.transpose(1, 0, 2).reshape(blocks.shape[1], -1)


def _blocks_from_full(name, full):
    if name in CHUNKED:
        return full
    if name in ROW_SHARDED:
        return full.reshape(N_SHARDS, -1, full.shape[-1])
    return full.reshape(full.shape[0], N_SHARDS, -1).transpose(1, 0, 2)


FFN1_GROUP = ("ffn1_w_gate", "ffn1_w_up", "ffn1_w_down")
MIX_GROUP = ("w_in",) + RWKV_MAT
OUT_GROUP = ("w_out", "ffn2_w_gate", "ffn2_w_up", "ffn2_w_down")
FFN2_GROUP = OUT_GROUP[1:]
LATE_GROUP = ("w_in", "w_out") + RWKV_MAT


class _Exchange:
    def __init__(self, given):
        self.given = given
        first = self._gather_start(FFN1_GROUP, _HalfGatherViews, jnp.zeros(DEP_SHAPE, F32), "gather_ffn1_start")
        self.mix = self._gather_start(MIX_GROUP, _GatherViews, first[4], "gather_mix_start")
        self.out = self._gather_start(OUT_GROUP, _GatherViews, self.mix[4], "gather_out_start")
        self.first_dep = self.out[4]
        halves = _push_wait(first, _HalfGatherViews, (self.first_dep,), "gather_ffn1_wait")
        passed = _push_start([], halves, _ForwardViews, halves[0], "gather_ffn1_pass_start")
        self.first_weights = self._full(FFN1_GROUP, _push_wait(passed, _ForwardViews, (passed[4],), "gather_ffn1_pass_wait"))
        self.parts, self.recv = {}, {}

    def _shards(self, names):
        return [_shard_2d(n, self.given[n]).astype(BF16) for n in names]

    @staticmethod
    def _full(names, blocks):
        out = {}
        for n, b in zip(names, blocks):
            full = _full_from_blocks(n, b)
            out[n] = full.astype(F32) if n in RWKV_MAT else full
        return out

    def _gather_start(self, names, views, after, name):
        shards = self._shards(names)
        return _push_start(shards, _empty_lands(shards, N_SHARDS, True), views, after, name)

    def mix_weights(self, after):
        return self._full(MIX_GROUP, _push_wait(self.mix, _GatherViews, after, "gather_mix_wait"))

    def out_weights(self, after):
        return self._full(OUT_GROUP, _push_wait(self.out, _GatherViews, after, "gather_out_wait"))

    def _scatter_start(self, grads, name):
        names = tuple(grads)
        parts = [_blocks_from_full(n, grads[n]) for n in names]
        self.parts.update(zip(names, parts))
        lands = [lax.empty((3,) + p.shape[1:], BF16) for p in parts]
        return _push_start([p.astype(BF16) for p in parts], lands, _ScatterViews, jnp.zeros(DEP_SHAPE, F32), name)

    def _scatter_done(self, started, names, after, name):
        outs = _push_wait(started, _ScatterViews, after, name, with_sources=True)
        for n, sent, got in zip(names, outs[:len(names)], outs[len(names):]):
            self.recv[n] = got
            if self.parts[n].dtype == BF16:
                self.parts[n] = sent

    def send_ffn2(self, grads):
        self.ffn2 = self._scatter_start(grads, "scatter_ffn2_start")
        return self.ffn2[4]

    def send_mix(self, grads, after):
        self._scatter_done(self.ffn2, FFN2_GROUP, after, "scatter_ffn2_wait")
        self.late = self._scatter_start(grads, "scatter_late_start")
        return self.late[4]

    def send_ffn1(self, grads):
        self.ffn1 = self._scatter_start(grads, "scatter_ffn1_start")
        return self.ffn1[4]

    def late_received(self, after):
        self._scatter_done(self.late, LATE_GROUP, after, "scatter_late_wait")

    def ffn1_received(self, after):
        self._scatter_done(self.ffn1, FFN1_GROUP, after, "scatter_ffn1_wait")


def kernel(
        x, ffn1_norm, ffn1_w_gate, ffn1_w_up, ffn1_w_down, mix_norm, w_in, q_norm, k_norm, mu_r, mu_k, mu_v, mu_w,
        mu_a, mu_g, w0, w1, w2, a0, a1, a2, g1, g2, k_k, k_a, r_k, ln_x_w, ln_x_b, w_out, ffn2_norm, ffn2_w_gate,
        ffn2_w_up, ffn2_w_down, loss_target, m_ffn1_norm, m_ffn1_w_gate, m_ffn1_w_up, m_ffn1_w_down, m_mix_norm,
        m_w_in, m_q_norm, m_k_norm, m_mu_r, m_mu_k, m_mu_v, m_mu_w, m_mu_a, m_mu_g, m_w0, m_w1, m_w2, m_a0, m_a1,
        m_a2, m_g1, m_g2, m_k_k, m_k_a, m_r_k, m_ln_x_w, m_ln_x_b, m_w_out, m_ffn2_norm, m_ffn2_w_gate, m_ffn2_w_up,
        m_ffn2_w_down, v_ffn1_norm, v_ffn1_w_gate, v_ffn1_w_up, v_ffn1_w_down, v_mix_norm, v_w_in, v_q_norm, v_k_norm,
        v_mu_r, v_mu_k, v_mu_v, v_mu_w, v_mu_a, v_mu_g, v_w0, v_w1, v_w2, v_a0, v_a1, v_a2, v_g1, v_g2, v_k_k, v_k_a,
        v_r_k, v_ln_x_w, v_ln_x_b, v_w_out, v_ffn2_norm, v_ffn2_w_gate, v_ffn2_w_up, v_ffn2_w_down):
    given = dict(locals())
    sharded = COL_SHARDED + ROW_SHARDED
    sharded = tuple(n for n in WEIGHTS if n in sharded)
    small = tuple(n for n in WEIGHTS if n not in sharded)

    ex = _Exchange(given)
    w = {n: given[n] for n in small}
    w.update(ex.first_weights)
    loss, dx, g = _local_step(x[0], loss_target[0], w, ex)
    dep = ex.send_ffn1({n: g[n] for n in FFN1_GROUP})

    me = (2 * lax.axis_index("x") + lax.axis_index("y")).astype(jnp.int32).reshape(1)
    out = {}

    def settle(names, dep, tag):
        done = []
        for kind, sub, r_steps, a_steps in (("large", tuple(n for n in names if n not in RWKV_MAT), 4, 8),
                                            ("small", tuple(n for n in names if n in RWKV_MAT), 1, 1)):
            if not sub:
                continue
            parts = [ex.parts[n].reshape(N_SHARDS, -1, ex.parts[n].shape[-1]) for n in sub]
            recvs = [ex.recv[n].reshape(3, -1, ex.recv[n].shape[-1]) for n in sub]
            mine = _reduce_own(me, parts, recvs, dep, r_steps, f"reduce_{tag}_{kind}")
            theirs = _sibling_swap(mine, f"sibling_swap_{tag}_{kind}")
            res = _adamw([_shard_2d(n, given[n]) for n in sub], mine, theirs, [_shard_2d(n, given["m_" + n]) for n in sub],
                         [_shard_2d(n, given["v_" + n]) for n in sub], a_steps, f"adamw_{tag}_{kind}")
            for n, rs in zip(sub, res):
                out[n] = [(r.T if n in TRANSPOSED else r).reshape(given[n].shape) for r in rs]
                done.append(out[n][1])
        return tuple(done)

    ex.late_received((dep,))
    last = settle(tuple(n for n in sharded if n not in FFN1_GROUP), dep, "rest")

    gpack = _pack([g[n] for n in small], extra_rows=1)
    n_rows = sum(-(-given[n].size // PACK_COLS) for n in small)
    gpack = gpack.at[n_rows, :loss.shape[1]].set(loss[0])
    gsum = _allreduce_small(gpack)
    res = _adamw([_pack([given[n] for n in small], 1)], [gsum], [jnp.zeros_like(gsum)], [_pack([given["m_" + n] for n in small], 1)],
                 [_pack([given["v_" + n] for n in small], 1)], 1, "adamw_replicated")[0]
    like = [given[n] for n in small]
    for j, r in enumerate(res):
        for n, a in zip(small, _unpack(r, like)):
            out.setdefault(n, [None] * 4)[j] = a
    total_loss = gsum[n_rows, 0]

    ex.ffn1_received((*last, res[1]))
    settle(FFN1_GROUP, jnp.zeros(DEP_SHAPE, F32), "ffn1")
    return (total_loss, dx[None], *[out[n][0] for n in WEIGHTS], *[out[n][1] for n in WEIGHTS],
            *[out[n][2] for n in WEIGHTS], *[out[n][3] for n in WEIGHTS])
```

```python
import functools

import jax
import jax.numpy as jnp
from jax import lax
from jax.experimental import pallas as pl
from jax.experimental.pallas import tpu as pltpu

F32 = jnp.float32
BF16 = jnp.bfloat16
MESH = pl.DeviceIdType.MESH

RMS_EPS = 1e-6
GN_EPS = 64e-5
NEG_INF = -1e30
FFN_RESIDUAL = 0.5
HEAD_DIM = 64
ATT_BLOCK = 128
DILATIONS = (1, 4, 16)
SCAN_CHUNK = 64
TOKEN_TILE = 256
FFN_BWD_TILE = 512

ADAM_LR = 0.001
ADAM_B1 = 0.9
ADAM_B2 = 0.999
ADAM_EPS = 1e-08
ADAM_WD = 0.01
ADAM_STEP = 10

VMEM_FULL = pl.BlockSpec(memory_space=pltpu.VMEM)
ANY = pl.BlockSpec(memory_space=pl.ANY)


VMEM_LIMIT = 56 * 1024 * 1024


def _params(*sem):
    return pltpu.CompilerParams(dimension_semantics=sem, vmem_limit_bytes=VMEM_LIMIT)


def _dot(a, b, dims):
    return lax.dot_general(a.astype(BF16), b.astype(BF16), (dims, ((), ())), preferred_element_type=F32)


def _dot_nn(a, b):
    return _dot(a, b, ((1,), (0,)))


def _dot_nt(a, b):
    return _dot(a, b, ((1,), (1,)))


def _dot_tn(a, b):
    return _dot(a, b, ((0,), (0,)))


@jax.custom_vjp
def _mm(a, b):
    return _dot_nn(a, b)


def _mm_fwd(a, b):
    return _dot_nn(a, b), (a, b)


def _mm_bwd(res, g):
    a, b = res
    return _dot_nt(g, b).astype(a.dtype), _dot_tn(a, g).astype(b.dtype)


_mm.defvjp(_mm_fwd, _mm_bwd)


def _bdot(a, b, ca, cb):
    return lax.dot_general(a.astype(BF16), b.astype(BF16), (((ca,), (cb,)), ((0,), (0,))), preferred_element_type=F32)


@jax.custom_vjp
def _bmm_nt(a, b):
    return _bdot(a, b, 2, 2)


def _bmm_nt_fwd(a, b):
    return _bdot(a, b, 2, 2), (a, b)


def _bmm_nt_bwd(res, g):
    a, b = res
    return _bdot(g, b, 2, 1), _bdot(g, a, 1, 1)


_bmm_nt.defvjp(_bmm_nt_fwd, _bmm_nt_bwd)


@jax.custom_vjp
def _bmm_nn(a, b):
    return _bdot(a, b, 2, 1)


def _bmm_nn_fwd(a, b):
    return _bdot(a, b, 2, 1), (a, b)


def _bmm_nn_bwd(res, g):
    a, b = res
    return _bdot(g, b, 2, 2), _bdot(a, g, 1, 1)


_bmm_nn.defvjp(_bmm_nn_fwd, _bmm_nn_bwd)


@jax.custom_vjp
def _bmm_tn(a, b):
    return _bdot(a, b, 1, 1)


def _bmm_tn_fwd(a, b):
    return _bdot(a, b, 1, 1), (a, b)


def _bmm_tn_bwd(res, g):
    a, b = res
    return _bdot(b, g, 2, 2), _bdot(a, g, 2, 1)


_bmm_tn.defvjp(_bmm_tn_fwd, _bmm_tn_bwd)


def _hdot(a, b, ca, cb):
    return lax.dot_general(a, b, (((ca,), (cb,)), ((0,), (0,))), precision=lax.Precision.HIGH, preferred_element_type=F32)


def _sigmoid(x):
    return 1.0 / (1.0 + jnp.exp(-x))


def _rms(x):
    return lax.rsqrt(jnp.mean(x * x, axis=-1, keepdims=True) + RMS_EPS)


def _ffn_fwd(x, norm, wg, wu, wd, dep, name):
    t, d = x.shape
    nc, fc, _ = wg.shape
    tm = TOKEN_TILE

    def body(x_ref, n_ref, wg_ref, wu_ref, wd_ref, dep_ref, o_ref, g_ref, u_ref):
        xv = x_ref[...]
        h = (xv * _rms(xv) * n_ref[...]).astype(BF16)
        acc = jnp.zeros((tm, d), F32)
        for c in range(nc):
            g = _dot_nt(h, wg_ref[c])
            u = _dot_nt(h, wu_ref[c])
            g_ref[c] = g.astype(BF16)
            u_ref[c] = u.astype(BF16)
            a = (g * _sigmoid(g) * u).astype(BF16)
            acc = acc + jnp.dot(a, wd_ref[c], preferred_element_type=F32)
        o_ref[...] = xv + FFN_RESIDUAL * acc

    tile = pl.BlockSpec((tm, d), lambda i: (i, 0))
    hidden = pl.BlockSpec((nc, tm, fc), lambda i: (0, i, 0))
    hshape = jax.ShapeDtypeStruct((nc, t, fc), BF16)
    return pl.pallas_call(
        body, name=name, grid=(t // tm,), out_shape=(jax.ShapeDtypeStruct((t, d), F32), hshape, hshape),
        in_specs=[tile, pl.BlockSpec((1, d), lambda i: (0, 0)), VMEM_FULL, VMEM_FULL, VMEM_FULL, ANY],
        out_specs=(tile, hidden, hidden), compiler_params=_params("arbitrary"),
    )(x, norm, wg, wu, wd, dep)


def _rmsnorm_bwd(xv, gain, dh):
    rs = _rms(xv)
    xn = xv * rs
    dxn = dh * gain
    dx = rs * (dxn - xn * jnp.mean(dxn * xn, axis=-1, keepdims=True))
    return dx, jnp.sum(dh * xn, axis=0, keepdims=True)


def _ffn_bwd(x, norm, wg, wu, wd, gate, up, dy, dep, name):
    t, d = x.shape
    nc, fc, _ = wg.shape
    tm = FFN_BWD_TILE
    nt = t // tm

    def body(x_ref, n_ref, wg_ref, wu_ref, wd_ref, g_ref, u_ref, dy_ref, dep_ref, dx_ref, dn_ref, dwg_ref, dwu_ref,
             dwd_ref, dh_ref, ag_ref, au_ref, ad_ref):
        c, i = pl.program_id(0), pl.program_id(1)
        rows = pl.ds(pl.multiple_of(i * tm, tm), tm)
        xv = x_ref[...]
        gain = n_ref[...]
        h = (xv * _rms(xv) * gain).astype(BF16)
        dy = dy_ref[...]
        dyb = (FFN_RESIDUAL * dy).astype(BF16)
        g = g_ref[0].astype(F32)
        u = u_ref[0].astype(F32)
        sg = _sigmoid(g)
        s = g * sg
        a = (s * u).astype(BF16)
        da = _dot_nt(dyb, wd_ref[0])
        dub = (da * s).astype(BF16)
        dgb = (da * u * (sg * (1.0 + g * (1.0 - sg)))).astype(BF16)
        dwd_c = _dot_tn(a, dyb)
        dwg_c = _dot_tn(dgb, h)
        dwu_c = _dot_tn(dub, h)
        dh_c = _dot_nn(dgb, wg_ref[0]) + _dot_nn(dub, wu_ref[0])

        @pl.when(i == 0)
        def _():
            ad_ref[...] = dwd_c
            ag_ref[...] = dwg_c
            au_ref[...] = dwu_c

        @pl.when(i > 0)
        def _():
            ad_ref[...] += dwd_c
            ag_ref[...] += dwg_c
            au_ref[...] += dwu_c

        @pl.when(i == nt - 1)
        def _():
            dwd_ref[0] = ad_ref[...].astype(BF16)
            dwg_ref[0] = ag_ref[...].astype(BF16)
            dwu_ref[0] = au_ref[...].astype(BF16)

        @pl.when(c == 0)
        def _():
            dh_ref[rows, :] = dh_c

        @pl.when(c > 0)
        def _():
            dh_ref[rows, :] += dh_c

        @pl.when(c == nc - 1)
        def _():
            dx, dn = _rmsnorm_bwd(xv, gain, dh_ref[rows, :])
            dx_ref[...] = dx + dy

            @pl.when(i == 0)
            def _():
                dn_ref[...] = dn

            @pl.when(i > 0)
            def _():
                dn_ref[...] += dn

    tile = pl.BlockSpec((tm, d), lambda c, i: (i, 0))
    row = pl.BlockSpec((1, d), lambda c, i: (0, 0))
    wrow = pl.BlockSpec((1, fc, d), lambda c, i: (c, 0, 0), pipeline_mode=pl.Buffered(1))
    hidden = pl.BlockSpec((1, tm, fc), lambda c, i: (c, i, 0))
    last = pl.BlockSpec((tm, d), lambda c, i: (jnp.where(c == nc - 1, i, 0), 0))
    return pl.pallas_call(
        body, name=name, grid=(nc, nt),
        out_shape=(jax.ShapeDtypeStruct((t, d), F32), jax.ShapeDtypeStruct((1, d), F32),
                   jax.ShapeDtypeStruct(wg.shape, BF16), jax.ShapeDtypeStruct(wu.shape, BF16),
                   jax.ShapeDtypeStruct(wd.shape, BF16)),
        in_specs=[tile, row, wrow, wrow, wrow, hidden, hidden, tile, ANY],
        out_specs=(last, row, wrow, wrow, wrow),
        scratch_shapes=[pltpu.VMEM((t, d), F32)] + [pltpu.VMEM((fc, d), F32)] * 3,
        compiler_params=_params("arbitrary", "arbitrary"),
    )(x, norm, wg, wu, wd, gate, up, dy, dep)


def _store_heads(ref, v):
    for h in range(ref.shape[0]):
        ref[h] = v[:, h * HEAD_DIM:(h + 1) * HEAD_DIM]


def _load_heads(ref):
    return jnp.concatenate([ref[h] for h in range(ref.shape[0])], axis=-1)


N_HEAD_GROUPS = 3


def _proj_fwd(x, norm, w, c):
    t, d = x.shape
    nc, _, ncol = w.shape
    nh = c // HEAD_DIM
    tm = TOKEN_TILE
    wide = nc * ncol - N_HEAD_GROUPS * c

    def body(x_ref, n_ref, w_ref, q_ref, k_ref, v_ref, cur_ref):
        xv = x_ref[...]
        h = (xv * _rms(xv) * n_ref[...]).astype(BF16)
        full = jnp.concatenate([jnp.dot(h, w_ref[s], preferred_element_type=F32) for s in range(nc)], axis=1)
        for m, ref in enumerate((q_ref, k_ref, v_ref)):
            _store_heads(ref, full[:, m * c:(m + 1) * c])
        cur_ref[...] = full[:, N_HEAD_GROUPS * c:]

    heads = pl.BlockSpec((nh, tm, HEAD_DIM), lambda i: (0, i, 0))
    hshape = jax.ShapeDtypeStruct((nh, t, HEAD_DIM), F32)
    return pl.pallas_call(
        body, name="proj_fwd", grid=(t // tm,),
        out_shape=(hshape, hshape, hshape, jax.ShapeDtypeStruct((t, wide), F32)),
        in_specs=[pl.BlockSpec((tm, d), lambda i: (i, 0)), pl.BlockSpec((1, d), lambda i: (0, 0)), VMEM_FULL],
        out_specs=(heads, heads, heads, pl.BlockSpec((tm, wide), lambda i: (i, 0))),
        compiler_params=_params("arbitrary"),
    )(x, norm, w)


def _proj_bwd(x, norm, w, dq, dk, dv, dcur, dres):
    t, d = x.shape
    nc, _, ncol = w.shape
    nh = dq.shape[0]
    tm = TOKEN_TILE
    nt = t // tm
    wide = dcur.shape[1]

    def body(x_ref, n_ref, w_ref, dq_ref, dk_ref, dv_ref, dcur_ref, dres_ref, dx_ref, dn_ref, dw_ref, acc_ref):
        i = pl.program_id(0)

        @pl.when(i == 0)
        def _():
            acc_ref[...] = jnp.zeros_like(acc_ref)
            dn_ref[...] = jnp.zeros_like(dn_ref)

        xv = x_ref[...]
        gain = n_ref[...]
        h = (xv * _rms(xv) * gain).astype(BF16)
        dp = jnp.concatenate([_load_heads(dq_ref), _load_heads(dk_ref), _load_heads(dv_ref), dcur_ref[...]], axis=1).astype(BF16)
        dh = jnp.zeros((tm, d), F32)
        for s in range(nc):
            dps = dp[:, s * ncol:(s + 1) * ncol]
            acc_ref[s] += _dot_tn(h, dps)
            dh = dh + _dot_nt(dps, w_ref[s])
        dx, dn = _rmsnorm_bwd(xv, gain, dh)
        dx_ref[...] = dx + dres_ref[...]
        dn_ref[...] += dn

        @pl.when(i == nt - 1)
        def _():
            dw_ref[...] = acc_ref[...].astype(BF16)

    tile = pl.BlockSpec((tm, d), lambda i: (i, 0))
    row = pl.BlockSpec((1, d), lambda i: (0, 0))
    heads = pl.BlockSpec((nh, tm, HEAD_DIM), lambda i: (0, i, 0))
    return pl.pallas_call(
        body, name="proj_bwd", grid=(nt,),
        out_shape=(jax.ShapeDtypeStruct((t, d), F32), jax.ShapeDtypeStruct((1, d), F32),
                   jax.ShapeDtypeStruct(w.shape, BF16)),
        in_specs=[tile, row, VMEM_FULL, heads, heads, heads, pl.BlockSpec((tm, wide), lambda i: (i, 0)), tile],
        out_specs=(tile, row, VMEM_FULL),
        scratch_shapes=[pltpu.VMEM(w.shape, F32)], compiler_params=_params("arbitrary"),
    )(x, norm, w, dq, dk, dv, dcur, dres)


def _mixout_fwd(x, att, opg, gate, w):
    t, d = x.shape
    nh = att.shape[0]
    half = gate.shape[1]
    tm = TOKEN_TILE

    def body(x_ref, att_ref, opg_ref, g_ref, w_ref, o_ref):
        mix = jnp.concatenate([_load_heads(att_ref), _load_heads(opg_ref) * g_ref[...]], axis=-1).astype(BF16)
        o_ref[...] = x_ref[...] + jnp.dot(mix, w_ref[...], preferred_element_type=F32)

    tile = pl.BlockSpec((tm, d), lambda i: (i, 0))
    htile = pl.BlockSpec((tm, half), lambda i: (i, 0))
    heads = pl.BlockSpec((nh, tm, HEAD_DIM), lambda i: (0, i, 0))
    return pl.pallas_call(
        body, name="mixout_fwd", grid=(t // tm,), out_shape=jax.ShapeDtypeStruct((t, d), F32),
        in_specs=[tile, heads, heads, htile, VMEM_FULL], out_specs=tile, compiler_params=_params("arbitrary"),
    )(x, att, opg, gate, w)


def _mixout_bwd(att, opg, gate, w, dy, dep):
    nh, t, _ = att.shape
    half = gate.shape[1]
    d = dy.shape[1]
    tm = TOKEN_TILE

    def body(att_ref, opg_ref, g_ref, w_ref, dy_ref, dep_ref, datt_ref, dopg_ref, dg_ref, dw_ref):
        i = pl.program_id(0)
        opg_v, g_v = _load_heads(opg_ref), g_ref[...]
        mix = jnp.concatenate([_load_heads(att_ref), opg_v * g_v], axis=-1).astype(BF16)
        dyb = dy_ref[...].astype(BF16)
        dmix = _dot_nt(dyb, w_ref[...])
        dw = _dot_tn(mix, dyb)
        _store_heads(datt_ref, dmix[:, :half])
        drw = dmix[:, half:]
        _store_heads(dopg_ref, drw * g_v)
        dg_ref[...] = drw * opg_v

        @pl.when(i == 0)
        def _():
            dw_ref[...] = dw

        @pl.when(i > 0)
        def _():
            dw_ref[...] += dw

    tile = pl.BlockSpec((tm, d), lambda i: (i, 0))
    htile = pl.BlockSpec((tm, half), lambda i: (i, 0))
    heads = pl.BlockSpec((nh, tm, HEAD_DIM), lambda i: (0, i, 0))
    hshape = jax.ShapeDtypeStruct((nh, t, HEAD_DIM), F32)
    return pl.pallas_call(
        body, name="mixout_bwd", grid=(t // tm,),
        out_shape=(hshape, hshape, jax.ShapeDtypeStruct((t, half), F32), jax.ShapeDtypeStruct(w.shape, F32)),
        in_specs=[heads, heads, htile, VMEM_FULL, tile, ANY],
        out_specs=(heads, heads, htile, pl.BlockSpec(w.shape, lambda i: (0, 0))),
        compiler_params=_params("arbitrary"),
    )(att, opg, gate, w, dy, dep)


def _loss_head(y, target):
    t, d = y.shape
    tm = TOKEN_TILE

    def body(y_ref, t_ref, dy_ref, loss_ref):
        i = pl.program_id(0)
        err = y_ref[...] - t_ref[...]
        dy_ref[...] = err * (1.0 / d)
        part = 0.5 * jnp.sum(jnp.mean(err * err, axis=-1, keepdims=True), axis=0, keepdims=True)

        @pl.when(i == 0)
        def _():
            loss_ref[...] = jnp.zeros_like(loss_ref)

        loss_ref[...] += jnp.broadcast_to(part, loss_ref.shape)

    tile = pl.BlockSpec((tm, d), lambda i: (i, 0))
    return pl.pallas_call(
        body, name="loss_head", grid=(t // tm,),
        out_shape=(jax.ShapeDtypeStruct((t, d), F32), jax.ShapeDtypeStruct((1, 128), F32)),
        in_specs=[tile, tile], out_specs=(tile, pl.BlockSpec((1, 128), lambda i: (0, 0))),
        compiler_params=_params("arbitrary"),
    )(y, target)


def _head_norm(x, gain):
    return x * _rms(x) * gain


def _att_pattern(qh, kh, v, nb):
    g, blk, _ = qh.shape
    scale = HEAD_DIM ** -0.5
    qi = lax.broadcasted_iota(jnp.int32, (blk, blk), 0)
    kj = lax.broadcasted_iota(jnp.int32, (blk, blk), 1)
    sc = jnp.where(kj <= qi, _bmm_nt(qh, kh) * scale, NEG_INF)
    top = jnp.max(sc, axis=-1, keepdims=True)
    if nb > 1:
        khp = jnp.concatenate([kh[:1], kh[:-1]], axis=0)
        vp = jnp.concatenate([v[:1], v[:-1]], axis=0)
        has_prev = lax.broadcasted_iota(jnp.int32, (g, 1, 1), 0) % nb != 0
        sp = jnp.where((kj >= qi) & has_prev, _bmm_nt(qh, khp) * scale, NEG_INF)
        top = jnp.maximum(top, jnp.max(sp, axis=-1, keepdims=True))
    m = lax.stop_gradient(top)
    pc = jnp.exp(sc - m)
    den = jnp.sum(pc, axis=-1, keepdims=True)
    acc = _bmm_nn(pc, v)
    if nb > 1:
        pp = jnp.exp(sp - m)
        den = den + jnp.sum(pp, axis=-1, keepdims=True)
        acc = acc + _bmm_nn(pp, vp)
    o = acc / den
    return o, jnp.broadcast_to(m + jnp.log(den), o.shape)


def _pattern_rows(t, dil):
    nb = t // (ATT_BLOCK * dil)
    starts = [n * ATT_BLOCK * dil + r for r in range(dil) for n in range(nb)]
    return [pl.ds(s, ATT_BLOCK, stride=dil) if dil > 1 else pl.ds(s, ATT_BLOCK) for s in starts], nb


def _take(ref, rows):
    return jnp.stack([ref[0, r, :] for r in rows])


def _put(ref, rows, val):
    for g, r in enumerate(rows):
        ref[0, r, :] = val[g]


def _put_add(ref, rows, val):
    for g, r in enumerate(rows):
        ref[0, r, :] += val[g]


def _merge_fn(o1, o2, o3, l1, l2, l3):
    m = lax.stop_gradient(jnp.maximum(jnp.maximum(l1, l2), l3))
    e1, e2, e3 = jnp.exp(l1 - m), jnp.exp(l2 - m), jnp.exp(l3 - m)
    return (e1 * o1 + e2 * o2 + e3 * o3) / (e1 + e2 + e3)


def _token_rows(j):
    return pl.ds(pl.multiple_of(j * ATT_BLOCK, ATT_BLOCK), ATT_BLOCK)


def _norm_rows(t, q_ref, k_ref, gq, gk, qh_ref, kh_ref):
    def step(j, carry):
        rows = _token_rows(j)
        qh_ref[0, rows, :] = _head_norm(q_ref[0, rows, :], gq[0])
        kh_ref[0, rows, :] = _head_norm(k_ref[0, rows, :], gk[0])
        return carry

    lax.fori_loop(0, t // ATT_BLOCK, step, 0)


def _att_head_specs(t):
    head = pl.BlockSpec((1, t, HEAD_DIM), lambda h: (h, 0, 0))
    gain = pl.BlockSpec((1, 1, HEAD_DIM), lambda h: (0, 0, 0))
    return head, gain


def _att_fwd(q, k, v, qn, kn):
    nh, t, dh = q.shape
    head, gain = _att_head_specs(t)

    def body(q_ref, k_ref, v_ref, qn_ref, kn_ref, att_ref, o1, o2, o3, l1, l2, l3, qh_ref, kh_ref):
        saved = (o1, o2, o3, l1, l2, l3)
        _norm_rows(t, q_ref, k_ref, qn_ref[...], kn_ref[...], qh_ref, kh_ref)
        for p, dil in enumerate(DILATIONS):
            rows, nb = _pattern_rows(t, dil)
            o, lse = _att_pattern(_take(qh_ref, rows), _take(kh_ref, rows), _take(v_ref, rows), nb)
            _put(saved[p], rows, o)
            _put(saved[3 + p], rows, lse)

        def merge(j, carry):
            rows = _token_rows(j)
            att_ref[0, rows, :] = _merge_fn(*[r[0, rows, :] for r in saved])
            return carry

        lax.fori_loop(0, t // ATT_BLOCK, merge, 0)

    return pl.pallas_call(
        body, name="att_fwd", grid=(nh,), out_shape=(jax.ShapeDtypeStruct(q.shape, F32),) * 7,
        in_specs=[head, head, head, gain, gain], out_specs=(head,) * 7,
        scratch_shapes=[pltpu.VMEM((1, t, dh), F32)] * 2, compiler_params=_params("arbitrary"),
    )(q, k, v, qn, kn)


def _att_bwd(q, k, v, qn, kn, saved, datt):
    nh, t, dh = q.shape
    head, gain = _att_head_specs(t)

    def body(q_ref, k_ref, v_ref, qn_ref, kn_ref, o1, o2, o3, l1, l2, l3, datt_ref,
             dq_ref, dk_ref, dv_ref, dqn_ref, dkn_ref, qh_ref, kh_ref, dqh_ref, dkh_ref, *ct_refs):
        for ref in (dqh_ref, dkh_ref, dv_ref):
            ref[...] = jnp.zeros_like(ref)

        @pl.when(pl.program_id(0) == 0)
        def _():
            dqn_ref[...] = jnp.zeros_like(dqn_ref)
            dkn_ref[...] = jnp.zeros_like(dkn_ref)

        gq, gk = qn_ref[...], kn_ref[...]
        _norm_rows(t, q_ref, k_ref, gq, gk, qh_ref, kh_ref)

        def merge_cotangents(j, carry):
            rows = _token_rows(j)
            _, merge_vjp = jax.vjp(_merge_fn, *[r[0, rows, :] for r in (o1, o2, o3, l1, l2, l3)])
            for ref, val in zip(ct_refs, merge_vjp(datt_ref[0, rows, :])):
                ref[0, rows, :] = val
            return carry

        lax.fori_loop(0, t // ATT_BLOCK, merge_cotangents, 0)

        for p, dil in enumerate(DILATIONS):
            rows, nb = _pattern_rows(t, dil)
            _, pattern_vjp = jax.vjp(functools.partial(_att_pattern, nb=nb), _take(qh_ref, rows), _take(kh_ref, rows),
                                     _take(v_ref, rows))
            dqh, dkh, dv = pattern_vjp((_take(ct_refs[p], rows), _take(ct_refs[3 + p], rows)))
            _put_add(dqh_ref, rows, dqh)
            _put_add(dkh_ref, rows, dkh)
            _put_add(dv_ref, rows, dv)

        def norm_cotangents(j, carry):
            rows = _token_rows(j)
            out = []
            for x_ref, gain, dh_ref, dx_ref, acc in ((q_ref, gq, dqh_ref, dq_ref, carry[0]), (k_ref, gk, dkh_ref, dk_ref, carry[1])):
                _, norm_vjp = jax.vjp(_head_norm, x_ref[0, rows, :], gain[0])
                dx, dgain = norm_vjp(dh_ref[0, rows, :])
                dx_ref[0, rows, :] = dx
                out.append(acc + dgain)
            return tuple(out)

        zero = jnp.zeros((1, dh), F32)
        dgq, dgk = lax.fori_loop(0, t // ATT_BLOCK, norm_cotangents, (zero, zero))
        dqn_ref[0] += dgq
        dkn_ref[0] += dgk

    hshape = jax.ShapeDtypeStruct(q.shape, F32)
    gshape = jax.ShapeDtypeStruct((1, 1, dh), F32)
    return pl.pallas_call(
        body, name="att_bwd", grid=(nh,), out_shape=(hshape, hshape, hshape, gshape, gshape),
        in_specs=[head, head, head, gain, gain] + [head] * 7, out_specs=(head, head, head, gain, gain),
        scratch_shapes=[pltpu.VMEM((1, t, dh), F32)] * 10, compiler_params=_params("arbitrary"),
    )(q, k, v, qn, kn, *saved, datt)


RWKV_VEC = ("mu_r", "mu_k", "mu_v", "mu_w", "mu_a", "mu_g", "w0", "a0", "k_k", "k_a")
RWKV_MAT = ("w1", "w2", "a1", "a2", "g1", "g2")


def _rwkv_pre_fn(cur, prev, vec, w1, w2, a1, a2, g1, g2):
    c = cur.shape[1] // 4
    mu_r, mu_k, mu_v, mu_w, mu_a, mu_g, w0, a0, k_k, k_a = (vec[j:j + 1] for j in range(10))

    def lerp(j, mu):
        xc, xp = cur[:, j * c:(j + 1) * c], prev[:, j * c:(j + 1) * c]
        return xc + (xp - xc) * mu

    r, k, v = lerp(0, mu_r), lerp(1, mu_k), lerp(2, mu_v)
    cw, ca, cg = lerp(3, mu_w), lerp(3, mu_a), lerp(3, mu_g)
    z = w0 + _mm(jnp.tanh(_mm(cw, w1)), w2)
    w_log = jnp.minimum(z, 0.0) - jnp.log(1.0 + jnp.exp(-jnp.abs(z))) - 0.5
    lw = -jnp.exp(w_log)
    a = _sigmoid(a0 + _mm(_mm(ca, a1), a2))
    gate = _mm(_sigmoid(_mm(cg, g1)), g2)
    kkraw = k * k_k
    kmod = k * (1.0 + (a - 1.0) * k_a)
    return r, lw, kmod, v, kkraw, a, gate


HALO_ROWS = 8


def _rwkv_pre_specs(c, mats, tile_of):
    tm = TOKEN_TILE
    nh = c // HEAD_DIM
    wide = pl.BlockSpec((tm, 4 * c), lambda j: (tile_of(j), 0))
    halo = pl.BlockSpec((HALO_ROWS, 4 * c), lambda j: (jnp.maximum(tile_of(j) * (tm // HALO_ROWS) - 1, 0), 0))
    one = pl.BlockSpec((tm, c), lambda j: (tile_of(j), 0))
    heads = pl.BlockSpec((nh, tm, HEAD_DIM), lambda j: (0, tile_of(j), 0))
    vec = pl.BlockSpec((10, c), lambda j: (0, 0))
    mspecs = [pl.BlockSpec(m.shape, lambda j: (0, 0)) for m in mats]
    return wide, halo, one, heads, vec, mspecs


def _previous_rows(cur, halo, tile):
    first = jnp.where(tile > 0, halo[HALO_ROWS - 1:HALO_ROWS], 0.0)
    rows = lax.broadcasted_iota(jnp.int32, cur.shape, 0)
    return jnp.where(rows == 0, first, pltpu.roll(cur, 1, axis=0))


def _rwkv_pre_fwd(cur, vec, mats):
    t, c4 = cur.shape
    c = c4 // 4
    wide, halo, one, heads, vspec, mspecs = _rwkv_pre_specs(c, mats, lambda j: j)

    def body(cur_ref, halo_ref, vec_ref, *rest):
        mrefs, outs = rest[:6], rest[6:]
        cur_v = cur_ref[...]
        prev = _previous_rows(cur_v, halo_ref[...], pl.program_id(0))
        vals = _rwkv_pre_fn(cur_v, prev, vec_ref[...], *(m[...] for m in mrefs))
        for ref, val in zip(outs[:6], vals[:6]):
            _store_heads(ref, val)
        outs[6][...] = vals[6]

    hshape = jax.ShapeDtypeStruct((c // HEAD_DIM, t, HEAD_DIM), F32)
    return pl.pallas_call(
        body, name="rwkv_pre_fwd", grid=(t // TOKEN_TILE,), out_shape=(hshape,) * 6 + (jax.ShapeDtypeStruct((t, c), F32),),
        in_specs=[wide, halo, vspec] + mspecs, out_specs=(heads,) * 6 + (one,), compiler_params=_params("arbitrary"),
    )(cur, cur, vec, *mats)


def _rwkv_pre_bwd(cur, vec, mats, cts, dgate):
    t, c4 = cur.shape
    c = c4 // 4
    tm = TOKEN_TILE
    nt = t // tm
    wide, halo, one, heads, vspec, mspecs = _rwkv_pre_specs(c, mats, lambda j: nt - 1 - j)

    def body(cur_ref, halo_ref, vec_ref, *rest):
        mrefs, ctrefs, dgate_ref, outs, carry_ref = rest[:6], rest[6:12], rest[12], rest[13:-1], rest[-1]
        j = pl.program_id(0)

        @pl.when(j == 0)
        def _():
            carry_ref[...] = jnp.zeros_like(carry_ref)
            for ref in outs[1:]:
                ref[...] = jnp.zeros_like(ref)

        cur_v = cur_ref[...]
        prev = _previous_rows(cur_v, halo_ref[...], nt - 1 - j)
        _, vjp = jax.vjp(_rwkv_pre_fn, cur_v, prev, vec_ref[...], *(m[...] for m in mrefs))
        grads = vjp(tuple(_load_heads(r) for r in ctrefs) + (dgate_ref[...],))
        dprev = grads[1]
        rows = lax.broadcasted_iota(jnp.int32, dprev.shape, 0)
        outs[0][...] = grads[0] + jnp.where(rows == tm - 1, carry_ref[0:1], pltpu.roll(dprev, tm - 1, axis=0))
        carry_ref[0:1] = dprev[0:1]
        for ref, val in zip(outs[1:], grads[2:]):
            ref[...] += val

    return pl.pallas_call(
        body, name="rwkv_pre_bwd", grid=(nt,),
        out_shape=(jax.ShapeDtypeStruct(cur.shape, F32), jax.ShapeDtypeStruct(vec.shape, F32))
        + tuple(jax.ShapeDtypeStruct(m.shape, F32) for m in mats),
        in_specs=[wide, halo, vspec] + mspecs + [heads] * 6 + [one], out_specs=(wide, vspec) + tuple(mspecs),
        scratch_shapes=[pltpu.VMEM((HALO_ROWS, c4), F32)], compiler_params=_params("arbitrary"),
    )(cur, cur, vec, *mats, *cts, dgate)


def _scan_chunk_fn(h0, r, lw, k, v, kkraw, a, rk, lnw, lnb):
    n = r.shape[1]
    nrm = jnp.sqrt(jnp.sum(kkraw * kkraw, axis=-1, keepdims=True))
    kk = kkraw / jnp.maximum(nrm, 1e-12)
    av, bv = -kk, kk * a
    ti = lax.broadcasted_iota(jnp.int32, (n, n), 0)
    si = lax.broadcasted_iota(jnp.int32, (n, n), 1)
    incl, strict = ti >= si, ti > si
    ones = jnp.broadcast_to(incl.astype(F32)[None], (r.shape[0], n, n))
    cum = _hdot(ones, lw, 2, 1)
    at, rt = av * jnp.exp(cum - lw), r * jnp.exp(cum)
    inv = jnp.exp(-cum)
    bt, kt = bv * inv, k * inv
    gram = _hdot(jnp.concatenate([at, rt], axis=1), jnp.concatenate([bt, kt], axis=1), 2, 2)
    lab = jnp.where(strict, gram[:, :n, :n], 0.0)
    lak = jnp.where(strict, gram[:, :n, n:], 0.0)
    rb = jnp.where(incl, gram[:, n:, :n], 0.0)
    rkm = jnp.where(incl, gram[:, n:, n:], 0.0)
    nv = v.shape[2]
    u = _bmm_nn(jnp.concatenate([at, lak], axis=2), jnp.concatenate([h0, v], axis=1))
    p = lab
    m = 2
    while m < n:
        both = _bmm_nn(p, jnp.concatenate([u, p], axis=2))
        u, p = u + both[:, :, :nv], both[:, :, nv:]
        m *= 2
    u = u + _bmm_nn(p, u)
    y = _bmm_nn(jnp.concatenate([rt, rb, rkm], axis=2), jnp.concatenate([h0, u, v], axis=1))
    last = jnp.exp(jnp.sum(lw, axis=1, keepdims=True))
    h1 = jnp.swapaxes(last, 1, 2) * (h0 + _bmm_tn(jnp.concatenate([bt, kt], axis=1), jnp.concatenate([u, v], axis=1)))
    mean = jnp.mean(y, axis=-1, keepdims=True)
    yc = y - mean
    var = jnp.mean(yc * yc, axis=-1, keepdims=True)
    yn = yc * lax.rsqrt(var + GN_EPS) * lnw + lnb
    bonus = jnp.sum(r * k * rk, axis=-1, keepdims=True) * v
    return yn + bonus, h1


SCAN_GROUP = 2


def _scan_group_fn(h0, r, lw, k, v, kkraw, a, rk, lnw, lnb):
    outs = []
    for j in range(SCAN_GROUP):
        rows = slice(j * SCAN_CHUNK, (j + 1) * SCAN_CHUNK)
        o, h0 = _scan_chunk_fn(h0, r[:, rows], lw[:, rows], k[:, rows], v[:, rows], kkraw[:, rows], a[:, rows], rk, lnw, lnb)
        outs.append(o)
    return jnp.concatenate(outs, axis=1), h0


def _scan_specs(h, t, dh, rev):
    n = SCAN_CHUNK * SCAN_GROUP
    nc = t // n
    pos = (lambda c: (0, nc - 1 - c, 0)) if rev else (lambda c: (0, c, 0))
    st = (lambda c: (nc - 1 - c, 0, 0, 0)) if rev else (lambda c: (c, 0, 0, 0))
    seq = pl.BlockSpec((h, n, dh), pos)
    par = pl.BlockSpec((h, 1, dh), lambda c: (0, 0, 0))
    state = pl.BlockSpec((1, h, dh, dh), st)
    return seq, par, state


def _scan_fwd(seqs, pars):
    h, t, dh = seqs[0].shape
    nc = t // (SCAN_CHUNK * SCAN_GROUP)
    seq, par, state = _scan_specs(h, t, dh, False)

    def body(r, lw, k, v, kkraw, a, rk, lnw, lnb, o_ref, st_ref, h_ref):
        @pl.when(pl.program_id(0) == 0)
        def _():
            h_ref[...] = jnp.zeros_like(h_ref)

        h0 = h_ref[...]
        st_ref[0] = h0
        o, h1 = _scan_group_fn(h0, r[...], lw[...], k[...], v[...], kkraw[...], a[...], rk[...], lnw[...], lnb[...])
        o_ref[...] = o
        h_ref[...] = h1

    return pl.pallas_call(
        body, name="rwkv_scan_fwd", grid=(nc,),
        out_shape=(jax.ShapeDtypeStruct((h, t, dh), F32), jax.ShapeDtypeStruct((nc, h, dh, dh), F32)),
        in_specs=[seq] * 6 + [par] * 3, out_specs=(seq, state),
        scratch_shapes=[pltpu.VMEM((h, dh, dh), F32)], compiler_params=_params("arbitrary"),
    )(*seqs, *pars)


def _scan_bwd(seqs, pars, states, do):
    h, t, dh = seqs[0].shape
    nc = t // (SCAN_CHUNK * SCAN_GROUP)
    seq, par, state = _scan_specs(h, t, dh, True)

    def body(r, lw, k, v, kkraw, a, rk, lnw, lnb, st_ref, do_ref, *rest):
        douts, dpars, dh_ref = rest[:6], rest[6:9], rest[9]
        first = pl.program_id(0) == 0

        @pl.when(first)
        def _():
            dh_ref[...] = jnp.zeros_like(dh_ref)

        _, vjp = jax.vjp(_scan_group_fn, st_ref[0], r[...], lw[...], k[...], v[...], kkraw[...], a[...],
                         rk[...], lnw[...], lnb[...])
        grads = vjp((do_ref[...], dh_ref[...]))
        dh_ref[...] = grads[0]
        for ref, val in zip(douts, grads[1:7]):
            ref[...] = val

        @pl.when(first)
        def _():
            for ref, val in zip(dpars, grads[7:]):
                ref[...] = val

        @pl.when(jnp.logical_not(first))
        def _():
            for ref, val in zip(dpars, grads[7:]):
                ref[...] += val

    sshape = jax.ShapeDtypeStruct((h, t, dh), F32)
    pshape = jax.ShapeDtypeStruct((h, 1, dh), F32)
    return pl.pallas_call(
        body, name="rwkv_scan_bwd", grid=(nc,), out_shape=(sshape,) * 6 + (pshape,) * 3,
        in_specs=[seq] * 6 + [par] * 3 + [state, seq], out_specs=(seq,) * 6 + (par,) * 3,
        scratch_shapes=[pltpu.VMEM((h, dh, dh), F32)], compiler_params=_params("arbitrary"),
    )(*seqs, *pars, states, do)


def _local_step(x, target, w, ex):
    w = dict(w)
    c = w["mu_r"].shape[-1]
    qn, kn = w["q_norm"].reshape(1, 1, HEAD_DIM), w["k_norm"].reshape(1, 1, HEAD_DIM)
    vec = jnp.concatenate([w[n].reshape(1, c) for n in RWKV_VEC], axis=0)
    pars = [w[n].reshape(-1, 1, HEAD_DIM) for n in ("r_k", "ln_x_w", "ln_x_b")]
    no_dep = jnp.zeros(DEP_SHAPE, F32)

    x1, gate1, up1 = _ffn_fwd(x, w["ffn1_norm"], w["ffn1_w_gate"], w["ffn1_w_up"], w["ffn1_w_down"], ex.first_dep, "ffn1_fwd")
    w.update(ex.mix_weights((x1,)))
    mats = [w[n] for n in RWKV_MAT]
    q, k, v, cur = _proj_fwd(x1, w["mix_norm"], w["w_in"], c)
    att, *saved = _att_fwd(q, k, v, qn, kn)
    pre = _rwkv_pre_fwd(cur, vec, mats)
    seqs, gate = pre[:6], pre[6]
    opg, states = _scan_fwd(seqs, pars)
    w.update(ex.out_weights((att, opg)))
    x2 = _mixout_fwd(x1, att, opg, gate, w["w_out"])
    x3, gate2, up2 = _ffn_fwd(x2, w["ffn2_norm"], w["ffn2_w_gate"], w["ffn2_w_up"], w["ffn2_w_down"], no_dep, "ffn2_fwd")
    dy, loss = _loss_head(x3, target)

    g = {}
    dx2, g["ffn2_norm"], g["ffn2_w_gate"], g["ffn2_w_up"], g["ffn2_w_down"] = _ffn_bwd(
        x2, w["ffn2_norm"], w["ffn2_w_gate"], w["ffn2_w_up"], w["ffn2_w_down"], gate2, up2, dy, no_dep, "ffn2_bwd")
    dep = ex.send_ffn2({n: g[n] for n in ("ffn2_w_gate", "ffn2_w_up", "ffn2_w_down")})
    datt, dopg, dgate, g["w_out"] = _mixout_bwd(att, opg, gate, w["w_out"], dx2, dep)
    dscan = _scan_bwd(seqs, pars, states, dopg)
    for n, d in zip(("r_k", "ln_x_w", "ln_x_b"), dscan[6:]):
        g[n] = d
    dcur, dvec, *dmats = _rwkv_pre_bwd(cur, vec, mats, dscan[:6], dgate)
    for n, d in zip(RWKV_MAT, dmats):
        g[n] = d
    for j, n in enumerate(RWKV_VEC):
        g[n] = dvec[j:j + 1]
    dq, dk, dv, g["q_norm"], g["k_norm"] = _att_bwd(q, k, v, qn, kn, saved, datt)
    dx1, g["mix_norm"], g["w_in"] = _proj_bwd(x1, w["mix_norm"], w["w_in"], dq, dk, dv, dcur, dx2)
    dep = ex.send_mix({n: g[n] for n in ("w_in", "w_out") + RWKV_MAT}, (dx1,))
    dx, g["ffn1_norm"], g["ffn1_w_gate"], g["ffn1_w_up"], g["ffn1_w_down"] = _ffn_bwd(
        x, w["ffn1_norm"], w["ffn1_w_gate"], w["ffn1_w_up"], w["ffn1_w_down"], gate1, up1, dx1, dep, "ffn1_bwd")
    return loss, dx, g


N_SHARDS = 4


def _place():
    return lax.axis_index("x"), lax.axis_index("y"), lax.axis_index("c")


def _chip_peers(x, y):
    return [(1 - x, y), (x, 1 - y), (1 - x, 1 - y)]


HBM = pl.BlockSpec(memory_space=pltpu.HBM)
SEM = pl.BlockSpec(memory_space=pltpu.SEMAPHORE)
DEP_SHAPE = (8, 128)


class _Views:
    to_sibling = False


class _GatherViews(_Views):
    @staticmethod
    def send(i, srcs, lands, k, at):
        return srcs[i], lands[i].at[at[3]]

    @staticmethod
    def landing(i, srcs, lands, k, at):
        return srcs[i], lands[i].at[2 * at[4] + at[5]]


class _ScatterViews(_Views):
    @staticmethod
    def send(i, srcs, lands, k, at):
        return srcs[i].at[2 * at[4] + at[5]], lands[i].at[k]

    @staticmethod
    def landing(i, srcs, lands, k, at):
        return srcs[i].at[at[3]], lands[i].at[k]


def _half_rows(ref, slot, half):
    rows = ref.shape[1] // 2
    return ref.at[slot, pl.ds(pl.multiple_of(half * rows, BF16_SUBLANES), rows)]


class _HalfGatherViews(_Views):
    @staticmethod
    def send(i, srcs, lands, k, at):
        rows = srcs[i].shape[0] // 2
        return srcs[i].at[pl.ds(pl.multiple_of(at[2] * rows, BF16_SUBLANES), rows)], _half_rows(lands[i], at[3], at[2])

    @staticmethod
    def landing(i, srcs, lands, k, at):
        rows = srcs[i].shape[0] // 2
        return srcs[i].at[pl.ds(pl.multiple_of(at[2] * rows, BF16_SUBLANES), rows)], _half_rows(lands[i], 2 * at[4] + at[5], at[2])


class _ForwardViews(_Views):
    to_sibling = True

    @staticmethod
    def send(i, srcs, lands, k, at):
        mine = _half_rows(lands[i], 2 * at[4] + at[5], at[2])
        return mine, mine

    @staticmethod
    def landing(i, srcs, lands, k, at):
        theirs = _half_rows(lands[i], 2 * at[4] + at[5], 1 - at[2])
        return theirs, theirs


def _push_start(srcs, lands, views, after, name):
    ns, nl = len(srcs), len(lands)

    def body(*refs):
        src_refs, land_refs = refs[:ns], refs[ns:ns + nl]
        send_sems, recv_sems = refs[ns + nl + 1:ns + nl + 3]
        token = refs[2 * (ns + nl) + 3]
        x, y, c = _place()
        for i in range(nl):
            for k, (px, py) in enumerate(_chip_peers(x, y)):
                src, dst = views.send(i, src_refs, land_refs, k, (x, y, c, 2 * x + y, px, py))
                pltpu.make_async_remote_copy(
                    src_ref=src, dst_ref=dst, send_sem=send_sems.at[3 * i + k], recv_sem=recv_sems.at[3 * i + k],
                    device_id=(x, y, 1 - c) if views.to_sibling else (px, py, c), device_id_type=MESH).start()
        token[...] = jnp.zeros_like(token)

    sems = pltpu.SemaphoreType.DMA((3 * nl,))
    both = [pltpu.with_memory_space_constraint(a, pltpu.HBM) for a in (*srcs, *lands)]
    outs = pl.pallas_call(
        body, name=name,
        out_shape=(sems, sems, *[pltpu.HBM(a.shape, a.dtype) for a in both], jax.ShapeDtypeStruct(DEP_SHAPE, F32)),
        in_specs=[HBM] * (ns + nl) + [ANY], out_specs=(SEM, SEM, *[HBM] * (ns + nl), VMEM_FULL),
        input_output_aliases={i: 2 + i for i in range(ns + nl)},
        compiler_params=pltpu.CompilerParams(has_side_effects=pltpu.SideEffectType.DATAFLOW_SIDE_EFFECTING),
    )(*both, after)
    return outs[0], outs[1], outs[2:2 + ns], outs[2 + ns:2 + ns + nl], outs[2 + ns + nl]


def _push_wait(started, views, after, name, with_sources=False):
    send_sems, recv_sems, srcs, lands, _ = started
    ns, nl = len(srcs), len(lands)

    def body(*refs):
        src_refs, land_refs = refs[:ns], refs[ns:ns + nl]
        send_sems, recv_sems = refs[ns + nl:ns + nl + 2]
        x, y, c = _place()
        for i in range(nl):
            for k, (px, py) in enumerate(_chip_peers(x, y)):
                src, dst = views.landing(i, src_refs, land_refs, k, (x, y, c, 2 * x + y, px, py))
                landing = pltpu.make_async_remote_copy(
                    src_ref=src, dst_ref=dst, send_sem=send_sems.at[3 * i + k], recv_sem=recv_sems.at[3 * i + k],
                    device_id=(x, y, 1 - c) if views.to_sibling else (px, py, c), device_id_type=MESH)
                landing.wait_send()
                landing.wait_recv()

    outs = pl.pallas_call(
        body, name=name,
        out_shape=tuple(pltpu.HBM(a.shape, a.dtype) for a in (*srcs, *lands)),
        in_specs=[HBM] * (ns + nl) + [SEM, SEM] + [ANY] * len(after), out_specs=(HBM,) * (ns + nl),
        input_output_aliases={i: i for i in range(ns + nl)},
        compiler_params=pltpu.CompilerParams(has_side_effects=pltpu.SideEffectType.DATAFLOW_SIDE_EFFECTING),
    )(*srcs, *lands, send_sems, recv_sems, *after)
    return outs if with_sources else outs[ns:]


def _empty_lands(shards, slots, own_slot):
    lands = [lax.empty((slots,) + s.shape, s.dtype) for s in shards]
    if own_slot:
        me = 2 * lax.axis_index("x") + lax.axis_index("y")
        lands = [lax.dynamic_update_index_in_dim(z, s, me, 0) for z, s in zip(lands, shards)]
    return lands


def _sibling_swap(arrays, name, other_half=False):
    n = len(arrays)

    def body(*refs):
        ins, outs = refs[:n], refs[n:2 * n]
        send_sems, recv_sems = refs[2 * n:]
        x, y, c = _place()
        copies = []
        for i in range(n):
            src = ins[i]
            if other_half:
                rows = src.shape[1] // 2
                src = src.at[:, pl.ds(pl.multiple_of((1 - c) * rows, BF16_SUBLANES), rows)]
            cp = pltpu.make_async_remote_copy(
                src_ref=src, dst_ref=outs[i], send_sem=send_sems.at[i], recv_sem=recv_sems.at[i],
                device_id=(x, y, 1 - c), device_id_type=MESH)
            cp.start()
            copies.append(cp)
        for cp in copies:
            cp.wait()

    shapes = [(a.shape[0], a.shape[1] // 2, a.shape[2]) if other_half else a.shape for a in arrays]
    return pl.pallas_call(
        body, name=name,
        out_shape=tuple(jax.ShapeDtypeStruct(s, a.dtype) for s, a in zip(shapes, arrays)),
        in_specs=[ANY] * n, out_specs=(ANY,) * n,
        scratch_shapes=[pltpu.SemaphoreType.DMA((n,)), pltpu.SemaphoreType.DMA((n,))],
    )(*arrays)


FOLD_STEPS = 2


def _fold_add(core, parts, theirs, name):
    n = len(parts)
    s, r, cols = parts[0].shape
    tr = r // 2 // FOLD_STEPS

    def body(core_ref, *refs):
        for p_ref, t_ref, o_ref in zip(refs[:n], refs[n:2 * n], refs[2 * n:]):
            o_ref[...] = (p_ref[...].astype(F32) + t_ref[...].astype(F32)).astype(BF16)

    half = pl.BlockSpec((1, tr, cols), lambda j, i, core_ref: (j, i, 0))
    return pl.pallas_call(
        body, name=name, out_shape=tuple(jax.ShapeDtypeStruct((s, r // 2, cols), BF16) for _ in parts),
        grid_spec=pltpu.PrefetchScalarGridSpec(
            num_scalar_prefetch=1, grid=(s, FOLD_STEPS),
            in_specs=[pl.BlockSpec((1, tr, cols), lambda j, i, core_ref: (j, core_ref[0] * FOLD_STEPS + i, 0))] * n + [half] * n,
            out_specs=(half,) * n),
        compiler_params=_params("arbitrary", "arbitrary"),
    )(core, *parts, *theirs)


N_DEV = 8


def _allreduce_small(pack):
    def body(in_ref, out_ref, buf, send_sems, recv_sems):
        x, y, c = _place()
        me = 4 * x + 2 * y + c
        buf[me] = in_ref[...]

        def copy(j, slot):
            px, py, pc = x ^ (j >> 2), y ^ ((j >> 1) & 1), c ^ (j & 1)
            return pltpu.make_async_remote_copy(
                src_ref=in_ref, dst_ref=buf.at[slot(px, py, pc)], send_sem=send_sems.at[j], recv_sem=recv_sems.at[j],
                device_id=(px, py, pc), device_id_type=MESH)

        for j in range(1, N_DEV):
            copy(j, lambda px, py, pc: me).start()
        for j in range(1, N_DEV):
            landing = copy(j, lambda px, py, pc: 4 * px + 2 * py + pc)
            landing.wait_send()
            landing.wait_recv()
        acc = buf[0]
        for s in range(1, N_DEV):
            acc = acc + buf[s]
        out_ref[...] = acc

    return pl.pallas_call(
        body, name="allreduce_small", out_shape=jax.ShapeDtypeStruct(pack.shape, F32),
        in_specs=[VMEM_FULL], out_specs=VMEM_FULL,
        scratch_shapes=[pltpu.VMEM((N_DEV,) + pack.shape, F32), pltpu.SemaphoreType.DMA((N_DEV,)),
                        pltpu.SemaphoreType.DMA((N_DEV,))],
    )(pack)


BF16_SUBLANES = 16


def _reduce_own(me, parts, recvs, dep, steps, name):
    n = len(parts)

    def body(me_ref, *refs):
        for p_ref, rv_ref, o_ref in zip(refs[:n], refs[n:2 * n], refs[2 * n + 1:]):
            acc = p_ref[0].astype(F32)
            for k in range(3):
                acc = acc + rv_ref[k].astype(F32)
            o_ref[...] = acc

    shapes = [(p.shape[1] // steps, p.shape[2]) for p in parts]
    return pl.pallas_call(
        body, name=name, out_shape=tuple(jax.ShapeDtypeStruct(p.shape[1:], F32) for p in parts),
        grid_spec=pltpu.PrefetchScalarGridSpec(
            num_scalar_prefetch=1, grid=(steps,),
            in_specs=[pl.BlockSpec((1, tr, c), lambda i, me_ref: (me_ref[0], i, 0)) for tr, c in shapes]
            + [pl.BlockSpec((3, tr, c), lambda i, me_ref: (0, i, 0)) for tr, c in shapes] + [ANY],
            out_specs=tuple(pl.BlockSpec((tr, c), lambda i, me_ref: (i, 0)) for tr, c in shapes)),
        compiler_params=_params("arbitrary"),
    )(me, *parts, *recvs, dep)


def _adamw(ws, gas, gbs, ms, vs, steps, name):
    n = len(ws)
    c1 = 1.0 - ADAM_B1 ** ADAM_STEP
    c2 = 1.0 - ADAM_B2 ** ADAM_STEP
    operands = [ws, gas, ms, vs] if gbs is None else [ws, gas, gbs, ms, vs]
    k = len(operands)

    def body(*refs):
        ins, outs = refs[:k * n], refs[k * n:]
        for j in range(n):
            w_ref, ga_ref, *gb_ref, m_ref, v_ref = ins[j::n]
            g_out, d_out, m_out, v_out = outs[j::n]
            g = ga_ref[...] + gb_ref[0][...] if gb_ref else ga_ref[...]
            mn = ADAM_B1 * m_ref[...] + (1.0 - ADAM_B1) * g
            vn = ADAM_B2 * v_ref[...] + (1.0 - ADAM_B2) * (g * g)
            g_out[...] = g
            m_out[...] = mn
            v_out[...] = vn
            d_out[...] = -ADAM_LR * ((mn / c1) / (jnp.sqrt(vn / c2) + ADAM_EPS) + ADAM_WD * w_ref[...])

    tiles = [pl.BlockSpec((w.shape[0] // steps, w.shape[1]), lambda i: (i, 0)) for w in ws]
    shapes = [jax.ShapeDtypeStruct(w.shape, F32) for w in ws]
    outs = pl.pallas_call(
        body, name=name, grid=(steps,), out_shape=tuple(shapes * 4), in_specs=tiles * k, out_specs=tuple(tiles * 4),
        compiler_params=_params("arbitrary"),
    )(*[a for group in operands for a in group])
    return [outs[j::n] for j in range(n)]


PACK_COLS = 512


def _to_rows(a):
    flat = a.reshape(-1)
    pad = (-flat.shape[0]) % PACK_COLS
    return jnp.pad(flat, (0, pad)).reshape(-1, PACK_COLS)


def _pack(arrays, extra_rows=0):
    rows = [_to_rows(a) for a in arrays]
    n = sum(r.shape[0] for r in rows) + extra_rows
    pad = (-n) % 8
    return jnp.concatenate(rows + [jnp.zeros((extra_rows + pad, PACK_COLS), F32)], axis=0)


def _unpack(pack, like):
    out, at = [], 0
    for a in like:
        n = -(-a.size // PACK_COLS)
        out.append(pack[at:at + n].reshape(-1)[:a.size].reshape(a.shape))
        at += n
    return out


COL_SHARDED = ("ffn1_w_gate", "ffn1_w_up", "w_in", "ffn2_w_gate", "ffn2_w_up", "w2", "a2", "g2")
ROW_SHARDED = ("ffn1_w_down", "ffn2_w_down", "w_out", "w1", "a1", "g1")
CHUNKED = ("ffn1_w_gate", "ffn1_w_up", "ffn1_w_down", "w_in", "ffn2_w_gate", "ffn2_w_up", "ffn2_w_down")
WEIGHTS = ("ffn1_norm", "ffn1_w_gate", "ffn1_w_up", "ffn1_w_down", "mix_norm", "w_in", "q_norm", "k_norm",
           "mu_r", "mu_k", "mu_v", "mu_w", "mu_a", "mu_g", "w0", "w1", "w2", "a0", "a1", "a2", "g1", "g2",
           "k_k", "k_a", "r_k", "ln_x_w", "ln_x_b", "w_out", "ffn2_norm", "ffn2_w_gate", "ffn2_w_up", "ffn2_w_down")


TRANSPOSED = ("ffn1_w_gate", "ffn1_w_up", "ffn2_w_gate", "ffn2_w_up")


def _shard_2d(name, a):
    return a[0].T if name in TRANSPOSED else a[0]


def _full_from_blocks(name, blocks):
    if name in CHUNKED:
        return blocks
    if name in ROW_SHARDED:
        return blocks.reshape(-1, blocks.shape[-1])
    return blocks.transpose(1, 0, 2).reshape(blocks.shape[1], -1)


def _blocks_from_full(name, full):
    if name in CHUNKED:
        return full
    if name in ROW_SHARDED:
        return full.reshape(N_SHARDS, -1, full.shape[-1])
    return full.reshape(full.shape[0], N_SHARDS, -1).transpose(1, 0, 2)


FFN1_GROUP = ("ffn1_w_gate", "ffn1_w_up", "ffn1_w_down")
MIX_GROUP = ("w_in",) + RWKV_MAT
OUT_GROUP = ("w_out", "ffn2_w_gate", "ffn2_w_up", "ffn2_w_down")
FFN2_GROUP = OUT_GROUP[1:]
LATE_GROUP = ("w_in", "w_out") + RWKV_MAT


class _Exchange:
    def __init__(self, given):
        self.given = given
        first = self._gather_start(FFN1_GROUP, _HalfGatherViews, jnp.zeros(DEP_SHAPE, F32), "gather_ffn1_start")
        self.mix = self._gather_start(MIX_GROUP, _GatherViews, first[4], "gather_mix_start")
        self.out = self._gather_start(OUT_GROUP, _GatherViews, self.mix[4], "gather_out_start")
        self.first_dep = self.out[4]
        halves = _push_wait(first, _HalfGatherViews, (self.first_dep,), "gather_ffn1_wait")
        passed = _push_start([], halves, _ForwardViews, halves[0], "gather_ffn1_pass_start")
        self.first_weights = self._full(FFN1_GROUP, _push_wait(passed, _ForwardViews, (passed[4],), "gather_ffn1_pass_wait"))
        self.parts, self.recv = {}, {}

    def _shards(self, names):
        return [_shard_2d(n, self.given[n]).astype(BF16) for n in names]

    @staticmethod
    def _full(names, blocks):
        out = {}
        for n, b in zip(names, blocks):
            full = _full_from_blocks(n, b)
            out[n] = full.astype(F32) if n in RWKV_MAT else full
        return out

    def _gather_start(self, names, views, after, name):
        shards = self._shards(names)
        return _push_start(shards, _empty_lands(shards, N_SHARDS, True), views, after, name)

    def mix_weights(self, after):
        return self._full(MIX_GROUP, _push_wait(self.mix, _GatherViews, after, "gather_mix_wait"))

    def out_weights(self, after):
        return self._full(OUT_GROUP, _push_wait(self.out, _GatherViews, after, "gather_out_wait"))

    def _scatter_start(self, grads, name):
        names = tuple(grads)
        parts = [_blocks_from_full(n, grads[n]) for n in names]
        self.parts.update(zip(names, parts))
        lands = [lax.empty((3,) + p.shape[1:], BF16) for p in parts]
        return _push_start([p.astype(BF16) for p in parts], lands, _ScatterViews, jnp.zeros(DEP_SHAPE, F32), name)

    def _scatter_done(self, started, names, after, name):
        outs = _push_wait(started, _ScatterViews, after, name, with_sources=True)
        for n, sent, got in zip(names, outs[:len(names)], outs[len(names):]):
            self.recv[n] = got
            if self.parts[n].dtype == BF16:
                self.parts[n] = sent

    def send_ffn2(self, grads):
        self.ffn2 = self._scatter_start(grads, "scatter_ffn2_start")
        return self.ffn2[4]

    def send_mix(self, grads, after):
        self._scatter_done(self.ffn2, FFN2_GROUP, after, "scatter_ffn2_wait")
        self.late = self._scatter_start(grads, "scatter_late_start")
        return self.late[4]

    def send_ffn1(self, grads):
        self.ffn1 = self._scatter_start(grads, "scatter_ffn1_start")
        return self.ffn1[4]

    def late_received(self, after):
        self._scatter_done(self.late, LATE_GROUP, after, "scatter_late_wait")

    def ffn1_received(self, after):
        self._scatter_done(self.ffn1, FFN1_GROUP, after, "scatter_ffn1_wait")


def kernel(
        x, ffn1_norm, ffn1_w_gate, ffn1_w_up, ffn1_w_down, mix_norm, w_in, q_norm, k_norm, mu_r, mu_k, mu_v, mu_w,
        mu_a, mu_g, w0, w1, w2, a0, a1, a2, g1, g2, k_k, k_a, r_k, ln_x_w, ln_x_b, w_out, ffn2_norm, ffn2_w_gate,
        ffn2_w_up, ffn2_w_down, loss_target, m_ffn1_norm, m_ffn1_w_gate, m_ffn1_w_up, m_ffn1_w_down, m_mix_norm,
        m_w_in, m_q_norm, m_k_norm, m_mu_r, m_mu_k, m_mu_v, m_mu_w, m_mu_a, m_mu_g, m_w0, m_w1, m_w2, m_a0, m_a1,
        m_a2, m_g1, m_g2, m_k_k, m_k_a, m_r_k, m_ln_x_w, m_ln_x_b, m_w_out, m_ffn2_norm, m_ffn2_w_gate, m_ffn2_w_up,
        m_ffn2_w_down, v_ffn1_norm, v_ffn1_w_gate, v_ffn1_w_up, v_ffn1_w_down, v_mix_norm, v_w_in, v_q_norm, v_k_norm,
        v_mu_r, v_mu_k, v_mu_v, v_mu_w, v_mu_a, v_mu_g, v_w0, v_w1, v_w2, v_a0, v_a1, v_a2, v_g1, v_g2, v_k_k, v_k_a,
        v_r_k, v_ln_x_w, v_ln_x_b, v_w_out, v_ffn2_norm, v_ffn2_w_gate, v_ffn2_w_up, v_ffn2_w_down):
    given = dict(locals())
    sharded = COL_SHARDED + ROW_SHARDED
    sharded = tuple(n for n in WEIGHTS if n in sharded)
    small = tuple(n for n in WEIGHTS if n not in sharded)

    ex = _Exchange(given)
    w = {n: given[n] for n in small}
    w.update(ex.first_weights)
    loss, dx, g = _local_step(x[0], loss_target[0], w, ex)

    core = lax.axis_index("c").astype(jnp.int32).reshape(1)
    late = [g[n] for n in FFN1_GROUP]
    folded = _fold_add(core, late, _sibling_swap(late, "fold_swap_ffn1", other_half=True), "fold_add_ffn1")
    dep = ex.send_ffn1(dict(zip(FFN1_GROUP, folded)))

    me = (2 * lax.axis_index("x") + lax.axis_index("y")).astype(jnp.int32).reshape(1)
    out = {}

    def settle(names, dep, tag):
        done = []
        for kind, sub, r_steps, a_steps in (("large", tuple(n for n in names if n not in RWKV_MAT), 4, 8),
                                            ("small", tuple(n for n in names if n in RWKV_MAT), 1, 1)):
            if not sub:
                continue
            parts = [ex.parts[n].reshape(N_SHARDS, -1, ex.parts[n].shape[-1]) for n in sub]
            recvs = [ex.recv[n].reshape(3, -1, ex.recv[n].shape[-1]) for n in sub]
            mine = _reduce_own(me, parts, recvs, dep, r_steps, f"reduce_{tag}_{kind}")
            theirs = _sibling_swap(mine, f"sibling_swap_{tag}_{kind}")
            res = _adamw([_shard_2d(n, given[n]) for n in sub], mine, theirs, [_shard_2d(n, given["m_" + n]) for n in sub],
                         [_shard_2d(n, given["v_" + n]) for n in sub], a_steps, f"adamw_{tag}_{kind}")
            for n, rs in zip(sub, res):
                out[n] = [(r.T if n in TRANSPOSED else r).reshape(given[n].shape) for r in rs]
                done.append(out[n][1])
        return tuple(done)

    ex.late_received((dep,))
    last = settle(tuple(n for n in sharded if n not in FFN1_GROUP), dep, "rest")

    gpack = _pack([g[n] for n in small], extra_rows=1)
    n_rows = sum(-(-given[n].size // PACK_COLS) for n in small)
    gpack = gpack.at[n_rows, :loss.shape[1]].set(loss[0])
    gsum = _allreduce_small(gpack)
    res = _adamw([_pack([given[n] for n in small], 1)], [gsum], None, [_pack([given["m_" + n] for n in small], 1)],
                 [_pack([given["v_" + n] for n in small], 1)], 1, "adamw_replicated")[0]
    like = [given[n] for n in small]
    for j, r in enumerate(res):
        for n, a in zip(small, _unpack(r, like)):
            out.setdefault(n, [None] * 4)[j] = a
    total_loss = gsum[n_rows, 0]

    ex.ffn1_received((*last, res[1]))
    halves = _reduce_own(me, [ex.parts[n] for n in FFN1_GROUP], [ex.recv[n] for n in FFN1_GROUP],
                         jnp.zeros(DEP_SHAPE, F32), FOLD_STEPS, "reduce_ffn1")
    others = _sibling_swap(halves, "sibling_swap_ffn1")
    first = lax.axis_index("c") == 0
    grads = [jnp.concatenate([jnp.where(first, a, b), jnp.where(first, b, a)], axis=0) for a, b in zip(halves, others)]
    res = _adamw([_shard_2d(n, given[n]) for n in FFN1_GROUP], grads, None, [_shard_2d(n, given["m_" + n]) for n in FFN1_GROUP],
                 [_shard_2d(n, given["v_" + n]) for n in FFN1_GROUP], 8, "adamw_ffn1")
    for n, rs in zip(FFN1_GROUP, res):
        out[n] = [(r.T if n in TRANSPOSED else r).reshape(given[n].shape) for r in rs]
    return (total_loss, dx[None], *[out[n][0] for n in WEIGHTS], *[out[n][1] for n in WEIGHTS],
            *[out[n][2] for n in WEIGHTS], *[out[n][3] for n in WEIGHTS])
```

```python
import functools

import jax
import jax.numpy as jnp
from jax import lax
from jax.experimental import pallas as pl
from jax.experimental.pallas import tpu as pltpu

F32 = jnp.float32
BF16 = jnp.bfloat16
MESH = pl.DeviceIdType.MESH

RMS_EPS = 1e-6
GN_EPS = 64e-5
NEG_INF = -1e30
FFN_RESIDUAL = 0.5
HEAD_DIM = 64
ATT_BLOCK = 128
DILATIONS = (1, 4, 16)
SCAN_CHUNK = 64
TOKEN_TILE = 256
FFN_BWD_TILE = 512

ADAM_LR = 0.001
ADAM_B1 = 0.9
ADAM_B2 = 0.999
ADAM_EPS = 1e-08
ADAM_WD = 0.01
ADAM_STEP = 10

VMEM_FULL = pl.BlockSpec(memory_space=pltpu.VMEM)
ANY = pl.BlockSpec(memory_space=pl.ANY)


VMEM_LIMIT = 56 * 1024 * 1024


def _params(*sem):
    return pltpu.CompilerParams(dimension_semantics=sem, vmem_limit_bytes=VMEM_LIMIT)


def _dot(a, b, dims):
    return lax.dot_general(a.astype(BF16), b.astype(BF16), (dims, ((), ())), preferred_element_type=F32)


def _dot_nn(a, b):
    return _dot(a, b, ((1,), (0,)))


def _dot_nt(a, b):
    return _dot(a, b, ((1,), (1,)))


def _dot_tn(a, b):
    return _dot(a, b, ((0,), (0,)))


@jax.custom_vjp
def _mm(a, b):
    return _dot_nn(a, b)


def _mm_fwd(a, b):
    return _dot_nn(a, b), (a, b)


def _mm_bwd(res, g):
    a, b = res
    return _dot_nt(g, b).astype(a.dtype), _dot_tn(a, g).astype(b.dtype)


_mm.defvjp(_mm_fwd, _mm_bwd)


def _bdot(a, b, ca, cb):
    return lax.dot_general(a.astype(BF16), b.astype(BF16), (((ca,), (cb,)), ((0,), (0,))), preferred_element_type=F32)


@jax.custom_vjp
def _bmm_nt(a, b):
    return _bdot(a, b, 2, 2)


def _bmm_nt_fwd(a, b):
    return _bdot(a, b, 2, 2), (a, b)


def _bmm_nt_bwd(res, g):
    a, b = res
    return _bdot(g, b, 2, 1), _bdot(g, a, 1, 1)


_bmm_nt.defvjp(_bmm_nt_fwd, _bmm_nt_bwd)


@jax.custom_vjp
def _bmm_nn(a, b):
    return _bdot(a, b, 2, 1)


def _bmm_nn_fwd(a, b):
    return _bdot(a, b, 2, 1), (a, b)


def _bmm_nn_bwd(res, g):
    a, b = res
    return _bdot(g, b, 2, 2), _bdot(a, g, 1, 1)


_bmm_nn.defvjp(_bmm_nn_fwd, _bmm_nn_bwd)


@jax.custom_vjp
def _bmm_tn(a, b):
    return _bdot(a, b, 1, 1)


def _bmm_tn_fwd(a, b):
    return _bdot(a, b, 1, 1), (a, b)


def _bmm_tn_bwd(res, g):
    a, b = res
    return _bdot(b, g, 2, 2), _bdot(a, g, 2, 1)


_bmm_tn.defvjp(_bmm_tn_fwd, _bmm_tn_bwd)


def _hdot(a, b, ca, cb):
    return lax.dot_general(a, b, (((ca,), (cb,)), ((0,), (0,))), precision=lax.Precision.HIGH, preferred_element_type=F32)


def _sigmoid(x):
    return 1.0 / (1.0 + jnp.exp(-x))


def _rms(x):
    return lax.rsqrt(jnp.mean(x * x, axis=-1, keepdims=True) + RMS_EPS)


def _ffn_fwd(x, norm, wg, wu, wd, dep, name, target=None):
    t, d = x.shape
    nc, fc, _ = wg.shape
    tm = TOKEN_TILE

    def body(x_ref, n_ref, wg_ref, wu_ref, wd_ref, dep_ref, *rest):
        o_ref, g_ref, u_ref = rest[-3:] if target is None else rest[1:4]
        xv = x_ref[...]
        h = (xv * _rms(xv) * n_ref[...]).astype(BF16)
        acc = jnp.zeros((tm, d), F32)
        for c in range(nc):
            g = _dot_nt(h, wg_ref[c])
            u = _dot_nt(h, wu_ref[c])
            g_ref[c] = g.astype(BF16)
            u_ref[c] = u.astype(BF16)
            a = (g * _sigmoid(g) * u).astype(BF16)
            acc = acc + jnp.dot(a, wd_ref[c], preferred_element_type=F32)
        y = xv + FFN_RESIDUAL * acc
        if target is None:
            o_ref[...] = y
        else:
            t_ref, loss_ref = rest[0], rest[4]
            err = y - t_ref[...]
            o_ref[...] = err * (1.0 / d)
            part = 0.5 * jnp.sum(jnp.mean(err * err, axis=-1, keepdims=True), axis=0, keepdims=True)

            @pl.when(pl.program_id(0) == 0)
            def _():
                loss_ref[...] = jnp.zeros_like(loss_ref)

            loss_ref[...] += jnp.broadcast_to(part, loss_ref.shape)

    tile = pl.BlockSpec((tm, d), lambda i: (i, 0))
    hidden = pl.BlockSpec((nc, tm, fc), lambda i: (0, i, 0))
    hshape = jax.ShapeDtypeStruct((nc, t, fc), BF16)
    with_loss = target is not None
    return pl.pallas_call(
        body, name=name, grid=(t // tm,),
        out_shape=(jax.ShapeDtypeStruct((t, d), F32), hshape, hshape) + ((jax.ShapeDtypeStruct((1, 128), F32),) if with_loss else ()),
        in_specs=[tile, pl.BlockSpec((1, d), lambda i: (0, 0)), VMEM_FULL, VMEM_FULL, VMEM_FULL, ANY] + ([tile] if with_loss else []),
        out_specs=(tile, hidden, hidden) + ((pl.BlockSpec((1, 128), lambda i: (0, 0)),) if with_loss else ()),
        compiler_params=_params("arbitrary"),
    )(x, norm, wg, wu, wd, dep, *((target,) if with_loss else ()))


def _rmsnorm_bwd(xv, gain, dh):
    rs = _rms(xv)
    xn = xv * rs
    dxn = dh * gain
    dx = rs * (dxn - xn * jnp.mean(dxn * xn, axis=-1, keepdims=True))
    return dx, jnp.sum(dh * xn, axis=0, keepdims=True)


def _ffn_bwd(x, norm, wg, wu, wd, gate, up, dy, dep, name):
    t, d = x.shape
    nc, fc, _ = wg.shape
    tm = FFN_BWD_TILE
    nt = t // tm

    def body(x_ref, n_ref, wg_ref, wu_ref, wd_ref, g_ref, u_ref, dy_ref, dep_ref, dx_ref, dn_ref, dwg_ref, dwu_ref,
             dwd_ref, dh_ref, ag_ref, au_ref, ad_ref):
        c, i = pl.program_id(0), pl.program_id(1)
        rows = pl.ds(pl.multiple_of(i * tm, tm), tm)
        xv = x_ref[...]
        gain = n_ref[...]
        h = (xv * _rms(xv) * gain).astype(BF16)
        dy = dy_ref[...]
        dyb = (FFN_RESIDUAL * dy).astype(BF16)
        g = g_ref[0].astype(F32)
        u = u_ref[0].astype(F32)
        sg = _sigmoid(g)
        s = g * sg
        a = (s * u).astype(BF16)
        da = _dot_nt(dyb, wd_ref[0])
        dub = (da * s).astype(BF16)
        dgb = (da * u * (sg * (1.0 + g * (1.0 - sg)))).astype(BF16)
        dwd_c = _dot_tn(a, dyb)
        dwg_c = _dot_tn(dgb, h)
        dwu_c = _dot_tn(dub, h)
        dh_c = _dot_nn(dgb, wg_ref[0]) + _dot_nn(dub, wu_ref[0])

        @pl.when(i == 0)
        def _():
            ad_ref[...] = dwd_c
            ag_ref[...] = dwg_c
            au_ref[...] = dwu_c

        @pl.when(i > 0)
        def _():
            ad_ref[...] += dwd_c
            ag_ref[...] += dwg_c
            au_ref[...] += dwu_c

        @pl.when(i == nt - 1)
        def _():
            dwd_ref[0] = ad_ref[...].astype(BF16)
            dwg_ref[0] = ag_ref[...].astype(BF16)
            dwu_ref[0] = au_ref[...].astype(BF16)

        @pl.when(c == 0)
        def _():
            dh_ref[rows, :] = dh_c

        @pl.when(c > 0)
        def _():
            dh_ref[rows, :] += dh_c

        @pl.when(c == nc - 1)
        def _():
            dx, dn = _rmsnorm_bwd(xv, gain, dh_ref[rows, :])
            dx_ref[...] = dx + dy

            @pl.when(i == 0)
            def _():
                dn_ref[...] = dn

            @pl.when(i > 0)
            def _():
                dn_ref[...] += dn

    tile = pl.BlockSpec((tm, d), lambda c, i: (i, 0))
    row = pl.BlockSpec((1, d), lambda c, i: (0, 0))
    wrow = pl.BlockSpec((1, fc, d), lambda c, i: (c, 0, 0), pipeline_mode=pl.Buffered(1))
    hidden = pl.BlockSpec((1, tm, fc), lambda c, i: (c, i, 0))
    last = pl.BlockSpec((tm, d), lambda c, i: (jnp.where(c == nc - 1, i, 0), 0))
    return pl.pallas_call(
        body, name=name, grid=(nc, nt),
        out_shape=(jax.ShapeDtypeStruct((t, d), F32), jax.ShapeDtypeStruct((1, d), F32),
                   jax.ShapeDtypeStruct(wg.shape, BF16), jax.ShapeDtypeStruct(wu.shape, BF16),
                   jax.ShapeDtypeStruct(wd.shape, BF16)),
        in_specs=[tile, row, wrow, wrow, wrow, hidden, hidden, tile, ANY],
        out_specs=(last, row, wrow, wrow, wrow),
        scratch_shapes=[pltpu.VMEM((t, d), F32)] + [pltpu.VMEM((fc, d), F32)] * 3,
        compiler_params=_params("arbitrary", "arbitrary"),
    )(x, norm, wg, wu, wd, gate, up, dy, dep)


def _store_heads(ref, v):
    for h in range(ref.shape[0]):
        ref[h] = v[:, h * HEAD_DIM:(h + 1) * HEAD_DIM]


def _load_heads(ref):
    return jnp.concatenate([ref[h] for h in range(ref.shape[0])], axis=-1)


N_HEAD_GROUPS = 3


def _proj_fwd(x, norm, w, c):
    t, d = x.shape
    nc, _, ncol = w.shape
    nh = c // HEAD_DIM
    tm = TOKEN_TILE
    wide = nc * ncol - N_HEAD_GROUPS * c

    def body(x_ref, n_ref, w_ref, q_ref, k_ref, v_ref, cur_ref):
        xv = x_ref[...]
        h = (xv * _rms(xv) * n_ref[...]).astype(BF16)
        full = jnp.concatenate([jnp.dot(h, w_ref[s], preferred_element_type=F32) for s in range(nc)], axis=1)
        for m, ref in enumerate((q_ref, k_ref, v_ref)):
            _store_heads(ref, full[:, m * c:(m + 1) * c])
        cur_ref[...] = full[:, N_HEAD_GROUPS * c:]

    heads = pl.BlockSpec((nh, tm, HEAD_DIM), lambda i: (0, i, 0))
    hshape = jax.ShapeDtypeStruct((nh, t, HEAD_DIM), F32)
    return pl.pallas_call(
        body, name="proj_fwd", grid=(t // tm,),
        out_shape=(hshape, hshape, hshape, jax.ShapeDtypeStruct((t, wide), F32)),
        in_specs=[pl.BlockSpec((tm, d), lambda i: (i, 0)), pl.BlockSpec((1, d), lambda i: (0, 0)), VMEM_FULL],
        out_specs=(heads, heads, heads, pl.BlockSpec((tm, wide), lambda i: (i, 0))),
        compiler_params=_params("arbitrary"),
    )(x, norm, w)


def _proj_bwd(x, norm, w, dq, dk, dv, dcur, dres):
    t, d = x.shape
    nc, _, ncol = w.shape
    nh = dq.shape[0]
    tm = TOKEN_TILE
    nt = t // tm
    wide = dcur.shape[1]

    def body(x_ref, n_ref, w_ref, dq_ref, dk_ref, dv_ref, dcur_ref, dres_ref, dx_ref, dn_ref, dw_ref, acc_ref):
        i = pl.program_id(0)

        @pl.when(i == 0)
        def _():
            acc_ref[...] = jnp.zeros_like(acc_ref)
            dn_ref[...] = jnp.zeros_like(dn_ref)

        xv = x_ref[...]
        gain = n_ref[...]
        h = (xv * _rms(xv) * gain).astype(BF16)
        dp = jnp.concatenate([_load_heads(dq_ref), _load_heads(dk_ref), _load_heads(dv_ref), dcur_ref[...]], axis=1).astype(BF16)
        dh = jnp.zeros((tm, d), F32)
        for s in range(nc):
            dps = dp[:, s * ncol:(s + 1) * ncol]
            acc_ref[s] += _dot_tn(h, dps)
            dh = dh + _dot_nt(dps, w_ref[s])
        dx, dn = _rmsnorm_bwd(xv, gain, dh)
        dx_ref[...] = dx + dres_ref[...]
        dn_ref[...] += dn

        @pl.when(i == nt - 1)
        def _():
            dw_ref[...] = acc_ref[...].astype(BF16)

    tile = pl.BlockSpec((tm, d), lambda i: (i, 0))
    row = pl.BlockSpec((1, d), lambda i: (0, 0))
    heads = pl.BlockSpec((nh, tm, HEAD_DIM), lambda i: (0, i, 0))
    return pl.pallas_call(
        body, name="proj_bwd", grid=(nt,),
        out_shape=(jax.ShapeDtypeStruct((t, d), F32), jax.ShapeDtypeStruct((1, d), F32),
                   jax.ShapeDtypeStruct(w.shape, BF16)),
        in_specs=[tile, row, VMEM_FULL, heads, heads, heads, pl.BlockSpec((tm, wide), lambda i: (i, 0)), tile],
        out_specs=(tile, row, VMEM_FULL),
        scratch_shapes=[pltpu.VMEM(w.shape, F32)], compiler_params=_params("arbitrary"),
    )(x, norm, w, dq, dk, dv, dcur, dres)


def _mixout_fwd(x, att, opg, gate, w):
    t, d = x.shape
    nh = att.shape[0]
    half = gate.shape[1]
    tm = TOKEN_TILE

    def body(x_ref, att_ref, opg_ref, g_ref, w_ref, o_ref):
        mix = jnp.concatenate([_load_heads(att_ref), _load_heads(opg_ref) * g_ref[...]], axis=-1).astype(BF16)
        o_ref[...] = x_ref[...] + jnp.dot(mix, w_ref[...], preferred_element_type=F32)

    tile = pl.BlockSpec((tm, d), lambda i: (i, 0))
    htile = pl.BlockSpec((tm, half), lambda i: (i, 0))
    heads = pl.BlockSpec((nh, tm, HEAD_DIM), lambda i: (0, i, 0))
    return pl.pallas_call(
        body, name="mixout_fwd", grid=(t // tm,), out_shape=jax.ShapeDtypeStruct((t, d), F32),
        in_specs=[tile, heads, heads, htile, VMEM_FULL], out_specs=tile, compiler_params=_params("arbitrary"),
    )(x, att, opg, gate, w)


def _mixout_bwd(att, opg, gate, w, dy, dep):
    nh, t, _ = att.shape
    half = gate.shape[1]
    d = dy.shape[1]
    tm = TOKEN_TILE

    def body(att_ref, opg_ref, g_ref, w_ref, dy_ref, dep_ref, datt_ref, dopg_ref, dg_ref, dw_ref):
        i = pl.program_id(0)
        opg_v, g_v = _load_heads(opg_ref), g_ref[...]
        mix = jnp.concatenate([_load_heads(att_ref), opg_v * g_v], axis=-1).astype(BF16)
        dyb = dy_ref[...].astype(BF16)
        dmix = _dot_nt(dyb, w_ref[...])
        dw = _dot_tn(mix, dyb)
        _store_heads(datt_ref, dmix[:, :half])
        drw = dmix[:, half:]
        _store_heads(dopg_ref, drw * g_v)
        dg_ref[...] = drw * opg_v

        @pl.when(i == 0)
        def _():
            dw_ref[...] = dw

        @pl.when(i > 0)
        def _():
            dw_ref[...] += dw

    tile = pl.BlockSpec((tm, d), lambda i: (i, 0))
    htile = pl.BlockSpec((tm, half), lambda i: (i, 0))
    heads = pl.BlockSpec((nh, tm, HEAD_DIM), lambda i: (0, i, 0))
    hshape = jax.ShapeDtypeStruct((nh, t, HEAD_DIM), F32)
    return pl.pallas_call(
        body, name="mixout_bwd", grid=(t // tm,),
        out_shape=(hshape, hshape, jax.ShapeDtypeStruct((t, half), F32), jax.ShapeDtypeStruct(w.shape, F32)),
        in_specs=[heads, heads, htile, VMEM_FULL, tile, ANY],
        out_specs=(heads, heads, htile, pl.BlockSpec(w.shape, lambda i: (0, 0))),
        compiler_params=_params("arbitrary"),
    )(att, opg, gate, w, dy, dep)


def _head_norm(x, gain):
    return x * _rms(x) * gain


def _att_pattern(qh, kh, v, nb):
    g, blk, _ = qh.shape
    scale = HEAD_DIM ** -0.5
    qi = lax.broadcasted_iota(jnp.int32, (blk, blk), 0)
    kj = lax.broadcasted_iota(jnp.int32, (blk, blk), 1)
    sc = jnp.where(kj <= qi, _bmm_nt(qh, kh) * scale, NEG_INF)
    top = jnp.max(sc, axis=-1, keepdims=True)
    if nb > 1:
        khp = jnp.concatenate([kh[:1], kh[:-1]], axis=0)
        vp = jnp.concatenate([v[:1], v[:-1]], axis=0)
        has_prev = lax.broadcasted_iota(jnp.int32, (g, 1, 1), 0) % nb != 0
        sp = jnp.where((kj >= qi) & has_prev, _bmm_nt(qh, khp) * scale, NEG_INF)
        top = jnp.maximum(top, jnp.max(sp, axis=-1, keepdims=True))
    m = lax.stop_gradient(top)
    pc = jnp.exp(sc - m)
    den = jnp.sum(pc, axis=-1, keepdims=True)
    acc = _bmm_nn(pc, v)
    if nb > 1:
        pp = jnp.exp(sp - m)
        den = den + jnp.sum(pp, axis=-1, keepdims=True)
        acc = acc + _bmm_nn(pp, vp)
    o = acc / den
    return o, jnp.broadcast_to(m + jnp.log(den), o.shape)


def _pattern_rows(t, dil):
    nb = t // (ATT_BLOCK * dil)
    starts = [n * ATT_BLOCK * dil + r for r in range(dil) for n in range(nb)]
    return [pl.ds(s, ATT_BLOCK, stride=dil) if dil > 1 else pl.ds(s, ATT_BLOCK) for s in starts], nb


def _take(ref, rows):
    return jnp.stack([ref[0, r, :] for r in rows])


def _put(ref, rows, val):
    for g, r in enumerate(rows):
        ref[0, r, :] = val[g]


def _put_add(ref, rows, val):
    for g, r in enumerate(rows):
        ref[0, r, :] += val[g]


def _merge_fn(o1, o2, o3, l1, l2, l3):
    m = lax.stop_gradient(jnp.maximum(jnp.maximum(l1, l2), l3))
    e1, e2, e3 = jnp.exp(l1 - m), jnp.exp(l2 - m), jnp.exp(l3 - m)
    return (e1 * o1 + e2 * o2 + e3 * o3) / (e1 + e2 + e3)


def _token_rows(j):
    return pl.ds(pl.multiple_of(j * ATT_BLOCK, ATT_BLOCK), ATT_BLOCK)


def _norm_rows(t, q_ref, k_ref, gq, gk, qh_ref, kh_ref):
    def step(j, carry):
        rows = _token_rows(j)
        qh_ref[0, rows, :] = _head_norm(q_ref[0, rows, :], gq[0])
        kh_ref[0, rows, :] = _head_norm(k_ref[0, rows, :], gk[0])
        return carry

    lax.fori_loop(0, t // ATT_BLOCK, step, 0)


def _att_head_specs(t):
    head = pl.BlockSpec((1, t, HEAD_DIM), lambda h: (h, 0, 0))
    gain = pl.BlockSpec((1, 1, HEAD_DIM), lambda h: (0, 0, 0))
    return head, gain


def _att_fwd(q, k, v, qn, kn):
    nh, t, dh = q.shape
    head, gain = _att_head_specs(t)

    def body(q_ref, k_ref, v_ref, qn_ref, kn_ref, att_ref, o1, o2, o3, l1, l2, l3, qh_ref, kh_ref):
        saved = (o1, o2, o3, l1, l2, l3)
        _norm_rows(t, q_ref, k_ref, qn_ref[...], kn_ref[...], qh_ref, kh_ref)
        for p, dil in enumerate(DILATIONS):
            rows, nb = _pattern_rows(t, dil)
            o, lse = _att_pattern(_take(qh_ref, rows), _take(kh_ref, rows), _take(v_ref, rows), nb)
            _put(saved[p], rows, o)
            _put(saved[3 + p], rows, lse)

        def merge(j, carry):
            rows = _token_rows(j)
            att_ref[0, rows, :] = _merge_fn(*[r[0, rows, :] for r in saved])
            return carry

        lax.fori_loop(0, t // ATT_BLOCK, merge, 0)

    return pl.pallas_call(
        body, name="att_fwd", grid=(nh,), out_shape=(jax.ShapeDtypeStruct(q.shape, F32),) * 7,
        in_specs=[head, head, head, gain, gain], out_specs=(head,) * 7,
        scratch_shapes=[pltpu.VMEM((1, t, dh), F32)] * 2, compiler_params=_params("arbitrary"),
    )(q, k, v, qn, kn)


def _att_bwd(q, k, v, qn, kn, saved, datt):
    nh, t, dh = q.shape
    head, gain = _att_head_specs(t)

    def body(q_ref, k_ref, v_ref, qn_ref, kn_ref, o1, o2, o3, l1, l2, l3, datt_ref,
             dq_ref, dk_ref, dv_ref, dqn_ref, dkn_ref, qh_ref, kh_ref, dqh_ref, dkh_ref, *ct_refs):
        for ref in (dqh_ref, dkh_ref, dv_ref):
            ref[...] = jnp.zeros_like(ref)

        @pl.when(pl.program_id(0) == 0)
        def _():
            dqn_ref[...] = jnp.zeros_like(dqn_ref)
            dkn_ref[...] = jnp.zeros_like(dkn_ref)

        gq, gk = qn_ref[...], kn_ref[...]
        _norm_rows(t, q_ref, k_ref, gq, gk, qh_ref, kh_ref)

        def merge_cotangents(j, carry):
            rows = _token_rows(j)
            _, merge_vjp = jax.vjp(_merge_fn, *[r[0, rows, :] for r in (o1, o2, o3, l1, l2, l3)])
            for ref, val in zip(ct_refs, merge_vjp(datt_ref[0, rows, :])):
                ref[0, rows, :] = val
            return carry

        lax.fori_loop(0, t // ATT_BLOCK, merge_cotangents, 0)

        for p, dil in enumerate(DILATIONS):
            rows, nb = _pattern_rows(t, dil)
            _, pattern_vjp = jax.vjp(functools.partial(_att_pattern, nb=nb), _take(qh_ref, rows), _take(kh_ref, rows),
                                     _take(v_ref, rows))
            dqh, dkh, dv = pattern_vjp((_take(ct_refs[p], rows), _take(ct_refs[3 + p], rows)))
            _put_add(dqh_ref, rows, dqh)
            _put_add(dkh_ref, rows, dkh)
            _put_add(dv_ref, rows, dv)

        def norm_cotangents(j, carry):
            rows = _token_rows(j)
            out = []
            for x_ref, gain, dh_ref, dx_ref, acc in ((q_ref, gq, dqh_ref, dq_ref, carry[0]), (k_ref, gk, dkh_ref, dk_ref, carry[1])):
                _, norm_vjp = jax.vjp(_head_norm, x_ref[0, rows, :], gain[0])
                dx, dgain = norm_vjp(dh_ref[0, rows, :])
                dx_ref[0, rows, :] = dx
                out.append(acc + dgain)
            return tuple(out)

        zero = jnp.zeros((1, dh), F32)
        dgq, dgk = lax.fori_loop(0, t // ATT_BLOCK, norm_cotangents, (zero, zero))
        dqn_ref[0] += dgq
        dkn_ref[0] += dgk

    hshape = jax.ShapeDtypeStruct(q.shape, F32)
    gshape = jax.ShapeDtypeStruct((1, 1, dh), F32)
    return pl.pallas_call(
        body, name="att_bwd", grid=(nh,), out_shape=(hshape, hshape, hshape, gshape, gshape),
        in_specs=[head, head, head, gain, gain] + [head] * 7, out_specs=(head, head, head, gain, gain),
        scratch_shapes=[pltpu.VMEM((1, t, dh), F32)] * 10, compiler_params=_params("arbitrary"),
    )(q, k, v, qn, kn, *saved, datt)


RWKV_VEC = ("mu_r", "mu_k", "mu_v", "mu_w", "mu_a", "mu_g", "w0", "a0", "k_k", "k_a")
RWKV_MAT = ("w1", "w2", "a1", "a2", "g1", "g2")


def _rwkv_pre_fn(cur, prev, vec, w1, w2, a1, a2, g1, g2):
    c = cur.shape[1] // 4
    mu_r, mu_k, mu_v, mu_w, mu_a, mu_g, w0, a0, k_k, k_a = (vec[j:j + 1] for j in range(10))

    def lerp(j, mu):
        xc, xp = cur[:, j * c:(j + 1) * c], prev[:, j * c:(j + 1) * c]
        return xc + (xp - xc) * mu

    r, k, v = lerp(0, mu_r), lerp(1, mu_k), lerp(2, mu_v)
    cw, ca, cg = lerp(3, mu_w), lerp(3, mu_a), lerp(3, mu_g)
    z = w0 + _mm(jnp.tanh(_mm(cw, w1)), w2)
    w_log = jnp.minimum(z, 0.0) - jnp.log(1.0 + jnp.exp(-jnp.abs(z))) - 0.5
    lw = -jnp.exp(w_log)
    a = _sigmoid(a0 + _mm(_mm(ca, a1), a2))
    gate = _mm(_sigmoid(_mm(cg, g1)), g2)
    kkraw = k * k_k
    kmod = k * (1.0 + (a - 1.0) * k_a)
    return r, lw, kmod, v, kkraw, a, gate


HALO_ROWS = 8


def _rwkv_pre_specs(c, mats, tile_of):
    tm = TOKEN_TILE
    nh = c // HEAD_DIM
    wide = pl.BlockSpec((tm, 4 * c), lambda j: (tile_of(j), 0))
    halo = pl.BlockSpec((HALO_ROWS, 4 * c), lambda j: (jnp.maximum(tile_of(j) * (tm // HALO_ROWS) - 1, 0), 0))
    one = pl.BlockSpec((tm, c), lambda j: (tile_of(j), 0))
    heads = pl.BlockSpec((nh, tm, HEAD_DIM), lambda j: (0, tile_of(j), 0))
    vec = pl.BlockSpec((10, c), lambda j: (0, 0))
    mspecs = [pl.BlockSpec(m.shape, lambda j: (0, 0)) for m in mats]
    return wide, halo, one, heads, vec, mspecs


def _previous_rows(cur, halo, tile):
    first = jnp.where(tile > 0, halo[HALO_ROWS - 1:HALO_ROWS], 0.0)
    rows = lax.broadcasted_iota(jnp.int32, cur.shape, 0)
    return jnp.where(rows == 0, first, pltpu.roll(cur, 1, axis=0))


def _rwkv_pre_fwd(cur, vec, mats):
    t, c4 = cur.shape
    c = c4 // 4
    wide, halo, one, heads, vspec, mspecs = _rwkv_pre_specs(c, mats, lambda j: j)

    def body(cur_ref, halo_ref, vec_ref, *rest):
        mrefs, outs = rest[:6], rest[6:]
        cur_v = cur_ref[...]
        prev = _previous_rows(cur_v, halo_ref[...], pl.program_id(0))
        vals = _rwkv_pre_fn(cur_v, prev, vec_ref[...], *(m[...] for m in mrefs))
        for ref, val in zip(outs[:6], vals[:6]):
            _store_heads(ref, val)
        outs[6][...] = vals[6]

    hshape = jax.ShapeDtypeStruct((c // HEAD_DIM, t, HEAD_DIM), F32)
    return pl.pallas_call(
        body, name="rwkv_pre_fwd", grid=(t // TOKEN_TILE,), out_shape=(hshape,) * 6 + (jax.ShapeDtypeStruct((t, c), F32),),
        in_specs=[wide, halo, vspec] + mspecs, out_specs=(heads,) * 6 + (one,), compiler_params=_params("arbitrary"),
    )(cur, cur, vec, *mats)


def _rwkv_pre_bwd(cur, vec, mats, cts, dgate):
    t, c4 = cur.shape
    c = c4 // 4
    tm = TOKEN_TILE
    nt = t // tm
    wide, halo, one, heads, vspec, mspecs = _rwkv_pre_specs(c, mats, lambda j: nt - 1 - j)

    def body(cur_ref, halo_ref, vec_ref, *rest):
        mrefs, ctrefs, dgate_ref, outs, carry_ref = rest[:6], rest[6:12], rest[12], rest[13:-1], rest[-1]
        j = pl.program_id(0)

        @pl.when(j == 0)
        def _():
            carry_ref[...] = jnp.zeros_like(carry_ref)
            for ref in outs[1:]:
                ref[...] = jnp.zeros_like(ref)

        cur_v = cur_ref[...]
        prev = _previous_rows(cur_v, halo_ref[...], nt - 1 - j)
        _, vjp = jax.vjp(_rwkv_pre_fn, cur_v, prev, vec_ref[...], *(m[...] for m in mrefs))
        grads = vjp(tuple(_load_heads(r) for r in ctrefs) + (dgate_ref[...],))
        dprev = grads[1]
        rows = lax.broadcasted_iota(jnp.int32, dprev.shape, 0)
        outs[0][...] = grads[0] + jnp.where(rows == tm - 1, carry_ref[0:1], pltpu.roll(dprev, tm - 1, axis=0))
        carry_ref[0:1] = dprev[0:1]
        for ref, val in zip(outs[1:], grads[2:]):
            ref[...] += val

    return pl.pallas_call(
        body, name="rwkv_pre_bwd", grid=(nt,),
        out_shape=(jax.ShapeDtypeStruct(cur.shape, F32), jax.ShapeDtypeStruct(vec.shape, F32))
        + tuple(jax.ShapeDtypeStruct(m.shape, F32) for m in mats),
        in_specs=[wide, halo, vspec] + mspecs + [heads] * 6 + [one], out_specs=(wide, vspec) + tuple(mspecs),
        scratch_shapes=[pltpu.VMEM((HALO_ROWS, c4), F32)], compiler_params=_params("arbitrary"),
    )(cur, cur, vec, *mats, *cts, dgate)


def _scan_chunk_fn(h0, r, lw, k, v, kkraw, a, rk, lnw, lnb):
    n = r.shape[1]
    nrm = jnp.sqrt(jnp.sum(kkraw * kkraw, axis=-1, keepdims=True))
    kk = kkraw / jnp.maximum(nrm, 1e-12)
    av, bv = -kk, kk * a
    ti = lax.broadcasted_iota(jnp.int32, (n, n), 0)
    si = lax.broadcasted_iota(jnp.int32, (n, n), 1)
    incl, strict = ti >= si, ti > si
    ones = jnp.broadcast_to(incl.astype(F32)[None], (r.shape[0], n, n))
    cum = _hdot(ones, lw, 2, 1)
    at, rt = av * jnp.exp(cum - lw), r * jnp.exp(cum)
    inv = jnp.exp(-cum)
    bt, kt = bv * inv, k * inv
    gram = _hdot(jnp.concatenate([at, rt], axis=1), jnp.concatenate([bt, kt], axis=1), 2, 2)
    lab = jnp.where(strict, gram[:, :n, :n], 0.0)
    lak = jnp.where(strict, gram[:, :n, n:], 0.0)
    rb = jnp.where(incl, gram[:, n:, :n], 0.0)
    rkm = jnp.where(incl, gram[:, n:, n:], 0.0)
    nv = v.shape[2]
    u = _bmm_nn(jnp.concatenate([at, lak], axis=2), jnp.concatenate([h0, v], axis=1))
    p = lab
    m = 2
    while m < n:
        both = _bmm_nn(p, jnp.concatenate([u, p], axis=2))
        u, p = u + both[:, :, :nv], both[:, :, nv:]
        m *= 2
    u = u + _bmm_nn(p, u)
    y = _bmm_nn(jnp.concatenate([rt, rb, rkm], axis=2), jnp.concatenate([h0, u, v], axis=1))
    last = jnp.exp(jnp.sum(lw, axis=1, keepdims=True))
    h1 = jnp.swapaxes(last, 1, 2) * (h0 + _bmm_tn(jnp.concatenate([bt, kt], axis=1), jnp.concatenate([u, v], axis=1)))
    mean = jnp.mean(y, axis=-1, keepdims=True)
    yc = y - mean
    var = jnp.mean(yc * yc, axis=-1, keepdims=True)
    yn = yc * lax.rsqrt(var + GN_EPS) * lnw + lnb
    bonus = jnp.sum(r * k * rk, axis=-1, keepdims=True) * v
    return yn + bonus, h1


SCAN_GROUP = 2


def _scan_group_fn(h0, r, lw, k, v, kkraw, a, rk, lnw, lnb):
    outs = []
    for j in range(SCAN_GROUP):
        rows = slice(j * SCAN_CHUNK, (j + 1) * SCAN_CHUNK)
        o, h0 = _scan_chunk_fn(h0, r[:, rows], lw[:, rows], k[:, rows], v[:, rows], kkraw[:, rows], a[:, rows], rk, lnw, lnb)
        outs.append(o)
    return jnp.concatenate(outs, axis=1), h0


def _scan_specs(h, t, dh, rev):
    n = SCAN_CHUNK * SCAN_GROUP
    nc = t // n
    pos = (lambda c: (0, nc - 1 - c, 0)) if rev else (lambda c: (0, c, 0))
    st = (lambda c: (nc - 1 - c, 0, 0, 0)) if rev else (lambda c: (c, 0, 0, 0))
    seq = pl.BlockSpec((h, n, dh), pos)
    par = pl.BlockSpec((h, 1, dh), lambda c: (0, 0, 0))
    state = pl.BlockSpec((1, h, dh, dh), st)
    return seq, par, state


def _scan_fwd(seqs, pars):
    h, t, dh = seqs[0].shape
    nc = t // (SCAN_CHUNK * SCAN_GROUP)
    seq, par, state = _scan_specs(h, t, dh, False)

    def body(r, lw, k, v, kkraw, a, rk, lnw, lnb, o_ref, st_ref, h_ref):
        @pl.when(pl.program_id(0) == 0)
        def _():
            h_ref[...] = jnp.zeros_like(h_ref)

        h0 = h_ref[...]
        st_ref[0] = h0
        o, h1 = _scan_group_fn(h0, r[...], lw[...], k[...], v[...], kkraw[...], a[...], rk[...], lnw[...], lnb[...])
        o_ref[...] = o
        h_ref[...] = h1

    return pl.pallas_call(
        body, name="rwkv_scan_fwd", grid=(nc,),
        out_shape=(jax.ShapeDtypeStruct((h, t, dh), F32), jax.ShapeDtypeStruct((nc, h, dh, dh), F32)),
        in_specs=[seq] * 6 + [par] * 3, out_specs=(seq, state),
        scratch_shapes=[pltpu.VMEM((h, dh, dh), F32)], compiler_params=_params("arbitrary"),
    )(*seqs, *pars)


def _scan_bwd(seqs, pars, states, do):
    h, t, dh = seqs[0].shape
    nc = t // (SCAN_CHUNK * SCAN_GROUP)
    seq, par, state = _scan_specs(h, t, dh, True)

    def body(r, lw, k, v, kkraw, a, rk, lnw, lnb, st_ref, do_ref, *rest):
        douts, dpars, dh_ref = rest[:6], rest[6:9], rest[9]
        first = pl.program_id(0) == 0

        @pl.when(first)
        def _():
            dh_ref[...] = jnp.zeros_like(dh_ref)

        _, vjp = jax.vjp(_scan_group_fn, st_ref[0], r[...], lw[...], k[...], v[...], kkraw[...], a[...],
                         rk[...], lnw[...], lnb[...])
        grads = vjp((do_ref[...], dh_ref[...]))
        dh_ref[...] = grads[0]
        for ref, val in zip(douts, grads[1:7]):
            ref[...] = val

        @pl.when(first)
        def _():
            for ref, val in zip(dpars, grads[7:]):
                ref[...] = val

        @pl.when(jnp.logical_not(first))
        def _():
            for ref, val in zip(dpars, grads[7:]):
                ref[...] += val

    sshape = jax.ShapeDtypeStruct((h, t, dh), F32)
    pshape = jax.ShapeDtypeStruct((h, 1, dh), F32)
    return pl.pallas_call(
        body, name="rwkv_scan_bwd", grid=(nc,), out_shape=(sshape,) * 6 + (pshape,) * 3,
        in_specs=[seq] * 6 + [par] * 3 + [state, seq], out_specs=(seq,) * 6 + (par,) * 3,
        scratch_shapes=[pltpu.VMEM((h, dh, dh), F32)], compiler_params=_params("arbitrary"),
    )(*seqs, *pars, states, do)


def _local_step(x, target, w, ex):
    w = dict(w)
    c = w["mu_r"].shape[-1]
    qn, kn = w["q_norm"].reshape(1, 1, HEAD_DIM), w["k_norm"].reshape(1, 1, HEAD_DIM)
    vec = jnp.concatenate([w[n].reshape(1, c) for n in RWKV_VEC], axis=0)
    pars = [w[n].reshape(-1, 1, HEAD_DIM) for n in ("r_k", "ln_x_w", "ln_x_b")]
    no_dep = jnp.zeros(DEP_SHAPE, F32)

    x1, gate1, up1 = _ffn_fwd(x, w["ffn1_norm"], w["ffn1_w_gate"], w["ffn1_w_up"], w["ffn1_w_down"], ex.first_dep, "ffn1_fwd")
    w.update(ex.mix_weights((x1,)))
    mats = [w[n] for n in RWKV_MAT]
    q, k, v, cur = _proj_fwd(x1, w["mix_norm"], w["w_in"], c)
    att, *saved = _att_fwd(q, k, v, qn, kn)
    pre = _rwkv_pre_fwd(cur, vec, mats)
    seqs, gate = pre[:6], pre[6]
    opg, states = _scan_fwd(seqs, pars)
    w.update(ex.out_weights((att, opg)))
    x2 = _mixout_fwd(x1, att, opg, gate, w["w_out"])
    dy, gate2, up2, loss = _ffn_fwd(x2, w["ffn2_norm"], w["ffn2_w_gate"], w["ffn2_w_up"], w["ffn2_w_down"], no_dep, "ffn2_fwd",
                                    target=target)

    g = {}
    dx2, g["ffn2_norm"], g["ffn2_w_gate"], g["ffn2_w_up"], g["ffn2_w_down"] = _ffn_bwd(
        x2, w["ffn2_norm"], w["ffn2_w_gate"], w["ffn2_w_up"], w["ffn2_w_down"], gate2, up2, dy, no_dep, "ffn2_bwd")
    dep = ex.send_ffn2({n: g[n] for n in ("ffn2_w_gate", "ffn2_w_up", "ffn2_w_down")})
    datt, dopg, dgate, g["w_out"] = _mixout_bwd(att, opg, gate, w["w_out"], dx2, dep)
    dscan = _scan_bwd(seqs, pars, states, dopg)
    for n, d in zip(("r_k", "ln_x_w", "ln_x_b"), dscan[6:]):
        g[n] = d
    dcur, dvec, *dmats = _rwkv_pre_bwd(cur, vec, mats, dscan[:6], dgate)
    for n, d in zip(RWKV_MAT, dmats):
        g[n] = d
    for j, n in enumerate(RWKV_VEC):
        g[n] = dvec[j:j + 1]
    dq, dk, dv, g["q_norm"], g["k_norm"] = _att_bwd(q, k, v, qn, kn, saved, datt)
    dx1, g["mix_norm"], g["w_in"] = _proj_bwd(x1, w["mix_norm"], w["w_in"], dq, dk, dv, dcur, dx2)
    dep = ex.send_mix({n: g[n] for n in ("w_in", "w_out") + RWKV_MAT}, (dx1,))
    dx, g["ffn1_norm"], g["ffn1_w_gate"], g["ffn1_w_up"], g["ffn1_w_down"] = _ffn_bwd(
        x, w["ffn1_norm"], w["ffn1_w_gate"], w["ffn1_w_up"], w["ffn1_w_down"], gate1, up1, dx1, dep, "ffn1_bwd")
    return loss, dx, g


N_SHARDS = 4


def _place():
    return lax.axis_index("x"), lax.axis_index("y"), lax.axis_index("c")


def _chip_peers(x, y):
    return [(1 - x, y), (x, 1 - y), (1 - x, 1 - y)]


HBM = pl.BlockSpec(memory_space=pltpu.HBM)
SEM = pl.BlockSpec(memory_space=pltpu.SEMAPHORE)
DEP_SHAPE = (8, 128)


class _Views:
    to_sibling = False


class _GatherViews(_Views):
    @staticmethod
    def send(i, srcs, lands, k, at):
        return srcs[i], lands[i].at[at[3]]

    @staticmethod
    def landing(i, srcs, lands, k, at):
        return srcs[i], lands[i].at[2 * at[4] + at[5]]


class _ScatterViews(_Views):
    @staticmethod
    def send(i, srcs, lands, k, at):
        return srcs[i].at[2 * at[4] + at[5]], lands[i].at[k]

    @staticmethod
    def landing(i, srcs, lands, k, at):
        return srcs[i].at[at[3]], lands[i].at[k]


def _half_rows(ref, slot, half):
    rows = ref.shape[1] // 2
    return ref.at[slot, pl.ds(pl.multiple_of(half * rows, BF16_SUBLANES), rows)]


class _HalfGatherViews(_Views):
    @staticmethod
    def send(i, srcs, lands, k, at):
        rows = srcs[i].shape[0] // 2
        return srcs[i].at[pl.ds(pl.multiple_of(at[2] * rows, BF16_SUBLANES), rows)], _half_rows(lands[i], at[3], at[2])

    @staticmethod
    def landing(i, srcs, lands, k, at):
        rows = srcs[i].shape[0] // 2
        return srcs[i].at[pl.ds(pl.multiple_of(at[2] * rows, BF16_SUBLANES), rows)], _half_rows(lands[i], 2 * at[4] + at[5], at[2])


class _ForwardViews(_Views):
    to_sibling = True

    @staticmethod
    def send(i, srcs, lands, k, at):
        mine = _half_rows(lands[i], 2 * at[4] + at[5], at[2])
        return mine, mine

    @staticmethod
    def landing(i, srcs, lands, k, at):
        theirs = _half_rows(lands[i], 2 * at[4] + at[5], 1 - at[2])
        return theirs, theirs


def _push_start(srcs, lands, views, after, name):
    ns, nl = len(srcs), len(lands)

    def body(*refs):
        src_refs, land_refs = refs[:ns], refs[ns:ns + nl]
        send_sems, recv_sems = refs[ns + nl + 1:ns + nl + 3]
        token = refs[2 * (ns + nl) + 3]
        x, y, c = _place()
        for i in range(nl):
            for k, (px, py) in enumerate(_chip_peers(x, y)):
                src, dst = views.send(i, src_refs, land_refs, k, (x, y, c, 2 * x + y, px, py))
                pltpu.make_async_remote_copy(
                    src_ref=src, dst_ref=dst, send_sem=send_sems.at[3 * i + k], recv_sem=recv_sems.at[3 * i + k],
                    device_id=(x, y, 1 - c) if views.to_sibling else (px, py, c), device_id_type=MESH).start()
        token[...] = jnp.zeros_like(token)

    sems = pltpu.SemaphoreType.DMA((3 * nl,))
    both = [pltpu.with_memory_space_constraint(a, pltpu.HBM) for a in (*srcs, *lands)]
    outs = pl.pallas_call(
        body, name=name,
        out_shape=(sems, sems, *[pltpu.HBM(a.shape, a.dtype) for a in both], jax.ShapeDtypeStruct(DEP_SHAPE, F32)),
        in_specs=[HBM] * (ns + nl) + [ANY], out_specs=(SEM, SEM, *[HBM] * (ns + nl), VMEM_FULL),
        input_output_aliases={i: 2 + i for i in range(ns + nl)},
        compiler_params=pltpu.CompilerParams(has_side_effects=pltpu.SideEffectType.DATAFLOW_SIDE_EFFECTING),
    )(*both, after)
    return outs[0], outs[1], outs[2:2 + ns], outs[2 + ns:2 + ns + nl], outs[2 + ns + nl]


def _push_wait(started, views, after, name, with_sources=False):
    send_sems, recv_sems, srcs, lands, _ = started
    ns, nl = len(srcs), len(lands)

    def body(*refs):
        src_refs, land_refs = refs[:ns], refs[ns:ns + nl]
        send_sems, recv_sems = refs[ns + nl:ns + nl + 2]
        x, y, c = _place()
        for i in range(nl):
            for k, (px, py) in enumerate(_chip_peers(x, y)):
                src, dst = views.landing(i, src_refs, land_refs, k, (x, y, c, 2 * x + y, px, py))
                landing = pltpu.make_async_remote_copy(
                    src_ref=src, dst_ref=dst, send_sem=send_sems.at[3 * i + k], recv_sem=recv_sems.at[3 * i + k],
                    device_id=(x, y, 1 - c) if views.to_sibling else (px, py, c), device_id_type=MESH)
                landing.wait_send()
                landing.wait_recv()

    outs = pl.pallas_call(
        body, name=name,
        out_shape=tuple(pltpu.HBM(a.shape, a.dtype) for a in (*srcs, *lands)),
        in_specs=[HBM] * (ns + nl) + [SEM, SEM] + [ANY] * len(after), out_specs=(HBM,) * (ns + nl),
        input_output_aliases={i: i for i in range(ns + nl)},
        compiler_params=pltpu.CompilerParams(has_side_effects=pltpu.SideEffectType.DATAFLOW_SIDE_EFFECTING),
    )(*srcs, *lands, send_sems, recv_sems, *after)
    return outs if with_sources else outs[ns:]


def _empty_lands(shards, slots, own_slot):
    lands = [lax.empty((slots,) + s.shape, s.dtype) for s in shards]
    if own_slot:
        me = 2 * lax.axis_index("x") + lax.axis_index("y")
        lands = [lax.dynamic_update_index_in_dim(z, s, me, 0) for z, s in zip(lands, shards)]
    return lands


def _sibling_swap(arrays, name, other_half=False):
    n = len(arrays)

    def body(*refs):
        ins, outs = refs[:n], refs[n:2 * n]
        send_sems, recv_sems = refs[2 * n:]
        x, y, c = _place()
        copies = []
        for i in range(n):
            src = ins[i]
            if other_half:
                rows = src.shape[1] // 2
                src = src.at[:, pl.ds(pl.multiple_of((1 - c) * rows, BF16_SUBLANES), rows)]
            cp = pltpu.make_async_remote_copy(
                src_ref=src, dst_ref=outs[i], send_sem=send_sems.at[i], recv_sem=recv_sems.at[i],
                device_id=(x, y, 1 - c), device_id_type=MESH)
            cp.start()
            copies.append(cp)
        for cp in copies:
            cp.wait()

    shapes = [(a.shape[0], a.shape[1] // 2, a.shape[2]) if other_half else a.shape for a in arrays]
    return pl.pallas_call(
        body, name=name,
        out_shape=tuple(jax.ShapeDtypeStruct(s, a.dtype) for s, a in zip(shapes, arrays)),
        in_specs=[ANY] * n, out_specs=(ANY,) * n,
        scratch_shapes=[pltpu.SemaphoreType.DMA((n,)), pltpu.SemaphoreType.DMA((n,))],
    )(*arrays)


FOLD_STEPS = 2


def _fold_add(core, parts, theirs, name):
    n = len(parts)
    s, r, cols = parts[0].shape
    tr = r // 2 // FOLD_STEPS

    def body(core_ref, *refs):
        for p_ref, t_ref, o_ref in zip(refs[:n], refs[n:2 * n], refs[2 * n:]):
            o_ref[...] = (p_ref[...].astype(F32) + t_ref[...].astype(F32)).astype(BF16)

    half = pl.BlockSpec((1, tr, cols), lambda j, i, core_ref: (j, i, 0))
    return pl.pallas_call(
        body, name=name, out_shape=tuple(jax.ShapeDtypeStruct((s, r // 2, cols), BF16) for _ in parts),
        grid_spec=pltpu.PrefetchScalarGridSpec(
            num_scalar_prefetch=1, grid=(s, FOLD_STEPS),
            in_specs=[pl.BlockSpec((1, tr, cols), lambda j, i, core_ref: (j, core_ref[0] * FOLD_STEPS + i, 0))] * n + [half] * n,
            out_specs=(half,) * n),
        compiler_params=_params("arbitrary", "arbitrary"),
    )(core, *parts, *theirs)


N_DEV = 8


def _allreduce_small(pack):
    def body(in_ref, out_ref, buf, send_sems, recv_sems):
        x, y, c = _place()
        me = 4 * x + 2 * y + c
        buf[me] = in_ref[...]

        def copy(j, slot):
            px, py, pc = x ^ (j >> 2), y ^ ((j >> 1) & 1), c ^ (j & 1)
            return pltpu.make_async_remote_copy(
                src_ref=in_ref, dst_ref=buf.at[slot(px, py, pc)], send_sem=send_sems.at[j], recv_sem=recv_sems.at[j],
                device_id=(px, py, pc), device_id_type=MESH)

        for j in range(1, N_DEV):
            copy(j, lambda px, py, pc: me).start()
        for j in range(1, N_DEV):
            landing = copy(j, lambda px, py, pc: 4 * px + 2 * py + pc)
            landing.wait_send()
            landing.wait_recv()
        acc = buf[0]
        for s in range(1, N_DEV):
            acc = acc + buf[s]
        out_ref[...] = acc

    return pl.pallas_call(
        body, name="allreduce_small", out_shape=jax.ShapeDtypeStruct(pack.shape, F32),
        in_specs=[VMEM_FULL], out_specs=VMEM_FULL,
        scratch_shapes=[pltpu.VMEM((N_DEV,) + pack.shape, F32), pltpu.SemaphoreType.DMA((N_DEV,)),
                        pltpu.SemaphoreType.DMA((N_DEV,))],
    )(pack)


BF16_SUBLANES = 16


def _reduce_own(me, parts, recvs, dep, steps, name):
    n = len(parts)

    def body(me_ref, *refs):
        for p_ref, rv_ref, o_ref in zip(refs[:n], refs[n:2 * n], refs[2 * n + 1:]):
            acc = p_ref[0].astype(F32)
            for k in range(3):
                acc = acc + rv_ref[k].astype(F32)
            o_ref[...] = acc

    shapes = [(p.shape[1] // steps, p.shape[2]) for p in parts]
    return pl.pallas_call(
        body, name=name, out_shape=tuple(jax.ShapeDtypeStruct(p.shape[1:], F32) for p in parts),
        grid_spec=pltpu.PrefetchScalarGridSpec(
            num_scalar_prefetch=1, grid=(steps,),
            in_specs=[pl.BlockSpec((1, tr, c), lambda i, me_ref: (me_ref[0], i, 0)) for tr, c in shapes]
            + [pl.BlockSpec((3, tr, c), lambda i, me_ref: (0, i, 0)) for tr, c in shapes] + [ANY],
            out_specs=tuple(pl.BlockSpec((tr, c), lambda i, me_ref: (i, 0)) for tr, c in shapes)),
        compiler_params=_params("arbitrary"),
    )(me, *parts, *recvs, dep)


def _adamw(ws, gas, gbs, ms, vs, steps, name):
    n = len(ws)
    c1 = 1.0 - ADAM_B1 ** ADAM_STEP
    c2 = 1.0 - ADAM_B2 ** ADAM_STEP
    operands = [ws, gas, ms, vs] if gbs is None else [ws, gas, gbs, ms, vs]
    k = len(operands)

    def body(*refs):
        ins, outs = refs[:k * n], refs[k * n:]
        for j in range(n):
            w_ref, ga_ref, *gb_ref, m_ref, v_ref = ins[j::n]
            g_out, d_out, m_out, v_out = outs[j::n]
            g = ga_ref[...] + gb_ref[0][...] if gb_ref else ga_ref[...]
            mn = ADAM_B1 * m_ref[...] + (1.0 - ADAM_B1) * g
            vn = ADAM_B2 * v_ref[...] + (1.0 - ADAM_B2) * (g * g)
            g_out[...] = g
            m_out[...] = mn
            v_out[...] = vn
            d_out[...] = -ADAM_LR * ((mn / c1) / (jnp.sqrt(vn / c2) + ADAM_EPS) + ADAM_WD * w_ref[...])

    tiles = [pl.BlockSpec((w.shape[0] // steps, w.shape[1]), lambda i: (i, 0)) for w in ws]
    shapes = [jax.ShapeDtypeStruct(w.shape, F32) for w in ws]
    outs = pl.pallas_call(
        body, name=name, grid=(steps,), out_shape=tuple(shapes * 4), in_specs=tiles * k, out_specs=tuple(tiles * 4),
        compiler_params=_params("arbitrary"),
    )(*[a for group in operands for a in group])
    return [outs[j::n] for j in range(n)]


PACK_COLS = 512


def _to_rows(a):
    flat = a.reshape(-1)
    pad = (-flat.shape[0]) % PACK_COLS
    return jnp.pad(flat, (0, pad)).reshape(-1, PACK_COLS)


def _pack(arrays, extra_rows=0):
    rows = [_to_rows(a) for a in arrays]
    n = sum(r.shape[0] for r in rows) + extra_rows
    pad = (-n) % 8
    return jnp.concatenate(rows + [jnp.zeros((extra_rows + pad, PACK_COLS), F32)], axis=0)


def _unpack(pack, like):
    out, at = [], 0
    for a in like:
        n = -(-a.size // PACK_COLS)
        out.append(pack[at:at + n].reshape(-1)[:a.size].reshape(a.shape))
        at += n
    return out


COL_SHARDED = ("ffn1_w_gate", "ffn1_w_up", "w_in", "ffn2_w_gate", "ffn2_w_up", "w2", "a2", "g2")
ROW_SHARDED = ("ffn1_w_down", "ffn2_w_down", "w_out", "w1", "a1", "g1")
CHUNKED = ("ffn1_w_gate", "ffn1_w_up", "ffn1_w_down", "w_in", "ffn2_w_gate", "ffn2_w_up", "ffn2_w_down")
WEIGHTS = ("ffn1_norm", "ffn1_w_gate", "ffn1_w_up", "ffn1_w_down", "mix_norm", "w_in", "q_norm", "k_norm",
           "mu_r", "mu_k", "mu_v", "mu_w", "mu_a", "mu_g", "w0", "w1", "w2", "a0", "a1", "a2", "g1", "g2",
           "k_k", "k_a", "r_k", "ln_x_w", "ln_x_b", "w_out", "ffn2_norm", "ffn2_w_gate", "ffn2_w_up", "ffn2_w_down")


TRANSPOSED = ("ffn1_w_gate", "ffn1_w_up", "ffn2_w_gate", "ffn2_w_up")


def _shard_2d(name, a):
    return a[0].T if name in TRANSPOSED else a[0]


def _full_from_blocks(name, blocks):
    if name in CHUNKED:
        return blocks
    if name in ROW_SHARDED:
        return blocks.reshape(-1, blocks.shape[-1])
    return blocks.transpose(1, 0, 2).reshape(blocks.shape[1], -1)


def _blocks_from_full(name, full):
    if name in CHUNKED:
        return full
    if name in ROW_SHARDED:
        return full.reshape(N_SHARDS, -1, full.shape[-1])
    return full.reshape(full.shape[0], N_SHARDS, -1).transpose(1, 0, 2)


FFN1_GROUP = ("ffn1_w_gate", "ffn1_w_up", "ffn1_w_down")
MIX_GROUP = ("w_in",) + RWKV_MAT
OUT_GROUP = ("w_out", "ffn2_w_gate", "ffn2_w_up", "ffn2_w_down")
FFN2_GROUP = OUT_GROUP[1:]
LATE_GROUP = ("w_in", "w_out") + RWKV_MAT


class _Exchange:
    def __init__(self, given):
        self.given = given
        first = self._gather_start(FFN1_GROUP, _HalfGatherViews, jnp.zeros(DEP_SHAPE, F32), "gather_ffn1_start")
        self.mix = self._gather_start(MIX_GROUP, _GatherViews, first[4], "gather_mix_start")
        self.out = self._gather_start(OUT_GROUP, _GatherViews, self.mix[4], "gather_out_start")
        self.first_dep = self.out[4]
        halves = _push_wait(first, _HalfGatherViews, (self.first_dep,), "gather_ffn1_wait")
        passed = _push_start([], halves, _ForwardViews, jnp.zeros(DEP_SHAPE, F32), "gather_ffn1_pass_start")
        self.first_weights = self._full(FFN1_GROUP, _push_wait(passed, _ForwardViews, (passed[4],), "gather_ffn1_pass_wait"))
        self.parts, self.recv = {}, {}

    def _shards(self, names):
        return [_shard_2d(n, self.given[n]).astype(BF16) for n in names]

    @staticmethod
    def _full(names, blocks):
        out = {}
        for n, b in zip(names, blocks):
            full = _full_from_blocks(n, b)
            out[n] = full.astype(F32) if n in RWKV_MAT else full
        return out

    def _gather_start(self, names, views, after, name):
        shards = self._shards(names)
        return _push_start(shards, _empty_lands(shards, N_SHARDS, True), views, after, name)

    def mix_weights(self, after):
        return self._full(MIX_GROUP, _push_wait(self.mix, _GatherViews, after, "gather_mix_wait"))

    def out_weights(self, after):
        return self._full(OUT_GROUP, _push_wait(self.out, _GatherViews, after, "gather_out_wait"))

    def _scatter_start(self, grads, name):
        names = tuple(grads)
        parts = [_blocks_from_full(n, grads[n]) for n in names]
        self.parts.update(zip(names, parts))
        lands = [lax.empty((3,) + p.shape[1:], BF16) for p in parts]
        return _push_start([p.astype(BF16) for p in parts], lands, _ScatterViews, jnp.zeros(DEP_SHAPE, F32), name)

    def _scatter_done(self, started, names, after, name):
        outs = _push_wait(started, _ScatterViews, after, name, with_sources=True)
        for n, sent, got in zip(names, outs[:len(names)], outs[len(names):]):
            self.recv[n] = got
            if self.parts[n].dtype == BF16:
                self.parts[n] = sent

    def send_ffn2(self, grads):
        self.ffn2 = self._scatter_start(grads, "scatter_ffn2_start")
        return self.ffn2[4]

    def send_mix(self, grads, after):
        self._scatter_done(self.ffn2, FFN2_GROUP, after, "scatter_ffn2_wait")
        self.late = self._scatter_start(grads, "scatter_late_start")
        return self.late[4]

    def send_ffn1(self, grads):
        self.ffn1 = self._scatter_start(grads, "scatter_ffn1_start")
        return self.ffn1[4]

    def late_received(self, after):
        self._scatter_done(self.late, LATE_GROUP, after, "scatter_late_wait")

    def ffn1_received(self, after):
        self._scatter_done(self.ffn1, FFN1_GROUP, after, "scatter_ffn1_wait")


def kernel(
        x, ffn1_norm, ffn1_w_gate, ffn1_w_up, ffn1_w_down, mix_norm, w_in, q_norm, k_norm, mu_r, mu_k, mu_v, mu_w,
        mu_a, mu_g, w0, w1, w2, a0, a1, a2, g1, g2, k_k, k_a, r_k, ln_x_w, ln_x_b, w_out, ffn2_norm, ffn2_w_gate,
        ffn2_w_up, ffn2_w_down, loss_target, m_ffn1_norm, m_ffn1_w_gate, m_ffn1_w_up, m_ffn1_w_down, m_mix_norm,
        m_w_in, m_q_norm, m_k_norm, m_mu_r, m_mu_k, m_mu_v, m_mu_w, m_mu_a, m_mu_g, m_w0, m_w1, m_w2, m_a0, m_a1,
        m_a2, m_g1, m_g2, m_k_k, m_k_a, m_r_k, m_ln_x_w, m_ln_x_b, m_w_out, m_ffn2_norm, m_ffn2_w_gate, m_ffn2_w_up,
        m_ffn2_w_down, v_ffn1_norm, v_ffn1_w_gate, v_ffn1_w_up, v_ffn1_w_down, v_mix_norm, v_w_in, v_q_norm, v_k_norm,
        v_mu_r, v_mu_k, v_mu_v, v_mu_w, v_mu_a, v_mu_g, v_w0, v_w1, v_w2, v_a0, v_a1, v_a2, v_g1, v_g2, v_k_k, v_k_a,
        v_r_k, v_ln_x_w, v_ln_x_b, v_w_out, v_ffn2_norm, v_ffn2_w_gate, v_ffn2_w_up, v_ffn2_w_down):
    given = dict(locals())
    sharded = COL_SHARDED + ROW_SHARDED
    sharded = tuple(n for n in WEIGHTS if n in sharded)
    small = tuple(n for n in WEIGHTS if n not in sharded)

    ex = _Exchange(given)
    w = {n: given[n] for n in small}
    w.update(ex.first_weights)
    loss, dx, g = _local_step(x[0], loss_target[0], w, ex)

    core = lax.axis_index("c").astype(jnp.int32).reshape(1)
    late = [g[n] for n in FFN1_GROUP]
    folded = _fold_add(core, late, _sibling_swap(late, "fold_swap_ffn1", other_half=True), "fold_add_ffn1")
    dep = ex.send_ffn1(dict(zip(FFN1_GROUP, folded)))

    me = (2 * lax.axis_index("x") + lax.axis_index("y")).astype(jnp.int32).reshape(1)
    out = {}

    def settle(names, dep, tag):
        done = []
        for kind, sub, r_steps, a_steps in (("large", tuple(n for n in names if n not in RWKV_MAT), 4, 8),
                                            ("small", tuple(n for n in names if n in RWKV_MAT), 1, 1)):
            if not sub:
                continue
            parts = [ex.parts[n].reshape(N_SHARDS, -1, ex.parts[n].shape[-1]) for n in sub]
            recvs = [ex.recv[n].reshape(3, -1, ex.recv[n].shape[-1]) for n in sub]
            mine = _reduce_own(me, parts, recvs, dep, r_steps, f"reduce_{tag}_{kind}")
            theirs = _sibling_swap(mine, f"sibling_swap_{tag}_{kind}")
            res = _adamw([_shard_2d(n, given[n]) for n in sub], mine, theirs, [_shard_2d(n, given["m_" + n]) for n in sub],
                         [_shard_2d(n, given["v_" + n]) for n in sub], a_steps, f"adamw_{tag}_{kind}")
            for n, rs in zip(sub, res):
                out[n] = [(r.T if n in TRANSPOSED else r).reshape(given[n].shape) for r in rs]
                done.append(out[n][1])
        return tuple(done)

    ex.late_received((dep,))
    last = settle(tuple(n for n in sharded if n not in FFN1_GROUP), dep, "rest")

    gpack = _pack([g[n] for n in small], extra_rows=1)
    n_rows = sum(-(-given[n].size // PACK_COLS) for n in small)
    gpack = gpack.at[n_rows, :loss.shape[1]].set(loss[0])
    gsum = _allreduce_small(gpack)
    res = _adamw([_pack([given[n] for n in small], 1)], [gsum], None, [_pack([given["m_" + n] for n in small], 1)],
                 [_pack([given["v_" + n] for n in small], 1)], 1, "adamw_replicated")[0]
    like = [given[n] for n in small]
    for j, r in enumerate(res):
        for n, a in zip(small, _unpack(r, like)):
            out.setdefault(n, [None] * 4)[j] = a
    total_loss = gsum[n_rows, 0]

    ex.ffn1_received((*last, res[1]))
    halves = _reduce_own(me, [ex.parts[n] for n in FFN1_GROUP], [ex.recv[n] for n in FFN1_GROUP],
                         jnp.zeros(DEP_SHAPE, F32), FOLD_STEPS, "reduce_ffn1")
    others = _sibling_swap(halves, "sibling_swap_ffn1")
    first = lax.axis_index("c") == 0
    grads = [jnp.concatenate([jnp.where(first, a, b), jnp.where(first, b, a)], axis=0) for a, b in zip(halves, others)]
    res = _adamw([_shard_2d(n, given[n]) for n in FFN1_GROUP], grads, None, [_shard_2d(n, given["m_" + n]) for n in FFN1_GROUP],
                 [_shard_2d(n, given["v_" + n]) for n in FFN1_GROUP], 8, "adamw_ffn1")
    for n, rs in zip(FFN1_GROUP, res):
        out[n] = [(r.T if n in TRANSPOSED else r).reshape(given[n].shape) for r in rs]
    return (total_loss, dx[None], *[out[n][0] for n in WEIGHTS], *[out[n][1] for n in WEIGHTS],
            *[out[n][2] for n in WEIGHTS], *[out[n][3] for n in WEIGHTS])
```

```python
import functools

import jax
import jax.numpy as jnp
from jax import lax
from jax.experimental import pallas as pl
from jax.experimental.pallas import tpu as pltpu

F32 = jnp.float32
BF16 = jnp.bfloat16
MESH = pl.DeviceIdType.MESH

RMS_EPS = 1e-6
GN_EPS = 64e-5
NEG_INF = -1e30
FFN_RESIDUAL = 0.5
HEAD_DIM = 64
ATT_BLOCK = 128
DILATIONS = (1, 4, 16)
SCAN_CHUNK = 64
TOKEN_TILE = 256
FFN_BWD_TILE = 512

ADAM_LR = 0.001
ADAM_B1 = 0.9
ADAM_B2 = 0.999
ADAM_EPS = 1e-08
ADAM_WD = 0.01
ADAM_STEP = 10

VMEM_FULL = pl.BlockSpec(memory_space=pltpu.VMEM)
ANY = pl.BlockSpec(memory_space=pl.ANY)


VMEM_LIMIT = 56 * 1024 * 1024


def _params(*sem):
    return pltpu.CompilerParams(dimension_semantics=sem, vmem_limit_bytes=VMEM_LIMIT)


def _dot(a, b, dims):
    return lax.dot_general(a.astype(BF16), b.astype(BF16), (dims, ((), ())), preferred_element_type=F32)


def _dot_nn(a, b):
    return _dot(a, b, ((1,), (0,)))


def _dot_nt(a, b):
    return _dot(a, b, ((1,), (1,)))


def _dot_tn(a, b):
    return _dot(a, b, ((0,), (0,)))


@jax.custom_vjp
def _mm(a, b):
    return _dot_nn(a, b)


def _mm_fwd(a, b):
    return _dot_nn(a, b), (a, b)


def _mm_bwd(res, g):
    a, b = res
    return _dot_nt(g, b).astype(a.dtype), _dot_tn(a, g).astype(b.dtype)


_mm.defvjp(_mm_fwd, _mm_bwd)


def _bdot(a, b, ca, cb):
    return lax.dot_general(a.astype(BF16), b.astype(BF16), (((ca,), (cb,)), ((0,), (0,))), preferred_element_type=F32)


@jax.custom_vjp
def _bmm_nt(a, b):
    return _bdot(a, b, 2, 2)


def _bmm_nt_fwd(a, b):
    return _bdot(a, b, 2, 2), (a, b)


def _bmm_nt_bwd(res, g):
    a, b = res
    return _bdot(g, b, 2, 1), _bdot(g, a, 1, 1)


_bmm_nt.defvjp(_bmm_nt_fwd, _bmm_nt_bwd)


@jax.custom_vjp
def _bmm_nn(a, b):
    return _bdot(a, b, 2, 1)


def _bmm_nn_fwd(a, b):
    return _bdot(a, b, 2, 1), (a, b)


def _bmm_nn_bwd(res, g):
    a, b = res
    return _bdot(g, b, 2, 2), _bdot(a, g, 1, 1)


_bmm_nn.defvjp(_bmm_nn_fwd, _bmm_nn_bwd)


@jax.custom_vjp
def _bmm_tn(a, b):
    return _bdot(a, b, 1, 1)


def _bmm_tn_fwd(a, b):
    return _bdot(a, b, 1, 1), (a, b)


def _bmm_tn_bwd(res, g):
    a, b = res
    return _bdot(b, g, 2, 2), _bdot(a, g, 2, 1)


_bmm_tn.defvjp(_bmm_tn_fwd, _bmm_tn_bwd)


def _hdot(a, b, ca, cb):
    return lax.dot_general(a, b, (((ca,), (cb,)), ((0,), (0,))), precision=lax.Precision.HIGH, preferred_element_type=F32)


def _sigmoid(x):
    return 1.0 / (1.0 + jnp.exp(-x))


def _rms(x):
    return lax.rsqrt(jnp.mean(x * x, axis=-1, keepdims=True) + RMS_EPS)


def _ffn_fwd(x, norm, wg, wu, wd, dep, name, target=None):
    t, d = x.shape
    nc, fc, _ = wg.shape
    tm = TOKEN_TILE

    def body(x_ref, n_ref, wg_ref, wu_ref, wd_ref, dep_ref, *rest):
        o_ref, g_ref, u_ref = rest[-3:] if target is None else rest[1:4]
        xv = x_ref[...]
        h = (xv * _rms(xv) * n_ref[...]).astype(BF16)
        acc = jnp.zeros((tm, d), F32)
        for c in range(nc):
            g = _dot_nt(h, wg_ref[c])
            u = _dot_nt(h, wu_ref[c])
            g_ref[c] = g.astype(BF16)
            u_ref[c] = u.astype(BF16)
            a = (g * _sigmoid(g) * u).astype(BF16)
            acc = acc + jnp.dot(a, wd_ref[c], preferred_element_type=F32)
        y = xv + FFN_RESIDUAL * acc
        if target is None:
            o_ref[...] = y
        else:
            t_ref, loss_ref = rest[0], rest[4]
            err = y - t_ref[...]
            o_ref[...] = err * (1.0 / d)
            part = 0.5 * jnp.sum(jnp.mean(err * err, axis=-1, keepdims=True), axis=0, keepdims=True)

            @pl.when(pl.program_id(0) == 0)
            def _():
                loss_ref[...] = jnp.zeros_like(loss_ref)

            loss_ref[...] += jnp.broadcast_to(part, loss_ref.shape)

    tile = pl.BlockSpec((tm, d), lambda i: (i, 0))
    hidden = pl.BlockSpec((nc, tm, fc), lambda i: (0, i, 0))
    hshape = jax.ShapeDtypeStruct((nc, t, fc), BF16)
    with_loss = target is not None
    return pl.pallas_call(
        body, name=name, grid=(t // tm,),
        out_shape=(jax.ShapeDtypeStruct((t, d), F32), hshape, hshape) + ((jax.ShapeDtypeStruct((1, 128), F32),) if with_loss else ()),
        in_specs=[tile, pl.BlockSpec((1, d), lambda i: (0, 0)), VMEM_FULL, VMEM_FULL, VMEM_FULL, ANY] + ([tile] if with_loss else []),
        out_specs=(tile, hidden, hidden) + ((pl.BlockSpec((1, 128), lambda i: (0, 0)),) if with_loss else ()),
        compiler_params=_params("arbitrary"),
    )(x, norm, wg, wu, wd, dep, *((target,) if with_loss else ()))


def _rmsnorm_bwd(xv, gain, dh):
    rs = _rms(xv)
    xn = xv * rs
    dxn = dh * gain
    dx = rs * (dxn - xn * jnp.mean(dxn * xn, axis=-1, keepdims=True))
    return dx, jnp.sum(dh * xn, axis=0, keepdims=True)


def _ffn_bwd(x, norm, wg, wu, wd, gate, up, dy, dep, name):
    t, d = x.shape
    nc, fc, _ = wg.shape
    tm = FFN_BWD_TILE
    nt = t // tm

    def body(x_ref, n_ref, wg_ref, wu_ref, wd_ref, g_ref, u_ref, dy_ref, dep_ref, dx_ref, dn_ref, dwg_ref, dwu_ref,
             dwd_ref, dh_ref, ag_ref, au_ref, ad_ref):
        c, i = pl.program_id(0), pl.program_id(1)
        rows = pl.ds(pl.multiple_of(i * tm, tm), tm)
        xv = x_ref[...]
        gain = n_ref[...]
        h = (xv * _rms(xv) * gain).astype(BF16)
        dy = dy_ref[...]
        dyb = (FFN_RESIDUAL * dy).astype(BF16)
        g = g_ref[0].astype(F32)
        u = u_ref[0].astype(F32)
        sg = _sigmoid(g)
        s = g * sg
        a = (s * u).astype(BF16)
        da = _dot_nt(dyb, wd_ref[0])
        dub = (da * s).astype(BF16)
        dgb = (da * u * (sg * (1.0 + g * (1.0 - sg)))).astype(BF16)
        dwd_c = _dot_tn(a, dyb)
        dwg_c = _dot_tn(dgb, h)
        dwu_c = _dot_tn(dub, h)
        dh_c = _dot_nn(dgb, wg_ref[0]) + _dot_nn(dub, wu_ref[0])

        @pl.when(i == 0)
        def _():
            ad_ref[...] = dwd_c
            ag_ref[...] = dwg_c
            au_ref[...] = dwu_c

        @pl.when(i > 0)
        def _():
            ad_ref[...] += dwd_c
            ag_ref[...] += dwg_c
            au_ref[...] += dwu_c

        @pl.when(i == nt - 1)
        def _():
            dwd_ref[0] = ad_ref[...].astype(BF16)
            dwg_ref[0] = ag_ref[...].astype(BF16)
            dwu_ref[0] = au_ref[...].astype(BF16)

        @pl.when(c == 0)
        def _():
            dh_ref[rows, :] = dh_c

        @pl.when(c > 0)
        def _():
            dh_ref[rows, :] += dh_c

        @pl.when(c == nc - 1)
        def _():
            dx, dn = _rmsnorm_bwd(xv, gain, dh_ref[rows, :])
            dx_ref[...] = dx + dy

            @pl.when(i == 0)
            def _():
                dn_ref[...] = dn

            @pl.when(i > 0)
            def _():
                dn_ref[...] += dn

    tile = pl.BlockSpec((tm, d), lambda c, i: (i, 0))
    row = pl.BlockSpec((1, d), lambda c, i: (0, 0))
    wrow = pl.BlockSpec((1, fc, d), lambda c, i: (c, 0, 0), pipeline_mode=pl.Buffered(1))
    hidden = pl.BlockSpec((1, tm, fc), lambda c, i: (c, i, 0))
    last = pl.BlockSpec((tm, d), lambda c, i: (jnp.where(c == nc - 1, i, 0), 0))
    return pl.pallas_call(
        body, name=name, grid=(nc, nt),
        out_shape=(jax.ShapeDtypeStruct((t, d), F32), jax.ShapeDtypeStruct((1, d), F32),
                   jax.ShapeDtypeStruct(wg.shape, BF16), jax.ShapeDtypeStruct(wu.shape, BF16),
                   jax.ShapeDtypeStruct(wd.shape, BF16)),
        in_specs=[tile, row, wrow, wrow, wrow, hidden, hidden, tile, ANY],
        out_specs=(last, row, wrow, wrow, wrow),
        scratch_shapes=[pltpu.VMEM((t, d), F32)] + [pltpu.VMEM((fc, d), F32)] * 3,
        compiler_params=_params("arbitrary", "arbitrary"),
    )(x, norm, wg, wu, wd, gate, up, dy, dep)


def _store_heads(ref, v):
    for h in range(ref.shape[0]):
        ref[h] = v[:, h * HEAD_DIM:(h + 1) * HEAD_DIM]


def _load_heads(ref):
    return jnp.concatenate([ref[h] for h in range(ref.shape[0])], axis=-1)


N_HEAD_GROUPS = 3


def _proj_fwd(x, norm, w, c):
    t, d = x.shape
    nc, _, ncol = w.shape
    nh = c // HEAD_DIM
    tm = TOKEN_TILE
    wide = nc * ncol - N_HEAD_GROUPS * c

    def body(x_ref, n_ref, w_ref, q_ref, k_ref, v_ref, cur_ref):
        xv = x_ref[...]
        h = (xv * _rms(xv) * n_ref[...]).astype(BF16)
        full = jnp.concatenate([jnp.dot(h, w_ref[s], preferred_element_type=F32) for s in range(nc)], axis=1)
        for m, ref in enumerate((q_ref, k_ref, v_ref)):
            _store_heads(ref, full[:, m * c:(m + 1) * c])
        cur_ref[...] = full[:, N_HEAD_GROUPS * c:]

    heads = pl.BlockSpec((nh, tm, HEAD_DIM), lambda i: (0, i, 0))
    hshape = jax.ShapeDtypeStruct((nh, t, HEAD_DIM), F32)
    return pl.pallas_call(
        body, name="proj_fwd", grid=(t // tm,),
        out_shape=(hshape, hshape, hshape, jax.ShapeDtypeStruct((t, wide), F32)),
        in_specs=[pl.BlockSpec((tm, d), lambda i: (i, 0)), pl.BlockSpec((1, d), lambda i: (0, 0)), VMEM_FULL],
        out_specs=(heads, heads, heads, pl.BlockSpec((tm, wide), lambda i: (i, 0))),
        compiler_params=_params("arbitrary"),
    )(x, norm, w)


def _proj_bwd(x, norm, w, dq, dk, dv, dcur, dres):
    t, d = x.shape
    nc, _, ncol = w.shape
    nh = dq.shape[0]
    tm = TOKEN_TILE
    nt = t // tm
    wide = dcur.shape[1]

    def body(x_ref, n_ref, w_ref, dq_ref, dk_ref, dv_ref, dcur_ref, dres_ref, dx_ref, dn_ref, dw_ref, acc_ref):
        i = pl.program_id(0)

        @pl.when(i == 0)
        def _():
            acc_ref[...] = jnp.zeros_like(acc_ref)
            dn_ref[...] = jnp.zeros_like(dn_ref)

        xv = x_ref[...]
        gain = n_ref[...]
        h = (xv * _rms(xv) * gain).astype(BF16)
        dp = jnp.concatenate([_load_heads(dq_ref), _load_heads(dk_ref), _load_heads(dv_ref), dcur_ref[...]], axis=1).astype(BF16)
        dh = jnp.zeros((tm, d), F32)
        for s in range(nc):
            dps = dp[:, s * ncol:(s + 1) * ncol]
            acc_ref[s] += _dot_tn(h, dps)
            dh = dh + _dot_nt(dps, w_ref[s])
        dx, dn = _rmsnorm_bwd(xv, gain, dh)
        dx_ref[...] = dx + dres_ref[...]
        dn_ref[...] += dn

        @pl.when(i == nt - 1)
        def _():
            dw_ref[...] = acc_ref[...].astype(BF16)

    tile = pl.BlockSpec((tm, d), lambda i: (i, 0))
    row = pl.BlockSpec((1, d), lambda i: (0, 0))
    heads = pl.BlockSpec((nh, tm, HEAD_DIM), lambda i: (0, i, 0))
    return pl.pallas_call(
        body, name="proj_bwd", grid=(nt,),
        out_shape=(jax.ShapeDtypeStruct((t, d), F32), jax.ShapeDtypeStruct((1, d), F32),
                   jax.ShapeDtypeStruct(w.shape, BF16)),
        in_specs=[tile, row, VMEM_FULL, heads, heads, heads, pl.BlockSpec((tm, wide), lambda i: (i, 0)), tile],
        out_specs=(tile, row, VMEM_FULL),
        scratch_shapes=[pltpu.VMEM(w.shape, F32)], compiler_params=_params("arbitrary"),
    )(x, norm, w, dq, dk, dv, dcur, dres)


def _mixout_fwd(x, att, opg, gate, w):
    t, d = x.shape
    nh = att.shape[0]
    half = gate.shape[1]
    tm = TOKEN_TILE

    def body(x_ref, att_ref, opg_ref, g_ref, w_ref, o_ref):
        mix = jnp.concatenate([_load_heads(att_ref), _load_heads(opg_ref) * g_ref[...]], axis=-1).astype(BF16)
        o_ref[...] = x_ref[...] + jnp.dot(mix, w_ref[...], preferred_element_type=F32)

    tile = pl.BlockSpec((tm, d), lambda i: (i, 0))
    htile = pl.BlockSpec((tm, half), lambda i: (i, 0))
    heads = pl.BlockSpec((nh, tm, HEAD_DIM), lambda i: (0, i, 0))
    return pl.pallas_call(
        body, name="mixout_fwd", grid=(t // tm,), out_shape=jax.ShapeDtypeStruct((t, d), F32),
        in_specs=[tile, heads, heads, htile, VMEM_FULL], out_specs=tile, compiler_params=_params("arbitrary"),
    )(x, att, opg, gate, w)


def _mixout_bwd(att, opg, gate, w, dy, dep):
    nh, t, _ = att.shape
    half = gate.shape[1]
    d = dy.shape[1]
    tm = TOKEN_TILE

    def body(att_ref, opg_ref, g_ref, w_ref, dy_ref, dep_ref, datt_ref, dopg_ref, dg_ref, dw_ref):
        i = pl.program_id(0)
        opg_v, g_v = _load_heads(opg_ref), g_ref[...]
        mix = jnp.concatenate([_load_heads(att_ref), opg_v * g_v], axis=-1).astype(BF16)
        dyb = dy_ref[...].astype(BF16)
        dmix = _dot_nt(dyb, w_ref[...])
        dw = _dot_tn(mix, dyb)
        _store_heads(datt_ref, dmix[:, :half])
        drw = dmix[:, half:]
        _store_heads(dopg_ref, drw * g_v)
        dg_ref[...] = drw * opg_v

        @pl.when(i == 0)
        def _():
            dw_ref[...] = dw

        @pl.when(i > 0)
        def _():
            dw_ref[...] += dw

    tile = pl.BlockSpec((tm, d), lambda i: (i, 0))
    htile = pl.BlockSpec((tm, half), lambda i: (i, 0))
    heads = pl.BlockSpec((nh, tm, HEAD_DIM), lambda i: (0, i, 0))
    hshape = jax.ShapeDtypeStruct((nh, t, HEAD_DIM), F32)
    return pl.pallas_call(
        body, name="mixout_bwd", grid=(t // tm,),
        out_shape=(hshape, hshape, jax.ShapeDtypeStruct((t, half), F32), jax.ShapeDtypeStruct(w.shape, F32)),
        in_specs=[heads, heads, htile, VMEM_FULL, tile, ANY],
        out_specs=(heads, heads, htile, pl.BlockSpec(w.shape, lambda i: (0, 0))),
        compiler_params=_params("arbitrary"),
    )(att, opg, gate, w, dy, dep)


def _head_norm(x, gain):
    return x * _rms(x) * gain


def _att_pattern(qh, kh, v, nb):
    g, blk, _ = qh.shape
    scale = HEAD_DIM ** -0.5
    qi = lax.broadcasted_iota(jnp.int32, (blk, blk), 0)
    kj = lax.broadcasted_iota(jnp.int32, (blk, blk), 1)
    sc = jnp.where(kj <= qi, _bmm_nt(qh, kh) * scale, NEG_INF)
    top = jnp.max(sc, axis=-1, keepdims=True)
    if nb > 1:
        khp = jnp.concatenate([kh[:1], kh[:-1]], axis=0)
        vp = jnp.concatenate([v[:1], v[:-1]], axis=0)
        has_prev = lax.broadcasted_iota(jnp.int32, (g, 1, 1), 0) % nb != 0
        sp = jnp.where((kj >= qi) & has_prev, _bmm_nt(qh, khp) * scale, NEG_INF)
        top = jnp.maximum(top, jnp.max(sp, axis=-1, keepdims=True))
    m = lax.stop_gradient(top)
    pc = jnp.exp(sc - m)
    den = jnp.sum(pc, axis=-1, keepdims=True)
    acc = _bmm_nn(pc, v)
    if nb > 1:
        pp = jnp.exp(sp - m)
        den = den + jnp.sum(pp, axis=-1, keepdims=True)
        acc = acc + _bmm_nn(pp, vp)
    o = acc / den
    return o, jnp.broadcast_to(m + jnp.log(den), o.shape)


def _pattern_rows(t, dil):
    nb = t // (ATT_BLOCK * dil)
    starts = [n * ATT_BLOCK * dil + r for r in range(dil) for n in range(nb)]
    return [pl.ds(s, ATT_BLOCK, stride=dil) if dil > 1 else pl.ds(s, ATT_BLOCK) for s in starts], nb


def _take(ref, rows):
    return jnp.stack([ref[0, r, :] for r in rows])


def _put(ref, rows, val):
    for g, r in enumerate(rows):
        ref[0, r, :] = val[g]


def _put_add(ref, rows, val):
    for g, r in enumerate(rows):
        ref[0, r, :] += val[g]


def _merge_fn(o1, o2, o3, l1, l2, l3):
    m = lax.stop_gradient(jnp.maximum(jnp.maximum(l1, l2), l3))
    e1, e2, e3 = jnp.exp(l1 - m), jnp.exp(l2 - m), jnp.exp(l3 - m)
    return (e1 * o1 + e2 * o2 + e3 * o3) / (e1 + e2 + e3)


def _token_rows(j):
    return pl.ds(pl.multiple_of(j * ATT_BLOCK, ATT_BLOCK), ATT_BLOCK)


def _norm_rows(t, q_ref, k_ref, gq, gk, qh_ref, kh_ref):
    def step(j, carry):
        rows = _token_rows(j)
        qh_ref[0, rows, :] = _head_norm(q_ref[0, rows, :], gq[0])
        kh_ref[0, rows, :] = _head_norm(k_ref[0, rows, :], gk[0])
        return carry

    lax.fori_loop(0, t // ATT_BLOCK, step, 0)


def _att_head_specs(t):
    head = pl.BlockSpec((1, t, HEAD_DIM), lambda h: (h, 0, 0))
    gain = pl.BlockSpec((1, 1, HEAD_DIM), lambda h: (0, 0, 0))
    return head, gain


def _att_fwd(q, k, v, qn, kn):
    nh, t, dh = q.shape
    head, gain = _att_head_specs(t)

    def body(q_ref, k_ref, v_ref, qn_ref, kn_ref, att_ref, o1, o2, o3, l1, l2, l3, qh_ref, kh_ref):
        saved = (o1, o2, o3, l1, l2, l3)
        _norm_rows(t, q_ref, k_ref, qn_ref[...], kn_ref[...], qh_ref, kh_ref)
        for p, dil in enumerate(DILATIONS):
            rows, nb = _pattern_rows(t, dil)
            o, lse = _att_pattern(_take(qh_ref, rows), _take(kh_ref, rows), _take(v_ref, rows), nb)
            _put(saved[p], rows, o)
            _put(saved[3 + p], rows, lse)

        def merge(j, carry):
            rows = _token_rows(j)
            att_ref[0, rows, :] = _merge_fn(*[r[0, rows, :] for r in saved])
            return carry

        lax.fori_loop(0, t // ATT_BLOCK, merge, 0)

    return pl.pallas_call(
        body, name="att_fwd", grid=(nh,), out_shape=(jax.ShapeDtypeStruct(q.shape, F32),) * 7,
        in_specs=[head, head, head, gain, gain], out_specs=(head,) * 7,
        scratch_shapes=[pltpu.VMEM((1, t, dh), F32)] * 2, compiler_params=_params("arbitrary"),
    )(q, k, v, qn, kn)


def _att_bwd(q, k, v, qn, kn, saved, datt):
    nh, t, dh = q.shape
    head, gain = _att_head_specs(t)

    def body(q_ref, k_ref, v_ref, qn_ref, kn_ref, o1, o2, o3, l1, l2, l3, datt_ref,
             dq_ref, dk_ref, dv_ref, dqn_ref, dkn_ref, qh_ref, kh_ref, dqh_ref, dkh_ref, *ct_refs):
        for ref in (dqh_ref, dkh_ref, dv_ref):
            ref[...] = jnp.zeros_like(ref)

        @pl.when(pl.program_id(0) == 0)
        def _():
            dqn_ref[...] = jnp.zeros_like(dqn_ref)
            dkn_ref[...] = jnp.zeros_like(dkn_ref)

        gq, gk = qn_ref[...], kn_ref[...]
        _norm_rows(t, q_ref, k_ref, gq, gk, qh_ref, kh_ref)

        def merge_cotangents(j, carry):
            rows = _token_rows(j)
            _, merge_vjp = jax.vjp(_merge_fn, *[r[0, rows, :] for r in (o1, o2, o3, l1, l2, l3)])
            for ref, val in zip(ct_refs, merge_vjp(datt_ref[0, rows, :])):
                ref[0, rows, :] = val
            return carry

        lax.fori_loop(0, t // ATT_BLOCK, merge_cotangents, 0)

        for p, dil in enumerate(DILATIONS):
            rows, nb = _pattern_rows(t, dil)
            _, pattern_vjp = jax.vjp(functools.partial(_att_pattern, nb=nb), _take(qh_ref, rows), _take(kh_ref, rows),
                                     _take(v_ref, rows))
            dqh, dkh, dv = pattern_vjp((_take(ct_refs[p], rows), _take(ct_refs[3 + p], rows)))
            _put_add(dqh_ref, rows, dqh)
            _put_add(dkh_ref, rows, dkh)
            _put_add(dv_ref, rows, dv)

        def norm_cotangents(j, carry):
            rows = _token_rows(j)
            out = []
            for x_ref, gain, dh_ref, dx_ref, acc in ((q_ref, gq, dqh_ref, dq_ref, carry[0]), (k_ref, gk, dkh_ref, dk_ref, carry[1])):
                _, norm_vjp = jax.vjp(_head_norm, x_ref[0, rows, :], gain[0])
                dx, dgain = norm_vjp(dh_ref[0, rows, :])
                dx_ref[0, rows, :] = dx
                out.append(acc + dgain)
            return tuple(out)

        zero = jnp.zeros((1, dh), F32)
        dgq, dgk = lax.fori_loop(0, t // ATT_BLOCK, norm_cotangents, (zero, zero))
        dqn_ref[0] += dgq
        dkn_ref[0] += dgk

    hshape = jax.ShapeDtypeStruct(q.shape, F32)
    gshape = jax.ShapeDtypeStruct((1, 1, dh), F32)
    return pl.pallas_call(
        body, name="att_bwd", grid=(nh,), out_shape=(hshape, hshape, hshape, gshape, gshape),
        in_specs=[head, head, head, gain, gain] + [head] * 7, out_specs=(head, head, head, gain, gain),
        scratch_shapes=[pltpu.VMEM((1, t, dh), F32)] * 10, compiler_params=_params("arbitrary"),
    )(q, k, v, qn, kn, *saved, datt)


RWKV_VEC = ("mu_r", "mu_k", "mu_v", "mu_w", "mu_a", "mu_g", "w0", "a0", "k_k", "k_a")
RWKV_MAT = ("w1", "w2", "a1", "a2", "g1", "g2")


def _rwkv_pre_fn(cur, prev, vec, w1, w2, a1, a2, g1, g2):
    c = cur.shape[1] // 4
    mu_r, mu_k, mu_v, mu_w, mu_a, mu_g, w0, a0, k_k, k_a = (vec[j:j + 1] for j in range(10))

    def lerp(j, mu):
        xc, xp = cur[:, j * c:(j + 1) * c], prev[:, j * c:(j + 1) * c]
        return xc + (xp - xc) * mu

    r, k, v = lerp(0, mu_r), lerp(1, mu_k), lerp(2, mu_v)
    cw, ca, cg = lerp(3, mu_w), lerp(3, mu_a), lerp(3, mu_g)
    z = w0 + _mm(jnp.tanh(_mm(cw, w1)), w2)
    w_log = jnp.minimum(z, 0.0) - jnp.log(1.0 + jnp.exp(-jnp.abs(z))) - 0.5
    lw = -jnp.exp(w_log)
    a = _sigmoid(a0 + _mm(_mm(ca, a1), a2))
    gate = _mm(_sigmoid(_mm(cg, g1)), g2)
    kkraw = k * k_k
    kmod = k * (1.0 + (a - 1.0) * k_a)
    return r, lw, kmod, v, kkraw, a, gate


HALO_ROWS = 8


def _rwkv_pre_specs(c, mats, tile_of):
    tm = TOKEN_TILE
    nh = c // HEAD_DIM
    wide = pl.BlockSpec((tm, 4 * c), lambda j: (tile_of(j), 0))
    halo = pl.BlockSpec((HALO_ROWS, 4 * c), lambda j: (jnp.maximum(tile_of(j) * (tm // HALO_ROWS) - 1, 0), 0))
    one = pl.BlockSpec((tm, c), lambda j: (tile_of(j), 0))
    heads = pl.BlockSpec((nh, tm, HEAD_DIM), lambda j: (0, tile_of(j), 0))
    vec = pl.BlockSpec((10, c), lambda j: (0, 0))
    mspecs = [pl.BlockSpec(m.shape, lambda j: (0, 0)) for m in mats]
    return wide, halo, one, heads, vec, mspecs


def _previous_rows(cur, halo, tile):
    first = jnp.where(tile > 0, halo[HALO_ROWS - 1:HALO_ROWS], 0.0)
    rows = lax.broadcasted_iota(jnp.int32, cur.shape, 0)
    return jnp.where(rows == 0, first, pltpu.roll(cur, 1, axis=0))


def _rwkv_pre_fwd(cur, vec, mats):
    t, c4 = cur.shape
    c = c4 // 4
    wide, halo, one, heads, vspec, mspecs = _rwkv_pre_specs(c, mats, lambda j: j)

    def body(cur_ref, halo_ref, vec_ref, *rest):
        mrefs, outs = rest[:6], rest[6:]
        cur_v = cur_ref[...]
        prev = _previous_rows(cur_v, halo_ref[...], pl.program_id(0))
        vals = _rwkv_pre_fn(cur_v, prev, vec_ref[...], *(m[...] for m in mrefs))
        for ref, val in zip(outs[:6], vals[:6]):
            _store_heads(ref, val)
        outs[6][...] = vals[6]

    hshape = jax.ShapeDtypeStruct((c // HEAD_DIM, t, HEAD_DIM), F32)
    return pl.pallas_call(
        body, name="rwkv_pre_fwd", grid=(t // TOKEN_TILE,), out_shape=(hshape,) * 6 + (jax.ShapeDtypeStruct((t, c), F32),),
        in_specs=[wide, halo, vspec] + mspecs, out_specs=(heads,) * 6 + (one,), compiler_params=_params("arbitrary"),
    )(cur, cur, vec, *mats)


def _rwkv_pre_bwd(cur, vec, mats, cts, dgate):
    t, c4 = cur.shape
    c = c4 // 4
    tm = TOKEN_TILE
    nt = t // tm
    wide, halo, one, heads, vspec, mspecs = _rwkv_pre_specs(c, mats, lambda j: nt - 1 - j)

    def body(cur_ref, halo_ref, vec_ref, *rest):
        mrefs, ctrefs, dgate_ref, outs, carry_ref = rest[:6], rest[6:12], rest[12], rest[13:-1], rest[-1]
        j = pl.program_id(0)

        @pl.when(j == 0)
        def _():
            carry_ref[...] = jnp.zeros_like(carry_ref)
            for ref in outs[1:]:
                ref[...] = jnp.zeros_like(ref)

        cur_v = cur_ref[...]
        prev = _previous_rows(cur_v, halo_ref[...], nt - 1 - j)
        _, vjp = jax.vjp(_rwkv_pre_fn, cur_v, prev, vec_ref[...], *(m[...] for m in mrefs))
        grads = vjp(tuple(_load_heads(r) for r in ctrefs) + (dgate_ref[...],))
        dprev = grads[1]
        rows = lax.broadcasted_iota(jnp.int32, dprev.shape, 0)
        outs[0][...] = grads[0] + jnp.where(rows == tm - 1, carry_ref[0:1], pltpu.roll(dprev, tm - 1, axis=0))
        carry_ref[0:1] = dprev[0:1]
        for ref, val in zip(outs[1:], grads[2:]):
            ref[...] += val

    return pl.pallas_call(
        body, name="rwkv_pre_bwd", grid=(nt,),
        out_shape=(jax.ShapeDtypeStruct(cur.shape, F32), jax.ShapeDtypeStruct(vec.shape, F32))
        + tuple(jax.ShapeDtypeStruct(m.shape, F32) for m in mats),
        in_specs=[wide, halo, vspec] + mspecs + [heads] * 6 + [one], out_specs=(wide, vspec) + tuple(mspecs),
        scratch_shapes=[pltpu.VMEM((HALO_ROWS, c4), F32)], compiler_params=_params("arbitrary"),
    )(cur, cur, vec, *mats, *cts, dgate)


def _scan_chunk_fn(h0, r, lw, k, v, kkraw, a, rk, lnw, lnb):
    n = r.shape[1]
    spread = jnp.ones((r.shape[0], r.shape[2], r.shape[2]), F32)

    def lane_sum(x):
        return _hdot(x, spread, 2, 1)

    nrm = jnp.sqrt(lane_sum(kkraw * kkraw))
    kk = kkraw / jnp.maximum(nrm, 1e-12)
    av, bv = -kk, kk * a
    ti = lax.broadcasted_iota(jnp.int32, (n, n), 0)
    si = lax.broadcasted_iota(jnp.int32, (n, n), 1)
    incl, strict = ti >= si, ti > si
    ones = jnp.broadcast_to(incl.astype(F32)[None], (r.shape[0], n, n))
    cum = _hdot(ones, lw, 2, 1)
    at, rt = av * jnp.exp(cum - lw), r * jnp.exp(cum)
    inv = jnp.exp(-cum)
    bt, kt = bv * inv, k * inv
    gram = _hdot(jnp.concatenate([at, rt], axis=1), jnp.concatenate([bt, kt], axis=1), 2, 2)
    lab = jnp.where(strict, gram[:, :n, :n], 0.0)
    lak = jnp.where(strict, gram[:, :n, n:], 0.0)
    rb = jnp.where(incl, gram[:, n:, :n], 0.0)
    rkm = jnp.where(incl, gram[:, n:, n:], 0.0)
    nv = v.shape[2]
    u = _bmm_nn(jnp.concatenate([at, lak], axis=2), jnp.concatenate([h0, v], axis=1))
    p = lab
    m = 2
    while m < n:
        both = _bmm_nn(p, jnp.concatenate([u, p], axis=2))
        u, p = u + both[:, :, :nv], both[:, :, nv:]
        m *= 2
    u = u + _bmm_nn(p, u)
    y = _bmm_nn(jnp.concatenate([rt, rb, rkm], axis=2), jnp.concatenate([h0, u, v], axis=1))
    last = jnp.exp(jnp.sum(lw, axis=1, keepdims=True))
    h1 = jnp.swapaxes(last, 1, 2) * (h0 + _bmm_tn(jnp.concatenate([bt, kt], axis=1), jnp.concatenate([u, v], axis=1)))
    width = 1.0 / y.shape[2]
    yc = y - lane_sum(y) * width
    var = lane_sum(yc * yc) * width
    yn = yc * lax.rsqrt(var + GN_EPS) * lnw + lnb
    bonus = lane_sum(r * k * rk) * v
    return yn + bonus, h1


SCAN_GROUP = 2


def _scan_group_fn(h0, r, lw, k, v, kkraw, a, rk, lnw, lnb):
    outs = []
    for j in range(SCAN_GROUP):
        rows = slice(j * SCAN_CHUNK, (j + 1) * SCAN_CHUNK)
        o, h0 = _scan_chunk_fn(h0, r[:, rows], lw[:, rows], k[:, rows], v[:, rows], kkraw[:, rows], a[:, rows], rk, lnw, lnb)
        outs.append(o)
    return jnp.concatenate(outs, axis=1), h0


def _scan_specs(h, t, dh, rev):
    n = SCAN_CHUNK * SCAN_GROUP
    nc = t // n
    pos = (lambda c: (0, nc - 1 - c, 0)) if rev else (lambda c: (0, c, 0))
    st = (lambda c: (nc - 1 - c, 0, 0, 0)) if rev else (lambda c: (c, 0, 0, 0))
    seq = pl.BlockSpec((h, n, dh), pos)
    par = pl.BlockSpec((h, 1, dh), lambda c: (0, 0, 0))
    state = pl.BlockSpec((1, h, dh, dh), st)
    return seq, par, state


def _scan_fwd(seqs, pars):
    h, t, dh = seqs[0].shape
    nc = t // (SCAN_CHUNK * SCAN_GROUP)
    seq, par, state = _scan_specs(h, t, dh, False)

    def body(r, lw, k, v, kkraw, a, rk, lnw, lnb, o_ref, st_ref, h_ref):
        @pl.when(pl.program_id(0) == 0)
        def _():
            h_ref[...] = jnp.zeros_like(h_ref)

        h0 = h_ref[...]
        st_ref[0] = h0
        o, h1 = _scan_group_fn(h0, r[...], lw[...], k[...], v[...], kkraw[...], a[...], rk[...], lnw[...], lnb[...])
        o_ref[...] = o
        h_ref[...] = h1

    return pl.pallas_call(
        body, name="rwkv_scan_fwd", grid=(nc,),
        out_shape=(jax.ShapeDtypeStruct((h, t, dh), F32), jax.ShapeDtypeStruct((nc, h, dh, dh), F32)),
        in_specs=[seq] * 6 + [par] * 3, out_specs=(seq, state),
        scratch_shapes=[pltpu.VMEM((h, dh, dh), F32)], compiler_params=_params("arbitrary"),
    )(*seqs, *pars)


def _scan_bwd(seqs, pars, states, do):
    h, t, dh = seqs[0].shape
    nc = t // (SCAN_CHUNK * SCAN_GROUP)
    seq, par, state = _scan_specs(h, t, dh, True)

    def body(r, lw, k, v, kkraw, a, rk, lnw, lnb, st_ref, do_ref, *rest):
        douts, dpars, dh_ref = rest[:6], rest[6:9], rest[9]
        first = pl.program_id(0) == 0

        @pl.when(first)
        def _():
            dh_ref[...] = jnp.zeros_like(dh_ref)

        _, vjp = jax.vjp(_scan_group_fn, st_ref[0], r[...], lw[...], k[...], v[...], kkraw[...], a[...],
                         rk[...], lnw[...], lnb[...])
        grads = vjp((do_ref[...], dh_ref[...]))
        dh_ref[...] = grads[0]
        for ref, val in zip(douts, grads[1:7]):
            ref[...] = val

        @pl.when(first)
        def _():
            for ref, val in zip(dpars, grads[7:]):
                ref[...] = val

        @pl.when(jnp.logical_not(first))
        def _():
            for ref, val in zip(dpars, grads[7:]):
                ref[...] += val

    sshape = jax.ShapeDtypeStruct((h, t, dh), F32)
    pshape = jax.ShapeDtypeStruct((h, 1, dh), F32)
    return pl.pallas_call(
        body, name="rwkv_scan_bwd", grid=(nc,), out_shape=(sshape,) * 6 + (pshape,) * 3,
        in_specs=[seq] * 6 + [par] * 3 + [state, seq], out_specs=(seq,) * 6 + (par,) * 3,
        scratch_shapes=[pltpu.VMEM((h, dh, dh), F32)], compiler_params=_params("arbitrary"),
    )(*seqs, *pars, states, do)


def _local_step(x, target, w, ex):
    w = dict(w)
    c = w["mu_r"].shape[-1]
    qn, kn = w["q_norm"].reshape(1, 1, HEAD_DIM), w["k_norm"].reshape(1, 1, HEAD_DIM)
    vec = jnp.concatenate([w[n].reshape(1, c) for n in RWKV_VEC], axis=0)
    pars = [w[n].reshape(-1, 1, HEAD_DIM) for n in ("r_k", "ln_x_w", "ln_x_b")]
    no_dep = jnp.zeros(DEP_SHAPE, F32)

    x1, gate1, up1 = _ffn_fwd(x, w["ffn1_norm"], w["ffn1_w_gate"], w["ffn1_w_up"], w["ffn1_w_down"], ex.first_dep, "ffn1_fwd")
    w.update(ex.mix_weights((x1,)))
    mats = [w[n] for n in RWKV_MAT]
    q, k, v, cur = _proj_fwd(x1, w["mix_norm"], w["w_in"], c)
    att, *saved = _att_fwd(q, k, v, qn, kn)
    pre = _rwkv_pre_fwd(cur, vec, mats)
    seqs, gate = pre[:6], pre[6]
    opg, states = _scan_fwd(seqs, pars)
    w.update(ex.out_weights((att, opg)))
    x2 = _mixout_fwd(x1, att, opg, gate, w["w_out"])
    dy, gate2, up2, loss = _ffn_fwd(x2, w["ffn2_norm"], w["ffn2_w_gate"], w["ffn2_w_up"], w["ffn2_w_down"], no_dep, "ffn2_fwd",
                                    target=target)

    g = {}
    dx2, g["ffn2_norm"], g["ffn2_w_gate"], g["ffn2_w_up"], g["ffn2_w_down"] = _ffn_bwd(
        x2, w["ffn2_norm"], w["ffn2_w_gate"], w["ffn2_w_up"], w["ffn2_w_down"], gate2, up2, dy, no_dep, "ffn2_bwd")
    dep = ex.send_ffn2({n: g[n] for n in ("ffn2_w_gate", "ffn2_w_up", "ffn2_w_down")})
    datt, dopg, dgate, g["w_out"] = _mixout_bwd(att, opg, gate, w["w_out"], dx2, dep)
    dscan = _scan_bwd(seqs, pars, states, dopg)
    for n, d in zip(("r_k", "ln_x_w", "ln_x_b"), dscan[6:]):
        g[n] = d
    dcur, dvec, *dmats = _rwkv_pre_bwd(cur, vec, mats, dscan[:6], dgate)
    for n, d in zip(RWKV_MAT, dmats):
        g[n] = d
    for j, n in enumerate(RWKV_VEC):
        g[n] = dvec[j:j + 1]
    dq, dk, dv, g["q_norm"], g["k_norm"] = _att_bwd(q, k, v, qn, kn, saved, datt)
    dx1, g["mix_norm"], g["w_in"] = _proj_bwd(x1, w["mix_norm"], w["w_in"], dq, dk, dv, dcur, dx2)
    dep = ex.send_mix({n: g[n] for n in ("w_in", "w_out") + RWKV_MAT}, (dx1,))
    dx, g["ffn1_norm"], g["ffn1_w_gate"], g["ffn1_w_up"], g["ffn1_w_down"] = _ffn_bwd(
        x, w["ffn1_norm"], w["ffn1_w_gate"], w["ffn1_w_up"], w["ffn1_w_down"], gate1, up1, dx1, dep, "ffn1_bwd")
    return loss, dx, g


N_SHARDS = 4


def _place():
    return lax.axis_index("x"), lax.axis_index("y"), lax.axis_index("c")


def _chip_peers(x, y):
    return [(1 - x, y), (x, 1 - y), (1 - x, 1 - y)]


HBM = pl.BlockSpec(memory_space=pltpu.HBM)
SEM = pl.BlockSpec(memory_space=pltpu.SEMAPHORE)
DEP_SHAPE = (8, 128)


class _Views:
    to_sibling = False


class _GatherViews(_Views):
    @staticmethod
    def send(i, srcs, lands, k, at):
        return srcs[i], lands[i].at[at[3]]

    @staticmethod
    def landing(i, srcs, lands, k, at):
        return srcs[i], lands[i].at[2 * at[4] + at[5]]


class _ScatterViews(_Views):
    @staticmethod
    def send(i, srcs, lands, k, at):
        return srcs[i].at[2 * at[4] + at[5]], lands[i].at[k]

    @staticmethod
    def landing(i, srcs, lands, k, at):
        return srcs[i].at[at[3]], lands[i].at[k]


def _half_rows(ref, slot, half):
    rows = ref.shape[1] // 2
    return ref.at[slot, pl.ds(pl.multiple_of(half * rows, BF16_SUBLANES), rows)]


class _HalfGatherViews(_Views):
    @staticmethod
    def send(i, srcs, lands, k, at):
        rows = srcs[i].shape[0] // 2
        return srcs[i].at[pl.ds(pl.multiple_of(at[2] * rows, BF16_SUBLANES), rows)], _half_rows(lands[i], at[3], at[2])

    @staticmethod
    def landing(i, srcs, lands, k, at):
        rows = srcs[i].shape[0] // 2
        return srcs[i].at[pl.ds(pl.multiple_of(at[2] * rows, BF16_SUBLANES), rows)], _half_rows(lands[i], 2 * at[4] + at[5], at[2])


class _ForwardViews(_Views):
    to_sibling = True

    @staticmethod
    def send(i, srcs, lands, k, at):
        mine = _half_rows(lands[i], 2 * at[4] + at[5], at[2])
        return mine, mine

    @staticmethod
    def landing(i, srcs, lands, k, at):
        theirs = _half_rows(lands[i], 2 * at[4] + at[5], 1 - at[2])
        return theirs, theirs


def _push_start(srcs, lands, views, after, name):
    ns, nl = len(srcs), len(lands)

    def body(*refs):
        src_refs, land_refs = refs[:ns], refs[ns:ns + nl]
        send_sems, recv_sems = refs[ns + nl + 1:ns + nl + 3]
        token = refs[2 * (ns + nl) + 3]
        x, y, c = _place()
        for i in range(nl):
            for k, (px, py) in enumerate(_chip_peers(x, y)):
                src, dst = views.send(i, src_refs, land_refs, k, (x, y, c, 2 * x + y, px, py))
                pltpu.make_async_remote_copy(
                    src_ref=src, dst_ref=dst, send_sem=send_sems.at[3 * i + k], recv_sem=recv_sems.at[3 * i + k],
                    device_id=(x, y, 1 - c) if views.to_sibling else (px, py, c), device_id_type=MESH).start()
        token[...] = jnp.zeros_like(token)

    sems = pltpu.SemaphoreType.DMA((3 * nl,))
    both = [pltpu.with_memory_space_constraint(a, pltpu.HBM) for a in (*srcs, *lands)]
    outs = pl.pallas_call(
        body, name=name,
        out_shape=(sems, sems, *[pltpu.HBM(a.shape, a.dtype) for a in both], jax.ShapeDtypeStruct(DEP_SHAPE, F32)),
        in_specs=[HBM] * (ns + nl) + [ANY], out_specs=(SEM, SEM, *[HBM] * (ns + nl), VMEM_FULL),
        input_output_aliases={i: 2 + i for i in range(ns + nl)},
        compiler_params=pltpu.CompilerParams(has_side_effects=pltpu.SideEffectType.DATAFLOW_SIDE_EFFECTING),
    )(*both, after)
    return outs[0], outs[1], outs[2:2 + ns], outs[2 + ns:2 + ns + nl], outs[2 + ns + nl]


def _push_wait(started, views, after, name, with_sources=False):
    send_sems, recv_sems, srcs, lands, _ = started
    ns, nl = len(srcs), len(lands)

    def body(*refs):
        src_refs, land_refs = refs[:ns], refs[ns:ns + nl]
        send_sems, recv_sems = refs[ns + nl:ns + nl + 2]
        x, y, c = _place()
        for i in range(nl):
            for k, (px, py) in enumerate(_chip_peers(x, y)):
                src, dst = views.landing(i, src_refs, land_refs, k, (x, y, c, 2 * x + y, px, py))
                landing = pltpu.make_async_remote_copy(
                    src_ref=src, dst_ref=dst, send_sem=send_sems.at[3 * i + k], recv_sem=recv_sems.at[3 * i + k],
                    device_id=(x, y, 1 - c) if views.to_sibling else (px, py, c), device_id_type=MESH)
                landing.wait_send()
                landing.wait_recv()

    outs = pl.pallas_call(
        body, name=name,
        out_shape=tuple(pltpu.HBM(a.shape, a.dtype) for a in (*srcs, *lands)),
        in_specs=[HBM] * (ns + nl) + [SEM, SEM] + [ANY] * len(after), out_specs=(HBM,) * (ns + nl),
        input_output_aliases={i: i for i in range(ns + nl)},
        compiler_params=pltpu.CompilerParams(has_side_effects=pltpu.SideEffectType.DATAFLOW_SIDE_EFFECTING),
    )(*srcs, *lands, send_sems, recv_sems, *after)
    return outs if with_sources else outs[ns:]


def _empty_lands(shards, slots, own_slot):
    lands = [lax.empty((slots,) + s.shape, s.dtype) for s in shards]
    if own_slot:
        me = 2 * lax.axis_index("x") + lax.axis_index("y")
        lands = [lax.dynamic_update_index_in_dim(z, s, me, 0) for z, s in zip(lands, shards)]
    return lands


def _sibling_swap(arrays, name, other_half=False):
    n = len(arrays)

    def body(*refs):
        ins, outs = refs[:n], refs[n:2 * n]
        send_sems, recv_sems = refs[2 * n:]
        x, y, c = _place()
        copies = []
        for i in range(n):
            src = ins[i]
            if other_half:
                rows = src.shape[1] // 2
                src = src.at[:, pl.ds(pl.multiple_of((1 - c) * rows, BF16_SUBLANES), rows)]
            cp = pltpu.make_async_remote_copy(
                src_ref=src, dst_ref=outs[i], send_sem=send_sems.at[i], recv_sem=recv_sems.at[i],
                device_id=(x, y, 1 - c), device_id_type=MESH)
            cp.start()
            copies.append(cp)
        for cp in copies:
            cp.wait()

    shapes = [(a.shape[0], a.shape[1] // 2, a.shape[2]) if other_half else a.shape for a in arrays]
    return pl.pallas_call(
        body, name=name,
        out_shape=tuple(jax.ShapeDtypeStruct(s, a.dtype) for s, a in zip(shapes, arrays)),
        in_specs=[ANY] * n, out_specs=(ANY,) * n,
        scratch_shapes=[pltpu.SemaphoreType.DMA((n,)), pltpu.SemaphoreType.DMA((n,))],
    )(*arrays)


FOLD_STEPS = 2


def _fold_add(core, parts, theirs, name):
    n = len(parts)
    s, r, cols = parts[0].shape
    tr = r // 2 // FOLD_STEPS

    def body(core_ref, *refs):
        for p_ref, t_ref, o_ref in zip(refs[:n], refs[n:2 * n], refs[2 * n:]):
            o_ref[...] = (p_ref[...].astype(F32) + t_ref[...].astype(F32)).astype(BF16)

    half = pl.BlockSpec((1, tr, cols), lambda j, i, core_ref: (j, i, 0))
    return pl.pallas_call(
        body, name=name, out_shape=tuple(jax.ShapeDtypeStruct((s, r // 2, cols), BF16) for _ in parts),
        grid_spec=pltpu.PrefetchScalarGridSpec(
            num_scalar_prefetch=1, grid=(s, FOLD_STEPS),
            in_specs=[pl.BlockSpec((1, tr, cols), lambda j, i, core_ref: (j, core_ref[0] * FOLD_STEPS + i, 0))] * n + [half] * n,
            out_specs=(half,) * n),
        compiler_params=_params("arbitrary", "arbitrary"),
    )(core, *parts, *theirs)


N_DEV = 8


def _allreduce_small(pack):
    def body(in_ref, out_ref, buf, send_sems, recv_sems):
        x, y, c = _place()
        me = 4 * x + 2 * y + c
        buf[me] = in_ref[...]

        def copy(j, slot):
            px, py, pc = x ^ (j >> 2), y ^ ((j >> 1) & 1), c ^ (j & 1)
            return pltpu.make_async_remote_copy(
                src_ref=in_ref, dst_ref=buf.at[slot(px, py, pc)], send_sem=send_sems.at[j], recv_sem=recv_sems.at[j],
                device_id=(px, py, pc), device_id_type=MESH)

        for j in range(1, N_DEV):
            copy(j, lambda px, py, pc: me).start()
        for j in range(1, N_DEV):
            landing = copy(j, lambda px, py, pc: 4 * px + 2 * py + pc)
            landing.wait_send()
            landing.wait_recv()
        acc = buf[0]
        for s in range(1, N_DEV):
            acc = acc + buf[s]
        out_ref[...] = acc

    return pl.pallas_call(
        body, name="allreduce_small", out_shape=jax.ShapeDtypeStruct(pack.shape, F32),
        in_specs=[VMEM_FULL], out_specs=VMEM_FULL,
        scratch_shapes=[pltpu.VMEM((N_DEV,) + pack.shape, F32), pltpu.SemaphoreType.DMA((N_DEV,)),
                        pltpu.SemaphoreType.DMA((N_DEV,))],
    )(pack)


BF16_SUBLANES = 16


def _reduce_own(me, parts, recvs, dep, steps, name):
    n = len(parts)

    def body(me_ref, *refs):
        for p_ref, rv_ref, o_ref in zip(refs[:n], refs[n:2 * n], refs[2 * n + 1:]):
            acc = p_ref[0].astype(F32)
            for k in range(3):
                acc = acc + rv_ref[k].astype(F32)
            o_ref[...] = acc

    shapes = [(p.shape[1] // steps, p.shape[2]) for p in parts]
    return pl.pallas_call(
        body, name=name, out_shape=tuple(jax.ShapeDtypeStruct(p.shape[1:], F32) for p in parts),
        grid_spec=pltpu.PrefetchScalarGridSpec(
            num_scalar_prefetch=1, grid=(steps,),
            in_specs=[pl.BlockSpec((1, tr, c), lambda i, me_ref: (me_ref[0], i, 0)) for tr, c in shapes]
            + [pl.BlockSpec((3, tr, c), lambda i, me_ref: (0, i, 0)) for tr, c in shapes] + [ANY],
            out_specs=tuple(pl.BlockSpec((tr, c), lambda i, me_ref: (i, 0)) for tr, c in shapes)),
        compiler_params=_params("arbitrary"),
    )(me, *parts, *recvs, dep)


def _adamw(ws, gas, gbs, ms, vs, steps, name):
    n = len(ws)
    c1 = 1.0 - ADAM_B1 ** ADAM_STEP
    c2 = 1.0 - ADAM_B2 ** ADAM_STEP
    operands = [ws, gas, ms, vs] if gbs is None else [ws, gas, gbs, ms, vs]
    k = len(operands)

    def body(*refs):
        ins, outs = refs[:k * n], refs[k * n:]
        for j in range(n):
            w_ref, ga_ref, *gb_ref, m_ref, v_ref = ins[j::n]
            g_out, d_out, m_out, v_out = outs[j::n]
            g = ga_ref[...] + gb_ref[0][...] if gb_ref else ga_ref[...]
            mn = ADAM_B1 * m_ref[...] + (1.0 - ADAM_B1) * g
            vn = ADAM_B2 * v_ref[...] + (1.0 - ADAM_B2) * (g * g)
            g_out[...] = g
            m_out[...] = mn
            v_out[...] = vn
            d_out[...] = -ADAM_LR * ((mn / c1) / (jnp.sqrt(vn / c2) + ADAM_EPS) + ADAM_WD * w_ref[...])

    tiles = [pl.BlockSpec((w.shape[0] // steps, w.shape[1]), lambda i: (i, 0)) for w in ws]
    shapes = [jax.ShapeDtypeStruct(w.shape, F32) for w in ws]
    outs = pl.pallas_call(
        body, name=name, grid=(steps,), out_shape=tuple(shapes * 4), in_specs=tiles * k, out_specs=tuple(tiles * 4),
        compiler_params=_params("arbitrary"),
    )(*[a for group in operands for a in group])
    return [outs[j::n] for j in range(n)]


PACK_COLS = 512


def _to_rows(a):
    flat = a.reshape(-1)
    pad = (-flat.shape[0]) % PACK_COLS
    return jnp.pad(flat, (0, pad)).reshape(-1, PACK_COLS)


def _pack(arrays, extra_rows=0):
    rows = [_to_rows(a) for a in arrays]
    n = sum(r.shape[0] for r in rows) + extra_rows
    pad = (-n) % 8
    return jnp.concatenate(rows + [jnp.zeros((extra_rows + pad, PACK_COLS), F32)], axis=0)


def _unpack(pack, like):
    out, at = [], 0
    for a in like:
        n = -(-a.size // PACK_COLS)
        out.append(pack[at:at + n].reshape(-1)[:a.size].reshape(a.shape))
        at += n
    return out


COL_SHARDED = ("ffn1_w_gate", "ffn1_w_up", "w_in", "ffn2_w_gate", "ffn2_w_up", "w2", "a2", "g2")
ROW_SHARDED = ("ffn1_w_down", "ffn2_w_down", "w_out", "w1", "a1", "g1")
CHUNKED = ("ffn1_w_gate", "ffn1_w_up", "ffn1_w_down", "w_in", "ffn2_w_gate", "ffn2_w_up", "ffn2_w_down")
WEIGHTS = ("ffn1_norm", "ffn1_w_gate", "ffn1_w_up", "ffn1_w_down", "mix_norm", "w_in", "q_norm", "k_norm",
           "mu_r", "mu_k", "mu_v", "mu_w", "mu_a", "mu_g", "w0", "w1", "w2", "a0", "a1", "a2", "g1", "g2",
           "k_k", "k_a", "r_k", "ln_x_w", "ln_x_b", "w_out", "ffn2_norm", "ffn2_w_gate", "ffn2_w_up", "ffn2_w_down")


TRANSPOSED = ("ffn1_w_gate", "ffn1_w_up", "ffn2_w_gate", "ffn2_w_up")


def _shard_2d(name, a):
    return a[0].T if name in TRANSPOSED else a[0]


def _full_from_blocks(name, blocks):
    if name in CHUNKED:
        return blocks
    if name in ROW_SHARDED:
        return blocks.reshape(-1, blocks.shape[-1])
    return blocks.transpose(1, 0, 2).reshape(blocks.shape[1], -1)


def _blocks_from_full(name, full):
    if name in CHUNKED:
        return full
    if name in ROW_SHARDED:
        return full.reshape(N_SHARDS, -1, full.shape[-1])
    return full.reshape(full.shape[0], N_SHARDS, -1).transpose(1, 0, 2)


FFN1_GROUP = ("ffn1_w_gate", "ffn1_w_up", "ffn1_w_down")
MIX_GROUP = ("w_in",) + RWKV_MAT
OUT_GROUP = ("w_out", "ffn2_w_gate", "ffn2_w_up", "ffn2_w_down")
FFN2_GROUP = OUT_GROUP[1:]
LATE_GROUP = ("w_in", "w_out") + RWKV_MAT


class _Exchange:
    def __init__(self, given):
        self.given = given
        first = self._gather_start(FFN1_GROUP, _HalfGatherViews, jnp.zeros(DEP_SHAPE, F32), "gather_ffn1_start")
        self.mix = self._gather_start(MIX_GROUP, _GatherViews, first[4], "gather_mix_start")
        self.out = self._gather_start(OUT_GROUP, _GatherViews, self.mix[4], "gather_out_start")
        self.first_dep = self.out[4]
        halves = _push_wait(first, _HalfGatherViews, (self.first_dep,), "gather_ffn1_wait")
        passed = _push_start([], halves, _ForwardViews, jnp.zeros(DEP_SHAPE, F32), "gather_ffn1_pass_start")
        self.first_weights = self._full(FFN1_GROUP, _push_wait(passed, _ForwardViews, (passed[4],), "gather_ffn1_pass_wait"))
        self.parts, self.recv = {}, {}

    def _shards(self, names):
        return [_shard_2d(n, self.given[n]).astype(BF16) for n in names]

    @staticmethod
    def _full(names, blocks):
        out = {}
        for n, b in zip(names, blocks):
            full = _full_from_blocks(n, b)
            out[n] = full.astype(F32) if n in RWKV_MAT else full
        return out

    def _gather_start(self, names, views, after, name):
        shards = self._shards(names)
        return _push_start(shards, _empty_lands(shards, N_SHARDS, True), views, after, name)

    def mix_weights(self, after):
        return self._full(MIX_GROUP, _push_wait(self.mix, _GatherViews, after, "gather_mix_wait"))

    def out_weights(self, after):
        return self._full(OUT_GROUP, _push_wait(self.out, _GatherViews, after, "gather_out_wait"))

    def _scatter_start(self, grads, name):
        names = tuple(grads)
        parts = [_blocks_from_full(n, grads[n]) for n in names]
        self.parts.update(zip(names, parts))
        lands = [lax.empty((3,) + p.shape[1:], BF16) for p in parts]
        return _push_start([p.astype(BF16) for p in parts], lands, _ScatterViews, jnp.zeros(DEP_SHAPE, F32), name)

    def _scatter_done(self, started, names, after, name):
        outs = _push_wait(started, _ScatterViews, after, name, with_sources=True)
        for n, sent, got in zip(names, outs[:len(names)], outs[len(names):]):
            self.recv[n] = got
            if self.parts[n].dtype == BF16:
                self.parts[n] = sent

    def send_ffn2(self, grads):
        self.ffn2 = self._scatter_start(grads, "scatter_ffn2_start")
        return self.ffn2[4]

    def send_mix(self, grads, after):
        self._scatter_done(self.ffn2, FFN2_GROUP, after, "scatter_ffn2_wait")
        self.late = self._scatter_start(grads, "scatter_late_start")
        return self.late[4]

    def send_ffn1(self, grads):
        self.ffn1 = self._scatter_start(grads, "scatter_ffn1_start")
        return self.ffn1[4]

    def late_received(self, after):
        self._scatter_done(self.late, LATE_GROUP, after, "scatter_late_wait")

    def ffn1_received(self, after):
        self._scatter_done(self.ffn1, FFN1_GROUP, after, "scatter_ffn1_wait")


def kernel(
        x, ffn1_norm, ffn1_w_gate, ffn1_w_up, ffn1_w_down, mix_norm, w_in, q_norm, k_norm, mu_r, mu_k, mu_v, mu_w,
        mu_a, mu_g, w0, w1, w2, a0, a1, a2, g1, g2, k_k, k_a, r_k, ln_x_w, ln_x_b, w_out, ffn2_norm, ffn2_w_gate,
        ffn2_w_up, ffn2_w_down, loss_target, m_ffn1_norm, m_ffn1_w_gate, m_ffn1_w_up, m_ffn1_w_down, m_mix_norm,
        m_w_in, m_q_norm, m_k_norm, m_mu_r, m_mu_k, m_mu_v, m_mu_w, m_mu_a, m_mu_g, m_w0, m_w1, m_w2, m_a0, m_a1,
        m_a2, m_g1, m_g2, m_k_k, m_k_a, m_r_k, m_ln_x_w, m_ln_x_b, m_w_out, m_ffn2_norm, m_ffn2_w_gate, m_ffn2_w_up,
        m_ffn2_w_down, v_ffn1_norm, v_ffn1_w_gate, v_ffn1_w_up, v_ffn1_w_down, v_mix_norm, v_w_in, v_q_norm, v_k_norm,
        v_mu_r, v_mu_k, v_mu_v, v_mu_w, v_mu_a, v_mu_g, v_w0, v_w1, v_w2, v_a0, v_a1, v_a2, v_g1, v_g2, v_k_k, v_k_a,
        v_r_k, v_ln_x_w, v_ln_x_b, v_w_out, v_ffn2_norm, v_ffn2_w_gate, v_ffn2_w_up, v_ffn2_w_down):
    given = dict(locals())
    sharded = COL_SHARDED + ROW_SHARDED
    sharded = tuple(n for n in WEIGHTS if n in sharded)
    small = tuple(n for n in WEIGHTS if n not in sharded)

    ex = _Exchange(given)
    w = {n: given[n] for n in small}
    w.update(ex.first_weights)
    loss, dx, g = _local_step(x[0], loss_target[0], w, ex)

    core = lax.axis_index("c").astype(jnp.int32).reshape(1)
    late = [g[n] for n in FFN1_GROUP]
    folded = _fold_add(core, late, _sibling_swap(late, "fold_swap_ffn1", other_half=True), "fold_add_ffn1")
    dep = ex.send_ffn1(dict(zip(FFN1_GROUP, folded)))

    me = (2 * lax.axis_index("x") + lax.axis_index("y")).astype(jnp.int32).reshape(1)
    out = {}

    def settle(names, dep, tag):
        done = []
        for kind, sub, r_steps, a_steps in (("large", tuple(n for n in names if n not in RWKV_MAT), 4, 8),
                                            ("small", tuple(n for n in names if n in RWKV_MAT), 1, 1)):
            if not sub:
                continue
            parts = [ex.parts[n].reshape(N_SHARDS, -1, ex.parts[n].shape[-1]) for n in sub]
            recvs = [ex.recv[n].reshape(3, -1, ex.recv[n].shape[-1]) for n in sub]
            mine = _reduce_own(me, parts, recvs, dep, r_steps, f"reduce_{tag}_{kind}")
            theirs = _sibling_swap(mine, f"sibling_swap_{tag}_{kind}")
            res = _adamw([_shard_2d(n, given[n]) for n in sub], mine, theirs, [_shard_2d(n, given["m_" + n]) for n in sub],
                         [_shard_2d(n, given["v_" + n]) for n in sub], a_steps, f"adamw_{tag}_{kind}")
            for n, rs in zip(sub, res):
                out[n] = [(r.T if n in TRANSPOSED else r).reshape(given[n].shape) for r in rs]
                done.append(out[n][1])
        return tuple(done)

    ex.late_received((dep,))
    last = settle(tuple(n for n in sharded if n not in FFN1_GROUP), dep, "rest")

    gpack = _pack([g[n] for n in small], extra_rows=1)
    n_rows = sum(-(-given[n].size // PACK_COLS) for n in small)
    gpack = gpack.at[n_rows, :loss.shape[1]].set(loss[0])
    gsum = _allreduce_small(gpack)
    res = _adamw([_pack([given[n] for n in small], 1)], [gsum], None, [_pack([given["m_" + n] for n in small], 1)],
                 [_pack([given["v_" + n] for n in small], 1)], 1, "adamw_replicated")[0]
    like = [given[n] for n in small]
    for j, r in enumerate(res):
        for n, a in zip(small, _unpack(r, like)):
            out.setdefault(n, [None] * 4)[j] = a
    total_loss = gsum[n_rows, 0]

    ex.ffn1_received((*last, res[1]))
    halves = _reduce_own(me, [ex.parts[n] for n in FFN1_GROUP], [ex.recv[n] for n in FFN1_GROUP],
                         jnp.zeros(DEP_SHAPE, F32), FOLD_STEPS, "reduce_ffn1")
    others = _sibling_swap(halves, "sibling_swap_ffn1")
    first = lax.axis_index("c") == 0
    grads = [jnp.concatenate([jnp.where(first, a, b), jnp.where(first, b, a)], axis=0) for a, b in zip(halves, others)]
    res = _adamw([_shard_2d(n, given[n]) for n in FFN1_GROUP], grads, None, [_shard_2d(n, given["m_" + n]) for n in FFN1_GROUP],
                 [_shard_2d(n, given["v_" + n]) for n in FFN1_GROUP], 8, "adamw_ffn1")
    for n, rs in zip(FFN1_GROUP, res):
        out[n] = [(r.T if n in TRANSPOSED else r).reshape(given[n].shape) for r in rs]
    return (total_loss, dx[None], *[out[n][0] for n in WEIGHTS], *[out[n][1] for n in WEIGHTS],
            *[out[n][2] for n in WEIGHTS], *[out[n][3] for n in WEIGHTS])
```

```python
import functools

import jax
import jax.numpy as jnp
from jax import lax
from jax.experimental import pallas as pl
from jax.experimental.pallas import tpu as pltpu

F32 = jnp.float32
BF16 = jnp.bfloat16
MESH = pl.DeviceIdType.MESH

RMS_EPS = 1e-6
GN_EPS = 64e-5
NEG_INF = -1e30
FFN_RESIDUAL = 0.5
HEAD_DIM = 64
ATT_BLOCK = 128
DILATIONS = (1, 4, 16)
SCAN_CHUNK = 64
TOKEN_TILE = 256
FFN_BWD_TILE = 512

ADAM_LR = 0.001
ADAM_B1 = 0.9
ADAM_B2 = 0.999
ADAM_EPS = 1e-08
ADAM_WD = 0.01
ADAM_STEP = 10

VMEM_FULL = pl.BlockSpec(memory_space=pltpu.VMEM)
ANY = pl.BlockSpec(memory_space=pl.ANY)


VMEM_LIMIT = 56 * 1024 * 1024


def _params(*sem):
    return pltpu.CompilerParams(dimension_semantics=sem, vmem_limit_bytes=VMEM_LIMIT)


def _dot(a, b, dims):
    return lax.dot_general(a.astype(BF16), b.astype(BF16), (dims, ((), ())), preferred_element_type=F32)


def _dot_nn(a, b):
    return _dot(a, b, ((1,), (0,)))


def _dot_nt(a, b):
    return _dot(a, b, ((1,), (1,)))


def _dot_tn(a, b):
    return _dot(a, b, ((0,), (0,)))


@jax.custom_vjp
def _mm(a, b):
    return _dot_nn(a, b)


def _mm_fwd(a, b):
    return _dot_nn(a, b), (a, b)


def _mm_bwd(res, g):
    a, b = res
    return _dot_nt(g, b).astype(a.dtype), _dot_tn(a, g).astype(b.dtype)


_mm.defvjp(_mm_fwd, _mm_bwd)


def _bdot(a, b, ca, cb):
    return lax.dot_general(a.astype(BF16), b.astype(BF16), (((ca,), (cb,)), ((0,), (0,))), preferred_element_type=F32)


@jax.custom_vjp
def _bmm_nt(a, b):
    return _bdot(a, b, 2, 2)


def _bmm_nt_fwd(a, b):
    return _bdot(a, b, 2, 2), (a, b)


def _bmm_nt_bwd(res, g):
    a, b = res
    return _bdot(g, b, 2, 1), _bdot(g, a, 1, 1)


_bmm_nt.defvjp(_bmm_nt_fwd, _bmm_nt_bwd)


@jax.custom_vjp
def _bmm_nn(a, b):
    return _bdot(a, b, 2, 1)


def _bmm_nn_fwd(a, b):
    return _bdot(a, b, 2, 1), (a, b)


def _bmm_nn_bwd(res, g):
    a, b = res
    return _bdot(g, b, 2, 2), _bdot(a, g, 1, 1)


_bmm_nn.defvjp(_bmm_nn_fwd, _bmm_nn_bwd)


@jax.custom_vjp
def _bmm_tn(a, b):
    return _bdot(a, b, 1, 1)


def _bmm_tn_fwd(a, b):
    return _bdot(a, b, 1, 1), (a, b)


def _bmm_tn_bwd(res, g):
    a, b = res
    return _bdot(b, g, 2, 2), _bdot(a, g, 2, 1)


_bmm_tn.defvjp(_bmm_tn_fwd, _bmm_tn_bwd)


def _hdot(a, b, ca, cb):
    return lax.dot_general(a, b, (((ca,), (cb,)), ((0,), (0,))), precision=lax.Precision.HIGH, preferred_element_type=F32)


def _sigmoid(x):
    return 1.0 / (1.0 + jnp.exp(-x))


def _rms(x):
    return lax.rsqrt(jnp.mean(x * x, axis=-1, keepdims=True) + RMS_EPS)


def _ffn_fwd(x, norm, wg, wu, wd, dep, name, target=None):
    t, d = x.shape
    nc, fc, _ = wg.shape
    tm = TOKEN_TILE

    def body(x_ref, n_ref, wg_ref, wu_ref, wd_ref, dep_ref, *rest):
        o_ref, g_ref, u_ref = rest[-3:] if target is None else rest[1:4]
        xv = x_ref[...]
        h = (xv * _rms(xv) * n_ref[...]).astype(BF16)
        acc = jnp.zeros((tm, d), F32)
        for c in range(nc):
            g = _dot_nt(h, wg_ref[c])
            u = _dot_nt(h, wu_ref[c])
            g_ref[c] = g.astype(BF16)
            u_ref[c] = u.astype(BF16)
            a = (g * _sigmoid(g) * u).astype(BF16)
            acc = acc + jnp.dot(a, wd_ref[c], preferred_element_type=F32)
        y = xv + FFN_RESIDUAL * acc
        if target is None:
            o_ref[...] = y
        else:
            t_ref, loss_ref = rest[0], rest[4]
            err = y - t_ref[...]
            o_ref[...] = err * (1.0 / d)
            part = 0.5 * jnp.sum(jnp.mean(err * err, axis=-1, keepdims=True), axis=0, keepdims=True)

            @pl.when(pl.program_id(0) == 0)
            def _():
                loss_ref[...] = jnp.zeros_like(loss_ref)

            loss_ref[...] += jnp.broadcast_to(part, loss_ref.shape)

    tile = pl.BlockSpec((tm, d), lambda i: (i, 0))
    hidden = pl.BlockSpec((nc, tm, fc), lambda i: (0, i, 0))
    hshape = jax.ShapeDtypeStruct((nc, t, fc), BF16)
    with_loss = target is not None
    return pl.pallas_call(
        body, name=name, grid=(t // tm,),
        out_shape=(jax.ShapeDtypeStruct((t, d), F32), hshape, hshape) + ((jax.ShapeDtypeStruct((1, 128), F32),) if with_loss else ()),
        in_specs=[tile, pl.BlockSpec((1, d), lambda i: (0, 0)), VMEM_FULL, VMEM_FULL, VMEM_FULL, ANY] + ([tile] if with_loss else []),
        out_specs=(tile, hidden, hidden) + ((pl.BlockSpec((1, 128), lambda i: (0, 0)),) if with_loss else ()),
        compiler_params=_params("arbitrary"),
    )(x, norm, wg, wu, wd, dep, *((target,) if with_loss else ()))


def _rmsnorm_bwd(xv, gain, dh):
    rs = _rms(xv)
    xn = xv * rs
    dxn = dh * gain
    dx = rs * (dxn - xn * jnp.mean(dxn * xn, axis=-1, keepdims=True))
    return dx, jnp.sum(dh * xn, axis=0, keepdims=True)


def _ffn_bwd(x, norm, wg, wu, wd, gate, up, dy, dep, name):
    t, d = x.shape
    nc, fc, _ = wg.shape
    tm = FFN_BWD_TILE
    nt = t // tm

    def body(x_ref, n_ref, wg_ref, wu_ref, wd_ref, g_ref, u_ref, dy_ref, dep_ref, dx_ref, dn_ref, dwg_ref, dwu_ref,
             dwd_ref, dh_ref, ag_ref, au_ref, ad_ref):
        c, i = pl.program_id(0), pl.program_id(1)
        rows = pl.ds(pl.multiple_of(i * tm, tm), tm)
        xv = x_ref[...]
        gain = n_ref[...]
        h = (xv * _rms(xv) * gain).astype(BF16)
        dy = dy_ref[...]
        dyb = (FFN_RESIDUAL * dy).astype(BF16)
        g = g_ref[0].astype(F32)
        u = u_ref[0].astype(F32)
        sg = _sigmoid(g)
        s = g * sg
        a = (s * u).astype(BF16)
        da = _dot_nt(dyb, wd_ref[0])
        dub = (da * s).astype(BF16)
        dgb = (da * u * (sg * (1.0 + g * (1.0 - sg)))).astype(BF16)
        dwd_c = _dot_tn(a, dyb)
        dwg_c = _dot_tn(dgb, h)
        dwu_c = _dot_tn(dub, h)
        dh_c = _dot_nn(dgb, wg_ref[0]) + _dot_nn(dub, wu_ref[0])

        @pl.when(i == 0)
        def _():
            ad_ref[...] = dwd_c
            ag_ref[...] = dwg_c
            au_ref[...] = dwu_c

        @pl.when(i > 0)
        def _():
            ad_ref[...] += dwd_c
            ag_ref[...] += dwg_c
            au_ref[...] += dwu_c

        @pl.when(i == nt - 1)
        def _():
            dwd_ref[0] = ad_ref[...].astype(BF16)
            dwg_ref[0] = ag_ref[...].astype(BF16)
            dwu_ref[0] = au_ref[...].astype(BF16)

        @pl.when(c == 0)
        def _():
            dh_ref[rows, :] = dh_c

        @pl.when(c > 0)
        def _():
            dh_ref[rows, :] += dh_c

        @pl.when(c == nc - 1)
        def _():
            dx, dn = _rmsnorm_bwd(xv, gain, dh_ref[rows, :])
            dx_ref[...] = dx + dy

            @pl.when(i == 0)
            def _():
                dn_ref[...] = dn

            @pl.when(i > 0)
            def _():
                dn_ref[...] += dn

    tile = pl.BlockSpec((tm, d), lambda c, i: (i, 0))
    row = pl.BlockSpec((1, d), lambda c, i: (0, 0))
    wrow = pl.BlockSpec((1, fc, d), lambda c, i: (c, 0, 0), pipeline_mode=pl.Buffered(1))
    hidden = pl.BlockSpec((1, tm, fc), lambda c, i: (c, i, 0))
    last = pl.BlockSpec((tm, d), lambda c, i: (jnp.where(c == nc - 1, i, 0), 0))
    return pl.pallas_call(
        body, name=name, grid=(nc, nt),
        out_shape=(jax.ShapeDtypeStruct((t, d), F32), jax.ShapeDtypeStruct((1, d), F32),
                   jax.ShapeDtypeStruct(wg.shape, BF16), jax.ShapeDtypeStruct(wu.shape, BF16),
                   jax.ShapeDtypeStruct(wd.shape, BF16)),
        in_specs=[tile, row, wrow, wrow, wrow, hidden, hidden, tile, ANY],
        out_specs=(last, row, wrow, wrow, wrow),
        scratch_shapes=[pltpu.VMEM((t, d), F32)] + [pltpu.VMEM((fc, d), F32)] * 3,
        compiler_params=_params("arbitrary", "arbitrary"),
    )(x, norm, wg, wu, wd, gate, up, dy, dep)


def _store_heads(ref, v):
    for h in range(ref.shape[0]):
        ref[h] = v[:, h * HEAD_DIM:(h + 1) * HEAD_DIM]


def _load_heads(ref):
    return jnp.concatenate([ref[h] for h in range(ref.shape[0])], axis=-1)


N_HEAD_GROUPS = 3


def _proj_fwd(x, norm, w, c):
    t, d = x.shape
    nc, _, ncol = w.shape
    nh = c // HEAD_DIM
    tm = TOKEN_TILE
    wide = nc * ncol - N_HEAD_GROUPS * c

    def body(x_ref, n_ref, w_ref, q_ref, k_ref, v_ref, cur_ref):
        xv = x_ref[...]
        h = (xv * _rms(xv) * n_ref[...]).astype(BF16)
        full = jnp.concatenate([jnp.dot(h, w_ref[s], preferred_element_type=F32) for s in range(nc)], axis=1)
        for m, ref in enumerate((q_ref, k_ref, v_ref)):
            _store_heads(ref, full[:, m * c:(m + 1) * c])
        cur_ref[...] = full[:, N_HEAD_GROUPS * c:]

    heads = pl.BlockSpec((nh, tm, HEAD_DIM), lambda i: (0, i, 0))
    hshape = jax.ShapeDtypeStruct((nh, t, HEAD_DIM), F32)
    return pl.pallas_call(
        body, name="proj_fwd", grid=(t // tm,),
        out_shape=(hshape, hshape, hshape, jax.ShapeDtypeStruct((t, wide), F32)),
        in_specs=[pl.BlockSpec((tm, d), lambda i: (i, 0)), pl.BlockSpec((1, d), lambda i: (0, 0)), VMEM_FULL],
        out_specs=(heads, heads, heads, pl.BlockSpec((tm, wide), lambda i: (i, 0))),
        compiler_params=_params("arbitrary"),
    )(x, norm, w)


def _proj_bwd(x, norm, w, dq, dk, dv, dcur, dres):
    t, d = x.shape
    nc, _, ncol = w.shape
    nh = dq.shape[0]
    tm = TOKEN_TILE
    nt = t // tm
    wide = dcur.shape[1]

    def body(x_ref, n_ref, w_ref, dq_ref, dk_ref, dv_ref, dcur_ref, dres_ref, dx_ref, dn_ref, dw_ref, acc_ref):
        i = pl.program_id(0)

        @pl.when(i == 0)
        def _():
            acc_ref[...] = jnp.zeros_like(acc_ref)
            dn_ref[...] = jnp.zeros_like(dn_ref)

        xv = x_ref[...]
        gain = n_ref[...]
        h = (xv * _rms(xv) * gain).astype(BF16)
        dp = jnp.concatenate([_load_heads(dq_ref), _load_heads(dk_ref), _load_heads(dv_ref), dcur_ref[...]], axis=1).astype(BF16)
        dh = jnp.zeros((tm, d), F32)
        for s in range(nc):
            dps = dp[:, s * ncol:(s + 1) * ncol]
            acc_ref[s] += _dot_tn(h, dps)
            dh = dh + _dot_nt(dps, w_ref[s])
        dx, dn = _rmsnorm_bwd(xv, gain, dh)
        dx_ref[...] = dx + dres_ref[...]
        dn_ref[...] += dn

        @pl.when(i == nt - 1)
        def _():
            dw_ref[...] = acc_ref[...].astype(BF16)

    tile = pl.BlockSpec((tm, d), lambda i: (i, 0))
    row = pl.BlockSpec((1, d), lambda i: (0, 0))
    heads = pl.BlockSpec((nh, tm, HEAD_DIM), lambda i: (0, i, 0))
    return pl.pallas_call(
        body, name="proj_bwd", grid=(nt,),
        out_shape=(jax.ShapeDtypeStruct((t, d), F32), jax.ShapeDtypeStruct((1, d), F32),
                   jax.ShapeDtypeStruct(w.shape, BF16)),
        in_specs=[tile, row, VMEM_FULL, heads, heads, heads, pl.BlockSpec((tm, wide), lambda i: (i, 0)), tile],
        out_specs=(tile, row, VMEM_FULL),
        scratch_shapes=[pltpu.VMEM(w.shape, F32)], compiler_params=_params("arbitrary"),
    )(x, norm, w, dq, dk, dv, dcur, dres)


def _mixout_fwd(x, att, opg, gate, w):
    t, d = x.shape
    nh = att.shape[0]
    half = gate.shape[1]
    tm = TOKEN_TILE

    def body(x_ref, att_ref, opg_ref, g_ref, w_ref, o_ref):
        mix = jnp.concatenate([_load_heads(att_ref), _load_heads(opg_ref) * g_ref[...]], axis=-1).astype(BF16)
        o_ref[...] = x_ref[...] + jnp.dot(mix, w_ref[...], preferred_element_type=F32)

    tile = pl.BlockSpec((tm, d), lambda i: (i, 0))
    htile = pl.BlockSpec((tm, half), lambda i: (i, 0))
    heads = pl.BlockSpec((nh, tm, HEAD_DIM), lambda i: (0, i, 0))
    return pl.pallas_call(
        body, name="mixout_fwd", grid=(t // tm,), out_shape=jax.ShapeDtypeStruct((t, d), F32),
        in_specs=[tile, heads, heads, htile, VMEM_FULL], out_specs=tile, compiler_params=_params("arbitrary"),
    )(x, att, opg, gate, w)


def _mixout_bwd(att, opg, gate, w, dy, dep):
    nh, t, _ = att.shape
    half = gate.shape[1]
    d = dy.shape[1]
    tm = TOKEN_TILE

    def body(att_ref, opg_ref, g_ref, w_ref, dy_ref, dep_ref, datt_ref, dopg_ref, dg_ref, dw_ref):
        i = pl.program_id(0)
        opg_v, g_v = _load_heads(opg_ref), g_ref[...]
        mix = jnp.concatenate([_load_heads(att_ref), opg_v * g_v], axis=-1).astype(BF16)
        dyb = dy_ref[...].astype(BF16)
        dmix = _dot_nt(dyb, w_ref[...])
        dw = _dot_tn(mix, dyb)
        _store_heads(datt_ref, dmix[:, :half])
        drw = dmix[:, half:]
        _store_heads(dopg_ref, drw * g_v)
        dg_ref[...] = drw * opg_v

        @pl.when(i == 0)
        def _():
            dw_ref[...] = dw

        @pl.when(i > 0)
        def _():
            dw_ref[...] += dw

    tile = pl.BlockSpec((tm, d), lambda i: (i, 0))
    htile = pl.BlockSpec((tm, half), lambda i: (i, 0))
    heads = pl.BlockSpec((nh, tm, HEAD_DIM), lambda i: (0, i, 0))
    hshape = jax.ShapeDtypeStruct((nh, t, HEAD_DIM), F32)
    return pl.pallas_call(
        body, name="mixout_bwd", grid=(t // tm,),
        out_shape=(hshape, hshape, jax.ShapeDtypeStruct((t, half), F32), jax.ShapeDtypeStruct(w.shape, F32)),
        in_specs=[heads, heads, htile, VMEM_FULL, tile, ANY],
        out_specs=(heads, heads, htile, pl.BlockSpec(w.shape, lambda i: (0, 0))),
        compiler_params=_params("arbitrary"),
    )(att, opg, gate, w, dy, dep)


def _head_norm(x, gain):
    return x * _rms(x) * gain


def _att_pattern(qh, kh, v, nb):
    g, blk, _ = qh.shape
    scale = HEAD_DIM ** -0.5
    qi = lax.broadcasted_iota(jnp.int32, (blk, blk), 0)
    kj = lax.broadcasted_iota(jnp.int32, (blk, blk), 1)
    sc = jnp.where(kj <= qi, _bmm_nt(qh, kh) * scale, NEG_INF)
    top = jnp.max(sc, axis=-1, keepdims=True)
    if nb > 1:
        khp = jnp.concatenate([kh[:1], kh[:-1]], axis=0)
        vp = jnp.concatenate([v[:1], v[:-1]], axis=0)
        has_prev = lax.broadcasted_iota(jnp.int32, (g, 1, 1), 0) % nb != 0
        sp = jnp.where((kj >= qi) & has_prev, _bmm_nt(qh, khp) * scale, NEG_INF)
        top = jnp.maximum(top, jnp.max(sp, axis=-1, keepdims=True))
    m = lax.stop_gradient(top)
    pc = jnp.exp(sc - m)
    den = jnp.sum(pc, axis=-1, keepdims=True)
    acc = _bmm_nn(pc, v)
    if nb > 1:
        pp = jnp.exp(sp - m)
        den = den + jnp.sum(pp, axis=-1, keepdims=True)
        acc = acc + _bmm_nn(pp, vp)
    o = acc / den
    return o, jnp.broadcast_to(m + jnp.log(den), o.shape)


def _pattern_rows(t, dil):
    length = t // dil
    return [pl.ds(r, length, stride=dil) if dil > 1 else pl.ds(0, length) for r in range(dil)], length // ATT_BLOCK


def _take(ref, rows, nb):
    return jnp.concatenate([ref[0, r, :].reshape(nb, ATT_BLOCK, HEAD_DIM) for r in rows], axis=0)


def _put(ref, rows, nb, val):
    for j, r in enumerate(rows):
        ref[0, r, :] = val[j * nb:(j + 1) * nb].reshape(nb * ATT_BLOCK, HEAD_DIM)


def _put_add(ref, rows, nb, val):
    for j, r in enumerate(rows):
        ref[0, r, :] += val[j * nb:(j + 1) * nb].reshape(nb * ATT_BLOCK, HEAD_DIM)


def _merge_fn(o1, o2, o3, l1, l2, l3):
    m = lax.stop_gradient(jnp.maximum(jnp.maximum(l1, l2), l3))
    e1, e2, e3 = jnp.exp(l1 - m), jnp.exp(l2 - m), jnp.exp(l3 - m)
    return (e1 * o1 + e2 * o2 + e3 * o3) / (e1 + e2 + e3)


def _token_rows(j):
    return pl.ds(pl.multiple_of(j * ATT_BLOCK, ATT_BLOCK), ATT_BLOCK)


def _norm_rows(t, q_ref, k_ref, gq, gk, qh_ref, kh_ref):
    def step(j, carry):
        rows = _token_rows(j)
        qh_ref[0, rows, :] = _head_norm(q_ref[0, rows, :], gq[0])
        kh_ref[0, rows, :] = _head_norm(k_ref[0, rows, :], gk[0])
        return carry

    lax.fori_loop(0, t // ATT_BLOCK, step, 0)


def _att_head_specs(t):
    head = pl.BlockSpec((1, t, HEAD_DIM), lambda h: (h, 0, 0))
    gain = pl.BlockSpec((1, 1, HEAD_DIM), lambda h: (0, 0, 0))
    return head, gain


def _att_fwd(q, k, v, qn, kn):
    nh, t, dh = q.shape
    head, gain = _att_head_specs(t)

    def body(q_ref, k_ref, v_ref, qn_ref, kn_ref, att_ref, o1, o2, o3, l1, l2, l3, qh_ref, kh_ref):
        saved = (o1, o2, o3, l1, l2, l3)
        _norm_rows(t, q_ref, k_ref, qn_ref[...], kn_ref[...], qh_ref, kh_ref)
        for p, dil in enumerate(DILATIONS):
            rows, nb = _pattern_rows(t, dil)
            o, lse = _att_pattern(_take(qh_ref, rows, nb), _take(kh_ref, rows, nb), _take(v_ref, rows, nb), nb)
            _put(saved[p], rows, nb, o)
            _put(saved[3 + p], rows, nb, lse)

        def merge(j, carry):
            rows = _token_rows(j)
            att_ref[0, rows, :] = _merge_fn(*[r[0, rows, :] for r in saved])
            return carry

        lax.fori_loop(0, t // ATT_BLOCK, merge, 0)

    return pl.pallas_call(
        body, name="att_fwd", grid=(nh,), out_shape=(jax.ShapeDtypeStruct(q.shape, F32),) * 7,
        in_specs=[head, head, head, gain, gain], out_specs=(head,) * 7,
        scratch_shapes=[pltpu.VMEM((1, t, dh), F32)] * 2, compiler_params=_params("arbitrary"),
    )(q, k, v, qn, kn)


def _att_bwd(q, k, v, qn, kn, saved, datt):
    nh, t, dh = q.shape
    head, gain = _att_head_specs(t)

    def body(q_ref, k_ref, v_ref, qn_ref, kn_ref, o1, o2, o3, l1, l2, l3, datt_ref,
             dq_ref, dk_ref, dv_ref, dqn_ref, dkn_ref, qh_ref, kh_ref, dqh_ref, dkh_ref, *ct_refs):
        for ref in (dqh_ref, dkh_ref, dv_ref):
            ref[...] = jnp.zeros_like(ref)

        @pl.when(pl.program_id(0) == 0)
        def _():
            dqn_ref[...] = jnp.zeros_like(dqn_ref)
            dkn_ref[...] = jnp.zeros_like(dkn_ref)

        gq, gk = qn_ref[...], kn_ref[...]
        _norm_rows(t, q_ref, k_ref, gq, gk, qh_ref, kh_ref)

        def merge_cotangents(j, carry):
            rows = _token_rows(j)
            _, merge_vjp = jax.vjp(_merge_fn, *[r[0, rows, :] for r in (o1, o2, o3, l1, l2, l3)])
            for ref, val in zip(ct_refs, merge_vjp(datt_ref[0, rows, :])):
                ref[0, rows, :] = val
            return carry

        lax.fori_loop(0, t // ATT_BLOCK, merge_cotangents, 0)

        for p, dil in enumerate(DILATIONS):
            rows, nb = _pattern_rows(t, dil)
            _, pattern_vjp = jax.vjp(functools.partial(_att_pattern, nb=nb), _take(qh_ref, rows, nb), _take(kh_ref, rows, nb),
                                     _take(v_ref, rows, nb))
            dqh, dkh, dv = pattern_vjp((_take(ct_refs[p], rows, nb), _take(ct_refs[3 + p], rows, nb)))
            _put_add(dqh_ref, rows, nb, dqh)
            _put_add(dkh_ref, rows, nb, dkh)
            _put_add(dv_ref, rows, nb, dv)

        def norm_cotangents(j, carry):
            rows = _token_rows(j)
            out = []
            for x_ref, gain, dh_ref, dx_ref, acc in ((q_ref, gq, dqh_ref, dq_ref, carry[0]), (k_ref, gk, dkh_ref, dk_ref, carry[1])):
                _, norm_vjp = jax.vjp(_head_norm, x_ref[0, rows, :], gain[0])
                dx, dgain = norm_vjp(dh_ref[0, rows, :])
                dx_ref[0, rows, :] = dx
                out.append(acc + dgain)
            return tuple(out)

        zero = jnp.zeros((1, dh), F32)
        dgq, dgk = lax.fori_loop(0, t // ATT_BLOCK, norm_cotangents, (zero, zero))
        dqn_ref[0] += dgq
        dkn_ref[0] += dgk

    hshape = jax.ShapeDtypeStruct(q.shape, F32)
    gshape = jax.ShapeDtypeStruct((1, 1, dh), F32)
    return pl.pallas_call(
        body, name="att_bwd", grid=(nh,), out_shape=(hshape, hshape, hshape, gshape, gshape),
        in_specs=[head, head, head, gain, gain] + [head] * 7, out_specs=(head, head, head, gain, gain),
        scratch_shapes=[pltpu.VMEM((1, t, dh), F32)] * 10, compiler_params=_params("arbitrary"),
    )(q, k, v, qn, kn, *saved, datt)


RWKV_VEC = ("mu_r", "mu_k", "mu_v", "mu_w", "mu_a", "mu_g", "w0", "a0", "k_k", "k_a")
RWKV_MAT = ("w1", "w2", "a1", "a2", "g1", "g2")


def _rwkv_pre_fn(cur, prev, vec, w1, w2, a1, a2, g1, g2):
    c = cur.shape[1] // 4
    mu_r, mu_k, mu_v, mu_w, mu_a, mu_g, w0, a0, k_k, k_a = (vec[j:j + 1] for j in range(10))

    def lerp(j, mu):
        xc, xp = cur[:, j * c:(j + 1) * c], prev[:, j * c:(j + 1) * c]
        return xc + (xp - xc) * mu

    r, k, v = lerp(0, mu_r), lerp(1, mu_k), lerp(2, mu_v)
    cw, ca, cg = lerp(3, mu_w), lerp(3, mu_a), lerp(3, mu_g)
    z = w0 + _mm(jnp.tanh(_mm(cw, w1)), w2)
    w_log = jnp.minimum(z, 0.0) - jnp.log(1.0 + jnp.exp(-jnp.abs(z))) - 0.5
    lw = -jnp.exp(w_log)
    a = _sigmoid(a0 + _mm(_mm(ca, a1), a2))
    gate = _mm(_sigmoid(_mm(cg, g1)), g2)
    kkraw = k * k_k
    kmod = k * (1.0 + (a - 1.0) * k_a)
    return r, lw, kmod, v, kkraw, a, gate


HALO_ROWS = 8


def _rwkv_pre_specs(c, mats, tile_of):
    tm = TOKEN_TILE
    nh = c // HEAD_DIM
    wide = pl.BlockSpec((tm, 4 * c), lambda j: (tile_of(j), 0))
    halo = pl.BlockSpec((HALO_ROWS, 4 * c), lambda j: (jnp.maximum(tile_of(j) * (tm // HALO_ROWS) - 1, 0), 0))
    one = pl.BlockSpec((tm, c), lambda j: (tile_of(j), 0))
    heads = pl.BlockSpec((nh, tm, HEAD_DIM), lambda j: (0, tile_of(j), 0))
    vec = pl.BlockSpec((10, c), lambda j: (0, 0))
    mspecs = [pl.BlockSpec(m.shape, lambda j: (0, 0)) for m in mats]
    return wide, halo, one, heads, vec, mspecs


def _previous_rows(cur, halo, tile):
    first = jnp.where(tile > 0, halo[HALO_ROWS - 1:HALO_ROWS], 0.0)
    rows = lax.broadcasted_iota(jnp.int32, cur.shape, 0)
    return jnp.where(rows == 0, first, pltpu.roll(cur, 1, axis=0))


def _rwkv_pre_fwd(cur, vec, mats):
    t, c4 = cur.shape
    c = c4 // 4
    wide, halo, one, heads, vspec, mspecs = _rwkv_pre_specs(c, mats, lambda j: j)

    def body(cur_ref, halo_ref, vec_ref, *rest):
        mrefs, outs = rest[:6], rest[6:]
        cur_v = cur_ref[...]
        prev = _previous_rows(cur_v, halo_ref[...], pl.program_id(0))
        vals = _rwkv_pre_fn(cur_v, prev, vec_ref[...], *(m[...] for m in mrefs))
        for ref, val in zip(outs[:6], vals[:6]):
            _store_heads(ref, val)
        outs[6][...] = vals[6]

    hshape = jax.ShapeDtypeStruct((c // HEAD_DIM, t, HEAD_DIM), F32)
    return pl.pallas_call(
        body, name="rwkv_pre_fwd", grid=(t // TOKEN_TILE,), out_shape=(hshape,) * 6 + (jax.ShapeDtypeStruct((t, c), F32),),
        in_specs=[wide, halo, vspec] + mspecs, out_specs=(heads,) * 6 + (one,), compiler_params=_params("arbitrary"),
    )(cur, cur, vec, *mats)


def _rwkv_pre_bwd(cur, vec, mats, cts, dgate):
    t, c4 = cur.shape
    c = c4 // 4
    tm = TOKEN_TILE
    nt = t // tm
    wide, halo, one, heads, vspec, mspecs = _rwkv_pre_specs(c, mats, lambda j: nt - 1 - j)

    def body(cur_ref, halo_ref, vec_ref, *rest):
        mrefs, ctrefs, dgate_ref, outs, carry_ref = rest[:6], rest[6:12], rest[12], rest[13:-1], rest[-1]
        j = pl.program_id(0)

        @pl.when(j == 0)
        def _():
            carry_ref[...] = jnp.zeros_like(carry_ref)
            for ref in outs[1:]:
                ref[...] = jnp.zeros_like(ref)

        cur_v = cur_ref[...]
        prev = _previous_rows(cur_v, halo_ref[...], nt - 1 - j)
        _, vjp = jax.vjp(_rwkv_pre_fn, cur_v, prev, vec_ref[...], *(m[...] for m in mrefs))
        grads = vjp(tuple(_load_heads(r) for r in ctrefs) + (dgate_ref[...],))
        dprev = grads[1]
        rows = lax.broadcasted_iota(jnp.int32, dprev.shape, 0)
        outs[0][...] = grads[0] + jnp.where(rows == tm - 1, carry_ref[0:1], pltpu.roll(dprev, tm - 1, axis=0))
        carry_ref[0:1] = dprev[0:1]
        for ref, val in zip(outs[1:], grads[2:]):
            ref[...] += val

    return pl.pallas_call(
        body, name="rwkv_pre_bwd", grid=(nt,),
        out_shape=(jax.ShapeDtypeStruct(cur.shape, F32), jax.ShapeDtypeStruct(vec.shape, F32))
        + tuple(jax.ShapeDtypeStruct(m.shape, F32) for m in mats),
        in_specs=[wide, halo, vspec] + mspecs + [heads] * 6 + [one], out_specs=(wide, vspec) + tuple(mspecs),
        scratch_shapes=[pltpu.VMEM((HALO_ROWS, c4), F32)], compiler_params=_params("arbitrary"),
    )(cur, cur, vec, *mats, *cts, dgate)


def _scan_chunk_fn(h0, r, lw, k, v, kkraw, a, rk, lnw, lnb):
    n = r.shape[1]
    nrm = jnp.sqrt(jnp.sum(kkraw * kkraw, axis=-1, keepdims=True))
    kk = kkraw / jnp.maximum(nrm, 1e-12)
    av, bv = -kk, kk * a
    ti = lax.broadcasted_iota(jnp.int32, (n, n), 0)
    si = lax.broadcasted_iota(jnp.int32, (n, n), 1)
    incl, strict = ti >= si, ti > si
    ones = jnp.broadcast_to(incl.astype(F32)[None], (r.shape[0], n, n))
    cum = _hdot(ones, lw, 2, 1)
    at, rt = av * jnp.exp(cum - lw), r * jnp.exp(cum)
    inv = jnp.exp(-cum)
    bt, kt = bv * inv, k * inv
    gram = _hdot(jnp.concatenate([at, rt], axis=1), jnp.concatenate([bt, kt], axis=1), 2, 2)
    lab = jnp.where(strict, gram[:, :n, :n], 0.0)
    lak = jnp.where(strict, gram[:, :n, n:], 0.0)
    rb = jnp.where(incl, gram[:, n:, :n], 0.0)
    rkm = jnp.where(incl, gram[:, n:, n:], 0.0)
    nv = v.shape[2]
    u = _bmm_nn(jnp.concatenate([at, lak], axis=2), jnp.concatenate([h0, v], axis=1))
    p = lab
    m = 2
    while m < n:
        both = _bmm_nn(p, jnp.concatenate([u, p], axis=2))
        u, p = u + both[:, :, :nv], both[:, :, nv:]
        m *= 2
    u = u + _bmm_nn(p, u)
    y = _bmm_nn(jnp.concatenate([rt, rb, rkm], axis=2), jnp.concatenate([h0, u, v], axis=1))
    last = jnp.exp(jnp.sum(lw, axis=1, keepdims=True))
    h1 = jnp.swapaxes(last, 1, 2) * (h0 + _bmm_tn(jnp.concatenate([bt, kt], axis=1), jnp.concatenate([u, v], axis=1)))
    mean = jnp.mean(y, axis=-1, keepdims=True)
    yc = y - mean
    var = jnp.mean(yc * yc, axis=-1, keepdims=True)
    yn = yc * lax.rsqrt(var + GN_EPS) * lnw + lnb
    bonus = jnp.sum(r * k * rk, axis=-1, keepdims=True) * v
    return yn + bonus, h1


SCAN_GROUP = 2


def _scan_group_fn(h0, r, lw, k, v, kkraw, a, rk, lnw, lnb):
    outs = []
    for j in range(SCAN_GROUP):
        rows = slice(j * SCAN_CHUNK, (j + 1) * SCAN_CHUNK)
        o, h0 = _scan_chunk_fn(h0, r[:, rows], lw[:, rows], k[:, rows], v[:, rows], kkraw[:, rows], a[:, rows], rk, lnw, lnb)
        outs.append(o)
    return jnp.concatenate(outs, axis=1), h0


def _scan_specs(h, t, dh, rev):
    n = SCAN_CHUNK * SCAN_GROUP
    nc = t // n
    pos = (lambda c: (0, nc - 1 - c, 0)) if rev else (lambda c: (0, c, 0))
    st = (lambda c: (nc - 1 - c, 0, 0, 0)) if rev else (lambda c: (c, 0, 0, 0))
    seq = pl.BlockSpec((h, n, dh), pos)
    par = pl.BlockSpec((h, 1, dh), lambda c: (0, 0, 0))
    state = pl.BlockSpec((1, h, dh, dh), st)
    return seq, par, state


def _scan_fwd(seqs, pars):
    h, t, dh = seqs[0].shape
    nc = t // (SCAN_CHUNK * SCAN_GROUP)
    seq, par, state = _scan_specs(h, t, dh, False)

    def body(r, lw, k, v, kkraw, a, rk, lnw, lnb, o_ref, st_ref, h_ref):
        @pl.when(pl.program_id(0) == 0)
        def _():
            h_ref[...] = jnp.zeros_like(h_ref)

        h0 = h_ref[...]
        st_ref[0] = h0
        o, h1 = _scan_group_fn(h0, r[...], lw[...], k[...], v[...], kkraw[...], a[...], rk[...], lnw[...], lnb[...])
        o_ref[...] = o
        h_ref[...] = h1

    return pl.pallas_call(
        body, name="rwkv_scan_fwd", grid=(nc,),
        out_shape=(jax.ShapeDtypeStruct((h, t, dh), F32), jax.ShapeDtypeStruct((nc, h, dh, dh), F32)),
        in_specs=[seq] * 6 + [par] * 3, out_specs=(seq, state),
        scratch_shapes=[pltpu.VMEM((h, dh, dh), F32)], compiler_params=_params("arbitrary"),
    )(*seqs, *pars)


def _scan_bwd(seqs, pars, states, do):
    h, t, dh = seqs[0].shape
    nc = t // (SCAN_CHUNK * SCAN_GROUP)
    seq, par, state = _scan_specs(h, t, dh, True)

    def body(r, lw, k, v, kkraw, a, rk, lnw, lnb, st_ref, do_ref, *rest):
        douts, dpars, dh_ref = rest[:6], rest[6:9], rest[9]
        first = pl.program_id(0) == 0

        @pl.when(first)
        def _():
            dh_ref[...] = jnp.zeros_like(dh_ref)

        _, vjp = jax.vjp(_scan_group_fn, st_ref[0], r[...], lw[...], k[...], v[...], kkraw[...], a[...],
                         rk[...], lnw[...], lnb[...])
        grads = vjp((do_ref[...], dh_ref[...]))
        dh_ref[...] = grads[0]
        for ref, val in zip(douts, grads[1:7]):
            ref[...] = val

        @pl.when(first)
        def _():
            for ref, val in zip(dpars, grads[7:]):
                ref[...] = val

        @pl.when(jnp.logical_not(first))
        def _():
            for ref, val in zip(dpars, grads[7:]):
                ref[...] += val

    sshape = jax.ShapeDtypeStruct((h, t, dh), F32)
    pshape = jax.ShapeDtypeStruct((h, 1, dh), F32)
    return pl.pallas_call(
        body, name="rwkv_scan_bwd", grid=(nc,), out_shape=(sshape,) * 6 + (pshape,) * 3,
        in_specs=[seq] * 6 + [par] * 3 + [state, seq], out_specs=(seq,) * 6 + (par,) * 3,
        scratch_shapes=[pltpu.VMEM((h, dh, dh), F32)], compiler_params=_params("arbitrary"),
    )(*seqs, *pars, states, do)


def _local_step(x, target, w, ex):
    w = dict(w)
    c = w["mu_r"].shape[-1]
    qn, kn = w["q_norm"].reshape(1, 1, HEAD_DIM), w["k_norm"].reshape(1, 1, HEAD_DIM)
    vec = jnp.concatenate([w[n].reshape(1, c) for n in RWKV_VEC], axis=0)
    pars = [w[n].reshape(-1, 1, HEAD_DIM) for n in ("r_k", "ln_x_w", "ln_x_b")]
    no_dep = jnp.zeros(DEP_SHAPE, F32)

    x1, gate1, up1 = _ffn_fwd(x, w["ffn1_norm"], w["ffn1_w_gate"], w["ffn1_w_up"], w["ffn1_w_down"], ex.first_dep, "ffn1_fwd")
    w.update(ex.mix_weights((x1,)))
    mats = [w[n] for n in RWKV_MAT]
    q, k, v, cur = _proj_fwd(x1, w["mix_norm"], w["w_in"], c)
    att, *saved = _att_fwd(q, k, v, qn, kn)
    pre = _rwkv_pre_fwd(cur, vec, mats)
    seqs, gate = pre[:6], pre[6]
    opg, states = _scan_fwd(seqs, pars)
    w.update(ex.out_weights((att, opg)))
    x2 = _mixout_fwd(x1, att, opg, gate, w["w_out"])
    dy, gate2, up2, loss = _ffn_fwd(x2, w["ffn2_norm"], w["ffn2_w_gate"], w["ffn2_w_up"], w["ffn2_w_down"], no_dep, "ffn2_fwd",
                                    target=target)

    g = {}
    dx2, g["ffn2_norm"], g["ffn2_w_gate"], g["ffn2_w_up"], g["ffn2_w_down"] = _ffn_bwd(
        x2, w["ffn2_norm"], w["ffn2_w_gate"], w["ffn2_w_up"], w["ffn2_w_down"], gate2, up2, dy, no_dep, "ffn2_bwd")
    dep = ex.send_ffn2({n: g[n] for n in ("ffn2_w_gate", "ffn2_w_up", "ffn2_w_down")})
    datt, dopg, dgate, g["w_out"] = _mixout_bwd(att, opg, gate, w["w_out"], dx2, dep)
    dscan = _scan_bwd(seqs, pars, states, dopg)
    for n, d in zip(("r_k", "ln_x_w", "ln_x_b"), dscan[6:]):
        g[n] = d
    dcur, dvec, *dmats = _rwkv_pre_bwd(cur, vec, mats, dscan[:6], dgate)
    for n, d in zip(RWKV_MAT, dmats):
        g[n] = d
    for j, n in enumerate(RWKV_VEC):
        g[n] = dvec[j:j + 1]
    dq, dk, dv, g["q_norm"], g["k_norm"] = _att_bwd(q, k, v, qn, kn, saved, datt)
    dx1, g["mix_norm"], g["w_in"] = _proj_bwd(x1, w["mix_norm"], w["w_in"], dq, dk, dv, dcur, dx2)
    dep = ex.send_mix({n: g[n] for n in ("w_in", "w_out") + RWKV_MAT}, (dx1,))
    dx, g["ffn1_norm"], g["ffn1_w_gate"], g["ffn1_w_up"], g["ffn1_w_down"] = _ffn_bwd(
        x, w["ffn1_norm"], w["ffn1_w_gate"], w["ffn1_w_up"], w["ffn1_w_down"], gate1, up1, dx1, dep, "ffn1_bwd")
    return loss, dx, g


N_SHARDS = 4


def _place():
    return lax.axis_index("x"), lax.axis_index("y"), lax.axis_index("c")


def _chip_peers(x, y):
    return [(1 - x, y), (x, 1 - y), (1 - x, 1 - y)]


HBM = pl.BlockSpec(memory_space=pltpu.HBM)
SEM = pl.BlockSpec(memory_space=pltpu.SEMAPHORE)
DEP_SHAPE = (8, 128)


class _Views:
    to_sibling = False


class _GatherViews(_Views):
    @staticmethod
    def send(i, srcs, lands, k, at):
        return srcs[i], lands[i].at[at[3]]

    @staticmethod
    def landing(i, srcs, lands, k, at):
        return srcs[i], lands[i].at[2 * at[4] + at[5]]


class _ScatterViews(_Views):
    @staticmethod
    def send(i, srcs, lands, k, at):
        return srcs[i].at[2 * at[4] + at[5]], lands[i].at[k]

    @staticmethod
    def landing(i, srcs, lands, k, at):
        return srcs[i].at[at[3]], lands[i].at[k]


def _half_rows(ref, slot, half):
    rows = ref.shape[1] // 2
    return ref.at[slot, pl.ds(pl.multiple_of(half * rows, BF16_SUBLANES), rows)]


class _HalfGatherViews(_Views):
    @staticmethod
    def send(i, srcs, lands, k, at):
        rows = srcs[i].shape[0] // 2
        return srcs[i].at[pl.ds(pl.multiple_of(at[2] * rows, BF16_SUBLANES), rows)], _half_rows(lands[i], at[3], at[2])

    @staticmethod
    def landing(i, srcs, lands, k, at):
        rows = srcs[i].shape[0] // 2
        return srcs[i].at[pl.ds(pl.multiple_of(at[2] * rows, BF16_SUBLANES), rows)], _half_rows(lands[i], 2 * at[4] + at[5], at[2])


class _ForwardViews(_Views):
    to_sibling = True

    @staticmethod
    def send(i, srcs, lands, k, at):
        mine = _half_rows(lands[i], 2 * at[4] + at[5], at[2])
        return mine, mine

    @staticmethod
    def landing(i, srcs, lands, k, at):
        theirs = _half_rows(lands[i], 2 * at[4] + at[5], 1 - at[2])
        return theirs, theirs


def _push_start(srcs, lands, views, after, name):
    ns, nl = len(srcs), len(lands)

    def body(*refs):
        src_refs, land_refs = refs[:ns], refs[ns:ns + nl]
        send_sems, recv_sems = refs[ns + nl + 1:ns + nl + 3]
        token = refs[2 * (ns + nl) + 3]
        x, y, c = _place()
        for i in range(nl):
            for k, (px, py) in enumerate(_chip_peers(x, y)):
                src, dst = views.send(i, src_refs, land_refs, k, (x, y, c, 2 * x + y, px, py))
                pltpu.make_async_remote_copy(
                    src_ref=src, dst_ref=dst, send_sem=send_sems.at[3 * i + k], recv_sem=recv_sems.at[3 * i + k],
                    device_id=(x, y, 1 - c) if views.to_sibling else (px, py, c), device_id_type=MESH).start()
        token[...] = jnp.zeros_like(token)

    sems = pltpu.SemaphoreType.DMA((3 * nl,))
    both = [pltpu.with_memory_space_constraint(a, pltpu.HBM) for a in (*srcs, *lands)]
    outs = pl.pallas_call(
        body, name=name,
        out_shape=(sems, sems, *[pltpu.HBM(a.shape, a.dtype) for a in both], jax.ShapeDtypeStruct(DEP_SHAPE, F32)),
        in_specs=[HBM] * (ns + nl) + [ANY], out_specs=(SEM, SEM, *[HBM] * (ns + nl), VMEM_FULL),
        input_output_aliases={i: 2 + i for i in range(ns + nl)},
        compiler_params=pltpu.CompilerParams(has_side_effects=pltpu.SideEffectType.DATAFLOW_SIDE_EFFECTING),
    )(*both, after)
    return outs[0], outs[1], outs[2:2 + ns], outs[2 + ns:2 + ns + nl], outs[2 + ns + nl]


def _push_wait(started, views, after, name, with_sources=False):
    send_sems, recv_sems, srcs, lands, _ = started
    ns, nl = len(srcs), len(lands)

    def body(*refs):
        src_refs, land_refs = refs[:ns], refs[ns:ns + nl]
        send_sems, recv_sems = refs[ns + nl:ns + nl + 2]
        x, y, c = _place()
        for i in range(nl):
            for k, (px, py) in enumerate(_chip_peers(x, y)):
                src, dst = views.landing(i, src_refs, land_refs, k, (x, y, c, 2 * x + y, px, py))
                landing = pltpu.make_async_remote_copy(
                    src_ref=src, dst_ref=dst, send_sem=send_sems.at[3 * i + k], recv_sem=recv_sems.at[3 * i + k],
                    device_id=(x, y, 1 - c) if views.to_sibling else (px, py, c), device_id_type=MESH)
                landing.wait_send()
                landing.wait_recv()

    outs = pl.pallas_call(
        body, name=name,
        out_shape=tuple(pltpu.HBM(a.shape, a.dtype) for a in (*srcs, *lands)),
        in_specs=[HBM] * (ns + nl) + [SEM, SEM] + [ANY] * len(after), out_specs=(HBM,) * (ns + nl),
        input_output_aliases={i: i for i in range(ns + nl)},
        compiler_params=pltpu.CompilerParams(has_side_effects=pltpu.SideEffectType.DATAFLOW_SIDE_EFFECTING),
    )(*srcs, *lands, send_sems, recv_sems, *after)
    return outs if with_sources else outs[ns:]


def _empty_lands(shards, slots, own_slot):
    lands = [lax.empty((slots,) + s.shape, s.dtype) for s in shards]
    if own_slot:
        me = 2 * lax.axis_index("x") + lax.axis_index("y")
        lands = [lax.dynamic_update_index_in_dim(z, s, me, 0) for z, s in zip(lands, shards)]
    return lands


def _sibling_swap(arrays, name, other_half=False):
    n = len(arrays)

    def body(*refs):
        ins, outs = refs[:n], refs[n:2 * n]
        send_sems, recv_sems = refs[2 * n:]
        x, y, c = _place()
        copies = []
        for i in range(n):
            src = ins[i]
            if other_half:
                rows = src.shape[1] // 2
                src = src.at[:, pl.ds(pl.multiple_of((1 - c) * rows, BF16_SUBLANES), rows)]
            cp = pltpu.make_async_remote_copy(
                src_ref=src, dst_ref=outs[i], send_sem=send_sems.at[i], recv_sem=recv_sems.at[i],
                device_id=(x, y, 1 - c), device_id_type=MESH)
            cp.start()
            copies.append(cp)
        for cp in copies:
            cp.wait()

    shapes = [(a.shape[0], a.shape[1] // 2, a.shape[2]) if other_half else a.shape for a in arrays]
    return pl.pallas_call(
        body, name=name,
        out_shape=tuple(jax.ShapeDtypeStruct(s, a.dtype) for s, a in zip(shapes, arrays)),
        in_specs=[ANY] * n, out_specs=(ANY,) * n,
        scratch_shapes=[pltpu.SemaphoreType.DMA((n,)), pltpu.SemaphoreType.DMA((n,))],
    )(*arrays)


FOLD_STEPS = 2


def _fold_add(core, parts, theirs, name):
    n = len(parts)
    s, r, cols = parts[0].shape
    tr = r // 2 // FOLD_STEPS

    def body(core_ref, *refs):
        for p_ref, t_ref, o_ref in zip(refs[:n], refs[n:2 * n], refs[2 * n:]):
            o_ref[...] = (p_ref[...].astype(F32) + t_ref[...].astype(F32)).astype(BF16)

    half = pl.BlockSpec((1, tr, cols), lambda j, i, core_ref: (j, i, 0))
    return pl.pallas_call(
        body, name=name, out_shape=tuple(jax.ShapeDtypeStruct((s, r // 2, cols), BF16) for _ in parts),
        grid_spec=pltpu.PrefetchScalarGridSpec(
            num_scalar_prefetch=1, grid=(s, FOLD_STEPS),
            in_specs=[pl.BlockSpec((1, tr, cols), lambda j, i, core_ref: (j, core_ref[0] * FOLD_STEPS + i, 0))] * n + [half] * n,
            out_specs=(half,) * n),
        compiler_params=_params("arbitrary", "arbitrary"),
    )(core, *parts, *theirs)


N_DEV = 8


def _allreduce_small(pack):
    def body(in_ref, out_ref, buf, send_sems, recv_sems):
        x, y, c = _place()
        me = 4 * x + 2 * y + c
        buf[me] = in_ref[...]

        def copy(j, slot):
            px, py, pc = x ^ (j >> 2), y ^ ((j >> 1) & 1), c ^ (j & 1)
            return pltpu.make_async_remote_copy(
                src_ref=in_ref, dst_ref=buf.at[slot(px, py, pc)], send_sem=send_sems.at[j], recv_sem=recv_sems.at[j],
                device_id=(px, py, pc), device_id_type=MESH)

        for j in range(1, N_DEV):
            copy(j, lambda px, py, pc: me).start()
        for j in range(1, N_DEV):
            landing = copy(j, lambda px, py, pc: 4 * px + 2 * py + pc)
            landing.wait_send()
            landing.wait_recv()
        acc = buf[0]
        for s in range(1, N_DEV):
            acc = acc + buf[s]
        out_ref[...] = acc

    return pl.pallas_call(
        body, name="allreduce_small", out_shape=jax.ShapeDtypeStruct(pack.shape, F32),
        in_specs=[VMEM_FULL], out_specs=VMEM_FULL,
        scratch_shapes=[pltpu.VMEM((N_DEV,) + pack.shape, F32), pltpu.SemaphoreType.DMA((N_DEV,)),
                        pltpu.SemaphoreType.DMA((N_DEV,))],
    )(pack)


BF16_SUBLANES = 16


def _reduce_own(me, parts, recvs, dep, steps, name):
    n = len(parts)

    def body(me_ref, *refs):
        for p_ref, rv_ref, o_ref in zip(refs[:n], refs[n:2 * n], refs[2 * n + 1:]):
            acc = p_ref[0].astype(F32)
            for k in range(3):
                acc = acc + rv_ref[k].astype(F32)
            o_ref[...] = acc

    shapes = [(p.shape[1] // steps, p.shape[2]) for p in parts]
    return pl.pallas_call(
        body, name=name, out_shape=tuple(jax.ShapeDtypeStruct(p.shape[1:], F32) for p in parts),
        grid_spec=pltpu.PrefetchScalarGridSpec(
            num_scalar_prefetch=1, grid=(steps,),
            in_specs=[pl.BlockSpec((1, tr, c), lambda i, me_ref: (me_ref[0], i, 0)) for tr, c in shapes]
            + [pl.BlockSpec((3, tr, c), lambda i, me_ref: (0, i, 0)) for tr, c in shapes] + [ANY],
            out_specs=tuple(pl.BlockSpec((tr, c), lambda i, me_ref: (i, 0)) for tr, c in shapes)),
        compiler_params=_params("arbitrary"),
    )(me, *parts, *recvs, dep)


def _adamw(ws, gas, gbs, ms, vs, steps, name):
    n = len(ws)
    c1 = 1.0 - ADAM_B1 ** ADAM_STEP
    c2 = 1.0 - ADAM_B2 ** ADAM_STEP
    operands = [ws, gas, ms, vs] if gbs is None else [ws, gas, gbs, ms, vs]
    k = len(operands)

    def body(*refs):
        ins, outs = refs[:k * n], refs[k * n:]
        for j in range(n):
            w_ref, ga_ref, *gb_ref, m_ref, v_ref = ins[j::n]
            g_out, d_out, m_out, v_out = outs[j::n]
            g = ga_ref[...] + gb_ref[0][...] if gb_ref else ga_ref[...]
            mn = ADAM_B1 * m_ref[...] + (1.0 - ADAM_B1) * g
            vn = ADAM_B2 * v_ref[...] + (1.0 - ADAM_B2) * (g * g)
            g_out[...] = g
            m_out[...] = mn
            v_out[...] = vn
            d_out[...] = -ADAM_LR * ((mn / c1) / (jnp.sqrt(vn / c2) + ADAM_EPS) + ADAM_WD * w_ref[...])

    tiles = [pl.BlockSpec((w.shape[0] // steps, w.shape[1]), lambda i: (i, 0)) for w in ws]
    shapes = [jax.ShapeDtypeStruct(w.shape, F32) for w in ws]
    outs = pl.pallas_call(
        body, name=name, grid=(steps,), out_shape=tuple(shapes * 4), in_specs=tiles * k, out_specs=tuple(tiles * 4),
        compiler_params=_params("arbitrary"),
    )(*[a for group in operands for a in group])
    return [outs[j::n] for j in range(n)]


PACK_COLS = 512


def _to_rows(a):
    flat = a.reshape(-1)
    pad = (-flat.shape[0]) % PACK_COLS
    return jnp.pad(flat, (0, pad)).reshape(-1, PACK_COLS)


def _pack(arrays, extra_rows=0):
    rows = [_to_rows(a) for a in arrays]
    n = sum(r.shape[0] for r in rows) + extra_rows
    pad = (-n) % 8
    return jnp.concatenate(rows + [jnp.zeros((extra_rows + pad, PACK_COLS), F32)], axis=0)


def _unpack(pack, like):
    out, at = [], 0
    for a in like:
        n = -(-a.size // PACK_COLS)
        out.append(pack[at:at + n].reshape(-1)[:a.size].reshape(a.shape))
        at += n
    return out


COL_SHARDED = ("ffn1_w_gate", "ffn1_w_up", "w_in", "ffn2_w_gate", "ffn2_w_up", "w2", "a2", "g2")
ROW_SHARDED = ("ffn1_w_down", "ffn2_w_down", "w_out", "w1", "a1", "g1")
CHUNKED = ("ffn1_w_gate", "ffn1_w_up", "ffn1_w_down", "w_in", "ffn2_w_gate", "ffn2_w_up", "ffn2_w_down")
WEIGHTS = ("ffn1_norm", "ffn1_w_gate", "ffn1_w_up", "ffn1_w_down", "mix_norm", "w_in", "q_norm", "k_norm",
           "mu_r", "mu_k", "mu_v", "mu_w", "mu_a", "mu_g", "w0", "w1", "w2", "a0", "a1", "a2", "g1", "g2",
           "k_k", "k_a", "r_k", "ln_x_w", "ln_x_b", "w_out", "ffn2_norm", "ffn2_w_gate", "ffn2_w_up", "ffn2_w_down")


TRANSPOSED = ("ffn1_w_gate", "ffn1_w_up", "ffn2_w_gate", "ffn2_w_up")


def _shard_2d(name, a):
    return a[0].T if name in TRANSPOSED else a[0]


def _full_from_blocks(name, blocks):
    if name in CHUNKED:
        return blocks
    if name in ROW_SHARDED:
        return blocks.reshape(-1, blocks.shape[-1])
    return blocks.transpose(1, 0, 2).reshape(blocks.shape[1], -1)


def _blocks_from_full(name, full):
    if name in CHUNKED:
        return full
    if name in ROW_SHARDED:
        return full.reshape(N_SHARDS, -1, full.shape[-1])
    return full.reshape(full.shape[0], N_SHARDS, -1).transpose(1, 0, 2)


FFN1_GROUP = ("ffn1_w_gate", "ffn1_w_up", "ffn1_w_down")
MIX_GROUP = ("w_in",) + RWKV_MAT
OUT_GROUP = ("w_out", "ffn2_w_gate", "ffn2_w_up", "ffn2_w_down")
FFN2_GROUP = OUT_GROUP[1:]
LATE_GROUP = ("w_in", "w_out") + RWKV_MAT


class _Exchange:
    def __init__(self, given):
        self.given = given
        first = self._gather_start(FFN1_GROUP, _HalfGatherViews, jnp.zeros(DEP_SHAPE, F32), "gather_ffn1_start")
        self.mix = self._gather_start(MIX_GROUP, _GatherViews, first[4], "gather_mix_start")
        self.out = self._gather_start(OUT_GROUP, _GatherViews, self.mix[4], "gather_out_start")
        self.first_dep = self.out[4]
        halves = _push_wait(first, _HalfGatherViews, (self.first_dep,), "gather_ffn1_wait")
        passed = _push_start([], halves, _ForwardViews, jnp.zeros(DEP_SHAPE, F32), "gather_ffn1_pass_start")
        self.first_weights = self._full(FFN1_GROUP, _push_wait(passed, _ForwardViews, (passed[4],), "gather_ffn1_pass_wait"))
        self.parts, self.recv = {}, {}

    @staticmethod
    def _full(names, blocks):
        out = {}
        for n, b in zip(names, blocks):
            full = _full_from_blocks(n, b)
            out[n] = full.astype(F32) if n in RWKV_MAT else full
        return out

    def _gather_start(self, names, views, after, name):
        after, raw = lax.optimization_barrier((after, [_shard_2d(n, self.given[n]) for n in names]))
        shards = [a.astype(BF16) for a in raw]
        return _push_start(shards, _empty_lands(shards, N_SHARDS, True), views, after, name)

    def mix_weights(self, after):
        return self._full(MIX_GROUP, _push_wait(self.mix, _GatherViews, after, "gather_mix_wait"))

    def out_weights(self, after):
        return self._full(OUT_GROUP, _push_wait(self.out, _GatherViews, after, "gather_out_wait"))

    def _scatter_start(self, grads, name):
        names = tuple(grads)
        parts = [_blocks_from_full(n, grads[n]) for n in names]
        self.parts.update(zip(names, parts))
        lands = [lax.empty((3,) + p.shape[1:], BF16) for p in parts]
        return _push_start([p.astype(BF16) for p in parts], lands, _ScatterViews, jnp.zeros(DEP_SHAPE, F32), name)

    def _scatter_done(self, started, names, after, name):
        outs = _push_wait(started, _ScatterViews, after, name, with_sources=True)
        for n, sent, got in zip(names, outs[:len(names)], outs[len(names):]):
            self.recv[n] = got
            if self.parts[n].dtype == BF16:
                self.parts[n] = sent

    def send_ffn2(self, grads):
        self.ffn2 = self._scatter_start(grads, "scatter_ffn2_start")
        return self.ffn2[4]

    def send_mix(self, grads, after):
        self._scatter_done(self.ffn2, FFN2_GROUP, after, "scatter_ffn2_wait")
        self.late = self._scatter_start(grads, "scatter_late_start")
        return self.late[4]

    def send_ffn1(self, grads):
        self.ffn1 = self._scatter_start(grads, "scatter_ffn1_start")
        return self.ffn1[4]

    def late_received(self, after):
        self._scatter_done(self.late, LATE_GROUP, after, "scatter_late_wait")

    def ffn1_received(self, after):
        self._scatter_done(self.ffn1, FFN1_GROUP, after, "scatter_ffn1_wait")


def kernel(
        x, ffn1_norm, ffn1_w_gate, ffn1_w_up, ffn1_w_down, mix_norm, w_in, q_norm, k_norm, mu_r, mu_k, mu_v, mu_w,
        mu_a, mu_g, w0, w1, w2, a0, a1, a2, g1, g2, k_k, k_a, r_k, ln_x_w, ln_x_b, w_out, ffn2_norm, ffn2_w_gate,
        ffn2_w_up, ffn2_w_down, loss_target, m_ffn1_norm, m_ffn1_w_gate, m_ffn1_w_up, m_ffn1_w_down, m_mix_norm,
        m_w_in, m_q_norm, m_k_norm, m_mu_r, m_mu_k, m_mu_v, m_mu_w, m_mu_a, m_mu_g, m_w0, m_w1, m_w2, m_a0, m_a1,
        m_a2, m_g1, m_g2, m_k_k, m_k_a, m_r_k, m_ln_x_w, m_ln_x_b, m_w_out, m_ffn2_norm, m_ffn2_w_gate, m_ffn2_w_up,
        m_ffn2_w_down, v_ffn1_norm, v_ffn1_w_gate, v_ffn1_w_up, v_ffn1_w_down, v_mix_norm, v_w_in, v_q_norm, v_k_norm,
        v_mu_r, v_mu_k, v_mu_v, v_mu_w, v_mu_a, v_mu_g, v_w0, v_w1, v_w2, v_a0, v_a1, v_a2, v_g1, v_g2, v_k_k, v_k_a,
        v_r_k, v_ln_x_w, v_ln_x_b, v_w_out, v_ffn2_norm, v_ffn2_w_gate, v_ffn2_w_up, v_ffn2_w_down):
    given = dict(locals())
    sharded = COL_SHARDED + ROW_SHARDED
    sharded = tuple(n for n in WEIGHTS if n in sharded)
    small = tuple(n for n in WEIGHTS if n not in sharded)

    ex = _Exchange(given)
    w = {n: given[n] for n in small}
    w.update(ex.first_weights)
    loss, dx, g = _local_step(x[0], loss_target[0], w, ex)

    core = lax.axis_index("c").astype(jnp.int32).reshape(1)
    late = [g[n] for n in FFN1_GROUP]
    folded = _fold_add(core, late, _sibling_swap(late, "fold_swap_ffn1", other_half=True), "fold_add_ffn1")
    dep = ex.send_ffn1(dict(zip(FFN1_GROUP, folded)))

    me = (2 * lax.axis_index("x") + lax.axis_index("y")).astype(jnp.int32).reshape(1)
    out = {}

    def settle(names, dep, tag):
        done = []
        for kind, sub, r_steps, a_steps in (("large", tuple(n for n in names if n not in RWKV_MAT), 4, 8),
                                            ("small", tuple(n for n in names if n in RWKV_MAT), 1, 1)):
            if not sub:
                continue
            parts = [ex.parts[n].reshape(N_SHARDS, -1, ex.parts[n].shape[-1]) for n in sub]
            recvs = [ex.recv[n].reshape(3, -1, ex.recv[n].shape[-1]) for n in sub]
            mine = _reduce_own(me, parts, recvs, dep, r_steps, f"reduce_{tag}_{kind}")
            theirs = _sibling_swap(mine, f"sibling_swap_{tag}_{kind}")
            res = _adamw([_shard_2d(n, given[n]) for n in sub], mine, theirs, [_shard_2d(n, given["m_" + n]) for n in sub],
                         [_shard_2d(n, given["v_" + n]) for n in sub], a_steps, f"adamw_{tag}_{kind}")
            for n, rs in zip(sub, res):
                out[n] = [(r.T if n in TRANSPOSED else r).reshape(given[n].shape) for r in rs]
                done.append(out[n][1])
        return tuple(done)

    ex.late_received((dep,))
    last = settle(tuple(n for n in sharded if n not in FFN1_GROUP), dep, "rest")

    gpack = _pack([g[n] for n in small], extra_rows=1)
    n_rows = sum(-(-given[n].size // PACK_COLS) for n in small)
    gpack = gpack.at[n_rows, :loss.shape[1]].set(loss[0])
    gsum = _allreduce_small(gpack)
    res = _adamw([_pack([given[n] for n in small], 1)], [gsum], None, [_pack([given["m_" + n] for n in small], 1)],
                 [_pack([given["v_" + n] for n in small], 1)], 1, "adamw_replicated")[0]
    like = [given[n] for n in small]
    for j, r in enumerate(res):
        for n, a in zip(small, _unpack(r, like)):
            out.setdefault(n, [None] * 4)[j] = a
    total_loss = gsum[n_rows, 0]

    ex.ffn1_received((*last, res[1]))
    halves = _reduce_own(me, [ex.parts[n] for n in FFN1_GROUP], [ex.recv[n] for n in FFN1_GROUP],
                         jnp.zeros(DEP_SHAPE, F32), FOLD_STEPS, "reduce_ffn1")
    others = _sibling_swap(halves, "sibling_swap_ffn1")
    first = lax.axis_index("c") == 0
    grads = [jnp.concatenate([jnp.where(first, a, b), jnp.where(first, b, a)], axis=0) for a, b in zip(halves, others)]
    res = _adamw([_shard_2d(n, given[n]) for n in FFN1_GROUP], grads, None, [_shard_2d(n, given["m_" + n]) for n in FFN1_GROUP],
                 [_shard_2d(n, given["v_" + n]) for n in FFN1_GROUP], 8, "adamw_ffn1")
    for n, rs in zip(FFN1_GROUP, res):
        out[n] = [(r.T if n in TRANSPOSED else r).reshape(given[n].shape) for r in rs]
    return (total_loss, dx[None], *[out[n][0] for n in WEIGHTS], *[out[n][1] for n in WEIGHTS],
            *[out[n][2] for n in WEIGHTS], *[out[n][3] for n in WEIGHTS])
```

```python
import functools

import jax
import jax.numpy as jnp
from jax import lax
from jax.experimental import pallas as pl
from jax.experimental.pallas import tpu as pltpu

F32 = jnp.float32
BF16 = jnp.bfloat16
MESH = pl.DeviceIdType.MESH

RMS_EPS = 1e-6
GN_EPS = 64e-5
NEG_INF = -1e30
FFN_RESIDUAL = 0.5
HEAD_DIM = 64
ATT_BLOCK = 128
DILATIONS = (1, 4, 16)
SCAN_CHUNK = 64
TOKEN_TILE = 256
FFN_BWD_TILE = 512

ADAM_LR = 0.001
ADAM_B1 = 0.9
ADAM_B2 = 0.999
ADAM_EPS = 1e-08
ADAM_WD = 0.01
ADAM_STEP = 10

VMEM_FULL = pl.BlockSpec(memory_space=pltpu.VMEM)
ANY = pl.BlockSpec(memory_space=pl.ANY)


VMEM_LIMIT = 56 * 1024 * 1024


def _params(*sem):
    return pltpu.CompilerParams(dimension_semantics=sem, vmem_limit_bytes=VMEM_LIMIT)


def _dot(a, b, dims):
    return lax.dot_general(a.astype(BF16), b.astype(BF16), (dims, ((), ())), preferred_element_type=F32)


def _dot_nn(a, b):
    return _dot(a, b, ((1,), (0,)))


def _dot_nt(a, b):
    return _dot(a, b, ((1,), (1,)))


def _dot_tn(a, b):
    return _dot(a, b, ((0,), (0,)))


@jax.custom_vjp
def _mm(a, b):
    return _dot_nn(a, b)


def _mm_fwd(a, b):
    return _dot_nn(a, b), (a, b)


def _mm_bwd(res, g):
    a, b = res
    return _dot_nt(g, b).astype(a.dtype), _dot_tn(a, g).astype(b.dtype)


_mm.defvjp(_mm_fwd, _mm_bwd)


def _bdot(a, b, ca, cb):
    return lax.dot_general(a.astype(BF16), b.astype(BF16), (((ca,), (cb,)), ((0,), (0,))), preferred_element_type=F32)


@jax.custom_vjp
def _bmm_nt(a, b):
    return _bdot(a, b, 2, 2)


def _bmm_nt_fwd(a, b):
    return _bdot(a, b, 2, 2), (a, b)


def _bmm_nt_bwd(res, g):
    a, b = res
    return _bdot(g, b, 2, 1), _bdot(g, a, 1, 1)


_bmm_nt.defvjp(_bmm_nt_fwd, _bmm_nt_bwd)


@jax.custom_vjp
def _bmm_nn(a, b):
    return _bdot(a, b, 2, 1)


def _bmm_nn_fwd(a, b):
    return _bdot(a, b, 2, 1), (a, b)


def _bmm_nn_bwd(res, g):
    a, b = res
    return _bdot(g, b, 2, 2), _bdot(a, g, 1, 1)


_bmm_nn.defvjp(_bmm_nn_fwd, _bmm_nn_bwd)


@jax.custom_vjp
def _bmm_tn(a, b):
    return _bdot(a, b, 1, 1)


def _bmm_tn_fwd(a, b):
    return _bdot(a, b, 1, 1), (a, b)


def _bmm_tn_bwd(res, g):
    a, b = res
    return _bdot(b, g, 2, 2), _bdot(a, g, 2, 1)


_bmm_tn.defvjp(_bmm_tn_fwd, _bmm_tn_bwd)


def _hdot(a, b, ca, cb):
    return lax.dot_general(a, b, (((ca,), (cb,)), ((0,), (0,))), precision=lax.Precision.HIGH, preferred_element_type=F32)


def _sigmoid(x):
    return 1.0 / (1.0 + jnp.exp(-x))


def _rms(x):
    return lax.rsqrt(jnp.mean(x * x, axis=-1, keepdims=True) + RMS_EPS)


def _ffn_fwd(x, norm, wg, wu, wd, dep, name, target=None):
    t, d = x.shape
    nc, fc, _ = wg.shape
    tm = FFN_BWD_TILE

    def body(x_ref, n_ref, wg_ref, wu_ref, wd_ref, dep_ref, *rest):
        o_ref, g_ref, u_ref = rest[-3:] if target is None else rest[1:4]
        xv = x_ref[...]
        h = (xv * _rms(xv) * n_ref[...]).astype(BF16)
        acc = jnp.zeros((tm, d), F32)
        for c in range(nc):
            g = _dot_nt(h, wg_ref[c])
            u = _dot_nt(h, wu_ref[c])
            g_ref[c] = g.astype(BF16)
            u_ref[c] = u.astype(BF16)
            a = (g * _sigmoid(g) * u).astype(BF16)
            acc = acc + jnp.dot(a, wd_ref[c], preferred_element_type=F32)
        y = xv + FFN_RESIDUAL * acc
        if target is None:
            o_ref[...] = y
        else:
            t_ref, loss_ref = rest[0], rest[4]
            err = y - t_ref[...]
            o_ref[...] = err * (1.0 / d)
            part = 0.5 * jnp.sum(jnp.mean(err * err, axis=-1, keepdims=True), axis=0, keepdims=True)

            @pl.when(pl.program_id(0) == 0)
            def _():
                loss_ref[...] = jnp.zeros_like(loss_ref)

            loss_ref[...] += jnp.broadcast_to(part, loss_ref.shape)

    tile = pl.BlockSpec((tm, d), lambda i: (i, 0))
    hidden = pl.BlockSpec((nc, tm, fc), lambda i: (0, i, 0))
    hshape = jax.ShapeDtypeStruct((nc, t, fc), BF16)
    with_loss = target is not None
    return pl.pallas_call(
        body, name=name, grid=(t // tm,),
        out_shape=(jax.ShapeDtypeStruct((t, d), F32), hshape, hshape) + ((jax.ShapeDtypeStruct((1, 128), F32),) if with_loss else ()),
        in_specs=[tile, pl.BlockSpec((1, d), lambda i: (0, 0)), VMEM_FULL, VMEM_FULL, VMEM_FULL, ANY] + ([tile] if with_loss else []),
        out_specs=(tile, hidden, hidden) + ((pl.BlockSpec((1, 128), lambda i: (0, 0)),) if with_loss else ()),
        compiler_params=_params("arbitrary"),
    )(x, norm, wg, wu, wd, dep, *((target,) if with_loss else ()))


def _rmsnorm_bwd(xv, gain, dh):
    rs = _rms(xv)
    xn = xv * rs
    dxn = dh * gain
    dx = rs * (dxn - xn * jnp.mean(dxn * xn, axis=-1, keepdims=True))
    return dx, jnp.sum(dh * xn, axis=0, keepdims=True)


def _ffn_bwd(x, norm, wg, wu, wd, gate, up, dy, dep, name):
    t, d = x.shape
    nc, fc, _ = wg.shape
    tm = FFN_BWD_TILE
    nt = t // tm

    def body(x_ref, n_ref, wg_ref, wu_ref, wd_ref, g_ref, u_ref, dy_ref, dep_ref, dx_ref, dn_ref, dwg_ref, dwu_ref,
             dwd_ref, dh_ref, ag_ref, au_ref, ad_ref):
        c, i = pl.program_id(0), pl.program_id(1)
        rows = pl.ds(pl.multiple_of(i * tm, tm), tm)
        xv = x_ref[...]
        gain = n_ref[...]
        h = (xv * _rms(xv) * gain).astype(BF16)
        dy = dy_ref[...]
        dyb = (FFN_RESIDUAL * dy).astype(BF16)
        g = g_ref[0].astype(F32)
        u = u_ref[0].astype(F32)
        sg = _sigmoid(g)
        s = g * sg
        a = (s * u).astype(BF16)
        da = _dot_nt(dyb, wd_ref[0])
        dub = (da * s).astype(BF16)
        dgb = (da * u * (sg * (1.0 + g * (1.0 - sg)))).astype(BF16)
        dwd_c = _dot_tn(a, dyb)
        dwg_c = _dot_tn(dgb, h)
        dwu_c = _dot_tn(dub, h)
        dh_c = _dot_nn(dgb, wg_ref[0]) + _dot_nn(dub, wu_ref[0])

        @pl.when(i == 0)
        def _():
            ad_ref[...] = dwd_c
            ag_ref[...] = dwg_c
            au_ref[...] = dwu_c

        @pl.when(i > 0)
        def _():
            ad_ref[...] += dwd_c
            ag_ref[...] += dwg_c
            au_ref[...] += dwu_c

        @pl.when(i == nt - 1)
        def _():
            dwd_ref[0] = ad_ref[...].astype(BF16)
            dwg_ref[0] = ag_ref[...].astype(BF16)
            dwu_ref[0] = au_ref[...].astype(BF16)

        @pl.when(c == 0)
        def _():
            dh_ref[rows, :] = dh_c

        @pl.when(c > 0)
        def _():
            dh_ref[rows, :] += dh_c

        @pl.when(c == nc - 1)
        def _():
            dx, dn = _rmsnorm_bwd(xv, gain, dh_ref[rows, :])
            dx_ref[...] = dx + dy

            @pl.when(i == 0)
            def _():
                dn_ref[...] = dn

            @pl.when(i > 0)
            def _():
                dn_ref[...] += dn

    tile = pl.BlockSpec((tm, d), lambda c, i: (i, 0))
    row = pl.BlockSpec((1, d), lambda c, i: (0, 0))
    wrow = pl.BlockSpec((1, fc, d), lambda c, i: (c, 0, 0), pipeline_mode=pl.Buffered(1))
    hidden = pl.BlockSpec((1, tm, fc), lambda c, i: (c, i, 0))
    last = pl.BlockSpec((tm, d), lambda c, i: (jnp.where(c == nc - 1, i, 0), 0))
    return pl.pallas_call(
        body, name=name, grid=(nc, nt),
        out_shape=(jax.ShapeDtypeStruct((t, d), F32), jax.ShapeDtypeStruct((1, d), F32),
                   jax.ShapeDtypeStruct(wg.shape, BF16), jax.ShapeDtypeStruct(wu.shape, BF16),
                   jax.ShapeDtypeStruct(wd.shape, BF16)),
        in_specs=[tile, row, wrow, wrow, wrow, hidden, hidden, tile, ANY],
        out_specs=(last, row, wrow, wrow, wrow),
        scratch_shapes=[pltpu.VMEM((t, d), F32)] + [pltpu.VMEM((fc, d), F32)] * 3,
        compiler_params=_params("arbitrary", "arbitrary"),
    )(x, norm, wg, wu, wd, gate, up, dy, dep)


def _store_heads(ref, v):
    for h in range(ref.shape[0]):
        ref[h] = v[:, h * HEAD_DIM:(h + 1) * HEAD_DIM]


def _load_heads(ref):
    return jnp.concatenate([ref[h] for h in range(ref.shape[0])], axis=-1)


N_HEAD_GROUPS = 3


def _proj_fwd(x, norm, w, c):
    t, d = x.shape
    nc, _, ncol = w.shape
    nh = c // HEAD_DIM
    tm = TOKEN_TILE
    wide = nc * ncol - N_HEAD_GROUPS * c

    def body(x_ref, n_ref, w_ref, q_ref, k_ref, v_ref, cur_ref):
        xv = x_ref[...]
        h = (xv * _rms(xv) * n_ref[...]).astype(BF16)
        full = jnp.concatenate([jnp.dot(h, w_ref[s], preferred_element_type=F32) for s in range(nc)], axis=1)
        for m, ref in enumerate((q_ref, k_ref, v_ref)):
            _store_heads(ref, full[:, m * c:(m + 1) * c])
        cur_ref[...] = full[:, N_HEAD_GROUPS * c:]

    heads = pl.BlockSpec((nh, tm, HEAD_DIM), lambda i: (0, i, 0))
    hshape = jax.ShapeDtypeStruct((nh, t, HEAD_DIM), F32)
    return pl.pallas_call(
        body, name="proj_fwd", grid=(t // tm,),
        out_shape=(hshape, hshape, hshape, jax.ShapeDtypeStruct((t, wide), F32)),
        in_specs=[pl.BlockSpec((tm, d), lambda i: (i, 0)), pl.BlockSpec((1, d), lambda i: (0, 0)), VMEM_FULL],
        out_specs=(heads, heads, heads, pl.BlockSpec((tm, wide), lambda i: (i, 0))),
        compiler_params=_params("arbitrary"),
    )(x, norm, w)


def _proj_bwd(x, norm, w, dq, dk, dv, dcur, dres):
    t, d = x.shape
    nc, _, ncol = w.shape
    nh = dq.shape[0]
    tm = TOKEN_TILE
    nt = t // tm
    wide = dcur.shape[1]

    def body(x_ref, n_ref, w_ref, dq_ref, dk_ref, dv_ref, dcur_ref, dres_ref, dx_ref, dn_ref, dw_ref, acc_ref):
        i = pl.program_id(0)

        @pl.when(i == 0)
        def _():
            acc_ref[...] = jnp.zeros_like(acc_ref)
            dn_ref[...] = jnp.zeros_like(dn_ref)

        xv = x_ref[...]
        gain = n_ref[...]
        h = (xv * _rms(xv) * gain).astype(BF16)
        dp = jnp.concatenate([_load_heads(dq_ref), _load_heads(dk_ref), _load_heads(dv_ref), dcur_ref[...]], axis=1).astype(BF16)
        dh = jnp.zeros((tm, d), F32)
        for s in range(nc):
            dps = dp[:, s * ncol:(s + 1) * ncol]
            acc_ref[s] += _dot_tn(h, dps)
            dh = dh + _dot_nt(dps, w_ref[s])
        dx, dn = _rmsnorm_bwd(xv, gain, dh)
        dx_ref[...] = dx + dres_ref[...]
        dn_ref[...] += dn

        @pl.when(i == nt - 1)
        def _():
            dw_ref[...] = acc_ref[...].astype(BF16)

    tile = pl.BlockSpec((tm, d), lambda i: (i, 0))
    row = pl.BlockSpec((1, d), lambda i: (0, 0))
    heads = pl.BlockSpec((nh, tm, HEAD_DIM), lambda i: (0, i, 0))
    return pl.pallas_call(
        body, name="proj_bwd", grid=(nt,),
        out_shape=(jax.ShapeDtypeStruct((t, d), F32), jax.ShapeDtypeStruct((1, d), F32),
                   jax.ShapeDtypeStruct(w.shape, BF16)),
        in_specs=[tile, row, VMEM_FULL, heads, heads, heads, pl.BlockSpec((tm, wide), lambda i: (i, 0)), tile],
        out_specs=(tile, row, VMEM_FULL),
        scratch_shapes=[pltpu.VMEM(w.shape, F32)], compiler_params=_params("arbitrary"),
    )(x, norm, w, dq, dk, dv, dcur, dres)


def _mixout_fwd(x, att, opg, gate, w):
    t, d = x.shape
    nh = att.shape[0]
    half = gate.shape[1]
    tm = TOKEN_TILE

    def body(x_ref, att_ref, opg_ref, g_ref, w_ref, o_ref):
        mix = jnp.concatenate([_load_heads(att_ref), _load_heads(opg_ref) * g_ref[...]], axis=-1).astype(BF16)
        o_ref[...] = x_ref[...] + jnp.dot(mix, w_ref[...], preferred_element_type=F32)

    tile = pl.BlockSpec((tm, d), lambda i: (i, 0))
    htile = pl.BlockSpec((tm, half), lambda i: (i, 0))
    heads = pl.BlockSpec((nh, tm, HEAD_DIM), lambda i: (0, i, 0))
    return pl.pallas_call(
        body, name="mixout_fwd", grid=(t // tm,), out_shape=jax.ShapeDtypeStruct((t, d), F32),
        in_specs=[tile, heads, heads, htile, VMEM_FULL], out_specs=tile, compiler_params=_params("arbitrary"),
    )(x, att, opg, gate, w)


def _mixout_bwd(att, opg, gate, w, dy, dep):
    nh, t, _ = att.shape
    half = gate.shape[1]
    d = dy.shape[1]
    tm = TOKEN_TILE

    def body(att_ref, opg_ref, g_ref, w_ref, dy_ref, dep_ref, datt_ref, dopg_ref, dg_ref, dw_ref):
        i = pl.program_id(0)
        opg_v, g_v = _load_heads(opg_ref), g_ref[...]
        mix = jnp.concatenate([_load_heads(att_ref), opg_v * g_v], axis=-1).astype(BF16)
        dyb = dy_ref[...].astype(BF16)
        dmix = _dot_nt(dyb, w_ref[...])
        dw = _dot_tn(mix, dyb)
        _store_heads(datt_ref, dmix[:, :half])
        drw = dmix[:, half:]
        _store_heads(dopg_ref, drw * g_v)
        dg_ref[...] = drw * opg_v

        @pl.when(i == 0)
        def _():
            dw_ref[...] = dw

        @pl.when(i > 0)
        def _():
            dw_ref[...] += dw

    tile = pl.BlockSpec((tm, d), lambda i: (i, 0))
    htile = pl.BlockSpec((tm, half), lambda i: (i, 0))
    heads = pl.BlockSpec((nh, tm, HEAD_DIM), lambda i: (0, i, 0))
    hshape = jax.ShapeDtypeStruct((nh, t, HEAD_DIM), F32)
    return pl.pallas_call(
        body, name="mixout_bwd", grid=(t // tm,),
        out_shape=(hshape, hshape, jax.ShapeDtypeStruct((t, half), F32), jax.ShapeDtypeStruct(w.shape, F32)),
        in_specs=[heads, heads, htile, VMEM_FULL, tile, ANY],
        out_specs=(heads, heads, htile, pl.BlockSpec(w.shape, lambda i: (0, 0))),
        compiler_params=_params("arbitrary"),
    )(att, opg, gate, w, dy, dep)


def _head_norm(x, gain):
    return x * _rms(x) * gain


def _att_pattern(qh, kh, v, nb):
    g, blk, _ = qh.shape
    scale = HEAD_DIM ** -0.5
    qi = lax.broadcasted_iota(jnp.int32, (blk, blk), 0)
    kj = lax.broadcasted_iota(jnp.int32, (blk, blk), 1)
    sc = jnp.where(kj <= qi, _bmm_nt(qh, kh) * scale, NEG_INF)
    top = jnp.max(sc, axis=-1, keepdims=True)
    if nb > 1:
        khp = jnp.concatenate([kh[:1], kh[:-1]], axis=0)
        vp = jnp.concatenate([v[:1], v[:-1]], axis=0)
        has_prev = lax.broadcasted_iota(jnp.int32, (g, 1, 1), 0) % nb != 0
        sp = jnp.where((kj >= qi) & has_prev, _bmm_nt(qh, khp) * scale, NEG_INF)
        top = jnp.maximum(top, jnp.max(sp, axis=-1, keepdims=True))
    m = lax.stop_gradient(top)
    pc = jnp.exp(sc - m)
    den = jnp.sum(pc, axis=-1, keepdims=True)
    acc = _bmm_nn(pc, v)
    if nb > 1:
        pp = jnp.exp(sp - m)
        den = den + jnp.sum(pp, axis=-1, keepdims=True)
        acc = acc + _bmm_nn(pp, vp)
    o = acc / den
    return o, jnp.broadcast_to(m + jnp.log(den), o.shape)


def _pattern_rows(t, dil):
    length = t // dil
    return [pl.ds(r, length, stride=dil) if dil > 1 else pl.ds(0, length) for r in range(dil)], length // ATT_BLOCK


def _take(ref, rows, nb):
    return jnp.concatenate([ref[0, r, :].reshape(nb, ATT_BLOCK, HEAD_DIM) for r in rows], axis=0)


def _put(ref, rows, nb, val):
    for j, r in enumerate(rows):
        ref[0, r, :] = val[j * nb:(j + 1) * nb].reshape(nb * ATT_BLOCK, HEAD_DIM)


def _put_add(ref, rows, nb, val):
    for j, r in enumerate(rows):
        ref[0, r, :] += val[j * nb:(j + 1) * nb].reshape(nb * ATT_BLOCK, HEAD_DIM)


def _merge_fn(o1, o2, o3, l1, l2, l3):
    m = lax.stop_gradient(jnp.maximum(jnp.maximum(l1, l2), l3))
    e1, e2, e3 = jnp.exp(l1 - m), jnp.exp(l2 - m), jnp.exp(l3 - m)
    return (e1 * o1 + e2 * o2 + e3 * o3) / (e1 + e2 + e3)


def _token_rows(j):
    return pl.ds(pl.multiple_of(j * ATT_BLOCK, ATT_BLOCK), ATT_BLOCK)


def _norm_rows(t, q_ref, k_ref, gq, gk, qh_ref, kh_ref):
    def step(j, carry):
        rows = _token_rows(j)
        qh_ref[0, rows, :] = _head_norm(q_ref[0, rows, :], gq[0])
        kh_ref[0, rows, :] = _head_norm(k_ref[0, rows, :], gk[0])
        return carry

    lax.fori_loop(0, t // ATT_BLOCK, step, 0)


def _att_head_specs(t):
    head = pl.BlockSpec((1, t, HEAD_DIM), lambda h: (h, 0, 0))
    gain = pl.BlockSpec((1, 1, HEAD_DIM), lambda h: (0, 0, 0))
    return head, gain


def _att_fwd(q, k, v, qn, kn):
    nh, t, dh = q.shape
    head, gain = _att_head_specs(t)

    def body(q_ref, k_ref, v_ref, qn_ref, kn_ref, att_ref, o1, o2, o3, l1, l2, l3, qh_ref, kh_ref):
        saved = (o1, o2, o3, l1, l2, l3)
        _norm_rows(t, q_ref, k_ref, qn_ref[...], kn_ref[...], qh_ref, kh_ref)
        for p, dil in enumerate(DILATIONS):
            rows, nb = _pattern_rows(t, dil)
            o, lse = _att_pattern(_take(qh_ref, rows, nb), _take(kh_ref, rows, nb), _take(v_ref, rows, nb), nb)
            _put(saved[p], rows, nb, o)
            _put(saved[3 + p], rows, nb, lse)

        def merge(j, carry):
            rows = _token_rows(j)
            att_ref[0, rows, :] = _merge_fn(*[r[0, rows, :] for r in saved])
            return carry

        lax.fori_loop(0, t // ATT_BLOCK, merge, 0)

    return pl.pallas_call(
        body, name="att_fwd", grid=(nh,), out_shape=(jax.ShapeDtypeStruct(q.shape, F32),) * 7,
        in_specs=[head, head, head, gain, gain], out_specs=(head,) * 7,
        scratch_shapes=[pltpu.VMEM((1, t, dh), F32)] * 2, compiler_params=_params("arbitrary"),
    )(q, k, v, qn, kn)


def _att_bwd(q, k, v, qn, kn, saved, datt):
    nh, t, dh = q.shape
    head, gain = _att_head_specs(t)

    def body(q_ref, k_ref, v_ref, qn_ref, kn_ref, o1, o2, o3, l1, l2, l3, datt_ref,
             dq_ref, dk_ref, dv_ref, dqn_ref, dkn_ref, qh_ref, kh_ref, dqh_ref, dkh_ref, *ct_refs):
        for ref in (dqh_ref, dkh_ref, dv_ref):
            ref[...] = jnp.zeros_like(ref)

        @pl.when(pl.program_id(0) == 0)
        def _():
            dqn_ref[...] = jnp.zeros_like(dqn_ref)
            dkn_ref[...] = jnp.zeros_like(dkn_ref)

        gq, gk = qn_ref[...], kn_ref[...]
        _norm_rows(t, q_ref, k_ref, gq, gk, qh_ref, kh_ref)

        def merge_cotangents(j, carry):
            rows = _token_rows(j)
            _, merge_vjp = jax.vjp(_merge_fn, *[r[0, rows, :] for r in (o1, o2, o3, l1, l2, l3)])
            for ref, val in zip(ct_refs, merge_vjp(datt_ref[0, rows, :])):
                ref[0, rows, :] = val
            return carry

        lax.fori_loop(0, t // ATT_BLOCK, merge_cotangents, 0)

        for p, dil in enumerate(DILATIONS):
            rows, nb = _pattern_rows(t, dil)
            _, pattern_vjp = jax.vjp(functools.partial(_att_pattern, nb=nb), _take(qh_ref, rows, nb), _take(kh_ref, rows, nb),
                                     _take(v_ref, rows, nb))
            dqh, dkh, dv = pattern_vjp((_take(ct_refs[p], rows, nb), _take(ct_refs[3 + p], rows, nb)))
            _put_add(dqh_ref, rows, nb, dqh)
            _put_add(dkh_ref, rows, nb, dkh)
            _put_add(dv_ref, rows, nb, dv)

        def norm_cotangents(j, carry):
            rows = _token_rows(j)
            out = []
            for x_ref, gain, dh_ref, dx_ref, acc in ((q_ref, gq, dqh_ref, dq_ref, carry[0]), (k_ref, gk, dkh_ref, dk_ref, carry[1])):
                _, norm_vjp = jax.vjp(_head_norm, x_ref[0, rows, :], gain[0])
                dx, dgain = norm_vjp(dh_ref[0, rows, :])
                dx_ref[0, rows, :] = dx
                out.append(acc + dgain)
            return tuple(out)

        zero = jnp.zeros((1, dh), F32)
        dgq, dgk = lax.fori_loop(0, t // ATT_BLOCK, norm_cotangents, (zero, zero))
        dqn_ref[0] += dgq
        dkn_ref[0] += dgk

    hshape = jax.ShapeDtypeStruct(q.shape, F32)
    gshape = jax.ShapeDtypeStruct((1, 1, dh), F32)
    return pl.pallas_call(
        body, name="att_bwd", grid=(nh,), out_shape=(hshape, hshape, hshape, gshape, gshape),
        in_specs=[head, head, head, gain, gain] + [head] * 7, out_specs=(head, head, head, gain, gain),
        scratch_shapes=[pltpu.VMEM((1, t, dh), F32)] * 10, compiler_params=_params("arbitrary"),
    )(q, k, v, qn, kn, *saved, datt)


RWKV_VEC = ("mu_r", "mu_k", "mu_v", "mu_w", "mu_a", "mu_g", "w0", "a0", "k_k", "k_a")
RWKV_MAT = ("w1", "w2", "a1", "a2", "g1", "g2")


def _rwkv_pre_fn(cur, prev, vec, w1, w2, a1, a2, g1, g2):
    c = cur.shape[1] // 4
    mu_r, mu_k, mu_v, mu_w, mu_a, mu_g, w0, a0, k_k, k_a = (vec[j:j + 1] for j in range(10))

    def lerp(j, mu):
        xc, xp = cur[:, j * c:(j + 1) * c], prev[:, j * c:(j + 1) * c]
        return xc + (xp - xc) * mu

    r, k, v = lerp(0, mu_r), lerp(1, mu_k), lerp(2, mu_v)
    cw, ca, cg = lerp(3, mu_w), lerp(3, mu_a), lerp(3, mu_g)
    z = w0 + _mm(jnp.tanh(_mm(cw, w1)), w2)
    w_log = jnp.minimum(z, 0.0) - jnp.log(1.0 + jnp.exp(-jnp.abs(z))) - 0.5
    lw = -jnp.exp(w_log)
    a = _sigmoid(a0 + _mm(_mm(ca, a1), a2))
    gate = _mm(_sigmoid(_mm(cg, g1)), g2)
    kkraw = k * k_k
    kmod = k * (1.0 + (a - 1.0) * k_a)
    return r, lw, kmod, v, kkraw, a, gate


HALO_ROWS = 8


def _rwkv_pre_specs(c, mats, tile_of):
    tm = TOKEN_TILE
    nh = c // HEAD_DIM
    wide = pl.BlockSpec((tm, 4 * c), lambda j: (tile_of(j), 0))
    halo = pl.BlockSpec((HALO_ROWS, 4 * c), lambda j: (jnp.maximum(tile_of(j) * (tm // HALO_ROWS) - 1, 0), 0))
    one = pl.BlockSpec((tm, c), lambda j: (tile_of(j), 0))
    heads = pl.BlockSpec((nh, tm, HEAD_DIM), lambda j: (0, tile_of(j), 0))
    vec = pl.BlockSpec((10, c), lambda j: (0, 0))
    mspecs = [pl.BlockSpec(m.shape, lambda j: (0, 0)) for m in mats]
    return wide, halo, one, heads, vec, mspecs


def _previous_rows(cur, halo, tile):
    first = jnp.where(tile > 0, halo[HALO_ROWS - 1:HALO_ROWS], 0.0)
    rows = lax.broadcasted_iota(jnp.int32, cur.shape, 0)
    return jnp.where(rows == 0, first, pltpu.roll(cur, 1, axis=0))


def _rwkv_pre_fwd(cur, vec, mats):
    t, c4 = cur.shape
    c = c4 // 4
    wide, halo, one, heads, vspec, mspecs = _rwkv_pre_specs(c, mats, lambda j: j)

    def body(cur_ref, halo_ref, vec_ref, *rest):
        mrefs, outs = rest[:6], rest[6:]
        cur_v = cur_ref[...]
        prev = _previous_rows(cur_v, halo_ref[...], pl.program_id(0))
        vals = _rwkv_pre_fn(cur_v, prev, vec_ref[...], *(m[...] for m in mrefs))
        for ref, val in zip(outs[:6], vals[:6]):
            _store_heads(ref, val)
        outs[6][...] = vals[6]

    hshape = jax.ShapeDtypeStruct((c // HEAD_DIM, t, HEAD_DIM), F32)
    return pl.pallas_call(
        body, name="rwkv_pre_fwd", grid=(t // TOKEN_TILE,), out_shape=(hshape,) * 6 + (jax.ShapeDtypeStruct((t, c), F32),),
        in_specs=[wide, halo, vspec] + mspecs, out_specs=(heads,) * 6 + (one,), compiler_params=_params("arbitrary"),
    )(cur, cur, vec, *mats)


def _rwkv_pre_bwd(cur, vec, mats, cts, dgate):
    t, c4 = cur.shape
    c = c4 // 4
    tm = TOKEN_TILE
    nt = t // tm
    wide, halo, one, heads, vspec, mspecs = _rwkv_pre_specs(c, mats, lambda j: nt - 1 - j)

    def body(cur_ref, halo_ref, vec_ref, *rest):
        mrefs, ctrefs, dgate_ref, outs, carry_ref = rest[:6], rest[6:12], rest[12], rest[13:-1], rest[-1]
        j = pl.program_id(0)

        @pl.when(j == 0)
        def _():
            carry_ref[...] = jnp.zeros_like(carry_ref)
            for ref in outs[1:]:
                ref[...] = jnp.zeros_like(ref)

        cur_v = cur_ref[...]
        prev = _previous_rows(cur_v, halo_ref[...], nt - 1 - j)
        _, vjp = jax.vjp(_rwkv_pre_fn, cur_v, prev, vec_ref[...], *(m[...] for m in mrefs))
        grads = vjp(tuple(_load_heads(r) for r in ctrefs) + (dgate_ref[...],))
        dprev = grads[1]
        rows = lax.broadcasted_iota(jnp.int32, dprev.shape, 0)
        outs[0][...] = grads[0] + jnp.where(rows == tm - 1, carry_ref[0:1], pltpu.roll(dprev, tm - 1, axis=0))
        carry_ref[0:1] = dprev[0:1]
        for ref, val in zip(outs[1:], grads[2:]):
            ref[...] += val

    return pl.pallas_call(
        body, name="rwkv_pre_bwd", grid=(nt,),
        out_shape=(jax.ShapeDtypeStruct(cur.shape, F32), jax.ShapeDtypeStruct(vec.shape, F32))
        + tuple(jax.ShapeDtypeStruct(m.shape, F32) for m in mats),
        in_specs=[wide, halo, vspec] + mspecs + [heads] * 6 + [one], out_specs=(wide, vspec) + tuple(mspecs),
        scratch_shapes=[pltpu.VMEM((HALO_ROWS, c4), F32)], compiler_params=_params("arbitrary"),
    )(cur, cur, vec, *mats, *cts, dgate)


def _scan_chunk_fn(h0, r, lw, k, v, kkraw, a, rk, lnw, lnb):
    n = r.shape[1]
    nrm = jnp.sqrt(jnp.sum(kkraw * kkraw, axis=-1, keepdims=True))
    kk = kkraw / jnp.maximum(nrm, 1e-12)
    av, bv = -kk, kk * a
    ti = lax.broadcasted_iota(jnp.int32, (n, n), 0)
    si = lax.broadcasted_iota(jnp.int32, (n, n), 1)
    incl, strict = ti >= si, ti > si
    ones = jnp.broadcast_to(incl.astype(F32)[None], (r.shape[0], n, n))
    cum = _hdot(ones, lw, 2, 1)
    at, rt = av * jnp.exp(cum - lw), r * jnp.exp(cum)
    inv = jnp.exp(-cum)
    bt, kt = bv * inv, k * inv
    gram = _hdot(jnp.concatenate([at, rt], axis=1), jnp.concatenate([bt, kt], axis=1), 2, 2)
    lab = jnp.where(strict, gram[:, :n, :n], 0.0)
    lak = jnp.where(strict, gram[:, :n, n:], 0.0)
    rb = jnp.where(incl, gram[:, n:, :n], 0.0)
    rkm = jnp.where(incl, gram[:, n:, n:], 0.0)
    nv = v.shape[2]
    u = _bmm_nn(jnp.concatenate([at, lak], axis=2), jnp.concatenate([h0, v], axis=1))
    p = lab
    m = 2
    while m < n:
        both = _bmm_nn(p, jnp.concatenate([u, p], axis=2))
        u, p = u + both[:, :, :nv], both[:, :, nv:]
        m *= 2
    u = u + _bmm_nn(p, u)
    y = _bmm_nn(jnp.concatenate([rt, rb, rkm], axis=2), jnp.concatenate([h0, u, v], axis=1))
    last = jnp.exp(jnp.sum(lw, axis=1, keepdims=True))
    h1 = jnp.swapaxes(last, 1, 2) * (h0 + _bmm_tn(jnp.concatenate([bt, kt], axis=1), jnp.concatenate([u, v], axis=1)))
    mean = jnp.mean(y, axis=-1, keepdims=True)
    yc = y - mean
    var = jnp.mean(yc * yc, axis=-1, keepdims=True)
    yn = yc * lax.rsqrt(var + GN_EPS) * lnw + lnb
    bonus = jnp.sum(r * k * rk, axis=-1, keepdims=True) * v
    return yn + bonus, h1


SCAN_GROUP = 4


def _scan_group_fn(h0, r, lw, k, v, kkraw, a, rk, lnw, lnb):
    outs = []
    for j in range(SCAN_GROUP):
        rows = slice(j * SCAN_CHUNK, (j + 1) * SCAN_CHUNK)
        o, h0 = _scan_chunk_fn(h0, r[:, rows], lw[:, rows], k[:, rows], v[:, rows], kkraw[:, rows], a[:, rows], rk, lnw, lnb)
        outs.append(o)
    return jnp.concatenate(outs, axis=1), h0


def _scan_specs(h, t, dh, rev):
    n = SCAN_CHUNK * SCAN_GROUP
    nc = t // n
    pos = (lambda c: (0, nc - 1 - c, 0)) if rev else (lambda c: (0, c, 0))
    st = (lambda c: (nc - 1 - c, 0, 0, 0)) if rev else (lambda c: (c, 0, 0, 0))
    seq = pl.BlockSpec((h, n, dh), pos)
    par = pl.BlockSpec((h, 1, dh), lambda c: (0, 0, 0))
    state = pl.BlockSpec((1, h, dh, dh), st)
    return seq, par, state


def _scan_fwd(seqs, pars):
    h, t, dh = seqs[0].shape
    nc = t // (SCAN_CHUNK * SCAN_GROUP)
    seq, par, state = _scan_specs(h, t, dh, False)

    def body(r, lw, k, v, kkraw, a, rk, lnw, lnb, o_ref, st_ref, h_ref):
        @pl.when(pl.program_id(0) == 0)
        def _():
            h_ref[...] = jnp.zeros_like(h_ref)

        h0 = h_ref[...]
        st_ref[0] = h0
        o, h1 = _scan_group_fn(h0, r[...], lw[...], k[...], v[...], kkraw[...], a[...], rk[...], lnw[...], lnb[...])
        o_ref[...] = o
        h_ref[...] = h1

    return pl.pallas_call(
        body, name="rwkv_scan_fwd", grid=(nc,),
        out_shape=(jax.ShapeDtypeStruct((h, t, dh), F32), jax.ShapeDtypeStruct((nc, h, dh, dh), F32)),
        in_specs=[seq] * 6 + [par] * 3, out_specs=(seq, state),
        scratch_shapes=[pltpu.VMEM((h, dh, dh), F32)], compiler_params=_params("arbitrary"),
    )(*seqs, *pars)


def _scan_bwd(seqs, pars, states, do):
    h, t, dh = seqs[0].shape
    nc = t // (SCAN_CHUNK * SCAN_GROUP)
    seq, par, state = _scan_specs(h, t, dh, True)

    def body(r, lw, k, v, kkraw, a, rk, lnw, lnb, st_ref, do_ref, *rest):
        douts, dpars, dh_ref = rest[:6], rest[6:9], rest[9]
        first = pl.program_id(0) == 0

        @pl.when(first)
        def _():
            dh_ref[...] = jnp.zeros_like(dh_ref)

        _, vjp = jax.vjp(_scan_group_fn, st_ref[0], r[...], lw[...], k[...], v[...], kkraw[...], a[...],
                         rk[...], lnw[...], lnb[...])
        grads = vjp((do_ref[...], dh_ref[...]))
        dh_ref[...] = grads[0]
        for ref, val in zip(douts, grads[1:7]):
            ref[...] = val

        @pl.when(first)
        def _():
            for ref, val in zip(dpars, grads[7:]):
                ref[...] = val

        @pl.when(jnp.logical_not(first))
        def _():
            for ref, val in zip(dpars, grads[7:]):
                ref[...] += val

    sshape = jax.ShapeDtypeStruct((h, t, dh), F32)
    pshape = jax.ShapeDtypeStruct((h, 1, dh), F32)
    return pl.pallas_call(
        body, name="rwkv_scan_bwd", grid=(nc,), out_shape=(sshape,) * 6 + (pshape,) * 3,
        in_specs=[seq] * 6 + [par] * 3 + [state, seq], out_specs=(seq,) * 6 + (par,) * 3,
        scratch_shapes=[pltpu.VMEM((h, dh, dh), F32)], compiler_params=_params("arbitrary"),
    )(*seqs, *pars, states, do)


def _local_step(x, target, w, ex):
    w = dict(w)
    c = w["mu_r"].shape[-1]
    qn, kn = w["q_norm"].reshape(1, 1, HEAD_DIM), w["k_norm"].reshape(1, 1, HEAD_DIM)
    vec = jnp.concatenate([w[n].reshape(1, c) for n in RWKV_VEC], axis=0)
    pars = [w[n].reshape(-1, 1, HEAD_DIM) for n in ("r_k", "ln_x_w", "ln_x_b")]
    no_dep = jnp.zeros(DEP_SHAPE, F32)

    x1, gate1, up1 = _ffn_fwd(x, w["ffn1_norm"], w["ffn1_w_gate"], w["ffn1_w_up"], w["ffn1_w_down"], ex.first_dep, "ffn1_fwd")
    w.update(ex.mix_weights((x1,)))
    mats = [w[n] for n in RWKV_MAT]
    q, k, v, cur = _proj_fwd(x1, w["mix_norm"], w["w_in"], c)
    att, *saved = _att_fwd(q, k, v, qn, kn)
    pre = _rwkv_pre_fwd(cur, vec, mats)
    seqs, gate = pre[:6], pre[6]
    opg, states = _scan_fwd(seqs, pars)
    w.update(ex.out_weights((att, opg)))
    x2 = _mixout_fwd(x1, att, opg, gate, w["w_out"])
    dy, gate2, up2, loss = _ffn_fwd(x2, w["ffn2_norm"], w["ffn2_w_gate"], w["ffn2_w_up"], w["ffn2_w_down"], no_dep, "ffn2_fwd",
                                    target=target)

    g = {}
    dx2, g["ffn2_norm"], g["ffn2_w_gate"], g["ffn2_w_up"], g["ffn2_w_down"] = _ffn_bwd(
        x2, w["ffn2_norm"], w["ffn2_w_gate"], w["ffn2_w_up"], w["ffn2_w_down"], gate2, up2, dy, no_dep, "ffn2_bwd")
    dep = ex.send_ffn2({n: g[n] for n in ("ffn2_w_gate", "ffn2_w_up", "ffn2_w_down")})
    datt, dopg, dgate, g["w_out"] = _mixout_bwd(att, opg, gate, w["w_out"], dx2, dep)
    dscan = _scan_bwd(seqs, pars, states, dopg)
    for n, d in zip(("r_k", "ln_x_w", "ln_x_b"), dscan[6:]):
        g[n] = d
    dcur, dvec, *dmats = _rwkv_pre_bwd(cur, vec, mats, dscan[:6], dgate)
    for n, d in zip(RWKV_MAT, dmats):
        g[n] = d
    for j, n in enumerate(RWKV_VEC):
        g[n] = dvec[j:j + 1]
    dq, dk, dv, g["q_norm"], g["k_norm"] = _att_bwd(q, k, v, qn, kn, saved, datt)
    dx1, g["mix_norm"], g["w_in"] = _proj_bwd(x1, w["mix_norm"], w["w_in"], dq, dk, dv, dcur, dx2)
    dep = ex.send_mix({n: g[n] for n in ("w_in", "w_out") + RWKV_MAT}, (dx1,))
    dx, g["ffn1_norm"], g["ffn1_w_gate"], g["ffn1_w_up"], g["ffn1_w_down"] = _ffn_bwd(
        x, w["ffn1_norm"], w["ffn1_w_gate"], w["ffn1_w_up"], w["ffn1_w_down"], gate1, up1, dx1, dep, "ffn1_bwd")
    return loss, dx, g


N_SHARDS = 4


def _place():
    return lax.axis_index("x"), lax.axis_index("y"), lax.axis_index("c")


def _chip_peers(x, y):
    return [(1 - x, y), (x, 1 - y), (1 - x, 1 - y)]


HBM = pl.BlockSpec(memory_space=pltpu.HBM)
SEM = pl.BlockSpec(memory_space=pltpu.SEMAPHORE)
DEP_SHAPE = (8, 128)


class _Views:
    to_sibling = False


class _GatherViews(_Views):
    @staticmethod
    def send(i, srcs, lands, k, at):
        return srcs[i], lands[i].at[at[3]]

    @staticmethod
    def landing(i, srcs, lands, k, at):
        return srcs[i], lands[i].at[2 * at[4] + at[5]]


class _ScatterViews(_Views):
    @staticmethod
    def send(i, srcs, lands, k, at):
        return srcs[i].at[2 * at[4] + at[5]], lands[i].at[k]

    @staticmethod
    def landing(i, srcs, lands, k, at):
        return srcs[i].at[at[3]], lands[i].at[k]


def _half_rows(ref, slot, half):
    rows = ref.shape[1] // 2
    return ref.at[slot, pl.ds(pl.multiple_of(half * rows, BF16_SUBLANES), rows)]


class _HalfGatherViews(_Views):
    @staticmethod
    def send(i, srcs, lands, k, at):
        rows = srcs[i].shape[0] // 2
        return srcs[i].at[pl.ds(pl.multiple_of(at[2] * rows, BF16_SUBLANES), rows)], _half_rows(lands[i], at[3], at[2])

    @staticmethod
    def landing(i, srcs, lands, k, at):
        rows = srcs[i].shape[0] // 2
        return srcs[i].at[pl.ds(pl.multiple_of(at[2] * rows, BF16_SUBLANES), rows)], _half_rows(lands[i], 2 * at[4] + at[5], at[2])


class _ForwardViews(_Views):
    to_sibling = True

    @staticmethod
    def send(i, srcs, lands, k, at):
        mine = _half_rows(lands[i], 2 * at[4] + at[5], at[2])
        return mine, mine

    @staticmethod
    def landing(i, srcs, lands, k, at):
        theirs = _half_rows(lands[i], 2 * at[4] + at[5], 1 - at[2])
        return theirs, theirs


def _push_start(srcs, lands, views, after, name):
    ns, nl = len(srcs), len(lands)

    def body(*refs):
        src_refs, land_refs = refs[:ns], refs[ns:ns + nl]
        send_sems, recv_sems = refs[ns + nl + 1:ns + nl + 3]
        token = refs[2 * (ns + nl) + 3]
        x, y, c = _place()
        for i in range(nl):
            for k, (px, py) in enumerate(_chip_peers(x, y)):
                src, dst = views.send(i, src_refs, land_refs, k, (x, y, c, 2 * x + y, px, py))
                pltpu.make_async_remote_copy(
                    src_ref=src, dst_ref=dst, send_sem=send_sems.at[3 * i + k], recv_sem=recv_sems.at[3 * i + k],
                    device_id=(x, y, 1 - c) if views.to_sibling else (px, py, c), device_id_type=MESH).start()
        token[...] = jnp.zeros_like(token)

    sems = pltpu.SemaphoreType.DMA((3 * nl,))
    both = [pltpu.with_memory_space_constraint(a, pltpu.HBM) for a in (*srcs, *lands)]
    outs = pl.pallas_call(
        body, name=name,
        out_shape=(sems, sems, *[pltpu.HBM(a.shape, a.dtype) for a in both], jax.ShapeDtypeStruct(DEP_SHAPE, F32)),
        in_specs=[HBM] * (ns + nl) + [ANY], out_specs=(SEM, SEM, *[HBM] * (ns + nl), VMEM_FULL),
        input_output_aliases={i: 2 + i for i in range(ns + nl)},
        compiler_params=pltpu.CompilerParams(has_side_effects=pltpu.SideEffectType.DATAFLOW_SIDE_EFFECTING),
    )(*both, after)
    return outs[0], outs[1], outs[2:2 + ns], outs[2 + ns:2 + ns + nl], outs[2 + ns + nl]


def _push_wait(started, views, after, name, with_sources=False):
    send_sems, recv_sems, srcs, lands, _ = started
    ns, nl = len(srcs), len(lands)

    def body(*refs):
        src_refs, land_refs = refs[:ns], refs[ns:ns + nl]
        send_sems, recv_sems = refs[ns + nl:ns + nl + 2]
        x, y, c = _place()
        for i in range(nl):
            for k, (px, py) in enumerate(_chip_peers(x, y)):
                src, dst = views.landing(i, src_refs, land_refs, k, (x, y, c, 2 * x + y, px, py))
                landing = pltpu.make_async_remote_copy(
                    src_ref=src, dst_ref=dst, send_sem=send_sems.at[3 * i + k], recv_sem=recv_sems.at[3 * i + k],
                    device_id=(x, y, 1 - c) if views.to_sibling else (px, py, c), device_id_type=MESH)
                landing.wait_send()
                landing.wait_recv()

    outs = pl.pallas_call(
        body, name=name,
        out_shape=tuple(pltpu.HBM(a.shape, a.dtype) for a in (*srcs, *lands)),
        in_specs=[HBM] * (ns + nl) + [SEM, SEM] + [ANY] * len(after), out_specs=(HBM,) * (ns + nl),
        input_output_aliases={i: i for i in range(ns + nl)},
        compiler_params=pltpu.CompilerParams(has_side_effects=pltpu.SideEffectType.DATAFLOW_SIDE_EFFECTING),
    )(*srcs, *lands, send_sems, recv_sems, *after)
    return outs if with_sources else outs[ns:]


def _empty_lands(shards, slots, own_slot):
    lands = [lax.empty((slots,) + s.shape, s.dtype) for s in shards]
    if own_slot:
        me = 2 * lax.axis_index("x") + lax.axis_index("y")
        lands = [lax.dynamic_update_index_in_dim(z, s, me, 0) for z, s in zip(lands, shards)]
    return lands


def _sibling_swap(arrays, name, other_half=False):
    n = len(arrays)

    def body(*refs):
        ins, outs = refs[:n], refs[n:2 * n]
        send_sems, recv_sems = refs[2 * n:]
        x, y, c = _place()
        copies = []
        for i in range(n):
            src = ins[i]
            if other_half:
                rows = src.shape[1] // 2
                src = src.at[:, pl.ds(pl.multiple_of((1 - c) * rows, BF16_SUBLANES), rows)]
            cp = pltpu.make_async_remote_copy(
                src_ref=src, dst_ref=outs[i], send_sem=send_sems.at[i], recv_sem=recv_sems.at[i],
                device_id=(x, y, 1 - c), device_id_type=MESH)
            cp.start()
            copies.append(cp)
        for cp in copies:
            cp.wait()

    shapes = [(a.shape[0], a.shape[1] // 2, a.shape[2]) if other_half else a.shape for a in arrays]
    return pl.pallas_call(
        body, name=name,
        out_shape=tuple(jax.ShapeDtypeStruct(s, a.dtype) for s, a in zip(shapes, arrays)),
        in_specs=[ANY] * n, out_specs=(ANY,) * n,
        scratch_shapes=[pltpu.SemaphoreType.DMA((n,)), pltpu.SemaphoreType.DMA((n,))],
    )(*arrays)


FOLD_STEPS = 2


def _fold_add(core, parts, theirs, name):
    n = len(parts)
    s, r, cols = parts[0].shape
    tr = r // 2 // FOLD_STEPS

    def body(core_ref, *refs):
        for p_ref, t_ref, o_ref in zip(refs[:n], refs[n:2 * n], refs[2 * n:]):
            o_ref[...] = (p_ref[...].astype(F32) + t_ref[...].astype(F32)).astype(BF16)

    half = pl.BlockSpec((1, tr, cols), lambda j, i, core_ref: (j, i, 0))
    return pl.pallas_call(
        body, name=name, out_shape=tuple(jax.ShapeDtypeStruct((s, r // 2, cols), BF16) for _ in parts),
        grid_spec=pltpu.PrefetchScalarGridSpec(
            num_scalar_prefetch=1, grid=(s, FOLD_STEPS),
            in_specs=[pl.BlockSpec((1, tr, cols), lambda j, i, core_ref: (j, core_ref[0] * FOLD_STEPS + i, 0))] * n + [half] * n,
            out_specs=(half,) * n),
        compiler_params=_params("arbitrary", "arbitrary"),
    )(core, *parts, *theirs)


N_DEV = 8


def _allreduce_small(pack):
    def body(in_ref, out_ref, buf, send_sems, recv_sems):
        x, y, c = _place()
        me = 4 * x + 2 * y + c
        buf[me] = in_ref[...]

        def copy(j, slot):
            px, py, pc = x ^ (j >> 2), y ^ ((j >> 1) & 1), c ^ (j & 1)
            return pltpu.make_async_remote_copy(
                src_ref=in_ref, dst_ref=buf.at[slot(px, py, pc)], send_sem=send_sems.at[j], recv_sem=recv_sems.at[j],
                device_id=(px, py, pc), device_id_type=MESH)

        for j in range(1, N_DEV):
            copy(j, lambda px, py, pc: me).start()
        for j in range(1, N_DEV):
            landing = copy(j, lambda px, py, pc: 4 * px + 2 * py + pc)
            landing.wait_send()
            landing.wait_recv()
        acc = buf[0]
        for s in range(1, N_DEV):
            acc = acc + buf[s]
        out_ref[...] = acc

    return pl.pallas_call(
        body, name="allreduce_small", out_shape=jax.ShapeDtypeStruct(pack.shape, F32),
        in_specs=[VMEM_FULL], out_specs=VMEM_FULL,
        scratch_shapes=[pltpu.VMEM((N_DEV,) + pack.shape, F32), pltpu.SemaphoreType.DMA((N_DEV,)),
                        pltpu.SemaphoreType.DMA((N_DEV,))],
    )(pack)


BF16_SUBLANES = 16


def _reduce_own(me, parts, recvs, dep, steps, name):
    n = len(parts)

    def body(me_ref, *refs):
        for p_ref, rv_ref, o_ref in zip(refs[:n], refs[n:2 * n], refs[2 * n + 1:]):
            acc = p_ref[0].astype(F32)
            for k in range(3):
                acc = acc + rv_ref[k].astype(F32)
            o_ref[...] = acc

    shapes = [(p.shape[1] // steps, p.shape[2]) for p in parts]
    return pl.pallas_call(
        body, name=name, out_shape=tuple(jax.ShapeDtypeStruct(p.shape[1:], F32) for p in parts),
        grid_spec=pltpu.PrefetchScalarGridSpec(
            num_scalar_prefetch=1, grid=(steps,),
            in_specs=[pl.BlockSpec((1, tr, c), lambda i, me_ref: (me_ref[0], i, 0)) for tr, c in shapes]
            + [pl.BlockSpec((3, tr, c), lambda i, me_ref: (0, i, 0)) for tr, c in shapes] + [ANY],
            out_specs=tuple(pl.BlockSpec((tr, c), lambda i, me_ref: (i, 0)) for tr, c in shapes)),
        compiler_params=_params("arbitrary"),
    )(me, *parts, *recvs, dep)


def _adamw(ws, gas, gbs, ms, vs, steps, name):
    n = len(ws)
    c1 = 1.0 - ADAM_B1 ** ADAM_STEP
    c2 = 1.0 - ADAM_B2 ** ADAM_STEP
    operands = [ws, gas, ms, vs] if gbs is None else [ws, gas, gbs, ms, vs]
    k = len(operands)

    def body(*refs):
        ins, outs = refs[:k * n], refs[k * n:]
        for j in range(n):
            w_ref, ga_ref, *gb_ref, m_ref, v_ref = ins[j::n]
            g_out, d_out, m_out, v_out = outs[j::n]
            g = ga_ref[...] + gb_ref[0][...] if gb_ref else ga_ref[...]
            mn = ADAM_B1 * m_ref[...] + (1.0 - ADAM_B1) * g
            vn = ADAM_B2 * v_ref[...] + (1.0 - ADAM_B2) * (g * g)
            g_out[...] = g
            m_out[...] = mn
            v_out[...] = vn
            d_out[...] = -ADAM_LR * ((mn / c1) / (jnp.sqrt(vn / c2) + ADAM_EPS) + ADAM_WD * w_ref[...])

    tiles = [pl.BlockSpec((w.shape[0] // steps, w.shape[1]), lambda i: (i, 0)) for w in ws]
    shapes = [jax.ShapeDtypeStruct(w.shape, F32) for w in ws]
    outs = pl.pallas_call(
        body, name=name, grid=(steps,), out_shape=tuple(shapes * 4), in_specs=tiles * k, out_specs=tuple(tiles * 4),
        compiler_params=_params("arbitrary"),
    )(*[a for group in operands for a in group])
    return [outs[j::n] for j in range(n)]


PACK_COLS = 512


def _to_rows(a):
    flat = a.reshape(-1)
    pad = (-flat.shape[0]) % PACK_COLS
    return jnp.pad(flat, (0, pad)).reshape(-1, PACK_COLS)


def _pack(arrays, extra_rows=0):
    rows = [_to_rows(a) for a in arrays]
    n = sum(r.shape[0] for r in rows) + extra_rows
    pad = (-n) % 8
    return jnp.concatenate(rows + [jnp.zeros((extra_rows + pad, PACK_COLS), F32)], axis=0)


def _unpack(pack, like):
    out, at = [], 0
    for a in like:
        n = -(-a.size // PACK_COLS)
        out.append(pack[at:at + n].reshape(-1)[:a.size].reshape(a.shape))
        at += n
    return out


COL_SHARDED = ("ffn1_w_gate", "ffn1_w_up", "w_in", "ffn2_w_gate", "ffn2_w_up", "w2", "a2", "g2")
ROW_SHARDED = ("ffn1_w_down", "ffn2_w_down", "w_out", "w1", "a1", "g1")
CHUNKED = ("ffn1_w_gate", "ffn1_w_up", "ffn1_w_down", "w_in", "ffn2_w_gate", "ffn2_w_up", "ffn2_w_down")
WEIGHTS = ("ffn1_norm", "ffn1_w_gate", "ffn1_w_up", "ffn1_w_down", "mix_norm", "w_in", "q_norm", "k_norm",
           "mu_r", "mu_k", "mu_v", "mu_w", "mu_a", "mu_g", "w0", "w1", "w2", "a0", "a1", "a2", "g1", "g2",
           "k_k", "k_a", "r_k", "ln_x_w", "ln_x_b", "w_out", "ffn2_norm", "ffn2_w_gate", "ffn2_w_up", "ffn2_w_down")


TRANSPOSED = ("ffn1_w_gate", "ffn1_w_up", "ffn2_w_gate", "ffn2_w_up")


def _shard_2d(name, a):
    return a[0].T if name in TRANSPOSED else a[0]


def _full_from_blocks(name, blocks):
    if name in CHUNKED:
        return blocks
    if name in ROW_SHARDED:
        return blocks.reshape(-1, blocks.shape[-1])
    return blocks.transpose(1, 0, 2).reshape(blocks.shape[1], -1)


def _blocks_from_full(name, full):
    if name in CHUNKED:
        return full
    if name in ROW_SHARDED:
        return full.reshape(N_SHARDS, -1, full.shape[-1])
    return full.reshape(full.shape[0], N_SHARDS, -1).transpose(1, 0, 2)


FFN1_GROUP = ("ffn1_w_gate", "ffn1_w_up", "ffn1_w_down")
MIX_GROUP = ("w_in",) + RWKV_MAT
OUT_GROUP = ("w_out", "ffn2_w_gate", "ffn2_w_up", "ffn2_w_down")
FFN2_GROUP = OUT_GROUP[1:]
LATE_GROUP = ("w_in", "w_out") + RWKV_MAT


class _Exchange:
    def __init__(self, given):
        self.given = given
        first = self._gather_start(FFN1_GROUP, _HalfGatherViews, jnp.zeros(DEP_SHAPE, F32), "gather_ffn1_start")
        self.mix = self._gather_start(MIX_GROUP, _GatherViews, first[4], "gather_mix_start")
        self.out = self._gather_start(OUT_GROUP, _GatherViews, self.mix[4], "gather_out_start")
        self.first_dep = self.out[4]
        halves = _push_wait(first, _HalfGatherViews, (self.first_dep,), "gather_ffn1_wait")
        passed = _push_start([], halves, _ForwardViews, jnp.zeros(DEP_SHAPE, F32), "gather_ffn1_pass_start")
        self.first_weights = self._full(FFN1_GROUP, _push_wait(passed, _ForwardViews, (passed[4],), "gather_ffn1_pass_wait"))
        self.parts, self.recv = {}, {}

    @staticmethod
    def _full(names, blocks):
        out = {}
        for n, b in zip(names, blocks):
            full = _full_from_blocks(n, b)
            out[n] = full.astype(F32) if n in RWKV_MAT else full
        return out

    def _gather_start(self, names, views, after, name):
        after, raw = lax.optimization_barrier((after, [_shard_2d(n, self.given[n]) for n in names]))
        shards = [a.astype(BF16) for a in raw]
        return _push_start(shards, _empty_lands(shards, N_SHARDS, True), views, after, name)

    def mix_weights(self, after):
        return self._full(MIX_GROUP, _push_wait(self.mix, _GatherViews, after, "gather_mix_wait"))

    def out_weights(self, after):
        return self._full(OUT_GROUP, _push_wait(self.out, _GatherViews, after, "gather_out_wait"))

    def _scatter_start(self, grads, name):
        names = tuple(grads)
        parts = [_blocks_from_full(n, grads[n]) for n in names]
        self.parts.update(zip(names, parts))
        lands = [lax.empty((3,) + p.shape[1:], BF16) for p in parts]
        return _push_start([p.astype(BF16) for p in parts], lands, _ScatterViews, jnp.zeros(DEP_SHAPE, F32), name)

    def _scatter_done(self, started, names, after, name):
        outs = _push_wait(started, _ScatterViews, after, name, with_sources=True)
        for n, sent, got in zip(names, outs[:len(names)], outs[len(names):]):
            self.recv[n] = got
            if self.parts[n].dtype == BF16:
                self.parts[n] = sent

    def send_ffn2(self, grads):
        self.ffn2 = self._scatter_start(grads, "scatter_ffn2_start")
        return self.ffn2[4]

    def send_mix(self, grads, after):
        self._scatter_done(self.ffn2, FFN2_GROUP, after, "scatter_ffn2_wait")
        self.late = self._scatter_start(grads, "scatter_late_start")
        return self.late[4]

    def send_ffn1(self, grads):
        self.ffn1 = self._scatter_start(grads, "scatter_ffn1_start")
        return self.ffn1[4]

    def late_received(self, after):
        self._scatter_done(self.late, LATE_GROUP, after, "scatter_late_wait")

    def ffn1_received(self, after):
        self._scatter_done(self.ffn1, FFN1_GROUP, after, "scatter_ffn1_wait")


def kernel(
        x, ffn1_norm, ffn1_w_gate, ffn1_w_up, ffn1_w_down, mix_norm, w_in, q_norm, k_norm, mu_r, mu_k, mu_v, mu_w,
        mu_a, mu_g, w0, w1, w2, a0, a1, a2, g1, g2, k_k, k_a, r_k, ln_x_w, ln_x_b, w_out, ffn2_norm, ffn2_w_gate,
        ffn2_w_up, ffn2_w_down, loss_target, m_ffn1_norm, m_ffn1_w_gate, m_ffn1_w_up, m_ffn1_w_down, m_mix_norm,
        m_w_in, m_q_norm, m_k_norm, m_mu_r, m_mu_k, m_mu_v, m_mu_w, m_mu_a, m_mu_g, m_w0, m_w1, m_w2, m_a0, m_a1,
        m_a2, m_g1, m_g2, m_k_k, m_k_a, m_r_k, m_ln_x_w, m_ln_x_b, m_w_out, m_ffn2_norm, m_ffn2_w_gate, m_ffn2_w_up,
        m_ffn2_w_down, v_ffn1_norm, v_ffn1_w_gate, v_ffn1_w_up, v_ffn1_w_down, v_mix_norm, v_w_in, v_q_norm, v_k_norm,
        v_mu_r, v_mu_k, v_mu_v, v_mu_w, v_mu_a, v_mu_g, v_w0, v_w1, v_w2, v_a0, v_a1, v_a2, v_g1, v_g2, v_k_k, v_k_a,
        v_r_k, v_ln_x_w, v_ln_x_b, v_w_out, v_ffn2_norm, v_ffn2_w_gate, v_ffn2_w_up, v_ffn2_w_down):
    given = dict(locals())
    sharded = COL_SHARDED + ROW_SHARDED
    sharded = tuple(n for n in WEIGHTS if n in sharded)
    small = tuple(n for n in WEIGHTS if n not in sharded)

    ex = _Exchange(given)
    w = {n: given[n] for n in small}
    w.update(ex.first_weights)
    loss, dx, g = _local_step(x[0], loss_target[0], w, ex)

    core = lax.axis_index("c").astype(jnp.int32).reshape(1)
    late = [g[n] for n in FFN1_GROUP]
    folded = _fold_add(core, late, _sibling_swap(late, "fold_swap_ffn1", other_half=True), "fold_add_ffn1")
    dep = ex.send_ffn1(dict(zip(FFN1_GROUP, folded)))

    me = (2 * lax.axis_index("x") + lax.axis_index("y")).astype(jnp.int32).reshape(1)
    out = {}

    def settle(names, dep, tag):
        done = []
        for kind, sub, r_steps, a_steps in (("large", tuple(n for n in names if n not in RWKV_MAT), 4, 8),
                                            ("small", tuple(n for n in names if n in RWKV_MAT), 1, 1)):
            if not sub:
                continue
            parts = [ex.parts[n].reshape(N_SHARDS, -1, ex.parts[n].shape[-1]) for n in sub]
            recvs = [ex.recv[n].reshape(3, -1, ex.recv[n].shape[-1]) for n in sub]
            mine = _reduce_own(me, parts, recvs, dep, r_steps, f"reduce_{tag}_{kind}")
            theirs = _sibling_swap(mine, f"sibling_swap_{tag}_{kind}")
            res = _adamw([_shard_2d(n, given[n]) for n in sub], mine, theirs, [_shard_2d(n, given["m_" + n]) for n in sub],
                         [_shard_2d(n, given["v_" + n]) for n in sub], a_steps, f"adamw_{tag}_{kind}")
            for n, rs in zip(sub, res):
                out[n] = [(r.T if n in TRANSPOSED else r).reshape(given[n].shape) for r in rs]
                done.append(out[n][1])
        return tuple(done)

    ex.late_received((dep,))
    last = settle(tuple(n for n in sharded if n not in FFN1_GROUP), dep, "rest")

    gpack = _pack([g[n] for n in small], extra_rows=1)
    n_rows = sum(-(-given[n].size // PACK_COLS) for n in small)
    gpack = gpack.at[n_rows, :loss.shape[1]].set(loss[0])
    gsum = _allreduce_small(gpack)
    res = _adamw([_pack([given[n] for n in small], 1)], [gsum], None, [_pack([given["m_" + n] for n in small], 1)],
                 [_pack([given["v_" + n] for n in small], 1)], 1, "adamw_replicated")[0]
    like = [given[n] for n in small]
    for j, r in enumerate(res):
        for n, a in zip(small, _unpack(r, like)):
            out.setdefault(n, [None] * 4)[j] = a
    total_loss = gsum[n_rows, 0]

    ex.ffn1_received((*last, res[1]))
    halves = _reduce_own(me, [ex.parts[n] for n in FFN1_GROUP], [ex.recv[n] for n in FFN1_GROUP],
                         jnp.zeros(DEP_SHAPE, F32), FOLD_STEPS, "reduce_ffn1")
    others = _sibling_swap(halves, "sibling_swap_ffn1")
    first = lax.axis_index("c") == 0
    grads = [jnp.concatenate([jnp.where(first, a, b), jnp.where(first, b, a)], axis=0) for a, b in zip(halves, others)]
    res = _adamw([_shard_2d(n, given[n]) for n in FFN1_GROUP], grads, None, [_shard_2d(n, given["m_" + n]) for n in FFN1_GROUP],
                 [_shard_2d(n, given["v_" + n]) for n in FFN1_GROUP], 8, "adamw_ffn1")
    for n, rs in zip(FFN1_GROUP, res):
        out[n] = [(r.T if n in TRANSPOSED else r).reshape(given[n].shape) for r in rs]
    return (total_loss, dx[None], *[out[n][0] for n in WEIGHTS], *[out[n][1] for n in WEIGHTS],
            *[out[n][2] for n in WEIGHTS], *[out[n][3] for n in WEIGHTS])
```

```python
import functools

import jax
import jax.numpy as jnp
from jax import lax
from jax.experimental import pallas as pl
from jax.experimental.pallas import tpu as pltpu

F32 = jnp.float32
BF16 = jnp.bfloat16
MESH = pl.DeviceIdType.MESH

RMS_EPS = 1e-6
GN_EPS = 64e-5
NEG_INF = -1e30
FFN_RESIDUAL = 0.5
HEAD_DIM = 64
ATT_BLOCK = 128
DILATIONS = (1, 4, 16)
SCAN_CHUNK = 64
TOKEN_TILE = 256
FFN_BWD_TILE = 512

ADAM_LR = 0.001
ADAM_B1 = 0.9
ADAM_B2 = 0.999
ADAM_EPS = 1e-08
ADAM_WD = 0.01
ADAM_STEP = 10

VMEM_FULL = pl.BlockSpec(memory_space=pltpu.VMEM)
ANY = pl.BlockSpec(memory_space=pl.ANY)


VMEM_LIMIT = 56 * 1024 * 1024


def _params(*sem):
    return pltpu.CompilerParams(dimension_semantics=sem, vmem_limit_bytes=VMEM_LIMIT)


def _dot(a, b, dims):
    return lax.dot_general(a.astype(BF16), b.astype(BF16), (dims, ((), ())), preferred_element_type=F32)


def _dot_nn(a, b):
    return _dot(a, b, ((1,), (0,)))


def _dot_nt(a, b):
    return _dot(a, b, ((1,), (1,)))


def _dot_tn(a, b):
    return _dot(a, b, ((0,), (0,)))


@jax.custom_vjp
def _mm(a, b):
    return _dot_nn(a, b)


def _mm_fwd(a, b):
    return _dot_nn(a, b), (a, b)


def _mm_bwd(res, g):
    a, b = res
    return _dot_nt(g, b).astype(a.dtype), _dot_tn(a, g).astype(b.dtype)


_mm.defvjp(_mm_fwd, _mm_bwd)


@jax.custom_vjp
def _mm_nt(a, bt):
    return _dot_nt(a, bt)


def _mm_nt_fwd(a, bt):
    return _dot_nt(a, bt), (a, bt)


def _mm_nt_bwd(res, g):
    a, bt = res
    return _dot_nn(g, bt).astype(a.dtype), _dot_tn(g, a).astype(bt.dtype)


_mm_nt.defvjp(_mm_nt_fwd, _mm_nt_bwd)


def _bdot(a, b, ca, cb):
    return lax.dot_general(a.astype(BF16), b.astype(BF16), (((ca,), (cb,)), ((0,), (0,))), preferred_element_type=F32)


@jax.custom_vjp
def _bmm_nt(a, b):
    return _bdot(a, b, 2, 2)


def _bmm_nt_fwd(a, b):
    return _bdot(a, b, 2, 2), (a, b)


def _bmm_nt_bwd(res, g):
    a, b = res
    return _bdot(g, b, 2, 1), _bdot(g, a, 1, 1)


_bmm_nt.defvjp(_bmm_nt_fwd, _bmm_nt_bwd)


@jax.custom_vjp
def _bmm_nn(a, b):
    return _bdot(a, b, 2, 1)


def _bmm_nn_fwd(a, b):
    return _bdot(a, b, 2, 1), (a, b)


def _bmm_nn_bwd(res, g):
    a, b = res
    return _bdot(g, b, 2, 2), _bdot(a, g, 1, 1)


_bmm_nn.defvjp(_bmm_nn_fwd, _bmm_nn_bwd)


@jax.custom_vjp
def _bmm_tn(a, b):
    return _bdot(a, b, 1, 1)


def _bmm_tn_fwd(a, b):
    return _bdot(a, b, 1, 1), (a, b)


def _bmm_tn_bwd(res, g):
    a, b = res
    return _bdot(b, g, 2, 2), _bdot(a, g, 2, 1)


_bmm_tn.defvjp(_bmm_tn_fwd, _bmm_tn_bwd)


def _hdot(a, b, ca, cb):
    return lax.dot_general(a, b, (((ca,), (cb,)), ((0,), (0,))), precision=lax.Precision.HIGH, preferred_element_type=F32)


def _sigmoid(x):
    return 1.0 / (1.0 + jnp.exp(-x))


def _rms(x):
    return lax.rsqrt(jnp.mean(x * x, axis=-1, keepdims=True) + RMS_EPS)


def _ffn_fwd(x, norm, wg, wu, wd, dep, name, target=None):
    t, d = x.shape
    nc, fc, _ = wg.shape
    tm = TOKEN_TILE

    def body(x_ref, n_ref, wg_ref, wu_ref, wd_ref, dep_ref, *rest):
        o_ref, g_ref, u_ref = rest[-3:] if target is None else rest[1:4]
        xv = x_ref[...]
        h = (xv * _rms(xv) * n_ref[...]).astype(BF16)
        acc = jnp.zeros((tm, d), F32)
        for c in range(nc):
            g = _dot_nt(h, wg_ref[c])
            u = _dot_nt(h, wu_ref[c])
            g_ref[c] = g.astype(BF16)
            u_ref[c] = u.astype(BF16)
            a = (g * _sigmoid(g) * u).astype(BF16)
            acc = acc + jnp.dot(a, wd_ref[c], preferred_element_type=F32)
        y = xv + FFN_RESIDUAL * acc
        if target is None:
            o_ref[...] = y
        else:
            t_ref, loss_ref = rest[0], rest[4]
            err = y - t_ref[...]
            o_ref[...] = err * (1.0 / d)
            part = 0.5 * jnp.sum(jnp.mean(err * err, axis=-1, keepdims=True), axis=0, keepdims=True)

            @pl.when(pl.program_id(0) == 0)
            def _():
                loss_ref[...] = jnp.zeros_like(loss_ref)

            loss_ref[...] += jnp.broadcast_to(part, loss_ref.shape)

    tile = pl.BlockSpec((tm, d), lambda i: (i, 0))
    hidden = pl.BlockSpec((nc, tm, fc), lambda i: (0, i, 0))
    hshape = jax.ShapeDtypeStruct((nc, t, fc), BF16)
    with_loss = target is not None
    return pl.pallas_call(
        body, name=name, grid=(t // tm,),
        out_shape=(jax.ShapeDtypeStruct((t, d), F32), hshape, hshape) + ((jax.ShapeDtypeStruct((1, 128), F32),) if with_loss else ()),
        in_specs=[tile, pl.BlockSpec((1, d), lambda i: (0, 0)), VMEM_FULL, VMEM_FULL, VMEM_FULL, ANY] + ([tile] if with_loss else []),
        out_specs=(tile, hidden, hidden) + ((pl.BlockSpec((1, 128), lambda i: (0, 0)),) if with_loss else ()),
        compiler_params=_params("arbitrary"),
    )(x, norm, wg, wu, wd, dep, *((target,) if with_loss else ()))


def _rmsnorm_bwd(xv, gain, dh):
    rs = _rms(xv)
    xn = xv * rs
    dxn = dh * gain
    dx = rs * (dxn - xn * jnp.mean(dxn * xn, axis=-1, keepdims=True))
    return dx, jnp.sum(dh * xn, axis=0, keepdims=True)


def _ffn_bwd(x, norm, wg, wu, wd, gate, up, dy, dep, name):
    t, d = x.shape
    nc, fc, _ = wg.shape
    tm = FFN_BWD_TILE
    nt = t // tm

    def body(x_ref, n_ref, wg_ref, wu_ref, wd_ref, g_ref, u_ref, dy_ref, dep_ref, dx_ref, dn_ref, dwg_ref, dwu_ref,
             dwd_ref, dh_ref, ag_ref, au_ref, ad_ref):
        c, i = pl.program_id(0), pl.program_id(1)
        rows = pl.ds(pl.multiple_of(i * tm, tm), tm)
        xv = x_ref[...]
        gain = n_ref[...]
        h = (xv * _rms(xv) * gain).astype(BF16)
        dy = dy_ref[...]
        dyb = (FFN_RESIDUAL * dy).astype(BF16)
        g = g_ref[0].astype(F32)
        u = u_ref[0].astype(F32)
        sg = _sigmoid(g)
        s = g * sg
        a = (s * u).astype(BF16)
        da = _dot_nt(dyb, wd_ref[0])
        dub = (da * s).astype(BF16)
        dgb = (da * u * (sg * (1.0 + g * (1.0 - sg)))).astype(BF16)
        dwd_c = _dot_tn(a, dyb)
        dwg_c = _dot_tn(dgb, h)
        dwu_c = _dot_tn(dub, h)
        dh_c = _dot_nn(dgb, wg_ref[0]) + _dot_nn(dub, wu_ref[0])

        @pl.when(i == 0)
        def _():
            ad_ref[...] = dwd_c
            ag_ref[...] = dwg_c
            au_ref[...] = dwu_c

        @pl.when(i > 0)
        def _():
            ad_ref[...] += dwd_c
            ag_ref[...] += dwg_c
            au_ref[...] += dwu_c

        @pl.when(i == nt - 1)
        def _():
            dwd_ref[0] = ad_ref[...].astype(BF16)
            dwg_ref[0] = ag_ref[...].astype(BF16)
            dwu_ref[0] = au_ref[...].astype(BF16)

        @pl.when(c == 0)
        def _():
            dh_ref[rows, :] = dh_c

        @pl.when(c > 0)
        def _():
            dh_ref[rows, :] += dh_c

        @pl.when(c == nc - 1)
        def _():
            dx, dn = _rmsnorm_bwd(xv, gain, dh_ref[rows, :])
            dx_ref[...] = dx + dy

            @pl.when(i == 0)
            def _():
                dn_ref[...] = dn

            @pl.when(i > 0)
            def _():
                dn_ref[...] += dn

    tile = pl.BlockSpec((tm, d), lambda c, i: (i, 0))
    row = pl.BlockSpec((1, d), lambda c, i: (0, 0))
    wrow = pl.BlockSpec((1, fc, d), lambda c, i: (c, 0, 0), pipeline_mode=pl.Buffered(1))
    hidden = pl.BlockSpec((1, tm, fc), lambda c, i: (c, i, 0))
    last = pl.BlockSpec((tm, d), lambda c, i: (jnp.where(c == nc - 1, i, 0), 0))
    return pl.pallas_call(
        body, name=name, grid=(nc, nt),
        out_shape=(jax.ShapeDtypeStruct((t, d), F32), jax.ShapeDtypeStruct((1, d), F32),
                   jax.ShapeDtypeStruct(wg.shape, BF16), jax.ShapeDtypeStruct(wu.shape, BF16),
                   jax.ShapeDtypeStruct(wd.shape, BF16)),
        in_specs=[tile, row, wrow, wrow, wrow, hidden, hidden, tile, ANY],
        out_specs=(last, row, wrow, wrow, wrow),
        scratch_shapes=[pltpu.VMEM((t, d), F32)] + [pltpu.VMEM((fc, d), F32)] * 3,
        compiler_params=_params("arbitrary", "arbitrary"),
    )(x, norm, wg, wu, wd, gate, up, dy, dep)


def _store_heads(ref, v):
    for h in range(ref.shape[0]):
        ref[h] = v[:, h * HEAD_DIM:(h + 1) * HEAD_DIM]


def _load_heads(ref):
    return jnp.concatenate([ref[h] for h in range(ref.shape[0])], axis=-1)


N_HEAD_GROUPS = 3


def _proj_fwd(x, norm, w, c):
    t, d = x.shape
    nc, _, ncol = w.shape
    nh = c // HEAD_DIM
    tm = TOKEN_TILE
    wide = nc * ncol - N_HEAD_GROUPS * c

    def body(x_ref, n_ref, w_ref, q_ref, k_ref, v_ref, cur_ref):
        xv = x_ref[...]
        h = (xv * _rms(xv) * n_ref[...]).astype(BF16)
        full = jnp.concatenate([jnp.dot(h, w_ref[s], preferred_element_type=F32) for s in range(nc)], axis=1)
        for m, ref in enumerate((q_ref, k_ref, v_ref)):
            _store_heads(ref, full[:, m * c:(m + 1) * c])
        cur_ref[...] = full[:, N_HEAD_GROUPS * c:]

    heads = pl.BlockSpec((nh, tm, HEAD_DIM), lambda i: (0, i, 0))
    hshape = jax.ShapeDtypeStruct((nh, t, HEAD_DIM), F32)
    return pl.pallas_call(
        body, name="proj_fwd", grid=(t // tm,),
        out_shape=(hshape, hshape, hshape, jax.ShapeDtypeStruct((t, wide), F32)),
        in_specs=[pl.BlockSpec((tm, d), lambda i: (i, 0)), pl.BlockSpec((1, d), lambda i: (0, 0)), VMEM_FULL],
        out_specs=(heads, heads, heads, pl.BlockSpec((tm, wide), lambda i: (i, 0))),
        compiler_params=_params("arbitrary"),
    )(x, norm, w)


def _proj_bwd(x, norm, w, dq, dk, dv, dcur, dres):
    t, d = x.shape
    nc, _, ncol = w.shape
    nh = dq.shape[0]
    tm = TOKEN_TILE
    nt = t // tm
    wide = dcur.shape[1]

    def body(x_ref, n_ref, w_ref, dq_ref, dk_ref, dv_ref, dcur_ref, dres_ref, dx_ref, dn_ref, dw_ref, acc_ref):
        i = pl.program_id(0)

        @pl.when(i == 0)
        def _():
            acc_ref[...] = jnp.zeros_like(acc_ref)
            dn_ref[...] = jnp.zeros_like(dn_ref)

        xv = x_ref[...]
        gain = n_ref[...]
        h = (xv * _rms(xv) * gain).astype(BF16)
        dp = jnp.concatenate([_load_heads(dq_ref), _load_heads(dk_ref), _load_heads(dv_ref), dcur_ref[...]], axis=1).astype(BF16)
        dh = jnp.zeros((tm, d), F32)
        for s in range(nc):
            dps = dp[:, s * ncol:(s + 1) * ncol]
            acc_ref[s] += _dot_tn(h, dps)
            dh = dh + _dot_nt(dps, w_ref[s])
        dx, dn = _rmsnorm_bwd(xv, gain, dh)
        dx_ref[...] = dx + dres_ref[...]
        dn_ref[...] += dn

        @pl.when(i == nt - 1)
        def _():
            dw_ref[...] = acc_ref[...].astype(BF16)

    tile = pl.BlockSpec((tm, d), lambda i: (i, 0))
    row = pl.BlockSpec((1, d), lambda i: (0, 0))
    heads = pl.BlockSpec((nh, tm, HEAD_DIM), lambda i: (0, i, 0))
    return pl.pallas_call(
        body, name="proj_bwd", grid=(nt,),
        out_shape=(jax.ShapeDtypeStruct((t, d), F32), jax.ShapeDtypeStruct((1, d), F32),
                   jax.ShapeDtypeStruct(w.shape, BF16)),
        in_specs=[tile, row, VMEM_FULL, heads, heads, heads, pl.BlockSpec((tm, wide), lambda i: (i, 0)), tile],
        out_specs=(tile, row, VMEM_FULL),
        scratch_shapes=[pltpu.VMEM(w.shape, F32)], compiler_params=_params("arbitrary"),
    )(x, norm, w, dq, dk, dv, dcur, dres)


def _mixout_fwd(x, att, opg, gate, w):
    t, d = x.shape
    nh = att.shape[0]
    half = gate.shape[1]
    tm = TOKEN_TILE

    def body(x_ref, att_ref, opg_ref, g_ref, w_ref, o_ref):
        mix = jnp.concatenate([_load_heads(att_ref), _load_heads(opg_ref) * g_ref[...]], axis=-1).astype(BF16)
        o_ref[...] = x_ref[...] + jnp.dot(mix, w_ref[...], preferred_element_type=F32)

    tile = pl.BlockSpec((tm, d), lambda i: (i, 0))
    htile = pl.BlockSpec((tm, half), lambda i: (i, 0))
    heads = pl.BlockSpec((nh, tm, HEAD_DIM), lambda i: (0, i, 0))
    return pl.pallas_call(
        body, name="mixout_fwd", grid=(t // tm,), out_shape=jax.ShapeDtypeStruct((t, d), F32),
        in_specs=[tile, heads, heads, htile, VMEM_FULL], out_specs=tile, compiler_params=_params("arbitrary"),
    )(x, att, opg, gate, w)


def _mixout_bwd(att, opg, gate, w, dy, dep):
    nh, t, _ = att.shape
    half = gate.shape[1]
    d = dy.shape[1]
    tm = TOKEN_TILE

    def body(att_ref, opg_ref, g_ref, w_ref, dy_ref, dep_ref, datt_ref, dopg_ref, dg_ref, dw_ref):
        i = pl.program_id(0)
        opg_v, g_v = _load_heads(opg_ref), g_ref[...]
        mix = jnp.concatenate([_load_heads(att_ref), opg_v * g_v], axis=-1).astype(BF16)
        dyb = dy_ref[...].astype(BF16)
        dmix = _dot_nt(dyb, w_ref[...])
        dw = _dot_tn(mix, dyb)
        _store_heads(datt_ref, dmix[:, :half])
        drw = dmix[:, half:]
        _store_heads(dopg_ref, drw * g_v)
        dg_ref[...] = drw * opg_v

        @pl.when(i == 0)
        def _():
            dw_ref[...] = dw

        @pl.when(i > 0)
        def _():
            dw_ref[...] += dw

    tile = pl.BlockSpec((tm, d), lambda i: (i, 0))
    htile = pl.BlockSpec((tm, half), lambda i: (i, 0))
    heads = pl.BlockSpec((nh, tm, HEAD_DIM), lambda i: (0, i, 0))
    hshape = jax.ShapeDtypeStruct((nh, t, HEAD_DIM), F32)
    return pl.pallas_call(
        body, name="mixout_bwd", grid=(t // tm,),
        out_shape=(hshape, hshape, jax.ShapeDtypeStruct((t, half), F32), jax.ShapeDtypeStruct(w.shape, F32)),
        in_specs=[heads, heads, htile, VMEM_FULL, tile, ANY],
        out_specs=(heads, heads, htile, pl.BlockSpec(w.shape, lambda i: (0, 0))),
        compiler_params=_params("arbitrary"),
    )(att, opg, gate, w, dy, dep)


def _head_norm(x, gain):
    return x * _rms(x) * gain


def _att_pattern(qh, kh, v, nb):
    g, blk, _ = qh.shape
    scale = HEAD_DIM ** -0.5
    qi = lax.broadcasted_iota(jnp.int32, (blk, blk), 0)
    kj = lax.broadcasted_iota(jnp.int32, (blk, blk), 1)
    sc = jnp.where(kj <= qi, _bmm_nt(qh, kh) * scale, NEG_INF)
    top = jnp.max(sc, axis=-1, keepdims=True)
    if nb > 1:
        khp = jnp.concatenate([kh[:1], kh[:-1]], axis=0)
        vp = jnp.concatenate([v[:1], v[:-1]], axis=0)
        has_prev = lax.broadcasted_iota(jnp.int32, (g, 1, 1), 0) % nb != 0
        sp = jnp.where((kj >= qi) & has_prev, _bmm_nt(qh, khp) * scale, NEG_INF)
        top = jnp.maximum(top, jnp.max(sp, axis=-1, keepdims=True))
    m = lax.stop_gradient(top)
    pc = jnp.exp(sc - m)
    den = jnp.sum(pc, axis=-1, keepdims=True)
    acc = _bmm_nn(pc, v)
    if nb > 1:
        pp = jnp.exp(sp - m)
        den = den + jnp.sum(pp, axis=-1, keepdims=True)
        acc = acc + _bmm_nn(pp, vp)
    o = acc / den
    return o, jnp.broadcast_to(m + jnp.log(den), o.shape)


def _pattern_rows(t, dil):
    length = t // dil
    return [pl.ds(r, length, stride=dil) if dil > 1 else pl.ds(0, length) for r in range(dil)], length // ATT_BLOCK


def _take(ref, rows, nb):
    return jnp.concatenate([ref[0, r, :].reshape(nb, ATT_BLOCK, HEAD_DIM) for r in rows], axis=0)


def _put(ref, rows, nb, val):
    for j, r in enumerate(rows):
        ref[0, r, :] = val[j * nb:(j + 1) * nb].reshape(nb * ATT_BLOCK, HEAD_DIM)


def _put_add(ref, rows, nb, val):
    for j, r in enumerate(rows):
        ref[0, r, :] += val[j * nb:(j + 1) * nb].reshape(nb * ATT_BLOCK, HEAD_DIM)


def _merge_fn(o1, o2, o3, l1, l2, l3):
    m = lax.stop_gradient(jnp.maximum(jnp.maximum(l1, l2), l3))
    e1, e2, e3 = jnp.exp(l1 - m), jnp.exp(l2 - m), jnp.exp(l3 - m)
    return (e1 * o1 + e2 * o2 + e3 * o3) / (e1 + e2 + e3)


def _token_rows(j):
    return pl.ds(pl.multiple_of(j * ATT_BLOCK, ATT_BLOCK), ATT_BLOCK)


def _norm_rows(t, q_ref, k_ref, gq, gk, qh_ref, kh_ref):
    def step(j, carry):
        rows = _token_rows(j)
        qh_ref[0, rows, :] = _head_norm(q_ref[0, rows, :], gq[0])
        kh_ref[0, rows, :] = _head_norm(k_ref[0, rows, :], gk[0])
        return carry

    lax.fori_loop(0, t // ATT_BLOCK, step, 0)


def _att_head_specs(t):
    head = pl.BlockSpec((1, t, HEAD_DIM), lambda h: (h, 0, 0))
    gain = pl.BlockSpec((1, 1, HEAD_DIM), lambda h: (0, 0, 0))
    return head, gain


def _att_fwd(q, k, v, qn, kn):
    nh, t, dh = q.shape
    head, gain = _att_head_specs(t)

    def body(q_ref, k_ref, v_ref, qn_ref, kn_ref, att_ref, o1, o2, o3, l1, l2, l3, qh_ref, kh_ref):
        saved = (o1, o2, o3, l1, l2, l3)
        _norm_rows(t, q_ref, k_ref, qn_ref[...], kn_ref[...], qh_ref, kh_ref)
        for p, dil in enumerate(DILATIONS):
            rows, nb = _pattern_rows(t, dil)
            o, lse = _att_pattern(_take(qh_ref, rows, nb), _take(kh_ref, rows, nb), _take(v_ref, rows, nb), nb)
            _put(saved[p], rows, nb, o)
            _put(saved[3 + p], rows, nb, lse)

        def merge(j, carry):
            rows = _token_rows(j)
            att_ref[0, rows, :] = _merge_fn(*[r[0, rows, :] for r in saved])
            return carry

        lax.fori_loop(0, t // ATT_BLOCK, merge, 0)

    return pl.pallas_call(
        body, name="att_fwd", grid=(nh,), out_shape=(jax.ShapeDtypeStruct(q.shape, F32),) * 7,
        in_specs=[head, head, head, gain, gain], out_specs=(head,) * 7,
        scratch_shapes=[pltpu.VMEM((1, t, dh), F32)] * 2, compiler_params=_params("arbitrary"),
    )(q, k, v, qn, kn)


def _att_bwd(q, k, v, qn, kn, saved, datt):
    nh, t, dh = q.shape
    head, gain = _att_head_specs(t)

    def body(q_ref, k_ref, v_ref, qn_ref, kn_ref, o1, o2, o3, l1, l2, l3, datt_ref,
             dq_ref, dk_ref, dv_ref, dqn_ref, dkn_ref, qh_ref, kh_ref, dqh_ref, dkh_ref, *ct_refs):
        for ref in (dqh_ref, dkh_ref, dv_ref):
            ref[...] = jnp.zeros_like(ref)

        @pl.when(pl.program_id(0) == 0)
        def _():
            dqn_ref[...] = jnp.zeros_like(dqn_ref)
            dkn_ref[...] = jnp.zeros_like(dkn_ref)

        gq, gk = qn_ref[...], kn_ref[...]
        _norm_rows(t, q_ref, k_ref, gq, gk, qh_ref, kh_ref)

        def merge_cotangents(j, carry):
            rows = _token_rows(j)
            _, merge_vjp = jax.vjp(_merge_fn, *[r[0, rows, :] for r in (o1, o2, o3, l1, l2, l3)])
            for ref, val in zip(ct_refs, merge_vjp(datt_ref[0, rows, :])):
                ref[0, rows, :] = val
            return carry

        lax.fori_loop(0, t // ATT_BLOCK, merge_cotangents, 0)

        for p, dil in enumerate(DILATIONS):
            rows, nb = _pattern_rows(t, dil)
            _, pattern_vjp = jax.vjp(functools.partial(_att_pattern, nb=nb), _take(qh_ref, rows, nb), _take(kh_ref, rows, nb),
                                     _take(v_ref, rows, nb))
            dqh, dkh, dv = pattern_vjp((_take(ct_refs[p], rows, nb), _take(ct_refs[3 + p], rows, nb)))
            _put_add(dqh_ref, rows, nb, dqh)
            _put_add(dkh_ref, rows, nb, dkh)
            _put_add(dv_ref, rows, nb, dv)

        def norm_cotangents(j, carry):
            rows = _token_rows(j)
            out = []
            for x_ref, gain, dh_ref, dx_ref, acc in ((q_ref, gq, dqh_ref, dq_ref, carry[0]), (k_ref, gk, dkh_ref, dk_ref, carry[1])):
                _, norm_vjp = jax.vjp(_head_norm, x_ref[0, rows, :], gain[0])
                dx, dgain = norm_vjp(dh_ref[0, rows, :])
                dx_ref[0, rows, :] = dx
                out.append(acc + dgain)
            return tuple(out)

        zero = jnp.zeros((1, dh), F32)
        dgq, dgk = lax.fori_loop(0, t // ATT_BLOCK, norm_cotangents, (zero, zero))
        dqn_ref[0] += dgq
        dkn_ref[0] += dgk

    hshape = jax.ShapeDtypeStruct(q.shape, F32)
    gshape = jax.ShapeDtypeStruct((1, 1, dh), F32)
    return pl.pallas_call(
        body, name="att_bwd", grid=(nh,), out_shape=(hshape, hshape, hshape, gshape, gshape),
        in_specs=[head, head, head, gain, gain] + [head] * 7, out_specs=(head, head, head, gain, gain),
        scratch_shapes=[pltpu.VMEM((1, t, dh), F32)] * 10, compiler_params=_params("arbitrary"),
    )(q, k, v, qn, kn, *saved, datt)


RWKV_VEC = ("mu_r", "mu_k", "mu_v", "mu_w", "mu_a", "mu_g", "w0", "a0", "k_k", "k_a")
RWKV_MAT = ("w1", "w2", "a1", "a2", "g1", "g2")


def _rwkv_pre_fn(cur, prev, vec, w1t, w2, a1t, a2, g1t, g2):
    c = cur.shape[1] // 4
    mu_r, mu_k, mu_v, mu_w, mu_a, mu_g, w0, a0, k_k, k_a = (vec[j:j + 1] for j in range(10))

    def lerp(j, mu):
        xc, xp = cur[:, j * c:(j + 1) * c], prev[:, j * c:(j + 1) * c]
        return xc + (xp - xc) * mu

    r, k, v = lerp(0, mu_r), lerp(1, mu_k), lerp(2, mu_v)
    cw, ca, cg = lerp(3, mu_w), lerp(3, mu_a), lerp(3, mu_g)
    z = w0 + _mm(jnp.tanh(_mm_nt(cw, w1t)), w2)
    w_log = jnp.minimum(z, 0.0) - jnp.log(1.0 + jnp.exp(-jnp.abs(z))) - 0.5
    lw = -jnp.exp(w_log)
    a = _sigmoid(a0 + _mm(_mm_nt(ca, a1t), a2))
    gate = _mm(_sigmoid(_mm_nt(cg, g1t)), g2)
    kkraw = k * k_k
    kmod = k * (1.0 + (a - 1.0) * k_a)
    return r, lw, kmod, v, kkraw, a, gate


HALO_ROWS = 8


def _rwkv_pre_specs(c, mats, tile_of):
    tm = TOKEN_TILE
    nh = c // HEAD_DIM
    wide = pl.BlockSpec((tm, 4 * c), lambda j: (tile_of(j), 0))
    halo = pl.BlockSpec((HALO_ROWS, 4 * c), lambda j: (jnp.maximum(tile_of(j) * (tm // HALO_ROWS) - 1, 0), 0))
    one = pl.BlockSpec((tm, c), lambda j: (tile_of(j), 0))
    heads = pl.BlockSpec((nh, tm, HEAD_DIM), lambda j: (0, tile_of(j), 0))
    vec = pl.BlockSpec((10, c), lambda j: (0, 0))
    mspecs = [pl.BlockSpec(m.shape, lambda j: (0, 0)) for m in mats]
    return wide, halo, one, heads, vec, mspecs


def _previous_rows(cur, halo, tile):
    first = jnp.where(tile > 0, halo[HALO_ROWS - 1:HALO_ROWS], 0.0)
    rows = lax.broadcasted_iota(jnp.int32, cur.shape, 0)
    return jnp.where(rows == 0, first, pltpu.roll(cur, 1, axis=0))


def _rwkv_pre_fwd(cur, vec, mats):
    t, c4 = cur.shape
    c = c4 // 4
    wide, halo, one, heads, vspec, mspecs = _rwkv_pre_specs(c, mats, lambda j: j)

    def body(cur_ref, halo_ref, vec_ref, *rest):
        mrefs, outs = rest[:6], rest[6:]
        cur_v = cur_ref[...]
        prev = _previous_rows(cur_v, halo_ref[...], pl.program_id(0))
        vals = _rwkv_pre_fn(cur_v, prev, vec_ref[...], *(m[...] for m in mrefs))
        for ref, val in zip(outs[:6], vals[:6]):
            _store_heads(ref, val)
        outs[6][...] = vals[6]

    hshape = jax.ShapeDtypeStruct((c // HEAD_DIM, t, HEAD_DIM), F32)
    return pl.pallas_call(
        body, name="rwkv_pre_fwd", grid=(t // TOKEN_TILE,), out_shape=(hshape,) * 6 + (jax.ShapeDtypeStruct((t, c), F32),),
        in_specs=[wide, halo, vspec] + mspecs, out_specs=(heads,) * 6 + (one,), compiler_params=_params("arbitrary"),
    )(cur, cur, vec, *mats)


def _rwkv_pre_bwd(cur, vec, mats, cts, dgate):
    t, c4 = cur.shape
    c = c4 // 4
    tm = TOKEN_TILE
    nt = t // tm
    wide, halo, one, heads, vspec, mspecs = _rwkv_pre_specs(c, mats, lambda j: nt - 1 - j)

    def body(cur_ref, halo_ref, vec_ref, *rest):
        mrefs, ctrefs, dgate_ref, outs, carry_ref = rest[:6], rest[6:12], rest[12], rest[13:-1], rest[-1]
        j = pl.program_id(0)

        @pl.when(j == 0)
        def _():
            carry_ref[...] = jnp.zeros_like(carry_ref)
            for ref in outs[1:]:
                ref[...] = jnp.zeros_like(ref)

        cur_v = cur_ref[...]
        prev = _previous_rows(cur_v, halo_ref[...], nt - 1 - j)
        _, vjp = jax.vjp(_rwkv_pre_fn, cur_v, prev, vec_ref[...], *(m[...] for m in mrefs))
        grads = vjp(tuple(_load_heads(r) for r in ctrefs) + (dgate_ref[...],))
        dprev = grads[1]
        rows = lax.broadcasted_iota(jnp.int32, dprev.shape, 0)
        outs[0][...] = grads[0] + jnp.where(rows == tm - 1, carry_ref[0:1], pltpu.roll(dprev, tm - 1, axis=0))
        carry_ref[0:1] = dprev[0:1]
        for ref, val in zip(outs[1:], grads[2:]):
            ref[...] += val

    return pl.pallas_call(
        body, name="rwkv_pre_bwd", grid=(nt,),
        out_shape=(jax.ShapeDtypeStruct(cur.shape, F32), jax.ShapeDtypeStruct(vec.shape, F32))
        + tuple(jax.ShapeDtypeStruct(m.shape, F32) for m in mats),
        in_specs=[wide, halo, vspec] + mspecs + [heads] * 6 + [one], out_specs=(wide, vspec) + tuple(mspecs),
        scratch_shapes=[pltpu.VMEM((HALO_ROWS, c4), F32)], compiler_params=_params("arbitrary"),
    )(cur, cur, vec, *mats, *cts, dgate)


def _scan_chunk_fn(h0, r, lw, k, v, kkraw, a, rk, lnw, lnb):
    n = r.shape[1]
    nrm = jnp.sqrt(jnp.sum(kkraw * kkraw, axis=-1, keepdims=True))
    kk = kkraw / jnp.maximum(nrm, 1e-12)
    av, bv = -kk, kk * a
    ti = lax.broadcasted_iota(jnp.int32, (n, n), 0)
    si = lax.broadcasted_iota(jnp.int32, (n, n), 1)
    incl, strict = ti >= si, ti > si
    ones = jnp.broadcast_to(incl.astype(F32)[None], (r.shape[0], n, n))
    cum = _hdot(ones, lw, 2, 1)
    at, rt = av * jnp.exp(cum - lw), r * jnp.exp(cum)
    inv = jnp.exp(-cum)
    bt, kt = bv * inv, k * inv
    gram = _hdot(jnp.concatenate([at, rt], axis=1), jnp.concatenate([bt, kt], axis=1), 2, 2)
    lab = jnp.where(strict, gram[:, :n, :n], 0.0)
    lak = jnp.where(strict, gram[:, :n, n:], 0.0)
    rb = jnp.where(incl, gram[:, n:, :n], 0.0)
    rkm = jnp.where(incl, gram[:, n:, n:], 0.0)
    nv = v.shape[2]
    u = _bmm_nn(jnp.concatenate([at, lak], axis=2), jnp.concatenate([h0, v], axis=1))
    p = lab
    m = 2
    while m < n:
        both = _bmm_nn(p, jnp.concatenate([u, p], axis=2))
        u, p = u + both[:, :, :nv], both[:, :, nv:]
        m *= 2
    u = u + _bmm_nn(p, u)
    y = _bmm_nn(jnp.concatenate([rt, rb, rkm], axis=2), jnp.concatenate([h0, u, v], axis=1))
    last = jnp.exp(jnp.sum(lw, axis=1, keepdims=True))
    h1 = jnp.swapaxes(last, 1, 2) * (h0 + _bmm_tn(jnp.concatenate([bt, kt], axis=1), jnp.concatenate([u, v], axis=1)))
    mean = jnp.mean(y, axis=-1, keepdims=True)
    yc = y - mean
    var = jnp.mean(yc * yc, axis=-1, keepdims=True)
    yn = yc * lax.rsqrt(var + GN_EPS) * lnw + lnb
    bonus = jnp.sum(r * k * rk, axis=-1, keepdims=True) * v
    return yn + bonus, h1


SCAN_GROUP = 2


def _scan_group_fn(h0, r, lw, k, v, kkraw, a, rk, lnw, lnb):
    outs = []
    for j in range(SCAN_GROUP):
        rows = slice(j * SCAN_CHUNK, (j + 1) * SCAN_CHUNK)
        o, h0 = _scan_chunk_fn(h0, r[:, rows], lw[:, rows], k[:, rows], v[:, rows], kkraw[:, rows], a[:, rows], rk, lnw, lnb)
        outs.append(o)
    return jnp.concatenate(outs, axis=1), h0


def _scan_specs(h, t, dh, rev):
    n = SCAN_CHUNK * SCAN_GROUP
    nc = t // n
    pos = (lambda c: (0, nc - 1 - c, 0)) if rev else (lambda c: (0, c, 0))
    st = (lambda c: (nc - 1 - c, 0, 0, 0)) if rev else (lambda c: (c, 0, 0, 0))
    seq = pl.BlockSpec((h, n, dh), pos)
    par = pl.BlockSpec((h, 1, dh), lambda c: (0, 0, 0))
    state = pl.BlockSpec((1, h, dh, dh), st)
    return seq, par, state


def _scan_fwd(seqs, pars):
    h, t, dh = seqs[0].shape
    nc = t // (SCAN_CHUNK * SCAN_GROUP)
    seq, par, state = _scan_specs(h, t, dh, False)

    def body(r, lw, k, v, kkraw, a, rk, lnw, lnb, o_ref, st_ref, h_ref):
        @pl.when(pl.program_id(0) == 0)
        def _():
            h_ref[...] = jnp.zeros_like(h_ref)

        h0 = h_ref[...]
        st_ref[0] = h0
        o, h1 = _scan_group_fn(h0, r[...], lw[...], k[...], v[...], kkraw[...], a[...], rk[...], lnw[...], lnb[...])
        o_ref[...] = o
        h_ref[...] = h1

    return pl.pallas_call(
        body, name="rwkv_scan_fwd", grid=(nc,),
        out_shape=(jax.ShapeDtypeStruct((h, t, dh), F32), jax.ShapeDtypeStruct((nc, h, dh, dh), F32)),
        in_specs=[seq] * 6 + [par] * 3, out_specs=(seq, state),
        scratch_shapes=[pltpu.VMEM((h, dh, dh), F32)], compiler_params=_params("arbitrary"),
    )(*seqs, *pars)


def _scan_bwd(seqs, pars, states, do):
    h, t, dh = seqs[0].shape
    nc = t // (SCAN_CHUNK * SCAN_GROUP)
    seq, par, state = _scan_specs(h, t, dh, True)

    def body(r, lw, k, v, kkraw, a, rk, lnw, lnb, st_ref, do_ref, *rest):
        douts, dpars, dh_ref = rest[:6], rest[6:9], rest[9]
        first = pl.program_id(0) == 0

        @pl.when(first)
        def _():
            dh_ref[...] = jnp.zeros_like(dh_ref)

        _, vjp = jax.vjp(_scan_group_fn, st_ref[0], r[...], lw[...], k[...], v[...], kkraw[...], a[...],
                         rk[...], lnw[...], lnb[...])
        grads = vjp((do_ref[...], dh_ref[...]))
        dh_ref[...] = grads[0]
        for ref, val in zip(douts, grads[1:7]):
            ref[...] = val

        @pl.when(first)
        def _():
            for ref, val in zip(dpars, grads[7:]):
                ref[...] = val

        @pl.when(jnp.logical_not(first))
        def _():
            for ref, val in zip(dpars, grads[7:]):
                ref[...] += val

    sshape = jax.ShapeDtypeStruct((h, t, dh), F32)
    pshape = jax.ShapeDtypeStruct((h, 1, dh), F32)
    return pl.pallas_call(
        body, name="rwkv_scan_bwd", grid=(nc,), out_shape=(sshape,) * 6 + (pshape,) * 3,
        in_specs=[seq] * 6 + [par] * 3 + [state, seq], out_specs=(seq,) * 6 + (par,) * 3,
        scratch_shapes=[pltpu.VMEM((h, dh, dh), F32)], compiler_params=_params("arbitrary"),
    )(*seqs, *pars, states, do)


def _local_step(x, target, w, ex):
    w = dict(w)
    c = w["mu_r"].shape[-1]
    qn, kn = w["q_norm"].reshape(1, 1, HEAD_DIM), w["k_norm"].reshape(1, 1, HEAD_DIM)
    vec = jnp.concatenate([w[n].reshape(1, c) for n in RWKV_VEC], axis=0)
    pars = [w[n].reshape(-1, 1, HEAD_DIM) for n in ("r_k", "ln_x_w", "ln_x_b")]
    no_dep = jnp.zeros(DEP_SHAPE, F32)

    x1, gate1, up1 = _ffn_fwd(x, w["ffn1_norm"], w["ffn1_w_gate"], w["ffn1_w_up"], w["ffn1_w_down"], ex.first_dep, "ffn1_fwd")
    w.update(ex.mix_weights((x1,)))
    mats = [w[n] for n in RWKV_MAT]
    q, k, v, cur = _proj_fwd(x1, w["mix_norm"], w["w_in"], c)
    att, *saved = _att_fwd(q, k, v, qn, kn)
    pre = _rwkv_pre_fwd(cur, vec, mats)
    seqs, gate = pre[:6], pre[6]
    opg, states = _scan_fwd(seqs, pars)
    w.update(ex.out_weights((att, opg)))
    x2 = _mixout_fwd(x1, att, opg, gate, w["w_out"])
    dy, gate2, up2, loss = _ffn_fwd(x2, w["ffn2_norm"], w["ffn2_w_gate"], w["ffn2_w_up"], w["ffn2_w_down"], no_dep, "ffn2_fwd",
                                    target=target)

    g = {}
    dx2, g["ffn2_norm"], g["ffn2_w_gate"], g["ffn2_w_up"], g["ffn2_w_down"] = _ffn_bwd(
        x2, w["ffn2_norm"], w["ffn2_w_gate"], w["ffn2_w_up"], w["ffn2_w_down"], gate2, up2, dy, no_dep, "ffn2_bwd")
    dep = ex.send_ffn2({n: g[n] for n in ("ffn2_w_gate", "ffn2_w_up", "ffn2_w_down")})
    datt, dopg, dgate, g["w_out"] = _mixout_bwd(att, opg, gate, w["w_out"], dx2, dep)
    dscan = _scan_bwd(seqs, pars, states, dopg)
    for n, d in zip(("r_k", "ln_x_w", "ln_x_b"), dscan[6:]):
        g[n] = d
    dcur, dvec, *dmats = _rwkv_pre_bwd(cur, vec, mats, dscan[:6], dgate)
    for n, d in zip(RWKV_MAT, dmats):
        g[n] = d
    for j, n in enumerate(RWKV_VEC):
        g[n] = dvec[j:j + 1]
    dq, dk, dv, g["q_norm"], g["k_norm"] = _att_bwd(q, k, v, qn, kn, saved, datt)
    dx1, g["mix_norm"], g["w_in"] = _proj_bwd(x1, w["mix_norm"], w["w_in"], dq, dk, dv, dcur, dx2)
    dep = ex.send_mix({n: g[n] for n in ("w_in", "w_out") + RWKV_MAT}, (dx1,))
    dx, g["ffn1_norm"], g["ffn1_w_gate"], g["ffn1_w_up"], g["ffn1_w_down"] = _ffn_bwd(
        x, w["ffn1_norm"], w["ffn1_w_gate"], w["ffn1_w_up"], w["ffn1_w_down"], gate1, up1, dx1, dep, "ffn1_bwd")
    return loss, dx, g


N_SHARDS = 4


def _place():
    return lax.axis_index("x"), lax.axis_index("y"), lax.axis_index("c")


def _chip_peers(x, y):
    return [(1 - x, y), (x, 1 - y), (1 - x, 1 - y)]


HBM = pl.BlockSpec(memory_space=pltpu.HBM)
SEM = pl.BlockSpec(memory_space=pltpu.SEMAPHORE)
DEP_SHAPE = (8, 128)


class _Views:
    to_sibling = False


class _GatherViews(_Views):
    @staticmethod
    def send(i, srcs, lands, k, at):
        return srcs[i], lands[i].at[at[3]]

    @staticmethod
    def landing(i, srcs, lands, k, at):
        return srcs[i], lands[i].at[2 * at[4] + at[5]]


class _ScatterViews(_Views):
    @staticmethod
    def send(i, srcs, lands, k, at):
        return srcs[i].at[2 * at[4] + at[5]], lands[i].at[k]

    @staticmethod
    def landing(i, srcs, lands, k, at):
        return srcs[i].at[at[3]], lands[i].at[k]


def _half_rows(ref, slot, half):
    rows = ref.shape[1] // 2
    return ref.at[slot, pl.ds(pl.multiple_of(half * rows, BF16_SUBLANES), rows)]


class _HalfGatherViews(_Views):
    @staticmethod
    def send(i, srcs, lands, k, at):
        rows = srcs[i].shape[0] // 2
        return srcs[i].at[pl.ds(pl.multiple_of(at[2] * rows, BF16_SUBLANES), rows)], _half_rows(lands[i], at[3], at[2])

    @staticmethod
    def landing(i, srcs, lands, k, at):
        rows = srcs[i].shape[0] // 2
        return srcs[i].at[pl.ds(pl.multiple_of(at[2] * rows, BF16_SUBLANES), rows)], _half_rows(lands[i], 2 * at[4] + at[5], at[2])


class _ForwardViews(_Views):
    to_sibling = True

    @staticmethod
    def send(i, srcs, lands, k, at):
        mine = _half_rows(lands[i], 2 * at[4] + at[5], at[2])
        return mine, mine

    @staticmethod
    def landing(i, srcs, lands, k, at):
        theirs = _half_rows(lands[i], 2 * at[4] + at[5], 1 - at[2])
        return theirs, theirs


def _push_start(srcs, lands, views, after, name):
    ns, nl = len(srcs), len(lands)

    def body(*refs):
        src_refs, land_refs = refs[:ns], refs[ns:ns + nl]
        send_sems, recv_sems = refs[ns + nl + 1:ns + nl + 3]
        token = refs[2 * (ns + nl) + 3]
        x, y, c = _place()
        for i in range(nl):
            for k, (px, py) in enumerate(_chip_peers(x, y)):
                src, dst = views.send(i, src_refs, land_refs, k, (x, y, c, 2 * x + y, px, py))
                pltpu.make_async_remote_copy(
                    src_ref=src, dst_ref=dst, send_sem=send_sems.at[3 * i + k], recv_sem=recv_sems.at[3 * i + k],
                    device_id=(x, y, 1 - c) if views.to_sibling else (px, py, c), device_id_type=MESH).start()
        token[...] = jnp.zeros_like(token)

    sems = pltpu.SemaphoreType.DMA((3 * nl,))
    both = [pltpu.with_memory_space_constraint(a, pltpu.HBM) for a in (*srcs, *lands)]
    outs = pl.pallas_call(
        body, name=name,
        out_shape=(sems, sems, *[pltpu.HBM(a.shape, a.dtype) for a in both], jax.ShapeDtypeStruct(DEP_SHAPE, F32)),
        in_specs=[HBM] * (ns + nl) + [ANY], out_specs=(SEM, SEM, *[HBM] * (ns + nl), VMEM_FULL),
        input_output_aliases={i: 2 + i for i in range(ns + nl)},
        compiler_params=pltpu.CompilerParams(has_side_effects=pltpu.SideEffectType.DATAFLOW_SIDE_EFFECTING),
    )(*both, after)
    return outs[0], outs[1], outs[2:2 + ns], outs[2 + ns:2 + ns + nl], outs[2 + ns + nl]


def _push_wait(started, views, after, name, with_sources=False):
    send_sems, recv_sems, srcs, lands, _ = started
    ns, nl = len(srcs), len(lands)

    def body(*refs):
        src_refs, land_refs = refs[:ns], refs[ns:ns + nl]
        send_sems, recv_sems = refs[ns + nl:ns + nl + 2]
        x, y, c = _place()
        for i in range(nl):
            for k, (px, py) in enumerate(_chip_peers(x, y)):
                src, dst = views.landing(i, src_refs, land_refs, k, (x, y, c, 2 * x + y, px, py))
                landing = pltpu.make_async_remote_copy(
                    src_ref=src, dst_ref=dst, send_sem=send_sems.at[3 * i + k], recv_sem=recv_sems.at[3 * i + k],
                    device_id=(x, y, 1 - c) if views.to_sibling else (px, py, c), device_id_type=MESH)
                landing.wait_send()
                landing.wait_recv()

    outs = pl.pallas_call(
        body, name=name,
        out_shape=tuple(pltpu.HBM(a.shape, a.dtype) for a in (*srcs, *lands)),
        in_specs=[HBM] * (ns + nl) + [SEM, SEM] + [ANY] * len(after), out_specs=(HBM,) * (ns + nl),
        input_output_aliases={i: i for i in range(ns + nl)},
        compiler_params=pltpu.CompilerParams(has_side_effects=pltpu.SideEffectType.DATAFLOW_SIDE_EFFECTING),
    )(*srcs, *lands, send_sems, recv_sems, *after)
    return outs if with_sources else outs[ns:]


def _empty_lands(shards, slots, own_slot):
    lands = [lax.empty((slots,) + s.shape, s.dtype) for s in shards]
    if own_slot:
        me = 2 * lax.axis_index("x") + lax.axis_index("y")
        lands = [lax.dynamic_update_index_in_dim(z, s, me, 0) for z, s in zip(lands, shards)]
    return lands


def _sibling_swap(arrays, name, other_half=False):
    n = len(arrays)

    def body(*refs):
        ins, outs = refs[:n], refs[n:2 * n]
        send_sems, recv_sems = refs[2 * n:]
        x, y, c = _place()
        copies = []
        for i in range(n):
            src = ins[i]
            if other_half:
                rows = src.shape[1] // 2
                src = src.at[:, pl.ds(pl.multiple_of((1 - c) * rows, BF16_SUBLANES), rows)]
            cp = pltpu.make_async_remote_copy(
                src_ref=src, dst_ref=outs[i], send_sem=send_sems.at[i], recv_sem=recv_sems.at[i],
                device_id=(x, y, 1 - c), device_id_type=MESH)
            cp.start()
            copies.append(cp)
        for cp in copies:
            cp.wait()

    shapes = [(a.shape[0], a.shape[1] // 2, a.shape[2]) if other_half else a.shape for a in arrays]
    return pl.pallas_call(
        body, name=name,
        out_shape=tuple(jax.ShapeDtypeStruct(s, a.dtype) for s, a in zip(shapes, arrays)),
        in_specs=[ANY] * n, out_specs=(ANY,) * n,
        scratch_shapes=[pltpu.SemaphoreType.DMA((n,)), pltpu.SemaphoreType.DMA((n,))],
    )(*arrays)


FOLD_STEPS = 2


def _fold_add(core, parts, theirs, name):
    n = len(parts)
    s, r, cols = parts[0].shape
    tr = r // 2 // FOLD_STEPS

    def body(core_ref, *refs):
        for p_ref, t_ref, o_ref in zip(refs[:n], refs[n:2 * n], refs[2 * n:]):
            o_ref[...] = (p_ref[...].astype(F32) + t_ref[...].astype(F32)).astype(BF16)

    half = pl.BlockSpec((1, tr, cols), lambda j, i, core_ref: (j, i, 0))
    return pl.pallas_call(
        body, name=name, out_shape=tuple(jax.ShapeDtypeStruct((s, r // 2, cols), BF16) for _ in parts),
        grid_spec=pltpu.PrefetchScalarGridSpec(
            num_scalar_prefetch=1, grid=(s, FOLD_STEPS),
            in_specs=[pl.BlockSpec((1, tr, cols), lambda j, i, core_ref: (j, core_ref[0] * FOLD_STEPS + i, 0))] * n + [half] * n,
            out_specs=(half,) * n),
        compiler_params=_params("arbitrary", "arbitrary"),
    )(core, *parts, *theirs)


N_DEV = 8


def _allreduce_small(pack):
    def body(in_ref, out_ref, buf, send_sems, recv_sems):
        x, y, c = _place()
        me = 4 * x + 2 * y + c
        buf[me] = in_ref[...]

        def copy(j, slot):
            px, py, pc = x ^ (j >> 2), y ^ ((j >> 1) & 1), c ^ (j & 1)
            return pltpu.make_async_remote_copy(
                src_ref=in_ref, dst_ref=buf.at[slot(px, py, pc)], send_sem=send_sems.at[j], recv_sem=recv_sems.at[j],
                device_id=(px, py, pc), device_id_type=MESH)

        for j in range(1, N_DEV):
            copy(j, lambda px, py, pc: me).start()
        for j in range(1, N_DEV):
            landing = copy(j, lambda px, py, pc: 4 * px + 2 * py + pc)
            landing.wait_send()
            landing.wait_recv()
        acc = buf[0]
        for s in range(1, N_DEV):
            acc = acc + buf[s]
        out_ref[...] = acc

    return pl.pallas_call(
        body, name="allreduce_small", out_shape=jax.ShapeDtypeStruct(pack.shape, F32),
        in_specs=[VMEM_FULL], out_specs=VMEM_FULL,
        scratch_shapes=[pltpu.VMEM((N_DEV,) + pack.shape, F32), pltpu.SemaphoreType.DMA((N_DEV,)),
                        pltpu.SemaphoreType.DMA((N_DEV,))],
    )(pack)


BF16_SUBLANES = 16


def _reduce_own(me, parts, recvs, dep, steps, name):
    n = len(parts)

    def body(me_ref, *refs):
        for p_ref, rv_ref, o_ref in zip(refs[:n], refs[n:2 * n], refs[2 * n + 1:]):
            acc = p_ref[0].astype(F32)
            for k in range(3):
                acc = acc + rv_ref[k].astype(F32)
            o_ref[...] = acc

    shapes = [(p.shape[1] // steps, p.shape[2]) for p in parts]
    return pl.pallas_call(
        body, name=name, out_shape=tuple(jax.ShapeDtypeStruct(p.shape[1:], F32) for p in parts),
        grid_spec=pltpu.PrefetchScalarGridSpec(
            num_scalar_prefetch=1, grid=(steps,),
            in_specs=[pl.BlockSpec((1, tr, c), lambda i, me_ref: (me_ref[0], i, 0)) for tr, c in shapes]
            + [pl.BlockSpec((3, tr, c), lambda i, me_ref: (0, i, 0)) for tr, c in shapes] + [ANY],
            out_specs=tuple(pl.BlockSpec((tr, c), lambda i, me_ref: (i, 0)) for tr, c in shapes)),
        compiler_params=_params("arbitrary"),
    )(me, *parts, *recvs, dep)


def _adamw(ws, gas, gbs, ms, vs, steps, name):
    n = len(ws)
    c1 = 1.0 - ADAM_B1 ** ADAM_STEP
    c2 = 1.0 - ADAM_B2 ** ADAM_STEP
    operands = [ws, gas, ms, vs] if gbs is None else [ws, gas, gbs, ms, vs]
    k = len(operands)

    def body(*refs):
        ins, outs = refs[:k * n], refs[k * n:]
        for j in range(n):
            w_ref, ga_ref, *gb_ref, m_ref, v_ref = ins[j::n]
            g_out, d_out, m_out, v_out = outs[j::n]
            g = ga_ref[...] + gb_ref[0][...] if gb_ref else ga_ref[...]
            mn = ADAM_B1 * m_ref[...] + (1.0 - ADAM_B1) * g
            vn = ADAM_B2 * v_ref[...] + (1.0 - ADAM_B2) * (g * g)
            g_out[...] = g
            m_out[...] = mn
            v_out[...] = vn
            d_out[...] = -ADAM_LR * ((mn / c1) / (jnp.sqrt(vn / c2) + ADAM_EPS) + ADAM_WD * w_ref[...])

    tiles = [pl.BlockSpec((w.shape[0] // steps, w.shape[1]), lambda i: (i, 0)) for w in ws]
    shapes = [jax.ShapeDtypeStruct(w.shape, F32) for w in ws]
    outs = pl.pallas_call(
        body, name=name, grid=(steps,), out_shape=tuple(shapes * 4), in_specs=tiles * k, out_specs=tuple(tiles * 4),
        compiler_params=_params("arbitrary"),
    )(*[a for group in operands for a in group])
    return [outs[j::n] for j in range(n)]


PACK_COLS = 512


def _to_rows(a):
    flat = a.reshape(-1)
    pad = (-flat.shape[0]) % PACK_COLS
    return jnp.pad(flat, (0, pad)).reshape(-1, PACK_COLS)


def _pack(arrays, extra_rows=0):
    rows = [_to_rows(a) for a in arrays]
    n = sum(r.shape[0] for r in rows) + extra_rows
    pad = (-n) % 8
    return jnp.concatenate(rows + [jnp.zeros((extra_rows + pad, PACK_COLS), F32)], axis=0)


def _unpack(pack, like):
    out, at = [], 0
    for a in like:
        n = -(-a.size // PACK_COLS)
        out.append(pack[at:at + n].reshape(-1)[:a.size].reshape(a.shape))
        at += n
    return out


COL_SHARDED = ("ffn1_w_gate", "ffn1_w_up", "w_in", "ffn2_w_gate", "ffn2_w_up", "w1", "w2", "a1", "a2", "g1", "g2")
ROW_SHARDED = ("ffn1_w_down", "ffn2_w_down", "w_out")
CHUNKED = ("ffn1_w_gate", "ffn1_w_up", "ffn1_w_down", "w_in", "ffn2_w_gate", "ffn2_w_up", "ffn2_w_down")
WEIGHTS = ("ffn1_norm", "ffn1_w_gate", "ffn1_w_up", "ffn1_w_down", "mix_norm", "w_in", "q_norm", "k_norm",
           "mu_r", "mu_k", "mu_v", "mu_w", "mu_a", "mu_g", "w0", "w1", "w2", "a0", "a1", "a2", "g1", "g2",
           "k_k", "k_a", "r_k", "ln_x_w", "ln_x_b", "w_out", "ffn2_norm", "ffn2_w_gate", "ffn2_w_up", "ffn2_w_down")


TRANSPOSED = ("ffn1_w_gate", "ffn1_w_up", "ffn2_w_gate", "ffn2_w_up", "w1", "a1", "g1")


def _shard_2d(name, a):
    return a[0].T if name in TRANSPOSED else a[0]


def _full_from_blocks(name, blocks):
    if name in CHUNKED:
        return blocks
    if name in ROW_SHARDED:
        return blocks.reshape(-1, blocks.shape[-1])
    return blocks.transpose(1, 0, 2).reshape(blocks.shape[1], -1)


def _blocks_from_full(name, full):
    if name in CHUNKED:
        return full
    if name in ROW_SHARDED:
        return full.reshape(N_SHARDS, -1, full.shape[-1])
    return full.reshape(full.shape[0], N_SHARDS, -1).transpose(1, 0, 2)


FFN1_GROUP = ("ffn1_w_gate", "ffn1_w_up", "ffn1_w_down")
MIX_GROUP = ("w_in",) + RWKV_MAT
OUT_GROUP = ("w_out", "ffn2_w_gate", "ffn2_w_up", "ffn2_w_down")
FFN2_GROUP = OUT_GROUP[1:]
LATE_GROUP = ("w_in", "w_out") + RWKV_MAT


class _Exchange:
    def __init__(self, given):
        self.given = given
        first = self._gather_start(FFN1_GROUP, _HalfGatherViews, jnp.zeros(DEP_SHAPE, F32), "gather_ffn1_start")
        self.mix = self._gather_start(MIX_GROUP, _GatherViews, first[4], "gather_mix_start")
        self.out = self._gather_start(OUT_GROUP, _GatherViews, self.mix[4], "gather_out_start")
        self.first_dep = self.out[4]
        halves = _push_wait(first, _HalfGatherViews, (self.first_dep,), "gather_ffn1_wait")
        passed = _push_start([], halves, _ForwardViews, jnp.zeros(DEP_SHAPE, F32), "gather_ffn1_pass_start")
        self.first_weights = self._full(FFN1_GROUP, _push_wait(passed, _ForwardViews, (passed[4],), "gather_ffn1_pass_wait"))
        self.parts, self.recv = {}, {}

    @staticmethod
    def _full(names, blocks):
        out = {}
        for n, b in zip(names, blocks):
            full = _full_from_blocks(n, b)
            out[n] = full.astype(F32) if n in RWKV_MAT else full
        return out

    def _gather_start(self, names, views, after, name):
        after, raw = lax.optimization_barrier((after, [_shard_2d(n, self.given[n]) for n in names]))
        shards = [a.astype(BF16) for a in raw]
        return _push_start(shards, _empty_lands(shards, N_SHARDS, True), views, after, name)

    def mix_weights(self, after):
        return self._full(MIX_GROUP, _push_wait(self.mix, _GatherViews, after, "gather_mix_wait"))

    def out_weights(self, after):
        return self._full(OUT_GROUP, _push_wait(self.out, _GatherViews, after, "gather_out_wait"))

    def _scatter_start(self, grads, name):
        names = tuple(grads)
        parts = [_blocks_from_full(n, grads[n]) for n in names]
        self.parts.update(zip(names, parts))
        lands = [lax.empty((3,) + p.shape[1:], BF16) for p in parts]
        return _push_start([p.astype(BF16) for p in parts], lands, _ScatterViews, jnp.zeros(DEP_SHAPE, F32), name)

    def _scatter_done(self, started, names, after, name):
        outs = _push_wait(started, _ScatterViews, after, name, with_sources=True)
        for n, sent, got in zip(names, outs[:len(names)], outs[len(names):]):
            self.recv[n] = got
            if self.parts[n].dtype == BF16:
                self.parts[n] = sent

    def send_ffn2(self, grads):
        self.ffn2 = self._scatter_start(grads, "scatter_ffn2_start")
        return self.ffn2[4]

    def send_mix(self, grads, after):
        self._scatter_done(self.ffn2, FFN2_GROUP, after, "scatter_ffn2_wait")
        self.late = self._scatter_start(grads, "scatter_late_start")
        return self.late[4]

    def send_ffn1(self, grads):
        self.ffn1 = self._scatter_start(grads, "scatter_ffn1_start")
        return self.ffn1[4]

    def late_received(self, after):
        self._scatter_done(self.late, LATE_GROUP, after, "scatter_late_wait")

    def ffn1_received(self, after):
        self._scatter_done(self.ffn1, FFN1_GROUP, after, "scatter_ffn1_wait")


def kernel(
        x, ffn1_norm, ffn1_w_gate, ffn1_w_up, ffn1_w_down, mix_norm, w_in, q_norm, k_norm, mu_r, mu_k, mu_v, mu_w,
        mu_a, mu_g, w0, w1, w2, a0, a1, a2, g1, g2, k_k, k_a, r_k, ln_x_w, ln_x_b, w_out, ffn2_norm, ffn2_w_gate,
        ffn2_w_up, ffn2_w_down, loss_target, m_ffn1_norm, m_ffn1_w_gate, m_ffn1_w_up, m_ffn1_w_down, m_mix_norm,
        m_w_in, m_q_norm, m_k_norm, m_mu_r, m_mu_k, m_mu_v, m_mu_w, m_mu_a, m_mu_g, m_w0, m_w1, m_w2, m_a0, m_a1,
        m_a2, m_g1, m_g2, m_k_k, m_k_a, m_r_k, m_ln_x_w, m_ln_x_b, m_w_out, m_ffn2_norm, m_ffn2_w_gate, m_ffn2_w_up,
        m_ffn2_w_down, v_ffn1_norm, v_ffn1_w_gate, v_ffn1_w_up, v_ffn1_w_down, v_mix_norm, v_w_in, v_q_norm, v_k_norm,
        v_mu_r, v_mu_k, v_mu_v, v_mu_w, v_mu_a, v_mu_g, v_w0, v_w1, v_w2, v_a0, v_a1, v_a2, v_g1, v_g2, v_k_k, v_k_a,
        v_r_k, v_ln_x_w, v_ln_x_b, v_w_out, v_ffn2_norm, v_ffn2_w_gate, v_ffn2_w_up, v_ffn2_w_down):
    given = dict(locals())
    sharded = COL_SHARDED + ROW_SHARDED
    sharded = tuple(n for n in WEIGHTS if n in sharded)
    small = tuple(n for n in WEIGHTS if n not in sharded)

    ex = _Exchange(given)
    w = {n: given[n] for n in small}
    w.update(ex.first_weights)
    loss, dx, g = _local_step(x[0], loss_target[0], w, ex)

    core = lax.axis_index("c").astype(jnp.int32).reshape(1)
    late = [g[n] for n in FFN1_GROUP]
    folded = _fold_add(core, late, _sibling_swap(late, "fold_swap_ffn1", other_half=True), "fold_add_ffn1")
    dep = ex.send_ffn1(dict(zip(FFN1_GROUP, folded)))

    me = (2 * lax.axis_index("x") + lax.axis_index("y")).astype(jnp.int32).reshape(1)
    out = {}

    def settle(names, dep, tag):
        done = []
        for kind, sub, r_steps, a_steps in (("large", tuple(n for n in names if n not in RWKV_MAT), 4, 8),
                                            ("small", tuple(n for n in names if n in RWKV_MAT), 1, 1)):
            if not sub:
                continue
            parts = [ex.parts[n].reshape(N_SHARDS, -1, ex.parts[n].shape[-1]) for n in sub]
            recvs = [ex.recv[n].reshape(3, -1, ex.recv[n].shape[-1]) for n in sub]
            mine = _reduce_own(me, parts, recvs, dep, r_steps, f"reduce_{tag}_{kind}")
            theirs = _sibling_swap(mine, f"sibling_swap_{tag}_{kind}")
            res = _adamw([_shard_2d(n, given[n]) for n in sub], mine, theirs, [_shard_2d(n, given["m_" + n]) for n in sub],
                         [_shard_2d(n, given["v_" + n]) for n in sub], a_steps, f"adamw_{tag}_{kind}")
            for n, rs in zip(sub, res):
                out[n] = [(r.T if n in TRANSPOSED else r).reshape(given[n].shape) for r in rs]
                done.append(out[n][1])
        return tuple(done)

    ex.late_received((dep,))
    last = settle(tuple(n for n in sharded if n not in FFN1_GROUP), dep, "rest")

    gpack = _pack([g[n] for n in small], extra_rows=1)
    n_rows = sum(-(-given[n].size // PACK_COLS) for n in small)
    gpack = gpack.at[n_rows, :loss.shape[1]].set(loss[0])
    gsum = _allreduce_small(gpack)
    res = _adamw([_pack([given[n] for n in small], 1)], [gsum], None, [_pack([given["m_" + n] for n in small], 1)],
                 [_pack([given["v_" + n] for n in small], 1)], 1, "adamw_replicated")[0]
    like = [given[n] for n in small]
    for j, r in enumerate(res):
        for n, a in zip(small, _unpack(r, like)):
            out.setdefault(n, [None] * 4)[j] = a
    total_loss = gsum[n_rows, 0]

    ex.ffn1_received((*last, res[1]))
    halves = _reduce_own(me, [ex.parts[n] for n in FFN1_GROUP], [ex.recv[n] for n in FFN1_GROUP],
                         jnp.zeros(DEP_SHAPE, F32), FOLD_STEPS, "reduce_ffn1")
    others = _sibling_swap(halves, "sibling_swap_ffn1")
    first = lax.axis_index("c") == 0
    grads = [jnp.concatenate([jnp.where(first, a, b), jnp.where(first, b, a)], axis=0) for a, b in zip(halves, others)]
    res = _adamw([_shard_2d(n, given[n]) for n in FFN1_GROUP], grads, None, [_shard_2d(n, given["m_" + n]) for n in FFN1_GROUP],
                 [_shard_2d(n, given["v_" + n]) for n in FFN1_GROUP], 8, "adamw_ffn1")
    for n, rs in zip(FFN1_GROUP, res):
        out[n] = [(r.T if n in TRANSPOSED else r).reshape(given[n].shape) for r in rs]
    return (total_loss, dx[None], *[out[n][0] for n in WEIGHTS], *[out[n][1] for n in WEIGHTS],
            *[out[n][2] for n in WEIGHTS], *[out[n][3] for n in WEIGHTS])
```

```python
import functools

import jax
import jax.numpy as jnp
from jax import lax
from jax.experimental import pallas as pl
from jax.experimental.pallas import tpu as pltpu

F32 = jnp.float32
BF16 = jnp.bfloat16
MESH = pl.DeviceIdType.MESH

RMS_EPS = 1e-6
GN_EPS = 64e-5
NEG_INF = -1e30
FFN_RESIDUAL = 0.5
HEAD_DIM = 64
ATT_BLOCK = 128
DILATIONS = (1, 4, 16)
SCAN_CHUNK = 64
TOKEN_TILE = 256
FFN_BWD_TILE = 512

ADAM_LR = 0.001
ADAM_B1 = 0.9
ADAM_B2 = 0.999
ADAM_EPS = 1e-08
ADAM_WD = 0.01
ADAM_STEP = 10

VMEM_FULL = pl.BlockSpec(memory_space=pltpu.VMEM)
ANY = pl.BlockSpec(memory_space=pl.ANY)


VMEM_LIMIT = 56 * 1024 * 1024


def _params(*sem):
    return pltpu.CompilerParams(dimension_semantics=sem, vmem_limit_bytes=VMEM_LIMIT)


def _dot(a, b, dims):
    return lax.dot_general(a.astype(BF16), b.astype(BF16), (dims, ((), ())), preferred_element_type=F32)


def _dot_nn(a, b):
    return _dot(a, b, ((1,), (0,)))


def _dot_nt(a, b):
    return _dot(a, b, ((1,), (1,)))


def _dot_tn(a, b):
    return _dot(a, b, ((0,), (0,)))


@jax.custom_vjp
def _mm(a, b):
    return _dot_nn(a, b)


def _mm_fwd(a, b):
    return _dot_nn(a, b), (a, b)


def _mm_bwd(res, g):
    a, b = res
    return _dot_nt(g, b).astype(a.dtype), _dot_tn(a, g).astype(b.dtype)


_mm.defvjp(_mm_fwd, _mm_bwd)


@jax.custom_vjp
def _mm_nt(a, bt):
    return _dot_nt(a, bt)


def _mm_nt_fwd(a, bt):
    return _dot_nt(a, bt), (a, bt)


def _mm_nt_bwd(res, g):
    a, bt = res
    return _dot_nn(g, bt).astype(a.dtype), _dot_tn(g, a).astype(bt.dtype)


_mm_nt.defvjp(_mm_nt_fwd, _mm_nt_bwd)


def _bdot(a, b, ca, cb):
    return lax.dot_general(a.astype(BF16), b.astype(BF16), (((ca,), (cb,)), ((0,), (0,))), preferred_element_type=F32)


@jax.custom_vjp
def _bmm_nt(a, b):
    return _bdot(a, b, 2, 2)


def _bmm_nt_fwd(a, b):
    return _bdot(a, b, 2, 2), (a, b)


def _bmm_nt_bwd(res, g):
    a, b = res
    return _bdot(g, b, 2, 1), _bdot(g, a, 1, 1)


_bmm_nt.defvjp(_bmm_nt_fwd, _bmm_nt_bwd)


@jax.custom_vjp
def _bmm_nn(a, b):
    return _bdot(a, b, 2, 1)


def _bmm_nn_fwd(a, b):
    return _bdot(a, b, 2, 1), (a, b)


def _bmm_nn_bwd(res, g):
    a, b = res
    return _bdot(g, b, 2, 2), _bdot(a, g, 1, 1)


_bmm_nn.defvjp(_bmm_nn_fwd, _bmm_nn_bwd)


@jax.custom_vjp
def _bmm_tn(a, b):
    return _bdot(a, b, 1, 1)


def _bmm_tn_fwd(a, b):
    return _bdot(a, b, 1, 1), (a, b)


def _bmm_tn_bwd(res, g):
    a, b = res
    return _bdot(b, g, 2, 2), _bdot(a, g, 2, 1)


_bmm_tn.defvjp(_bmm_tn_fwd, _bmm_tn_bwd)


def _hdot(a, b, ca, cb):
    return lax.dot_general(a, b, (((ca,), (cb,)), ((0,), (0,))), precision=lax.Precision.HIGH, preferred_element_type=F32)


def _sigmoid(x):
    return 1.0 / (1.0 + jnp.exp(-x))


def _rms(x):
    return lax.rsqrt(jnp.mean(x * x, axis=-1, keepdims=True) + RMS_EPS)


def _ffn_fwd(x, norm, wg, wu, wd, dep, name, target=None):
    t, d = x.shape
    nc, fc, _ = wg.shape
    tm = TOKEN_TILE

    def body(x_ref, n_ref, wg_ref, wu_ref, wd_ref, dep_ref, *rest):
        o_ref, g_ref, u_ref = rest[-3:] if target is None else rest[1:4]
        xv = x_ref[...]
        h = (xv * _rms(xv) * n_ref[...]).astype(BF16)
        acc = jnp.zeros((tm, d), F32)
        for c in range(nc):
            g = _dot_nt(h, wg_ref[c])
            u = _dot_nt(h, wu_ref[c])
            g_ref[c] = g.astype(BF16)
            u_ref[c] = u.astype(BF16)
            a = (g * _sigmoid(g) * u).astype(BF16)
            acc = acc + jnp.dot(a, wd_ref[c], preferred_element_type=F32)
        y = xv + FFN_RESIDUAL * acc
        if target is None:
            o_ref[...] = y
        else:
            t_ref, loss_ref = rest[0], rest[4]
            err = y - t_ref[...]
            o_ref[...] = err * (1.0 / d)
            part = 0.5 * jnp.sum(jnp.mean(err * err, axis=-1, keepdims=True), axis=0, keepdims=True)

            @pl.when(pl.program_id(0) == 0)
            def _():
                loss_ref[...] = jnp.zeros_like(loss_ref)

            loss_ref[...] += jnp.broadcast_to(part, loss_ref.shape)

    tile = pl.BlockSpec((tm, d), lambda i: (i, 0))
    hidden = pl.BlockSpec((nc, tm, fc), lambda i: (0, i, 0))
    hshape = jax.ShapeDtypeStruct((nc, t, fc), BF16)
    with_loss = target is not None
    return pl.pallas_call(
        body, name=name, grid=(t // tm,),
        out_shape=(jax.ShapeDtypeStruct((t, d), F32), hshape, hshape) + ((jax.ShapeDtypeStruct((1, 128), F32),) if with_loss else ()),
        in_specs=[tile, pl.BlockSpec((1, d), lambda i: (0, 0)), VMEM_FULL, VMEM_FULL, VMEM_FULL, ANY] + ([tile] if with_loss else []),
        out_specs=(tile, hidden, hidden) + ((pl.BlockSpec((1, 128), lambda i: (0, 0)),) if with_loss else ()),
        compiler_params=_params("arbitrary"),
    )(x, norm, wg, wu, wd, dep, *((target,) if with_loss else ()))


def _rmsnorm_bwd(xv, gain, dh):
    rs = _rms(xv)
    xn = xv * rs
    dxn = dh * gain
    dx = rs * (dxn - xn * jnp.mean(dxn * xn, axis=-1, keepdims=True))
    return dx, jnp.sum(dh * xn, axis=0, keepdims=True)


def _ffn_bwd(x, norm, wg, wu, wd, gate, up, dy, dep, name):
    t, d = x.shape
    nc, fc, _ = wg.shape
    tm = FFN_BWD_TILE
    nt = t // tm

    def body(x_ref, n_ref, wg_ref, wu_ref, wd_ref, g_ref, u_ref, dy_ref, dep_ref, dx_ref, dn_ref, dwg_ref, dwu_ref,
             dwd_ref, dh_ref, ag_ref, au_ref, ad_ref):
        c, i = pl.program_id(0), pl.program_id(1)
        rows = pl.ds(pl.multiple_of(i * tm, tm), tm)
        xv = x_ref[...]
        gain = n_ref[...]
        h = (xv * _rms(xv) * gain).astype(BF16)
        dy = dy_ref[...]
        dyb = (FFN_RESIDUAL * dy).astype(BF16)
        g = g_ref[0].astype(F32)
        u = u_ref[0].astype(F32)
        sg = _sigmoid(g)
        s = g * sg
        a = (s * u).astype(BF16)
        da = _dot_nt(dyb, wd_ref[0])
        dub = (da * s).astype(BF16)
        dgb = (da * u * (sg * (1.0 + g * (1.0 - sg)))).astype(BF16)
        dwd_c = _dot_tn(a, dyb)
        dwg_c = _dot_tn(dgb, h)
        dwu_c = _dot_tn(dub, h)
        dh_c = _dot_nn(dgb, wg_ref[0]) + _dot_nn(dub, wu_ref[0])

        @pl.when(i == 0)
        def _():
            ad_ref[...] = dwd_c
            ag_ref[...] = dwg_c
            au_ref[...] = dwu_c

        @pl.when(i > 0)
        def _():
            ad_ref[...] += dwd_c
            ag_ref[...] += dwg_c
            au_ref[...] += dwu_c

        @pl.when(i == nt - 1)
        def _():
            dwd_ref[0] = ad_ref[...].astype(BF16)
            dwg_ref[0] = ag_ref[...].astype(BF16)
            dwu_ref[0] = au_ref[...].astype(BF16)

        @pl.when(c == 0)
        def _():
            dh_ref[rows, :] = dh_c

        @pl.when(c > 0)
        def _():
            dh_ref[rows, :] += dh_c

        @pl.when(c == nc - 1)
        def _():
            dx, dn = _rmsnorm_bwd(xv, gain, dh_ref[rows, :])
            dx_ref[...] = dx + dy

            @pl.when(i == 0)
            def _():
                dn_ref[...] = dn

            @pl.when(i > 0)
            def _():
                dn_ref[...] += dn

    tile = pl.BlockSpec((tm, d), lambda c, i: (i, 0))
    row = pl.BlockSpec((1, d), lambda c, i: (0, 0))
    wrow = pl.BlockSpec((1, fc, d), lambda c, i: (c, 0, 0), pipeline_mode=pl.Buffered(1))
    hidden = pl.BlockSpec((1, tm, fc), lambda c, i: (c, i, 0))
    last = pl.BlockSpec((tm, d), lambda c, i: (jnp.where(c == nc - 1, i, 0), 0))
    return pl.pallas_call(
        body, name=name, grid=(nc, nt),
        out_shape=(jax.ShapeDtypeStruct((t, d), F32), jax.ShapeDtypeStruct((1, d), F32),
                   jax.ShapeDtypeStruct(wg.shape, BF16), jax.ShapeDtypeStruct(wu.shape, BF16),
                   jax.ShapeDtypeStruct(wd.shape, BF16)),
        in_specs=[tile, row, wrow, wrow, wrow, hidden, hidden, tile, ANY],
        out_specs=(last, row, wrow, wrow, wrow),
        scratch_shapes=[pltpu.VMEM((t, d), F32)] + [pltpu.VMEM((fc, d), F32)] * 3,
        compiler_params=_params("arbitrary", "arbitrary"),
    )(x, norm, wg, wu, wd, gate, up, dy, dep)


def _store_heads(ref, v):
    for h in range(ref.shape[0]):
        ref[h] = v[:, h * HEAD_DIM:(h + 1) * HEAD_DIM]


def _load_heads(ref):
    return jnp.concatenate([ref[h] for h in range(ref.shape[0])], axis=-1)


N_HEAD_GROUPS = 3


def _proj_fwd(x, norm, w, c):
    t, d = x.shape
    nc, _, ncol = w.shape
    nh = c // HEAD_DIM
    tm = TOKEN_TILE
    wide = nc * ncol - N_HEAD_GROUPS * c

    def body(x_ref, n_ref, w_ref, q_ref, k_ref, v_ref, cur_ref):
        xv = x_ref[...]
        h = (xv * _rms(xv) * n_ref[...]).astype(BF16)
        full = jnp.concatenate([jnp.dot(h, w_ref[s], preferred_element_type=F32) for s in range(nc)], axis=1)
        for m, ref in enumerate((q_ref, k_ref, v_ref)):
            _store_heads(ref, full[:, m * c:(m + 1) * c])
        cur_ref[...] = full[:, N_HEAD_GROUPS * c:]

    heads = pl.BlockSpec((nh, tm, HEAD_DIM), lambda i: (0, i, 0))
    hshape = jax.ShapeDtypeStruct((nh, t, HEAD_DIM), F32)
    return pl.pallas_call(
        body, name="proj_fwd", grid=(t // tm,),
        out_shape=(hshape, hshape, hshape, jax.ShapeDtypeStruct((t, wide), F32)),
        in_specs=[pl.BlockSpec((tm, d), lambda i: (i, 0)), pl.BlockSpec((1, d), lambda i: (0, 0)), VMEM_FULL],
        out_specs=(heads, heads, heads, pl.BlockSpec((tm, wide), lambda i: (i, 0))),
        compiler_params=_params("arbitrary"),
    )(x, norm, w)


def _proj_bwd(x, norm, w, dq, dk, dv, dcur, dres):
    t, d = x.shape
    nc, _, ncol = w.shape
    nh = dq.shape[0]
    tm = TOKEN_TILE
    nt = t // tm
    wide = dcur.shape[1]

    def body(x_ref, n_ref, w_ref, dq_ref, dk_ref, dv_ref, dcur_ref, dres_ref, dx_ref, dn_ref, dw_ref, acc_ref):
        i = pl.program_id(0)

        @pl.when(i == 0)
        def _():
            acc_ref[...] = jnp.zeros_like(acc_ref)
            dn_ref[...] = jnp.zeros_like(dn_ref)

        xv = x_ref[...]
        gain = n_ref[...]
        h = (xv * _rms(xv) * gain).astype(BF16)
        dp = jnp.concatenate([_load_heads(dq_ref), _load_heads(dk_ref), _load_heads(dv_ref), dcur_ref[...]], axis=1).astype(BF16)
        dh = jnp.zeros((tm, d), F32)
        for s in range(nc):
            dps = dp[:, s * ncol:(s + 1) * ncol]
            acc_ref[s] += _dot_tn(h, dps)
            dh = dh + _dot_nt(dps, w_ref[s])
        dx, dn = _rmsnorm_bwd(xv, gain, dh)
        dx_ref[...] = dx + dres_ref[...]
        dn_ref[...] += dn

        @pl.when(i == nt - 1)
        def _():
            dw_ref[...] = acc_ref[...].astype(BF16)

    tile = pl.BlockSpec((tm, d), lambda i: (i, 0))
    row = pl.BlockSpec((1, d), lambda i: (0, 0))
    heads = pl.BlockSpec((nh, tm, HEAD_DIM), lambda i: (0, i, 0))
    return pl.pallas_call(
        body, name="proj_bwd", grid=(nt,),
        out_shape=(jax.ShapeDtypeStruct((t, d), F32), jax.ShapeDtypeStruct((1, d), F32),
                   jax.ShapeDtypeStruct(w.shape, BF16)),
        in_specs=[tile, row, VMEM_FULL, heads, heads, heads, pl.BlockSpec((tm, wide), lambda i: (i, 0)), tile],
        out_specs=(tile, row, VMEM_FULL),
        scratch_shapes=[pltpu.VMEM(w.shape, F32)], compiler_params=_params("arbitrary"),
    )(x, norm, w, dq, dk, dv, dcur, dres)


def _mixout_fwd(x, att, opg, gate, w):
    t, d = x.shape
    nh = att.shape[0]
    half = gate.shape[1]
    tm = TOKEN_TILE

    def body(x_ref, att_ref, opg_ref, g_ref, w_ref, o_ref):
        mix = jnp.concatenate([_load_heads(att_ref), _load_heads(opg_ref) * g_ref[...]], axis=-1).astype(BF16)
        o_ref[...] = x_ref[...] + jnp.dot(mix, w_ref[...], preferred_element_type=F32)

    tile = pl.BlockSpec((tm, d), lambda i: (i, 0))
    htile = pl.BlockSpec((tm, half), lambda i: (i, 0))
    heads = pl.BlockSpec((nh, tm, HEAD_DIM), lambda i: (0, i, 0))
    return pl.pallas_call(
        body, name="mixout_fwd", grid=(t // tm,), out_shape=jax.ShapeDtypeStruct((t, d), F32),
        in_specs=[tile, heads, heads, htile, VMEM_FULL], out_specs=tile, compiler_params=_params("arbitrary"),
    )(x, att, opg, gate, w)


def _mixout_bwd(att, opg, gate, w, dy, dep):
    nh, t, _ = att.shape
    half = gate.shape[1]
    d = dy.shape[1]
    tm = TOKEN_TILE

    def body(att_ref, opg_ref, g_ref, w_ref, dy_ref, dep_ref, datt_ref, dopg_ref, dg_ref, dw_ref):
        i = pl.program_id(0)
        opg_v, g_v = _load_heads(opg_ref), g_ref[...]
        mix = jnp.concatenate([_load_heads(att_ref), opg_v * g_v], axis=-1).astype(BF16)
        dyb = dy_ref[...].astype(BF16)
        dmix = _dot_nt(dyb, w_ref[...])
        dw = _dot_tn(mix, dyb)
        _store_heads(datt_ref, dmix[:, :half])
        drw = dmix[:, half:]
        _store_heads(dopg_ref, drw * g_v)
        dg_ref[...] = drw * opg_v

        @pl.when(i == 0)
        def _():
            dw_ref[...] = dw

        @pl.when(i > 0)
        def _():
            dw_ref[...] += dw

    tile = pl.BlockSpec((tm, d), lambda i: (i, 0))
    htile = pl.BlockSpec((tm, half), lambda i: (i, 0))
    heads = pl.BlockSpec((nh, tm, HEAD_DIM), lambda i: (0, i, 0))
    hshape = jax.ShapeDtypeStruct((nh, t, HEAD_DIM), F32)
    return pl.pallas_call(
        body, name="mixout_bwd", grid=(t // tm,),
        out_shape=(hshape, hshape, jax.ShapeDtypeStruct((t, half), F32), jax.ShapeDtypeStruct(w.shape, F32)),
        in_specs=[heads, heads, htile, VMEM_FULL, tile, ANY],
        out_specs=(heads, heads, htile, pl.BlockSpec(w.shape, lambda i: (0, 0))),
        compiler_params=_params("arbitrary"),
    )(att, opg, gate, w, dy, dep)


def _head_norm(x, gain):
    return x * _rms(x) * gain


def _att_pattern(qh, kh, v, nb):
    g, blk, _ = qh.shape
    scale = HEAD_DIM ** -0.5
    qi = lax.broadcasted_iota(jnp.int32, (blk, blk), 0)
    kj = lax.broadcasted_iota(jnp.int32, (blk, blk), 1)
    sc = jnp.where(kj <= qi, _bmm_nt(qh, kh) * scale, NEG_INF)
    top = jnp.max(sc, axis=-1, keepdims=True)
    if nb > 1:
        khp = jnp.concatenate([kh[:1], kh[:-1]], axis=0)
        vp = jnp.concatenate([v[:1], v[:-1]], axis=0)
        has_prev = lax.broadcasted_iota(jnp.int32, (g, 1, 1), 0) % nb != 0
        sp = jnp.where((kj >= qi) & has_prev, _bmm_nt(qh, khp) * scale, NEG_INF)
        top = jnp.maximum(top, jnp.max(sp, axis=-1, keepdims=True))
    m = lax.stop_gradient(top)
    pc = jnp.exp(sc - m)
    den = jnp.sum(pc, axis=-1, keepdims=True)
    acc = _bmm_nn(pc, v)
    if nb > 1:
        pp = jnp.exp(sp - m)
        den = den + jnp.sum(pp, axis=-1, keepdims=True)
        acc = acc + _bmm_nn(pp, vp)
    o = acc / den
    return o, jnp.broadcast_to(m + jnp.log(den), o.shape)


def _pattern_rows(t, dil):
    length = t // dil
    return [pl.ds(r, length, stride=dil) if dil > 1 else pl.ds(0, length) for r in range(dil)], length // ATT_BLOCK


def _take(ref, rows, nb):
    return jnp.concatenate([ref[0, r, :].reshape(nb, ATT_BLOCK, HEAD_DIM) for r in rows], axis=0)


def _put(ref, rows, nb, val):
    for j, r in enumerate(rows):
        ref[0, r, :] = val[j * nb:(j + 1) * nb].reshape(nb * ATT_BLOCK, HEAD_DIM)


def _put_add(ref, rows, nb, val):
    for j, r in enumerate(rows):
        ref[0, r, :] += val[j * nb:(j + 1) * nb].reshape(nb * ATT_BLOCK, HEAD_DIM)


def _merge_fn(o1, o2, o3, l1, l2, l3):
    m = lax.stop_gradient(jnp.maximum(jnp.maximum(l1, l2), l3))
    e1, e2, e3 = jnp.exp(l1 - m), jnp.exp(l2 - m), jnp.exp(l3 - m)
    return (e1 * o1 + e2 * o2 + e3 * o3) / (e1 + e2 + e3)


def _token_rows(j):
    return pl.ds(pl.multiple_of(j * ATT_BLOCK, ATT_BLOCK), ATT_BLOCK)


def _norm_rows(t, q_ref, k_ref, gq, gk, qh_ref, kh_ref):
    def step(j, carry):
        rows = _token_rows(j)
        qh_ref[0, rows, :] = _head_norm(q_ref[0, rows, :], gq[0])
        kh_ref[0, rows, :] = _head_norm(k_ref[0, rows, :], gk[0])
        return carry

    lax.fori_loop(0, t // ATT_BLOCK, step, 0)


def _att_head_specs(t):
    head = pl.BlockSpec((1, t, HEAD_DIM), lambda h: (h, 0, 0))
    gain = pl.BlockSpec((1, 1, HEAD_DIM), lambda h: (0, 0, 0))
    return head, gain


def _att_fwd(q, k, v, qn, kn):
    nh, t, dh = q.shape
    head, gain = _att_head_specs(t)

    def body(q_ref, k_ref, v_ref, qn_ref, kn_ref, att_ref, o1, o2, o3, l1, l2, l3, qh_ref, kh_ref):
        saved = (o1, o2, o3, l1, l2, l3)
        _norm_rows(t, q_ref, k_ref, qn_ref[...], kn_ref[...], qh_ref, kh_ref)
        for p, dil in enumerate(DILATIONS):
            rows, nb = _pattern_rows(t, dil)
            o, lse = _att_pattern(_take(qh_ref, rows, nb), _take(kh_ref, rows, nb), _take(v_ref, rows, nb), nb)
            _put(saved[p], rows, nb, o)
            _put(saved[3 + p], rows, nb, lse)

        def merge(j, carry):
            rows = _token_rows(j)
            att_ref[0, rows, :] = _merge_fn(*[r[0, rows, :] for r in saved])
            return carry

        lax.fori_loop(0, t // ATT_BLOCK, merge, 0)

    return pl.pallas_call(
        body, name="att_fwd", grid=(nh,), out_shape=(jax.ShapeDtypeStruct(q.shape, F32),) * 7,
        in_specs=[head, head, head, gain, gain], out_specs=(head,) * 7,
        scratch_shapes=[pltpu.VMEM((1, t, dh), F32)] * 2, compiler_params=_params("arbitrary"),
    )(q, k, v, qn, kn)


def _att_bwd(q, k, v, qn, kn, saved, datt):
    nh, t, dh = q.shape
    head, gain = _att_head_specs(t)

    def body(q_ref, k_ref, v_ref, qn_ref, kn_ref, o1, o2, o3, l1, l2, l3, datt_ref,
             dq_ref, dk_ref, dv_ref, dqn_ref, dkn_ref, qh_ref, kh_ref, dqh_ref, dkh_ref, *ct_refs):
        for ref in (dqh_ref, dkh_ref, dv_ref):
            ref[...] = jnp.zeros_like(ref)

        @pl.when(pl.program_id(0) == 0)
        def _():
            dqn_ref[...] = jnp.zeros_like(dqn_ref)
            dkn_ref[...] = jnp.zeros_like(dkn_ref)

        gq, gk = qn_ref[...], kn_ref[...]
        _norm_rows(t, q_ref, k_ref, gq, gk, qh_ref, kh_ref)

        def merge_cotangents(j, carry):
            rows = _token_rows(j)
            _, merge_vjp = jax.vjp(_merge_fn, *[r[0, rows, :] for r in (o1, o2, o3, l1, l2, l3)])
            for ref, val in zip(ct_refs, merge_vjp(datt_ref[0, rows, :])):
                ref[0, rows, :] = val
            return carry

        lax.fori_loop(0, t // ATT_BLOCK, merge_cotangents, 0)

        for p, dil in enumerate(DILATIONS):
            rows, nb = _pattern_rows(t, dil)
            _, pattern_vjp = jax.vjp(functools.partial(_att_pattern, nb=nb), _take(qh_ref, rows, nb), _take(kh_ref, rows, nb),
                                     _take(v_ref, rows, nb))
            dqh, dkh, dv = pattern_vjp((_take(ct_refs[p], rows, nb), _take(ct_refs[3 + p], rows, nb)))
            _put_add(dqh_ref, rows, nb, dqh)
            _put_add(dkh_ref, rows, nb, dkh)
            _put_add(dv_ref, rows, nb, dv)

        def norm_cotangents(j, carry):
            rows = _token_rows(j)
            out = []
            for x_ref, gain, dh_ref, dx_ref, acc in ((q_ref, gq, dqh_ref, dq_ref, carry[0]), (k_ref, gk, dkh_ref, dk_ref, carry[1])):
                _, norm_vjp = jax.vjp(_head_norm, x_ref[0, rows, :], gain[0])
                dx, dgain = norm_vjp(dh_ref[0, rows, :])
                dx_ref[0, rows, :] = dx
                out.append(acc + dgain)
            return tuple(out)

        zero = jnp.zeros((1, dh), F32)
        dgq, dgk = lax.fori_loop(0, t // ATT_BLOCK, norm_cotangents, (zero, zero))
        dqn_ref[0] += dgq
        dkn_ref[0] += dgk

    hshape = jax.ShapeDtypeStruct(q.shape, F32)
    gshape = jax.ShapeDtypeStruct((1, 1, dh), F32)
    return pl.pallas_call(
        body, name="att_bwd", grid=(nh,), out_shape=(hshape, hshape, hshape, gshape, gshape),
        in_specs=[head, head, head, gain, gain] + [head] * 7, out_specs=(head, head, head, gain, gain),
        scratch_shapes=[pltpu.VMEM((1, t, dh), F32)] * 10, compiler_params=_params("arbitrary"),
    )(q, k, v, qn, kn, *saved, datt)


RWKV_VEC = ("mu_r", "mu_k", "mu_v", "mu_w", "mu_a", "mu_g", "w0", "a0", "k_k", "k_a")
RWKV_MAT = ("w1", "w2", "a1", "a2", "g1", "g2")


def _rwkv_pre_fn(cur, prev, vec, w1t, w2, a1t, a2, g1t, g2):
    c = cur.shape[1] // 4
    mu_r, mu_k, mu_v, mu_w, mu_a, mu_g, w0, a0, k_k, k_a = (vec[j:j + 1] for j in range(10))

    def lerp(j, mu):
        xc, xp = cur[:, j * c:(j + 1) * c], prev[:, j * c:(j + 1) * c]
        return xc + (xp - xc) * mu

    r, k, v = lerp(0, mu_r), lerp(1, mu_k), lerp(2, mu_v)
    cw, ca, cg = lerp(3, mu_w), lerp(3, mu_a), lerp(3, mu_g)
    z = w0 + _mm(jnp.tanh(_mm_nt(cw, w1t)), w2)
    w_log = jnp.minimum(z, 0.0) - jnp.log(1.0 + jnp.exp(-jnp.abs(z))) - 0.5
    lw = -jnp.exp(w_log)
    a = _sigmoid(a0 + _mm(_mm_nt(ca, a1t), a2))
    gate = _mm(_sigmoid(_mm_nt(cg, g1t)), g2)
    kkraw = k * k_k
    kmod = k * (1.0 + (a - 1.0) * k_a)
    return r, lw, kmod, v, kkraw, a, gate


HALO_ROWS = 8


def _rwkv_pre_specs(c, mats, tile_of):
    tm = TOKEN_TILE
    nh = c // HEAD_DIM
    wide = pl.BlockSpec((tm, 4 * c), lambda j: (tile_of(j), 0))
    halo = pl.BlockSpec((HALO_ROWS, 4 * c), lambda j: (jnp.maximum(tile_of(j) * (tm // HALO_ROWS) - 1, 0), 0))
    one = pl.BlockSpec((tm, c), lambda j: (tile_of(j), 0))
    heads = pl.BlockSpec((nh, tm, HEAD_DIM), lambda j: (0, tile_of(j), 0))
    vec = pl.BlockSpec((10, c), lambda j: (0, 0))
    mspecs = [pl.BlockSpec(m.shape, lambda j: (0, 0)) for m in mats]
    return wide, halo, one, heads, vec, mspecs


def _previous_rows(cur, halo, tile):
    first = jnp.where(tile > 0, halo[HALO_ROWS - 1:HALO_ROWS], 0.0)
    rows = lax.broadcasted_iota(jnp.int32, cur.shape, 0)
    return jnp.where(rows == 0, first, pltpu.roll(cur, 1, axis=0))


def _rwkv_pre_fwd(cur, vec, mats):
    t, c4 = cur.shape
    c = c4 // 4
    wide, halo, one, heads, vspec, mspecs = _rwkv_pre_specs(c, mats, lambda j: j)

    def body(cur_ref, halo_ref, vec_ref, *rest):
        mrefs, outs = rest[:6], rest[6:]
        cur_v = cur_ref[...]
        prev = _previous_rows(cur_v, halo_ref[...], pl.program_id(0))
        vals = _rwkv_pre_fn(cur_v, prev, vec_ref[...], *(m[...] for m in mrefs))
        for ref, val in zip(outs[:6], vals[:6]):
            _store_heads(ref, val)
        outs[6][...] = vals[6]

    hshape = jax.ShapeDtypeStruct((c // HEAD_DIM, t, HEAD_DIM), F32)
    return pl.pallas_call(
        body, name="rwkv_pre_fwd", grid=(t // TOKEN_TILE,), out_shape=(hshape,) * 6 + (jax.ShapeDtypeStruct((t, c), F32),),
        in_specs=[wide, halo, vspec] + mspecs, out_specs=(heads,) * 6 + (one,), compiler_params=_params("arbitrary"),
    )(cur, cur, vec, *mats)


def _rwkv_pre_bwd(cur, vec, mats, cts, dgate):
    t, c4 = cur.shape
    c = c4 // 4
    tm = TOKEN_TILE
    nt = t // tm
    wide, halo, one, heads, vspec, mspecs = _rwkv_pre_specs(c, mats, lambda j: nt - 1 - j)

    def body(cur_ref, halo_ref, vec_ref, *rest):
        mrefs, ctrefs, dgate_ref, outs, carry_ref = rest[:6], rest[6:12], rest[12], rest[13:-1], rest[-1]
        j = pl.program_id(0)

        @pl.when(j == 0)
        def _():
            carry_ref[...] = jnp.zeros_like(carry_ref)
            for ref in outs[1:]:
                ref[...] = jnp.zeros_like(ref)

        cur_v = cur_ref[...]
        prev = _previous_rows(cur_v, halo_ref[...], nt - 1 - j)
        _, vjp = jax.vjp(_rwkv_pre_fn, cur_v, prev, vec_ref[...], *(m[...] for m in mrefs))
        grads = vjp(tuple(_load_heads(r) for r in ctrefs) + (dgate_ref[...],))
        dprev = grads[1]
        rows = lax.broadcasted_iota(jnp.int32, dprev.shape, 0)
        outs[0][...] = grads[0] + jnp.where(rows == tm - 1, carry_ref[0:1], pltpu.roll(dprev, tm - 1, axis=0))
        carry_ref[0:1] = dprev[0:1]
        for ref, val in zip(outs[1:], grads[2:]):
            ref[...] += val

    return pl.pallas_call(
        body, name="rwkv_pre_bwd", grid=(nt,),
        out_shape=(jax.ShapeDtypeStruct(cur.shape, F32), jax.ShapeDtypeStruct(vec.shape, F32))
        + tuple(jax.ShapeDtypeStruct(m.shape, F32) for m in mats),
        in_specs=[wide, halo, vspec] + mspecs + [heads] * 6 + [one], out_specs=(wide, vspec) + tuple(mspecs),
        scratch_shapes=[pltpu.VMEM((HALO_ROWS, c4), F32)], compiler_params=_params("arbitrary"),
    )(cur, cur, vec, *mats, *cts, dgate)


def _scan_chunk_fn(h0, r, lw, k, v, kkraw, a, rk, lnw, lnb):
    n = r.shape[1]
    nrm = jnp.sqrt(jnp.sum(kkraw * kkraw, axis=-1, keepdims=True))
    kk = kkraw / jnp.maximum(nrm, 1e-12)
    av, bv = -kk, kk * a
    ti = lax.broadcasted_iota(jnp.int32, (n, n), 0)
    si = lax.broadcasted_iota(jnp.int32, (n, n), 1)
    incl, strict = ti >= si, ti > si
    ones = jnp.broadcast_to(incl.astype(F32)[None], (r.shape[0], n, n))
    cum = _hdot(ones, lw, 2, 1)
    at, rt = av * jnp.exp(cum - lw), r * jnp.exp(cum)
    inv = jnp.exp(-cum)
    bt, kt = bv * inv, k * inv
    gram = _hdot(jnp.concatenate([at, rt], axis=1), jnp.concatenate([bt, kt], axis=1), 2, 2)
    lab = jnp.where(strict, gram[:, :n, :n], 0.0)
    lak = jnp.where(strict, gram[:, :n, n:], 0.0)
    rb = jnp.where(incl, gram[:, n:, :n], 0.0)
    rkm = jnp.where(incl, gram[:, n:, n:], 0.0)
    nv = v.shape[2]
    u = _bmm_nn(jnp.concatenate([at, lak], axis=2), jnp.concatenate([h0, v], axis=1))
    p = lab
    m = 2
    while m < n:
        both = _bmm_nn(p, jnp.concatenate([u, p], axis=2))
        u, p = u + both[:, :, :nv], both[:, :, nv:]
        m *= 2
    u = u + _bmm_nn(p, u)
    y = _bmm_nn(jnp.concatenate([rt, rb, rkm], axis=2), jnp.concatenate([h0, u, v], axis=1))
    last = jnp.exp(jnp.sum(lw, axis=1, keepdims=True))
    h1 = jnp.swapaxes(last, 1, 2) * (h0 + _bmm_tn(jnp.concatenate([bt, kt], axis=1), jnp.concatenate([u, v], axis=1)))
    mean = jnp.mean(y, axis=-1, keepdims=True)
    yc = y - mean
    var = jnp.mean(yc * yc, axis=-1, keepdims=True)
    yn = yc * lax.rsqrt(var + GN_EPS) * lnw + lnb
    bonus = jnp.sum(r * k * rk, axis=-1, keepdims=True) * v
    return yn + bonus, h1


SCAN_GROUP = 2


def _scan_group_fn(h0, r, lw, k, v, kkraw, a, rk, lnw, lnb):
    outs = []
    for j in range(SCAN_GROUP):
        rows = slice(j * SCAN_CHUNK, (j + 1) * SCAN_CHUNK)
        o, h0 = _scan_chunk_fn(h0, r[:, rows], lw[:, rows], k[:, rows], v[:, rows], kkraw[:, rows], a[:, rows], rk, lnw, lnb)
        outs.append(o)
    return jnp.concatenate(outs, axis=1), h0


def _scan_specs(h, t, dh, rev):
    n = SCAN_CHUNK * SCAN_GROUP
    nc = t // n
    pos = (lambda c: (0, nc - 1 - c, 0)) if rev else (lambda c: (0, c, 0))
    st = (lambda c: (nc - 1 - c, 0, 0, 0)) if rev else (lambda c: (c, 0, 0, 0))
    seq = pl.BlockSpec((h, n, dh), pos)
    par = pl.BlockSpec((h, 1, dh), lambda c: (0, 0, 0))
    state = pl.BlockSpec((1, h, dh, dh), st)
    return seq, par, state


def _scan_fwd(seqs, pars):
    h, t, dh = seqs[0].shape
    nc = t // (SCAN_CHUNK * SCAN_GROUP)
    seq, par, state = _scan_specs(h, t, dh, False)

    def body(r, lw, k, v, kkraw, a, rk, lnw, lnb, o_ref, st_ref, h_ref):
        @pl.when(pl.program_id(0) == 0)
        def _():
            h_ref[...] = jnp.zeros_like(h_ref)

        h0 = h_ref[...]
        st_ref[0] = h0
        o, h1 = _scan_group_fn(h0, r[...], lw[...], k[...], v[...], kkraw[...], a[...], rk[...], lnw[...], lnb[...])
        o_ref[...] = o
        h_ref[...] = h1

    return pl.pallas_call(
        body, name="rwkv_scan_fwd", grid=(nc,),
        out_shape=(jax.ShapeDtypeStruct((h, t, dh), F32), jax.ShapeDtypeStruct((nc, h, dh, dh), F32)),
        in_specs=[seq] * 6 + [par] * 3, out_specs=(seq, state),
        scratch_shapes=[pltpu.VMEM((h, dh, dh), F32)], compiler_params=_params("arbitrary"),
    )(*seqs, *pars)


def _scan_bwd(seqs, pars, states, do):
    h, t, dh = seqs[0].shape
    nc = t // (SCAN_CHUNK * SCAN_GROUP)
    seq, par, state = _scan_specs(h, t, dh, True)

    def body(r, lw, k, v, kkraw, a, rk, lnw, lnb, st_ref, do_ref, *rest):
        douts, dpars, dh_ref = rest[:6], rest[6:9], rest[9]
        first = pl.program_id(0) == 0

        @pl.when(first)
        def _():
            dh_ref[...] = jnp.zeros_like(dh_ref)

        _, vjp = jax.vjp(_scan_group_fn, st_ref[0], r[...], lw[...], k[...], v[...], kkraw[...], a[...],
                         rk[...], lnw[...], lnb[...])
        grads = vjp((do_ref[...], dh_ref[...]))
        dh_ref[...] = grads[0]
        for ref, val in zip(douts, grads[1:7]):
            ref[...] = val

        @pl.when(first)
        def _():
            for ref, val in zip(dpars, grads[7:]):
                ref[...] = val

        @pl.when(jnp.logical_not(first))
        def _():
            for ref, val in zip(dpars, grads[7:]):
                ref[...] += val

    sshape = jax.ShapeDtypeStruct((h, t, dh), F32)
    pshape = jax.ShapeDtypeStruct((h, 1, dh), F32)
    return pl.pallas_call(
        body, name="rwkv_scan_bwd", grid=(nc,), out_shape=(sshape,) * 6 + (pshape,) * 3,
        in_specs=[seq] * 6 + [par] * 3 + [state, seq], out_specs=(seq,) * 6 + (par,) * 3,
        scratch_shapes=[pltpu.VMEM((h, dh, dh), F32)], compiler_params=_params("arbitrary"),
    )(*seqs, *pars, states, do)


def _local_step(x, target, w, ex):
    w = dict(w)
    c = w["mu_r"].shape[-1]
    qn, kn = w["q_norm"].reshape(1, 1, HEAD_DIM), w["k_norm"].reshape(1, 1, HEAD_DIM)
    vec = jnp.concatenate([w[n].reshape(1, c) for n in RWKV_VEC], axis=0)
    pars = [w[n].reshape(-1, 1, HEAD_DIM) for n in ("r_k", "ln_x_w", "ln_x_b")]
    no_dep = jnp.zeros(DEP_SHAPE, F32)

    x1, gate1, up1 = _ffn_fwd(x, w["ffn1_norm"], w["ffn1_w_gate"], w["ffn1_w_up"], w["ffn1_w_down"], ex.first_dep, "ffn1_fwd")
    w.update(ex.mix_weights((x1,)))
    mats = [w[n] for n in RWKV_MAT]
    q, k, v, cur = _proj_fwd(x1, w["mix_norm"], w["w_in"], c)
    att, *saved = _att_fwd(q, k, v, qn, kn)
    pre = _rwkv_pre_fwd(cur, vec, mats)
    seqs, gate = pre[:6], pre[6]
    opg, states = _scan_fwd(seqs, pars)
    w.update(ex.out_weights((att, opg)))
    x2 = _mixout_fwd(x1, att, opg, gate, w["w_out"])
    dy, gate2, up2, loss = _ffn_fwd(x2, w["ffn2_norm"], w["ffn2_w_gate"], w["ffn2_w_up"], w["ffn2_w_down"], no_dep, "ffn2_fwd",
                                    target=target)

    g = {}
    dx2, g["ffn2_norm"], g["ffn2_w_gate"], g["ffn2_w_up"], g["ffn2_w_down"] = _ffn_bwd(
        x2, w["ffn2_norm"], w["ffn2_w_gate"], w["ffn2_w_up"], w["ffn2_w_down"], gate2, up2, dy, no_dep, "ffn2_bwd")
    dep = ex.send_ffn2({n: g[n] for n in ("ffn2_w_gate", "ffn2_w_up", "ffn2_w_down")})
    datt, dopg, dgate, g["w_out"] = _mixout_bwd(att, opg, gate, w["w_out"], dx2, dep)
    dscan = _scan_bwd(seqs, pars, states, dopg)
    for n, d in zip(("r_k", "ln_x_w", "ln_x_b"), dscan[6:]):
        g[n] = d
    dcur, dvec, *dmats = _rwkv_pre_bwd(cur, vec, mats, dscan[:6], dgate)
    for n, d in zip(RWKV_MAT, dmats):
        g[n] = d
    g["rwkv_vec"] = dvec
    dq, dk, dv, g["q_norm"], g["k_norm"] = _att_bwd(q, k, v, qn, kn, saved, datt)
    dx1, g["mix_norm"], g["w_in"] = _proj_bwd(x1, w["mix_norm"], w["w_in"], dq, dk, dv, dcur, dx2)
    dep = ex.send_mix({n: g[n] for n in ("w_in", "w_out") + RWKV_MAT}, (dx1,))
    dx, g["ffn1_norm"], g["ffn1_w_gate"], g["ffn1_w_up"], g["ffn1_w_down"] = _ffn_bwd(
        x, w["ffn1_norm"], w["ffn1_w_gate"], w["ffn1_w_up"], w["ffn1_w_down"], gate1, up1, dx1, dep, "ffn1_bwd")
    return loss, dx, g


N_SHARDS = 4


def _place():
    return lax.axis_index("x"), lax.axis_index("y"), lax.axis_index("c")


def _chip_peers(x, y):
    return [(1 - x, y), (x, 1 - y), (1 - x, 1 - y)]


HBM = pl.BlockSpec(memory_space=pltpu.HBM)
SEM = pl.BlockSpec(memory_space=pltpu.SEMAPHORE)
DEP_SHAPE = (8, 128)


class _Views:
    to_sibling = False


class _GatherViews(_Views):
    @staticmethod
    def send(i, srcs, lands, k, at):
        return srcs[i], lands[i].at[at[3]]

    @staticmethod
    def landing(i, srcs, lands, k, at):
        return srcs[i], lands[i].at[2 * at[4] + at[5]]


class _ScatterViews(_Views):
    @staticmethod
    def send(i, srcs, lands, k, at):
        return srcs[i].at[2 * at[4] + at[5]], lands[i].at[k]

    @staticmethod
    def landing(i, srcs, lands, k, at):
        return srcs[i].at[at[3]], lands[i].at[k]


def _half_rows(ref, slot, half):
    rows = ref.shape[1] // 2
    return ref.at[slot, pl.ds(pl.multiple_of(half * rows, BF16_SUBLANES), rows)]


class _HalfGatherViews(_Views):
    @staticmethod
    def send(i, srcs, lands, k, at):
        rows = srcs[i].shape[0] // 2
        return srcs[i].at[pl.ds(pl.multiple_of(at[2] * rows, BF16_SUBLANES), rows)], _half_rows(lands[i], at[3], at[2])

    @staticmethod
    def landing(i, srcs, lands, k, at):
        rows = srcs[i].shape[0] // 2
        return srcs[i].at[pl.ds(pl.multiple_of(at[2] * rows, BF16_SUBLANES), rows)], _half_rows(lands[i], 2 * at[4] + at[5], at[2])


class _ForwardViews(_Views):
    to_sibling = True

    @staticmethod
    def send(i, srcs, lands, k, at):
        mine = _half_rows(lands[i], 2 * at[4] + at[5], at[2])
        return mine, mine

    @staticmethod
    def landing(i, srcs, lands, k, at):
        theirs = _half_rows(lands[i], 2 * at[4] + at[5], 1 - at[2])
        return theirs, theirs


def _push_start(srcs, lands, views, after, name):
    ns, nl = len(srcs), len(lands)

    def body(*refs):
        src_refs, land_refs = refs[:ns], refs[ns:ns + nl]
        send_sems, recv_sems = refs[ns + nl + 1:ns + nl + 3]
        token = refs[2 * (ns + nl) + 3]
        x, y, c = _place()
        for i in range(nl):
            for k, (px, py) in enumerate(_chip_peers(x, y)):
                src, dst = views.send(i, src_refs, land_refs, k, (x, y, c, 2 * x + y, px, py))
                pltpu.make_async_remote_copy(
                    src_ref=src, dst_ref=dst, send_sem=send_sems.at[3 * i + k], recv_sem=recv_sems.at[3 * i + k],
                    device_id=(x, y, 1 - c) if views.to_sibling else (px, py, c), device_id_type=MESH).start()
        token[...] = jnp.zeros_like(token)

    sems = pltpu.SemaphoreType.DMA((3 * nl,))
    both = [pltpu.with_memory_space_constraint(a, pltpu.HBM) for a in (*srcs, *lands)]
    outs = pl.pallas_call(
        body, name=name,
        out_shape=(sems, sems, *[pltpu.HBM(a.shape, a.dtype) for a in both], jax.ShapeDtypeStruct(DEP_SHAPE, F32)),
        in_specs=[HBM] * (ns + nl) + [ANY], out_specs=(SEM, SEM, *[HBM] * (ns + nl), VMEM_FULL),
        input_output_aliases={i: 2 + i for i in range(ns + nl)},
        compiler_params=pltpu.CompilerParams(has_side_effects=pltpu.SideEffectType.DATAFLOW_SIDE_EFFECTING),
    )(*both, after)
    return outs[0], outs[1], outs[2:2 + ns], outs[2 + ns:2 + ns + nl], outs[2 + ns + nl]


def _push_wait(started, views, after, name, with_sources=False):
    send_sems, recv_sems, srcs, lands, _ = started
    ns, nl = len(srcs), len(lands)

    def body(*refs):
        src_refs, land_refs = refs[:ns], refs[ns:ns + nl]
        send_sems, recv_sems = refs[ns + nl:ns + nl + 2]
        x, y, c = _place()
        for i in range(nl):
            for k, (px, py) in enumerate(_chip_peers(x, y)):
                src, dst = views.landing(i, src_refs, land_refs, k, (x, y, c, 2 * x + y, px, py))
                landing = pltpu.make_async_remote_copy(
                    src_ref=src, dst_ref=dst, send_sem=send_sems.at[3 * i + k], recv_sem=recv_sems.at[3 * i + k],
                    device_id=(x, y, 1 - c) if views.to_sibling else (px, py, c), device_id_type=MESH)
                landing.wait_send()
                landing.wait_recv()

    outs = pl.pallas_call(
        body, name=name,
        out_shape=tuple(pltpu.HBM(a.shape, a.dtype) for a in (*srcs, *lands)),
        in_specs=[HBM] * (ns + nl) + [SEM, SEM] + [ANY] * len(after), out_specs=(HBM,) * (ns + nl),
        input_output_aliases={i: i for i in range(ns + nl)},
        compiler_params=pltpu.CompilerParams(has_side_effects=pltpu.SideEffectType.DATAFLOW_SIDE_EFFECTING),
    )(*srcs, *lands, send_sems, recv_sems, *after)
    return outs if with_sources else outs[ns:]


def _empty_lands(shards, slots, own_slot):
    lands = [lax.empty((slots,) + s.shape, s.dtype) for s in shards]
    if own_slot:
        me = 2 * lax.axis_index("x") + lax.axis_index("y")
        lands = [lax.dynamic_update_index_in_dim(z, s, me, 0) for z, s in zip(lands, shards)]
    return lands


def _sibling_swap(arrays, name, other_half=False):
    n = len(arrays)

    def body(*refs):
        ins, outs = refs[:n], refs[n:2 * n]
        send_sems, recv_sems = refs[2 * n:]
        x, y, c = _place()
        copies = []
        for i in range(n):
            src = ins[i]
            if other_half:
                rows = src.shape[1] // 2
                src = src.at[:, pl.ds(pl.multiple_of((1 - c) * rows, BF16_SUBLANES), rows)]
            cp = pltpu.make_async_remote_copy(
                src_ref=src, dst_ref=outs[i], send_sem=send_sems.at[i], recv_sem=recv_sems.at[i],
                device_id=(x, y, 1 - c), device_id_type=MESH)
            cp.start()
            copies.append(cp)
        for cp in copies:
            cp.wait()

    shapes = [(a.shape[0], a.shape[1] // 2, a.shape[2]) if other_half else a.shape for a in arrays]
    return pl.pallas_call(
        body, name=name,
        out_shape=tuple(jax.ShapeDtypeStruct(s, a.dtype) for s, a in zip(shapes, arrays)),
        in_specs=[ANY] * n, out_specs=(ANY,) * n,
        scratch_shapes=[pltpu.SemaphoreType.DMA((n,)), pltpu.SemaphoreType.DMA((n,))],
    )(*arrays)


FOLD_STEPS = 2


def _fold_add(core, parts, theirs, name):
    n = len(parts)
    s, r, cols = parts[0].shape
    tr = r // 2 // FOLD_STEPS

    def body(core_ref, *refs):
        for p_ref, t_ref, o_ref in zip(refs[:n], refs[n:2 * n], refs[2 * n:]):
            o_ref[...] = (p_ref[...].astype(F32) + t_ref[...].astype(F32)).astype(BF16)

    half = pl.BlockSpec((1, tr, cols), lambda j, i, core_ref: (j, i, 0))
    return pl.pallas_call(
        body, name=name, out_shape=tuple(jax.ShapeDtypeStruct((s, r // 2, cols), BF16) for _ in parts),
        grid_spec=pltpu.PrefetchScalarGridSpec(
            num_scalar_prefetch=1, grid=(s, FOLD_STEPS),
            in_specs=[pl.BlockSpec((1, tr, cols), lambda j, i, core_ref: (j, core_ref[0] * FOLD_STEPS + i, 0))] * n + [half] * n,
            out_specs=(half,) * n),
        compiler_params=_params("arbitrary", "arbitrary"),
    )(core, *parts, *theirs)


N_DEV = 8


PACK_COLS = 1024
PACK_ROWS = 24


def _put_row(pack_ref, row, ref):
    if len(ref.shape) == 2:
        pack_ref[row:row + 1, :ref.shape[1]] = ref[...]
    else:
        for h in range(ref.shape[0]):
            pack_ref[row:row + 1, h * HEAD_DIM:(h + 1) * HEAD_DIM] = ref[h]


def _allreduce_small(grads, rows):
    n = len(grads)

    def body(*refs):
        in_ref, out_ref, buf, send_sems, recv_sems = refs[n + 1], refs[n], *refs[n + 2:]
        in_ref[...] = jnp.zeros_like(in_ref)
        for ref, row in zip(refs[:n], rows):
            if len(ref.shape) == 2 and ref.shape[0] > 1:
                in_ref[row:row + ref.shape[0], :ref.shape[1]] = ref[...]
            else:
                _put_row(in_ref, row, ref)
        x, y, c = _place()
        me = 4 * x + 2 * y + c
        buf[me] = in_ref[...]

        def copy(j, slot):
            px, py, pc = x ^ (j >> 2), y ^ ((j >> 1) & 1), c ^ (j & 1)
            return pltpu.make_async_remote_copy(
                src_ref=in_ref, dst_ref=buf.at[slot(px, py, pc)], send_sem=send_sems.at[j], recv_sem=recv_sems.at[j],
                device_id=(px, py, pc), device_id_type=MESH)

        for j in range(1, N_DEV):
            copy(j, lambda px, py, pc: me).start()
        for j in range(1, N_DEV):
            landing = copy(j, lambda px, py, pc: 4 * px + 2 * py + pc)
            landing.wait_send()
            landing.wait_recv()
        acc = buf[0]
        for s in range(1, N_DEV):
            acc = acc + buf[s]
        out_ref[...] = acc

    shape = (PACK_ROWS, PACK_COLS)
    return pl.pallas_call(
        body, name="allreduce_small", out_shape=jax.ShapeDtypeStruct(shape, F32),
        in_specs=[VMEM_FULL] * n, out_specs=VMEM_FULL,
        scratch_shapes=[pltpu.VMEM(shape, F32), pltpu.VMEM((N_DEV,) + shape, F32), pltpu.SemaphoreType.DMA((N_DEV,)),
                        pltpu.SemaphoreType.DMA((N_DEV,))],
    )(*grads)


BF16_SUBLANES = 16


def _reduce_own(me, parts, recvs, dep, steps, name):
    n = len(parts)

    def body(me_ref, *refs):
        for p_ref, rv_ref, o_ref in zip(refs[:n], refs[n:2 * n], refs[2 * n + 1:]):
            acc = p_ref[0].astype(F32)
            for k in range(3):
                acc = acc + rv_ref[k].astype(F32)
            o_ref[...] = acc

    shapes = [(p.shape[1] // steps, p.shape[2]) for p in parts]
    return pl.pallas_call(
        body, name=name, out_shape=tuple(jax.ShapeDtypeStruct(p.shape[1:], F32) for p in parts),
        grid_spec=pltpu.PrefetchScalarGridSpec(
            num_scalar_prefetch=1, grid=(steps,),
            in_specs=[pl.BlockSpec((1, tr, c), lambda i, me_ref: (me_ref[0], i, 0)) for tr, c in shapes]
            + [pl.BlockSpec((3, tr, c), lambda i, me_ref: (0, i, 0)) for tr, c in shapes] + [ANY],
            out_specs=tuple(pl.BlockSpec((tr, c), lambda i, me_ref: (i, 0)) for tr, c in shapes)),
        compiler_params=_params("arbitrary"),
    )(me, *parts, *recvs, dep)


def _adamw(ws, gas, gbs, ms, vs, steps, name):
    n = len(ws)
    c1 = 1.0 - ADAM_B1 ** ADAM_STEP
    c2 = 1.0 - ADAM_B2 ** ADAM_STEP
    operands = [ws, gas, ms, vs] if gbs is None else [ws, gas, gbs, ms, vs]
    k = len(operands)

    def body(*refs):
        ins, outs = refs[:k * n], refs[k * n:]
        for j in range(n):
            w_ref, ga_ref, *gb_ref, m_ref, v_ref = ins[j::n]
            g_out, d_out, m_out, v_out = outs[j::n]
            g = ga_ref[...] + gb_ref[0][...] if gb_ref else ga_ref[...]
            mn = ADAM_B1 * m_ref[...] + (1.0 - ADAM_B1) * g
            vn = ADAM_B2 * v_ref[...] + (1.0 - ADAM_B2) * (g * g)
            g_out[...] = g
            m_out[...] = mn
            v_out[...] = vn
            d_out[...] = -ADAM_LR * ((mn / c1) / (jnp.sqrt(vn / c2) + ADAM_EPS) + ADAM_WD * w_ref[...])

    tiles = [pl.BlockSpec((w.shape[0] // steps, w.shape[1]), lambda i: (i, 0)) for w in ws]
    shapes = [jax.ShapeDtypeStruct(w.shape, F32) for w in ws]
    outs = pl.pallas_call(
        body, name=name, grid=(steps,), out_shape=tuple(shapes * 4), in_specs=tiles * k, out_specs=tuple(tiles * 4),
        compiler_params=_params("arbitrary"),
    )(*[a for group in operands for a in group])
    return [outs[j::n] for j in range(n)]


def _adamw_replicated(gsum, ws, ms, vs):
    n = len(ws)
    c1 = 1.0 - ADAM_B1 ** ADAM_STEP
    c2 = 1.0 - ADAM_B2 ** ADAM_STEP

    def body(g_ref, *refs):
        ins, outs = refs[:3 * n], refs[3 * n:]
        for i in range(n):
            w_ref, m_ref, v_ref = ins[i::n]
            shape = w_ref.shape
            if len(shape) == 2:
                g = g_ref[i:i + 1, :shape[1]]
            else:
                g = jnp.concatenate([g_ref[i:i + 1, h * HEAD_DIM:(h + 1) * HEAD_DIM] for h in range(shape[1])], axis=0)[None]
            mn = ADAM_B1 * m_ref[...] + (1.0 - ADAM_B1) * g
            vn = ADAM_B2 * v_ref[...] + (1.0 - ADAM_B2) * (g * g)
            g_out, d_out, m_out, v_out = outs[i::n]
            g_out[...] = g
            m_out[...] = mn
            v_out[...] = vn
            d_out[...] = -ADAM_LR * ((mn / c1) / (jnp.sqrt(vn / c2) + ADAM_EPS) + ADAM_WD * w_ref[...])

    shapes = [jax.ShapeDtypeStruct(w.shape, F32) for w in ws]
    outs = pl.pallas_call(
        body, name="adamw_replicated", out_shape=tuple(shapes * 4),
        in_specs=[VMEM_FULL] * (1 + 3 * n), out_specs=(VMEM_FULL,) * (4 * n),
    )(gsum, *ws, *ms, *vs)
    return [outs[i::n] for i in range(n)]


COL_SHARDED = ("ffn1_w_gate", "ffn1_w_up", "w_in", "ffn2_w_gate", "ffn2_w_up", "w1", "w2", "a1", "a2", "g1", "g2")
ROW_SHARDED = ("ffn1_w_down", "ffn2_w_down", "w_out")
CHUNKED = ("ffn1_w_gate", "ffn1_w_up", "ffn1_w_down", "w_in", "ffn2_w_gate", "ffn2_w_up", "ffn2_w_down")
WEIGHTS = ("ffn1_norm", "ffn1_w_gate", "ffn1_w_up", "ffn1_w_down", "mix_norm", "w_in", "q_norm", "k_norm",
           "mu_r", "mu_k", "mu_v", "mu_w", "mu_a", "mu_g", "w0", "w1", "w2", "a0", "a1", "a2", "g1", "g2",
           "k_k", "k_a", "r_k", "ln_x_w", "ln_x_b", "w_out", "ffn2_norm", "ffn2_w_gate", "ffn2_w_up", "ffn2_w_down")


TRANSPOSED = ("ffn1_w_gate", "ffn1_w_up", "ffn2_w_gate", "ffn2_w_up", "w1", "a1", "g1")


def _shard_2d(name, a):
    return a[0].T if name in TRANSPOSED else a[0]


def _full_from_blocks(name, blocks):
    if name in CHUNKED:
        return blocks
    if name in ROW_SHARDED:
        return blocks.reshape(-1, blocks.shape[-1])
    return blocks.transpose(1, 0, 2).reshape(blocks.shape[1], -1)


def _blocks_from_full(name, full):
    if name in CHUNKED:
        return full
    if name in ROW_SHARDED:
        return full.reshape(N_SHARDS, -1, full.shape[-1])
    return full.reshape(full.shape[0], N_SHARDS, -1).transpose(1, 0, 2)


FFN1_GROUP = ("ffn1_w_gate", "ffn1_w_up", "ffn1_w_down")
MIX_GROUP = ("w_in",) + RWKV_MAT
OUT_GROUP = ("w_out", "ffn2_w_gate", "ffn2_w_up", "ffn2_w_down")
FFN2_GROUP = OUT_GROUP[1:]
LATE_GROUP = ("w_in", "w_out") + RWKV_MAT


class _Exchange:
    def __init__(self, given):
        self.given = given
        first = self._gather_start(FFN1_GROUP, _HalfGatherViews, jnp.zeros(DEP_SHAPE, F32), "gather_ffn1_start")
        self.mix = self._gather_start(MIX_GROUP, _GatherViews, first[4], "gather_mix_start")
        self.out = self._gather_start(OUT_GROUP, _GatherViews, self.mix[4], "gather_out_start")
        self.first_dep = self.out[4]
        halves = _push_wait(first, _HalfGatherViews, (self.first_dep,), "gather_ffn1_wait")
        passed = _push_start([], halves, _ForwardViews, jnp.zeros(DEP_SHAPE, F32), "gather_ffn1_pass_start")
        self.first_weights = self._full(FFN1_GROUP, _push_wait(passed, _ForwardViews, (passed[4],), "gather_ffn1_pass_wait"))
        self.parts, self.recv = {}, {}

    @staticmethod
    def _full(names, blocks):
        out = {}
        for n, b in zip(names, blocks):
            full = _full_from_blocks(n, b)
            out[n] = full.astype(F32) if n in RWKV_MAT else full
        return out

    def _gather_start(self, names, views, after, name):
        after, raw = lax.optimization_barrier((after, [_shard_2d(n, self.given[n]) for n in names]))
        shards = [a.astype(BF16) for a in raw]
        return _push_start(shards, _empty_lands(shards, N_SHARDS, True), views, after, name)

    def mix_weights(self, after):
        return self._full(MIX_GROUP, _push_wait(self.mix, _GatherViews, after, "gather_mix_wait"))

    def out_weights(self, after):
        return self._full(OUT_GROUP, _push_wait(self.out, _GatherViews, after, "gather_out_wait"))

    def _scatter_start(self, grads, name):
        names = tuple(grads)
        parts = [_blocks_from_full(n, grads[n]) for n in names]
        self.parts.update(zip(names, parts))
        lands = [lax.empty((3,) + p.shape[1:], BF16) for p in parts]
        return _push_start([p.astype(BF16) for p in parts], lands, _ScatterViews, jnp.zeros(DEP_SHAPE, F32), name)

    def _scatter_done(self, started, names, after, name):
        outs = _push_wait(started, _ScatterViews, after, name, with_sources=True)
        for n, sent, got in zip(names, outs[:len(names)], outs[len(names):]):
            self.recv[n] = got
            if self.parts[n].dtype == BF16:
                self.parts[n] = sent

    def send_ffn2(self, grads):
        self.ffn2 = self._scatter_start(grads, "scatter_ffn2_start")
        return self.ffn2[4]

    def send_mix(self, grads, after):
        self._scatter_done(self.ffn2, FFN2_GROUP, after, "scatter_ffn2_wait")
        self.late = self._scatter_start(grads, "scatter_late_start")
        return self.late[4]

    def send_ffn1(self, grads):
        self.ffn1 = self._scatter_start(grads, "scatter_ffn1_start")
        return self.ffn1[4]

    def late_received(self, after):
        self._scatter_done(self.late, LATE_GROUP, after, "scatter_late_wait")

    def ffn1_received(self, after):
        self._scatter_done(self.ffn1, FFN1_GROUP, after, "scatter_ffn1_wait")


def kernel(
        x, ffn1_norm, ffn1_w_gate, ffn1_w_up, ffn1_w_down, mix_norm, w_in, q_norm, k_norm, mu_r, mu_k, mu_v, mu_w,
        mu_a, mu_g, w0, w1, w2, a0, a1, a2, g1, g2, k_k, k_a, r_k, ln_x_w, ln_x_b, w_out, ffn2_norm, ffn2_w_gate,
        ffn2_w_up, ffn2_w_down, loss_target, m_ffn1_norm, m_ffn1_w_gate, m_ffn1_w_up, m_ffn1_w_down, m_mix_norm,
        m_w_in, m_q_norm, m_k_norm, m_mu_r, m_mu_k, m_mu_v, m_mu_w, m_mu_a, m_mu_g, m_w0, m_w1, m_w2, m_a0, m_a1,
        m_a2, m_g1, m_g2, m_k_k, m_k_a, m_r_k, m_ln_x_w, m_ln_x_b, m_w_out, m_ffn2_norm, m_ffn2_w_gate, m_ffn2_w_up,
        m_ffn2_w_down, v_ffn1_norm, v_ffn1_w_gate, v_ffn1_w_up, v_ffn1_w_down, v_mix_norm, v_w_in, v_q_norm, v_k_norm,
        v_mu_r, v_mu_k, v_mu_v, v_mu_w, v_mu_a, v_mu_g, v_w0, v_w1, v_w2, v_a0, v_a1, v_a2, v_g1, v_g2, v_k_k, v_k_a,
        v_r_k, v_ln_x_w, v_ln_x_b, v_w_out, v_ffn2_norm, v_ffn2_w_gate, v_ffn2_w_up, v_ffn2_w_down):
    given = dict(locals())
    sharded = COL_SHARDED + ROW_SHARDED
    sharded = tuple(n for n in WEIGHTS if n in sharded)
    small = tuple(n for n in WEIGHTS if n not in sharded)

    ex = _Exchange(given)
    w = {n: given[n] for n in small}
    w.update(ex.first_weights)
    loss, dx, g = _local_step(x[0], loss_target[0], w, ex)

    core = lax.axis_index("c").astype(jnp.int32).reshape(1)
    late = [g[n] for n in FFN1_GROUP]
    folded = _fold_add(core, late, _sibling_swap(late, "fold_swap_ffn1", other_half=True), "fold_add_ffn1")
    dep = ex.send_ffn1(dict(zip(FFN1_GROUP, folded)))

    me = (2 * lax.axis_index("x") + lax.axis_index("y")).astype(jnp.int32).reshape(1)
    out = {}

    def settle(names, dep, tag):
        done = []
        for kind, sub, r_steps, a_steps in (("large", tuple(n for n in names if n not in RWKV_MAT), 4, 8),
                                            ("small", tuple(n for n in names if n in RWKV_MAT), 1, 1)):
            if not sub:
                continue
            parts = [ex.parts[n].reshape(N_SHARDS, -1, ex.parts[n].shape[-1]) for n in sub]
            recvs = [ex.recv[n].reshape(3, -1, ex.recv[n].shape[-1]) for n in sub]
            mine = _reduce_own(me, parts, recvs, dep, r_steps, f"reduce_{tag}_{kind}")
            theirs = _sibling_swap(mine, f"sibling_swap_{tag}_{kind}")
            res = _adamw([_shard_2d(n, given[n]) for n in sub], mine, theirs, [_shard_2d(n, given["m_" + n]) for n in sub],
                         [_shard_2d(n, given["v_" + n]) for n in sub], a_steps, f"adamw_{tag}_{kind}")
            for n, rs in zip(sub, res):
                out[n] = [(r.T if n in TRANSPOSED else r).reshape(given[n].shape) for r in rs]
                done.append(out[n][1])
        return tuple(done)

    ex.late_received((dep,))
    last = settle(tuple(n for n in sharded if n not in FFN1_GROUP), dep, "rest")

    row = {n: i for i, n in enumerate(small)}
    singles = [n for n in small if n not in RWKV_VEC]
    gsum = _allreduce_small([g[n] for n in singles] + [g["rwkv_vec"], loss],
                            [row[n] for n in singles] + [row[RWKV_VEC[0]], len(small)])
    res = _adamw_replicated(gsum, [given[n] for n in small], [given["m_" + n] for n in small], [given["v_" + n] for n in small])
    for n, rs in zip(small, res):
        out[n] = list(rs)
    total_loss = gsum[len(small), 0]

    ex.ffn1_received((*last, res[0][1]))
    halves = _reduce_own(me, [ex.parts[n] for n in FFN1_GROUP], [ex.recv[n] for n in FFN1_GROUP],
                         jnp.zeros(DEP_SHAPE, F32), FOLD_STEPS, "reduce_ffn1")
    others = _sibling_swap(halves, "sibling_swap_ffn1")
    first = lax.axis_index("c") == 0
    grads = [jnp.concatenate([jnp.where(first, a, b), jnp.where(first, b, a)], axis=0) for a, b in zip(halves, others)]
    res = _adamw([_shard_2d(n, given[n]) for n in FFN1_GROUP], grads, None, [_shard_2d(n, given["m_" + n]) for n in FFN1_GROUP],
                 [_shard_2d(n, given["v_" + n]) for n in FFN1_GROUP], 8, "adamw_ffn1")
    for n, rs in zip(FFN1_GROUP, res):
        out[n] = [(r.T if n in TRANSPOSED else r).reshape(given[n].shape) for r in rs]
    return (total_loss, dx[None], *[out[n][0] for n in WEIGHTS], *[out[n][1] for n in WEIGHTS],
            *[out[n][2] for n in WEIGHTS], *[out[n][3] for n in WEIGHTS])
```

```python
import functools

import jax
import jax.numpy as jnp
from jax import lax
from jax.experimental import pallas as pl
from jax.experimental.pallas import tpu as pltpu

F32 = jnp.float32
BF16 = jnp.bfloat16
MESH = pl.DeviceIdType.MESH

RMS_EPS = 1e-6
GN_EPS = 64e-5
NEG_INF = -1e30
FFN_RESIDUAL = 0.5
HEAD_DIM = 64
ATT_BLOCK = 128
DILATIONS = (1, 4, 16)
SCAN_CHUNK = 64
TOKEN_TILE = 256
FFN_BWD_TILE = 512

ADAM_LR = 0.001
ADAM_B1 = 0.9
ADAM_B2 = 0.999
ADAM_EPS = 1e-08
ADAM_WD = 0.01
ADAM_STEP = 10

VMEM_FULL = pl.BlockSpec(memory_space=pltpu.VMEM)
ANY = pl.BlockSpec(memory_space=pl.ANY)


VMEM_LIMIT = 56 * 1024 * 1024


def _params(*sem):
    return pltpu.CompilerParams(dimension_semantics=sem, vmem_limit_bytes=VMEM_LIMIT)


def _dot(a, b, dims):
    return lax.dot_general(a.astype(BF16), b.astype(BF16), (dims, ((), ())), preferred_element_type=F32)


def _dot_nn(a, b):
    return _dot(a, b, ((1,), (0,)))


def _dot_nt(a, b):
    return _dot(a, b, ((1,), (1,)))


def _dot_tn(a, b):
    return _dot(a, b, ((0,), (0,)))


@jax.custom_vjp
def _mm(a, b):
    return _dot_nn(a, b)


def _mm_fwd(a, b):
    return _dot_nn(a, b), (a, b)


def _mm_bwd(res, g):
    a, b = res
    return _dot_nt(g, b).astype(a.dtype), _dot_tn(a, g).astype(b.dtype)


_mm.defvjp(_mm_fwd, _mm_bwd)


@jax.custom_vjp
def _mm_nt(a, bt):
    return _dot_nt(a, bt)


def _mm_nt_fwd(a, bt):
    return _dot_nt(a, bt), (a, bt)


def _mm_nt_bwd(res, g):
    a, bt = res
    return _dot_nn(g, bt).astype(a.dtype), _dot_tn(g, a).astype(bt.dtype)


_mm_nt.defvjp(_mm_nt_fwd, _mm_nt_bwd)


def _bdot(a, b, ca, cb):
    return lax.dot_general(a.astype(BF16), b.astype(BF16), (((ca,), (cb,)), ((0,), (0,))), preferred_element_type=F32)


@jax.custom_vjp
def _bmm_nt(a, b):
    return _bdot(a, b, 2, 2)


def _bmm_nt_fwd(a, b):
    return _bdot(a, b, 2, 2), (a, b)


def _bmm_nt_bwd(res, g):
    a, b = res
    return _bdot(g, b, 2, 1), _bdot(g, a, 1, 1)


_bmm_nt.defvjp(_bmm_nt_fwd, _bmm_nt_bwd)


@jax.custom_vjp
def _bmm_nn(a, b):
    return _bdot(a, b, 2, 1)


def _bmm_nn_fwd(a, b):
    return _bdot(a, b, 2, 1), (a, b)


def _bmm_nn_bwd(res, g):
    a, b = res
    return _bdot(g, b, 2, 2), _bdot(a, g, 1, 1)


_bmm_nn.defvjp(_bmm_nn_fwd, _bmm_nn_bwd)


@jax.custom_vjp
def _bmm_tn(a, b):
    return _bdot(a, b, 1, 1)


def _bmm_tn_fwd(a, b):
    return _bdot(a, b, 1, 1), (a, b)


def _bmm_tn_bwd(res, g):
    a, b = res
    return _bdot(b, g, 2, 2), _bdot(a, g, 2, 1)


_bmm_tn.defvjp(_bmm_tn_fwd, _bmm_tn_bwd)


def _hdot(a, b, ca, cb):
    return lax.dot_general(a, b, (((ca,), (cb,)), ((0,), (0,))), precision=lax.Precision.HIGH, preferred_element_type=F32)


def _sigmoid(x):
    return 1.0 / (1.0 + jnp.exp(-x))


def _rms(x):
    return lax.rsqrt(jnp.mean(x * x, axis=-1, keepdims=True) + RMS_EPS)


def _ffn_fwd(x, norm, wg, wu, wd, dep, name, target=None):
    t, d = x.shape
    nc, fc, _ = wg.shape
    tm = TOKEN_TILE

    def body(x_ref, n_ref, wg_ref, wu_ref, wd_ref, dep_ref, *rest):
        o_ref, g_ref, u_ref = rest[-3:] if target is None else rest[1:4]
        xv = x_ref[...]
        h = (xv * _rms(xv) * n_ref[...]).astype(BF16)
        acc = jnp.zeros((tm, d), F32)
        for c in range(nc):
            g = _dot_nt(h, wg_ref[c])
            u = _dot_nt(h, wu_ref[c])
            g_ref[c] = g.astype(BF16)
            u_ref[c] = u.astype(BF16)
            a = (g * _sigmoid(g) * u).astype(BF16)
            acc = acc + jnp.dot(a, wd_ref[c], preferred_element_type=F32)
        y = xv + FFN_RESIDUAL * acc
        if target is None:
            o_ref[...] = y
        else:
            t_ref, loss_ref = rest[0], rest[4]
            err = y - t_ref[...]
            o_ref[...] = err * (1.0 / d)
            part = 0.5 * jnp.sum(jnp.mean(err * err, axis=-1, keepdims=True), axis=0, keepdims=True)

            @pl.when(pl.program_id(0) == 0)
            def _():
                loss_ref[...] = jnp.zeros_like(loss_ref)

            loss_ref[...] += jnp.broadcast_to(part, loss_ref.shape)

    tile = pl.BlockSpec((tm, d), lambda i: (i, 0))
    hidden = pl.BlockSpec((nc, tm, fc), lambda i: (0, i, 0))
    hshape = jax.ShapeDtypeStruct((nc, t, fc), BF16)
    with_loss = target is not None
    return pl.pallas_call(
        body, name=name, grid=(t // tm,),
        out_shape=(jax.ShapeDtypeStruct((t, d), F32), hshape, hshape) + ((jax.ShapeDtypeStruct((1, 128), F32),) if with_loss else ()),
        in_specs=[tile, pl.BlockSpec((1, d), lambda i: (0, 0)), VMEM_FULL, VMEM_FULL, VMEM_FULL, ANY] + ([tile] if with_loss else []),
        out_specs=(tile, hidden, hidden) + ((pl.BlockSpec((1, 128), lambda i: (0, 0)),) if with_loss else ()),
        compiler_params=_params("arbitrary"),
    )(x, norm, wg, wu, wd, dep, *((target,) if with_loss else ()))


def _rmsnorm_bwd(xv, gain, dh):
    rs = _rms(xv)
    xn = xv * rs
    dxn = dh * gain
    dx = rs * (dxn - xn * jnp.mean(dxn * xn, axis=-1, keepdims=True))
    return dx, jnp.sum(dh * xn, axis=0, keepdims=True)


def _ffn_bwd(x, norm, wg, wu, wd, gate, up, dy, dep, name):
    t, d = x.shape
    nc, fc, _ = wg.shape
    tm = FFN_BWD_TILE
    nt = t // tm

    def body(x_ref, n_ref, wg_ref, wu_ref, wd_ref, g_ref, u_ref, dy_ref, dep_ref, dx_ref, dn_ref, dwg_ref, dwu_ref,
             dwd_ref, dh_ref, ag_ref, au_ref, ad_ref):
        c, i = pl.program_id(0), pl.program_id(1)
        rows = pl.ds(pl.multiple_of(i * tm, tm), tm)
        xv = x_ref[...]
        gain = n_ref[...]
        h = (xv * _rms(xv) * gain).astype(BF16)
        dy = dy_ref[...]
        dyb = (FFN_RESIDUAL * dy).astype(BF16)
        g = g_ref[0].astype(F32)
        u = u_ref[0].astype(F32)
        sg = _sigmoid(g)
        s = g * sg
        a = (s * u).astype(BF16)
        da = _dot_nt(dyb, wd_ref[0])
        dub = (da * s).astype(BF16)
        dgb = (da * u * (sg * (1.0 + g * (1.0 - sg)))).astype(BF16)
        dwd_c = _dot_tn(a, dyb)
        dwg_c = _dot_tn(dgb, h)
        dwu_c = _dot_tn(dub, h)
        dh_c = _dot_nn(dgb, wg_ref[0]) + _dot_nn(dub, wu_ref[0])

        @pl.when(i == 0)
        def _():
            ad_ref[...] = dwd_c
            ag_ref[...] = dwg_c
            au_ref[...] = dwu_c

        @pl.when(i > 0)
        def _():
            ad_ref[...] += dwd_c
            ag_ref[...] += dwg_c
            au_ref[...] += dwu_c

        @pl.when(i == nt - 1)
        def _():
            dwd_ref[0] = ad_ref[...].astype(BF16)
            dwg_ref[0] = ag_ref[...].astype(BF16)
            dwu_ref[0] = au_ref[...].astype(BF16)

        @pl.when(c == 0)
        def _():
            dh_ref[rows, :] = dh_c

        @pl.when(c > 0)
        def _():
            dh_ref[rows, :] += dh_c

        @pl.when(c == nc - 1)
        def _():
            dx, dn = _rmsnorm_bwd(xv, gain, dh_ref[rows, :])
            dx_ref[...] = dx + dy

            @pl.when(i == 0)
            def _():
                dn_ref[...] = dn

            @pl.when(i > 0)
            def _():
                dn_ref[...] += dn

    tile = pl.BlockSpec((tm, d), lambda c, i: (i, 0))
    row = pl.BlockSpec((1, d), lambda c, i: (0, 0))
    wrow = pl.BlockSpec((1, fc, d), lambda c, i: (c, 0, 0), pipeline_mode=pl.Buffered(1))
    hidden = pl.BlockSpec((1, tm, fc), lambda c, i: (c, i, 0))
    last = pl.BlockSpec((tm, d), lambda c, i: (jnp.where(c == nc - 1, i, 0), 0))
    return pl.pallas_call(
        body, name=name, grid=(nc, nt),
        out_shape=(jax.ShapeDtypeStruct((t, d), F32), jax.ShapeDtypeStruct((1, d), F32),
                   jax.ShapeDtypeStruct(wg.shape, BF16), jax.ShapeDtypeStruct(wu.shape, BF16),
                   jax.ShapeDtypeStruct(wd.shape, BF16)),
        in_specs=[tile, row, wrow, wrow, wrow, hidden, hidden, tile, ANY],
        out_specs=(last, row, wrow, wrow, wrow),
        scratch_shapes=[pltpu.VMEM((t, d), F32)] + [pltpu.VMEM((fc, d), F32)] * 3,
        compiler_params=_params("arbitrary", "arbitrary"),
    )(x, norm, wg, wu, wd, gate, up, dy, dep)


def _store_heads(ref, v):
    for h in range(ref.shape[0]):
        ref[h] = v[:, h * HEAD_DIM:(h + 1) * HEAD_DIM]


def _load_heads(ref):
    return jnp.concatenate([ref[h] for h in range(ref.shape[0])], axis=-1)


N_HEAD_GROUPS = 3


def _proj_fwd(x, norm, w, c):
    t, d = x.shape
    nc, _, ncol = w.shape
    nh = c // HEAD_DIM
    tm = TOKEN_TILE
    wide = nc * ncol - N_HEAD_GROUPS * c

    def body(x_ref, n_ref, w_ref, q_ref, k_ref, v_ref, cur_ref):
        xv = x_ref[...]
        h = (xv * _rms(xv) * n_ref[...]).astype(BF16)
        full = jnp.concatenate([jnp.dot(h, w_ref[s], preferred_element_type=F32) for s in range(nc)], axis=1)
        for m, ref in enumerate((q_ref, k_ref, v_ref)):
            _store_heads(ref, full[:, m * c:(m + 1) * c])
        cur_ref[...] = full[:, N_HEAD_GROUPS * c:]

    heads = pl.BlockSpec((nh, tm, HEAD_DIM), lambda i: (0, i, 0))
    hshape = jax.ShapeDtypeStruct((nh, t, HEAD_DIM), F32)
    return pl.pallas_call(
        body, name="proj_fwd", grid=(t // tm,),
        out_shape=(hshape, hshape, hshape, jax.ShapeDtypeStruct((t, wide), F32)),
        in_specs=[pl.BlockSpec((tm, d), lambda i: (i, 0)), pl.BlockSpec((1, d), lambda i: (0, 0)), VMEM_FULL],
        out_specs=(heads, heads, heads, pl.BlockSpec((tm, wide), lambda i: (i, 0))),
        compiler_params=_params("arbitrary"),
    )(x, norm, w)


def _proj_bwd(x, norm, w, dq, dk, dv, dcur, dres):
    t, d = x.shape
    nc, _, ncol = w.shape
    nh = dq.shape[0]
    tm = TOKEN_TILE
    nt = t // tm
    wide = dcur.shape[1]

    def body(x_ref, n_ref, w_ref, dq_ref, dk_ref, dv_ref, dcur_ref, dres_ref, dx_ref, dn_ref, dw_ref, acc_ref):
        i = pl.program_id(0)

        @pl.when(i == 0)
        def _():
            acc_ref[...] = jnp.zeros_like(acc_ref)
            dn_ref[...] = jnp.zeros_like(dn_ref)

        xv = x_ref[...]
        gain = n_ref[...]
        h = (xv * _rms(xv) * gain).astype(BF16)
        dp = jnp.concatenate([_load_heads(dq_ref), _load_heads(dk_ref), _load_heads(dv_ref), dcur_ref[...]], axis=1).astype(BF16)
        dh = jnp.zeros((tm, d), F32)
        for s in range(nc):
            dps = dp[:, s * ncol:(s + 1) * ncol]
            acc_ref[s] += _dot_tn(h, dps)
            dh = dh + _dot_nt(dps, w_ref[s])
        dx, dn = _rmsnorm_bwd(xv, gain, dh)
        dx_ref[...] = dx + dres_ref[...]
        dn_ref[...] += dn

        @pl.when(i == nt - 1)
        def _():
            dw_ref[...] = acc_ref[...].astype(BF16)

    tile = pl.BlockSpec((tm, d), lambda i: (i, 0))
    row = pl.BlockSpec((1, d), lambda i: (0, 0))
    heads = pl.BlockSpec((nh, tm, HEAD_DIM), lambda i: (0, i, 0))
    return pl.pallas_call(
        body, name="proj_bwd", grid=(nt,),
        out_shape=(jax.ShapeDtypeStruct((t, d), F32), jax.ShapeDtypeStruct((1, d), F32),
                   jax.ShapeDtypeStruct(w.shape, BF16)),
        in_specs=[tile, row, VMEM_FULL, heads, heads, heads, pl.BlockSpec((tm, wide), lambda i: (i, 0)), tile],
        out_specs=(tile, row, VMEM_FULL),
        scratch_shapes=[pltpu.VMEM(w.shape, F32)], compiler_params=_params("arbitrary"),
    )(x, norm, w, dq, dk, dv, dcur, dres)


def _mixout_fwd(x, att, opg, gate, w):
    t, d = x.shape
    nh = att.shape[0]
    half = gate.shape[1]
    tm = TOKEN_TILE

    def body(x_ref, att_ref, opg_ref, g_ref, w_ref, o_ref):
        mix = jnp.concatenate([_load_heads(att_ref), _load_heads(opg_ref) * g_ref[...]], axis=-1).astype(BF16)
        o_ref[...] = x_ref[...] + jnp.dot(mix, w_ref[...], preferred_element_type=F32)

    tile = pl.BlockSpec((tm, d), lambda i: (i, 0))
    htile = pl.BlockSpec((tm, half), lambda i: (i, 0))
    heads = pl.BlockSpec((nh, tm, HEAD_DIM), lambda i: (0, i, 0))
    return pl.pallas_call(
        body, name="mixout_fwd", grid=(t // tm,), out_shape=jax.ShapeDtypeStruct((t, d), F32),
        in_specs=[tile, heads, heads, htile, VMEM_FULL], out_specs=tile, compiler_params=_params("arbitrary"),
    )(x, att, opg, gate, w)


def _mixout_bwd(att, opg, gate, w, dy, dep):
    nh, t, _ = att.shape
    half = gate.shape[1]
    d = dy.shape[1]
    tm = TOKEN_TILE

    def body(att_ref, opg_ref, g_ref, w_ref, dy_ref, dep_ref, datt_ref, dopg_ref, dg_ref, dw_ref):
        i = pl.program_id(0)
        opg_v, g_v = _load_heads(opg_ref), g_ref[...]
        mix = jnp.concatenate([_load_heads(att_ref), opg_v * g_v], axis=-1).astype(BF16)
        dyb = dy_ref[...].astype(BF16)
        dmix = _dot_nt(dyb, w_ref[...])
        dw = _dot_tn(mix, dyb)
        _store_heads(datt_ref, dmix[:, :half])
        drw = dmix[:, half:]
        _store_heads(dopg_ref, drw * g_v)
        dg_ref[...] = drw * opg_v

        @pl.when(i == 0)
        def _():
            dw_ref[...] = dw

        @pl.when(i > 0)
        def _():
            dw_ref[...] += dw

    tile = pl.BlockSpec((tm, d), lambda i: (i, 0))
    htile = pl.BlockSpec((tm, half), lambda i: (i, 0))
    heads = pl.BlockSpec((nh, tm, HEAD_DIM), lambda i: (0, i, 0))
    hshape = jax.ShapeDtypeStruct((nh, t, HEAD_DIM), F32)
    return pl.pallas_call(
        body, name="mixout_bwd", grid=(t // tm,),
        out_shape=(hshape, hshape, jax.ShapeDtypeStruct((t, half), F32), jax.ShapeDtypeStruct(w.shape, F32)),
        in_specs=[heads, heads, htile, VMEM_FULL, tile, ANY],
        out_specs=(heads, heads, htile, pl.BlockSpec(w.shape, lambda i: (0, 0))),
        compiler_params=_params("arbitrary"),
    )(att, opg, gate, w, dy, dep)


def _head_norm(x, gain):
    return x * _rms(x) * gain


def _att_pattern(qh, kh, v, nb):
    g, blk, _ = qh.shape
    scale = HEAD_DIM ** -0.5
    qi = lax.broadcasted_iota(jnp.int32, (blk, blk), 0)
    kj = lax.broadcasted_iota(jnp.int32, (blk, blk), 1)
    sc = jnp.where(kj <= qi, _bmm_nt(qh, kh) * scale, NEG_INF)
    top = jnp.max(sc, axis=-1, keepdims=True)
    if nb > 1:
        khp = jnp.concatenate([kh[:1], kh[:-1]], axis=0)
        vp = jnp.concatenate([v[:1], v[:-1]], axis=0)
        has_prev = lax.broadcasted_iota(jnp.int32, (g, 1, 1), 0) % nb != 0
        sp = jnp.where((kj >= qi) & has_prev, _bmm_nt(qh, khp) * scale, NEG_INF)
        top = jnp.maximum(top, jnp.max(sp, axis=-1, keepdims=True))
    m = lax.stop_gradient(top)
    pc = jnp.exp(sc - m)
    den = jnp.sum(pc, axis=-1, keepdims=True)
    acc = _bmm_nn(pc, v)
    if nb > 1:
        pp = jnp.exp(sp - m)
        den = den + jnp.sum(pp, axis=-1, keepdims=True)
        acc = acc + _bmm_nn(pp, vp)
    o = acc / den
    return o, jnp.broadcast_to(m + jnp.log(den), o.shape)


def _pattern_rows(t, dil):
    length = t // dil
    return [pl.ds(r, length, stride=dil) if dil > 1 else pl.ds(0, length) for r in range(dil)], length // ATT_BLOCK


def _take(ref, rows, nb):
    return jnp.concatenate([ref[0, r, :].reshape(nb, ATT_BLOCK, HEAD_DIM) for r in rows], axis=0)


def _put(ref, rows, nb, val):
    for j, r in enumerate(rows):
        ref[0, r, :] = val[j * nb:(j + 1) * nb].reshape(nb * ATT_BLOCK, HEAD_DIM)


def _put_add(ref, rows, nb, val):
    for j, r in enumerate(rows):
        ref[0, r, :] += val[j * nb:(j + 1) * nb].reshape(nb * ATT_BLOCK, HEAD_DIM)


def _merge_fn(o1, o2, o3, l1, l2, l3):
    m = lax.stop_gradient(jnp.maximum(jnp.maximum(l1, l2), l3))
    e1, e2, e3 = jnp.exp(l1 - m), jnp.exp(l2 - m), jnp.exp(l3 - m)
    return (e1 * o1 + e2 * o2 + e3 * o3) / (e1 + e2 + e3)


def _token_rows(j):
    return pl.ds(pl.multiple_of(j * ATT_BLOCK, ATT_BLOCK), ATT_BLOCK)


def _norm_rows(t, q_ref, k_ref, gq, gk, qh_ref, kh_ref):
    def step(j, carry):
        rows = _token_rows(j)
        qh_ref[0, rows, :] = _head_norm(q_ref[0, rows, :], gq[0])
        kh_ref[0, rows, :] = _head_norm(k_ref[0, rows, :], gk[0])
        return carry

    lax.fori_loop(0, t // ATT_BLOCK, step, 0)


def _att_head_specs(t):
    head = pl.BlockSpec((1, t, HEAD_DIM), lambda h: (h, 0, 0))
    gain = pl.BlockSpec((1, 1, HEAD_DIM), lambda h: (0, 0, 0))
    return head, gain


def _att_fwd(q, k, v, qn, kn):
    nh, t, dh = q.shape
    head, gain = _att_head_specs(t)

    def body(q_ref, k_ref, v_ref, qn_ref, kn_ref, att_ref, o1, o2, o3, l1, l2, l3, qh_ref, kh_ref):
        saved = (o1, o2, o3, l1, l2, l3)
        _norm_rows(t, q_ref, k_ref, qn_ref[...], kn_ref[...], qh_ref, kh_ref)
        for p, dil in enumerate(DILATIONS):
            rows, nb = _pattern_rows(t, dil)
            o, lse = _att_pattern(_take(qh_ref, rows, nb), _take(kh_ref, rows, nb), _take(v_ref, rows, nb), nb)
            _put(saved[p], rows, nb, o)
            _put(saved[3 + p], rows, nb, lse)

        def merge(j, carry):
            rows = _token_rows(j)
            att_ref[0, rows, :] = _merge_fn(*[r[0, rows, :] for r in saved])
            return carry

        lax.fori_loop(0, t // ATT_BLOCK, merge, 0)

    return pl.pallas_call(
        body, name="att_fwd", grid=(nh,), out_shape=(jax.ShapeDtypeStruct(q.shape, F32),) * 7,
        in_specs=[head, head, head, gain, gain], out_specs=(head,) * 7,
        scratch_shapes=[pltpu.VMEM((1, t, dh), F32)] * 2, compiler_params=_params("arbitrary"),
    )(q, k, v, qn, kn)


def _att_bwd(q, k, v, qn, kn, saved, datt):
    nh, t, dh = q.shape
    head, gain = _att_head_specs(t)

    def body(q_ref, k_ref, v_ref, qn_ref, kn_ref, o1, o2, o3, l1, l2, l3, datt_ref,
             dq_ref, dk_ref, dv_ref, dqn_ref, dkn_ref, qh_ref, kh_ref, dqh_ref, dkh_ref, *ct_refs):
        for ref in (dqh_ref, dkh_ref, dv_ref):
            ref[...] = jnp.zeros_like(ref)

        @pl.when(pl.program_id(0) == 0)
        def _():
            dqn_ref[...] = jnp.zeros_like(dqn_ref)
            dkn_ref[...] = jnp.zeros_like(dkn_ref)

        gq, gk = qn_ref[...], kn_ref[...]
        _norm_rows(t, q_ref, k_ref, gq, gk, qh_ref, kh_ref)

        def merge_cotangents(j, carry):
            rows = _token_rows(j)
            _, merge_vjp = jax.vjp(_merge_fn, *[r[0, rows, :] for r in (o1, o2, o3, l1, l2, l3)])
            for ref, val in zip(ct_refs, merge_vjp(datt_ref[0, rows, :])):
                ref[0, rows, :] = val
            return carry

        lax.fori_loop(0, t // ATT_BLOCK, merge_cotangents, 0)

        for p, dil in enumerate(DILATIONS):
            rows, nb = _pattern_rows(t, dil)
            _, pattern_vjp = jax.vjp(functools.partial(_att_pattern, nb=nb), _take(qh_ref, rows, nb), _take(kh_ref, rows, nb),
                                     _take(v_ref, rows, nb))
            dqh, dkh, dv = pattern_vjp((_take(ct_refs[p], rows, nb), _take(ct_refs[3 + p], rows, nb)))
            _put_add(dqh_ref, rows, nb, dqh)
            _put_add(dkh_ref, rows, nb, dkh)
            _put_add(dv_ref, rows, nb, dv)

        def norm_cotangents(j, carry):
            rows = _token_rows(j)
            out = []
            for x_ref, gain, dh_ref, dx_ref, acc in ((q_ref, gq, dqh_ref, dq_ref, carry[0]), (k_ref, gk, dkh_ref, dk_ref, carry[1])):
                _, norm_vjp = jax.vjp(_head_norm, x_ref[0, rows, :], gain[0])
                dx, dgain = norm_vjp(dh_ref[0, rows, :])
                dx_ref[0, rows, :] = dx
                out.append(acc + dgain)
            return tuple(out)

        zero = jnp.zeros((1, dh), F32)
        dgq, dgk = lax.fori_loop(0, t // ATT_BLOCK, norm_cotangents, (zero, zero))
        dqn_ref[0] += dgq
        dkn_ref[0] += dgk

    hshape = jax.ShapeDtypeStruct(q.shape, F32)
    gshape = jax.ShapeDtypeStruct((1, 1, dh), F32)
    return pl.pallas_call(
        body, name="att_bwd", grid=(nh,), out_shape=(hshape, hshape, hshape, gshape, gshape),
        in_specs=[head, head, head, gain, gain] + [head] * 7, out_specs=(head, head, head, gain, gain),
        scratch_shapes=[pltpu.VMEM((1, t, dh), F32)] * 10, compiler_params=_params("arbitrary"),
    )(q, k, v, qn, kn, *saved, datt)


RWKV_VEC = ("mu_r", "mu_k", "mu_v", "mu_w", "mu_a", "mu_g", "w0", "a0", "k_k", "k_a")
RWKV_MAT = ("w1", "w2", "a1", "a2", "g1", "g2")


def _rwkv_pre_fn(cur, prev, vec, w1t, w2, a1t, a2, g1t, g2):
    c = cur.shape[1] // 4
    mu_r, mu_k, mu_v, mu_w, mu_a, mu_g, w0, a0, k_k, k_a = (vec[j:j + 1] for j in range(10))

    def lerp(j, mu):
        xc, xp = cur[:, j * c:(j + 1) * c], prev[:, j * c:(j + 1) * c]
        return xc + (xp - xc) * mu

    r, k, v = lerp(0, mu_r), lerp(1, mu_k), lerp(2, mu_v)
    cw, ca, cg = lerp(3, mu_w), lerp(3, mu_a), lerp(3, mu_g)
    z = w0 + _mm(jnp.tanh(_mm_nt(cw, w1t)), w2)
    w_log = jnp.minimum(z, 0.0) - jnp.log(1.0 + jnp.exp(-jnp.abs(z))) - 0.5
    lw = -jnp.exp(w_log)
    a = _sigmoid(a0 + _mm(_mm_nt(ca, a1t), a2))
    gate = _mm(_sigmoid(_mm_nt(cg, g1t)), g2)
    kkraw = k * k_k
    kmod = k * (1.0 + (a - 1.0) * k_a)
    return r, lw, kmod, v, kkraw, a, gate


HALO_ROWS = 8


def _rwkv_pre_specs(c, mats, tile_of):
    tm = TOKEN_TILE
    nh = c // HEAD_DIM
    wide = pl.BlockSpec((tm, 4 * c), lambda j: (tile_of(j), 0))
    halo = pl.BlockSpec((HALO_ROWS, 4 * c), lambda j: (jnp.maximum(tile_of(j) * (tm // HALO_ROWS) - 1, 0), 0))
    one = pl.BlockSpec((tm, c), lambda j: (tile_of(j), 0))
    heads = pl.BlockSpec((nh, tm, HEAD_DIM), lambda j: (0, tile_of(j), 0))
    vec = pl.BlockSpec((10, c), lambda j: (0, 0))
    mspecs = [pl.BlockSpec(m.shape, lambda j: (0, 0)) for m in mats]
    return wide, halo, one, heads, vec, mspecs


def _previous_rows(cur, halo, tile):
    first = jnp.where(tile > 0, halo[HALO_ROWS - 1:HALO_ROWS], 0.0)
    rows = lax.broadcasted_iota(jnp.int32, cur.shape, 0)
    return jnp.where(rows == 0, first, pltpu.roll(cur, 1, axis=0))


def _rwkv_pre_fwd(cur, vec, mats):
    t, c4 = cur.shape
    c = c4 // 4
    wide, halo, one, heads, vspec, mspecs = _rwkv_pre_specs(c, mats, lambda j: j)

    def body(cur_ref, halo_ref, vec_ref, *rest):
        mrefs, outs = rest[:6], rest[6:]
        cur_v = cur_ref[...]
        prev = _previous_rows(cur_v, halo_ref[...], pl.program_id(0))
        vals = _rwkv_pre_fn(cur_v, prev, vec_ref[...], *(m[...] for m in mrefs))
        for ref, val in zip(outs[:6], vals[:6]):
            _store_heads(ref, val)
        outs[6][...] = vals[6]

    hshape = jax.ShapeDtypeStruct((c // HEAD_DIM, t, HEAD_DIM), F32)
    return pl.pallas_call(
        body, name="rwkv_pre_fwd", grid=(t // TOKEN_TILE,), out_shape=(hshape,) * 6 + (jax.ShapeDtypeStruct((t, c), F32),),
        in_specs=[wide, halo, vspec] + mspecs, out_specs=(heads,) * 6 + (one,), compiler_params=_params("arbitrary"),
    )(cur, cur, vec, *mats)


def _rwkv_pre_bwd(cur, vec, mats, cts, dgate):
    t, c4 = cur.shape
    c = c4 // 4
    tm = TOKEN_TILE
    nt = t // tm
    wide, halo, one, heads, vspec, mspecs = _rwkv_pre_specs(c, mats, lambda j: nt - 1 - j)

    def body(cur_ref, halo_ref, vec_ref, *rest):
        mrefs, ctrefs, dgate_ref, outs, carry_ref = rest[:6], rest[6:12], rest[12], rest[13:-1], rest[-1]
        j = pl.program_id(0)

        @pl.when(j == 0)
        def _():
            carry_ref[...] = jnp.zeros_like(carry_ref)
            for ref in outs[1:]:
                ref[...] = jnp.zeros_like(ref)

        cur_v = cur_ref[...]
        prev = _previous_rows(cur_v, halo_ref[...], nt - 1 - j)
        _, vjp = jax.vjp(_rwkv_pre_fn, cur_v, prev, vec_ref[...], *(m[...] for m in mrefs))
        grads = vjp(tuple(_load_heads(r) for r in ctrefs) + (dgate_ref[...],))
        dprev = grads[1]
        rows = lax.broadcasted_iota(jnp.int32, dprev.shape, 0)
        outs[0][...] = grads[0] + jnp.where(rows == tm - 1, carry_ref[0:1], pltpu.roll(dprev, tm - 1, axis=0))
        carry_ref[0:1] = dprev[0:1]
        for ref, val in zip(outs[1:], grads[2:]):
            ref[...] += val

    return pl.pallas_call(
        body, name="rwkv_pre_bwd", grid=(nt,),
        out_shape=(jax.ShapeDtypeStruct(cur.shape, F32), jax.ShapeDtypeStruct(vec.shape, F32))
        + tuple(jax.ShapeDtypeStruct(m.shape, F32) for m in mats),
        in_specs=[wide, halo, vspec] + mspecs + [heads] * 6 + [one], out_specs=(wide, vspec) + tuple(mspecs),
        scratch_shapes=[pltpu.VMEM((HALO_ROWS, c4), F32)], compiler_params=_params("arbitrary"),
    )(cur, cur, vec, *mats, *cts, dgate)


def _scan_chunk_fn(h0, r, lw, k, v, kkraw, a, rk, lnw, lnb):
    n = r.shape[1]
    nrm = jnp.sqrt(jnp.sum(kkraw * kkraw, axis=-1, keepdims=True))
    kk = kkraw / jnp.maximum(nrm, 1e-12)
    av, bv = -kk, kk * a
    ti = lax.broadcasted_iota(jnp.int32, (n, n), 0)
    si = lax.broadcasted_iota(jnp.int32, (n, n), 1)
    incl, strict = ti >= si, ti > si
    ones = jnp.broadcast_to(incl.astype(F32)[None], (r.shape[0], n, n))
    cum = _hdot(ones, lw, 2, 1)
    at, rt = av * jnp.exp(cum - lw), r * jnp.exp(cum)
    inv = jnp.exp(-cum)
    bt, kt = bv * inv, k * inv
    gram = _hdot(jnp.concatenate([at, rt], axis=1), jnp.concatenate([bt, kt], axis=1), 2, 2)
    lab = jnp.where(strict, gram[:, :n, :n], 0.0)
    lak = jnp.where(strict, gram[:, :n, n:], 0.0)
    rb = jnp.where(incl, gram[:, n:, :n], 0.0)
    rkm = jnp.where(incl, gram[:, n:, n:], 0.0)
    nv = v.shape[2]
    u = _bmm_nn(jnp.concatenate([at, lak], axis=2), jnp.concatenate([h0, v], axis=1))
    p = lab
    m = 2
    while m < n:
        both = _bmm_nn(p, jnp.concatenate([u, p], axis=2))
        u, p = u + both[:, :, :nv], both[:, :, nv:]
        m *= 2
    u = u + _bmm_nn(p, u)
    y = _bmm_nn(jnp.concatenate([rt, rb, rkm], axis=2), jnp.concatenate([h0, u, v], axis=1))
    last = jnp.exp(jnp.sum(lw, axis=1, keepdims=True))
    h1 = jnp.swapaxes(last, 1, 2) * (h0 + _bmm_tn(jnp.concatenate([bt, kt], axis=1), jnp.concatenate([u, v], axis=1)))
    mean = jnp.mean(y, axis=-1, keepdims=True)
    yc = y - mean
    var = jnp.mean(yc * yc, axis=-1, keepdims=True)
    yn = yc * lax.rsqrt(var + GN_EPS) * lnw + lnb
    bonus = jnp.sum(r * k * rk, axis=-1, keepdims=True) * v
    return yn + bonus, h1


SCAN_GROUP = 2


def _scan_group_fn(h0, r, lw, k, v, kkraw, a, rk, lnw, lnb):
    outs = []
    for j in range(SCAN_GROUP):
        rows = slice(j * SCAN_CHUNK, (j + 1) * SCAN_CHUNK)
        o, h0 = _scan_chunk_fn(h0, r[:, rows], lw[:, rows], k[:, rows], v[:, rows], kkraw[:, rows], a[:, rows], rk, lnw, lnb)
        outs.append(o)
    return jnp.concatenate(outs, axis=1), h0


def _scan_specs(h, t, dh, rev):
    n = SCAN_CHUNK * SCAN_GROUP
    nc = t // n
    pos = (lambda c: (0, nc - 1 - c, 0)) if rev else (lambda c: (0, c, 0))
    st = (lambda c: (nc - 1 - c, 0, 0, 0)) if rev else (lambda c: (c, 0, 0, 0))
    seq = pl.BlockSpec((h, n, dh), pos)
    par = pl.BlockSpec((h, 1, dh), lambda c: (0, 0, 0))
    state = pl.BlockSpec((1, h, dh, dh), st)
    return seq, par, state


def _scan_fwd(seqs, pars):
    h, t, dh = seqs[0].shape
    nc = t // (SCAN_CHUNK * SCAN_GROUP)
    seq, par, state = _scan_specs(h, t, dh, False)

    def body(r, lw, k, v, kkraw, a, rk, lnw, lnb, o_ref, st_ref, h_ref):
        @pl.when(pl.program_id(0) == 0)
        def _():
            h_ref[...] = jnp.zeros_like(h_ref)

        h0 = h_ref[...]
        st_ref[0] = h0
        o, h1 = _scan_group_fn(h0, r[...], lw[...], k[...], v[...], kkraw[...], a[...], rk[...], lnw[...], lnb[...])
        o_ref[...] = o
        h_ref[...] = h1

    return pl.pallas_call(
        body, name="rwkv_scan_fwd", grid=(nc,),
        out_shape=(jax.ShapeDtypeStruct((h, t, dh), F32), jax.ShapeDtypeStruct((nc, h, dh, dh), F32)),
        in_specs=[seq] * 6 + [par] * 3, out_specs=(seq, state),
        scratch_shapes=[pltpu.VMEM((h, dh, dh), F32)], compiler_params=_params("arbitrary"),
    )(*seqs, *pars)


def _scan_bwd(seqs, pars, states, do):
    h, t, dh = seqs[0].shape
    nc = t // (SCAN_CHUNK * SCAN_GROUP)
    seq, par, state = _scan_specs(h, t, dh, True)

    def body(r, lw, k, v, kkraw, a, rk, lnw, lnb, st_ref, do_ref, *rest):
        douts, dpars, dh_ref = rest[:6], rest[6:9], rest[9]
        first = pl.program_id(0) == 0

        @pl.when(first)
        def _():
            dh_ref[...] = jnp.zeros_like(dh_ref)

        _, vjp = jax.vjp(_scan_group_fn, st_ref[0], r[...], lw[...], k[...], v[...], kkraw[...], a[...],
                         rk[...], lnw[...], lnb[...])
        grads = vjp((do_ref[...], dh_ref[...]))
        dh_ref[...] = grads[0]
        for ref, val in zip(douts, grads[1:7]):
            ref[...] = val

        @pl.when(first)
        def _():
            for ref, val in zip(dpars, grads[7:]):
                ref[...] = val

        @pl.when(jnp.logical_not(first))
        def _():
            for ref, val in zip(dpars, grads[7:]):
                ref[...] += val

    sshape = jax.ShapeDtypeStruct((h, t, dh), F32)
    pshape = jax.ShapeDtypeStruct((h, 1, dh), F32)
    return pl.pallas_call(
        body, name="rwkv_scan_bwd", grid=(nc,), out_shape=(sshape,) * 6 + (pshape,) * 3,
        in_specs=[seq] * 6 + [par] * 3 + [state, seq], out_specs=(seq,) * 6 + (par,) * 3,
        scratch_shapes=[pltpu.VMEM((h, dh, dh), F32)], compiler_params=_params("arbitrary"),
    )(*seqs, *pars, states, do)


def _local_step(x, target, w, ex):
    w = dict(w)
    c = w["mu_r"].shape[-1]
    qn, kn = w["q_norm"].reshape(1, 1, HEAD_DIM), w["k_norm"].reshape(1, 1, HEAD_DIM)
    vec = jnp.concatenate([w[n].reshape(1, c) for n in RWKV_VEC], axis=0)
    pars = [w[n].reshape(-1, 1, HEAD_DIM) for n in ("r_k", "ln_x_w", "ln_x_b")]
    no_dep = jnp.zeros(DEP_SHAPE, F32)

    x1, gate1, up1 = _ffn_fwd(x, w["ffn1_norm"], w["ffn1_w_gate"], w["ffn1_w_up"], w["ffn1_w_down"], ex.first_dep, "ffn1_fwd")
    w.update(ex.mix_weights((x1,)))
    mats = [w[n] for n in RWKV_MAT]
    q, k, v, cur = _proj_fwd(x1, w["mix_norm"], w["w_in"], c)
    att, *saved = _att_fwd(q, k, v, qn, kn)
    pre = _rwkv_pre_fwd(cur, vec, mats)
    seqs, gate = pre[:6], pre[6]
    opg, states = _scan_fwd(seqs, pars)
    w.update(ex.out_weights((att, opg)))
    x2 = _mixout_fwd(x1, att, opg, gate, w["w_out"])
    dy, gate2, up2, loss = _ffn_fwd(x2, w["ffn2_norm"], w["ffn2_w_gate"], w["ffn2_w_up"], w["ffn2_w_down"], no_dep, "ffn2_fwd",
                                    target=target)

    g = {}
    dx2, g["ffn2_norm"], g["ffn2_w_gate"], g["ffn2_w_up"], g["ffn2_w_down"] = _ffn_bwd(
        x2, w["ffn2_norm"], w["ffn2_w_gate"], w["ffn2_w_up"], w["ffn2_w_down"], gate2, up2, dy, no_dep, "ffn2_bwd")
    dep = ex.send_ffn2({n: g[n] for n in ("ffn2_w_gate", "ffn2_w_up", "ffn2_w_down")})
    datt, dopg, dgate, g["w_out"] = _mixout_bwd(att, opg, gate, w["w_out"], dx2, dep)
    dscan = _scan_bwd(seqs, pars, states, dopg)
    for n, d in zip(("r_k", "ln_x_w", "ln_x_b"), dscan[6:]):
        g[n] = d
    dcur, dvec, *dmats = _rwkv_pre_bwd(cur, vec, mats, dscan[:6], dgate)
    for n, d in zip(RWKV_MAT, dmats):
        g[n] = d
    g["rwkv_vec"] = dvec
    dq, dk, dv, g["q_norm"], g["k_norm"] = _att_bwd(q, k, v, qn, kn, saved, datt)
    dx1, g["mix_norm"], g["w_in"] = _proj_bwd(x1, w["mix_norm"], w["w_in"], dq, dk, dv, dcur, dx2)
    dep = ex.send_mix({n: g[n] for n in ("w_in", "w_out") + RWKV_MAT}, (dx1,))
    dx, g["ffn1_norm"], g["ffn1_w_gate"], g["ffn1_w_up"], g["ffn1_w_down"] = _ffn_bwd(
        x, w["ffn1_norm"], w["ffn1_w_gate"], w["ffn1_w_up"], w["ffn1_w_down"], gate1, up1, dx1, dep, "ffn1_bwd")
    return loss, dx, g


N_SHARDS = 4


def _place():
    return lax.axis_index("x"), lax.axis_index("y"), lax.axis_index("c")


def _chip_peers(x, y):
    return [(1 - x, y), (x, 1 - y), (1 - x, 1 - y)]


HBM = pl.BlockSpec(memory_space=pltpu.HBM)
SEM = pl.BlockSpec(memory_space=pltpu.SEMAPHORE)
DEP_SHAPE = (8, 128)


class _Views:
    to_sibling = False


class _GatherViews(_Views):
    @staticmethod
    def send(i, srcs, lands, k, at):
        return srcs[i], lands[i].at[at[3]]

    @staticmethod
    def landing(i, srcs, lands, k, at):
        return srcs[i], lands[i].at[2 * at[4] + at[5]]


class _ScatterViews(_Views):
    @staticmethod
    def send(i, srcs, lands, k, at):
        return srcs[i].at[2 * at[4] + at[5]], lands[i].at[k]

    @staticmethod
    def landing(i, srcs, lands, k, at):
        return srcs[i].at[at[3]], lands[i].at[k]


def _half_rows(ref, slot, half):
    rows = ref.shape[1] // 2
    return ref.at[slot, pl.ds(pl.multiple_of(half * rows, BF16_SUBLANES), rows)]


class _HalfGatherViews(_Views):
    @staticmethod
    def send(i, srcs, lands, k, at):
        rows = srcs[i].shape[0] // 2
        return srcs[i].at[pl.ds(pl.multiple_of(at[2] * rows, BF16_SUBLANES), rows)], _half_rows(lands[i], at[3], at[2])

    @staticmethod
    def landing(i, srcs, lands, k, at):
        rows = srcs[i].shape[0] // 2
        return srcs[i].at[pl.ds(pl.multiple_of(at[2] * rows, BF16_SUBLANES), rows)], _half_rows(lands[i], 2 * at[4] + at[5], at[2])


class _ForwardViews(_Views):
    to_sibling = True

    @staticmethod
    def send(i, srcs, lands, k, at):
        mine = _half_rows(lands[i], 2 * at[4] + at[5], at[2])
        return mine, mine

    @staticmethod
    def landing(i, srcs, lands, k, at):
        theirs = _half_rows(lands[i], 2 * at[4] + at[5], 1 - at[2])
        return theirs, theirs


def _push_start(srcs, lands, views, after, name):
    ns, nl = len(srcs), len(lands)

    def body(*refs):
        src_refs, land_refs = refs[:ns], refs[ns:ns + nl]
        send_sems, recv_sems = refs[ns + nl + 1:ns + nl + 3]
        token = refs[2 * (ns + nl) + 3]
        x, y, c = _place()
        for i in range(nl):
            for k, (px, py) in enumerate(_chip_peers(x, y)):
                src, dst = views.send(i, src_refs, land_refs, k, (x, y, c, 2 * x + y, px, py))
                pltpu.make_async_remote_copy(
                    src_ref=src, dst_ref=dst, send_sem=send_sems.at[3 * i + k], recv_sem=recv_sems.at[3 * i + k],
                    device_id=(x, y, 1 - c) if views.to_sibling else (px, py, c), device_id_type=MESH).start()
        token[...] = jnp.zeros_like(token)

    sems = pltpu.SemaphoreType.DMA((3 * nl,))
    both = [pltpu.with_memory_space_constraint(a, pltpu.HBM) for a in (*srcs, *lands)]
    outs = pl.pallas_call(
        body, name=name,
        out_shape=(sems, sems, *[pltpu.HBM(a.shape, a.dtype) for a in both], jax.ShapeDtypeStruct(DEP_SHAPE, F32)),
        in_specs=[HBM] * (ns + nl) + [ANY], out_specs=(SEM, SEM, *[HBM] * (ns + nl), VMEM_FULL),
        input_output_aliases={i: 2 + i for i in range(ns + nl)},
        compiler_params=pltpu.CompilerParams(has_side_effects=pltpu.SideEffectType.DATAFLOW_SIDE_EFFECTING),
    )(*both, after)
    return outs[0], outs[1], outs[2:2 + ns], outs[2 + ns:2 + ns + nl], outs[2 + ns + nl]


def _push_wait(started, views, after, name, with_sources=False):
    send_sems, recv_sems, srcs, lands, _ = started
    ns, nl = len(srcs), len(lands)

    def body(*refs):
        src_refs, land_refs = refs[:ns], refs[ns:ns + nl]
        send_sems, recv_sems = refs[ns + nl:ns + nl + 2]
        x, y, c = _place()
        for i in range(nl):
            for k, (px, py) in enumerate(_chip_peers(x, y)):
                src, dst = views.landing(i, src_refs, land_refs, k, (x, y, c, 2 * x + y, px, py))
                landing = pltpu.make_async_remote_copy(
                    src_ref=src, dst_ref=dst, send_sem=send_sems.at[3 * i + k], recv_sem=recv_sems.at[3 * i + k],
                    device_id=(x, y, 1 - c) if views.to_sibling else (px, py, c), device_id_type=MESH)
                landing.wait_send()
                landing.wait_recv()

    outs = pl.pallas_call(
        body, name=name,
        out_shape=tuple(pltpu.HBM(a.shape, a.dtype) for a in (*srcs, *lands)),
        in_specs=[HBM] * (ns + nl) + [SEM, SEM] + [ANY] * len(after), out_specs=(HBM,) * (ns + nl),
        input_output_aliases={i: i for i in range(ns + nl)},
        compiler_params=pltpu.CompilerParams(has_side_effects=pltpu.SideEffectType.DATAFLOW_SIDE_EFFECTING),
    )(*srcs, *lands, send_sems, recv_sems, *after)
    return outs if with_sources else outs[ns:]


def _empty_lands(shards, slots, own_slot):
    lands = [lax.empty((slots,) + s.shape, s.dtype) for s in shards]
    if own_slot:
        me = 2 * lax.axis_index("x") + lax.axis_index("y")
        lands = [lax.dynamic_update_index_in_dim(z, s, me, 0) for z, s in zip(lands, shards)]
    return lands


def _sibling_swap(arrays, name, other_half=False):
    n = len(arrays)

    def body(*refs):
        ins, outs = refs[:n], refs[n:2 * n]
        send_sems, recv_sems = refs[2 * n:]
        x, y, c = _place()
        copies = []
        for i in range(n):
            src = ins[i]
            if other_half:
                rows = src.shape[1] // 2
                src = src.at[:, pl.ds(pl.multiple_of((1 - c) * rows, BF16_SUBLANES), rows)]
            cp = pltpu.make_async_remote_copy(
                src_ref=src, dst_ref=outs[i], send_sem=send_sems.at[i], recv_sem=recv_sems.at[i],
                device_id=(x, y, 1 - c), device_id_type=MESH)
            cp.start()
            copies.append(cp)
        for cp in copies:
            cp.wait()

    shapes = [(a.shape[0], a.shape[1] // 2, a.shape[2]) if other_half else a.shape for a in arrays]
    return pl.pallas_call(
        body, name=name,
        out_shape=tuple(jax.ShapeDtypeStruct(s, a.dtype) for s, a in zip(shapes, arrays)),
        in_specs=[ANY] * n, out_specs=(ANY,) * n,
        scratch_shapes=[pltpu.SemaphoreType.DMA((n,)), pltpu.SemaphoreType.DMA((n,))],
    )(*arrays)


def _sibling_fill(arrays, name):
    n = len(arrays)

    def body(*refs):
        outs = refs[n:2 * n]
        send_sems, recv_sems = refs[2 * n:]
        x, y, c = _place()
        copies = []
        for i in range(n):
            rows = outs[i].shape[0] // 2
            mine = outs[i].at[pl.ds(pl.multiple_of(c * rows, BF16_SUBLANES), rows)]
            cp = pltpu.make_async_remote_copy(
                src_ref=mine, dst_ref=mine, send_sem=send_sems.at[i], recv_sem=recv_sems.at[i],
                device_id=(x, y, 1 - c), device_id_type=MESH)
            cp.start()
            copies.append(cp)
        for cp in copies:
            cp.wait()

    return pl.pallas_call(
        body, name=name, out_shape=tuple(jax.ShapeDtypeStruct(a.shape, a.dtype) for a in arrays),
        in_specs=[ANY] * n, out_specs=(ANY,) * n, input_output_aliases={i: i for i in range(n)},
        scratch_shapes=[pltpu.SemaphoreType.DMA((n,)), pltpu.SemaphoreType.DMA((n,))],
    )(*arrays)


FOLD_STEPS = 2


def _fold_add(core, parts, theirs, name):
    n = len(parts)
    s, r, cols = parts[0].shape
    tr = r // 2 // FOLD_STEPS

    def body(core_ref, *refs):
        for p_ref, t_ref, o_ref in zip(refs[:n], refs[n:2 * n], refs[2 * n:]):
            o_ref[...] = (p_ref[...].astype(F32) + t_ref[...].astype(F32)).astype(BF16)

    half = pl.BlockSpec((1, tr, cols), lambda j, i, core_ref: (j, i, 0))
    return pl.pallas_call(
        body, name=name, out_shape=tuple(jax.ShapeDtypeStruct((s, r // 2, cols), BF16) for _ in parts),
        grid_spec=pltpu.PrefetchScalarGridSpec(
            num_scalar_prefetch=1, grid=(s, FOLD_STEPS),
            in_specs=[pl.BlockSpec((1, tr, cols), lambda j, i, core_ref: (j, core_ref[0] * FOLD_STEPS + i, 0))] * n + [half] * n,
            out_specs=(half,) * n),
        compiler_params=_params("arbitrary", "arbitrary"),
    )(core, *parts, *theirs)


N_DEV = 8


PACK_COLS = 1024
PACK_ROWS = 24


def _put_row(pack_ref, row, ref):
    if len(ref.shape) == 2:
        pack_ref[row:row + 1, :ref.shape[1]] = ref[...]
    else:
        for h in range(ref.shape[0]):
            pack_ref[row:row + 1, h * HEAD_DIM:(h + 1) * HEAD_DIM] = ref[h]


def _allreduce_small(grads, rows):
    n = len(grads)

    def body(*refs):
        in_ref, out_ref, buf, send_sems, recv_sems = refs[n + 1], refs[n], *refs[n + 2:]
        in_ref[...] = jnp.zeros_like(in_ref)
        for ref, row in zip(refs[:n], rows):
            if len(ref.shape) == 2 and ref.shape[0] > 1:
                in_ref[row:row + ref.shape[0], :ref.shape[1]] = ref[...]
            else:
                _put_row(in_ref, row, ref)
        x, y, c = _place()
        me = 4 * x + 2 * y + c
        buf[me] = in_ref[...]

        def copy(j, slot):
            px, py, pc = x ^ (j >> 2), y ^ ((j >> 1) & 1), c ^ (j & 1)
            return pltpu.make_async_remote_copy(
                src_ref=in_ref, dst_ref=buf.at[slot(px, py, pc)], send_sem=send_sems.at[j], recv_sem=recv_sems.at[j],
                device_id=(px, py, pc), device_id_type=MESH)

        for j in range(1, N_DEV):
            copy(j, lambda px, py, pc: me).start()
        for j in range(1, N_DEV):
            landing = copy(j, lambda px, py, pc: 4 * px + 2 * py + pc)
            landing.wait_send()
            landing.wait_recv()
        acc = buf[0]
        for s in range(1, N_DEV):
            acc = acc + buf[s]
        out_ref[...] = acc

    shape = (PACK_ROWS, PACK_COLS)
    return pl.pallas_call(
        body, name="allreduce_small", out_shape=jax.ShapeDtypeStruct(shape, F32),
        in_specs=[VMEM_FULL] * n, out_specs=VMEM_FULL,
        scratch_shapes=[pltpu.VMEM(shape, F32), pltpu.VMEM((N_DEV,) + shape, F32), pltpu.SemaphoreType.DMA((N_DEV,)),
                        pltpu.SemaphoreType.DMA((N_DEV,))],
    )(*grads)


BF16_SUBLANES = 16


def _reduce_own(me, parts, recvs, dep, steps, name, half=None):
    n = len(parts)
    where = me if half is None else jnp.concatenate([me, half])
    offset = (lambda w: 0) if half is None else (lambda w: w[1] * steps)

    def body(where_ref, *refs):
        for p_ref, rv_ref, o_ref in zip(refs[:n], refs[n:2 * n], refs[2 * n + 1:]):
            acc = p_ref[0].astype(F32)
            for k in range(3):
                acc = acc + rv_ref[k].astype(F32)
            o_ref[...] = acc

    shapes = [(p.shape[1] // steps, p.shape[2]) for p in parts]
    rows = 1 if half is None else 2
    return pl.pallas_call(
        body, name=name, out_shape=tuple(jax.ShapeDtypeStruct((rows * p.shape[1], p.shape[2]), F32) for p in parts),
        grid_spec=pltpu.PrefetchScalarGridSpec(
            num_scalar_prefetch=1, grid=(steps,),
            in_specs=[pl.BlockSpec((1, tr, c), lambda i, w: (w[0], i, 0)) for tr, c in shapes]
            + [pl.BlockSpec((3, tr, c), lambda i, w: (0, i, 0)) for tr, c in shapes] + [ANY],
            out_specs=tuple(pl.BlockSpec((tr, c), lambda i, w: (offset(w) + i, 0)) for tr, c in shapes)),
        compiler_params=_params("arbitrary"),
    )(where, *parts, *recvs, dep)


def _adamw_step(w, g, m, v):
    mn = ADAM_B1 * m + (1.0 - ADAM_B1) * g
    vn = ADAM_B2 * v + (1.0 - ADAM_B2) * (g * g)
    m_hat = mn / (1.0 - ADAM_B1 ** ADAM_STEP)
    v_hat = vn / (1.0 - ADAM_B2 ** ADAM_STEP)
    return -ADAM_LR * (m_hat / (jnp.sqrt(v_hat) + ADAM_EPS) + ADAM_WD * w), mn, vn


def _adamw(ws, gas, gbs, ms, vs, steps, name):
    n = len(ws)
    operands = [ws, gas, ms, vs] if gbs is None else [ws, gas, gbs, ms, vs]
    k = len(operands)

    def body(*refs):
        ins, outs = refs[:k * n], refs[k * n:]
        for j in range(n):
            w_ref, ga_ref, *gb_ref, m_ref, v_ref = ins[j::n]
            g_out, d_out, m_out, v_out = outs[j::n]
            g = ga_ref[...] + gb_ref[0][...] if gb_ref else ga_ref[...]
            g_out[...] = g
            d_out[...], m_out[...], v_out[...] = _adamw_step(w_ref[...], g, m_ref[...], v_ref[...])

    tiles = [pl.BlockSpec((w.shape[0] // steps, w.shape[1]), lambda i: (i, 0)) for w in ws]
    shapes = [jax.ShapeDtypeStruct(w.shape, F32) for w in ws]
    outs = pl.pallas_call(
        body, name=name, grid=(steps,), out_shape=tuple(shapes * 4), in_specs=tiles * k, out_specs=tuple(tiles * 4),
        compiler_params=_params("arbitrary"),
    )(*[a for group in operands for a in group])
    return [outs[j::n] for j in range(n)]


def _adamw_replicated(gsum, ws, ms, vs):
    n = len(ws)

    def body(g_ref, *refs):
        ins, outs = refs[:3 * n], refs[3 * n:]
        for i in range(n):
            w_ref, m_ref, v_ref = ins[i::n]
            shape = w_ref.shape
            if len(shape) == 2:
                g = g_ref[i:i + 1, :shape[1]]
            else:
                g = jnp.concatenate([g_ref[i:i + 1, h * HEAD_DIM:(h + 1) * HEAD_DIM] for h in range(shape[1])], axis=0)[None]
            g_out, d_out, m_out, v_out = outs[i::n]
            g_out[...] = g
            d_out[...], m_out[...], v_out[...] = _adamw_step(w_ref[...], g, m_ref[...], v_ref[...])

    shapes = [jax.ShapeDtypeStruct(w.shape, F32) for w in ws]
    outs = pl.pallas_call(
        body, name="adamw_replicated", out_shape=tuple(shapes * 4),
        in_specs=[VMEM_FULL] * (1 + 3 * n), out_specs=(VMEM_FULL,) * (4 * n),
    )(gsum, *ws, *ms, *vs)
    return [outs[i::n] for i in range(n)]


COL_SHARDED = ("ffn1_w_gate", "ffn1_w_up", "w_in", "ffn2_w_gate", "ffn2_w_up", "w1", "w2", "a1", "a2", "g1", "g2")
ROW_SHARDED = ("ffn1_w_down", "ffn2_w_down", "w_out")
CHUNKED = ("ffn1_w_gate", "ffn1_w_up", "ffn1_w_down", "w_in", "ffn2_w_gate", "ffn2_w_up", "ffn2_w_down")
WEIGHTS = ("ffn1_norm", "ffn1_w_gate", "ffn1_w_up", "ffn1_w_down", "mix_norm", "w_in", "q_norm", "k_norm",
           "mu_r", "mu_k", "mu_v", "mu_w", "mu_a", "mu_g", "w0", "w1", "w2", "a0", "a1", "a2", "g1", "g2",
           "k_k", "k_a", "r_k", "ln_x_w", "ln_x_b", "w_out", "ffn2_norm", "ffn2_w_gate", "ffn2_w_up", "ffn2_w_down")


TRANSPOSED = ("ffn1_w_gate", "ffn1_w_up", "ffn2_w_gate", "ffn2_w_up", "w1", "a1", "g1")


def _shard_2d(name, a):
    return a[0].T if name in TRANSPOSED else a[0]


def _full_from_blocks(name, blocks):
    if name in CHUNKED:
        return blocks
    if name in ROW_SHARDED:
        return blocks.reshape(-1, blocks.shape[-1])
    return blocks.transpose(1, 0, 2).reshape(blocks.shape[1], -1)


def _blocks_from_full(name, full):
    if name in CHUNKED:
        return full
    if name in ROW_SHARDED:
        return full.reshape(N_SHARDS, -1, full.shape[-1])
    return full.reshape(full.shape[0], N_SHARDS, -1).transpose(1, 0, 2)


FFN1_GROUP = ("ffn1_w_gate", "ffn1_w_up", "ffn1_w_down")
MIX_GROUP = ("w_in",) + RWKV_MAT
OUT_GROUP = ("w_out", "ffn2_w_gate", "ffn2_w_up", "ffn2_w_down")
FFN2_GROUP = OUT_GROUP[1:]
LATE_GROUP = ("w_in", "w_out") + RWKV_MAT


class _Exchange:
    def __init__(self, given):
        self.given = given
        first = self._gather_start(FFN1_GROUP, _HalfGatherViews, jnp.zeros(DEP_SHAPE, F32), "gather_ffn1_start")
        self.mix = self._gather_start(MIX_GROUP, _GatherViews, first[4], "gather_mix_start")
        self.out = self._gather_start(OUT_GROUP, _GatherViews, self.mix[4], "gather_out_start")
        self.first_dep = self.out[4]
        halves = _push_wait(first, _HalfGatherViews, (self.first_dep,), "gather_ffn1_wait")
        passed = _push_start([], halves, _ForwardViews, jnp.zeros(DEP_SHAPE, F32), "gather_ffn1_pass_start")
        self.first_weights = self._full(FFN1_GROUP, _push_wait(passed, _ForwardViews, (passed[4],), "gather_ffn1_pass_wait"))
        self.parts, self.recv = {}, {}

    @staticmethod
    def _full(names, blocks):
        out = {}
        for n, b in zip(names, blocks):
            full = _full_from_blocks(n, b)
            out[n] = full.astype(F32) if n in RWKV_MAT else full
        return out

    def _gather_start(self, names, views, after, name):
        after, raw = lax.optimization_barrier((after, [_shard_2d(n, self.given[n]) for n in names]))
        shards = [a.astype(BF16) for a in raw]
        return _push_start(shards, _empty_lands(shards, N_SHARDS, True), views, after, name)

    def mix_weights(self, after):
        return self._full(MIX_GROUP, _push_wait(self.mix, _GatherViews, after, "gather_mix_wait"))

    def out_weights(self, after):
        return self._full(OUT_GROUP, _push_wait(self.out, _GatherViews, after, "gather_out_wait"))

    def _scatter_start(self, grads, name):
        names = tuple(grads)
        parts = [_blocks_from_full(n, grads[n]) for n in names]
        self.parts.update(zip(names, parts))
        lands = [lax.empty((3,) + p.shape[1:], BF16) for p in parts]
        return _push_start([p.astype(BF16) for p in parts], lands, _ScatterViews, jnp.zeros(DEP_SHAPE, F32), name)

    def _scatter_done(self, started, names, after, name):
        outs = _push_wait(started, _ScatterViews, after, name, with_sources=True)
        for n, sent, got in zip(names, outs[:len(names)], outs[len(names):]):
            self.recv[n] = got
            if self.parts[n].dtype == BF16:
                self.parts[n] = sent

    def send_ffn2(self, grads):
        self.ffn2 = self._scatter_start(grads, "scatter_ffn2_start")
        return self.ffn2[4]

    def send_mix(self, grads, after):
        self._scatter_done(self.ffn2, FFN2_GROUP, after, "scatter_ffn2_wait")
        self.late = self._scatter_start(grads, "scatter_late_start")
        return self.late[4]

    def send_ffn1(self, grads):
        self.ffn1 = self._scatter_start(grads, "scatter_ffn1_start")
        return self.ffn1[4]

    def late_received(self, after):
        self._scatter_done(self.late, LATE_GROUP, after, "scatter_late_wait")

    def ffn1_received(self, after):
        self._scatter_done(self.ffn1, FFN1_GROUP, after, "scatter_ffn1_wait")


def kernel(
        x, ffn1_norm, ffn1_w_gate, ffn1_w_up, ffn1_w_down, mix_norm, w_in, q_norm, k_norm, mu_r, mu_k, mu_v, mu_w,
        mu_a, mu_g, w0, w1, w2, a0, a1, a2, g1, g2, k_k, k_a, r_k, ln_x_w, ln_x_b, w_out, ffn2_norm, ffn2_w_gate,
        ffn2_w_up, ffn2_w_down, loss_target, m_ffn1_norm, m_ffn1_w_gate, m_ffn1_w_up, m_ffn1_w_down, m_mix_norm,
        m_w_in, m_q_norm, m_k_norm, m_mu_r, m_mu_k, m_mu_v, m_mu_w, m_mu_a, m_mu_g, m_w0, m_w1, m_w2, m_a0, m_a1,
        m_a2, m_g1, m_g2, m_k_k, m_k_a, m_r_k, m_ln_x_w, m_ln_x_b, m_w_out, m_ffn2_norm, m_ffn2_w_gate, m_ffn2_w_up,
        m_ffn2_w_down, v_ffn1_norm, v_ffn1_w_gate, v_ffn1_w_up, v_ffn1_w_down, v_mix_norm, v_w_in, v_q_norm, v_k_norm,
        v_mu_r, v_mu_k, v_mu_v, v_mu_w, v_mu_a, v_mu_g, v_w0, v_w1, v_w2, v_a0, v_a1, v_a2, v_g1, v_g2, v_k_k, v_k_a,
        v_r_k, v_ln_x_w, v_ln_x_b, v_w_out, v_ffn2_norm, v_ffn2_w_gate, v_ffn2_w_up, v_ffn2_w_down):
    given = dict(locals())
    sharded = COL_SHARDED + ROW_SHARDED
    sharded = tuple(n for n in WEIGHTS if n in sharded)
    small = tuple(n for n in WEIGHTS if n not in sharded)

    ex = _Exchange(given)
    w = {n: given[n] for n in small}
    w.update(ex.first_weights)
    loss, dx, g = _local_step(x[0], loss_target[0], w, ex)

    core = lax.axis_index("c").astype(jnp.int32).reshape(1)
    late = [g[n] for n in FFN1_GROUP]
    folded = _fold_add(core, late, _sibling_swap(late, "fold_swap_ffn1", other_half=True), "fold_add_ffn1")
    dep = ex.send_ffn1(dict(zip(FFN1_GROUP, folded)))

    me = (2 * lax.axis_index("x") + lax.axis_index("y")).astype(jnp.int32).reshape(1)
    out = {}

    def settle(names, dep, tag):
        done = []
        for kind, sub, r_steps, a_steps in (("large", tuple(n for n in names if n not in RWKV_MAT), 4, 8),
                                            ("small", tuple(n for n in names if n in RWKV_MAT), 1, 1)):
            if not sub:
                continue
            parts = [ex.parts[n].reshape(N_SHARDS, -1, ex.parts[n].shape[-1]) for n in sub]
            recvs = [ex.recv[n].reshape(3, -1, ex.recv[n].shape[-1]) for n in sub]
            mine = _reduce_own(me, parts, recvs, dep, r_steps, f"reduce_{tag}_{kind}")
            theirs = _sibling_swap(mine, f"sibling_swap_{tag}_{kind}")
            res = _adamw([_shard_2d(n, given[n]) for n in sub], mine, theirs, [_shard_2d(n, given["m_" + n]) for n in sub],
                         [_shard_2d(n, given["v_" + n]) for n in sub], a_steps, f"adamw_{tag}_{kind}")
            for n, rs in zip(sub, res):
                out[n] = [(r.T if n in TRANSPOSED else r).reshape(given[n].shape) for r in rs]
                done.append(out[n][1])
        return tuple(done)

    ex.late_received((dep,))
    last = settle(tuple(n for n in sharded if n not in FFN1_GROUP), dep, "rest")

    row = {n: i for i, n in enumerate(small)}
    singles = [n for n in small if n not in RWKV_VEC]
    gsum = _allreduce_small([g[n] for n in singles] + [g["rwkv_vec"], loss],
                            [row[n] for n in singles] + [row[RWKV_VEC[0]], len(small)])
    res = _adamw_replicated(gsum, [given[n] for n in small], [given["m_" + n] for n in small], [given["v_" + n] for n in small])
    for n, rs in zip(small, res):
        out[n] = list(rs)
    total_loss = gsum[len(small), 0]

    ex.ffn1_received((*last, res[0][1]))
    halves = _reduce_own(me, [ex.parts[n] for n in FFN1_GROUP], [ex.recv[n] for n in FFN1_GROUP],
                         jnp.zeros(DEP_SHAPE, F32), FOLD_STEPS, "reduce_ffn1", half=core)
    grads = _sibling_fill(halves, "sibling_fill_ffn1")
    res = _adamw([_shard_2d(n, given[n]) for n in FFN1_GROUP], grads, None, [_shard_2d(n, given["m_" + n]) for n in FFN1_GROUP],
                 [_shard_2d(n, given["v_" + n]) for n in FFN1_GROUP], 8, "adamw_ffn1")
    for n, rs in zip(FFN1_GROUP, res):
        out[n] = [(r.T if n in TRANSPOSED else r).reshape(given[n].shape) for r in rs]
    return (total_loss, dx[None], *[out[n][0] for n in WEIGHTS], *[out[n][1] for n in WEIGHTS],
            *[out[n][2] for n in WEIGHTS], *[out[n][3] for n in WEIGHTS])
```

```python
import functools

import jax
import jax.numpy as jnp
from jax import lax
from jax.experimental import pallas as pl
from jax.experimental.pallas import tpu as pltpu

F32 = jnp.float32
BF16 = jnp.bfloat16
MESH = pl.DeviceIdType.MESH

RMS_EPS = 1e-6
GN_EPS = 64e-5
NEG_INF = -1e30
FFN_RESIDUAL = 0.5
HEAD_DIM = 64
ATT_BLOCK = 128
DILATIONS = (1, 4, 16)
SCAN_CHUNK = 64
TOKEN_TILE = 256
FFN_BWD_TILE = 512

ADAM_LR = 0.001
ADAM_B1 = 0.9
ADAM_B2 = 0.999
ADAM_EPS = 1e-08
ADAM_WD = 0.01
ADAM_STEP = 10

VMEM_FULL = pl.BlockSpec(memory_space=pltpu.VMEM)
ANY = pl.BlockSpec(memory_space=pl.ANY)


VMEM_LIMIT = 56 * 1024 * 1024


def _params(*sem):
    return pltpu.CompilerParams(dimension_semantics=sem, vmem_limit_bytes=VMEM_LIMIT)


def _dot(a, b, dims):
    return lax.dot_general(a.astype(BF16), b.astype(BF16), (dims, ((), ())), preferred_element_type=F32)


def _dot_nn(a, b):
    return _dot(a, b, ((1,), (0,)))


def _dot_nt(a, b):
    return _dot(a, b, ((1,), (1,)))


def _dot_tn(a, b):
    return _dot(a, b, ((0,), (0,)))


@jax.custom_vjp
def _mm(a, b):
    return _dot_nn(a, b)


def _mm_fwd(a, b):
    return _dot_nn(a, b), (a, b)


def _mm_bwd(res, g):
    a, b = res
    return _dot_nt(g, b).astype(a.dtype), _dot_tn(a, g).astype(b.dtype)


_mm.defvjp(_mm_fwd, _mm_bwd)


@jax.custom_vjp
def _mm_nt(a, bt):
    return _dot_nt(a, bt)


def _mm_nt_fwd(a, bt):
    return _dot_nt(a, bt), (a, bt)


def _mm_nt_bwd(res, g):
    a, bt = res
    return _dot_nn(g, bt).astype(a.dtype), _dot_tn(g, a).astype(bt.dtype)


_mm_nt.defvjp(_mm_nt_fwd, _mm_nt_bwd)


def _bdot(a, b, ca, cb):
    return lax.dot_general(a.astype(BF16), b.astype(BF16), (((ca,), (cb,)), ((0,), (0,))), preferred_element_type=F32)


@jax.custom_vjp
def _bmm_nt(a, b):
    return _bdot(a, b, 2, 2)


def _bmm_nt_fwd(a, b):
    return _bdot(a, b, 2, 2), (a, b)


def _bmm_nt_bwd(res, g):
    a, b = res
    return _bdot(g, b, 2, 1), _bdot(g, a, 1, 1)


_bmm_nt.defvjp(_bmm_nt_fwd, _bmm_nt_bwd)


@jax.custom_vjp
def _bmm_nn(a, b):
    return _bdot(a, b, 2, 1)


def _bmm_nn_fwd(a, b):
    return _bdot(a, b, 2, 1), (a, b)


def _bmm_nn_bwd(res, g):
    a, b = res
    return _bdot(g, b, 2, 2), _bdot(a, g, 1, 1)


_bmm_nn.defvjp(_bmm_nn_fwd, _bmm_nn_bwd)


@jax.custom_vjp
def _bmm_tn(a, b):
    return _bdot(a, b, 1, 1)


def _bmm_tn_fwd(a, b):
    return _bdot(a, b, 1, 1), (a, b)


def _bmm_tn_bwd(res, g):
    a, b = res
    return _bdot(b, g, 2, 2), _bdot(a, g, 2, 1)


_bmm_tn.defvjp(_bmm_tn_fwd, _bmm_tn_bwd)


def _hdot(a, b, ca, cb):
    return lax.dot_general(a, b, (((ca,), (cb,)), ((0,), (0,))), precision=lax.Precision.HIGH, preferred_element_type=F32)


def _sigmoid(x):
    return 1.0 / (1.0 + jnp.exp(-x))


def _rms(x):
    return lax.rsqrt(jnp.mean(x * x, axis=-1, keepdims=True) + RMS_EPS)


def _ffn_fwd(x, norm, wg, wu, wd, dep, name, target=None):
    t, d = x.shape
    nc, fc, _ = wg.shape
    tm = TOKEN_TILE

    def body(x_ref, n_ref, wg_ref, wu_ref, wd_ref, dep_ref, *rest):
        o_ref, g_ref, u_ref = rest[-3:] if target is None else rest[1:4]
        xv = x_ref[...]
        h = (xv * _rms(xv) * n_ref[...]).astype(BF16)
        acc = jnp.zeros((tm, d), F32)
        for c in range(nc):
            g = _dot_nt(h, wg_ref[c])
            u = _dot_nt(h, wu_ref[c])
            g_ref[c] = g.astype(BF16)
            u_ref[c] = u.astype(BF16)
            a = (g * _sigmoid(g) * u).astype(BF16)
            acc = acc + jnp.dot(a, wd_ref[c], preferred_element_type=F32)
        y = xv + FFN_RESIDUAL * acc
        if target is None:
            o_ref[...] = y
        else:
            t_ref, loss_ref = rest[0], rest[4]
            err = y - t_ref[...]
            o_ref[...] = err * (1.0 / d)
            part = 0.5 * jnp.sum(jnp.mean(err * err, axis=-1, keepdims=True), axis=0, keepdims=True)

            @pl.when(pl.program_id(0) == 0)
            def _():
                loss_ref[...] = jnp.zeros_like(loss_ref)

            loss_ref[...] += jnp.broadcast_to(part, loss_ref.shape)

    tile = pl.BlockSpec((tm, d), lambda i: (i, 0))
    hidden = pl.BlockSpec((nc, tm, fc), lambda i: (0, i, 0))
    hshape = jax.ShapeDtypeStruct((nc, t, fc), BF16)
    with_loss = target is not None
    return pl.pallas_call(
        body, name=name, grid=(t // tm,),
        out_shape=(jax.ShapeDtypeStruct((t, d), F32), hshape, hshape) + ((jax.ShapeDtypeStruct((1, 128), F32),) if with_loss else ()),
        in_specs=[tile, pl.BlockSpec((1, d), lambda i: (0, 0)), VMEM_FULL, VMEM_FULL, VMEM_FULL, ANY] + ([tile] if with_loss else []),
        out_specs=(tile, hidden, hidden) + ((pl.BlockSpec((1, 128), lambda i: (0, 0)),) if with_loss else ()),
        compiler_params=_params("arbitrary"),
    )(x, norm, wg, wu, wd, dep, *((target,) if with_loss else ()))


def _rmsnorm_bwd(xv, gain, dh):
    rs = _rms(xv)
    xn = xv * rs
    dxn = dh * gain
    dx = rs * (dxn - xn * jnp.mean(dxn * xn, axis=-1, keepdims=True))
    return dx, jnp.sum(dh * xn, axis=0, keepdims=True)


def _ffn_bwd(x, norm, wg, wu, wd, gate, up, dy, dep, name):
    t, d = x.shape
    nc, fc, _ = wg.shape
    tm = FFN_BWD_TILE
    nt = t // tm

    def body(x_ref, n_ref, wg_ref, wu_ref, wd_ref, g_ref, u_ref, dy_ref, dep_ref, dx_ref, dn_ref, dwg_ref, dwu_ref,
             dwd_ref, dh_ref, ag_ref, au_ref, ad_ref):
        c, i = pl.program_id(0), pl.program_id(1)
        rows = pl.ds(pl.multiple_of(i * tm, tm), tm)
        xv = x_ref[...]
        gain = n_ref[...]
        h = (xv * _rms(xv) * gain).astype(BF16)
        dy = dy_ref[...]
        dyb = (FFN_RESIDUAL * dy).astype(BF16)
        g = g_ref[0].astype(F32)
        u = u_ref[0].astype(F32)
        sg = _sigmoid(g)
        s = g * sg
        a = (s * u).astype(BF16)
        da = _dot_nt(dyb, wd_ref[0])
        dub = (da * s).astype(BF16)
        dgb = (da * u * (sg * (1.0 + g * (1.0 - sg)))).astype(BF16)
        dwd_c = _dot_tn(a, dyb)
        dwg_c = _dot_tn(dgb, h)
        dwu_c = _dot_tn(dub, h)
        dh_c = _dot_nn(dgb, wg_ref[0]) + _dot_nn(dub, wu_ref[0])

        @pl.when(i == 0)
        def _():
            ad_ref[...] = dwd_c
            ag_ref[...] = dwg_c
            au_ref[...] = dwu_c

        @pl.when(i > 0)
        def _():
            ad_ref[...] += dwd_c
            ag_ref[...] += dwg_c
            au_ref[...] += dwu_c

        @pl.when(i == nt - 1)
        def _():
            dwd_ref[0] = ad_ref[...].astype(BF16)
            dwg_ref[0] = ag_ref[...].astype(BF16)
            dwu_ref[0] = au_ref[...].astype(BF16)

        @pl.when(c == 0)
        def _():
            dh_ref[rows, :] = dh_c

        @pl.when(c > 0)
        def _():
            dh_ref[rows, :] += dh_c

        @pl.when(c == nc - 1)
        def _():
            dx, dn = _rmsnorm_bwd(xv, gain, dh_ref[rows, :])
            dx_ref[...] = dx + dy

            @pl.when(i == 0)
            def _():
                dn_ref[...] = dn

            @pl.when(i > 0)
            def _():
                dn_ref[...] += dn

    tile = pl.BlockSpec((tm, d), lambda c, i: (i, 0))
    row = pl.BlockSpec((1, d), lambda c, i: (0, 0))
    wrow = pl.BlockSpec((1, fc, d), lambda c, i: (c, 0, 0), pipeline_mode=pl.Buffered(1))
    hidden = pl.BlockSpec((1, tm, fc), lambda c, i: (c, i, 0))
    last = pl.BlockSpec((tm, d), lambda c, i: (jnp.where(c == nc - 1, i, 0), 0))
    return pl.pallas_call(
        body, name=name, grid=(nc, nt),
        out_shape=(jax.ShapeDtypeStruct((t, d), F32), jax.ShapeDtypeStruct((1, d), F32),
                   jax.ShapeDtypeStruct(wg.shape, BF16), jax.ShapeDtypeStruct(wu.shape, BF16),
                   jax.ShapeDtypeStruct(wd.shape, BF16)),
        in_specs=[tile, row, wrow, wrow, wrow, hidden, hidden, tile, ANY],
        out_specs=(last, row, wrow, wrow, wrow),
        scratch_shapes=[pltpu.VMEM((t, d), F32)] + [pltpu.VMEM((fc, d), F32)] * 3,
        compiler_params=_params("arbitrary", "arbitrary"),
    )(x, norm, wg, wu, wd, gate, up, dy, dep)


def _store_heads(ref, v):
    for h in range(ref.shape[0]):
        ref[h] = v[:, h * HEAD_DIM:(h + 1) * HEAD_DIM]


def _load_heads(ref):
    return jnp.concatenate([ref[h] for h in range(ref.shape[0])], axis=-1)


N_HEAD_GROUPS = 3


def _proj_fwd(x, norm, w, c):
    t, d = x.shape
    nc, _, ncol = w.shape
    nh = c // HEAD_DIM
    tm = TOKEN_TILE
    wide = nc * ncol - N_HEAD_GROUPS * c

    def body(x_ref, n_ref, w_ref, q_ref, k_ref, v_ref, cur_ref):
        xv = x_ref[...]
        h = (xv * _rms(xv) * n_ref[...]).astype(BF16)
        full = jnp.concatenate([jnp.dot(h, w_ref[s], preferred_element_type=F32) for s in range(nc)], axis=1)
        for m, ref in enumerate((q_ref, k_ref, v_ref)):
            _store_heads(ref, full[:, m * c:(m + 1) * c])
        cur_ref[...] = full[:, N_HEAD_GROUPS * c:]

    heads = pl.BlockSpec((nh, tm, HEAD_DIM), lambda i: (0, i, 0))
    hshape = jax.ShapeDtypeStruct((nh, t, HEAD_DIM), F32)
    return pl.pallas_call(
        body, name="proj_fwd", grid=(t // tm,),
        out_shape=(hshape, hshape, hshape, jax.ShapeDtypeStruct((t, wide), F32)),
        in_specs=[pl.BlockSpec((tm, d), lambda i: (i, 0)), pl.BlockSpec((1, d), lambda i: (0, 0)), VMEM_FULL],
        out_specs=(heads, heads, heads, pl.BlockSpec((tm, wide), lambda i: (i, 0))),
        compiler_params=_params("arbitrary"),
    )(x, norm, w)


def _proj_bwd(x, norm, w, dq, dk, dv, dcur, dres):
    t, d = x.shape
    nc, _, ncol = w.shape
    nh = dq.shape[0]
    tm = TOKEN_TILE
    nt = t // tm
    wide = dcur.shape[1]

    def body(x_ref, n_ref, w_ref, dq_ref, dk_ref, dv_ref, dcur_ref, dres_ref, dx_ref, dn_ref, dw_ref, acc_ref):
        i = pl.program_id(0)

        @pl.when(i == 0)
        def _():
            acc_ref[...] = jnp.zeros_like(acc_ref)
            dn_ref[...] = jnp.zeros_like(dn_ref)

        xv = x_ref[...]
        gain = n_ref[...]
        h = (xv * _rms(xv) * gain).astype(BF16)
        dp = jnp.concatenate([_load_heads(dq_ref), _load_heads(dk_ref), _load_heads(dv_ref), dcur_ref[...]], axis=1).astype(BF16)
        dh = jnp.zeros((tm, d), F32)
        for s in range(nc):
            dps = dp[:, s * ncol:(s + 1) * ncol]
            acc_ref[s] += _dot_tn(h, dps)
            dh = dh + _dot_nt(dps, w_ref[s])
        dx, dn = _rmsnorm_bwd(xv, gain, dh)
        dx_ref[...] = dx + dres_ref[...]
        dn_ref[...] += dn

        @pl.when(i == nt - 1)
        def _():
            dw_ref[...] = acc_ref[...].astype(BF16)

    tile = pl.BlockSpec((tm, d), lambda i: (i, 0))
    row = pl.BlockSpec((1, d), lambda i: (0, 0))
    heads = pl.BlockSpec((nh, tm, HEAD_DIM), lambda i: (0, i, 0))
    return pl.pallas_call(
        body, name="proj_bwd", grid=(nt,),
        out_shape=(jax.ShapeDtypeStruct((t, d), F32), jax.ShapeDtypeStruct((1, d), F32),
                   jax.ShapeDtypeStruct(w.shape, BF16)),
        in_specs=[tile, row, VMEM_FULL, heads, heads, heads, pl.BlockSpec((tm, wide), lambda i: (i, 0)), tile],
        out_specs=(tile, row, VMEM_FULL),
        scratch_shapes=[pltpu.VMEM(w.shape, F32)], compiler_params=_params("arbitrary"),
    )(x, norm, w, dq, dk, dv, dcur, dres)


def _mixout_fwd(x, att, opg, gate, w):
    t, d = x.shape
    nh = att.shape[0]
    half = gate.shape[1]
    tm = TOKEN_TILE

    def body(x_ref, att_ref, opg_ref, g_ref, w_ref, o_ref):
        mix = jnp.concatenate([_load_heads(att_ref), _load_heads(opg_ref) * g_ref[...]], axis=-1).astype(BF16)
        o_ref[...] = x_ref[...] + jnp.dot(mix, w_ref[...], preferred_element_type=F32)

    tile = pl.BlockSpec((tm, d), lambda i: (i, 0))
    htile = pl.BlockSpec((tm, half), lambda i: (i, 0))
    heads = pl.BlockSpec((nh, tm, HEAD_DIM), lambda i: (0, i, 0))
    return pl.pallas_call(
        body, name="mixout_fwd", grid=(t // tm,), out_shape=jax.ShapeDtypeStruct((t, d), F32),
        in_specs=[tile, heads, heads, htile, VMEM_FULL], out_specs=tile, compiler_params=_params("arbitrary"),
    )(x, att, opg, gate, w)


def _mixout_bwd(att, opg, gate, w, dy, dep):
    nh, t, _ = att.shape
    half = gate.shape[1]
    d = dy.shape[1]
    tm = TOKEN_TILE

    def body(att_ref, opg_ref, g_ref, w_ref, dy_ref, dep_ref, datt_ref, dopg_ref, dg_ref, dw_ref):
        i = pl.program_id(0)
        opg_v, g_v = _load_heads(opg_ref), g_ref[...]
        mix = jnp.concatenate([_load_heads(att_ref), opg_v * g_v], axis=-1).astype(BF16)
        dyb = dy_ref[...].astype(BF16)
        dmix = _dot_nt(dyb, w_ref[...])
        dw = _dot_tn(mix, dyb)
        _store_heads(datt_ref, dmix[:, :half])
        drw = dmix[:, half:]
        _store_heads(dopg_ref, drw * g_v)
        dg_ref[...] = drw * opg_v

        @pl.when(i == 0)
        def _():
            dw_ref[...] = dw

        @pl.when(i > 0)
        def _():
            dw_ref[...] += dw

    tile = pl.BlockSpec((tm, d), lambda i: (i, 0))
    htile = pl.BlockSpec((tm, half), lambda i: (i, 0))
    heads = pl.BlockSpec((nh, tm, HEAD_DIM), lambda i: (0, i, 0))
    hshape = jax.ShapeDtypeStruct((nh, t, HEAD_DIM), F32)
    return pl.pallas_call(
        body, name="mixout_bwd", grid=(t // tm,),
        out_shape=(hshape, hshape, jax.ShapeDtypeStruct((t, half), F32), jax.ShapeDtypeStruct(w.shape, F32)),
        in_specs=[heads, heads, htile, VMEM_FULL, tile, ANY],
        out_specs=(heads, heads, htile, pl.BlockSpec(w.shape, lambda i: (0, 0))),
        compiler_params=_params("arbitrary"),
    )(att, opg, gate, w, dy, dep)


def _head_norm(x, gain):
    return x * _rms(x) * gain


def _att_pattern(qh, kh, v, nb):
    g, blk, _ = qh.shape
    scale = HEAD_DIM ** -0.5
    qi = lax.broadcasted_iota(jnp.int32, (blk, blk), 0)
    kj = lax.broadcasted_iota(jnp.int32, (blk, blk), 1)
    sc = jnp.where(kj <= qi, _bmm_nt(qh, kh) * scale, NEG_INF)
    top = jnp.max(sc, axis=-1, keepdims=True)
    if nb > 1:
        khp = jnp.concatenate([kh[:1], kh[:-1]], axis=0)
        vp = jnp.concatenate([v[:1], v[:-1]], axis=0)
        has_prev = lax.broadcasted_iota(jnp.int32, (g, 1, 1), 0) % nb != 0
        sp = jnp.where((kj >= qi) & has_prev, _bmm_nt(qh, khp) * scale, NEG_INF)
        top = jnp.maximum(top, jnp.max(sp, axis=-1, keepdims=True))
    m = lax.stop_gradient(top)
    pc = jnp.exp(sc - m)
    den = jnp.sum(pc, axis=-1, keepdims=True)
    acc = _bmm_nn(pc, v)
    if nb > 1:
        pp = jnp.exp(sp - m)
        den = den + jnp.sum(pp, axis=-1, keepdims=True)
        acc = acc + _bmm_nn(pp, vp)
    o = acc / den
    return o, jnp.broadcast_to(m + jnp.log(den), o.shape)


def _pattern_rows(t, dil):
    length = t // dil
    return [pl.ds(r, length, stride=dil) if dil > 1 else pl.ds(0, length) for r in range(dil)], length // ATT_BLOCK


def _take(ref, rows, nb):
    return jnp.concatenate([ref[0, r, :].reshape(nb, ATT_BLOCK, HEAD_DIM) for r in rows], axis=0)


def _put(ref, rows, nb, val):
    for j, r in enumerate(rows):
        ref[0, r, :] = val[j * nb:(j + 1) * nb].reshape(nb * ATT_BLOCK, HEAD_DIM)


def _put_add(ref, rows, nb, val):
    for j, r in enumerate(rows):
        ref[0, r, :] += val[j * nb:(j + 1) * nb].reshape(nb * ATT_BLOCK, HEAD_DIM)


def _merge_fn(o1, o2, o3, l1, l2, l3):
    m = lax.stop_gradient(jnp.maximum(jnp.maximum(l1, l2), l3))
    e1, e2, e3 = jnp.exp(l1 - m), jnp.exp(l2 - m), jnp.exp(l3 - m)
    return (e1 * o1 + e2 * o2 + e3 * o3) / (e1 + e2 + e3)


def _token_rows(j):
    return pl.ds(pl.multiple_of(j * ATT_BLOCK, ATT_BLOCK), ATT_BLOCK)


def _norm_rows(t, q_ref, k_ref, gq, gk, qh_ref, kh_ref):
    def step(j, carry):
        rows = _token_rows(j)
        qh_ref[0, rows, :] = _head_norm(q_ref[0, rows, :], gq[0])
        kh_ref[0, rows, :] = _head_norm(k_ref[0, rows, :], gk[0])
        return carry

    lax.fori_loop(0, t // ATT_BLOCK, step, 0)


def _att_head_specs(t):
    head = pl.BlockSpec((1, t, HEAD_DIM), lambda h: (h, 0, 0))
    gain = pl.BlockSpec((1, 1, HEAD_DIM), lambda h: (0, 0, 0))
    return head, gain


def _att_fwd(q, k, v, qn, kn):
    nh, t, dh = q.shape
    head, gain = _att_head_specs(t)

    def body(q_ref, k_ref, v_ref, qn_ref, kn_ref, att_ref, o1, o2, o3, l1, l2, l3, qh_ref, kh_ref):
        saved = (o1, o2, o3, l1, l2, l3)
        _norm_rows(t, q_ref, k_ref, qn_ref[...], kn_ref[...], qh_ref, kh_ref)
        for p, dil in enumerate(DILATIONS):
            rows, nb = _pattern_rows(t, dil)
            o, lse = _att_pattern(_take(qh_ref, rows, nb), _take(kh_ref, rows, nb), _take(v_ref, rows, nb), nb)
            _put(saved[p], rows, nb, o)
            _put(saved[3 + p], rows, nb, lse)

        def merge(j, carry):
            rows = _token_rows(j)
            att_ref[0, rows, :] = _merge_fn(*[r[0, rows, :] for r in saved])
            return carry

        lax.fori_loop(0, t // ATT_BLOCK, merge, 0)

    return pl.pallas_call(
        body, name="att_fwd", grid=(nh,), out_shape=(jax.ShapeDtypeStruct(q.shape, F32),) * 7,
        in_specs=[head, head, head, gain, gain], out_specs=(head,) * 7,
        scratch_shapes=[pltpu.VMEM((1, t, dh), F32)] * 2, compiler_params=_params("arbitrary"),
    )(q, k, v, qn, kn)


def _att_bwd(q, k, v, qn, kn, saved, datt):
    nh, t, dh = q.shape
    head, gain = _att_head_specs(t)

    def body(q_ref, k_ref, v_ref, qn_ref, kn_ref, o1, o2, o3, l1, l2, l3, datt_ref,
             dq_ref, dk_ref, dv_ref, dqn_ref, dkn_ref, qh_ref, kh_ref, dqh_ref, dkh_ref, *ct_refs):
        for ref in (dqh_ref, dkh_ref, dv_ref):
            ref[...] = jnp.zeros_like(ref)

        @pl.when(pl.program_id(0) == 0)
        def _():
            dqn_ref[...] = jnp.zeros_like(dqn_ref)
            dkn_ref[...] = jnp.zeros_like(dkn_ref)

        gq, gk = qn_ref[...], kn_ref[...]
        _norm_rows(t, q_ref, k_ref, gq, gk, qh_ref, kh_ref)

        def merge_cotangents(j, carry):
            rows = _token_rows(j)
            _, merge_vjp = jax.vjp(_merge_fn, *[r[0, rows, :] for r in (o1, o2, o3, l1, l2, l3)])
            for ref, val in zip(ct_refs, merge_vjp(datt_ref[0, rows, :])):
                ref[0, rows, :] = val
            return carry

        lax.fori_loop(0, t // ATT_BLOCK, merge_cotangents, 0)

        for p, dil in enumerate(DILATIONS):
            rows, nb = _pattern_rows(t, dil)
            _, pattern_vjp = jax.vjp(functools.partial(_att_pattern, nb=nb), _take(qh_ref, rows, nb), _take(kh_ref, rows, nb),
                                     _take(v_ref, rows, nb))
            dqh, dkh, dv = pattern_vjp((_take(ct_refs[p], rows, nb), _take(ct_refs[3 + p], rows, nb)))
            _put_add(dqh_ref, rows, nb, dqh)
            _put_add(dkh_ref, rows, nb, dkh)
            _put_add(dv_ref, rows, nb, dv)

        def norm_cotangents(j, carry):
            rows = _token_rows(j)
            out = []
            for x_ref, gain, dh_ref, dx_ref, acc in ((q_ref, gq, dqh_ref, dq_ref, carry[0]), (k_ref, gk, dkh_ref, dk_ref, carry[1])):
                _, norm_vjp = jax.vjp(_head_norm, x_ref[0, rows, :], gain[0])
                dx, dgain = norm_vjp(dh_ref[0, rows, :])
                dx_ref[0, rows, :] = dx
                out.append(acc + dgain)
            return tuple(out)

        zero = jnp.zeros((1, dh), F32)
        dgq, dgk = lax.fori_loop(0, t // ATT_BLOCK, norm_cotangents, (zero, zero))
        dqn_ref[0] += dgq
        dkn_ref[0] += dgk

    hshape = jax.ShapeDtypeStruct(q.shape, F32)
    gshape = jax.ShapeDtypeStruct((1, 1, dh), F32)
    return pl.pallas_call(
        body, name="att_bwd", grid=(nh,), out_shape=(hshape, hshape, hshape, gshape, gshape),
        in_specs=[head, head, head, gain, gain] + [head] * 7, out_specs=(head, head, head, gain, gain),
        scratch_shapes=[pltpu.VMEM((1, t, dh), F32)] * 10, compiler_params=_params("arbitrary"),
    )(q, k, v, qn, kn, *saved, datt)


RWKV_VEC = ("mu_r", "mu_k", "mu_v", "mu_w", "mu_a", "mu_g", "w0", "a0", "k_k", "k_a")
RWKV_MAT = ("w1", "w2", "a1", "a2", "g1", "g2")


def _rwkv_pre_fn(cur, prev, vec, w1t, w2, a1t, a2, g1t, g2):
    c = cur.shape[1] // 4
    mu_r, mu_k, mu_v, mu_w, mu_a, mu_g, w0, a0, k_k, k_a = (vec[j:j + 1] for j in range(10))

    def lerp(j, mu):
        xc, xp = cur[:, j * c:(j + 1) * c], prev[:, j * c:(j + 1) * c]
        return xc + (xp - xc) * mu

    r, k, v = lerp(0, mu_r), lerp(1, mu_k), lerp(2, mu_v)
    cw, ca, cg = lerp(3, mu_w), lerp(3, mu_a), lerp(3, mu_g)
    z = w0 + _mm(jnp.tanh(_mm_nt(cw, w1t)), w2)
    w_log = jnp.minimum(z, 0.0) - jnp.log(1.0 + jnp.exp(-jnp.abs(z))) - 0.5
    lw = -jnp.exp(w_log)
    a = _sigmoid(a0 + _mm(_mm_nt(ca, a1t), a2))
    gate = _mm(_sigmoid(_mm_nt(cg, g1t)), g2)
    kkraw = k * k_k
    kmod = k * (1.0 + (a - 1.0) * k_a)
    return r, lw, kmod, v, kkraw, a, gate


HALO_ROWS = 8


def _rwkv_pre_specs(c, mats, tile_of):
    tm = TOKEN_TILE
    nh = c // HEAD_DIM
    wide = pl.BlockSpec((tm, 4 * c), lambda j: (tile_of(j), 0))
    halo = pl.BlockSpec((HALO_ROWS, 4 * c), lambda j: (jnp.maximum(tile_of(j) * (tm // HALO_ROWS) - 1, 0), 0))
    one = pl.BlockSpec((tm, c), lambda j: (tile_of(j), 0))
    heads = pl.BlockSpec((nh, tm, HEAD_DIM), lambda j: (0, tile_of(j), 0))
    vec = pl.BlockSpec((10, c), lambda j: (0, 0))
    mspecs = [pl.BlockSpec(m.shape, lambda j: (0, 0)) for m in mats]
    return wide, halo, one, heads, vec, mspecs


def _previous_rows(cur, halo, tile):
    first = jnp.where(tile > 0, halo[HALO_ROWS - 1:HALO_ROWS], 0.0)
    rows = lax.broadcasted_iota(jnp.int32, cur.shape, 0)
    return jnp.where(rows == 0, first, pltpu.roll(cur, 1, axis=0))


def _rwkv_pre_fwd(cur, vec, mats):
    t, c4 = cur.shape
    c = c4 // 4
    wide, halo, one, heads, vspec, mspecs = _rwkv_pre_specs(c, mats, lambda j: j)

    def body(cur_ref, halo_ref, vec_ref, *rest):
        mrefs, outs = rest[:6], rest[6:]
        cur_v = cur_ref[...]
        prev = _previous_rows(cur_v, halo_ref[...], pl.program_id(0))
        vals = _rwkv_pre_fn(cur_v, prev, vec_ref[...], *(m[...] for m in mrefs))
        for ref, val in zip(outs[:6], vals[:6]):
            _store_heads(ref, val)
        outs[6][...] = vals[6]

    hshape = jax.ShapeDtypeStruct((c // HEAD_DIM, t, HEAD_DIM), F32)
    return pl.pallas_call(
        body, name="rwkv_pre_fwd", grid=(t // TOKEN_TILE,), out_shape=(hshape,) * 6 + (jax.ShapeDtypeStruct((t, c), F32),),
        in_specs=[wide, halo, vspec] + mspecs, out_specs=(heads,) * 6 + (one,), compiler_params=_params("arbitrary"),
    )(cur, cur, vec, *mats)


def _rwkv_pre_bwd(cur, vec, mats, cts, dgate):
    t, c4 = cur.shape
    c = c4 // 4
    tm = TOKEN_TILE
    nt = t // tm
    wide, halo, one, heads, vspec, mspecs = _rwkv_pre_specs(c, mats, lambda j: nt - 1 - j)

    def body(cur_ref, halo_ref, vec_ref, *rest):
        mrefs, ctrefs, dgate_ref, outs, carry_ref = rest[:6], rest[6:12], rest[12], rest[13:-1], rest[-1]
        j = pl.program_id(0)

        @pl.when(j == 0)
        def _():
            carry_ref[...] = jnp.zeros_like(carry_ref)
            for ref in outs[1:]:
                ref[...] = jnp.zeros_like(ref)

        cur_v = cur_ref[...]
        prev = _previous_rows(cur_v, halo_ref[...], nt - 1 - j)
        _, vjp = jax.vjp(_rwkv_pre_fn, cur_v, prev, vec_ref[...], *(m[...] for m in mrefs))
        grads = vjp(tuple(_load_heads(r) for r in ctrefs) + (dgate_ref[...],))
        dprev = grads[1]
        rows = lax.broadcasted_iota(jnp.int32, dprev.shape, 0)
        outs[0][...] = grads[0] + jnp.where(rows == tm - 1, carry_ref[0:1], pltpu.roll(dprev, tm - 1, axis=0))
        carry_ref[0:1] = dprev[0:1]
        for ref, val in zip(outs[1:], grads[2:]):
            ref[...] += val

    return pl.pallas_call(
        body, name="rwkv_pre_bwd", grid=(nt,),
        out_shape=(jax.ShapeDtypeStruct(cur.shape, F32), jax.ShapeDtypeStruct(vec.shape, F32))
        + tuple(jax.ShapeDtypeStruct(m.shape, F32) for m in mats),
        in_specs=[wide, halo, vspec] + mspecs + [heads] * 6 + [one], out_specs=(wide, vspec) + tuple(mspecs),
        scratch_shapes=[pltpu.VMEM((HALO_ROWS, c4), F32)], compiler_params=_params("arbitrary"),
    )(cur, cur, vec, *mats, *cts, dgate)


def _scan_chunk_fn(h0, r, lw, k, v, kkraw, a, rk, lnw, lnb):
    n = r.shape[1]
    nrm = jnp.sqrt(jnp.sum(kkraw * kkraw, axis=-1, keepdims=True))
    kk = kkraw / jnp.maximum(nrm, 1e-12)
    av, bv = -kk, kk * a
    ti = lax.broadcasted_iota(jnp.int32, (n, n), 0)
    si = lax.broadcasted_iota(jnp.int32, (n, n), 1)
    incl, strict = ti >= si, ti > si
    ones = jnp.broadcast_to(incl.astype(F32)[None], (r.shape[0], n, n))
    cum = _hdot(ones, lw, 2, 1)
    at, rt = av * jnp.exp(cum - lw), r * jnp.exp(cum)
    inv = jnp.exp(-cum)
    bt, kt = bv * inv, k * inv
    gram = _hdot(jnp.concatenate([at, rt], axis=1), jnp.concatenate([bt, kt], axis=1), 2, 2)
    lab = jnp.where(strict, gram[:, :n, :n], 0.0)
    lak = jnp.where(strict, gram[:, :n, n:], 0.0)
    rb = jnp.where(incl, gram[:, n:, :n], 0.0)
    rkm = jnp.where(incl, gram[:, n:, n:], 0.0)
    nv = v.shape[2]
    u = _bmm_nn(jnp.concatenate([at, lak], axis=2), jnp.concatenate([h0, v], axis=1))
    p = lab
    m = 2
    while m < n:
        both = _bmm_nn(p, jnp.concatenate([u, p], axis=2))
        u, p = u + both[:, :, :nv], both[:, :, nv:]
        m *= 2
    u = u + _bmm_nn(p, u)
    y = _bmm_nn(jnp.concatenate([rt, rb, rkm], axis=2), jnp.concatenate([h0, u, v], axis=1))
    last = jnp.exp(jnp.sum(lw, axis=1, keepdims=True))
    h1 = jnp.swapaxes(last, 1, 2) * (h0 + _bmm_tn(jnp.concatenate([bt, kt], axis=1), jnp.concatenate([u, v], axis=1)))
    mean = jnp.mean(y, axis=-1, keepdims=True)
    yc = y - mean
    var = jnp.mean(yc * yc, axis=-1, keepdims=True)
    yn = yc * lax.rsqrt(var + GN_EPS) * lnw + lnb
    bonus = jnp.sum(r * k * rk, axis=-1, keepdims=True) * v
    return yn + bonus, h1


SCAN_GROUP = 2


def _scan_group_fn(h0, r, lw, k, v, kkraw, a, rk, lnw, lnb):
    outs = []
    for j in range(SCAN_GROUP):
        rows = slice(j * SCAN_CHUNK, (j + 1) * SCAN_CHUNK)
        o, h0 = _scan_chunk_fn(h0, r[:, rows], lw[:, rows], k[:, rows], v[:, rows], kkraw[:, rows], a[:, rows], rk, lnw, lnb)
        outs.append(o)
    return jnp.concatenate(outs, axis=1), h0


def _scan_specs(h, t, dh, rev):
    n = SCAN_CHUNK * SCAN_GROUP
    nc = t // n
    pos = (lambda c: (0, nc - 1 - c, 0)) if rev else (lambda c: (0, c, 0))
    st = (lambda c: (nc - 1 - c, 0, 0, 0)) if rev else (lambda c: (c, 0, 0, 0))
    seq = pl.BlockSpec((h, n, dh), pos)
    par = pl.BlockSpec((h, 1, dh), lambda c: (0, 0, 0))
    state = pl.BlockSpec((1, h, dh, dh), st)
    return seq, par, state


def _scan_fwd(seqs, pars):
    h, t, dh = seqs[0].shape
    nc = t // (SCAN_CHUNK * SCAN_GROUP)
    seq, par, state = _scan_specs(h, t, dh, False)

    def body(r, lw, k, v, kkraw, a, rk, lnw, lnb, o_ref, st_ref, h_ref):
        @pl.when(pl.program_id(0) == 0)
        def _():
            h_ref[...] = jnp.zeros_like(h_ref)

        h0 = h_ref[...]
        st_ref[0] = h0
        o, h1 = _scan_group_fn(h0, r[...], lw[...], k[...], v[...], kkraw[...], a[...], rk[...], lnw[...], lnb[...])
        o_ref[...] = o
        h_ref[...] = h1

    return pl.pallas_call(
        body, name="rwkv_scan_fwd", grid=(nc,),
        out_shape=(jax.ShapeDtypeStruct((h, t, dh), F32), jax.ShapeDtypeStruct((nc, h, dh, dh), F32)),
        in_specs=[seq] * 6 + [par] * 3, out_specs=(seq, state),
        scratch_shapes=[pltpu.VMEM((h, dh, dh), F32)], compiler_params=_params("arbitrary"),
    )(*seqs, *pars)


def _scan_bwd(seqs, pars, states, do):
    h, t, dh = seqs[0].shape
    nc = t // (SCAN_CHUNK * SCAN_GROUP)
    seq, par, state = _scan_specs(h, t, dh, True)

    def body(r, lw, k, v, kkraw, a, rk, lnw, lnb, st_ref, do_ref, *rest):
        douts, dpars, dh_ref = rest[:6], rest[6:9], rest[9]
        first = pl.program_id(0) == 0

        @pl.when(first)
        def _():
            dh_ref[...] = jnp.zeros_like(dh_ref)

        _, vjp = jax.vjp(_scan_group_fn, st_ref[0], r[...], lw[...], k[...], v[...], kkraw[...], a[...],
                         rk[...], lnw[...], lnb[...])
        grads = vjp((do_ref[...], dh_ref[...]))
        dh_ref[...] = grads[0]
        for ref, val in zip(douts, grads[1:7]):
            ref[...] = val

        @pl.when(first)
        def _():
            for ref, val in zip(dpars, grads[7:]):
                ref[...] = val

        @pl.when(jnp.logical_not(first))
        def _():
            for ref, val in zip(dpars, grads[7:]):
                ref[...] += val

    sshape = jax.ShapeDtypeStruct((h, t, dh), F32)
    pshape = jax.ShapeDtypeStruct((h, 1, dh), F32)
    return pl.pallas_call(
        body, name="rwkv_scan_bwd", grid=(nc,), out_shape=(sshape,) * 6 + (pshape,) * 3,
        in_specs=[seq] * 6 + [par] * 3 + [state, seq], out_specs=(seq,) * 6 + (par,) * 3,
        scratch_shapes=[pltpu.VMEM((h, dh, dh), F32)], compiler_params=_params("arbitrary"),
    )(*seqs, *pars, states, do)


def _local_step(x, target, w, ex):
    w = dict(w)
    c = w["mu_r"].shape[-1]
    qn, kn = w["q_norm"].reshape(1, 1, HEAD_DIM), w["k_norm"].reshape(1, 1, HEAD_DIM)
    vec = jnp.concatenate([w[n].reshape(1, c) for n in RWKV_VEC], axis=0)
    pars = [w[n].reshape(-1, 1, HEAD_DIM) for n in ("r_k", "ln_x_w", "ln_x_b")]
    no_dep = jnp.zeros(DEP_SHAPE, F32)

    x1, gate1, up1 = _ffn_fwd(x, w["ffn1_norm"], w["ffn1_w_gate"], w["ffn1_w_up"], w["ffn1_w_down"], ex.first_dep, "ffn1_fwd")
    w.update(ex.mix_weights((x1,)))
    mats = [w[n] for n in RWKV_MAT]
    q, k, v, cur = _proj_fwd(x1, w["mix_norm"], w["w_in"], c)
    att, *saved = _att_fwd(q, k, v, qn, kn)
    pre = _rwkv_pre_fwd(cur, vec, mats)
    seqs, gate = pre[:6], pre[6]
    opg, states = _scan_fwd(seqs, pars)
    w.update(ex.out_weights((att, opg)))
    x2 = _mixout_fwd(x1, att, opg, gate, w["w_out"])
    dy, gate2, up2, loss = _ffn_fwd(x2, w["ffn2_norm"], w["ffn2_w_gate"], w["ffn2_w_up"], w["ffn2_w_down"], no_dep, "ffn2_fwd",
                                    target=target)

    g = {}
    dx2, g["ffn2_norm"], g["ffn2_w_gate"], g["ffn2_w_up"], g["ffn2_w_down"] = _ffn_bwd(
        x2, w["ffn2_norm"], w["ffn2_w_gate"], w["ffn2_w_up"], w["ffn2_w_down"], gate2, up2, dy, no_dep, "ffn2_bwd")
    dep = ex.send_ffn2({n: g[n] for n in ("ffn2_w_gate", "ffn2_w_up", "ffn2_w_down")})
    datt, dopg, dgate, g["w_out"] = _mixout_bwd(att, opg, gate, w["w_out"], dx2, dep)
    dscan = _scan_bwd(seqs, pars, states, dopg)
    for n, d in zip(("r_k", "ln_x_w", "ln_x_b"), dscan[6:]):
        g[n] = d
    dcur, dvec, *dmats = _rwkv_pre_bwd(cur, vec, mats, dscan[:6], dgate)
    for n, d in zip(RWKV_MAT, dmats):
        g[n] = d
    g["rwkv_vec"] = dvec
    dq, dk, dv, g["q_norm"], g["k_norm"] = _att_bwd(q, k, v, qn, kn, saved, datt)
    dx1, g["mix_norm"], g["w_in"] = _proj_bwd(x1, w["mix_norm"], w["w_in"], dq, dk, dv, dcur, dx2)
    dep = ex.send_mix({n: g[n] for n in ("w_in", "w_out") + RWKV_MAT}, (dx1,))
    dx, g["ffn1_norm"], g["ffn1_w_gate"], g["ffn1_w_up"], g["ffn1_w_down"] = _ffn_bwd(
        x, w["ffn1_norm"], w["ffn1_w_gate"], w["ffn1_w_up"], w["ffn1_w_down"], gate1, up1, dx1, dep, "ffn1_bwd")
    return loss, dx, g


N_SHARDS = 4


def _place():
    return lax.axis_index("x"), lax.axis_index("y"), lax.axis_index("c")


def _chip_peers(x, y):
    return [(1 - x, y), (x, 1 - y), (1 - x, 1 - y)]


HBM = pl.BlockSpec(memory_space=pltpu.HBM)
SEM = pl.BlockSpec(memory_space=pltpu.SEMAPHORE)
DEP_SHAPE = (8, 128)


class _Views:
    to_sibling = False


class _GatherViews(_Views):
    @staticmethod
    def send(i, srcs, lands, k, at):
        return srcs[i], lands[i].at[at[3]]

    @staticmethod
    def landing(i, srcs, lands, k, at):
        return srcs[i], lands[i].at[2 * at[4] + at[5]]


class _ScatterViews(_Views):
    @staticmethod
    def send(i, srcs, lands, k, at):
        return srcs[i].at[2 * at[4] + at[5]], lands[i].at[k]

    @staticmethod
    def landing(i, srcs, lands, k, at):
        return srcs[i].at[at[3]], lands[i].at[k]


def _half_rows(ref, slot, half):
    rows = ref.shape[1] // 2
    return ref.at[slot, pl.ds(pl.multiple_of(half * rows, BF16_SUBLANES), rows)]


class _HalfGatherViews(_Views):
    @staticmethod
    def send(i, srcs, lands, k, at):
        rows = srcs[i].shape[0] // 2
        return srcs[i].at[pl.ds(pl.multiple_of(at[2] * rows, BF16_SUBLANES), rows)], _half_rows(lands[i], at[3], at[2])

    @staticmethod
    def landing(i, srcs, lands, k, at):
        rows = srcs[i].shape[0] // 2
        return srcs[i].at[pl.ds(pl.multiple_of(at[2] * rows, BF16_SUBLANES), rows)], _half_rows(lands[i], 2 * at[4] + at[5], at[2])


class _ForwardViews(_Views):
    to_sibling = True

    @staticmethod
    def send(i, srcs, lands, k, at):
        mine = _half_rows(lands[i], 2 * at[4] + at[5], at[2])
        return mine, mine

    @staticmethod
    def landing(i, srcs, lands, k, at):
        theirs = _half_rows(lands[i], 2 * at[4] + at[5], 1 - at[2])
        return theirs, theirs


class _SiblingViews(_Views):
    to_sibling = True

    @staticmethod
    def _block(ref, k):
        size = -(-ref.shape[0] // 3 // BF16_SUBLANES) * BF16_SUBLANES
        return ref.at[pl.ds(k * size, min(size, ref.shape[0] - k * size))]

    @classmethod
    def send(cls, i, srcs, lands, k, at):
        return cls._block(srcs[i], k), cls._block(lands[i], k)

    landing = send


def _push_start(srcs, lands, views, after, name):
    ns, nl = len(srcs), len(lands)

    def body(*refs):
        src_refs, land_refs = refs[:ns], refs[ns:ns + nl]
        send_sems, recv_sems = refs[ns + nl + 1:ns + nl + 3]
        token = refs[2 * (ns + nl) + 3]
        x, y, c = _place()
        for i in range(nl):
            for k, (px, py) in enumerate(_chip_peers(x, y)):
                src, dst = views.send(i, src_refs, land_refs, k, (x, y, c, 2 * x + y, px, py))
                pltpu.make_async_remote_copy(
                    src_ref=src, dst_ref=dst, send_sem=send_sems.at[3 * i + k], recv_sem=recv_sems.at[3 * i + k],
                    device_id=(x, y, 1 - c) if views.to_sibling else (px, py, c), device_id_type=MESH).start()
        token[...] = jnp.zeros_like(token)

    sems = pltpu.SemaphoreType.DMA((3 * nl,))
    both = [pltpu.with_memory_space_constraint(a, pltpu.HBM) for a in (*srcs, *lands)]
    outs = pl.pallas_call(
        body, name=name,
        out_shape=(sems, sems, *[pltpu.HBM(a.shape, a.dtype) for a in both], jax.ShapeDtypeStruct(DEP_SHAPE, F32)),
        in_specs=[HBM] * (ns + nl) + [ANY], out_specs=(SEM, SEM, *[HBM] * (ns + nl), VMEM_FULL),
        input_output_aliases={i: 2 + i for i in range(ns + nl)},
        compiler_params=pltpu.CompilerParams(has_side_effects=pltpu.SideEffectType.DATAFLOW_SIDE_EFFECTING),
    )(*both, after)
    return outs[0], outs[1], outs[2:2 + ns], outs[2 + ns:2 + ns + nl], outs[2 + ns + nl]


def _push_wait(started, views, after, name, with_sources=False):
    send_sems, recv_sems, srcs, lands, _ = started
    ns, nl = len(srcs), len(lands)

    def body(*refs):
        src_refs, land_refs = refs[:ns], refs[ns:ns + nl]
        send_sems, recv_sems = refs[ns + nl:ns + nl + 2]
        x, y, c = _place()
        for i in range(nl):
            for k, (px, py) in enumerate(_chip_peers(x, y)):
                src, dst = views.landing(i, src_refs, land_refs, k, (x, y, c, 2 * x + y, px, py))
                landing = pltpu.make_async_remote_copy(
                    src_ref=src, dst_ref=dst, send_sem=send_sems.at[3 * i + k], recv_sem=recv_sems.at[3 * i + k],
                    device_id=(x, y, 1 - c) if views.to_sibling else (px, py, c), device_id_type=MESH)
                landing.wait_send()
                landing.wait_recv()

    outs = pl.pallas_call(
        body, name=name,
        out_shape=tuple(pltpu.HBM(a.shape, a.dtype) for a in (*srcs, *lands)),
        in_specs=[HBM] * (ns + nl) + [SEM, SEM] + [ANY] * len(after), out_specs=(HBM,) * (ns + nl),
        input_output_aliases={i: i for i in range(ns + nl)},
        compiler_params=pltpu.CompilerParams(has_side_effects=pltpu.SideEffectType.DATAFLOW_SIDE_EFFECTING),
    )(*srcs, *lands, send_sems, recv_sems, *after)
    return outs if with_sources else outs[ns:]


def _empty_lands(shards, slots, own_slot):
    lands = [lax.empty((slots,) + s.shape, s.dtype) for s in shards]
    if own_slot:
        me = 2 * lax.axis_index("x") + lax.axis_index("y")
        lands = [lax.dynamic_update_index_in_dim(z, s, me, 0) for z, s in zip(lands, shards)]
    return lands


def _sibling_swap(arrays, name, other_half=False):
    n = len(arrays)

    def body(*refs):
        ins, outs = refs[:n], refs[n:2 * n]
        send_sems, recv_sems = refs[2 * n:]
        x, y, c = _place()
        copies = []
        for i in range(n):
            src = ins[i]
            if other_half:
                rows = src.shape[1] // 2
                src = src.at[:, pl.ds(pl.multiple_of((1 - c) * rows, BF16_SUBLANES), rows)]
            cp = pltpu.make_async_remote_copy(
                src_ref=src, dst_ref=outs[i], send_sem=send_sems.at[i], recv_sem=recv_sems.at[i],
                device_id=(x, y, 1 - c), device_id_type=MESH)
            cp.start()
            copies.append(cp)
        for cp in copies:
            cp.wait()

    shapes = [(a.shape[0], a.shape[1] // 2, a.shape[2]) if other_half else a.shape for a in arrays]
    return pl.pallas_call(
        body, name=name,
        out_shape=tuple(jax.ShapeDtypeStruct(s, a.dtype) for s, a in zip(shapes, arrays)),
        in_specs=[ANY] * n, out_specs=(ANY,) * n,
        scratch_shapes=[pltpu.SemaphoreType.DMA((n,)), pltpu.SemaphoreType.DMA((n,))],
    )(*arrays)


def _sibling_fill(arrays, name):
    n = len(arrays)

    def body(*refs):
        outs = refs[n:2 * n]
        send_sems, recv_sems = refs[2 * n:]
        x, y, c = _place()
        copies = []
        for i in range(n):
            rows = outs[i].shape[0] // 2
            mine = outs[i].at[pl.ds(pl.multiple_of(c * rows, BF16_SUBLANES), rows)]
            cp = pltpu.make_async_remote_copy(
                src_ref=mine, dst_ref=mine, send_sem=send_sems.at[i], recv_sem=recv_sems.at[i],
                device_id=(x, y, 1 - c), device_id_type=MESH)
            cp.start()
            copies.append(cp)
        for cp in copies:
            cp.wait()

    return pl.pallas_call(
        body, name=name, out_shape=tuple(jax.ShapeDtypeStruct(a.shape, a.dtype) for a in arrays),
        in_specs=[ANY] * n, out_specs=(ANY,) * n, input_output_aliases={i: i for i in range(n)},
        scratch_shapes=[pltpu.SemaphoreType.DMA((n,)), pltpu.SemaphoreType.DMA((n,))],
    )(*arrays)


FOLD_STEPS = 2


def _fold_add(core, parts, theirs, name):
    n = len(parts)
    s, r, cols = parts[0].shape
    tr = r // 2 // FOLD_STEPS

    def body(core_ref, *refs):
        for p_ref, t_ref, o_ref in zip(refs[:n], refs[n:2 * n], refs[2 * n:]):
            o_ref[...] = (p_ref[...].astype(F32) + t_ref[...].astype(F32)).astype(BF16)

    half = pl.BlockSpec((1, tr, cols), lambda j, i, core_ref: (j, i, 0))
    return pl.pallas_call(
        body, name=name, out_shape=tuple(jax.ShapeDtypeStruct((s, r // 2, cols), BF16) for _ in parts),
        grid_spec=pltpu.PrefetchScalarGridSpec(
            num_scalar_prefetch=1, grid=(s, FOLD_STEPS),
            in_specs=[pl.BlockSpec((1, tr, cols), lambda j, i, core_ref: (j, core_ref[0] * FOLD_STEPS + i, 0))] * n + [half] * n,
            out_specs=(half,) * n),
        compiler_params=_params("arbitrary", "arbitrary"),
    )(core, *parts, *theirs)


N_DEV = 8


PACK_COLS = 1024
PACK_ROWS = 24


def _put_row(pack_ref, row, ref):
    if len(ref.shape) == 2:
        pack_ref[row:row + 1, :ref.shape[1]] = ref[...]
    else:
        for h in range(ref.shape[0]):
            pack_ref[row:row + 1, h * HEAD_DIM:(h + 1) * HEAD_DIM] = ref[h]


def _allreduce_small(grads, rows):
    n = len(grads)

    def body(*refs):
        in_ref, out_ref, buf, send_sems, recv_sems = refs[n + 1], refs[n], *refs[n + 2:]
        in_ref[...] = jnp.zeros_like(in_ref)
        for ref, row in zip(refs[:n], rows):
            if len(ref.shape) == 2 and ref.shape[0] > 1:
                in_ref[row:row + ref.shape[0], :ref.shape[1]] = ref[...]
            else:
                _put_row(in_ref, row, ref)
        x, y, c = _place()
        me = 4 * x + 2 * y + c
        buf[me] = in_ref[...]

        def copy(j, slot):
            px, py, pc = x ^ (j >> 2), y ^ ((j >> 1) & 1), c ^ (j & 1)
            return pltpu.make_async_remote_copy(
                src_ref=in_ref, dst_ref=buf.at[slot(px, py, pc)], send_sem=send_sems.at[j], recv_sem=recv_sems.at[j],
                device_id=(px, py, pc), device_id_type=MESH)

        for j in range(1, N_DEV):
            copy(j, lambda px, py, pc: me).start()
        for j in range(1, N_DEV):
            landing = copy(j, lambda px, py, pc: 4 * px + 2 * py + pc)
            landing.wait_send()
            landing.wait_recv()
        acc = buf[0]
        for s in range(1, N_DEV):
            acc = acc + buf[s]
        out_ref[...] = acc

    shape = (PACK_ROWS, PACK_COLS)
    return pl.pallas_call(
        body, name="allreduce_small", out_shape=jax.ShapeDtypeStruct(shape, F32),
        in_specs=[VMEM_FULL] * n, out_specs=VMEM_FULL,
        scratch_shapes=[pltpu.VMEM(shape, F32), pltpu.VMEM((N_DEV,) + shape, F32), pltpu.SemaphoreType.DMA((N_DEV,)),
                        pltpu.SemaphoreType.DMA((N_DEV,))],
    )(*grads)


BF16_SUBLANES = 16


def _reduce_own(me, parts, recvs, dep, steps, name, half=None):
    n = len(parts)
    where = me if half is None else jnp.concatenate([me, half])
    offset = (lambda w: 0) if half is None else (lambda w: w[1] * steps)

    def body(where_ref, *refs):
        for p_ref, rv_ref, o_ref in zip(refs[:n], refs[n:2 * n], refs[2 * n + 1:]):
            acc = p_ref[0].astype(F32)
            for k in range(3):
                acc = acc + rv_ref[k].astype(F32)
            o_ref[...] = acc

    shapes = [(p.shape[1] // steps, p.shape[2]) for p in parts]
    rows = 1 if half is None else 2
    return pl.pallas_call(
        body, name=name, out_shape=tuple(jax.ShapeDtypeStruct((rows * p.shape[1], p.shape[2]), F32) for p in parts),
        grid_spec=pltpu.PrefetchScalarGridSpec(
            num_scalar_prefetch=1, grid=(steps,),
            in_specs=[pl.BlockSpec((1, tr, c), lambda i, w: (w[0], i, 0)) for tr, c in shapes]
            + [pl.BlockSpec((3, tr, c), lambda i, w: (0, i, 0)) for tr, c in shapes] + [ANY],
            out_specs=tuple(pl.BlockSpec((tr, c), lambda i, w: (offset(w) + i, 0)) for tr, c in shapes)),
        compiler_params=_params("arbitrary"),
    )(where, *parts, *recvs, dep)


def _adamw_step(w, g, m, v):
    mn = ADAM_B1 * m + (1.0 - ADAM_B1) * g
    vn = ADAM_B2 * v + (1.0 - ADAM_B2) * (g * g)
    m_hat = mn / (1.0 - ADAM_B1 ** ADAM_STEP)
    v_hat = vn / (1.0 - ADAM_B2 ** ADAM_STEP)
    return -ADAM_LR * (m_hat / (jnp.sqrt(v_hat) + ADAM_EPS) + ADAM_WD * w), mn, vn


def _adamw(ws, gas, gbs, ms, vs, steps, name):
    n = len(ws)
    operands = [ws, gas, ms, vs] if gbs is None else [ws, gas, gbs, ms, vs]
    k = len(operands)

    def body(*refs):
        ins, outs = refs[:k * n], refs[k * n:]
        for j in range(n):
            w_ref, ga_ref, *gb_ref, m_ref, v_ref = ins[j::n]
            g_out, d_out, m_out, v_out = outs[j::n]
            g = ga_ref[...] + gb_ref[0][...] if gb_ref else ga_ref[...]
            g_out[...] = g
            d_out[...], m_out[...], v_out[...] = _adamw_step(w_ref[...], g, m_ref[...], v_ref[...])

    tiles = [pl.BlockSpec((w.shape[0] // steps, w.shape[1]), lambda i: (i, 0)) for w in ws]
    shapes = [jax.ShapeDtypeStruct(w.shape, F32) for w in ws]
    outs = pl.pallas_call(
        body, name=name, grid=(steps,), out_shape=tuple(shapes * 4), in_specs=tiles * k, out_specs=tuple(tiles * 4),
        compiler_params=_params("arbitrary"),
    )(*[a for group in operands for a in group])
    return [outs[j::n] for j in range(n)]


def _adamw_replicated(gsum, ws, ms, vs):
    n = len(ws)

    def body(g_ref, *refs):
        ins, outs = refs[:3 * n], refs[3 * n:]
        for i in range(n):
            w_ref, m_ref, v_ref = ins[i::n]
            shape = w_ref.shape
            if len(shape) == 2:
                g = g_ref[i:i + 1, :shape[1]]
            else:
                g = jnp.concatenate([g_ref[i:i + 1, h * HEAD_DIM:(h + 1) * HEAD_DIM] for h in range(shape[1])], axis=0)[None]
            g_out, d_out, m_out, v_out = outs[i::n]
            g_out[...] = g
            d_out[...], m_out[...], v_out[...] = _adamw_step(w_ref[...], g, m_ref[...], v_ref[...])

    shapes = [jax.ShapeDtypeStruct(w.shape, F32) for w in ws]
    outs = pl.pallas_call(
        body, name="adamw_replicated", out_shape=tuple(shapes * 4),
        in_specs=[VMEM_FULL] * (1 + 3 * n), out_specs=(VMEM_FULL,) * (4 * n),
    )(gsum, *ws, *ms, *vs)
    return [outs[i::n] for i in range(n)]


COL_SHARDED = ("ffn1_w_gate", "ffn1_w_up", "w_in", "ffn2_w_gate", "ffn2_w_up", "w1", "w2", "a1", "a2", "g1", "g2")
ROW_SHARDED = ("ffn1_w_down", "ffn2_w_down", "w_out")
CHUNKED = ("ffn1_w_gate", "ffn1_w_up", "ffn1_w_down", "w_in", "ffn2_w_gate", "ffn2_w_up", "ffn2_w_down")
WEIGHTS = ("ffn1_norm", "ffn1_w_gate", "ffn1_w_up", "ffn1_w_down", "mix_norm", "w_in", "q_norm", "k_norm",
           "mu_r", "mu_k", "mu_v", "mu_w", "mu_a", "mu_g", "w0", "w1", "w2", "a0", "a1", "a2", "g1", "g2",
           "k_k", "k_a", "r_k", "ln_x_w", "ln_x_b", "w_out", "ffn2_norm", "ffn2_w_gate", "ffn2_w_up", "ffn2_w_down")


TRANSPOSED = ("ffn1_w_gate", "ffn1_w_up", "ffn2_w_gate", "ffn2_w_up", "w1", "a1", "g1")


def _shard_2d(name, a):
    return a[0].T if name in TRANSPOSED else a[0]


def _full_from_blocks(name, blocks):
    if name in CHUNKED:
        return blocks
    if name in ROW_SHARDED:
        return blocks.reshape(-1, blocks.shape[-1])
    return blocks.transpose(1, 0, 2).reshape(blocks.shape[1], -1)


def _blocks_from_full(name, full):
    if name in CHUNKED:
        return full
    if name in ROW_SHARDED:
        return full.reshape(N_SHARDS, -1, full.shape[-1])
    return full.reshape(full.shape[0], N_SHARDS, -1).transpose(1, 0, 2)


FFN1_GROUP = ("ffn1_w_gate", "ffn1_w_up", "ffn1_w_down")
MIX_GROUP = ("w_in",) + RWKV_MAT
OUT_GROUP = ("w_out", "ffn2_w_gate", "ffn2_w_up", "ffn2_w_down")
FFN2_GROUP = OUT_GROUP[1:]
LATE_GROUP = ("w_in", "w_out") + RWKV_MAT


class _Exchange:
    def __init__(self, given):
        self.given = given
        first = self._gather_start(FFN1_GROUP, _HalfGatherViews, jnp.zeros(DEP_SHAPE, F32), "gather_ffn1_start")
        self.mix = self._gather_start(MIX_GROUP, _GatherViews, first[4], "gather_mix_start")
        self.out = self._gather_start(OUT_GROUP, _GatherViews, self.mix[4], "gather_out_start")
        self.first_dep = self.out[4]
        halves = _push_wait(first, _HalfGatherViews, (self.first_dep,), "gather_ffn1_wait")
        passed = _push_start([], halves, _ForwardViews, jnp.zeros(DEP_SHAPE, F32), "gather_ffn1_pass_start")
        self.first_weights = self._full(FFN1_GROUP, _push_wait(passed, _ForwardViews, (passed[4],), "gather_ffn1_pass_wait"))
        self.parts, self.recv = {}, {}

    @staticmethod
    def _full(names, blocks):
        out = {}
        for n, b in zip(names, blocks):
            full = _full_from_blocks(n, b)
            out[n] = full.astype(F32) if n in RWKV_MAT else full
        return out

    def _gather_start(self, names, views, after, name):
        after, raw = lax.optimization_barrier((after, [_shard_2d(n, self.given[n]) for n in names]))
        shards = [a.astype(BF16) for a in raw]
        return _push_start(shards, _empty_lands(shards, N_SHARDS, True), views, after, name)

    def mix_weights(self, after):
        return self._full(MIX_GROUP, _push_wait(self.mix, _GatherViews, after, "gather_mix_wait"))

    def out_weights(self, after):
        return self._full(OUT_GROUP, _push_wait(self.out, _GatherViews, after, "gather_out_wait"))

    def _scatter_start(self, grads, name):
        names = tuple(grads)
        parts = [_blocks_from_full(n, grads[n]) for n in names]
        self.parts.update(zip(names, parts))
        lands = [lax.empty((3,) + p.shape[1:], BF16) for p in parts]
        return _push_start([p.astype(BF16) for p in parts], lands, _ScatterViews, jnp.zeros(DEP_SHAPE, F32), name)

    def _scatter_done(self, started, names, after, name):
        outs = _push_wait(started, _ScatterViews, after, name, with_sources=True)
        for n, sent, got in zip(names, outs[:len(names)], outs[len(names):]):
            self.recv[n] = got
            if self.parts[n].dtype == BF16:
                self.parts[n] = sent

    def send_ffn2(self, grads):
        self.ffn2 = self._scatter_start(grads, "scatter_ffn2_start")
        return self.ffn2[4]

    def send_mix(self, grads, after):
        self._scatter_done(self.ffn2, FFN2_GROUP, after, "scatter_ffn2_wait")
        self.late = self._scatter_start(grads, "scatter_late_start")
        return self.late[4]

    def send_ffn1(self, grads):
        self.ffn1 = self._scatter_start(grads, "scatter_ffn1_start")
        return self.ffn1[4]

    def late_received(self, after):
        self._scatter_done(self.late, LATE_GROUP, after, "scatter_late_wait")

    def ffn1_received(self, after):
        self._scatter_done(self.ffn1, FFN1_GROUP, after, "scatter_ffn1_wait")


def kernel(
        x, ffn1_norm, ffn1_w_gate, ffn1_w_up, ffn1_w_down, mix_norm, w_in, q_norm, k_norm, mu_r, mu_k, mu_v, mu_w,
        mu_a, mu_g, w0, w1, w2, a0, a1, a2, g1, g2, k_k, k_a, r_k, ln_x_w, ln_x_b, w_out, ffn2_norm, ffn2_w_gate,
        ffn2_w_up, ffn2_w_down, loss_target, m_ffn1_norm, m_ffn1_w_gate, m_ffn1_w_up, m_ffn1_w_down, m_mix_norm,
        m_w_in, m_q_norm, m_k_norm, m_mu_r, m_mu_k, m_mu_v, m_mu_w, m_mu_a, m_mu_g, m_w0, m_w1, m_w2, m_a0, m_a1,
        m_a2, m_g1, m_g2, m_k_k, m_k_a, m_r_k, m_ln_x_w, m_ln_x_b, m_w_out, m_ffn2_norm, m_ffn2_w_gate, m_ffn2_w_up,
        m_ffn2_w_down, v_ffn1_norm, v_ffn1_w_gate, v_ffn1_w_up, v_ffn1_w_down, v_mix_norm, v_w_in, v_q_norm, v_k_norm,
        v_mu_r, v_mu_k, v_mu_v, v_mu_w, v_mu_a, v_mu_g, v_w0, v_w1, v_w2, v_a0, v_a1, v_a2, v_g1, v_g2, v_k_k, v_k_a,
        v_r_k, v_ln_x_w, v_ln_x_b, v_w_out, v_ffn2_norm, v_ffn2_w_gate, v_ffn2_w_up, v_ffn2_w_down):
    given = dict(locals())
    sharded = COL_SHARDED + ROW_SHARDED
    sharded = tuple(n for n in WEIGHTS if n in sharded)
    small = tuple(n for n in WEIGHTS if n not in sharded)

    ex = _Exchange(given)
    w = {n: given[n] for n in small}
    w.update(ex.first_weights)
    loss, dx, g = _local_step(x[0], loss_target[0], w, ex)

    core = lax.axis_index("c").astype(jnp.int32).reshape(1)
    late = [g[n] for n in FFN1_GROUP]
    folded = _fold_add(core, late, _sibling_swap(late, "fold_swap_ffn1", other_half=True), "fold_add_ffn1")
    dep = ex.send_ffn1(dict(zip(FFN1_GROUP, folded)))

    me = (2 * lax.axis_index("x") + lax.axis_index("y")).astype(jnp.int32).reshape(1)
    out = {}

    def reduced(sub, steps, tag):
        parts = [ex.parts[n].reshape(N_SHARDS, -1, ex.parts[n].shape[-1]) for n in sub]
        recvs = [ex.recv[n].reshape(3, -1, ex.recv[n].shape[-1]) for n in sub]
        return _reduce_own(me, parts, recvs, dep, steps, f"reduce_{tag}")

    def updated(sub, mine, theirs, steps, tag):
        res = _adamw([_shard_2d(n, given[n]) for n in sub], mine, theirs, [_shard_2d(n, given["m_" + n]) for n in sub],
                     [_shard_2d(n, given["v_" + n]) for n in sub], steps, f"adamw_{tag}")
        for n, rs in zip(sub, res):
            out[n] = [(r.T if n in TRANSPOSED else r).reshape(given[n].shape) for r in rs]
        return [out[n][1] for n in sub]

    ex.late_received((dep,))
    rest = tuple(n for n in sharded if n not in FFN1_GROUP)
    large, lora = tuple(n for n in rest if n not in RWKV_MAT), tuple(n for n in rest if n in RWKV_MAT)
    mine = reduced(large, 4, "rest_large")
    swap = _push_start(mine, [lax.empty(a.shape, a.dtype) for a in mine], _SiblingViews, dep, "swap_rest_start")
    mine_lora = reduced(lora, 1, "rest_lora")
    last = updated(lora, mine_lora, _sibling_swap(mine_lora, "sibling_swap_rest_lora"), 1, "rest_lora")

    row = {n: i for i, n in enumerate(small)}
    singles = [n for n in small if n not in RWKV_VEC]
    gsum = _allreduce_small([g[n] for n in singles] + [g["rwkv_vec"], loss],
                            [row[n] for n in singles] + [row[RWKV_VEC[0]], len(small)])
    res = _adamw_replicated(gsum, [given[n] for n in small], [given["m_" + n] for n in small], [given["v_" + n] for n in small])
    for n, rs in zip(small, res):
        out[n] = list(rs)
    total_loss = gsum[len(small), 0]

    both = _push_wait(swap, _SiblingViews, (*last, res[0][1]), "swap_rest_wait", with_sources=True)
    last = updated(large, both[:len(large)], both[len(large):], 8, "rest_large")

    ex.ffn1_received((*last, res[0][1]))
    halves = _reduce_own(me, [ex.parts[n] for n in FFN1_GROUP], [ex.recv[n] for n in FFN1_GROUP],
                         jnp.zeros(DEP_SHAPE, F32), FOLD_STEPS, "reduce_ffn1", half=core)
    grads = _sibling_fill(halves, "sibling_fill_ffn1")
    res = _adamw([_shard_2d(n, given[n]) for n in FFN1_GROUP], grads, None, [_shard_2d(n, given["m_" + n]) for n in FFN1_GROUP],
                 [_shard_2d(n, given["v_" + n]) for n in FFN1_GROUP], 8, "adamw_ffn1")
    for n, rs in zip(FFN1_GROUP, res):
        out[n] = [(r.T if n in TRANSPOSED else r).reshape(given[n].shape) for r in rs]
    return (total_loss, dx[None], *[out[n][0] for n in WEIGHTS], *[out[n][1] for n in WEIGHTS],
            *[out[n][2] for n in WEIGHTS], *[out[n][3] for n in WEIGHTS])
```

```python
import functools

import jax
import jax.numpy as jnp
from jax import lax
from jax.experimental import pallas as pl
from jax.experimental.pallas import tpu as pltpu

F32 = jnp.float32
BF16 = jnp.bfloat16
MESH = pl.DeviceIdType.MESH

RMS_EPS = 1e-6
GN_EPS = 64e-5
NEG_INF = -1e30
FFN_RESIDUAL = 0.5
HEAD_DIM = 64
ATT_BLOCK = 128
DILATIONS = (1, 4, 16)
SCAN_CHUNK = 64
TOKEN_TILE = 256
FFN_BWD_TILE = 512

ADAM_LR = 0.001
ADAM_B1 = 0.9
ADAM_B2 = 0.999
ADAM_EPS = 1e-08
ADAM_WD = 0.01
ADAM_STEP = 10

VMEM_FULL = pl.BlockSpec(memory_space=pltpu.VMEM)
ANY = pl.BlockSpec(memory_space=pl.ANY)


VMEM_LIMIT = 56 * 1024 * 1024


def _params(*sem):
    return pltpu.CompilerParams(dimension_semantics=sem, vmem_limit_bytes=VMEM_LIMIT)


def _dot(a, b, dims):
    return lax.dot_general(a.astype(BF16), b.astype(BF16), (dims, ((), ())), preferred_element_type=F32)


def _dot_nn(a, b):
    return _dot(a, b, ((1,), (0,)))


def _dot_nt(a, b):
    return _dot(a, b, ((1,), (1,)))


def _dot_tn(a, b):
    return _dot(a, b, ((0,), (0,)))


@jax.custom_vjp
def _mm(a, b):
    return _dot_nn(a, b)


def _mm_fwd(a, b):
    return _dot_nn(a, b), (a, b)


def _mm_bwd(res, g):
    a, b = res
    return _dot_nt(g, b).astype(a.dtype), _dot_tn(a, g).astype(b.dtype)


_mm.defvjp(_mm_fwd, _mm_bwd)


@jax.custom_vjp
def _mm_nt(a, bt):
    return _dot_nt(a, bt)


def _mm_nt_fwd(a, bt):
    return _dot_nt(a, bt), (a, bt)


def _mm_nt_bwd(res, g):
    a, bt = res
    return _dot_nn(g, bt).astype(a.dtype), _dot_tn(g, a).astype(bt.dtype)


_mm_nt.defvjp(_mm_nt_fwd, _mm_nt_bwd)


def _bdot(a, b, ca, cb):
    return lax.dot_general(a.astype(BF16), b.astype(BF16), (((ca,), (cb,)), ((0,), (0,))), preferred_element_type=F32)


@jax.custom_vjp
def _bmm_nt(a, b):
    return _bdot(a, b, 2, 2)


def _bmm_nt_fwd(a, b):
    return _bdot(a, b, 2, 2), (a, b)


def _bmm_nt_bwd(res, g):
    a, b = res
    return _bdot(g, b, 2, 1), _bdot(g, a, 1, 1)


_bmm_nt.defvjp(_bmm_nt_fwd, _bmm_nt_bwd)


@jax.custom_vjp
def _bmm_nn(a, b):
    return _bdot(a, b, 2, 1)


def _bmm_nn_fwd(a, b):
    return _bdot(a, b, 2, 1), (a, b)


def _bmm_nn_bwd(res, g):
    a, b = res
    return _bdot(g, b, 2, 2), _bdot(a, g, 1, 1)


_bmm_nn.defvjp(_bmm_nn_fwd, _bmm_nn_bwd)


@jax.custom_vjp
def _bmm_tn(a, b):
    return _bdot(a, b, 1, 1)


def _bmm_tn_fwd(a, b):
    return _bdot(a, b, 1, 1), (a, b)


def _bmm_tn_bwd(res, g):
    a, b = res
    return _bdot(b, g, 2, 2), _bdot(a, g, 2, 1)


_bmm_tn.defvjp(_bmm_tn_fwd, _bmm_tn_bwd)


def _hdot(a, b, ca, cb):
    return lax.dot_general(a, b, (((ca,), (cb,)), ((0,), (0,))), precision=lax.Precision.HIGH, preferred_element_type=F32)


def _sigmoid(x):
    return 1.0 / (1.0 + jnp.exp(-x))


def _rms(x):
    return lax.rsqrt(jnp.mean(x * x, axis=-1, keepdims=True) + RMS_EPS)


def _ffn_fwd(x, norm, wg, wu, wd, dep, name, target=None):
    t, d = x.shape
    nc, fc, _ = wg.shape
    tm = TOKEN_TILE

    def body(x_ref, n_ref, wg_ref, wu_ref, wd_ref, dep_ref, *rest):
        o_ref, g_ref, u_ref = rest[-3:] if target is None else rest[1:4]
        xv = x_ref[...]
        h = (xv * _rms(xv) * n_ref[...]).astype(BF16)
        acc = jnp.zeros((tm, d), F32)
        for c in range(nc):
            g = _dot_nt(h, wg_ref[c])
            u = _dot_nt(h, wu_ref[c])
            g_ref[c] = g.astype(BF16)
            u_ref[c] = u.astype(BF16)
            a = (g * _sigmoid(g) * u).astype(BF16)
            acc = acc + jnp.dot(a, wd_ref[c], preferred_element_type=F32)
        y = xv + FFN_RESIDUAL * acc
        if target is None:
            o_ref[...] = y
        else:
            t_ref, loss_ref = rest[0], rest[4]
            err = y - t_ref[...]
            o_ref[...] = err * (1.0 / d)
            part = 0.5 * jnp.sum(jnp.mean(err * err, axis=-1, keepdims=True), axis=0, keepdims=True)

            @pl.when(pl.program_id(0) == 0)
            def _():
                loss_ref[...] = jnp.zeros_like(loss_ref)

            loss_ref[...] += jnp.broadcast_to(part, loss_ref.shape)

    tile = pl.BlockSpec((tm, d), lambda i: (i, 0))
    hidden = pl.BlockSpec((nc, tm, fc), lambda i: (0, i, 0))
    hshape = jax.ShapeDtypeStruct((nc, t, fc), BF16)
    with_loss = target is not None
    return pl.pallas_call(
        body, name=name, grid=(t // tm,),
        out_shape=(jax.ShapeDtypeStruct((t, d), F32), hshape, hshape) + ((jax.ShapeDtypeStruct((1, 128), F32),) if with_loss else ()),
        in_specs=[tile, pl.BlockSpec((1, d), lambda i: (0, 0)), VMEM_FULL, VMEM_FULL, VMEM_FULL, ANY] + ([tile] if with_loss else []),
        out_specs=(tile, hidden, hidden) + ((pl.BlockSpec((1, 128), lambda i: (0, 0)),) if with_loss else ()),
        compiler_params=_params("arbitrary"),
    )(x, norm, wg, wu, wd, dep, *((target,) if with_loss else ()))


def _rmsnorm_bwd(xv, gain, dh):
    rs = _rms(xv)
    xn = xv * rs
    dxn = dh * gain
    dx = rs * (dxn - xn * jnp.mean(dxn * xn, axis=-1, keepdims=True))
    return dx, jnp.sum(dh * xn, axis=0, keepdims=True)


def _ffn_bwd(x, norm, wg, wu, wd, gate, up, dy, dep, name):
    t, d = x.shape
    nc, fc, _ = wg.shape
    tm = FFN_BWD_TILE
    nt = t // tm

    def body(x_ref, n_ref, wg_ref, wu_ref, wd_ref, g_ref, u_ref, dy_ref, dep_ref, dx_ref, dn_ref, dwg_ref, dwu_ref,
             dwd_ref, dh_ref, ag_ref, au_ref, ad_ref):
        c, i = pl.program_id(0), pl.program_id(1)
        rows = pl.ds(pl.multiple_of(i * tm, tm), tm)
        xv = x_ref[...]
        gain = n_ref[...]
        h = (xv * _rms(xv) * gain).astype(BF16)
        dy = dy_ref[...]
        dyb = (FFN_RESIDUAL * dy).astype(BF16)
        g = g_ref[0].astype(F32)
        u = u_ref[0].astype(F32)
        sg = _sigmoid(g)
        s = g * sg
        a = (s * u).astype(BF16)
        da = _dot_nt(dyb, wd_ref[0])
        dub = (da * s).astype(BF16)
        dgb = (da * u * (sg * (1.0 + g * (1.0 - sg)))).astype(BF16)
        dwd_c = _dot_tn(a, dyb)
        dwg_c = _dot_tn(dgb, h)
        dwu_c = _dot_tn(dub, h)
        dh_c = _dot_nn(dgb, wg_ref[0]) + _dot_nn(dub, wu_ref[0])

        @pl.when(i == 0)
        def _():
            ad_ref[...] = dwd_c
            ag_ref[...] = dwg_c
            au_ref[...] = dwu_c

        @pl.when(i > 0)
        def _():
            ad_ref[...] += dwd_c
            ag_ref[...] += dwg_c
            au_ref[...] += dwu_c

        @pl.when(i == nt - 1)
        def _():
            dwd_ref[0] = ad_ref[...].astype(BF16)
            dwg_ref[0] = ag_ref[...].astype(BF16)
            dwu_ref[0] = au_ref[...].astype(BF16)

        @pl.when(c == 0)
        def _():
            dh_ref[rows, :] = dh_c

        @pl.when(c > 0)
        def _():
            dh_ref[rows, :] += dh_c

        @pl.when(c == nc - 1)
        def _():
            dx, dn = _rmsnorm_bwd(xv, gain, dh_ref[rows, :])
            dx_ref[...] = dx + dy

            @pl.when(i == 0)
            def _():
                dn_ref[...] = dn

            @pl.when(i > 0)
            def _():
                dn_ref[...] += dn

    tile = pl.BlockSpec((tm, d), lambda c, i: (i, 0))
    row = pl.BlockSpec((1, d), lambda c, i: (0, 0))
    wrow = pl.BlockSpec((1, fc, d), lambda c, i: (c, 0, 0), pipeline_mode=pl.Buffered(1))
    hidden = pl.BlockSpec((1, tm, fc), lambda c, i: (c, i, 0))
    last = pl.BlockSpec((tm, d), lambda c, i: (jnp.where(c == nc - 1, i, 0), 0))
    return pl.pallas_call(
        body, name=name, grid=(nc, nt),
        out_shape=(jax.ShapeDtypeStruct((t, d), F32), jax.ShapeDtypeStruct((1, d), F32),
                   jax.ShapeDtypeStruct(wg.shape, BF16), jax.ShapeDtypeStruct(wu.shape, BF16),
                   jax.ShapeDtypeStruct(wd.shape, BF16)),
        in_specs=[tile, row, wrow, wrow, wrow, hidden, hidden, tile, ANY],
        out_specs=(last, row, wrow, wrow, wrow),
        scratch_shapes=[pltpu.VMEM((t, d), F32)] + [pltpu.VMEM((fc, d), F32)] * 3,
        compiler_params=_params("arbitrary", "arbitrary"),
    )(x, norm, wg, wu, wd, gate, up, dy, dep)


def _store_heads(ref, v):
    for h in range(ref.shape[0]):
        ref[h] = v[:, h * HEAD_DIM:(h + 1) * HEAD_DIM]


def _load_heads(ref):
    return jnp.concatenate([ref[h] for h in range(ref.shape[0])], axis=-1)


N_HEAD_GROUPS = 3


def _proj_fwd(x, norm, w, c):
    t, d = x.shape
    nc, _, ncol = w.shape
    nh = c // HEAD_DIM
    tm = TOKEN_TILE
    wide = nc * ncol - N_HEAD_GROUPS * c

    def body(x_ref, n_ref, w_ref, q_ref, k_ref, v_ref, cur_ref):
        xv = x_ref[...]
        h = (xv * _rms(xv) * n_ref[...]).astype(BF16)
        full = jnp.concatenate([jnp.dot(h, w_ref[s], preferred_element_type=F32) for s in range(nc)], axis=1)
        for m, ref in enumerate((q_ref, k_ref, v_ref)):
            _store_heads(ref, full[:, m * c:(m + 1) * c])
        cur_ref[...] = full[:, N_HEAD_GROUPS * c:]

    heads = pl.BlockSpec((nh, tm, HEAD_DIM), lambda i: (0, i, 0))
    hshape = jax.ShapeDtypeStruct((nh, t, HEAD_DIM), F32)
    return pl.pallas_call(
        body, name="proj_fwd", grid=(t // tm,),
        out_shape=(hshape, hshape, hshape, jax.ShapeDtypeStruct((t, wide), F32)),
        in_specs=[pl.BlockSpec((tm, d), lambda i: (i, 0)), pl.BlockSpec((1, d), lambda i: (0, 0)), VMEM_FULL],
        out_specs=(heads, heads, heads, pl.BlockSpec((tm, wide), lambda i: (i, 0))),
        compiler_params=_params("arbitrary"),
    )(x, norm, w)


def _proj_bwd(x, norm, w, dq, dk, dv, dcur, dres):
    t, d = x.shape
    nc, _, ncol = w.shape
    nh = dq.shape[0]
    tm = TOKEN_TILE
    nt = t // tm
    wide = dcur.shape[1]

    def body(x_ref, n_ref, w_ref, dq_ref, dk_ref, dv_ref, dcur_ref, dres_ref, dx_ref, dn_ref, dw_ref, acc_ref):
        i = pl.program_id(0)

        @pl.when(i == 0)
        def _():
            acc_ref[...] = jnp.zeros_like(acc_ref)
            dn_ref[...] = jnp.zeros_like(dn_ref)

        xv = x_ref[...]
        gain = n_ref[...]
        h = (xv * _rms(xv) * gain).astype(BF16)
        dp = jnp.concatenate([_load_heads(dq_ref), _load_heads(dk_ref), _load_heads(dv_ref), dcur_ref[...]], axis=1).astype(BF16)
        dh = jnp.zeros((tm, d), F32)
        for s in range(nc):
            dps = dp[:, s * ncol:(s + 1) * ncol]
            acc_ref[s] += _dot_tn(h, dps)
            dh = dh + _dot_nt(dps, w_ref[s])
        dx, dn = _rmsnorm_bwd(xv, gain, dh)
        dx_ref[...] = dx + dres_ref[...]
        dn_ref[...] += dn

        @pl.when(i == nt - 1)
        def _():
            dw_ref[...] = acc_ref[...].astype(BF16)

    tile = pl.BlockSpec((tm, d), lambda i: (i, 0))
    row = pl.BlockSpec((1, d), lambda i: (0, 0))
    heads = pl.BlockSpec((nh, tm, HEAD_DIM), lambda i: (0, i, 0))
    return pl.pallas_call(
        body, name="proj_bwd", grid=(nt,),
        out_shape=(jax.ShapeDtypeStruct((t, d), F32), jax.ShapeDtypeStruct((1, d), F32),
                   jax.ShapeDtypeStruct(w.shape, BF16)),
        in_specs=[tile, row, VMEM_FULL, heads, heads, heads, pl.BlockSpec((tm, wide), lambda i: (i, 0)), tile],
        out_specs=(tile, row, VMEM_FULL),
        scratch_shapes=[pltpu.VMEM(w.shape, F32)], compiler_params=_params("arbitrary"),
    )(x, norm, w, dq, dk, dv, dcur, dres)


def _mixout_fwd(x, att, opg, gate, w):
    t, d = x.shape
    nh = att.shape[0]
    half = gate.shape[1]
    tm = TOKEN_TILE

    def body(x_ref, att_ref, opg_ref, g_ref, w_ref, o_ref):
        mix = jnp.concatenate([_load_heads(att_ref), _load_heads(opg_ref) * g_ref[...]], axis=-1).astype(BF16)
        o_ref[...] = x_ref[...] + jnp.dot(mix, w_ref[...], preferred_element_type=F32)

    tile = pl.BlockSpec((tm, d), lambda i: (i, 0))
    htile = pl.BlockSpec((tm, half), lambda i: (i, 0))
    heads = pl.BlockSpec((nh, tm, HEAD_DIM), lambda i: (0, i, 0))
    return pl.pallas_call(
        body, name="mixout_fwd", grid=(t // tm,), out_shape=jax.ShapeDtypeStruct((t, d), F32),
        in_specs=[tile, heads, heads, htile, VMEM_FULL], out_specs=tile, compiler_params=_params("arbitrary"),
    )(x, att, opg, gate, w)


def _mixout_bwd(att, opg, gate, w, dy, dep):
    nh, t, _ = att.shape
    half = gate.shape[1]
    d = dy.shape[1]
    tm = TOKEN_TILE

    def body(att_ref, opg_ref, g_ref, w_ref, dy_ref, dep_ref, datt_ref, dopg_ref, dg_ref, dw_ref):
        i = pl.program_id(0)
        opg_v, g_v = _load_heads(opg_ref), g_ref[...]
        mix = jnp.concatenate([_load_heads(att_ref), opg_v * g_v], axis=-1).astype(BF16)
        dyb = dy_ref[...].astype(BF16)
        dmix = _dot_nt(dyb, w_ref[...])
        dw = _dot_tn(mix, dyb)
        _store_heads(datt_ref, dmix[:, :half])
        drw = dmix[:, half:]
        _store_heads(dopg_ref, drw * g_v)
        dg_ref[...] = drw * opg_v

        @pl.when(i == 0)
        def _():
            dw_ref[...] = dw

        @pl.when(i > 0)
        def _():
            dw_ref[...] += dw

    tile = pl.BlockSpec((tm, d), lambda i: (i, 0))
    htile = pl.BlockSpec((tm, half), lambda i: (i, 0))
    heads = pl.BlockSpec((nh, tm, HEAD_DIM), lambda i: (0, i, 0))
    hshape = jax.ShapeDtypeStruct((nh, t, HEAD_DIM), F32)
    return pl.pallas_call(
        body, name="mixout_bwd", grid=(t // tm,),
        out_shape=(hshape, hshape, jax.ShapeDtypeStruct((t, half), F32), jax.ShapeDtypeStruct(w.shape, F32)),
        in_specs=[heads, heads, htile, VMEM_FULL, tile, ANY],
        out_specs=(heads, heads, htile, pl.BlockSpec(w.shape, lambda i: (0, 0))),
        compiler_params=_params("arbitrary"),
    )(att, opg, gate, w, dy, dep)


def _head_norm(x, gain):
    return x * _rms(x) * gain


def _att_pattern(qh, kh, v, nb):
    g, blk, _ = qh.shape
    scale = HEAD_DIM ** -0.5
    qi = lax.broadcasted_iota(jnp.int32, (blk, blk), 0)
    kj = lax.broadcasted_iota(jnp.int32, (blk, blk), 1)
    sc = jnp.where(kj <= qi, _bmm_nt(qh, kh) * scale, NEG_INF)
    top = jnp.max(sc, axis=-1, keepdims=True)
    if nb > 1:
        khp = jnp.concatenate([kh[:1], kh[:-1]], axis=0)
        vp = jnp.concatenate([v[:1], v[:-1]], axis=0)
        has_prev = lax.broadcasted_iota(jnp.int32, (g, 1, 1), 0) % nb != 0
        sp = jnp.where((kj >= qi) & has_prev, _bmm_nt(qh, khp) * scale, NEG_INF)
        top = jnp.maximum(top, jnp.max(sp, axis=-1, keepdims=True))
    m = lax.stop_gradient(top)
    pc = jnp.exp(sc - m)
    den = jnp.sum(pc, axis=-1, keepdims=True)
    acc = _bmm_nn(pc, v)
    if nb > 1:
        pp = jnp.exp(sp - m)
        den = den + jnp.sum(pp, axis=-1, keepdims=True)
        acc = acc + _bmm_nn(pp, vp)
    o = acc / den
    return o, jnp.broadcast_to(m + jnp.log(den), o.shape)


def _pattern_rows(t, dil):
    length = t // dil
    return [pl.ds(r, length, stride=dil) if dil > 1 else pl.ds(0, length) for r in range(dil)], length // ATT_BLOCK


def _take(ref, rows, nb):
    return jnp.concatenate([ref[0, r, :].reshape(nb, ATT_BLOCK, HEAD_DIM) for r in rows], axis=0)


def _put(ref, rows, nb, val):
    for j, r in enumerate(rows):
        ref[0, r, :] = val[j * nb:(j + 1) * nb].reshape(nb * ATT_BLOCK, HEAD_DIM)


def _put_add(ref, rows, nb, val):
    for j, r in enumerate(rows):
        ref[0, r, :] += val[j * nb:(j + 1) * nb].reshape(nb * ATT_BLOCK, HEAD_DIM)


def _merge_fn(o1, o2, o3, l1, l2, l3):
    m = lax.stop_gradient(jnp.maximum(jnp.maximum(l1, l2), l3))
    e1, e2, e3 = jnp.exp(l1 - m), jnp.exp(l2 - m), jnp.exp(l3 - m)
    return (e1 * o1 + e2 * o2 + e3 * o3) / (e1 + e2 + e3)


def _token_rows(j):
    return pl.ds(pl.multiple_of(j * ATT_BLOCK, ATT_BLOCK), ATT_BLOCK)


def _norm_rows(t, q_ref, k_ref, gq, gk, qh_ref, kh_ref):
    def step(j, carry):
        rows = _token_rows(j)
        qh_ref[0, rows, :] = _head_norm(q_ref[0, rows, :], gq[0])
        kh_ref[0, rows, :] = _head_norm(k_ref[0, rows, :], gk[0])
        return carry

    lax.fori_loop(0, t // ATT_BLOCK, step, 0)


def _att_head_specs(t):
    head = pl.BlockSpec((1, t, HEAD_DIM), lambda h: (h, 0, 0))
    gain = pl.BlockSpec((1, 1, HEAD_DIM), lambda h: (0, 0, 0))
    return head, gain


def _att_fwd(q, k, v, qn, kn):
    nh, t, dh = q.shape
    head, gain = _att_head_specs(t)

    def body(q_ref, k_ref, v_ref, qn_ref, kn_ref, att_ref, o1, o2, o3, l1, l2, l3, qh_ref, kh_ref):
        saved = (o1, o2, o3, l1, l2, l3)
        _norm_rows(t, q_ref, k_ref, qn_ref[...], kn_ref[...], qh_ref, kh_ref)
        for p, dil in enumerate(DILATIONS):
            rows, nb = _pattern_rows(t, dil)
            o, lse = _att_pattern(_take(qh_ref, rows, nb), _take(kh_ref, rows, nb), _take(v_ref, rows, nb), nb)
            _put(saved[p], rows, nb, o)
            _put(saved[3 + p], rows, nb, lse)

        def merge(j, carry):
            rows = _token_rows(j)
            att_ref[0, rows, :] = _merge_fn(*[r[0, rows, :] for r in saved])
            return carry

        lax.fori_loop(0, t // ATT_BLOCK, merge, 0)

    return pl.pallas_call(
        body, name="att_fwd", grid=(nh,), out_shape=(jax.ShapeDtypeStruct(q.shape, F32),) * 7,
        in_specs=[head, head, head, gain, gain], out_specs=(head,) * 7,
        scratch_shapes=[pltpu.VMEM((1, t, dh), F32)] * 2, compiler_params=_params("arbitrary"),
    )(q, k, v, qn, kn)


def _att_bwd(q, k, v, qn, kn, saved, datt):
    nh, t, dh = q.shape
    head, gain = _att_head_specs(t)

    def body(q_ref, k_ref, v_ref, qn_ref, kn_ref, o1, o2, o3, l1, l2, l3, datt_ref,
             dq_ref, dk_ref, dv_ref, dqn_ref, dkn_ref, qh_ref, kh_ref, dqh_ref, dkh_ref, *ct_refs):
        for ref in (dqh_ref, dkh_ref, dv_ref):
            ref[...] = jnp.zeros_like(ref)

        @pl.when(pl.program_id(0) == 0)
        def _():
            dqn_ref[...] = jnp.zeros_like(dqn_ref)
            dkn_ref[...] = jnp.zeros_like(dkn_ref)

        gq, gk = qn_ref[...], kn_ref[...]
        _norm_rows(t, q_ref, k_ref, gq, gk, qh_ref, kh_ref)

        def merge_cotangents(j, carry):
            rows = _token_rows(j)
            _, merge_vjp = jax.vjp(_merge_fn, *[r[0, rows, :] for r in (o1, o2, o3, l1, l2, l3)])
            for ref, val in zip(ct_refs, merge_vjp(datt_ref[0, rows, :])):
                ref[0, rows, :] = val
            return carry

        lax.fori_loop(0, t // ATT_BLOCK, merge_cotangents, 0)

        for p, dil in enumerate(DILATIONS):
            rows, nb = _pattern_rows(t, dil)
            _, pattern_vjp = jax.vjp(functools.partial(_att_pattern, nb=nb), _take(qh_ref, rows, nb), _take(kh_ref, rows, nb),
                                     _take(v_ref, rows, nb))
            dqh, dkh, dv = pattern_vjp((_take(ct_refs[p], rows, nb), _take(ct_refs[3 + p], rows, nb)))
            _put_add(dqh_ref, rows, nb, dqh)
            _put_add(dkh_ref, rows, nb, dkh)
            _put_add(dv_ref, rows, nb, dv)

        def norm_cotangents(j, carry):
            rows = _token_rows(j)
            out = []
            for x_ref, gain, dh_ref, dx_ref, acc in ((q_ref, gq, dqh_ref, dq_ref, carry[0]), (k_ref, gk, dkh_ref, dk_ref, carry[1])):
                _, norm_vjp = jax.vjp(_head_norm, x_ref[0, rows, :], gain[0])
                dx, dgain = norm_vjp(dh_ref[0, rows, :])
                dx_ref[0, rows, :] = dx
                out.append(acc + dgain)
            return tuple(out)

        zero = jnp.zeros((1, dh), F32)
        dgq, dgk = lax.fori_loop(0, t // ATT_BLOCK, norm_cotangents, (zero, zero))
        dqn_ref[0] += dgq
        dkn_ref[0] += dgk

    hshape = jax.ShapeDtypeStruct(q.shape, F32)
    gshape = jax.ShapeDtypeStruct((1, 1, dh), F32)
    return pl.pallas_call(
        body, name="att_bwd", grid=(nh,), out_shape=(hshape, hshape, hshape, gshape, gshape),
        in_specs=[head, head, head, gain, gain] + [head] * 7, out_specs=(head, head, head, gain, gain),
        scratch_shapes=[pltpu.VMEM((1, t, dh), F32)] * 10, compiler_params=_params("arbitrary"),
    )(q, k, v, qn, kn, *saved, datt)


RWKV_VEC = ("mu_r", "mu_k", "mu_v", "mu_w", "mu_a", "mu_g", "w0", "a0", "k_k", "k_a")
RWKV_MAT = ("w1", "w2", "a1", "a2", "g1", "g2")


def _rwkv_pre_fn(cur, prev, vec, w1t, w2, a1t, a2, g1t, g2):
    c = cur.shape[1] // 4
    mu_r, mu_k, mu_v, mu_w, mu_a, mu_g, w0, a0, k_k, k_a = (vec[j:j + 1] for j in range(10))

    def lerp(j, mu):
        xc, xp = cur[:, j * c:(j + 1) * c], prev[:, j * c:(j + 1) * c]
        return xc + (xp - xc) * mu

    r, k, v = lerp(0, mu_r), lerp(1, mu_k), lerp(2, mu_v)
    cw, ca, cg = lerp(3, mu_w), lerp(3, mu_a), lerp(3, mu_g)
    z = w0 + _mm(jnp.tanh(_mm_nt(cw, w1t)), w2)
    w_log = jnp.minimum(z, 0.0) - jnp.log(1.0 + jnp.exp(-jnp.abs(z))) - 0.5
    lw = -jnp.exp(w_log)
    a = _sigmoid(a0 + _mm(_mm_nt(ca, a1t), a2))
    gate = _mm(_sigmoid(_mm_nt(cg, g1t)), g2)
    kkraw = k * k_k
    kmod = k * (1.0 + (a - 1.0) * k_a)
    return r, lw, kmod, v, kkraw, a, gate


HALO_ROWS = 8


def _rwkv_pre_specs(c, mats, tile_of):
    tm = TOKEN_TILE
    nh = c // HEAD_DIM
    wide = pl.BlockSpec((tm, 4 * c), lambda j: (tile_of(j), 0))
    halo = pl.BlockSpec((HALO_ROWS, 4 * c), lambda j: (jnp.maximum(tile_of(j) * (tm // HALO_ROWS) - 1, 0), 0))
    one = pl.BlockSpec((tm, c), lambda j: (tile_of(j), 0))
    heads = pl.BlockSpec((nh, tm, HEAD_DIM), lambda j: (0, tile_of(j), 0))
    vec = pl.BlockSpec((10, c), lambda j: (0, 0))
    mspecs = [pl.BlockSpec(m.shape, lambda j: (0, 0)) for m in mats]
    return wide, halo, one, heads, vec, mspecs


def _previous_rows(cur, halo, tile):
    first = jnp.where(tile > 0, halo[HALO_ROWS - 1:HALO_ROWS], 0.0)
    rows = lax.broadcasted_iota(jnp.int32, cur.shape, 0)
    return jnp.where(rows == 0, first, pltpu.roll(cur, 1, axis=0))


def _rwkv_pre_fwd(cur, vec, mats):
    t, c4 = cur.shape
    c = c4 // 4
    wide, halo, one, heads, vspec, mspecs = _rwkv_pre_specs(c, mats, lambda j: j)

    def body(cur_ref, halo_ref, vec_ref, *rest):
        mrefs, outs = rest[:6], rest[6:]
        cur_v = cur_ref[...]
        prev = _previous_rows(cur_v, halo_ref[...], pl.program_id(0))
        vals = _rwkv_pre_fn(cur_v, prev, vec_ref[...], *(m[...] for m in mrefs))
        for ref, val in zip(outs[:6], vals[:6]):
            _store_heads(ref, val)
        outs[6][...] = vals[6]

    hshape = jax.ShapeDtypeStruct((c // HEAD_DIM, t, HEAD_DIM), F32)
    return pl.pallas_call(
        body, name="rwkv_pre_fwd", grid=(t // TOKEN_TILE,), out_shape=(hshape,) * 6 + (jax.ShapeDtypeStruct((t, c), F32),),
        in_specs=[wide, halo, vspec] + mspecs, out_specs=(heads,) * 6 + (one,), compiler_params=_params("arbitrary"),
    )(cur, cur, vec, *mats)


def _rwkv_pre_bwd(cur, vec, mats, cts, dgate):
    t, c4 = cur.shape
    c = c4 // 4
    tm = TOKEN_TILE
    nt = t // tm
    wide, halo, one, heads, vspec, mspecs = _rwkv_pre_specs(c, mats, lambda j: nt - 1 - j)

    def body(cur_ref, halo_ref, vec_ref, *rest):
        mrefs, ctrefs, dgate_ref, outs, carry_ref = rest[:6], rest[6:12], rest[12], rest[13:-1], rest[-1]
        j = pl.program_id(0)

        @pl.when(j == 0)
        def _():
            carry_ref[...] = jnp.zeros_like(carry_ref)
            for ref in outs[1:]:
                ref[...] = jnp.zeros_like(ref)

        cur_v = cur_ref[...]
        prev = _previous_rows(cur_v, halo_ref[...], nt - 1 - j)
        _, vjp = jax.vjp(_rwkv_pre_fn, cur_v, prev, vec_ref[...], *(m[...] for m in mrefs))
        grads = vjp(tuple(_load_heads(r) for r in ctrefs) + (dgate_ref[...],))
        dprev = grads[1]
        rows = lax.broadcasted_iota(jnp.int32, dprev.shape, 0)
        outs[0][...] = grads[0] + jnp.where(rows == tm - 1, carry_ref[0:1], pltpu.roll(dprev, tm - 1, axis=0))
        carry_ref[0:1] = dprev[0:1]
        for ref, val in zip(outs[1:], grads[2:]):
            ref[...] += val

    return pl.pallas_call(
        body, name="rwkv_pre_bwd", grid=(nt,),
        out_shape=(jax.ShapeDtypeStruct(cur.shape, F32), jax.ShapeDtypeStruct(vec.shape, F32))
        + tuple(jax.ShapeDtypeStruct(m.shape, F32) for m in mats),
        in_specs=[wide, halo, vspec] + mspecs + [heads] * 6 + [one], out_specs=(wide, vspec) + tuple(mspecs),
        scratch_shapes=[pltpu.VMEM((HALO_ROWS, c4), F32)], compiler_params=_params("arbitrary"),
    )(cur, cur, vec, *mats, *cts, dgate)


def _scan_chunk_fn(h0, r, lw, k, v, kkraw, a, rk, lnw, lnb):
    n = r.shape[1]
    nrm = jnp.sqrt(jnp.sum(kkraw * kkraw, axis=-1, keepdims=True))
    kk = kkraw / jnp.maximum(nrm, 1e-12)
    av, bv = -kk, kk * a
    ti = lax.broadcasted_iota(jnp.int32, (n, n), 0)
    si = lax.broadcasted_iota(jnp.int32, (n, n), 1)
    incl, strict = ti >= si, ti > si
    ones = jnp.broadcast_to(incl.astype(F32)[None], (r.shape[0], n, n))
    cum = _hdot(ones, lw, 2, 1)
    at, rt = av * jnp.exp(cum - lw), r * jnp.exp(cum)
    inv = jnp.exp(-cum)
    bt, kt = bv * inv, k * inv
    gram = _hdot(jnp.concatenate([at, rt], axis=1), jnp.concatenate([bt, kt], axis=1), 2, 2)
    lab = jnp.where(strict, gram[:, :n, :n], 0.0)
    lak = jnp.where(strict, gram[:, :n, n:], 0.0)
    rb = jnp.where(incl, gram[:, n:, :n], 0.0)
    rkm = jnp.where(incl, gram[:, n:, n:], 0.0)
    nv = v.shape[2]
    u = _bmm_nn(jnp.concatenate([at, lak], axis=2), jnp.concatenate([h0, v], axis=1))
    p = lab
    m = 2
    while m < n:
        both = _bmm_nn(p, jnp.concatenate([u, p], axis=2))
        u, p = u + both[:, :, :nv], both[:, :, nv:]
        m *= 2
    u = u + _bmm_nn(p, u)
    y = _bmm_nn(jnp.concatenate([rt, rb, rkm], axis=2), jnp.concatenate([h0, u, v], axis=1))
    last = jnp.exp(jnp.sum(lw, axis=1, keepdims=True))
    h1 = jnp.swapaxes(last, 1, 2) * (h0 + _bmm_tn(jnp.concatenate([bt, kt], axis=1), jnp.concatenate([u, v], axis=1)))
    mean = jnp.mean(y, axis=-1, keepdims=True)
    yc = y - mean
    var = jnp.mean(yc * yc, axis=-1, keepdims=True)
    yn = yc * lax.rsqrt(var + GN_EPS) * lnw + lnb
    bonus = jnp.sum(r * k * rk, axis=-1, keepdims=True) * v
    return yn + bonus, h1


SCAN_GROUP = 2


def _scan_group_fn(h0, r, lw, k, v, kkraw, a, rk, lnw, lnb):
    outs = []
    for j in range(SCAN_GROUP):
        rows = slice(j * SCAN_CHUNK, (j + 1) * SCAN_CHUNK)
        o, h0 = _scan_chunk_fn(h0, r[:, rows], lw[:, rows], k[:, rows], v[:, rows], kkraw[:, rows], a[:, rows], rk, lnw, lnb)
        outs.append(o)
    return jnp.concatenate(outs, axis=1), h0


def _scan_specs(h, t, dh, rev):
    n = SCAN_CHUNK * SCAN_GROUP
    nc = t // n
    pos = (lambda c: (0, nc - 1 - c, 0)) if rev else (lambda c: (0, c, 0))
    st = (lambda c: (nc - 1 - c, 0, 0, 0)) if rev else (lambda c: (c, 0, 0, 0))
    seq = pl.BlockSpec((h, n, dh), pos)
    par = pl.BlockSpec((h, 1, dh), lambda c: (0, 0, 0))
    state = pl.BlockSpec((1, h, dh, dh), st)
    return seq, par, state


def _scan_fwd(seqs, pars):
    h, t, dh = seqs[0].shape
    nc = t // (SCAN_CHUNK * SCAN_GROUP)
    seq, par, state = _scan_specs(h, t, dh, False)

    def body(r, lw, k, v, kkraw, a, rk, lnw, lnb, o_ref, st_ref, h_ref):
        @pl.when(pl.program_id(0) == 0)
        def _():
            h_ref[...] = jnp.zeros_like(h_ref)

        h0 = h_ref[...]
        st_ref[0] = h0
        o, h1 = _scan_group_fn(h0, r[...], lw[...], k[...], v[...], kkraw[...], a[...], rk[...], lnw[...], lnb[...])
        o_ref[...] = o
        h_ref[...] = h1

    return pl.pallas_call(
        body, name="rwkv_scan_fwd", grid=(nc,),
        out_shape=(jax.ShapeDtypeStruct((h, t, dh), F32), jax.ShapeDtypeStruct((nc, h, dh, dh), F32)),
        in_specs=[seq] * 6 + [par] * 3, out_specs=(seq, state),
        scratch_shapes=[pltpu.VMEM((h, dh, dh), F32)], compiler_params=_params("arbitrary"),
    )(*seqs, *pars)


def _scan_bwd(seqs, pars, states, do):
    h, t, dh = seqs[0].shape
    nc = t // (SCAN_CHUNK * SCAN_GROUP)
    seq, par, state = _scan_specs(h, t, dh, True)

    def body(r, lw, k, v, kkraw, a, rk, lnw, lnb, st_ref, do_ref, *rest):
        douts, dpars, dh_ref = rest[:6], rest[6:9], rest[9]
        first = pl.program_id(0) == 0

        @pl.when(first)
        def _():
            dh_ref[...] = jnp.zeros_like(dh_ref)

        _, vjp = jax.vjp(_scan_group_fn, st_ref[0], r[...], lw[...], k[...], v[...], kkraw[...], a[...],
                         rk[...], lnw[...], lnb[...])
        grads = vjp((do_ref[...], dh_ref[...]))
        dh_ref[...] = grads[0]
        for ref, val in zip(douts, grads[1:7]):
            ref[...] = val

        @pl.when(first)
        def _():
            for ref, val in zip(dpars, grads[7:]):
                ref[...] = val

        @pl.when(jnp.logical_not(first))
        def _():
            for ref, val in zip(dpars, grads[7:]):
                ref[...] += val

    sshape = jax.ShapeDtypeStruct((h, t, dh), F32)
    pshape = jax.ShapeDtypeStruct((h, 1, dh), F32)
    return pl.pallas_call(
        body, name="rwkv_scan_bwd", grid=(nc,), out_shape=(sshape,) * 6 + (pshape,) * 3,
        in_specs=[seq] * 6 + [par] * 3 + [state, seq], out_specs=(seq,) * 6 + (par,) * 3,
        scratch_shapes=[pltpu.VMEM((h, dh, dh), F32)], compiler_params=_params("arbitrary"),
    )(*seqs, *pars, states, do)


def _local_step(x, target, w, ex):
    w = dict(w)
    c = w["mu_r"].shape[-1]
    qn, kn = w["q_norm"].reshape(1, 1, HEAD_DIM), w["k_norm"].reshape(1, 1, HEAD_DIM)
    vec = jnp.concatenate([w[n].reshape(1, c) for n in RWKV_VEC], axis=0)
    pars = [w[n].reshape(-1, 1, HEAD_DIM) for n in ("r_k", "ln_x_w", "ln_x_b")]
    no_dep = jnp.zeros(DEP_SHAPE, F32)

    x1, gate1, up1 = _ffn_fwd(x, w["ffn1_norm"], w["ffn1_w_gate"], w["ffn1_w_up"], w["ffn1_w_down"], ex.first_dep, "ffn1_fwd")
    w.update(ex.mix_weights((x1,)))
    mats = [w[n] for n in RWKV_MAT]
    q, k, v, cur = _proj_fwd(x1, w["mix_norm"], w["w_in"], c)
    att, *saved = _att_fwd(q, k, v, qn, kn)
    pre = _rwkv_pre_fwd(cur, vec, mats)
    seqs, gate = pre[:6], pre[6]
    opg, states = _scan_fwd(seqs, pars)
    w.update(ex.out_weights((att, opg)))
    x2 = _mixout_fwd(x1, att, opg, gate, w["w_out"])
    dy, gate2, up2, loss = _ffn_fwd(x2, w["ffn2_norm"], w["ffn2_w_gate"], w["ffn2_w_up"], w["ffn2_w_down"], no_dep, "ffn2_fwd",
                                    target=target)

    g = {}
    dx2, g["ffn2_norm"], g["ffn2_w_gate"], g["ffn2_w_up"], g["ffn2_w_down"] = _ffn_bwd(
        x2, w["ffn2_norm"], w["ffn2_w_gate"], w["ffn2_w_up"], w["ffn2_w_down"], gate2, up2, dy, no_dep, "ffn2_bwd")
    dep = ex.send_ffn2({n: g[n] for n in ("ffn2_w_gate", "ffn2_w_up", "ffn2_w_down")})
    datt, dopg, dgate, g["w_out"] = _mixout_bwd(att, opg, gate, w["w_out"], dx2, dep)
    dscan = _scan_bwd(seqs, pars, states, dopg)
    for n, d in zip(("r_k", "ln_x_w", "ln_x_b"), dscan[6:]):
        g[n] = d
    dcur, dvec, *dmats = _rwkv_pre_bwd(cur, vec, mats, dscan[:6], dgate)
    for n, d in zip(RWKV_MAT, dmats):
        g[n] = d
    g["rwkv_vec"] = dvec
    dq, dk, dv, g["q_norm"], g["k_norm"] = _att_bwd(q, k, v, qn, kn, saved, datt)
    dx1, g["mix_norm"], g["w_in"] = _proj_bwd(x1, w["mix_norm"], w["w_in"], dq, dk, dv, dcur, dx2)
    dep = ex.send_mix({n: g[n] for n in ("w_in", "w_out") + RWKV_MAT}, (dx1,))
    dx, g["ffn1_norm"], g["ffn1_w_gate"], g["ffn1_w_up"], g["ffn1_w_down"] = _ffn_bwd(
        x, w["ffn1_norm"], w["ffn1_w_gate"], w["ffn1_w_up"], w["ffn1_w_down"], gate1, up1, dx1, dep, "ffn1_bwd")
    return loss, dx, g


N_SHARDS = 4


def _place():
    return lax.axis_index("x"), lax.axis_index("y"), lax.axis_index("c")


def _chip_peers(x, y):
    return [(1 - x, y), (x, 1 - y), (1 - x, 1 - y)]


HBM = pl.BlockSpec(memory_space=pltpu.HBM)
SEM = pl.BlockSpec(memory_space=pltpu.SEMAPHORE)
DEP_SHAPE = (8, 128)


class _Views:
    to_sibling = False


class _GatherViews(_Views):
    @staticmethod
    def send(i, srcs, lands, k, at):
        return srcs[i], lands[i].at[at[3]]

    @staticmethod
    def landing(i, srcs, lands, k, at):
        return srcs[i], lands[i].at[2 * at[4] + at[5]]


class _ScatterViews(_Views):
    @staticmethod
    def send(i, srcs, lands, k, at):
        return srcs[i].at[2 * at[4] + at[5]], lands[i].at[k]

    @staticmethod
    def landing(i, srcs, lands, k, at):
        return srcs[i].at[at[3]], lands[i].at[k]


def _half_rows(ref, slot, half):
    rows = ref.shape[1] // 2
    return ref.at[slot, pl.ds(pl.multiple_of(half * rows, BF16_SUBLANES), rows)]


class _HalfGatherViews(_Views):
    @staticmethod
    def send(i, srcs, lands, k, at):
        rows = srcs[i].shape[0] // 2
        return srcs[i].at[pl.ds(pl.multiple_of(at[2] * rows, BF16_SUBLANES), rows)], _half_rows(lands[i], at[3], at[2])

    @staticmethod
    def landing(i, srcs, lands, k, at):
        rows = srcs[i].shape[0] // 2
        return srcs[i].at[pl.ds(pl.multiple_of(at[2] * rows, BF16_SUBLANES), rows)], _half_rows(lands[i], 2 * at[4] + at[5], at[2])


class _ForwardViews(_Views):
    to_sibling = True

    @staticmethod
    def send(i, srcs, lands, k, at):
        mine = _half_rows(lands[i], 2 * at[4] + at[5], at[2])
        return mine, mine

    @staticmethod
    def landing(i, srcs, lands, k, at):
        theirs = _half_rows(lands[i], 2 * at[4] + at[5], 1 - at[2])
        return theirs, theirs


class _SiblingViews(_Views):
    to_sibling = True

    @staticmethod
    def _block(ref, k):
        size = -(-ref.shape[0] // 3 // BF16_SUBLANES) * BF16_SUBLANES
        return ref.at[pl.ds(k * size, min(size, ref.shape[0] - k * size))]

    @classmethod
    def send(cls, i, srcs, lands, k, at):
        return cls._block(srcs[i], k), cls._block(lands[i], k)

    landing = send


def _push_start(srcs, lands, views, after, name):
    ns, nl = len(srcs), len(lands)

    def body(*refs):
        src_refs, land_refs = refs[:ns], refs[ns:ns + nl]
        send_sems, recv_sems = refs[ns + nl + 1:ns + nl + 3]
        token = refs[2 * (ns + nl) + 3]
        x, y, c = _place()
        for i in range(nl):
            for k, (px, py) in enumerate(_chip_peers(x, y)):
                src, dst = views.send(i, src_refs, land_refs, k, (x, y, c, 2 * x + y, px, py))
                pltpu.make_async_remote_copy(
                    src_ref=src, dst_ref=dst, send_sem=send_sems.at[3 * i + k], recv_sem=recv_sems.at[3 * i + k],
                    device_id=(x, y, 1 - c) if views.to_sibling else (px, py, c), device_id_type=MESH).start()
        token[...] = jnp.zeros_like(token)

    sems = pltpu.SemaphoreType.DMA((3 * nl,))
    both = [pltpu.with_memory_space_constraint(a, pltpu.HBM) for a in (*srcs, *lands)]
    outs = pl.pallas_call(
        body, name=name,
        out_shape=(sems, sems, *[pltpu.HBM(a.shape, a.dtype) for a in both], jax.ShapeDtypeStruct(DEP_SHAPE, F32)),
        in_specs=[HBM] * (ns + nl) + [ANY], out_specs=(SEM, SEM, *[HBM] * (ns + nl), VMEM_FULL),
        input_output_aliases={i: 2 + i for i in range(ns + nl)},
        compiler_params=pltpu.CompilerParams(has_side_effects=pltpu.SideEffectType.DATAFLOW_SIDE_EFFECTING),
    )(*both, after)
    return outs[0], outs[1], outs[2:2 + ns], outs[2 + ns:2 + ns + nl], outs[2 + ns + nl]


def _push_wait(started, views, after, name, with_sources=False):
    send_sems, recv_sems, srcs, lands, _ = started
    ns, nl = len(srcs), len(lands)

    def body(*refs):
        src_refs, land_refs = refs[:ns], refs[ns:ns + nl]
        send_sems, recv_sems = refs[ns + nl:ns + nl + 2]
        x, y, c = _place()
        for i in range(nl):
            for k, (px, py) in enumerate(_chip_peers(x, y)):
                src, dst = views.landing(i, src_refs, land_refs, k, (x, y, c, 2 * x + y, px, py))
                landing = pltpu.make_async_remote_copy(
                    src_ref=src, dst_ref=dst, send_sem=send_sems.at[3 * i + k], recv_sem=recv_sems.at[3 * i + k],
                    device_id=(x, y, 1 - c) if views.to_sibling else (px, py, c), device_id_type=MESH)
                landing.wait_send()
                landing.wait_recv()

    outs = pl.pallas_call(
        body, name=name,
        out_shape=tuple(pltpu.HBM(a.shape, a.dtype) for a in (*srcs, *lands)),
        in_specs=[HBM] * (ns + nl) + [SEM, SEM] + [ANY] * len(after), out_specs=(HBM,) * (ns + nl),
        input_output_aliases={i: i for i in range(ns + nl)},
        compiler_params=pltpu.CompilerParams(has_side_effects=pltpu.SideEffectType.DATAFLOW_SIDE_EFFECTING),
    )(*srcs, *lands, send_sems, recv_sems, *after)
    return outs if with_sources else outs[ns:]


def _empty_lands(shards, slots, own_slot):
    lands = [lax.empty((slots,) + s.shape, s.dtype) for s in shards]
    if own_slot:
        me = 2 * lax.axis_index("x") + lax.axis_index("y")
        lands = [lax.dynamic_update_index_in_dim(z, s, me, 0) for z, s in zip(lands, shards)]
    return lands


def _sibling_swap(arrays, name, other_half=False):
    n = len(arrays)

    def body(*refs):
        ins, outs = refs[:n], refs[n:2 * n]
        send_sems, recv_sems = refs[2 * n:]
        x, y, c = _place()
        copies = []
        for i in range(n):
            src = ins[i]
            if other_half:
                rows = src.shape[1] // 2
                src = src.at[:, pl.ds(pl.multiple_of((1 - c) * rows, BF16_SUBLANES), rows)]
            cp = pltpu.make_async_remote_copy(
                src_ref=src, dst_ref=outs[i], send_sem=send_sems.at[i], recv_sem=recv_sems.at[i],
                device_id=(x, y, 1 - c), device_id_type=MESH)
            cp.start()
            copies.append(cp)
        for cp in copies:
            cp.wait()

    shapes = [(a.shape[0], a.shape[1] // 2, a.shape[2]) if other_half else a.shape for a in arrays]
    return pl.pallas_call(
        body, name=name,
        out_shape=tuple(jax.ShapeDtypeStruct(s, a.dtype) for s, a in zip(shapes, arrays)),
        in_specs=[ANY] * n, out_specs=(ANY,) * n,
        scratch_shapes=[pltpu.SemaphoreType.DMA((n,)), pltpu.SemaphoreType.DMA((n,))],
    )(*arrays)


def _sibling_fill(arrays, name):
    n = len(arrays)

    def body(*refs):
        outs = refs[n:2 * n]
        send_sems, recv_sems = refs[2 * n:]
        x, y, c = _place()
        copies = []
        for i in range(n):
            rows = outs[i].shape[0] // 2
            mine = outs[i].at[pl.ds(pl.multiple_of(c * rows, BF16_SUBLANES), rows)]
            cp = pltpu.make_async_remote_copy(
                src_ref=mine, dst_ref=mine, send_sem=send_sems.at[i], recv_sem=recv_sems.at[i],
                device_id=(x, y, 1 - c), device_id_type=MESH)
            cp.start()
            copies.append(cp)
        for cp in copies:
            cp.wait()

    return pl.pallas_call(
        body, name=name, out_shape=tuple(jax.ShapeDtypeStruct(a.shape, a.dtype) for a in arrays),
        in_specs=[ANY] * n, out_specs=(ANY,) * n, input_output_aliases={i: i for i in range(n)},
        scratch_shapes=[pltpu.SemaphoreType.DMA((n,)), pltpu.SemaphoreType.DMA((n,))],
    )(*arrays)


FOLD_STEPS = 2


def _fold_add(core, parts, theirs, name):
    n = len(parts)
    s, r, cols = parts[0].shape
    tr = r // 2 // FOLD_STEPS

    def body(core_ref, *refs):
        for p_ref, t_ref, o_ref in zip(refs[:n], refs[n:2 * n], refs[2 * n:]):
            o_ref[...] = (p_ref[...].astype(F32) + t_ref[...].astype(F32)).astype(BF16)

    half = pl.BlockSpec((1, tr, cols), lambda j, i, core_ref: (j, i, 0))
    return pl.pallas_call(
        body, name=name, out_shape=tuple(jax.ShapeDtypeStruct((s, r // 2, cols), BF16) for _ in parts),
        grid_spec=pltpu.PrefetchScalarGridSpec(
            num_scalar_prefetch=1, grid=(s, FOLD_STEPS),
            in_specs=[pl.BlockSpec((1, tr, cols), lambda j, i, core_ref: (j, core_ref[0] * FOLD_STEPS + i, 0))] * n + [half] * n,
            out_specs=(half,) * n),
        compiler_params=_params("arbitrary", "arbitrary"),
    )(core, *parts, *theirs)


N_DEV = 8


PACK_COLS = 1024
PACK_ROWS = 24


def _put_row(pack_ref, row, ref):
    if len(ref.shape) == 2:
        pack_ref[row:row + 1, :ref.shape[1]] = ref[...]
    else:
        for h in range(ref.shape[0]):
            pack_ref[row:row + 1, h * HEAD_DIM:(h + 1) * HEAD_DIM] = ref[h]


def _allreduce_small(grads, rows):
    n = len(grads)

    def body(*refs):
        in_ref, out_ref, buf, send_sems, recv_sems = refs[n + 1], refs[n], *refs[n + 2:]
        in_ref[...] = jnp.zeros_like(in_ref)
        for ref, row in zip(refs[:n], rows):
            if len(ref.shape) == 2 and ref.shape[0] > 1:
                in_ref[row:row + ref.shape[0], :ref.shape[1]] = ref[...]
            else:
                _put_row(in_ref, row, ref)
        x, y, c = _place()
        me = 4 * x + 2 * y + c
        buf[me] = in_ref[...]

        def copy(j, slot):
            px, py, pc = x ^ (j >> 2), y ^ ((j >> 1) & 1), c ^ (j & 1)
            return pltpu.make_async_remote_copy(
                src_ref=in_ref, dst_ref=buf.at[slot(px, py, pc)], send_sem=send_sems.at[j], recv_sem=recv_sems.at[j],
                device_id=(px, py, pc), device_id_type=MESH)

        for j in range(1, N_DEV):
            copy(j, lambda px, py, pc: me).start()
        for j in range(1, N_DEV):
            landing = copy(j, lambda px, py, pc: 4 * px + 2 * py + pc)
            landing.wait_send()
            landing.wait_recv()
        acc = buf[0]
        for s in range(1, N_DEV):
            acc = acc + buf[s]
        out_ref[...] = acc

    shape = (PACK_ROWS, PACK_COLS)
    return pl.pallas_call(
        body, name="allreduce_small", out_shape=jax.ShapeDtypeStruct(shape, F32),
        in_specs=[VMEM_FULL] * n, out_specs=VMEM_FULL,
        scratch_shapes=[pltpu.VMEM(shape, F32), pltpu.VMEM((N_DEV,) + shape, F32), pltpu.SemaphoreType.DMA((N_DEV,)),
                        pltpu.SemaphoreType.DMA((N_DEV,))],
    )(*grads)


BF16_SUBLANES = 16


def _reduce_own(me, parts, recvs, dep, steps, name, half=None):
    n = len(parts)
    where = me if half is None else jnp.concatenate([me, half])
    offset = (lambda w: 0) if half is None else (lambda w: w[1] * steps)

    def body(where_ref, *refs):
        for p_ref, rv_ref, o_ref in zip(refs[:n], refs[n:2 * n], refs[2 * n + 1:]):
            acc = p_ref[0].astype(F32)
            for k in range(3):
                acc = acc + rv_ref[k].astype(F32)
            o_ref[...] = acc

    shapes = [(p.shape[1] // steps, p.shape[2]) for p in parts]
    rows = 1 if half is None else 2
    return pl.pallas_call(
        body, name=name, out_shape=tuple(jax.ShapeDtypeStruct((rows * p.shape[1], p.shape[2]), F32) for p in parts),
        grid_spec=pltpu.PrefetchScalarGridSpec(
            num_scalar_prefetch=1, grid=(steps,),
            in_specs=[pl.BlockSpec((1, tr, c), lambda i, w: (w[0], i, 0)) for tr, c in shapes]
            + [pl.BlockSpec((3, tr, c), lambda i, w: (0, i, 0)) for tr, c in shapes] + [ANY],
            out_specs=tuple(pl.BlockSpec((tr, c), lambda i, w: (offset(w) + i, 0)) for tr, c in shapes)),
        compiler_params=_params("arbitrary"),
    )(where, *parts, *recvs, dep)


def _adamw_step(w, g, m, v):
    mn = ADAM_B1 * m + (1.0 - ADAM_B1) * g
    vn = ADAM_B2 * v + (1.0 - ADAM_B2) * (g * g)
    m_hat = mn / (1.0 - ADAM_B1 ** ADAM_STEP)
    v_hat = vn / (1.0 - ADAM_B2 ** ADAM_STEP)
    return -ADAM_LR * (m_hat / (jnp.sqrt(v_hat) + ADAM_EPS) + ADAM_WD * w), mn, vn


def _adamw(ws, gas, gbs, ms, vs, steps, name):
    n = len(ws)
    operands = [ws, gas, ms, vs] if gbs is None else [ws, gas, gbs, ms, vs]
    k = len(operands)

    def body(*refs):
        ins, outs = refs[:k * n], refs[k * n:]
        for j in range(n):
            w_ref, ga_ref, *gb_ref, m_ref, v_ref = ins[j::n]
            g_out, d_out, m_out, v_out = outs[j::n]
            g = ga_ref[...] + gb_ref[0][...] if gb_ref else ga_ref[...]
            g_out[...] = g
            d_out[...], m_out[...], v_out[...] = _adamw_step(w_ref[...], g, m_ref[...], v_ref[...])

    tiles = [pl.BlockSpec((w.shape[0] // steps, w.shape[1]), lambda i: (i, 0)) for w in ws]
    shapes = [jax.ShapeDtypeStruct(w.shape, F32) for w in ws]
    outs = pl.pallas_call(
        body, name=name, grid=(steps,), out_shape=tuple(shapes * 4), in_specs=tiles * k, out_specs=tuple(tiles * 4),
        compiler_params=_params("arbitrary"),
    )(*[a for group in operands for a in group])
    return [outs[j::n] for j in range(n)]


def _adamw_replicated(gsum, ws, ms, vs):
    n = len(ws)

    def body(g_ref, *refs):
        ins, outs = refs[:3 * n], refs[3 * n:]
        for i in range(n):
            w_ref, m_ref, v_ref = ins[i::n]
            shape = w_ref.shape
            if len(shape) == 2:
                g = g_ref[i:i + 1, :shape[1]]
            else:
                g = jnp.concatenate([g_ref[i:i + 1, h * HEAD_DIM:(h + 1) * HEAD_DIM] for h in range(shape[1])], axis=0)[None]
            g_out, d_out, m_out, v_out = outs[i::n]
            g_out[...] = g
            d_out[...], m_out[...], v_out[...] = _adamw_step(w_ref[...], g, m_ref[...], v_ref[...])

    shapes = [jax.ShapeDtypeStruct(w.shape, F32) for w in ws]
    outs = pl.pallas_call(
        body, name="adamw_replicated", out_shape=tuple(shapes * 4),
        in_specs=[VMEM_FULL] * (1 + 3 * n), out_specs=(VMEM_FULL,) * (4 * n),
    )(gsum, *ws, *ms, *vs)
    return [outs[i::n] for i in range(n)]


COL_SHARDED = ("ffn1_w_gate", "ffn1_w_up", "w_in", "ffn2_w_gate", "ffn2_w_up", "w1", "w2", "a1", "a2", "g1", "g2")
ROW_SHARDED = ("ffn1_w_down", "ffn2_w_down", "w_out")
CHUNKED = ("ffn1_w_gate", "ffn1_w_up", "ffn1_w_down", "w_in", "ffn2_w_gate", "ffn2_w_up", "ffn2_w_down")
WEIGHTS = ("ffn1_norm", "ffn1_w_gate", "ffn1_w_up", "ffn1_w_down", "mix_norm", "w_in", "q_norm", "k_norm",
           "mu_r", "mu_k", "mu_v", "mu_w", "mu_a", "mu_g", "w0", "w1", "w2", "a0", "a1", "a2", "g1", "g2",
           "k_k", "k_a", "r_k", "ln_x_w", "ln_x_b", "w_out", "ffn2_norm", "ffn2_w_gate", "ffn2_w_up", "ffn2_w_down")


TRANSPOSED = ("ffn1_w_gate", "ffn1_w_up", "ffn2_w_gate", "ffn2_w_up", "w1", "a1", "g1")


def _shard_2d(name, a):
    return a[0].T if name in TRANSPOSED else a[0]


def _full_from_blocks(name, blocks):
    if name in CHUNKED:
        return blocks
    if name in ROW_SHARDED:
        return blocks.reshape(-1, blocks.shape[-1])
    return blocks.transpose(1, 0, 2).reshape(blocks.shape[1], -1)


def _blocks_from_full(name, full):
    if name in CHUNKED:
        return full
    if name in ROW_SHARDED:
        return full.reshape(N_SHARDS, -1, full.shape[-1])
    return full.reshape(full.shape[0], N_SHARDS, -1).transpose(1, 0, 2)


FFN1_GROUP = ("ffn1_w_gate", "ffn1_w_up", "ffn1_w_down")
MIX_GROUP = ("w_in",) + RWKV_MAT
OUT_GROUP = ("w_out", "ffn2_w_gate", "ffn2_w_up", "ffn2_w_down")
FFN2_GROUP = OUT_GROUP[1:]
LATE_GROUP = ("w_in", "w_out") + RWKV_MAT


class _Exchange:
    def __init__(self, given):
        self.given = given
        first = self._gather_start(FFN1_GROUP, _HalfGatherViews, jnp.zeros(DEP_SHAPE, F32), "gather_ffn1_start")
        self.mix = self._gather_start(MIX_GROUP, _GatherViews, first[4], "gather_mix_start")
        self.out = self._gather_start(OUT_GROUP, _GatherViews, self.mix[4], "gather_out_start")
        self.first_dep = self.out[4]
        halves = _push_wait(first, _HalfGatherViews, (self.first_dep,), "gather_ffn1_wait")
        passed = _push_start([], halves, _ForwardViews, jnp.zeros(DEP_SHAPE, F32), "gather_ffn1_pass_start")
        self.first_weights = self._full(FFN1_GROUP, _push_wait(passed, _ForwardViews, (passed[4],), "gather_ffn1_pass_wait"))
        self.parts, self.recv = {}, {}

    @staticmethod
    def _full(names, blocks):
        out = {}
        for n, b in zip(names, blocks):
            full = _full_from_blocks(n, b)
            out[n] = full.astype(F32) if n in RWKV_MAT else full
        return out

    def _gather_start(self, names, views, after, name):
        after, raw = lax.optimization_barrier((after, [_shard_2d(n, self.given[n]) for n in names]))
        shards = [a.astype(BF16) for a in raw]
        return _push_start(shards, _empty_lands(shards, N_SHARDS, True), views, after, name)

    def mix_weights(self, after):
        return self._full(MIX_GROUP, _push_wait(self.mix, _GatherViews, after, "gather_mix_wait"))

    def out_weights(self, after):
        return self._full(OUT_GROUP, _push_wait(self.out, _GatherViews, after, "gather_out_wait"))

    def _scatter_start(self, grads, name):
        names = tuple(grads)
        parts = [_blocks_from_full(n, grads[n]) for n in names]
        self.parts.update(zip(names, parts))
        lands = [lax.empty((3,) + p.shape[1:], BF16) for p in parts]
        return _push_start([p.astype(BF16) for p in parts], lands, _ScatterViews, jnp.zeros(DEP_SHAPE, F32), name)

    def _scatter_done(self, started, names, after, name):
        outs = _push_wait(started, _ScatterViews, after, name, with_sources=True)
        for n, sent, got in zip(names, outs[:len(names)], outs[len(names):]):
            self.recv[n] = got
            if self.parts[n].dtype == BF16:
                self.parts[n] = sent

    def send_ffn2(self, grads):
        self.ffn2 = self._scatter_start(grads, "scatter_ffn2_start")
        return self.ffn2[4]

    def send_mix(self, grads, after):
        self._scatter_done(self.ffn2, FFN2_GROUP, after, "scatter_ffn2_wait")
        self.late = self._scatter_start(grads, "scatter_late_start")
        return self.late[4]

    def send_ffn1(self, grads):
        self.ffn1 = self._scatter_start(grads, "scatter_ffn1_start")
        return self.ffn1[4]

    def late_received(self, after):
        self._scatter_done(self.late, LATE_GROUP, after, "scatter_late_wait")

    def ffn1_received(self, after):
        self._scatter_done(self.ffn1, FFN1_GROUP, after, "scatter_ffn1_wait")


def kernel(
        x, ffn1_norm, ffn1_w_gate, ffn1_w_up, ffn1_w_down, mix_norm, w_in, q_norm, k_norm, mu_r, mu_k, mu_v, mu_w,
        mu_a, mu_g, w0, w1, w2, a0, a1, a2, g1, g2, k_k, k_a, r_k, ln_x_w, ln_x_b, w_out, ffn2_norm, ffn2_w_gate,
        ffn2_w_up, ffn2_w_down, loss_target, m_ffn1_norm, m_ffn1_w_gate, m_ffn1_w_up, m_ffn1_w_down, m_mix_norm,
        m_w_in, m_q_norm, m_k_norm, m_mu_r, m_mu_k, m_mu_v, m_mu_w, m_mu_a, m_mu_g, m_w0, m_w1, m_w2, m_a0, m_a1,
        m_a2, m_g1, m_g2, m_k_k, m_k_a, m_r_k, m_ln_x_w, m_ln_x_b, m_w_out, m_ffn2_norm, m_ffn2_w_gate, m_ffn2_w_up,
        m_ffn2_w_down, v_ffn1_norm, v_ffn1_w_gate, v_ffn1_w_up, v_ffn1_w_down, v_mix_norm, v_w_in, v_q_norm, v_k_norm,
        v_mu_r, v_mu_k, v_mu_v, v_mu_w, v_mu_a, v_mu_g, v_w0, v_w1, v_w2, v_a0, v_a1, v_a2, v_g1, v_g2, v_k_k, v_k_a,
        v_r_k, v_ln_x_w, v_ln_x_b, v_w_out, v_ffn2_norm, v_ffn2_w_gate, v_ffn2_w_up, v_ffn2_w_down):
    given = dict(locals())
    sharded = COL_SHARDED + ROW_SHARDED
    sharded = tuple(n for n in WEIGHTS if n in sharded)
    small = tuple(n for n in WEIGHTS if n not in sharded)

    ex = _Exchange(given)
    w = {n: given[n] for n in small}
    w.update(ex.first_weights)
    loss, dx, g = _local_step(x[0], loss_target[0], w, ex)

    core = lax.axis_index("c").astype(jnp.int32).reshape(1)
    late = [g[n] for n in FFN1_GROUP]
    folded = _fold_add(core, late, _sibling_swap(late, "fold_swap_ffn1", other_half=True), "fold_add_ffn1")
    dep = ex.send_ffn1(dict(zip(FFN1_GROUP, folded)))

    me = (2 * lax.axis_index("x") + lax.axis_index("y")).astype(jnp.int32).reshape(1)
    out = {}

    def reduced(sub, steps, tag):
        parts = [ex.parts[n].reshape(N_SHARDS, -1, ex.parts[n].shape[-1]) for n in sub]
        recvs = [ex.recv[n].reshape(3, -1, ex.recv[n].shape[-1]) for n in sub]
        return _reduce_own(me, parts, recvs, dep, steps, f"reduce_{tag}")

    def updated(sub, mine, theirs, steps, tag):
        res = _adamw([_shard_2d(n, given[n]) for n in sub], mine, theirs, [_shard_2d(n, given["m_" + n]) for n in sub],
                     [_shard_2d(n, given["v_" + n]) for n in sub], steps, f"adamw_{tag}")
        for n, rs in zip(sub, res):
            out[n] = [(r.T if n in TRANSPOSED else r).reshape(given[n].shape) for r in rs]
        return [out[n][1] for n in sub]

    ex.late_received((dep,))
    rest = tuple(n for n in sharded if n not in FFN1_GROUP)
    large, lora = tuple(n for n in rest if n not in RWKV_MAT), tuple(n for n in rest if n in RWKV_MAT)
    mine = reduced(large, 4, "rest_large")
    mine_lora = reduced(lora, 1, "rest_lora")
    theirs_lora = _sibling_swap(mine_lora, "sibling_swap_rest_lora")
    swap = _push_start(mine, [lax.empty(a.shape, a.dtype) for a in mine], _SiblingViews, theirs_lora[0], "swap_rest_start")
    last = updated(lora, mine_lora, theirs_lora, 1, "rest_lora")

    row = {n: i for i, n in enumerate(small)}
    singles = [n for n in small if n not in RWKV_VEC]
    gsum = _allreduce_small([g[n] for n in singles] + [g["rwkv_vec"], loss],
                            [row[n] for n in singles] + [row[RWKV_VEC[0]], len(small)])
    res = _adamw_replicated(gsum, [given[n] for n in small], [given["m_" + n] for n in small], [given["v_" + n] for n in small])
    for n, rs in zip(small, res):
        out[n] = list(rs)
    total_loss = gsum[len(small), 0]

    both = _push_wait(swap, _SiblingViews, (*last, res[0][1]), "swap_rest_wait", with_sources=True)
    last = updated(large, both[:len(large)], both[len(large):], 8, "rest_large")

    ex.ffn1_received((*last, res[0][1]))
    halves = _reduce_own(me, [ex.parts[n] for n in FFN1_GROUP], [ex.recv[n] for n in FFN1_GROUP],
                         jnp.zeros(DEP_SHAPE, F32), FOLD_STEPS, "reduce_ffn1", half=core)
    grads = _sibling_fill(halves, "sibling_fill_ffn1")
    res = _adamw([_shard_2d(n, given[n]) for n in FFN1_GROUP], grads, None, [_shard_2d(n, given["m_" + n]) for n in FFN1_GROUP],
                 [_shard_2d(n, given["v_" + n]) for n in FFN1_GROUP], 8, "adamw_ffn1")
    for n, rs in zip(FFN1_GROUP, res):
        out[n] = [(r.T if n in TRANSPOSED else r).reshape(given[n].shape) for r in rs]
    return (total_loss, dx[None], *[out[n][0] for n in WEIGHTS], *[out[n][1] for n in WEIGHTS],
            *[out[n][2] for n in WEIGHTS], *[out[n][3] for n in WEIGHTS])
```

```python
import functools

import jax
import jax.numpy as jnp
from jax import lax
from jax.experimental import pallas as pl
from jax.experimental.pallas import tpu as pltpu

F32 = jnp.float32
BF16 = jnp.bfloat16
MESH = pl.DeviceIdType.MESH

RMS_EPS = 1e-6
GN_EPS = 64e-5
NEG_INF = -1e30
FFN_RESIDUAL = 0.5
HEAD_DIM = 64
ATT_BLOCK = 128
DILATIONS = (1, 4, 16)
SCAN_CHUNK = 64
TOKEN_TILE = 256
FFN_BWD_TILE = 512

ADAM_LR = 0.001
ADAM_B1 = 0.9
ADAM_B2 = 0.999
ADAM_EPS = 1e-08
ADAM_WD = 0.01
ADAM_STEP = 10

VMEM_FULL = pl.BlockSpec(memory_space=pltpu.VMEM)
ANY = pl.BlockSpec(memory_space=pl.ANY)


VMEM_LIMIT = 56 * 1024 * 1024


def _params(*sem):
    return pltpu.CompilerParams(dimension_semantics=sem, vmem_limit_bytes=VMEM_LIMIT)


def _dot(a, b, dims):
    return lax.dot_general(a.astype(BF16), b.astype(BF16), (dims, ((), ())), preferred_element_type=F32)


def _dot_nn(a, b):
    return _dot(a, b, ((1,), (0,)))


def _dot_nt(a, b):
    return _dot(a, b, ((1,), (1,)))


def _dot_tn(a, b):
    return _dot(a, b, ((0,), (0,)))


@jax.custom_vjp
def _mm(a, b):
    return _dot_nn(a, b)


def _mm_fwd(a, b):
    return _dot_nn(a, b), (a, b)


def _mm_bwd(res, g):
    a, b = res
    return _dot_nt(g, b).astype(a.dtype), _dot_tn(a, g).astype(b.dtype)


_mm.defvjp(_mm_fwd, _mm_bwd)


@jax.custom_vjp
def _mm_nt(a, bt):
    return _dot_nt(a, bt)


def _mm_nt_fwd(a, bt):
    return _dot_nt(a, bt), (a, bt)


def _mm_nt_bwd(res, g):
    a, bt = res
    return _dot_nn(g, bt).astype(a.dtype), _dot_tn(g, a).astype(bt.dtype)


_mm_nt.defvjp(_mm_nt_fwd, _mm_nt_bwd)


def _bdot(a, b, ca, cb):
    return lax.dot_general(a.astype(BF16), b.astype(BF16), (((ca,), (cb,)), ((0,), (0,))), preferred_element_type=F32)


@jax.custom_vjp
def _bmm_nt(a, b):
    return _bdot(a, b, 2, 2)


def _bmm_nt_fwd(a, b):
    return _bdot(a, b, 2, 2), (a, b)


def _bmm_nt_bwd(res, g):
    a, b = res
    return _bdot(g, b, 2, 1), _bdot(g, a, 1, 1)


_bmm_nt.defvjp(_bmm_nt_fwd, _bmm_nt_bwd)


@jax.custom_vjp
def _bmm_nn(a, b):
    return _bdot(a, b, 2, 1)


def _bmm_nn_fwd(a, b):
    return _bdot(a, b, 2, 1), (a, b)


def _bmm_nn_bwd(res, g):
    a, b = res
    return _bdot(g, b, 2, 2), _bdot(a, g, 1, 1)


_bmm_nn.defvjp(_bmm_nn_fwd, _bmm_nn_bwd)


@jax.custom_vjp
def _bmm_tn(a, b):
    return _bdot(a, b, 1, 1)


def _bmm_tn_fwd(a, b):
    return _bdot(a, b, 1, 1), (a, b)


def _bmm_tn_bwd(res, g):
    a, b = res
    return _bdot(b, g, 2, 2), _bdot(a, g, 2, 1)


_bmm_tn.defvjp(_bmm_tn_fwd, _bmm_tn_bwd)


def _hdot(a, b, ca, cb):
    return lax.dot_general(a, b, (((ca,), (cb,)), ((0,), (0,))), precision=lax.Precision.HIGH, preferred_element_type=F32)


def _sigmoid(x):
    return 1.0 / (1.0 + jnp.exp(-x))


def _rms(x):
    return lax.rsqrt(jnp.mean(x * x, axis=-1, keepdims=True) + RMS_EPS)


def _ffn_fwd(x, norm, wg, wu, wd, dep, name, target=None):
    t, d = x.shape
    nc, fc, _ = wg.shape
    tm = TOKEN_TILE

    def body(x_ref, n_ref, wg_ref, wu_ref, wd_ref, dep_ref, *rest):
        o_ref, g_ref, u_ref = rest[-3:] if target is None else rest[1:4]
        xv = x_ref[...]
        h = (xv * _rms(xv) * n_ref[...]).astype(BF16)
        acc = jnp.zeros((tm, d), F32)
        for c in range(nc):
            g = _dot_nt(h, wg_ref[c])
            u = _dot_nt(h, wu_ref[c])
            g_ref[c] = g.astype(BF16)
            u_ref[c] = u.astype(BF16)
            a = (g * _sigmoid(g) * u).astype(BF16)
            acc = acc + jnp.dot(a, wd_ref[c], preferred_element_type=F32)
        y = xv + FFN_RESIDUAL * acc
        if target is None:
            o_ref[...] = y
        else:
            t_ref, loss_ref = rest[0], rest[4]
            err = y - t_ref[...]
            o_ref[...] = err * (1.0 / d)
            part = 0.5 * jnp.sum(jnp.mean(err * err, axis=-1, keepdims=True), axis=0, keepdims=True)

            @pl.when(pl.program_id(0) == 0)
            def _():
                loss_ref[...] = jnp.zeros_like(loss_ref)

            loss_ref[...] += jnp.broadcast_to(part, loss_ref.shape)

    tile = pl.BlockSpec((tm, d), lambda i: (i, 0))
    hidden = pl.BlockSpec((nc, tm, fc), lambda i: (0, i, 0))
    hshape = jax.ShapeDtypeStruct((nc, t, fc), BF16)
    with_loss = target is not None
    return pl.pallas_call(
        body, name=name, grid=(t // tm,),
        out_shape=(jax.ShapeDtypeStruct((t, d), F32), hshape, hshape) + ((jax.ShapeDtypeStruct((1, 128), F32),) if with_loss else ()),
        in_specs=[tile, pl.BlockSpec((1, d), lambda i: (0, 0)), VMEM_FULL, VMEM_FULL, VMEM_FULL, ANY] + ([tile] if with_loss else []),
        out_specs=(tile, hidden, hidden) + ((pl.BlockSpec((1, 128), lambda i: (0, 0)),) if with_loss else ()),
        compiler_params=_params("arbitrary"),
    )(x, norm, wg, wu, wd, dep, *((target,) if with_loss else ()))


def _rmsnorm_bwd(xv, gain, dh):
    rs = _rms(xv)
    xn = xv * rs
    dxn = dh * gain
    dx = rs * (dxn - xn * jnp.mean(dxn * xn, axis=-1, keepdims=True))
    return dx, jnp.sum(dh * xn, axis=0, keepdims=True)


def _ffn_bwd(x, norm, wg, wu, wd, gate, up, dy, dep, name):
    t, d = x.shape
    nc, fc, _ = wg.shape
    tm = FFN_BWD_TILE
    nt = t // tm

    def body(x_ref, n_ref, wg_ref, wu_ref, wd_ref, g_ref, u_ref, dy_ref, dep_ref, dx_ref, dn_ref, dwg_ref, dwu_ref,
             dwd_ref, dh_ref, ag_ref, au_ref, ad_ref):
        c, i = pl.program_id(0), pl.program_id(1)
        rows = pl.ds(pl.multiple_of(i * tm, tm), tm)
        xv = x_ref[...]
        gain = n_ref[...]
        h = (xv * _rms(xv) * gain).astype(BF16)
        dy = dy_ref[...]
        dyb = (FFN_RESIDUAL * dy).astype(BF16)
        g = g_ref[0].astype(F32)
        u = u_ref[0].astype(F32)
        sg = _sigmoid(g)
        s = g * sg
        a = (s * u).astype(BF16)
        da = _dot_nt(dyb, wd_ref[0])
        dub = (da * s).astype(BF16)
        dgb = (da * u * (sg * (1.0 + g * (1.0 - sg)))).astype(BF16)
        dwd_c = _dot_tn(a, dyb)
        dwg_c = _dot_tn(dgb, h)
        dwu_c = _dot_tn(dub, h)
        dh_c = _dot_nn(dgb, wg_ref[0]) + _dot_nn(dub, wu_ref[0])

        @pl.when(i == 0)
        def _():
            ad_ref[...] = dwd_c
            ag_ref[...] = dwg_c
            au_ref[...] = dwu_c

        @pl.when(i > 0)
        def _():
            ad_ref[...] += dwd_c
            ag_ref[...] += dwg_c
            au_ref[...] += dwu_c

        @pl.when(i == nt - 1)
        def _():
            dwd_ref[0] = ad_ref[...].astype(BF16)
            dwg_ref[0] = ag_ref[...].astype(BF16)
            dwu_ref[0] = au_ref[...].astype(BF16)

        @pl.when(c == 0)
        def _():
            dh_ref[rows, :] = dh_c

        @pl.when(c > 0)
        def _():
            dh_ref[rows, :] += dh_c

        @pl.when(c == nc - 1)
        def _():
            dx, dn = _rmsnorm_bwd(xv, gain, dh_ref[rows, :])
            dx_ref[...] = dx + dy

            @pl.when(i == 0)
            def _():
                dn_ref[...] = dn

            @pl.when(i > 0)
            def _():
                dn_ref[...] += dn

    tile = pl.BlockSpec((tm, d), lambda c, i: (i, 0))
    row = pl.BlockSpec((1, d), lambda c, i: (0, 0))
    wrow = pl.BlockSpec((1, fc, d), lambda c, i: (c, 0, 0), pipeline_mode=pl.Buffered(1))
    hidden = pl.BlockSpec((1, tm, fc), lambda c, i: (c, i, 0))
    last = pl.BlockSpec((tm, d), lambda c, i: (jnp.where(c == nc - 1, i, 0), 0))
    return pl.pallas_call(
        body, name=name, grid=(nc, nt),
        out_shape=(jax.ShapeDtypeStruct((t, d), F32), jax.ShapeDtypeStruct((1, d), F32),
                   jax.ShapeDtypeStruct(wg.shape, BF16), jax.ShapeDtypeStruct(wu.shape, BF16),
                   jax.ShapeDtypeStruct(wd.shape, BF16)),
        in_specs=[tile, row, wrow, wrow, wrow, hidden, hidden, tile, ANY],
        out_specs=(last, row, wrow, wrow, wrow),
        scratch_shapes=[pltpu.VMEM((t, d), F32)] + [pltpu.VMEM((fc, d), F32)] * 3,
        compiler_params=_params("arbitrary", "arbitrary"),
    )(x, norm, wg, wu, wd, gate, up, dy, dep)


def _store_heads(ref, v):
    for h in range(ref.shape[0]):
        ref[h] = v[:, h * HEAD_DIM:(h + 1) * HEAD_DIM]


def _load_heads(ref):
    return jnp.concatenate([ref[h] for h in range(ref.shape[0])], axis=-1)


N_HEAD_GROUPS = 3


def _proj_fwd(x, norm, w, c):
    t, d = x.shape
    nc, _, ncol = w.shape
    nh = c // HEAD_DIM
    tm = TOKEN_TILE
    wide = nc * ncol - N_HEAD_GROUPS * c

    def body(x_ref, n_ref, w_ref, q_ref, k_ref, v_ref, cur_ref):
        xv = x_ref[...]
        h = (xv * _rms(xv) * n_ref[...]).astype(BF16)
        full = jnp.concatenate([jnp.dot(h, w_ref[s], preferred_element_type=F32) for s in range(nc)], axis=1)
        for m, ref in enumerate((q_ref, k_ref, v_ref)):
            _store_heads(ref, full[:, m * c:(m + 1) * c])
        cur_ref[...] = full[:, N_HEAD_GROUPS * c:]

    heads = pl.BlockSpec((nh, tm, HEAD_DIM), lambda i: (0, i, 0))
    hshape = jax.ShapeDtypeStruct((nh, t, HEAD_DIM), F32)
    return pl.pallas_call(
        body, name="proj_fwd", grid=(t // tm,),
        out_shape=(hshape, hshape, hshape, jax.ShapeDtypeStruct((t, wide), F32)),
        in_specs=[pl.BlockSpec((tm, d), lambda i: (i, 0)), pl.BlockSpec((1, d), lambda i: (0, 0)), VMEM_FULL],
        out_specs=(heads, heads, heads, pl.BlockSpec((tm, wide), lambda i: (i, 0))),
        compiler_params=_params("arbitrary"),
    )(x, norm, w)


def _proj_bwd(x, norm, w, dq, dk, dv, dcur, dres):
    t, d = x.shape
    nc, _, ncol = w.shape
    nh = dq.shape[0]
    tm = TOKEN_TILE
    nt = t // tm
    wide = dcur.shape[1]

    def body(x_ref, n_ref, w_ref, dq_ref, dk_ref, dv_ref, dcur_ref, dres_ref, dx_ref, dn_ref, dw_ref, acc_ref):
        i = pl.program_id(0)

        @pl.when(i == 0)
        def _():
            acc_ref[...] = jnp.zeros_like(acc_ref)
            dn_ref[...] = jnp.zeros_like(dn_ref)

        xv = x_ref[...]
        gain = n_ref[...]
        h = (xv * _rms(xv) * gain).astype(BF16)
        dp = jnp.concatenate([_load_heads(dq_ref), _load_heads(dk_ref), _load_heads(dv_ref), dcur_ref[...]], axis=1).astype(BF16)
        dh = jnp.zeros((tm, d), F32)
        for s in range(nc):
            dps = dp[:, s * ncol:(s + 1) * ncol]
            acc_ref[s] += _dot_tn(h, dps)
            dh = dh + _dot_nt(dps, w_ref[s])
        dx, dn = _rmsnorm_bwd(xv, gain, dh)
        dx_ref[...] = dx + dres_ref[...]
        dn_ref[...] += dn

        @pl.when(i == nt - 1)
        def _():
            dw_ref[...] = acc_ref[...].astype(BF16)

    tile = pl.BlockSpec((tm, d), lambda i: (i, 0))
    row = pl.BlockSpec((1, d), lambda i: (0, 0))
    heads = pl.BlockSpec((nh, tm, HEAD_DIM), lambda i: (0, i, 0))
    return pl.pallas_call(
        body, name="proj_bwd", grid=(nt,),
        out_shape=(jax.ShapeDtypeStruct((t, d), F32), jax.ShapeDtypeStruct((1, d), F32),
                   jax.ShapeDtypeStruct(w.shape, BF16)),
        in_specs=[tile, row, VMEM_FULL, heads, heads, heads, pl.BlockSpec((tm, wide), lambda i: (i, 0)), tile],
        out_specs=(tile, row, VMEM_FULL),
        scratch_shapes=[pltpu.VMEM(w.shape, F32)], compiler_params=_params("arbitrary"),
    )(x, norm, w, dq, dk, dv, dcur, dres)


def _mixout_fwd(x, att, opg, gate, w):
    t, d = x.shape
    nh = att.shape[0]
    half = gate.shape[1]
    tm = TOKEN_TILE

    def body(x_ref, att_ref, opg_ref, g_ref, w_ref, o_ref):
        mix = jnp.concatenate([_load_heads(att_ref), _load_heads(opg_ref) * g_ref[...]], axis=-1).astype(BF16)
        o_ref[...] = x_ref[...] + jnp.dot(mix, w_ref[...], preferred_element_type=F32)

    tile = pl.BlockSpec((tm, d), lambda i: (i, 0))
    htile = pl.BlockSpec((tm, half), lambda i: (i, 0))
    heads = pl.BlockSpec((nh, tm, HEAD_DIM), lambda i: (0, i, 0))
    return pl.pallas_call(
        body, name="mixout_fwd", grid=(t // tm,), out_shape=jax.ShapeDtypeStruct((t, d), F32),
        in_specs=[tile, heads, heads, htile, VMEM_FULL], out_specs=tile, compiler_params=_params("arbitrary"),
    )(x, att, opg, gate, w)


def _mixout_bwd(att, opg, gate, w, dy, dep):
    nh, t, _ = att.shape
    half = gate.shape[1]
    d = dy.shape[1]
    tm = TOKEN_TILE

    def body(att_ref, opg_ref, g_ref, w_ref, dy_ref, dep_ref, datt_ref, dopg_ref, dg_ref, dw_ref):
        i = pl.program_id(0)
        opg_v, g_v = _load_heads(opg_ref), g_ref[...]
        mix = jnp.concatenate([_load_heads(att_ref), opg_v * g_v], axis=-1).astype(BF16)
        dyb = dy_ref[...].astype(BF16)
        dmix = _dot_nt(dyb, w_ref[...])
        dw = _dot_tn(mix, dyb)
        _store_heads(datt_ref, dmix[:, :half])
        drw = dmix[:, half:]
        _store_heads(dopg_ref, drw * g_v)
        dg_ref[...] = drw * opg_v

        @pl.when(i == 0)
        def _():
            dw_ref[...] = dw

        @pl.when(i > 0)
        def _():
            dw_ref[...] += dw

    tile = pl.BlockSpec((tm, d), lambda i: (i, 0))
    htile = pl.BlockSpec((tm, half), lambda i: (i, 0))
    heads = pl.BlockSpec((nh, tm, HEAD_DIM), lambda i: (0, i, 0))
    hshape = jax.ShapeDtypeStruct((nh, t, HEAD_DIM), F32)
    return pl.pallas_call(
        body, name="mixout_bwd", grid=(t // tm,),
        out_shape=(hshape, hshape, jax.ShapeDtypeStruct((t, half), F32), jax.ShapeDtypeStruct(w.shape, F32)),
        in_specs=[heads, heads, htile, VMEM_FULL, tile, ANY],
        out_specs=(heads, heads, htile, pl.BlockSpec(w.shape, lambda i: (0, 0))),
        compiler_params=_params("arbitrary"),
    )(att, opg, gate, w, dy, dep)


def _head_norm(x, gain):
    return x * _rms(x) * gain


def _att_pattern(qh, kh, v, nb):
    g, blk, _ = qh.shape
    scale = HEAD_DIM ** -0.5
    qi = lax.broadcasted_iota(jnp.int32, (blk, blk), 0)
    kj = lax.broadcasted_iota(jnp.int32, (blk, blk), 1)
    sc = jnp.where(kj <= qi, _bmm_nt(qh, kh) * scale, NEG_INF)
    top = jnp.max(sc, axis=-1, keepdims=True)
    if nb > 1:
        khp = jnp.concatenate([kh[:1], kh[:-1]], axis=0)
        vp = jnp.concatenate([v[:1], v[:-1]], axis=0)
        has_prev = lax.broadcasted_iota(jnp.int32, (g, 1, 1), 0) % nb != 0
        sp = jnp.where((kj >= qi) & has_prev, _bmm_nt(qh, khp) * scale, NEG_INF)
        top = jnp.maximum(top, jnp.max(sp, axis=-1, keepdims=True))
    m = lax.stop_gradient(top)
    pc = jnp.exp(sc - m)
    den = jnp.sum(pc, axis=-1, keepdims=True)
    acc = _bmm_nn(pc, v)
    if nb > 1:
        pp = jnp.exp(sp - m)
        den = den + jnp.sum(pp, axis=-1, keepdims=True)
        acc = acc + _bmm_nn(pp, vp)
    o = acc / den
    return o, jnp.broadcast_to(m + jnp.log(den), o.shape)


def _pattern_rows(t, dil):
    length = t // dil
    return [pl.ds(r, length, stride=dil) if dil > 1 else pl.ds(0, length) for r in range(dil)], length // ATT_BLOCK


def _take(ref, rows, nb):
    return jnp.concatenate([ref[0, r, :].reshape(nb, ATT_BLOCK, HEAD_DIM) for r in rows], axis=0)


def _put(ref, rows, nb, val):
    for j, r in enumerate(rows):
        ref[0, r, :] = val[j * nb:(j + 1) * nb].reshape(nb * ATT_BLOCK, HEAD_DIM)


def _put_add(ref, rows, nb, val):
    for j, r in enumerate(rows):
        ref[0, r, :] += val[j * nb:(j + 1) * nb].reshape(nb * ATT_BLOCK, HEAD_DIM)


def _merge_fn(o1, o2, o3, l1, l2, l3):
    m = lax.stop_gradient(jnp.maximum(jnp.maximum(l1, l2), l3))
    e1, e2, e3 = jnp.exp(l1 - m), jnp.exp(l2 - m), jnp.exp(l3 - m)
    return (e1 * o1 + e2 * o2 + e3 * o3) / (e1 + e2 + e3)


def _token_rows(j):
    return pl.ds(pl.multiple_of(j * ATT_BLOCK, ATT_BLOCK), ATT_BLOCK)


def _norm_rows(t, q_ref, k_ref, gq, gk, qh_ref, kh_ref):
    def step(j, carry):
        rows = _token_rows(j)
        qh_ref[0, rows, :] = _head_norm(q_ref[0, rows, :], gq[0])
        kh_ref[0, rows, :] = _head_norm(k_ref[0, rows, :], gk[0])
        return carry

    lax.fori_loop(0, t // ATT_BLOCK, step, 0)


def _att_head_specs(t):
    head = pl.BlockSpec((1, t, HEAD_DIM), lambda h: (h, 0, 0))
    gain = pl.BlockSpec((1, 1, HEAD_DIM), lambda h: (0, 0, 0))
    return head, gain


def _att_fwd(q, k, v, qn, kn):
    nh, t, dh = q.shape
    head, gain = _att_head_specs(t)

    def body(q_ref, k_ref, v_ref, qn_ref, kn_ref, att_ref, o1, o2, o3, l1, l2, l3, qh_ref, kh_ref):
        saved = (o1, o2, o3, l1, l2, l3)
        _norm_rows(t, q_ref, k_ref, qn_ref[...], kn_ref[...], qh_ref, kh_ref)
        for p, dil in enumerate(DILATIONS):
            rows, nb = _pattern_rows(t, dil)
            o, lse = _att_pattern(_take(qh_ref, rows, nb), _take(kh_ref, rows, nb), _take(v_ref, rows, nb), nb)
            _put(saved[p], rows, nb, o)
            _put(saved[3 + p], rows, nb, lse)

        def merge(j, carry):
            rows = _token_rows(j)
            att_ref[0, rows, :] = _merge_fn(*[r[0, rows, :] for r in saved])
            return carry

        lax.fori_loop(0, t // ATT_BLOCK, merge, 0)

    return pl.pallas_call(
        body, name="att_fwd", grid=(nh,), out_shape=(jax.ShapeDtypeStruct(q.shape, F32),) * 7,
        in_specs=[head, head, head, gain, gain], out_specs=(head,) * 7,
        scratch_shapes=[pltpu.VMEM((1, t, dh), F32)] * 2, compiler_params=_params("arbitrary"),
    )(q, k, v, qn, kn)


def _att_bwd(q, k, v, qn, kn, saved, datt):
    nh, t, dh = q.shape
    head, gain = _att_head_specs(t)

    def body(q_ref, k_ref, v_ref, qn_ref, kn_ref, o1, o2, o3, l1, l2, l3, datt_ref,
             dq_ref, dk_ref, dv_ref, dqn_ref, dkn_ref, qh_ref, kh_ref, dqh_ref, dkh_ref, *ct_refs):
        for ref in (dqh_ref, dkh_ref, dv_ref):
            ref[...] = jnp.zeros_like(ref)

        @pl.when(pl.program_id(0) == 0)
        def _():
            dqn_ref[...] = jnp.zeros_like(dqn_ref)
            dkn_ref[...] = jnp.zeros_like(dkn_ref)

        gq, gk = qn_ref[...], kn_ref[...]
        _norm_rows(t, q_ref, k_ref, gq, gk, qh_ref, kh_ref)

        def merge_cotangents(j, carry):
            rows = _token_rows(j)
            _, merge_vjp = jax.vjp(_merge_fn, *[r[0, rows, :] for r in (o1, o2, o3, l1, l2, l3)])
            for ref, val in zip(ct_refs, merge_vjp(datt_ref[0, rows, :])):
                ref[0, rows, :] = val
            return carry

        lax.fori_loop(0, t // ATT_BLOCK, merge_cotangents, 0)

        for p, dil in enumerate(DILATIONS):
            rows, nb = _pattern_rows(t, dil)
            _, pattern_vjp = jax.vjp(functools.partial(_att_pattern, nb=nb), _take(qh_ref, rows, nb), _take(kh_ref, rows, nb),
                                     _take(v_ref, rows, nb))
            dqh, dkh, dv = pattern_vjp((_take(ct_refs[p], rows, nb), _take(ct_refs[3 + p], rows, nb)))
            _put_add(dqh_ref, rows, nb, dqh)
            _put_add(dkh_ref, rows, nb, dkh)
            _put_add(dv_ref, rows, nb, dv)

        def norm_cotangents(j, carry):
            rows = _token_rows(j)
            out = []
            for x_ref, gain, dh_ref, dx_ref, acc in ((q_ref, gq, dqh_ref, dq_ref, carry[0]), (k_ref, gk, dkh_ref, dk_ref, carry[1])):
                _, norm_vjp = jax.vjp(_head_norm, x_ref[0, rows, :], gain[0])
                dx, dgain = norm_vjp(dh_ref[0, rows, :])
                dx_ref[0, rows, :] = dx
                out.append(acc + dgain)
            return tuple(out)

        zero = jnp.zeros((1, dh), F32)
        dgq, dgk = lax.fori_loop(0, t // ATT_BLOCK, norm_cotangents, (zero, zero))
        dqn_ref[0] += dgq
        dkn_ref[0] += dgk

    hshape = jax.ShapeDtypeStruct(q.shape, F32)
    gshape = jax.ShapeDtypeStruct((1, 1, dh), F32)
    return pl.pallas_call(
        body, name="att_bwd", grid=(nh,), out_shape=(hshape, hshape, hshape, gshape, gshape),
        in_specs=[head, head, head, gain, gain] + [head] * 7, out_specs=(head, head, head, gain, gain),
        scratch_shapes=[pltpu.VMEM((1, t, dh), F32)] * 10, compiler_params=_params("arbitrary"),
    )(q, k, v, qn, kn, *saved, datt)


RWKV_VEC = ("mu_r", "mu_k", "mu_v", "mu_w", "mu_a", "mu_g", "w0", "a0", "k_k", "k_a")
RWKV_MAT = ("w1", "w2", "a1", "a2", "g1", "g2")


def _rwkv_pre_fn(cur, prev, vec, w1t, w2, a1t, a2, g1t, g2):
    c = cur.shape[1] // 4
    mu_r, mu_k, mu_v, mu_w, mu_a, mu_g, w0, a0, k_k, k_a = (vec[j:j + 1] for j in range(10))

    def lerp(j, mu):
        xc, xp = cur[:, j * c:(j + 1) * c], prev[:, j * c:(j + 1) * c]
        return xc + (xp - xc) * mu

    r, k, v = lerp(0, mu_r), lerp(1, mu_k), lerp(2, mu_v)
    cw, ca, cg = lerp(3, mu_w), lerp(3, mu_a), lerp(3, mu_g)
    z = w0 + _mm(jnp.tanh(_mm_nt(cw, w1t)), w2)
    w_log = jnp.minimum(z, 0.0) - jnp.log(1.0 + jnp.exp(-jnp.abs(z))) - 0.5
    lw = -jnp.exp(w_log)
    a = _sigmoid(a0 + _mm(_mm_nt(ca, a1t), a2))
    gate = _mm(_sigmoid(_mm_nt(cg, g1t)), g2)
    kkraw = k * k_k
    kmod = k * (1.0 + (a - 1.0) * k_a)
    return r, lw, kmod, v, kkraw, a, gate


HALO_ROWS = 8


def _rwkv_pre_specs(c, mats, tile_of):
    tm = TOKEN_TILE
    nh = c // HEAD_DIM
    wide = pl.BlockSpec((tm, 4 * c), lambda j: (tile_of(j), 0))
    halo = pl.BlockSpec((HALO_ROWS, 4 * c), lambda j: (jnp.maximum(tile_of(j) * (tm // HALO_ROWS) - 1, 0), 0))
    one = pl.BlockSpec((tm, c), lambda j: (tile_of(j), 0))
    heads = pl.BlockSpec((nh, tm, HEAD_DIM), lambda j: (0, tile_of(j), 0))
    vec = pl.BlockSpec((10, c), lambda j: (0, 0))
    mspecs = [pl.BlockSpec(m.shape, lambda j: (0, 0)) for m in mats]
    return wide, halo, one, heads, vec, mspecs


def _previous_rows(cur, halo, tile):
    first = jnp.where(tile > 0, halo[HALO_ROWS - 1:HALO_ROWS], 0.0)
    rows = lax.broadcasted_iota(jnp.int32, cur.shape, 0)
    return jnp.where(rows == 0, first, pltpu.roll(cur, 1, axis=0))


def _rwkv_pre_fwd(cur, vec, mats):
    t, c4 = cur.shape
    c = c4 // 4
    wide, halo, one, heads, vspec, mspecs = _rwkv_pre_specs(c, mats, lambda j: j)

    def body(cur_ref, halo_ref, vec_ref, *rest):
        mrefs, outs = rest[:6], rest[6:]
        cur_v = cur_ref[...]
        prev = _previous_rows(cur_v, halo_ref[...], pl.program_id(0))
        vals = _rwkv_pre_fn(cur_v, prev, vec_ref[...], *(m[...] for m in mrefs))
        for ref, val in zip(outs[:6], vals[:6]):
            _store_heads(ref, val)
        outs[6][...] = vals[6]

    hshape = jax.ShapeDtypeStruct((c // HEAD_DIM, t, HEAD_DIM), F32)
    return pl.pallas_call(
        body, name="rwkv_pre_fwd", grid=(t // TOKEN_TILE,), out_shape=(hshape,) * 6 + (jax.ShapeDtypeStruct((t, c), F32),),
        in_specs=[wide, halo, vspec] + mspecs, out_specs=(heads,) * 6 + (one,), compiler_params=_params("arbitrary"),
    )(cur, cur, vec, *mats)


def _rwkv_pre_bwd(cur, vec, mats, cts, dgate):
    t, c4 = cur.shape
    c = c4 // 4
    tm = TOKEN_TILE
    nt = t // tm
    wide, halo, one, heads, vspec, mspecs = _rwkv_pre_specs(c, mats, lambda j: nt - 1 - j)

    def body(cur_ref, halo_ref, vec_ref, *rest):
        mrefs, ctrefs, dgate_ref, outs, carry_ref = rest[:6], rest[6:12], rest[12], rest[13:-1], rest[-1]
        j = pl.program_id(0)

        @pl.when(j == 0)
        def _():
            carry_ref[...] = jnp.zeros_like(carry_ref)
            for ref in outs[1:]:
                ref[...] = jnp.zeros_like(ref)

        cur_v = cur_ref[...]
        prev = _previous_rows(cur_v, halo_ref[...], nt - 1 - j)
        _, vjp = jax.vjp(_rwkv_pre_fn, cur_v, prev, vec_ref[...], *(m[...] for m in mrefs))
        grads = vjp(tuple(_load_heads(r) for r in ctrefs) + (dgate_ref[...],))
        dprev = grads[1]
        rows = lax.broadcasted_iota(jnp.int32, dprev.shape, 0)
        outs[0][...] = grads[0] + jnp.where(rows == tm - 1, carry_ref[0:1], pltpu.roll(dprev, tm - 1, axis=0))
        carry_ref[0:1] = dprev[0:1]
        for ref, val in zip(outs[1:], grads[2:]):
            ref[...] += val

    return pl.pallas_call(
        body, name="rwkv_pre_bwd", grid=(nt,),
        out_shape=(jax.ShapeDtypeStruct(cur.shape, F32), jax.ShapeDtypeStruct(vec.shape, F32))
        + tuple(jax.ShapeDtypeStruct(m.shape, F32) for m in mats),
        in_specs=[wide, halo, vspec] + mspecs + [heads] * 6 + [one], out_specs=(wide, vspec) + tuple(mspecs),
        scratch_shapes=[pltpu.VMEM((HALO_ROWS, c4), F32)], compiler_params=_params("arbitrary"),
    )(cur, cur, vec, *mats, *cts, dgate)


def _scan_chunk_fn(h0, r, lw, k, v, kkraw, a, rk, lnw, lnb):
    n = r.shape[1]
    nrm = jnp.sqrt(jnp.sum(kkraw * kkraw, axis=-1, keepdims=True))
    kk = kkraw / jnp.maximum(nrm, 1e-12)
    av, bv = -kk, kk * a
    ti = lax.broadcasted_iota(jnp.int32, (n, n), 0)
    si = lax.broadcasted_iota(jnp.int32, (n, n), 1)
    incl, strict = ti >= si, ti > si
    ones = jnp.broadcast_to(incl.astype(F32)[None], (r.shape[0], n, n))
    cum = _hdot(ones, lw, 2, 1)
    at, rt = av * jnp.exp(cum - lw), r * jnp.exp(cum)
    inv = jnp.exp(-cum)
    bt, kt = bv * inv, k * inv
    gram = _hdot(jnp.concatenate([at, rt], axis=1), jnp.concatenate([bt, kt], axis=1), 2, 2)
    lab = jnp.where(strict, gram[:, :n, :n], 0.0)
    lak = jnp.where(strict, gram[:, :n, n:], 0.0)
    rb = jnp.where(incl, gram[:, n:, :n], 0.0)
    rkm = jnp.where(incl, gram[:, n:, n:], 0.0)
    nv = v.shape[2]
    u = _bmm_nn(jnp.concatenate([at, lak], axis=2), jnp.concatenate([h0, v], axis=1))
    p = lab
    m = 2
    while m < n:
        both = _bmm_nn(p, jnp.concatenate([u, p], axis=2))
        u, p = u + both[:, :, :nv], both[:, :, nv:]
        m *= 2
    u = u + _bmm_nn(p, u)
    y = _bmm_nn(jnp.concatenate([rt, rb, rkm], axis=2), jnp.concatenate([h0, u, v], axis=1))
    last = jnp.exp(jnp.sum(lw, axis=1, keepdims=True))
    h1 = jnp.swapaxes(last, 1, 2) * (h0 + _bmm_tn(jnp.concatenate([bt, kt], axis=1), jnp.concatenate([u, v], axis=1)))
    mean = jnp.mean(y, axis=-1, keepdims=True)
    yc = y - mean
    var = jnp.mean(yc * yc, axis=-1, keepdims=True)
    yn = yc * lax.rsqrt(var + GN_EPS) * lnw + lnb
    bonus = jnp.sum(r * k * rk, axis=-1, keepdims=True) * v
    return yn + bonus, h1


SCAN_GROUP = 2


def _scan_group_fn(h0, r, lw, k, v, kkraw, a, rk, lnw, lnb):
    outs = []
    for j in range(SCAN_GROUP):
        rows = slice(j * SCAN_CHUNK, (j + 1) * SCAN_CHUNK)
        o, h0 = _scan_chunk_fn(h0, r[:, rows], lw[:, rows], k[:, rows], v[:, rows], kkraw[:, rows], a[:, rows], rk, lnw, lnb)
        outs.append(o)
    return jnp.concatenate(outs, axis=1), h0


def _scan_specs(h, t, dh, rev):
    n = SCAN_CHUNK * SCAN_GROUP
    nc = t // n
    pos = (lambda c: (0, nc - 1 - c, 0)) if rev else (lambda c: (0, c, 0))
    st = (lambda c: (nc - 1 - c, 0, 0, 0)) if rev else (lambda c: (c, 0, 0, 0))
    seq = pl.BlockSpec((h, n, dh), pos)
    par = pl.BlockSpec((h, 1, dh), lambda c: (0, 0, 0))
    state = pl.BlockSpec((1, h, dh, dh), st)
    return seq, par, state


def _scan_fwd(seqs, pars):
    h, t, dh = seqs[0].shape
    nc = t // (SCAN_CHUNK * SCAN_GROUP)
    seq, par, state = _scan_specs(h, t, dh, False)

    def body(r, lw, k, v, kkraw, a, rk, lnw, lnb, o_ref, st_ref, h_ref):
        @pl.when(pl.program_id(0) == 0)
        def _():
            h_ref[...] = jnp.zeros_like(h_ref)

        h0 = h_ref[...]
        st_ref[0] = h0
        o, h1 = _scan_group_fn(h0, r[...], lw[...], k[...], v[...], kkraw[...], a[...], rk[...], lnw[...], lnb[...])
        o_ref[...] = o
        h_ref[...] = h1

    return pl.pallas_call(
        body, name="rwkv_scan_fwd", grid=(nc,),
        out_shape=(jax.ShapeDtypeStruct((h, t, dh), F32), jax.ShapeDtypeStruct((nc, h, dh, dh), F32)),
        in_specs=[seq] * 6 + [par] * 3, out_specs=(seq, state),
        scratch_shapes=[pltpu.VMEM((h, dh, dh), F32)], compiler_params=_params("arbitrary"),
    )(*seqs, *pars)


def _scan_bwd(seqs, pars, states, do):
    h, t, dh = seqs[0].shape
    nc = t // (SCAN_CHUNK * SCAN_GROUP)
    seq, par, state = _scan_specs(h, t, dh, True)

    def body(r, lw, k, v, kkraw, a, rk, lnw, lnb, st_ref, do_ref, *rest):
        douts, dpars, dh_ref = rest[:6], rest[6:9], rest[9]
        first = pl.program_id(0) == 0

        @pl.when(first)
        def _():
            dh_ref[...] = jnp.zeros_like(dh_ref)

        _, vjp = jax.vjp(_scan_group_fn, st_ref[0], r[...], lw[...], k[...], v[...], kkraw[...], a[...],
                         rk[...], lnw[...], lnb[...])
        grads = vjp((do_ref[...], dh_ref[...]))
        dh_ref[...] = grads[0]
        for ref, val in zip(douts, grads[1:7]):
            ref[...] = val

        @pl.when(first)
        def _():
            for ref, val in zip(dpars, grads[7:]):
                ref[...] = val

        @pl.when(jnp.logical_not(first))
        def _():
            for ref, val in zip(dpars, grads[7:]):
                ref[...] += val

    sshape = jax.ShapeDtypeStruct((h, t, dh), F32)
    pshape = jax.ShapeDtypeStruct((h, 1, dh), F32)
    return pl.pallas_call(
        body, name="rwkv_scan_bwd", grid=(nc,), out_shape=(sshape,) * 6 + (pshape,) * 3,
        in_specs=[seq] * 6 + [par] * 3 + [state, seq], out_specs=(seq,) * 6 + (par,) * 3,
        scratch_shapes=[pltpu.VMEM((h, dh, dh), F32)], compiler_params=_params("arbitrary"),
    )(*seqs, *pars, states, do)


def _local_step(x, target, w, ex):
    w = dict(w)
    c = w["mu_r"].shape[-1]
    qn, kn = w["q_norm"].reshape(1, 1, HEAD_DIM), w["k_norm"].reshape(1, 1, HEAD_DIM)
    vec = jnp.concatenate([w[n].reshape(1, c) for n in RWKV_VEC], axis=0)
    pars = [w[n].reshape(-1, 1, HEAD_DIM) for n in ("r_k", "ln_x_w", "ln_x_b")]
    no_dep = jnp.zeros(DEP_SHAPE, F32)

    x1, gate1, up1 = _ffn_fwd(x, w["ffn1_norm"], w["ffn1_w_gate"], w["ffn1_w_up"], w["ffn1_w_down"], ex.first_dep, "ffn1_fwd")
    w.update(ex.mix_weights((x1,)))
    mats = [w[n] for n in RWKV_MAT]
    q, k, v, cur = _proj_fwd(x1, w["mix_norm"], w["w_in"], c)
    att, *saved = _att_fwd(q, k, v, qn, kn)
    pre = _rwkv_pre_fwd(cur, vec, mats)
    seqs, gate = pre[:6], pre[6]
    opg, states = _scan_fwd(seqs, pars)
    w.update(ex.out_weights((att, opg)))
    x2 = _mixout_fwd(x1, att, opg, gate, w["w_out"])
    dy, gate2, up2, loss = _ffn_fwd(x2, w["ffn2_norm"], w["ffn2_w_gate"], w["ffn2_w_up"], w["ffn2_w_down"], no_dep, "ffn2_fwd",
                                    target=target)

    g = {}
    dx2, g["ffn2_norm"], g["ffn2_w_gate"], g["ffn2_w_up"], g["ffn2_w_down"] = _ffn_bwd(
        x2, w["ffn2_norm"], w["ffn2_w_gate"], w["ffn2_w_up"], w["ffn2_w_down"], gate2, up2, dy, no_dep, "ffn2_bwd")
    dep = ex.send_ffn2({n: g[n] for n in ("ffn2_w_gate", "ffn2_w_up", "ffn2_w_down")})
    datt, dopg, dgate, g["w_out"] = _mixout_bwd(att, opg, gate, w["w_out"], dx2, dep)
    dscan = _scan_bwd(seqs, pars, states, dopg)
    for n, d in zip(("r_k", "ln_x_w", "ln_x_b"), dscan[6:]):
        g[n] = d
    dcur, dvec, *dmats = _rwkv_pre_bwd(cur, vec, mats, dscan[:6], dgate)
    for n, d in zip(RWKV_MAT, dmats):
        g[n] = d
    g["rwkv_vec"] = dvec
    dq, dk, dv, g["q_norm"], g["k_norm"] = _att_bwd(q, k, v, qn, kn, saved, datt)
    dx1, g["mix_norm"], g["w_in"] = _proj_bwd(x1, w["mix_norm"], w["w_in"], dq, dk, dv, dcur, dx2)
    dep = ex.send_mix({n: g[n] for n in ("w_in", "w_out") + RWKV_MAT}, (dx1,))
    dx, g["ffn1_norm"], g["ffn1_w_gate"], g["ffn1_w_up"], g["ffn1_w_down"] = _ffn_bwd(
        x, w["ffn1_norm"], w["ffn1_w_gate"], w["ffn1_w_up"], w["ffn1_w_down"], gate1, up1, dx1, dep, "ffn1_bwd")
    return loss, dx, g


N_SHARDS = 4


def _place():
    return lax.axis_index("x"), lax.axis_index("y"), lax.axis_index("c")


def _chip_peers(x, y):
    return [(1 - x, y), (x, 1 - y), (1 - x, 1 - y)]


HBM = pl.BlockSpec(memory_space=pltpu.HBM)
SEM = pl.BlockSpec(memory_space=pltpu.SEMAPHORE)
DEP_SHAPE = (8, 128)


class _Views:
    to_sibling = False


class _GatherViews(_Views):
    @staticmethod
    def send(i, srcs, lands, k, at):
        return srcs[i], lands[i].at[at[3]]

    @staticmethod
    def landing(i, srcs, lands, k, at):
        return srcs[i], lands[i].at[2 * at[4] + at[5]]


class _ScatterViews(_Views):
    @staticmethod
    def send(i, srcs, lands, k, at):
        return srcs[i].at[2 * at[4] + at[5]], lands[i].at[k]

    @staticmethod
    def landing(i, srcs, lands, k, at):
        return srcs[i].at[at[3]], lands[i].at[k]


def _half_rows(ref, slot, half):
    rows = ref.shape[1] // 2
    return ref.at[slot, pl.ds(pl.multiple_of(half * rows, BF16_SUBLANES), rows)]


class _HalfGatherViews(_Views):
    @staticmethod
    def send(i, srcs, lands, k, at):
        rows = srcs[i].shape[0] // 2
        return srcs[i].at[pl.ds(pl.multiple_of(at[2] * rows, BF16_SUBLANES), rows)], _half_rows(lands[i], at[3], at[2])

    @staticmethod
    def landing(i, srcs, lands, k, at):
        rows = srcs[i].shape[0] // 2
        return srcs[i].at[pl.ds(pl.multiple_of(at[2] * rows, BF16_SUBLANES), rows)], _half_rows(lands[i], 2 * at[4] + at[5], at[2])


class _ForwardViews(_Views):
    to_sibling = True

    @staticmethod
    def send(i, srcs, lands, k, at):
        mine = _half_rows(lands[i], 2 * at[4] + at[5], at[2])
        return mine, mine

    @staticmethod
    def landing(i, srcs, lands, k, at):
        theirs = _half_rows(lands[i], 2 * at[4] + at[5], 1 - at[2])
        return theirs, theirs


class _SiblingViews(_Views):
    to_sibling = True

    @staticmethod
    def _block(ref, k):
        size = -(-ref.shape[0] // 3 // BF16_SUBLANES) * BF16_SUBLANES
        return ref.at[pl.ds(k * size, min(size, ref.shape[0] - k * size))]

    @classmethod
    def send(cls, i, srcs, lands, k, at):
        return cls._block(srcs[i], k), cls._block(lands[i], k)

    landing = send


class _FoldViews(_Views):
    to_sibling = True

    @staticmethod
    def _blocks(ref, k, offset):
        size = -(-ref.shape[1] // 3 // BF16_SUBLANES) * BF16_SUBLANES
        start = offset + k * size
        start = start if isinstance(start, int) else pl.multiple_of(start, BF16_SUBLANES)
        return pl.ds(start, min(size, ref.shape[1] - k * size))

    @classmethod
    def send(cls, i, srcs, lands, k, at):
        rows = cls._blocks(lands[i], k, 0)
        theirs = cls._blocks(lands[i], k, (1 - at[2]) * lands[i].shape[1])
        return srcs[i].at[:, theirs], lands[i].at[:, rows]

    landing = send


def _push_start(srcs, lands, views, after, name):
    ns, nl = len(srcs), len(lands)

    def body(*refs):
        src_refs, land_refs = refs[:ns], refs[ns:ns + nl]
        send_sems, recv_sems = refs[ns + nl + 1:ns + nl + 3]
        token = refs[2 * (ns + nl) + 3]
        x, y, c = _place()
        for i in range(nl):
            for k, (px, py) in enumerate(_chip_peers(x, y)):
                src, dst = views.send(i, src_refs, land_refs, k, (x, y, c, 2 * x + y, px, py))
                pltpu.make_async_remote_copy(
                    src_ref=src, dst_ref=dst, send_sem=send_sems.at[3 * i + k], recv_sem=recv_sems.at[3 * i + k],
                    device_id=(x, y, 1 - c) if views.to_sibling else (px, py, c), device_id_type=MESH).start()
        token[...] = jnp.zeros_like(token)

    sems = pltpu.SemaphoreType.DMA((3 * nl,))
    both = [pltpu.with_memory_space_constraint(a, pltpu.HBM) for a in (*srcs, *lands)]
    outs = pl.pallas_call(
        body, name=name,
        out_shape=(sems, sems, *[pltpu.HBM(a.shape, a.dtype) for a in both], jax.ShapeDtypeStruct(DEP_SHAPE, F32)),
        in_specs=[HBM] * (ns + nl) + [ANY], out_specs=(SEM, SEM, *[HBM] * (ns + nl), VMEM_FULL),
        input_output_aliases={i: 2 + i for i in range(ns + nl)},
        compiler_params=pltpu.CompilerParams(has_side_effects=pltpu.SideEffectType.DATAFLOW_SIDE_EFFECTING),
    )(*both, after)
    return outs[0], outs[1], outs[2:2 + ns], outs[2 + ns:2 + ns + nl], outs[2 + ns + nl]


def _push_wait(started, views, after, name, with_sources=False):
    send_sems, recv_sems, srcs, lands, _ = started
    ns, nl = len(srcs), len(lands)

    def body(*refs):
        src_refs, land_refs = refs[:ns], refs[ns:ns + nl]
        send_sems, recv_sems = refs[ns + nl:ns + nl + 2]
        x, y, c = _place()
        for i in range(nl):
            for k, (px, py) in enumerate(_chip_peers(x, y)):
                src, dst = views.landing(i, src_refs, land_refs, k, (x, y, c, 2 * x + y, px, py))
                landing = pltpu.make_async_remote_copy(
                    src_ref=src, dst_ref=dst, send_sem=send_sems.at[3 * i + k], recv_sem=recv_sems.at[3 * i + k],
                    device_id=(x, y, 1 - c) if views.to_sibling else (px, py, c), device_id_type=MESH)
                landing.wait_send()
                landing.wait_recv()

    outs = pl.pallas_call(
        body, name=name,
        out_shape=tuple(pltpu.HBM(a.shape, a.dtype) for a in (*srcs, *lands)),
        in_specs=[HBM] * (ns + nl) + [SEM, SEM] + [ANY] * len(after), out_specs=(HBM,) * (ns + nl),
        input_output_aliases={i: i for i in range(ns + nl)},
        compiler_params=pltpu.CompilerParams(has_side_effects=pltpu.SideEffectType.DATAFLOW_SIDE_EFFECTING),
    )(*srcs, *lands, send_sems, recv_sems, *after)
    return outs if with_sources else outs[ns:]


def _empty_lands(shards, slots, own_slot):
    lands = [lax.empty((slots,) + s.shape, s.dtype) for s in shards]
    if own_slot:
        me = 2 * lax.axis_index("x") + lax.axis_index("y")
        lands = [lax.dynamic_update_index_in_dim(z, s, me, 0) for z, s in zip(lands, shards)]
    return lands


def _sibling_swap(arrays, name):
    n = len(arrays)

    def body(*refs):
        ins, outs = refs[:n], refs[n:2 * n]
        send_sems, recv_sems = refs[2 * n:]
        x, y, c = _place()
        copies = []
        for i in range(n):
            cp = pltpu.make_async_remote_copy(
                src_ref=ins[i], dst_ref=outs[i], send_sem=send_sems.at[i], recv_sem=recv_sems.at[i],
                device_id=(x, y, 1 - c), device_id_type=MESH)
            cp.start()
            copies.append(cp)
        for cp in copies:
            cp.wait()

    return pl.pallas_call(
        body, name=name,
        out_shape=tuple(jax.ShapeDtypeStruct(a.shape, a.dtype) for a in arrays),
        in_specs=[ANY] * n, out_specs=(ANY,) * n,
        scratch_shapes=[pltpu.SemaphoreType.DMA((n,)), pltpu.SemaphoreType.DMA((n,))],
    )(*arrays)


def _sibling_fill(arrays, name):
    n = len(arrays)

    def body(*refs):
        outs = refs[n:2 * n]
        send_sems, recv_sems = refs[2 * n:]
        x, y, c = _place()
        copies = []
        for i in range(n):
            rows = outs[i].shape[0] // 2
            mine = outs[i].at[pl.ds(pl.multiple_of(c * rows, BF16_SUBLANES), rows)]
            cp = pltpu.make_async_remote_copy(
                src_ref=mine, dst_ref=mine, send_sem=send_sems.at[i], recv_sem=recv_sems.at[i],
                device_id=(x, y, 1 - c), device_id_type=MESH)
            cp.start()
            copies.append(cp)
        for cp in copies:
            cp.wait()

    return pl.pallas_call(
        body, name=name, out_shape=tuple(jax.ShapeDtypeStruct(a.shape, a.dtype) for a in arrays),
        in_specs=[ANY] * n, out_specs=(ANY,) * n, input_output_aliases={i: i for i in range(n)},
        scratch_shapes=[pltpu.SemaphoreType.DMA((n,)), pltpu.SemaphoreType.DMA((n,))],
    )(*arrays)


FOLD_STEPS = 2


def _fold_add(core, parts, theirs, name):
    n = len(parts)
    s, r, cols = parts[0].shape
    tr = r // 2 // FOLD_STEPS

    def body(core_ref, *refs):
        for p_ref, t_ref, o_ref in zip(refs[:n], refs[n:2 * n], refs[2 * n:]):
            o_ref[...] = (p_ref[...].astype(F32) + t_ref[...].astype(F32)).astype(BF16)

    half = pl.BlockSpec((1, tr, cols), lambda j, i, core_ref: (j, i, 0))
    return pl.pallas_call(
        body, name=name, out_shape=tuple(jax.ShapeDtypeStruct((s, r // 2, cols), BF16) for _ in parts),
        grid_spec=pltpu.PrefetchScalarGridSpec(
            num_scalar_prefetch=1, grid=(s, FOLD_STEPS),
            in_specs=[pl.BlockSpec((1, tr, cols), lambda j, i, core_ref: (j, core_ref[0] * FOLD_STEPS + i, 0))] * n + [half] * n,
            out_specs=(half,) * n),
        compiler_params=_params("arbitrary", "arbitrary"),
    )(core, *parts, *theirs)


N_DEV = 8


PACK_COLS = 1024
PACK_ROWS = 24


def _put_row(pack_ref, row, ref):
    if len(ref.shape) == 2:
        pack_ref[row:row + 1, :ref.shape[1]] = ref[...]
    else:
        for h in range(ref.shape[0]):
            pack_ref[row:row + 1, h * HEAD_DIM:(h + 1) * HEAD_DIM] = ref[h]


def _allreduce_small(grads, rows):
    n = len(grads)

    def body(*refs):
        in_ref, out_ref, buf, send_sems, recv_sems = refs[n + 1], refs[n], *refs[n + 2:]
        in_ref[...] = jnp.zeros_like(in_ref)
        for ref, row in zip(refs[:n], rows):
            if len(ref.shape) == 2 and ref.shape[0] > 1:
                in_ref[row:row + ref.shape[0], :ref.shape[1]] = ref[...]
            else:
                _put_row(in_ref, row, ref)
        x, y, c = _place()
        me = 4 * x + 2 * y + c
        buf[me] = in_ref[...]

        def copy(j, slot):
            px, py, pc = x ^ (j >> 2), y ^ ((j >> 1) & 1), c ^ (j & 1)
            return pltpu.make_async_remote_copy(
                src_ref=in_ref, dst_ref=buf.at[slot(px, py, pc)], send_sem=send_sems.at[j], recv_sem=recv_sems.at[j],
                device_id=(px, py, pc), device_id_type=MESH)

        for j in range(1, N_DEV):
            copy(j, lambda px, py, pc: me).start()
        for j in range(1, N_DEV):
            landing = copy(j, lambda px, py, pc: 4 * px + 2 * py + pc)
            landing.wait_send()
            landing.wait_recv()
        acc = buf[0]
        for s in range(1, N_DEV):
            acc = acc + buf[s]
        out_ref[...] = acc

    shape = (PACK_ROWS, PACK_COLS)
    return pl.pallas_call(
        body, name="allreduce_small", out_shape=jax.ShapeDtypeStruct(shape, F32),
        in_specs=[VMEM_FULL] * n, out_specs=VMEM_FULL,
        scratch_shapes=[pltpu.VMEM(shape, F32), pltpu.VMEM((N_DEV,) + shape, F32), pltpu.SemaphoreType.DMA((N_DEV,)),
                        pltpu.SemaphoreType.DMA((N_DEV,))],
    )(*grads)


BF16_SUBLANES = 16


def _reduce_own(me, parts, recvs, dep, steps, name, half=None):
    n = len(parts)
    where = me if half is None else jnp.concatenate([me, half])
    offset = (lambda w: 0) if half is None else (lambda w: w[1] * steps)

    def body(where_ref, *refs):
        for p_ref, rv_ref, o_ref in zip(refs[:n], refs[n:2 * n], refs[2 * n + 1:]):
            acc = p_ref[0].astype(F32)
            for k in range(3):
                acc = acc + rv_ref[k].astype(F32)
            o_ref[...] = acc

    shapes = [(p.shape[1] // steps, p.shape[2]) for p in parts]
    rows = 1 if half is None else 2
    return pl.pallas_call(
        body, name=name, out_shape=tuple(jax.ShapeDtypeStruct((rows * p.shape[1], p.shape[2]), F32) for p in parts),
        grid_spec=pltpu.PrefetchScalarGridSpec(
            num_scalar_prefetch=1, grid=(steps,),
            in_specs=[pl.BlockSpec((1, tr, c), lambda i, w: (w[0], i, 0)) for tr, c in shapes]
            + [pl.BlockSpec((3, tr, c), lambda i, w: (0, i, 0)) for tr, c in shapes] + [ANY],
            out_specs=tuple(pl.BlockSpec((tr, c), lambda i, w: (offset(w) + i, 0)) for tr, c in shapes)),
        compiler_params=_params("arbitrary"),
    )(where, *parts, *recvs, dep)


def _adamw_step(w, g, m, v):
    mn = ADAM_B1 * m + (1.0 - ADAM_B1) * g
    vn = ADAM_B2 * v + (1.0 - ADAM_B2) * (g * g)
    m_hat = mn / (1.0 - ADAM_B1 ** ADAM_STEP)
    v_hat = vn / (1.0 - ADAM_B2 ** ADAM_STEP)
    return -ADAM_LR * (m_hat / (jnp.sqrt(v_hat) + ADAM_EPS) + ADAM_WD * w), mn, vn


def _adamw(ws, gas, gbs, ms, vs, steps, name):
    n = len(ws)
    operands = [ws, gas, ms, vs] if gbs is None else [ws, gas, gbs, ms, vs]
    k = len(operands)

    def body(*refs):
        ins, outs = refs[:k * n], refs[k * n:]
        for j in range(n):
            w_ref, ga_ref, *gb_ref, m_ref, v_ref = ins[j::n]
            g_out, d_out, m_out, v_out = outs[j::n]
            g = ga_ref[...] + gb_ref[0][...] if gb_ref else ga_ref[...]
            g_out[...] = g
            d_out[...], m_out[...], v_out[...] = _adamw_step(w_ref[...], g, m_ref[...], v_ref[...])

    tiles = [pl.BlockSpec((w.shape[0] // steps, w.shape[1]), lambda i: (i, 0)) for w in ws]
    shapes = [jax.ShapeDtypeStruct(w.shape, F32) for w in ws]
    outs = pl.pallas_call(
        body, name=name, grid=(steps,), out_shape=tuple(shapes * 4), in_specs=tiles * k, out_specs=tuple(tiles * 4),
        compiler_params=_params("arbitrary"),
    )(*[a for group in operands for a in group])
    return [outs[j::n] for j in range(n)]


def _adamw_replicated(gsum, ws, ms, vs):
    n = len(ws)

    def body(g_ref, *refs):
        ins, outs = refs[:3 * n], refs[3 * n:]
        for i in range(n):
            w_ref, m_ref, v_ref = ins[i::n]
            shape = w_ref.shape
            if len(shape) == 2:
                g = g_ref[i:i + 1, :shape[1]]
            else:
                g = jnp.concatenate([g_ref[i:i + 1, h * HEAD_DIM:(h + 1) * HEAD_DIM] for h in range(shape[1])], axis=0)[None]
            g_out, d_out, m_out, v_out = outs[i::n]
            g_out[...] = g
            d_out[...], m_out[...], v_out[...] = _adamw_step(w_ref[...], g, m_ref[...], v_ref[...])

    shapes = [jax.ShapeDtypeStruct(w.shape, F32) for w in ws]
    outs = pl.pallas_call(
        body, name="adamw_replicated", out_shape=tuple(shapes * 4),
        in_specs=[VMEM_FULL] * (1 + 3 * n), out_specs=(VMEM_FULL,) * (4 * n),
    )(gsum, *ws, *ms, *vs)
    return [outs[i::n] for i in range(n)]


COL_SHARDED = ("ffn1_w_gate", "ffn1_w_up", "w_in", "ffn2_w_gate", "ffn2_w_up", "w1", "w2", "a1", "a2", "g1", "g2")
ROW_SHARDED = ("ffn1_w_down", "ffn2_w_down", "w_out")
CHUNKED = ("ffn1_w_gate", "ffn1_w_up", "ffn1_w_down", "w_in", "ffn2_w_gate", "ffn2_w_up", "ffn2_w_down")
WEIGHTS = ("ffn1_norm", "ffn1_w_gate", "ffn1_w_up", "ffn1_w_down", "mix_norm", "w_in", "q_norm", "k_norm",
           "mu_r", "mu_k", "mu_v", "mu_w", "mu_a", "mu_g", "w0", "w1", "w2", "a0", "a1", "a2", "g1", "g2",
           "k_k", "k_a", "r_k", "ln_x_w", "ln_x_b", "w_out", "ffn2_norm", "ffn2_w_gate", "ffn2_w_up", "ffn2_w_down")


TRANSPOSED = ("ffn1_w_gate", "ffn1_w_up", "ffn2_w_gate", "ffn2_w_up", "w1", "a1", "g1")


def _shard_2d(name, a):
    return a[0].T if name in TRANSPOSED else a[0]


def _full_from_blocks(name, blocks):
    if name in CHUNKED:
        return blocks
    if name in ROW_SHARDED:
        return blocks.reshape(-1, blocks.shape[-1])
    return blocks.transpose(1, 0, 2).reshape(blocks.shape[1], -1)


def _blocks_from_full(name, full):
    if name in CHUNKED:
        return full
    if name in ROW_SHARDED:
        return full.reshape(N_SHARDS, -1, full.shape[-1])
    return full.reshape(full.shape[0], N_SHARDS, -1).transpose(1, 0, 2)


FFN1_GROUP = ("ffn1_w_gate", "ffn1_w_up", "ffn1_w_down")
MIX_GROUP = ("w_in",) + RWKV_MAT
OUT_GROUP = ("w_out", "ffn2_w_gate", "ffn2_w_up", "ffn2_w_down")
FFN2_GROUP = OUT_GROUP[1:]
LATE_GROUP = ("w_in", "w_out") + RWKV_MAT


class _Exchange:
    def __init__(self, given):
        self.given = given
        first = self._gather_start(FFN1_GROUP, _HalfGatherViews, jnp.zeros(DEP_SHAPE, F32), "gather_ffn1_start")
        self.mix = self._gather_start(MIX_GROUP, _GatherViews, first[4], "gather_mix_start")
        self.out = self._gather_start(OUT_GROUP, _GatherViews, self.mix[4], "gather_out_start")
        self.first_dep = self.out[4]
        halves = _push_wait(first, _HalfGatherViews, (self.first_dep,), "gather_ffn1_wait")
        passed = _push_start([], halves, _ForwardViews, jnp.zeros(DEP_SHAPE, F32), "gather_ffn1_pass_start")
        self.first_weights = self._full(FFN1_GROUP, _push_wait(passed, _ForwardViews, (passed[4],), "gather_ffn1_pass_wait"))
        self.parts, self.recv = {}, {}

    @staticmethod
    def _full(names, blocks):
        out = {}
        for n, b in zip(names, blocks):
            full = _full_from_blocks(n, b)
            out[n] = full.astype(F32) if n in RWKV_MAT else full
        return out

    def _gather_start(self, names, views, after, name):
        after, raw = lax.optimization_barrier((after, [_shard_2d(n, self.given[n]) for n in names]))
        shards = [a.astype(BF16) for a in raw]
        return _push_start(shards, _empty_lands(shards, N_SHARDS, True), views, after, name)

    def mix_weights(self, after):
        return self._full(MIX_GROUP, _push_wait(self.mix, _GatherViews, after, "gather_mix_wait"))

    def out_weights(self, after):
        return self._full(OUT_GROUP, _push_wait(self.out, _GatherViews, after, "gather_out_wait"))

    def _scatter_start(self, grads, name):
        names = tuple(grads)
        parts = [_blocks_from_full(n, grads[n]) for n in names]
        self.parts.update(zip(names, parts))
        lands = [lax.empty((3,) + p.shape[1:], BF16) for p in parts]
        return _push_start([p.astype(BF16) for p in parts], lands, _ScatterViews, jnp.zeros(DEP_SHAPE, F32), name)

    def _scatter_done(self, started, names, after, name):
        outs = _push_wait(started, _ScatterViews, after, name, with_sources=True)
        for n, sent, got in zip(names, outs[:len(names)], outs[len(names):]):
            self.recv[n] = got
            if self.parts[n].dtype == BF16:
                self.parts[n] = sent

    def send_ffn2(self, grads):
        self.ffn2 = self._scatter_start(grads, "scatter_ffn2_start")
        return self.ffn2[4]

    def send_mix(self, grads, after):
        self._scatter_done(self.ffn2, FFN2_GROUP, after, "scatter_ffn2_wait")
        self.late = self._scatter_start(grads, "scatter_late_start")
        return self.late[4]

    def send_ffn1(self, grads):
        self.ffn1 = self._scatter_start(grads, "scatter_ffn1_start")
        return self.ffn1[4]

    def late_received(self, after):
        self._scatter_done(self.late, LATE_GROUP, after, "scatter_late_wait")

    def ffn1_received(self, after):
        self._scatter_done(self.ffn1, FFN1_GROUP, after, "scatter_ffn1_wait")


def kernel(
        x, ffn1_norm, ffn1_w_gate, ffn1_w_up, ffn1_w_down, mix_norm, w_in, q_norm, k_norm, mu_r, mu_k, mu_v, mu_w,
        mu_a, mu_g, w0, w1, w2, a0, a1, a2, g1, g2, k_k, k_a, r_k, ln_x_w, ln_x_b, w_out, ffn2_norm, ffn2_w_gate,
        ffn2_w_up, ffn2_w_down, loss_target, m_ffn1_norm, m_ffn1_w_gate, m_ffn1_w_up, m_ffn1_w_down, m_mix_norm,
        m_w_in, m_q_norm, m_k_norm, m_mu_r, m_mu_k, m_mu_v, m_mu_w, m_mu_a, m_mu_g, m_w0, m_w1, m_w2, m_a0, m_a1,
        m_a2, m_g1, m_g2, m_k_k, m_k_a, m_r_k, m_ln_x_w, m_ln_x_b, m_w_out, m_ffn2_norm, m_ffn2_w_gate, m_ffn2_w_up,
        m_ffn2_w_down, v_ffn1_norm, v_ffn1_w_gate, v_ffn1_w_up, v_ffn1_w_down, v_mix_norm, v_w_in, v_q_norm, v_k_norm,
        v_mu_r, v_mu_k, v_mu_v, v_mu_w, v_mu_a, v_mu_g, v_w0, v_w1, v_w2, v_a0, v_a1, v_a2, v_g1, v_g2, v_k_k, v_k_a,
        v_r_k, v_ln_x_w, v_ln_x_b, v_w_out, v_ffn2_norm, v_ffn2_w_gate, v_ffn2_w_up, v_ffn2_w_down):
    given = dict(locals())
    sharded = COL_SHARDED + ROW_SHARDED
    sharded = tuple(n for n in WEIGHTS if n in sharded)
    small = tuple(n for n in WEIGHTS if n not in sharded)

    ex = _Exchange(given)
    w = {n: given[n] for n in small}
    w.update(ex.first_weights)
    loss, dx, g = _local_step(x[0], loss_target[0], w, ex)

    core = lax.axis_index("c").astype(jnp.int32).reshape(1)
    late = [g[n] for n in FFN1_GROUP]
    fold = _push_start(late, [lax.empty((a.shape[0], a.shape[1] // 2, a.shape[2]), a.dtype) for a in late], _FoldViews,
                       jnp.zeros(DEP_SHAPE, F32), "fold_ffn1_start")

    me = (2 * lax.axis_index("x") + lax.axis_index("y")).astype(jnp.int32).reshape(1)
    out = {}

    def reduced(sub, steps, dep, tag):
        parts = [ex.parts[n].reshape(N_SHARDS, -1, ex.parts[n].shape[-1]) for n in sub]
        recvs = [ex.recv[n].reshape(3, -1, ex.recv[n].shape[-1]) for n in sub]
        return _reduce_own(me, parts, recvs, dep, steps, f"reduce_{tag}")

    def updated(sub, mine, theirs, steps, tag):
        res = _adamw([_shard_2d(n, given[n]) for n in sub], mine, theirs, [_shard_2d(n, given["m_" + n]) for n in sub],
                     [_shard_2d(n, given["v_" + n]) for n in sub], steps, f"adamw_{tag}")
        for n, rs in zip(sub, res):
            out[n] = [(r.T if n in TRANSPOSED else r).reshape(given[n].shape) for r in rs]
        return [out[n][1] for n in sub]

    ex.late_received((fold[4],))
    rest = tuple(n for n in sharded if n not in FFN1_GROUP)
    large, lora = tuple(n for n in rest if n not in RWKV_MAT), tuple(n for n in rest if n in RWKV_MAT)
    mine = reduced(large, 4, fold[4], "rest_large")
    both = _push_wait(fold, _FoldViews, tuple(mine), "fold_ffn1_wait", with_sources=True)
    folded = _fold_add(core, both[:len(late)], both[len(late):], "fold_add_ffn1")
    dep = ex.send_ffn1(dict(zip(FFN1_GROUP, folded)))
    swap = _push_start(mine, [lax.empty(a.shape, a.dtype) for a in mine], _SiblingViews, dep, "swap_rest_start")
    mine_lora = reduced(lora, 1, dep, "rest_lora")
    last = updated(lora, mine_lora, _sibling_swap(mine_lora, "sibling_swap_rest_lora"), 1, "rest_lora")

    row = {n: i for i, n in enumerate(small)}
    singles = [n for n in small if n not in RWKV_VEC]
    gsum = _allreduce_small([g[n] for n in singles] + [g["rwkv_vec"], loss],
                            [row[n] for n in singles] + [row[RWKV_VEC[0]], len(small)])
    res = _adamw_replicated(gsum, [given[n] for n in small], [given["m_" + n] for n in small], [given["v_" + n] for n in small])
    for n, rs in zip(small, res):
        out[n] = list(rs)
    total_loss = gsum[len(small), 0]

    both = _push_wait(swap, _SiblingViews, (*last, res[0][1]), "swap_rest_wait", with_sources=True)
    last = updated(large, both[:len(large)], both[len(large):], 8, "rest_large")

    ex.ffn1_received((*last, res[0][1]))
    halves = _reduce_own(me, [ex.parts[n] for n in FFN1_GROUP], [ex.recv[n] for n in FFN1_GROUP],
                         jnp.zeros(DEP_SHAPE, F32), FOLD_STEPS, "reduce_ffn1", half=core)
    grads = _sibling_fill(halves, "sibling_fill_ffn1")
    res = _adamw([_shard_2d(n, given[n]) for n in FFN1_GROUP], grads, None, [_shard_2d(n, given["m_" + n]) for n in FFN1_GROUP],
                 [_shard_2d(n, given["v_" + n]) for n in FFN1_GROUP], 8, "adamw_ffn1")
    for n, rs in zip(FFN1_GROUP, res):
        out[n] = [(r.T if n in TRANSPOSED else r).reshape(given[n].shape) for r in rs]
    return (total_loss, dx[None], *[out[n][0] for n in WEIGHTS], *[out[n][1] for n in WEIGHTS],
            *[out[n][2] for n in WEIGHTS], *[out[n][3] for n in WEIGHTS])
```

```python
import functools

import jax
import jax.numpy as jnp
from jax import lax
from jax.experimental import pallas as pl
from jax.experimental.pallas import tpu as pltpu

F32 = jnp.float32
BF16 = jnp.bfloat16
MESH = pl.DeviceIdType.MESH

RMS_EPS = 1e-6
GN_EPS = 64e-5
NEG_INF = -1e30
FFN_RESIDUAL = 0.5
HEAD_DIM = 64
ATT_BLOCK = 128
DILATIONS = (1, 4, 16)
SCAN_CHUNK = 64
TOKEN_TILE = 256
FFN_BWD_TILE = 512

ADAM_LR = 0.001
ADAM_B1 = 0.9
ADAM_B2 = 0.999
ADAM_EPS = 1e-08
ADAM_WD = 0.01
ADAM_STEP = 10

VMEM_FULL = pl.BlockSpec(memory_space=pltpu.VMEM)
ANY = pl.BlockSpec(memory_space=pl.ANY)


VMEM_LIMIT = 56 * 1024 * 1024


def _params(*sem):
    return pltpu.CompilerParams(dimension_semantics=sem, vmem_limit_bytes=VMEM_LIMIT)


def _dot(a, b, dims):
    return lax.dot_general(a.astype(BF16), b.astype(BF16), (dims, ((), ())), preferred_element_type=F32)


def _dot_nn(a, b):
    return _dot(a, b, ((1,), (0,)))


def _dot_nt(a, b):
    return _dot(a, b, ((1,), (1,)))


def _dot_tn(a, b):
    return _dot(a, b, ((0,), (0,)))


@jax.custom_vjp
def _mm(a, b):
    return _dot_nn(a, b)


def _mm_fwd(a, b):
    return _dot_nn(a, b), (a, b)


def _mm_bwd(res, g):
    a, b = res
    return _dot_nt(g, b).astype(a.dtype), _dot_tn(a, g).astype(b.dtype)


_mm.defvjp(_mm_fwd, _mm_bwd)


@jax.custom_vjp
def _mm_nt(a, bt):
    return _dot_nt(a, bt)


def _mm_nt_fwd(a, bt):
    return _dot_nt(a, bt), (a, bt)


def _mm_nt_bwd(res, g):
    a, bt = res
    return _dot_nn(g, bt).astype(a.dtype), _dot_tn(g, a).astype(bt.dtype)


_mm_nt.defvjp(_mm_nt_fwd, _mm_nt_bwd)


def _bdot(a, b, ca, cb):
    return lax.dot_general(a.astype(BF16), b.astype(BF16), (((ca,), (cb,)), ((0,), (0,))), preferred_element_type=F32)


@jax.custom_vjp
def _bmm_nt(a, b):
    return _bdot(a, b, 2, 2)


def _bmm_nt_fwd(a, b):
    return _bdot(a, b, 2, 2), (a, b)


def _bmm_nt_bwd(res, g):
    a, b = res
    return _bdot(g, b, 2, 1), _bdot(g, a, 1, 1)


_bmm_nt.defvjp(_bmm_nt_fwd, _bmm_nt_bwd)


@jax.custom_vjp
def _bmm_nn(a, b):
    return _bdot(a, b, 2, 1)


def _bmm_nn_fwd(a, b):
    return _bdot(a, b, 2, 1), (a, b)


def _bmm_nn_bwd(res, g):
    a, b = res
    return _bdot(g, b, 2, 2), _bdot(a, g, 1, 1)


_bmm_nn.defvjp(_bmm_nn_fwd, _bmm_nn_bwd)


@jax.custom_vjp
def _bmm_tn(a, b):
    return _bdot(a, b, 1, 1)


def _bmm_tn_fwd(a, b):
    return _bdot(a, b, 1, 1), (a, b)


def _bmm_tn_bwd(res, g):
    a, b = res
    return _bdot(b, g, 2, 2), _bdot(a, g, 2, 1)


_bmm_tn.defvjp(_bmm_tn_fwd, _bmm_tn_bwd)


def _hdot(a, b, ca, cb):
    return lax.dot_general(a, b, (((ca,), (cb,)), ((0,), (0,))), precision=lax.Precision.HIGH, preferred_element_type=F32)


def _sigmoid(x):
    return 1.0 / (1.0 + jnp.exp(-x))


def _rms(x):
    return lax.rsqrt(jnp.mean(x * x, axis=-1, keepdims=True) + RMS_EPS)


def _ffn_fwd(x, norm, wg, wu, wd, dep, name, target=None):
    t, d = x.shape
    nc, fc, _ = wg.shape
    tm = TOKEN_TILE

    def body(x_ref, n_ref, wg_ref, wu_ref, wd_ref, dep_ref, *rest):
        o_ref, g_ref, u_ref = rest[-3:] if target is None else rest[1:4]
        xv = x_ref[...]
        h = (xv * _rms(xv) * n_ref[...]).astype(BF16)
        acc = jnp.zeros((tm, d), F32)
        for c in range(nc):
            g = _dot_nt(h, wg_ref[c])
            u = _dot_nt(h, wu_ref[c])
            g_ref[c] = g.astype(BF16)
            u_ref[c] = u.astype(BF16)
            a = (g * _sigmoid(g) * u).astype(BF16)
            acc = acc + jnp.dot(a, wd_ref[c], preferred_element_type=F32)
        y = xv + FFN_RESIDUAL * acc
        if target is None:
            o_ref[...] = y
        else:
            t_ref, loss_ref = rest[0], rest[4]
            err = y - t_ref[...]
            o_ref[...] = err * (1.0 / d)
            part = 0.5 * jnp.sum(jnp.mean(err * err, axis=-1, keepdims=True), axis=0, keepdims=True)

            @pl.when(pl.program_id(0) == 0)
            def _():
                loss_ref[...] = jnp.zeros_like(loss_ref)

            loss_ref[...] += jnp.broadcast_to(part, loss_ref.shape)

    tile = pl.BlockSpec((tm, d), lambda i: (i, 0))
    hidden = pl.BlockSpec((nc, tm, fc), lambda i: (0, i, 0))
    hshape = jax.ShapeDtypeStruct((nc, t, fc), BF16)
    with_loss = target is not None
    return pl.pallas_call(
        body, name=name, grid=(t // tm,),
        out_shape=(jax.ShapeDtypeStruct((t, d), F32), hshape, hshape) + ((jax.ShapeDtypeStruct((1, 128), F32),) if with_loss else ()),
        in_specs=[tile, pl.BlockSpec((1, d), lambda i: (0, 0)), VMEM_FULL, VMEM_FULL, VMEM_FULL, ANY] + ([tile] if with_loss else []),
        out_specs=(tile, hidden, hidden) + ((pl.BlockSpec((1, 128), lambda i: (0, 0)),) if with_loss else ()),
        compiler_params=_params("arbitrary"),
    )(x, norm, wg, wu, wd, dep, *((target,) if with_loss else ()))


def _rmsnorm_bwd(xv, gain, dh):
    rs = _rms(xv)
    xn = xv * rs
    dxn = dh * gain
    dx = rs * (dxn - xn * jnp.mean(dxn * xn, axis=-1, keepdims=True))
    return dx, jnp.sum(dh * xn, axis=0, keepdims=True)


def _ffn_bwd(x, norm, wg, wu, wd, gate, up, dy, dep, name):
    t, d = x.shape
    nc, fc, _ = wg.shape
    tm = FFN_BWD_TILE
    nt = t // tm

    def body(x_ref, n_ref, wg_ref, wu_ref, wd_ref, g_ref, u_ref, dy_ref, dep_ref, dx_ref, dn_ref, dwg_ref, dwu_ref,
             dwd_ref, dh_ref, ag_ref, au_ref, ad_ref):
        c, i = pl.program_id(0), pl.program_id(1)
        rows = pl.ds(pl.multiple_of(i * tm, tm), tm)
        xv = x_ref[...]
        gain = n_ref[...]
        h = (xv * _rms(xv) * gain).astype(BF16)
        dy = dy_ref[...]
        dyb = (FFN_RESIDUAL * dy).astype(BF16)
        g = g_ref[0].astype(F32)
        u = u_ref[0].astype(F32)
        sg = _sigmoid(g)
        s = g * sg
        a = (s * u).astype(BF16)
        da = _dot_nt(dyb, wd_ref[0])
        dub = (da * s).astype(BF16)
        dgb = (da * u * (sg * (1.0 + g * (1.0 - sg)))).astype(BF16)
        dwd_c = _dot_tn(a, dyb)
        dwg_c = _dot_tn(dgb, h)
        dwu_c = _dot_tn(dub, h)
        dh_c = _dot_nn(dgb, wg_ref[0]) + _dot_nn(dub, wu_ref[0])

        @pl.when(i == 0)
        def _():
            ad_ref[...] = dwd_c
            ag_ref[...] = dwg_c
            au_ref[...] = dwu_c

        @pl.when(i > 0)
        def _():
            ad_ref[...] += dwd_c
            ag_ref[...] += dwg_c
            au_ref[...] += dwu_c

        @pl.when(i == nt - 1)
        def _():
            dwd_ref[0] = ad_ref[...].astype(BF16)
            dwg_ref[0] = ag_ref[...].astype(BF16)
            dwu_ref[0] = au_ref[...].astype(BF16)

        @pl.when(c == 0)
        def _():
            dh_ref[rows, :] = dh_c

        @pl.when(c > 0)
        def _():
            dh_ref[rows, :] += dh_c

        @pl.when(c == nc - 1)
        def _():
            dx, dn = _rmsnorm_bwd(xv, gain, dh_ref[rows, :])
            dx_ref[...] = dx + dy

            @pl.when(i == 0)
            def _():
                dn_ref[...] = dn

            @pl.when(i > 0)
            def _():
                dn_ref[...] += dn

    tile = pl.BlockSpec((tm, d), lambda c, i: (i, 0))
    row = pl.BlockSpec((1, d), lambda c, i: (0, 0))
    wrow = pl.BlockSpec((1, fc, d), lambda c, i: (c, 0, 0), pipeline_mode=pl.Buffered(1))
    hidden = pl.BlockSpec((1, tm, fc), lambda c, i: (c, i, 0))
    last = pl.BlockSpec((tm, d), lambda c, i: (jnp.where(c == nc - 1, i, 0), 0))
    return pl.pallas_call(
        body, name=name, grid=(nc, nt),
        out_shape=(jax.ShapeDtypeStruct((t, d), F32), jax.ShapeDtypeStruct((1, d), F32),
                   jax.ShapeDtypeStruct(wg.shape, BF16), jax.ShapeDtypeStruct(wu.shape, BF16),
                   jax.ShapeDtypeStruct(wd.shape, BF16)),
        in_specs=[tile, row, wrow, wrow, wrow, hidden, hidden, tile, ANY],
        out_specs=(last, row, wrow, wrow, wrow),
        scratch_shapes=[pltpu.VMEM((t, d), F32)] + [pltpu.VMEM((fc, d), F32)] * 3,
        compiler_params=_params("arbitrary", "arbitrary"),
    )(x, norm, wg, wu, wd, gate, up, dy, dep)


def _store_heads(ref, v):
    for h in range(ref.shape[0]):
        ref[h] = v[:, h * HEAD_DIM:(h + 1) * HEAD_DIM]


def _load_heads(ref):
    return jnp.concatenate([ref[h] for h in range(ref.shape[0])], axis=-1)


N_HEAD_GROUPS = 3


def _proj_fwd(x, norm, w, c):
    t, d = x.shape
    nc, _, ncol = w.shape
    nh = c // HEAD_DIM
    tm = TOKEN_TILE
    wide = nc * ncol - N_HEAD_GROUPS * c

    def body(x_ref, n_ref, w_ref, q_ref, k_ref, v_ref, cur_ref):
        xv = x_ref[...]
        h = (xv * _rms(xv) * n_ref[...]).astype(BF16)
        full = jnp.concatenate([jnp.dot(h, w_ref[s], preferred_element_type=F32) for s in range(nc)], axis=1)
        for m, ref in enumerate((q_ref, k_ref, v_ref)):
            _store_heads(ref, full[:, m * c:(m + 1) * c])
        cur_ref[...] = full[:, N_HEAD_GROUPS * c:]

    heads = pl.BlockSpec((nh, tm, HEAD_DIM), lambda i: (0, i, 0))
    hshape = jax.ShapeDtypeStruct((nh, t, HEAD_DIM), F32)
    return pl.pallas_call(
        body, name="proj_fwd", grid=(t // tm,),
        out_shape=(hshape, hshape, hshape, jax.ShapeDtypeStruct((t, wide), F32)),
        in_specs=[pl.BlockSpec((tm, d), lambda i: (i, 0)), pl.BlockSpec((1, d), lambda i: (0, 0)), VMEM_FULL],
        out_specs=(heads, heads, heads, pl.BlockSpec((tm, wide), lambda i: (i, 0))),
        compiler_params=_params("arbitrary"),
    )(x, norm, w)


def _proj_bwd(x, norm, w, dq, dk, dv, dcur, dres):
    t, d = x.shape
    nc, _, ncol = w.shape
    nh = dq.shape[0]
    tm = TOKEN_TILE
    nt = t // tm
    wide = dcur.shape[1]

    def body(x_ref, n_ref, w_ref, dq_ref, dk_ref, dv_ref, dcur_ref, dres_ref, dx_ref, dn_ref, dw_ref, acc_ref):
        i = pl.program_id(0)

        @pl.when(i == 0)
        def _():
            acc_ref[...] = jnp.zeros_like(acc_ref)
            dn_ref[...] = jnp.zeros_like(dn_ref)

        xv = x_ref[...]
        gain = n_ref[...]
        h = (xv * _rms(xv) * gain).astype(BF16)
        dp = jnp.concatenate([_load_heads(dq_ref), _load_heads(dk_ref), _load_heads(dv_ref), dcur_ref[...]], axis=1).astype(BF16)
        dh = jnp.zeros((tm, d), F32)
        for s in range(nc):
            dps = dp[:, s * ncol:(s + 1) * ncol]
            acc_ref[s] += _dot_tn(h, dps)
            dh = dh + _dot_nt(dps, w_ref[s])
        dx, dn = _rmsnorm_bwd(xv, gain, dh)
        dx_ref[...] = dx + dres_ref[...]
        dn_ref[...] += dn

        @pl.when(i == nt - 1)
        def _():
            dw_ref[...] = acc_ref[...].astype(BF16)

    tile = pl.BlockSpec((tm, d), lambda i: (i, 0))
    row = pl.BlockSpec((1, d), lambda i: (0, 0))
    heads = pl.BlockSpec((nh, tm, HEAD_DIM), lambda i: (0, i, 0))
    return pl.pallas_call(
        body, name="proj_bwd", grid=(nt,),
        out_shape=(jax.ShapeDtypeStruct((t, d), F32), jax.ShapeDtypeStruct((1, d), F32),
                   jax.ShapeDtypeStruct(w.shape, BF16)),
        in_specs=[tile, row, VMEM_FULL, heads, heads, heads, pl.BlockSpec((tm, wide), lambda i: (i, 0)), tile],
        out_specs=(tile, row, VMEM_FULL),
        scratch_shapes=[pltpu.VMEM(w.shape, F32)], compiler_params=_params("arbitrary"),
    )(x, norm, w, dq, dk, dv, dcur, dres)


def _mixout_fwd(x, att, opg, gate, w):
    t, d = x.shape
    nh = att.shape[0]
    half = gate.shape[1]
    tm = TOKEN_TILE

    def body(x_ref, att_ref, opg_ref, g_ref, w_ref, o_ref):
        mix = jnp.concatenate([_load_heads(att_ref), _load_heads(opg_ref) * g_ref[...]], axis=-1).astype(BF16)
        o_ref[...] = x_ref[...] + jnp.dot(mix, w_ref[...], preferred_element_type=F32)

    tile = pl.BlockSpec((tm, d), lambda i: (i, 0))
    htile = pl.BlockSpec((tm, half), lambda i: (i, 0))
    heads = pl.BlockSpec((nh, tm, HEAD_DIM), lambda i: (0, i, 0))
    return pl.pallas_call(
        body, name="mixout_fwd", grid=(t // tm,), out_shape=jax.ShapeDtypeStruct((t, d), F32),
        in_specs=[tile, heads, heads, htile, VMEM_FULL], out_specs=tile, compiler_params=_params("arbitrary"),
    )(x, att, opg, gate, w)


def _mixout_bwd(att, opg, gate, w, dy, dep):
    nh, t, _ = att.shape
    half = gate.shape[1]
    d = dy.shape[1]
    tm = TOKEN_TILE

    def body(att_ref, opg_ref, g_ref, w_ref, dy_ref, dep_ref, datt_ref, dopg_ref, dg_ref, dw_ref):
        i = pl.program_id(0)
        opg_v, g_v = _load_heads(opg_ref), g_ref[...]
        mix = jnp.concatenate([_load_heads(att_ref), opg_v * g_v], axis=-1).astype(BF16)
        dyb = dy_ref[...].astype(BF16)
        dmix = _dot_nt(dyb, w_ref[...])
        dw = _dot_tn(mix, dyb)
        _store_heads(datt_ref, dmix[:, :half])
        drw = dmix[:, half:]
        _store_heads(dopg_ref, drw * g_v)
        dg_ref[...] = drw * opg_v

        @pl.when(i == 0)
        def _():
            dw_ref[...] = dw

        @pl.when(i > 0)
        def _():
            dw_ref[...] += dw

    tile = pl.BlockSpec((tm, d), lambda i: (i, 0))
    htile = pl.BlockSpec((tm, half), lambda i: (i, 0))
    heads = pl.BlockSpec((nh, tm, HEAD_DIM), lambda i: (0, i, 0))
    hshape = jax.ShapeDtypeStruct((nh, t, HEAD_DIM), F32)
    return pl.pallas_call(
        body, name="mixout_bwd", grid=(t // tm,),
        out_shape=(hshape, hshape, jax.ShapeDtypeStruct((t, half), F32), jax.ShapeDtypeStruct(w.shape, F32)),
        in_specs=[heads, heads, htile, VMEM_FULL, tile, ANY],
        out_specs=(heads, heads, htile, pl.BlockSpec(w.shape, lambda i: (0, 0))),
        compiler_params=_params("arbitrary"),
    )(att, opg, gate, w, dy, dep)


ATT_SCALE = HEAD_DIM ** -0.5


def _head_norm(x, gain):
    return x * _rms(x) * gain


def _att_pattern(qh, kh, v, nb):
    g, blk, _ = qh.shape
    qi = lax.broadcasted_iota(jnp.int32, (blk, blk), 0)
    kj = lax.broadcasted_iota(jnp.int32, (blk, blk), 1)
    sc = jnp.where(kj <= qi, _bmm_nt(qh, kh), NEG_INF)
    top = jnp.max(sc, axis=-1, keepdims=True)
    if nb > 1:
        khp = jnp.concatenate([kh[:1], kh[:-1]], axis=0)
        vp = jnp.concatenate([v[:1], v[:-1]], axis=0)
        has_prev = lax.broadcasted_iota(jnp.int32, (g, 1, 1), 0) % nb != 0
        sp = jnp.where((kj >= qi) & has_prev, _bmm_nt(qh, khp), NEG_INF)
        top = jnp.maximum(top, jnp.max(sp, axis=-1, keepdims=True))
    m = lax.stop_gradient(top)
    pc = jnp.exp(sc - m)
    den = jnp.sum(pc, axis=-1, keepdims=True)
    acc = _bmm_nn(pc, v)
    if nb > 1:
        pp = jnp.exp(sp - m)
        den = den + jnp.sum(pp, axis=-1, keepdims=True)
        acc = acc + _bmm_nn(pp, vp)
    o = acc / den
    return o, jnp.broadcast_to(m + jnp.log(den), o.shape)


def _pattern_rows(t, dil):
    length = t // dil
    return [pl.ds(r, length, stride=dil) if dil > 1 else pl.ds(0, length) for r in range(dil)], length // ATT_BLOCK


def _take(ref, rows, nb):
    return jnp.concatenate([ref[0, r, :].reshape(nb, ATT_BLOCK, HEAD_DIM) for r in rows], axis=0)


def _put(ref, rows, nb, val):
    for j, r in enumerate(rows):
        ref[0, r, :] = val[j * nb:(j + 1) * nb].reshape(nb * ATT_BLOCK, HEAD_DIM)


def _put_add(ref, rows, nb, val):
    for j, r in enumerate(rows):
        ref[0, r, :] += val[j * nb:(j + 1) * nb].reshape(nb * ATT_BLOCK, HEAD_DIM)


def _merge_fn(o1, o2, o3, l1, l2, l3):
    m = lax.stop_gradient(jnp.maximum(jnp.maximum(l1, l2), l3))
    e1, e2, e3 = jnp.exp(l1 - m), jnp.exp(l2 - m), jnp.exp(l3 - m)
    return (e1 * o1 + e2 * o2 + e3 * o3) / (e1 + e2 + e3)


def _token_rows(j):
    return pl.ds(pl.multiple_of(j * ATT_BLOCK, ATT_BLOCK), ATT_BLOCK)


def _norm_rows(t, q_ref, k_ref, gq, gk, qh_ref, kh_ref):
    def step(j, carry):
        rows = _token_rows(j)
        qh_ref[0, rows, :] = _head_norm(q_ref[0, rows, :], gq[0] * ATT_SCALE)
        kh_ref[0, rows, :] = _head_norm(k_ref[0, rows, :], gk[0])
        return carry

    lax.fori_loop(0, t // ATT_BLOCK, step, 0)


def _att_head_specs(t):
    head = pl.BlockSpec((1, t, HEAD_DIM), lambda h: (h, 0, 0))
    gain = pl.BlockSpec((1, 1, HEAD_DIM), lambda h: (0, 0, 0))
    return head, gain


def _att_fwd(q, k, v, qn, kn):
    nh, t, dh = q.shape
    head, gain = _att_head_specs(t)

    def body(q_ref, k_ref, v_ref, qn_ref, kn_ref, att_ref, o1, o2, o3, l1, l2, l3, qh_ref, kh_ref):
        saved = (o1, o2, o3, l1, l2, l3)
        _norm_rows(t, q_ref, k_ref, qn_ref[...], kn_ref[...], qh_ref, kh_ref)
        for p, dil in enumerate(DILATIONS):
            rows, nb = _pattern_rows(t, dil)
            o, lse = _att_pattern(_take(qh_ref, rows, nb), _take(kh_ref, rows, nb), _take(v_ref, rows, nb), nb)
            _put(saved[p], rows, nb, o)
            _put(saved[3 + p], rows, nb, lse)

        def merge(j, carry):
            rows = _token_rows(j)
            att_ref[0, rows, :] = _merge_fn(*[r[0, rows, :] for r in saved])
            return carry

        lax.fori_loop(0, t // ATT_BLOCK, merge, 0)

    return pl.pallas_call(
        body, name="att_fwd", grid=(nh,), out_shape=(jax.ShapeDtypeStruct(q.shape, F32),) * 7,
        in_specs=[head, head, head, gain, gain], out_specs=(head,) * 7,
        scratch_shapes=[pltpu.VMEM((1, t, dh), F32)] * 2, compiler_params=_params("arbitrary"),
    )(q, k, v, qn, kn)


def _att_bwd(q, k, v, qn, kn, saved, datt):
    nh, t, dh = q.shape
    head, gain = _att_head_specs(t)

    def body(q_ref, k_ref, v_ref, qn_ref, kn_ref, o1, o2, o3, l1, l2, l3, datt_ref,
             dq_ref, dk_ref, dv_ref, dqn_ref, dkn_ref, qh_ref, kh_ref, dqh_ref, dkh_ref, *ct_refs):
        for ref in (dqh_ref, dkh_ref, dv_ref):
            ref[...] = jnp.zeros_like(ref)

        @pl.when(pl.program_id(0) == 0)
        def _():
            dqn_ref[...] = jnp.zeros_like(dqn_ref)
            dkn_ref[...] = jnp.zeros_like(dkn_ref)

        gq, gk = qn_ref[...], kn_ref[...]
        _norm_rows(t, q_ref, k_ref, gq, gk, qh_ref, kh_ref)

        def merge_cotangents(j, carry):
            rows = _token_rows(j)
            _, merge_vjp = jax.vjp(_merge_fn, *[r[0, rows, :] for r in (o1, o2, o3, l1, l2, l3)])
            for ref, val in zip(ct_refs, merge_vjp(datt_ref[0, rows, :])):
                ref[0, rows, :] = val
            return carry

        lax.fori_loop(0, t // ATT_BLOCK, merge_cotangents, 0)

        for p, dil in enumerate(DILATIONS):
            rows, nb = _pattern_rows(t, dil)
            _, pattern_vjp = jax.vjp(functools.partial(_att_pattern, nb=nb), _take(qh_ref, rows, nb), _take(kh_ref, rows, nb),
                                     _take(v_ref, rows, nb))
            dqh, dkh, dv = pattern_vjp((_take(ct_refs[p], rows, nb), _take(ct_refs[3 + p], rows, nb)))
            _put_add(dqh_ref, rows, nb, dqh)
            _put_add(dkh_ref, rows, nb, dkh)
            _put_add(dv_ref, rows, nb, dv)

        def norm_cotangents(j, carry):
            rows = _token_rows(j)
            out = []
            for x_ref, gain, dh_ref, dx_ref, acc in ((q_ref, gq * ATT_SCALE, dqh_ref, dq_ref, carry[0]),
                                                     (k_ref, gk, dkh_ref, dk_ref, carry[1])):
                _, norm_vjp = jax.vjp(_head_norm, x_ref[0, rows, :], gain[0])
                dx, dgain = norm_vjp(dh_ref[0, rows, :])
                dx_ref[0, rows, :] = dx
                out.append(acc + dgain)
            return tuple(out)

        zero = jnp.zeros((1, dh), F32)
        dgq, dgk = lax.fori_loop(0, t // ATT_BLOCK, norm_cotangents, (zero, zero))
        dqn_ref[0] += dgq * ATT_SCALE
        dkn_ref[0] += dgk

    hshape = jax.ShapeDtypeStruct(q.shape, F32)
    gshape = jax.ShapeDtypeStruct((1, 1, dh), F32)
    return pl.pallas_call(
        body, name="att_bwd", grid=(nh,), out_shape=(hshape, hshape, hshape, gshape, gshape),
        in_specs=[head, head, head, gain, gain] + [head] * 7, out_specs=(head, head, head, gain, gain),
        scratch_shapes=[pltpu.VMEM((1, t, dh), F32)] * 10, compiler_params=_params("arbitrary"),
    )(q, k, v, qn, kn, *saved, datt)


RWKV_VEC = ("mu_r", "mu_k", "mu_v", "mu_w", "mu_a", "mu_g", "w0", "a0", "k_k", "k_a")
RWKV_MAT = ("w1", "w2", "a1", "a2", "g1", "g2")


def _rwkv_pre_fn(cur, prev, vec, w1t, w2, a1t, a2, g1t, g2):
    c = cur.shape[1] // 4
    mu_r, mu_k, mu_v, mu_w, mu_a, mu_g, w0, a0, k_k, k_a = (vec[j:j + 1] for j in range(10))

    def lerp(j, mu):
        xc, xp = cur[:, j * c:(j + 1) * c], prev[:, j * c:(j + 1) * c]
        return xc + (xp - xc) * mu

    r, k, v = lerp(0, mu_r), lerp(1, mu_k), lerp(2, mu_v)
    cw, ca, cg = lerp(3, mu_w), lerp(3, mu_a), lerp(3, mu_g)
    z = w0 + _mm(jnp.tanh(_mm_nt(cw, w1t)), w2)
    w_log = jnp.minimum(z, 0.0) - jnp.log(1.0 + jnp.exp(-jnp.abs(z))) - 0.5
    lw = -jnp.exp(w_log)
    a = _sigmoid(a0 + _mm(_mm_nt(ca, a1t), a2))
    gate = _mm(_sigmoid(_mm_nt(cg, g1t)), g2)
    kkraw = k * k_k
    kmod = k * (1.0 + (a - 1.0) * k_a)
    return r, lw, kmod, v, kkraw, a, gate


HALO_ROWS = 8


def _rwkv_pre_specs(c, mats, tile_of):
    tm = TOKEN_TILE
    nh = c // HEAD_DIM
    wide = pl.BlockSpec((tm, 4 * c), lambda j: (tile_of(j), 0))
    halo = pl.BlockSpec((HALO_ROWS, 4 * c), lambda j: (jnp.maximum(tile_of(j) * (tm // HALO_ROWS) - 1, 0), 0))
    one = pl.BlockSpec((tm, c), lambda j: (tile_of(j), 0))
    heads = pl.BlockSpec((nh, tm, HEAD_DIM), lambda j: (0, tile_of(j), 0))
    vec = pl.BlockSpec((10, c), lambda j: (0, 0))
    mspecs = [pl.BlockSpec(m.shape, lambda j: (0, 0)) for m in mats]
    return wide, halo, one, heads, vec, mspecs


def _previous_rows(cur, halo, tile):
    first = jnp.where(tile > 0, halo[HALO_ROWS - 1:HALO_ROWS], 0.0)
    rows = lax.broadcasted_iota(jnp.int32, cur.shape, 0)
    return jnp.where(rows == 0, first, pltpu.roll(cur, 1, axis=0))


def _rwkv_pre_fwd(cur, vec, mats):
    t, c4 = cur.shape
    c = c4 // 4
    wide, halo, one, heads, vspec, mspecs = _rwkv_pre_specs(c, mats, lambda j: j)

    def body(cur_ref, halo_ref, vec_ref, *rest):
        mrefs, outs = rest[:6], rest[6:]
        cur_v = cur_ref[...]
        prev = _previous_rows(cur_v, halo_ref[...], pl.program_id(0))
        vals = _rwkv_pre_fn(cur_v, prev, vec_ref[...], *(m[...] for m in mrefs))
        for ref, val in zip(outs[:6], vals[:6]):
            _store_heads(ref, val)
        outs[6][...] = vals[6]

    hshape = jax.ShapeDtypeStruct((c // HEAD_DIM, t, HEAD_DIM), F32)
    return pl.pallas_call(
        body, name="rwkv_pre_fwd", grid=(t // TOKEN_TILE,), out_shape=(hshape,) * 6 + (jax.ShapeDtypeStruct((t, c), F32),),
        in_specs=[wide, halo, vspec] + mspecs, out_specs=(heads,) * 6 + (one,), compiler_params=_params("arbitrary"),
    )(cur, cur, vec, *mats)


def _rwkv_pre_bwd(cur, vec, mats, cts, dgate):
    t, c4 = cur.shape
    c = c4 // 4
    tm = TOKEN_TILE
    nt = t // tm
    wide, halo, one, heads, vspec, mspecs = _rwkv_pre_specs(c, mats, lambda j: nt - 1 - j)

    def body(cur_ref, halo_ref, vec_ref, *rest):
        mrefs, ctrefs, dgate_ref, outs, carry_ref = rest[:6], rest[6:12], rest[12], rest[13:-1], rest[-1]
        j = pl.program_id(0)

        @pl.when(j == 0)
        def _():
            carry_ref[...] = jnp.zeros_like(carry_ref)
            for ref in outs[1:]:
                ref[...] = jnp.zeros_like(ref)

        cur_v = cur_ref[...]
        prev = _previous_rows(cur_v, halo_ref[...], nt - 1 - j)
        _, vjp = jax.vjp(_rwkv_pre_fn, cur_v, prev, vec_ref[...], *(m[...] for m in mrefs))
        grads = vjp(tuple(_load_heads(r) for r in ctrefs) + (dgate_ref[...],))
        dprev = grads[1]
        rows = lax.broadcasted_iota(jnp.int32, dprev.shape, 0)
        outs[0][...] = grads[0] + jnp.where(rows == tm - 1, carry_ref[0:1], pltpu.roll(dprev, tm - 1, axis=0))
        carry_ref[0:1] = dprev[0:1]
        for ref, val in zip(outs[1:], grads[2:]):
            ref[...] += val

    return pl.pallas_call(
        body, name="rwkv_pre_bwd", grid=(nt,),
        out_shape=(jax.ShapeDtypeStruct(cur.shape, F32), jax.ShapeDtypeStruct(vec.shape, F32))
        + tuple(jax.ShapeDtypeStruct(m.shape, F32) for m in mats),
        in_specs=[wide, halo, vspec] + mspecs + [heads] * 6 + [one], out_specs=(wide, vspec) + tuple(mspecs),
        scratch_shapes=[pltpu.VMEM((HALO_ROWS, c4), F32)], compiler_params=_params("arbitrary"),
    )(cur, cur, vec, *mats, *cts, dgate)


def _scan_chunk_fn(h0, r, lw, k, v, kkraw, a, rk, lnw, lnb):
    n = r.shape[1]
    nrm = jnp.sqrt(jnp.sum(kkraw * kkraw, axis=-1, keepdims=True))
    kk = kkraw / jnp.maximum(nrm, 1e-12)
    av, bv = -kk, kk * a
    ti = lax.broadcasted_iota(jnp.int32, (n, n), 0)
    si = lax.broadcasted_iota(jnp.int32, (n, n), 1)
    incl, strict = ti >= si, ti > si
    ones = jnp.broadcast_to(incl.astype(F32)[None], (r.shape[0], n, n))
    cum = _hdot(ones, lw, 2, 1)
    at, rt = av * jnp.exp(cum - lw), r * jnp.exp(cum)
    inv = jnp.exp(-cum)
    bt, kt = bv * inv, k * inv
    gram = _hdot(jnp.concatenate([at, rt], axis=1), jnp.concatenate([bt, kt], axis=1), 2, 2)
    lab = jnp.where(strict, gram[:, :n, :n], 0.0)
    lak = jnp.where(strict, gram[:, :n, n:], 0.0)
    rb = jnp.where(incl, gram[:, n:, :n], 0.0)
    rkm = jnp.where(incl, gram[:, n:, n:], 0.0)
    nv = v.shape[2]
    u = _bmm_nn(jnp.concatenate([at, lak], axis=2), jnp.concatenate([h0, v], axis=1))
    p = lab
    m = 2
    while m < n:
        both = _bmm_nn(p, jnp.concatenate([u, p], axis=2))
        u, p = u + both[:, :, :nv], both[:, :, nv:]
        m *= 2
    u = u + _bmm_nn(p, u)
    y = _bmm_nn(jnp.concatenate([rt, rb, rkm], axis=2), jnp.concatenate([h0, u, v], axis=1))
    last = jnp.exp(jnp.sum(lw, axis=1, keepdims=True))
    h1 = jnp.swapaxes(last, 1, 2) * (h0 + _bmm_tn(jnp.concatenate([bt, kt], axis=1), jnp.concatenate([u, v], axis=1)))
    mean = jnp.mean(y, axis=-1, keepdims=True)
    yc = y - mean
    var = jnp.mean(yc * yc, axis=-1, keepdims=True)
    yn = yc * lax.rsqrt(var + GN_EPS) * lnw + lnb
    bonus = jnp.sum(r * k * rk, axis=-1, keepdims=True) * v
    return yn + bonus, h1


SCAN_GROUP = 2


def _scan_group_fn(h0, r, lw, k, v, kkraw, a, rk, lnw, lnb):
    outs = []
    for j in range(SCAN_GROUP):
        rows = slice(j * SCAN_CHUNK, (j + 1) * SCAN_CHUNK)
        o, h0 = _scan_chunk_fn(h0, r[:, rows], lw[:, rows], k[:, rows], v[:, rows], kkraw[:, rows], a[:, rows], rk, lnw, lnb)
        outs.append(o)
    return jnp.concatenate(outs, axis=1), h0


def _scan_specs(h, t, dh, rev):
    n = SCAN_CHUNK * SCAN_GROUP
    nc = t // n
    pos = (lambda c: (0, nc - 1 - c, 0)) if rev else (lambda c: (0, c, 0))
    st = (lambda c: (nc - 1 - c, 0, 0, 0)) if rev else (lambda c: (c, 0, 0, 0))
    seq = pl.BlockSpec((h, n, dh), pos)
    par = pl.BlockSpec((h, 1, dh), lambda c: (0, 0, 0))
    state = pl.BlockSpec((1, h, dh, dh), st)
    return seq, par, state


def _scan_fwd(seqs, pars):
    h, t, dh = seqs[0].shape
    nc = t // (SCAN_CHUNK * SCAN_GROUP)
    seq, par, state = _scan_specs(h, t, dh, False)

    def body(r, lw, k, v, kkraw, a, rk, lnw, lnb, o_ref, st_ref, h_ref):
        @pl.when(pl.program_id(0) == 0)
        def _():
            h_ref[...] = jnp.zeros_like(h_ref)

        h0 = h_ref[...]
        st_ref[0] = h0
        o, h1 = _scan_group_fn(h0, r[...], lw[...], k[...], v[...], kkraw[...], a[...], rk[...], lnw[...], lnb[...])
        o_ref[...] = o
        h_ref[...] = h1

    return pl.pallas_call(
        body, name="rwkv_scan_fwd", grid=(nc,),
        out_shape=(jax.ShapeDtypeStruct((h, t, dh), F32), jax.ShapeDtypeStruct((nc, h, dh, dh), F32)),
        in_specs=[seq] * 6 + [par] * 3, out_specs=(seq, state),
        scratch_shapes=[pltpu.VMEM((h, dh, dh), F32)], compiler_params=_params("arbitrary"),
    )(*seqs, *pars)


def _scan_bwd(seqs, pars, states, do):
    h, t, dh = seqs[0].shape
    nc = t // (SCAN_CHUNK * SCAN_GROUP)
    seq, par, state = _scan_specs(h, t, dh, True)

    def body(r, lw, k, v, kkraw, a, rk, lnw, lnb, st_ref, do_ref, *rest):
        douts, dpars, dh_ref = rest[:6], rest[6:9], rest[9]
        first = pl.program_id(0) == 0

        @pl.when(first)
        def _():
            dh_ref[...] = jnp.zeros_like(dh_ref)

        _, vjp = jax.vjp(_scan_group_fn, st_ref[0], r[...], lw[...], k[...], v[...], kkraw[...], a[...],
                         rk[...], lnw[...], lnb[...])
        grads = vjp((do_ref[...], dh_ref[...]))
        dh_ref[...] = grads[0]
        for ref, val in zip(douts, grads[1:7]):
            ref[...] = val

        @pl.when(first)
        def _():
            for ref, val in zip(dpars, grads[7:]):
                ref[...] = val

        @pl.when(jnp.logical_not(first))
        def _():
            for ref, val in zip(dpars, grads[7:]):
                ref[...] += val

    sshape = jax.ShapeDtypeStruct((h, t, dh), F32)
    pshape = jax.ShapeDtypeStruct((h, 1, dh), F32)
    return pl.pallas_call(
        body, name="rwkv_scan_bwd", grid=(nc,), out_shape=(sshape,) * 6 + (pshape,) * 3,
        in_specs=[seq] * 6 + [par] * 3 + [state, seq], out_specs=(seq,) * 6 + (par,) * 3,
        scratch_shapes=[pltpu.VMEM((h, dh, dh), F32)], compiler_params=_params("arbitrary"),
    )(*seqs, *pars, states, do)


def _local_step(x, target, w, ex):
    w = dict(w)
    c = w["mu_r"].shape[-1]
    qn, kn = w["q_norm"].reshape(1, 1, HEAD_DIM), w["k_norm"].reshape(1, 1, HEAD_DIM)
    vec = jnp.concatenate([w[n].reshape(1, c) for n in RWKV_VEC], axis=0)
    pars = [w[n].reshape(-1, 1, HEAD_DIM) for n in ("r_k", "ln_x_w", "ln_x_b")]
    no_dep = jnp.zeros(DEP_SHAPE, F32)

    x1, gate1, up1 = _ffn_fwd(x, w["ffn1_norm"], w["ffn1_w_gate"], w["ffn1_w_up"], w["ffn1_w_down"], ex.first_dep, "ffn1_fwd")
    w.update(ex.mix_weights((x1,)))
    mats = [w[n] for n in RWKV_MAT]
    q, k, v, cur = _proj_fwd(x1, w["mix_norm"], w["w_in"], c)
    att, *saved = _att_fwd(q, k, v, qn, kn)
    pre = _rwkv_pre_fwd(cur, vec, mats)
    seqs, gate = pre[:6], pre[6]
    opg, states = _scan_fwd(seqs, pars)
    w.update(ex.out_weights((att, opg)))
    x2 = _mixout_fwd(x1, att, opg, gate, w["w_out"])
    dy, gate2, up2, loss = _ffn_fwd(x2, w["ffn2_norm"], w["ffn2_w_gate"], w["ffn2_w_up"], w["ffn2_w_down"], no_dep, "ffn2_fwd",
                                    target=target)

    g = {}
    dx2, g["ffn2_norm"], g["ffn2_w_gate"], g["ffn2_w_up"], g["ffn2_w_down"] = _ffn_bwd(
        x2, w["ffn2_norm"], w["ffn2_w_gate"], w["ffn2_w_up"], w["ffn2_w_down"], gate2, up2, dy, no_dep, "ffn2_bwd")
    dep = ex.send_ffn2({n: g[n] for n in ("ffn2_w_gate", "ffn2_w_up", "ffn2_w_down")})
    datt, dopg, dgate, g["w_out"] = _mixout_bwd(att, opg, gate, w["w_out"], dx2, dep)
    dscan = _scan_bwd(seqs, pars, states, dopg)
    for n, d in zip(("r_k", "ln_x_w", "ln_x_b"), dscan[6:]):
        g[n] = d
    dcur, dvec, *dmats = _rwkv_pre_bwd(cur, vec, mats, dscan[:6], dgate)
    for n, d in zip(RWKV_MAT, dmats):
        g[n] = d
    g["rwkv_vec"] = dvec
    dq, dk, dv, g["q_norm"], g["k_norm"] = _att_bwd(q, k, v, qn, kn, saved, datt)
    dx1, g["mix_norm"], g["w_in"] = _proj_bwd(x1, w["mix_norm"], w["w_in"], dq, dk, dv, dcur, dx2)
    dep = ex.send_mix({n: g[n] for n in ("w_in", "w_out") + RWKV_MAT}, (dx1,))
    dx, g["ffn1_norm"], g["ffn1_w_gate"], g["ffn1_w_up"], g["ffn1_w_down"] = _ffn_bwd(
        x, w["ffn1_norm"], w["ffn1_w_gate"], w["ffn1_w_up"], w["ffn1_w_down"], gate1, up1, dx1, dep, "ffn1_bwd")
    return loss, dx, g


N_SHARDS = 4


def _place():
    return lax.axis_index("x"), lax.axis_index("y"), lax.axis_index("c")


def _chip_peers(x, y):
    return [(1 - x, y), (x, 1 - y), (1 - x, 1 - y)]


HBM = pl.BlockSpec(memory_space=pltpu.HBM)
SEM = pl.BlockSpec(memory_space=pltpu.SEMAPHORE)
DEP_SHAPE = (8, 128)


class _Views:
    to_sibling = False


class _GatherViews(_Views):
    @staticmethod
    def send(i, srcs, lands, k, at):
        return srcs[i], lands[i].at[at[3]]

    @staticmethod
    def landing(i, srcs, lands, k, at):
        return srcs[i], lands[i].at[2 * at[4] + at[5]]


class _ScatterViews(_Views):
    @staticmethod
    def send(i, srcs, lands, k, at):
        return srcs[i].at[2 * at[4] + at[5]], lands[i].at[k]

    @staticmethod
    def landing(i, srcs, lands, k, at):
        return srcs[i].at[at[3]], lands[i].at[k]


def _half_rows(ref, slot, half):
    rows = ref.shape[1] // 2
    return ref.at[slot, pl.ds(pl.multiple_of(half * rows, BF16_SUBLANES), rows)]


class _HalfGatherViews(_Views):
    @staticmethod
    def send(i, srcs, lands, k, at):
        rows = srcs[i].shape[0] // 2
        return srcs[i].at[pl.ds(pl.multiple_of(at[2] * rows, BF16_SUBLANES), rows)], _half_rows(lands[i], at[3], at[2])

    @staticmethod
    def landing(i, srcs, lands, k, at):
        rows = srcs[i].shape[0] // 2
        return srcs[i].at[pl.ds(pl.multiple_of(at[2] * rows, BF16_SUBLANES), rows)], _half_rows(lands[i], 2 * at[4] + at[5], at[2])


class _ForwardViews(_Views):
    to_sibling = True

    @staticmethod
    def send(i, srcs, lands, k, at):
        mine = _half_rows(lands[i], 2 * at[4] + at[5], at[2])
        return mine, mine

    @staticmethod
    def landing(i, srcs, lands, k, at):
        theirs = _half_rows(lands[i], 2 * at[4] + at[5], 1 - at[2])
        return theirs, theirs


class _SiblingViews(_Views):
    to_sibling = True

    @staticmethod
    def _block(ref, k):
        size = -(-ref.shape[0] // 3 // BF16_SUBLANES) * BF16_SUBLANES
        return ref.at[pl.ds(k * size, min(size, ref.shape[0] - k * size))]

    @classmethod
    def send(cls, i, srcs, lands, k, at):
        return cls._block(srcs[i], k), cls._block(lands[i], k)

    landing = send


def _push_start(srcs, lands, views, after, name):
    ns, nl = len(srcs), len(lands)

    def body(*refs):
        src_refs, land_refs = refs[:ns], refs[ns:ns + nl]
        send_sems, recv_sems = refs[ns + nl + 1:ns + nl + 3]
        token = refs[2 * (ns + nl) + 3]
        x, y, c = _place()
        for i in range(nl):
            for k, (px, py) in enumerate(_chip_peers(x, y)):
                src, dst = views.send(i, src_refs, land_refs, k, (x, y, c, 2 * x + y, px, py))
                pltpu.make_async_remote_copy(
                    src_ref=src, dst_ref=dst, send_sem=send_sems.at[3 * i + k], recv_sem=recv_sems.at[3 * i + k],
                    device_id=(x, y, 1 - c) if views.to_sibling else (px, py, c), device_id_type=MESH).start()
        token[...] = jnp.zeros_like(token)

    sems = pltpu.SemaphoreType.DMA((3 * nl,))
    both = [pltpu.with_memory_space_constraint(a, pltpu.HBM) for a in (*srcs, *lands)]
    outs = pl.pallas_call(
        body, name=name,
        out_shape=(sems, sems, *[pltpu.HBM(a.shape, a.dtype) for a in both], jax.ShapeDtypeStruct(DEP_SHAPE, F32)),
        in_specs=[HBM] * (ns + nl) + [ANY], out_specs=(SEM, SEM, *[HBM] * (ns + nl), VMEM_FULL),
        input_output_aliases={i: 2 + i for i in range(ns + nl)},
        compiler_params=pltpu.CompilerParams(has_side_effects=pltpu.SideEffectType.DATAFLOW_SIDE_EFFECTING),
    )(*both, after)
    return outs[0], outs[1], outs[2:2 + ns], outs[2 + ns:2 + ns + nl], outs[2 + ns + nl]


def _push_wait(started, views, after, name, with_sources=False):
    send_sems, recv_sems, srcs, lands, _ = started
    ns, nl = len(srcs), len(lands)

    def body(*refs):
        src_refs, land_refs = refs[:ns], refs[ns:ns + nl]
        send_sems, recv_sems = refs[ns + nl:ns + nl + 2]
        x, y, c = _place()
        for i in range(nl):
            for k, (px, py) in enumerate(_chip_peers(x, y)):
                src, dst = views.landing(i, src_refs, land_refs, k, (x, y, c, 2 * x + y, px, py))
                landing = pltpu.make_async_remote_copy(
                    src_ref=src, dst_ref=dst, send_sem=send_sems.at[3 * i + k], recv_sem=recv_sems.at[3 * i + k],
                    device_id=(x, y, 1 - c) if views.to_sibling else (px, py, c), device_id_type=MESH)
                landing.wait_send()
                landing.wait_recv()

    outs = pl.pallas_call(
        body, name=name,
        out_shape=tuple(pltpu.HBM(a.shape, a.dtype) for a in (*srcs, *lands)),
        in_specs=[HBM] * (ns + nl) + [SEM, SEM] + [ANY] * len(after), out_specs=(HBM,) * (ns + nl),
        input_output_aliases={i: i for i in range(ns + nl)},
        compiler_params=pltpu.CompilerParams(has_side_effects=pltpu.SideEffectType.DATAFLOW_SIDE_EFFECTING),
    )(*srcs, *lands, send_sems, recv_sems, *after)
    return outs if with_sources else outs[ns:]


def _empty_lands(shards, slots, own_slot):
    lands = [lax.empty((slots,) + s.shape, s.dtype) for s in shards]
    if own_slot:
        me = 2 * lax.axis_index("x") + lax.axis_index("y")
        lands = [lax.dynamic_update_index_in_dim(z, s, me, 0) for z, s in zip(lands, shards)]
    return lands


def _sibling_swap(arrays, name, other_half=False):
    n = len(arrays)

    def body(*refs):
        ins, outs = refs[:n], refs[n:2 * n]
        send_sems, recv_sems = refs[2 * n:]
        x, y, c = _place()
        copies = []
        for i in range(n):
            src = ins[i]
            if other_half:
                rows = src.shape[1] // 2
                src = src.at[:, pl.ds(pl.multiple_of((1 - c) * rows, BF16_SUBLANES), rows)]
            cp = pltpu.make_async_remote_copy(
                src_ref=src, dst_ref=outs[i], send_sem=send_sems.at[i], recv_sem=recv_sems.at[i],
                device_id=(x, y, 1 - c), device_id_type=MESH)
            cp.start()
            copies.append(cp)
        for cp in copies:
            cp.wait()

    shapes = [(a.shape[0], a.shape[1] // 2, a.shape[2]) if other_half else a.shape for a in arrays]
    return pl.pallas_call(
        body, name=name,
        out_shape=tuple(jax.ShapeDtypeStruct(s, a.dtype) for s, a in zip(shapes, arrays)),
        in_specs=[ANY] * n, out_specs=(ANY,) * n,
        scratch_shapes=[pltpu.SemaphoreType.DMA((n,)), pltpu.SemaphoreType.DMA((n,))],
    )(*arrays)


def _sibling_fill(arrays, name):
    n = len(arrays)

    def body(*refs):
        outs = refs[n:2 * n]
        send_sems, recv_sems = refs[2 * n:]
        x, y, c = _place()
        copies = []
        for i in range(n):
            rows = outs[i].shape[0] // 2
            mine = outs[i].at[pl.ds(pl.multiple_of(c * rows, BF16_SUBLANES), rows)]
            cp = pltpu.make_async_remote_copy(
                src_ref=mine, dst_ref=mine, send_sem=send_sems.at[i], recv_sem=recv_sems.at[i],
                device_id=(x, y, 1 - c), device_id_type=MESH)
            cp.start()
            copies.append(cp)
        for cp in copies:
            cp.wait()

    return pl.pallas_call(
        body, name=name, out_shape=tuple(jax.ShapeDtypeStruct(a.shape, a.dtype) for a in arrays),
        in_specs=[ANY] * n, out_specs=(ANY,) * n, input_output_aliases={i: i for i in range(n)},
        scratch_shapes=[pltpu.SemaphoreType.DMA((n,)), pltpu.SemaphoreType.DMA((n,))],
    )(*arrays)


FOLD_STEPS = 2


def _fold_add(core, parts, theirs, name):
    n = len(parts)
    s, r, cols = parts[0].shape
    tr = r // 2 // FOLD_STEPS

    def body(core_ref, *refs):
        for p_ref, t_ref, o_ref in zip(refs[:n], refs[n:2 * n], refs[2 * n:]):
            o_ref[...] = (p_ref[...].astype(F32) + t_ref[...].astype(F32)).astype(BF16)

    half = pl.BlockSpec((1, tr, cols), lambda j, i, core_ref: (j, i, 0))
    return pl.pallas_call(
        body, name=name, out_shape=tuple(jax.ShapeDtypeStruct((s, r // 2, cols), BF16) for _ in parts),
        grid_spec=pltpu.PrefetchScalarGridSpec(
            num_scalar_prefetch=1, grid=(s, FOLD_STEPS),
            in_specs=[pl.BlockSpec((1, tr, cols), lambda j, i, core_ref: (j, core_ref[0] * FOLD_STEPS + i, 0))] * n + [half] * n,
            out_specs=(half,) * n),
        compiler_params=_params("arbitrary", "arbitrary"),
    )(core, *parts, *theirs)


N_DEV = 8


PACK_COLS = 1024
PACK_ROWS = 24


def _put_row(pack_ref, row, ref):
    if len(ref.shape) == 2:
        pack_ref[row:row + 1, :ref.shape[1]] = ref[...]
    else:
        for h in range(ref.shape[0]):
            pack_ref[row:row + 1, h * HEAD_DIM:(h + 1) * HEAD_DIM] = ref[h]


def _allreduce_small(grads, rows):
    n = len(grads)

    def body(*refs):
        in_ref, out_ref, buf, send_sems, recv_sems = refs[n + 1], refs[n], *refs[n + 2:]
        in_ref[...] = jnp.zeros_like(in_ref)
        for ref, row in zip(refs[:n], rows):
            if len(ref.shape) == 2 and ref.shape[0] > 1:
                in_ref[row:row + ref.shape[0], :ref.shape[1]] = ref[...]
            else:
                _put_row(in_ref, row, ref)
        x, y, c = _place()
        me = 4 * x + 2 * y + c
        buf[me] = in_ref[...]

        def copy(j, slot):
            px, py, pc = x ^ (j >> 2), y ^ ((j >> 1) & 1), c ^ (j & 1)
            return pltpu.make_async_remote_copy(
                src_ref=in_ref, dst_ref=buf.at[slot(px, py, pc)], send_sem=send_sems.at[j], recv_sem=recv_sems.at[j],
                device_id=(px, py, pc), device_id_type=MESH)

        for j in range(1, N_DEV):
            copy(j, lambda px, py, pc: me).start()
        for j in range(1, N_DEV):
            landing = copy(j, lambda px, py, pc: 4 * px + 2 * py + pc)
            landing.wait_send()
            landing.wait_recv()
        acc = buf[0]
        for s in range(1, N_DEV):
            acc = acc + buf[s]
        out_ref[...] = acc

    shape = (PACK_ROWS, PACK_COLS)
    return pl.pallas_call(
        body, name="allreduce_small", out_shape=jax.ShapeDtypeStruct(shape, F32),
        in_specs=[VMEM_FULL] * n, out_specs=VMEM_FULL,
        scratch_shapes=[pltpu.VMEM(shape, F32), pltpu.VMEM((N_DEV,) + shape, F32), pltpu.SemaphoreType.DMA((N_DEV,)),
                        pltpu.SemaphoreType.DMA((N_DEV,))],
    )(*grads)


BF16_SUBLANES = 16


def _reduce_own(me, parts, recvs, dep, steps, name, half=None):
    n = len(parts)
    where = me if half is None else jnp.concatenate([me, half])
    offset = (lambda w: 0) if half is None else (lambda w: w[1] * steps)

    def body(where_ref, *refs):
        for p_ref, rv_ref, o_ref in zip(refs[:n], refs[n:2 * n], refs[2 * n + 1:]):
            acc = p_ref[0].astype(F32)
            for k in range(3):
                acc = acc + rv_ref[k].astype(F32)
            o_ref[...] = acc

    shapes = [(p.shape[1] // steps, p.shape[2]) for p in parts]
    rows = 1 if half is None else 2
    return pl.pallas_call(
        body, name=name, out_shape=tuple(jax.ShapeDtypeStruct((rows * p.shape[1], p.shape[2]), F32) for p in parts),
        grid_spec=pltpu.PrefetchScalarGridSpec(
            num_scalar_prefetch=1, grid=(steps,),
            in_specs=[pl.BlockSpec((1, tr, c), lambda i, w: (w[0], i, 0)) for tr, c in shapes]
            + [pl.BlockSpec((3, tr, c), lambda i, w: (0, i, 0)) for tr, c in shapes] + [ANY],
            out_specs=tuple(pl.BlockSpec((tr, c), lambda i, w: (offset(w) + i, 0)) for tr, c in shapes)),
        compiler_params=_params("arbitrary"),
    )(where, *parts, *recvs, dep)


def _adamw_step(w, g, m, v):
    mn = ADAM_B1 * m + (1.0 - ADAM_B1) * g
    vn = ADAM_B2 * v + (1.0 - ADAM_B2) * (g * g)
    m_hat = mn / (1.0 - ADAM_B1 ** ADAM_STEP)
    v_hat = vn / (1.0 - ADAM_B2 ** ADAM_STEP)
    return -ADAM_LR * (m_hat / (jnp.sqrt(v_hat) + ADAM_EPS) + ADAM_WD * w), mn, vn


def _adamw(ws, gas, gbs, ms, vs, steps, name):
    n = len(ws)
    operands = [ws, gas, ms, vs] if gbs is None else [ws, gas, gbs, ms, vs]
    k = len(operands)

    def body(*refs):
        ins, outs = refs[:k * n], refs[k * n:]
        for j in range(n):
            w_ref, ga_ref, *gb_ref, m_ref, v_ref = ins[j::n]
            g_out, d_out, m_out, v_out = outs[j::n]
            g = ga_ref[...] + gb_ref[0][...] if gb_ref else ga_ref[...]
            g_out[...] = g
            d_out[...], m_out[...], v_out[...] = _adamw_step(w_ref[...], g, m_ref[...], v_ref[...])

    tiles = [pl.BlockSpec((w.shape[0] // steps, w.shape[1]), lambda i: (i, 0)) for w in ws]
    shapes = [jax.ShapeDtypeStruct(w.shape, F32) for w in ws]
    outs = pl.pallas_call(
        body, name=name, grid=(steps,), out_shape=tuple(shapes * 4), in_specs=tiles * k, out_specs=tuple(tiles * 4),
        compiler_params=_params("arbitrary"),
    )(*[a for group in operands for a in group])
    return [outs[j::n] for j in range(n)]


def _adamw_replicated(gsum, ws, ms, vs):
    n = len(ws)

    def body(g_ref, *refs):
        ins, outs = refs[:3 * n], refs[3 * n:]
        for i in range(n):
            w_ref, m_ref, v_ref = ins[i::n]
            shape = w_ref.shape
            if len(shape) == 2:
                g = g_ref[i:i + 1, :shape[1]]
            else:
                g = jnp.concatenate([g_ref[i:i + 1, h * HEAD_DIM:(h + 1) * HEAD_DIM] for h in range(shape[1])], axis=0)[None]
            g_out, d_out, m_out, v_out = outs[i::n]
            g_out[...] = g
            d_out[...], m_out[...], v_out[...] = _adamw_step(w_ref[...], g, m_ref[...], v_ref[...])

    shapes = [jax.ShapeDtypeStruct(w.shape, F32) for w in ws]
    outs = pl.pallas_call(
        body, name="adamw_replicated", out_shape=tuple(shapes * 4),
        in_specs=[VMEM_FULL] * (1 + 3 * n), out_specs=(VMEM_FULL,) * (4 * n),
    )(gsum, *ws, *ms, *vs)
    return [outs[i::n] for i in range(n)]


COL_SHARDED = ("ffn1_w_gate", "ffn1_w_up", "w_in", "ffn2_w_gate", "ffn2_w_up", "w1", "w2", "a1", "a2", "g1", "g2")
ROW_SHARDED = ("ffn1_w_down", "ffn2_w_down", "w_out")
CHUNKED = ("ffn1_w_gate", "ffn1_w_up", "ffn1_w_down", "w_in", "ffn2_w_gate", "ffn2_w_up", "ffn2_w_down")
WEIGHTS = ("ffn1_norm", "ffn1_w_gate", "ffn1_w_up", "ffn1_w_down", "mix_norm", "w_in", "q_norm", "k_norm",
           "mu_r", "mu_k", "mu_v", "mu_w", "mu_a", "mu_g", "w0", "w1", "w2", "a0", "a1", "a2", "g1", "g2",
           "k_k", "k_a", "r_k", "ln_x_w", "ln_x_b", "w_out", "ffn2_norm", "ffn2_w_gate", "ffn2_w_up", "ffn2_w_down")


TRANSPOSED = ("ffn1_w_gate", "ffn1_w_up", "ffn2_w_gate", "ffn2_w_up", "w1", "a1", "g1")


def _shard_2d(name, a):
    return a[0].T if name in TRANSPOSED else a[0]


def _full_from_blocks(name, blocks):
    if name in CHUNKED:
        return blocks
    if name in ROW_SHARDED:
        return blocks.reshape(-1, blocks.shape[-1])
    return blocks.transpose(1, 0, 2).reshape(blocks.shape[1], -1)


def _blocks_from_full(name, full):
    if name in CHUNKED:
        return full
    if name in ROW_SHARDED:
        return full.reshape(N_SHARDS, -1, full.shape[-1])
    return full.reshape(full.shape[0], N_SHARDS, -1).transpose(1, 0, 2)


FFN1_GROUP = ("ffn1_w_gate", "ffn1_w_up", "ffn1_w_down")
MIX_GROUP = ("w_in",) + RWKV_MAT
OUT_GROUP = ("w_out", "ffn2_w_gate", "ffn2_w_up", "ffn2_w_down")
FFN2_GROUP = OUT_GROUP[1:]
LATE_GROUP = ("w_in", "w_out") + RWKV_MAT


class _Exchange:
    def __init__(self, given):
        self.given = given
        first = self._gather_start(FFN1_GROUP, _HalfGatherViews, jnp.zeros(DEP_SHAPE, F32), "gather_ffn1_start")
        self.mix = self._gather_start(MIX_GROUP, _GatherViews, first[4], "gather_mix_start")
        self.out = self._gather_start(OUT_GROUP, _GatherViews, self.mix[4], "gather_out_start")
        self.first_dep = self.out[4]
        halves = _push_wait(first, _HalfGatherViews, (self.first_dep,), "gather_ffn1_wait")
        passed = _push_start([], halves, _ForwardViews, jnp.zeros(DEP_SHAPE, F32), "gather_ffn1_pass_start")
        self.first_weights = self._full(FFN1_GROUP, _push_wait(passed, _ForwardViews, (passed[4],), "gather_ffn1_pass_wait"))
        self.parts, self.recv = {}, {}

    @staticmethod
    def _full(names, blocks):
        out = {}
        for n, b in zip(names, blocks):
            full = _full_from_blocks(n, b)
            out[n] = full.astype(F32) if n in RWKV_MAT else full
        return out

    def _gather_start(self, names, views, after, name):
        after, raw = lax.optimization_barrier((after, [_shard_2d(n, self.given[n]) for n in names]))
        shards = [a.astype(BF16) for a in raw]
        return _push_start(shards, _empty_lands(shards, N_SHARDS, True), views, after, name)

    def mix_weights(self, after):
        return self._full(MIX_GROUP, _push_wait(self.mix, _GatherViews, after, "gather_mix_wait"))

    def out_weights(self, after):
        return self._full(OUT_GROUP, _push_wait(self.out, _GatherViews, after, "gather_out_wait"))

    def _scatter_start(self, grads, name):
        names = tuple(grads)
        parts = [_blocks_from_full(n, grads[n]) for n in names]
        self.parts.update(zip(names, parts))
        lands = [lax.empty((3,) + p.shape[1:], BF16) for p in parts]
        return _push_start([p.astype(BF16) for p in parts], lands, _ScatterViews, jnp.zeros(DEP_SHAPE, F32), name)

    def _scatter_done(self, started, names, after, name):
        outs = _push_wait(started, _ScatterViews, after, name, with_sources=True)
        for n, sent, got in zip(names, outs[:len(names)], outs[len(names):]):
            self.recv[n] = got
            if self.parts[n].dtype == BF16:
                self.parts[n] = sent

    def send_ffn2(self, grads):
        self.ffn2 = self._scatter_start(grads, "scatter_ffn2_start")
        return self.ffn2[4]

    def send_mix(self, grads, after):
        self._scatter_done(self.ffn2, FFN2_GROUP, after, "scatter_ffn2_wait")
        self.late = self._scatter_start(grads, "scatter_late_start")
        return self.late[4]

    def send_ffn1(self, grads):
        self.ffn1 = self._scatter_start(grads, "scatter_ffn1_start")
        return self.ffn1[4]

    def late_received(self, after):
        self._scatter_done(self.late, LATE_GROUP, after, "scatter_late_wait")

    def ffn1_received(self, after):
        self._scatter_done(self.ffn1, FFN1_GROUP, after, "scatter_ffn1_wait")


def kernel(
        x, ffn1_norm, ffn1_w_gate, ffn1_w_up, ffn1_w_down, mix_norm, w_in, q_norm, k_norm, mu_r, mu_k, mu_v, mu_w,
        mu_a, mu_g, w0, w1, w2, a0, a1, a2, g1, g2, k_k, k_a, r_k, ln_x_w, ln_x_b, w_out, ffn2_norm, ffn2_w_gate,
        ffn2_w_up, ffn2_w_down, loss_target, m_ffn1_norm, m_ffn1_w_gate, m_ffn1_w_up, m_ffn1_w_down, m_mix_norm,
        m_w_in, m_q_norm, m_k_norm, m_mu_r, m_mu_k, m_mu_v, m_mu_w, m_mu_a, m_mu_g, m_w0, m_w1, m_w2, m_a0, m_a1,
        m_a2, m_g1, m_g2, m_k_k, m_k_a, m_r_k, m_ln_x_w, m_ln_x_b, m_w_out, m_ffn2_norm, m_ffn2_w_gate, m_ffn2_w_up,
        m_ffn2_w_down, v_ffn1_norm, v_ffn1_w_gate, v_ffn1_w_up, v_ffn1_w_down, v_mix_norm, v_w_in, v_q_norm, v_k_norm,
        v_mu_r, v_mu_k, v_mu_v, v_mu_w, v_mu_a, v_mu_g, v_w0, v_w1, v_w2, v_a0, v_a1, v_a2, v_g1, v_g2, v_k_k, v_k_a,
        v_r_k, v_ln_x_w, v_ln_x_b, v_w_out, v_ffn2_norm, v_ffn2_w_gate, v_ffn2_w_up, v_ffn2_w_down):
    given = dict(locals())
    sharded = COL_SHARDED + ROW_SHARDED
    sharded = tuple(n for n in WEIGHTS if n in sharded)
    small = tuple(n for n in WEIGHTS if n not in sharded)

    ex = _Exchange(given)
    w = {n: given[n] for n in small}
    w.update(ex.first_weights)
    loss, dx, g = _local_step(x[0], loss_target[0], w, ex)

    core = lax.axis_index("c").astype(jnp.int32).reshape(1)
    late = [g[n] for n in FFN1_GROUP]
    folded = _fold_add(core, late, _sibling_swap(late, "fold_swap_ffn1", other_half=True), "fold_add_ffn1")
    dep = ex.send_ffn1(dict(zip(FFN1_GROUP, folded)))

    me = (2 * lax.axis_index("x") + lax.axis_index("y")).astype(jnp.int32).reshape(1)
    out = {}

    def reduced(sub, steps, tag):
        parts = [ex.parts[n].reshape(N_SHARDS, -1, ex.parts[n].shape[-1]) for n in sub]
        recvs = [ex.recv[n].reshape(3, -1, ex.recv[n].shape[-1]) for n in sub]
        return _reduce_own(me, parts, recvs, dep, steps, f"reduce_{tag}")

    def updated(sub, mine, theirs, steps, tag):
        res = _adamw([_shard_2d(n, given[n]) for n in sub], mine, theirs, [_shard_2d(n, given["m_" + n]) for n in sub],
                     [_shard_2d(n, given["v_" + n]) for n in sub], steps, f"adamw_{tag}")
        for n, rs in zip(sub, res):
            out[n] = [(r.T if n in TRANSPOSED else r).reshape(given[n].shape) for r in rs]
        return [out[n][1] for n in sub]

    ex.late_received((dep,))
    rest = tuple(n for n in sharded if n not in FFN1_GROUP)
    large, lora = tuple(n for n in rest if n not in RWKV_MAT), tuple(n for n in rest if n in RWKV_MAT)
    mine = reduced(large, 4, "rest_large")
    swap = _push_start(mine, [lax.empty(a.shape, a.dtype) for a in mine], _SiblingViews, dep, "swap_rest_start")
    mine_lora = reduced(lora, 1, "rest_lora")
    last = updated(lora, mine_lora, _sibling_swap(mine_lora, "sibling_swap_rest_lora"), 1, "rest_lora")

    row = {n: i for i, n in enumerate(small)}
    singles = [n for n in small if n not in RWKV_VEC]
    gsum = _allreduce_small([g[n] for n in singles] + [g["rwkv_vec"], loss],
                            [row[n] for n in singles] + [row[RWKV_VEC[0]], len(small)])
    res = _adamw_replicated(gsum, [given[n] for n in small], [given["m_" + n] for n in small], [given["v_" + n] for n in small])
    for n, rs in zip(small, res):
        out[n] = list(rs)
    total_loss = gsum[len(small), 0]

    both = _push_wait(swap, _SiblingViews, (*last, res[0][1]), "swap_rest_wait", with_sources=True)
    last = updated(large, both[:len(large)], both[len(large):], 8, "rest_large")

    ex.ffn1_received((*last, res[0][1]))
    halves = _reduce_own(me, [ex.parts[n] for n in FFN1_GROUP], [ex.recv[n] for n in FFN1_GROUP],
                         jnp.zeros(DEP_SHAPE, F32), FOLD_STEPS, "reduce_ffn1", half=core)
    grads = _sibling_fill(halves, "sibling_fill_ffn1")
    res = _adamw([_shard_2d(n, given[n]) for n in FFN1_GROUP], grads, None, [_shard_2d(n, given["m_" + n]) for n in FFN1_GROUP],
                 [_shard_2d(n, given["v_" + n]) for n in FFN1_GROUP], 8, "adamw_ffn1")
    for n, rs in zip(FFN1_GROUP, res):
        out[n] = [(r.T if n in TRANSPOSED else r).reshape(given[n].shape) for r in rs]
    return (total_loss, dx[None], *[out[n][0] for n in WEIGHTS], *[out[n][1] for n in WEIGHTS],
            *[out[n][2] for n in WEIGHTS], *[out[n][3] for n in WEIGHTS])
```

```python
import functools

import jax
import jax.numpy as jnp
from jax import lax
from jax.experimental import pallas as pl
from jax.experimental.pallas import tpu as pltpu

F32 = jnp.float32
BF16 = jnp.bfloat16
MESH = pl.DeviceIdType.MESH

RMS_EPS = 1e-6
GN_EPS = 64e-5
NEG_INF = -1e30
FFN_RESIDUAL = 0.5
HEAD_DIM = 64
ATT_BLOCK = 128
DILATIONS = (1, 4, 16)
SCAN_CHUNK = 64
TOKEN_TILE = 256
FFN_BWD_TILE = 512

ADAM_LR = 0.001
ADAM_B1 = 0.9
ADAM_B2 = 0.999
ADAM_EPS = 1e-08
ADAM_WD = 0.01
ADAM_STEP = 10

VMEM_FULL = pl.BlockSpec(memory_space=pltpu.VMEM)
ANY = pl.BlockSpec(memory_space=pl.ANY)


VMEM_LIMIT = 56 * 1024 * 1024


def _params(*sem):
    return pltpu.CompilerParams(dimension_semantics=sem, vmem_limit_bytes=VMEM_LIMIT)


def _dot(a, b, dims):
    return lax.dot_general(a.astype(BF16), b.astype(BF16), (dims, ((), ())), preferred_element_type=F32)


def _dot_nn(a, b):
    return _dot(a, b, ((1,), (0,)))


def _dot_nt(a, b):
    return _dot(a, b, ((1,), (1,)))


def _dot_tn(a, b):
    return _dot(a, b, ((0,), (0,)))


@jax.custom_vjp
def _mm(a, b):
    return _dot_nn(a, b)


def _mm_fwd(a, b):
    return _dot_nn(a, b), (a, b)


def _mm_bwd(res, g):
    a, b = res
    return _dot_nt(g, b).astype(a.dtype), _dot_tn(a, g).astype(b.dtype)


_mm.defvjp(_mm_fwd, _mm_bwd)


@jax.custom_vjp
def _mm_nt(a, bt):
    return _dot_nt(a, bt)


def _mm_nt_fwd(a, bt):
    return _dot_nt(a, bt), (a, bt)


def _mm_nt_bwd(res, g):
    a, bt = res
    return _dot_nn(g, bt).astype(a.dtype), _dot_tn(g, a).astype(bt.dtype)


_mm_nt.defvjp(_mm_nt_fwd, _mm_nt_bwd)


def _bdot(a, b, ca, cb):
    return lax.dot_general(a.astype(BF16), b.astype(BF16), (((ca,), (cb,)), ((0,), (0,))), preferred_element_type=F32)


@jax.custom_vjp
def _bmm_nt(a, b):
    return _bdot(a, b, 2, 2)


def _bmm_nt_fwd(a, b):
    return _bdot(a, b, 2, 2), (a, b)


def _bmm_nt_bwd(res, g):
    a, b = res
    return _bdot(g, b, 2, 1), _bdot(g, a, 1, 1)


_bmm_nt.defvjp(_bmm_nt_fwd, _bmm_nt_bwd)


@jax.custom_vjp
def _bmm_nn(a, b):
    return _bdot(a, b, 2, 1)


def _bmm_nn_fwd(a, b):
    return _bdot(a, b, 2, 1), (a, b)


def _bmm_nn_bwd(res, g):
    a, b = res
    return _bdot(g, b, 2, 2), _bdot(a, g, 1, 1)


_bmm_nn.defvjp(_bmm_nn_fwd, _bmm_nn_bwd)


@jax.custom_vjp
def _bmm_tn(a, b):
    return _bdot(a, b, 1, 1)


def _bmm_tn_fwd(a, b):
    return _bdot(a, b, 1, 1), (a, b)


def _bmm_tn_bwd(res, g):
    a, b = res
    return _bdot(b, g, 2, 2), _bdot(a, g, 2, 1)


_bmm_tn.defvjp(_bmm_tn_fwd, _bmm_tn_bwd)


def _hdot(a, b, ca, cb):
    return lax.dot_general(a, b, (((ca,), (cb,)), ((0,), (0,))), precision=lax.Precision.HIGH, preferred_element_type=F32)


def _sigmoid(x):
    return 1.0 / (1.0 + jnp.exp(-x))


def _rms(x):
    return lax.rsqrt(jnp.mean(x * x, axis=-1, keepdims=True) + RMS_EPS)


def _ffn_fwd(x, norm, wg, wu, wd, dep, name, target=None):
    t, d = x.shape
    nc, fc, _ = wg.shape
    tm = TOKEN_TILE

    def body(x_ref, n_ref, wg_ref, wu_ref, wd_ref, dep_ref, *rest):
        o_ref, g_ref, u_ref = rest[-3:] if target is None else rest[1:4]
        xv = x_ref[...]
        h = (xv * _rms(xv) * n_ref[...]).astype(BF16)
        acc = jnp.zeros((tm, d), F32)
        for c in range(nc):
            g = _dot_nt(h, wg_ref[c])
            u = _dot_nt(h, wu_ref[c])
            g_ref[c] = g.astype(BF16)
            u_ref[c] = u.astype(BF16)
            a = (g * _sigmoid(g) * u).astype(BF16)
            acc = acc + jnp.dot(a, wd_ref[c], preferred_element_type=F32)
        y = xv + FFN_RESIDUAL * acc
        if target is None:
            o_ref[...] = y
        else:
            t_ref, loss_ref = rest[0], rest[4]
            err = y - t_ref[...]
            o_ref[...] = err * (1.0 / d)
            part = 0.5 * jnp.sum(jnp.mean(err * err, axis=-1, keepdims=True), axis=0, keepdims=True)

            @pl.when(pl.program_id(0) == 0)
            def _():
                loss_ref[...] = jnp.zeros_like(loss_ref)

            loss_ref[...] += jnp.broadcast_to(part, loss_ref.shape)

    tile = pl.BlockSpec((tm, d), lambda i: (i, 0))
    hidden = pl.BlockSpec((nc, tm, fc), lambda i: (0, i, 0))
    hshape = jax.ShapeDtypeStruct((nc, t, fc), BF16)
    with_loss = target is not None
    return pl.pallas_call(
        body, name=name, grid=(t // tm,),
        out_shape=(jax.ShapeDtypeStruct((t, d), F32), hshape, hshape) + ((jax.ShapeDtypeStruct((1, 128), F32),) if with_loss else ()),
        in_specs=[tile, pl.BlockSpec((1, d), lambda i: (0, 0)), VMEM_FULL, VMEM_FULL, VMEM_FULL, ANY] + ([tile] if with_loss else []),
        out_specs=(tile, hidden, hidden) + ((pl.BlockSpec((1, 128), lambda i: (0, 0)),) if with_loss else ()),
        compiler_params=_params("arbitrary"),
    )(x, norm, wg, wu, wd, dep, *((target,) if with_loss else ()))


def _rmsnorm_bwd(xv, gain, dh):
    rs = _rms(xv)
    xn = xv * rs
    dxn = dh * gain
    dx = rs * (dxn - xn * jnp.mean(dxn * xn, axis=-1, keepdims=True))
    return dx, jnp.sum(dh * xn, axis=0, keepdims=True)


def _ffn_bwd(x, norm, wg, wu, wd, gate, up, dy, dep, name):
    t, d = x.shape
    nc, fc, _ = wg.shape
    tm = FFN_BWD_TILE
    nt = t // tm

    def body(x_ref, n_ref, wg_ref, wu_ref, wd_ref, g_ref, u_ref, dy_ref, dep_ref, dx_ref, dn_ref, dwg_ref, dwu_ref,
             dwd_ref, dh_ref, ag_ref, au_ref, ad_ref):
        c, i = pl.program_id(0), pl.program_id(1)
        rows = pl.ds(pl.multiple_of(i * tm, tm), tm)
        xv = x_ref[...]
        gain = n_ref[...]
        h = (xv * _rms(xv) * gain).astype(BF16)
        dy = dy_ref[...]
        dyb = (FFN_RESIDUAL * dy).astype(BF16)
        g = g_ref[0].astype(F32)
        u = u_ref[0].astype(F32)
        sg = _sigmoid(g)
        s = g * sg
        a = (s * u).astype(BF16)
        da = _dot_nt(dyb, wd_ref[0])
        dub = (da * s).astype(BF16)
        dgb = (da * u * (sg * (1.0 + g * (1.0 - sg)))).astype(BF16)
        dwd_c = _dot_tn(a, dyb)
        dwg_c = _dot_tn(dgb, h)
        dwu_c = _dot_tn(dub, h)
        dh_c = _dot_nn(dgb, wg_ref[0]) + _dot_nn(dub, wu_ref[0])

        @pl.when(i == 0)
        def _():
            ad_ref[...] = dwd_c
            ag_ref[...] = dwg_c
            au_ref[...] = dwu_c

        @pl.when(i > 0)
        def _():
            ad_ref[...] += dwd_c
            ag_ref[...] += dwg_c
            au_ref[...] += dwu_c

        @pl.when(i == nt - 1)
        def _():
            dwd_ref[0] = ad_ref[...].astype(BF16)
            dwg_ref[0] = ag_ref[...].astype(BF16)
            dwu_ref[0] = au_ref[...].astype(BF16)

        @pl.when(c == 0)
        def _():
            dh_ref[rows, :] = dh_c

        @pl.when(c > 0)
        def _():
            dh_ref[rows, :] += dh_c

        @pl.when(c == nc - 1)
        def _():
            dx, dn = _rmsnorm_bwd(xv, gain, dh_ref[rows, :])
            dx_ref[...] = dx + dy

            @pl.when(i == 0)
            def _():
                dn_ref[...] = dn

            @pl.when(i > 0)
            def _():
                dn_ref[...] += dn

    tile = pl.BlockSpec((tm, d), lambda c, i: (i, 0))
    row = pl.BlockSpec((1, d), lambda c, i: (0, 0))
    wrow = pl.BlockSpec((1, fc, d), lambda c, i: (c, 0, 0), pipeline_mode=pl.Buffered(1))
    hidden = pl.BlockSpec((1, tm, fc), lambda c, i: (c, i, 0))
    last = pl.BlockSpec((tm, d), lambda c, i: (jnp.where(c == nc - 1, i, 0), 0))
    return pl.pallas_call(
        body, name=name, grid=(nc, nt),
        out_shape=(jax.ShapeDtypeStruct((t, d), F32), jax.ShapeDtypeStruct((1, d), F32),
                   jax.ShapeDtypeStruct(wg.shape, BF16), jax.ShapeDtypeStruct(wu.shape, BF16),
                   jax.ShapeDtypeStruct(wd.shape, BF16)),
        in_specs=[tile, row, wrow, wrow, wrow, hidden, hidden, tile, ANY],
        out_specs=(last, row, wrow, wrow, wrow),
        scratch_shapes=[pltpu.VMEM((t, d), F32)] + [pltpu.VMEM((fc, d), F32)] * 3,
        compiler_params=_params("arbitrary", "arbitrary"),
    )(x, norm, wg, wu, wd, gate, up, dy, dep)


def _store_heads(ref, v):
    for h in range(ref.shape[0]):
        ref[h] = v[:, h * HEAD_DIM:(h + 1) * HEAD_DIM]


def _load_heads(ref):
    return jnp.concatenate([ref[h] for h in range(ref.shape[0])], axis=-1)


N_HEAD_GROUPS = 3


def _proj_fwd(x, norm, w, c):
    t, d = x.shape
    nc, _, ncol = w.shape
    nh = c // HEAD_DIM
    tm = TOKEN_TILE
    wide = nc * ncol - N_HEAD_GROUPS * c

    def body(x_ref, n_ref, w_ref, q_ref, k_ref, v_ref, cur_ref):
        xv = x_ref[...]
        h = (xv * _rms(xv) * n_ref[...]).astype(BF16)
        full = jnp.concatenate([jnp.dot(h, w_ref[s], preferred_element_type=F32) for s in range(nc)], axis=1)
        for m, ref in enumerate((q_ref, k_ref, v_ref)):
            _store_heads(ref, full[:, m * c:(m + 1) * c])
        cur_ref[...] = full[:, N_HEAD_GROUPS * c:]

    heads = pl.BlockSpec((nh, tm, HEAD_DIM), lambda i: (0, i, 0))
    hshape = jax.ShapeDtypeStruct((nh, t, HEAD_DIM), F32)
    return pl.pallas_call(
        body, name="proj_fwd", grid=(t // tm,),
        out_shape=(hshape, hshape, hshape, jax.ShapeDtypeStruct((t, wide), F32)),
        in_specs=[pl.BlockSpec((tm, d), lambda i: (i, 0)), pl.BlockSpec((1, d), lambda i: (0, 0)), VMEM_FULL],
        out_specs=(heads, heads, heads, pl.BlockSpec((tm, wide), lambda i: (i, 0))),
        compiler_params=_params("arbitrary"),
    )(x, norm, w)


def _proj_bwd(x, norm, w, dq, dk, dv, dcur, dres):
    t, d = x.shape
    nc, _, ncol = w.shape
    nh = dq.shape[0]
    tm = TOKEN_TILE
    nt = t // tm
    wide = dcur.shape[1]

    def body(x_ref, n_ref, w_ref, dq_ref, dk_ref, dv_ref, dcur_ref, dres_ref, dx_ref, dn_ref, dw_ref, acc_ref):
        i = pl.program_id(0)

        @pl.when(i == 0)
        def _():
            acc_ref[...] = jnp.zeros_like(acc_ref)
            dn_ref[...] = jnp.zeros_like(dn_ref)

        xv = x_ref[...]
        gain = n_ref[...]
        h = (xv * _rms(xv) * gain).astype(BF16)
        dp = jnp.concatenate([_load_heads(dq_ref), _load_heads(dk_ref), _load_heads(dv_ref), dcur_ref[...]], axis=1).astype(BF16)
        dh = jnp.zeros((tm, d), F32)
        for s in range(nc):
            dps = dp[:, s * ncol:(s + 1) * ncol]
            acc_ref[s] += _dot_tn(h, dps)
            dh = dh + _dot_nt(dps, w_ref[s])
        dx, dn = _rmsnorm_bwd(xv, gain, dh)
        dx_ref[...] = dx + dres_ref[...]
        dn_ref[...] += dn

        @pl.when(i == nt - 1)
        def _():
            dw_ref[...] = acc_ref[...].astype(BF16)

    tile = pl.BlockSpec((tm, d), lambda i: (i, 0))
    row = pl.BlockSpec((1, d), lambda i: (0, 0))
    heads = pl.BlockSpec((nh, tm, HEAD_DIM), lambda i: (0, i, 0))
    return pl.pallas_call(
        body, name="proj_bwd", grid=(nt,),
        out_shape=(jax.ShapeDtypeStruct((t, d), F32), jax.ShapeDtypeStruct((1, d), F32),
                   jax.ShapeDtypeStruct(w.shape, BF16)),
        in_specs=[tile, row, VMEM_FULL, heads, heads, heads, pl.BlockSpec((tm, wide), lambda i: (i, 0)), tile],
        out_specs=(tile, row, VMEM_FULL),
        scratch_shapes=[pltpu.VMEM(w.shape, F32)], compiler_params=_params("arbitrary"),
    )(x, norm, w, dq, dk, dv, dcur, dres)


def _mixout_fwd(x, att, opg, gate, w):
    t, d = x.shape
    nh = att.shape[0]
    half = gate.shape[1]
    tm = TOKEN_TILE

    def body(x_ref, att_ref, opg_ref, g_ref, w_ref, o_ref):
        mix = jnp.concatenate([_load_heads(att_ref), _load_heads(opg_ref) * g_ref[...]], axis=-1).astype(BF16)
        o_ref[...] = x_ref[...] + jnp.dot(mix, w_ref[...], preferred_element_type=F32)

    tile = pl.BlockSpec((tm, d), lambda i: (i, 0))
    htile = pl.BlockSpec((tm, half), lambda i: (i, 0))
    heads = pl.BlockSpec((nh, tm, HEAD_DIM), lambda i: (0, i, 0))
    return pl.pallas_call(
        body, name="mixout_fwd", grid=(t // tm,), out_shape=jax.ShapeDtypeStruct((t, d), F32),
        in_specs=[tile, heads, heads, htile, VMEM_FULL], out_specs=tile, compiler_params=_params("arbitrary"),
    )(x, att, opg, gate, w)


def _mixout_bwd(att, opg, gate, w, dy, dep):
    nh, t, _ = att.shape
    half = gate.shape[1]
    d = dy.shape[1]
    tm = TOKEN_TILE

    def body(att_ref, opg_ref, g_ref, w_ref, dy_ref, dep_ref, datt_ref, dopg_ref, dg_ref, dw_ref):
        i = pl.program_id(0)
        opg_v, g_v = _load_heads(opg_ref), g_ref[...]
        mix = jnp.concatenate([_load_heads(att_ref), opg_v * g_v], axis=-1).astype(BF16)
        dyb = dy_ref[...].astype(BF16)
        dmix = _dot_nt(dyb, w_ref[...])
        dw = _dot_tn(mix, dyb)
        _store_heads(datt_ref, dmix[:, :half])
        drw = dmix[:, half:]
        _store_heads(dopg_ref, drw * g_v)
        dg_ref[...] = drw * opg_v

        @pl.when(i == 0)
        def _():
            dw_ref[...] = dw

        @pl.when(i > 0)
        def _():
            dw_ref[...] += dw

    tile = pl.BlockSpec((tm, d), lambda i: (i, 0))
    htile = pl.BlockSpec((tm, half), lambda i: (i, 0))
    heads = pl.BlockSpec((nh, tm, HEAD_DIM), lambda i: (0, i, 0))
    hshape = jax.ShapeDtypeStruct((nh, t, HEAD_DIM), F32)
    return pl.pallas_call(
        body, name="mixout_bwd", grid=(t // tm,),
        out_shape=(hshape, hshape, jax.ShapeDtypeStruct((t, half), F32), jax.ShapeDtypeStruct(w.shape, F32)),
        in_specs=[heads, heads, htile, VMEM_FULL, tile, ANY],
        out_specs=(heads, heads, htile, pl.BlockSpec(w.shape, lambda i: (0, 0))),
        compiler_params=_params("arbitrary"),
    )(att, opg, gate, w, dy, dep)


def _head_norm(x, gain):
    return x * _rms(x) * gain


def _att_pattern(qh, kh, v, nb):
    g, blk, _ = qh.shape
    scale = HEAD_DIM ** -0.5
    qi = lax.broadcasted_iota(jnp.int32, (blk, blk), 0)
    kj = lax.broadcasted_iota(jnp.int32, (blk, blk), 1)
    sc = jnp.where(kj <= qi, _bmm_nt(qh, kh) * scale, NEG_INF)
    top = jnp.max(sc, axis=-1, keepdims=True)
    if nb > 1:
        khp = jnp.concatenate([kh[:1], kh[:-1]], axis=0)
        vp = jnp.concatenate([v[:1], v[:-1]], axis=0)
        has_prev = lax.broadcasted_iota(jnp.int32, (g, 1, 1), 0) % nb != 0
        sp = jnp.where((kj >= qi) & has_prev, _bmm_nt(qh, khp) * scale, NEG_INF)
        top = jnp.maximum(top, jnp.max(sp, axis=-1, keepdims=True))
    m = lax.stop_gradient(top)
    pc = jnp.exp(sc - m)
    den = jnp.sum(pc, axis=-1, keepdims=True)
    acc = _bmm_nn(pc, v)
    if nb > 1:
        pp = jnp.exp(sp - m)
        den = den + jnp.sum(pp, axis=-1, keepdims=True)
        acc = acc + _bmm_nn(pp, vp)
    o = acc / den
    return o, jnp.broadcast_to(m + jnp.log(den), o.shape)


def _pattern_rows(t, dil):
    length = t // dil
    return [pl.ds(r, length, stride=dil) if dil > 1 else pl.ds(0, length) for r in range(dil)], length // ATT_BLOCK


def _take(ref, rows, nb):
    return jnp.concatenate([ref[0, r, :].reshape(nb, ATT_BLOCK, HEAD_DIM) for r in rows], axis=0)


def _put(ref, rows, nb, val):
    for j, r in enumerate(rows):
        ref[0, r, :] = val[j * nb:(j + 1) * nb].reshape(nb * ATT_BLOCK, HEAD_DIM)


def _put_add(ref, rows, nb, val):
    for j, r in enumerate(rows):
        ref[0, r, :] += val[j * nb:(j + 1) * nb].reshape(nb * ATT_BLOCK, HEAD_DIM)


def _merge_fn(o1, o2, o3, l1, l2, l3):
    m = lax.stop_gradient(jnp.maximum(jnp.maximum(l1, l2), l3))
    e1, e2, e3 = jnp.exp(l1 - m), jnp.exp(l2 - m), jnp.exp(l3 - m)
    return (e1 * o1 + e2 * o2 + e3 * o3) / (e1 + e2 + e3)


def _token_rows(j):
    return pl.ds(pl.multiple_of(j * ATT_BLOCK, ATT_BLOCK), ATT_BLOCK)


def _norm_rows(t, q_ref, k_ref, gq, gk, qh_ref, kh_ref):
    def step(j, carry):
        rows = _token_rows(j)
        qh_ref[0, rows, :] = _head_norm(q_ref[0, rows, :], gq[0])
        kh_ref[0, rows, :] = _head_norm(k_ref[0, rows, :], gk[0])
        return carry

    lax.fori_loop(0, t // ATT_BLOCK, step, 0)


def _att_head_specs(t):
    head = pl.BlockSpec((1, t, HEAD_DIM), lambda h: (h, 0, 0))
    gain = pl.BlockSpec((1, 1, HEAD_DIM), lambda h: (0, 0, 0))
    return head, gain


def _att_fwd(q, k, v, qn, kn):
    nh, t, dh = q.shape
    head, gain = _att_head_specs(t)

    def body(q_ref, k_ref, v_ref, qn_ref, kn_ref, att_ref, o1, o2, o3, l1, l2, l3, qh_ref, kh_ref):
        saved = (o1, o2, o3, l1, l2, l3)
        _norm_rows(t, q_ref, k_ref, qn_ref[...], kn_ref[...], qh_ref, kh_ref)
        for p, dil in enumerate(DILATIONS):
            rows, nb = _pattern_rows(t, dil)
            o, lse = _att_pattern(_take(qh_ref, rows, nb), _take(kh_ref, rows, nb), _take(v_ref, rows, nb), nb)
            _put(saved[p], rows, nb, o)
            _put(saved[3 + p], rows, nb, lse)

        def merge(j, carry):
            rows = _token_rows(j)
            att_ref[0, rows, :] = _merge_fn(*[r[0, rows, :] for r in saved])
            return carry

        lax.fori_loop(0, t // ATT_BLOCK, merge, 0)

    return pl.pallas_call(
        body, name="att_fwd", grid=(nh,), out_shape=(jax.ShapeDtypeStruct(q.shape, F32),) * 7,
        in_specs=[head, head, head, gain, gain], out_specs=(head,) * 7,
        scratch_shapes=[pltpu.VMEM((1, t, dh), F32)] * 2, compiler_params=_params("arbitrary"),
    )(q, k, v, qn, kn)


def _att_bwd(q, k, v, qn, kn, saved, datt):
    nh, t, dh = q.shape
    head, gain = _att_head_specs(t)

    def body(q_ref, k_ref, v_ref, qn_ref, kn_ref, o1, o2, o3, l1, l2, l3, datt_ref,
             dq_ref, dk_ref, dv_ref, dqn_ref, dkn_ref, qh_ref, kh_ref, dqh_ref, dkh_ref, *ct_refs):
        for ref in (dqh_ref, dkh_ref, dv_ref):
            ref[...] = jnp.zeros_like(ref)

        @pl.when(pl.program_id(0) == 0)
        def _():
            dqn_ref[...] = jnp.zeros_like(dqn_ref)
            dkn_ref[...] = jnp.zeros_like(dkn_ref)

        gq, gk = qn_ref[...], kn_ref[...]
        _norm_rows(t, q_ref, k_ref, gq, gk, qh_ref, kh_ref)

        def merge_cotangents(j, carry):
            rows = _token_rows(j)
            _, merge_vjp = jax.vjp(_merge_fn, *[r[0, rows, :] for r in (o1, o2, o3, l1, l2, l3)])
            for ref, val in zip(ct_refs, merge_vjp(datt_ref[0, rows, :])):
                ref[0, rows, :] = val
            return carry

        lax.fori_loop(0, t // ATT_BLOCK, merge_cotangents, 0)

        for p, dil in enumerate(DILATIONS):
            rows, nb = _pattern_rows(t, dil)
            _, pattern_vjp = jax.vjp(functools.partial(_att_pattern, nb=nb), _take(qh_ref, rows, nb), _take(kh_ref, rows, nb),
                                     _take(v_ref, rows, nb))
            dqh, dkh, dv = pattern_vjp((_take(ct_refs[p], rows, nb), _take(ct_refs[3 + p], rows, nb)))
            _put_add(dqh_ref, rows, nb, dqh)
            _put_add(dkh_ref, rows, nb, dkh)
            _put_add(dv_ref, rows, nb, dv)

        def norm_cotangents(j, carry):
            rows = _token_rows(j)
            out = []
            for x_ref, gain, dh_ref, dx_ref, acc in ((q_ref, gq, dqh_ref, dq_ref, carry[0]), (k_ref, gk, dkh_ref, dk_ref, carry[1])):
                _, norm_vjp = jax.vjp(_head_norm, x_ref[0, rows, :], gain[0])
                dx, dgain = norm_vjp(dh_ref[0, rows, :])
                dx_ref[0, rows, :] = dx
                out.append(acc + dgain)
            return tuple(out)

        zero = jnp.zeros((1, dh), F32)
        dgq, dgk = lax.fori_loop(0, t // ATT_BLOCK, norm_cotangents, (zero, zero))
        dqn_ref[0] += dgq
        dkn_ref[0] += dgk

    hshape = jax.ShapeDtypeStruct(q.shape, F32)
    gshape = jax.ShapeDtypeStruct((1, 1, dh), F32)
    return pl.pallas_call(
        body, name="att_bwd", grid=(nh,), out_shape=(hshape, hshape, hshape, gshape, gshape),
        in_specs=[head, head, head, gain, gain] + [head] * 7, out_specs=(head, head, head, gain, gain),
        scratch_shapes=[pltpu.VMEM((1, t, dh), F32)] * 10, compiler_params=_params("arbitrary"),
    )(q, k, v, qn, kn, *saved, datt)


RWKV_VEC = ("mu_r", "mu_k", "mu_v", "mu_w", "mu_a", "mu_g", "w0", "a0", "k_k", "k_a")
RWKV_MAT = ("w1", "w2", "a1", "a2", "g1", "g2")


def _rwkv_pre_fn(cur, prev, vec, w1t, w2, a1t, a2, g1t, g2):
    c = cur.shape[1] // 4
    mu_r, mu_k, mu_v, mu_w, mu_a, mu_g, w0, a0, k_k, k_a = (vec[j:j + 1] for j in range(10))

    def lerp(j, mu):
        xc, xp = cur[:, j * c:(j + 1) * c], prev[:, j * c:(j + 1) * c]
        return xc + (xp - xc) * mu

    r, k, v = lerp(0, mu_r), lerp(1, mu_k), lerp(2, mu_v)
    cw, ca, cg = lerp(3, mu_w), lerp(3, mu_a), lerp(3, mu_g)
    z = w0 + _mm(jnp.tanh(_mm_nt(cw, w1t)), w2)
    w_log = jnp.minimum(z, 0.0) - jnp.log(1.0 + jnp.exp(-jnp.abs(z))) - 0.5
    lw = -jnp.exp(w_log)
    a = _sigmoid(a0 + _mm(_mm_nt(ca, a1t), a2))
    gate = _mm(_sigmoid(_mm_nt(cg, g1t)), g2)
    kkraw = k * k_k
    kmod = k * (1.0 + (a - 1.0) * k_a)
    return r, lw, kmod, v, kkraw, a, gate


HALO_ROWS = 8


def _rwkv_pre_specs(c, mats, tile_of):
    tm = TOKEN_TILE
    nh = c // HEAD_DIM
    wide = pl.BlockSpec((tm, 4 * c), lambda j: (tile_of(j), 0))
    halo = pl.BlockSpec((HALO_ROWS, 4 * c), lambda j: (jnp.maximum(tile_of(j) * (tm // HALO_ROWS) - 1, 0), 0))
    one = pl.BlockSpec((tm, c), lambda j: (tile_of(j), 0))
    heads = pl.BlockSpec((nh, tm, HEAD_DIM), lambda j: (0, tile_of(j), 0))
    vec = pl.BlockSpec((10, c), lambda j: (0, 0))
    mspecs = [pl.BlockSpec(m.shape, lambda j: (0, 0)) for m in mats]
    return wide, halo, one, heads, vec, mspecs


def _previous_rows(cur, halo, tile):
    first = jnp.where(tile > 0, halo[HALO_ROWS - 1:HALO_ROWS], 0.0)
    rows = lax.broadcasted_iota(jnp.int32, cur.shape, 0)
    return jnp.where(rows == 0, first, pltpu.roll(cur, 1, axis=0))


def _rwkv_pre_fwd(cur, vec, mats):
    t, c4 = cur.shape
    c = c4 // 4
    wide, halo, one, heads, vspec, mspecs = _rwkv_pre_specs(c, mats, lambda j: j)

    def body(cur_ref, halo_ref, vec_ref, *rest):
        mrefs, outs = rest[:6], rest[6:]
        cur_v = cur_ref[...]
        prev = _previous_rows(cur_v, halo_ref[...], pl.program_id(0))
        vals = _rwkv_pre_fn(cur_v, prev, vec_ref[...], *(m[...] for m in mrefs))
        for ref, val in zip(outs[:6], vals[:6]):
            _store_heads(ref, val)
        outs[6][...] = vals[6]

    hshape = jax.ShapeDtypeStruct((c // HEAD_DIM, t, HEAD_DIM), F32)
    return pl.pallas_call(
        body, name="rwkv_pre_fwd", grid=(t // TOKEN_TILE,), out_shape=(hshape,) * 6 + (jax.ShapeDtypeStruct((t, c), F32),),
        in_specs=[wide, halo, vspec] + mspecs, out_specs=(heads,) * 6 + (one,), compiler_params=_params("arbitrary"),
    )(cur, cur, vec, *mats)


def _rwkv_pre_bwd(cur, vec, mats, cts, dgate):
    t, c4 = cur.shape
    c = c4 // 4
    tm = TOKEN_TILE
    nt = t // tm
    wide, halo, one, heads, vspec, mspecs = _rwkv_pre_specs(c, mats, lambda j: nt - 1 - j)

    def body(cur_ref, halo_ref, vec_ref, *rest):
        mrefs, ctrefs, dgate_ref, outs, carry_ref = rest[:6], rest[6:12], rest[12], rest[13:-1], rest[-1]
        j = pl.program_id(0)

        @pl.when(j == 0)
        def _():
            carry_ref[...] = jnp.zeros_like(carry_ref)
            for ref in outs[1:]:
                ref[...] = jnp.zeros_like(ref)

        cur_v = cur_ref[...]
        prev = _previous_rows(cur_v, halo_ref[...], nt - 1 - j)
        _, vjp = jax.vjp(_rwkv_pre_fn, cur_v, prev, vec_ref[...], *(m[...] for m in mrefs))
        grads = vjp(tuple(_load_heads(r) for r in ctrefs) + (dgate_ref[...],))
        dprev = grads[1]
        rows = lax.broadcasted_iota(jnp.int32, dprev.shape, 0)
        outs[0][...] = grads[0] + jnp.where(rows == tm - 1, carry_ref[0:1], pltpu.roll(dprev, tm - 1, axis=0))
        carry_ref[0:1] = dprev[0:1]
        for ref, val in zip(outs[1:], grads[2:]):
            ref[...] += val

    return pl.pallas_call(
        body, name="rwkv_pre_bwd", grid=(nt,),
        out_shape=(jax.ShapeDtypeStruct(cur.shape, F32), jax.ShapeDtypeStruct(vec.shape, F32))
        + tuple(jax.ShapeDtypeStruct(m.shape, F32) for m in mats),
        in_specs=[wide, halo, vspec] + mspecs + [heads] * 6 + [one], out_specs=(wide, vspec) + tuple(mspecs),
        scratch_shapes=[pltpu.VMEM((HALO_ROWS, c4), F32)], compiler_params=_params("arbitrary"),
    )(cur, cur, vec, *mats, *cts, dgate)


def _scan_chunk_fn(h0, r, lw, k, v, kkraw, a, rk, lnw, lnb):
    n = r.shape[1]
    nrm = jnp.sqrt(jnp.sum(kkraw * kkraw, axis=-1, keepdims=True))
    kk = kkraw / jnp.maximum(nrm, 1e-12)
    av, bv = -kk, kk * a
    ti = lax.broadcasted_iota(jnp.int32, (n, n), 0)
    si = lax.broadcasted_iota(jnp.int32, (n, n), 1)
    incl, strict = ti >= si, ti > si
    ones = jnp.broadcast_to(incl.astype(F32)[None], (r.shape[0], n, n))
    cum = _hdot(ones, lw, 2, 1)
    at, rt = av * jnp.exp(cum - lw), r * jnp.exp(cum)
    inv = jnp.exp(-cum)
    bt, kt = bv * inv, k * inv
    gram = _hdot(jnp.concatenate([at, rt], axis=1), jnp.concatenate([bt, kt], axis=1), 2, 2)
    lab = jnp.where(strict, gram[:, :n, :n], 0.0)
    lak = jnp.where(strict, gram[:, :n, n:], 0.0)
    rb = jnp.where(incl, gram[:, n:, :n], 0.0)
    rkm = jnp.where(incl, gram[:, n:, n:], 0.0)
    nv = v.shape[2]
    u = _bmm_nn(jnp.concatenate([at, lak], axis=2), jnp.concatenate([h0, v], axis=1))
    p = lab
    m = 2
    while m < n:
        both = _bmm_nn(p, jnp.concatenate([u, p], axis=2))
        u, p = u + both[:, :, :nv], both[:, :, nv:]
        m *= 2
    u = u + _bmm_nn(p, u)
    y = _bmm_nn(jnp.concatenate([rt, rb, rkm], axis=2), jnp.concatenate([h0, u, v], axis=1))
    last = jnp.exp(jnp.sum(lw, axis=1, keepdims=True))
    h1 = jnp.swapaxes(last, 1, 2) * (h0 + _bmm_tn(jnp.concatenate([bt, kt], axis=1), jnp.concatenate([u, v], axis=1)))
    mean = jnp.mean(y, axis=-1, keepdims=True)
    yc = y - mean
    var = jnp.mean(yc * yc, axis=-1, keepdims=True)
    yn = yc * lax.rsqrt(var + GN_EPS) * lnw + lnb
    bonus = jnp.sum(r * k * rk, axis=-1, keepdims=True) * v
    return yn + bonus, h1


SCAN_GROUP = 2


def _scan_group_fn(h0, r, lw, k, v, kkraw, a, rk, lnw, lnb):
    outs = []
    for j in range(SCAN_GROUP):
        rows = slice(j * SCAN_CHUNK, (j + 1) * SCAN_CHUNK)
        o, h0 = _scan_chunk_fn(h0, r[:, rows], lw[:, rows], k[:, rows], v[:, rows], kkraw[:, rows], a[:, rows], rk, lnw, lnb)
        outs.append(o)
    return jnp.concatenate(outs, axis=1), h0


def _scan_specs(h, t, dh, rev):
    n = SCAN_CHUNK * SCAN_GROUP
    nc = t // n
    pos = (lambda c: (0, nc - 1 - c, 0)) if rev else (lambda c: (0, c, 0))
    st = (lambda c: (nc - 1 - c, 0, 0, 0)) if rev else (lambda c: (c, 0, 0, 0))
    seq = pl.BlockSpec((h, n, dh), pos)
    par = pl.BlockSpec((h, 1, dh), lambda c: (0, 0, 0))
    state = pl.BlockSpec((1, h, dh, dh), st)
    return seq, par, state


def _scan_fwd(seqs, pars):
    h, t, dh = seqs[0].shape
    nc = t // (SCAN_CHUNK * SCAN_GROUP)
    seq, par, state = _scan_specs(h, t, dh, False)

    def body(r, lw, k, v, kkraw, a, rk, lnw, lnb, o_ref, st_ref, h_ref):
        @pl.when(pl.program_id(0) == 0)
        def _():
            h_ref[...] = jnp.zeros_like(h_ref)

        h0 = h_ref[...]
        st_ref[0] = h0
        o, h1 = _scan_group_fn(h0, r[...], lw[...], k[...], v[...], kkraw[...], a[...], rk[...], lnw[...], lnb[...])
        o_ref[...] = o
        h_ref[...] = h1

    return pl.pallas_call(
        body, name="rwkv_scan_fwd", grid=(nc,),
        out_shape=(jax.ShapeDtypeStruct((h, t, dh), F32), jax.ShapeDtypeStruct((nc, h, dh, dh), F32)),
        in_specs=[seq] * 6 + [par] * 3, out_specs=(seq, state),
        scratch_shapes=[pltpu.VMEM((h, dh, dh), F32)], compiler_params=_params("arbitrary"),
    )(*seqs, *pars)


def _scan_bwd(seqs, pars, states, do):
    h, t, dh = seqs[0].shape
    nc = t // (SCAN_CHUNK * SCAN_GROUP)
    seq, par, state = _scan_specs(h, t, dh, True)

    def body(r, lw, k, v, kkraw, a, rk, lnw, lnb, st_ref, do_ref, *rest):
        douts, dpars, dh_ref = rest[:6], rest[6:9], rest[9]
        first = pl.program_id(0) == 0

        @pl.when(first)
        def _():
            dh_ref[...] = jnp.zeros_like(dh_ref)

        _, vjp = jax.vjp(_scan_group_fn, st_ref[0], r[...], lw[...], k[...], v[...], kkraw[...], a[...],
                         rk[...], lnw[...], lnb[...])
        grads = vjp((do_ref[...], dh_ref[...]))
        dh_ref[...] = grads[0]
        for ref, val in zip(douts, grads[1:7]):
            ref[...] = val

        @pl.when(first)
        def _():
            for ref, val in zip(dpars, grads[7:]):
                ref[...] = val

        @pl.when(jnp.logical_not(first))
        def _():
            for ref, val in zip(dpars, grads[7:]):
                ref[...] += val

    sshape = jax.ShapeDtypeStruct((h, t, dh), F32)
    pshape = jax.ShapeDtypeStruct((h, 1, dh), F32)
    return pl.pallas_call(
        body, name="rwkv_scan_bwd", grid=(nc,), out_shape=(sshape,) * 6 + (pshape,) * 3,
        in_specs=[seq] * 6 + [par] * 3 + [state, seq], out_specs=(seq,) * 6 + (par,) * 3,
        scratch_shapes=[pltpu.VMEM((h, dh, dh), F32)], compiler_params=_params("arbitrary"),
    )(*seqs, *pars, states, do)


def _local_step(x, target, w, ex):
    w = dict(w)
    c = w["mu_r"].shape[-1]
    qn, kn = w["q_norm"].reshape(1, 1, HEAD_DIM), w["k_norm"].reshape(1, 1, HEAD_DIM)
    vec = jnp.concatenate([w[n].reshape(1, c) for n in RWKV_VEC], axis=0)
    pars = [w[n].reshape(-1, 1, HEAD_DIM) for n in ("r_k", "ln_x_w", "ln_x_b")]
    no_dep = jnp.zeros(DEP_SHAPE, F32)

    x1, gate1, up1 = _ffn_fwd(x, w["ffn1_norm"], w["ffn1_w_gate"], w["ffn1_w_up"], w["ffn1_w_down"], ex.first_dep, "ffn1_fwd")
    w.update(ex.mix_weights((x1,)))
    mats = [w[n] for n in RWKV_MAT]
    q, k, v, cur = _proj_fwd(x1, w["mix_norm"], w["w_in"], c)
    att, *saved = _att_fwd(q, k, v, qn, kn)
    pre = _rwkv_pre_fwd(cur, vec, mats)
    seqs, gate = pre[:6], pre[6]
    opg, states = _scan_fwd(seqs, pars)
    w.update(ex.out_weights((att, opg)))
    x2 = _mixout_fwd(x1, att, opg, gate, w["w_out"])
    dy, gate2, up2, loss = _ffn_fwd(x2, w["ffn2_norm"], w["ffn2_w_gate"], w["ffn2_w_up"], w["ffn2_w_down"], no_dep, "ffn2_fwd",
                                    target=target)

    g = {}
    dx2, g["ffn2_norm"], g["ffn2_w_gate"], g["ffn2_w_up"], g["ffn2_w_down"] = _ffn_bwd(
        x2, w["ffn2_norm"], w["ffn2_w_gate"], w["ffn2_w_up"], w["ffn2_w_down"], gate2, up2, dy, no_dep, "ffn2_bwd")
    dep = ex.send_ffn2({n: g[n] for n in ("ffn2_w_gate", "ffn2_w_up", "ffn2_w_down")})
    datt, dopg, dgate, g["w_out"] = _mixout_bwd(att, opg, gate, w["w_out"], dx2, dep)
    dscan = _scan_bwd(seqs, pars, states, dopg)
    for n, d in zip(("r_k", "ln_x_w", "ln_x_b"), dscan[6:]):
        g[n] = d
    dcur, dvec, *dmats = _rwkv_pre_bwd(cur, vec, mats, dscan[:6], dgate)
    for n, d in zip(RWKV_MAT, dmats):
        g[n] = d
    g["rwkv_vec"] = dvec
    dq, dk, dv, g["q_norm"], g["k_norm"] = _att_bwd(q, k, v, qn, kn, saved, datt)
    dx1, g["mix_norm"], g["w_in"] = _proj_bwd(x1, w["mix_norm"], w["w_in"], dq, dk, dv, dcur, dx2)
    dep = ex.send_mix({n: g[n] for n in ("w_in", "w_out") + RWKV_MAT}, (dx1,))
    dx, g["ffn1_norm"], g["ffn1_w_gate"], g["ffn1_w_up"], g["ffn1_w_down"] = _ffn_bwd(
        x, w["ffn1_norm"], w["ffn1_w_gate"], w["ffn1_w_up"], w["ffn1_w_down"], gate1, up1, dx1, dep, "ffn1_bwd")
    return loss, dx, g


N_SHARDS = 4


def _place():
    return lax.axis_index("x"), lax.axis_index("y"), lax.axis_index("c")


def _chip_peers(x, y):
    return [(1 - x, y), (x, 1 - y), (1 - x, 1 - y)]


HBM = pl.BlockSpec(memory_space=pltpu.HBM)
SEM = pl.BlockSpec(memory_space=pltpu.SEMAPHORE)
DEP_SHAPE = (8, 128)


class _Views:
    to_sibling = False


class _GatherViews(_Views):
    @staticmethod
    def send(i, srcs, lands, k, at):
        return srcs[i], lands[i].at[at[3]]

    @staticmethod
    def landing(i, srcs, lands, k, at):
        return srcs[i], lands[i].at[2 * at[4] + at[5]]


class _ScatterViews(_Views):
    @staticmethod
    def send(i, srcs, lands, k, at):
        return srcs[i].at[2 * at[4] + at[5]], lands[i].at[k]

    @staticmethod
    def landing(i, srcs, lands, k, at):
        return srcs[i].at[at[3]], lands[i].at[k]


def _half_rows(ref, slot, half):
    rows = ref.shape[1] // 2
    return ref.at[slot, pl.ds(pl.multiple_of(half * rows, BF16_SUBLANES), rows)]


class _HalfGatherViews(_Views):
    @staticmethod
    def send(i, srcs, lands, k, at):
        rows = srcs[i].shape[0] // 2
        return srcs[i].at[pl.ds(pl.multiple_of(at[2] * rows, BF16_SUBLANES), rows)], _half_rows(lands[i], at[3], at[2])

    @staticmethod
    def landing(i, srcs, lands, k, at):
        rows = srcs[i].shape[0] // 2
        return srcs[i].at[pl.ds(pl.multiple_of(at[2] * rows, BF16_SUBLANES), rows)], _half_rows(lands[i], 2 * at[4] + at[5], at[2])


class _ForwardViews(_Views):
    to_sibling = True

    @staticmethod
    def send(i, srcs, lands, k, at):
        mine = _half_rows(lands[i], 2 * at[4] + at[5], at[2])
        return mine, mine

    @staticmethod
    def landing(i, srcs, lands, k, at):
        theirs = _half_rows(lands[i], 2 * at[4] + at[5], 1 - at[2])
        return theirs, theirs


class _SiblingViews(_Views):
    to_sibling = True

    @staticmethod
    def _block(ref, k):
        size = -(-ref.shape[0] // 3 // BF16_SUBLANES) * BF16_SUBLANES
        return ref.at[pl.ds(k * size, min(size, ref.shape[0] - k * size))]

    @classmethod
    def send(cls, i, srcs, lands, k, at):
        return cls._block(srcs[i], k), cls._block(lands[i], k)

    landing = send


def _push_start(srcs, lands, views, after, name):
    ns, nl = len(srcs), len(lands)

    def body(*refs):
        src_refs, land_refs = refs[:ns], refs[ns:ns + nl]
        send_sems, recv_sems = refs[ns + nl + 1:ns + nl + 3]
        token = refs[2 * (ns + nl) + 3]
        x, y, c = _place()
        for i in range(nl):
            for k, (px, py) in enumerate(_chip_peers(x, y)):
                src, dst = views.send(i, src_refs, land_refs, k, (x, y, c, 2 * x + y, px, py))
                pltpu.make_async_remote_copy(
                    src_ref=src, dst_ref=dst, send_sem=send_sems.at[3 * i + k], recv_sem=recv_sems.at[3 * i + k],
                    device_id=(x, y, 1 - c) if views.to_sibling else (px, py, c), device_id_type=MESH).start()
        token[...] = jnp.zeros_like(token)

    sems = pltpu.SemaphoreType.DMA((3 * nl,))
    both = [pltpu.with_memory_space_constraint(a, pltpu.HBM) for a in (*srcs, *lands)]
    outs = pl.pallas_call(
        body, name=name,
        out_shape=(sems, sems, *[pltpu.HBM(a.shape, a.dtype) for a in both], jax.ShapeDtypeStruct(DEP_SHAPE, F32)),
        in_specs=[HBM] * (ns + nl) + [ANY], out_specs=(SEM, SEM, *[HBM] * (ns + nl), VMEM_FULL),
        input_output_aliases={i: 2 + i for i in range(ns + nl)},
        compiler_params=pltpu.CompilerParams(has_side_effects=pltpu.SideEffectType.DATAFLOW_SIDE_EFFECTING),
    )(*both, after)
    return outs[0], outs[1], outs[2:2 + ns], outs[2 + ns:2 + ns + nl], outs[2 + ns + nl]


def _push_wait(started, views, after, name, with_sources=False):
    send_sems, recv_sems, srcs, lands, _ = started
    ns, nl = len(srcs), len(lands)

    def body(*refs):
        src_refs, land_refs = refs[:ns], refs[ns:ns + nl]
        send_sems, recv_sems = refs[ns + nl:ns + nl + 2]
        x, y, c = _place()
        for i in range(nl):
            for k, (px, py) in enumerate(_chip_peers(x, y)):
                src, dst = views.landing(i, src_refs, land_refs, k, (x, y, c, 2 * x + y, px, py))
                landing = pltpu.make_async_remote_copy(
                    src_ref=src, dst_ref=dst, send_sem=send_sems.at[3 * i + k], recv_sem=recv_sems.at[3 * i + k],
                    device_id=(x, y, 1 - c) if views.to_sibling else (px, py, c), device_id_type=MESH)
                landing.wait_send()
                landing.wait_recv()

    outs = pl.pallas_call(
        body, name=name,
        out_shape=tuple(pltpu.HBM(a.shape, a.dtype) for a in (*srcs, *lands)),
        in_specs=[HBM] * (ns + nl) + [SEM, SEM] + [ANY] * len(after), out_specs=(HBM,) * (ns + nl),
        input_output_aliases={i: i for i in range(ns + nl)},
        compiler_params=pltpu.CompilerParams(has_side_effects=pltpu.SideEffectType.DATAFLOW_SIDE_EFFECTING),
    )(*srcs, *lands, send_sems, recv_sems, *after)
    return outs if with_sources else outs[ns:]


def _empty_lands(shards, slots, own_slot):
    lands = [lax.empty((slots,) + s.shape, s.dtype) for s in shards]
    if own_slot:
        me = 2 * lax.axis_index("x") + lax.axis_index("y")
        lands = [lax.dynamic_update_index_in_dim(z, s, me, 0) for z, s in zip(lands, shards)]
    return lands


def _sibling_swap(arrays, name, other_half=False):
    n = len(arrays)

    def body(*refs):
        ins, outs = refs[:n], refs[n:2 * n]
        send_sems, recv_sems = refs[2 * n:]
        x, y, c = _place()
        copies = []
        for i in range(n):
            src = ins[i]
            if other_half:
                rows = src.shape[1] // 2
                src = src.at[:, pl.ds(pl.multiple_of((1 - c) * rows, BF16_SUBLANES), rows)]
            cp = pltpu.make_async_remote_copy(
                src_ref=src, dst_ref=outs[i], send_sem=send_sems.at[i], recv_sem=recv_sems.at[i],
                device_id=(x, y, 1 - c), device_id_type=MESH)
            cp.start()
            copies.append(cp)
        for cp in copies:
            cp.wait()

    shapes = [(a.shape[0], a.shape[1] // 2, a.shape[2]) if other_half else a.shape for a in arrays]
    return pl.pallas_call(
        body, name=name,
        out_shape=tuple(jax.ShapeDtypeStruct(s, a.dtype) for s, a in zip(shapes, arrays)),
        in_specs=[ANY] * n, out_specs=(ANY,) * n,
        scratch_shapes=[pltpu.SemaphoreType.DMA((n,)), pltpu.SemaphoreType.DMA((n,))],
    )(*arrays)


def _sibling_fill(arrays, name):
    n = len(arrays)

    def body(*refs):
        outs = refs[n:2 * n]
        send_sems, recv_sems = refs[2 * n:]
        x, y, c = _place()
        copies = []
        for i in range(n):
            rows = outs[i].shape[0] // 2
            mine = outs[i].at[pl.ds(pl.multiple_of(c * rows, BF16_SUBLANES), rows)]
            cp = pltpu.make_async_remote_copy(
                src_ref=mine, dst_ref=mine, send_sem=send_sems.at[i], recv_sem=recv_sems.at[i],
                device_id=(x, y, 1 - c), device_id_type=MESH)
            cp.start()
            copies.append(cp)
        for cp in copies:
            cp.wait()

    return pl.pallas_call(
        body, name=name, out_shape=tuple(jax.ShapeDtypeStruct(a.shape, a.dtype) for a in arrays),
        in_specs=[ANY] * n, out_specs=(ANY,) * n, input_output_aliases={i: i for i in range(n)},
        scratch_shapes=[pltpu.SemaphoreType.DMA((n,)), pltpu.SemaphoreType.DMA((n,))],
    )(*arrays)


FOLD_STEPS = 2


def _fold_add(core, parts, theirs, name):
    n = len(parts)
    s, r, cols = parts[0].shape
    tr = r // 2 // FOLD_STEPS

    def body(core_ref, *refs):
        for p_ref, t_ref, o_ref in zip(refs[:n], refs[n:2 * n], refs[2 * n:]):
            o_ref[...] = (p_ref[...].astype(F32) + t_ref[...].astype(F32)).astype(BF16)

    half = pl.BlockSpec((1, tr, cols), lambda j, i, core_ref: (j, i, 0))
    return pl.pallas_call(
        body, name=name, out_shape=tuple(jax.ShapeDtypeStruct((s, r // 2, cols), BF16) for _ in parts),
        grid_spec=pltpu.PrefetchScalarGridSpec(
            num_scalar_prefetch=1, grid=(s, FOLD_STEPS),
            in_specs=[pl.BlockSpec((1, tr, cols), lambda j, i, core_ref: (j, core_ref[0] * FOLD_STEPS + i, 0))] * n + [half] * n,
            out_specs=(half,) * n),
        compiler_params=_params("arbitrary", "arbitrary"),
    )(core, *parts, *theirs)


N_DEV = 8


PACK_COLS = 1024
PACK_ROWS = 24


def _put_row(pack_ref, row, ref):
    if len(ref.shape) == 2:
        pack_ref[row:row + 1, :ref.shape[1]] = ref[...]
    else:
        for h in range(ref.shape[0]):
            pack_ref[row:row + 1, h * HEAD_DIM:(h + 1) * HEAD_DIM] = ref[h]


def _allreduce_small(grads, rows):
    n = len(grads)

    def body(*refs):
        in_ref, out_ref, buf, send_sems, recv_sems = refs[n + 1], refs[n], *refs[n + 2:]
        in_ref[...] = jnp.zeros_like(in_ref)
        for ref, row in zip(refs[:n], rows):
            if len(ref.shape) == 2 and ref.shape[0] > 1:
                in_ref[row:row + ref.shape[0], :ref.shape[1]] = ref[...]
            else:
                _put_row(in_ref, row, ref)
        x, y, c = _place()
        me = 4 * x + 2 * y + c
        buf[me] = in_ref[...]

        def copy(j, slot):
            px, py, pc = x ^ (j >> 2), y ^ ((j >> 1) & 1), c ^ (j & 1)
            return pltpu.make_async_remote_copy(
                src_ref=in_ref, dst_ref=buf.at[slot(px, py, pc)], send_sem=send_sems.at[j], recv_sem=recv_sems.at[j],
                device_id=(px, py, pc), device_id_type=MESH)

        for j in range(1, N_DEV):
            copy(j, lambda px, py, pc: me).start()
        for j in range(1, N_DEV):
            landing = copy(j, lambda px, py, pc: 4 * px + 2 * py + pc)
            landing.wait_send()
            landing.wait_recv()
        acc = buf[0]
        for s in range(1, N_DEV):
            acc = acc + buf[s]
        out_ref[...] = acc

    shape = (PACK_ROWS, PACK_COLS)
    return pl.pallas_call(
        body, name="allreduce_small", out_shape=jax.ShapeDtypeStruct(shape, F32),
        in_specs=[VMEM_FULL] * n, out_specs=VMEM_FULL,
        scratch_shapes=[pltpu.VMEM(shape, F32), pltpu.VMEM((N_DEV,) + shape, F32), pltpu.SemaphoreType.DMA((N_DEV,)),
                        pltpu.SemaphoreType.DMA((N_DEV,))],
    )(*grads)


BF16_SUBLANES = 16


def _reduce_own(me, parts, recvs, dep, steps, name, half=None):
    n = len(parts)
    where = me if half is None else jnp.concatenate([me, half])
    offset = (lambda w: 0) if half is None else (lambda w: w[1] * steps)

    def body(where_ref, *refs):
        for p_ref, rv_ref, o_ref in zip(refs[:n], refs[n:2 * n], refs[2 * n + 1:]):
            acc = p_ref[0].astype(F32)
            for k in range(3):
                acc = acc + rv_ref[k].astype(F32)
            o_ref[...] = acc

    shapes = [(p.shape[1] // steps, p.shape[2]) for p in parts]
    rows = 1 if half is None else 2
    return pl.pallas_call(
        body, name=name, out_shape=tuple(jax.ShapeDtypeStruct((rows * p.shape[1], p.shape[2]), F32) for p in parts),
        grid_spec=pltpu.PrefetchScalarGridSpec(
            num_scalar_prefetch=1, grid=(steps,),
            in_specs=[pl.BlockSpec((1, tr, c), lambda i, w: (w[0], i, 0)) for tr, c in shapes]
            + [pl.BlockSpec((3, tr, c), lambda i, w: (0, i, 0)) for tr, c in shapes] + [ANY],
            out_specs=tuple(pl.BlockSpec((tr, c), lambda i, w: (offset(w) + i, 0)) for tr, c in shapes)),
        compiler_params=_params("arbitrary"),
    )(where, *parts, *recvs, dep)


def _adamw_step(w, g, m, v):
    mn = ADAM_B1 * m + (1.0 - ADAM_B1) * g
    vn = ADAM_B2 * v + (1.0 - ADAM_B2) * (g * g)
    m_hat = mn / (1.0 - ADAM_B1 ** ADAM_STEP)
    v_hat = vn / (1.0 - ADAM_B2 ** ADAM_STEP)
    return -ADAM_LR * (m_hat / (jnp.sqrt(v_hat) + ADAM_EPS) + ADAM_WD * w), mn, vn


def _adamw(ws, gas, gbs, ms, vs, steps, name):
    n = len(ws)
    operands = [ws, gas, ms, vs] if gbs is None else [ws, gas, gbs, ms, vs]
    k = len(operands)

    def body(*refs):
        ins, outs = refs[:k * n], refs[k * n:]
        for j in range(n):
            w_ref, ga_ref, *gb_ref, m_ref, v_ref = ins[j::n]
            g_out, d_out, m_out, v_out = outs[j::n]
            g = ga_ref[...] + gb_ref[0][...] if gb_ref else ga_ref[...]
            g_out[...] = g
            d_out[...], m_out[...], v_out[...] = _adamw_step(w_ref[...], g, m_ref[...], v_ref[...])

    tiles = [pl.BlockSpec((w.shape[0] // steps, w.shape[1]), lambda i: (i, 0)) for w in ws]
    shapes = [jax.ShapeDtypeStruct(w.shape, F32) for w in ws]
    outs = pl.pallas_call(
        body, name=name, grid=(steps,), out_shape=tuple(shapes * 4), in_specs=tiles * k, out_specs=tuple(tiles * 4),
        compiler_params=_params("arbitrary"),
    )(*[a for group in operands for a in group])
    return [outs[j::n] for j in range(n)]


def _adamw_replicated(gsum, ws, ms, vs):
    n = len(ws)

    def body(g_ref, *refs):
        ins, outs = refs[:3 * n], refs[3 * n:]
        for i in range(n):
            w_ref, m_ref, v_ref = ins[i::n]
            shape = w_ref.shape
            if len(shape) == 2:
                g = g_ref[i:i + 1, :shape[1]]
            else:
                g = jnp.concatenate([g_ref[i:i + 1, h * HEAD_DIM:(h + 1) * HEAD_DIM] for h in range(shape[1])], axis=0)[None]
            g_out, d_out, m_out, v_out = outs[i::n]
            g_out[...] = g
            d_out[...], m_out[...], v_out[...] = _adamw_step(w_ref[...], g, m_ref[...], v_ref[...])

    shapes = [jax.ShapeDtypeStruct(w.shape, F32) for w in ws]
    outs = pl.pallas_call(
        body, name="adamw_replicated", out_shape=tuple(shapes * 4),
        in_specs=[VMEM_FULL] * (1 + 3 * n), out_specs=(VMEM_FULL,) * (4 * n),
    )(gsum, *ws, *ms, *vs)
    return [outs[i::n] for i in range(n)]


COL_SHARDED = ("ffn1_w_gate", "ffn1_w_up", "w_in", "ffn2_w_gate", "ffn2_w_up", "w1", "w2", "a1", "a2", "g1", "g2")
ROW_SHARDED = ("ffn1_w_down", "ffn2_w_down", "w_out")
CHUNKED = ("ffn1_w_gate", "ffn1_w_up", "ffn1_w_down", "w_in", "ffn2_w_gate", "ffn2_w_up", "ffn2_w_down")
WEIGHTS = ("ffn1_norm", "ffn1_w_gate", "ffn1_w_up", "ffn1_w_down", "mix_norm", "w_in", "q_norm", "k_norm",
           "mu_r", "mu_k", "mu_v", "mu_w", "mu_a", "mu_g", "w0", "w1", "w2", "a0", "a1", "a2", "g1", "g2",
           "k_k", "k_a", "r_k", "ln_x_w", "ln_x_b", "w_out", "ffn2_norm", "ffn2_w_gate", "ffn2_w_up", "ffn2_w_down")


TRANSPOSED = ("ffn1_w_gate", "ffn1_w_up", "ffn2_w_gate", "ffn2_w_up", "w1", "a1", "g1")


def _shard_2d(name, a):
    return a[0].T if name in TRANSPOSED else a[0]


def _full_from_blocks(name, blocks):
    if name in CHUNKED:
        return blocks
    if name in ROW_SHARDED:
        return blocks.reshape(-1, blocks.shape[-1])
    return blocks.transpose(1, 0, 2).reshape(blocks.shape[1], -1)


def _blocks_from_full(name, full):
    if name in CHUNKED:
        return full
    if name in ROW_SHARDED:
        return full.reshape(N_SHARDS, -1, full.shape[-1])
    return full.reshape(full.shape[0], N_SHARDS, -1).transpose(1, 0, 2)


FFN1_GROUP = ("ffn1_w_gate", "ffn1_w_up", "ffn1_w_down")
MIX_GROUP = ("w_in",) + RWKV_MAT
OUT_GROUP = ("w_out", "ffn2_w_gate", "ffn2_w_up", "ffn2_w_down")
FFN2_GROUP = OUT_GROUP[1:]
LATE_GROUP = ("w_in", "w_out") + RWKV_MAT


class _Exchange:
    def __init__(self, given):
        self.given = given
        first = self._gather_start(FFN1_GROUP, _HalfGatherViews, jnp.zeros(DEP_SHAPE, F32), "gather_ffn1_start")
        self.mix = self._gather_start(MIX_GROUP, _HalfGatherViews, first[4], "gather_mix_start")
        self.out = self._gather_start(OUT_GROUP, _GatherViews, self.mix[4], "gather_out_start")
        self.first_dep = self.out[4]
        halves = _push_wait(first, _HalfGatherViews, (self.first_dep,), "gather_ffn1_wait")
        passed = _push_start([], halves, _ForwardViews, jnp.zeros(DEP_SHAPE, F32), "gather_ffn1_pass_start")
        self.first_weights = self._full(FFN1_GROUP, _push_wait(passed, _ForwardViews, (passed[4],), "gather_ffn1_pass_wait"))
        self.parts, self.recv = {}, {}

    @staticmethod
    def _full(names, blocks):
        out = {}
        for n, b in zip(names, blocks):
            full = _full_from_blocks(n, b)
            out[n] = full.astype(F32) if n in RWKV_MAT else full
        return out

    def _gather_start(self, names, views, after, name):
        after, raw = lax.optimization_barrier((after, [_shard_2d(n, self.given[n]) for n in names]))
        shards = [a.astype(BF16) for a in raw]
        return _push_start(shards, _empty_lands(shards, N_SHARDS, True), views, after, name)

    def mix_weights(self, after):
        halves = _push_wait(self.mix, _HalfGatherViews, after, "gather_mix_wait")
        passed = _push_start([], halves, _ForwardViews, jnp.zeros(DEP_SHAPE, F32), "gather_mix_pass_start")
        return self._full(MIX_GROUP, _push_wait(passed, _ForwardViews, (passed[4],), "gather_mix_pass_wait"))

    def out_weights(self, after):
        return self._full(OUT_GROUP, _push_wait(self.out, _GatherViews, after, "gather_out_wait"))

    def _scatter_start(self, grads, name):
        names = tuple(grads)
        parts = [_blocks_from_full(n, grads[n]) for n in names]
        self.parts.update(zip(names, parts))
        lands = [lax.empty((3,) + p.shape[1:], BF16) for p in parts]
        return _push_start([p.astype(BF16) for p in parts], lands, _ScatterViews, jnp.zeros(DEP_SHAPE, F32), name)

    def _scatter_done(self, started, names, after, name):
        outs = _push_wait(started, _ScatterViews, after, name, with_sources=True)
        for n, sent, got in zip(names, outs[:len(names)], outs[len(names):]):
            self.recv[n] = got
            if self.parts[n].dtype == BF16:
                self.parts[n] = sent

    def send_ffn2(self, grads):
        self.ffn2 = self._scatter_start(grads, "scatter_ffn2_start")
        return self.ffn2[4]

    def send_mix(self, grads, after):
        self._scatter_done(self.ffn2, FFN2_GROUP, after, "scatter_ffn2_wait")
        self.late = self._scatter_start(grads, "scatter_late_start")
        return self.late[4]

    def send_ffn1(self, grads):
        self.ffn1 = self._scatter_start(grads, "scatter_ffn1_start")
        return self.ffn1[4]

    def late_received(self, after):
        self._scatter_done(self.late, LATE_GROUP, after, "scatter_late_wait")

    def ffn1_received(self, after):
        self._scatter_done(self.ffn1, FFN1_GROUP, after, "scatter_ffn1_wait")


def kernel(
        x, ffn1_norm, ffn1_w_gate, ffn1_w_up, ffn1_w_down, mix_norm, w_in, q_norm, k_norm, mu_r, mu_k, mu_v, mu_w,
        mu_a, mu_g, w0, w1, w2, a0, a1, a2, g1, g2, k_k, k_a, r_k, ln_x_w, ln_x_b, w_out, ffn2_norm, ffn2_w_gate,
        ffn2_w_up, ffn2_w_down, loss_target, m_ffn1_norm, m_ffn1_w_gate, m_ffn1_w_up, m_ffn1_w_down, m_mix_norm,
        m_w_in, m_q_norm, m_k_norm, m_mu_r, m_mu_k, m_mu_v, m_mu_w, m_mu_a, m_mu_g, m_w0, m_w1, m_w2, m_a0, m_a1,
        m_a2, m_g1, m_g2, m_k_k, m_k_a, m_r_k, m_ln_x_w, m_ln_x_b, m_w_out, m_ffn2_norm, m_ffn2_w_gate, m_ffn2_w_up,
        m_ffn2_w_down, v_ffn1_norm, v_ffn1_w_gate, v_ffn1_w_up, v_ffn1_w_down, v_mix_norm, v_w_in, v_q_norm, v_k_norm,
        v_mu_r, v_mu_k, v_mu_v, v_mu_w, v_mu_a, v_mu_g, v_w0, v_w1, v_w2, v_a0, v_a1, v_a2, v_g1, v_g2, v_k_k, v_k_a,
        v_r_k, v_ln_x_w, v_ln_x_b, v_w_out, v_ffn2_norm, v_ffn2_w_gate, v_ffn2_w_up, v_ffn2_w_down):
    given = dict(locals())
    sharded = COL_SHARDED + ROW_SHARDED
    sharded = tuple(n for n in WEIGHTS if n in sharded)
    small = tuple(n for n in WEIGHTS if n not in sharded)

    ex = _Exchange(given)
    w = {n: given[n] for n in small}
    w.update(ex.first_weights)
    loss, dx, g = _local_step(x[0], loss_target[0], w, ex)

    core = lax.axis_index("c").astype(jnp.int32).reshape(1)
    late = [g[n] for n in FFN1_GROUP]
    folded = _fold_add(core, late, _sibling_swap(late, "fold_swap_ffn1", other_half=True), "fold_add_ffn1")
    dep = ex.send_ffn1(dict(zip(FFN1_GROUP, folded)))

    me = (2 * lax.axis_index("x") + lax.axis_index("y")).astype(jnp.int32).reshape(1)
    out = {}

    def reduced(sub, steps, tag):
        parts = [ex.parts[n].reshape(N_SHARDS, -1, ex.parts[n].shape[-1]) for n in sub]
        recvs = [ex.recv[n].reshape(3, -1, ex.recv[n].shape[-1]) for n in sub]
        return _reduce_own(me, parts, recvs, dep, steps, f"reduce_{tag}")

    def updated(sub, mine, theirs, steps, tag):
        res = _adamw([_shard_2d(n, given[n]) for n in sub], mine, theirs, [_shard_2d(n, given["m_" + n]) for n in sub],
                     [_shard_2d(n, given["v_" + n]) for n in sub], steps, f"adamw_{tag}")
        for n, rs in zip(sub, res):
            out[n] = [(r.T if n in TRANSPOSED else r).reshape(given[n].shape) for r in rs]
        return [out[n][1] for n in sub]

    ex.late_received((dep,))
    rest = tuple(n for n in sharded if n not in FFN1_GROUP)
    large, lora = tuple(n for n in rest if n not in RWKV_MAT), tuple(n for n in rest if n in RWKV_MAT)
    mine = reduced(large, 4, "rest_large")
    swap = _push_start(mine, [lax.empty(a.shape, a.dtype) for a in mine], _SiblingViews, dep, "swap_rest_start")
    mine_lora = reduced(lora, 1, "rest_lora")
    last = updated(lora, mine_lora, _sibling_swap(mine_lora, "sibling_swap_rest_lora"), 1, "rest_lora")

    row = {n: i for i, n in enumerate(small)}
    singles = [n for n in small if n not in RWKV_VEC]
    gsum = _allreduce_small([g[n] for n in singles] + [g["rwkv_vec"], loss],
                            [row[n] for n in singles] + [row[RWKV_VEC[0]], len(small)])
    res = _adamw_replicated(gsum, [given[n] for n in small], [given["m_" + n] for n in small], [given["v_" + n] for n in small])
    for n, rs in zip(small, res):
        out[n] = list(rs)
    total_loss = gsum[len(small), 0]

    both = _push_wait(swap, _SiblingViews, (*last, res[0][1]), "swap_rest_wait", with_sources=True)
    last = updated(large, both[:len(large)], both[len(large):], 8, "rest_large")

    ex.ffn1_received((*last, res[0][1]))
    halves = _reduce_own(me, [ex.parts[n] for n in FFN1_GROUP], [ex.recv[n] for n in FFN1_GROUP],
                         jnp.zeros(DEP_SHAPE, F32), FOLD_STEPS, "reduce_ffn1", half=core)
    grads = _sibling_fill(halves, "sibling_fill_ffn1")
    res = _adamw([_shard_2d(n, given[n]) for n in FFN1_GROUP], grads, None, [_shard_2d(n, given["m_" + n]) for n in FFN1_GROUP],
                 [_shard_2d(n, given["v_" + n]) for n in FFN1_GROUP], 8, "adamw_ffn1")
    for n, rs in zip(FFN1_GROUP, res):
        out[n] = [(r.T if n in TRANSPOSED else r).reshape(given[n].shape) for r in rs]
    return (total_loss, dx[None], *[out[n][0] for n in WEIGHTS], *[out[n][1] for n in WEIGHTS],
            *[out[n][2] for n in WEIGHTS], *[out[n][3] for n in WEIGHTS])
```
